```python
import math
import jax, jax.numpy as jnp
from jax import lax
import numpy as np

D_MODEL = 1024
BATCH = 8
SEQ = 8192
DEPTH = 2

N_MIXERS = 2
CONV_WIDTH = 31
N_HEADS = 16
N_KV_HEADS = 2
HEAD_DIM = 64
GROUP = N_HEADS // N_KV_HEADS
WINDOW = 128
BLOCK = WINDOW
QKV_DIM = (N_HEADS + 2 * N_KV_HEADS) * HEAD_DIM
ATTN_DIM = N_HEADS * HEAD_DIM
D_FF = 4 * D_MODEL
REL_BUCKETS = 32
REL_MAX_DIST = 128
NORM_EPS = 1e-6
N_CONV_LAYERS = (DEPTH + 1) // 2
N_ATTN_LAYERS = DEPTH // 2
NEG_INF = -1e30

kernel_name = "hybrid_conformer_conv_swa_sink_trunk"


def rms_norm(x, g):
    xf = x.astype(jnp.float32)
    y = xf * lax.rsqrt(jnp.mean(xf * xf, axis=-1, keepdims=True) + NORM_EPS)
    return (y * g.astype(jnp.float32)).astype(x.dtype)


def layer_norm(x, g, b):
    xf = x.astype(jnp.float32)
    mu = jnp.mean(xf, axis=-1, keepdims=True)
    xc = xf - mu
    y = xc * lax.rsqrt(jnp.mean(xc * xc, axis=-1, keepdims=True) + NORM_EPS)
    return (y * g.astype(jnp.float32) + b.astype(jnp.float32)).astype(x.dtype)


def t5_causal_bucket(dist):
    n = np.maximum(dist, 0)
    max_exact = REL_BUCKETS // 2
    large = max_exact + (np.log(np.maximum(n, 1).astype(np.float32) / max_exact)
                         / math.log(REL_MAX_DIST / max_exact)
                         * (REL_BUCKETS - max_exact)).astype(np.int32)
    large = np.minimum(large, REL_BUCKETS - 1)
    return np.where(n < max_exact, n, large).astype(np.int32)


def conformer_conv(x, norm_g, w_in, b_in, dw, dw_b, ln_g, ln_b, w_out, b_out):
    h = rms_norm(x, norm_g)
    u = h @ w_in + b_in
    a, gate = jnp.split(u, 2, axis=-1)
    u = a * jax.nn.sigmoid(gate)
    u = lax.conv_general_dilated(
        u, dw[:, None, :].astype(u.dtype), window_strides=(1,),
        padding=[(CONV_WIDTH - 1, 0)],
        dimension_numbers=("NWC", "WIO", "NWC"),
        feature_group_count=D_MODEL) + dw_b
    u = jax.nn.silu(layer_norm(u, ln_g, ln_b))
    return u @ w_out + b_out


def swa_sink_attention(x, norm_g, w_qkv, b_qkv, q_g, k_g, sinks, w_o, b_o, rel_bias):
    B, S, _ = x.shape
    nb = S // BLOCK
    h = rms_norm(x, norm_g)
    qkv = h @ w_qkv + b_qkv
    q, k, v = jnp.split(qkv, [ATTN_DIM, ATTN_DIM + N_KV_HEADS * HEAD_DIM], axis=-1)
    q = rms_norm(q.reshape(B, S, N_HEADS, HEAD_DIM), q_g)
    k = rms_norm(k.reshape(B, S, N_KV_HEADS, HEAD_DIM), k_g)
    v = v.reshape(B, S, N_KV_HEADS, HEAD_DIM)

    qb = q.reshape(B, nb, BLOCK, N_KV_HEADS, GROUP, HEAD_DIM)

    def band(t):
        tp = jnp.pad(t, ((0, 0), (BLOCK, 0), (0, 0), (0, 0)))
        tb = tp.reshape(B, nb + 1, BLOCK, N_KV_HEADS, HEAD_DIM)
        return jnp.concatenate([tb[:, :-1], tb[:, 1:]], axis=2)

    kband, vband = band(k), band(v)

    q_loc = np.arange(BLOCK)[:, None]
    k_loc = np.arange(2 * BLOCK)[None, :]
    dist = q_loc + BLOCK - k_loc
    in_window = (dist >= 0) & (dist < WINDOW)
    k_abs = (np.arange(nb)[:, None, None] - 1) * BLOCK + k_loc[None]
    mask = jnp.asarray(in_window[None] & (k_abs >= 0))
    bias = rel_bias[jnp.asarray(t5_causal_bucket(dist))]
    bias = jnp.transpose(bias, (2, 0, 1)).reshape(N_KV_HEADS, GROUP, BLOCK, 2 * BLOCK)

    scale = 1.0 / math.sqrt(HEAD_DIM)
    logits = jnp.einsum("bnqkgd,bnskd->bnkgqs", qb, kband).astype(jnp.float32) * scale
    logits = logits + bias.astype(jnp.float32)[None, None]
    logits = jnp.where(mask[None, :, None, None], logits, NEG_INF)
    sink_col = jnp.broadcast_to(
        sinks.astype(jnp.float32).reshape(1, 1, N_KV_HEADS, GROUP, 1, 1),
        logits.shape[:-1] + (1,))
    probs = jax.nn.softmax(jnp.concatenate([logits, sink_col], axis=-1), axis=-1)[..., :-1]
    out = jnp.einsum("bnkgqs,bnskd->bnqkgd", probs.astype(vband.dtype), vband)
    out = out.reshape(B, S, ATTN_DIM)
    return out @ w_o + b_o


def sq_relu_mlp(x, norm_g, w_up, w_down):
    h = rms_norm(x, norm_g)
    return jnp.square(jax.nn.relu(h @ w_up)) @ w_down


def _fwd_setup_inputs(seed: int = 0) -> dict:
    key = jax.random.key(seed)
    ks = iter(jax.random.split(key, 32))
    f32 = jnp.float32

    def nrm(shape, scale):
        return jax.random.normal(next(ks), shape, f32) * scale

    NC, NA = N_CONV_LAYERS, N_ATTN_LAYERS
    return {
        "x": nrm((BATCH, SEQ, D_MODEL), 1.0),
        "conv_norm_g": 1.0 + nrm((NC, D_MODEL), 0.02),
        "conv_w_in": nrm((NC, D_MODEL, 2 * D_MODEL), D_MODEL ** -0.5),
        "conv_b_in": nrm((NC, 2 * D_MODEL), 0.02),
        "conv_dw": nrm((NC, CONV_WIDTH, D_MODEL), CONV_WIDTH ** -0.5),
        "conv_dw_b": nrm((NC, D_MODEL), 0.02),
        "conv_ln_g": 1.0 + nrm((NC, D_MODEL), 0.02),
        "conv_ln_b": nrm((NC, D_MODEL), 0.02),
        "conv_w_out": nrm((NC, D_MODEL, D_MODEL), D_MODEL ** -0.5),
        "conv_b_out": nrm((NC, D_MODEL), 0.02),
        "attn_norm_g": 1.0 + nrm((NA, D_MODEL), 0.02),
        "w_qkv": nrm((NA, D_MODEL, QKV_DIM), D_MODEL ** -0.5),
        "b_qkv": nrm((NA, QKV_DIM), 0.02),
        "q_norm_g": 1.0 + nrm((NA, HEAD_DIM), 0.02),
        "k_norm_g": 1.0 + nrm((NA, HEAD_DIM), 0.02),
        "sinks": nrm((NA, N_HEADS), 0.5),
        "w_o": nrm((NA, ATTN_DIM, D_MODEL), ATTN_DIM ** -0.5),
        "b_o": nrm((NA, D_MODEL), 0.02),
        "rel_bias": nrm((REL_BUCKETS, N_HEADS), 0.5),
        "mlp_norm_g": 1.0 + nrm((DEPTH, D_MODEL), 0.02),
        "w_up": nrm((DEPTH, D_MODEL, D_FF), D_MODEL ** -0.5),
        "w_down": nrm((DEPTH, D_FF, D_MODEL), D_FF ** -0.5),
    }


def _fwd_reference(x, conv_norm_g, conv_w_in, conv_b_in, conv_dw, conv_dw_b, conv_ln_g,
              conv_ln_b, conv_w_out, conv_b_out, attn_norm_g, w_qkv, b_qkv, q_norm_g,
              k_norm_g, sinks, w_o, b_o, rel_bias, mlp_norm_g, w_up, w_down):
    for i in range(DEPTH):
        j = i // N_MIXERS
        if i % N_MIXERS == 0:
            x = x + conformer_conv(x, conv_norm_g[j], conv_w_in[j], conv_b_in[j],
                                   conv_dw[j], conv_dw_b[j], conv_ln_g[j], conv_ln_b[j],
                                   conv_w_out[j], conv_b_out[j])
        else:
            x = x + swa_sink_attention(x, attn_norm_g[j], w_qkv[j], b_qkv[j],
                                       q_norm_g[j], k_norm_g[j], sinks[j], w_o[j],
                                       b_o[j], rel_bias)
        x = x + sq_relu_mlp(x, mlp_norm_g[i], w_up[i], w_down[i])
    return x


import jax as _jax
import jax.numpy as _jnp

TWIN_FORMAT = 'train_step'
FWD_PARAMS = ['x', 'conv_norm_g', 'conv_w_in', 'conv_b_in', 'conv_dw', 'conv_dw_b', 'conv_ln_g', 'conv_ln_b', 'conv_w_out', 'conv_b_out', 'attn_norm_g', 'w_qkv', 'b_qkv', 'q_norm_g', 'k_norm_g', 'sinks', 'w_o', 'b_o', 'rel_bias', 'mlp_norm_g', 'w_up', 'w_down']
TWIN_WEIGHTS = ['conv_norm_g', 'conv_w_in', 'conv_b_in', 'conv_dw', 'conv_dw_b', 'conv_ln_g', 'conv_ln_b', 'conv_w_out', 'conv_b_out', 'attn_norm_g', 'w_qkv', 'b_qkv', 'q_norm_g', 'k_norm_g', 'sinks', 'w_o', 'b_o', 'rel_bias', 'mlp_norm_g', 'w_up', 'w_down']
TWIN_DIFF_INPUT = 'x'
TWIN_INPUTS = ['x', 'conv_norm_g', 'conv_w_in', 'conv_b_in', 'conv_dw', 'conv_dw_b', 'conv_ln_g', 'conv_ln_b', 'conv_w_out', 'conv_b_out', 'attn_norm_g', 'w_qkv', 'b_qkv', 'q_norm_g', 'k_norm_g', 'sinks', 'w_o', 'b_o', 'rel_bias', 'mlp_norm_g', 'w_up', 'w_down', 'loss_target', 'm_conv_norm_g', 'm_conv_w_in', 'm_conv_b_in', 'm_conv_dw', 'm_conv_dw_b', 'm_conv_ln_g', 'm_conv_ln_b', 'm_conv_w_out', 'm_conv_b_out', 'm_attn_norm_g', 'm_w_qkv', 'm_b_qkv', 'm_q_norm_g', 'm_k_norm_g', 'm_sinks', 'm_w_o', 'm_b_o', 'm_rel_bias', 'm_mlp_norm_g', 'm_w_up', 'm_w_down', 'v_conv_norm_g', 'v_conv_w_in', 'v_conv_b_in', 'v_conv_dw', 'v_conv_dw_b', 'v_conv_ln_g', 'v_conv_ln_b', 'v_conv_w_out', 'v_conv_b_out', 'v_attn_norm_g', 'v_w_qkv', 'v_b_qkv', 'v_q_norm_g', 'v_k_norm_g', 'v_sinks', 'v_w_o', 'v_b_o', 'v_rel_bias', 'v_mlp_norm_g', 'v_w_up', 'v_w_down']
TWIN_OUTPUTS = ['loss', 'grad_x', 'grad_conv_norm_g', 'grad_conv_w_in', 'grad_conv_b_in', 'grad_conv_dw', 'grad_conv_dw_b', 'grad_conv_ln_g', 'grad_conv_ln_b', 'grad_conv_w_out', 'grad_conv_b_out', 'grad_attn_norm_g', 'grad_w_qkv', 'grad_b_qkv', 'grad_q_norm_g', 'grad_k_norm_g', 'grad_sinks', 'grad_w_o', 'grad_b_o', 'grad_rel_bias', 'grad_mlp_norm_g', 'grad_w_up', 'grad_w_down', 'delta_conv_norm_g', 'delta_conv_w_in', 'delta_conv_b_in', 'delta_conv_dw', 'delta_conv_dw_b', 'delta_conv_ln_g', 'delta_conv_ln_b', 'delta_conv_w_out', 'delta_conv_b_out', 'delta_attn_norm_g', 'delta_w_qkv', 'delta_b_qkv', 'delta_q_norm_g', 'delta_k_norm_g', 'delta_sinks', 'delta_w_o', 'delta_b_o', 'delta_rel_bias', 'delta_mlp_norm_g', 'delta_w_up', 'delta_w_down', 'new_m_conv_norm_g', 'new_m_conv_w_in', 'new_m_conv_b_in', 'new_m_conv_dw', 'new_m_conv_dw_b', 'new_m_conv_ln_g', 'new_m_conv_ln_b', 'new_m_conv_w_out', 'new_m_conv_b_out', 'new_m_attn_norm_g', 'new_m_w_qkv', 'new_m_b_qkv', 'new_m_q_norm_g', 'new_m_k_norm_g', 'new_m_sinks', 'new_m_w_o', 'new_m_b_o', 'new_m_rel_bias', 'new_m_mlp_norm_g', 'new_m_w_up', 'new_m_w_down', 'new_v_conv_norm_g', 'new_v_conv_w_in', 'new_v_conv_b_in', 'new_v_conv_dw', 'new_v_conv_dw_b', 'new_v_conv_ln_g', 'new_v_conv_ln_b', 'new_v_conv_w_out', 'new_v_conv_b_out', 'new_v_attn_norm_g', 'new_v_w_qkv', 'new_v_b_qkv', 'new_v_q_norm_g', 'new_v_k_norm_g', 'new_v_sinks', 'new_v_w_o', 'new_v_b_o', 'new_v_rel_bias', 'new_v_mlp_norm_g', 'new_v_w_up', 'new_v_w_down']
TWIN_LEAF_KINDS = {'loss': 'loss', 'grad_x': 'grad_x', 'grad_conv_norm_g': 'grad_w', 'grad_conv_w_in': 'grad_w', 'grad_conv_b_in': 'grad_w', 'grad_conv_dw': 'grad_w', 'grad_conv_dw_b': 'grad_w', 'grad_conv_ln_g': 'grad_w', 'grad_conv_ln_b': 'grad_w', 'grad_conv_w_out': 'grad_w', 'grad_conv_b_out': 'grad_w', 'grad_attn_norm_g': 'grad_w', 'grad_w_qkv': 'grad_w', 'grad_b_qkv': 'grad_w', 'grad_q_norm_g': 'grad_w', 'grad_k_norm_g': 'grad_w', 'grad_sinks': 'grad_w', 'grad_w_o': 'grad_w', 'grad_b_o': 'grad_w', 'grad_rel_bias': 'grad_w', 'grad_mlp_norm_g': 'grad_w', 'grad_w_up': 'grad_w', 'grad_w_down': 'grad_w', 'delta_conv_norm_g': 'delta_w', 'delta_conv_w_in': 'delta_w', 'delta_conv_b_in': 'delta_w', 'delta_conv_dw': 'delta_w', 'delta_conv_dw_b': 'delta_w', 'delta_conv_ln_g': 'delta_w', 'delta_conv_ln_b': 'delta_w', 'delta_conv_w_out': 'delta_w', 'delta_conv_b_out': 'delta_w', 'delta_attn_norm_g': 'delta_w', 'delta_w_qkv': 'delta_w', 'delta_b_qkv': 'delta_w', 'delta_q_norm_g': 'delta_w', 'delta_k_norm_g': 'delta_w', 'delta_sinks': 'delta_w', 'delta_w_o': 'delta_w', 'delta_b_o': 'delta_w', 'delta_rel_bias': 'delta_w', 'delta_mlp_norm_g': 'delta_w', 'delta_w_up': 'delta_w', 'delta_w_down': 'delta_w', 'new_m_conv_norm_g': 'new_m', 'new_m_conv_w_in': 'new_m', 'new_m_conv_b_in': 'new_m', 'new_m_conv_dw': 'new_m', 'new_m_conv_dw_b': 'new_m', 'new_m_conv_ln_g': 'new_m', 'new_m_conv_ln_b': 'new_m', 'new_m_conv_w_out': 'new_m', 'new_m_conv_b_out': 'new_m', 'new_m_attn_norm_g': 'new_m', 'new_m_w_qkv': 'new_m', 'new_m_b_qkv': 'new_m', 'new_m_q_norm_g': 'new_m', 'new_m_k_norm_g': 'new_m', 'new_m_sinks': 'new_m', 'new_m_w_o': 'new_m', 'new_m_b_o': 'new_m', 'new_m_rel_bias': 'new_m', 'new_m_mlp_norm_g': 'new_m', 'new_m_w_up': 'new_m', 'new_m_w_down': 'new_m', 'new_v_conv_norm_g': 'new_v', 'new_v_conv_w_in': 'new_v', 'new_v_conv_b_in': 'new_v', 'new_v_conv_dw': 'new_v', 'new_v_conv_dw_b': 'new_v', 'new_v_conv_ln_g': 'new_v', 'new_v_conv_ln_b': 'new_v', 'new_v_conv_w_out': 'new_v', 'new_v_conv_b_out': 'new_v', 'new_v_attn_norm_g': 'new_v', 'new_v_w_qkv': 'new_v', 'new_v_b_qkv': 'new_v', 'new_v_q_norm_g': 'new_v', 'new_v_k_norm_g': 'new_v', 'new_v_sinks': 'new_v', 'new_v_w_o': 'new_v', 'new_v_b_o': 'new_v', 'new_v_rel_bias': 'new_v', 'new_v_mlp_norm_g': 'new_v', 'new_v_w_up': 'new_v', 'new_v_w_down': 'new_v'}


def _forward(args):
    return _fwd_reference(*[args[k] for k in FWD_PARAMS])


def _output_shape():
    def fwd():
        inp = _fwd_setup_inputs(0)
        return _fwd_reference(*[inp[k] for k in FWD_PARAMS])
    out = _jax.eval_shape(fwd)
    return out.shape, out.dtype

N_MICROBATCH = 1
ADAM_LR = 0.001
ADAM_B1 = 0.9
ADAM_B2 = 0.999
ADAM_EPS = 1e-08
ADAM_WD = 0.01
ADAM_STEP = 10
PER_EXAMPLE_BATCH_AXIS = {'x': 0, 'loss_target': 0}
SHARED_INPUTS = []
_WEIGHT_DTYPES = {'conv_norm_g': _jnp.float32, 'conv_w_in': _jnp.float32, 'conv_b_in': _jnp.float32, 'conv_dw': _jnp.float32, 'conv_dw_b': _jnp.float32, 'conv_ln_g': _jnp.float32, 'conv_ln_b': _jnp.float32, 'conv_w_out': _jnp.float32, 'conv_b_out': _jnp.float32, 'attn_norm_g': _jnp.float32, 'w_qkv': _jnp.float32, 'b_qkv': _jnp.float32, 'q_norm_g': _jnp.float32, 'k_norm_g': _jnp.float32, 'sinks': _jnp.float32, 'w_o': _jnp.float32, 'b_o': _jnp.float32, 'rel_bias': _jnp.float32, 'mlp_norm_g': _jnp.float32, 'w_up': _jnp.float32, 'w_down': _jnp.float32}
MOMENT_SCALE = {'conv_norm_g': 1.834108e+00, 'conv_w_in': 1.006733e+00, 'conv_b_in': 4.033091e+01, 'conv_dw': 7.076243e+00, 'conv_dw_b': 1.088321e+02, 'conv_ln_g': 5.496488e+01, 'conv_ln_b': 6.858708e+01, 'conv_w_out': 2.830391e+01, 'conv_b_out': 1.327473e+02, 'attn_norm_g': 3.623034e+01, 'w_qkv': 3.041356e+01, 'b_qkv': 8.907172e+01, 'q_norm_g': 1.099218e+01, 'k_norm_g': 1.112730e+01, 'sinks': 2.743669e+00, 'w_o': 2.891330e+01, 'b_o': 7.788911e+01, 'rel_bias': 1.159794e+00, 'mlp_norm_g': 1.996984e+02, 'w_up': 1.386137e+01, 'w_down': 4.522799e+01}


def _to_microbatches(a, axis):
    t = _jnp.moveaxis(a, axis, 0)
    t = t.reshape((N_MICROBATCH, t.shape[0] // N_MICROBATCH) + t.shape[1:])
    return _jnp.moveaxis(t, 1, axis + 1)


def setup_inputs(seed: int = 0) -> dict:
    inp = _fwd_setup_inputs(seed)
    key = _jax.random.fold_in(_jax.random.key(seed), 7919)
    shape, _ = _output_shape()
    out = dict(inp)
    out["loss_target"] = _jax.random.normal(_jax.random.fold_in(key, 0), shape, _jnp.float32)
    for i, name in enumerate(TWIN_WEIGHTS):
        w = inp[name].astype(_jnp.float32)
        if MOMENT_SCALE is None:
            s = _jnp.sqrt(_jnp.mean(_jnp.square(w)) + 1e-30)
        else:
            s = MOMENT_SCALE[name]
        km, kv = _jax.random.split(_jax.random.fold_in(key, i + 1))
        out[name] = w
        out["m_" + name] = s * _jax.random.normal(km, w.shape, _jnp.float32)
        out["v_" + name] = (s * s) * _jax.random.uniform(kv, w.shape, _jnp.float32, 0.5, 1.5)
    if N_MICROBATCH > 1:
        for name, axis in PER_EXAMPLE_BATCH_AXIS.items():
            out[name] = _to_microbatches(out[name], axis)
    return {'x': out['x'], 'conv_norm_g': out['conv_norm_g'], 'conv_w_in': out['conv_w_in'], 'conv_b_in': out['conv_b_in'], 'conv_dw': out['conv_dw'], 'conv_dw_b': out['conv_dw_b'], 'conv_ln_g': out['conv_ln_g'], 'conv_ln_b': out['conv_ln_b'], 'conv_w_out': out['conv_w_out'], 'conv_b_out': out['conv_b_out'], 'attn_norm_g': out['attn_norm_g'], 'w_qkv': out['w_qkv'], 'b_qkv': out['b_qkv'], 'q_norm_g': out['q_norm_g'], 'k_norm_g': out['k_norm_g'], 'sinks': out['sinks'], 'w_o': out['w_o'], 'b_o': out['b_o'], 'rel_bias': out['rel_bias'], 'mlp_norm_g': out['mlp_norm_g'], 'w_up': out['w_up'], 'w_down': out['w_down'], 'loss_target': out['loss_target'], 'm_conv_norm_g': out['m_conv_norm_g'], 'm_conv_w_in': out['m_conv_w_in'], 'm_conv_b_in': out['m_conv_b_in'], 'm_conv_dw': out['m_conv_dw'], 'm_conv_dw_b': out['m_conv_dw_b'], 'm_conv_ln_g': out['m_conv_ln_g'], 'm_conv_ln_b': out['m_conv_ln_b'], 'm_conv_w_out': out['m_conv_w_out'], 'm_conv_b_out': out['m_conv_b_out'], 'm_attn_norm_g': out['m_attn_norm_g'], 'm_w_qkv': out['m_w_qkv'], 'm_b_qkv': out['m_b_qkv'], 'm_q_norm_g': out['m_q_norm_g'], 'm_k_norm_g': out['m_k_norm_g'], 'm_sinks': out['m_sinks'], 'm_w_o': out['m_w_o'], 'm_b_o': out['m_b_o'], 'm_rel_bias': out['m_rel_bias'], 'm_mlp_norm_g': out['m_mlp_norm_g'], 'm_w_up': out['m_w_up'], 'm_w_down': out['m_w_down'], 'v_conv_norm_g': out['v_conv_norm_g'], 'v_conv_w_in': out['v_conv_w_in'], 'v_conv_b_in': out['v_conv_b_in'], 'v_conv_dw': out['v_conv_dw'], 'v_conv_dw_b': out['v_conv_dw_b'], 'v_conv_ln_g': out['v_conv_ln_g'], 'v_conv_ln_b': out['v_conv_ln_b'], 'v_conv_w_out': out['v_conv_w_out'], 'v_conv_b_out': out['v_conv_b_out'], 'v_attn_norm_g': out['v_attn_norm_g'], 'v_w_qkv': out['v_w_qkv'], 'v_b_qkv': out['v_b_qkv'], 'v_q_norm_g': out['v_q_norm_g'], 'v_k_norm_g': out['v_k_norm_g'], 'v_sinks': out['v_sinks'], 'v_w_o': out['v_w_o'], 'v_b_o': out['v_b_o'], 'v_rel_bias': out['v_rel_bias'], 'v_mlp_norm_g': out['v_mlp_norm_g'], 'v_w_up': out['v_w_up'], 'v_w_down': out['v_w_down']}


def _loss(weights, diff, rest, loss_target):
    with _jax.named_scope("forward"):
        args = {**rest, TWIN_DIFF_INPUT: diff, **{k: w.astype(_WEIGHT_DTYPES[k]) for k, w in weights.items()}}
        y = _forward(args)
    with _jax.named_scope("loss_head"):
        err = _jnp.square(y.astype(_jnp.float32) - loss_target)
        return 0.5 * _jnp.sum(_jnp.mean(err, axis=-1)) if err.ndim else 0.5 * err


def _adamw(w, g, m, v):
    m = ADAM_B1 * m + (1.0 - ADAM_B1) * g
    v = ADAM_B2 * v + (1.0 - ADAM_B2) * _jnp.square(g)
    m_hat = m / (1.0 - ADAM_B1 ** ADAM_STEP)
    v_hat = v / (1.0 - ADAM_B2 ** ADAM_STEP)
    delta = -ADAM_LR * (m_hat / (_jnp.sqrt(v_hat) + ADAM_EPS) + ADAM_WD * w)
    return delta, m, v


def reference(x, conv_norm_g, conv_w_in, conv_b_in, conv_dw, conv_dw_b, conv_ln_g, conv_ln_b, conv_w_out, conv_b_out, attn_norm_g, w_qkv, b_qkv, q_norm_g, k_norm_g, sinks, w_o, b_o, rel_bias, mlp_norm_g, w_up, w_down, loss_target, m_conv_norm_g, m_conv_w_in, m_conv_b_in, m_conv_dw, m_conv_dw_b, m_conv_ln_g, m_conv_ln_b, m_conv_w_out, m_conv_b_out, m_attn_norm_g, m_w_qkv, m_b_qkv, m_q_norm_g, m_k_norm_g, m_sinks, m_w_o, m_b_o, m_rel_bias, m_mlp_norm_g, m_w_up, m_w_down, v_conv_norm_g, v_conv_w_in, v_conv_b_in, v_conv_dw, v_conv_dw_b, v_conv_ln_g, v_conv_ln_b, v_conv_w_out, v_conv_b_out, v_attn_norm_g, v_w_qkv, v_b_qkv, v_q_norm_g, v_k_norm_g, v_sinks, v_w_o, v_b_o, v_rel_bias, v_mlp_norm_g, v_w_up, v_w_down):
    given = dict(x=x, conv_norm_g=conv_norm_g, conv_w_in=conv_w_in, conv_b_in=conv_b_in, conv_dw=conv_dw, conv_dw_b=conv_dw_b, conv_ln_g=conv_ln_g, conv_ln_b=conv_ln_b, conv_w_out=conv_w_out, conv_b_out=conv_b_out, attn_norm_g=attn_norm_g, w_qkv=w_qkv, b_qkv=b_qkv, q_norm_g=q_norm_g, k_norm_g=k_norm_g, sinks=sinks, w_o=w_o, b_o=b_o, rel_bias=rel_bias, mlp_norm_g=mlp_norm_g, w_up=w_up, w_down=w_down, loss_target=loss_target, m_conv_norm_g=m_conv_norm_g, m_conv_w_in=m_conv_w_in, m_conv_b_in=m_conv_b_in, m_conv_dw=m_conv_dw, m_conv_dw_b=m_conv_dw_b, m_conv_ln_g=m_conv_ln_g, m_conv_ln_b=m_conv_ln_b, m_conv_w_out=m_conv_w_out, m_conv_b_out=m_conv_b_out, m_attn_norm_g=m_attn_norm_g, m_w_qkv=m_w_qkv, m_b_qkv=m_b_qkv, m_q_norm_g=m_q_norm_g, m_k_norm_g=m_k_norm_g, m_sinks=m_sinks, m_w_o=m_w_o, m_b_o=m_b_o, m_rel_bias=m_rel_bias, m_mlp_norm_g=m_mlp_norm_g, m_w_up=m_w_up, m_w_down=m_w_down, v_conv_norm_g=v_conv_norm_g, v_conv_w_in=v_conv_w_in, v_conv_b_in=v_conv_b_in, v_conv_dw=v_conv_dw, v_conv_dw_b=v_conv_dw_b, v_conv_ln_g=v_conv_ln_g, v_conv_ln_b=v_conv_ln_b, v_conv_w_out=v_conv_w_out, v_conv_b_out=v_conv_b_out, v_attn_norm_g=v_attn_norm_g, v_w_qkv=v_w_qkv, v_b_qkv=v_b_qkv, v_q_norm_g=v_q_norm_g, v_k_norm_g=v_k_norm_g, v_sinks=v_sinks, v_w_o=v_w_o, v_b_o=v_b_o, v_rel_bias=v_rel_bias, v_mlp_norm_g=v_mlp_norm_g, v_w_up=v_w_up, v_w_down=v_w_down)
    weights = {n: given[n] for n in TWIN_WEIGHTS}
    shared = {n: given[n] for n in SHARED_INPUTS}
    per_example = {n: given[n] for n in ['x']}
    grad_fn = _jax.value_and_grad(_loss, argnums=(0, 1))

    def one_microbatch(ex, loss_target):
        ex = dict(ex)
        diff = ex.pop(TWIN_DIFF_INPUT)
        return grad_fn(weights, diff, {**shared, **ex}, loss_target)

    if N_MICROBATCH == 1:
        loss, (grad_w, grad_x) = one_microbatch(per_example, given["loss_target"])
    else:
        def body(carry, xs):
            loss_sum, grad_sum = carry
            l_k, (gw_k, gx_k) = one_microbatch(xs[0], xs[1])
            with _jax.named_scope("update"):
                return (loss_sum + l_k, _jax.tree.map(_jnp.add, grad_sum, gw_k)), gx_k

        init = (_jnp.zeros((), _jnp.float32), _jax.tree.map(_jnp.zeros_like, weights))
        (loss, grad_w), grad_x = _jax.lax.scan(body, init, (per_example, given["loss_target"]))
    with _jax.named_scope("update"):
        delta_w, new_m, new_v = {}, {}, {}
        for n in TWIN_WEIGHTS:
            delta_w[n], new_m[n], new_v[n] = _adamw(weights[n], grad_w[n], given["m_" + n], given["v_" + n])
    return (loss, grad_x, *[grad_w[n] for n in TWIN_WEIGHTS], *[delta_w[n] for n in TWIN_WEIGHTS],
            *[new_m[n] for n in TWIN_WEIGHTS], *[new_v[n] for n in TWIN_WEIGHTS])
```

```python
import math

import numpy as np
import jax
import jax.numpy as jnp
from jax import lax
from jax.experimental import pallas as pl
from jax.experimental.pallas import tpu as pltpu

F32 = jnp.float32
BF = jnp.bfloat16
MESH = pl.DeviceIdType.MESH

D_MODEL = 1024
D_FF = 4096
N_HEADS = 16
N_KV = 2
GROUP = N_HEADS // N_KV
HEAD_DIM = 64
ATTN_DIM = N_HEADS * HEAD_DIM
KV_DIM = N_KV * HEAD_DIM
QKV_DIM = ATTN_DIM + 2 * KV_DIM
BLOCK = 128
CONV_W = 31
HALO = 32
REL_BUCKETS = 32
REL_MAX_DIST = 128
NORM_EPS = 1e-6
NEG_INF = -1e30
N_SHARD = 4
LANES = 1024

ADAM_LR = 0.001
ADAM_B1 = 0.9
ADAM_B2 = 0.999
ADAM_EPS = 1e-08
ADAM_WD = 0.01
ADAM_STEP = 10

VMEM_LIMIT = 56 * 1024 * 1024


def _params(n_axes):
    return pltpu.CompilerParams(dimension_semantics=("arbitrary",) * n_axes, vmem_limit_bytes=VMEM_LIMIT)


def _dot(a, b, ca, cb):
    return lax.dot_general(a, b, (((ca,), (cb,)), ((), ())), preferred_element_type=F32)


def _mm(name, a, b, *, nt, tm, tn, tk, ep_fn, outs, a_fn=None, b_sm=False, ep_in=()):
    M, K = a.shape
    if b_sm:
        S = b.shape[0]
        N, per = (b.shape[1], b.shape[2] // tk) if nt else (S * b.shape[2], b.shape[2] // tn)
        assert (S * b.shape[2] == K) if nt else (b.shape[1] == K)
    else:
        N = b.shape[0] if nt else b.shape[1]
        assert (b.shape[1] if nt else b.shape[0]) == K
    assert M % tm == 0 and N % tn == 0 and K % tk == 0
    nk = K // tk
    ne, no = len(ep_in), len(outs)

    def body(a_ref, b_ref, *rest):
        ep_refs, out_refs = rest[:ne], rest[ne:ne + no]
        i, k = pl.program_id(1), pl.program_id(2)
        av = a_ref[...]
        if a_fn is not None:
            av = a_fn(av)
        part = _dot(av.astype(BF), b_ref[...].astype(BF), 1, 1 if nt else 0)

        def finish(acc):
            vals = ep_fn(acc, *[r[...] for r in ep_refs])
            for (kind, dt), ref, val in zip(outs, out_refs, vals):
                if kind == "tile":
                    ref[...] = val.astype(dt)
                else:
                    @pl.when(i == 0)
                    def _():
                        ref[...] = val

                    @pl.when(i > 0)
                    def _():
                        ref[...] += val

        if nk == 1:
            finish(part)
        else:
            acc_ref = rest[-1]

            @pl.when(k == 0)
            def _():
                acc_ref[...] = part

            @pl.when(k > 0)
            def _():
                acc_ref[...] += part

            @pl.when(k == nk - 1)
            def _():
                finish(acc_ref[...])

    if b_sm and nt:
        b_spec = pl.BlockSpec((None, tn, tk), lambda j, i, k: (k // per, j, k % per))
    elif b_sm:
        b_spec = pl.BlockSpec((None, tk, tn), lambda j, i, k: (j // per, k, j % per))
    elif nt:
        b_spec = pl.BlockSpec((tn, tk), lambda j, i, k: (j, k))
    else:
        b_spec = pl.BlockSpec((tk, tn), lambda j, i, k: (k, j))
    in_specs = [pl.BlockSpec((tm, tk), lambda j, i, k: (i, k)), b_spec]
    for arr, kind in ep_in:
        if kind == "tile":
            assert arr.shape == (M, N)
            in_specs.append(pl.BlockSpec((tm, tn), lambda j, i, k: (i, j)))
        else:
            assert arr.shape == (1, N)
            in_specs.append(pl.BlockSpec((1, tn), lambda j, i, k: (0, j)))
    out_shape, out_specs = [], []
    for kind, dt in outs:
        if kind == "tile":
            out_shape.append(jax.ShapeDtypeStruct((M, N), dt))
            out_specs.append(pl.BlockSpec((tm, tn), lambda j, i, k: (i, j)))
        else:
            out_shape.append(jax.ShapeDtypeStruct((1, N), F32))
            out_specs.append(pl.BlockSpec((1, tn), lambda j, i, k: (0, j)))
    return pl.pallas_call(
        body, name=name, grid=(N // tn, M // tm, nk), in_specs=in_specs, out_specs=out_specs, out_shape=out_shape,
        scratch_shapes=[pltpu.VMEM((tm, tn), F32)] if nk > 1 else [],
        compiler_params=_params(3),
    )(a, b, *[arr for arr, _ in ep_in])


def _mm_tn(name, a, b, *, tm, tn, tk, a_fn=None, out_sm=None):
    T, Ka = a.shape
    N = b.shape[1]
    assert b.shape[0] == T and T % tk == 0 and Ka % tm == 0 and N % tn == 0
    nk = T // tk

    def body(a_ref, b_ref, o_ref, acc_ref):
        k = pl.program_id(2)
        av = a_ref[...]
        if a_fn is not None:
            av = a_fn(av)
        part = _dot(av.astype(BF), b_ref[...].astype(BF), 0, 0)

        @pl.when(k == 0)
        def _():
            acc_ref[...] = part

        @pl.when(k > 0)
        def _():
            acc_ref[...] += part

        @pl.when(k == nk - 1)
        def _():
            o_ref[...] = acc_ref[...].astype(BF)

    if out_sm is None:
        out_shape = jax.ShapeDtypeStruct((Ka, N), BF)
        out_spec = pl.BlockSpec((tm, tn), lambda i, j, k: (i, j))
    else:
        per = (N // out_sm) // tn
        assert per * tn * out_sm == N
        out_shape = jax.ShapeDtypeStruct((out_sm, Ka, N // out_sm), BF)
        out_spec = pl.BlockSpec((None, tm, tn), lambda i, j, k: (j // per, i, j % per))
    return pl.pallas_call(
        body, name=name, grid=(Ka // tm, N // tn, nk),
        in_specs=[pl.BlockSpec((tk, tm), lambda i, j, k: (k, i)), pl.BlockSpec((tk, tn), lambda i, j, k: (k, j))],
        out_specs=out_spec, out_shape=out_shape, scratch_shapes=[pltpu.VMEM((tm, tn), F32)],
        compiler_params=_params(3),
    )(a, b)


def _relu2(v):
    r = jnp.maximum(v.astype(F32), 0.0)
    return r * r


def _rms_bwd_ep(dh, x, g, dres):
    rstd = lax.rsqrt(jnp.mean(x * x, axis=-1, keepdims=True) + NORM_EPS)
    xh = x * rstd
    dxh = dh * g
    dx = rstd * (dxh - xh * jnp.mean(dxh * xh, axis=-1, keepdims=True))
    tot = dres + dx
    return tot, jnp.sum(dh * xh, axis=0, keepdims=True), jnp.sum(tot, axis=0, keepdims=True)


def _rms_fwd(name, x, g, tm=512):
    T, Dm = x.shape

    def body(x_ref, g_ref, o_ref):
        xv = x_ref[...]
        rstd = lax.rsqrt(jnp.mean(xv * xv, axis=-1, keepdims=True) + NORM_EPS)
        o_ref[...] = (xv * rstd * g_ref[...]).astype(BF)

    return pl.pallas_call(
        body, name=name, grid=(T // tm,),
        in_specs=[pl.BlockSpec((tm, Dm), lambda i: (i, 0)), pl.BlockSpec((1, Dm), lambda i: (0, 0))],
        out_specs=pl.BlockSpec((tm, Dm), lambda i: (i, 0)), out_shape=jax.ShapeDtypeStruct((T, Dm), BF),
        compiler_params=_params(1),
    )(x, g)


def _head_sum(v, ones_bd):
    hi = v.astype(BF)
    lo = (v - hi.astype(F32)).astype(BF)
    return _dot(hi, ones_bd, 1, 0) + _dot(lo, ones_bd, 1, 0)


def _block_ones(n):
    idx = np.arange(n) // HEAD_DIM
    return jnp.asarray((idx[:, None] == idx[None, :]).astype(np.float32), dtype=BF)


def _qk_norm_fwd(qkv, qg_t, kg_t, tm=256):
    T = qkv.shape[0]
    scale = 1.0 / math.sqrt(HEAD_DIM)

    def body(x_ref, qg_ref, kg_ref, bq_ref, bk_ref, q_ref, k_ref, v_ref):
        q = x_ref[:, pl.ds(0, ATTN_DIM)]
        rq = lax.rsqrt(_head_sum(q * q, bq_ref[...]) * (1.0 / HEAD_DIM) + NORM_EPS)
        q_ref[...] = (q * rq * qg_ref[...] * scale).astype(BF)
        k = x_ref[:, pl.ds(ATTN_DIM, KV_DIM)]
        rk = lax.rsqrt(_head_sum(k * k, bk_ref[...]) * (1.0 / HEAD_DIM) + NORM_EPS)
        k_ref[...] = (k * rk * kg_ref[...]).astype(BF)
        v_ref[...] = x_ref[:, pl.ds(ATTN_DIM + KV_DIM, KV_DIM)].astype(BF)

    full = lambda shape: pl.BlockSpec(shape, lambda i: (0, 0))
    return pl.pallas_call(
        body, name="qk_norm_fwd", grid=(T // tm,),
        in_specs=[pl.BlockSpec((tm, QKV_DIM), lambda i: (i, 0)), full((1, ATTN_DIM)), full((1, KV_DIM)),
                  full((ATTN_DIM, ATTN_DIM)), full((KV_DIM, KV_DIM))],
        out_specs=[pl.BlockSpec((tm, ATTN_DIM), lambda i: (i, 0)), pl.BlockSpec((tm, KV_DIM), lambda i: (i, 0)),
                   pl.BlockSpec((tm, KV_DIM), lambda i: (i, 0))],
        out_shape=[jax.ShapeDtypeStruct((T, ATTN_DIM), BF), jax.ShapeDtypeStruct((T, KV_DIM), BF),
                   jax.ShapeDtypeStruct((T, KV_DIM), BF)],
        compiler_params=_params(1),
    )(qkv, qg_t, kg_t, _block_ones(ATTN_DIM), _block_ones(KV_DIM))


def _qk_norm_bwd(qkv, dqn, dkn, dv, qg_t, kg_t, tm=256):
    T = qkv.shape[0]

    def body(x_ref, dq_ref, dk_ref, dv_ref, qg_ref, kg_ref, bq_ref, bk_ref, o_ref, db_ref, dqg_ref, dkg_ref):
        i = pl.program_id(0)

        def one(x, dy, g, ones_bd):
            r = lax.rsqrt(_head_sum(x * x, ones_bd) * (1.0 / HEAD_DIM) + NORM_EPS)
            xh = x * r
            dxh = dy * g
            dx = r * (dxh - xh * (_head_sum(dxh * xh, ones_bd) * (1.0 / HEAD_DIM)))
            return dx, jnp.sum(dy * xh, axis=0, keepdims=True)

        dq, dqg = one(x_ref[:, pl.ds(0, ATTN_DIM)], dq_ref[...], qg_ref[...], bq_ref[...])
        dk, dkg = one(x_ref[:, pl.ds(ATTN_DIM, KV_DIM)], dk_ref[...], kg_ref[...], bk_ref[...])
        dvv = dv_ref[...]
        o_ref[:, pl.ds(0, ATTN_DIM)] = dq.astype(BF)
        o_ref[:, pl.ds(ATTN_DIM, KV_DIM)] = dk.astype(BF)
        o_ref[:, pl.ds(ATTN_DIM + KV_DIM, KV_DIM)] = dvv.astype(BF)
        sq, sk, sv = (jnp.sum(t, axis=0, keepdims=True) for t in (dq, dk, dvv))

        @pl.when(i == 0)
        def _():
            db_ref[:, pl.ds(0, ATTN_DIM)] = sq
            db_ref[:, pl.ds(ATTN_DIM, KV_DIM)] = sk
            db_ref[:, pl.ds(ATTN_DIM + KV_DIM, KV_DIM)] = sv
            dqg_ref[...] = dqg
            dkg_ref[...] = dkg

        @pl.when(i > 0)
        def _():
            db_ref[:, pl.ds(0, ATTN_DIM)] += sq
            db_ref[:, pl.ds(ATTN_DIM, KV_DIM)] += sk
            db_ref[:, pl.ds(ATTN_DIM + KV_DIM, KV_DIM)] += sv
            dqg_ref[...] += dqg
            dkg_ref[...] += dkg

    full = lambda shape: pl.BlockSpec(shape, lambda i: (0, 0))
    row = lambda n: pl.BlockSpec((tm, n), lambda i: (i, 0))
    return pl.pallas_call(
        body, name="qk_norm_bwd", grid=(T // tm,),
        in_specs=[row(QKV_DIM), row(ATTN_DIM), row(KV_DIM), row(KV_DIM), full((1, ATTN_DIM)), full((1, KV_DIM)),
                  full((ATTN_DIM, ATTN_DIM)), full((KV_DIM, KV_DIM))],
        out_specs=[row(QKV_DIM), full((1, QKV_DIM)), full((1, ATTN_DIM)), full((1, KV_DIM))],
        out_shape=[jax.ShapeDtypeStruct((T, QKV_DIM), BF), jax.ShapeDtypeStruct((1, QKV_DIM), F32),
                   jax.ShapeDtypeStruct((1, ATTN_DIM), F32), jax.ShapeDtypeStruct((1, KV_DIM), F32)],
        compiler_params=_params(1),
    )(qkv, dqn, dkn, dv, qg_t, kg_t, _block_ones(ATTN_DIM), _block_ones(KV_DIM))


ROWS = 64
COLS = 128


def _glu(a, g):
    return a.astype(F32) * jax.nn.sigmoid(g.astype(F32))


def _conv_fwd(u, dw_pad, dw_b, ln_g, ln_b, tm=256):
    T = u.shape[0]
    Dm = D_MODEL
    hpt = tm // HALO

    def body(ac_ref, gc_ref, ap_ref, gp_ref, w_ref, wb_ref, lg_ref, lb_ref, cv_ref, s_ref, ext):
        i = pl.program_id(0)
        ext[pl.ds(0, HALO), :] = jnp.where(i > 0, _glu(ap_ref[...], gp_ref[...]), 0.0)
        ext[pl.ds(HALO, tm), :] = _glu(ac_ref[...], gc_ref[...])

        def rows(r, carry):
            r0 = pl.multiple_of(r * ROWS, ROWS)
            for c in range(Dm // COLS):
                cs = pl.ds(c * COLS, COLS)
                xe = ext[pl.ds(r0, ROWS + HALO), cs]
                acc = jnp.zeros((ROWS, COLS), F32)
                for j in range(CONV_W):
                    off = HALO - (CONV_W - 1) + j
                    acc = acc + xe[off:off + ROWS, :] * w_ref[pl.ds(j, 1), cs]
                cv_ref[pl.ds(r0, ROWS), cs] = acc + wb_ref[:, cs]
            return carry

        lax.fori_loop(0, tm // ROWS, rows, 0)
        cv = cv_ref[...]
        xc = cv - jnp.mean(cv, axis=-1, keepdims=True)
        y = xc * lax.rsqrt(jnp.mean(xc * xc, axis=-1, keepdims=True) + NORM_EPS) * lg_ref[...] + lb_ref[...]
        s_ref[...] = (y * jax.nn.sigmoid(y)).astype(BF)

    full = lambda shape: pl.BlockSpec(shape, lambda i: (0, 0))
    return pl.pallas_call(
        body, name="conv_fwd", grid=(T // tm,),
        in_specs=[pl.BlockSpec((tm, Dm), lambda i: (i, 0)), pl.BlockSpec((tm, Dm), lambda i: (i, 1)),
                  pl.BlockSpec((HALO, Dm), lambda i: (jnp.maximum(i * hpt - 1, 0), 0)),
                  pl.BlockSpec((HALO, Dm), lambda i: (jnp.maximum(i * hpt - 1, 0), 1)),
                  full((HALO, Dm)), full((1, Dm)), full((1, Dm)), full((1, Dm))],
        out_specs=[pl.BlockSpec((tm, Dm), lambda i: (i, 0)), pl.BlockSpec((tm, Dm), lambda i: (i, 0))],
        out_shape=[jax.ShapeDtypeStruct((T, Dm), F32), jax.ShapeDtypeStruct((T, Dm), BF)],
        scratch_shapes=[pltpu.VMEM((tm + HALO, Dm), F32)],
        compiler_params=_params(1),
    )(u, u, u, u, dw_pad, dw_b, ln_g, ln_b)


def _ln_silu_bwd_ep(ds, cv, lg, lb):
    xc = cv - jnp.mean(cv, axis=-1, keepdims=True)
    rstd = lax.rsqrt(jnp.mean(xc * xc, axis=-1, keepdims=True) + NORM_EPS)
    xh = xc * rstd
    y = xh * lg + lb
    sg = jax.nn.sigmoid(y)
    dy = ds * (sg * (1.0 + y * (1.0 - sg)))
    dxh = dy * lg
    dcv = rstd * (dxh - jnp.mean(dxh, axis=-1, keepdims=True) - xh * jnp.mean(dxh * xh, axis=-1, keepdims=True))
    return (dcv, jnp.sum(dy * xh, axis=0, keepdims=True), jnp.sum(dy, axis=0, keepdims=True),
            jnp.sum(dcv, axis=0, keepdims=True))


def _conv_bwd(u, dcv, dw_pad, tm=256):
    T = u.shape[0]
    Dm = D_MODEL
    hpt = tm // HALO
    last = T // HALO - 1
    nt = T // tm

    def body(ac_ref, gc_ref, ap_ref, gp_ref, dc_ref, dn_ref, w_ref, du_ref, db_ref, dw_ref, ext_g, ext_d):
        i = pl.program_id(0)
        ext_g[pl.ds(0, HALO), :] = jnp.where(i > 0, _glu(ap_ref[...], gp_ref[...]), 0.0)
        ext_g[pl.ds(HALO, tm), :] = _glu(ac_ref[...], gc_ref[...])
        ext_d[pl.ds(0, tm), :] = dc_ref[...]
        ext_d[pl.ds(tm, HALO), :] = jnp.where(i < nt - 1, dn_ref[...], 0.0)

        @pl.when(i == 0)
        def _():
            db_ref[...] = jnp.zeros_like(db_ref)
            dw_ref[...] = jnp.zeros_like(dw_ref)

        def rows(r, carry):
            r0 = pl.multiple_of(r * ROWS, ROWS)
            rs = pl.ds(r0, ROWS)
            for c in range(Dm // COLS):
                cs = pl.ds(c * COLS, COLS)
                cs2 = pl.ds(Dm + c * COLS, COLS)
                de = ext_d[pl.ds(r0, ROWS + HALO), cs]
                ge = ext_g[pl.ds(r0, ROWS + HALO), cs]
                dcur = de[0:ROWS, :]
                acc = jnp.zeros((ROWS, COLS), F32)
                for j in range(CONV_W):
                    off = CONV_W - 1 - j
                    acc = acc + de[off:off + ROWS, :] * w_ref[pl.ds(j, 1), cs]
                    goff = HALO - (CONV_W - 1) + j
                    prod = dcur * ge[goff:goff + ROWS, :]
                    dw_ref[j, :, cs] += jnp.sum(prod.reshape(ROWS // 8, 8, COLS), axis=0)
                a = ac_ref[rs, cs].astype(F32)
                sg = jax.nn.sigmoid(gc_ref[rs, cs].astype(F32))
                da = acc * sg
                dg = acc * a * sg * (1.0 - sg)
                du_ref[rs, cs] = da.astype(BF)
                du_ref[rs, cs2] = dg.astype(BF)
                db_ref[:, cs] += jnp.sum(da, axis=0, keepdims=True)
                db_ref[:, cs2] += jnp.sum(dg, axis=0, keepdims=True)
            return carry

        lax.fori_loop(0, tm // ROWS, rows, 0)

    return pl.pallas_call(
        body, name="conv_bwd", grid=(nt,),
        in_specs=[pl.BlockSpec((tm, Dm), lambda i: (i, 0)), pl.BlockSpec((tm, Dm), lambda i: (i, 1)),
                  pl.BlockSpec((HALO, Dm), lambda i: (jnp.maximum(i * hpt - 1, 0), 0)),
                  pl.BlockSpec((HALO, Dm), lambda i: (jnp.maximum(i * hpt - 1, 0), 1)),
                  pl.BlockSpec((tm, Dm), lambda i: (i, 0)),
                  pl.BlockSpec((HALO, Dm), lambda i: (jnp.minimum((i + 1) * hpt, last), 0)),
                  pl.BlockSpec((HALO, Dm), lambda i: (0, 0))],
        out_specs=[pl.BlockSpec((tm, 2 * Dm), lambda i: (i, 0)), pl.BlockSpec((1, 2 * Dm), lambda i: (0, 0)),
                   pl.BlockSpec((HALO, 8, Dm), lambda i: (0, 0, 0))],
        out_shape=[jax.ShapeDtypeStruct((T, 2 * Dm), BF), jax.ShapeDtypeStruct((1, 2 * Dm), F32),
                   jax.ShapeDtypeStruct((HALO, 8, Dm), F32)],
        scratch_shapes=[pltpu.VMEM((tm + HALO, Dm), F32), pltpu.VMEM((tm + HALO, Dm), F32)],
        compiler_params=_params(1),
    )(u, u, u, u, dcv, dcv, dw_pad)


def _bucket_table():
    q_loc = np.arange(BLOCK)[:, None]
    k_loc = np.arange(2 * BLOCK)[None, :]
    dist = q_loc + BLOCK - k_loc
    n = np.maximum(dist, 0)
    max_exact = REL_BUCKETS // 2
    large = max_exact + (np.log(np.maximum(n, 1).astype(np.float32) / max_exact)
                         / math.log(REL_MAX_DIST / max_exact) * (REL_BUCKETS - max_exact)).astype(np.int32)
    large = np.minimum(large, REL_BUCKETS - 1)
    bucket = np.where(n < max_exact, n, large).astype(np.int32)
    return jnp.asarray(np.where((dist >= 0) & (dist < BLOCK), bucket, -1).astype(np.int32))


def _bias_table(rel_bias, bucket):
    def body(rb_ref, bk_ref, o_ref):
        bk = bk_ref[...]
        for h in range(N_HEADS):
            acc = jnp.full((BLOCK, 2 * BLOCK), NEG_INF, F32)
            for b in range(REL_BUCKETS):
                acc = jnp.where(bk == b, rb_ref[b, h], acc)
            o_ref[h] = acc

    return pl.pallas_call(
        body, name="bias_table", out_shape=jax.ShapeDtypeStruct((N_HEADS, BLOCK, 2 * BLOCK), F32),
        in_specs=[pl.BlockSpec(memory_space=pltpu.SMEM), pl.BlockSpec(memory_space=pltpu.VMEM)],
        out_specs=pl.BlockSpec(memory_space=pltpu.VMEM),
    )(rel_bias, bucket)


def _bias_grad(dbias, bucket):
    def body(db_ref, bk_ref, o_ref):
        bk = bk_ref[...]
        for b in range(REL_BUCKETS):
            sel = bk == b
            for h in range(N_HEADS):
                o_ref[b, h] = jnp.sum(jnp.where(sel, db_ref[h], 0.0))

    return pl.pallas_call(
        body, name="bias_grad", out_shape=jax.ShapeDtypeStruct((REL_BUCKETS, N_HEADS), F32),
        in_specs=[pl.BlockSpec(memory_space=pltpu.VMEM), pl.BlockSpec(memory_space=pltpu.VMEM)],
        out_specs=pl.BlockSpec(memory_space=pltpu.SMEM),
    )(dbias, bucket)


def _band_probs(q, k, bias_h, sink, first):
    s = _dot(q, k, 1, 1) + bias_h
    s = jnp.where(first, NEG_INF, s)
    m = jnp.maximum(jnp.max(s, axis=-1, keepdims=True), sink)
    p = jnp.exp(s - m)
    ps = jnp.exp(sink - m)
    inv = 1.0 / (jnp.sum(p, axis=-1, keepdims=True) + ps)
    return p * inv, ps * inv


def _band(prev_ref, cur_ref, g):
    hs = pl.ds(g * HEAD_DIM, HEAD_DIM)
    return jnp.concatenate([prev_ref[:, hs], cur_ref[:, hs]], axis=0)


def _first_mask(n):
    col = lax.broadcasted_iota(jnp.int32, (BLOCK, 2 * BLOCK), 1)
    return jnp.logical_and(n == 0, col < BLOCK)


def _attn_fwd(qn, kn, vv, bias, sinks):
    T = qn.shape[0]
    nb = T // BLOCK

    def body(sk_ref, q_ref, kc_ref, kp_ref, vc_ref, vp_ref, b_ref, o_ref):
        first = _first_mask(pl.program_id(0))
        for g in range(N_KV):
            k = _band(kp_ref, kc_ref, g)
            v = _band(vp_ref, vc_ref, g)
            for hh in range(GROUP):
                h = g * GROUP + hh
                hs = pl.ds(h * HEAD_DIM, HEAD_DIM)
                pn, _ = _band_probs(q_ref[:, hs], k, b_ref[h], sk_ref[h], first)
                o_ref[:, hs] = _dot(pn.astype(BF), v, 1, 0).astype(BF)

    cur = lambda n: (n, 0)
    prev = lambda n: (jnp.maximum(n - 1, 0), 0)
    return pl.pallas_call(
        body, name="attn_fwd", grid=(nb,),
        in_specs=[pl.BlockSpec(memory_space=pltpu.SMEM), pl.BlockSpec((BLOCK, ATTN_DIM), cur),
                  pl.BlockSpec((BLOCK, KV_DIM), cur), pl.BlockSpec((BLOCK, KV_DIM), prev),
                  pl.BlockSpec((BLOCK, KV_DIM), cur), pl.BlockSpec((BLOCK, KV_DIM), prev),
                  pl.BlockSpec((N_HEADS, BLOCK, 2 * BLOCK), lambda n: (0, 0, 0))],
        out_specs=pl.BlockSpec((BLOCK, ATTN_DIM), cur), out_shape=jax.ShapeDtypeStruct((T, ATTN_DIM), BF),
        compiler_params=_params(1),
    )(sinks, qn, kn, kn, vv, vv, bias)


def _attn_bwd(qn, kn, vv, bias, sinks, do):
    T = qn.shape[0]
    nb = T // BLOCK
    scale = 1.0 / math.sqrt(HEAD_DIM)

    def body(sk_ref, q_ref, kc_ref, kp_ref, vc_ref, vp_ref, b_ref, do_ref,
             dq_ref, dk_ref, dv_ref, db_ref, dsk_ref, dk_full, dv_full, dk_carry, dv_carry):
        n = pl.program_id(0)

        @pl.when(n == 0)
        def _():
            db_ref[...] = jnp.zeros_like(db_ref)
            dk_carry[...] = jnp.zeros_like(dk_carry)
            dv_carry[...] = jnp.zeros_like(dv_carry)
            for h in range(N_HEADS):
                dsk_ref[h] = 0.0

        @pl.when(n < nb)
        def _():
            first = _first_mask(n)
            for g in range(N_KV):
                k = _band(kp_ref, kc_ref, g)
                v = _band(vp_ref, vc_ref, g)
                dk_g = jnp.zeros((2 * BLOCK, HEAD_DIM), F32)
                dv_g = jnp.zeros((2 * BLOCK, HEAD_DIM), F32)
                for hh in range(GROUP):
                    h = g * GROUP + hh
                    hs = pl.ds(h * HEAD_DIM, HEAD_DIM)
                    q = q_ref[:, hs]
                    doh = do_ref[:, hs]
                    pn, psink = _band_probs(q, k, b_ref[h], sk_ref[h], first)
                    dp = _dot(doh, v, 1, 1)
                    delta = jnp.sum(pn * dp, axis=-1, keepdims=True)
                    ds = pn * (dp - delta)
                    dsk_ref[h] += -jnp.sum(psink * delta)
                    db_ref[h] += ds
                    dsb = ds.astype(BF)
                    dq_ref[:, hs] = _dot(dsb, k, 1, 0) * scale
                    dk_g = dk_g + _dot(dsb, q, 0, 0)
                    dv_g = dv_g + _dot(pn.astype(BF), doh, 0, 0)
                gs = pl.ds(g * HEAD_DIM, HEAD_DIM)
                dk_full[:, gs] = dk_g
                dv_full[:, gs] = dv_g

        @pl.when(n == nb)
        def _():
            dk_full[...] = jnp.zeros_like(dk_full)
            dv_full[...] = jnp.zeros_like(dv_full)

        dk_ref[...] = dk_carry[...] + dk_full[pl.ds(0, BLOCK), :]
        dv_ref[...] = dv_carry[...] + dv_full[pl.ds(0, BLOCK), :]
        dk_carry[...] = dk_full[pl.ds(BLOCK, BLOCK), :]
        dv_carry[...] = dv_full[pl.ds(BLOCK, BLOCK), :]

    cur = lambda n: (jnp.minimum(n, nb - 1), 0)
    prev = lambda n: (jnp.maximum(jnp.minimum(n, nb - 1) - 1, 0), 0)
    out_kv = lambda n: (jnp.maximum(n - 1, 0), 0)
    return pl.pallas_call(
        body, name="attn_bwd", grid=(nb + 1,),
        in_specs=[pl.BlockSpec(memory_space=pltpu.SMEM), pl.BlockSpec((BLOCK, ATTN_DIM), cur),
                  pl.BlockSpec((BLOCK, KV_DIM), cur), pl.BlockSpec((BLOCK, KV_DIM), prev),
                  pl.BlockSpec((BLOCK, KV_DIM), cur), pl.BlockSpec((BLOCK, KV_DIM), prev),
                  pl.BlockSpec((N_HEADS, BLOCK, 2 * BLOCK), lambda n: (0, 0, 0)),
                  pl.BlockSpec((BLOCK, ATTN_DIM), cur)],
        out_specs=[pl.BlockSpec((BLOCK, ATTN_DIM), cur), pl.BlockSpec((BLOCK, KV_DIM), out_kv),
                   pl.BlockSpec((BLOCK, KV_DIM), out_kv),
                   pl.BlockSpec((N_HEADS, BLOCK, 2 * BLOCK), lambda n: (0, 0, 0)),
                   pl.BlockSpec(memory_space=pltpu.SMEM)],
        out_shape=[jax.ShapeDtypeStruct((T, ATTN_DIM), F32), jax.ShapeDtypeStruct((T, KV_DIM), F32),
                   jax.ShapeDtypeStruct((T, KV_DIM), F32),
                   jax.ShapeDtypeStruct((N_HEADS, BLOCK, 2 * BLOCK), F32), jax.ShapeDtypeStruct((N_HEADS,), F32)],
        scratch_shapes=[pltpu.VMEM((2 * BLOCK, KV_DIM), F32), pltpu.VMEM((2 * BLOCK, KV_DIM), F32),
                        pltpu.VMEM((BLOCK, KV_DIM), F32), pltpu.VMEM((BLOCK, KV_DIM), F32)],
        compiler_params=_params(1),
    )(sinks, qn, kn, kn, vv, vv, bias, do)


def _coords():
    return lax.axis_index("x"), lax.axis_index("y"), lax.axis_index("c")


def _gather8(name, v, with_sum):
    R = v.shape[0]

    def body(v_ref, all_ref, *rest):
        sum_ref = rest[0] if with_sum else None
        send_sems, recv_sems, local_sem = rest[-3:]
        x, y, c = _coords()
        me = 4 * x + 2 * y + c
        local = pltpu.make_async_copy(v_ref, all_ref.at[me], local_sem)
        local.start()
        sends = []
        for k in range(1, 8):
            peer = (x ^ (k >> 2), y ^ ((k >> 1) & 1), c ^ (k & 1))
            cp = pltpu.make_async_remote_copy(src_ref=v_ref, dst_ref=all_ref.at[me], send_sem=send_sems.at[k - 1],
                                              recv_sem=recv_sems.at[k - 1], device_id=peer, device_id_type=MESH)
            cp.start()
            sends.append(cp)
        for k in range(1, 8):
            peer = (x ^ (k >> 2), y ^ ((k >> 1) & 1), c ^ (k & 1))
            pltpu.make_async_remote_copy(src_ref=v_ref, dst_ref=all_ref.at[me ^ k], send_sem=send_sems.at[k - 1],
                                         recv_sem=recv_sems.at[k - 1], device_id=peer, device_id_type=MESH).wait_recv()
        for cp in sends:
            cp.wait_send()
        local.wait()
        if with_sum:
            tot = all_ref[0]
            for d in range(1, 8):
                tot = tot + all_ref[d]
            sum_ref[...] = tot

    out_shape = [jax.ShapeDtypeStruct((8, R, LANES), F32)]
    if with_sum:
        out_shape.append(jax.ShapeDtypeStruct((R, LANES), F32))
    vm = pl.BlockSpec(memory_space=pltpu.VMEM)
    return pl.pallas_call(
        body, name=name, out_shape=out_shape, in_specs=[vm], out_specs=[vm] * len(out_shape),
        scratch_shapes=[pltpu.SemaphoreType.DMA((7,)), pltpu.SemaphoreType.DMA((7,)), pltpu.SemaphoreType.DMA],
    )(v)


CHIP_FLIPS = ((1, 0), (0, 1), (1, 1))


def _gather_weights(shards):
    n = len(shards)

    def body(*refs):
        src, dst = refs[:n], refs[n:2 * n]
        send_sems, recv_sems, local_sems = refs[2 * n:]
        x, y, c = _coords()
        s = 2 * x + y
        started = []
        for t in range(n):
            cp = pltpu.make_async_copy(src[t], dst[t].at[s], local_sems.at[t])
            cp.start()
            started.append(cp)
        sends = []
        for t in range(n):
            for j, (fx, fy) in enumerate(CHIP_FLIPS):
                cp = pltpu.make_async_remote_copy(src_ref=src[t], dst_ref=dst[t].at[s], send_sem=send_sems.at[3 * t + j],
                                                  recv_sem=recv_sems.at[3 * t + j], device_id=(x ^ fx, y ^ fy, c),
                                                  device_id_type=MESH)
                cp.start()
                sends.append(cp)
        for t in range(n):
            for j, (fx, fy) in enumerate(CHIP_FLIPS):
                ps = 2 * (x ^ fx) + (y ^ fy)
                pltpu.make_async_remote_copy(src_ref=src[t], dst_ref=dst[t].at[ps], send_sem=send_sems.at[3 * t + j],
                                             recv_sem=recv_sems.at[3 * t + j], device_id=(x ^ fx, y ^ fy, c),
                                             device_id_type=MESH).wait_recv()
        for cp in sends:
            cp.wait_send()
        for cp in started:
            cp.wait()

    anyspec = pl.BlockSpec(memory_space=pl.ANY)
    return pl.pallas_call(
        body, name="gather_weights", out_shape=[jax.ShapeDtypeStruct((N_SHARD,) + w.shape, w.dtype) for w in shards],
        in_specs=[anyspec] * n, out_specs=[anyspec] * n,
        scratch_shapes=[pltpu.SemaphoreType.DMA((3 * n,)), pltpu.SemaphoreType.DMA((3 * n,)),
                        pltpu.SemaphoreType.DMA((n,))],
    )(*shards)


def _swap_halves(grads):
    n = len(grads)

    def body(*refs):
        src, dst = refs[:n], refs[n:2 * n]
        send_sems, recv_sems = refs[2 * n:]
        x, y, c = _coords()
        cps = []
        for t in range(n):
            rh = grads[t].shape[1] // 2
            cp = pltpu.make_async_remote_copy(src_ref=src[t].at[:, pl.ds((1 - c) * rh, rh), :], dst_ref=dst[t],
                                              send_sem=send_sems.at[t], recv_sem=recv_sems.at[t],
                                              device_id=(x, y, 1 - c), device_id_type=MESH)
            cp.start()
            cps.append(cp)
        for cp in cps:
            cp.wait()

    anyspec = pl.BlockSpec(memory_space=pl.ANY)
    return pl.pallas_call(
        body, name="swap_halves",
        out_shape=[jax.ShapeDtypeStruct((g.shape[0], g.shape[1] // 2, g.shape[2]), g.dtype) for g in grads],
        in_specs=[anyspec] * n, out_specs=[anyspec] * n,
        scratch_shapes=[pltpu.SemaphoreType.DMA((n,)), pltpu.SemaphoreType.DMA((n,))],
    )(*grads)


def _scatter_chips(parts):
    n = len(parts)

    def body(*refs):
        src, dst = refs[:n], refs[n:2 * n]
        send_sems, recv_sems, local_sems = refs[2 * n:]
        x, y, c = _coords()
        s = 2 * x + y
        started = []
        for t in range(n):
            cp = pltpu.make_async_copy(src[t].at[s], dst[t].at[s], local_sems.at[t])
            cp.start()
            started.append(cp)
        sends = []
        for t in range(n):
            for j, (fx, fy) in enumerate(CHIP_FLIPS):
                ps = 2 * (x ^ fx) + (y ^ fy)
                cp = pltpu.make_async_remote_copy(src_ref=src[t].at[ps], dst_ref=dst[t].at[s],
                                                  send_sem=send_sems.at[3 * t + j], recv_sem=recv_sems.at[3 * t + j],
                                                  device_id=(x ^ fx, y ^ fy, c), device_id_type=MESH)
                cp.start()
                sends.append(cp)
        for t in range(n):
            for j, (fx, fy) in enumerate(CHIP_FLIPS):
                ps = 2 * (x ^ fx) + (y ^ fy)
                pltpu.make_async_remote_copy(src_ref=src[t].at[ps], dst_ref=dst[t].at[ps],
                                             send_sem=send_sems.at[3 * t + j], recv_sem=recv_sems.at[3 * t + j],
                                             device_id=(x ^ fx, y ^ fy, c), device_id_type=MESH).wait_recv()
        for cp in sends:
            cp.wait_send()
        for cp in started:
            cp.wait()

    anyspec = pl.BlockSpec(memory_space=pl.ANY)
    return pl.pallas_call(
        body, name="scatter_chips", out_shape=[jax.ShapeDtypeStruct(p.shape, p.dtype) for p in parts],
        in_specs=[anyspec] * n, out_specs=[anyspec] * n,
        scratch_shapes=[pltpu.SemaphoreType.DMA((3 * n,)), pltpu.SemaphoreType.DMA((3 * n,)),
                        pltpu.SemaphoreType.DMA((n,))],
    )(*parts)


def _join_halves(halves):
    n = len(halves)

    def body(*refs):
        src, dst = refs[:n], refs[n:2 * n]
        send_sems, recv_sems, local_sems = refs[2 * n:]
        x, y, c = _coords()
        started, sends = [], []
        for t in range(n):
            cp = pltpu.make_async_copy(src[t], dst[t].at[c], local_sems.at[t])
            cp.start()
            started.append(cp)
            cp = pltpu.make_async_remote_copy(src_ref=src[t], dst_ref=dst[t].at[c], send_sem=send_sems.at[t],
                                              recv_sem=recv_sems.at[t], device_id=(x, y, 1 - c), device_id_type=MESH)
            cp.start()
            sends.append(cp)
        for t in range(n):
            pltpu.make_async_remote_copy(src_ref=src[t], dst_ref=dst[t].at[1 - c], send_sem=send_sems.at[t],
                                         recv_sem=recv_sems.at[t], device_id=(x, y, 1 - c),
                                         device_id_type=MESH).wait_recv()
        for cp in sends:
            cp.wait_send()
        for cp in started:
            cp.wait()

    anyspec = pl.BlockSpec(memory_space=pl.ANY)
    return pl.pallas_call(
        body, name="join_halves", out_shape=[jax.ShapeDtypeStruct((2,) + h.shape, h.dtype) for h in halves],
        in_specs=[anyspec] * n, out_specs=[anyspec] * n,
        scratch_shapes=[pltpu.SemaphoreType.DMA((n,)), pltpu.SemaphoreType.DMA((n,)), pltpu.SemaphoreType.DMA((n,))],
    )(*halves)


def _row_block(rows):
    for rb in (512, 256, 128, 64, 32, 16):
        if rows % rb == 0:
            return rb
    raise ValueError(rows)


def _add_pair(name, a, b):
    S, Rh, C = a.shape
    rb = _row_block(Rh)

    def body(a_ref, b_ref, o_ref):
        o_ref[...] = (a_ref[...].astype(F32) + b_ref[...].astype(F32)).astype(BF)

    spec = pl.BlockSpec((None, rb, C), lambda s, r: (s, r, 0))
    return pl.pallas_call(
        body, name=name, grid=(S, Rh // rb), in_specs=[spec, spec], out_specs=spec,
        out_shape=jax.ShapeDtypeStruct((S, Rh, C), BF), compiler_params=_params(2),
    )(a, b)


def _sum_chips(name, parts):
    S, Rh, C = parts.shape
    rb = _row_block(Rh)

    def body(p_ref, o_ref):
        tot = p_ref[0].astype(F32)
        for s in range(1, S):
            tot = tot + p_ref[s].astype(F32)
        o_ref[...] = tot

    return pl.pallas_call(
        body, name=name, grid=(Rh // rb,), in_specs=[pl.BlockSpec((S, rb, C), lambda r: (0, r, 0))],
        out_specs=pl.BlockSpec((rb, C), lambda r: (r, 0)), out_shape=jax.ShapeDtypeStruct((Rh, C), F32),
        compiler_params=_params(1),
    )(parts)


def _adamw_math(w, g, m, v):
    m2 = ADAM_B1 * m + (1.0 - ADAM_B1) * g
    v2 = ADAM_B2 * v + (1.0 - ADAM_B2) * (g * g)
    m_hat = m2 / (1.0 - ADAM_B1 ** ADAM_STEP)
    v_hat = v2 / (1.0 - ADAM_B2 ** ADAM_STEP)
    delta = -ADAM_LR * (m_hat / (jnp.sqrt(v_hat) + ADAM_EPS) + ADAM_WD * w)
    return delta, m2, v2


def _adamw(name, w, m, v, gs):
    L, R, C = w.shape
    Rh = R // 2
    rb = _row_block(Rh)
    nbh = Rh // rb
    assert len(gs) == L

    def body(w_ref, m_ref, v_ref, *rest):
        g_refs, (go_ref, d_ref, m2_ref, v2_ref) = rest[:L], rest[L:]
        layer = pl.program_id(0)
        g = g_refs[0][...]
        for t in range(1, L):
            g = jnp.where(layer == t, g_refs[t][...], g)
        delta, m2, v2 = _adamw_math(w_ref[...], g, m_ref[...], v_ref[...])
        go_ref[...] = g
        d_ref[...] = delta
        m2_ref[...] = m2
        v2_ref[...] = v2

    wspec = pl.BlockSpec((None, rb, C), lambda l, h, r: (l, h * nbh + r, 0))
    gspec = pl.BlockSpec((None, rb, C), lambda l, h, r: (h, r, 0))
    return pl.pallas_call(
        body, name=name, grid=(L, 2, nbh), in_specs=[wspec] * 3 + [gspec] * L, out_specs=[wspec] * 4,
        out_shape=[jax.ShapeDtypeStruct((L, R, C), F32)] * 4, compiler_params=_params(3),
    )(w, m, v, *gs)


def _adamw_small(w, g, m, v):
    def body(w_ref, g_ref, m_ref, v_ref, d_ref, m2_ref, v2_ref):
        delta, m2, v2 = _adamw_math(w_ref[...], g_ref[...], m_ref[...], v_ref[...])
        d_ref[...] = delta
        m2_ref[...] = m2
        v2_ref[...] = v2

    return pl.pallas_call(body, name="adamw_small", out_shape=[jax.ShapeDtypeStruct(w.shape, F32)] * 3)(w, g, m, v)


def _pack(arrays):
    rows = []
    for a in arrays:
        a = a.astype(F32).reshape(-1, a.shape[-1])
        r, c = a.shape
        k = -(-c // LANES)
        a = jnp.pad(a, ((0, 0), (0, k * LANES - c))).reshape(r * k, LANES)
        rows.append(jnp.pad(a, ((0, -(r * k) % 8), (0, 0))))
    return jnp.concatenate(rows, axis=0)


def _unpack(buf, shapes):
    out, r0 = [], 0
    for shp in shapes:
        c = shp[-1]
        r = int(np.prod(shp)) // c
        k = -(-c // LANES)
        out.append(buf[r0:r0 + r * k].reshape(r, k * LANES)[:, :c].reshape(shp))
        r0 += r * k + (-(r * k) % 8)
    return out


def _mlp_fwd(tag, x, g, w_up_sm, w_down):
    h = _rms_fwd(f"mlp{tag}_norm", x, g)
    (up,) = _mm(f"mlp{tag}_up", h, w_up_sm, nt=False, b_sm=True, tm=512, tn=1024, tk=1024,
                ep_fn=lambda acc: (acc,), outs=(("tile", BF),))
    return h, up


def _mlp_bwd(tag, dy, x, g, h, up, w_up_sm, w_down):
    (dup,) = _mm(f"mlp{tag}_dup", dy, w_down, nt=True, tm=512, tn=1024, tk=1024, ep_in=((up, "tile"),),
                 ep_fn=lambda acc, u: (acc * (2.0 * jnp.maximum(u.astype(F32), 0.0)),), outs=(("tile", BF),))
    dw_down = _mm_tn(f"mlp{tag}_dw_down", up, dy, tm=1024, tn=1024, tk=512, a_fn=_relu2)
    dw_up = _mm_tn(f"mlp{tag}_dw_up", h, dup, tm=1024, tn=1024, tk=512, out_sm=N_SHARD)
    dx, dg, dx_sum = _mm(f"mlp{tag}_dx", dup, w_up_sm, nt=True, b_sm=True, tm=512, tn=1024, tk=1024,
                         ep_in=((x, "tile"), (g, "row"), (dy, "tile")), ep_fn=_rms_bwd_ep,
                         outs=(("tile", F32), ("colsum", F32), ("colsum", F32)))
    return dx, dg, dx_sum, dw_up, dw_down


def kernel(x, conv_norm_g, conv_w_in, conv_b_in, conv_dw, conv_dw_b, conv_ln_g, conv_ln_b, conv_w_out, conv_b_out, attn_norm_g, w_qkv, b_qkv, q_norm_g, k_norm_g, sinks, w_o, b_o, rel_bias, mlp_norm_g, w_up, w_down, loss_target, m_conv_norm_g, m_conv_w_in, m_conv_b_in, m_conv_dw, m_conv_dw_b, m_conv_ln_g, m_conv_ln_b, m_conv_w_out, m_conv_b_out, m_attn_norm_g, m_w_qkv, m_b_qkv, m_q_norm_g, m_k_norm_g, m_sinks, m_w_o, m_b_o, m_rel_bias, m_mlp_norm_g, m_w_up, m_w_down, v_conv_norm_g, v_conv_w_in, v_conv_b_in, v_conv_dw, v_conv_dw_b, v_conv_ln_g, v_conv_ln_b, v_conv_w_out, v_conv_b_out, v_attn_norm_g, v_w_qkv, v_b_qkv, v_q_norm_g, v_k_norm_g, v_sinks, v_w_o, v_b_o, v_rel_bias, v_mlp_norm_g, v_w_up, v_w_down):
    Dm = D_MODEL
    x2d = x[0]
    tgt = loss_target[0]
    T = x2d.shape[0]
    shard = 2 * lax.axis_index("x") + lax.axis_index("y")

    big = [conv_w_in[0], conv_w_out[0], w_qkv[0], w_o[0], w_up[0], w_up[1], w_down[0], w_down[1]]
    g_in, g_out, g_qkv, g_o, g_up0, g_up1, g_down0, g_down1 = _gather_weights([w.astype(BF) for w in big])
    w_in_sm = g_in
    w_out_f = g_out.reshape(Dm, Dm)
    w_qkv_f = jnp.transpose(g_qkv, (1, 0, 2)).reshape(Dm, QKV_DIM)
    w_o_f = g_o.reshape(ATTN_DIM, Dm)
    w_up_sm = (g_up0, g_up1)
    w_down_f = (g_down0.reshape(D_FF, Dm), g_down1.reshape(D_FF, Dm))

    sharded_small = [conv_dw[0], attn_norm_g, b_qkv, b_o]
    (gathered,) = _gather8("gather_small_weights", _pack(sharded_small), with_sum=False)
    chips = [_unpack(gathered[2 * s], [a.shape for a in sharded_small]) for s in range(N_SHARD)]
    dw_f, attn_norm_f, b_qkv_f, b_o_f = (jnp.concatenate([chips[s][t] for s in range(N_SHARD)], axis=-1)
                                         for t in range(len(sharded_small)))
    dw_pad = jnp.pad(dw_f, ((0, HALO - CONV_W), (0, 0)))

    h0 = _rms_fwd("conv_norm", x2d, conv_norm_g)
    (u,) = _mm("conv_in", h0, w_in_sm, nt=False, b_sm=True, tm=512, tn=512, tk=1024, ep_in=((conv_b_in, "row"),),
               ep_fn=lambda acc, b: (acc + b,), outs=(("tile", BF),))
    cv, s_act = _conv_fwd(u, dw_pad, conv_dw_b, conv_ln_g, conv_ln_b)
    (x1,) = _mm("conv_out", s_act, w_out_f, nt=False, tm=512, tn=1024, tk=1024,
                ep_in=((conv_b_out, "row"), (x2d, "tile")), ep_fn=lambda acc, b, r: (acc + b + r,),
                outs=(("tile", F32),))
    h1, up0 = _mlp_fwd(0, x1, mlp_norm_g[0:1], w_up_sm[0], w_down_f[0])
    (x2,) = _mm("mlp0_down", up0, w_down_f[0], nt=False, tm=512, tn=1024, tk=1024, a_fn=_relu2,
                ep_in=((x1, "tile"),), ep_fn=lambda acc, r: (acc + r,), outs=(("tile", F32),))

    h2 = _rms_fwd("attn_norm", x2, attn_norm_f)
    (qkv,) = _mm("attn_qkv", h2, w_qkv_f, nt=False, tm=512, tn=QKV_DIM, tk=1024, ep_in=((b_qkv_f, "row"),),
                 ep_fn=lambda acc, b: (acc + b,), outs=(("tile", F32),))
    qg_t = jnp.tile(q_norm_g, (1, N_HEADS))
    kg_t = jnp.tile(k_norm_g, (1, N_KV))
    qn, kn, vv = _qk_norm_fwd(qkv, qg_t, kg_t)
    bucket = _bucket_table()
    bias = _bias_table(rel_bias, bucket)
    sinks1 = sinks[0]
    att = _attn_fwd(qn, kn, vv, bias, sinks1)
    (x3,) = _mm("attn_out", att, w_o_f, nt=False, tm=512, tn=1024, tk=1024,
                ep_in=((b_o_f, "row"), (x2, "tile")), ep_fn=lambda acc, b, r: (acc + b + r,), outs=(("tile", F32),))
    h3, up1 = _mlp_fwd(1, x3, mlp_norm_g[1:2], w_up_sm[1], w_down_f[1])

    def loss_ep(acc, r, t):
        diff = acc + r - t
        return diff * (1.0 / Dm), jnp.sum(diff * diff, axis=0, keepdims=True)

    dy, sq = _mm("mlp1_down_loss", up1, w_down_f[1], nt=False, tm=512, tn=1024, tk=1024, a_fn=_relu2,
                 ep_in=((x3, "tile"), (tgt, "tile")), ep_fn=loss_ep, outs=(("tile", F32), ("colsum", F32)))
    loss = lax.psum(0.5 * jnp.sum(sq) * (1.0 / Dm), ("x", "y", "c"))

    dx3, dg_mlp1, db_o, dw_up1, dw_down1 = _mlp_bwd(1, dy, x3, mlp_norm_g[1:2], h3, up1, w_up_sm[1], w_down_f[1])

    ident = lambda acc: (acc,)
    (datt,) = _mm("attn_dout", dx3, w_o_f, nt=True, tm=512, tn=1024, tk=1024, ep_fn=ident, outs=(("tile", BF),))
    dw_o = _mm_tn("attn_dw_o", att, dx3, tm=1024, tn=1024, tk=512)
    dqn, dkn, dvv, dbias, dsinks = _attn_bwd(qn, kn, vv, bias, sinks1, datt)
    drel = _bias_grad(dbias, bucket)
    dqkv, db_qkv, dqg_t, dkg_t = _qk_norm_bwd(qkv, dqn, dkn, dvv, qg_t, kg_t)
    dw_qkv = _mm_tn("attn_dw_qkv", h2, dqkv, tm=1024, tn=QKV_DIM, tk=512)
    dx2, dg_attn, _ = _mm("attn_dx", dqkv, w_qkv_f, nt=True, tm=512, tn=1024, tk=QKV_DIM,
                             ep_in=((x2, "tile"), (attn_norm_f, "row"), (dx3, "tile")), ep_fn=_rms_bwd_ep,
                             outs=(("tile", F32), ("colsum", F32), ("colsum", F32)))

    dx1, dg_mlp0, db_out, dw_up0, dw_down0 = _mlp_bwd(0, dx2, x1, mlp_norm_g[0:1], h1, up0, w_up_sm[0], w_down_f[0])

    dcv, dln_g, dln_b, ddw_b = _mm("conv_ds", dx1, w_out_f, nt=True, tm=512, tn=1024, tk=1024,
                                   ep_in=((cv, "tile"), (conv_ln_g, "row"), (conv_ln_b, "row")),
                                   ep_fn=_ln_silu_bwd_ep,
                                   outs=(("tile", F32), ("colsum", F32), ("colsum", F32), ("colsum", F32)))
    dw_out = _mm_tn("conv_dw_out", s_act, dx1, tm=1024, tn=1024, tk=512)
    du, db_in, ddw8 = _conv_bwd(u, dcv, dw_pad)
    dw_in = _mm_tn("conv_dw_in", h0, du, tm=1024, tn=512, tk=1024, out_sm=N_SHARD)
    gx, dg_conv, _ = _mm("conv_dx", du, w_in_sm, nt=True, b_sm=True, tm=512, tn=1024, tk=512,
                         ep_in=((x2d, "tile"), (conv_norm_g, "row"), (dx1, "tile")), ep_fn=_rms_bwd_ep,
                         outs=(("tile", F32), ("colsum", F32), ("colsum", F32)))

    grads = [dw_in, dw_out.reshape(N_SHARD, Dm // N_SHARD, Dm),
             jnp.transpose(dw_qkv.reshape(Dm, N_SHARD, QKV_DIM // N_SHARD), (1, 0, 2)),
             dw_o.reshape(N_SHARD, ATTN_DIM // N_SHARD, Dm), dw_up0, dw_up1,
             dw_down0.reshape(N_SHARD, D_FF // N_SHARD, Dm), dw_down1.reshape(N_SHARD, D_FF // N_SHARD, Dm)]
    names = ["w_in", "w_out", "w_qkv", "w_o", "w_up0", "w_up1", "w_down0", "w_down1"]
    core = lax.axis_index("c")
    from_sibling = _swap_halves(grads)
    mine = [lax.dynamic_slice_in_dim(g, core * (g.shape[1] // 2), g.shape[1] // 2, axis=1) for g in grads]
    chip_sums = [_add_pair(f"add_sibling_{nm}", a, b) for nm, a, b in zip(names, mine, from_sibling)]
    by_chip = _scatter_chips(chip_sums)
    halves = [_sum_chips(f"sum_chips_{nm}", p) for nm, p in zip(names, by_chip)]
    r_in, r_out, r_qkv, r_o, r_up0, r_up1, r_down0, r_down1 = _join_halves(halves)

    big_out = {}
    for nm, w, m, v, gs in (("conv_w_in", conv_w_in, m_conv_w_in, v_conv_w_in, (r_in,)),
                            ("conv_w_out", conv_w_out, m_conv_w_out, v_conv_w_out, (r_out,)),
                            ("w_qkv", w_qkv, m_w_qkv, v_w_qkv, (r_qkv,)),
                            ("w_o", w_o, m_w_o, v_w_o, (r_o,)),
                            ("w_up", w_up, m_w_up, v_w_up, (r_up0, r_up1)),
                            ("w_down", w_down, m_w_down, v_w_down, (r_down0, r_down1))):
        big_out[nm] = _adamw(f"adamw_{nm}", w, m, v, gs)

    dqg = dqg_t.reshape(N_HEADS, HEAD_DIM).sum(axis=0, keepdims=True)
    dkg = dkg_t.reshape(N_KV, HEAD_DIM).sum(axis=0, keepdims=True)
    small_full = [dg_conv, db_in, ddw8.sum(axis=1)[:CONV_W], ddw_b, dln_g, dln_b, db_out, dg_attn, db_qkv, dqg, dkg,
                  dsinks[None, :], db_o, drel, jnp.pad(dg_mlp0, ((0, 1), (0, 0))) + jnp.pad(dg_mlp1, ((1, 0), (0, 0)))]
    _, small_sum = _gather8("reduce_small_grads", _pack(small_full), with_sum=True)
    (r_norm, r_b_in, r_dw, r_dw_b, r_ln_g, r_ln_b, r_b_out, r_attn_norm, r_b_qkv, r_qg, r_kg, r_sinks, r_b_o, r_rel,
     r_mlp_norm) = _unpack(small_sum, [a.shape for a in small_full])

    def cols(a, width):
        return lax.dynamic_slice_in_dim(a, shard * width, width, axis=a.ndim - 1)

    small_names = ["conv_norm_g", "conv_b_in", "conv_dw", "conv_dw_b", "conv_ln_g", "conv_ln_b", "conv_b_out",
                   "attn_norm_g", "b_qkv", "q_norm_g", "k_norm_g", "sinks", "b_o", "rel_bias", "mlp_norm_g"]
    small_g = [r_norm, r_b_in, cols(r_dw, Dm // N_SHARD)[None], r_dw_b, r_ln_g, r_ln_b, r_b_out,
               cols(r_attn_norm, Dm // N_SHARD), cols(r_b_qkv, QKV_DIM // N_SHARD), r_qg, r_kg, r_sinks,
               cols(r_b_o, Dm // N_SHARD), r_rel, r_mlp_norm]
    small_w = [conv_norm_g, conv_b_in, conv_dw, conv_dw_b, conv_ln_g, conv_ln_b, conv_b_out, attn_norm_g, b_qkv,
               q_norm_g, k_norm_g, sinks, b_o, rel_bias, mlp_norm_g]
    small_m = [m_conv_norm_g, m_conv_b_in, m_conv_dw, m_conv_dw_b, m_conv_ln_g, m_conv_ln_b, m_conv_b_out,
               m_attn_norm_g, m_b_qkv, m_q_norm_g, m_k_norm_g, m_sinks, m_b_o, m_rel_bias, m_mlp_norm_g]
    small_v = [v_conv_norm_g, v_conv_b_in, v_conv_dw, v_conv_dw_b, v_conv_ln_g, v_conv_ln_b, v_conv_b_out,
               v_attn_norm_g, v_b_qkv, v_q_norm_g, v_k_norm_g, v_sinks, v_b_o, v_rel_bias, v_mlp_norm_g]
    flat2 = lambda a: a.reshape(-1, a.shape[-1])
    shapes2 = [flat2(w).shape for w in small_w]
    pk = lambda arrs: _pack([flat2(a) for a in arrs])
    packed_g = pk(small_g)
    d_s, m_s, v_s = _adamw_small(pk(small_w), packed_g, pk(small_m), pk(small_v))
    small_out = {}
    for nm, w, g, d, m2, v2 in zip(small_names, small_w, _unpack(packed_g, shapes2), _unpack(d_s, shapes2),
                                   _unpack(m_s, shapes2), _unpack(v_s, shapes2)):
        small_out[nm] = tuple(a.reshape(w.shape) for a in (g, d, m2, v2))

    order = ["conv_norm_g", "conv_w_in", "conv_b_in", "conv_dw", "conv_dw_b", "conv_ln_g", "conv_ln_b", "conv_w_out",
             "conv_b_out", "attn_norm_g", "w_qkv", "b_qkv", "q_norm_g", "k_norm_g", "sinks", "w_o", "b_o", "rel_bias",
             "mlp_norm_g", "w_up", "w_down"]
    res = {**small_out, **big_out}
    outs = [loss, gx[None]]
    for part in range(4):
        outs += [res[nm][part] for nm in order]
    return tuple(outs)
```

```python
import math

import numpy as np
import jax
import jax.numpy as jnp
from jax import lax
from jax.experimental import pallas as pl
from jax.experimental.pallas import tpu as pltpu

F32 = jnp.float32
BF = jnp.bfloat16
MESH = pl.DeviceIdType.MESH

D_MODEL = 1024
D_FF = 4096
N_HEADS = 16
N_KV = 2
GROUP = N_HEADS // N_KV
HEAD_DIM = 64
ATTN_DIM = N_HEADS * HEAD_DIM
KV_DIM = N_KV * HEAD_DIM
QKV_DIM = ATTN_DIM + 2 * KV_DIM
BLOCK = 128
CONV_W = 31
HALO = 32
REL_BUCKETS = 32
REL_MAX_DIST = 128
NORM_EPS = 1e-6
NEG_INF = -1e30
N_SHARD = 4
LANES = 1024

ADAM_LR = 0.001
ADAM_B1 = 0.9
ADAM_B2 = 0.999
ADAM_EPS = 1e-08
ADAM_WD = 0.01
ADAM_STEP = 10

VMEM_LIMIT = 56 * 1024 * 1024


def _params(n_axes):
    return pltpu.CompilerParams(dimension_semantics=("arbitrary",) * n_axes, vmem_limit_bytes=VMEM_LIMIT)


def _dot(a, b, ca, cb):
    return lax.dot_general(a, b, (((ca,), (cb,)), ((), ())), preferred_element_type=F32)


def _mm(name, a, b, *, nt, tm, tn, tk, ep_fn, outs, a_fn=None, b_sm=False, ep_in=()):
    M, K = a.shape
    if b_sm:
        S = b.shape[0]
        N, per = (b.shape[1], b.shape[2] // tk) if nt else (S * b.shape[2], b.shape[2] // tn)
        assert (S * b.shape[2] == K) if nt else (b.shape[1] == K)
    else:
        N = b.shape[0] if nt else b.shape[1]
        assert (b.shape[1] if nt else b.shape[0]) == K
    assert M % tm == 0 and N % tn == 0 and K % tk == 0
    nk = K // tk
    ne, no = len(ep_in), len(outs)

    def body(a_ref, b_ref, *rest):
        ep_refs, out_refs = rest[:ne], rest[ne:ne + no]
        i, k = pl.program_id(1), pl.program_id(2)
        av = a_ref[...]
        if a_fn is not None:
            av = a_fn(av)
        part = _dot(av.astype(BF), b_ref[...].astype(BF), 1, 1 if nt else 0)

        def finish(acc):
            vals = ep_fn(acc, *[r[...] for r in ep_refs])
            for (kind, dt), ref, val in zip(outs, out_refs, vals):
                if kind == "tile":
                    ref[...] = val.astype(dt)
                else:
                    @pl.when(i == 0)
                    def _():
                        ref[...] = val

                    @pl.when(i > 0)
                    def _():
                        ref[...] += val

        if nk == 1:
            finish(part)
        else:
            acc_ref = rest[-1]

            @pl.when(k == 0)
            def _():
                acc_ref[...] = part

            @pl.when(k > 0)
            def _():
                acc_ref[...] += part

            @pl.when(k == nk - 1)
            def _():
                finish(acc_ref[...])

    if b_sm and nt:
        b_spec = pl.BlockSpec((None, tn, tk), lambda j, i, k: (k // per, j, k % per))
    elif b_sm:
        b_spec = pl.BlockSpec((None, tk, tn), lambda j, i, k: (j // per, k, j % per))
    elif nt:
        b_spec = pl.BlockSpec((tn, tk), lambda j, i, k: (j, k))
    else:
        b_spec = pl.BlockSpec((tk, tn), lambda j, i, k: (k, j))
    in_specs = [pl.BlockSpec((tm, tk), lambda j, i, k: (i, k)), b_spec]
    for arr, kind in ep_in:
        if kind == "tile":
            assert arr.shape == (M, N)
            in_specs.append(pl.BlockSpec((tm, tn), lambda j, i, k: (i, j)))
        else:
            assert arr.shape == (1, N)
            in_specs.append(pl.BlockSpec((1, tn), lambda j, i, k: (0, j)))
    out_shape, out_specs = [], []
    for kind, dt in outs:
        if kind == "tile":
            out_shape.append(jax.ShapeDtypeStruct((M, N), dt))
            out_specs.append(pl.BlockSpec((tm, tn), lambda j, i, k: (i, j)))
        else:
            out_shape.append(jax.ShapeDtypeStruct((1, N), F32))
            out_specs.append(pl.BlockSpec((1, tn), lambda j, i, k: (0, j)))
    return pl.pallas_call(
        body, name=name, grid=(N // tn, M // tm, nk), in_specs=in_specs, out_specs=out_specs, out_shape=out_shape,
        scratch_shapes=[pltpu.VMEM((tm, tn), F32)] if nk > 1 else [],
        compiler_params=_params(3),
    )(a, b, *[arr for arr, _ in ep_in])


def _mm_tn(name, a, b, *, tm, tn, tk, a_fn=None, out_sm=None):
    T, Ka = a.shape
    N = b.shape[1]
    assert b.shape[0] == T and T % tk == 0 and Ka % tm == 0 and N % tn == 0
    nk = T // tk

    def body(a_ref, b_ref, o_ref, acc_ref):
        k = pl.program_id(2)
        av = a_ref[...]
        if a_fn is not None:
            av = a_fn(av)
        part = _dot(av.astype(BF), b_ref[...].astype(BF), 0, 0)

        @pl.when(k == 0)
        def _():
            acc_ref[...] = part

        @pl.when(k > 0)
        def _():
            acc_ref[...] += part

        @pl.when(k == nk - 1)
        def _():
            o_ref[...] = acc_ref[...].astype(BF)

    if out_sm is None:
        out_shape = jax.ShapeDtypeStruct((Ka, N), BF)
        out_spec = pl.BlockSpec((tm, tn), lambda i, j, k: (i, j))
    else:
        per = (N // out_sm) // tn
        assert per * tn * out_sm == N
        out_shape = jax.ShapeDtypeStruct((out_sm, Ka, N // out_sm), BF)
        out_spec = pl.BlockSpec((None, tm, tn), lambda i, j, k: (j // per, i, j % per))
    return pl.pallas_call(
        body, name=name, grid=(Ka // tm, N // tn, nk),
        in_specs=[pl.BlockSpec((tk, tm), lambda i, j, k: (k, i)), pl.BlockSpec((tk, tn), lambda i, j, k: (k, j))],
        out_specs=out_spec, out_shape=out_shape, scratch_shapes=[pltpu.VMEM((tm, tn), F32)],
        compiler_params=_params(3),
    )(a, b)


def _relu2(v):
    r = jnp.maximum(v.astype(F32), 0.0)
    return r * r


def _rms_bwd_ep(dh, x, g, dres):
    rstd = lax.rsqrt(jnp.mean(x * x, axis=-1, keepdims=True) + NORM_EPS)
    xh = x * rstd
    dxh = dh * g
    dx = rstd * (dxh - xh * jnp.mean(dxh * xh, axis=-1, keepdims=True))
    tot = dres + dx
    return tot, jnp.sum(dh * xh, axis=0, keepdims=True), jnp.sum(tot, axis=0, keepdims=True)


def _rms_fwd(name, x, g, tm=512, deps=()):
    T, Dm = x.shape

    def body(x_ref, g_ref, *rest):
        o_ref = rest[-1]
        xv = x_ref[...]
        rstd = lax.rsqrt(jnp.mean(xv * xv, axis=-1, keepdims=True) + NORM_EPS)
        o_ref[...] = (xv * rstd * g_ref[...]).astype(BF)

    return pl.pallas_call(
        body, name=name, grid=(T // tm,),
        in_specs=[pl.BlockSpec((tm, Dm), lambda i: (i, 0)), pl.BlockSpec((1, Dm), lambda i: (0, 0))]
        + [pl.BlockSpec(memory_space=pl.ANY)] * len(deps),
        out_specs=pl.BlockSpec((tm, Dm), lambda i: (i, 0)), out_shape=jax.ShapeDtypeStruct((T, Dm), BF),
        compiler_params=_params(1),
    )(x, g, *deps)


def _head_sum(v, ones_bd):
    hi = v.astype(BF)
    lo = (v - hi.astype(F32)).astype(BF)
    return _dot(hi, ones_bd, 1, 0) + _dot(lo, ones_bd, 1, 0)


def _block_ones(n):
    idx = np.arange(n) // HEAD_DIM
    return jnp.asarray((idx[:, None] == idx[None, :]).astype(np.float32), dtype=BF)


def _qk_norm_fwd(qkv, qg_t, kg_t, tm=256):
    T = qkv.shape[0]
    scale = 1.0 / math.sqrt(HEAD_DIM)

    def body(x_ref, qg_ref, kg_ref, bq_ref, bk_ref, q_ref, k_ref, v_ref):
        q = x_ref[:, pl.ds(0, ATTN_DIM)]
        rq = lax.rsqrt(_head_sum(q * q, bq_ref[...]) * (1.0 / HEAD_DIM) + NORM_EPS)
        q_ref[...] = (q * rq * qg_ref[...] * scale).astype(BF)
        k = x_ref[:, pl.ds(ATTN_DIM, KV_DIM)]
        rk = lax.rsqrt(_head_sum(k * k, bk_ref[...]) * (1.0 / HEAD_DIM) + NORM_EPS)
        k_ref[...] = (k * rk * kg_ref[...]).astype(BF)
        v_ref[...] = x_ref[:, pl.ds(ATTN_DIM + KV_DIM, KV_DIM)].astype(BF)

    full = lambda shape: pl.BlockSpec(shape, lambda i: (0, 0))
    return pl.pallas_call(
        body, name="qk_norm_fwd", grid=(T // tm,),
        in_specs=[pl.BlockSpec((tm, QKV_DIM), lambda i: (i, 0)), full((1, ATTN_DIM)), full((1, KV_DIM)),
                  full((ATTN_DIM, ATTN_DIM)), full((KV_DIM, KV_DIM))],
        out_specs=[pl.BlockSpec((tm, ATTN_DIM), lambda i: (i, 0)), pl.BlockSpec((tm, KV_DIM), lambda i: (i, 0)),
                   pl.BlockSpec((tm, KV_DIM), lambda i: (i, 0))],
        out_shape=[jax.ShapeDtypeStruct((T, ATTN_DIM), BF), jax.ShapeDtypeStruct((T, KV_DIM), BF),
                   jax.ShapeDtypeStruct((T, KV_DIM), BF)],
        compiler_params=_params(1),
    )(qkv, qg_t, kg_t, _block_ones(ATTN_DIM), _block_ones(KV_DIM))


def _qk_norm_bwd(qkv, dqn, dkn, dv, qg_t, kg_t, tm=256):
    T = qkv.shape[0]

    def body(x_ref, dq_ref, dk_ref, dv_ref, qg_ref, kg_ref, bq_ref, bk_ref, o_ref, db_ref, dqg_ref, dkg_ref):
        i = pl.program_id(0)

        def one(x, dy, g, ones_bd):
            r = lax.rsqrt(_head_sum(x * x, ones_bd) * (1.0 / HEAD_DIM) + NORM_EPS)
            xh = x * r
            dxh = dy * g
            dx = r * (dxh - xh * (_head_sum(dxh * xh, ones_bd) * (1.0 / HEAD_DIM)))
            return dx, jnp.sum(dy * xh, axis=0, keepdims=True)

        dq, dqg = one(x_ref[:, pl.ds(0, ATTN_DIM)], dq_ref[...], qg_ref[...], bq_ref[...])
        dk, dkg = one(x_ref[:, pl.ds(ATTN_DIM, KV_DIM)], dk_ref[...], kg_ref[...], bk_ref[...])
        dvv = dv_ref[...]
        o_ref[:, pl.ds(0, ATTN_DIM)] = dq.astype(BF)
        o_ref[:, pl.ds(ATTN_DIM, KV_DIM)] = dk.astype(BF)
        o_ref[:, pl.ds(ATTN_DIM + KV_DIM, KV_DIM)] = dvv.astype(BF)
        sq, sk, sv = (jnp.sum(t, axis=0, keepdims=True) for t in (dq, dk, dvv))

        @pl.when(i == 0)
        def _():
            db_ref[:, pl.ds(0, ATTN_DIM)] = sq
            db_ref[:, pl.ds(ATTN_DIM, KV_DIM)] = sk
            db_ref[:, pl.ds(ATTN_DIM + KV_DIM, KV_DIM)] = sv
            dqg_ref[...] = dqg
            dkg_ref[...] = dkg

        @pl.when(i > 0)
        def _():
            db_ref[:, pl.ds(0, ATTN_DIM)] += sq
            db_ref[:, pl.ds(ATTN_DIM, KV_DIM)] += sk
            db_ref[:, pl.ds(ATTN_DIM + KV_DIM, KV_DIM)] += sv
            dqg_ref[...] += dqg
            dkg_ref[...] += dkg

    full = lambda shape: pl.BlockSpec(shape, lambda i: (0, 0))
    row = lambda n: pl.BlockSpec((tm, n), lambda i: (i, 0))
    return pl.pallas_call(
        body, name="qk_norm_bwd", grid=(T // tm,),
        in_specs=[row(QKV_DIM), row(ATTN_DIM), row(KV_DIM), row(KV_DIM), full((1, ATTN_DIM)), full((1, KV_DIM)),
                  full((ATTN_DIM, ATTN_DIM)), full((KV_DIM, KV_DIM))],
        out_specs=[row(QKV_DIM), full((1, QKV_DIM)), full((1, ATTN_DIM)), full((1, KV_DIM))],
        out_shape=[jax.ShapeDtypeStruct((T, QKV_DIM), BF), jax.ShapeDtypeStruct((1, QKV_DIM), F32),
                   jax.ShapeDtypeStruct((1, ATTN_DIM), F32), jax.ShapeDtypeStruct((1, KV_DIM), F32)],
        compiler_params=_params(1),
    )(qkv, dqn, dkn, dv, qg_t, kg_t, _block_ones(ATTN_DIM), _block_ones(KV_DIM))


ROWS = 64
COLS = 128


def _glu(a, g):
    return a.astype(F32) * jax.nn.sigmoid(g.astype(F32))


def _conv_fwd(u, dw_pad, dw_b, ln_g, ln_b, tm=256):
    T = u.shape[0]
    Dm = D_MODEL
    hpt = tm // HALO

    def body(ac_ref, gc_ref, ap_ref, gp_ref, w_ref, wb_ref, lg_ref, lb_ref, cv_ref, s_ref, ext):
        i = pl.program_id(0)
        ext[pl.ds(0, HALO), :] = jnp.where(i > 0, _glu(ap_ref[...], gp_ref[...]), 0.0)
        ext[pl.ds(HALO, tm), :] = _glu(ac_ref[...], gc_ref[...])

        def rows(r, carry):
            r0 = pl.multiple_of(r * ROWS, ROWS)
            for c in range(Dm // COLS):
                cs = pl.ds(c * COLS, COLS)
                xe = ext[pl.ds(r0, ROWS + HALO), cs]
                acc = jnp.zeros((ROWS, COLS), F32)
                for j in range(CONV_W):
                    off = HALO - (CONV_W - 1) + j
                    acc = acc + xe[off:off + ROWS, :] * w_ref[pl.ds(j, 1), cs]
                cv_ref[pl.ds(r0, ROWS), cs] = acc + wb_ref[:, cs]
            return carry

        lax.fori_loop(0, tm // ROWS, rows, 0)
        cv = cv_ref[...]
        xc = cv - jnp.mean(cv, axis=-1, keepdims=True)
        y = xc * lax.rsqrt(jnp.mean(xc * xc, axis=-1, keepdims=True) + NORM_EPS) * lg_ref[...] + lb_ref[...]
        s_ref[...] = (y * jax.nn.sigmoid(y)).astype(BF)

    full = lambda shape: pl.BlockSpec(shape, lambda i: (0, 0))
    return pl.pallas_call(
        body, name="conv_fwd", grid=(T // tm,),
        in_specs=[pl.BlockSpec((tm, Dm), lambda i: (i, 0)), pl.BlockSpec((tm, Dm), lambda i: (i, 1)),
                  pl.BlockSpec((HALO, Dm), lambda i: (jnp.maximum(i * hpt - 1, 0), 0)),
                  pl.BlockSpec((HALO, Dm), lambda i: (jnp.maximum(i * hpt - 1, 0), 1)),
                  full((HALO, Dm)), full((1, Dm)), full((1, Dm)), full((1, Dm))],
        out_specs=[pl.BlockSpec((tm, Dm), lambda i: (i, 0)), pl.BlockSpec((tm, Dm), lambda i: (i, 0))],
        out_shape=[jax.ShapeDtypeStruct((T, Dm), F32), jax.ShapeDtypeStruct((T, Dm), BF)],
        scratch_shapes=[pltpu.VMEM((tm + HALO, Dm), F32)],
        compiler_params=_params(1),
    )(u, u, u, u, dw_pad, dw_b, ln_g, ln_b)


def _ln_silu_bwd_ep(ds, cv, lg, lb):
    xc = cv - jnp.mean(cv, axis=-1, keepdims=True)
    rstd = lax.rsqrt(jnp.mean(xc * xc, axis=-1, keepdims=True) + NORM_EPS)
    xh = xc * rstd
    y = xh * lg + lb
    sg = jax.nn.sigmoid(y)
    dy = ds * (sg * (1.0 + y * (1.0 - sg)))
    dxh = dy * lg
    dcv = rstd * (dxh - jnp.mean(dxh, axis=-1, keepdims=True) - xh * jnp.mean(dxh * xh, axis=-1, keepdims=True))
    return (dcv, jnp.sum(dy * xh, axis=0, keepdims=True), jnp.sum(dy, axis=0, keepdims=True),
            jnp.sum(dcv, axis=0, keepdims=True))


def _conv_bwd(u, dcv, dw_pad, tm=256):
    T = u.shape[0]
    Dm = D_MODEL
    hpt = tm // HALO
    last = T // HALO - 1
    nt = T // tm

    def body(ac_ref, gc_ref, ap_ref, gp_ref, dc_ref, dn_ref, w_ref, du_ref, db_ref, dw_ref, ext_g, ext_d):
        i = pl.program_id(0)
        ext_g[pl.ds(0, HALO), :] = jnp.where(i > 0, _glu(ap_ref[...], gp_ref[...]), 0.0)
        ext_g[pl.ds(HALO, tm), :] = _glu(ac_ref[...], gc_ref[...])
        ext_d[pl.ds(0, tm), :] = dc_ref[...]
        ext_d[pl.ds(tm, HALO), :] = jnp.where(i < nt - 1, dn_ref[...], 0.0)

        @pl.when(i == 0)
        def _():
            db_ref[...] = jnp.zeros_like(db_ref)
            dw_ref[...] = jnp.zeros_like(dw_ref)

        def rows(r, carry):
            r0 = pl.multiple_of(r * ROWS, ROWS)
            rs = pl.ds(r0, ROWS)
            for c in range(Dm // COLS):
                cs = pl.ds(c * COLS, COLS)
                cs2 = pl.ds(Dm + c * COLS, COLS)
                de = ext_d[pl.ds(r0, ROWS + HALO), cs]
                ge = ext_g[pl.ds(r0, ROWS + HALO), cs]
                dcur = de[0:ROWS, :]
                acc = jnp.zeros((ROWS, COLS), F32)
                for j in range(CONV_W):
                    off = CONV_W - 1 - j
                    acc = acc + de[off:off + ROWS, :] * w_ref[pl.ds(j, 1), cs]
                    goff = HALO - (CONV_W - 1) + j
                    prod = dcur * ge[goff:goff + ROWS, :]
                    dw_ref[j, :, cs] += jnp.sum(prod.reshape(ROWS // 8, 8, COLS), axis=0)
                a = ac_ref[rs, cs].astype(F32)
                sg = jax.nn.sigmoid(gc_ref[rs, cs].astype(F32))
                da = acc * sg
                dg = acc * a * sg * (1.0 - sg)
                du_ref[rs, cs] = da.astype(BF)
                du_ref[rs, cs2] = dg.astype(BF)
                db_ref[:, cs] += jnp.sum(da, axis=0, keepdims=True)
                db_ref[:, cs2] += jnp.sum(dg, axis=0, keepdims=True)
            return carry

        lax.fori_loop(0, tm // ROWS, rows, 0)

    return pl.pallas_call(
        body, name="conv_bwd", grid=(nt,),
        in_specs=[pl.BlockSpec((tm, Dm), lambda i: (i, 0)), pl.BlockSpec((tm, Dm), lambda i: (i, 1)),
                  pl.BlockSpec((HALO, Dm), lambda i: (jnp.maximum(i * hpt - 1, 0), 0)),
                  pl.BlockSpec((HALO, Dm), lambda i: (jnp.maximum(i * hpt - 1, 0), 1)),
                  pl.BlockSpec((tm, Dm), lambda i: (i, 0)),
                  pl.BlockSpec((HALO, Dm), lambda i: (jnp.minimum((i + 1) * hpt, last), 0)),
                  pl.BlockSpec((HALO, Dm), lambda i: (0, 0))],
        out_specs=[pl.BlockSpec((tm, 2 * Dm), lambda i: (i, 0)), pl.BlockSpec((1, 2 * Dm), lambda i: (0, 0)),
                   pl.BlockSpec((HALO, 8, Dm), lambda i: (0, 0, 0))],
        out_shape=[jax.ShapeDtypeStruct((T, 2 * Dm), BF), jax.ShapeDtypeStruct((1, 2 * Dm), F32),
                   jax.ShapeDtypeStruct((HALO, 8, Dm), F32)],
        scratch_shapes=[pltpu.VMEM((tm + HALO, Dm), F32), pltpu.VMEM((tm + HALO, Dm), F32)],
        compiler_params=_params(1),
    )(u, u, u, u, dcv, dcv, dw_pad)


def _bucket_table():
    q_loc = np.arange(BLOCK)[:, None]
    k_loc = np.arange(2 * BLOCK)[None, :]
    dist = q_loc + BLOCK - k_loc
    n = np.maximum(dist, 0)
    max_exact = REL_BUCKETS // 2
    large = max_exact + (np.log(np.maximum(n, 1).astype(np.float32) / max_exact)
                         / math.log(REL_MAX_DIST / max_exact) * (REL_BUCKETS - max_exact)).astype(np.int32)
    large = np.minimum(large, REL_BUCKETS - 1)
    bucket = np.where(n < max_exact, n, large).astype(np.int32)
    return jnp.asarray(np.where((dist >= 0) & (dist < BLOCK), bucket, -1).astype(np.int32))


def _bias_table(rel_bias, bucket):
    def body(rb_ref, bk_ref, o_ref):
        bk = bk_ref[...]
        for h in range(N_HEADS):
            acc = jnp.full((BLOCK, 2 * BLOCK), NEG_INF, F32)
            for b in range(REL_BUCKETS):
                acc = jnp.where(bk == b, rb_ref[b, h], acc)
            o_ref[h] = acc

    return pl.pallas_call(
        body, name="bias_table", out_shape=jax.ShapeDtypeStruct((N_HEADS, BLOCK, 2 * BLOCK), F32),
        in_specs=[pl.BlockSpec(memory_space=pltpu.SMEM), pl.BlockSpec(memory_space=pltpu.VMEM)],
        out_specs=pl.BlockSpec(memory_space=pltpu.VMEM),
    )(rel_bias, bucket)


def _bias_grad(dbias, bucket):
    def body(db_ref, bk_ref, o_ref):
        bk = bk_ref[...]
        for b in range(REL_BUCKETS):
            sel = bk == b
            for h in range(N_HEADS):
                o_ref[b, h] = jnp.sum(jnp.where(sel, db_ref[h], 0.0))

    return pl.pallas_call(
        body, name="bias_grad", out_shape=jax.ShapeDtypeStruct((REL_BUCKETS, N_HEADS), F32),
        in_specs=[pl.BlockSpec(memory_space=pltpu.VMEM), pl.BlockSpec(memory_space=pltpu.VMEM)],
        out_specs=pl.BlockSpec(memory_space=pltpu.SMEM),
    )(dbias, bucket)


def _band_probs(q, k, bias_h, sink, first):
    s = _dot(q, k, 1, 1) + bias_h
    s = jnp.where(first, NEG_INF, s)
    m = jnp.maximum(jnp.max(s, axis=-1, keepdims=True), sink)
    p = jnp.exp(s - m)
    ps = jnp.exp(sink - m)
    inv = 1.0 / (jnp.sum(p, axis=-1, keepdims=True) + ps)
    return p * inv, ps * inv


def _band(prev_ref, cur_ref, g):
    hs = pl.ds(g * HEAD_DIM, HEAD_DIM)
    return jnp.concatenate([prev_ref[:, hs], cur_ref[:, hs]], axis=0)


def _first_mask(n):
    col = lax.broadcasted_iota(jnp.int32, (BLOCK, 2 * BLOCK), 1)
    return jnp.logical_and(n == 0, col < BLOCK)


def _attn_fwd(qn, kn, vv, bias, sinks):
    T = qn.shape[0]
    nb = T // BLOCK

    def body(sk_ref, q_ref, kc_ref, kp_ref, vc_ref, vp_ref, b_ref, o_ref):
        first = _first_mask(pl.program_id(0))
        for g in range(N_KV):
            k = _band(kp_ref, kc_ref, g)
            v = _band(vp_ref, vc_ref, g)
            for hh in range(GROUP):
                h = g * GROUP + hh
                hs = pl.ds(h * HEAD_DIM, HEAD_DIM)
                pn, _ = _band_probs(q_ref[:, hs], k, b_ref[h], sk_ref[h], first)
                o_ref[:, hs] = _dot(pn.astype(BF), v, 1, 0).astype(BF)

    cur = lambda n: (n, 0)
    prev = lambda n: (jnp.maximum(n - 1, 0), 0)
    return pl.pallas_call(
        body, name="attn_fwd", grid=(nb,),
        in_specs=[pl.BlockSpec(memory_space=pltpu.SMEM), pl.BlockSpec((BLOCK, ATTN_DIM), cur),
                  pl.BlockSpec((BLOCK, KV_DIM), cur), pl.BlockSpec((BLOCK, KV_DIM), prev),
                  pl.BlockSpec((BLOCK, KV_DIM), cur), pl.BlockSpec((BLOCK, KV_DIM), prev),
                  pl.BlockSpec((N_HEADS, BLOCK, 2 * BLOCK), lambda n: (0, 0, 0))],
        out_specs=pl.BlockSpec((BLOCK, ATTN_DIM), cur), out_shape=jax.ShapeDtypeStruct((T, ATTN_DIM), BF),
        compiler_params=_params(1),
    )(sinks, qn, kn, kn, vv, vv, bias)


def _attn_bwd(qn, kn, vv, bias, sinks, do):
    T = qn.shape[0]
    nb = T // BLOCK
    scale = 1.0 / math.sqrt(HEAD_DIM)

    def body(sk_ref, q_ref, kc_ref, kp_ref, vc_ref, vp_ref, b_ref, do_ref,
             dq_ref, dk_ref, dv_ref, db_ref, dsk_ref, dk_full, dv_full, dk_carry, dv_carry):
        n = pl.program_id(0)

        @pl.when(n == 0)
        def _():
            db_ref[...] = jnp.zeros_like(db_ref)
            dk_carry[...] = jnp.zeros_like(dk_carry)
            dv_carry[...] = jnp.zeros_like(dv_carry)
            for h in range(N_HEADS):
                dsk_ref[h] = 0.0

        @pl.when(n < nb)
        def _():
            first = _first_mask(n)
            for g in range(N_KV):
                k = _band(kp_ref, kc_ref, g)
                v = _band(vp_ref, vc_ref, g)
                dk_g = jnp.zeros((2 * BLOCK, HEAD_DIM), F32)
                dv_g = jnp.zeros((2 * BLOCK, HEAD_DIM), F32)
                for hh in range(GROUP):
                    h = g * GROUP + hh
                    hs = pl.ds(h * HEAD_DIM, HEAD_DIM)
                    q = q_ref[:, hs]
                    doh = do_ref[:, hs]
                    pn, psink = _band_probs(q, k, b_ref[h], sk_ref[h], first)
                    dp = _dot(doh, v, 1, 1)
                    delta = jnp.sum(pn * dp, axis=-1, keepdims=True)
                    ds = pn * (dp - delta)
                    dsk_ref[h] += -jnp.sum(psink * delta)
                    db_ref[h] += ds
                    dsb = ds.astype(BF)
                    dq_ref[:, hs] = _dot(dsb, k, 1, 0) * scale
                    dk_g = dk_g + _dot(dsb, q, 0, 0)
                    dv_g = dv_g + _dot(pn.astype(BF), doh, 0, 0)
                gs = pl.ds(g * HEAD_DIM, HEAD_DIM)
                dk_full[:, gs] = dk_g
                dv_full[:, gs] = dv_g

        @pl.when(n == nb)
        def _():
            dk_full[...] = jnp.zeros_like(dk_full)
            dv_full[...] = jnp.zeros_like(dv_full)

        dk_ref[...] = dk_carry[...] + dk_full[pl.ds(0, BLOCK), :]
        dv_ref[...] = dv_carry[...] + dv_full[pl.ds(0, BLOCK), :]
        dk_carry[...] = dk_full[pl.ds(BLOCK, BLOCK), :]
        dv_carry[...] = dv_full[pl.ds(BLOCK, BLOCK), :]

    cur = lambda n: (jnp.minimum(n, nb - 1), 0)
    prev = lambda n: (jnp.maximum(jnp.minimum(n, nb - 1) - 1, 0), 0)
    out_kv = lambda n: (jnp.maximum(n - 1, 0), 0)
    return pl.pallas_call(
        body, name="attn_bwd", grid=(nb + 1,),
        in_specs=[pl.BlockSpec(memory_space=pltpu.SMEM), pl.BlockSpec((BLOCK, ATTN_DIM), cur),
                  pl.BlockSpec((BLOCK, KV_DIM), cur), pl.BlockSpec((BLOCK, KV_DIM), prev),
                  pl.BlockSpec((BLOCK, KV_DIM), cur), pl.BlockSpec((BLOCK, KV_DIM), prev),
                  pl.BlockSpec((N_HEADS, BLOCK, 2 * BLOCK), lambda n: (0, 0, 0)),
                  pl.BlockSpec((BLOCK, ATTN_DIM), cur)],
        out_specs=[pl.BlockSpec((BLOCK, ATTN_DIM), cur), pl.BlockSpec((BLOCK, KV_DIM), out_kv),
                   pl.BlockSpec((BLOCK, KV_DIM), out_kv),
                   pl.BlockSpec((N_HEADS, BLOCK, 2 * BLOCK), lambda n: (0, 0, 0)),
                   pl.BlockSpec(memory_space=pltpu.SMEM)],
        out_shape=[jax.ShapeDtypeStruct((T, ATTN_DIM), F32), jax.ShapeDtypeStruct((T, KV_DIM), F32),
                   jax.ShapeDtypeStruct((T, KV_DIM), F32),
                   jax.ShapeDtypeStruct((N_HEADS, BLOCK, 2 * BLOCK), F32), jax.ShapeDtypeStruct((N_HEADS,), F32)],
        scratch_shapes=[pltpu.VMEM((2 * BLOCK, KV_DIM), F32), pltpu.VMEM((2 * BLOCK, KV_DIM), F32),
                        pltpu.VMEM((BLOCK, KV_DIM), F32), pltpu.VMEM((BLOCK, KV_DIM), F32)],
        compiler_params=_params(1),
    )(sinks, qn, kn, kn, vv, vv, bias, do)


def _coords():
    return lax.axis_index("x"), lax.axis_index("y"), lax.axis_index("c")


def _gather8(name, v, with_sum):
    R = v.shape[0]

    def body(v_ref, all_ref, *rest):
        sum_ref = rest[0] if with_sum else None
        send_sems, recv_sems, local_sem = rest[-3:]
        x, y, c = _coords()
        me = 4 * x + 2 * y + c
        local = pltpu.make_async_copy(v_ref, all_ref.at[me], local_sem)
        local.start()
        sends = []
        for k in range(1, 8):
            peer = (x ^ (k >> 2), y ^ ((k >> 1) & 1), c ^ (k & 1))
            cp = pltpu.make_async_remote_copy(src_ref=v_ref, dst_ref=all_ref.at[me], send_sem=send_sems.at[k - 1],
                                              recv_sem=recv_sems.at[k - 1], device_id=peer, device_id_type=MESH)
            cp.start()
            sends.append(cp)
        for k in range(1, 8):
            peer = (x ^ (k >> 2), y ^ ((k >> 1) & 1), c ^ (k & 1))
            pltpu.make_async_remote_copy(src_ref=v_ref, dst_ref=all_ref.at[me ^ k], send_sem=send_sems.at[k - 1],
                                         recv_sem=recv_sems.at[k - 1], device_id=peer, device_id_type=MESH).wait_recv()
        for cp in sends:
            cp.wait_send()
        local.wait()
        if with_sum:
            tot = all_ref[0]
            for d in range(1, 8):
                tot = tot + all_ref[d]
            sum_ref[...] = tot

    out_shape = [jax.ShapeDtypeStruct((8, R, LANES), F32)]
    if with_sum:
        out_shape.append(jax.ShapeDtypeStruct((R, LANES), F32))
    vm = pl.BlockSpec(memory_space=pltpu.VMEM)
    return pl.pallas_call(
        body, name=name, out_shape=out_shape, in_specs=[vm], out_specs=[vm] * len(out_shape),
        scratch_shapes=[pltpu.SemaphoreType.DMA((7,)), pltpu.SemaphoreType.DMA((7,)), pltpu.SemaphoreType.DMA],
    )(v)


CHIP_FLIPS = ((1, 0), (0, 1), (1, 1))


HBM_SPEC = pl.BlockSpec(memory_space=pltpu.HBM)
SEM_SPEC = pl.BlockSpec(memory_space=pltpu.SEMAPHORE)
ANY_SPEC = pl.BlockSpec(memory_space=pl.ANY)
DATAFLOW = pltpu.SideEffectType.DATAFLOW_SIDE_EFFECTING


def _chip_copy(land, sems, idx, slot_src, slot_dst, peer):
    send_sems, recv_sems = sems
    return pltpu.make_async_remote_copy(src_ref=land.at[slot_src], dst_ref=land.at[slot_dst], send_sem=send_sems.at[idx],
                                        recv_sem=recv_sems.at[idx], device_id=peer, device_id_type=MESH)


def _gather_start(stacks, groups):
    n = len(stacks)
    ng = len(groups)

    def body(*refs):
        lands = refs[:n]
        sems = [(refs[n + 2 * g], refs[n + 2 * g + 1]) for g in range(ng)]
        token = refs[-1]
        x, y, c = _coords()
        s = 2 * x + y
        for g, members in enumerate(groups):
            for i, t in enumerate(members):
                for j, (fx, fy) in enumerate(CHIP_FLIPS):
                    _chip_copy(lands[t], sems[g], 3 * i + j, s, s, (x ^ fx, y ^ fy, c)).start()
        token[...] = jnp.zeros_like(token)

    out_shape = []
    for members in groups:
        out_shape += [pltpu.SemaphoreType.DMA((3 * len(members),))] * 2
    out_shape += [pltpu.HBM(w.shape, w.dtype) for w in stacks]
    out_shape.append(jax.ShapeDtypeStruct((8, 128), F32))
    res = pl.pallas_call(
        body, name="gather_start", out_shape=out_shape, in_specs=[HBM_SPEC] * n,
        out_specs=[SEM_SPEC] * (2 * ng) + [HBM_SPEC] * n + [pl.BlockSpec(memory_space=pltpu.VMEM)],
        input_output_aliases={t: 2 * ng + t for t in range(n)},
        compiler_params=pltpu.CompilerParams(has_side_effects=DATAFLOW),
    )(*[pltpu.with_memory_space_constraint(w, pltpu.HBM) for w in stacks])
    sems = [(res[2 * g], res[2 * g + 1]) for g in range(ng)]
    return sems, list(res[2 * ng:2 * ng + n]), res[-1]


def _gather_wait(name, stacks, sems, after):
    n = len(stacks)
    after = tuple(after)

    def body(*refs):
        lands = refs[:n]
        group_sems = (refs[n], refs[n + 1])
        x, y, c = _coords()
        s = 2 * x + y
        for i in range(n):
            for j, (fx, fy) in enumerate(CHIP_FLIPS):
                cp = _chip_copy(lands[i], group_sems, 3 * i + j, s, 2 * (x ^ fx) + (y ^ fy), (x ^ fx, y ^ fy, c))
                cp.wait_send()
                cp.wait_recv()

    return pl.pallas_call(
        body, name=name, out_shape=[pltpu.HBM(w.shape, w.dtype) for w in stacks],
        in_specs=[HBM_SPEC] * n + [SEM_SPEC, SEM_SPEC] + [ANY_SPEC] * len(after), out_specs=[HBM_SPEC] * n,
        input_output_aliases={t: t for t in range(n)},
        compiler_params=pltpu.CompilerParams(has_side_effects=DATAFLOW),
    )(*stacks, sems[0], sems[1], *after)


def _swap_halves(grads):
    n = len(grads)

    def body(*refs):
        src, dst = refs[:n], refs[n:2 * n]
        send_sems, recv_sems = refs[2 * n:]
        x, y, c = _coords()
        cps = []
        for t in range(n):
            rh = grads[t].shape[1] // 2
            cp = pltpu.make_async_remote_copy(src_ref=src[t].at[:, pl.ds((1 - c) * rh, rh), :], dst_ref=dst[t],
                                              send_sem=send_sems.at[t], recv_sem=recv_sems.at[t],
                                              device_id=(x, y, 1 - c), device_id_type=MESH)
            cp.start()
            cps.append(cp)
        for cp in cps:
            cp.wait()

    anyspec = pl.BlockSpec(memory_space=pl.ANY)
    return pl.pallas_call(
        body, name="swap_halves",
        out_shape=[jax.ShapeDtypeStruct((g.shape[0], g.shape[1] // 2, g.shape[2]), g.dtype) for g in grads],
        in_specs=[anyspec] * n, out_specs=[anyspec] * n,
        scratch_shapes=[pltpu.SemaphoreType.DMA((n,)), pltpu.SemaphoreType.DMA((n,))],
    )(*grads)


def _scatter_chips(parts):
    n = len(parts)

    def body(*refs):
        src, dst = refs[:n], refs[n:2 * n]
        send_sems, recv_sems, local_sems = refs[2 * n:]
        x, y, c = _coords()
        s = 2 * x + y
        started = []
        for t in range(n):
            cp = pltpu.make_async_copy(src[t].at[s], dst[t].at[s], local_sems.at[t])
            cp.start()
            started.append(cp)
        sends = []
        for t in range(n):
            for j, (fx, fy) in enumerate(CHIP_FLIPS):
                ps = 2 * (x ^ fx) + (y ^ fy)
                cp = pltpu.make_async_remote_copy(src_ref=src[t].at[ps], dst_ref=dst[t].at[s],
                                                  send_sem=send_sems.at[3 * t + j], recv_sem=recv_sems.at[3 * t + j],
                                                  device_id=(x ^ fx, y ^ fy, c), device_id_type=MESH)
                cp.start()
                sends.append(cp)
        for t in range(n):
            for j, (fx, fy) in enumerate(CHIP_FLIPS):
                ps = 2 * (x ^ fx) + (y ^ fy)
                pltpu.make_async_remote_copy(src_ref=src[t].at[ps], dst_ref=dst[t].at[ps],
                                             send_sem=send_sems.at[3 * t + j], recv_sem=recv_sems.at[3 * t + j],
                                             device_id=(x ^ fx, y ^ fy, c), device_id_type=MESH).wait_recv()
        for cp in sends:
            cp.wait_send()
        for cp in started:
            cp.wait()

    anyspec = pl.BlockSpec(memory_space=pl.ANY)
    return pl.pallas_call(
        body, name="scatter_chips", out_shape=[jax.ShapeDtypeStruct(p.shape, p.dtype) for p in parts],
        in_specs=[anyspec] * n, out_specs=[anyspec] * n,
        scratch_shapes=[pltpu.SemaphoreType.DMA((3 * n,)), pltpu.SemaphoreType.DMA((3 * n,)),
                        pltpu.SemaphoreType.DMA((n,))],
    )(*parts)


def _join_halves(halves):
    n = len(halves)

    def body(*refs):
        src, dst = refs[:n], refs[n:2 * n]
        send_sems, recv_sems = refs[2 * n:]
        x, y, c = _coords()
        cps = []
        for t in range(n):
            cp = pltpu.make_async_remote_copy(src_ref=src[t], dst_ref=dst[t], send_sem=send_sems.at[t],
                                              recv_sem=recv_sems.at[t], device_id=(x, y, 1 - c), device_id_type=MESH)
            cp.start()
            cps.append(cp)
        for cp in cps:
            cp.wait()

    anyspec = pl.BlockSpec(memory_space=pl.ANY)
    return pl.pallas_call(
        body, name="join_halves", out_shape=[jax.ShapeDtypeStruct(h.shape, h.dtype) for h in halves],
        in_specs=[anyspec] * n, out_specs=[anyspec] * n,
        scratch_shapes=[pltpu.SemaphoreType.DMA((n,)), pltpu.SemaphoreType.DMA((n,))],
    )(*halves)


def _row_block(rows):
    for rb in (512, 256, 128, 64, 32, 16):
        if rows % rb == 0:
            return rb
    raise ValueError(rows)


def _add_pair(name, a, b):
    S, Rh, C = a.shape
    rb = _row_block(Rh)

    def body(a_ref, b_ref, o_ref):
        o_ref[...] = (a_ref[...].astype(F32) + b_ref[...].astype(F32)).astype(BF)

    spec = pl.BlockSpec((None, rb, C), lambda s, r: (s, r, 0))
    return pl.pallas_call(
        body, name=name, grid=(S, Rh // rb), in_specs=[spec, spec], out_specs=spec,
        out_shape=jax.ShapeDtypeStruct((S, Rh, C), BF), compiler_params=_params(2),
    )(a, b)


def _sum_chips(name, parts):
    S, Rh, C = parts.shape
    rb = _row_block(Rh)

    def body(p_ref, o_ref):
        tot = p_ref[0].astype(F32)
        for s in range(1, S):
            tot = tot + p_ref[s].astype(F32)
        o_ref[...] = tot

    return pl.pallas_call(
        body, name=name, grid=(Rh // rb,), in_specs=[pl.BlockSpec((S, rb, C), lambda r: (0, r, 0))],
        out_specs=pl.BlockSpec((rb, C), lambda r: (r, 0)), out_shape=jax.ShapeDtypeStruct((Rh, C), F32),
        compiler_params=_params(1),
    )(parts)


def _adamw_math(w, g, m, v):
    m2 = ADAM_B1 * m + (1.0 - ADAM_B1) * g
    v2 = ADAM_B2 * v + (1.0 - ADAM_B2) * (g * g)
    m_hat = m2 / (1.0 - ADAM_B1 ** ADAM_STEP)
    v_hat = v2 / (1.0 - ADAM_B2 ** ADAM_STEP)
    delta = -ADAM_LR * (m_hat / (jnp.sqrt(v_hat) + ADAM_EPS) + ADAM_WD * w)
    return delta, m2, v2


def _adamw(name, w, m, v, gs):
    L, R, C = w.shape
    Rh = R // 2
    rb = _row_block(Rh)
    nbh = Rh // rb
    assert len(gs) == L

    def body(core_ref, w_ref, m_ref, v_ref, *rest):
        g_refs, (go_ref, d_ref, m2_ref, v2_ref) = rest[:2 * L], rest[2 * L:]
        layer, half = pl.program_id(0), pl.program_id(1)
        mine = half == core_ref[0]
        g = jnp.where(mine, g_refs[0][...], g_refs[1][...])
        for t in range(1, L):
            g = jnp.where(layer == t, jnp.where(mine, g_refs[2 * t][...], g_refs[2 * t + 1][...]), g)
        delta, m2, v2 = _adamw_math(w_ref[...], g, m_ref[...], v_ref[...])
        go_ref[...] = g
        d_ref[...] = delta
        m2_ref[...] = m2
        v2_ref[...] = v2

    wspec = pl.BlockSpec((None, rb, C), lambda l, h, r, core: (l, h * nbh + r, 0))
    gspec = pl.BlockSpec((rb, C), lambda l, h, r, core: (r, 0))
    return pl.pallas_call(
        body, name=name,
        grid_spec=pltpu.PrefetchScalarGridSpec(num_scalar_prefetch=1, grid=(L, 2, nbh),
                                               in_specs=[wspec] * 3 + [gspec] * (2 * L), out_specs=[wspec] * 4),
        out_shape=[jax.ShapeDtypeStruct((L, R, C), F32)] * 4, compiler_params=_params(3),
    )(lax.axis_index("c").astype(jnp.int32).reshape(1), w, m, v, *[g for pair in gs for g in pair])


def _adamw_small(w, g, m, v):
    def body(w_ref, g_ref, m_ref, v_ref, d_ref, m2_ref, v2_ref):
        delta, m2, v2 = _adamw_math(w_ref[...], g_ref[...], m_ref[...], v_ref[...])
        d_ref[...] = delta
        m2_ref[...] = m2
        v2_ref[...] = v2

    return pl.pallas_call(body, name="adamw_small", out_shape=[jax.ShapeDtypeStruct(w.shape, F32)] * 3)(w, g, m, v)


def _pack(arrays):
    rows = []
    for a in arrays:
        a = a.astype(F32).reshape(-1, a.shape[-1])
        r, c = a.shape
        k = -(-c // LANES)
        a = jnp.pad(a, ((0, 0), (0, k * LANES - c))).reshape(r * k, LANES)
        rows.append(jnp.pad(a, ((0, -(r * k) % 8), (0, 0))))
    return jnp.concatenate(rows, axis=0)


def _unpack(buf, shapes):
    out, r0 = [], 0
    for shp in shapes:
        c = shp[-1]
        r = int(np.prod(shp)) // c
        k = -(-c // LANES)
        out.append(buf[r0:r0 + r * k].reshape(r, k * LANES)[:, :c].reshape(shp))
        r0 += r * k + (-(r * k) % 8)
    return out


def _mlp_fwd(tag, x, g, w_up_sm, w_down):
    h = _rms_fwd(f"mlp{tag}_norm", x, g)
    (up,) = _mm(f"mlp{tag}_up", h, w_up_sm, nt=False, b_sm=True, tm=512, tn=1024, tk=1024,
                ep_fn=lambda acc: (acc,), outs=(("tile", BF),))
    return h, up


def _mlp_bwd(tag, dy, x, g, h, up, w_up_sm, w_down):
    (dup,) = _mm(f"mlp{tag}_dup", dy, w_down, nt=True, tm=512, tn=1024, tk=1024, ep_in=((up, "tile"),),
                 ep_fn=lambda acc, u: (acc * (2.0 * jnp.maximum(u.astype(F32), 0.0)),), outs=(("tile", BF),))
    dw_down = _mm_tn(f"mlp{tag}_dw_down", up, dy, tm=1024, tn=1024, tk=512, a_fn=_relu2)
    dw_up = _mm_tn(f"mlp{tag}_dw_up", h, dup, tm=1024, tn=1024, tk=512, out_sm=N_SHARD)
    dx, dg, dx_sum = _mm(f"mlp{tag}_dx", dup, w_up_sm, nt=True, b_sm=True, tm=512, tn=1024, tk=1024,
                         ep_in=((x, "tile"), (g, "row"), (dy, "tile")), ep_fn=_rms_bwd_ep,
                         outs=(("tile", F32), ("colsum", F32), ("colsum", F32)))
    return dx, dg, dx_sum, dw_up, dw_down


def kernel(x, conv_norm_g, conv_w_in, conv_b_in, conv_dw, conv_dw_b, conv_ln_g, conv_ln_b, conv_w_out, conv_b_out, attn_norm_g, w_qkv, b_qkv, q_norm_g, k_norm_g, sinks, w_o, b_o, rel_bias, mlp_norm_g, w_up, w_down, loss_target, m_conv_norm_g, m_conv_w_in, m_conv_b_in, m_conv_dw, m_conv_dw_b, m_conv_ln_g, m_conv_ln_b, m_conv_w_out, m_conv_b_out, m_attn_norm_g, m_w_qkv, m_b_qkv, m_q_norm_g, m_k_norm_g, m_sinks, m_w_o, m_b_o, m_rel_bias, m_mlp_norm_g, m_w_up, m_w_down, v_conv_norm_g, v_conv_w_in, v_conv_b_in, v_conv_dw, v_conv_dw_b, v_conv_ln_g, v_conv_ln_b, v_conv_w_out, v_conv_b_out, v_attn_norm_g, v_w_qkv, v_b_qkv, v_q_norm_g, v_k_norm_g, v_sinks, v_w_o, v_b_o, v_rel_bias, v_mlp_norm_g, v_w_up, v_w_down):
    Dm = D_MODEL
    x2d = x[0]
    tgt = loss_target[0]
    T = x2d.shape[0]
    shard = 2 * lax.axis_index("x") + lax.axis_index("y")

    big = [conv_w_in[0], conv_w_out[0], w_qkv[0], w_o[0], w_up[0], w_up[1], w_down[0], w_down[1]]
    stacks = [lax.dynamic_update_slice(jnp.zeros((N_SHARD,) + w.shape, BF), w.astype(BF)[None], (shard, 0, 0))
              for w in big]
    groups = ((0, 1), (4, 6), (2, 3), (5, 7))
    gather_sems, stacks, gather_token = _gather_start(stacks, groups)

    def gathered_group(g, name, after):
        return _gather_wait(name, [stacks[t] for t in groups[g]], gather_sems[g], after)

    sharded_small = [conv_dw[0], attn_norm_g, b_qkv, b_o]
    (gathered,) = _gather8("gather_small_weights", _pack(sharded_small), with_sum=False)
    chips = [_unpack(gathered[2 * s], [a.shape for a in sharded_small]) for s in range(N_SHARD)]
    dw_f, attn_norm_f, b_qkv_f, b_o_f = (jnp.concatenate([chips[s][t] for s in range(N_SHARD)], axis=-1)
                                         for t in range(len(sharded_small)))
    dw_pad = jnp.pad(dw_f, ((0, HALO - CONV_W), (0, 0)))
    bucket = _bucket_table()
    bias = _bias_table(rel_bias, bucket)

    h0 = _rms_fwd("conv_norm", x2d, conv_norm_g, deps=(gather_token,))
    w_in_sm, g_out = gathered_group(0, "gather_wait_conv", (h0, dw_pad, bias))
    w_out_f = g_out.reshape(Dm, Dm)
    (u,) = _mm("conv_in", h0, w_in_sm, nt=False, b_sm=True, tm=512, tn=512, tk=1024, ep_in=((conv_b_in, "row"),),
               ep_fn=lambda acc, b: (acc + b,), outs=(("tile", BF),))
    cv, s_act = _conv_fwd(u, dw_pad, conv_dw_b, conv_ln_g, conv_ln_b)
    (x1,) = _mm("conv_out", s_act, w_out_f, nt=False, tm=512, tn=1024, tk=1024,
                ep_in=((conv_b_out, "row"), (x2d, "tile")), ep_fn=lambda acc, b, r: (acc + b + r,),
                outs=(("tile", F32),))

    g_up0, g_down0 = gathered_group(1, "gather_wait_mlp0", (x1,))
    w_up_sm = [g_up0, None]
    w_down_f = [g_down0.reshape(D_FF, Dm), None]
    h1, up0 = _mlp_fwd(0, x1, mlp_norm_g[0:1], w_up_sm[0], w_down_f[0])
    (x2,) = _mm("mlp0_down", up0, w_down_f[0], nt=False, tm=512, tn=1024, tk=1024, a_fn=_relu2,
                ep_in=((x1, "tile"),), ep_fn=lambda acc, r: (acc + r,), outs=(("tile", F32),))

    g_qkv, g_o = gathered_group(2, "gather_wait_attn", (x2,))
    w_qkv_f = jnp.transpose(g_qkv, (1, 0, 2)).reshape(Dm, QKV_DIM)
    w_o_f = g_o.reshape(ATTN_DIM, Dm)
    h2 = _rms_fwd("attn_norm", x2, attn_norm_f)
    (qkv,) = _mm("attn_qkv", h2, w_qkv_f, nt=False, tm=512, tn=QKV_DIM, tk=1024, ep_in=((b_qkv_f, "row"),),
                 ep_fn=lambda acc, b: (acc + b,), outs=(("tile", F32),))
    qg_t = jnp.tile(q_norm_g, (1, N_HEADS))
    kg_t = jnp.tile(k_norm_g, (1, N_KV))
    qn, kn, vv = _qk_norm_fwd(qkv, qg_t, kg_t)
    sinks1 = sinks[0]
    att = _attn_fwd(qn, kn, vv, bias, sinks1)
    (x3,) = _mm("attn_out", att, w_o_f, nt=False, tm=512, tn=1024, tk=1024,
                ep_in=((b_o_f, "row"), (x2, "tile")), ep_fn=lambda acc, b, r: (acc + b + r,), outs=(("tile", F32),))

    g_up1, g_down1 = gathered_group(3, "gather_wait_mlp1", (x3,))
    w_up_sm[1] = g_up1
    w_down_f[1] = g_down1.reshape(D_FF, Dm)
    h3, up1 = _mlp_fwd(1, x3, mlp_norm_g[1:2], w_up_sm[1], w_down_f[1])

    def loss_ep(acc, r, t):
        diff = acc + r - t
        return diff * (1.0 / Dm), jnp.sum(diff * diff, axis=0, keepdims=True)

    dy, sq = _mm("mlp1_down_loss", up1, w_down_f[1], nt=False, tm=512, tn=1024, tk=1024, a_fn=_relu2,
                 ep_in=((x3, "tile"), (tgt, "tile")), ep_fn=loss_ep, outs=(("tile", F32), ("colsum", F32)))
    loss = lax.psum(0.5 * jnp.sum(sq) * (1.0 / Dm), ("x", "y", "c"))

    dx3, dg_mlp1, db_o, dw_up1, dw_down1 = _mlp_bwd(1, dy, x3, mlp_norm_g[1:2], h3, up1, w_up_sm[1], w_down_f[1])

    ident = lambda acc: (acc,)
    (datt,) = _mm("attn_dout", dx3, w_o_f, nt=True, tm=512, tn=1024, tk=1024, ep_fn=ident, outs=(("tile", BF),))
    dw_o = _mm_tn("attn_dw_o", att, dx3, tm=1024, tn=1024, tk=512)
    dqn, dkn, dvv, dbias, dsinks = _attn_bwd(qn, kn, vv, bias, sinks1, datt)
    drel = _bias_grad(dbias, bucket)
    dqkv, db_qkv, dqg_t, dkg_t = _qk_norm_bwd(qkv, dqn, dkn, dvv, qg_t, kg_t)
    dw_qkv = _mm_tn("attn_dw_qkv", h2, dqkv, tm=1024, tn=QKV_DIM, tk=512)
    dx2, dg_attn, _ = _mm("attn_dx", dqkv, w_qkv_f, nt=True, tm=512, tn=1024, tk=QKV_DIM,
                             ep_in=((x2, "tile"), (attn_norm_f, "row"), (dx3, "tile")), ep_fn=_rms_bwd_ep,
                             outs=(("tile", F32), ("colsum", F32), ("colsum", F32)))

    dx1, dg_mlp0, db_out, dw_up0, dw_down0 = _mlp_bwd(0, dx2, x1, mlp_norm_g[0:1], h1, up0, w_up_sm[0], w_down_f[0])

    dcv, dln_g, dln_b, ddw_b = _mm("conv_ds", dx1, w_out_f, nt=True, tm=512, tn=1024, tk=1024,
                                   ep_in=((cv, "tile"), (conv_ln_g, "row"), (conv_ln_b, "row")),
                                   ep_fn=_ln_silu_bwd_ep,
                                   outs=(("tile", F32), ("colsum", F32), ("colsum", F32), ("colsum", F32)))
    dw_out = _mm_tn("conv_dw_out", s_act, dx1, tm=1024, tn=1024, tk=512)
    du, db_in, ddw8 = _conv_bwd(u, dcv, dw_pad)
    dw_in = _mm_tn("conv_dw_in", h0, du, tm=1024, tn=512, tk=1024, out_sm=N_SHARD)
    gx, dg_conv, _ = _mm("conv_dx", du, w_in_sm, nt=True, b_sm=True, tm=512, tn=1024, tk=512,
                         ep_in=((x2d, "tile"), (conv_norm_g, "row"), (dx1, "tile")), ep_fn=_rms_bwd_ep,
                         outs=(("tile", F32), ("colsum", F32), ("colsum", F32)))

    grads = [dw_in, dw_out.reshape(N_SHARD, Dm // N_SHARD, Dm),
             jnp.transpose(dw_qkv.reshape(Dm, N_SHARD, QKV_DIM // N_SHARD), (1, 0, 2)),
             dw_o.reshape(N_SHARD, ATTN_DIM // N_SHARD, Dm), dw_up0, dw_up1,
             dw_down0.reshape(N_SHARD, D_FF // N_SHARD, Dm), dw_down1.reshape(N_SHARD, D_FF // N_SHARD, Dm)]
    names = ["w_in", "w_out", "w_qkv", "w_o", "w_up0", "w_up1", "w_down0", "w_down1"]
    core = lax.axis_index("c")
    from_sibling = _swap_halves(grads)
    mine = [lax.dynamic_slice_in_dim(g, core * (g.shape[1] // 2), g.shape[1] // 2, axis=1) for g in grads]
    chip_sums = [_add_pair(f"add_sibling_{nm}", a, b) for nm, a, b in zip(names, mine, from_sibling)]
    by_chip = _scatter_chips(chip_sums)
    halves = [_sum_chips(f"sum_chips_{nm}", p) for nm, p in zip(names, by_chip)]
    r_in, r_out, r_qkv, r_o, r_up0, r_up1, r_down0, r_down1 = zip(halves, _join_halves(halves))

    big_out = {}
    for nm, w, m, v, gs in (("conv_w_in", conv_w_in, m_conv_w_in, v_conv_w_in, (r_in,)),
                            ("conv_w_out", conv_w_out, m_conv_w_out, v_conv_w_out, (r_out,)),
                            ("w_qkv", w_qkv, m_w_qkv, v_w_qkv, (r_qkv,)),
                            ("w_o", w_o, m_w_o, v_w_o, (r_o,)),
                            ("w_up", w_up, m_w_up, v_w_up, (r_up0, r_up1)),
                            ("w_down", w_down, m_w_down, v_w_down, (r_down0, r_down1))):
        big_out[nm] = _adamw(f"adamw_{nm}", w, m, v, gs)

    dqg = dqg_t.reshape(N_HEADS, HEAD_DIM).sum(axis=0, keepdims=True)
    dkg = dkg_t.reshape(N_KV, HEAD_DIM).sum(axis=0, keepdims=True)
    small_full = [dg_conv, db_in, ddw8.sum(axis=1)[:CONV_W], ddw_b, dln_g, dln_b, db_out, dg_attn, db_qkv, dqg, dkg,
                  dsinks[None, :], db_o, drel, jnp.pad(dg_mlp0, ((0, 1), (0, 0))) + jnp.pad(dg_mlp1, ((1, 0), (0, 0)))]
    _, small_sum = _gather8("reduce_small_grads", _pack(small_full), with_sum=True)
    (r_norm, r_b_in, r_dw, r_dw_b, r_ln_g, r_ln_b, r_b_out, r_attn_norm, r_b_qkv, r_qg, r_kg, r_sinks, r_b_o, r_rel,
     r_mlp_norm) = _unpack(small_sum, [a.shape for a in small_full])

    def cols(a, width):
        return lax.dynamic_slice_in_dim(a, shard * width, width, axis=a.ndim - 1)

    small_names = ["conv_norm_g", "conv_b_in", "conv_dw", "conv_dw_b", "conv_ln_g", "conv_ln_b", "conv_b_out",
                   "attn_norm_g", "b_qkv", "q_norm_g", "k_norm_g", "sinks", "b_o", "rel_bias", "mlp_norm_g"]
    small_g = [r_norm, r_b_in, cols(r_dw, Dm // N_SHARD)[None], r_dw_b, r_ln_g, r_ln_b, r_b_out,
               cols(r_attn_norm, Dm // N_SHARD), cols(r_b_qkv, QKV_DIM // N_SHARD), r_qg, r_kg, r_sinks,
               cols(r_b_o, Dm // N_SHARD), r_rel, r_mlp_norm]
    small_w = [conv_norm_g, conv_b_in, conv_dw, conv_dw_b, conv_ln_g, conv_ln_b, conv_b_out, attn_norm_g, b_qkv,
               q_norm_g, k_norm_g, sinks, b_o, rel_bias, mlp_norm_g]
    small_m = [m_conv_norm_g, m_conv_b_in, m_conv_dw, m_conv_dw_b, m_conv_ln_g, m_conv_ln_b, m_conv_b_out,
               m_attn_norm_g, m_b_qkv, m_q_norm_g, m_k_norm_g, m_sinks, m_b_o, m_rel_bias, m_mlp_norm_g]
    small_v = [v_conv_norm_g, v_conv_b_in, v_conv_dw, v_conv_dw_b, v_conv_ln_g, v_conv_ln_b, v_conv_b_out,
               v_attn_norm_g, v_b_qkv, v_q_norm_g, v_k_norm_g, v_sinks, v_b_o, v_rel_bias, v_mlp_norm_g]
    flat2 = lambda a: a.reshape(-1, a.shape[-1])
    shapes2 = [flat2(w).shape for w in small_w]
    pk = lambda arrs: _pack([flat2(a) for a in arrs])
    packed_g = pk(small_g)
    d_s, m_s, v_s = _adamw_small(pk(small_w), packed_g, pk(small_m), pk(small_v))
    small_out = {}
    for nm, w, g, d, m2, v2 in zip(small_names, small_w, _unpack(packed_g, shapes2), _unpack(d_s, shapes2),
                                   _unpack(m_s, shapes2), _unpack(v_s, shapes2)):
        small_out[nm] = tuple(a.reshape(w.shape) for a in (g, d, m2, v2))

    order = ["conv_norm_g", "conv_w_in", "conv_b_in", "conv_dw", "conv_dw_b", "conv_ln_g", "conv_ln_b", "conv_w_out",
             "conv_b_out", "attn_norm_g", "w_qkv", "b_qkv", "q_norm_g", "k_norm_g", "sinks", "w_o", "b_o", "rel_bias",
             "mlp_norm_g", "w_up", "w_down"]
    res = {**small_out, **big_out}
    outs = [loss, gx[None]]
    for part in range(4):
        outs += [res[nm][part] for nm in order]
    return tuple(outs)
```

```python
import math

import numpy as np
import jax
import jax.numpy as jnp
from jax import lax
from jax.experimental import pallas as pl
from jax.experimental.pallas import tpu as pltpu

F32 = jnp.float32
BF = jnp.bfloat16
MESH = pl.DeviceIdType.MESH

D_MODEL = 1024
D_FF = 4096
N_HEADS = 16
N_KV = 2
GROUP = N_HEADS // N_KV
HEAD_DIM = 64
ATTN_DIM = N_HEADS * HEAD_DIM
KV_DIM = N_KV * HEAD_DIM
QKV_DIM = ATTN_DIM + 2 * KV_DIM
BLOCK = 128
CONV_W = 31
HALO = 32
REL_BUCKETS = 32
REL_MAX_DIST = 128
NORM_EPS = 1e-6
NEG_INF = -1e30
N_SHARD = 4
LANES = 1024

ADAM_LR = 0.001
ADAM_B1 = 0.9
ADAM_B2 = 0.999
ADAM_EPS = 1e-08
ADAM_WD = 0.01
ADAM_STEP = 10

VMEM_LIMIT = 56 * 1024 * 1024


def _params(n_axes):
    return pltpu.CompilerParams(dimension_semantics=("arbitrary",) * n_axes, vmem_limit_bytes=VMEM_LIMIT)


def _dot(a, b, ca, cb):
    return lax.dot_general(a, b, (((ca,), (cb,)), ((), ())), preferred_element_type=F32)


def _mm(name, a, b, *, nt, tm, tn, tk, ep_fn, outs, a_fn=None, b_sm=False, ep_in=(), deps=()):
    M, K = a.shape
    if b_sm:
        S = b.shape[0]
        N, per = (b.shape[1], b.shape[2] // tk) if nt else (S * b.shape[2], b.shape[2] // tn)
        assert (S * b.shape[2] == K) if nt else (b.shape[1] == K)
    else:
        N = b.shape[0] if nt else b.shape[1]
        assert (b.shape[1] if nt else b.shape[0]) == K
    assert M % tm == 0 and N % tn == 0 and K % tk == 0
    nk = K // tk
    ne, no, nd = len(ep_in), len(outs), len(deps)

    def body(a_ref, b_ref, *rest):
        ep_refs, out_refs = rest[:ne], rest[ne + nd:ne + nd + no]
        i, k = pl.program_id(1), pl.program_id(2)
        av = a_ref[...]
        if a_fn is not None:
            av = a_fn(av)
        part = _dot(av.astype(BF), b_ref[...].astype(BF), 1, 1 if nt else 0)

        def finish(acc):
            vals = ep_fn(acc, *[r[...] for r in ep_refs])
            for (kind, dt), ref, val in zip(outs, out_refs, vals):
                if kind == "tile":
                    ref[...] = val.astype(dt)
                else:
                    @pl.when(i == 0)
                    def _():
                        ref[...] = val

                    @pl.when(i > 0)
                    def _():
                        ref[...] += val

        if nk == 1:
            finish(part)
        else:
            acc_ref = rest[-1]

            @pl.when(k == 0)
            def _():
                acc_ref[...] = part

            @pl.when(k > 0)
            def _():
                acc_ref[...] += part

            @pl.when(k == nk - 1)
            def _():
                finish(acc_ref[...])

    if b_sm and nt:
        b_spec = pl.BlockSpec((None, tn, tk), lambda j, i, k: (k // per, j, k % per))
    elif b_sm:
        b_spec = pl.BlockSpec((None, tk, tn), lambda j, i, k: (j // per, k, j % per))
    elif nt:
        b_spec = pl.BlockSpec((tn, tk), lambda j, i, k: (j, k))
    else:
        b_spec = pl.BlockSpec((tk, tn), lambda j, i, k: (k, j))
    in_specs = [pl.BlockSpec((tm, tk), lambda j, i, k: (i, k)), b_spec]
    for arr, kind in ep_in:
        if kind == "tile":
            assert arr.shape == (M, N)
            in_specs.append(pl.BlockSpec((tm, tn), lambda j, i, k: (i, j)))
        else:
            assert arr.shape == (1, N)
            in_specs.append(pl.BlockSpec((1, tn), lambda j, i, k: (0, j)))
    in_specs += [pl.BlockSpec(memory_space=pl.ANY)] * nd
    out_shape, out_specs = [], []
    for kind, dt in outs:
        if kind == "tile":
            out_shape.append(jax.ShapeDtypeStruct((M, N), dt))
            out_specs.append(pl.BlockSpec((tm, tn), lambda j, i, k: (i, j)))
        else:
            out_shape.append(jax.ShapeDtypeStruct((1, N), F32))
            out_specs.append(pl.BlockSpec((1, tn), lambda j, i, k: (0, j)))
    return pl.pallas_call(
        body, name=name, grid=(N // tn, M // tm, nk), in_specs=in_specs, out_specs=out_specs, out_shape=out_shape,
        scratch_shapes=[pltpu.VMEM((tm, tn), F32)] if nk > 1 else [],
        compiler_params=_params(3),
    )(a, b, *[arr for arr, _ in ep_in], *deps)


def _mm_tn(name, a, b, *, tm, tn, tk, a_fn=None, out_sm=None):
    T, Ka = a.shape
    N = b.shape[1]
    assert b.shape[0] == T and T % tk == 0 and Ka % tm == 0 and N % tn == 0
    nk = T // tk

    def body(a_ref, b_ref, o_ref, acc_ref):
        k = pl.program_id(2)
        av = a_ref[...]
        if a_fn is not None:
            av = a_fn(av)
        part = _dot(av.astype(BF), b_ref[...].astype(BF), 0, 0)

        @pl.when(k == 0)
        def _():
            acc_ref[...] = part

        @pl.when(k > 0)
        def _():
            acc_ref[...] += part

        @pl.when(k == nk - 1)
        def _():
            o_ref[...] = acc_ref[...].astype(BF)

    if out_sm is None:
        out_shape = jax.ShapeDtypeStruct((Ka, N), BF)
        out_spec = pl.BlockSpec((tm, tn), lambda i, j, k: (i, j))
    else:
        per = (N // out_sm) // tn
        assert per * tn * out_sm == N
        out_shape = jax.ShapeDtypeStruct((out_sm, Ka, N // out_sm), BF)
        out_spec = pl.BlockSpec((None, tm, tn), lambda i, j, k: (j // per, i, j % per))
    return pl.pallas_call(
        body, name=name, grid=(Ka // tm, N // tn, nk),
        in_specs=[pl.BlockSpec((tk, tm), lambda i, j, k: (k, i)), pl.BlockSpec((tk, tn), lambda i, j, k: (k, j))],
        out_specs=out_spec, out_shape=out_shape, scratch_shapes=[pltpu.VMEM((tm, tn), F32)],
        compiler_params=_params(3),
    )(a, b)


def _relu2(v):
    r = jnp.maximum(v.astype(F32), 0.0)
    return r * r


def _rms_bwd_ep(dh, x, g, dres):
    rstd = lax.rsqrt(jnp.mean(x * x, axis=-1, keepdims=True) + NORM_EPS)
    xh = x * rstd
    dxh = dh * g
    dx = rstd * (dxh - xh * jnp.mean(dxh * xh, axis=-1, keepdims=True))
    tot = dres + dx
    return tot, jnp.sum(dh * xh, axis=0, keepdims=True), jnp.sum(tot, axis=0, keepdims=True)


def _rms_fwd(name, x, g, tm=512, deps=()):
    T, Dm = x.shape

    def body(x_ref, g_ref, *rest):
        o_ref = rest[-1]
        xv = x_ref[...]
        rstd = lax.rsqrt(jnp.mean(xv * xv, axis=-1, keepdims=True) + NORM_EPS)
        o_ref[...] = (xv * rstd * g_ref[...]).astype(BF)

    return pl.pallas_call(
        body, name=name, grid=(T // tm,),
        in_specs=[pl.BlockSpec((tm, Dm), lambda i: (i, 0)), pl.BlockSpec((1, Dm), lambda i: (0, 0))]
        + [pl.BlockSpec(memory_space=pl.ANY)] * len(deps),
        out_specs=pl.BlockSpec((tm, Dm), lambda i: (i, 0)), out_shape=jax.ShapeDtypeStruct((T, Dm), BF),
        compiler_params=_params(1),
    )(x, g, *deps)


def _head_sum(v, ones_bd):
    hi = v.astype(BF)
    lo = (v - hi.astype(F32)).astype(BF)
    return _dot(hi, ones_bd, 1, 0) + _dot(lo, ones_bd, 1, 0)


def _block_ones(n):
    idx = np.arange(n) // HEAD_DIM
    return jnp.asarray((idx[:, None] == idx[None, :]).astype(np.float32), dtype=BF)


def _qk_norm_fwd(qkv, qg_t, kg_t, tm=256):
    T = qkv.shape[0]
    scale = 1.0 / math.sqrt(HEAD_DIM)

    def body(x_ref, qg_ref, kg_ref, bq_ref, bk_ref, q_ref, k_ref, v_ref):
        q = x_ref[:, pl.ds(0, ATTN_DIM)]
        rq = lax.rsqrt(_head_sum(q * q, bq_ref[...]) * (1.0 / HEAD_DIM) + NORM_EPS)
        q_ref[...] = (q * rq * qg_ref[...] * scale).astype(BF)
        k = x_ref[:, pl.ds(ATTN_DIM, KV_DIM)]
        rk = lax.rsqrt(_head_sum(k * k, bk_ref[...]) * (1.0 / HEAD_DIM) + NORM_EPS)
        k_ref[...] = (k * rk * kg_ref[...]).astype(BF)
        v_ref[...] = x_ref[:, pl.ds(ATTN_DIM + KV_DIM, KV_DIM)].astype(BF)

    full = lambda shape: pl.BlockSpec(shape, lambda i: (0, 0))
    return pl.pallas_call(
        body, name="qk_norm_fwd", grid=(T // tm,),
        in_specs=[pl.BlockSpec((tm, QKV_DIM), lambda i: (i, 0)), full((1, ATTN_DIM)), full((1, KV_DIM)),
                  full((ATTN_DIM, ATTN_DIM)), full((KV_DIM, KV_DIM))],
        out_specs=[pl.BlockSpec((tm, ATTN_DIM), lambda i: (i, 0)), pl.BlockSpec((tm, KV_DIM), lambda i: (i, 0)),
                   pl.BlockSpec((tm, KV_DIM), lambda i: (i, 0))],
        out_shape=[jax.ShapeDtypeStruct((T, ATTN_DIM), BF), jax.ShapeDtypeStruct((T, KV_DIM), BF),
                   jax.ShapeDtypeStruct((T, KV_DIM), BF)],
        compiler_params=_params(1),
    )(qkv, qg_t, kg_t, _block_ones(ATTN_DIM), _block_ones(KV_DIM))


def _qk_norm_bwd(qkv, dqn, dkn, dv, qg_t, kg_t, tm=256):
    T = qkv.shape[0]

    def body(x_ref, dq_ref, dk_ref, dv_ref, qg_ref, kg_ref, bq_ref, bk_ref, o_ref, db_ref, dqg_ref, dkg_ref):
        i = pl.program_id(0)

        def one(x, dy, g, ones_bd):
            r = lax.rsqrt(_head_sum(x * x, ones_bd) * (1.0 / HEAD_DIM) + NORM_EPS)
            xh = x * r
            dxh = dy * g
            dx = r * (dxh - xh * (_head_sum(dxh * xh, ones_bd) * (1.0 / HEAD_DIM)))
            return dx, jnp.sum(dy * xh, axis=0, keepdims=True)

        dq, dqg = one(x_ref[:, pl.ds(0, ATTN_DIM)], dq_ref[...], qg_ref[...], bq_ref[...])
        dk, dkg = one(x_ref[:, pl.ds(ATTN_DIM, KV_DIM)], dk_ref[...], kg_ref[...], bk_ref[...])
        dvv = dv_ref[...]
        o_ref[:, pl.ds(0, ATTN_DIM)] = dq.astype(BF)
        o_ref[:, pl.ds(ATTN_DIM, KV_DIM)] = dk.astype(BF)
        o_ref[:, pl.ds(ATTN_DIM + KV_DIM, KV_DIM)] = dvv.astype(BF)
        sq, sk, sv = (jnp.sum(t, axis=0, keepdims=True) for t in (dq, dk, dvv))

        @pl.when(i == 0)
        def _():
            db_ref[:, pl.ds(0, ATTN_DIM)] = sq
            db_ref[:, pl.ds(ATTN_DIM, KV_DIM)] = sk
            db_ref[:, pl.ds(ATTN_DIM + KV_DIM, KV_DIM)] = sv
            dqg_ref[...] = dqg
            dkg_ref[...] = dkg

        @pl.when(i > 0)
        def _():
            db_ref[:, pl.ds(0, ATTN_DIM)] += sq
            db_ref[:, pl.ds(ATTN_DIM, KV_DIM)] += sk
            db_ref[:, pl.ds(ATTN_DIM + KV_DIM, KV_DIM)] += sv
            dqg_ref[...] += dqg
            dkg_ref[...] += dkg

    full = lambda shape: pl.BlockSpec(shape, lambda i: (0, 0))
    row = lambda n: pl.BlockSpec((tm, n), lambda i: (i, 0))
    return pl.pallas_call(
        body, name="qk_norm_bwd", grid=(T // tm,),
        in_specs=[row(QKV_DIM), row(ATTN_DIM), row(KV_DIM), row(KV_DIM), full((1, ATTN_DIM)), full((1, KV_DIM)),
                  full((ATTN_DIM, ATTN_DIM)), full((KV_DIM, KV_DIM))],
        out_specs=[row(QKV_DIM), full((1, QKV_DIM)), full((1, ATTN_DIM)), full((1, KV_DIM))],
        out_shape=[jax.ShapeDtypeStruct((T, QKV_DIM), BF), jax.ShapeDtypeStruct((1, QKV_DIM), F32),
                   jax.ShapeDtypeStruct((1, ATTN_DIM), F32), jax.ShapeDtypeStruct((1, KV_DIM), F32)],
        compiler_params=_params(1),
    )(qkv, dqn, dkn, dv, qg_t, kg_t, _block_ones(ATTN_DIM), _block_ones(KV_DIM))


ROWS = 64
COLS = 128


def _glu(a, g):
    return a.astype(F32) * jax.nn.sigmoid(g.astype(F32))


def _conv_fwd(u, dw_pad, dw_b, ln_g, ln_b, tm=256):
    T = u.shape[0]
    Dm = D_MODEL
    hpt = tm // HALO

    def body(ac_ref, gc_ref, ap_ref, gp_ref, w_ref, wb_ref, lg_ref, lb_ref, cv_ref, s_ref, ext):
        i = pl.program_id(0)
        ext[pl.ds(0, HALO), :] = jnp.where(i > 0, _glu(ap_ref[...], gp_ref[...]), 0.0)
        ext[pl.ds(HALO, tm), :] = _glu(ac_ref[...], gc_ref[...])

        def rows(r, carry):
            r0 = pl.multiple_of(r * ROWS, ROWS)
            for c in range(Dm // COLS):
                cs = pl.ds(c * COLS, COLS)
                xe = ext[pl.ds(r0, ROWS + HALO), cs]
                acc = jnp.zeros((ROWS, COLS), F32)
                for j in range(CONV_W):
                    off = HALO - (CONV_W - 1) + j
                    acc = acc + xe[off:off + ROWS, :] * w_ref[pl.ds(j, 1), cs]
                cv_ref[pl.ds(r0, ROWS), cs] = acc + wb_ref[:, cs]
            return carry

        lax.fori_loop(0, tm // ROWS, rows, 0)
        cv = cv_ref[...]
        xc = cv - jnp.mean(cv, axis=-1, keepdims=True)
        y = xc * lax.rsqrt(jnp.mean(xc * xc, axis=-1, keepdims=True) + NORM_EPS) * lg_ref[...] + lb_ref[...]
        s_ref[...] = (y * jax.nn.sigmoid(y)).astype(BF)

    full = lambda shape: pl.BlockSpec(shape, lambda i: (0, 0))
    return pl.pallas_call(
        body, name="conv_fwd", grid=(T // tm,),
        in_specs=[pl.BlockSpec((tm, Dm), lambda i: (i, 0)), pl.BlockSpec((tm, Dm), lambda i: (i, 1)),
                  pl.BlockSpec((HALO, Dm), lambda i: (jnp.maximum(i * hpt - 1, 0), 0)),
                  pl.BlockSpec((HALO, Dm), lambda i: (jnp.maximum(i * hpt - 1, 0), 1)),
                  full((HALO, Dm)), full((1, Dm)), full((1, Dm)), full((1, Dm))],
        out_specs=[pl.BlockSpec((tm, Dm), lambda i: (i, 0)), pl.BlockSpec((tm, Dm), lambda i: (i, 0))],
        out_shape=[jax.ShapeDtypeStruct((T, Dm), F32), jax.ShapeDtypeStruct((T, Dm), BF)],
        scratch_shapes=[pltpu.VMEM((tm + HALO, Dm), F32)],
        compiler_params=_params(1),
    )(u, u, u, u, dw_pad, dw_b, ln_g, ln_b)


def _ln_silu_bwd_ep(ds, cv, lg, lb):
    xc = cv - jnp.mean(cv, axis=-1, keepdims=True)
    rstd = lax.rsqrt(jnp.mean(xc * xc, axis=-1, keepdims=True) + NORM_EPS)
    xh = xc * rstd
    y = xh * lg + lb
    sg = jax.nn.sigmoid(y)
    dy = ds * (sg * (1.0 + y * (1.0 - sg)))
    dxh = dy * lg
    dcv = rstd * (dxh - jnp.mean(dxh, axis=-1, keepdims=True) - xh * jnp.mean(dxh * xh, axis=-1, keepdims=True))
    return (dcv, jnp.sum(dy * xh, axis=0, keepdims=True), jnp.sum(dy, axis=0, keepdims=True),
            jnp.sum(dcv, axis=0, keepdims=True))


def _conv_bwd(u, dcv, dw_pad, tm=256):
    T = u.shape[0]
    Dm = D_MODEL
    hpt = tm // HALO
    last = T // HALO - 1
    nt = T // tm

    def body(ac_ref, gc_ref, ap_ref, gp_ref, dc_ref, dn_ref, w_ref, du_ref, db_ref, dw_ref, ext_g, ext_d):
        i = pl.program_id(0)
        ext_g[pl.ds(0, HALO), :] = jnp.where(i > 0, _glu(ap_ref[...], gp_ref[...]), 0.0)
        ext_g[pl.ds(HALO, tm), :] = _glu(ac_ref[...], gc_ref[...])
        ext_d[pl.ds(0, tm), :] = dc_ref[...]
        ext_d[pl.ds(tm, HALO), :] = jnp.where(i < nt - 1, dn_ref[...], 0.0)

        @pl.when(i == 0)
        def _():
            db_ref[...] = jnp.zeros_like(db_ref)
            dw_ref[...] = jnp.zeros_like(dw_ref)

        def rows(r, carry):
            r0 = pl.multiple_of(r * ROWS, ROWS)
            rs = pl.ds(r0, ROWS)
            for c in range(Dm // COLS):
                cs = pl.ds(c * COLS, COLS)
                cs2 = pl.ds(Dm + c * COLS, COLS)
                de = ext_d[pl.ds(r0, ROWS + HALO), cs]
                ge = ext_g[pl.ds(r0, ROWS + HALO), cs]
                dcur = de[0:ROWS, :]
                acc = jnp.zeros((ROWS, COLS), F32)
                for j in range(CONV_W):
                    off = CONV_W - 1 - j
                    acc = acc + de[off:off + ROWS, :] * w_ref[pl.ds(j, 1), cs]
                    goff = HALO - (CONV_W - 1) + j
                    prod = dcur * ge[goff:goff + ROWS, :]
                    dw_ref[j, :, cs] += jnp.sum(prod.reshape(ROWS // 8, 8, COLS), axis=0)
                a = ac_ref[rs, cs].astype(F32)
                sg = jax.nn.sigmoid(gc_ref[rs, cs].astype(F32))
                da = acc * sg
                dg = acc * a * sg * (1.0 - sg)
                du_ref[rs, cs] = da.astype(BF)
                du_ref[rs, cs2] = dg.astype(BF)
                db_ref[:, cs] += jnp.sum(da, axis=0, keepdims=True)
                db_ref[:, cs2] += jnp.sum(dg, axis=0, keepdims=True)
            return carry

        lax.fori_loop(0, tm // ROWS, rows, 0)

    return pl.pallas_call(
        body, name="conv_bwd", grid=(nt,),
        in_specs=[pl.BlockSpec((tm, Dm), lambda i: (i, 0)), pl.BlockSpec((tm, Dm), lambda i: (i, 1)),
                  pl.BlockSpec((HALO, Dm), lambda i: (jnp.maximum(i * hpt - 1, 0), 0)),
                  pl.BlockSpec((HALO, Dm), lambda i: (jnp.maximum(i * hpt - 1, 0), 1)),
                  pl.BlockSpec((tm, Dm), lambda i: (i, 0)),
                  pl.BlockSpec((HALO, Dm), lambda i: (jnp.minimum((i + 1) * hpt, last), 0)),
                  pl.BlockSpec((HALO, Dm), lambda i: (0, 0))],
        out_specs=[pl.BlockSpec((tm, 2 * Dm), lambda i: (i, 0)), pl.BlockSpec((1, 2 * Dm), lambda i: (0, 0)),
                   pl.BlockSpec((HALO, 8, Dm), lambda i: (0, 0, 0))],
        out_shape=[jax.ShapeDtypeStruct((T, 2 * Dm), BF), jax.ShapeDtypeStruct((1, 2 * Dm), F32),
                   jax.ShapeDtypeStruct((HALO, 8, Dm), F32)],
        scratch_shapes=[pltpu.VMEM((tm + HALO, Dm), F32), pltpu.VMEM((tm + HALO, Dm), F32)],
        compiler_params=_params(1),
    )(u, u, u, u, dcv, dcv, dw_pad)


def _bucket_table():
    q_loc = np.arange(BLOCK)[:, None]
    k_loc = np.arange(2 * BLOCK)[None, :]
    dist = q_loc + BLOCK - k_loc
    n = np.maximum(dist, 0)
    max_exact = REL_BUCKETS // 2
    large = max_exact + (np.log(np.maximum(n, 1).astype(np.float32) / max_exact)
                         / math.log(REL_MAX_DIST / max_exact) * (REL_BUCKETS - max_exact)).astype(np.int32)
    large = np.minimum(large, REL_BUCKETS - 1)
    bucket = np.where(n < max_exact, n, large).astype(np.int32)
    return jnp.asarray(np.where((dist >= 0) & (dist < BLOCK), bucket, -1).astype(np.int32))


def _bias_table(rel_bias, bucket):
    def body(rb_ref, bk_ref, o_ref):
        bk = bk_ref[...]
        for h in range(N_HEADS):
            acc = jnp.full((BLOCK, 2 * BLOCK), NEG_INF, F32)
            for b in range(REL_BUCKETS):
                acc = jnp.where(bk == b, rb_ref[b, h], acc)
            o_ref[h] = acc

    return pl.pallas_call(
        body, name="bias_table", out_shape=jax.ShapeDtypeStruct((N_HEADS, BLOCK, 2 * BLOCK), F32),
        in_specs=[pl.BlockSpec(memory_space=pltpu.SMEM), pl.BlockSpec(memory_space=pltpu.VMEM)],
        out_specs=pl.BlockSpec(memory_space=pltpu.VMEM),
    )(rel_bias, bucket)


def _bias_grad(dbias, bucket):
    def body(db_ref, bk_ref, o_ref):
        bk = bk_ref[...]
        for b in range(REL_BUCKETS):
            sel = bk == b
            for h in range(N_HEADS):
                o_ref[b, h] = jnp.sum(jnp.where(sel, db_ref[h], 0.0))

    return pl.pallas_call(
        body, name="bias_grad", out_shape=jax.ShapeDtypeStruct((REL_BUCKETS, N_HEADS), F32),
        in_specs=[pl.BlockSpec(memory_space=pltpu.VMEM), pl.BlockSpec(memory_space=pltpu.VMEM)],
        out_specs=pl.BlockSpec(memory_space=pltpu.SMEM),
    )(dbias, bucket)


def _band_probs(q, k, bias_h, sink, first):
    s = _dot(q, k, 1, 1) + bias_h
    s = jnp.where(first, NEG_INF, s)
    m = jnp.maximum(jnp.max(s, axis=-1, keepdims=True), sink)
    p = jnp.exp(s - m)
    ps = jnp.exp(sink - m)
    inv = 1.0 / (jnp.sum(p, axis=-1, keepdims=True) + ps)
    return p * inv, ps * inv


def _band(prev_ref, cur_ref, g):
    hs = pl.ds(g * HEAD_DIM, HEAD_DIM)
    return jnp.concatenate([prev_ref[:, hs], cur_ref[:, hs]], axis=0)


def _first_mask(n):
    col = lax.broadcasted_iota(jnp.int32, (BLOCK, 2 * BLOCK), 1)
    return jnp.logical_and(n == 0, col < BLOCK)


def _attn_fwd(qn, kn, vv, bias, sinks):
    T = qn.shape[0]
    nb = T // BLOCK

    def body(sk_ref, q_ref, kc_ref, kp_ref, vc_ref, vp_ref, b_ref, o_ref):
        first = _first_mask(pl.program_id(0))
        for g in range(N_KV):
            k = _band(kp_ref, kc_ref, g)
            v = _band(vp_ref, vc_ref, g)
            for hh in range(GROUP):
                h = g * GROUP + hh
                hs = pl.ds(h * HEAD_DIM, HEAD_DIM)
                pn, _ = _band_probs(q_ref[:, hs], k, b_ref[h], sk_ref[h], first)
                o_ref[:, hs] = _dot(pn.astype(BF), v, 1, 0).astype(BF)

    cur = lambda n: (n, 0)
    prev = lambda n: (jnp.maximum(n - 1, 0), 0)
    return pl.pallas_call(
        body, name="attn_fwd", grid=(nb,),
        in_specs=[pl.BlockSpec(memory_space=pltpu.SMEM), pl.BlockSpec((BLOCK, ATTN_DIM), cur),
                  pl.BlockSpec((BLOCK, KV_DIM), cur), pl.BlockSpec((BLOCK, KV_DIM), prev),
                  pl.BlockSpec((BLOCK, KV_DIM), cur), pl.BlockSpec((BLOCK, KV_DIM), prev),
                  pl.BlockSpec((N_HEADS, BLOCK, 2 * BLOCK), lambda n: (0, 0, 0))],
        out_specs=pl.BlockSpec((BLOCK, ATTN_DIM), cur), out_shape=jax.ShapeDtypeStruct((T, ATTN_DIM), BF),
        compiler_params=_params(1),
    )(sinks, qn, kn, kn, vv, vv, bias)


def _attn_bwd(qn, kn, vv, bias, sinks, do):
    T = qn.shape[0]
    nb = T // BLOCK
    scale = 1.0 / math.sqrt(HEAD_DIM)

    def body(sk_ref, q_ref, kc_ref, kp_ref, vc_ref, vp_ref, b_ref, do_ref,
             dq_ref, dk_ref, dv_ref, db_ref, dsk_ref, dk_full, dv_full, dk_carry, dv_carry):
        n = pl.program_id(0)

        @pl.when(n == 0)
        def _():
            db_ref[...] = jnp.zeros_like(db_ref)
            dk_carry[...] = jnp.zeros_like(dk_carry)
            dv_carry[...] = jnp.zeros_like(dv_carry)
            for h in range(N_HEADS):
                dsk_ref[h] = 0.0

        @pl.when(n < nb)
        def _():
            first = _first_mask(n)
            for g in range(N_KV):
                k = _band(kp_ref, kc_ref, g)
                v = _band(vp_ref, vc_ref, g)
                dk_g = jnp.zeros((2 * BLOCK, HEAD_DIM), F32)
                dv_g = jnp.zeros((2 * BLOCK, HEAD_DIM), F32)
                for hh in range(GROUP):
                    h = g * GROUP + hh
                    hs = pl.ds(h * HEAD_DIM, HEAD_DIM)
                    q = q_ref[:, hs]
                    doh = do_ref[:, hs]
                    pn, psink = _band_probs(q, k, b_ref[h], sk_ref[h], first)
                    dp = _dot(doh, v, 1, 1)
                    delta = jnp.sum(pn * dp, axis=-1, keepdims=True)
                    ds = pn * (dp - delta)
                    dsk_ref[h] += -jnp.sum(psink * delta)
                    db_ref[h] += ds
                    dsb = ds.astype(BF)
                    dq_ref[:, hs] = _dot(dsb, k, 1, 0) * scale
                    dk_g = dk_g + _dot(dsb, q, 0, 0)
                    dv_g = dv_g + _dot(pn.astype(BF), doh, 0, 0)
                gs = pl.ds(g * HEAD_DIM, HEAD_DIM)
                dk_full[:, gs] = dk_g
                dv_full[:, gs] = dv_g

        @pl.when(n == nb)
        def _():
            dk_full[...] = jnp.zeros_like(dk_full)
            dv_full[...] = jnp.zeros_like(dv_full)

        dk_ref[...] = dk_carry[...] + dk_full[pl.ds(0, BLOCK), :]
        dv_ref[...] = dv_carry[...] + dv_full[pl.ds(0, BLOCK), :]
        dk_carry[...] = dk_full[pl.ds(BLOCK, BLOCK), :]
        dv_carry[...] = dv_full[pl.ds(BLOCK, BLOCK), :]

    cur = lambda n: (jnp.minimum(n, nb - 1), 0)
    prev = lambda n: (jnp.maximum(jnp.minimum(n, nb - 1) - 1, 0), 0)
    out_kv = lambda n: (jnp.maximum(n - 1, 0), 0)
    return pl.pallas_call(
        body, name="attn_bwd", grid=(nb + 1,),
        in_specs=[pl.BlockSpec(memory_space=pltpu.SMEM), pl.BlockSpec((BLOCK, ATTN_DIM), cur),
                  pl.BlockSpec((BLOCK, KV_DIM), cur), pl.BlockSpec((BLOCK, KV_DIM), prev),
                  pl.BlockSpec((BLOCK, KV_DIM), cur), pl.BlockSpec((BLOCK, KV_DIM), prev),
                  pl.BlockSpec((N_HEADS, BLOCK, 2 * BLOCK), lambda n: (0, 0, 0)),
                  pl.BlockSpec((BLOCK, ATTN_DIM), cur)],
        out_specs=[pl.BlockSpec((BLOCK, ATTN_DIM), cur), pl.BlockSpec((BLOCK, KV_DIM), out_kv),
                   pl.BlockSpec((BLOCK, KV_DIM), out_kv),
                   pl.BlockSpec((N_HEADS, BLOCK, 2 * BLOCK), lambda n: (0, 0, 0)),
                   pl.BlockSpec(memory_space=pltpu.SMEM)],
        out_shape=[jax.ShapeDtypeStruct((T, ATTN_DIM), F32), jax.ShapeDtypeStruct((T, KV_DIM), F32),
                   jax.ShapeDtypeStruct((T, KV_DIM), F32),
                   jax.ShapeDtypeStruct((N_HEADS, BLOCK, 2 * BLOCK), F32), jax.ShapeDtypeStruct((N_HEADS,), F32)],
        scratch_shapes=[pltpu.VMEM((2 * BLOCK, KV_DIM), F32), pltpu.VMEM((2 * BLOCK, KV_DIM), F32),
                        pltpu.VMEM((BLOCK, KV_DIM), F32), pltpu.VMEM((BLOCK, KV_DIM), F32)],
        compiler_params=_params(1),
    )(sinks, qn, kn, kn, vv, vv, bias, do)


def _coords():
    return lax.axis_index("x"), lax.axis_index("y"), lax.axis_index("c")


def _gather8(name, v, with_sum):
    R = v.shape[0]

    def body(v_ref, all_ref, *rest):
        sum_ref = rest[0] if with_sum else None
        send_sems, recv_sems, local_sem = rest[-3:]
        x, y, c = _coords()
        me = 4 * x + 2 * y + c
        local = pltpu.make_async_copy(v_ref, all_ref.at[me], local_sem)
        local.start()
        sends = []
        for k in range(1, 8):
            peer = (x ^ (k >> 2), y ^ ((k >> 1) & 1), c ^ (k & 1))
            cp = pltpu.make_async_remote_copy(src_ref=v_ref, dst_ref=all_ref.at[me], send_sem=send_sems.at[k - 1],
                                              recv_sem=recv_sems.at[k - 1], device_id=peer, device_id_type=MESH)
            cp.start()
            sends.append(cp)
        for k in range(1, 8):
            peer = (x ^ (k >> 2), y ^ ((k >> 1) & 1), c ^ (k & 1))
            pltpu.make_async_remote_copy(src_ref=v_ref, dst_ref=all_ref.at[me ^ k], send_sem=send_sems.at[k - 1],
                                         recv_sem=recv_sems.at[k - 1], device_id=peer, device_id_type=MESH).wait_recv()
        for cp in sends:
            cp.wait_send()
        local.wait()
        if with_sum:
            tot = all_ref[0]
            for d in range(1, 8):
                tot = tot + all_ref[d]
            sum_ref[...] = tot

    out_shape = [jax.ShapeDtypeStruct((8, R, LANES), F32)]
    if with_sum:
        out_shape.append(jax.ShapeDtypeStruct((R, LANES), F32))
    vm = pl.BlockSpec(memory_space=pltpu.VMEM)
    return pl.pallas_call(
        body, name=name, out_shape=out_shape, in_specs=[vm], out_specs=[vm] * len(out_shape),
        scratch_shapes=[pltpu.SemaphoreType.DMA((7,)), pltpu.SemaphoreType.DMA((7,)), pltpu.SemaphoreType.DMA],
    )(v)


CHIP_FLIPS = ((1, 0), (0, 1), (1, 1))


HBM_SPEC = pl.BlockSpec(memory_space=pltpu.HBM)
SEM_SPEC = pl.BlockSpec(memory_space=pltpu.SEMAPHORE)
ANY_SPEC = pl.BlockSpec(memory_space=pl.ANY)
DATAFLOW = pltpu.SideEffectType.DATAFLOW_SIDE_EFFECTING


def _chip_copy(land, sems, idx, slot_src, slot_dst, peer):
    send_sems, recv_sems = sems
    return pltpu.make_async_remote_copy(src_ref=land.at[slot_src], dst_ref=land.at[slot_dst], send_sem=send_sems.at[idx],
                                        recv_sem=recv_sems.at[idx], device_id=peer, device_id_type=MESH)


def _gather_start(stacks, groups):
    n = len(stacks)
    ng = len(groups)

    def body(*refs):
        lands = refs[:n]
        sems = [(refs[n + 2 * g], refs[n + 2 * g + 1]) for g in range(ng)]
        token = refs[-1]
        x, y, c = _coords()
        s = 2 * x + y
        for g, members in enumerate(groups):
            for i, t in enumerate(members):
                for j, (fx, fy) in enumerate(CHIP_FLIPS):
                    _chip_copy(lands[t], sems[g], 3 * i + j, s, s, (x ^ fx, y ^ fy, c)).start()
        token[...] = jnp.zeros_like(token)

    out_shape = []
    for members in groups:
        out_shape += [pltpu.SemaphoreType.DMA((3 * len(members),))] * 2
    out_shape += [pltpu.HBM(w.shape, w.dtype) for w in stacks]
    out_shape.append(jax.ShapeDtypeStruct((8, 128), F32))
    res = pl.pallas_call(
        body, name="gather_start", out_shape=out_shape, in_specs=[HBM_SPEC] * n,
        out_specs=[SEM_SPEC] * (2 * ng) + [HBM_SPEC] * n + [pl.BlockSpec(memory_space=pltpu.VMEM)],
        input_output_aliases={t: 2 * ng + t for t in range(n)},
        compiler_params=pltpu.CompilerParams(has_side_effects=DATAFLOW),
    )(*[pltpu.with_memory_space_constraint(w, pltpu.HBM) for w in stacks])
    sems = [(res[2 * g], res[2 * g + 1]) for g in range(ng)]
    return sems, list(res[2 * ng:2 * ng + n]), res[-1]


def _gather_wait(name, stacks, sems, after):
    n = len(stacks)
    after = tuple(after)

    def body(*refs):
        lands = refs[:n]
        group_sems = (refs[n], refs[n + 1])
        x, y, c = _coords()
        s = 2 * x + y
        for i in range(n):
            for j, (fx, fy) in enumerate(CHIP_FLIPS):
                cp = _chip_copy(lands[i], group_sems, 3 * i + j, s, 2 * (x ^ fx) + (y ^ fy), (x ^ fx, y ^ fy, c))
                cp.wait_send()
                cp.wait_recv()

    return pl.pallas_call(
        body, name=name, out_shape=[pltpu.HBM(w.shape, w.dtype) for w in stacks],
        in_specs=[HBM_SPEC] * n + [SEM_SPEC, SEM_SPEC] + [ANY_SPEC] * len(after), out_specs=[HBM_SPEC] * n,
        input_output_aliases={t: t for t in range(n)},
        compiler_params=pltpu.CompilerParams(has_side_effects=DATAFLOW),
    )(*stacks, sems[0], sems[1], *after)


N_PEERS = 7


def _peer(x, y, c, k):
    return x ^ (k >> 2), y ^ ((k >> 1) & 1), c ^ (k & 1)


def _reduce_copy(grad, land, sems, idx, x, y, c, k):
    px, py, pc = _peer(x, y, c, k)
    rh = grad.shape[1] // 2
    return pltpu.make_async_remote_copy(src_ref=grad.at[2 * px + py, pl.ds(pc * rh, rh), :], dst_ref=land.at[k - 1],
                                        send_sem=sems[0].at[idx], recv_sem=sems[1].at[idx], device_id=(px, py, pc),
                                        device_id_type=MESH)


def _reduce_start(name, grads):
    n = len(grads)

    def body(*refs):
        src, lands, sems, token = refs[:n], refs[n:2 * n], (refs[2 * n], refs[2 * n + 1]), refs[-1]
        x, y, c = _coords()
        for t in range(n):
            for k in range(1, N_PEERS + 1):
                _reduce_copy(src[t], lands[t], sems, N_PEERS * t + k - 1, x, y, c, k).start()
        token[...] = jnp.zeros_like(token)

    lands = [lax.empty((N_PEERS, g.shape[1] // 2, g.shape[2]), g.dtype) for g in grads]
    out_shape = [pltpu.SemaphoreType.DMA((N_PEERS * n,))] * 2
    out_shape += [pltpu.HBM(a.shape, a.dtype) for a in list(grads) + lands]
    out_shape.append(jax.ShapeDtypeStruct((8, 128), F32))
    res = pl.pallas_call(
        body, name=name, out_shape=out_shape, in_specs=[HBM_SPEC] * (2 * n),
        out_specs=[SEM_SPEC] * 2 + [HBM_SPEC] * (2 * n) + [pl.BlockSpec(memory_space=pltpu.VMEM)],
        input_output_aliases={t: 2 + t for t in range(2 * n)},
        compiler_params=pltpu.CompilerParams(has_side_effects=DATAFLOW),
    )(*[pltpu.with_memory_space_constraint(a, pltpu.HBM) for a in list(grads) + lands])
    return (res[0], res[1]), list(res[2:2 + n]), list(res[2 + n:2 + 2 * n]), res[-1]


def _reduce_wait(name, grads, lands, sems, after):
    n = len(grads)
    after = tuple(after)

    def body(*refs):
        src, dst, group_sems = refs[:n], refs[n:2 * n], (refs[2 * n], refs[2 * n + 1])
        x, y, c = _coords()
        for t in range(n):
            for k in range(1, N_PEERS + 1):
                cp = _reduce_copy(src[t], dst[t], group_sems, N_PEERS * t + k - 1, x, y, c, k)
                cp.wait_send()
                cp.wait_recv()

    res = pl.pallas_call(
        body, name=name, out_shape=[pltpu.HBM(a.shape, a.dtype) for a in list(grads) + list(lands)],
        in_specs=[HBM_SPEC] * (2 * n) + [SEM_SPEC, SEM_SPEC] + [ANY_SPEC] * len(after), out_specs=[HBM_SPEC] * (2 * n),
        input_output_aliases={t: t for t in range(2 * n)},
        compiler_params=pltpu.CompilerParams(has_side_effects=DATAFLOW),
    )(*grads, *lands, sems[0], sems[1], *after)
    return list(res[:n]), list(res[n:])


def _join_halves(name, halves):
    n = len(halves)

    def body(*refs):
        src, dst = refs[:n], refs[n:2 * n]
        send_sems, recv_sems = refs[2 * n:]
        x, y, c = _coords()
        cps = []
        for t in range(n):
            cp = pltpu.make_async_remote_copy(src_ref=src[t], dst_ref=dst[t], send_sem=send_sems.at[t],
                                              recv_sem=recv_sems.at[t], device_id=(x, y, 1 - c), device_id_type=MESH)
            cp.start()
            cps.append(cp)
        for cp in cps:
            cp.wait()

    anyspec = pl.BlockSpec(memory_space=pl.ANY)
    return pl.pallas_call(
        body, name=name, out_shape=[jax.ShapeDtypeStruct(h.shape, h.dtype) for h in halves],
        in_specs=[anyspec] * n, out_specs=[anyspec] * n,
        scratch_shapes=[pltpu.SemaphoreType.DMA((n,)), pltpu.SemaphoreType.DMA((n,))],
    )(*halves)


def _row_block(rows):
    for rb in (512, 256, 128, 64, 32, 16):
        if rows % rb == 0:
            return rb
    raise ValueError(rows)


def _sum_devices(name, grad, land, place):
    S, R, C = grad.shape
    rh = R // 2
    rb = _row_block(rh)
    nbh = rh // rb

    def body(place_ref, g_ref, l_ref, o_ref):
        tot = g_ref[...].astype(F32)
        for k in range(N_PEERS):
            tot = tot + l_ref[k].astype(F32)
        o_ref[...] = tot

    return pl.pallas_call(
        body, name=name,
        grid_spec=pltpu.PrefetchScalarGridSpec(
            num_scalar_prefetch=1, grid=(nbh,),
            in_specs=[pl.BlockSpec((None, rb, C), lambda r, place: (place[0], place[1] * nbh + r, 0)),
                      pl.BlockSpec((N_PEERS, rb, C), lambda r, place: (0, r, 0))],
            out_specs=pl.BlockSpec((rb, C), lambda r, place: (r, 0))),
        out_shape=jax.ShapeDtypeStruct((rh, C), F32), compiler_params=_params(1),
    )(place, grad, land)


def _adamw_math(w, g, m, v):
    m2 = ADAM_B1 * m + (1.0 - ADAM_B1) * g
    v2 = ADAM_B2 * v + (1.0 - ADAM_B2) * (g * g)
    m_hat = m2 / (1.0 - ADAM_B1 ** ADAM_STEP)
    v_hat = v2 / (1.0 - ADAM_B2 ** ADAM_STEP)
    delta = -ADAM_LR * (m_hat / (jnp.sqrt(v_hat) + ADAM_EPS) + ADAM_WD * w)
    return delta, m2, v2


def _adamw(name, w, m, v, gs):
    L, R, C = w.shape
    Rh = R // 2
    rb = _row_block(Rh)
    nbh = Rh // rb
    assert len(gs) == L

    def body(core_ref, w_ref, m_ref, v_ref, *rest):
        g_refs, (go_ref, d_ref, m2_ref, v2_ref) = rest[:2 * L], rest[2 * L:]
        layer, half = pl.program_id(0), pl.program_id(1)
        mine = half == core_ref[0]
        g = jnp.where(mine, g_refs[0][...], g_refs[1][...])
        for t in range(1, L):
            g = jnp.where(layer == t, jnp.where(mine, g_refs[2 * t][...], g_refs[2 * t + 1][...]), g)
        delta, m2, v2 = _adamw_math(w_ref[...], g, m_ref[...], v_ref[...])
        go_ref[...] = g
        d_ref[...] = delta
        m2_ref[...] = m2
        v2_ref[...] = v2

    wspec = pl.BlockSpec((None, rb, C), lambda l, h, r, core: (l, h * nbh + r, 0))
    gspec = pl.BlockSpec((rb, C), lambda l, h, r, core: (r, 0))
    return pl.pallas_call(
        body, name=name,
        grid_spec=pltpu.PrefetchScalarGridSpec(num_scalar_prefetch=1, grid=(L, 2, nbh),
                                               in_specs=[wspec] * 3 + [gspec] * (2 * L), out_specs=[wspec] * 4),
        out_shape=[jax.ShapeDtypeStruct((L, R, C), F32)] * 4, compiler_params=_params(3),
    )(lax.axis_index("c").astype(jnp.int32).reshape(1), w, m, v, *[g for pair in gs for g in pair])


def _adamw_small(w, g, m, v):
    def body(w_ref, g_ref, m_ref, v_ref, d_ref, m2_ref, v2_ref):
        delta, m2, v2 = _adamw_math(w_ref[...], g_ref[...], m_ref[...], v_ref[...])
        d_ref[...] = delta
        m2_ref[...] = m2
        v2_ref[...] = v2

    return pl.pallas_call(body, name="adamw_small", out_shape=[jax.ShapeDtypeStruct(w.shape, F32)] * 3)(w, g, m, v)


def _pack(arrays):
    rows = []
    for a in arrays:
        a = a.astype(F32).reshape(-1, a.shape[-1])
        r, c = a.shape
        k = -(-c // LANES)
        a = jnp.pad(a, ((0, 0), (0, k * LANES - c))).reshape(r * k, LANES)
        rows.append(jnp.pad(a, ((0, -(r * k) % 8), (0, 0))))
    return jnp.concatenate(rows, axis=0)


def _unpack(buf, shapes):
    out, r0 = [], 0
    for shp in shapes:
        c = shp[-1]
        r = int(np.prod(shp)) // c
        k = -(-c // LANES)
        out.append(buf[r0:r0 + r * k].reshape(r, k * LANES)[:, :c].reshape(shp))
        r0 += r * k + (-(r * k) % 8)
    return out


def _mlp_fwd(tag, x, g, w_up_sm, w_down):
    h = _rms_fwd(f"mlp{tag}_norm", x, g)
    (up,) = _mm(f"mlp{tag}_up", h, w_up_sm, nt=False, b_sm=True, tm=512, tn=1024, tk=1024,
                ep_fn=lambda acc: (acc,), outs=(("tile", BF),))
    return h, up


def _mlp_bwd(tag, dy, x, g, h, up, w_up_sm, w_down, place):
    (dup,) = _mm(f"mlp{tag}_dup", dy, w_down, nt=True, tm=512, tn=1024, tk=1024, ep_in=((up, "tile"),),
                 ep_fn=lambda acc, u: (acc * (2.0 * jnp.maximum(u.astype(F32), 0.0)),), outs=(("tile", BF),))
    dw_down = _mm_tn(f"mlp{tag}_dw_down", up, dy, tm=1024, tn=1024, tk=512, a_fn=_relu2)
    dw_up = _mm_tn(f"mlp{tag}_dw_up", h, dup, tm=1024, tn=1024, tk=512, out_sm=N_SHARD)
    red = _Reduction(f"mlp{tag}", [dw_up, dw_down.reshape(N_SHARD, D_FF // N_SHARD, D_MODEL)], place)
    dx, dg, dx_sum = _mm(f"mlp{tag}_dx", dup, w_up_sm, nt=True, b_sm=True, tm=512, tn=1024, tk=1024,
                         ep_in=((x, "tile"), (g, "row"), (dy, "tile")), ep_fn=_rms_bwd_ep,
                         outs=(("tile", F32), ("colsum", F32), ("colsum", F32)), deps=(red.token,))
    return dx, dg, dx_sum, red


class _Reduction:
    def __init__(self, tag, grads, place):
        self.tag, self.place = tag, place
        self.sems, self.grads, self.lands, self.token = _reduce_start(f"reduce_start_{tag}", grads)

    def finish(self, after):
        grads, lands = _reduce_wait(f"reduce_wait_{self.tag}", self.grads, self.lands, self.sems, after)
        halves = [_sum_devices(f"reduce_sum_{self.tag}{i}", g, l, self.place) for i, (g, l) in enumerate(zip(grads, lands))]
        return list(zip(halves, _join_halves(f"join_halves_{self.tag}", halves)))


def kernel(x, conv_norm_g, conv_w_in, conv_b_in, conv_dw, conv_dw_b, conv_ln_g, conv_ln_b, conv_w_out, conv_b_out, attn_norm_g, w_qkv, b_qkv, q_norm_g, k_norm_g, sinks, w_o, b_o, rel_bias, mlp_norm_g, w_up, w_down, loss_target, m_conv_norm_g, m_conv_w_in, m_conv_b_in, m_conv_dw, m_conv_dw_b, m_conv_ln_g, m_conv_ln_b, m_conv_w_out, m_conv_b_out, m_attn_norm_g, m_w_qkv, m_b_qkv, m_q_norm_g, m_k_norm_g, m_sinks, m_w_o, m_b_o, m_rel_bias, m_mlp_norm_g, m_w_up, m_w_down, v_conv_norm_g, v_conv_w_in, v_conv_b_in, v_conv_dw, v_conv_dw_b, v_conv_ln_g, v_conv_ln_b, v_conv_w_out, v_conv_b_out, v_attn_norm_g, v_w_qkv, v_b_qkv, v_q_norm_g, v_k_norm_g, v_sinks, v_w_o, v_b_o, v_rel_bias, v_mlp_norm_g, v_w_up, v_w_down):
    Dm = D_MODEL
    x2d = x[0]
    tgt = loss_target[0]
    T = x2d.shape[0]
    shard = 2 * lax.axis_index("x") + lax.axis_index("y")

    big = [conv_w_in[0], conv_w_out[0], w_qkv[0], w_o[0], w_up[0], w_up[1], w_down[0], w_down[1]]
    stacks = [lax.dynamic_update_slice(jnp.zeros((N_SHARD,) + w.shape, BF), w.astype(BF)[None], (shard, 0, 0))
              for w in big]
    groups = ((0, 1), (4, 6), (2, 3), (5, 7))
    gather_sems, stacks, gather_token = _gather_start(stacks, groups)

    def gathered_group(g, name, after):
        return _gather_wait(name, [stacks[t] for t in groups[g]], gather_sems[g], after)

    sharded_small = [conv_dw[0], attn_norm_g, b_qkv, b_o]
    (gathered,) = _gather8("gather_small_weights", _pack(sharded_small), with_sum=False)
    chips = [_unpack(gathered[2 * s], [a.shape for a in sharded_small]) for s in range(N_SHARD)]
    dw_f, attn_norm_f, b_qkv_f, b_o_f = (jnp.concatenate([chips[s][t] for s in range(N_SHARD)], axis=-1)
                                         for t in range(len(sharded_small)))
    dw_pad = jnp.pad(dw_f, ((0, HALO - CONV_W), (0, 0)))
    bucket = _bucket_table()
    bias = _bias_table(rel_bias, bucket)

    h0 = _rms_fwd("conv_norm", x2d, conv_norm_g, deps=(gather_token,))
    w_in_sm, g_out = gathered_group(0, "gather_wait_conv", (h0, dw_pad, bias))
    w_out_f = g_out.reshape(Dm, Dm)
    (u,) = _mm("conv_in", h0, w_in_sm, nt=False, b_sm=True, tm=512, tn=512, tk=1024, ep_in=((conv_b_in, "row"),),
               ep_fn=lambda acc, b: (acc + b,), outs=(("tile", BF),))
    cv, s_act = _conv_fwd(u, dw_pad, conv_dw_b, conv_ln_g, conv_ln_b)
    (x1,) = _mm("conv_out", s_act, w_out_f, nt=False, tm=512, tn=1024, tk=1024,
                ep_in=((conv_b_out, "row"), (x2d, "tile")), ep_fn=lambda acc, b, r: (acc + b + r,),
                outs=(("tile", F32),))

    g_up0, g_down0 = gathered_group(1, "gather_wait_mlp0", (x1,))
    w_up_sm = [g_up0, None]
    w_down_f = [g_down0.reshape(D_FF, Dm), None]
    h1, up0 = _mlp_fwd(0, x1, mlp_norm_g[0:1], w_up_sm[0], w_down_f[0])
    (x2,) = _mm("mlp0_down", up0, w_down_f[0], nt=False, tm=512, tn=1024, tk=1024, a_fn=_relu2,
                ep_in=((x1, "tile"),), ep_fn=lambda acc, r: (acc + r,), outs=(("tile", F32),))

    g_qkv, g_o = gathered_group(2, "gather_wait_attn", (x2,))
    w_qkv_f = jnp.transpose(g_qkv, (1, 0, 2)).reshape(Dm, QKV_DIM)
    w_o_f = g_o.reshape(ATTN_DIM, Dm)
    h2 = _rms_fwd("attn_norm", x2, attn_norm_f)
    (qkv,) = _mm("attn_qkv", h2, w_qkv_f, nt=False, tm=512, tn=QKV_DIM, tk=1024, ep_in=((b_qkv_f, "row"),),
                 ep_fn=lambda acc, b: (acc + b,), outs=(("tile", F32),))
    qg_t = jnp.tile(q_norm_g, (1, N_HEADS))
    kg_t = jnp.tile(k_norm_g, (1, N_KV))
    qn, kn, vv = _qk_norm_fwd(qkv, qg_t, kg_t)
    sinks1 = sinks[0]
    att = _attn_fwd(qn, kn, vv, bias, sinks1)
    (x3,) = _mm("attn_out", att, w_o_f, nt=False, tm=512, tn=1024, tk=1024,
                ep_in=((b_o_f, "row"), (x2, "tile")), ep_fn=lambda acc, b, r: (acc + b + r,), outs=(("tile", F32),))

    g_up1, g_down1 = gathered_group(3, "gather_wait_mlp1", (x3,))
    w_up_sm[1] = g_up1
    w_down_f[1] = g_down1.reshape(D_FF, Dm)
    h3, up1 = _mlp_fwd(1, x3, mlp_norm_g[1:2], w_up_sm[1], w_down_f[1])

    def loss_ep(acc, r, t):
        diff = acc + r - t
        return diff * (1.0 / Dm), jnp.sum(diff * diff, axis=0, keepdims=True)

    dy, sq = _mm("mlp1_down_loss", up1, w_down_f[1], nt=False, tm=512, tn=1024, tk=1024, a_fn=_relu2,
                 ep_in=((x3, "tile"), (tgt, "tile")), ep_fn=loss_ep, outs=(("tile", F32), ("colsum", F32)))
    loss = lax.psum(0.5 * jnp.sum(sq) * (1.0 / Dm), ("x", "y", "c"))

    place = jnp.stack([shard, lax.axis_index("c")]).astype(jnp.int32)
    dx3, dg_mlp1, db_o, red_mlp1 = _mlp_bwd(1, dy, x3, mlp_norm_g[1:2], h3, up1, w_up_sm[1], w_down_f[1], place)

    ident = lambda acc: (acc,)
    (datt,) = _mm("attn_dout", dx3, w_o_f, nt=True, tm=512, tn=1024, tk=1024, ep_fn=ident, outs=(("tile", BF),))
    dw_o = _mm_tn("attn_dw_o", att, dx3, tm=1024, tn=1024, tk=512)
    dqn, dkn, dvv, dbias, dsinks = _attn_bwd(qn, kn, vv, bias, sinks1, datt)
    (r_up1, r_down1) = red_mlp1.finish((dqn,))
    drel = _bias_grad(dbias, bucket)
    dqkv, db_qkv, dqg_t, dkg_t = _qk_norm_bwd(qkv, dqn, dkn, dvv, qg_t, kg_t)
    dw_qkv = _mm_tn("attn_dw_qkv", h2, dqkv, tm=1024, tn=QKV_DIM, tk=512)
    red_attn = _Reduction("attn", [jnp.transpose(dw_qkv.reshape(Dm, N_SHARD, QKV_DIM // N_SHARD), (1, 0, 2)),
                                   dw_o.reshape(N_SHARD, ATTN_DIM // N_SHARD, Dm)], place)
    dx2, dg_attn, _ = _mm("attn_dx", dqkv, w_qkv_f, nt=True, tm=512, tn=1024, tk=QKV_DIM,
                          ep_in=((x2, "tile"), (attn_norm_f, "row"), (dx3, "tile")), ep_fn=_rms_bwd_ep,
                          outs=(("tile", F32), ("colsum", F32), ("colsum", F32)), deps=(red_attn.token,))

    dx1, dg_mlp0, db_out, red_mlp0 = _mlp_bwd(0, dx2, x1, mlp_norm_g[0:1], h1, up0, w_up_sm[0], w_down_f[0], place)
    (r_qkv, r_o) = red_attn.finish((dx1,))

    dcv, dln_g, dln_b, ddw_b = _mm("conv_ds", dx1, w_out_f, nt=True, tm=512, tn=1024, tk=1024,
                                   ep_in=((cv, "tile"), (conv_ln_g, "row"), (conv_ln_b, "row")),
                                   ep_fn=_ln_silu_bwd_ep,
                                   outs=(("tile", F32), ("colsum", F32), ("colsum", F32), ("colsum", F32)))
    dw_out = _mm_tn("conv_dw_out", s_act, dx1, tm=1024, tn=1024, tk=512)
    du, db_in, ddw8 = _conv_bwd(u, dcv, dw_pad)
    (r_up0, r_down0) = red_mlp0.finish((du,))
    dw_in = _mm_tn("conv_dw_in", h0, du, tm=1024, tn=512, tk=1024, out_sm=N_SHARD)
    red_conv = _Reduction("conv", [dw_in, dw_out.reshape(N_SHARD, Dm // N_SHARD, Dm)], place)
    gx, dg_conv, _ = _mm("conv_dx", du, w_in_sm, nt=True, b_sm=True, tm=512, tn=1024, tk=512,
                         ep_in=((x2d, "tile"), (conv_norm_g, "row"), (dx1, "tile")), ep_fn=_rms_bwd_ep,
                         outs=(("tile", F32), ("colsum", F32), ("colsum", F32)), deps=(red_conv.token,))
    (r_in, r_out) = red_conv.finish((gx,))

    big_out = {}
    for nm, w, m, v, gs in (("conv_w_in", conv_w_in, m_conv_w_in, v_conv_w_in, (r_in,)),
                            ("conv_w_out", conv_w_out, m_conv_w_out, v_conv_w_out, (r_out,)),
                            ("w_qkv", w_qkv, m_w_qkv, v_w_qkv, (r_qkv,)),
                            ("w_o", w_o, m_w_o, v_w_o, (r_o,)),
                            ("w_up", w_up, m_w_up, v_w_up, (r_up0, r_up1)),
                            ("w_down", w_down, m_w_down, v_w_down, (r_down0, r_down1))):
        big_out[nm] = _adamw(f"adamw_{nm}", w, m, v, gs)

    dqg = dqg_t.reshape(N_HEADS, HEAD_DIM).sum(axis=0, keepdims=True)
    dkg = dkg_t.reshape(N_KV, HEAD_DIM).sum(axis=0, keepdims=True)
    small_full = [dg_conv, db_in, ddw8.sum(axis=1)[:CONV_W], ddw_b, dln_g, dln_b, db_out, dg_attn, db_qkv, dqg, dkg,
                  dsinks[None, :], db_o, drel, jnp.pad(dg_mlp0, ((0, 1), (0, 0))) + jnp.pad(dg_mlp1, ((1, 0), (0, 0)))]
    _, small_sum = _gather8("reduce_small_grads", _pack(small_full), with_sum=True)
    (r_norm, r_b_in, r_dw, r_dw_b, r_ln_g, r_ln_b, r_b_out, r_attn_norm, r_b_qkv, r_qg, r_kg, r_sinks, r_b_o, r_rel,
     r_mlp_norm) = _unpack(small_sum, [a.shape for a in small_full])

    def cols(a, width):
        return lax.dynamic_slice_in_dim(a, shard * width, width, axis=a.ndim - 1)

    small_names = ["conv_norm_g", "conv_b_in", "conv_dw", "conv_dw_b", "conv_ln_g", "conv_ln_b", "conv_b_out",
                   "attn_norm_g", "b_qkv", "q_norm_g", "k_norm_g", "sinks", "b_o", "rel_bias", "mlp_norm_g"]
    small_g = [r_norm, r_b_in, cols(r_dw, Dm // N_SHARD)[None], r_dw_b, r_ln_g, r_ln_b, r_b_out,
               cols(r_attn_norm, Dm // N_SHARD), cols(r_b_qkv, QKV_DIM // N_SHARD), r_qg, r_kg, r_sinks,
               cols(r_b_o, Dm // N_SHARD), r_rel, r_mlp_norm]
    small_w = [conv_norm_g, conv_b_in, conv_dw, conv_dw_b, conv_ln_g, conv_ln_b, conv_b_out, attn_norm_g, b_qkv,
               q_norm_g, k_norm_g, sinks, b_o, rel_bias, mlp_norm_g]
    small_m = [m_conv_norm_g, m_conv_b_in, m_conv_dw, m_conv_dw_b, m_conv_ln_g, m_conv_ln_b, m_conv_b_out,
               m_attn_norm_g, m_b_qkv, m_q_norm_g, m_k_norm_g, m_sinks, m_b_o, m_rel_bias, m_mlp_norm_g]
    small_v = [v_conv_norm_g, v_conv_b_in, v_conv_dw, v_conv_dw_b, v_conv_ln_g, v_conv_ln_b, v_conv_b_out,
               v_attn_norm_g, v_b_qkv, v_q_norm_g, v_k_norm_g, v_sinks, v_b_o, v_rel_bias, v_mlp_norm_g]
    flat2 = lambda a: a.reshape(-1, a.shape[-1])
    shapes2 = [flat2(w).shape for w in small_w]
    pk = lambda arrs: _pack([flat2(a) for a in arrs])
    packed_g = pk(small_g)
    d_s, m_s, v_s = _adamw_small(pk(small_w), packed_g, pk(small_m), pk(small_v))
    small_out = {}
    for nm, w, g, d, m2, v2 in zip(small_names, small_w, _unpack(packed_g, shapes2), _unpack(d_s, shapes2),
                                   _unpack(m_s, shapes2), _unpack(v_s, shapes2)):
        small_out[nm] = tuple(a.reshape(w.shape) for a in (g, d, m2, v2))

    order = ["conv_norm_g", "conv_w_in", "conv_b_in", "conv_dw", "conv_dw_b", "conv_ln_g", "conv_ln_b", "conv_w_out",
             "conv_b_out", "attn_norm_g", "w_qkv", "b_qkv", "q_norm_g", "k_norm_g", "sinks", "w_o", "b_o", "rel_bias",
             "mlp_norm_g", "w_up", "w_down"]
    res = {**small_out, **big_out}
    outs = [loss, gx[None]]
    for part in range(4):
        outs += [res[nm][part] for nm in order]
    return tuple(outs)
```

```python
import math

import numpy as np
import jax
import jax.numpy as jnp
from jax import lax
from jax.experimental import pallas as pl
from jax.experimental.pallas import tpu as pltpu

F32 = jnp.float32
BF = jnp.bfloat16
MESH = pl.DeviceIdType.MESH

D_MODEL = 1024
D_FF = 4096
N_HEADS = 16
N_KV = 2
GROUP = N_HEADS // N_KV
HEAD_DIM = 64
ATTN_DIM = N_HEADS * HEAD_DIM
KV_DIM = N_KV * HEAD_DIM
QKV_DIM = ATTN_DIM + 2 * KV_DIM
BLOCK = 128
CONV_W = 31
HALO = 32
REL_BUCKETS = 32
REL_MAX_DIST = 128
NORM_EPS = 1e-6
NEG_INF = -1e30
N_SHARD = 4
LANES = 1024

ADAM_LR = 0.001
ADAM_B1 = 0.9
ADAM_B2 = 0.999
ADAM_EPS = 1e-08
ADAM_WD = 0.01
ADAM_STEP = 10

VMEM_LIMIT = 56 * 1024 * 1024


def _params(n_axes):
    return pltpu.CompilerParams(dimension_semantics=("arbitrary",) * n_axes, vmem_limit_bytes=VMEM_LIMIT)


def _dot(a, b, ca, cb):
    return lax.dot_general(a, b, (((ca,), (cb,)), ((), ())), preferred_element_type=F32)


def _mm(name, a, b, *, nt, tm, tn, tk, ep_fn, outs, a_fn=None, b_sm=False, ep_in=(), deps=()):
    M, K = a.shape
    if b_sm:
        S = b.shape[0]
        N, per = (b.shape[1], b.shape[2] // tk) if nt else (S * b.shape[2], b.shape[2] // tn)
        assert (S * b.shape[2] == K) if nt else (b.shape[1] == K)
    else:
        N = b.shape[0] if nt else b.shape[1]
        assert (b.shape[1] if nt else b.shape[0]) == K
    assert M % tm == 0 and N % tn == 0 and K % tk == 0
    nk = K // tk
    ne, no, nd = len(ep_in), len(outs), len(deps)

    def body(a_ref, b_ref, *rest):
        ep_refs, out_refs = rest[:ne], rest[ne + nd:ne + nd + no]
        i, k = pl.program_id(1), pl.program_id(2)
        av = a_ref[...]
        if a_fn is not None:
            av = a_fn(av)
        part = _dot(av.astype(BF), b_ref[...].astype(BF), 1, 1 if nt else 0)

        def finish(acc):
            vals = ep_fn(acc, *[r[...] for r in ep_refs])
            for (kind, dt), ref, val in zip(outs, out_refs, vals):
                if kind == "tile":
                    ref[...] = val.astype(dt)
                else:
                    @pl.when(i == 0)
                    def _():
                        ref[...] = val

                    @pl.when(i > 0)
                    def _():
                        ref[...] += val

        if nk == 1:
            finish(part)
        else:
            acc_ref = rest[-1]

            @pl.when(k == 0)
            def _():
                acc_ref[...] = part

            @pl.when(k > 0)
            def _():
                acc_ref[...] += part

            @pl.when(k == nk - 1)
            def _():
                finish(acc_ref[...])

    if b_sm and nt:
        b_spec = pl.BlockSpec((None, tn, tk), lambda j, i, k: (k // per, j, k % per))
    elif b_sm:
        b_spec = pl.BlockSpec((None, tk, tn), lambda j, i, k: (j // per, k, j % per))
    elif nt:
        b_spec = pl.BlockSpec((tn, tk), lambda j, i, k: (j, k))
    else:
        b_spec = pl.BlockSpec((tk, tn), lambda j, i, k: (k, j))
    in_specs = [pl.BlockSpec((tm, tk), lambda j, i, k: (i, k)), b_spec]
    for arr, kind in ep_in:
        if kind == "tile":
            assert arr.shape == (M, N)
            in_specs.append(pl.BlockSpec((tm, tn), lambda j, i, k: (i, j)))
        else:
            assert arr.shape == (1, N)
            in_specs.append(pl.BlockSpec((1, tn), lambda j, i, k: (0, j)))
    in_specs += [pl.BlockSpec(memory_space=pl.ANY)] * nd
    out_shape, out_specs = [], []
    for kind, dt in outs:
        if kind == "tile":
            out_shape.append(jax.ShapeDtypeStruct((M, N), dt))
            out_specs.append(pl.BlockSpec((tm, tn), lambda j, i, k: (i, j)))
        else:
            out_shape.append(jax.ShapeDtypeStruct((1, N), F32))
            out_specs.append(pl.BlockSpec((1, tn), lambda j, i, k: (0, j)))
    return pl.pallas_call(
        body, name=name, grid=(N // tn, M // tm, nk), in_specs=in_specs, out_specs=out_specs, out_shape=out_shape,
        scratch_shapes=[pltpu.VMEM((tm, tn), F32)] if nk > 1 else [],
        compiler_params=_params(3),
    )(a, b, *[arr for arr, _ in ep_in], *deps)


def _mm_tn(name, a, b, *, tm, tn, tk, a_fn=None, out_sm=None):
    T, Ka = a.shape
    N = b.shape[1]
    assert b.shape[0] == T and T % tk == 0 and Ka % tm == 0 and N % tn == 0
    nk = T // tk

    def body(a_ref, b_ref, o_ref, acc_ref):
        k = pl.program_id(2)
        av = a_ref[...]
        if a_fn is not None:
            av = a_fn(av)
        part = _dot(av.astype(BF), b_ref[...].astype(BF), 0, 0)

        @pl.when(k == 0)
        def _():
            acc_ref[...] = part

        @pl.when(k > 0)
        def _():
            acc_ref[...] += part

        @pl.when(k == nk - 1)
        def _():
            o_ref[...] = acc_ref[...].astype(BF)

    if out_sm is None:
        out_shape = jax.ShapeDtypeStruct((Ka, N), BF)
        out_spec = pl.BlockSpec((tm, tn), lambda i, j, k: (i, j))
    else:
        per = (N // out_sm) // tn
        assert per * tn * out_sm == N
        out_shape = jax.ShapeDtypeStruct((out_sm, Ka, N // out_sm), BF)
        out_spec = pl.BlockSpec((None, tm, tn), lambda i, j, k: (j // per, i, j % per))
    return pl.pallas_call(
        body, name=name, grid=(Ka // tm, N // tn, nk),
        in_specs=[pl.BlockSpec((tk, tm), lambda i, j, k: (k, i)), pl.BlockSpec((tk, tn), lambda i, j, k: (k, j))],
        out_specs=out_spec, out_shape=out_shape, scratch_shapes=[pltpu.VMEM((tm, tn), F32)],
        compiler_params=_params(3),
    )(a, b)


def _relu2(v):
    r = jnp.maximum(v.astype(F32), 0.0)
    return r * r


def _rms_bwd_ep(dh, x, g, dres):
    rstd = lax.rsqrt(jnp.mean(x * x, axis=-1, keepdims=True) + NORM_EPS)
    xh = x * rstd
    dxh = dh * g
    dx = rstd * (dxh - xh * jnp.mean(dxh * xh, axis=-1, keepdims=True))
    tot = dres + dx
    return tot, jnp.sum(dh * xh, axis=0, keepdims=True), jnp.sum(tot, axis=0, keepdims=True)


def _rms_fwd(name, x, g, tm=512, deps=()):
    T, Dm = x.shape

    def body(x_ref, g_ref, *rest):
        o_ref = rest[-1]
        xv = x_ref[...]
        rstd = lax.rsqrt(jnp.mean(xv * xv, axis=-1, keepdims=True) + NORM_EPS)
        o_ref[...] = (xv * rstd * g_ref[...]).astype(BF)

    return pl.pallas_call(
        body, name=name, grid=(T // tm,),
        in_specs=[pl.BlockSpec((tm, Dm), lambda i: (i, 0)), pl.BlockSpec((1, Dm), lambda i: (0, 0))]
        + [pl.BlockSpec(memory_space=pl.ANY)] * len(deps),
        out_specs=pl.BlockSpec((tm, Dm), lambda i: (i, 0)), out_shape=jax.ShapeDtypeStruct((T, Dm), BF),
        compiler_params=_params(1),
    )(x, g, *deps)


def _head_sum(v, ones_bd):
    hi = v.astype(BF)
    lo = (v - hi.astype(F32)).astype(BF)
    return _dot(hi, ones_bd, 1, 0) + _dot(lo, ones_bd, 1, 0)


def _block_ones(n):
    idx = np.arange(n) // HEAD_DIM
    return jnp.asarray((idx[:, None] == idx[None, :]).astype(np.float32), dtype=BF)


def _qk_norm_fwd(qkv, qg_t, kg_t, tm=256):
    T = qkv.shape[0]
    scale = 1.0 / math.sqrt(HEAD_DIM)

    def body(x_ref, qg_ref, kg_ref, bq_ref, bk_ref, q_ref, k_ref, v_ref):
        q = x_ref[:, pl.ds(0, ATTN_DIM)]
        rq = lax.rsqrt(_head_sum(q * q, bq_ref[...]) * (1.0 / HEAD_DIM) + NORM_EPS)
        q_ref[...] = (q * rq * qg_ref[...] * scale).astype(BF)
        k = x_ref[:, pl.ds(ATTN_DIM, KV_DIM)]
        rk = lax.rsqrt(_head_sum(k * k, bk_ref[...]) * (1.0 / HEAD_DIM) + NORM_EPS)
        k_ref[...] = (k * rk * kg_ref[...]).astype(BF)
        v_ref[...] = x_ref[:, pl.ds(ATTN_DIM + KV_DIM, KV_DIM)].astype(BF)

    full = lambda shape: pl.BlockSpec(shape, lambda i: (0, 0))
    return pl.pallas_call(
        body, name="qk_norm_fwd", grid=(T // tm,),
        in_specs=[pl.BlockSpec((tm, QKV_DIM), lambda i: (i, 0)), full((1, ATTN_DIM)), full((1, KV_DIM)),
                  full((ATTN_DIM, ATTN_DIM)), full((KV_DIM, KV_DIM))],
        out_specs=[pl.BlockSpec((tm, ATTN_DIM), lambda i: (i, 0)), pl.BlockSpec((tm, KV_DIM), lambda i: (i, 0)),
                   pl.BlockSpec((tm, KV_DIM), lambda i: (i, 0))],
        out_shape=[jax.ShapeDtypeStruct((T, ATTN_DIM), BF), jax.ShapeDtypeStruct((T, KV_DIM), BF),
                   jax.ShapeDtypeStruct((T, KV_DIM), BF)],
        compiler_params=_params(1),
    )(qkv, qg_t, kg_t, _block_ones(ATTN_DIM), _block_ones(KV_DIM))


def _qk_norm_bwd(qkv, dqn, dkn, dv, qg_t, kg_t, tm=256):
    T = qkv.shape[0]

    def body(x_ref, dq_ref, dk_ref, dv_ref, qg_ref, kg_ref, bq_ref, bk_ref, o_ref, db_ref, dqg_ref, dkg_ref):
        i = pl.program_id(0)

        def one(x, dy, g, ones_bd):
            r = lax.rsqrt(_head_sum(x * x, ones_bd) * (1.0 / HEAD_DIM) + NORM_EPS)
            xh = x * r
            dxh = dy * g
            dx = r * (dxh - xh * (_head_sum(dxh * xh, ones_bd) * (1.0 / HEAD_DIM)))
            return dx, jnp.sum(dy * xh, axis=0, keepdims=True)

        dq, dqg = one(x_ref[:, pl.ds(0, ATTN_DIM)], dq_ref[...], qg_ref[...], bq_ref[...])
        dk, dkg = one(x_ref[:, pl.ds(ATTN_DIM, KV_DIM)], dk_ref[...], kg_ref[...], bk_ref[...])
        dvv = dv_ref[...]
        o_ref[:, pl.ds(0, ATTN_DIM)] = dq.astype(BF)
        o_ref[:, pl.ds(ATTN_DIM, KV_DIM)] = dk.astype(BF)
        o_ref[:, pl.ds(ATTN_DIM + KV_DIM, KV_DIM)] = dvv.astype(BF)
        sq, sk, sv = (jnp.sum(t, axis=0, keepdims=True) for t in (dq, dk, dvv))

        @pl.when(i == 0)
        def _():
            db_ref[:, pl.ds(0, ATTN_DIM)] = sq
            db_ref[:, pl.ds(ATTN_DIM, KV_DIM)] = sk
            db_ref[:, pl.ds(ATTN_DIM + KV_DIM, KV_DIM)] = sv
            dqg_ref[...] = dqg
            dkg_ref[...] = dkg

        @pl.when(i > 0)
        def _():
            db_ref[:, pl.ds(0, ATTN_DIM)] += sq
            db_ref[:, pl.ds(ATTN_DIM, KV_DIM)] += sk
            db_ref[:, pl.ds(ATTN_DIM + KV_DIM, KV_DIM)] += sv
            dqg_ref[...] += dqg
            dkg_ref[...] += dkg

    full = lambda shape: pl.BlockSpec(shape, lambda i: (0, 0))
    row = lambda n: pl.BlockSpec((tm, n), lambda i: (i, 0))
    return pl.pallas_call(
        body, name="qk_norm_bwd", grid=(T // tm,),
        in_specs=[row(QKV_DIM), row(ATTN_DIM), row(KV_DIM), row(KV_DIM), full((1, ATTN_DIM)), full((1, KV_DIM)),
                  full((ATTN_DIM, ATTN_DIM)), full((KV_DIM, KV_DIM))],
        out_specs=[row(QKV_DIM), full((1, QKV_DIM)), full((1, ATTN_DIM)), full((1, KV_DIM))],
        out_shape=[jax.ShapeDtypeStruct((T, QKV_DIM), BF), jax.ShapeDtypeStruct((1, QKV_DIM), F32),
                   jax.ShapeDtypeStruct((1, ATTN_DIM), F32), jax.ShapeDtypeStruct((1, KV_DIM), F32)],
        compiler_params=_params(1),
    )(qkv, dqn, dkn, dv, qg_t, kg_t, _block_ones(ATTN_DIM), _block_ones(KV_DIM))


ROWS = 64
COLS = 128


def _glu(a, g):
    return a.astype(F32) * jax.nn.sigmoid(g.astype(F32))


def _conv_fwd(u, dw_pad, dw_b, ln_g, ln_b, tm=256):
    T = u.shape[0]
    Dm = D_MODEL
    hpt = tm // HALO

    def body(ac_ref, gc_ref, ap_ref, gp_ref, w_ref, wb_ref, lg_ref, lb_ref, cv_ref, s_ref, ext):
        i = pl.program_id(0)
        ext[pl.ds(0, HALO), :] = jnp.where(i > 0, _glu(ap_ref[...], gp_ref[...]), 0.0)
        ext[pl.ds(HALO, tm), :] = _glu(ac_ref[...], gc_ref[...])

        def rows(r, carry):
            r0 = pl.multiple_of(r * ROWS, ROWS)
            for c in range(Dm // COLS):
                cs = pl.ds(c * COLS, COLS)
                xe = ext[pl.ds(r0, ROWS + HALO), cs]
                acc = jnp.zeros((ROWS, COLS), F32)
                for j in range(CONV_W):
                    off = HALO - (CONV_W - 1) + j
                    acc = acc + xe[off:off + ROWS, :] * w_ref[pl.ds(j, 1), cs]
                cv_ref[pl.ds(r0, ROWS), cs] = acc + wb_ref[:, cs]
            return carry

        lax.fori_loop(0, tm // ROWS, rows, 0)
        cv = cv_ref[...]
        xc = cv - jnp.mean(cv, axis=-1, keepdims=True)
        y = xc * lax.rsqrt(jnp.mean(xc * xc, axis=-1, keepdims=True) + NORM_EPS) * lg_ref[...] + lb_ref[...]
        s_ref[...] = (y * jax.nn.sigmoid(y)).astype(BF)

    full = lambda shape: pl.BlockSpec(shape, lambda i: (0, 0))
    return pl.pallas_call(
        body, name="conv_fwd", grid=(T // tm,),
        in_specs=[pl.BlockSpec((tm, Dm), lambda i: (i, 0)), pl.BlockSpec((tm, Dm), lambda i: (i, 1)),
                  pl.BlockSpec((HALO, Dm), lambda i: (jnp.maximum(i * hpt - 1, 0), 0)),
                  pl.BlockSpec((HALO, Dm), lambda i: (jnp.maximum(i * hpt - 1, 0), 1)),
                  full((HALO, Dm)), full((1, Dm)), full((1, Dm)), full((1, Dm))],
        out_specs=[pl.BlockSpec((tm, Dm), lambda i: (i, 0)), pl.BlockSpec((tm, Dm), lambda i: (i, 0))],
        out_shape=[jax.ShapeDtypeStruct((T, Dm), F32), jax.ShapeDtypeStruct((T, Dm), BF)],
        scratch_shapes=[pltpu.VMEM((tm + HALO, Dm), F32)],
        compiler_params=_params(1),
    )(u, u, u, u, dw_pad, dw_b, ln_g, ln_b)


def _ln_silu_bwd_ep(ds, cv, lg, lb):
    xc = cv - jnp.mean(cv, axis=-1, keepdims=True)
    rstd = lax.rsqrt(jnp.mean(xc * xc, axis=-1, keepdims=True) + NORM_EPS)
    xh = xc * rstd
    y = xh * lg + lb
    sg = jax.nn.sigmoid(y)
    dy = ds * (sg * (1.0 + y * (1.0 - sg)))
    dxh = dy * lg
    dcv = rstd * (dxh - jnp.mean(dxh, axis=-1, keepdims=True) - xh * jnp.mean(dxh * xh, axis=-1, keepdims=True))
    return (dcv, jnp.sum(dy * xh, axis=0, keepdims=True), jnp.sum(dy, axis=0, keepdims=True),
            jnp.sum(dcv, axis=0, keepdims=True))


def _conv_bwd(u, dcv, dw_pad, tm=256):
    T = u.shape[0]
    Dm = D_MODEL
    hpt = tm // HALO
    last = T // HALO - 1
    nt = T // tm

    def body(ac_ref, gc_ref, ap_ref, gp_ref, dc_ref, dn_ref, w_ref, du_ref, db_ref, dw_ref, ext_g, ext_d):
        i = pl.program_id(0)
        ext_g[pl.ds(0, HALO), :] = jnp.where(i > 0, _glu(ap_ref[...], gp_ref[...]), 0.0)
        ext_g[pl.ds(HALO, tm), :] = _glu(ac_ref[...], gc_ref[...])
        ext_d[pl.ds(0, tm), :] = dc_ref[...]
        ext_d[pl.ds(tm, HALO), :] = jnp.where(i < nt - 1, dn_ref[...], 0.0)

        @pl.when(i == 0)
        def _():
            db_ref[...] = jnp.zeros_like(db_ref)
            dw_ref[...] = jnp.zeros_like(dw_ref)

        def rows(r, carry):
            r0 = pl.multiple_of(r * ROWS, ROWS)
            rs = pl.ds(r0, ROWS)
            for c in range(Dm // COLS):
                cs = pl.ds(c * COLS, COLS)
                cs2 = pl.ds(Dm + c * COLS, COLS)
                de = ext_d[pl.ds(r0, ROWS + HALO), cs]
                ge = ext_g[pl.ds(r0, ROWS + HALO), cs]
                dcur = de[0:ROWS, :]
                acc = jnp.zeros((ROWS, COLS), F32)
                for j in range(CONV_W):
                    off = CONV_W - 1 - j
                    acc = acc + de[off:off + ROWS, :] * w_ref[pl.ds(j, 1), cs]
                    goff = HALO - (CONV_W - 1) + j
                    prod = dcur * ge[goff:goff + ROWS, :]
                    dw_ref[j, :, cs] += jnp.sum(prod.reshape(ROWS // 8, 8, COLS), axis=0)
                a = ac_ref[rs, cs].astype(F32)
                sg = jax.nn.sigmoid(gc_ref[rs, cs].astype(F32))
                da = acc * sg
                dg = acc * a * sg * (1.0 - sg)
                du_ref[rs, cs] = da.astype(BF)
                du_ref[rs, cs2] = dg.astype(BF)
                db_ref[:, cs] += jnp.sum(da, axis=0, keepdims=True)
                db_ref[:, cs2] += jnp.sum(dg, axis=0, keepdims=True)
            return carry

        lax.fori_loop(0, tm // ROWS, rows, 0)

    return pl.pallas_call(
        body, name="conv_bwd", grid=(nt,),
        in_specs=[pl.BlockSpec((tm, Dm), lambda i: (i, 0)), pl.BlockSpec((tm, Dm), lambda i: (i, 1)),
                  pl.BlockSpec((HALO, Dm), lambda i: (jnp.maximum(i * hpt - 1, 0), 0)),
                  pl.BlockSpec((HALO, Dm), lambda i: (jnp.maximum(i * hpt - 1, 0), 1)),
                  pl.BlockSpec((tm, Dm), lambda i: (i, 0)),
                  pl.BlockSpec((HALO, Dm), lambda i: (jnp.minimum((i + 1) * hpt, last), 0)),
                  pl.BlockSpec((HALO, Dm), lambda i: (0, 0))],
        out_specs=[pl.BlockSpec((tm, 2 * Dm), lambda i: (i, 0)), pl.BlockSpec((1, 2 * Dm), lambda i: (0, 0)),
                   pl.BlockSpec((HALO, 8, Dm), lambda i: (0, 0, 0))],
        out_shape=[jax.ShapeDtypeStruct((T, 2 * Dm), BF), jax.ShapeDtypeStruct((1, 2 * Dm), F32),
                   jax.ShapeDtypeStruct((HALO, 8, Dm), F32)],
        scratch_shapes=[pltpu.VMEM((tm + HALO, Dm), F32), pltpu.VMEM((tm + HALO, Dm), F32)],
        compiler_params=_params(1),
    )(u, u, u, u, dcv, dcv, dw_pad)


def _bucket_table():
    q_loc = np.arange(BLOCK)[:, None]
    k_loc = np.arange(2 * BLOCK)[None, :]
    dist = q_loc + BLOCK - k_loc
    n = np.maximum(dist, 0)
    max_exact = REL_BUCKETS // 2
    large = max_exact + (np.log(np.maximum(n, 1).astype(np.float32) / max_exact)
                         / math.log(REL_MAX_DIST / max_exact) * (REL_BUCKETS - max_exact)).astype(np.int32)
    large = np.minimum(large, REL_BUCKETS - 1)
    bucket = np.where(n < max_exact, n, large).astype(np.int32)
    return jnp.asarray(np.where((dist >= 0) & (dist < BLOCK), bucket, -1).astype(np.int32))


def _bias_table(rel_bias, bucket):
    def body(rb_ref, bk_ref, o_ref):
        bk = bk_ref[...]
        for h in range(N_HEADS):
            acc = jnp.full((BLOCK, 2 * BLOCK), NEG_INF, F32)
            for b in range(REL_BUCKETS):
                acc = jnp.where(bk == b, rb_ref[b, h], acc)
            o_ref[h] = acc

    return pl.pallas_call(
        body, name="bias_table", out_shape=jax.ShapeDtypeStruct((N_HEADS, BLOCK, 2 * BLOCK), F32),
        in_specs=[pl.BlockSpec(memory_space=pltpu.SMEM), pl.BlockSpec(memory_space=pltpu.VMEM)],
        out_specs=pl.BlockSpec(memory_space=pltpu.VMEM),
    )(rel_bias, bucket)


def _bias_grad(dbias, bucket):
    def body(db_ref, bk_ref, o_ref):
        bk = bk_ref[...]
        for b in range(REL_BUCKETS):
            sel = bk == b
            for h in range(N_HEADS):
                o_ref[b, h] = jnp.sum(jnp.where(sel, db_ref[h], 0.0))

    return pl.pallas_call(
        body, name="bias_grad", out_shape=jax.ShapeDtypeStruct((REL_BUCKETS, N_HEADS), F32),
        in_specs=[pl.BlockSpec(memory_space=pltpu.VMEM), pl.BlockSpec(memory_space=pltpu.VMEM)],
        out_specs=pl.BlockSpec(memory_space=pltpu.SMEM),
    )(dbias, bucket)


def _band_probs(q, k, bias_h, sink, first):
    s = _dot(q, k, 1, 1) + bias_h
    s = jnp.where(first, NEG_INF, s)
    m = jnp.maximum(jnp.max(s, axis=-1, keepdims=True), sink)
    p = jnp.exp(s - m)
    ps = jnp.exp(sink - m)
    inv = 1.0 / (jnp.sum(p, axis=-1, keepdims=True) + ps)
    return p * inv, ps * inv


def _band(prev_ref, cur_ref, g):
    hs = pl.ds(g * HEAD_DIM, HEAD_DIM)
    return jnp.concatenate([prev_ref[:, hs], cur_ref[:, hs]], axis=0)


def _first_mask(n):
    col = lax.broadcasted_iota(jnp.int32, (BLOCK, 2 * BLOCK), 1)
    return jnp.logical_and(n == 0, col < BLOCK)


def _attn_fwd(qn, kn, vv, bias, sinks):
    T = qn.shape[0]
    nb = T // BLOCK

    def body(sk_ref, q_ref, kc_ref, kp_ref, vc_ref, vp_ref, b_ref, o_ref):
        first = _first_mask(pl.program_id(0))
        for g in range(N_KV):
            k = _band(kp_ref, kc_ref, g)
            v = _band(vp_ref, vc_ref, g)
            for hh in range(GROUP):
                h = g * GROUP + hh
                hs = pl.ds(h * HEAD_DIM, HEAD_DIM)
                pn, _ = _band_probs(q_ref[:, hs], k, b_ref[h], sk_ref[h], first)
                o_ref[:, hs] = _dot(pn.astype(BF), v, 1, 0).astype(BF)

    cur = lambda n: (n, 0)
    prev = lambda n: (jnp.maximum(n - 1, 0), 0)
    return pl.pallas_call(
        body, name="attn_fwd", grid=(nb,),
        in_specs=[pl.BlockSpec(memory_space=pltpu.SMEM), pl.BlockSpec((BLOCK, ATTN_DIM), cur),
                  pl.BlockSpec((BLOCK, KV_DIM), cur), pl.BlockSpec((BLOCK, KV_DIM), prev),
                  pl.BlockSpec((BLOCK, KV_DIM), cur), pl.BlockSpec((BLOCK, KV_DIM), prev),
                  pl.BlockSpec((N_HEADS, BLOCK, 2 * BLOCK), lambda n: (0, 0, 0))],
        out_specs=pl.BlockSpec((BLOCK, ATTN_DIM), cur), out_shape=jax.ShapeDtypeStruct((T, ATTN_DIM), BF),
        compiler_params=_params(1),
    )(sinks, qn, kn, kn, vv, vv, bias)


def _attn_bwd(qn, kn, vv, bias, sinks, do):
    T = qn.shape[0]
    nb = T // BLOCK
    scale = 1.0 / math.sqrt(HEAD_DIM)

    def body(sk_ref, q_ref, kc_ref, kp_ref, vc_ref, vp_ref, b_ref, do_ref,
             dq_ref, dk_ref, dv_ref, db_ref, dsk_ref, dk_full, dv_full, dk_carry, dv_carry):
        n = pl.program_id(0)

        @pl.when(n == 0)
        def _():
            db_ref[...] = jnp.zeros_like(db_ref)
            dk_carry[...] = jnp.zeros_like(dk_carry)
            dv_carry[...] = jnp.zeros_like(dv_carry)
            for h in range(N_HEADS):
                dsk_ref[h] = 0.0

        @pl.when(n < nb)
        def _():
            first = _first_mask(n)
            for g in range(N_KV):
                k = _band(kp_ref, kc_ref, g)
                v = _band(vp_ref, vc_ref, g)
                dk_g = jnp.zeros((2 * BLOCK, HEAD_DIM), F32)
                dv_g = jnp.zeros((2 * BLOCK, HEAD_DIM), F32)
                for hh in range(GROUP):
                    h = g * GROUP + hh
                    hs = pl.ds(h * HEAD_DIM, HEAD_DIM)
                    q = q_ref[:, hs]
                    doh = do_ref[:, hs]
                    pn, psink = _band_probs(q, k, b_ref[h], sk_ref[h], first)
                    dp = _dot(doh, v, 1, 1)
                    delta = jnp.sum(pn * dp, axis=-1, keepdims=True)
                    ds = pn * (dp - delta)
                    dsk_ref[h] += -jnp.sum(psink * delta)
                    db_ref[h] += ds
                    dsb = ds.astype(BF)
                    dq_ref[:, hs] = _dot(dsb, k, 1, 0) * scale
                    dk_g = dk_g + _dot(dsb, q, 0, 0)
                    dv_g = dv_g + _dot(pn.astype(BF), doh, 0, 0)
                gs = pl.ds(g * HEAD_DIM, HEAD_DIM)
                dk_full[:, gs] = dk_g
                dv_full[:, gs] = dv_g

        @pl.when(n == nb)
        def _():
            dk_full[...] = jnp.zeros_like(dk_full)
            dv_full[...] = jnp.zeros_like(dv_full)

        dk_ref[...] = dk_carry[...] + dk_full[pl.ds(0, BLOCK), :]
        dv_ref[...] = dv_carry[...] + dv_full[pl.ds(0, BLOCK), :]
        dk_carry[...] = dk_full[pl.ds(BLOCK, BLOCK), :]
        dv_carry[...] = dv_full[pl.ds(BLOCK, BLOCK), :]

    cur = lambda n: (jnp.minimum(n, nb - 1), 0)
    prev = lambda n: (jnp.maximum(jnp.minimum(n, nb - 1) - 1, 0), 0)
    out_kv = lambda n: (jnp.maximum(n - 1, 0), 0)
    return pl.pallas_call(
        body, name="attn_bwd", grid=(nb + 1,),
        in_specs=[pl.BlockSpec(memory_space=pltpu.SMEM), pl.BlockSpec((BLOCK, ATTN_DIM), cur),
                  pl.BlockSpec((BLOCK, KV_DIM), cur), pl.BlockSpec((BLOCK, KV_DIM), prev),
                  pl.BlockSpec((BLOCK, KV_DIM), cur), pl.BlockSpec((BLOCK, KV_DIM), prev),
                  pl.BlockSpec((N_HEADS, BLOCK, 2 * BLOCK), lambda n: (0, 0, 0)),
                  pl.BlockSpec((BLOCK, ATTN_DIM), cur)],
        out_specs=[pl.BlockSpec((BLOCK, ATTN_DIM), cur), pl.BlockSpec((BLOCK, KV_DIM), out_kv),
                   pl.BlockSpec((BLOCK, KV_DIM), out_kv),
                   pl.BlockSpec((N_HEADS, BLOCK, 2 * BLOCK), lambda n: (0, 0, 0)),
                   pl.BlockSpec(memory_space=pltpu.SMEM)],
        out_shape=[jax.ShapeDtypeStruct((T, ATTN_DIM), F32), jax.ShapeDtypeStruct((T, KV_DIM), F32),
                   jax.ShapeDtypeStruct((T, KV_DIM), F32),
                   jax.ShapeDtypeStruct((N_HEADS, BLOCK, 2 * BLOCK), F32), jax.ShapeDtypeStruct((N_HEADS,), F32)],
        scratch_shapes=[pltpu.VMEM((2 * BLOCK, KV_DIM), F32), pltpu.VMEM((2 * BLOCK, KV_DIM), F32),
                        pltpu.VMEM((BLOCK, KV_DIM), F32), pltpu.VMEM((BLOCK, KV_DIM), F32)],
        compiler_params=_params(1),
    )(sinks, qn, kn, kn, vv, vv, bias, do)


def _coords():
    return lax.axis_index("x"), lax.axis_index("y"), lax.axis_index("c")


def _gather8(name, v, with_sum):
    R = v.shape[0]

    def body(v_ref, all_ref, *rest):
        sum_ref = rest[0] if with_sum else None
        send_sems, recv_sems, local_sem = rest[-3:]
        x, y, c = _coords()
        me = 4 * x + 2 * y + c
        local = pltpu.make_async_copy(v_ref, all_ref.at[me], local_sem)
        local.start()
        sends = []
        for k in range(1, 8):
            peer = (x ^ (k >> 2), y ^ ((k >> 1) & 1), c ^ (k & 1))
            cp = pltpu.make_async_remote_copy(src_ref=v_ref, dst_ref=all_ref.at[me], send_sem=send_sems.at[k - 1],
                                              recv_sem=recv_sems.at[k - 1], device_id=peer, device_id_type=MESH)
            cp.start()
            sends.append(cp)
        for k in range(1, 8):
            peer = (x ^ (k >> 2), y ^ ((k >> 1) & 1), c ^ (k & 1))
            pltpu.make_async_remote_copy(src_ref=v_ref, dst_ref=all_ref.at[me ^ k], send_sem=send_sems.at[k - 1],
                                         recv_sem=recv_sems.at[k - 1], device_id=peer, device_id_type=MESH).wait_recv()
        for cp in sends:
            cp.wait_send()
        local.wait()
        if with_sum:
            tot = all_ref[0]
            for d in range(1, 8):
                tot = tot + all_ref[d]
            sum_ref[...] = tot

    out_shape = [jax.ShapeDtypeStruct((8, R, LANES), F32)]
    if with_sum:
        out_shape.append(jax.ShapeDtypeStruct((R, LANES), F32))
    vm = pl.BlockSpec(memory_space=pltpu.VMEM)
    return pl.pallas_call(
        body, name=name, out_shape=out_shape, in_specs=[vm], out_specs=[vm] * len(out_shape),
        scratch_shapes=[pltpu.SemaphoreType.DMA((7,)), pltpu.SemaphoreType.DMA((7,)), pltpu.SemaphoreType.DMA],
    )(v)


CHIP_FLIPS = ((1, 0), (0, 1), (1, 1))


HBM_SPEC = pl.BlockSpec(memory_space=pltpu.HBM)
SEM_SPEC = pl.BlockSpec(memory_space=pltpu.SEMAPHORE)
ANY_SPEC = pl.BlockSpec(memory_space=pl.ANY)
DATAFLOW = pltpu.SideEffectType.DATAFLOW_SIDE_EFFECTING


def _chip_copy(land, sems, idx, slot_src, slot_dst, peer):
    send_sems, recv_sems = sems
    return pltpu.make_async_remote_copy(src_ref=land.at[slot_src], dst_ref=land.at[slot_dst], send_sem=send_sems.at[idx],
                                        recv_sem=recv_sems.at[idx], device_id=peer, device_id_type=MESH)


def _gather_start(stacks, groups, after):
    n = len(stacks)
    ng = len(groups)
    after = tuple(after)

    def body(*refs):
        lands = refs[:n]
        first = n + len(after)
        sems = [(refs[first + 2 * g], refs[first + 2 * g + 1]) for g in range(ng)]
        token = refs[-1]
        x, y, c = _coords()
        s = 2 * x + y
        for g, members in enumerate(groups):
            for i, t in enumerate(members):
                for j, (fx, fy) in enumerate(CHIP_FLIPS):
                    _chip_copy(lands[t], sems[g], 3 * i + j, s, s, (x ^ fx, y ^ fy, c)).start()
        token[...] = jnp.zeros_like(token)

    out_shape = []
    for members in groups:
        out_shape += [pltpu.SemaphoreType.DMA((3 * len(members),))] * 2
    out_shape += [pltpu.HBM(w.shape, w.dtype) for w in stacks]
    out_shape.append(jax.ShapeDtypeStruct((8, 128), F32))
    res = pl.pallas_call(
        body, name="gather_start", out_shape=out_shape, in_specs=[HBM_SPEC] * n + [ANY_SPEC] * len(after),
        out_specs=[SEM_SPEC] * (2 * ng) + [HBM_SPEC] * n + [pl.BlockSpec(memory_space=pltpu.VMEM)],
        input_output_aliases={t: 2 * ng + t for t in range(n)},
        compiler_params=pltpu.CompilerParams(has_side_effects=DATAFLOW),
    )(*[pltpu.with_memory_space_constraint(w, pltpu.HBM) for w in stacks], *after)
    sems = [(res[2 * g], res[2 * g + 1]) for g in range(ng)]
    return sems, list(res[2 * ng:2 * ng + n]), res[-1]


def _gather_wait(name, stacks, sems, after):
    n = len(stacks)
    after = tuple(after)

    def body(*refs):
        lands = refs[:n]
        group_sems = (refs[n], refs[n + 1])
        x, y, c = _coords()
        s = 2 * x + y
        for i in range(n):
            for j, (fx, fy) in enumerate(CHIP_FLIPS):
                cp = _chip_copy(lands[i], group_sems, 3 * i + j, s, 2 * (x ^ fx) + (y ^ fy), (x ^ fx, y ^ fy, c))
                cp.wait_send()
                cp.wait_recv()

    return pl.pallas_call(
        body, name=name, out_shape=[pltpu.HBM(w.shape, w.dtype) for w in stacks],
        in_specs=[HBM_SPEC] * n + [SEM_SPEC, SEM_SPEC] + [ANY_SPEC] * len(after), out_specs=[HBM_SPEC] * n,
        input_output_aliases={t: t for t in range(n)},
        compiler_params=pltpu.CompilerParams(has_side_effects=DATAFLOW),
    )(*stacks, sems[0], sems[1], *after)


N_PEERS = 7


def _peer(x, y, c, k):
    return x ^ (k >> 2), y ^ ((k >> 1) & 1), c ^ (k & 1)


def _reduce_copy(grad, land, sems, idx, x, y, c, k):
    px, py, pc = _peer(x, y, c, k)
    rh = grad.shape[1] // 2
    return pltpu.make_async_remote_copy(src_ref=grad.at[2 * px + py, pl.ds(pc * rh, rh), :], dst_ref=land.at[k - 1],
                                        send_sem=sems[0].at[idx], recv_sem=sems[1].at[idx], device_id=(px, py, pc),
                                        device_id_type=MESH)


def _reduce_start(name, grads):
    n = len(grads)

    def body(*refs):
        src, lands, sems, token = refs[:n], refs[n:2 * n], (refs[2 * n], refs[2 * n + 1]), refs[-1]
        x, y, c = _coords()
        for t in range(n):
            for k in range(1, N_PEERS + 1):
                _reduce_copy(src[t], lands[t], sems, N_PEERS * t + k - 1, x, y, c, k).start()
        token[...] = jnp.zeros_like(token)

    lands = [lax.empty((N_PEERS, g.shape[1] // 2, g.shape[2]), g.dtype) for g in grads]
    out_shape = [pltpu.SemaphoreType.DMA((N_PEERS * n,))] * 2
    out_shape += [pltpu.HBM(a.shape, a.dtype) for a in list(grads) + lands]
    out_shape.append(jax.ShapeDtypeStruct((8, 128), F32))
    res = pl.pallas_call(
        body, name=name, out_shape=out_shape, in_specs=[HBM_SPEC] * (2 * n),
        out_specs=[SEM_SPEC] * 2 + [HBM_SPEC] * (2 * n) + [pl.BlockSpec(memory_space=pltpu.VMEM)],
        input_output_aliases={t: 2 + t for t in range(2 * n)},
        compiler_params=pltpu.CompilerParams(has_side_effects=DATAFLOW),
    )(*[pltpu.with_memory_space_constraint(a, pltpu.HBM) for a in list(grads) + lands])
    return (res[0], res[1]), list(res[2:2 + n]), list(res[2 + n:2 + 2 * n]), res[-1]


def _reduce_wait(name, grads, lands, sems, after):
    n = len(grads)
    after = tuple(after)

    def body(*refs):
        src, dst, group_sems = refs[:n], refs[n:2 * n], (refs[2 * n], refs[2 * n + 1])
        x, y, c = _coords()
        for t in range(n):
            for k in range(1, N_PEERS + 1):
                cp = _reduce_copy(src[t], dst[t], group_sems, N_PEERS * t + k - 1, x, y, c, k)
                cp.wait_send()
                cp.wait_recv()

    res = pl.pallas_call(
        body, name=name, out_shape=[pltpu.HBM(a.shape, a.dtype) for a in list(grads) + list(lands)],
        in_specs=[HBM_SPEC] * (2 * n) + [SEM_SPEC, SEM_SPEC] + [ANY_SPEC] * len(after), out_specs=[HBM_SPEC] * (2 * n),
        input_output_aliases={t: t for t in range(2 * n)},
        compiler_params=pltpu.CompilerParams(has_side_effects=DATAFLOW),
    )(*grads, *lands, sems[0], sems[1], *after)
    return list(res[:n]), list(res[n:])


def _join_halves(name, halves):
    n = len(halves)

    def body(*refs):
        src, dst = refs[:n], refs[n:2 * n]
        send_sems, recv_sems = refs[2 * n:]
        x, y, c = _coords()
        cps = []
        for t in range(n):
            cp = pltpu.make_async_remote_copy(src_ref=src[t], dst_ref=dst[t], send_sem=send_sems.at[t],
                                              recv_sem=recv_sems.at[t], device_id=(x, y, 1 - c), device_id_type=MESH)
            cp.start()
            cps.append(cp)
        for cp in cps:
            cp.wait()

    anyspec = pl.BlockSpec(memory_space=pl.ANY)
    return pl.pallas_call(
        body, name=name, out_shape=[jax.ShapeDtypeStruct(h.shape, h.dtype) for h in halves],
        in_specs=[anyspec] * n, out_specs=[anyspec] * n,
        scratch_shapes=[pltpu.SemaphoreType.DMA((n,)), pltpu.SemaphoreType.DMA((n,))],
    )(*halves)


def _row_block(rows):
    for rb in (512, 256, 128, 64, 32, 16):
        if rows % rb == 0:
            return rb
    raise ValueError(rows)


def _sum_devices(name, grad, land, place):
    S, R, C = grad.shape
    rh = R // 2
    rb = _row_block(rh)
    nbh = rh // rb

    def body(place_ref, g_ref, l_ref, o_ref):
        tot = g_ref[...].astype(F32)
        for k in range(N_PEERS):
            tot = tot + l_ref[k].astype(F32)
        o_ref[...] = tot

    return pl.pallas_call(
        body, name=name,
        grid_spec=pltpu.PrefetchScalarGridSpec(
            num_scalar_prefetch=1, grid=(nbh,),
            in_specs=[pl.BlockSpec((None, rb, C), lambda r, place: (place[0], place[1] * nbh + r, 0)),
                      pl.BlockSpec((N_PEERS, rb, C), lambda r, place: (0, r, 0))],
            out_specs=pl.BlockSpec((rb, C), lambda r, place: (r, 0))),
        out_shape=jax.ShapeDtypeStruct((rh, C), F32), compiler_params=_params(1),
    )(place, grad, land)


def _adamw_math(w, g, m, v):
    m2 = ADAM_B1 * m + (1.0 - ADAM_B1) * g
    v2 = ADAM_B2 * v + (1.0 - ADAM_B2) * (g * g)
    m_hat = m2 / (1.0 - ADAM_B1 ** ADAM_STEP)
    v_hat = v2 / (1.0 - ADAM_B2 ** ADAM_STEP)
    delta = -ADAM_LR * (m_hat / (jnp.sqrt(v_hat) + ADAM_EPS) + ADAM_WD * w)
    return delta, m2, v2


def _adamw(name, w, m, v, gs):
    L, R, C = w.shape
    Rh = R // 2
    rb = _row_block(Rh)
    nbh = Rh // rb
    assert len(gs) == L

    def body(core_ref, w_ref, m_ref, v_ref, *rest):
        g_refs, (go_ref, d_ref, m2_ref, v2_ref) = rest[:2 * L], rest[2 * L:]
        layer, half = pl.program_id(0), pl.program_id(1)
        mine = half == core_ref[0]
        g = jnp.where(mine, g_refs[0][...], g_refs[1][...])
        for t in range(1, L):
            g = jnp.where(layer == t, jnp.where(mine, g_refs[2 * t][...], g_refs[2 * t + 1][...]), g)
        delta, m2, v2 = _adamw_math(w_ref[...], g, m_ref[...], v_ref[...])
        go_ref[...] = g
        d_ref[...] = delta
        m2_ref[...] = m2
        v2_ref[...] = v2

    wspec = pl.BlockSpec((None, rb, C), lambda l, h, r, core: (l, h * nbh + r, 0))
    gspec = pl.BlockSpec((rb, C), lambda l, h, r, core: (r, 0))
    return pl.pallas_call(
        body, name=name,
        grid_spec=pltpu.PrefetchScalarGridSpec(num_scalar_prefetch=1, grid=(L, 2, nbh),
                                               in_specs=[wspec] * 3 + [gspec] * (2 * L), out_specs=[wspec] * 4),
        out_shape=[jax.ShapeDtypeStruct((L, R, C), F32)] * 4, compiler_params=_params(3),
    )(lax.axis_index("c").astype(jnp.int32).reshape(1), w, m, v, *[g for pair in gs for g in pair])


def _adamw_small(w, g, m, v):
    def body(w_ref, g_ref, m_ref, v_ref, d_ref, m2_ref, v2_ref):
        delta, m2, v2 = _adamw_math(w_ref[...], g_ref[...], m_ref[...], v_ref[...])
        d_ref[...] = delta
        m2_ref[...] = m2
        v2_ref[...] = v2

    return pl.pallas_call(body, name="adamw_small", out_shape=[jax.ShapeDtypeStruct(w.shape, F32)] * 3)(w, g, m, v)


def _pack(arrays):
    rows = []
    for a in arrays:
        a = a.astype(F32).reshape(-1, a.shape[-1])
        r, c = a.shape
        k = -(-c // LANES)
        a = jnp.pad(a, ((0, 0), (0, k * LANES - c))).reshape(r * k, LANES)
        rows.append(jnp.pad(a, ((0, -(r * k) % 8), (0, 0))))
    return jnp.concatenate(rows, axis=0)


def _unpack(buf, shapes):
    out, r0 = [], 0
    for shp in shapes:
        c = shp[-1]
        r = int(np.prod(shp)) // c
        k = -(-c // LANES)
        out.append(buf[r0:r0 + r * k].reshape(r, k * LANES)[:, :c].reshape(shp))
        r0 += r * k + (-(r * k) % 8)
    return out


def _mlp_fwd(tag, x, g, w_up_sm, w_down):
    h = _rms_fwd(f"mlp{tag}_norm", x, g)
    (up,) = _mm(f"mlp{tag}_up", h, w_up_sm, nt=False, b_sm=True, tm=512, tn=1024, tk=1024,
                ep_fn=lambda acc: (acc,), outs=(("tile", BF),))
    return h, up


def _mlp_bwd(tag, dy, x, g, h, up, w_up_sm, w_down, place):
    (dup,) = _mm(f"mlp{tag}_dup", dy, w_down, nt=True, tm=512, tn=1024, tk=1024, ep_in=((up, "tile"),),
                 ep_fn=lambda acc, u: (acc * (2.0 * jnp.maximum(u.astype(F32), 0.0)),), outs=(("tile", BF),))
    dw_down = _mm_tn(f"mlp{tag}_dw_down", up, dy, tm=1024, tn=1024, tk=512, a_fn=_relu2)
    dw_up = _mm_tn(f"mlp{tag}_dw_up", h, dup, tm=1024, tn=1024, tk=512, out_sm=N_SHARD)
    red = _Reduction(f"mlp{tag}", [dw_up, dw_down.reshape(N_SHARD, D_FF // N_SHARD, D_MODEL)], place)
    dx, dg, dx_sum = _mm(f"mlp{tag}_dx", dup, w_up_sm, nt=True, b_sm=True, tm=512, tn=1024, tk=1024,
                         ep_in=((x, "tile"), (g, "row"), (dy, "tile")), ep_fn=_rms_bwd_ep,
                         outs=(("tile", F32), ("colsum", F32), ("colsum", F32)), deps=(red.token,))
    return dx, dg, dx_sum, red


class _Reduction:
    def __init__(self, tag, grads, place):
        self.tag, self.place = tag, place
        self.sems, self.grads, self.lands, self.token = _reduce_start(f"reduce_start_{tag}", grads)

    def finish(self, after):
        grads, lands = _reduce_wait(f"reduce_wait_{self.tag}", self.grads, self.lands, self.sems, after)
        halves = [_sum_devices(f"reduce_sum_{self.tag}{i}", g, l, self.place) for i, (g, l) in enumerate(zip(grads, lands))]
        return list(zip(halves, _join_halves(f"join_halves_{self.tag}", halves)))


def kernel(x, conv_norm_g, conv_w_in, conv_b_in, conv_dw, conv_dw_b, conv_ln_g, conv_ln_b, conv_w_out, conv_b_out, attn_norm_g, w_qkv, b_qkv, q_norm_g, k_norm_g, sinks, w_o, b_o, rel_bias, mlp_norm_g, w_up, w_down, loss_target, m_conv_norm_g, m_conv_w_in, m_conv_b_in, m_conv_dw, m_conv_dw_b, m_conv_ln_g, m_conv_ln_b, m_conv_w_out, m_conv_b_out, m_attn_norm_g, m_w_qkv, m_b_qkv, m_q_norm_g, m_k_norm_g, m_sinks, m_w_o, m_b_o, m_rel_bias, m_mlp_norm_g, m_w_up, m_w_down, v_conv_norm_g, v_conv_w_in, v_conv_b_in, v_conv_dw, v_conv_dw_b, v_conv_ln_g, v_conv_ln_b, v_conv_w_out, v_conv_b_out, v_attn_norm_g, v_w_qkv, v_b_qkv, v_q_norm_g, v_k_norm_g, v_sinks, v_w_o, v_b_o, v_rel_bias, v_mlp_norm_g, v_w_up, v_w_down):
    Dm = D_MODEL
    x2d = x[0]
    tgt = loss_target[0]
    T = x2d.shape[0]
    shard = 2 * lax.axis_index("x") + lax.axis_index("y")

    sharded_small = [conv_dw[0], attn_norm_g, b_qkv, b_o]
    (gathered,) = _gather8("gather_small_weights", _pack(sharded_small), with_sum=False)
    chips = [_unpack(gathered[2 * s], [a.shape for a in sharded_small]) for s in range(N_SHARD)]
    dw_f, attn_norm_f, b_qkv_f, b_o_f = (jnp.concatenate([chips[s][t] for s in range(N_SHARD)], axis=-1)
                                         for t in range(len(sharded_small)))
    dw_pad = jnp.pad(dw_f, ((0, HALO - CONV_W), (0, 0)))

    big = [conv_w_in[0], conv_w_out[0], w_qkv[0], w_o[0], w_up[0], w_up[1], w_down[0], w_down[1]]
    stacks = [lax.dynamic_update_slice(jnp.zeros((N_SHARD,) + w.shape, BF), w.astype(BF)[None], (shard, 0, 0))
              for w in big]
    groups = ((0, 1), (4, 6), (2, 3), (5, 7))
    gather_sems, stacks, gather_token = _gather_start(stacks, groups, after=(gathered,))

    def gathered_group(g, name, after):
        return _gather_wait(name, [stacks[t] for t in groups[g]], gather_sems[g], after)

    bucket = _bucket_table()
    bias = _bias_table(rel_bias, bucket)

    h0 = _rms_fwd("conv_norm", x2d, conv_norm_g, deps=(gather_token,))
    w_in_sm, g_out = gathered_group(0, "gather_wait_conv", (h0, dw_pad, bias))
    w_out_f = g_out.reshape(Dm, Dm)
    (u,) = _mm("conv_in", h0, w_in_sm, nt=False, b_sm=True, tm=512, tn=512, tk=1024, ep_in=((conv_b_in, "row"),),
               ep_fn=lambda acc, b: (acc + b,), outs=(("tile", BF),))
    cv, s_act = _conv_fwd(u, dw_pad, conv_dw_b, conv_ln_g, conv_ln_b)
    (x1,) = _mm("conv_out", s_act, w_out_f, nt=False, tm=512, tn=1024, tk=1024,
                ep_in=((conv_b_out, "row"), (x2d, "tile")), ep_fn=lambda acc, b, r: (acc + b + r,),
                outs=(("tile", F32),))

    g_up0, g_down0 = gathered_group(1, "gather_wait_mlp0", (x1,))
    w_up_sm = [g_up0, None]
    w_down_f = [g_down0.reshape(D_FF, Dm), None]
    h1, up0 = _mlp_fwd(0, x1, mlp_norm_g[0:1], w_up_sm[0], w_down_f[0])
    (x2,) = _mm("mlp0_down", up0, w_down_f[0], nt=False, tm=512, tn=1024, tk=1024, a_fn=_relu2,
                ep_in=((x1, "tile"),), ep_fn=lambda acc, r: (acc + r,), outs=(("tile", F32),))

    g_qkv, g_o = gathered_group(2, "gather_wait_attn", (x2,))
    w_qkv_f = jnp.transpose(g_qkv, (1, 0, 2)).reshape(Dm, QKV_DIM)
    w_o_f = g_o.reshape(ATTN_DIM, Dm)
    h2 = _rms_fwd("attn_norm", x2, attn_norm_f)
    (qkv,) = _mm("attn_qkv", h2, w_qkv_f, nt=False, tm=512, tn=QKV_DIM, tk=1024, ep_in=((b_qkv_f, "row"),),
                 ep_fn=lambda acc, b: (acc + b,), outs=(("tile", F32),))
    qg_t = jnp.tile(q_norm_g, (1, N_HEADS))
    kg_t = jnp.tile(k_norm_g, (1, N_KV))
    qn, kn, vv = _qk_norm_fwd(qkv, qg_t, kg_t)
    sinks1 = sinks[0]
    att = _attn_fwd(qn, kn, vv, bias, sinks1)
    (x3,) = _mm("attn_out", att, w_o_f, nt=False, tm=512, tn=1024, tk=1024,
                ep_in=((b_o_f, "row"), (x2, "tile")), ep_fn=lambda acc, b, r: (acc + b + r,), outs=(("tile", F32),))

    g_up1, g_down1 = gathered_group(3, "gather_wait_mlp1", (x3,))
    w_up_sm[1] = g_up1
    w_down_f[1] = g_down1.reshape(D_FF, Dm)
    h3, up1 = _mlp_fwd(1, x3, mlp_norm_g[1:2], w_up_sm[1], w_down_f[1])

    def loss_ep(acc, r, t):
        diff = acc + r - t
        return diff * (1.0 / Dm), jnp.sum(diff * diff, axis=0, keepdims=True)

    dy, sq = _mm("mlp1_down_loss", up1, w_down_f[1], nt=False, tm=512, tn=1024, tk=1024, a_fn=_relu2,
                 ep_in=((x3, "tile"), (tgt, "tile")), ep_fn=loss_ep, outs=(("tile", F32), ("colsum", F32)))
    loss = lax.psum(0.5 * jnp.sum(sq) * (1.0 / Dm), ("x", "y", "c"))

    place = jnp.stack([shard, lax.axis_index("c")]).astype(jnp.int32)
    dx3, dg_mlp1, db_o, red_mlp1 = _mlp_bwd(1, dy, x3, mlp_norm_g[1:2], h3, up1, w_up_sm[1], w_down_f[1], place)

    ident = lambda acc: (acc,)
    (datt,) = _mm("attn_dout", dx3, w_o_f, nt=True, tm=512, tn=1024, tk=1024, ep_fn=ident, outs=(("tile", BF),))
    dw_o = _mm_tn("attn_dw_o", att, dx3, tm=1024, tn=1024, tk=512)
    dqn, dkn, dvv, dbias, dsinks = _attn_bwd(qn, kn, vv, bias, sinks1, datt)
    (r_up1, r_down1) = red_mlp1.finish((dqn,))
    drel = _bias_grad(dbias, bucket)
    dqkv, db_qkv, dqg_t, dkg_t = _qk_norm_bwd(qkv, dqn, dkn, dvv, qg_t, kg_t)
    dw_qkv = _mm_tn("attn_dw_qkv", h2, dqkv, tm=1024, tn=QKV_DIM, tk=512)
    red_attn = _Reduction("attn", [jnp.transpose(dw_qkv.reshape(Dm, N_SHARD, QKV_DIM // N_SHARD), (1, 0, 2)),
                                   dw_o.reshape(N_SHARD, ATTN_DIM // N_SHARD, Dm)], place)
    dx2, dg_attn, _ = _mm("attn_dx", dqkv, w_qkv_f, nt=True, tm=512, tn=1024, tk=QKV_DIM,
                          ep_in=((x2, "tile"), (attn_norm_f, "row"), (dx3, "tile")), ep_fn=_rms_bwd_ep,
                          outs=(("tile", F32), ("colsum", F32), ("colsum", F32)), deps=(red_attn.token,))

    dx1, dg_mlp0, db_out, red_mlp0 = _mlp_bwd(0, dx2, x1, mlp_norm_g[0:1], h1, up0, w_up_sm[0], w_down_f[0], place)
    (r_qkv, r_o) = red_attn.finish((dx1,))

    dcv, dln_g, dln_b, ddw_b = _mm("conv_ds", dx1, w_out_f, nt=True, tm=512, tn=1024, tk=1024,
                                   ep_in=((cv, "tile"), (conv_ln_g, "row"), (conv_ln_b, "row")),
                                   ep_fn=_ln_silu_bwd_ep,
                                   outs=(("tile", F32), ("colsum", F32), ("colsum", F32), ("colsum", F32)))
    dw_out = _mm_tn("conv_dw_out", s_act, dx1, tm=1024, tn=1024, tk=512)
    du, db_in, ddw8 = _conv_bwd(u, dcv, dw_pad)
    (r_up0, r_down0) = red_mlp0.finish((du,))
    dw_in = _mm_tn("conv_dw_in", h0, du, tm=1024, tn=512, tk=1024, out_sm=N_SHARD)
    red_conv = _Reduction("conv", [dw_in, dw_out.reshape(N_SHARD, Dm // N_SHARD, Dm)], place)
    gx, dg_conv, _ = _mm("conv_dx", du, w_in_sm, nt=True, b_sm=True, tm=512, tn=1024, tk=512,
                         ep_in=((x2d, "tile"), (conv_norm_g, "row"), (dx1, "tile")), ep_fn=_rms_bwd_ep,
                         outs=(("tile", F32), ("colsum", F32), ("colsum", F32)), deps=(red_conv.token,))
    (r_in, r_out) = red_conv.finish((gx,))

    big_out = {}
    for nm, w, m, v, gs in (("conv_w_in", conv_w_in, m_conv_w_in, v_conv_w_in, (r_in,)),
                            ("conv_w_out", conv_w_out, m_conv_w_out, v_conv_w_out, (r_out,)),
                            ("w_qkv", w_qkv, m_w_qkv, v_w_qkv, (r_qkv,)),
                            ("w_o", w_o, m_w_o, v_w_o, (r_o,)),
                            ("w_up", w_up, m_w_up, v_w_up, (r_up0, r_up1)),
                            ("w_down", w_down, m_w_down, v_w_down, (r_down0, r_down1))):
        big_out[nm] = _adamw(f"adamw_{nm}", w, m, v, gs)

    dqg = dqg_t.reshape(N_HEADS, HEAD_DIM).sum(axis=0, keepdims=True)
    dkg = dkg_t.reshape(N_KV, HEAD_DIM).sum(axis=0, keepdims=True)
    small_full = [dg_conv, db_in, ddw8.sum(axis=1)[:CONV_W], ddw_b, dln_g, dln_b, db_out, dg_attn, db_qkv, dqg, dkg,
                  dsinks[None, :], db_o, drel, jnp.pad(dg_mlp0, ((0, 1), (0, 0))) + jnp.pad(dg_mlp1, ((1, 0), (0, 0)))]
    _, small_sum = _gather8("reduce_small_grads", _pack(small_full), with_sum=True)
    (r_norm, r_b_in, r_dw, r_dw_b, r_ln_g, r_ln_b, r_b_out, r_attn_norm, r_b_qkv, r_qg, r_kg, r_sinks, r_b_o, r_rel,
     r_mlp_norm) = _unpack(small_sum, [a.shape for a in small_full])

    def cols(a, width):
        return lax.dynamic_slice_in_dim(a, shard * width, width, axis=a.ndim - 1)

    small_names = ["conv_norm_g", "conv_b_in", "conv_dw", "conv_dw_b", "conv_ln_g", "conv_ln_b", "conv_b_out",
                   "attn_norm_g", "b_qkv", "q_norm_g", "k_norm_g", "sinks", "b_o", "rel_bias", "mlp_norm_g"]
    small_g = [r_norm, r_b_in, cols(r_dw, Dm // N_SHARD)[None], r_dw_b, r_ln_g, r_ln_b, r_b_out,
               cols(r_attn_norm, Dm // N_SHARD), cols(r_b_qkv, QKV_DIM // N_SHARD), r_qg, r_kg, r_sinks,
               cols(r_b_o, Dm // N_SHARD), r_rel, r_mlp_norm]
    small_w = [conv_norm_g, conv_b_in, conv_dw, conv_dw_b, conv_ln_g, conv_ln_b, conv_b_out, attn_norm_g, b_qkv,
               q_norm_g, k_norm_g, sinks, b_o, rel_bias, mlp_norm_g]
    small_m = [m_conv_norm_g, m_conv_b_in, m_conv_dw, m_conv_dw_b, m_conv_ln_g, m_conv_ln_b, m_conv_b_out,
               m_attn_norm_g, m_b_qkv, m_q_norm_g, m_k_norm_g, m_sinks, m_b_o, m_rel_bias, m_mlp_norm_g]
    small_v = [v_conv_norm_g, v_conv_b_in, v_conv_dw, v_conv_dw_b, v_conv_ln_g, v_conv_ln_b, v_conv_b_out,
               v_attn_norm_g, v_b_qkv, v_q_norm_g, v_k_norm_g, v_sinks, v_b_o, v_rel_bias, v_mlp_norm_g]
    flat2 = lambda a: a.reshape(-1, a.shape[-1])
    shapes2 = [flat2(w).shape for w in small_w]
    pk = lambda arrs: _pack([flat2(a) for a in arrs])
    packed_g = pk(small_g)
    d_s, m_s, v_s = _adamw_small(pk(small_w), packed_g, pk(small_m), pk(small_v))
    small_out = {}
    for nm, w, g, d, m2, v2 in zip(small_names, small_w, _unpack(packed_g, shapes2), _unpack(d_s, shapes2),
                                   _unpack(m_s, shapes2), _unpack(v_s, shapes2)):
        small_out[nm] = tuple(a.reshape(w.shape) for a in (g, d, m2, v2))

    order = ["conv_norm_g", "conv_w_in", "conv_b_in", "conv_dw", "conv_dw_b", "conv_ln_g", "conv_ln_b", "conv_w_out",
             "conv_b_out", "attn_norm_g", "w_qkv", "b_qkv", "q_norm_g", "k_norm_g", "sinks", "w_o", "b_o", "rel_bias",
             "mlp_norm_g", "w_up", "w_down"]
    res = {**small_out, **big_out}
    outs = [loss, gx[None]]
    for part in range(4):
        outs += [res[nm][part] for nm in order]
    return tuple(outs)
```

```python
import math

import numpy as np
import jax
import jax.numpy as jnp
from jax import lax
from jax.experimental import pallas as pl
from jax.experimental.pallas import tpu as pltpu

F32 = jnp.float32
BF = jnp.bfloat16
MESH = pl.DeviceIdType.MESH

D_MODEL = 1024
D_FF = 4096
N_HEADS = 16
N_KV = 2
GROUP = N_HEADS // N_KV
HEAD_DIM = 64
ATTN_DIM = N_HEADS * HEAD_DIM
KV_DIM = N_KV * HEAD_DIM
QKV_DIM = ATTN_DIM + 2 * KV_DIM
BLOCK = 128
CONV_W = 31
HALO = 32
REL_BUCKETS = 32
REL_MAX_DIST = 128
NORM_EPS = 1e-6
NEG_INF = -1e30
N_SHARD = 4
LANES = 1024

ADAM_LR = 0.001
ADAM_B1 = 0.9
ADAM_B2 = 0.999
ADAM_EPS = 1e-08
ADAM_WD = 0.01
ADAM_STEP = 10

VMEM_LIMIT = 56 * 1024 * 1024


def _params(n_axes):
    return pltpu.CompilerParams(dimension_semantics=("arbitrary",) * n_axes, vmem_limit_bytes=VMEM_LIMIT)


def _dot(a, b, ca, cb):
    return lax.dot_general(a, b, (((ca,), (cb,)), ((), ())), preferred_element_type=F32)


def _mm(name, a, b, *, nt, tm, tn, ep_fn, outs, a_fn=None, b_sm=False, ep_in=(), deps=(), rows=None):
    M, K = a.shape
    rows = tm if rows is None else rows
    if b_sm:
        S, ks = b.shape[0], b.shape[2]
        N, per = (b.shape[1], None) if nt else (S * b.shape[2], b.shape[2] // tn)
        assert (S * ks == K) if nt else (b.shape[1] == K)
    else:
        N = b.shape[0] if nt else b.shape[1]
        assert (b.shape[1] if nt else b.shape[0]) == K
    assert M % tm == 0 and N % tn == 0 and tm % rows == 0
    ne, no, nd = len(ep_in), len(outs), len(deps)

    def body(a_ref, b_ref, *rest):
        ep_refs, out_refs = rest[:ne], rest[ne + nd:ne + nd + no]
        i = pl.program_id(1)
        sums = [None] * no
        for r in range(tm // rows):
            rs = pl.ds(r * rows, rows)

            def lhs(cols):
                av = a_ref[rs, cols]
                return (av if a_fn is None else a_fn(av)).astype(BF)

            if b_sm and nt:
                acc = None
                for s in range(S):
                    part = _dot(lhs(pl.ds(s * ks, ks)), b_ref[s].astype(BF), 1, 1)
                    acc = part if acc is None else acc + part
            else:
                acc = _dot(lhs(slice(None)), b_ref[...].astype(BF), 1, 1 if nt else 0)
            ep_vals = [ref[rs, :] if kind == "tile" else ref[...] for ref, (_, kind) in zip(ep_refs, ep_in)]
            vals = ep_fn(acc, *ep_vals)
            for o, ((kind, dt), ref, val) in enumerate(zip(outs, out_refs, vals)):
                if kind == "tile":
                    ref[rs, :] = val.astype(dt)
                else:
                    sums[o] = val if sums[o] is None else sums[o] + val
        for (kind, dt), ref, val in zip(outs, out_refs, sums):
            if kind == "colsum":
                @pl.when(i == 0)
                def _():
                    ref[...] = val

                @pl.when(i > 0)
                def _():
                    ref[...] += val

    if b_sm and nt:
        b_spec = pl.BlockSpec((S, tn, ks), lambda j, i: (0, j, 0))
    elif b_sm:
        b_spec = pl.BlockSpec((None, K, tn), lambda j, i: (j // per, 0, j % per))
    elif nt:
        b_spec = pl.BlockSpec((tn, K), lambda j, i: (j, 0))
    else:
        b_spec = pl.BlockSpec((K, tn), lambda j, i: (0, j))
    in_specs = [pl.BlockSpec((tm, K), lambda j, i: (i, 0)), b_spec]
    for arr, kind in ep_in:
        if kind == "tile":
            assert arr.shape == (M, N)
            in_specs.append(pl.BlockSpec((tm, tn), lambda j, i: (i, j)))
        else:
            assert arr.shape == (1, N)
            in_specs.append(pl.BlockSpec((1, tn), lambda j, i: (0, j)))
    in_specs += [pl.BlockSpec(memory_space=pl.ANY)] * nd
    out_shape, out_specs = [], []
    for kind, dt in outs:
        if kind == "tile":
            out_shape.append(jax.ShapeDtypeStruct((M, N), dt))
            out_specs.append(pl.BlockSpec((tm, tn), lambda j, i: (i, j)))
        else:
            out_shape.append(jax.ShapeDtypeStruct((1, N), F32))
            out_specs.append(pl.BlockSpec((1, tn), lambda j, i: (0, j)))
    return pl.pallas_call(
        body, name=name, grid=(N // tn, M // tm), in_specs=in_specs, out_specs=out_specs, out_shape=out_shape,
        compiler_params=_params(2),
    )(a, b, *[arr for arr, _ in ep_in], *deps)


def _mm_tn(name, a, b, *, tm, tn, tk, a_fn=None, out_sm=None):
    T, Ka = a.shape
    N = b.shape[1]
    assert b.shape[0] == T and T % tk == 0 and Ka % tm == 0 and N % tn == 0
    nk = T // tk

    def body(a_ref, b_ref, o_ref, acc_ref):
        k = pl.program_id(2)

        @pl.when(k == 0)
        def _():
            acc_ref[...] = jnp.zeros_like(acc_ref)

        av = a_ref[...]
        if a_fn is not None:
            av = a_fn(av)
        acc_ref[...] += _dot(av.astype(BF), b_ref[...].astype(BF), 0, 0)

        @pl.when(k == nk - 1)
        def _():
            o_ref[...] = acc_ref[...].astype(BF)

    if out_sm is None:
        out_shape = jax.ShapeDtypeStruct((Ka, N), BF)
        out_spec = pl.BlockSpec((tm, tn), lambda i, j, k: (i, j))
    else:
        per = (N // out_sm) // tn
        assert per * tn * out_sm == N
        out_shape = jax.ShapeDtypeStruct((out_sm, Ka, N // out_sm), BF)
        out_spec = pl.BlockSpec((None, tm, tn), lambda i, j, k: (j // per, i, j % per))
    return pl.pallas_call(
        body, name=name, grid=(Ka // tm, N // tn, nk),
        in_specs=[pl.BlockSpec((tk, tm), lambda i, j, k: (k, i)), pl.BlockSpec((tk, tn), lambda i, j, k: (k, j))],
        out_specs=out_spec, out_shape=out_shape, scratch_shapes=[pltpu.VMEM((tm, tn), F32)],
        compiler_params=_params(3),
    )(a, b)


def _relu2(v):
    r = jnp.maximum(v.astype(F32), 0.0)
    return r * r


def _rms_bwd_ep(dh, x, g, dres):
    rstd = lax.rsqrt(jnp.mean(x * x, axis=-1, keepdims=True) + NORM_EPS)
    xh = x * rstd
    dxh = dh * g
    dx = rstd * (dxh - xh * jnp.mean(dxh * xh, axis=-1, keepdims=True))
    tot = dres + dx
    return tot, tot, jnp.sum(dh * xh, axis=0, keepdims=True), jnp.sum(tot, axis=0, keepdims=True)


def _rms_fwd(name, x, g, tm=512, deps=()):
    T, Dm = x.shape

    def body(x_ref, g_ref, *rest):
        o_ref = rest[-1]
        xv = x_ref[...]
        rstd = lax.rsqrt(jnp.mean(xv * xv, axis=-1, keepdims=True) + NORM_EPS)
        o_ref[...] = (xv * rstd * g_ref[...]).astype(BF)

    return pl.pallas_call(
        body, name=name, grid=(T // tm,),
        in_specs=[pl.BlockSpec((tm, Dm), lambda i: (i, 0)), pl.BlockSpec((1, Dm), lambda i: (0, 0))]
        + [pl.BlockSpec(memory_space=pl.ANY)] * len(deps),
        out_specs=pl.BlockSpec((tm, Dm), lambda i: (i, 0)), out_shape=jax.ShapeDtypeStruct((T, Dm), BF),
        compiler_params=_params(1),
    )(x, g, *deps)


def _head_sum(v, ones_bd):
    hi = v.astype(BF)
    lo = (v - hi.astype(F32)).astype(BF)
    return _dot(hi, ones_bd, 1, 0) + _dot(lo, ones_bd, 1, 0)


def _block_ones(n):
    idx = np.arange(n) // HEAD_DIM
    return jnp.asarray((idx[:, None] == idx[None, :]).astype(np.float32), dtype=BF)


def _qk_norm_fwd(qkv, qg_t, kg_t, tm=256):
    T = qkv.shape[0]
    scale = 1.0 / math.sqrt(HEAD_DIM)

    def body(x_ref, qg_ref, kg_ref, bq_ref, bk_ref, q_ref, k_ref, v_ref):
        q = x_ref[:, pl.ds(0, ATTN_DIM)]
        rq = lax.rsqrt(_head_sum(q * q, bq_ref[...]) * (1.0 / HEAD_DIM) + NORM_EPS)
        q_ref[...] = (q * rq * qg_ref[...] * scale).astype(BF)
        k = x_ref[:, pl.ds(ATTN_DIM, KV_DIM)]
        rk = lax.rsqrt(_head_sum(k * k, bk_ref[...]) * (1.0 / HEAD_DIM) + NORM_EPS)
        k_ref[...] = (k * rk * kg_ref[...]).astype(BF)
        v_ref[...] = x_ref[:, pl.ds(ATTN_DIM + KV_DIM, KV_DIM)].astype(BF)

    full = lambda shape: pl.BlockSpec(shape, lambda i: (0, 0))
    return pl.pallas_call(
        body, name="qk_norm_fwd", grid=(T // tm,),
        in_specs=[pl.BlockSpec((tm, QKV_DIM), lambda i: (i, 0)), full((1, ATTN_DIM)), full((1, KV_DIM)),
                  full((ATTN_DIM, ATTN_DIM)), full((KV_DIM, KV_DIM))],
        out_specs=[pl.BlockSpec((tm, ATTN_DIM), lambda i: (i, 0)), pl.BlockSpec((tm, KV_DIM), lambda i: (i, 0)),
                   pl.BlockSpec((tm, KV_DIM), lambda i: (i, 0))],
        out_shape=[jax.ShapeDtypeStruct((T, ATTN_DIM), BF), jax.ShapeDtypeStruct((T, KV_DIM), BF),
                   jax.ShapeDtypeStruct((T, KV_DIM), BF)],
        compiler_params=_params(1),
    )(qkv, qg_t, kg_t, _block_ones(ATTN_DIM), _block_ones(KV_DIM))


def _qk_norm_bwd(qkv, dqn, dkn, dv, qg_t, kg_t, tm=256):
    T = qkv.shape[0]

    def body(x_ref, dq_ref, dk_ref, dv_ref, qg_ref, kg_ref, bq_ref, bk_ref, o_ref, db_ref, dqg_ref, dkg_ref):
        i = pl.program_id(0)

        def one(x, dy, g, ones_bd):
            r = lax.rsqrt(_head_sum(x * x, ones_bd) * (1.0 / HEAD_DIM) + NORM_EPS)
            xh = x * r
            dxh = dy * g
            dx = r * (dxh - xh * (_head_sum(dxh * xh, ones_bd) * (1.0 / HEAD_DIM)))
            return dx, jnp.sum(dy * xh, axis=0, keepdims=True)

        dq, dqg = one(x_ref[:, pl.ds(0, ATTN_DIM)], dq_ref[...], qg_ref[...], bq_ref[...])
        dk, dkg = one(x_ref[:, pl.ds(ATTN_DIM, KV_DIM)], dk_ref[...], kg_ref[...], bk_ref[...])
        dvv = dv_ref[...]
        o_ref[:, pl.ds(0, ATTN_DIM)] = dq.astype(BF)
        o_ref[:, pl.ds(ATTN_DIM, KV_DIM)] = dk.astype(BF)
        o_ref[:, pl.ds(ATTN_DIM + KV_DIM, KV_DIM)] = dvv.astype(BF)
        sq, sk, sv = (jnp.sum(t, axis=0, keepdims=True) for t in (dq, dk, dvv))

        @pl.when(i == 0)
        def _():
            db_ref[:, pl.ds(0, ATTN_DIM)] = sq
            db_ref[:, pl.ds(ATTN_DIM, KV_DIM)] = sk
            db_ref[:, pl.ds(ATTN_DIM + KV_DIM, KV_DIM)] = sv
            dqg_ref[...] = dqg
            dkg_ref[...] = dkg

        @pl.when(i > 0)
        def _():
            db_ref[:, pl.ds(0, ATTN_DIM)] += sq
            db_ref[:, pl.ds(ATTN_DIM, KV_DIM)] += sk
            db_ref[:, pl.ds(ATTN_DIM + KV_DIM, KV_DIM)] += sv
            dqg_ref[...] += dqg
            dkg_ref[...] += dkg

    full = lambda shape: pl.BlockSpec(shape, lambda i: (0, 0))
    row = lambda n: pl.BlockSpec((tm, n), lambda i: (i, 0))
    return pl.pallas_call(
        body, name="qk_norm_bwd", grid=(T // tm,),
        in_specs=[row(QKV_DIM), row(ATTN_DIM), row(KV_DIM), row(KV_DIM), full((1, ATTN_DIM)), full((1, KV_DIM)),
                  full((ATTN_DIM, ATTN_DIM)), full((KV_DIM, KV_DIM))],
        out_specs=[row(QKV_DIM), full((1, QKV_DIM)), full((1, ATTN_DIM)), full((1, KV_DIM))],
        out_shape=[jax.ShapeDtypeStruct((T, QKV_DIM), BF), jax.ShapeDtypeStruct((1, QKV_DIM), F32),
                   jax.ShapeDtypeStruct((1, ATTN_DIM), F32), jax.ShapeDtypeStruct((1, KV_DIM), F32)],
        compiler_params=_params(1),
    )(qkv, dqn, dkn, dv, qg_t, kg_t, _block_ones(ATTN_DIM), _block_ones(KV_DIM))


ROWS = 64
COLS = 128


def _glu(a, g):
    return a.astype(F32) * jax.nn.sigmoid(g.astype(F32))


def _conv_fwd(u, dw_pad, dw_b, ln_g, ln_b, tm=256):
    T = u.shape[0]
    Dm = D_MODEL
    hpt = tm // HALO

    def body(ac_ref, gc_ref, ap_ref, gp_ref, w_ref, wb_ref, lg_ref, lb_ref, cv_ref, s_ref, ext):
        i = pl.program_id(0)
        ext[pl.ds(0, HALO), :] = jnp.where(i > 0, _glu(ap_ref[...], gp_ref[...]), 0.0)
        ext[pl.ds(HALO, tm), :] = _glu(ac_ref[...], gc_ref[...])

        def rows(r, carry):
            r0 = pl.multiple_of(r * ROWS, ROWS)
            for c in range(Dm // COLS):
                cs = pl.ds(c * COLS, COLS)
                xe = ext[pl.ds(r0, ROWS + HALO), cs]
                acc = jnp.zeros((ROWS, COLS), F32)
                for j in range(CONV_W):
                    off = HALO - (CONV_W - 1) + j
                    acc = acc + xe[off:off + ROWS, :] * w_ref[pl.ds(j, 1), cs]
                cv_ref[pl.ds(r0, ROWS), cs] = acc + wb_ref[:, cs]
            return carry

        lax.fori_loop(0, tm // ROWS, rows, 0)
        cv = cv_ref[...]
        xc = cv - jnp.mean(cv, axis=-1, keepdims=True)
        y = xc * lax.rsqrt(jnp.mean(xc * xc, axis=-1, keepdims=True) + NORM_EPS) * lg_ref[...] + lb_ref[...]
        s_ref[...] = (y * jax.nn.sigmoid(y)).astype(BF)

    full = lambda shape: pl.BlockSpec(shape, lambda i: (0, 0))
    return pl.pallas_call(
        body, name="conv_fwd", grid=(T // tm,),
        in_specs=[pl.BlockSpec((tm, Dm), lambda i: (i, 0)), pl.BlockSpec((tm, Dm), lambda i: (i, 1)),
                  pl.BlockSpec((HALO, Dm), lambda i: (jnp.maximum(i * hpt - 1, 0), 0)),
                  pl.BlockSpec((HALO, Dm), lambda i: (jnp.maximum(i * hpt - 1, 0), 1)),
                  full((HALO, Dm)), full((1, Dm)), full((1, Dm)), full((1, Dm))],
        out_specs=[pl.BlockSpec((tm, Dm), lambda i: (i, 0)), pl.BlockSpec((tm, Dm), lambda i: (i, 0))],
        out_shape=[jax.ShapeDtypeStruct((T, Dm), F32), jax.ShapeDtypeStruct((T, Dm), BF)],
        scratch_shapes=[pltpu.VMEM((tm + HALO, Dm), F32)],
        compiler_params=_params(1),
    )(u, u, u, u, dw_pad, dw_b, ln_g, ln_b)


def _ln_silu_bwd_ep(ds, cv, lg, lb):
    xc = cv - jnp.mean(cv, axis=-1, keepdims=True)
    rstd = lax.rsqrt(jnp.mean(xc * xc, axis=-1, keepdims=True) + NORM_EPS)
    xh = xc * rstd
    y = xh * lg + lb
    sg = jax.nn.sigmoid(y)
    dy = ds * (sg * (1.0 + y * (1.0 - sg)))
    dxh = dy * lg
    dcv = rstd * (dxh - jnp.mean(dxh, axis=-1, keepdims=True) - xh * jnp.mean(dxh * xh, axis=-1, keepdims=True))
    return (dcv, jnp.sum(dy * xh, axis=0, keepdims=True), jnp.sum(dy, axis=0, keepdims=True),
            jnp.sum(dcv, axis=0, keepdims=True))


def _conv_bwd(u, dcv, dw_pad, tm=256):
    T = u.shape[0]
    Dm = D_MODEL
    hpt = tm // HALO
    last = T // HALO - 1
    nt = T // tm

    def body(ac_ref, gc_ref, ap_ref, gp_ref, dc_ref, dn_ref, w_ref, du_ref, db_ref, dw_ref, ext_g, ext_d):
        i = pl.program_id(0)
        ext_g[pl.ds(0, HALO), :] = jnp.where(i > 0, _glu(ap_ref[...], gp_ref[...]), 0.0)
        ext_g[pl.ds(HALO, tm), :] = _glu(ac_ref[...], gc_ref[...])
        ext_d[pl.ds(0, tm), :] = dc_ref[...]
        ext_d[pl.ds(tm, HALO), :] = jnp.where(i < nt - 1, dn_ref[...], 0.0)

        @pl.when(i == 0)
        def _():
            db_ref[...] = jnp.zeros_like(db_ref)
            dw_ref[...] = jnp.zeros_like(dw_ref)

        def rows(r, carry):
            r0 = pl.multiple_of(r * ROWS, ROWS)
            rs = pl.ds(r0, ROWS)
            for c in range(Dm // COLS):
                cs = pl.ds(c * COLS, COLS)
                cs2 = pl.ds(Dm + c * COLS, COLS)
                de = ext_d[pl.ds(r0, ROWS + HALO), cs]
                ge = ext_g[pl.ds(r0, ROWS + HALO), cs]
                dcur = de[0:ROWS, :]
                acc = jnp.zeros((ROWS, COLS), F32)
                for j in range(CONV_W):
                    off = CONV_W - 1 - j
                    acc = acc + de[off:off + ROWS, :] * w_ref[pl.ds(j, 1), cs]
                    goff = HALO - (CONV_W - 1) + j
                    prod = dcur * ge[goff:goff + ROWS, :]
                    dw_ref[j, :, cs] += jnp.sum(prod.reshape(ROWS // 8, 8, COLS), axis=0)
                a = ac_ref[rs, cs].astype(F32)
                sg = jax.nn.sigmoid(gc_ref[rs, cs].astype(F32))
                da = acc * sg
                dg = acc * a * sg * (1.0 - sg)
                du_ref[rs, cs] = da.astype(BF)
                du_ref[rs, cs2] = dg.astype(BF)
                db_ref[:, cs] += jnp.sum(da, axis=0, keepdims=True)
                db_ref[:, cs2] += jnp.sum(dg, axis=0, keepdims=True)
            return carry

        lax.fori_loop(0, tm // ROWS, rows, 0)

    return pl.pallas_call(
        body, name="conv_bwd", grid=(nt,),
        in_specs=[pl.BlockSpec((tm, Dm), lambda i: (i, 0)), pl.BlockSpec((tm, Dm), lambda i: (i, 1)),
                  pl.BlockSpec((HALO, Dm), lambda i: (jnp.maximum(i * hpt - 1, 0), 0)),
                  pl.BlockSpec((HALO, Dm), lambda i: (jnp.maximum(i * hpt - 1, 0), 1)),
                  pl.BlockSpec((tm, Dm), lambda i: (i, 0)),
                  pl.BlockSpec((HALO, Dm), lambda i: (jnp.minimum((i + 1) * hpt, last), 0)),
                  pl.BlockSpec((HALO, Dm), lambda i: (0, 0))],
        out_specs=[pl.BlockSpec((tm, 2 * Dm), lambda i: (i, 0)), pl.BlockSpec((1, 2 * Dm), lambda i: (0, 0)),
                   pl.BlockSpec((HALO, 8, Dm), lambda i: (0, 0, 0))],
        out_shape=[jax.ShapeDtypeStruct((T, 2 * Dm), BF), jax.ShapeDtypeStruct((1, 2 * Dm), F32),
                   jax.ShapeDtypeStruct((HALO, 8, Dm), F32)],
        scratch_shapes=[pltpu.VMEM((tm + HALO, Dm), F32), pltpu.VMEM((tm + HALO, Dm), F32)],
        compiler_params=_params(1),
    )(u, u, u, u, dcv, dcv, dw_pad)


def _bucket_table():
    q_loc = np.arange(BLOCK)[:, None]
    k_loc = np.arange(2 * BLOCK)[None, :]
    dist = q_loc + BLOCK - k_loc
    n = np.maximum(dist, 0)
    max_exact = REL_BUCKETS // 2
    large = max_exact + (np.log(np.maximum(n, 1).astype(np.float32) / max_exact)
                         / math.log(REL_MAX_DIST / max_exact) * (REL_BUCKETS - max_exact)).astype(np.int32)
    large = np.minimum(large, REL_BUCKETS - 1)
    bucket = np.where(n < max_exact, n, large).astype(np.int32)
    return jnp.asarray(np.where((dist >= 0) & (dist < BLOCK), bucket, -1).astype(np.int32))


def _bias_table(rel_bias, bucket):
    def body(rb_ref, bk_ref, o_ref):
        bk = bk_ref[...]
        for h in range(N_HEADS):
            acc = jnp.full((BLOCK, 2 * BLOCK), NEG_INF, F32)
            for b in range(REL_BUCKETS):
                acc = jnp.where(bk == b, rb_ref[b, h], acc)
            o_ref[h] = acc

    return pl.pallas_call(
        body, name="bias_table", out_shape=jax.ShapeDtypeStruct((N_HEADS, BLOCK, 2 * BLOCK), F32),
        in_specs=[pl.BlockSpec(memory_space=pltpu.SMEM), pl.BlockSpec(memory_space=pltpu.VMEM)],
        out_specs=pl.BlockSpec(memory_space=pltpu.VMEM),
    )(rel_bias, bucket)


def _bias_grad(dbias, bucket):
    def body(db_ref, bk_ref, o_ref):
        bk = bk_ref[...]
        for b in range(REL_BUCKETS):
            sel = bk == b
            for h in range(N_HEADS):
                o_ref[b, h] = jnp.sum(jnp.where(sel, db_ref[h], 0.0))

    return pl.pallas_call(
        body, name="bias_grad", out_shape=jax.ShapeDtypeStruct((REL_BUCKETS, N_HEADS), F32),
        in_specs=[pl.BlockSpec(memory_space=pltpu.VMEM), pl.BlockSpec(memory_space=pltpu.VMEM)],
        out_specs=pl.BlockSpec(memory_space=pltpu.SMEM),
    )(dbias, bucket)


def _band_probs(q, k, bias_h, sink, first):
    s = _dot(q, k, 1, 1) + bias_h
    s = jnp.where(first, NEG_INF, s)
    m = jnp.maximum(jnp.max(s, axis=-1, keepdims=True), sink)
    p = jnp.exp(s - m)
    ps = jnp.exp(sink - m)
    inv = 1.0 / (jnp.sum(p, axis=-1, keepdims=True) + ps)
    return p * inv, ps * inv


def _band(prev_ref, cur_ref, g):
    hs = pl.ds(g * HEAD_DIM, HEAD_DIM)
    return jnp.concatenate([prev_ref[:, hs], cur_ref[:, hs]], axis=0)


def _first_mask(n):
    col = lax.broadcasted_iota(jnp.int32, (BLOCK, 2 * BLOCK), 1)
    return jnp.logical_and(n == 0, col < BLOCK)


def _attn_fwd(qn, kn, vv, bias, sinks):
    T = qn.shape[0]
    nb = T // BLOCK

    def body(sk_ref, q_ref, kc_ref, kp_ref, vc_ref, vp_ref, b_ref, o_ref):
        first = _first_mask(pl.program_id(0))
        for g in range(N_KV):
            k = _band(kp_ref, kc_ref, g)
            v = _band(vp_ref, vc_ref, g)
            for hh in range(GROUP):
                h = g * GROUP + hh
                hs = pl.ds(h * HEAD_DIM, HEAD_DIM)
                pn, _ = _band_probs(q_ref[:, hs], k, b_ref[h], sk_ref[h], first)
                o_ref[:, hs] = _dot(pn.astype(BF), v, 1, 0).astype(BF)

    cur = lambda n: (n, 0)
    prev = lambda n: (jnp.maximum(n - 1, 0), 0)
    return pl.pallas_call(
        body, name="attn_fwd", grid=(nb,),
        in_specs=[pl.BlockSpec(memory_space=pltpu.SMEM), pl.BlockSpec((BLOCK, ATTN_DIM), cur),
                  pl.BlockSpec((BLOCK, KV_DIM), cur), pl.BlockSpec((BLOCK, KV_DIM), prev),
                  pl.BlockSpec((BLOCK, KV_DIM), cur), pl.BlockSpec((BLOCK, KV_DIM), prev),
                  pl.BlockSpec((N_HEADS, BLOCK, 2 * BLOCK), lambda n: (0, 0, 0))],
        out_specs=pl.BlockSpec((BLOCK, ATTN_DIM), cur), out_shape=jax.ShapeDtypeStruct((T, ATTN_DIM), BF),
        compiler_params=_params(1),
    )(sinks, qn, kn, kn, vv, vv, bias)


def _attn_bwd(qn, kn, vv, bias, sinks, do):
    T = qn.shape[0]
    nb = T // BLOCK
    scale = 1.0 / math.sqrt(HEAD_DIM)

    def body(sk_ref, q_ref, kc_ref, kp_ref, vc_ref, vp_ref, b_ref, do_ref,
             dq_ref, dk_ref, dv_ref, db_ref, dsk_ref, dk_full, dv_full, dk_carry, dv_carry):
        n = pl.program_id(0)

        @pl.when(n == 0)
        def _():
            db_ref[...] = jnp.zeros_like(db_ref)
            dk_carry[...] = jnp.zeros_like(dk_carry)
            dv_carry[...] = jnp.zeros_like(dv_carry)
            for h in range(N_HEADS):
                dsk_ref[h] = 0.0

        @pl.when(n < nb)
        def _():
            first = _first_mask(n)
            for g in range(N_KV):
                k = _band(kp_ref, kc_ref, g)
                v = _band(vp_ref, vc_ref, g)
                dk_g = jnp.zeros((2 * BLOCK, HEAD_DIM), F32)
                dv_g = jnp.zeros((2 * BLOCK, HEAD_DIM), F32)
                for hh in range(GROUP):
                    h = g * GROUP + hh
                    hs = pl.ds(h * HEAD_DIM, HEAD_DIM)
                    q = q_ref[:, hs]
                    doh = do_ref[:, hs]
                    pn, psink = _band_probs(q, k, b_ref[h], sk_ref[h], first)
                    dp = _dot(doh, v, 1, 1)
                    delta = jnp.sum(pn * dp, axis=-1, keepdims=True)
                    ds = pn * (dp - delta)
                    dsk_ref[h] += -jnp.sum(psink * delta)
                    db_ref[h] += ds
                    dsb = ds.astype(BF)
                    dq_ref[:, hs] = _dot(dsb, k, 1, 0) * scale
                    dk_g = dk_g + _dot(dsb, q, 0, 0)
                    dv_g = dv_g + _dot(pn.astype(BF), doh, 0, 0)
                gs = pl.ds(g * HEAD_DIM, HEAD_DIM)
                dk_full[:, gs] = dk_g
                dv_full[:, gs] = dv_g

        @pl.when(n == nb)
        def _():
            dk_full[...] = jnp.zeros_like(dk_full)
            dv_full[...] = jnp.zeros_like(dv_full)

        dk_ref[...] = dk_carry[...] + dk_full[pl.ds(0, BLOCK), :]
        dv_ref[...] = dv_carry[...] + dv_full[pl.ds(0, BLOCK), :]
        dk_carry[...] = dk_full[pl.ds(BLOCK, BLOCK), :]
        dv_carry[...] = dv_full[pl.ds(BLOCK, BLOCK), :]

    cur = lambda n: (jnp.minimum(n, nb - 1), 0)
    prev = lambda n: (jnp.maximum(jnp.minimum(n, nb - 1) - 1, 0), 0)
    out_kv = lambda n: (jnp.maximum(n - 1, 0), 0)
    return pl.pallas_call(
        body, name="attn_bwd", grid=(nb + 1,),
        in_specs=[pl.BlockSpec(memory_space=pltpu.SMEM), pl.BlockSpec((BLOCK, ATTN_DIM), cur),
                  pl.BlockSpec((BLOCK, KV_DIM), cur), pl.BlockSpec((BLOCK, KV_DIM), prev),
                  pl.BlockSpec((BLOCK, KV_DIM), cur), pl.BlockSpec((BLOCK, KV_DIM), prev),
                  pl.BlockSpec((N_HEADS, BLOCK, 2 * BLOCK), lambda n: (0, 0, 0)),
                  pl.BlockSpec((BLOCK, ATTN_DIM), cur)],
        out_specs=[pl.BlockSpec((BLOCK, ATTN_DIM), cur), pl.BlockSpec((BLOCK, KV_DIM), out_kv),
                   pl.BlockSpec((BLOCK, KV_DIM), out_kv),
                   pl.BlockSpec((N_HEADS, BLOCK, 2 * BLOCK), lambda n: (0, 0, 0)),
                   pl.BlockSpec(memory_space=pltpu.SMEM)],
        out_shape=[jax.ShapeDtypeStruct((T, ATTN_DIM), F32), jax.ShapeDtypeStruct((T, KV_DIM), F32),
                   jax.ShapeDtypeStruct((T, KV_DIM), F32),
                   jax.ShapeDtypeStruct((N_HEADS, BLOCK, 2 * BLOCK), F32), jax.ShapeDtypeStruct((N_HEADS,), F32)],
        scratch_shapes=[pltpu.VMEM((2 * BLOCK, KV_DIM), F32), pltpu.VMEM((2 * BLOCK, KV_DIM), F32),
                        pltpu.VMEM((BLOCK, KV_DIM), F32), pltpu.VMEM((BLOCK, KV_DIM), F32)],
        compiler_params=_params(1),
    )(sinks, qn, kn, kn, vv, vv, bias, do)


def _coords():
    return lax.axis_index("x"), lax.axis_index("y"), lax.axis_index("c")


def _gather8(name, v, with_sum):
    R = v.shape[0]

    def body(v_ref, all_ref, *rest):
        sum_ref = rest[0] if with_sum else None
        send_sems, recv_sems, local_sem = rest[-3:]
        x, y, c = _coords()
        me = 4 * x + 2 * y + c
        local = pltpu.make_async_copy(v_ref, all_ref.at[me], local_sem)
        local.start()
        sends = []
        for k in range(1, 8):
            peer = (x ^ (k >> 2), y ^ ((k >> 1) & 1), c ^ (k & 1))
            cp = pltpu.make_async_remote_copy(src_ref=v_ref, dst_ref=all_ref.at[me], send_sem=send_sems.at[k - 1],
                                              recv_sem=recv_sems.at[k - 1], device_id=peer, device_id_type=MESH)
            cp.start()
            sends.append(cp)
        for k in range(1, 8):
            peer = (x ^ (k >> 2), y ^ ((k >> 1) & 1), c ^ (k & 1))
            pltpu.make_async_remote_copy(src_ref=v_ref, dst_ref=all_ref.at[me ^ k], send_sem=send_sems.at[k - 1],
                                         recv_sem=recv_sems.at[k - 1], device_id=peer, device_id_type=MESH).wait_recv()
        for cp in sends:
            cp.wait_send()
        local.wait()
        if with_sum:
            tot = all_ref[0]
            for d in range(1, 8):
                tot = tot + all_ref[d]
            sum_ref[...] = tot

    out_shape = [jax.ShapeDtypeStruct((8, R, LANES), F32)]
    if with_sum:
        out_shape.append(jax.ShapeDtypeStruct((R, LANES), F32))
    vm = pl.BlockSpec(memory_space=pltpu.VMEM)
    return pl.pallas_call(
        body, name=name, out_shape=out_shape, in_specs=[vm], out_specs=[vm] * len(out_shape),
        scratch_shapes=[pltpu.SemaphoreType.DMA((7,)), pltpu.SemaphoreType.DMA((7,)), pltpu.SemaphoreType.DMA],
    )(v)


CHIP_FLIPS = ((1, 0), (0, 1), (1, 1))


HBM_SPEC = pl.BlockSpec(memory_space=pltpu.HBM)
SEM_SPEC = pl.BlockSpec(memory_space=pltpu.SEMAPHORE)
ANY_SPEC = pl.BlockSpec(memory_space=pl.ANY)
DATAFLOW = pltpu.SideEffectType.DATAFLOW_SIDE_EFFECTING


def _chip_copy(land, sems, idx, slot_src, slot_dst, peer):
    send_sems, recv_sems = sems
    return pltpu.make_async_remote_copy(src_ref=land.at[slot_src], dst_ref=land.at[slot_dst], send_sem=send_sems.at[idx],
                                        recv_sem=recv_sems.at[idx], device_id=peer, device_id_type=MESH)


def _gather_start(stacks, groups, after):
    n = len(stacks)
    ng = len(groups)
    after = tuple(after)

    def body(*refs):
        lands = refs[:n]
        first = n + len(after)
        sems = [(refs[first + 2 * g], refs[first + 2 * g + 1]) for g in range(ng)]
        token = refs[-1]
        x, y, c = _coords()
        s = 2 * x + y
        for g, members in enumerate(groups):
            for i, t in enumerate(members):
                for j, (fx, fy) in enumerate(CHIP_FLIPS):
                    _chip_copy(lands[t], sems[g], 3 * i + j, s, s, (x ^ fx, y ^ fy, c)).start()
        token[...] = jnp.zeros_like(token)

    out_shape = []
    for members in groups:
        out_shape += [pltpu.SemaphoreType.DMA((3 * len(members),))] * 2
    out_shape += [pltpu.HBM(w.shape, w.dtype) for w in stacks]
    out_shape.append(jax.ShapeDtypeStruct((8, 128), F32))
    res = pl.pallas_call(
        body, name="gather_start", out_shape=out_shape, in_specs=[HBM_SPEC] * n + [ANY_SPEC] * len(after),
        out_specs=[SEM_SPEC] * (2 * ng) + [HBM_SPEC] * n + [pl.BlockSpec(memory_space=pltpu.VMEM)],
        input_output_aliases={t: 2 * ng + t for t in range(n)},
        compiler_params=pltpu.CompilerParams(has_side_effects=DATAFLOW),
    )(*[pltpu.with_memory_space_constraint(w, pltpu.HBM) for w in stacks], *after)
    sems = [(res[2 * g], res[2 * g + 1]) for g in range(ng)]
    return sems, list(res[2 * ng:2 * ng + n]), res[-1]


def _gather_wait(name, stacks, sems, after):
    n = len(stacks)
    after = tuple(after)

    def body(*refs):
        lands = refs[:n]
        group_sems = (refs[n], refs[n + 1])
        x, y, c = _coords()
        s = 2 * x + y
        for i in range(n):
            for j, (fx, fy) in enumerate(CHIP_FLIPS):
                cp = _chip_copy(lands[i], group_sems, 3 * i + j, s, 2 * (x ^ fx) + (y ^ fy), (x ^ fx, y ^ fy, c))
                cp.wait_send()
                cp.wait_recv()

    return pl.pallas_call(
        body, name=name, out_shape=[pltpu.HBM(w.shape, w.dtype) for w in stacks],
        in_specs=[HBM_SPEC] * n + [SEM_SPEC, SEM_SPEC] + [ANY_SPEC] * len(after), out_specs=[HBM_SPEC] * n,
        input_output_aliases={t: t for t in range(n)},
        compiler_params=pltpu.CompilerParams(has_side_effects=DATAFLOW),
    )(*stacks, sems[0], sems[1], *after)


N_PEERS = 7


def _peer(x, y, c, k):
    return x ^ (k >> 2), y ^ ((k >> 1) & 1), c ^ (k & 1)


def _reduce_copy(grad, land, sems, idx, x, y, c, k):
    px, py, pc = _peer(x, y, c, k)
    rh = grad.shape[1] // 2
    return pltpu.make_async_remote_copy(src_ref=grad.at[2 * px + py, pl.ds(pc * rh, rh), :], dst_ref=land.at[k - 1],
                                        send_sem=sems[0].at[idx], recv_sem=sems[1].at[idx], device_id=(px, py, pc),
                                        device_id_type=MESH)


def _reduce_start(name, grads):
    n = len(grads)

    def body(*refs):
        src, lands, sems, token = refs[:n], refs[n:2 * n], (refs[2 * n], refs[2 * n + 1]), refs[-1]
        x, y, c = _coords()
        for t in range(n):
            for k in range(1, N_PEERS + 1):
                _reduce_copy(src[t], lands[t], sems, N_PEERS * t + k - 1, x, y, c, k).start()
        token[...] = jnp.zeros_like(token)

    lands = [lax.empty((N_PEERS, g.shape[1] // 2, g.shape[2]), g.dtype) for g in grads]
    out_shape = [pltpu.SemaphoreType.DMA((N_PEERS * n,))] * 2
    out_shape += [pltpu.HBM(a.shape, a.dtype) for a in list(grads) + lands]
    out_shape.append(jax.ShapeDtypeStruct((8, 128), F32))
    res = pl.pallas_call(
        body, name=name, out_shape=out_shape, in_specs=[HBM_SPEC] * (2 * n),
        out_specs=[SEM_SPEC] * 2 + [HBM_SPEC] * (2 * n) + [pl.BlockSpec(memory_space=pltpu.VMEM)],
        input_output_aliases={t: 2 + t for t in range(2 * n)},
        compiler_params=pltpu.CompilerParams(has_side_effects=DATAFLOW),
    )(*[pltpu.with_memory_space_constraint(a, pltpu.HBM) for a in list(grads) + lands])
    return (res[0], res[1]), list(res[2:2 + n]), list(res[2 + n:2 + 2 * n]), res[-1]


def _reduce_wait(name, grads, lands, sems, after):
    n = len(grads)
    after = tuple(after)

    def body(*refs):
        src, dst, group_sems = refs[:n], refs[n:2 * n], (refs[2 * n], refs[2 * n + 1])
        x, y, c = _coords()
        for t in range(n):
            for k in range(1, N_PEERS + 1):
                cp = _reduce_copy(src[t], dst[t], group_sems, N_PEERS * t + k - 1, x, y, c, k)
                cp.wait_send()
                cp.wait_recv()

    res = pl.pallas_call(
        body, name=name, out_shape=[pltpu.HBM(a.shape, a.dtype) for a in list(grads) + list(lands)],
        in_specs=[HBM_SPEC] * (2 * n) + [SEM_SPEC, SEM_SPEC] + [ANY_SPEC] * len(after), out_specs=[HBM_SPEC] * (2 * n),
        input_output_aliases={t: t for t in range(2 * n)},
        compiler_params=pltpu.CompilerParams(has_side_effects=DATAFLOW),
    )(*grads, *lands, sems[0], sems[1], *after)
    return list(res[:n]), list(res[n:])


def _join_halves(name, halves):
    n = len(halves)

    def body(*refs):
        src, dst = refs[:n], refs[n:2 * n]
        send_sems, recv_sems = refs[2 * n:]
        x, y, c = _coords()
        cps = []
        for t in range(n):
            cp = pltpu.make_async_remote_copy(src_ref=src[t], dst_ref=dst[t], send_sem=send_sems.at[t],
                                              recv_sem=recv_sems.at[t], device_id=(x, y, 1 - c), device_id_type=MESH)
            cp.start()
            cps.append(cp)
        for cp in cps:
            cp.wait()

    anyspec = pl.BlockSpec(memory_space=pl.ANY)
    return pl.pallas_call(
        body, name=name, out_shape=[jax.ShapeDtypeStruct(h.shape, h.dtype) for h in halves],
        in_specs=[anyspec] * n, out_specs=[anyspec] * n,
        scratch_shapes=[pltpu.SemaphoreType.DMA((n,)), pltpu.SemaphoreType.DMA((n,))],
    )(*halves)


def _row_block(rows):
    for rb in (512, 256, 128, 64, 32, 16):
        if rows % rb == 0:
            return rb
    raise ValueError(rows)


def _sum_devices(name, grad, land, place):
    S, R, C = grad.shape
    rh = R // 2
    rb = _row_block(rh)
    nbh = rh // rb

    def body(place_ref, g_ref, l_ref, o_ref):
        tot = g_ref[...].astype(F32)
        for k in range(N_PEERS):
            tot = tot + l_ref[k].astype(F32)
        o_ref[...] = tot

    return pl.pallas_call(
        body, name=name,
        grid_spec=pltpu.PrefetchScalarGridSpec(
            num_scalar_prefetch=1, grid=(nbh,),
            in_specs=[pl.BlockSpec((None, rb, C), lambda r, place: (place[0], place[1] * nbh + r, 0)),
                      pl.BlockSpec((N_PEERS, rb, C), lambda r, place: (0, r, 0))],
            out_specs=pl.BlockSpec((rb, C), lambda r, place: (r, 0))),
        out_shape=jax.ShapeDtypeStruct((rh, C), F32), compiler_params=_params(1),
    )(place, grad, land)


def _adamw_math(w, g, m, v):
    m2 = ADAM_B1 * m + (1.0 - ADAM_B1) * g
    v2 = ADAM_B2 * v + (1.0 - ADAM_B2) * (g * g)
    m_hat = m2 / (1.0 - ADAM_B1 ** ADAM_STEP)
    v_hat = v2 / (1.0 - ADAM_B2 ** ADAM_STEP)
    delta = -ADAM_LR * (m_hat / (jnp.sqrt(v_hat) + ADAM_EPS) + ADAM_WD * w)
    return delta, m2, v2


def _adamw(name, w, m, v, gs):
    L, R, C = w.shape
    Rh = R // 2
    rb = _row_block(Rh)
    nbh = Rh // rb
    assert len(gs) == L

    def body(core_ref, w_ref, m_ref, v_ref, *rest):
        g_refs, (go_ref, d_ref, m2_ref, v2_ref) = rest[:2 * L], rest[2 * L:]
        layer, half = pl.program_id(0), pl.program_id(1)
        mine = half == core_ref[0]
        g = jnp.where(mine, g_refs[0][...], g_refs[1][...])
        for t in range(1, L):
            g = jnp.where(layer == t, jnp.where(mine, g_refs[2 * t][...], g_refs[2 * t + 1][...]), g)
        delta, m2, v2 = _adamw_math(w_ref[...], g, m_ref[...], v_ref[...])
        go_ref[...] = g
        d_ref[...] = delta
        m2_ref[...] = m2
        v2_ref[...] = v2

    wspec = pl.BlockSpec((None, rb, C), lambda l, h, r, core: (l, h * nbh + r, 0))
    gspec = pl.BlockSpec((rb, C), lambda l, h, r, core: (r, 0))
    return pl.pallas_call(
        body, name=name,
        grid_spec=pltpu.PrefetchScalarGridSpec(num_scalar_prefetch=1, grid=(L, 2, nbh),
                                               in_specs=[wspec] * 3 + [gspec] * (2 * L), out_specs=[wspec] * 4),
        out_shape=[jax.ShapeDtypeStruct((L, R, C), F32)] * 4, compiler_params=_params(3),
    )(lax.axis_index("c").astype(jnp.int32).reshape(1), w, m, v, *[g for pair in gs for g in pair])


def _adamw_small(w, g, m, v):
    def body(w_ref, g_ref, m_ref, v_ref, d_ref, m2_ref, v2_ref):
        delta, m2, v2 = _adamw_math(w_ref[...], g_ref[...], m_ref[...], v_ref[...])
        d_ref[...] = delta
        m2_ref[...] = m2
        v2_ref[...] = v2

    return pl.pallas_call(body, name="adamw_small", out_shape=[jax.ShapeDtypeStruct(w.shape, F32)] * 3)(w, g, m, v)


def _pack(arrays):
    rows = []
    for a in arrays:
        a = a.astype(F32).reshape(-1, a.shape[-1])
        r, c = a.shape
        k = -(-c // LANES)
        a = jnp.pad(a, ((0, 0), (0, k * LANES - c))).reshape(r * k, LANES)
        rows.append(jnp.pad(a, ((0, -(r * k) % 8), (0, 0))))
    return jnp.concatenate(rows, axis=0)


def _unpack(buf, shapes):
    out, r0 = [], 0
    for shp in shapes:
        c = shp[-1]
        r = int(np.prod(shp)) // c
        k = -(-c // LANES)
        out.append(buf[r0:r0 + r * k].reshape(r, k * LANES)[:, :c].reshape(shp))
        r0 += r * k + (-(r * k) % 8)
    return out


def _mlp_fwd(tag, x, g, w_up_sm, w_down):
    h = _rms_fwd(f"mlp{tag}_norm", x, g)
    (up,) = _mm(f"mlp{tag}_up", h, w_up_sm, nt=False, b_sm=True, tm=1024, tn=1024, rows=256,
                ep_fn=lambda acc: (acc,), outs=(("tile", BF),))
    return h, up


RMS_BWD_OUTS = (("tile", F32), ("tile", BF), ("colsum", F32), ("colsum", F32))


def _mlp_bwd(tag, dy, dy_bf, x, g, h, up, w_up_sm, w_down, place):
    (dup,) = _mm(f"mlp{tag}_dup", dy_bf, w_down, nt=True, tm=1024, tn=1024, rows=256, ep_in=((up, "tile"),),
                 ep_fn=lambda acc, u: (acc * (2.0 * jnp.maximum(u.astype(F32), 0.0)),), outs=(("tile", BF),))
    dw_down = _mm_tn(f"mlp{tag}_dw_down", up, dy_bf, tm=1024, tn=1024, tk=2048, a_fn=_relu2)
    dw_up = _mm_tn(f"mlp{tag}_dw_up", h, dup, tm=1024, tn=1024, tk=2048, out_sm=N_SHARD)
    red = _Reduction(f"mlp{tag}", [dw_up, dw_down.reshape(N_SHARD, D_FF // N_SHARD, D_MODEL)], place)
    dx, dx_bf, dg, dx_sum = _mm(f"mlp{tag}_dx", dup, w_up_sm, nt=True, b_sm=True, tm=512, tn=1024, rows=256,
                                ep_in=((x, "tile"), (g, "row"), (dy, "tile")), ep_fn=_rms_bwd_ep, outs=RMS_BWD_OUTS,
                                deps=(red.token,))
    return dx, dx_bf, dg, dx_sum, red


class _Reduction:
    def __init__(self, tag, grads, place):
        self.tag, self.place = tag, place
        self.sems, self.grads, self.lands, self.token = _reduce_start(f"reduce_start_{tag}", grads)

    def finish(self, after):
        grads, lands = _reduce_wait(f"reduce_wait_{self.tag}", self.grads, self.lands, self.sems, after)
        halves = [_sum_devices(f"reduce_sum_{self.tag}{i}", g, l, self.place) for i, (g, l) in enumerate(zip(grads, lands))]
        return list(zip(halves, _join_halves(f"join_halves_{self.tag}", halves)))


def kernel(x, conv_norm_g, conv_w_in, conv_b_in, conv_dw, conv_dw_b, conv_ln_g, conv_ln_b, conv_w_out, conv_b_out, attn_norm_g, w_qkv, b_qkv, q_norm_g, k_norm_g, sinks, w_o, b_o, rel_bias, mlp_norm_g, w_up, w_down, loss_target, m_conv_norm_g, m_conv_w_in, m_conv_b_in, m_conv_dw, m_conv_dw_b, m_conv_ln_g, m_conv_ln_b, m_conv_w_out, m_conv_b_out, m_attn_norm_g, m_w_qkv, m_b_qkv, m_q_norm_g, m_k_norm_g, m_sinks, m_w_o, m_b_o, m_rel_bias, m_mlp_norm_g, m_w_up, m_w_down, v_conv_norm_g, v_conv_w_in, v_conv_b_in, v_conv_dw, v_conv_dw_b, v_conv_ln_g, v_conv_ln_b, v_conv_w_out, v_conv_b_out, v_attn_norm_g, v_w_qkv, v_b_qkv, v_q_norm_g, v_k_norm_g, v_sinks, v_w_o, v_b_o, v_rel_bias, v_mlp_norm_g, v_w_up, v_w_down):
    Dm = D_MODEL
    x2d = x[0]
    tgt = loss_target[0]
    T = x2d.shape[0]
    shard = 2 * lax.axis_index("x") + lax.axis_index("y")

    sharded_small = [conv_dw[0], attn_norm_g, b_qkv, b_o]
    (gathered,) = _gather8("gather_small_weights", _pack(sharded_small), with_sum=False)
    chips = [_unpack(gathered[2 * s], [a.shape for a in sharded_small]) for s in range(N_SHARD)]
    dw_f, attn_norm_f, b_qkv_f, b_o_f = (jnp.concatenate([chips[s][t] for s in range(N_SHARD)], axis=-1)
                                         for t in range(len(sharded_small)))
    dw_pad = jnp.pad(dw_f, ((0, HALO - CONV_W), (0, 0)))

    big = [conv_w_in[0], conv_w_out[0], w_qkv[0], w_o[0], w_up[0], w_up[1], w_down[0], w_down[1]]
    stacks = [lax.dynamic_update_slice(jnp.zeros((N_SHARD,) + w.shape, BF), w.astype(BF)[None], (shard, 0, 0))
              for w in big]
    groups = ((0, 1), (4, 6), (2, 3), (5, 7))
    gather_sems, stacks, gather_token = _gather_start(stacks, groups, after=(gathered,))

    def gathered_group(g, name, after):
        return _gather_wait(name, [stacks[t] for t in groups[g]], gather_sems[g], after)

    bucket = _bucket_table()
    bias = _bias_table(rel_bias, bucket)

    h0 = _rms_fwd("conv_norm", x2d, conv_norm_g, deps=(gather_token,))
    w_in_sm, g_out = gathered_group(0, "gather_wait_conv", (h0, dw_pad, bias))
    w_out_f = g_out.reshape(Dm, Dm)
    (u,) = _mm("conv_in", h0, w_in_sm, nt=False, b_sm=True, tm=1024, tn=512, rows=256, ep_in=((conv_b_in, "row"),),
               ep_fn=lambda acc, b: (acc + b,), outs=(("tile", BF),))
    cv, s_act = _conv_fwd(u, dw_pad, conv_dw_b, conv_ln_g, conv_ln_b)
    (x1,) = _mm("conv_out", s_act, w_out_f, nt=False, tm=1024, tn=1024, rows=256,
                ep_in=((conv_b_out, "row"), (x2d, "tile")), ep_fn=lambda acc, b, r: (acc + b + r,),
                outs=(("tile", F32),))

    g_up0, g_down0 = gathered_group(1, "gather_wait_mlp0", (x1,))
    w_up_sm = [g_up0, None]
    w_down_f = [g_down0.reshape(D_FF, Dm), None]
    h1, up0 = _mlp_fwd(0, x1, mlp_norm_g[0:1], w_up_sm[0], w_down_f[0])
    (x2,) = _mm("mlp0_down", up0, w_down_f[0], nt=False, tm=512, tn=1024, rows=256, a_fn=_relu2,
                ep_in=((x1, "tile"),), ep_fn=lambda acc, r: (acc + r,), outs=(("tile", F32),))

    g_qkv, g_o = gathered_group(2, "gather_wait_attn", (x2,))
    w_qkv_f = jnp.transpose(g_qkv, (1, 0, 2)).reshape(Dm, QKV_DIM)
    w_o_f = g_o.reshape(ATTN_DIM, Dm)
    h2 = _rms_fwd("attn_norm", x2, attn_norm_f)
    (qkv,) = _mm("attn_qkv", h2, w_qkv_f, nt=False, tm=1024, tn=QKV_DIM, rows=256, ep_in=((b_qkv_f, "row"),),
                 ep_fn=lambda acc, b: (acc + b,), outs=(("tile", F32),))
    qg_t = jnp.tile(q_norm_g, (1, N_HEADS))
    kg_t = jnp.tile(k_norm_g, (1, N_KV))
    qn, kn, vv = _qk_norm_fwd(qkv, qg_t, kg_t)
    sinks1 = sinks[0]
    att = _attn_fwd(qn, kn, vv, bias, sinks1)
    (x3,) = _mm("attn_out", att, w_o_f, nt=False, tm=1024, tn=1024, rows=256,
                ep_in=((b_o_f, "row"), (x2, "tile")), ep_fn=lambda acc, b, r: (acc + b + r,), outs=(("tile", F32),))

    g_up1, g_down1 = gathered_group(3, "gather_wait_mlp1", (x3,))
    w_up_sm[1] = g_up1
    w_down_f[1] = g_down1.reshape(D_FF, Dm)
    h3, up1 = _mlp_fwd(1, x3, mlp_norm_g[1:2], w_up_sm[1], w_down_f[1])

    def loss_ep(acc, r, t):
        diff = acc + r - t
        dy = diff * (1.0 / Dm)
        return dy, dy, jnp.sum(diff * diff, axis=0, keepdims=True)

    dy, dy_bf, sq = _mm("mlp1_down_loss", up1, w_down_f[1], nt=False, tm=512, tn=1024, rows=256, a_fn=_relu2,
                        ep_in=((x3, "tile"), (tgt, "tile")), ep_fn=loss_ep,
                        outs=(("tile", F32), ("tile", BF), ("colsum", F32)))
    loss = lax.psum(0.5 * jnp.sum(sq) * (1.0 / Dm), ("x", "y", "c"))

    place = jnp.stack([shard, lax.axis_index("c")]).astype(jnp.int32)
    dx3, dx3_bf, dg_mlp1, db_o, red_mlp1 = _mlp_bwd(1, dy, dy_bf, x3, mlp_norm_g[1:2], h3, up1, w_up_sm[1],
                                                    w_down_f[1], place)

    ident = lambda acc: (acc,)
    (datt,) = _mm("attn_dout", dx3_bf, w_o_f, nt=True, tm=1024, tn=1024, rows=256, ep_fn=ident, outs=(("tile", BF),))
    dw_o = _mm_tn("attn_dw_o", att, dx3_bf, tm=1024, tn=1024, tk=2048)
    dqn, dkn, dvv, dbias, dsinks = _attn_bwd(qn, kn, vv, bias, sinks1, datt)
    (r_up1, r_down1) = red_mlp1.finish((dqn,))
    drel = _bias_grad(dbias, bucket)
    dqkv, db_qkv, dqg_t, dkg_t = _qk_norm_bwd(qkv, dqn, dkn, dvv, qg_t, kg_t)
    dw_qkv = _mm_tn("attn_dw_qkv", h2, dqkv, tm=1024, tn=QKV_DIM, tk=2048)
    red_attn = _Reduction("attn", [jnp.transpose(dw_qkv.reshape(Dm, N_SHARD, QKV_DIM // N_SHARD), (1, 0, 2)),
                                   dw_o.reshape(N_SHARD, ATTN_DIM // N_SHARD, Dm)], place)
    dx2, dx2_bf, dg_attn, _ = _mm("attn_dx", dqkv, w_qkv_f, nt=True, tm=512, tn=1024, rows=256,
                                  ep_in=((x2, "tile"), (attn_norm_f, "row"), (dx3, "tile")), ep_fn=_rms_bwd_ep,
                                  outs=RMS_BWD_OUTS, deps=(red_attn.token,))

    dx1, dx1_bf, dg_mlp0, db_out, red_mlp0 = _mlp_bwd(0, dx2, dx2_bf, x1, mlp_norm_g[0:1], h1, up0, w_up_sm[0],
                                                      w_down_f[0], place)
    (r_qkv, r_o) = red_attn.finish((dx1,))

    dcv, dln_g, dln_b, ddw_b = _mm("conv_ds", dx1_bf, w_out_f, nt=True, tm=512, tn=1024, rows=256,
                                   ep_in=((cv, "tile"), (conv_ln_g, "row"), (conv_ln_b, "row")),
                                   ep_fn=_ln_silu_bwd_ep,
                                   outs=(("tile", F32), ("colsum", F32), ("colsum", F32), ("colsum", F32)))
    dw_out = _mm_tn("conv_dw_out", s_act, dx1_bf, tm=1024, tn=1024, tk=2048)
    du, db_in, ddw8 = _conv_bwd(u, dcv, dw_pad)
    (r_up0, r_down0) = red_mlp0.finish((du,))
    dw_in = _mm_tn("conv_dw_in", h0, du, tm=1024, tn=512, tk=2048, out_sm=N_SHARD)
    red_conv = _Reduction("conv", [dw_in, dw_out.reshape(N_SHARD, Dm // N_SHARD, Dm)], place)
    def first_layer_ep(*args):
        tot, _, dg, _ = _rms_bwd_ep(*args)
        return tot, dg

    gx, dg_conv = _mm("conv_dx", du, w_in_sm, nt=True, b_sm=True, tm=512, tn=1024, rows=256,
                      ep_in=((x2d, "tile"), (conv_norm_g, "row"), (dx1, "tile")), ep_fn=first_layer_ep,
                      outs=(("tile", F32), ("colsum", F32)), deps=(red_conv.token,))
    (r_in, r_out) = red_conv.finish((gx,))

    big_out = {}
    for nm, w, m, v, gs in (("conv_w_in", conv_w_in, m_conv_w_in, v_conv_w_in, (r_in,)),
                            ("conv_w_out", conv_w_out, m_conv_w_out, v_conv_w_out, (r_out,)),
                            ("w_qkv", w_qkv, m_w_qkv, v_w_qkv, (r_qkv,)),
                            ("w_o", w_o, m_w_o, v_w_o, (r_o,)),
                            ("w_up", w_up, m_w_up, v_w_up, (r_up0, r_up1)),
                            ("w_down", w_down, m_w_down, v_w_down, (r_down0, r_down1))):
        big_out[nm] = _adamw(f"adamw_{nm}", w, m, v, gs)

    dqg = dqg_t.reshape(N_HEADS, HEAD_DIM).sum(axis=0, keepdims=True)
    dkg = dkg_t.reshape(N_KV, HEAD_DIM).sum(axis=0, keepdims=True)
    small_full = [dg_conv, db_in, ddw8.sum(axis=1)[:CONV_W], ddw_b, dln_g, dln_b, db_out, dg_attn, db_qkv, dqg, dkg,
                  dsinks[None, :], db_o, drel, jnp.pad(dg_mlp0, ((0, 1), (0, 0))) + jnp.pad(dg_mlp1, ((1, 0), (0, 0)))]
    _, small_sum = _gather8("reduce_small_grads", _pack(small_full), with_sum=True)
    (r_norm, r_b_in, r_dw, r_dw_b, r_ln_g, r_ln_b, r_b_out, r_attn_norm, r_b_qkv, r_qg, r_kg, r_sinks, r_b_o, r_rel,
     r_mlp_norm) = _unpack(small_sum, [a.shape for a in small_full])

    def cols(a, width):
        return lax.dynamic_slice_in_dim(a, shard * width, width, axis=a.ndim - 1)

    small_names = ["conv_norm_g", "conv_b_in", "conv_dw", "conv_dw_b", "conv_ln_g", "conv_ln_b", "conv_b_out",
                   "attn_norm_g", "b_qkv", "q_norm_g", "k_norm_g", "sinks", "b_o", "rel_bias", "mlp_norm_g"]
    small_g = [r_norm, r_b_in, cols(r_dw, Dm // N_SHARD)[None], r_dw_b, r_ln_g, r_ln_b, r_b_out,
               cols(r_attn_norm, Dm // N_SHARD), cols(r_b_qkv, QKV_DIM // N_SHARD), r_qg, r_kg, r_sinks,
               cols(r_b_o, Dm // N_SHARD), r_rel, r_mlp_norm]
    small_w = [conv_norm_g, conv_b_in, conv_dw, conv_dw_b, conv_ln_g, conv_ln_b, conv_b_out, attn_norm_g, b_qkv,
               q_norm_g, k_norm_g, sinks, b_o, rel_bias, mlp_norm_g]
    small_m = [m_conv_norm_g, m_conv_b_in, m_conv_dw, m_conv_dw_b, m_conv_ln_g, m_conv_ln_b, m_conv_b_out,
               m_attn_norm_g, m_b_qkv, m_q_norm_g, m_k_norm_g, m_sinks, m_b_o, m_rel_bias, m_mlp_norm_g]
    small_v = [v_conv_norm_g, v_conv_b_in, v_conv_dw, v_conv_dw_b, v_conv_ln_g, v_conv_ln_b, v_conv_b_out,
               v_attn_norm_g, v_b_qkv, v_q_norm_g, v_k_norm_g, v_sinks, v_b_o, v_rel_bias, v_mlp_norm_g]
    flat2 = lambda a: a.reshape(-1, a.shape[-1])
    shapes2 = [flat2(w).shape for w in small_w]
    pk = lambda arrs: _pack([flat2(a) for a in arrs])
    packed_g = pk(small_g)
    d_s, m_s, v_s = _adamw_small(pk(small_w), packed_g, pk(small_m), pk(small_v))
    small_out = {}
    for nm, w, g, d, m2, v2 in zip(small_names, small_w, _unpack(packed_g, shapes2), _unpack(d_s, shapes2),
                                   _unpack(m_s, shapes2), _unpack(v_s, shapes2)):
        small_out[nm] = tuple(a.reshape(w.shape) for a in (g, d, m2, v2))

    order = ["conv_norm_g", "conv_w_in", "conv_b_in", "conv_dw", "conv_dw_b", "conv_ln_g", "conv_ln_b", "conv_w_out",
             "conv_b_out", "attn_norm_g", "w_qkv", "b_qkv", "q_norm_g", "k_norm_g", "sinks", "w_o", "b_o", "rel_bias",
             "mlp_norm_g", "w_up", "w_down"]
    res = {**small_out, **big_out}
    outs = [loss, gx[None]]
    for part in range(4):
        outs += [res[nm][part] for nm in order]
    return tuple(outs)
```

```python
import math

import numpy as np
import jax
import jax.numpy as jnp
from jax import lax
from jax.experimental import pallas as pl
from jax.experimental.pallas import tpu as pltpu

F32 = jnp.float32
BF = jnp.bfloat16
MESH = pl.DeviceIdType.MESH

D_MODEL = 1024
D_FF = 4096
N_HEADS = 16
N_KV = 2
GROUP = N_HEADS // N_KV
HEAD_DIM = 64
ATTN_DIM = N_HEADS * HEAD_DIM
KV_DIM = N_KV * HEAD_DIM
QKV_DIM = ATTN_DIM + 2 * KV_DIM
BLOCK = 128
CONV_W = 31
HALO = 32
REL_BUCKETS = 32
REL_MAX_DIST = 128
NORM_EPS = 1e-6
NEG_INF = -1e30
N_SHARD = 4
LANES = 1024

ADAM_LR = 0.001
ADAM_B1 = 0.9
ADAM_B2 = 0.999
ADAM_EPS = 1e-08
ADAM_WD = 0.01
ADAM_STEP = 10

VMEM_LIMIT = 56 * 1024 * 1024


def _params(n_axes):
    return pltpu.CompilerParams(dimension_semantics=("arbitrary",) * n_axes, vmem_limit_bytes=VMEM_LIMIT)


def _dot(a, b, ca, cb):
    return lax.dot_general(a, b, (((ca,), (cb,)), ((), ())), preferred_element_type=F32)


def _mm(name, a, b, *, nt, tm, tn, ep_fn, outs, a_fn=None, b_sm=False, ep_in=(), deps=(), rows=None):
    M, K = a.shape
    rows = tm if rows is None else rows
    if b_sm:
        S, ks = b.shape[0], b.shape[2]
        N, per = (b.shape[1], None) if nt else (S * b.shape[2], b.shape[2] // tn)
        assert (S * ks == K) if nt else (b.shape[1] == K)
    else:
        N = b.shape[0] if nt else b.shape[1]
        assert (b.shape[1] if nt else b.shape[0]) == K
    assert M % tm == 0 and N % tn == 0 and tm % rows == 0
    ne, no, nd = len(ep_in), len(outs), len(deps)

    def body(a_ref, b_ref, *rest):
        ep_refs, out_refs = rest[:ne], rest[ne + nd:ne + nd + no]
        i = pl.program_id(1)
        sums = [None] * no
        for r in range(tm // rows):
            rs = pl.ds(r * rows, rows)

            def lhs(cols):
                av = a_ref[rs, cols]
                return (av if a_fn is None else a_fn(av)).astype(BF)

            if b_sm and nt:
                acc = None
                for s in range(S):
                    part = _dot(lhs(pl.ds(s * ks, ks)), b_ref[s].astype(BF), 1, 1)
                    acc = part if acc is None else acc + part
            else:
                acc = _dot(lhs(slice(None)), b_ref[...].astype(BF), 1, 1 if nt else 0)
            ep_vals = [ref[rs, :] if kind == "tile" else ref[...] for ref, (_, kind) in zip(ep_refs, ep_in)]
            vals = ep_fn(acc, *ep_vals)
            for o, ((kind, dt), ref, val) in enumerate(zip(outs, out_refs, vals)):
                if kind == "tile":
                    ref[rs, :] = val.astype(dt)
                else:
                    sums[o] = val if sums[o] is None else sums[o] + val
        for (kind, dt), ref, val in zip(outs, out_refs, sums):
            if kind == "colsum":
                @pl.when(i == 0)
                def _():
                    ref[...] = val

                @pl.when(i > 0)
                def _():
                    ref[...] += val

    if b_sm and nt:
        b_spec = pl.BlockSpec((S, tn, ks), lambda j, i: (0, j, 0))
    elif b_sm:
        b_spec = pl.BlockSpec((None, K, tn), lambda j, i: (j // per, 0, j % per))
    elif nt:
        b_spec = pl.BlockSpec((tn, K), lambda j, i: (j, 0))
    else:
        b_spec = pl.BlockSpec((K, tn), lambda j, i: (0, j))
    in_specs = [pl.BlockSpec((tm, K), lambda j, i: (i, 0)), b_spec]
    for arr, kind in ep_in:
        if kind == "tile":
            assert arr.shape == (M, N)
            in_specs.append(pl.BlockSpec((tm, tn), lambda j, i: (i, j)))
        else:
            assert arr.shape == (1, N)
            in_specs.append(pl.BlockSpec((1, tn), lambda j, i: (0, j)))
    in_specs += [pl.BlockSpec(memory_space=pl.ANY)] * nd
    out_shape, out_specs = [], []
    for kind, dt in outs:
        if kind == "tile":
            out_shape.append(jax.ShapeDtypeStruct((M, N), dt))
            out_specs.append(pl.BlockSpec((tm, tn), lambda j, i: (i, j)))
        else:
            out_shape.append(jax.ShapeDtypeStruct((1, N), F32))
            out_specs.append(pl.BlockSpec((1, tn), lambda j, i: (0, j)))
    return pl.pallas_call(
        body, name=name, grid=(N // tn, M // tm), in_specs=in_specs, out_specs=out_specs, out_shape=out_shape,
        compiler_params=_params(2),
    )(a, b, *[arr for arr, _ in ep_in], *deps)


def _mm_tn(name, a, b, *, tm, tn, tk, a_fn=None, out_sm=None):
    T, Ka = a.shape
    N = b.shape[1]
    assert b.shape[0] == T and T % tk == 0 and Ka % tm == 0 and N % tn == 0
    nk = T // tk

    def body(a_ref, b_ref, o_ref, acc_ref):
        k = pl.program_id(2)

        @pl.when(k == 0)
        def _():
            acc_ref[...] = jnp.zeros_like(acc_ref)

        av = a_ref[...]
        if a_fn is not None:
            av = a_fn(av)
        acc_ref[...] += _dot(av.astype(BF), b_ref[...].astype(BF), 0, 0)

        @pl.when(k == nk - 1)
        def _():
            o_ref[...] = acc_ref[...].astype(BF)

    if out_sm is None:
        out_shape = jax.ShapeDtypeStruct((Ka, N), BF)
        out_spec = pl.BlockSpec((tm, tn), lambda i, j, k: (i, j))
    else:
        per = (N // out_sm) // tn
        assert per * tn * out_sm == N
        out_shape = jax.ShapeDtypeStruct((out_sm, Ka, N // out_sm), BF)
        out_spec = pl.BlockSpec((None, tm, tn), lambda i, j, k: (j // per, i, j % per))
    return pl.pallas_call(
        body, name=name, grid=(Ka // tm, N // tn, nk),
        in_specs=[pl.BlockSpec((tk, tm), lambda i, j, k: (k, i)), pl.BlockSpec((tk, tn), lambda i, j, k: (k, j))],
        out_specs=out_spec, out_shape=out_shape, scratch_shapes=[pltpu.VMEM((tm, tn), F32)],
        compiler_params=_params(3),
    )(a, b)


def _relu2(v):
    r = jnp.maximum(v.astype(F32), 0.0)
    return r * r


def _rms_bwd_ep(dh, x, g, dres):
    rstd = lax.rsqrt(jnp.mean(x * x, axis=-1, keepdims=True) + NORM_EPS)
    xh = x * rstd
    dxh = dh * g
    dx = rstd * (dxh - xh * jnp.mean(dxh * xh, axis=-1, keepdims=True))
    tot = dres + dx
    return tot, tot, jnp.sum(dh * xh, axis=0, keepdims=True), jnp.sum(tot, axis=0, keepdims=True)


def _rms_fwd(name, x, g, tm=512, deps=()):
    T, Dm = x.shape

    def body(x_ref, g_ref, *rest):
        o_ref = rest[-1]
        xv = x_ref[...]
        rstd = lax.rsqrt(jnp.mean(xv * xv, axis=-1, keepdims=True) + NORM_EPS)
        o_ref[...] = (xv * rstd * g_ref[...]).astype(BF)

    return pl.pallas_call(
        body, name=name, grid=(T // tm,),
        in_specs=[pl.BlockSpec((tm, Dm), lambda i: (i, 0)), pl.BlockSpec((1, Dm), lambda i: (0, 0))]
        + [pl.BlockSpec(memory_space=pl.ANY)] * len(deps),
        out_specs=pl.BlockSpec((tm, Dm), lambda i: (i, 0)), out_shape=jax.ShapeDtypeStruct((T, Dm), BF),
        compiler_params=_params(1),
    )(x, g, *deps)


def _head_sum(v, ones_bd):
    hi = v.astype(BF)
    lo = (v - hi.astype(F32)).astype(BF)
    return _dot(hi, ones_bd, 1, 0) + _dot(lo, ones_bd, 1, 0)


def _block_ones(n):
    idx = np.arange(n) // HEAD_DIM
    return jnp.asarray((idx[:, None] == idx[None, :]).astype(np.float32), dtype=BF)


def _qk_norm_fwd(qkv, qg_t, kg_t, tm=256):
    T = qkv.shape[0]
    scale = 1.0 / math.sqrt(HEAD_DIM)

    def body(x_ref, qg_ref, kg_ref, bq_ref, bk_ref, q_ref, k_ref, v_ref):
        q = x_ref[:, pl.ds(0, ATTN_DIM)]
        rq = lax.rsqrt(_head_sum(q * q, bq_ref[...]) * (1.0 / HEAD_DIM) + NORM_EPS)
        q_ref[...] = (q * rq * qg_ref[...] * scale).astype(BF)
        k = x_ref[:, pl.ds(ATTN_DIM, KV_DIM)]
        rk = lax.rsqrt(_head_sum(k * k, bk_ref[...]) * (1.0 / HEAD_DIM) + NORM_EPS)
        k_ref[...] = (k * rk * kg_ref[...]).astype(BF)
        v_ref[...] = x_ref[:, pl.ds(ATTN_DIM + KV_DIM, KV_DIM)].astype(BF)

    full = lambda shape: pl.BlockSpec(shape, lambda i: (0, 0))
    return pl.pallas_call(
        body, name="qk_norm_fwd", grid=(T // tm,),
        in_specs=[pl.BlockSpec((tm, QKV_DIM), lambda i: (i, 0)), full((1, ATTN_DIM)), full((1, KV_DIM)),
                  full((ATTN_DIM, ATTN_DIM)), full((KV_DIM, KV_DIM))],
        out_specs=[pl.BlockSpec((tm, ATTN_DIM), lambda i: (i, 0)), pl.BlockSpec((tm, KV_DIM), lambda i: (i, 0)),
                   pl.BlockSpec((tm, KV_DIM), lambda i: (i, 0))],
        out_shape=[jax.ShapeDtypeStruct((T, ATTN_DIM), BF), jax.ShapeDtypeStruct((T, KV_DIM), BF),
                   jax.ShapeDtypeStruct((T, KV_DIM), BF)],
        compiler_params=_params(1),
    )(qkv, qg_t, kg_t, _block_ones(ATTN_DIM), _block_ones(KV_DIM))


def _qk_norm_bwd(qkv, dqn, dkn, dv, qg_t, kg_t, tm=256):
    T = qkv.shape[0]

    def body(x_ref, dq_ref, dk_ref, dv_ref, qg_ref, kg_ref, bq_ref, bk_ref, o_ref, db_ref, dqg_ref, dkg_ref):
        i = pl.program_id(0)

        def one(x, dy, g, ones_bd):
            r = lax.rsqrt(_head_sum(x * x, ones_bd) * (1.0 / HEAD_DIM) + NORM_EPS)
            xh = x * r
            dxh = dy * g
            dx = r * (dxh - xh * (_head_sum(dxh * xh, ones_bd) * (1.0 / HEAD_DIM)))
            return dx, jnp.sum(dy * xh, axis=0, keepdims=True)

        dq, dqg = one(x_ref[:, pl.ds(0, ATTN_DIM)], dq_ref[...], qg_ref[...], bq_ref[...])
        dk, dkg = one(x_ref[:, pl.ds(ATTN_DIM, KV_DIM)], dk_ref[...], kg_ref[...], bk_ref[...])
        dvv = dv_ref[...]
        o_ref[:, pl.ds(0, ATTN_DIM)] = dq.astype(BF)
        o_ref[:, pl.ds(ATTN_DIM, KV_DIM)] = dk.astype(BF)
        o_ref[:, pl.ds(ATTN_DIM + KV_DIM, KV_DIM)] = dvv.astype(BF)
        sq, sk, sv = (jnp.sum(t, axis=0, keepdims=True) for t in (dq, dk, dvv))

        @pl.when(i == 0)
        def _():
            db_ref[:, pl.ds(0, ATTN_DIM)] = sq
            db_ref[:, pl.ds(ATTN_DIM, KV_DIM)] = sk
            db_ref[:, pl.ds(ATTN_DIM + KV_DIM, KV_DIM)] = sv
            dqg_ref[...] = dqg
            dkg_ref[...] = dkg

        @pl.when(i > 0)
        def _():
            db_ref[:, pl.ds(0, ATTN_DIM)] += sq
            db_ref[:, pl.ds(ATTN_DIM, KV_DIM)] += sk
            db_ref[:, pl.ds(ATTN_DIM + KV_DIM, KV_DIM)] += sv
            dqg_ref[...] += dqg
            dkg_ref[...] += dkg

    full = lambda shape: pl.BlockSpec(shape, lambda i: (0, 0))
    row = lambda n: pl.BlockSpec((tm, n), lambda i: (i, 0))
    return pl.pallas_call(
        body, name="qk_norm_bwd", grid=(T // tm,),
        in_specs=[row(QKV_DIM), row(ATTN_DIM), row(KV_DIM), row(KV_DIM), full((1, ATTN_DIM)), full((1, KV_DIM)),
                  full((ATTN_DIM, ATTN_DIM)), full((KV_DIM, KV_DIM))],
        out_specs=[row(QKV_DIM), full((1, QKV_DIM)), full((1, ATTN_DIM)), full((1, KV_DIM))],
        out_shape=[jax.ShapeDtypeStruct((T, QKV_DIM), BF), jax.ShapeDtypeStruct((1, QKV_DIM), F32),
                   jax.ShapeDtypeStruct((1, ATTN_DIM), F32), jax.ShapeDtypeStruct((1, KV_DIM), F32)],
        compiler_params=_params(1),
    )(qkv, dqn, dkn, dv, qg_t, kg_t, _block_ones(ATTN_DIM), _block_ones(KV_DIM))


ROWS = 64
COLS = 128


def _glu(a, g):
    return a.astype(F32) * jax.nn.sigmoid(g.astype(F32))


def _conv_fwd(u, dw_pad, dw_b, ln_g, ln_b, tm=256):
    T = u.shape[0]
    Dm = D_MODEL
    hpt = tm // HALO

    def body(ac_ref, gc_ref, ap_ref, gp_ref, w_ref, wb_ref, lg_ref, lb_ref, cv_ref, s_ref, ext):
        i = pl.program_id(0)
        ext[pl.ds(0, HALO), :] = jnp.where(i > 0, _glu(ap_ref[...], gp_ref[...]), 0.0)
        ext[pl.ds(HALO, tm), :] = _glu(ac_ref[...], gc_ref[...])

        def rows(r, carry):
            r0 = pl.multiple_of(r * ROWS, ROWS)
            for c in range(Dm // COLS):
                cs = pl.ds(c * COLS, COLS)
                xe = ext[pl.ds(r0, ROWS + HALO), cs]
                acc = jnp.zeros((ROWS, COLS), F32)
                for j in range(CONV_W):
                    off = HALO - (CONV_W - 1) + j
                    acc = acc + xe[off:off + ROWS, :] * w_ref[pl.ds(j, 1), cs]
                cv_ref[pl.ds(r0, ROWS), cs] = acc + wb_ref[:, cs]
            return carry

        lax.fori_loop(0, tm // ROWS, rows, 0)
        cv = cv_ref[...]
        xc = cv - jnp.mean(cv, axis=-1, keepdims=True)
        y = xc * lax.rsqrt(jnp.mean(xc * xc, axis=-1, keepdims=True) + NORM_EPS) * lg_ref[...] + lb_ref[...]
        s_ref[...] = (y * jax.nn.sigmoid(y)).astype(BF)

    full = lambda shape: pl.BlockSpec(shape, lambda i: (0, 0))
    return pl.pallas_call(
        body, name="conv_fwd", grid=(T // tm,),
        in_specs=[pl.BlockSpec((tm, Dm), lambda i: (i, 0)), pl.BlockSpec((tm, Dm), lambda i: (i, 1)),
                  pl.BlockSpec((HALO, Dm), lambda i: (jnp.maximum(i * hpt - 1, 0), 0)),
                  pl.BlockSpec((HALO, Dm), lambda i: (jnp.maximum(i * hpt - 1, 0), 1)),
                  full((HALO, Dm)), full((1, Dm)), full((1, Dm)), full((1, Dm))],
        out_specs=[pl.BlockSpec((tm, Dm), lambda i: (i, 0)), pl.BlockSpec((tm, Dm), lambda i: (i, 0))],
        out_shape=[jax.ShapeDtypeStruct((T, Dm), F32), jax.ShapeDtypeStruct((T, Dm), BF)],
        scratch_shapes=[pltpu.VMEM((tm + HALO, Dm), F32)],
        compiler_params=_params(1),
    )(u, u, u, u, dw_pad, dw_b, ln_g, ln_b)


def _ln_silu_bwd_ep(ds, cv, lg, lb):
    xc = cv - jnp.mean(cv, axis=-1, keepdims=True)
    rstd = lax.rsqrt(jnp.mean(xc * xc, axis=-1, keepdims=True) + NORM_EPS)
    xh = xc * rstd
    y = xh * lg + lb
    sg = jax.nn.sigmoid(y)
    dy = ds * (sg * (1.0 + y * (1.0 - sg)))
    dxh = dy * lg
    dcv = rstd * (dxh - jnp.mean(dxh, axis=-1, keepdims=True) - xh * jnp.mean(dxh * xh, axis=-1, keepdims=True))
    return (dcv, jnp.sum(dy * xh, axis=0, keepdims=True), jnp.sum(dy, axis=0, keepdims=True),
            jnp.sum(dcv, axis=0, keepdims=True))


def _conv_bwd(u, dcv, dw_pad, tm=256):
    T = u.shape[0]
    Dm = D_MODEL
    hpt = tm // HALO
    last = T // HALO - 1
    nt = T // tm

    def body(ac_ref, gc_ref, ap_ref, gp_ref, dc_ref, dn_ref, w_ref, du_ref, db_ref, dw_ref, ext_g, ext_d):
        i = pl.program_id(0)
        ext_g[pl.ds(0, HALO), :] = jnp.where(i > 0, _glu(ap_ref[...], gp_ref[...]), 0.0)
        ext_g[pl.ds(HALO, tm), :] = _glu(ac_ref[...], gc_ref[...])
        ext_d[pl.ds(0, tm), :] = dc_ref[...]
        ext_d[pl.ds(tm, HALO), :] = jnp.where(i < nt - 1, dn_ref[...], 0.0)

        @pl.when(i == 0)
        def _():
            db_ref[...] = jnp.zeros_like(db_ref)
            dw_ref[...] = jnp.zeros_like(dw_ref)

        def rows(r, carry):
            r0 = pl.multiple_of(r * ROWS, ROWS)
            rs = pl.ds(r0, ROWS)
            for c in range(Dm // COLS):
                cs = pl.ds(c * COLS, COLS)
                cs2 = pl.ds(Dm + c * COLS, COLS)
                de = ext_d[pl.ds(r0, ROWS + HALO), cs]
                ge = ext_g[pl.ds(r0, ROWS + HALO), cs]
                dcur = de[0:ROWS, :]
                acc = jnp.zeros((ROWS, COLS), F32)
                for j in range(CONV_W):
                    off = CONV_W - 1 - j
                    acc = acc + de[off:off + ROWS, :] * w_ref[pl.ds(j, 1), cs]
                    goff = HALO - (CONV_W - 1) + j
                    prod = dcur * ge[goff:goff + ROWS, :]
                    dw_ref[j, :, cs] += jnp.sum(prod.reshape(ROWS // 8, 8, COLS), axis=0)
                a = ac_ref[rs, cs].astype(F32)
                sg = jax.nn.sigmoid(gc_ref[rs, cs].astype(F32))
                da = acc * sg
                dg = acc * a * sg * (1.0 - sg)
                du_ref[rs, cs] = da.astype(BF)
                du_ref[rs, cs2] = dg.astype(BF)
                db_ref[:, cs] += jnp.sum(da, axis=0, keepdims=True)
                db_ref[:, cs2] += jnp.sum(dg, axis=0, keepdims=True)
            return carry

        lax.fori_loop(0, tm // ROWS, rows, 0)

    return pl.pallas_call(
        body, name="conv_bwd", grid=(nt,),
        in_specs=[pl.BlockSpec((tm, Dm), lambda i: (i, 0)), pl.BlockSpec((tm, Dm), lambda i: (i, 1)),
                  pl.BlockSpec((HALO, Dm), lambda i: (jnp.maximum(i * hpt - 1, 0), 0)),
                  pl.BlockSpec((HALO, Dm), lambda i: (jnp.maximum(i * hpt - 1, 0), 1)),
                  pl.BlockSpec((tm, Dm), lambda i: (i, 0)),
                  pl.BlockSpec((HALO, Dm), lambda i: (jnp.minimum((i + 1) * hpt, last), 0)),
                  pl.BlockSpec((HALO, Dm), lambda i: (0, 0))],
        out_specs=[pl.BlockSpec((tm, 2 * Dm), lambda i: (i, 0)), pl.BlockSpec((1, 2 * Dm), lambda i: (0, 0)),
                   pl.BlockSpec((HALO, 8, Dm), lambda i: (0, 0, 0))],
        out_shape=[jax.ShapeDtypeStruct((T, 2 * Dm), BF), jax.ShapeDtypeStruct((1, 2 * Dm), F32),
                   jax.ShapeDtypeStruct((HALO, 8, Dm), F32)],
        scratch_shapes=[pltpu.VMEM((tm + HALO, Dm), F32), pltpu.VMEM((tm + HALO, Dm), F32)],
        compiler_params=_params(1),
    )(u, u, u, u, dcv, dcv, dw_pad)


def _bucket_table():
    q_loc = np.arange(BLOCK)[:, None]
    k_loc = np.arange(2 * BLOCK)[None, :]
    dist = q_loc + BLOCK - k_loc
    n = np.maximum(dist, 0)
    max_exact = REL_BUCKETS // 2
    large = max_exact + (np.log(np.maximum(n, 1).astype(np.float32) / max_exact)
                         / math.log(REL_MAX_DIST / max_exact) * (REL_BUCKETS - max_exact)).astype(np.int32)
    large = np.minimum(large, REL_BUCKETS - 1)
    bucket = np.where(n < max_exact, n, large).astype(np.int32)
    return jnp.asarray(np.where((dist >= 0) & (dist < BLOCK), bucket, -1).astype(np.int32))


def _bias_table(rel_bias, bucket):
    def body(rb_ref, bk_ref, o_ref):
        bk = bk_ref[...]
        for h in range(N_HEADS):
            acc = jnp.full((BLOCK, 2 * BLOCK), NEG_INF, F32)
            for b in range(REL_BUCKETS):
                acc = jnp.where(bk == b, rb_ref[b, h], acc)
            o_ref[h] = acc

    return pl.pallas_call(
        body, name="bias_table", out_shape=jax.ShapeDtypeStruct((N_HEADS, BLOCK, 2 * BLOCK), F32),
        in_specs=[pl.BlockSpec(memory_space=pltpu.SMEM), pl.BlockSpec(memory_space=pltpu.VMEM)],
        out_specs=pl.BlockSpec(memory_space=pltpu.VMEM),
    )(rel_bias, bucket)


def _bias_grad(dbias, bucket):
    def body(db_ref, bk_ref, o_ref):
        bk = bk_ref[...]
        for b in range(REL_BUCKETS):
            sel = bk == b
            for h in range(N_HEADS):
                o_ref[b, h] = jnp.sum(jnp.where(sel, db_ref[h], 0.0))

    return pl.pallas_call(
        body, name="bias_grad", out_shape=jax.ShapeDtypeStruct((REL_BUCKETS, N_HEADS), F32),
        in_specs=[pl.BlockSpec(memory_space=pltpu.VMEM), pl.BlockSpec(memory_space=pltpu.VMEM)],
        out_specs=pl.BlockSpec(memory_space=pltpu.SMEM),
    )(dbias, bucket)


GROUP_ROWS = GROUP * BLOCK


def _head_probs(qk, bias_h, sink, first):
    s = jnp.where(first, NEG_INF, qk + bias_h)
    m = jnp.maximum(jnp.max(s, axis=-1, keepdims=True), sink)
    p = jnp.exp(s - m)
    ps = jnp.exp(sink - m)
    inv = 1.0 / (jnp.sum(p, axis=-1, keepdims=True) + ps)
    return p * inv, ps * inv


def _band(prev_ref, cur_ref, g):
    hs = pl.ds(g * HEAD_DIM, HEAD_DIM)
    return jnp.concatenate([prev_ref[:, hs], cur_ref[:, hs]], axis=0)


def _stack_heads(ref, g):
    return jnp.concatenate([ref[:, pl.ds((g * GROUP + hh) * HEAD_DIM, HEAD_DIM)] for hh in range(GROUP)], axis=0)


def _unstack_heads(ref, g, stacked, dtype):
    for hh in range(GROUP):
        ref[:, pl.ds((g * GROUP + hh) * HEAD_DIM, HEAD_DIM)] = stacked[hh * BLOCK:(hh + 1) * BLOCK, :].astype(dtype)


def _first_mask(n):
    col = lax.broadcasted_iota(jnp.int32, (1, 2 * BLOCK), 1)
    return jnp.logical_and(n == 0, col < BLOCK)


def _head_rows(hh):
    return pl.ds(hh * BLOCK, BLOCK)


def _attn_fwd(qn, kn, vv, bias, sinks):
    T = qn.shape[0]
    nb = T // BLOCK

    def body(sk_ref, q_ref, kc_ref, kp_ref, vc_ref, vp_ref, b_ref, o_ref, qk_buf, p_buf):
        first = _first_mask(pl.program_id(0))
        for g in range(N_KV):
            k = _band(kp_ref, kc_ref, g)
            v = _band(vp_ref, vc_ref, g)
            qk_buf[...] = _dot(_stack_heads(q_ref, g), k, 1, 1)
            for hh in range(GROUP):
                h = g * GROUP + hh
                pn, _ = _head_probs(qk_buf[_head_rows(hh), :], b_ref[h], sk_ref[h], first)
                p_buf[_head_rows(hh), :] = pn.astype(BF)
            _unstack_heads(o_ref, g, _dot(p_buf[...], v, 1, 0), BF)

    cur = lambda n: (n, 0)
    prev = lambda n: (jnp.maximum(n - 1, 0), 0)
    return pl.pallas_call(
        body, name="attn_fwd", grid=(nb,),
        in_specs=[pl.BlockSpec(memory_space=pltpu.SMEM), pl.BlockSpec((BLOCK, ATTN_DIM), cur),
                  pl.BlockSpec((BLOCK, KV_DIM), cur), pl.BlockSpec((BLOCK, KV_DIM), prev),
                  pl.BlockSpec((BLOCK, KV_DIM), cur), pl.BlockSpec((BLOCK, KV_DIM), prev),
                  pl.BlockSpec((N_HEADS, BLOCK, 2 * BLOCK), lambda n: (0, 0, 0))],
        out_specs=pl.BlockSpec((BLOCK, ATTN_DIM), cur), out_shape=jax.ShapeDtypeStruct((T, ATTN_DIM), BF),
        scratch_shapes=[pltpu.VMEM((GROUP_ROWS, 2 * BLOCK), F32), pltpu.VMEM((GROUP_ROWS, 2 * BLOCK), BF)],
        compiler_params=_params(1),
    )(sinks, qn, kn, kn, vv, vv, bias)


def _attn_bwd(qn, kn, vv, bias, sinks, do):
    T = qn.shape[0]
    nb = T // BLOCK
    scale = 1.0 / math.sqrt(HEAD_DIM)

    def body(sk_ref, q_ref, kc_ref, kp_ref, vc_ref, vp_ref, b_ref, do_ref,
             dq_ref, dk_ref, dv_ref, db_ref, dsk_ref, dk_full, dv_full, dk_carry, dv_carry, qk_buf, dp_buf, p_buf, ds_buf):
        n = pl.program_id(0)

        @pl.when(n == 0)
        def _():
            db_ref[...] = jnp.zeros_like(db_ref)
            dk_carry[...] = jnp.zeros_like(dk_carry)
            dv_carry[...] = jnp.zeros_like(dv_carry)
            for h in range(N_HEADS):
                dsk_ref[h] = 0.0

        @pl.when(n < nb)
        def _():
            first = _first_mask(n)
            for g in range(N_KV):
                k = _band(kp_ref, kc_ref, g)
                v = _band(vp_ref, vc_ref, g)
                q = _stack_heads(q_ref, g)
                dout = _stack_heads(do_ref, g)
                qk_buf[...] = _dot(q, k, 1, 1)
                dp_buf[...] = _dot(dout, v, 1, 1)
                for hh in range(GROUP):
                    h = g * GROUP + hh
                    rows = _head_rows(hh)
                    pn, psink = _head_probs(qk_buf[rows, :], b_ref[h], sk_ref[h], first)
                    dp = dp_buf[rows, :]
                    delta = jnp.sum(pn * dp, axis=-1, keepdims=True)
                    ds = pn * (dp - delta)
                    dsk_ref[h] += -jnp.sum(psink * delta)
                    db_ref[h] += ds
                    ds_buf[rows, :] = ds.astype(BF)
                    p_buf[rows, :] = pn.astype(BF)
                dsb = ds_buf[...]
                _unstack_heads(dq_ref, g, _dot(dsb, k, 1, 0) * scale, F32)
                gs = pl.ds(g * HEAD_DIM, HEAD_DIM)
                dk_full[:, gs] = _dot(dsb, q, 0, 0)
                dv_full[:, gs] = _dot(p_buf[...], dout, 0, 0)

        @pl.when(n == nb)
        def _():
            dk_full[...] = jnp.zeros_like(dk_full)
            dv_full[...] = jnp.zeros_like(dv_full)

        dk_ref[...] = dk_carry[...] + dk_full[pl.ds(0, BLOCK), :]
        dv_ref[...] = dv_carry[...] + dv_full[pl.ds(0, BLOCK), :]
        dk_carry[...] = dk_full[pl.ds(BLOCK, BLOCK), :]
        dv_carry[...] = dv_full[pl.ds(BLOCK, BLOCK), :]

    cur = lambda n: (jnp.minimum(n, nb - 1), 0)
    prev = lambda n: (jnp.maximum(jnp.minimum(n, nb - 1) - 1, 0), 0)
    out_kv = lambda n: (jnp.maximum(n - 1, 0), 0)
    return pl.pallas_call(
        body, name="attn_bwd", grid=(nb + 1,),
        in_specs=[pl.BlockSpec(memory_space=pltpu.SMEM), pl.BlockSpec((BLOCK, ATTN_DIM), cur),
                  pl.BlockSpec((BLOCK, KV_DIM), cur), pl.BlockSpec((BLOCK, KV_DIM), prev),
                  pl.BlockSpec((BLOCK, KV_DIM), cur), pl.BlockSpec((BLOCK, KV_DIM), prev),
                  pl.BlockSpec((N_HEADS, BLOCK, 2 * BLOCK), lambda n: (0, 0, 0)),
                  pl.BlockSpec((BLOCK, ATTN_DIM), cur)],
        out_specs=[pl.BlockSpec((BLOCK, ATTN_DIM), cur), pl.BlockSpec((BLOCK, KV_DIM), out_kv),
                   pl.BlockSpec((BLOCK, KV_DIM), out_kv),
                   pl.BlockSpec((N_HEADS, BLOCK, 2 * BLOCK), lambda n: (0, 0, 0)),
                   pl.BlockSpec(memory_space=pltpu.SMEM)],
        out_shape=[jax.ShapeDtypeStruct((T, ATTN_DIM), F32), jax.ShapeDtypeStruct((T, KV_DIM), F32),
                   jax.ShapeDtypeStruct((T, KV_DIM), F32),
                   jax.ShapeDtypeStruct((N_HEADS, BLOCK, 2 * BLOCK), F32), jax.ShapeDtypeStruct((N_HEADS,), F32)],
        scratch_shapes=[pltpu.VMEM((2 * BLOCK, KV_DIM), F32), pltpu.VMEM((2 * BLOCK, KV_DIM), F32),
                        pltpu.VMEM((BLOCK, KV_DIM), F32), pltpu.VMEM((BLOCK, KV_DIM), F32),
                        pltpu.VMEM((GROUP_ROWS, 2 * BLOCK), F32), pltpu.VMEM((GROUP_ROWS, 2 * BLOCK), F32),
                        pltpu.VMEM((GROUP_ROWS, 2 * BLOCK), BF), pltpu.VMEM((GROUP_ROWS, 2 * BLOCK), BF)],
        compiler_params=_params(1),
    )(sinks, qn, kn, kn, vv, vv, bias, do)


def _coords():
    return lax.axis_index("x"), lax.axis_index("y"), lax.axis_index("c")


def _gather8(name, v, with_sum):
    R = v.shape[0]

    def body(v_ref, all_ref, *rest):
        sum_ref = rest[0] if with_sum else None
        send_sems, recv_sems, local_sem = rest[-3:]
        x, y, c = _coords()
        me = 4 * x + 2 * y + c
        local = pltpu.make_async_copy(v_ref, all_ref.at[me], local_sem)
        local.start()
        sends = []
        for k in range(1, 8):
            peer = (x ^ (k >> 2), y ^ ((k >> 1) & 1), c ^ (k & 1))
            cp = pltpu.make_async_remote_copy(src_ref=v_ref, dst_ref=all_ref.at[me], send_sem=send_sems.at[k - 1],
                                              recv_sem=recv_sems.at[k - 1], device_id=peer, device_id_type=MESH)
            cp.start()
            sends.append(cp)
        for k in range(1, 8):
            peer = (x ^ (k >> 2), y ^ ((k >> 1) & 1), c ^ (k & 1))
            pltpu.make_async_remote_copy(src_ref=v_ref, dst_ref=all_ref.at[me ^ k], send_sem=send_sems.at[k - 1],
                                         recv_sem=recv_sems.at[k - 1], device_id=peer, device_id_type=MESH).wait_recv()
        for cp in sends:
            cp.wait_send()
        local.wait()
        if with_sum:
            tot = all_ref[0]
            for d in range(1, 8):
                tot = tot + all_ref[d]
            sum_ref[...] = tot

    out_shape = [jax.ShapeDtypeStruct((8, R, LANES), F32)]
    if with_sum:
        out_shape.append(jax.ShapeDtypeStruct((R, LANES), F32))
    vm = pl.BlockSpec(memory_space=pltpu.VMEM)
    return pl.pallas_call(
        body, name=name, out_shape=out_shape, in_specs=[vm], out_specs=[vm] * len(out_shape),
        scratch_shapes=[pltpu.SemaphoreType.DMA((7,)), pltpu.SemaphoreType.DMA((7,)), pltpu.SemaphoreType.DMA],
    )(v)


CHIP_FLIPS = ((1, 0), (0, 1), (1, 1))


HBM_SPEC = pl.BlockSpec(memory_space=pltpu.HBM)
SEM_SPEC = pl.BlockSpec(memory_space=pltpu.SEMAPHORE)
ANY_SPEC = pl.BlockSpec(memory_space=pl.ANY)
DATAFLOW = pltpu.SideEffectType.DATAFLOW_SIDE_EFFECTING


def _chip_copy(land, sems, idx, slot_src, slot_dst, peer):
    send_sems, recv_sems = sems
    return pltpu.make_async_remote_copy(src_ref=land.at[slot_src], dst_ref=land.at[slot_dst], send_sem=send_sems.at[idx],
                                        recv_sem=recv_sems.at[idx], device_id=peer, device_id_type=MESH)


def _gather_start(stacks, groups, after):
    n = len(stacks)
    ng = len(groups)
    after = tuple(after)

    def body(*refs):
        lands = refs[:n]
        first = n + len(after)
        sems = [(refs[first + 2 * g], refs[first + 2 * g + 1]) for g in range(ng)]
        token = refs[-1]
        x, y, c = _coords()
        s = 2 * x + y
        for g, members in enumerate(groups):
            for i, t in enumerate(members):
                for j, (fx, fy) in enumerate(CHIP_FLIPS):
                    _chip_copy(lands[t], sems[g], 3 * i + j, s, s, (x ^ fx, y ^ fy, c)).start()
        token[...] = jnp.zeros_like(token)

    out_shape = []
    for members in groups:
        out_shape += [pltpu.SemaphoreType.DMA((3 * len(members),))] * 2
    out_shape += [pltpu.HBM(w.shape, w.dtype) for w in stacks]
    out_shape.append(jax.ShapeDtypeStruct((8, 128), F32))
    res = pl.pallas_call(
        body, name="gather_start", out_shape=out_shape, in_specs=[HBM_SPEC] * n + [ANY_SPEC] * len(after),
        out_specs=[SEM_SPEC] * (2 * ng) + [HBM_SPEC] * n + [pl.BlockSpec(memory_space=pltpu.VMEM)],
        input_output_aliases={t: 2 * ng + t for t in range(n)},
        compiler_params=pltpu.CompilerParams(has_side_effects=DATAFLOW),
    )(*[pltpu.with_memory_space_constraint(w, pltpu.HBM) for w in stacks], *after)
    sems = [(res[2 * g], res[2 * g + 1]) for g in range(ng)]
    return sems, list(res[2 * ng:2 * ng + n]), res[-1]


def _gather_wait(name, stacks, sems, after):
    n = len(stacks)
    after = tuple(after)

    def body(*refs):
        lands = refs[:n]
        group_sems = (refs[n], refs[n + 1])
        x, y, c = _coords()
        s = 2 * x + y
        for i in range(n):
            for j, (fx, fy) in enumerate(CHIP_FLIPS):
                cp = _chip_copy(lands[i], group_sems, 3 * i + j, s, 2 * (x ^ fx) + (y ^ fy), (x ^ fx, y ^ fy, c))
                cp.wait_send()
                cp.wait_recv()

    return pl.pallas_call(
        body, name=name, out_shape=[pltpu.HBM(w.shape, w.dtype) for w in stacks],
        in_specs=[HBM_SPEC] * n + [SEM_SPEC, SEM_SPEC] + [ANY_SPEC] * len(after), out_specs=[HBM_SPEC] * n,
        input_output_aliases={t: t for t in range(n)},
        compiler_params=pltpu.CompilerParams(has_side_effects=DATAFLOW),
    )(*stacks, sems[0], sems[1], *after)


N_PEERS = 7


def _peer(x, y, c, k):
    return x ^ (k >> 2), y ^ ((k >> 1) & 1), c ^ (k & 1)


def _reduce_copy(grad, land, sems, idx, x, y, c, k):
    px, py, pc = _peer(x, y, c, k)
    rh = grad.shape[1] // 2
    return pltpu.make_async_remote_copy(src_ref=grad.at[2 * px + py, pl.ds(pc * rh, rh), :], dst_ref=land.at[k - 1],
                                        send_sem=sems[0].at[idx], recv_sem=sems[1].at[idx], device_id=(px, py, pc),
                                        device_id_type=MESH)


def _reduce_start(name, grads):
    n = len(grads)

    def body(*refs):
        src, lands, sems, token = refs[:n], refs[n:2 * n], (refs[2 * n], refs[2 * n + 1]), refs[-1]
        x, y, c = _coords()
        for t in range(n):
            for k in range(1, N_PEERS + 1):
                _reduce_copy(src[t], lands[t], sems, N_PEERS * t + k - 1, x, y, c, k).start()
        token[...] = jnp.zeros_like(token)

    lands = [lax.empty((N_PEERS, g.shape[1] // 2, g.shape[2]), g.dtype) for g in grads]
    out_shape = [pltpu.SemaphoreType.DMA((N_PEERS * n,))] * 2
    out_shape += [pltpu.HBM(a.shape, a.dtype) for a in list(grads) + lands]
    out_shape.append(jax.ShapeDtypeStruct((8, 128), F32))
    res = pl.pallas_call(
        body, name=name, out_shape=out_shape, in_specs=[HBM_SPEC] * (2 * n),
        out_specs=[SEM_SPEC] * 2 + [HBM_SPEC] * (2 * n) + [pl.BlockSpec(memory_space=pltpu.VMEM)],
        input_output_aliases={t: 2 + t for t in range(2 * n)},
        compiler_params=pltpu.CompilerParams(has_side_effects=DATAFLOW),
    )(*[pltpu.with_memory_space_constraint(a, pltpu.HBM) for a in list(grads) + lands])
    return (res[0], res[1]), list(res[2:2 + n]), list(res[2 + n:2 + 2 * n]), res[-1]


def _reduce_wait(name, grads, lands, sems, after):
    n = len(grads)
    after = tuple(after)

    def body(*refs):
        src, dst, group_sems = refs[:n], refs[n:2 * n], (refs[2 * n], refs[2 * n + 1])
        x, y, c = _coords()
        for t in range(n):
            for k in range(1, N_PEERS + 1):
                cp = _reduce_copy(src[t], dst[t], group_sems, N_PEERS * t + k - 1, x, y, c, k)
                cp.wait_send()
                cp.wait_recv()

    res = pl.pallas_call(
        body, name=name, out_shape=[pltpu.HBM(a.shape, a.dtype) for a in list(grads) + list(lands)],
        in_specs=[HBM_SPEC] * (2 * n) + [SEM_SPEC, SEM_SPEC] + [ANY_SPEC] * len(after), out_specs=[HBM_SPEC] * (2 * n),
        input_output_aliases={t: t for t in range(2 * n)},
        compiler_params=pltpu.CompilerParams(has_side_effects=DATAFLOW),
    )(*grads, *lands, sems[0], sems[1], *after)
    return list(res[:n]), list(res[n:])


def _join_halves(name, halves):
    n = len(halves)

    def body(*refs):
        src, dst = refs[:n], refs[n:2 * n]
        send_sems, recv_sems = refs[2 * n:]
        x, y, c = _coords()
        cps = []
        for t in range(n):
            cp = pltpu.make_async_remote_copy(src_ref=src[t], dst_ref=dst[t], send_sem=send_sems.at[t],
                                              recv_sem=recv_sems.at[t], device_id=(x, y, 1 - c), device_id_type=MESH)
            cp.start()
            cps.append(cp)
        for cp in cps:
            cp.wait()

    anyspec = pl.BlockSpec(memory_space=pl.ANY)
    return pl.pallas_call(
        body, name=name, out_shape=[jax.ShapeDtypeStruct(h.shape, h.dtype) for h in halves],
        in_specs=[anyspec] * n, out_specs=[anyspec] * n,
        scratch_shapes=[pltpu.SemaphoreType.DMA((n,)), pltpu.SemaphoreType.DMA((n,))],
    )(*halves)


def _row_block(rows):
    for rb in (512, 256, 128, 64, 32, 16):
        if rows % rb == 0:
            return rb
    raise ValueError(rows)


def _sum_devices(name, grad, land, place):
    S, R, C = grad.shape
    rh = R // 2
    rb = _row_block(rh)
    nbh = rh // rb

    def body(place_ref, g_ref, l_ref, o_ref):
        tot = g_ref[...].astype(F32)
        for k in range(N_PEERS):
            tot = tot + l_ref[k].astype(F32)
        o_ref[...] = tot

    return pl.pallas_call(
        body, name=name,
        grid_spec=pltpu.PrefetchScalarGridSpec(
            num_scalar_prefetch=1, grid=(nbh,),
            in_specs=[pl.BlockSpec((None, rb, C), lambda r, place: (place[0], place[1] * nbh + r, 0)),
                      pl.BlockSpec((N_PEERS, rb, C), lambda r, place: (0, r, 0))],
            out_specs=pl.BlockSpec((rb, C), lambda r, place: (r, 0))),
        out_shape=jax.ShapeDtypeStruct((rh, C), F32), compiler_params=_params(1),
    )(place, grad, land)


def _adamw_math(w, g, m, v):
    m2 = ADAM_B1 * m + (1.0 - ADAM_B1) * g
    v2 = ADAM_B2 * v + (1.0 - ADAM_B2) * (g * g)
    m_hat = m2 / (1.0 - ADAM_B1 ** ADAM_STEP)
    v_hat = v2 / (1.0 - ADAM_B2 ** ADAM_STEP)
    delta = -ADAM_LR * (m_hat / (jnp.sqrt(v_hat) + ADAM_EPS) + ADAM_WD * w)
    return delta, m2, v2


def _adamw(name, w, m, v, gs):
    L, R, C = w.shape
    Rh = R // 2
    rb = _row_block(Rh)
    nbh = Rh // rb
    assert len(gs) == L

    def body(core_ref, w_ref, m_ref, v_ref, *rest):
        g_refs, (go_ref, d_ref, m2_ref, v2_ref) = rest[:2 * L], rest[2 * L:]
        layer, half = pl.program_id(0), pl.program_id(1)
        mine = half == core_ref[0]
        g = jnp.where(mine, g_refs[0][...], g_refs[1][...])
        for t in range(1, L):
            g = jnp.where(layer == t, jnp.where(mine, g_refs[2 * t][...], g_refs[2 * t + 1][...]), g)
        delta, m2, v2 = _adamw_math(w_ref[...], g, m_ref[...], v_ref[...])
        go_ref[...] = g
        d_ref[...] = delta
        m2_ref[...] = m2
        v2_ref[...] = v2

    wspec = pl.BlockSpec((None, rb, C), lambda l, h, r, core: (l, h * nbh + r, 0))
    gspec = pl.BlockSpec((rb, C), lambda l, h, r, core: (r, 0))
    return pl.pallas_call(
        body, name=name,
        grid_spec=pltpu.PrefetchScalarGridSpec(num_scalar_prefetch=1, grid=(L, 2, nbh),
                                               in_specs=[wspec] * 3 + [gspec] * (2 * L), out_specs=[wspec] * 4),
        out_shape=[jax.ShapeDtypeStruct((L, R, C), F32)] * 4, compiler_params=_params(3),
    )(lax.axis_index("c").astype(jnp.int32).reshape(1), w, m, v, *[g for pair in gs for g in pair])


def _adamw_small(w, g, m, v):
    def body(w_ref, g_ref, m_ref, v_ref, d_ref, m2_ref, v2_ref):
        delta, m2, v2 = _adamw_math(w_ref[...], g_ref[...], m_ref[...], v_ref[...])
        d_ref[...] = delta
        m2_ref[...] = m2
        v2_ref[...] = v2

    return pl.pallas_call(body, name="adamw_small", out_shape=[jax.ShapeDtypeStruct(w.shape, F32)] * 3)(w, g, m, v)


def _pack(arrays):
    rows = []
    for a in arrays:
        a = a.astype(F32).reshape(-1, a.shape[-1])
        r, c = a.shape
        k = -(-c // LANES)
        a = jnp.pad(a, ((0, 0), (0, k * LANES - c))).reshape(r * k, LANES)
        rows.append(jnp.pad(a, ((0, -(r * k) % 8), (0, 0))))
    return jnp.concatenate(rows, axis=0)


def _unpack(buf, shapes):
    out, r0 = [], 0
    for shp in shapes:
        c = shp[-1]
        r = int(np.prod(shp)) // c
        k = -(-c // LANES)
        out.append(buf[r0:r0 + r * k].reshape(r, k * LANES)[:, :c].reshape(shp))
        r0 += r * k + (-(r * k) % 8)
    return out


def _mlp_fwd(tag, x, g, w_up_sm, w_down):
    h = _rms_fwd(f"mlp{tag}_norm", x, g)
    (up,) = _mm(f"mlp{tag}_up", h, w_up_sm, nt=False, b_sm=True, tm=1024, tn=1024, rows=256,
                ep_fn=lambda acc: (acc,), outs=(("tile", BF),))
    return h, up


RMS_BWD_OUTS = (("tile", F32), ("tile", BF), ("colsum", F32), ("colsum", F32))


def _mlp_bwd(tag, dy, dy_bf, x, g, h, up, w_up_sm, w_down, place):
    (dup,) = _mm(f"mlp{tag}_dup", dy_bf, w_down, nt=True, tm=1024, tn=1024, rows=256, ep_in=((up, "tile"),),
                 ep_fn=lambda acc, u: (acc * (2.0 * jnp.maximum(u.astype(F32), 0.0)),), outs=(("tile", BF),))
    dw_down = _mm_tn(f"mlp{tag}_dw_down", up, dy_bf, tm=1024, tn=1024, tk=2048, a_fn=_relu2)
    dw_up = _mm_tn(f"mlp{tag}_dw_up", h, dup, tm=1024, tn=1024, tk=2048, out_sm=N_SHARD)
    red = _Reduction(f"mlp{tag}", [dw_up, dw_down.reshape(N_SHARD, D_FF // N_SHARD, D_MODEL)], place)
    dx, dx_bf, dg, dx_sum = _mm(f"mlp{tag}_dx", dup, w_up_sm, nt=True, b_sm=True, tm=512, tn=1024, rows=256,
                                ep_in=((x, "tile"), (g, "row"), (dy, "tile")), ep_fn=_rms_bwd_ep, outs=RMS_BWD_OUTS,
                                deps=(red.token,))
    return dx, dx_bf, dg, dx_sum, red


class _Reduction:
    def __init__(self, tag, grads, place):
        self.tag, self.place = tag, place
        self.sems, self.grads, self.lands, self.token = _reduce_start(f"reduce_start_{tag}", grads)

    def finish(self, after):
        grads, lands = _reduce_wait(f"reduce_wait_{self.tag}", self.grads, self.lands, self.sems, after)
        halves = [_sum_devices(f"reduce_sum_{self.tag}{i}", g, l, self.place) for i, (g, l) in enumerate(zip(grads, lands))]
        return list(zip(halves, _join_halves(f"join_halves_{self.tag}", halves)))


def kernel(x, conv_norm_g, conv_w_in, conv_b_in, conv_dw, conv_dw_b, conv_ln_g, conv_ln_b, conv_w_out, conv_b_out, attn_norm_g, w_qkv, b_qkv, q_norm_g, k_norm_g, sinks, w_o, b_o, rel_bias, mlp_norm_g, w_up, w_down, loss_target, m_conv_norm_g, m_conv_w_in, m_conv_b_in, m_conv_dw, m_conv_dw_b, m_conv_ln_g, m_conv_ln_b, m_conv_w_out, m_conv_b_out, m_attn_norm_g, m_w_qkv, m_b_qkv, m_q_norm_g, m_k_norm_g, m_sinks, m_w_o, m_b_o, m_rel_bias, m_mlp_norm_g, m_w_up, m_w_down, v_conv_norm_g, v_conv_w_in, v_conv_b_in, v_conv_dw, v_conv_dw_b, v_conv_ln_g, v_conv_ln_b, v_conv_w_out, v_conv_b_out, v_attn_norm_g, v_w_qkv, v_b_qkv, v_q_norm_g, v_k_norm_g, v_sinks, v_w_o, v_b_o, v_rel_bias, v_mlp_norm_g, v_w_up, v_w_down):
    Dm = D_MODEL
    x2d = x[0]
    tgt = loss_target[0]
    T = x2d.shape[0]
    shard = 2 * lax.axis_index("x") + lax.axis_index("y")

    sharded_small = [conv_dw[0], attn_norm_g, b_qkv, b_o]
    (gathered,) = _gather8("gather_small_weights", _pack(sharded_small), with_sum=False)
    chips = [_unpack(gathered[2 * s], [a.shape for a in sharded_small]) for s in range(N_SHARD)]
    dw_f, attn_norm_f, b_qkv_f, b_o_f = (jnp.concatenate([chips[s][t] for s in range(N_SHARD)], axis=-1)
                                         for t in range(len(sharded_small)))
    dw_pad = jnp.pad(dw_f, ((0, HALO - CONV_W), (0, 0)))

    big = [conv_w_in[0], conv_w_out[0], w_qkv[0], w_o[0], w_up[0], w_up[1], w_down[0], w_down[1]]
    stacks = [lax.dynamic_update_slice(jnp.zeros((N_SHARD,) + w.shape, BF), w.astype(BF)[None], (shard, 0, 0))
              for w in big]
    groups = ((0, 1), (4, 6), (2, 3), (5, 7))
    gather_sems, stacks, gather_token = _gather_start(stacks, groups, after=(gathered,))

    def gathered_group(g, name, after):
        return _gather_wait(name, [stacks[t] for t in groups[g]], gather_sems[g], after)

    bucket = _bucket_table()
    bias = _bias_table(rel_bias, bucket)

    h0 = _rms_fwd("conv_norm", x2d, conv_norm_g, deps=(gather_token,))
    w_in_sm, g_out = gathered_group(0, "gather_wait_conv", (h0, dw_pad, bias))
    w_out_f = g_out.reshape(Dm, Dm)
    (u,) = _mm("conv_in", h0, w_in_sm, nt=False, b_sm=True, tm=1024, tn=512, rows=256, ep_in=((conv_b_in, "row"),),
               ep_fn=lambda acc, b: (acc + b,), outs=(("tile", BF),))
    cv, s_act = _conv_fwd(u, dw_pad, conv_dw_b, conv_ln_g, conv_ln_b)
    (x1,) = _mm("conv_out", s_act, w_out_f, nt=False, tm=1024, tn=1024, rows=256,
                ep_in=((conv_b_out, "row"), (x2d, "tile")), ep_fn=lambda acc, b, r: (acc + b + r,),
                outs=(("tile", F32),))

    g_up0, g_down0 = gathered_group(1, "gather_wait_mlp0", (x1,))
    w_up_sm = [g_up0, None]
    w_down_f = [g_down0.reshape(D_FF, Dm), None]
    h1, up0 = _mlp_fwd(0, x1, mlp_norm_g[0:1], w_up_sm[0], w_down_f[0])
    (x2,) = _mm("mlp0_down", up0, w_down_f[0], nt=False, tm=512, tn=1024, rows=256, a_fn=_relu2,
                ep_in=((x1, "tile"),), ep_fn=lambda acc, r: (acc + r,), outs=(("tile", F32),))

    g_qkv, g_o = gathered_group(2, "gather_wait_attn", (x2,))
    w_qkv_f = jnp.transpose(g_qkv, (1, 0, 2)).reshape(Dm, QKV_DIM)
    w_o_f = g_o.reshape(ATTN_DIM, Dm)
    h2 = _rms_fwd("attn_norm", x2, attn_norm_f)
    (qkv,) = _mm("attn_qkv", h2, w_qkv_f, nt=False, tm=1024, tn=QKV_DIM, rows=256, ep_in=((b_qkv_f, "row"),),
                 ep_fn=lambda acc, b: (acc + b,), outs=(("tile", F32),))
    qg_t = jnp.tile(q_norm_g, (1, N_HEADS))
    kg_t = jnp.tile(k_norm_g, (1, N_KV))
    qn, kn, vv = _qk_norm_fwd(qkv, qg_t, kg_t)
    sinks1 = sinks[0]
    att = _attn_fwd(qn, kn, vv, bias, sinks1)
    (x3,) = _mm("attn_out", att, w_o_f, nt=False, tm=1024, tn=1024, rows=256,
                ep_in=((b_o_f, "row"), (x2, "tile")), ep_fn=lambda acc, b, r: (acc + b + r,), outs=(("tile", F32),))

    g_up1, g_down1 = gathered_group(3, "gather_wait_mlp1", (x3,))
    w_up_sm[1] = g_up1
    w_down_f[1] = g_down1.reshape(D_FF, Dm)
    h3, up1 = _mlp_fwd(1, x3, mlp_norm_g[1:2], w_up_sm[1], w_down_f[1])

    def loss_ep(acc, r, t):
        diff = acc + r - t
        dy = diff * (1.0 / Dm)
        return dy, dy, jnp.sum(diff * diff, axis=0, keepdims=True)

    dy, dy_bf, sq = _mm("mlp1_down_loss", up1, w_down_f[1], nt=False, tm=512, tn=1024, rows=256, a_fn=_relu2,
                        ep_in=((x3, "tile"), (tgt, "tile")), ep_fn=loss_ep,
                        outs=(("tile", F32), ("tile", BF), ("colsum", F32)))
    loss = lax.psum(0.5 * jnp.sum(sq) * (1.0 / Dm), ("x", "y", "c"))

    place = jnp.stack([shard, lax.axis_index("c")]).astype(jnp.int32)
    dx3, dx3_bf, dg_mlp1, db_o, red_mlp1 = _mlp_bwd(1, dy, dy_bf, x3, mlp_norm_g[1:2], h3, up1, w_up_sm[1],
                                                    w_down_f[1], place)

    ident = lambda acc: (acc,)
    (datt,) = _mm("attn_dout", dx3_bf, w_o_f, nt=True, tm=1024, tn=1024, rows=256, ep_fn=ident, outs=(("tile", BF),))
    dw_o = _mm_tn("attn_dw_o", att, dx3_bf, tm=1024, tn=1024, tk=2048)
    dqn, dkn, dvv, dbias, dsinks = _attn_bwd(qn, kn, vv, bias, sinks1, datt)
    (r_up1, r_down1) = red_mlp1.finish((dqn,))
    drel = _bias_grad(dbias, bucket)
    dqkv, db_qkv, dqg_t, dkg_t = _qk_norm_bwd(qkv, dqn, dkn, dvv, qg_t, kg_t)
    dw_qkv = _mm_tn("attn_dw_qkv", h2, dqkv, tm=1024, tn=QKV_DIM, tk=2048)
    red_attn = _Reduction("attn", [jnp.transpose(dw_qkv.reshape(Dm, N_SHARD, QKV_DIM // N_SHARD), (1, 0, 2)),
                                   dw_o.reshape(N_SHARD, ATTN_DIM // N_SHARD, Dm)], place)
    dx2, dx2_bf, dg_attn, _ = _mm("attn_dx", dqkv, w_qkv_f, nt=True, tm=512, tn=1024, rows=256,
                                  ep_in=((x2, "tile"), (attn_norm_f, "row"), (dx3, "tile")), ep_fn=_rms_bwd_ep,
                                  outs=RMS_BWD_OUTS, deps=(red_attn.token,))

    dx1, dx1_bf, dg_mlp0, db_out, red_mlp0 = _mlp_bwd(0, dx2, dx2_bf, x1, mlp_norm_g[0:1], h1, up0, w_up_sm[0],
                                                      w_down_f[0], place)
    (r_qkv, r_o) = red_attn.finish((dx1,))

    dcv, dln_g, dln_b, ddw_b = _mm("conv_ds", dx1_bf, w_out_f, nt=True, tm=512, tn=1024, rows=256,
                                   ep_in=((cv, "tile"), (conv_ln_g, "row"), (conv_ln_b, "row")),
                                   ep_fn=_ln_silu_bwd_ep,
                                   outs=(("tile", F32), ("colsum", F32), ("colsum", F32), ("colsum", F32)))
    dw_out = _mm_tn("conv_dw_out", s_act, dx1_bf, tm=1024, tn=1024, tk=2048)
    du, db_in, ddw8 = _conv_bwd(u, dcv, dw_pad)
    (r_up0, r_down0) = red_mlp0.finish((du,))
    dw_in = _mm_tn("conv_dw_in", h0, du, tm=1024, tn=512, tk=2048, out_sm=N_SHARD)
    red_conv = _Reduction("conv", [dw_in, dw_out.reshape(N_SHARD, Dm // N_SHARD, Dm)], place)
    def first_layer_ep(*args):
        tot, _, dg, _ = _rms_bwd_ep(*args)
        return tot, dg

    gx, dg_conv = _mm("conv_dx", du, w_in_sm, nt=True, b_sm=True, tm=512, tn=1024, rows=256,
                      ep_in=((x2d, "tile"), (conv_norm_g, "row"), (dx1, "tile")), ep_fn=first_layer_ep,
                      outs=(("tile", F32), ("colsum", F32)), deps=(red_conv.token,))
    (r_in, r_out) = red_conv.finish((gx,))

    big_out = {}
    for nm, w, m, v, gs in (("conv_w_in", conv_w_in, m_conv_w_in, v_conv_w_in, (r_in,)),
                            ("conv_w_out", conv_w_out, m_conv_w_out, v_conv_w_out, (r_out,)),
                            ("w_qkv", w_qkv, m_w_qkv, v_w_qkv, (r_qkv,)),
                            ("w_o", w_o, m_w_o, v_w_o, (r_o,)),
                            ("w_up", w_up, m_w_up, v_w_up, (r_up0, r_up1)),
                            ("w_down", w_down, m_w_down, v_w_down, (r_down0, r_down1))):
        big_out[nm] = _adamw(f"adamw_{nm}", w, m, v, gs)

    dqg = dqg_t.reshape(N_HEADS, HEAD_DIM).sum(axis=0, keepdims=True)
    dkg = dkg_t.reshape(N_KV, HEAD_DIM).sum(axis=0, keepdims=True)
    small_full = [dg_conv, db_in, ddw8.sum(axis=1)[:CONV_W], ddw_b, dln_g, dln_b, db_out, dg_attn, db_qkv, dqg, dkg,
                  dsinks[None, :], db_o, drel, jnp.pad(dg_mlp0, ((0, 1), (0, 0))) + jnp.pad(dg_mlp1, ((1, 0), (0, 0)))]
    _, small_sum = _gather8("reduce_small_grads", _pack(small_full), with_sum=True)
    (r_norm, r_b_in, r_dw, r_dw_b, r_ln_g, r_ln_b, r_b_out, r_attn_norm, r_b_qkv, r_qg, r_kg, r_sinks, r_b_o, r_rel,
     r_mlp_norm) = _unpack(small_sum, [a.shape for a in small_full])

    def cols(a, width):
        return lax.dynamic_slice_in_dim(a, shard * width, width, axis=a.ndim - 1)

    small_names = ["conv_norm_g", "conv_b_in", "conv_dw", "conv_dw_b", "conv_ln_g", "conv_ln_b", "conv_b_out",
                   "attn_norm_g", "b_qkv", "q_norm_g", "k_norm_g", "sinks", "b_o", "rel_bias", "mlp_norm_g"]
    small_g = [r_norm, r_b_in, cols(r_dw, Dm // N_SHARD)[None], r_dw_b, r_ln_g, r_ln_b, r_b_out,
               cols(r_attn_norm, Dm // N_SHARD), cols(r_b_qkv, QKV_DIM // N_SHARD), r_qg, r_kg, r_sinks,
               cols(r_b_o, Dm // N_SHARD), r_rel, r_mlp_norm]
    small_w = [conv_norm_g, conv_b_in, conv_dw, conv_dw_b, conv_ln_g, conv_ln_b, conv_b_out, attn_norm_g, b_qkv,
               q_norm_g, k_norm_g, sinks, b_o, rel_bias, mlp_norm_g]
    small_m = [m_conv_norm_g, m_conv_b_in, m_conv_dw, m_conv_dw_b, m_conv_ln_g, m_conv_ln_b, m_conv_b_out,
               m_attn_norm_g, m_b_qkv, m_q_norm_g, m_k_norm_g, m_sinks, m_b_o, m_rel_bias, m_mlp_norm_g]
    small_v = [v_conv_norm_g, v_conv_b_in, v_conv_dw, v_conv_dw_b, v_conv_ln_g, v_conv_ln_b, v_conv_b_out,
               v_attn_norm_g, v_b_qkv, v_q_norm_g, v_k_norm_g, v_sinks, v_b_o, v_rel_bias, v_mlp_norm_g]
    flat2 = lambda a: a.reshape(-1, a.shape[-1])
    shapes2 = [flat2(w).shape for w in small_w]
    pk = lambda arrs: _pack([flat2(a) for a in arrs])
    packed_g = pk(small_g)
    d_s, m_s, v_s = _adamw_small(pk(small_w), packed_g, pk(small_m), pk(small_v))
    small_out = {}
    for nm, w, g, d, m2, v2 in zip(small_names, small_w, _unpack(packed_g, shapes2), _unpack(d_s, shapes2),
                                   _unpack(m_s, shapes2), _unpack(v_s, shapes2)):
        small_out[nm] = tuple(a.reshape(w.shape) for a in (g, d, m2, v2))

    order = ["conv_norm_g", "conv_w_in", "conv_b_in", "conv_dw", "conv_dw_b", "conv_ln_g", "conv_ln_b", "conv_w_out",
             "conv_b_out", "attn_norm_g", "w_qkv", "b_qkv", "q_norm_g", "k_norm_g", "sinks", "w_o", "b_o", "rel_bias",
             "mlp_norm_g", "w_up", "w_down"]
    res = {**small_out, **big_out}
    outs = [loss, gx[None]]
    for part in range(4):
        outs += [res[nm][part] for nm in order]
    return tuple(outs)
```

```python
import math

import numpy as np
import jax
import jax.numpy as jnp
from jax import lax
from jax.experimental import pallas as pl
from jax.experimental.pallas import tpu as pltpu

F32 = jnp.float32
BF = jnp.bfloat16
MESH = pl.DeviceIdType.MESH

D_MODEL = 1024
D_FF = 4096
N_HEADS = 16
N_KV = 2
GROUP = N_HEADS // N_KV
HEAD_DIM = 64
ATTN_DIM = N_HEADS * HEAD_DIM
KV_DIM = N_KV * HEAD_DIM
QKV_DIM = ATTN_DIM + 2 * KV_DIM
BLOCK = 128
CONV_W = 31
HALO = 32
REL_BUCKETS = 32
REL_MAX_DIST = 128
NORM_EPS = 1e-6
NEG_INF = -1e30
N_SHARD = 4
LANES = 1024

ADAM_LR = 0.001
ADAM_B1 = 0.9
ADAM_B2 = 0.999
ADAM_EPS = 1e-08
ADAM_WD = 0.01
ADAM_STEP = 10

VMEM_LIMIT = 56 * 1024 * 1024


def _params(n_axes):
    return pltpu.CompilerParams(dimension_semantics=("arbitrary",) * n_axes, vmem_limit_bytes=VMEM_LIMIT)


def _dot(a, b, ca, cb):
    return lax.dot_general(a, b, (((ca,), (cb,)), ((), ())), preferred_element_type=F32)


def _mm(name, a, b, *, nt, tm, tn, ep_fn, outs, a_fn=None, b_sm=False, ep_in=(), deps=(), rows=None):
    M, K = a.shape
    rows = tm if rows is None else rows
    if b_sm:
        S, ks = b.shape[0], b.shape[2]
        N, per = (b.shape[1], None) if nt else (S * b.shape[2], b.shape[2] // tn)
        assert (S * ks == K) if nt else (b.shape[1] == K)
    else:
        N = b.shape[0] if nt else b.shape[1]
        assert (b.shape[1] if nt else b.shape[0]) == K
    assert M % tm == 0 and N % tn == 0 and tm % rows == 0
    ne, no, nd = len(ep_in), len(outs), len(deps)

    def body(a_ref, b_ref, *rest):
        ep_refs, out_refs = rest[:ne], rest[ne + nd:ne + nd + no]
        i = pl.program_id(1)
        sums = [None] * no
        for r in range(tm // rows):
            rs = pl.ds(r * rows, rows)

            def lhs(cols):
                av = a_ref[rs, cols]
                return (av if a_fn is None else a_fn(av)).astype(BF)

            if b_sm and nt:
                acc = None
                for s in range(S):
                    part = _dot(lhs(pl.ds(s * ks, ks)), b_ref[s].astype(BF), 1, 1)
                    acc = part if acc is None else acc + part
            else:
                acc = _dot(lhs(slice(None)), b_ref[...].astype(BF), 1, 1 if nt else 0)
            ep_vals = [ref[rs, :] if kind == "tile" else ref[...] for ref, (_, kind) in zip(ep_refs, ep_in)]
            vals = ep_fn(acc, *ep_vals)
            for o, ((kind, dt), ref, val) in enumerate(zip(outs, out_refs, vals)):
                if kind == "tile":
                    ref[rs, :] = val.astype(dt)
                else:
                    sums[o] = val if sums[o] is None else sums[o] + val
        for (kind, dt), ref, val in zip(outs, out_refs, sums):
            if kind == "colsum":
                @pl.when(i == 0)
                def _():
                    ref[...] = val

                @pl.when(i > 0)
                def _():
                    ref[...] += val

    if b_sm and nt:
        b_spec = pl.BlockSpec((S, tn, ks), lambda j, i: (0, j, 0))
    elif b_sm:
        b_spec = pl.BlockSpec((None, K, tn), lambda j, i: (j // per, 0, j % per))
    elif nt:
        b_spec = pl.BlockSpec((tn, K), lambda j, i: (j, 0))
    else:
        b_spec = pl.BlockSpec((K, tn), lambda j, i: (0, j))
    in_specs = [pl.BlockSpec((tm, K), lambda j, i: (i, 0)), b_spec]
    for arr, kind in ep_in:
        if kind == "tile":
            assert arr.shape == (M, N)
            in_specs.append(pl.BlockSpec((tm, tn), lambda j, i: (i, j)))
        else:
            assert arr.shape == (1, N)
            in_specs.append(pl.BlockSpec((1, tn), lambda j, i: (0, j)))
    in_specs += [pl.BlockSpec(memory_space=pl.ANY)] * nd
    out_shape, out_specs = [], []
    for kind, dt in outs:
        if kind == "tile":
            out_shape.append(jax.ShapeDtypeStruct((M, N), dt))
            out_specs.append(pl.BlockSpec((tm, tn), lambda j, i: (i, j)))
        else:
            out_shape.append(jax.ShapeDtypeStruct((1, N), F32))
            out_specs.append(pl.BlockSpec((1, tn), lambda j, i: (0, j)))
    return pl.pallas_call(
        body, name=name, grid=(N // tn, M // tm), in_specs=in_specs, out_specs=out_specs, out_shape=out_shape,
        compiler_params=_params(2),
    )(a, b, *[arr for arr, _ in ep_in], *deps)


def _mm_tn(name, a, b, *, tm, tn, tk, a_fn=None, out_sm=None):
    T, Ka = a.shape
    N = b.shape[1]
    assert b.shape[0] == T and T % tk == 0 and Ka % tm == 0 and N % tn == 0
    nk = T // tk

    def body(a_ref, b_ref, o_ref, acc_ref):
        k = pl.program_id(2)

        @pl.when(k == 0)
        def _():
            acc_ref[...] = jnp.zeros_like(acc_ref)

        av = a_ref[...]
        if a_fn is not None:
            av = a_fn(av)
        acc_ref[...] += _dot(av.astype(BF), b_ref[...].astype(BF), 0, 0)

        @pl.when(k == nk - 1)
        def _():
            o_ref[...] = acc_ref[...].astype(BF)

    if out_sm is None:
        out_shape = jax.ShapeDtypeStruct((Ka, N), BF)
        out_spec = pl.BlockSpec((tm, tn), lambda i, j, k: (i, j))
    else:
        per = (N // out_sm) // tn
        assert per * tn * out_sm == N
        out_shape = jax.ShapeDtypeStruct((out_sm, Ka, N // out_sm), BF)
        out_spec = pl.BlockSpec((None, tm, tn), lambda i, j, k: (j // per, i, j % per))
    return pl.pallas_call(
        body, name=name, grid=(Ka // tm, N // tn, nk),
        in_specs=[pl.BlockSpec((tk, tm), lambda i, j, k: (k, i)), pl.BlockSpec((tk, tn), lambda i, j, k: (k, j))],
        out_specs=out_spec, out_shape=out_shape, scratch_shapes=[pltpu.VMEM((tm, tn), F32)],
        compiler_params=_params(3),
    )(a, b)


def _relu2(v):
    r = jnp.maximum(v.astype(F32), 0.0)
    return r * r


def _rms_bwd_ep(dh, x, g, dres):
    rstd = lax.rsqrt(jnp.mean(x * x, axis=-1, keepdims=True) + NORM_EPS)
    xh = x * rstd
    dxh = dh * g
    dx = rstd * (dxh - xh * jnp.mean(dxh * xh, axis=-1, keepdims=True))
    tot = dres + dx
    return tot, tot, jnp.sum(dh * xh, axis=0, keepdims=True), jnp.sum(tot, axis=0, keepdims=True)


def _rms_fwd(name, x, g, tm=512, deps=()):
    T, Dm = x.shape

    def body(x_ref, g_ref, *rest):
        o_ref = rest[-1]
        xv = x_ref[...]
        rstd = lax.rsqrt(jnp.mean(xv * xv, axis=-1, keepdims=True) + NORM_EPS)
        o_ref[...] = (xv * rstd * g_ref[...]).astype(BF)

    return pl.pallas_call(
        body, name=name, grid=(T // tm,),
        in_specs=[pl.BlockSpec((tm, Dm), lambda i: (i, 0)), pl.BlockSpec((1, Dm), lambda i: (0, 0))]
        + [pl.BlockSpec(memory_space=pl.ANY)] * len(deps),
        out_specs=pl.BlockSpec((tm, Dm), lambda i: (i, 0)), out_shape=jax.ShapeDtypeStruct((T, Dm), BF),
        compiler_params=_params(1),
    )(x, g, *deps)


HEAD_COLS = 128


def _two_term_dot(v, m):
    hi = v.astype(BF)
    lo = (v - hi.astype(F32)).astype(BF)
    return _dot(hi, m, 1, 0) + _dot(lo, m, 1, 0)


def _head_sum(v, select):
    sel, sel_t = select
    return _two_term_dot(_two_term_dot(v, sel), sel_t)


def _head_select(n):
    sel = (np.arange(n)[:, None] // HEAD_DIM == np.arange(HEAD_COLS)[None, :]).astype(np.float32)
    return jnp.asarray(sel, dtype=BF), jnp.asarray(sel.T, dtype=BF)


def _qk_norm_fwd(qkv, qg_t, kg_t, tm=256):
    T = qkv.shape[0]
    scale = 1.0 / math.sqrt(HEAD_DIM)

    def body(x_ref, qg_ref, kg_ref, sq_ref, sqt_ref, sk_ref, skt_ref, q_ref, k_ref, v_ref):
        q = x_ref[:, pl.ds(0, ATTN_DIM)]
        rq = lax.rsqrt(_head_sum(q * q, (sq_ref[...], sqt_ref[...])) * (1.0 / HEAD_DIM) + NORM_EPS)
        q_ref[...] = (q * rq * qg_ref[...] * scale).astype(BF)
        k = x_ref[:, pl.ds(ATTN_DIM, KV_DIM)]
        rk = lax.rsqrt(_head_sum(k * k, (sk_ref[...], skt_ref[...])) * (1.0 / HEAD_DIM) + NORM_EPS)
        k_ref[...] = (k * rk * kg_ref[...]).astype(BF)
        v_ref[...] = x_ref[:, pl.ds(ATTN_DIM + KV_DIM, KV_DIM)].astype(BF)

    full = lambda shape: pl.BlockSpec(shape, lambda i: (0, 0))
    return pl.pallas_call(
        body, name="qk_norm_fwd", grid=(T // tm,),
        in_specs=[pl.BlockSpec((tm, QKV_DIM), lambda i: (i, 0)), full((1, ATTN_DIM)), full((1, KV_DIM)),
                  full((ATTN_DIM, HEAD_COLS)), full((HEAD_COLS, ATTN_DIM)), full((KV_DIM, HEAD_COLS)), full((HEAD_COLS, KV_DIM))],
        out_specs=[pl.BlockSpec((tm, ATTN_DIM), lambda i: (i, 0)), pl.BlockSpec((tm, KV_DIM), lambda i: (i, 0)),
                   pl.BlockSpec((tm, KV_DIM), lambda i: (i, 0))],
        out_shape=[jax.ShapeDtypeStruct((T, ATTN_DIM), BF), jax.ShapeDtypeStruct((T, KV_DIM), BF),
                   jax.ShapeDtypeStruct((T, KV_DIM), BF)],
        compiler_params=_params(1),
    )(qkv, qg_t, kg_t, *_head_select(ATTN_DIM), *_head_select(KV_DIM))


def _qk_norm_bwd(qkv, dqn, dkn, dv, qg_t, kg_t, tm=256):
    T = qkv.shape[0]

    def body(x_ref, dq_ref, dk_ref, dv_ref, qg_ref, kg_ref, sq_ref, sqt_ref, sk_ref, skt_ref,
             o_ref, db_ref, dqg_ref, dkg_ref):
        i = pl.program_id(0)

        def one(x, dy, g, select):
            r = lax.rsqrt(_head_sum(x * x, select) * (1.0 / HEAD_DIM) + NORM_EPS)
            xh = x * r
            dxh = dy * g
            dx = r * (dxh - xh * (_head_sum(dxh * xh, select) * (1.0 / HEAD_DIM)))
            return dx, jnp.sum(dy * xh, axis=0, keepdims=True)

        dq, dqg = one(x_ref[:, pl.ds(0, ATTN_DIM)], dq_ref[...], qg_ref[...], (sq_ref[...], sqt_ref[...]))
        dk, dkg = one(x_ref[:, pl.ds(ATTN_DIM, KV_DIM)], dk_ref[...], kg_ref[...], (sk_ref[...], skt_ref[...]))
        dvv = dv_ref[...]
        o_ref[:, pl.ds(0, ATTN_DIM)] = dq.astype(BF)
        o_ref[:, pl.ds(ATTN_DIM, KV_DIM)] = dk.astype(BF)
        o_ref[:, pl.ds(ATTN_DIM + KV_DIM, KV_DIM)] = dvv.astype(BF)
        sq, sk, sv = (jnp.sum(t, axis=0, keepdims=True) for t in (dq, dk, dvv))

        @pl.when(i == 0)
        def _():
            db_ref[:, pl.ds(0, ATTN_DIM)] = sq
            db_ref[:, pl.ds(ATTN_DIM, KV_DIM)] = sk
            db_ref[:, pl.ds(ATTN_DIM + KV_DIM, KV_DIM)] = sv
            dqg_ref[...] = dqg
            dkg_ref[...] = dkg

        @pl.when(i > 0)
        def _():
            db_ref[:, pl.ds(0, ATTN_DIM)] += sq
            db_ref[:, pl.ds(ATTN_DIM, KV_DIM)] += sk
            db_ref[:, pl.ds(ATTN_DIM + KV_DIM, KV_DIM)] += sv
            dqg_ref[...] += dqg
            dkg_ref[...] += dkg

    full = lambda shape: pl.BlockSpec(shape, lambda i: (0, 0))
    row = lambda n: pl.BlockSpec((tm, n), lambda i: (i, 0))
    return pl.pallas_call(
        body, name="qk_norm_bwd", grid=(T // tm,),
        in_specs=[row(QKV_DIM), row(ATTN_DIM), row(KV_DIM), row(KV_DIM), full((1, ATTN_DIM)), full((1, KV_DIM)),
                  full((ATTN_DIM, HEAD_COLS)), full((HEAD_COLS, ATTN_DIM)), full((KV_DIM, HEAD_COLS)), full((HEAD_COLS, KV_DIM))],
        out_specs=[row(QKV_DIM), full((1, QKV_DIM)), full((1, ATTN_DIM)), full((1, KV_DIM))],
        out_shape=[jax.ShapeDtypeStruct((T, QKV_DIM), BF), jax.ShapeDtypeStruct((1, QKV_DIM), F32),
                   jax.ShapeDtypeStruct((1, ATTN_DIM), F32), jax.ShapeDtypeStruct((1, KV_DIM), F32)],
        compiler_params=_params(1),
    )(qkv, dqn, dkn, dv, qg_t, kg_t, *_head_select(ATTN_DIM), *_head_select(KV_DIM))


ROWS = 64
COLS = 128


SUBLANES = 8
FIRST_TAP = HALO - (CONV_W - 1)


def _glu(a, g):
    return a.astype(F32) * jax.nn.sigmoid(g.astype(F32))


def _shifted(xe, s):
    return xe if s == 0 else pltpu.roll(xe, ROWS + HALO - s, axis=0)


def _conv_fwd(u, dw_pad, dw_b, ln_g, ln_b, tm=256):
    T = u.shape[0]
    Dm = D_MODEL
    hpt = tm // HALO

    def body(ac_ref, gc_ref, ap_ref, gp_ref, w_ref, wb_ref, lg_ref, lb_ref, cv_ref, s_ref, ext):
        i = pl.program_id(0)
        ext[pl.ds(0, HALO), :] = jnp.where(i > 0, _glu(ap_ref[...], gp_ref[...]), 0.0)
        ext[pl.ds(HALO, tm), :] = _glu(ac_ref[...], gc_ref[...])

        def rows(r, carry):
            r0 = pl.multiple_of(r * ROWS, ROWS)
            for c in range(Dm // COLS):
                cs = pl.ds(c * COLS, COLS)
                xe = ext[pl.ds(r0, ROWS + HALO), cs]
                acc = jnp.zeros((ROWS, COLS), F32)
                for s in range(SUBLANES):
                    xs = _shifted(xe, s)
                    for j in range(CONV_W):
                        off = FIRST_TAP + j
                        if off % SUBLANES == s:
                            acc = acc + xs[off - s:off - s + ROWS, :] * w_ref[pl.ds(j, 1), cs]
                cv_ref[pl.ds(r0, ROWS), cs] = acc + wb_ref[:, cs]
            return carry

        lax.fori_loop(0, tm // ROWS, rows, 0)
        cv = cv_ref[...]
        xc = cv - jnp.mean(cv, axis=-1, keepdims=True)
        y = xc * lax.rsqrt(jnp.mean(xc * xc, axis=-1, keepdims=True) + NORM_EPS) * lg_ref[...] + lb_ref[...]
        s_ref[...] = (y * jax.nn.sigmoid(y)).astype(BF)

    full = lambda shape: pl.BlockSpec(shape, lambda i: (0, 0))
    return pl.pallas_call(
        body, name="conv_fwd", grid=(T // tm,),
        in_specs=[pl.BlockSpec((tm, Dm), lambda i: (i, 0)), pl.BlockSpec((tm, Dm), lambda i: (i, 1)),
                  pl.BlockSpec((HALO, Dm), lambda i: (jnp.maximum(i * hpt - 1, 0), 0)),
                  pl.BlockSpec((HALO, Dm), lambda i: (jnp.maximum(i * hpt - 1, 0), 1)),
                  full((HALO, Dm)), full((1, Dm)), full((1, Dm)), full((1, Dm))],
        out_specs=[pl.BlockSpec((tm, Dm), lambda i: (i, 0)), pl.BlockSpec((tm, Dm), lambda i: (i, 0))],
        out_shape=[jax.ShapeDtypeStruct((T, Dm), F32), jax.ShapeDtypeStruct((T, Dm), BF)],
        scratch_shapes=[pltpu.VMEM((tm + HALO, Dm), F32)],
        compiler_params=_params(1),
    )(u, u, u, u, dw_pad, dw_b, ln_g, ln_b)


def _ln_silu_bwd_ep(ds, cv, lg, lb):
    xc = cv - jnp.mean(cv, axis=-1, keepdims=True)
    rstd = lax.rsqrt(jnp.mean(xc * xc, axis=-1, keepdims=True) + NORM_EPS)
    xh = xc * rstd
    y = xh * lg + lb
    sg = jax.nn.sigmoid(y)
    dy = ds * (sg * (1.0 + y * (1.0 - sg)))
    dxh = dy * lg
    dcv = rstd * (dxh - jnp.mean(dxh, axis=-1, keepdims=True) - xh * jnp.mean(dxh * xh, axis=-1, keepdims=True))
    return (dcv, jnp.sum(dy * xh, axis=0, keepdims=True), jnp.sum(dy, axis=0, keepdims=True),
            jnp.sum(dcv, axis=0, keepdims=True))


def _conv_bwd(u, dcv, dw_pad, tm=256):
    T = u.shape[0]
    Dm = D_MODEL
    hpt = tm // HALO
    last = T // HALO - 1
    nt = T // tm

    def body(ac_ref, gc_ref, ap_ref, gp_ref, dc_ref, dn_ref, w_ref, du_ref, db_ref, dw_ref, ext_g, ext_d):
        i = pl.program_id(0)
        ext_g[pl.ds(0, HALO), :] = jnp.where(i > 0, _glu(ap_ref[...], gp_ref[...]), 0.0)
        ext_g[pl.ds(HALO, tm), :] = _glu(ac_ref[...], gc_ref[...])
        ext_d[pl.ds(0, tm), :] = dc_ref[...]
        ext_d[pl.ds(tm, HALO), :] = jnp.where(i < nt - 1, dn_ref[...], 0.0)

        @pl.when(i == 0)
        def _():
            db_ref[...] = jnp.zeros_like(db_ref)
            dw_ref[...] = jnp.zeros_like(dw_ref)

        def rows(r, carry):
            r0 = pl.multiple_of(r * ROWS, ROWS)
            rs = pl.ds(r0, ROWS)
            for c in range(Dm // COLS):
                cs = pl.ds(c * COLS, COLS)
                cs2 = pl.ds(Dm + c * COLS, COLS)
                de = ext_d[pl.ds(r0, ROWS + HALO), cs]
                ge = ext_g[pl.ds(r0, ROWS + HALO), cs]
                dcur = de[0:ROWS, :]
                acc = jnp.zeros((ROWS, COLS), F32)
                for s in range(SUBLANES):
                    ds_, gs_ = _shifted(de, s), _shifted(ge, s)
                    for j in range(CONV_W):
                        off = CONV_W - 1 - j
                        if off % SUBLANES == s:
                            acc = acc + ds_[off - s:off - s + ROWS, :] * w_ref[pl.ds(j, 1), cs]
                        goff = FIRST_TAP + j
                        if goff % SUBLANES == s:
                            prod = dcur * gs_[goff - s:goff - s + ROWS, :]
                            dw_ref[j, :, cs] += jnp.sum(prod.reshape(ROWS // SUBLANES, SUBLANES, COLS), axis=0)
                a = ac_ref[rs, cs].astype(F32)
                sg = jax.nn.sigmoid(gc_ref[rs, cs].astype(F32))
                da = acc * sg
                dg = acc * a * sg * (1.0 - sg)
                du_ref[rs, cs] = da.astype(BF)
                du_ref[rs, cs2] = dg.astype(BF)
                db_ref[:, cs] += jnp.sum(da, axis=0, keepdims=True)
                db_ref[:, cs2] += jnp.sum(dg, axis=0, keepdims=True)
            return carry

        lax.fori_loop(0, tm // ROWS, rows, 0)

    return pl.pallas_call(
        body, name="conv_bwd", grid=(nt,),
        in_specs=[pl.BlockSpec((tm, Dm), lambda i: (i, 0)), pl.BlockSpec((tm, Dm), lambda i: (i, 1)),
                  pl.BlockSpec((HALO, Dm), lambda i: (jnp.maximum(i * hpt - 1, 0), 0)),
                  pl.BlockSpec((HALO, Dm), lambda i: (jnp.maximum(i * hpt - 1, 0), 1)),
                  pl.BlockSpec((tm, Dm), lambda i: (i, 0)),
                  pl.BlockSpec((HALO, Dm), lambda i: (jnp.minimum((i + 1) * hpt, last), 0)),
                  pl.BlockSpec((HALO, Dm), lambda i: (0, 0))],
        out_specs=[pl.BlockSpec((tm, 2 * Dm), lambda i: (i, 0)), pl.BlockSpec((1, 2 * Dm), lambda i: (0, 0)),
                   pl.BlockSpec((HALO, 8, Dm), lambda i: (0, 0, 0))],
        out_shape=[jax.ShapeDtypeStruct((T, 2 * Dm), BF), jax.ShapeDtypeStruct((1, 2 * Dm), F32),
                   jax.ShapeDtypeStruct((HALO, 8, Dm), F32)],
        scratch_shapes=[pltpu.VMEM((tm + HALO, Dm), F32), pltpu.VMEM((tm + HALO, Dm), F32)],
        compiler_params=_params(1),
    )(u, u, u, u, dcv, dcv, dw_pad)


def _bucket_table():
    q_loc = np.arange(BLOCK)[:, None]
    k_loc = np.arange(2 * BLOCK)[None, :]
    dist = q_loc + BLOCK - k_loc
    n = np.maximum(dist, 0)
    max_exact = REL_BUCKETS // 2
    large = max_exact + (np.log(np.maximum(n, 1).astype(np.float32) / max_exact)
                         / math.log(REL_MAX_DIST / max_exact) * (REL_BUCKETS - max_exact)).astype(np.int32)
    large = np.minimum(large, REL_BUCKETS - 1)
    bucket = np.where(n < max_exact, n, large).astype(np.int32)
    return jnp.asarray(np.where((dist >= 0) & (dist < BLOCK), bucket, -1).astype(np.int32))


def _bias_table(rel_bias, bucket):
    def body(rb_ref, bk_ref, o_ref):
        bk = bk_ref[...]
        for h in range(N_HEADS):
            acc = jnp.full((BLOCK, 2 * BLOCK), NEG_INF, F32)
            for b in range(REL_BUCKETS):
                acc = jnp.where(bk == b, rb_ref[b, h], acc)
            o_ref[h] = acc

    return pl.pallas_call(
        body, name="bias_table", out_shape=jax.ShapeDtypeStruct((N_HEADS, BLOCK, 2 * BLOCK), F32),
        in_specs=[pl.BlockSpec(memory_space=pltpu.SMEM), pl.BlockSpec(memory_space=pltpu.VMEM)],
        out_specs=pl.BlockSpec(memory_space=pltpu.VMEM),
    )(rel_bias, bucket)


def _bias_grad(dbias, bucket):
    def body(db_ref, bk_ref, o_ref):
        bk = bk_ref[...]
        for b in range(REL_BUCKETS):
            sel = bk == b
            for h in range(N_HEADS):
                o_ref[b, h] = jnp.sum(jnp.where(sel, db_ref[h], 0.0))

    return pl.pallas_call(
        body, name="bias_grad", out_shape=jax.ShapeDtypeStruct((REL_BUCKETS, N_HEADS), F32),
        in_specs=[pl.BlockSpec(memory_space=pltpu.VMEM), pl.BlockSpec(memory_space=pltpu.VMEM)],
        out_specs=pl.BlockSpec(memory_space=pltpu.SMEM),
    )(dbias, bucket)


GROUP_ROWS = GROUP * BLOCK


def _head_probs(qk, bias_h, sink, first):
    s = jnp.where(first, NEG_INF, qk + bias_h)
    m = jnp.maximum(jnp.max(s, axis=-1, keepdims=True), sink)
    p = jnp.exp(s - m)
    ps = jnp.exp(sink - m)
    inv = 1.0 / (jnp.sum(p, axis=-1, keepdims=True) + ps)
    return p * inv, ps * inv


def _band(prev_ref, cur_ref, g):
    hs = pl.ds(g * HEAD_DIM, HEAD_DIM)
    return jnp.concatenate([prev_ref[:, hs], cur_ref[:, hs]], axis=0)


def _stack_heads(ref, g):
    return jnp.concatenate([ref[:, pl.ds((g * GROUP + hh) * HEAD_DIM, HEAD_DIM)] for hh in range(GROUP)], axis=0)


def _unstack_heads(ref, g, stacked, dtype):
    for hh in range(GROUP):
        ref[:, pl.ds((g * GROUP + hh) * HEAD_DIM, HEAD_DIM)] = stacked[hh * BLOCK:(hh + 1) * BLOCK, :].astype(dtype)


def _first_mask(n):
    col = lax.broadcasted_iota(jnp.int32, (1, 2 * BLOCK), 1)
    return jnp.logical_and(n == 0, col < BLOCK)


def _head_rows(hh):
    return pl.ds(hh * BLOCK, BLOCK)


def _attn_fwd(qn, kn, vv, bias, sinks):
    T = qn.shape[0]
    nb = T // BLOCK

    def body(sk_ref, q_ref, kc_ref, kp_ref, vc_ref, vp_ref, b_ref, o_ref, qk_buf, p_buf):
        first = _first_mask(pl.program_id(0))
        for g in range(N_KV):
            k = _band(kp_ref, kc_ref, g)
            v = _band(vp_ref, vc_ref, g)
            qk_buf[...] = _dot(_stack_heads(q_ref, g), k, 1, 1)
            for hh in range(GROUP):
                h = g * GROUP + hh
                pn, _ = _head_probs(qk_buf[_head_rows(hh), :], b_ref[h], sk_ref[h], first)
                p_buf[_head_rows(hh), :] = pn.astype(BF)
            _unstack_heads(o_ref, g, _dot(p_buf[...], v, 1, 0), BF)

    cur = lambda n: (n, 0)
    prev = lambda n: (jnp.maximum(n - 1, 0), 0)
    return pl.pallas_call(
        body, name="attn_fwd", grid=(nb,),
        in_specs=[pl.BlockSpec(memory_space=pltpu.SMEM), pl.BlockSpec((BLOCK, ATTN_DIM), cur),
                  pl.BlockSpec((BLOCK, KV_DIM), cur), pl.BlockSpec((BLOCK, KV_DIM), prev),
                  pl.BlockSpec((BLOCK, KV_DIM), cur), pl.BlockSpec((BLOCK, KV_DIM), prev),
                  pl.BlockSpec((N_HEADS, BLOCK, 2 * BLOCK), lambda n: (0, 0, 0))],
        out_specs=pl.BlockSpec((BLOCK, ATTN_DIM), cur), out_shape=jax.ShapeDtypeStruct((T, ATTN_DIM), BF),
        scratch_shapes=[pltpu.VMEM((GROUP_ROWS, 2 * BLOCK), F32), pltpu.VMEM((GROUP_ROWS, 2 * BLOCK), BF)],
        compiler_params=_params(1),
    )(sinks, qn, kn, kn, vv, vv, bias)


def _attn_bwd(qn, kn, vv, bias, sinks, do):
    T = qn.shape[0]
    nb = T // BLOCK
    scale = 1.0 / math.sqrt(HEAD_DIM)

    def body(sk_ref, q_ref, kc_ref, kp_ref, vc_ref, vp_ref, b_ref, do_ref,
             dq_ref, dk_ref, dv_ref, db_ref, dsk_ref, dk_full, dv_full, dk_carry, dv_carry, qk_buf, dp_buf, p_buf, ds_buf):
        n = pl.program_id(0)

        @pl.when(n == 0)
        def _():
            db_ref[...] = jnp.zeros_like(db_ref)
            dk_carry[...] = jnp.zeros_like(dk_carry)
            dv_carry[...] = jnp.zeros_like(dv_carry)
            for h in range(N_HEADS):
                dsk_ref[h] = 0.0

        @pl.when(n < nb)
        def _():
            first = _first_mask(n)
            for g in range(N_KV):
                k = _band(kp_ref, kc_ref, g)
                v = _band(vp_ref, vc_ref, g)
                q = _stack_heads(q_ref, g)
                dout = _stack_heads(do_ref, g)
                qk_buf[...] = _dot(q, k, 1, 1)
                dp_buf[...] = _dot(dout, v, 1, 1)
                for hh in range(GROUP):
                    h = g * GROUP + hh
                    rows = _head_rows(hh)
                    pn, psink = _head_probs(qk_buf[rows, :], b_ref[h], sk_ref[h], first)
                    dp = dp_buf[rows, :]
                    delta = jnp.sum(pn * dp, axis=-1, keepdims=True)
                    ds = pn * (dp - delta)
                    dsk_ref[h] += -jnp.sum(psink * delta)
                    db_ref[h] += ds
                    ds_buf[rows, :] = ds.astype(BF)
                    p_buf[rows, :] = pn.astype(BF)
                dsb = ds_buf[...]
                _unstack_heads(dq_ref, g, _dot(dsb, k, 1, 0) * scale, F32)
                gs = pl.ds(g * HEAD_DIM, HEAD_DIM)
                dk_full[:, gs] = _dot(dsb, q, 0, 0)
                dv_full[:, gs] = _dot(p_buf[...], dout, 0, 0)

        @pl.when(n == nb)
        def _():
            dk_full[...] = jnp.zeros_like(dk_full)
            dv_full[...] = jnp.zeros_like(dv_full)

        dk_ref[...] = dk_carry[...] + dk_full[pl.ds(0, BLOCK), :]
        dv_ref[...] = dv_carry[...] + dv_full[pl.ds(0, BLOCK), :]
        dk_carry[...] = dk_full[pl.ds(BLOCK, BLOCK), :]
        dv_carry[...] = dv_full[pl.ds(BLOCK, BLOCK), :]

    cur = lambda n: (jnp.minimum(n, nb - 1), 0)
    prev = lambda n: (jnp.maximum(jnp.minimum(n, nb - 1) - 1, 0), 0)
    out_kv = lambda n: (jnp.maximum(n - 1, 0), 0)
    return pl.pallas_call(
        body, name="attn_bwd", grid=(nb + 1,),
        in_specs=[pl.BlockSpec(memory_space=pltpu.SMEM), pl.BlockSpec((BLOCK, ATTN_DIM), cur),
                  pl.BlockSpec((BLOCK, KV_DIM), cur), pl.BlockSpec((BLOCK, KV_DIM), prev),
                  pl.BlockSpec((BLOCK, KV_DIM), cur), pl.BlockSpec((BLOCK, KV_DIM), prev),
                  pl.BlockSpec((N_HEADS, BLOCK, 2 * BLOCK), lambda n: (0, 0, 0)),
                  pl.BlockSpec((BLOCK, ATTN_DIM), cur)],
        out_specs=[pl.BlockSpec((BLOCK, ATTN_DIM), cur), pl.BlockSpec((BLOCK, KV_DIM), out_kv),
                   pl.BlockSpec((BLOCK, KV_DIM), out_kv),
                   pl.BlockSpec((N_HEADS, BLOCK, 2 * BLOCK), lambda n: (0, 0, 0)),
                   pl.BlockSpec(memory_space=pltpu.SMEM)],
        out_shape=[jax.ShapeDtypeStruct((T, ATTN_DIM), F32), jax.ShapeDtypeStruct((T, KV_DIM), F32),
                   jax.ShapeDtypeStruct((T, KV_DIM), F32),
                   jax.ShapeDtypeStruct((N_HEADS, BLOCK, 2 * BLOCK), F32), jax.ShapeDtypeStruct((N_HEADS,), F32)],
        scratch_shapes=[pltpu.VMEM((2 * BLOCK, KV_DIM), F32), pltpu.VMEM((2 * BLOCK, KV_DIM), F32),
                        pltpu.VMEM((BLOCK, KV_DIM), F32), pltpu.VMEM((BLOCK, KV_DIM), F32),
                        pltpu.VMEM((GROUP_ROWS, 2 * BLOCK), F32), pltpu.VMEM((GROUP_ROWS, 2 * BLOCK), F32),
                        pltpu.VMEM((GROUP_ROWS, 2 * BLOCK), BF), pltpu.VMEM((GROUP_ROWS, 2 * BLOCK), BF)],
        compiler_params=_params(1),
    )(sinks, qn, kn, kn, vv, vv, bias, do)


def _coords():
    return lax.axis_index("x"), lax.axis_index("y"), lax.axis_index("c")


def _gather8(name, v, with_sum):
    R = v.shape[0]

    def body(v_ref, all_ref, *rest):
        sum_ref = rest[0] if with_sum else None
        send_sems, recv_sems, local_sem = rest[-3:]
        x, y, c = _coords()
        me = 4 * x + 2 * y + c
        local = pltpu.make_async_copy(v_ref, all_ref.at[me], local_sem)
        local.start()
        sends = []
        for k in range(1, 8):
            peer = (x ^ (k >> 2), y ^ ((k >> 1) & 1), c ^ (k & 1))
            cp = pltpu.make_async_remote_copy(src_ref=v_ref, dst_ref=all_ref.at[me], send_sem=send_sems.at[k - 1],
                                              recv_sem=recv_sems.at[k - 1], device_id=peer, device_id_type=MESH)
            cp.start()
            sends.append(cp)
        for k in range(1, 8):
            peer = (x ^ (k >> 2), y ^ ((k >> 1) & 1), c ^ (k & 1))
            pltpu.make_async_remote_copy(src_ref=v_ref, dst_ref=all_ref.at[me ^ k], send_sem=send_sems.at[k - 1],
                                         recv_sem=recv_sems.at[k - 1], device_id=peer, device_id_type=MESH).wait_recv()
        for cp in sends:
            cp.wait_send()
        local.wait()
        if with_sum:
            tot = all_ref[0]
            for d in range(1, 8):
                tot = tot + all_ref[d]
            sum_ref[...] = tot

    out_shape = [jax.ShapeDtypeStruct((8, R, LANES), F32)]
    if with_sum:
        out_shape.append(jax.ShapeDtypeStruct((R, LANES), F32))
    vm = pl.BlockSpec(memory_space=pltpu.VMEM)
    return pl.pallas_call(
        body, name=name, out_shape=out_shape, in_specs=[vm], out_specs=[vm] * len(out_shape),
        scratch_shapes=[pltpu.SemaphoreType.DMA((7,)), pltpu.SemaphoreType.DMA((7,)), pltpu.SemaphoreType.DMA],
    )(v)


CHIP_FLIPS = ((1, 0), (0, 1), (1, 1))


HBM_SPEC = pl.BlockSpec(memory_space=pltpu.HBM)
SEM_SPEC = pl.BlockSpec(memory_space=pltpu.SEMAPHORE)
ANY_SPEC = pl.BlockSpec(memory_space=pl.ANY)
DATAFLOW = pltpu.SideEffectType.DATAFLOW_SIDE_EFFECTING


def _chip_copy(land, sems, idx, slot_src, slot_dst, peer):
    send_sems, recv_sems = sems
    return pltpu.make_async_remote_copy(src_ref=land.at[slot_src], dst_ref=land.at[slot_dst], send_sem=send_sems.at[idx],
                                        recv_sem=recv_sems.at[idx], device_id=peer, device_id_type=MESH)


def _gather_start(stacks, groups, after):
    n = len(stacks)
    ng = len(groups)
    after = tuple(after)

    def body(*refs):
        lands = refs[:n]
        first = n + len(after)
        sems = [(refs[first + 2 * g], refs[first + 2 * g + 1]) for g in range(ng)]
        token = refs[-1]
        x, y, c = _coords()
        s = 2 * x + y
        for g, members in enumerate(groups):
            for i, t in enumerate(members):
                for j, (fx, fy) in enumerate(CHIP_FLIPS):
                    _chip_copy(lands[t], sems[g], 3 * i + j, s, s, (x ^ fx, y ^ fy, c)).start()
        token[...] = jnp.zeros_like(token)

    out_shape = []
    for members in groups:
        out_shape += [pltpu.SemaphoreType.DMA((3 * len(members),))] * 2
    out_shape += [pltpu.HBM(w.shape, w.dtype) for w in stacks]
    out_shape.append(jax.ShapeDtypeStruct((8, 128), F32))
    res = pl.pallas_call(
        body, name="gather_start", out_shape=out_shape, in_specs=[HBM_SPEC] * n + [ANY_SPEC] * len(after),
        out_specs=[SEM_SPEC] * (2 * ng) + [HBM_SPEC] * n + [pl.BlockSpec(memory_space=pltpu.VMEM)],
        input_output_aliases={t: 2 * ng + t for t in range(n)},
        compiler_params=pltpu.CompilerParams(has_side_effects=DATAFLOW),
    )(*[pltpu.with_memory_space_constraint(w, pltpu.HBM) for w in stacks], *after)
    sems = [(res[2 * g], res[2 * g + 1]) for g in range(ng)]
    return sems, list(res[2 * ng:2 * ng + n]), res[-1]


def _gather_wait(name, stacks, sems, after):
    n = len(stacks)
    after = tuple(after)

    def body(*refs):
        lands = refs[:n]
        group_sems = (refs[n], refs[n + 1])
        x, y, c = _coords()
        s = 2 * x + y
        for i in range(n):
            for j, (fx, fy) in enumerate(CHIP_FLIPS):
                cp = _chip_copy(lands[i], group_sems, 3 * i + j, s, 2 * (x ^ fx) + (y ^ fy), (x ^ fx, y ^ fy, c))
                cp.wait_send()
                cp.wait_recv()

    return pl.pallas_call(
        body, name=name, out_shape=[pltpu.HBM(w.shape, w.dtype) for w in stacks],
        in_specs=[HBM_SPEC] * n + [SEM_SPEC, SEM_SPEC] + [ANY_SPEC] * len(after), out_specs=[HBM_SPEC] * n,
        input_output_aliases={t: t for t in range(n)},
        compiler_params=pltpu.CompilerParams(has_side_effects=DATAFLOW),
    )(*stacks, sems[0], sems[1], *after)


N_PEERS = 7


def _peer(x, y, c, k):
    return x ^ (k >> 2), y ^ ((k >> 1) & 1), c ^ (k & 1)


def _reduce_copy(grad, land, sems, idx, x, y, c, k):
    px, py, pc = _peer(x, y, c, k)
    rh = grad.shape[1] // 2
    return pltpu.make_async_remote_copy(src_ref=grad.at[2 * px + py, pl.ds(pc * rh, rh), :], dst_ref=land.at[k - 1],
                                        send_sem=sems[0].at[idx], recv_sem=sems[1].at[idx], device_id=(px, py, pc),
                                        device_id_type=MESH)


def _reduce_start(name, grads):
    n = len(grads)

    def body(*refs):
        src, lands, sems, token = refs[:n], refs[n:2 * n], (refs[2 * n], refs[2 * n + 1]), refs[-1]
        x, y, c = _coords()
        for t in range(n):
            for k in range(1, N_PEERS + 1):
                _reduce_copy(src[t], lands[t], sems, N_PEERS * t + k - 1, x, y, c, k).start()
        token[...] = jnp.zeros_like(token)

    lands = [lax.empty((N_PEERS, g.shape[1] // 2, g.shape[2]), g.dtype) for g in grads]
    out_shape = [pltpu.SemaphoreType.DMA((N_PEERS * n,))] * 2
    out_shape += [pltpu.HBM(a.shape, a.dtype) for a in list(grads) + lands]
    out_shape.append(jax.ShapeDtypeStruct((8, 128), F32))
    res = pl.pallas_call(
        body, name=name, out_shape=out_shape, in_specs=[HBM_SPEC] * (2 * n),
        out_specs=[SEM_SPEC] * 2 + [HBM_SPEC] * (2 * n) + [pl.BlockSpec(memory_space=pltpu.VMEM)],
        input_output_aliases={t: 2 + t for t in range(2 * n)},
        compiler_params=pltpu.CompilerParams(has_side_effects=DATAFLOW),
    )(*[pltpu.with_memory_space_constraint(a, pltpu.HBM) for a in list(grads) + lands])
    return (res[0], res[1]), list(res[2:2 + n]), list(res[2 + n:2 + 2 * n]), res[-1]


def _reduce_wait(name, grads, lands, sems, after):
    n = len(grads)
    after = tuple(after)

    def body(*refs):
        src, dst, group_sems = refs[:n], refs[n:2 * n], (refs[2 * n], refs[2 * n + 1])
        x, y, c = _coords()
        for t in range(n):
            for k in range(1, N_PEERS + 1):
                cp = _reduce_copy(src[t], dst[t], group_sems, N_PEERS * t + k - 1, x, y, c, k)
                cp.wait_send()
                cp.wait_recv()

    res = pl.pallas_call(
        body, name=name, out_shape=[pltpu.HBM(a.shape, a.dtype) for a in list(grads) + list(lands)],
        in_specs=[HBM_SPEC] * (2 * n) + [SEM_SPEC, SEM_SPEC] + [ANY_SPEC] * len(after), out_specs=[HBM_SPEC] * (2 * n),
        input_output_aliases={t: t for t in range(2 * n)},
        compiler_params=pltpu.CompilerParams(has_side_effects=DATAFLOW),
    )(*grads, *lands, sems[0], sems[1], *after)
    return list(res[:n]), list(res[n:])


def _join_halves(name, halves):
    n = len(halves)

    def body(*refs):
        src, dst = refs[:n], refs[n:2 * n]
        send_sems, recv_sems = refs[2 * n:]
        x, y, c = _coords()
        cps = []
        for t in range(n):
            cp = pltpu.make_async_remote_copy(src_ref=src[t], dst_ref=dst[t], send_sem=send_sems.at[t],
                                              recv_sem=recv_sems.at[t], device_id=(x, y, 1 - c), device_id_type=MESH)
            cp.start()
            cps.append(cp)
        for cp in cps:
            cp.wait()

    anyspec = pl.BlockSpec(memory_space=pl.ANY)
    return pl.pallas_call(
        body, name=name, out_shape=[jax.ShapeDtypeStruct(h.shape, h.dtype) for h in halves],
        in_specs=[anyspec] * n, out_specs=[anyspec] * n,
        scratch_shapes=[pltpu.SemaphoreType.DMA((n,)), pltpu.SemaphoreType.DMA((n,))],
    )(*halves)


def _row_block(rows):
    for rb in (512, 256, 128, 64, 32, 16):
        if rows % rb == 0:
            return rb
    raise ValueError(rows)


def _sum_devices(name, grad, land, place):
    S, R, C = grad.shape
    rh = R // 2
    rb = _row_block(rh)
    nbh = rh // rb

    def body(place_ref, g_ref, l_ref, o_ref):
        tot = g_ref[...].astype(F32)
        for k in range(N_PEERS):
            tot = tot + l_ref[k].astype(F32)
        o_ref[...] = tot

    return pl.pallas_call(
        body, name=name,
        grid_spec=pltpu.PrefetchScalarGridSpec(
            num_scalar_prefetch=1, grid=(nbh,),
            in_specs=[pl.BlockSpec((None, rb, C), lambda r, place: (place[0], place[1] * nbh + r, 0)),
                      pl.BlockSpec((N_PEERS, rb, C), lambda r, place: (0, r, 0))],
            out_specs=pl.BlockSpec((rb, C), lambda r, place: (r, 0))),
        out_shape=jax.ShapeDtypeStruct((rh, C), F32), compiler_params=_params(1),
    )(place, grad, land)


def _adamw_math(w, g, m, v):
    m2 = ADAM_B1 * m + (1.0 - ADAM_B1) * g
    v2 = ADAM_B2 * v + (1.0 - ADAM_B2) * (g * g)
    m_hat = m2 / (1.0 - ADAM_B1 ** ADAM_STEP)
    v_hat = v2 / (1.0 - ADAM_B2 ** ADAM_STEP)
    delta = -ADAM_LR * (m_hat / (jnp.sqrt(v_hat) + ADAM_EPS) + ADAM_WD * w)
    return delta, m2, v2


def _adamw(name, w, m, v, gs):
    L, R, C = w.shape
    Rh = R // 2
    rb = _row_block(Rh)
    nbh = Rh // rb
    assert len(gs) == L

    def body(core_ref, w_ref, m_ref, v_ref, *rest):
        g_refs, (go_ref, d_ref, m2_ref, v2_ref) = rest[:2 * L], rest[2 * L:]
        layer, half = pl.program_id(0), pl.program_id(1)
        mine = half == core_ref[0]
        g = jnp.where(mine, g_refs[0][...], g_refs[1][...])
        for t in range(1, L):
            g = jnp.where(layer == t, jnp.where(mine, g_refs[2 * t][...], g_refs[2 * t + 1][...]), g)
        delta, m2, v2 = _adamw_math(w_ref[...], g, m_ref[...], v_ref[...])
        go_ref[...] = g
        d_ref[...] = delta
        m2_ref[...] = m2
        v2_ref[...] = v2

    wspec = pl.BlockSpec((None, rb, C), lambda l, h, r, core: (l, h * nbh + r, 0))
    gspec = pl.BlockSpec((rb, C), lambda l, h, r, core: (r, 0))
    return pl.pallas_call(
        body, name=name,
        grid_spec=pltpu.PrefetchScalarGridSpec(num_scalar_prefetch=1, grid=(L, 2, nbh),
                                               in_specs=[wspec] * 3 + [gspec] * (2 * L), out_specs=[wspec] * 4),
        out_shape=[jax.ShapeDtypeStruct((L, R, C), F32)] * 4, compiler_params=_params(3),
    )(lax.axis_index("c").astype(jnp.int32).reshape(1), w, m, v, *[g for pair in gs for g in pair])


def _adamw_small(w, g, m, v):
    def body(w_ref, g_ref, m_ref, v_ref, d_ref, m2_ref, v2_ref):
        delta, m2, v2 = _adamw_math(w_ref[...], g_ref[...], m_ref[...], v_ref[...])
        d_ref[...] = delta
        m2_ref[...] = m2
        v2_ref[...] = v2

    return pl.pallas_call(body, name="adamw_small", out_shape=[jax.ShapeDtypeStruct(w.shape, F32)] * 3)(w, g, m, v)


def _packed_rows(shape):
    c = shape[-1]
    return (int(np.prod(shape)) // c) * -(-c // LANES)


def _pack(arrays):
    total = sum(_packed_rows(a.shape) for a in arrays)
    total += -total % 8
    buf, r0 = None, 0
    for a in arrays:
        a = a.astype(F32).reshape(-1, a.shape[-1])
        r, c = a.shape
        k = -(-c // LANES)
        a = jnp.pad(a, ((0, 0), (0, k * LANES - c))).reshape(r * k, LANES)
        a = jnp.pad(a, ((r0, total - r0 - r * k), (0, 0)))
        buf = a if buf is None else buf + a
        r0 += r * k
    return buf


def _unpack(buf, shapes):
    out, r0 = [], 0
    for shp in shapes:
        c = shp[-1]
        rows = _packed_rows(shp)
        out.append(buf[r0:r0 + rows].reshape(-1, -(-c // LANES) * LANES)[:, :c].reshape(shp))
        r0 += rows
    return out


def _mlp_fwd(tag, x, g, w_up_sm, w_down):
    h = _rms_fwd(f"mlp{tag}_norm", x, g)
    (up,) = _mm(f"mlp{tag}_up", h, w_up_sm, nt=False, b_sm=True, tm=1024, tn=1024, rows=256,
                ep_fn=lambda acc: (acc,), outs=(("tile", BF),))
    return h, up


RMS_BWD_OUTS = (("tile", F32), ("tile", BF), ("colsum", F32), ("colsum", F32))


def _mlp_bwd(tag, dy, dy_bf, x, g, h, up, w_up_sm, w_down, place):
    (dup,) = _mm(f"mlp{tag}_dup", dy_bf, w_down, nt=True, tm=1024, tn=1024, rows=256, ep_in=((up, "tile"),),
                 ep_fn=lambda acc, u: (acc * (2.0 * jnp.maximum(u.astype(F32), 0.0)),), outs=(("tile", BF),))
    dw_down = _mm_tn(f"mlp{tag}_dw_down", up, dy_bf, tm=1024, tn=1024, tk=2048, a_fn=_relu2)
    dw_up = _mm_tn(f"mlp{tag}_dw_up", h, dup, tm=1024, tn=1024, tk=2048, out_sm=N_SHARD)
    red = _Reduction(f"mlp{tag}", [dw_up, dw_down.reshape(N_SHARD, D_FF // N_SHARD, D_MODEL)], place)
    dx, dx_bf, dg, dx_sum = _mm(f"mlp{tag}_dx", dup, w_up_sm, nt=True, b_sm=True, tm=512, tn=1024, rows=256,
                                ep_in=((x, "tile"), (g, "row"), (dy, "tile")), ep_fn=_rms_bwd_ep, outs=RMS_BWD_OUTS,
                                deps=(red.token,))
    return dx, dx_bf, dg, dx_sum, red


class _Reduction:
    def __init__(self, tag, grads, place):
        self.tag, self.place = tag, place
        self.sems, self.grads, self.lands, self.token = _reduce_start(f"reduce_start_{tag}", grads)

    def finish(self, after):
        grads, lands = _reduce_wait(f"reduce_wait_{self.tag}", self.grads, self.lands, self.sems, after)
        halves = [_sum_devices(f"reduce_sum_{self.tag}{i}", g, l, self.place) for i, (g, l) in enumerate(zip(grads, lands))]
        return list(zip(halves, _join_halves(f"join_halves_{self.tag}", halves)))


def kernel(x, conv_norm_g, conv_w_in, conv_b_in, conv_dw, conv_dw_b, conv_ln_g, conv_ln_b, conv_w_out, conv_b_out, attn_norm_g, w_qkv, b_qkv, q_norm_g, k_norm_g, sinks, w_o, b_o, rel_bias, mlp_norm_g, w_up, w_down, loss_target, m_conv_norm_g, m_conv_w_in, m_conv_b_in, m_conv_dw, m_conv_dw_b, m_conv_ln_g, m_conv_ln_b, m_conv_w_out, m_conv_b_out, m_attn_norm_g, m_w_qkv, m_b_qkv, m_q_norm_g, m_k_norm_g, m_sinks, m_w_o, m_b_o, m_rel_bias, m_mlp_norm_g, m_w_up, m_w_down, v_conv_norm_g, v_conv_w_in, v_conv_b_in, v_conv_dw, v_conv_dw_b, v_conv_ln_g, v_conv_ln_b, v_conv_w_out, v_conv_b_out, v_attn_norm_g, v_w_qkv, v_b_qkv, v_q_norm_g, v_k_norm_g, v_sinks, v_w_o, v_b_o, v_rel_bias, v_mlp_norm_g, v_w_up, v_w_down):
    Dm = D_MODEL
    x2d = x[0]
    tgt = loss_target[0]
    T = x2d.shape[0]
    shard = 2 * lax.axis_index("x") + lax.axis_index("y")

    sharded_small = [conv_dw[0], attn_norm_g, b_qkv, b_o]
    (gathered,) = _gather8("gather_small_weights", _pack(sharded_small), with_sum=False)
    chips = [_unpack(gathered[2 * s], [a.shape for a in sharded_small]) for s in range(N_SHARD)]
    dw_f, attn_norm_f, b_qkv_f, b_o_f = (jnp.concatenate([chips[s][t] for s in range(N_SHARD)], axis=-1)
                                         for t in range(len(sharded_small)))
    dw_pad = jnp.pad(dw_f, ((0, HALO - CONV_W), (0, 0)))

    big = [conv_w_in[0], conv_w_out[0], w_qkv[0], w_o[0], w_up[0], w_up[1], w_down[0], w_down[1]]
    stacks = [lax.dynamic_update_slice(jnp.zeros((N_SHARD,) + w.shape, BF), w.astype(BF)[None], (shard, 0, 0))
              for w in big]
    groups = ((0, 1), (4, 6), (2, 3), (5, 7))
    gather_sems, stacks, gather_token = _gather_start(stacks, groups, after=(gathered,))

    def gathered_group(g, name, after):
        return _gather_wait(name, [stacks[t] for t in groups[g]], gather_sems[g], after)

    bucket = _bucket_table()
    bias = _bias_table(rel_bias, bucket)

    h0 = _rms_fwd("conv_norm", x2d, conv_norm_g, deps=(gather_token,))
    w_in_sm, g_out = gathered_group(0, "gather_wait_conv", (h0, dw_pad, bias))
    w_out_f = g_out.reshape(Dm, Dm)
    (u,) = _mm("conv_in", h0, w_in_sm, nt=False, b_sm=True, tm=1024, tn=512, rows=256, ep_in=((conv_b_in, "row"),),
               ep_fn=lambda acc, b: (acc + b,), outs=(("tile", BF),))
    cv, s_act = _conv_fwd(u, dw_pad, conv_dw_b, conv_ln_g, conv_ln_b)
    (x1,) = _mm("conv_out", s_act, w_out_f, nt=False, tm=1024, tn=1024, rows=256,
                ep_in=((conv_b_out, "row"), (x2d, "tile")), ep_fn=lambda acc, b, r: (acc + b + r,),
                outs=(("tile", F32),))

    g_up0, g_down0 = gathered_group(1, "gather_wait_mlp0", (x1,))
    w_up_sm = [g_up0, None]
    w_down_f = [g_down0.reshape(D_FF, Dm), None]
    h1, up0 = _mlp_fwd(0, x1, mlp_norm_g[0:1], w_up_sm[0], w_down_f[0])
    (x2,) = _mm("mlp0_down", up0, w_down_f[0], nt=False, tm=512, tn=1024, rows=256, a_fn=_relu2,
                ep_in=((x1, "tile"),), ep_fn=lambda acc, r: (acc + r,), outs=(("tile", F32),))

    g_qkv, g_o = gathered_group(2, "gather_wait_attn", (x2,))
    w_qkv_f = jnp.transpose(g_qkv, (1, 0, 2)).reshape(Dm, QKV_DIM)
    w_o_f = g_o.reshape(ATTN_DIM, Dm)
    h2 = _rms_fwd("attn_norm", x2, attn_norm_f)
    (qkv,) = _mm("attn_qkv", h2, w_qkv_f, nt=False, tm=1024, tn=QKV_DIM, rows=256, ep_in=((b_qkv_f, "row"),),
                 ep_fn=lambda acc, b: (acc + b,), outs=(("tile", F32),))
    qg_t = jnp.tile(q_norm_g, (1, N_HEADS))
    kg_t = jnp.tile(k_norm_g, (1, N_KV))
    qn, kn, vv = _qk_norm_fwd(qkv, qg_t, kg_t)
    sinks1 = sinks[0]
    att = _attn_fwd(qn, kn, vv, bias, sinks1)
    (x3,) = _mm("attn_out", att, w_o_f, nt=False, tm=1024, tn=1024, rows=256,
                ep_in=((b_o_f, "row"), (x2, "tile")), ep_fn=lambda acc, b, r: (acc + b + r,), outs=(("tile", F32),))

    g_up1, g_down1 = gathered_group(3, "gather_wait_mlp1", (x3,))
    w_up_sm[1] = g_up1
    w_down_f[1] = g_down1.reshape(D_FF, Dm)
    h3, up1 = _mlp_fwd(1, x3, mlp_norm_g[1:2], w_up_sm[1], w_down_f[1])

    def loss_ep(acc, r, t):
        diff = acc + r - t
        dy = diff * (1.0 / Dm)
        return dy, dy, jnp.sum(diff * diff, axis=0, keepdims=True)

    dy, dy_bf, sq = _mm("mlp1_down_loss", up1, w_down_f[1], nt=False, tm=512, tn=1024, rows=256, a_fn=_relu2,
                        ep_in=((x3, "tile"), (tgt, "tile")), ep_fn=loss_ep,
                        outs=(("tile", F32), ("tile", BF), ("colsum", F32)))
    loss = lax.psum(0.5 * jnp.sum(sq) * (1.0 / Dm), ("x", "y", "c"))

    place = jnp.stack([shard, lax.axis_index("c")]).astype(jnp.int32)
    dx3, dx3_bf, dg_mlp1, db_o, red_mlp1 = _mlp_bwd(1, dy, dy_bf, x3, mlp_norm_g[1:2], h3, up1, w_up_sm[1],
                                                    w_down_f[1], place)

    ident = lambda acc: (acc,)
    (datt,) = _mm("attn_dout", dx3_bf, w_o_f, nt=True, tm=1024, tn=1024, rows=256, ep_fn=ident, outs=(("tile", BF),))
    dw_o = _mm_tn("attn_dw_o", att, dx3_bf, tm=1024, tn=1024, tk=2048)
    dqn, dkn, dvv, dbias, dsinks = _attn_bwd(qn, kn, vv, bias, sinks1, datt)
    (r_up1, r_down1) = red_mlp1.finish((dqn,))
    drel = _bias_grad(dbias, bucket)
    dqkv, db_qkv, dqg_t, dkg_t = _qk_norm_bwd(qkv, dqn, dkn, dvv, qg_t, kg_t)
    dw_qkv = _mm_tn("attn_dw_qkv", h2, dqkv, tm=1024, tn=QKV_DIM, tk=2048)
    red_attn = _Reduction("attn", [jnp.transpose(dw_qkv.reshape(Dm, N_SHARD, QKV_DIM // N_SHARD), (1, 0, 2)),
                                   dw_o.reshape(N_SHARD, ATTN_DIM // N_SHARD, Dm)], place)
    dx2, dx2_bf, dg_attn, _ = _mm("attn_dx", dqkv, w_qkv_f, nt=True, tm=512, tn=1024, rows=256,
                                  ep_in=((x2, "tile"), (attn_norm_f, "row"), (dx3, "tile")), ep_fn=_rms_bwd_ep,
                                  outs=RMS_BWD_OUTS, deps=(red_attn.token,))

    dx1, dx1_bf, dg_mlp0, db_out, red_mlp0 = _mlp_bwd(0, dx2, dx2_bf, x1, mlp_norm_g[0:1], h1, up0, w_up_sm[0],
                                                      w_down_f[0], place)
    (r_qkv, r_o) = red_attn.finish((dx1,))

    dcv, dln_g, dln_b, ddw_b = _mm("conv_ds", dx1_bf, w_out_f, nt=True, tm=512, tn=1024, rows=256,
                                   ep_in=((cv, "tile"), (conv_ln_g, "row"), (conv_ln_b, "row")),
                                   ep_fn=_ln_silu_bwd_ep,
                                   outs=(("tile", F32), ("colsum", F32), ("colsum", F32), ("colsum", F32)))
    dw_out = _mm_tn("conv_dw_out", s_act, dx1_bf, tm=1024, tn=1024, tk=2048)
    du, db_in, ddw8 = _conv_bwd(u, dcv, dw_pad)
    (r_up0, r_down0) = red_mlp0.finish((du,))
    dw_in = _mm_tn("conv_dw_in", h0, du, tm=1024, tn=512, tk=2048, out_sm=N_SHARD)
    red_conv = _Reduction("conv", [dw_in, dw_out.reshape(N_SHARD, Dm // N_SHARD, Dm)], place)
    def first_layer_ep(*args):
        tot, _, dg, _ = _rms_bwd_ep(*args)
        return tot, dg

    gx, dg_conv = _mm("conv_dx", du, w_in_sm, nt=True, b_sm=True, tm=512, tn=1024, rows=256,
                      ep_in=((x2d, "tile"), (conv_norm_g, "row"), (dx1, "tile")), ep_fn=first_layer_ep,
                      outs=(("tile", F32), ("colsum", F32)), deps=(red_conv.token,))
    (r_in, r_out) = red_conv.finish((gx,))

    big_out = {}
    for nm, w, m, v, gs in (("conv_w_in", conv_w_in, m_conv_w_in, v_conv_w_in, (r_in,)),
                            ("conv_w_out", conv_w_out, m_conv_w_out, v_conv_w_out, (r_out,)),
                            ("w_qkv", w_qkv, m_w_qkv, v_w_qkv, (r_qkv,)),
                            ("w_o", w_o, m_w_o, v_w_o, (r_o,)),
                            ("w_up", w_up, m_w_up, v_w_up, (r_up0, r_up1)),
                            ("w_down", w_down, m_w_down, v_w_down, (r_down0, r_down1))):
        big_out[nm] = _adamw(f"adamw_{nm}", w, m, v, gs)

    dqg = dqg_t.reshape(N_HEADS, HEAD_DIM).sum(axis=0, keepdims=True)
    dkg = dkg_t.reshape(N_KV, HEAD_DIM).sum(axis=0, keepdims=True)
    small_full = [dg_conv, db_in, ddw8.sum(axis=1)[:CONV_W], ddw_b, dln_g, dln_b, db_out, dg_attn, db_qkv, dqg, dkg,
                  dsinks[None, :], db_o, drel.reshape(1, REL_BUCKETS * N_HEADS), jnp.pad(dg_mlp0, ((0, 1), (0, 0))) + jnp.pad(dg_mlp1, ((1, 0), (0, 0)))]
    _, small_sum = _gather8("reduce_small_grads", _pack(small_full), with_sum=True)
    (r_norm, r_b_in, r_dw, r_dw_b, r_ln_g, r_ln_b, r_b_out, r_attn_norm, r_b_qkv, r_qg, r_kg, r_sinks, r_b_o, r_rel,
     r_mlp_norm) = _unpack(small_sum, [a.shape for a in small_full])

    def cols(a, width):
        return lax.dynamic_slice_in_dim(a, shard * width, width, axis=a.ndim - 1)

    small_names = ["conv_norm_g", "conv_b_in", "conv_dw", "conv_dw_b", "conv_ln_g", "conv_ln_b", "conv_b_out",
                   "attn_norm_g", "b_qkv", "q_norm_g", "k_norm_g", "sinks", "b_o", "rel_bias", "mlp_norm_g"]
    small_g = [r_norm, r_b_in, cols(r_dw, Dm // N_SHARD)[None], r_dw_b, r_ln_g, r_ln_b, r_b_out,
               cols(r_attn_norm, Dm // N_SHARD), cols(r_b_qkv, QKV_DIM // N_SHARD), r_qg, r_kg, r_sinks,
               cols(r_b_o, Dm // N_SHARD), r_rel.reshape(REL_BUCKETS, N_HEADS), r_mlp_norm]
    small_w = [conv_norm_g, conv_b_in, conv_dw, conv_dw_b, conv_ln_g, conv_ln_b, conv_b_out, attn_norm_g, b_qkv,
               q_norm_g, k_norm_g, sinks, b_o, rel_bias, mlp_norm_g]
    small_m = [m_conv_norm_g, m_conv_b_in, m_conv_dw, m_conv_dw_b, m_conv_ln_g, m_conv_ln_b, m_conv_b_out,
               m_attn_norm_g, m_b_qkv, m_q_norm_g, m_k_norm_g, m_sinks, m_b_o, m_rel_bias, m_mlp_norm_g]
    small_v = [v_conv_norm_g, v_conv_b_in, v_conv_dw, v_conv_dw_b, v_conv_ln_g, v_conv_ln_b, v_conv_b_out,
               v_attn_norm_g, v_b_qkv, v_q_norm_g, v_k_norm_g, v_sinks, v_b_o, v_rel_bias, v_mlp_norm_g]
    flat2 = lambda a: a.reshape(-1, a.shape[-1])
    shapes2 = [flat2(w).shape for w in small_w]
    pk = lambda arrs: _pack([flat2(a) for a in arrs])
    packed_g = pk(small_g)
    d_s, m_s, v_s = _adamw_small(pk(small_w), packed_g, pk(small_m), pk(small_v))
    small_out = {}
    for nm, w, g, d, m2, v2 in zip(small_names, small_w, _unpack(packed_g, shapes2), _unpack(d_s, shapes2),
                                   _unpack(m_s, shapes2), _unpack(v_s, shapes2)):
        small_out[nm] = tuple(a.reshape(w.shape) for a in (g, d, m2, v2))

    order = ["conv_norm_g", "conv_w_in", "conv_b_in", "conv_dw", "conv_dw_b", "conv_ln_g", "conv_ln_b", "conv_w_out",
             "conv_b_out", "attn_norm_g", "w_qkv", "b_qkv", "q_norm_g", "k_norm_g", "sinks", "w_o", "b_o", "rel_bias",
             "mlp_norm_g", "w_up", "w_down"]
    res = {**small_out, **big_out}
    outs = [loss, gx[None]]
    for part in range(4):
        outs += [res[nm][part] for nm in order]
    return tuple(outs)
```

```python
import math

import numpy as np
import jax
import jax.numpy as jnp
from jax import lax
from jax.experimental import pallas as pl
from jax.experimental.pallas import tpu as pltpu

F32 = jnp.float32
BF = jnp.bfloat16
MESH = pl.DeviceIdType.MESH

D_MODEL = 1024
D_FF = 4096
N_HEADS = 16
N_KV = 2
GROUP = N_HEADS // N_KV
HEAD_DIM = 64
ATTN_DIM = N_HEADS * HEAD_DIM
KV_DIM = N_KV * HEAD_DIM
QKV_DIM = ATTN_DIM + 2 * KV_DIM
BLOCK = 128
CONV_W = 31
HALO = 32
REL_BUCKETS = 32
REL_MAX_DIST = 128
NORM_EPS = 1e-6
NEG_INF = -1e30
N_SHARD = 4
LANES = 1024

ADAM_LR = 0.001
ADAM_B1 = 0.9
ADAM_B2 = 0.999
ADAM_EPS = 1e-08
ADAM_WD = 0.01
ADAM_STEP = 10

VMEM_LIMIT = 56 * 1024 * 1024


def _params(n_axes):
    return pltpu.CompilerParams(dimension_semantics=("arbitrary",) * n_axes, vmem_limit_bytes=VMEM_LIMIT)


def _dot(a, b, ca, cb):
    return lax.dot_general(a, b, (((ca,), (cb,)), ((), ())), preferred_element_type=F32)


def _mm(name, a, b, *, nt, tm, tn, ep_fn, outs, a_fn=None, b_sm=False, ep_in=(), deps=(), rows=None):
    M, K = a.shape
    rows = tm if rows is None else rows
    if b_sm:
        S, ks = b.shape[0], b.shape[2]
        N, per = (b.shape[1], None) if nt else (S * b.shape[2], b.shape[2] // tn)
        assert (S * ks == K) if nt else (b.shape[1] == K)
    else:
        N = b.shape[0] if nt else b.shape[1]
        assert (b.shape[1] if nt else b.shape[0]) == K
    assert M % tm == 0 and N % tn == 0 and tm % rows == 0
    ne, no, nd = len(ep_in), len(outs), len(deps)

    def body(a_ref, b_ref, *rest):
        ep_refs, out_refs = rest[:ne], rest[ne + nd:ne + nd + no]
        i = pl.program_id(1)
        sums = [None] * no
        for r in range(tm // rows):
            rs = pl.ds(r * rows, rows)

            def lhs(cols):
                av = a_ref[rs, cols]
                return (av if a_fn is None else a_fn(av)).astype(BF)

            if b_sm and nt:
                acc = None
                for s in range(S):
                    part = _dot(lhs(pl.ds(s * ks, ks)), b_ref[s].astype(BF), 1, 1)
                    acc = part if acc is None else acc + part
            else:
                acc = _dot(lhs(slice(None)), b_ref[...].astype(BF), 1, 1 if nt else 0)
            ep_vals = [ref[rs, :] if kind == "tile" else ref[...] for ref, (_, kind) in zip(ep_refs, ep_in)]
            vals = ep_fn(acc, *ep_vals)
            for o, ((kind, dt), ref, val) in enumerate(zip(outs, out_refs, vals)):
                if kind == "tile":
                    ref[rs, :] = val.astype(dt)
                else:
                    sums[o] = val if sums[o] is None else sums[o] + val
        for (kind, dt), ref, val in zip(outs, out_refs, sums):
            if kind == "colsum":
                @pl.when(i == 0)
                def _():
                    ref[...] = val

                @pl.when(i > 0)
                def _():
                    ref[...] += val

    if b_sm and nt:
        b_spec = pl.BlockSpec((S, tn, ks), lambda j, i: (0, j, 0))
    elif b_sm:
        b_spec = pl.BlockSpec((None, K, tn), lambda j, i: (j // per, 0, j % per))
    elif nt:
        b_spec = pl.BlockSpec((tn, K), lambda j, i: (j, 0))
    else:
        b_spec = pl.BlockSpec((K, tn), lambda j, i: (0, j))
    in_specs = [pl.BlockSpec((tm, K), lambda j, i: (i, 0)), b_spec]
    for arr, kind in ep_in:
        if kind == "tile":
            assert arr.shape == (M, N)
            in_specs.append(pl.BlockSpec((tm, tn), lambda j, i: (i, j)))
        else:
            assert arr.shape == (1, N)
            in_specs.append(pl.BlockSpec((1, tn), lambda j, i: (0, j)))
    in_specs += [pl.BlockSpec(memory_space=pl.ANY)] * nd
    out_shape, out_specs = [], []
    for kind, dt in outs:
        if kind == "tile":
            out_shape.append(jax.ShapeDtypeStruct((M, N), dt))
            out_specs.append(pl.BlockSpec((tm, tn), lambda j, i: (i, j)))
        else:
            out_shape.append(jax.ShapeDtypeStruct((1, N), F32))
            out_specs.append(pl.BlockSpec((1, tn), lambda j, i: (0, j)))
    return pl.pallas_call(
        body, name=name, grid=(N // tn, M // tm), in_specs=in_specs, out_specs=out_specs, out_shape=out_shape,
        compiler_params=_params(2),
    )(a, b, *[arr for arr, _ in ep_in], *deps)


def _mm_tn(name, a, b, *, tm, tn, tk, a_fn=None, out_sm=None):
    T, Ka = a.shape
    N = b.shape[1]
    assert b.shape[0] == T and T % tk == 0 and Ka % tm == 0 and N % tn == 0
    nk = T // tk

    def body(a_ref, b_ref, o_ref, acc_ref):
        k = pl.program_id(2)

        @pl.when(k == 0)
        def _():
            acc_ref[...] = jnp.zeros_like(acc_ref)

        av = a_ref[...]
        if a_fn is not None:
            av = a_fn(av)
        acc_ref[...] += _dot(av.astype(BF), b_ref[...].astype(BF), 0, 0)

        @pl.when(k == nk - 1)
        def _():
            o_ref[...] = acc_ref[...].astype(BF)

    if out_sm is None:
        out_shape = jax.ShapeDtypeStruct((Ka, N), BF)
        out_spec = pl.BlockSpec((tm, tn), lambda i, j, k: (i, j))
    else:
        per = (N // out_sm) // tn
        assert per * tn * out_sm == N
        out_shape = jax.ShapeDtypeStruct((out_sm, Ka, N // out_sm), BF)
        out_spec = pl.BlockSpec((None, tm, tn), lambda i, j, k: (j // per, i, j % per))
    return pl.pallas_call(
        body, name=name, grid=(Ka // tm, N // tn, nk),
        in_specs=[pl.BlockSpec((tk, tm), lambda i, j, k: (k, i)), pl.BlockSpec((tk, tn), lambda i, j, k: (k, j))],
        out_specs=out_spec, out_shape=out_shape, scratch_shapes=[pltpu.VMEM((tm, tn), F32)],
        compiler_params=_params(3),
    )(a, b)


def _relu2(v):
    r = jnp.maximum(v.astype(F32), 0.0)
    return r * r


def _rms_bwd_ep(dh, x, g, dres):
    rstd = lax.rsqrt(jnp.mean(x * x, axis=-1, keepdims=True) + NORM_EPS)
    xh = x * rstd
    dxh = dh * g
    dx = rstd * (dxh - xh * jnp.mean(dxh * xh, axis=-1, keepdims=True))
    tot = dres + dx
    return tot, tot, jnp.sum(dh * xh, axis=0, keepdims=True), jnp.sum(tot, axis=0, keepdims=True)


def _rms_fwd(name, x, g, tm=512, deps=()):
    T, Dm = x.shape

    def body(x_ref, g_ref, *rest):
        o_ref = rest[-1]
        xv = x_ref[...]
        rstd = lax.rsqrt(jnp.mean(xv * xv, axis=-1, keepdims=True) + NORM_EPS)
        o_ref[...] = (xv * rstd * g_ref[...]).astype(BF)

    return pl.pallas_call(
        body, name=name, grid=(T // tm,),
        in_specs=[pl.BlockSpec((tm, Dm), lambda i: (i, 0)), pl.BlockSpec((1, Dm), lambda i: (0, 0))]
        + [pl.BlockSpec(memory_space=pl.ANY)] * len(deps),
        out_specs=pl.BlockSpec((tm, Dm), lambda i: (i, 0)), out_shape=jax.ShapeDtypeStruct((T, Dm), BF),
        compiler_params=_params(1),
    )(x, g, *deps)


HEAD_COLS = 128


def _two_term_dot(v, m):
    hi = v.astype(BF)
    lo = (v - hi.astype(F32)).astype(BF)
    return _dot(hi, m, 1, 0) + _dot(lo, m, 1, 0)


def _head_sum(v, select):
    sel, sel_t = select
    return _two_term_dot(_two_term_dot(v, sel), sel_t)


def _head_select(n):
    sel = (np.arange(n)[:, None] // HEAD_DIM == np.arange(HEAD_COLS)[None, :]).astype(np.float32)
    return jnp.asarray(sel, dtype=BF), jnp.asarray(sel.T, dtype=BF)


def _qk_norm_fwd(qkv, qg_t, kg_t, tm=256):
    T = qkv.shape[0]
    scale = 1.0 / math.sqrt(HEAD_DIM)

    def body(x_ref, qg_ref, kg_ref, sq_ref, sqt_ref, sk_ref, skt_ref, q_ref, k_ref, v_ref):
        q = x_ref[:, pl.ds(0, ATTN_DIM)]
        rq = lax.rsqrt(_head_sum(q * q, (sq_ref[...], sqt_ref[...])) * (1.0 / HEAD_DIM) + NORM_EPS)
        q_ref[...] = (q * rq * qg_ref[...] * scale).astype(BF)
        k = x_ref[:, pl.ds(ATTN_DIM, KV_DIM)]
        rk = lax.rsqrt(_head_sum(k * k, (sk_ref[...], skt_ref[...])) * (1.0 / HEAD_DIM) + NORM_EPS)
        k_ref[...] = (k * rk * kg_ref[...]).astype(BF)
        v_ref[...] = x_ref[:, pl.ds(ATTN_DIM + KV_DIM, KV_DIM)].astype(BF)

    full = lambda shape: pl.BlockSpec(shape, lambda i: (0, 0))
    return pl.pallas_call(
        body, name="qk_norm_fwd", grid=(T // tm,),
        in_specs=[pl.BlockSpec((tm, QKV_DIM), lambda i: (i, 0)), full((1, ATTN_DIM)), full((1, KV_DIM)),
                  full((ATTN_DIM, HEAD_COLS)), full((HEAD_COLS, ATTN_DIM)), full((KV_DIM, HEAD_COLS)), full((HEAD_COLS, KV_DIM))],
        out_specs=[pl.BlockSpec((tm, ATTN_DIM), lambda i: (i, 0)), pl.BlockSpec((tm, KV_DIM), lambda i: (i, 0)),
                   pl.BlockSpec((tm, KV_DIM), lambda i: (i, 0))],
        out_shape=[jax.ShapeDtypeStruct((T, ATTN_DIM), BF), jax.ShapeDtypeStruct((T, KV_DIM), BF),
                   jax.ShapeDtypeStruct((T, KV_DIM), BF)],
        compiler_params=_params(1),
    )(qkv, qg_t, kg_t, *_head_select(ATTN_DIM), *_head_select(KV_DIM))


def _qk_norm_bwd(qkv, dqn, dkn, dv, qg_t, kg_t, tm=256):
    T = qkv.shape[0]

    def body(x_ref, dq_ref, dk_ref, dv_ref, qg_ref, kg_ref, sq_ref, sqt_ref, sk_ref, skt_ref,
             o_ref, db_ref, dqg_ref, dkg_ref):
        i = pl.program_id(0)

        def one(x, dy, g, select):
            r = lax.rsqrt(_head_sum(x * x, select) * (1.0 / HEAD_DIM) + NORM_EPS)
            xh = x * r
            dxh = dy * g
            dx = r * (dxh - xh * (_head_sum(dxh * xh, select) * (1.0 / HEAD_DIM)))
            return dx, jnp.sum(dy * xh, axis=0, keepdims=True)

        dq, dqg = one(x_ref[:, pl.ds(0, ATTN_DIM)], dq_ref[...], qg_ref[...], (sq_ref[...], sqt_ref[...]))
        dk, dkg = one(x_ref[:, pl.ds(ATTN_DIM, KV_DIM)], dk_ref[...], kg_ref[...], (sk_ref[...], skt_ref[...]))
        dvv = dv_ref[...]
        o_ref[:, pl.ds(0, ATTN_DIM)] = dq.astype(BF)
        o_ref[:, pl.ds(ATTN_DIM, KV_DIM)] = dk.astype(BF)
        o_ref[:, pl.ds(ATTN_DIM + KV_DIM, KV_DIM)] = dvv.astype(BF)
        sq, sk, sv = (jnp.sum(t, axis=0, keepdims=True) for t in (dq, dk, dvv))

        @pl.when(i == 0)
        def _():
            db_ref[:, pl.ds(0, ATTN_DIM)] = sq
            db_ref[:, pl.ds(ATTN_DIM, KV_DIM)] = sk
            db_ref[:, pl.ds(ATTN_DIM + KV_DIM, KV_DIM)] = sv
            dqg_ref[...] = dqg
            dkg_ref[...] = dkg

        @pl.when(i > 0)
        def _():
            db_ref[:, pl.ds(0, ATTN_DIM)] += sq
            db_ref[:, pl.ds(ATTN_DIM, KV_DIM)] += sk
            db_ref[:, pl.ds(ATTN_DIM + KV_DIM, KV_DIM)] += sv
            dqg_ref[...] += dqg
            dkg_ref[...] += dkg

    full = lambda shape: pl.BlockSpec(shape, lambda i: (0, 0))
    row = lambda n: pl.BlockSpec((tm, n), lambda i: (i, 0))
    return pl.pallas_call(
        body, name="qk_norm_bwd", grid=(T // tm,),
        in_specs=[row(QKV_DIM), row(ATTN_DIM), row(KV_DIM), row(KV_DIM), full((1, ATTN_DIM)), full((1, KV_DIM)),
                  full((ATTN_DIM, HEAD_COLS)), full((HEAD_COLS, ATTN_DIM)), full((KV_DIM, HEAD_COLS)), full((HEAD_COLS, KV_DIM))],
        out_specs=[row(QKV_DIM), full((1, QKV_DIM)), full((1, ATTN_DIM)), full((1, KV_DIM))],
        out_shape=[jax.ShapeDtypeStruct((T, QKV_DIM), BF), jax.ShapeDtypeStruct((1, QKV_DIM), F32),
                   jax.ShapeDtypeStruct((1, ATTN_DIM), F32), jax.ShapeDtypeStruct((1, KV_DIM), F32)],
        compiler_params=_params(1),
    )(qkv, dqn, dkn, dv, qg_t, kg_t, *_head_select(ATTN_DIM), *_head_select(KV_DIM))


ROWS = 64
COLS = 128


SUBLANES = 8
FIRST_TAP = HALO - (CONV_W - 1)


def _glu(a, g):
    return a.astype(F32) * jax.nn.sigmoid(g.astype(F32))


def _shifted(xe, s):
    return xe if s == 0 else pltpu.roll(xe, ROWS + HALO - s, axis=0)


def _conv_fwd(u, dw_pad, dw_b, ln_g, ln_b, tm=256):
    T = u.shape[0]
    Dm = D_MODEL
    hpt = tm // HALO

    def body(ac_ref, gc_ref, ap_ref, gp_ref, w_ref, wb_ref, lg_ref, lb_ref, cv_ref, s_ref, ext):
        i = pl.program_id(0)
        ext[pl.ds(0, HALO), :] = jnp.where(i > 0, _glu(ap_ref[...], gp_ref[...]), 0.0)
        ext[pl.ds(HALO, tm), :] = _glu(ac_ref[...], gc_ref[...])

        def rows(r, carry):
            r0 = pl.multiple_of(r * ROWS, ROWS)
            for c in range(Dm // COLS):
                cs = pl.ds(c * COLS, COLS)
                xe = ext[pl.ds(r0, ROWS + HALO), cs]
                acc = jnp.zeros((ROWS, COLS), F32)
                for s in range(SUBLANES):
                    xs = _shifted(xe, s)
                    for j in range(CONV_W):
                        off = FIRST_TAP + j
                        if off % SUBLANES == s:
                            acc = acc + xs[off - s:off - s + ROWS, :] * w_ref[pl.ds(j, 1), cs]
                cv_ref[pl.ds(r0, ROWS), cs] = acc + wb_ref[:, cs]
            return carry

        lax.fori_loop(0, tm // ROWS, rows, 0)
        cv = cv_ref[...]
        xc = cv - jnp.mean(cv, axis=-1, keepdims=True)
        y = xc * lax.rsqrt(jnp.mean(xc * xc, axis=-1, keepdims=True) + NORM_EPS) * lg_ref[...] + lb_ref[...]
        s_ref[...] = (y * jax.nn.sigmoid(y)).astype(BF)

    full = lambda shape: pl.BlockSpec(shape, lambda i: (0, 0))
    return pl.pallas_call(
        body, name="conv_fwd", grid=(T // tm,),
        in_specs=[pl.BlockSpec((tm, Dm), lambda i: (i, 0)), pl.BlockSpec((tm, Dm), lambda i: (i, 1)),
                  pl.BlockSpec((HALO, Dm), lambda i: (jnp.maximum(i * hpt - 1, 0), 0)),
                  pl.BlockSpec((HALO, Dm), lambda i: (jnp.maximum(i * hpt - 1, 0), 1)),
                  full((HALO, Dm)), full((1, Dm)), full((1, Dm)), full((1, Dm))],
        out_specs=[pl.BlockSpec((tm, Dm), lambda i: (i, 0)), pl.BlockSpec((tm, Dm), lambda i: (i, 0))],
        out_shape=[jax.ShapeDtypeStruct((T, Dm), F32), jax.ShapeDtypeStruct((T, Dm), BF)],
        scratch_shapes=[pltpu.VMEM((tm + HALO, Dm), F32)],
        compiler_params=_params(1),
    )(u, u, u, u, dw_pad, dw_b, ln_g, ln_b)


def _ln_silu_bwd_ep(ds, cv, lg, lb):
    xc = cv - jnp.mean(cv, axis=-1, keepdims=True)
    rstd = lax.rsqrt(jnp.mean(xc * xc, axis=-1, keepdims=True) + NORM_EPS)
    xh = xc * rstd
    y = xh * lg + lb
    sg = jax.nn.sigmoid(y)
    dy = ds * (sg * (1.0 + y * (1.0 - sg)))
    dxh = dy * lg
    dcv = rstd * (dxh - jnp.mean(dxh, axis=-1, keepdims=True) - xh * jnp.mean(dxh * xh, axis=-1, keepdims=True))
    return (dcv, jnp.sum(dy * xh, axis=0, keepdims=True), jnp.sum(dy, axis=0, keepdims=True),
            jnp.sum(dcv, axis=0, keepdims=True))


def _conv_bwd(u, dcv, dw_pad, tm=256):
    T = u.shape[0]
    Dm = D_MODEL
    hpt = tm // HALO
    last = T // HALO - 1
    nt = T // tm

    def body(ac_ref, gc_ref, ap_ref, gp_ref, dc_ref, dn_ref, w_ref, du_ref, db_ref, dw_ref, ext_g, ext_d):
        i = pl.program_id(0)
        ext_g[pl.ds(0, HALO), :] = jnp.where(i > 0, _glu(ap_ref[...], gp_ref[...]), 0.0)
        ext_g[pl.ds(HALO, tm), :] = _glu(ac_ref[...], gc_ref[...])
        ext_d[pl.ds(0, tm), :] = dc_ref[...]
        ext_d[pl.ds(tm, HALO), :] = jnp.where(i < nt - 1, dn_ref[...], 0.0)

        @pl.when(i == 0)
        def _():
            db_ref[...] = jnp.zeros_like(db_ref)
            dw_ref[...] = jnp.zeros_like(dw_ref)

        def rows(r, carry):
            r0 = pl.multiple_of(r * ROWS, ROWS)
            rs = pl.ds(r0, ROWS)
            for c in range(Dm // COLS):
                cs = pl.ds(c * COLS, COLS)
                cs2 = pl.ds(Dm + c * COLS, COLS)
                de = ext_d[pl.ds(r0, ROWS + HALO), cs]
                ge = ext_g[pl.ds(r0, ROWS + HALO), cs]
                dcur = de[0:ROWS, :]
                acc = jnp.zeros((ROWS, COLS), F32)
                for s in range(SUBLANES):
                    ds_, gs_ = _shifted(de, s), _shifted(ge, s)
                    for j in range(CONV_W):
                        off = CONV_W - 1 - j
                        if off % SUBLANES == s:
                            acc = acc + ds_[off - s:off - s + ROWS, :] * w_ref[pl.ds(j, 1), cs]
                        goff = FIRST_TAP + j
                        if goff % SUBLANES == s:
                            prod = dcur * gs_[goff - s:goff - s + ROWS, :]
                            dw_ref[j, :, cs] += jnp.sum(prod.reshape(ROWS // SUBLANES, SUBLANES, COLS), axis=0)
                a = ac_ref[rs, cs].astype(F32)
                sg = jax.nn.sigmoid(gc_ref[rs, cs].astype(F32))
                da = acc * sg
                dg = acc * a * sg * (1.0 - sg)
                du_ref[rs, cs] = da.astype(BF)
                du_ref[rs, cs2] = dg.astype(BF)
                db_ref[:, cs] += jnp.sum(da, axis=0, keepdims=True)
                db_ref[:, cs2] += jnp.sum(dg, axis=0, keepdims=True)
            return carry

        lax.fori_loop(0, tm // ROWS, rows, 0)

    return pl.pallas_call(
        body, name="conv_bwd", grid=(nt,),
        in_specs=[pl.BlockSpec((tm, Dm), lambda i: (i, 0)), pl.BlockSpec((tm, Dm), lambda i: (i, 1)),
                  pl.BlockSpec((HALO, Dm), lambda i: (jnp.maximum(i * hpt - 1, 0), 0)),
                  pl.BlockSpec((HALO, Dm), lambda i: (jnp.maximum(i * hpt - 1, 0), 1)),
                  pl.BlockSpec((tm, Dm), lambda i: (i, 0)),
                  pl.BlockSpec((HALO, Dm), lambda i: (jnp.minimum((i + 1) * hpt, last), 0)),
                  pl.BlockSpec((HALO, Dm), lambda i: (0, 0))],
        out_specs=[pl.BlockSpec((tm, 2 * Dm), lambda i: (i, 0)), pl.BlockSpec((1, 2 * Dm), lambda i: (0, 0)),
                   pl.BlockSpec((HALO, 8, Dm), lambda i: (0, 0, 0))],
        out_shape=[jax.ShapeDtypeStruct((T, 2 * Dm), BF), jax.ShapeDtypeStruct((1, 2 * Dm), F32),
                   jax.ShapeDtypeStruct((HALO, 8, Dm), F32)],
        scratch_shapes=[pltpu.VMEM((tm + HALO, Dm), F32), pltpu.VMEM((tm + HALO, Dm), F32)],
        compiler_params=_params(1),
    )(u, u, u, u, dcv, dcv, dw_pad)


def _bucket_table():
    q_loc = np.arange(BLOCK)[:, None]
    k_loc = np.arange(2 * BLOCK)[None, :]
    dist = q_loc + BLOCK - k_loc
    n = np.maximum(dist, 0)
    max_exact = REL_BUCKETS // 2
    large = max_exact + (np.log(np.maximum(n, 1).astype(np.float32) / max_exact)
                         / math.log(REL_MAX_DIST / max_exact) * (REL_BUCKETS - max_exact)).astype(np.int32)
    large = np.minimum(large, REL_BUCKETS - 1)
    bucket = np.where(n < max_exact, n, large).astype(np.int32)
    return jnp.asarray(np.where((dist >= 0) & (dist < BLOCK), bucket, -1).astype(np.int32))


def _bias_table(rel_bias, bucket):
    def body(rb_ref, bk_ref, o_ref):
        bk = bk_ref[...]
        for h in range(N_HEADS):
            acc = jnp.full((BLOCK, 2 * BLOCK), NEG_INF, F32)
            for b in range(REL_BUCKETS):
                acc = jnp.where(bk == b, rb_ref[b, h], acc)
            o_ref[h] = acc

    return pl.pallas_call(
        body, name="bias_table", out_shape=jax.ShapeDtypeStruct((N_HEADS, BLOCK, 2 * BLOCK), F32),
        in_specs=[pl.BlockSpec(memory_space=pltpu.SMEM), pl.BlockSpec(memory_space=pltpu.VMEM)],
        out_specs=pl.BlockSpec(memory_space=pltpu.VMEM),
    )(rel_bias, bucket)


def _bias_grad(dbias, bucket):
    def body(db_ref, bk_ref, o_ref):
        bk = bk_ref[...]
        for b in range(REL_BUCKETS):
            sel = bk == b
            for h in range(N_HEADS):
                o_ref[b, h] = jnp.sum(jnp.where(sel, db_ref[h], 0.0))

    return pl.pallas_call(
        body, name="bias_grad", out_shape=jax.ShapeDtypeStruct((REL_BUCKETS, N_HEADS), F32),
        in_specs=[pl.BlockSpec(memory_space=pltpu.VMEM), pl.BlockSpec(memory_space=pltpu.VMEM)],
        out_specs=pl.BlockSpec(memory_space=pltpu.SMEM),
    )(dbias, bucket)


GROUP_ROWS = GROUP * BLOCK


def _head_probs(qk, bias_h, sink, first):
    s = jnp.where(first, NEG_INF, qk + bias_h)
    m = jnp.maximum(jnp.max(s, axis=-1, keepdims=True), sink)
    p = jnp.exp(s - m)
    ps = jnp.exp(sink - m)
    inv = 1.0 / (jnp.sum(p, axis=-1, keepdims=True) + ps)
    return p * inv, ps * inv


def _band(prev_ref, cur_ref, g):
    hs = pl.ds(g * HEAD_DIM, HEAD_DIM)
    return jnp.concatenate([prev_ref[:, hs], cur_ref[:, hs]], axis=0)


def _stack_heads(ref, g):
    return jnp.concatenate([ref[:, pl.ds((g * GROUP + hh) * HEAD_DIM, HEAD_DIM)] for hh in range(GROUP)], axis=0)


def _unstack_heads(ref, g, stacked, dtype):
    for hh in range(GROUP):
        ref[:, pl.ds((g * GROUP + hh) * HEAD_DIM, HEAD_DIM)] = stacked[hh * BLOCK:(hh + 1) * BLOCK, :].astype(dtype)


def _first_mask(n):
    col = lax.broadcasted_iota(jnp.int32, (1, 2 * BLOCK), 1)
    return jnp.logical_and(n == 0, col < BLOCK)


def _head_rows(hh):
    return pl.ds(hh * BLOCK, BLOCK)


def _attn_fwd(qn, kn, vv, bias, sinks):
    T = qn.shape[0]
    nb = T // BLOCK

    def body(sk_ref, q_ref, kc_ref, kp_ref, vc_ref, vp_ref, b_ref, o_ref, qk_buf, p_buf):
        first = _first_mask(pl.program_id(0))
        for g in range(N_KV):
            k = _band(kp_ref, kc_ref, g)
            v = _band(vp_ref, vc_ref, g)
            qk_buf[...] = _dot(_stack_heads(q_ref, g), k, 1, 1)
            for hh in range(GROUP):
                h = g * GROUP + hh
                pn, _ = _head_probs(qk_buf[_head_rows(hh), :], b_ref[h], sk_ref[h], first)
                p_buf[_head_rows(hh), :] = pn.astype(BF)
            _unstack_heads(o_ref, g, _dot(p_buf[...], v, 1, 0), BF)

    cur = lambda n: (n, 0)
    prev = lambda n: (jnp.maximum(n - 1, 0), 0)
    return pl.pallas_call(
        body, name="attn_fwd", grid=(nb,),
        in_specs=[pl.BlockSpec(memory_space=pltpu.SMEM), pl.BlockSpec((BLOCK, ATTN_DIM), cur),
                  pl.BlockSpec((BLOCK, KV_DIM), cur), pl.BlockSpec((BLOCK, KV_DIM), prev),
                  pl.BlockSpec((BLOCK, KV_DIM), cur), pl.BlockSpec((BLOCK, KV_DIM), prev),
                  pl.BlockSpec((N_HEADS, BLOCK, 2 * BLOCK), lambda n: (0, 0, 0))],
        out_specs=pl.BlockSpec((BLOCK, ATTN_DIM), cur), out_shape=jax.ShapeDtypeStruct((T, ATTN_DIM), BF),
        scratch_shapes=[pltpu.VMEM((GROUP_ROWS, 2 * BLOCK), F32), pltpu.VMEM((GROUP_ROWS, 2 * BLOCK), BF)],
        compiler_params=_params(1),
    )(sinks, qn, kn, kn, vv, vv, bias)


def _attn_bwd(qn, kn, vv, bias, sinks, do):
    T = qn.shape[0]
    nb = T // BLOCK
    scale = 1.0 / math.sqrt(HEAD_DIM)

    def body(sk_ref, q_ref, kc_ref, kp_ref, vc_ref, vp_ref, b_ref, do_ref,
             dq_ref, dk_ref, dv_ref, db_ref, dsk_ref, dk_full, dv_full, dk_carry, dv_carry, qk_buf, dp_buf, p_buf, ds_buf):
        n = pl.program_id(0)

        @pl.when(n == 0)
        def _():
            db_ref[...] = jnp.zeros_like(db_ref)
            dk_carry[...] = jnp.zeros_like(dk_carry)
            dv_carry[...] = jnp.zeros_like(dv_carry)
            for h in range(N_HEADS):
                dsk_ref[h] = 0.0

        @pl.when(n < nb)
        def _():
            first = _first_mask(n)
            for g in range(N_KV):
                k = _band(kp_ref, kc_ref, g)
                v = _band(vp_ref, vc_ref, g)
                q = _stack_heads(q_ref, g)
                dout = _stack_heads(do_ref, g)
                qk_buf[...] = _dot(q, k, 1, 1)
                dp_buf[...] = _dot(dout, v, 1, 1)
                for hh in range(GROUP):
                    h = g * GROUP + hh
                    rows = _head_rows(hh)
                    pn, psink = _head_probs(qk_buf[rows, :], b_ref[h], sk_ref[h], first)
                    dp = dp_buf[rows, :]
                    delta = jnp.sum(pn * dp, axis=-1, keepdims=True)
                    ds = pn * (dp - delta)
                    dsk_ref[h] += -jnp.sum(psink * delta)
                    db_ref[h] += ds
                    ds_buf[rows, :] = ds.astype(BF)
                    p_buf[rows, :] = pn.astype(BF)
                dsb = ds_buf[...]
                _unstack_heads(dq_ref, g, _dot(dsb, k, 1, 0) * scale, F32)
                gs = pl.ds(g * HEAD_DIM, HEAD_DIM)
                dk_full[:, gs] = _dot(dsb, q, 0, 0)
                dv_full[:, gs] = _dot(p_buf[...], dout, 0, 0)

        @pl.when(n == nb)
        def _():
            dk_full[...] = jnp.zeros_like(dk_full)
            dv_full[...] = jnp.zeros_like(dv_full)

        dk_ref[...] = dk_carry[...] + dk_full[pl.ds(0, BLOCK), :]
        dv_ref[...] = dv_carry[...] + dv_full[pl.ds(0, BLOCK), :]
        dk_carry[...] = dk_full[pl.ds(BLOCK, BLOCK), :]
        dv_carry[...] = dv_full[pl.ds(BLOCK, BLOCK), :]

    cur = lambda n: (jnp.minimum(n, nb - 1), 0)
    prev = lambda n: (jnp.maximum(jnp.minimum(n, nb - 1) - 1, 0), 0)
    out_kv = lambda n: (jnp.maximum(n - 1, 0), 0)
    return pl.pallas_call(
        body, name="attn_bwd", grid=(nb + 1,),
        in_specs=[pl.BlockSpec(memory_space=pltpu.SMEM), pl.BlockSpec((BLOCK, ATTN_DIM), cur),
                  pl.BlockSpec((BLOCK, KV_DIM), cur), pl.BlockSpec((BLOCK, KV_DIM), prev),
                  pl.BlockSpec((BLOCK, KV_DIM), cur), pl.BlockSpec((BLOCK, KV_DIM), prev),
                  pl.BlockSpec((N_HEADS, BLOCK, 2 * BLOCK), lambda n: (0, 0, 0)),
                  pl.BlockSpec((BLOCK, ATTN_DIM), cur)],
        out_specs=[pl.BlockSpec((BLOCK, ATTN_DIM), cur), pl.BlockSpec((BLOCK, KV_DIM), out_kv),
                   pl.BlockSpec((BLOCK, KV_DIM), out_kv),
                   pl.BlockSpec((N_HEADS, BLOCK, 2 * BLOCK), lambda n: (0, 0, 0)),
                   pl.BlockSpec(memory_space=pltpu.SMEM)],
        out_shape=[jax.ShapeDtypeStruct((T, ATTN_DIM), F32), jax.ShapeDtypeStruct((T, KV_DIM), F32),
                   jax.ShapeDtypeStruct((T, KV_DIM), F32),
                   jax.ShapeDtypeStruct((N_HEADS, BLOCK, 2 * BLOCK), F32), jax.ShapeDtypeStruct((N_HEADS,), F32)],
        scratch_shapes=[pltpu.VMEM((2 * BLOCK, KV_DIM), F32), pltpu.VMEM((2 * BLOCK, KV_DIM), F32),
                        pltpu.VMEM((BLOCK, KV_DIM), F32), pltpu.VMEM((BLOCK, KV_DIM), F32),
                        pltpu.VMEM((GROUP_ROWS, 2 * BLOCK), F32), pltpu.VMEM((GROUP_ROWS, 2 * BLOCK), F32),
                        pltpu.VMEM((GROUP_ROWS, 2 * BLOCK), BF), pltpu.VMEM((GROUP_ROWS, 2 * BLOCK), BF)],
        compiler_params=_params(1),
    )(sinks, qn, kn, kn, vv, vv, bias, do)


def _coords():
    return lax.axis_index("x"), lax.axis_index("y"), lax.axis_index("c")


def _gather8(name, v, with_sum):
    R = v.shape[0]

    def body(v_ref, all_ref, *rest):
        sum_ref = rest[0] if with_sum else None
        send_sems, recv_sems, local_sem = rest[-3:]
        x, y, c = _coords()
        me = 4 * x + 2 * y + c
        local = pltpu.make_async_copy(v_ref, all_ref.at[me], local_sem)
        local.start()
        sends = []
        for k in range(1, 8):
            peer = (x ^ (k >> 2), y ^ ((k >> 1) & 1), c ^ (k & 1))
            cp = pltpu.make_async_remote_copy(src_ref=v_ref, dst_ref=all_ref.at[me], send_sem=send_sems.at[k - 1],
                                              recv_sem=recv_sems.at[k - 1], device_id=peer, device_id_type=MESH)
            cp.start()
            sends.append(cp)
        for k in range(1, 8):
            peer = (x ^ (k >> 2), y ^ ((k >> 1) & 1), c ^ (k & 1))
            pltpu.make_async_remote_copy(src_ref=v_ref, dst_ref=all_ref.at[me ^ k], send_sem=send_sems.at[k - 1],
                                         recv_sem=recv_sems.at[k - 1], device_id=peer, device_id_type=MESH).wait_recv()
        for cp in sends:
            cp.wait_send()
        local.wait()
        if with_sum:
            tot = all_ref[0]
            for d in range(1, 8):
                tot = tot + all_ref[d]
            sum_ref[...] = tot

    out_shape = [jax.ShapeDtypeStruct((8, R, LANES), F32)]
    if with_sum:
        out_shape.append(jax.ShapeDtypeStruct((R, LANES), F32))
    vm = pl.BlockSpec(memory_space=pltpu.VMEM)
    return pl.pallas_call(
        body, name=name, out_shape=out_shape, in_specs=[vm], out_specs=[vm] * len(out_shape),
        scratch_shapes=[pltpu.SemaphoreType.DMA((7,)), pltpu.SemaphoreType.DMA((7,)), pltpu.SemaphoreType.DMA],
    )(v)


CHIP_FLIPS = ((1, 0), (0, 1), (1, 1))


HBM_SPEC = pl.BlockSpec(memory_space=pltpu.HBM)
SEM_SPEC = pl.BlockSpec(memory_space=pltpu.SEMAPHORE)
ANY_SPEC = pl.BlockSpec(memory_space=pl.ANY)
DATAFLOW = pltpu.SideEffectType.DATAFLOW_SIDE_EFFECTING


def _chip_copy(land, sems, idx, slot_src, slot_dst, peer):
    send_sems, recv_sems = sems
    return pltpu.make_async_remote_copy(src_ref=land.at[slot_src], dst_ref=land.at[slot_dst], send_sem=send_sems.at[idx],
                                        recv_sem=recv_sems.at[idx], device_id=peer, device_id_type=MESH)


def _gather_start(stacks, groups, after):
    n = len(stacks)
    ng = len(groups)
    after = tuple(after)

    def body(*refs):
        lands = refs[:n]
        first = n + len(after)
        sems = [(refs[first + 2 * g], refs[first + 2 * g + 1]) for g in range(ng)]
        token = refs[-1]
        x, y, c = _coords()
        s = 2 * x + y
        for g, members in enumerate(groups):
            for i, t in enumerate(members):
                for j, (fx, fy) in enumerate(CHIP_FLIPS):
                    _chip_copy(lands[t], sems[g], 3 * i + j, s, s, (x ^ fx, y ^ fy, c)).start()
        token[...] = jnp.zeros_like(token)

    out_shape = []
    for members in groups:
        out_shape += [pltpu.SemaphoreType.DMA((3 * len(members),))] * 2
    out_shape += [pltpu.HBM(w.shape, w.dtype) for w in stacks]
    out_shape.append(jax.ShapeDtypeStruct((8, 128), F32))
    res = pl.pallas_call(
        body, name="gather_start", out_shape=out_shape, in_specs=[HBM_SPEC] * n + [ANY_SPEC] * len(after),
        out_specs=[SEM_SPEC] * (2 * ng) + [HBM_SPEC] * n + [pl.BlockSpec(memory_space=pltpu.VMEM)],
        input_output_aliases={t: 2 * ng + t for t in range(n)},
        compiler_params=pltpu.CompilerParams(has_side_effects=DATAFLOW),
    )(*[pltpu.with_memory_space_constraint(w, pltpu.HBM) for w in stacks], *after)
    sems = [(res[2 * g], res[2 * g + 1]) for g in range(ng)]
    return sems, list(res[2 * ng:2 * ng + n]), res[-1]


def _gather_wait(name, stacks, sems, after):
    n = len(stacks)
    after = tuple(after)

    def body(*refs):
        lands = refs[:n]
        group_sems = (refs[n], refs[n + 1])
        x, y, c = _coords()
        s = 2 * x + y
        for i in range(n):
            for j, (fx, fy) in enumerate(CHIP_FLIPS):
                cp = _chip_copy(lands[i], group_sems, 3 * i + j, s, 2 * (x ^ fx) + (y ^ fy), (x ^ fx, y ^ fy, c))
                cp.wait_send()
                cp.wait_recv()

    return pl.pallas_call(
        body, name=name, out_shape=[pltpu.HBM(w.shape, w.dtype) for w in stacks],
        in_specs=[HBM_SPEC] * n + [SEM_SPEC, SEM_SPEC] + [ANY_SPEC] * len(after), out_specs=[HBM_SPEC] * n,
        input_output_aliases={t: t for t in range(n)},
        compiler_params=pltpu.CompilerParams(has_side_effects=DATAFLOW),
    )(*stacks, sems[0], sems[1], *after)


N_PEERS = 7


def _peer(x, y, c, k):
    return x ^ (k >> 2), y ^ ((k >> 1) & 1), c ^ (k & 1)


def _reduce_copy(grad, land, sems, idx, x, y, c, k):
    px, py, pc = _peer(x, y, c, k)
    rh = grad.shape[1] // 2
    return pltpu.make_async_remote_copy(src_ref=grad.at[2 * px + py, pl.ds(pc * rh, rh), :], dst_ref=land.at[k - 1],
                                        send_sem=sems[0].at[idx], recv_sem=sems[1].at[idx], device_id=(px, py, pc),
                                        device_id_type=MESH)


def _reduce_start(name, grads):
    n = len(grads)

    def body(*refs):
        src, lands, sems, token = refs[:n], refs[n:2 * n], (refs[2 * n], refs[2 * n + 1]), refs[-1]
        x, y, c = _coords()
        for t in range(n):
            for k in range(1, N_PEERS + 1):
                _reduce_copy(src[t], lands[t], sems, N_PEERS * t + k - 1, x, y, c, k).start()
        token[...] = jnp.zeros_like(token)

    lands = [lax.empty((N_PEERS, g.shape[1] // 2, g.shape[2]), g.dtype) for g in grads]
    out_shape = [pltpu.SemaphoreType.DMA((N_PEERS * n,))] * 2
    out_shape += [pltpu.HBM(a.shape, a.dtype) for a in list(grads) + lands]
    out_shape.append(jax.ShapeDtypeStruct((8, 128), F32))
    res = pl.pallas_call(
        body, name=name, out_shape=out_shape, in_specs=[HBM_SPEC] * (2 * n),
        out_specs=[SEM_SPEC] * 2 + [HBM_SPEC] * (2 * n) + [pl.BlockSpec(memory_space=pltpu.VMEM)],
        input_output_aliases={t: 2 + t for t in range(2 * n)},
        compiler_params=pltpu.CompilerParams(has_side_effects=DATAFLOW),
    )(*[pltpu.with_memory_space_constraint(a, pltpu.HBM) for a in list(grads) + lands])
    return (res[0], res[1]), list(res[2:2 + n]), list(res[2 + n:2 + 2 * n]), res[-1]


def _reduce_wait(name, grads, lands, sems, after):
    n = len(grads)
    after = tuple(after)

    def body(*refs):
        src, dst, group_sems = refs[:n], refs[n:2 * n], (refs[2 * n], refs[2 * n + 1])
        x, y, c = _coords()
        for t in range(n):
            for k in range(1, N_PEERS + 1):
                cp = _reduce_copy(src[t], dst[t], group_sems, N_PEERS * t + k - 1, x, y, c, k)
                cp.wait_send()
                cp.wait_recv()

    res = pl.pallas_call(
        body, name=name, out_shape=[pltpu.HBM(a.shape, a.dtype) for a in list(grads) + list(lands)],
        in_specs=[HBM_SPEC] * (2 * n) + [SEM_SPEC, SEM_SPEC] + [ANY_SPEC] * len(after), out_specs=[HBM_SPEC] * (2 * n),
        input_output_aliases={t: t for t in range(2 * n)},
        compiler_params=pltpu.CompilerParams(has_side_effects=DATAFLOW),
    )(*grads, *lands, sems[0], sems[1], *after)
    return list(res[:n]), list(res[n:])


def _join_halves(name, halves):
    n = len(halves)

    def body(*refs):
        src, dst = refs[:n], refs[n:2 * n]
        send_sems, recv_sems = refs[2 * n:]
        x, y, c = _coords()
        cps = []
        for t in range(n):
            cp = pltpu.make_async_remote_copy(src_ref=src[t], dst_ref=dst[t], send_sem=send_sems.at[t],
                                              recv_sem=recv_sems.at[t], device_id=(x, y, 1 - c), device_id_type=MESH)
            cp.start()
            cps.append(cp)
        for cp in cps:
            cp.wait()

    anyspec = pl.BlockSpec(memory_space=pl.ANY)
    return pl.pallas_call(
        body, name=name, out_shape=[jax.ShapeDtypeStruct(h.shape, h.dtype) for h in halves],
        in_specs=[anyspec] * n, out_specs=[anyspec] * n,
        scratch_shapes=[pltpu.SemaphoreType.DMA((n,)), pltpu.SemaphoreType.DMA((n,))],
    )(*halves)


def _row_block(rows):
    for rb in (512, 256, 128, 64, 32, 16):
        if rows % rb == 0:
            return rb
    raise ValueError(rows)


def _sum_devices(name, grad, land, place):
    S, R, C = grad.shape
    rh = R // 2
    rb = _row_block(rh)
    nbh = rh // rb

    def body(place_ref, g_ref, l_ref, o_ref):
        tot = g_ref[...].astype(F32)
        for k in range(N_PEERS):
            tot = tot + l_ref[k].astype(F32)
        o_ref[...] = tot

    return pl.pallas_call(
        body, name=name,
        grid_spec=pltpu.PrefetchScalarGridSpec(
            num_scalar_prefetch=1, grid=(nbh,),
            in_specs=[pl.BlockSpec((None, rb, C), lambda r, place: (place[0], place[1] * nbh + r, 0)),
                      pl.BlockSpec((N_PEERS, rb, C), lambda r, place: (0, r, 0))],
            out_specs=pl.BlockSpec((rb, C), lambda r, place: (r, 0))),
        out_shape=jax.ShapeDtypeStruct((rh, C), F32), compiler_params=_params(1),
    )(place, grad, land)


def _adamw_math(w, g, m, v):
    m2 = ADAM_B1 * m + (1.0 - ADAM_B1) * g
    v2 = ADAM_B2 * v + (1.0 - ADAM_B2) * (g * g)
    m_hat = m2 / (1.0 - ADAM_B1 ** ADAM_STEP)
    v_hat = v2 / (1.0 - ADAM_B2 ** ADAM_STEP)
    delta = -ADAM_LR * (m_hat / (jnp.sqrt(v_hat) + ADAM_EPS) + ADAM_WD * w)
    return delta, m2, v2


def _adamw(name, w, m, v, gs):
    L, R, C = w.shape
    Rh = R // 2
    rb = _row_block(Rh)
    nbh = Rh // rb
    assert len(gs) == L

    def body(core_ref, w_ref, m_ref, v_ref, *rest):
        g_refs, (go_ref, d_ref, m2_ref, v2_ref) = rest[:2 * L], rest[2 * L:]
        layer, half = pl.program_id(0), pl.program_id(1)
        mine = half == core_ref[0]
        g = jnp.where(mine, g_refs[0][...], g_refs[1][...])
        for t in range(1, L):
            g = jnp.where(layer == t, jnp.where(mine, g_refs[2 * t][...], g_refs[2 * t + 1][...]), g)
        delta, m2, v2 = _adamw_math(w_ref[...], g, m_ref[...], v_ref[...])
        go_ref[...] = g
        d_ref[...] = delta
        m2_ref[...] = m2
        v2_ref[...] = v2

    wspec = pl.BlockSpec((None, rb, C), lambda l, h, r, core: (l, h * nbh + r, 0))
    gspec = pl.BlockSpec((rb, C), lambda l, h, r, core: (r, 0))
    return pl.pallas_call(
        body, name=name,
        grid_spec=pltpu.PrefetchScalarGridSpec(num_scalar_prefetch=1, grid=(L, 2, nbh),
                                               in_specs=[wspec] * 3 + [gspec] * (2 * L), out_specs=[wspec] * 4),
        out_shape=[jax.ShapeDtypeStruct((L, R, C), F32)] * 4, compiler_params=_params(3),
    )(lax.axis_index("c").astype(jnp.int32).reshape(1), w, m, v, *[g for pair in gs for g in pair])


def _adamw_small(w, g, m, v):
    def body(w_ref, g_ref, m_ref, v_ref, d_ref, m2_ref, v2_ref):
        delta, m2, v2 = _adamw_math(w_ref[...], g_ref[...], m_ref[...], v_ref[...])
        d_ref[...] = delta
        m2_ref[...] = m2
        v2_ref[...] = v2

    return pl.pallas_call(body, name="adamw_small", out_shape=[jax.ShapeDtypeStruct(w.shape, F32)] * 3)(w, g, m, v)


def _packed_rows(shape):
    c = shape[-1]
    return (int(np.prod(shape)) // c) * -(-c // LANES)


def _pack(arrays):
    total = sum(_packed_rows(a.shape) for a in arrays)
    total += -total % 8
    buf, r0 = None, 0
    for a in arrays:
        a = a.astype(F32).reshape(-1, a.shape[-1])
        r, c = a.shape
        k = -(-c // LANES)
        a = jnp.pad(a, ((0, 0), (0, k * LANES - c))).reshape(r * k, LANES)
        a = jnp.pad(a, ((r0, total - r0 - r * k), (0, 0)))
        buf = a if buf is None else buf + a
        r0 += r * k
    return buf


def _unpack(buf, shapes):
    out, r0 = [], 0
    for shp in shapes:
        c = shp[-1]
        rows = _packed_rows(shp)
        out.append(buf[r0:r0 + rows].reshape(-1, -(-c // LANES) * LANES)[:, :c].reshape(shp))
        r0 += rows
    return out


def _rms(x, g):
    return x * lax.rsqrt(jnp.mean(x * x, axis=-1, keepdims=True) + NORM_EPS) * g


def _residual_norm_ep(acc, *rest):
    *bias, res, gain = rest
    x = acc + res + (bias[0] if bias else 0.0)
    return x, _rms(x, gain)


RESIDUAL_NORM_OUTS = (("tile", F32), ("tile", BF))


def _mlp_up(tag, h, w_up_sm):
    (up,) = _mm(f"mlp{tag}_up", h, w_up_sm, nt=False, b_sm=True, tm=1024, tn=1024, rows=256,
                ep_fn=lambda acc: (acc,), outs=(("tile", BF),))
    return up


RMS_BWD_OUTS = (("tile", F32), ("tile", BF), ("colsum", F32), ("colsum", F32))


def _mlp_bwd(tag, dy, dy_bf, x, g, up, w_up_sm, w_down):
    (dup,) = _mm(f"mlp{tag}_dup", dy_bf, w_down, nt=True, tm=1024, tn=1024, rows=256, ep_in=((up, "tile"),),
                 ep_fn=lambda acc, u: (acc * (2.0 * jnp.maximum(u.astype(F32), 0.0)),), outs=(("tile", BF),))
    dx, dx_bf, dg, dx_sum = _mm(f"mlp{tag}_dx", dup, w_up_sm, nt=True, b_sm=True, tm=512, tn=1024, rows=256,
                                ep_in=((x, "tile"), (g, "row"), (dy, "tile")), ep_fn=_rms_bwd_ep, outs=RMS_BWD_OUTS)
    return dx, dx_bf, dg, dx_sum, dup


class _Reduction:
    def __init__(self, tag, grads, place):
        self.tag, self.place = tag, place
        self.sems, self.grads, self.lands, self.token = _reduce_start(f"reduce_start_{tag}", grads)

    def finish(self, after):
        grads, lands = _reduce_wait(f"reduce_wait_{self.tag}", self.grads, self.lands, self.sems, after)
        return [_sum_devices(f"reduce_sum_{self.tag}{i}", g, l, self.place) for i, (g, l) in enumerate(zip(grads, lands))]


def kernel(x, conv_norm_g, conv_w_in, conv_b_in, conv_dw, conv_dw_b, conv_ln_g, conv_ln_b, conv_w_out, conv_b_out, attn_norm_g, w_qkv, b_qkv, q_norm_g, k_norm_g, sinks, w_o, b_o, rel_bias, mlp_norm_g, w_up, w_down, loss_target, m_conv_norm_g, m_conv_w_in, m_conv_b_in, m_conv_dw, m_conv_dw_b, m_conv_ln_g, m_conv_ln_b, m_conv_w_out, m_conv_b_out, m_attn_norm_g, m_w_qkv, m_b_qkv, m_q_norm_g, m_k_norm_g, m_sinks, m_w_o, m_b_o, m_rel_bias, m_mlp_norm_g, m_w_up, m_w_down, v_conv_norm_g, v_conv_w_in, v_conv_b_in, v_conv_dw, v_conv_dw_b, v_conv_ln_g, v_conv_ln_b, v_conv_w_out, v_conv_b_out, v_attn_norm_g, v_w_qkv, v_b_qkv, v_q_norm_g, v_k_norm_g, v_sinks, v_w_o, v_b_o, v_rel_bias, v_mlp_norm_g, v_w_up, v_w_down):
    Dm = D_MODEL
    x2d = x[0]
    tgt = loss_target[0]
    T = x2d.shape[0]
    shard = 2 * lax.axis_index("x") + lax.axis_index("y")

    sharded_small = [conv_dw[0], attn_norm_g, b_qkv, b_o]
    (gathered,) = _gather8("gather_small_weights", _pack(sharded_small), with_sum=False)
    chips = [_unpack(gathered[2 * s], [a.shape for a in sharded_small]) for s in range(N_SHARD)]
    dw_f, attn_norm_f, b_qkv_f, b_o_f = (jnp.concatenate([chips[s][t] for s in range(N_SHARD)], axis=-1)
                                         for t in range(len(sharded_small)))
    dw_pad = jnp.pad(dw_f, ((0, HALO - CONV_W), (0, 0)))

    big = [conv_w_in[0], conv_w_out[0], w_qkv[0], w_o[0], w_up[0], w_up[1], w_down[0], w_down[1]]
    stacks = [lax.dynamic_update_slice(jnp.zeros((N_SHARD,) + w.shape, BF), w.astype(BF)[None], (shard, 0, 0))
              for w in big]
    groups = ((0,), (1,), (4, 6), (2, 3), (5, 7))
    gather_sems, stacks, gather_token = _gather_start(stacks, groups, after=(gathered,))

    def gathered_group(g, name, after):
        return _gather_wait(name, [stacks[t] for t in groups[g]], gather_sems[g], after)

    bucket = _bucket_table()
    bias = _bias_table(rel_bias, bucket)

    h0 = _rms_fwd("conv_norm", x2d, conv_norm_g, deps=(gather_token,))
    (w_in_sm,) = gathered_group(0, "gather_wait_conv_in", (h0, dw_pad, bias))
    (u,) = _mm("conv_in", h0, w_in_sm, nt=False, b_sm=True, tm=1024, tn=512, rows=256, ep_in=((conv_b_in, "row"),),
               ep_fn=lambda acc, b: (acc + b,), outs=(("tile", BF),))
    cv, s_act = _conv_fwd(u, dw_pad, conv_dw_b, conv_ln_g, conv_ln_b)
    (g_out,) = gathered_group(1, "gather_wait_conv_out", (s_act,))
    w_out_f = g_out.reshape(Dm, Dm)
    x1, h1 = _mm("conv_out", s_act, w_out_f, nt=False, tm=1024, tn=1024, rows=256,
                 ep_in=((conv_b_out, "row"), (x2d, "tile"), (mlp_norm_g[0:1], "row")), ep_fn=_residual_norm_ep,
                 outs=RESIDUAL_NORM_OUTS)

    g_up0, g_down0 = gathered_group(2, "gather_wait_mlp0", (x1,))
    w_up_sm = [g_up0, None]
    w_down_f = [g_down0.reshape(D_FF, Dm), None]
    up0 = _mlp_up(0, h1, w_up_sm[0])
    x2, h2 = _mm("mlp0_down", up0, w_down_f[0], nt=False, tm=512, tn=1024, rows=256, a_fn=_relu2,
                 ep_in=((x1, "tile"), (attn_norm_f, "row")), ep_fn=_residual_norm_ep, outs=RESIDUAL_NORM_OUTS)

    g_qkv, g_o = gathered_group(3, "gather_wait_attn", (x2,))
    w_qkv_f = jnp.transpose(g_qkv, (1, 0, 2)).reshape(Dm, QKV_DIM)
    w_o_f = g_o.reshape(ATTN_DIM, Dm)
    (qkv,) = _mm("attn_qkv", h2, w_qkv_f, nt=False, tm=1024, tn=QKV_DIM, rows=256, ep_in=((b_qkv_f, "row"),),
                 ep_fn=lambda acc, b: (acc + b,), outs=(("tile", F32),))
    qg_t = jnp.tile(q_norm_g, (1, N_HEADS))
    kg_t = jnp.tile(k_norm_g, (1, N_KV))
    qn, kn, vv = _qk_norm_fwd(qkv, qg_t, kg_t)
    sinks1 = sinks[0]
    att = _attn_fwd(qn, kn, vv, bias, sinks1)
    x3, h3 = _mm("attn_out", att, w_o_f, nt=False, tm=1024, tn=1024, rows=256,
                 ep_in=((b_o_f, "row"), (x2, "tile"), (mlp_norm_g[1:2], "row")), ep_fn=_residual_norm_ep,
                 outs=RESIDUAL_NORM_OUTS)

    g_up1, g_down1 = gathered_group(4, "gather_wait_mlp1", (x3,))
    w_up_sm[1] = g_up1
    w_down_f[1] = g_down1.reshape(D_FF, Dm)
    up1 = _mlp_up(1, h3, w_up_sm[1])

    def loss_ep(acc, r, t):
        diff = acc + r - t
        dy = diff * (1.0 / Dm)
        return dy, dy, jnp.sum(diff * diff, axis=0, keepdims=True)

    dy, dy_bf, sq = _mm("mlp1_down_loss", up1, w_down_f[1], nt=False, tm=512, tn=1024, rows=256, a_fn=_relu2,
                        ep_in=((x3, "tile"), (tgt, "tile")), ep_fn=loss_ep,
                        outs=(("tile", F32), ("tile", BF), ("colsum", F32)))
    loss = lax.psum(0.5 * jnp.sum(sq) * (1.0 / Dm), ("x", "y", "c"))

    place = jnp.stack([shard, lax.axis_index("c")]).astype(jnp.int32)
    dx3, dx3_bf, dg_mlp1, db_o, dup1 = _mlp_bwd(1, dy, dy_bf, x3, mlp_norm_g[1:2], up1, w_up_sm[1], w_down_f[1])
    dw_down1 = _mm_tn("mlp1_dw_down", up1, dy_bf, tm=1024, tn=1024, tk=2048, a_fn=_relu2)
    dw_up1 = _mm_tn("mlp1_dw_up", h3, dup1, tm=1024, tn=1024, tk=2048, out_sm=N_SHARD)
    red_mlp1 = _Reduction("mlp1", [dw_up1, dw_down1.reshape(N_SHARD, D_FF // N_SHARD, Dm)], place)

    ident = lambda acc: (acc,)
    (datt,) = _mm("attn_dout", dx3_bf, w_o_f, nt=True, tm=1024, tn=1024, rows=256, ep_fn=ident, outs=(("tile", BF),),
                  deps=(red_mlp1.token,))
    dw_o = _mm_tn("attn_dw_o", att, dx3_bf, tm=1024, tn=1024, tk=2048)
    dqn, dkn, dvv, dbias, dsinks = _attn_bwd(qn, kn, vv, bias, sinks1, datt)
    drel = _bias_grad(dbias, bucket)
    dqkv, db_qkv, dqg_t, dkg_t = _qk_norm_bwd(qkv, dqn, dkn, dvv, qg_t, kg_t)
    dw_qkv = _mm_tn("attn_dw_qkv", h2, dqkv, tm=1024, tn=QKV_DIM, tk=2048)
    red_attn = _Reduction("attn", [jnp.transpose(dw_qkv.reshape(Dm, N_SHARD, QKV_DIM // N_SHARD), (1, 0, 2)),
                                   dw_o.reshape(N_SHARD, ATTN_DIM // N_SHARD, Dm)], place)
    dx2, dx2_bf, dg_attn, _ = _mm("attn_dx", dqkv, w_qkv_f, nt=True, tm=512, tn=1024, rows=256,
                                  ep_in=((x2, "tile"), (attn_norm_f, "row"), (dx3, "tile")), ep_fn=_rms_bwd_ep,
                                  outs=RMS_BWD_OUTS, deps=(red_attn.token,))

    dx1, dx1_bf, dg_mlp0, db_out, dup0 = _mlp_bwd(0, dx2, dx2_bf, x1, mlp_norm_g[0:1], up0, w_up_sm[0], w_down_f[0])
    dw_down0 = _mm_tn("mlp0_dw_down", up0, dx2_bf, tm=1024, tn=1024, tk=2048, a_fn=_relu2)
    dw_up0 = _mm_tn("mlp0_dw_up", h1, dup0, tm=1024, tn=1024, tk=2048, out_sm=N_SHARD)
    red_mlp0 = _Reduction("mlp0", [dw_up0, dw_down0.reshape(N_SHARD, D_FF // N_SHARD, Dm)], place)
    (r_qkv, r_o) = red_attn.finish((dx1,))
    (r_up1, r_down1) = red_mlp1.finish((dx1,))

    dcv, dln_g, dln_b, ddw_b = _mm("conv_ds", dx1_bf, w_out_f, nt=True, tm=512, tn=1024, rows=256,
                                   ep_in=((cv, "tile"), (conv_ln_g, "row"), (conv_ln_b, "row")),
                                   ep_fn=_ln_silu_bwd_ep,
                                   outs=(("tile", F32), ("colsum", F32), ("colsum", F32), ("colsum", F32)),
                                   deps=(red_mlp0.token,))
    dw_out = _mm_tn("conv_dw_out", s_act, dx1_bf, tm=1024, tn=1024, tk=2048)
    du, db_in, ddw8 = _conv_bwd(u, dcv, dw_pad)
    (r_up0, r_down0) = red_mlp0.finish((du,))
    dw_in = _mm_tn("conv_dw_in", h0, du, tm=1024, tn=512, tk=2048, out_sm=N_SHARD)
    red_conv = _Reduction("conv", [dw_in, dw_out.reshape(N_SHARD, Dm // N_SHARD, Dm)], place)
    def first_layer_ep(*args):
        tot, _, dg, _ = _rms_bwd_ep(*args)
        return tot, dg

    gx, dg_conv = _mm("conv_dx", du, w_in_sm, nt=True, b_sm=True, tm=512, tn=1024, rows=256,
                      ep_in=((x2d, "tile"), (conv_norm_g, "row"), (dx1, "tile")), ep_fn=first_layer_ep,
                      outs=(("tile", F32), ("colsum", F32)), deps=(red_conv.token,))
    (r_in, r_out) = red_conv.finish((gx,))
    mine = [r_in, r_out, r_qkv, r_o, r_up0, r_up1, r_down0, r_down1]
    r_in, r_out, r_qkv, r_o, r_up0, r_up1, r_down0, r_down1 = zip(mine, _join_halves("join_halves", mine))

    big_out = {}
    for nm, w, m, v, gs in (("conv_w_in", conv_w_in, m_conv_w_in, v_conv_w_in, (r_in,)),
                            ("conv_w_out", conv_w_out, m_conv_w_out, v_conv_w_out, (r_out,)),
                            ("w_qkv", w_qkv, m_w_qkv, v_w_qkv, (r_qkv,)),
                            ("w_o", w_o, m_w_o, v_w_o, (r_o,)),
                            ("w_up", w_up, m_w_up, v_w_up, (r_up0, r_up1)),
                            ("w_down", w_down, m_w_down, v_w_down, (r_down0, r_down1))):
        big_out[nm] = _adamw(f"adamw_{nm}", w, m, v, gs)

    dqg = dqg_t.reshape(N_HEADS, HEAD_DIM).sum(axis=0, keepdims=True)
    dkg = dkg_t.reshape(N_KV, HEAD_DIM).sum(axis=0, keepdims=True)
    small_full = [dg_conv, db_in, ddw8.sum(axis=1)[:CONV_W], ddw_b, dln_g, dln_b, db_out, dg_attn, db_qkv, dqg, dkg,
                  dsinks[None, :], db_o, drel.reshape(1, REL_BUCKETS * N_HEADS), jnp.pad(dg_mlp0, ((0, 1), (0, 0))) + jnp.pad(dg_mlp1, ((1, 0), (0, 0)))]
    _, small_sum = _gather8("reduce_small_grads", _pack(small_full), with_sum=True)
    (r_norm, r_b_in, r_dw, r_dw_b, r_ln_g, r_ln_b, r_b_out, r_attn_norm, r_b_qkv, r_qg, r_kg, r_sinks, r_b_o, r_rel,
     r_mlp_norm) = _unpack(small_sum, [a.shape for a in small_full])

    def cols(a, width):
        return lax.dynamic_slice_in_dim(a, shard * width, width, axis=a.ndim - 1)

    small_names = ["conv_norm_g", "conv_b_in", "conv_dw", "conv_dw_b", "conv_ln_g", "conv_ln_b", "conv_b_out",
                   "attn_norm_g", "b_qkv", "q_norm_g", "k_norm_g", "sinks", "b_o", "rel_bias", "mlp_norm_g"]
    small_g = [r_norm, r_b_in, cols(r_dw, Dm // N_SHARD)[None], r_dw_b, r_ln_g, r_ln_b, r_b_out,
               cols(r_attn_norm, Dm // N_SHARD), cols(r_b_qkv, QKV_DIM // N_SHARD), r_qg, r_kg, r_sinks,
               cols(r_b_o, Dm // N_SHARD), r_rel.reshape(REL_BUCKETS, N_HEADS), r_mlp_norm]
    small_w = [conv_norm_g, conv_b_in, conv_dw, conv_dw_b, conv_ln_g, conv_ln_b, conv_b_out, attn_norm_g, b_qkv,
               q_norm_g, k_norm_g, sinks, b_o, rel_bias, mlp_norm_g]
    small_m = [m_conv_norm_g, m_conv_b_in, m_conv_dw, m_conv_dw_b, m_conv_ln_g, m_conv_ln_b, m_conv_b_out,
               m_attn_norm_g, m_b_qkv, m_q_norm_g, m_k_norm_g, m_sinks, m_b_o, m_rel_bias, m_mlp_norm_g]
    small_v = [v_conv_norm_g, v_conv_b_in, v_conv_dw, v_conv_dw_b, v_conv_ln_g, v_conv_ln_b, v_conv_b_out,
               v_attn_norm_g, v_b_qkv, v_q_norm_g, v_k_norm_g, v_sinks, v_b_o, v_rel_bias, v_mlp_norm_g]
    flat2 = lambda a: a.reshape(-1, a.shape[-1])
    shapes2 = [flat2(w).shape for w in small_w]
    pk = lambda arrs: _pack([flat2(a) for a in arrs])
    packed_g = pk(small_g)
    d_s, m_s, v_s = _adamw_small(pk(small_w), packed_g, pk(small_m), pk(small_v))
    small_out = {}
    for nm, w, g, d, m2, v2 in zip(small_names, small_w, _unpack(packed_g, shapes2), _unpack(d_s, shapes2),
                                   _unpack(m_s, shapes2), _unpack(v_s, shapes2)):
        small_out[nm] = tuple(a.reshape(w.shape) for a in (g, d, m2, v2))

    order = ["conv_norm_g", "conv_w_in", "conv_b_in", "conv_dw", "conv_dw_b", "conv_ln_g", "conv_ln_b", "conv_w_out",
             "conv_b_out", "attn_norm_g", "w_qkv", "b_qkv", "q_norm_g", "k_norm_g", "sinks", "w_o", "b_o", "rel_bias",
             "mlp_norm_g", "w_up", "w_down"]
    res = {**small_out, **big_out}
    outs = [loss, gx[None]]
    for part in range(4):
        outs += [res[nm][part] for nm in order]
    return tuple(outs)
```

```python
import math

import numpy as np
import jax
import jax.numpy as jnp
from jax import lax
from jax.experimental import pallas as pl
from jax.experimental.pallas import tpu as pltpu

F32 = jnp.float32
BF = jnp.bfloat16
MESH = pl.DeviceIdType.MESH

D_MODEL = 1024
D_FF = 4096
N_HEADS = 16
N_KV = 2
GROUP = N_HEADS // N_KV
HEAD_DIM = 64
ATTN_DIM = N_HEADS * HEAD_DIM
KV_DIM = N_KV * HEAD_DIM
QKV_DIM = ATTN_DIM + 2 * KV_DIM
BLOCK = 128
CONV_W = 31
HALO = 32
REL_BUCKETS = 32
REL_MAX_DIST = 128
NORM_EPS = 1e-6
NEG_INF = -1e30
N_SHARD = 4
LANES = 1024

ADAM_LR = 0.001
ADAM_B1 = 0.9
ADAM_B2 = 0.999
ADAM_EPS = 1e-08
ADAM_WD = 0.01
ADAM_STEP = 10

VMEM_LIMIT = 56 * 1024 * 1024


def _params(n_axes):
    return pltpu.CompilerParams(dimension_semantics=("arbitrary",) * n_axes, vmem_limit_bytes=VMEM_LIMIT)


def _dot(a, b, ca, cb):
    return lax.dot_general(a, b, (((ca,), (cb,)), ((), ())), preferred_element_type=F32)


def _mm(name, a, b, *, nt, tm, tn, ep_fn, outs, a_fn=None, b_sm=False, ep_in=(), deps=(), rows=None):
    M, K = a.shape
    rows = tm if rows is None else rows
    if b_sm:
        S, ks = b.shape[0], b.shape[2]
        N, per = (b.shape[1], None) if nt else (S * b.shape[2], b.shape[2] // tn)
        assert (S * ks == K) if nt else (b.shape[1] == K)
    else:
        N = b.shape[0] if nt else b.shape[1]
        assert (b.shape[1] if nt else b.shape[0]) == K
    assert M % tm == 0 and N % tn == 0 and tm % rows == 0
    ne, no, nd = len(ep_in), len(outs), len(deps)

    def body(a_ref, b_ref, *rest):
        ep_refs, out_refs = rest[:ne], rest[ne + nd:ne + nd + no]
        i = pl.program_id(1)
        sums = [None] * no
        for r in range(tm // rows):
            rs = pl.ds(r * rows, rows)

            def lhs(cols):
                av = a_ref[rs, cols]
                return (av if a_fn is None else a_fn(av)).astype(BF)

            if b_sm and nt:
                acc = None
                for s in range(S):
                    part = _dot(lhs(pl.ds(s * ks, ks)), b_ref[s].astype(BF), 1, 1)
                    acc = part if acc is None else acc + part
            else:
                acc = _dot(lhs(slice(None)), b_ref[...].astype(BF), 1, 1 if nt else 0)
            ep_vals = [ref[rs, :] if kind == "tile" else ref[...] for ref, (_, kind) in zip(ep_refs, ep_in)]
            vals = ep_fn(acc, *ep_vals)
            for o, ((kind, dt), ref, val) in enumerate(zip(outs, out_refs, vals)):
                if kind == "tile":
                    ref[rs, :] = val.astype(dt)
                else:
                    sums[o] = val if sums[o] is None else sums[o] + val
        for (kind, dt), ref, val in zip(outs, out_refs, sums):
            if kind == "colsum":
                @pl.when(i == 0)
                def _():
                    ref[...] = val

                @pl.when(i > 0)
                def _():
                    ref[...] += val

    if b_sm and nt:
        b_spec = pl.BlockSpec((S, tn, ks), lambda j, i: (0, j, 0))
    elif b_sm:
        b_spec = pl.BlockSpec((None, K, tn), lambda j, i: (j // per, 0, j % per))
    elif nt:
        b_spec = pl.BlockSpec((tn, K), lambda j, i: (j, 0))
    else:
        b_spec = pl.BlockSpec((K, tn), lambda j, i: (0, j))
    in_specs = [pl.BlockSpec((tm, K), lambda j, i: (i, 0)), b_spec]
    for arr, kind in ep_in:
        if kind == "tile":
            assert arr.shape == (M, N)
            in_specs.append(pl.BlockSpec((tm, tn), lambda j, i: (i, j)))
        else:
            assert arr.shape == (1, N)
            in_specs.append(pl.BlockSpec((1, tn), lambda j, i: (0, j)))
    in_specs += [pl.BlockSpec(memory_space=pl.ANY)] * nd
    out_shape, out_specs = [], []
    for kind, dt in outs:
        if kind == "tile":
            out_shape.append(jax.ShapeDtypeStruct((M, N), dt))
            out_specs.append(pl.BlockSpec((tm, tn), lambda j, i: (i, j)))
        else:
            out_shape.append(jax.ShapeDtypeStruct((1, N), F32))
            out_specs.append(pl.BlockSpec((1, tn), lambda j, i: (0, j)))
    return pl.pallas_call(
        body, name=name, grid=(N // tn, M // tm), in_specs=in_specs, out_specs=out_specs, out_shape=out_shape,
        compiler_params=_params(2),
    )(a, b, *[arr for arr, _ in ep_in], *deps)


def _mm_tn(name, a, b, *, tm, tn, tk, a_fn=None, out_sm=None):
    T, Ka = a.shape
    N = b.shape[1]
    assert b.shape[0] == T and T % tk == 0 and Ka % tm == 0 and N % tn == 0
    nk = T // tk

    def body(a_ref, b_ref, o_ref, acc_ref):
        k = pl.program_id(2)

        @pl.when(k == 0)
        def _():
            acc_ref[...] = jnp.zeros_like(acc_ref)

        av = a_ref[...]
        if a_fn is not None:
            av = a_fn(av)
        acc_ref[...] += _dot(av.astype(BF), b_ref[...].astype(BF), 0, 0)

        @pl.when(k == nk - 1)
        def _():
            o_ref[...] = acc_ref[...].astype(BF)

    if out_sm is None:
        out_shape = jax.ShapeDtypeStruct((Ka, N), BF)
        out_spec = pl.BlockSpec((tm, tn), lambda i, j, k: (i, j))
    else:
        per = (N // out_sm) // tn
        assert per * tn * out_sm == N
        out_shape = jax.ShapeDtypeStruct((out_sm, Ka, N // out_sm), BF)
        out_spec = pl.BlockSpec((None, tm, tn), lambda i, j, k: (j // per, i, j % per))
    return pl.pallas_call(
        body, name=name, grid=(Ka // tm, N // tn, nk),
        in_specs=[pl.BlockSpec((tk, tm), lambda i, j, k: (k, i)), pl.BlockSpec((tk, tn), lambda i, j, k: (k, j))],
        out_specs=out_spec, out_shape=out_shape, scratch_shapes=[pltpu.VMEM((tm, tn), F32)],
        compiler_params=_params(3),
    )(a, b)


def _relu2(v):
    r = jnp.maximum(v.astype(F32), 0.0)
    return r * r


def _rms_bwd_ep(dh, x, g, dres):
    rstd = lax.rsqrt(jnp.mean(x * x, axis=-1, keepdims=True) + NORM_EPS)
    xh = x * rstd
    dxh = dh * g
    dx = rstd * (dxh - xh * jnp.mean(dxh * xh, axis=-1, keepdims=True))
    tot = dres + dx
    return tot, tot, jnp.sum(dh * xh, axis=0, keepdims=True), jnp.sum(tot, axis=0, keepdims=True)


def _rms_fwd(name, x, g, tm=512, deps=()):
    T, Dm = x.shape

    def body(x_ref, g_ref, *rest):
        o_ref = rest[-1]
        xv = x_ref[...]
        rstd = lax.rsqrt(jnp.mean(xv * xv, axis=-1, keepdims=True) + NORM_EPS)
        o_ref[...] = (xv * rstd * g_ref[...]).astype(BF)

    return pl.pallas_call(
        body, name=name, grid=(T // tm,),
        in_specs=[pl.BlockSpec((tm, Dm), lambda i: (i, 0)), pl.BlockSpec((1, Dm), lambda i: (0, 0))]
        + [pl.BlockSpec(memory_space=pl.ANY)] * len(deps),
        out_specs=pl.BlockSpec((tm, Dm), lambda i: (i, 0)), out_shape=jax.ShapeDtypeStruct((T, Dm), BF),
        compiler_params=_params(1),
    )(x, g, *deps)


HEAD_COLS = 128


def _two_term_dot(v, m):
    hi = v.astype(BF)
    lo = (v - hi.astype(F32)).astype(BF)
    return _dot(hi, m, 1, 0) + _dot(lo, m, 1, 0)


def _head_sum(v, select):
    sel, sel_t = select
    return _two_term_dot(_two_term_dot(v, sel), sel_t)


def _head_select(n):
    sel = (np.arange(n)[:, None] // HEAD_DIM == np.arange(HEAD_COLS)[None, :]).astype(np.float32)
    return jnp.asarray(sel, dtype=BF), jnp.asarray(sel.T, dtype=BF)


def _qk_norm_fwd(qkv, qg_t, kg_t, tm=256):
    T = qkv.shape[0]
    scale = 1.0 / math.sqrt(HEAD_DIM)

    def body(x_ref, qg_ref, kg_ref, sq_ref, sqt_ref, sk_ref, skt_ref, q_ref, k_ref, v_ref):
        q = x_ref[:, pl.ds(0, ATTN_DIM)]
        rq = lax.rsqrt(_head_sum(q * q, (sq_ref[...], sqt_ref[...])) * (1.0 / HEAD_DIM) + NORM_EPS)
        q_ref[...] = (q * rq * qg_ref[...] * scale).astype(BF)
        k = x_ref[:, pl.ds(ATTN_DIM, KV_DIM)]
        rk = lax.rsqrt(_head_sum(k * k, (sk_ref[...], skt_ref[...])) * (1.0 / HEAD_DIM) + NORM_EPS)
        k_ref[...] = (k * rk * kg_ref[...]).astype(BF)
        v_ref[...] = x_ref[:, pl.ds(ATTN_DIM + KV_DIM, KV_DIM)].astype(BF)

    full = lambda shape: pl.BlockSpec(shape, lambda i: (0, 0))
    return pl.pallas_call(
        body, name="qk_norm_fwd", grid=(T // tm,),
        in_specs=[pl.BlockSpec((tm, QKV_DIM), lambda i: (i, 0)), full((1, ATTN_DIM)), full((1, KV_DIM)),
                  full((ATTN_DIM, HEAD_COLS)), full((HEAD_COLS, ATTN_DIM)), full((KV_DIM, HEAD_COLS)), full((HEAD_COLS, KV_DIM))],
        out_specs=[pl.BlockSpec((tm, ATTN_DIM), lambda i: (i, 0)), pl.BlockSpec((tm, KV_DIM), lambda i: (i, 0)),
                   pl.BlockSpec((tm, KV_DIM), lambda i: (i, 0))],
        out_shape=[jax.ShapeDtypeStruct((T, ATTN_DIM), BF), jax.ShapeDtypeStruct((T, KV_DIM), BF),
                   jax.ShapeDtypeStruct((T, KV_DIM), BF)],
        compiler_params=_params(1),
    )(qkv, qg_t, kg_t, *_head_select(ATTN_DIM), *_head_select(KV_DIM))


def _qk_norm_bwd(qkv, dqn, dkn, dv, qg_t, kg_t, tm=256):
    T = qkv.shape[0]

    def body(x_ref, dq_ref, dk_ref, dv_ref, qg_ref, kg_ref, sq_ref, sqt_ref, sk_ref, skt_ref,
             o_ref, db_ref, dqg_ref, dkg_ref):
        i = pl.program_id(0)

        def one(x, dy, g, select):
            r = lax.rsqrt(_head_sum(x * x, select) * (1.0 / HEAD_DIM) + NORM_EPS)
            xh = x * r
            dxh = dy * g
            dx = r * (dxh - xh * (_head_sum(dxh * xh, select) * (1.0 / HEAD_DIM)))
            return dx, jnp.sum(dy * xh, axis=0, keepdims=True)

        dq, dqg = one(x_ref[:, pl.ds(0, ATTN_DIM)], dq_ref[...], qg_ref[...], (sq_ref[...], sqt_ref[...]))
        dk, dkg = one(x_ref[:, pl.ds(ATTN_DIM, KV_DIM)], dk_ref[...], kg_ref[...], (sk_ref[...], skt_ref[...]))
        dvv = dv_ref[...]
        o_ref[:, pl.ds(0, ATTN_DIM)] = dq.astype(BF)
        o_ref[:, pl.ds(ATTN_DIM, KV_DIM)] = dk.astype(BF)
        o_ref[:, pl.ds(ATTN_DIM + KV_DIM, KV_DIM)] = dvv.astype(BF)
        sq, sk, sv = (jnp.sum(t, axis=0, keepdims=True) for t in (dq, dk, dvv))

        @pl.when(i == 0)
        def _():
            db_ref[:, pl.ds(0, ATTN_DIM)] = sq
            db_ref[:, pl.ds(ATTN_DIM, KV_DIM)] = sk
            db_ref[:, pl.ds(ATTN_DIM + KV_DIM, KV_DIM)] = sv
            dqg_ref[...] = dqg
            dkg_ref[...] = dkg

        @pl.when(i > 0)
        def _():
            db_ref[:, pl.ds(0, ATTN_DIM)] += sq
            db_ref[:, pl.ds(ATTN_DIM, KV_DIM)] += sk
            db_ref[:, pl.ds(ATTN_DIM + KV_DIM, KV_DIM)] += sv
            dqg_ref[...] += dqg
            dkg_ref[...] += dkg

    full = lambda shape: pl.BlockSpec(shape, lambda i: (0, 0))
    row = lambda n: pl.BlockSpec((tm, n), lambda i: (i, 0))
    return pl.pallas_call(
        body, name="qk_norm_bwd", grid=(T // tm,),
        in_specs=[row(QKV_DIM), row(ATTN_DIM), row(KV_DIM), row(KV_DIM), full((1, ATTN_DIM)), full((1, KV_DIM)),
                  full((ATTN_DIM, HEAD_COLS)), full((HEAD_COLS, ATTN_DIM)), full((KV_DIM, HEAD_COLS)), full((HEAD_COLS, KV_DIM))],
        out_specs=[row(QKV_DIM), full((1, QKV_DIM)), full((1, ATTN_DIM)), full((1, KV_DIM))],
        out_shape=[jax.ShapeDtypeStruct((T, QKV_DIM), BF), jax.ShapeDtypeStruct((1, QKV_DIM), F32),
                   jax.ShapeDtypeStruct((1, ATTN_DIM), F32), jax.ShapeDtypeStruct((1, KV_DIM), F32)],
        compiler_params=_params(1),
    )(qkv, dqn, dkn, dv, qg_t, kg_t, *_head_select(ATTN_DIM), *_head_select(KV_DIM))


ROWS = 64
COLS = 128


SUBLANES = 8
FIRST_TAP = HALO - (CONV_W - 1)


def _glu(a, g):
    return a.astype(F32) * jax.nn.sigmoid(g.astype(F32))


def _shifted(xe, s):
    return xe if s == 0 else pltpu.roll(xe, ROWS + HALO - s, axis=0)


def _conv_fwd(u, dw_pad, dw_b, ln_g, ln_b, tm=256):
    T = u.shape[0]
    Dm = D_MODEL
    hpt = tm // HALO

    def body(ac_ref, gc_ref, ap_ref, gp_ref, w_ref, wb_ref, lg_ref, lb_ref, cv_ref, s_ref, ext):
        i = pl.program_id(0)
        ext[pl.ds(0, HALO), :] = jnp.where(i > 0, _glu(ap_ref[...], gp_ref[...]), 0.0)
        ext[pl.ds(HALO, tm), :] = _glu(ac_ref[...], gc_ref[...])

        def rows(r, carry):
            r0 = pl.multiple_of(r * ROWS, ROWS)
            for c in range(Dm // COLS):
                cs = pl.ds(c * COLS, COLS)
                xe = ext[pl.ds(r0, ROWS + HALO), cs]
                acc = jnp.zeros((ROWS, COLS), F32)
                for s in range(SUBLANES):
                    xs = _shifted(xe, s)
                    for j in range(CONV_W):
                        off = FIRST_TAP + j
                        if off % SUBLANES == s:
                            acc = acc + xs[off - s:off - s + ROWS, :] * w_ref[pl.ds(j, 1), cs]
                cv_ref[pl.ds(r0, ROWS), cs] = acc + wb_ref[:, cs]
            return carry

        lax.fori_loop(0, tm // ROWS, rows, 0)
        cv = cv_ref[...]
        xc = cv - jnp.mean(cv, axis=-1, keepdims=True)
        y = xc * lax.rsqrt(jnp.mean(xc * xc, axis=-1, keepdims=True) + NORM_EPS) * lg_ref[...] + lb_ref[...]
        s_ref[...] = (y * jax.nn.sigmoid(y)).astype(BF)

    full = lambda shape: pl.BlockSpec(shape, lambda i: (0, 0))
    return pl.pallas_call(
        body, name="conv_fwd", grid=(T // tm,),
        in_specs=[pl.BlockSpec((tm, Dm), lambda i: (i, 0)), pl.BlockSpec((tm, Dm), lambda i: (i, 1)),
                  pl.BlockSpec((HALO, Dm), lambda i: (jnp.maximum(i * hpt - 1, 0), 0)),
                  pl.BlockSpec((HALO, Dm), lambda i: (jnp.maximum(i * hpt - 1, 0), 1)),
                  full((HALO, Dm)), full((1, Dm)), full((1, Dm)), full((1, Dm))],
        out_specs=[pl.BlockSpec((tm, Dm), lambda i: (i, 0)), pl.BlockSpec((tm, Dm), lambda i: (i, 0))],
        out_shape=[jax.ShapeDtypeStruct((T, Dm), F32), jax.ShapeDtypeStruct((T, Dm), BF)],
        scratch_shapes=[pltpu.VMEM((tm + HALO, Dm), F32)],
        compiler_params=_params(1),
    )(u, u, u, u, dw_pad, dw_b, ln_g, ln_b)


def _ln_silu_bwd_ep(ds, cv, lg, lb):
    xc = cv - jnp.mean(cv, axis=-1, keepdims=True)
    rstd = lax.rsqrt(jnp.mean(xc * xc, axis=-1, keepdims=True) + NORM_EPS)
    xh = xc * rstd
    y = xh * lg + lb
    sg = jax.nn.sigmoid(y)
    dy = ds * (sg * (1.0 + y * (1.0 - sg)))
    dxh = dy * lg
    dcv = rstd * (dxh - jnp.mean(dxh, axis=-1, keepdims=True) - xh * jnp.mean(dxh * xh, axis=-1, keepdims=True))
    return (dcv, jnp.sum(dy * xh, axis=0, keepdims=True), jnp.sum(dy, axis=0, keepdims=True),
            jnp.sum(dcv, axis=0, keepdims=True))


def _conv_bwd(u, dcv, dw_pad, tm=256):
    T = u.shape[0]
    Dm = D_MODEL
    hpt = tm // HALO
    last = T // HALO - 1
    nt = T // tm

    def body(ac_ref, gc_ref, ap_ref, gp_ref, dc_ref, dn_ref, w_ref, du_ref, db_ref, dw_ref, ext_g, ext_d):
        i = pl.program_id(0)
        ext_g[pl.ds(0, HALO), :] = jnp.where(i > 0, _glu(ap_ref[...], gp_ref[...]), 0.0)
        ext_g[pl.ds(HALO, tm), :] = _glu(ac_ref[...], gc_ref[...])
        ext_d[pl.ds(0, tm), :] = dc_ref[...]
        ext_d[pl.ds(tm, HALO), :] = jnp.where(i < nt - 1, dn_ref[...], 0.0)

        @pl.when(i == 0)
        def _():
            db_ref[...] = jnp.zeros_like(db_ref)
            dw_ref[...] = jnp.zeros_like(dw_ref)

        def rows(r, carry):
            r0 = pl.multiple_of(r * ROWS, ROWS)
            rs = pl.ds(r0, ROWS)
            for c in range(Dm // COLS):
                cs = pl.ds(c * COLS, COLS)
                cs2 = pl.ds(Dm + c * COLS, COLS)
                de = ext_d[pl.ds(r0, ROWS + HALO), cs]
                ge = ext_g[pl.ds(r0, ROWS + HALO), cs]
                dcur = de[0:ROWS, :]
                acc = jnp.zeros((ROWS, COLS), F32)
                for s in range(SUBLANES):
                    ds_, gs_ = _shifted(de, s), _shifted(ge, s)
                    for j in range(CONV_W):
                        off = CONV_W - 1 - j
                        if off % SUBLANES == s:
                            acc = acc + ds_[off - s:off - s + ROWS, :] * w_ref[pl.ds(j, 1), cs]
                        goff = FIRST_TAP + j
                        if goff % SUBLANES == s:
                            prod = dcur * gs_[goff - s:goff - s + ROWS, :]
                            dw_ref[j, :, cs] += jnp.sum(prod.reshape(ROWS // SUBLANES, SUBLANES, COLS), axis=0)
                a = ac_ref[rs, cs].astype(F32)
                sg = jax.nn.sigmoid(gc_ref[rs, cs].astype(F32))
                da = acc * sg
                dg = acc * a * sg * (1.0 - sg)
                du_ref[rs, cs] = da.astype(BF)
                du_ref[rs, cs2] = dg.astype(BF)
                db_ref[:, cs] += jnp.sum(da, axis=0, keepdims=True)
                db_ref[:, cs2] += jnp.sum(dg, axis=0, keepdims=True)
            return carry

        lax.fori_loop(0, tm // ROWS, rows, 0)

    return pl.pallas_call(
        body, name="conv_bwd", grid=(nt,),
        in_specs=[pl.BlockSpec((tm, Dm), lambda i: (i, 0)), pl.BlockSpec((tm, Dm), lambda i: (i, 1)),
                  pl.BlockSpec((HALO, Dm), lambda i: (jnp.maximum(i * hpt - 1, 0), 0)),
                  pl.BlockSpec((HALO, Dm), lambda i: (jnp.maximum(i * hpt - 1, 0), 1)),
                  pl.BlockSpec((tm, Dm), lambda i: (i, 0)),
                  pl.BlockSpec((HALO, Dm), lambda i: (jnp.minimum((i + 1) * hpt, last), 0)),
                  pl.BlockSpec((HALO, Dm), lambda i: (0, 0))],
        out_specs=[pl.BlockSpec((tm, 2 * Dm), lambda i: (i, 0)), pl.BlockSpec((1, 2 * Dm), lambda i: (0, 0)),
                   pl.BlockSpec((HALO, 8, Dm), lambda i: (0, 0, 0))],
        out_shape=[jax.ShapeDtypeStruct((T, 2 * Dm), BF), jax.ShapeDtypeStruct((1, 2 * Dm), F32),
                   jax.ShapeDtypeStruct((HALO, 8, Dm), F32)],
        scratch_shapes=[pltpu.VMEM((tm + HALO, Dm), F32), pltpu.VMEM((tm + HALO, Dm), F32)],
        compiler_params=_params(1),
    )(u, u, u, u, dcv, dcv, dw_pad)


def _bucket_table():
    q_loc = np.arange(BLOCK)[:, None]
    k_loc = np.arange(2 * BLOCK)[None, :]
    dist = q_loc + BLOCK - k_loc
    n = np.maximum(dist, 0)
    max_exact = REL_BUCKETS // 2
    large = max_exact + (np.log(np.maximum(n, 1).astype(np.float32) / max_exact)
                         / math.log(REL_MAX_DIST / max_exact) * (REL_BUCKETS - max_exact)).astype(np.int32)
    large = np.minimum(large, REL_BUCKETS - 1)
    bucket = np.where(n < max_exact, n, large).astype(np.int32)
    return jnp.asarray(np.where((dist >= 0) & (dist < BLOCK), bucket, -1).astype(np.int32))


def _bias_table(rel_bias, bucket):
    def body(rb_ref, bk_ref, o_ref):
        bk = bk_ref[...]
        for h in range(N_HEADS):
            acc = jnp.full((BLOCK, 2 * BLOCK), NEG_INF, F32)
            for b in range(REL_BUCKETS):
                acc = jnp.where(bk == b, rb_ref[b, h], acc)
            o_ref[h] = acc

    return pl.pallas_call(
        body, name="bias_table", out_shape=jax.ShapeDtypeStruct((N_HEADS, BLOCK, 2 * BLOCK), F32),
        in_specs=[pl.BlockSpec(memory_space=pltpu.SMEM), pl.BlockSpec(memory_space=pltpu.VMEM)],
        out_specs=pl.BlockSpec(memory_space=pltpu.VMEM),
    )(rel_bias, bucket)


def _bias_grad(dbias, bucket):
    def body(db_ref, bk_ref, o_ref):
        bk = bk_ref[...]
        for b in range(REL_BUCKETS):
            sel = bk == b
            for h in range(N_HEADS):
                o_ref[b, h] = jnp.sum(jnp.where(sel, db_ref[h], 0.0))

    return pl.pallas_call(
        body, name="bias_grad", out_shape=jax.ShapeDtypeStruct((REL_BUCKETS, N_HEADS), F32),
        in_specs=[pl.BlockSpec(memory_space=pltpu.VMEM), pl.BlockSpec(memory_space=pltpu.VMEM)],
        out_specs=pl.BlockSpec(memory_space=pltpu.SMEM),
    )(dbias, bucket)


GROUP_ROWS = GROUP * BLOCK


def _head_probs(qk, bias_h, sink, first):
    s = jnp.where(first, NEG_INF, qk + bias_h)
    m = jnp.maximum(jnp.max(s, axis=-1, keepdims=True), sink)
    p = jnp.exp(s - m)
    ps = jnp.exp(sink - m)
    inv = 1.0 / (jnp.sum(p, axis=-1, keepdims=True) + ps)
    return p * inv, ps * inv


def _band(prev_ref, cur_ref, g):
    hs = pl.ds(g * HEAD_DIM, HEAD_DIM)
    return jnp.concatenate([prev_ref[:, hs], cur_ref[:, hs]], axis=0)


def _stack_heads(ref, g):
    return jnp.concatenate([ref[:, pl.ds((g * GROUP + hh) * HEAD_DIM, HEAD_DIM)] for hh in range(GROUP)], axis=0)


def _unstack_heads(ref, g, stacked, dtype):
    for hh in range(GROUP):
        ref[:, pl.ds((g * GROUP + hh) * HEAD_DIM, HEAD_DIM)] = stacked[hh * BLOCK:(hh + 1) * BLOCK, :].astype(dtype)


def _first_mask(n):
    col = lax.broadcasted_iota(jnp.int32, (1, 2 * BLOCK), 1)
    return jnp.logical_and(n == 0, col < BLOCK)


def _head_rows(hh):
    return pl.ds(hh * BLOCK, BLOCK)


def _attn_fwd(qn, kn, vv, bias, sinks):
    T = qn.shape[0]
    nb = T // BLOCK

    def body(sk_ref, q_ref, kc_ref, kp_ref, vc_ref, vp_ref, b_ref, o_ref, qk_buf, p_buf):
        first = _first_mask(pl.program_id(0))
        for g in range(N_KV):
            k = _band(kp_ref, kc_ref, g)
            v = _band(vp_ref, vc_ref, g)
            qk_buf[...] = _dot(_stack_heads(q_ref, g), k, 1, 1)
            for hh in range(GROUP):
                h = g * GROUP + hh
                pn, _ = _head_probs(qk_buf[_head_rows(hh), :], b_ref[h], sk_ref[h], first)
                p_buf[_head_rows(hh), :] = pn.astype(BF)
            _unstack_heads(o_ref, g, _dot(p_buf[...], v, 1, 0), BF)

    cur = lambda n: (n, 0)
    prev = lambda n: (jnp.maximum(n - 1, 0), 0)
    return pl.pallas_call(
        body, name="attn_fwd", grid=(nb,),
        in_specs=[pl.BlockSpec(memory_space=pltpu.SMEM), pl.BlockSpec((BLOCK, ATTN_DIM), cur),
                  pl.BlockSpec((BLOCK, KV_DIM), cur), pl.BlockSpec((BLOCK, KV_DIM), prev),
                  pl.BlockSpec((BLOCK, KV_DIM), cur), pl.BlockSpec((BLOCK, KV_DIM), prev),
                  pl.BlockSpec((N_HEADS, BLOCK, 2 * BLOCK), lambda n: (0, 0, 0))],
        out_specs=pl.BlockSpec((BLOCK, ATTN_DIM), cur), out_shape=jax.ShapeDtypeStruct((T, ATTN_DIM), BF),
        scratch_shapes=[pltpu.VMEM((GROUP_ROWS, 2 * BLOCK), F32), pltpu.VMEM((GROUP_ROWS, 2 * BLOCK), BF)],
        compiler_params=_params(1),
    )(sinks, qn, kn, kn, vv, vv, bias)


def _attn_bwd(qn, kn, vv, bias, sinks, do):
    T = qn.shape[0]
    nb = T // BLOCK
    scale = 1.0 / math.sqrt(HEAD_DIM)

    def body(sk_ref, q_ref, kc_ref, kp_ref, vc_ref, vp_ref, b_ref, do_ref,
             dq_ref, dk_ref, dv_ref, db_ref, dsk_ref, dk_full, dv_full, dk_carry, dv_carry, qk_buf, dp_buf, p_buf, ds_buf):
        n = pl.program_id(0)

        @pl.when(n == 0)
        def _():
            db_ref[...] = jnp.zeros_like(db_ref)
            dk_carry[...] = jnp.zeros_like(dk_carry)
            dv_carry[...] = jnp.zeros_like(dv_carry)
            for h in range(N_HEADS):
                dsk_ref[h] = 0.0

        @pl.when(n < nb)
        def _():
            first = _first_mask(n)
            for g in range(N_KV):
                k = _band(kp_ref, kc_ref, g)
                v = _band(vp_ref, vc_ref, g)
                q = _stack_heads(q_ref, g)
                dout = _stack_heads(do_ref, g)
                qk_buf[...] = _dot(q, k, 1, 1)
                dp_buf[...] = _dot(dout, v, 1, 1)
                for hh in range(GROUP):
                    h = g * GROUP + hh
                    rows = _head_rows(hh)
                    pn, psink = _head_probs(qk_buf[rows, :], b_ref[h], sk_ref[h], first)
                    dp = dp_buf[rows, :]
                    delta = jnp.sum(pn * dp, axis=-1, keepdims=True)
                    ds = pn * (dp - delta)
                    dsk_ref[h] += -jnp.sum(psink * delta)
                    db_ref[h] += ds
                    ds_buf[rows, :] = ds.astype(BF)
                    p_buf[rows, :] = pn.astype(BF)
                dsb = ds_buf[...]
                _unstack_heads(dq_ref, g, _dot(dsb, k, 1, 0) * scale, F32)
                gs = pl.ds(g * HEAD_DIM, HEAD_DIM)
                dk_full[:, gs] = _dot(dsb, q, 0, 0)
                dv_full[:, gs] = _dot(p_buf[...], dout, 0, 0)

        @pl.when(n == nb)
        def _():
            dk_full[...] = jnp.zeros_like(dk_full)
            dv_full[...] = jnp.zeros_like(dv_full)

        dk_ref[...] = dk_carry[...] + dk_full[pl.ds(0, BLOCK), :]
        dv_ref[...] = dv_carry[...] + dv_full[pl.ds(0, BLOCK), :]
        dk_carry[...] = dk_full[pl.ds(BLOCK, BLOCK), :]
        dv_carry[...] = dv_full[pl.ds(BLOCK, BLOCK), :]

    cur = lambda n: (jnp.minimum(n, nb - 1), 0)
    prev = lambda n: (jnp.maximum(jnp.minimum(n, nb - 1) - 1, 0), 0)
    out_kv = lambda n: (jnp.maximum(n - 1, 0), 0)
    return pl.pallas_call(
        body, name="attn_bwd", grid=(nb + 1,),
        in_specs=[pl.BlockSpec(memory_space=pltpu.SMEM), pl.BlockSpec((BLOCK, ATTN_DIM), cur),
                  pl.BlockSpec((BLOCK, KV_DIM), cur), pl.BlockSpec((BLOCK, KV_DIM), prev),
                  pl.BlockSpec((BLOCK, KV_DIM), cur), pl.BlockSpec((BLOCK, KV_DIM), prev),
                  pl.BlockSpec((N_HEADS, BLOCK, 2 * BLOCK), lambda n: (0, 0, 0)),
                  pl.BlockSpec((BLOCK, ATTN_DIM), cur)],
        out_specs=[pl.BlockSpec((BLOCK, ATTN_DIM), cur), pl.BlockSpec((BLOCK, KV_DIM), out_kv),
                   pl.BlockSpec((BLOCK, KV_DIM), out_kv),
                   pl.BlockSpec((N_HEADS, BLOCK, 2 * BLOCK), lambda n: (0, 0, 0)),
                   pl.BlockSpec(memory_space=pltpu.SMEM)],
        out_shape=[jax.ShapeDtypeStruct((T, ATTN_DIM), F32), jax.ShapeDtypeStruct((T, KV_DIM), F32),
                   jax.ShapeDtypeStruct((T, KV_DIM), F32),
                   jax.ShapeDtypeStruct((N_HEADS, BLOCK, 2 * BLOCK), F32), jax.ShapeDtypeStruct((N_HEADS,), F32)],
        scratch_shapes=[pltpu.VMEM((2 * BLOCK, KV_DIM), F32), pltpu.VMEM((2 * BLOCK, KV_DIM), F32),
                        pltpu.VMEM((BLOCK, KV_DIM), F32), pltpu.VMEM((BLOCK, KV_DIM), F32),
                        pltpu.VMEM((GROUP_ROWS, 2 * BLOCK), F32), pltpu.VMEM((GROUP_ROWS, 2 * BLOCK), F32),
                        pltpu.VMEM((GROUP_ROWS, 2 * BLOCK), BF), pltpu.VMEM((GROUP_ROWS, 2 * BLOCK), BF)],
        compiler_params=_params(1),
    )(sinks, qn, kn, kn, vv, vv, bias, do)


def _coords():
    return lax.axis_index("x"), lax.axis_index("y"), lax.axis_index("c")


def _gather8(name, v, with_sum):
    R = v.shape[0]

    def body(v_ref, all_ref, *rest):
        sum_ref = rest[0] if with_sum else None
        send_sems, recv_sems, local_sem = rest[-3:]
        x, y, c = _coords()
        me = 4 * x + 2 * y + c
        local = pltpu.make_async_copy(v_ref, all_ref.at[me], local_sem)
        local.start()
        sends = []
        for k in range(1, 8):
            peer = (x ^ (k >> 2), y ^ ((k >> 1) & 1), c ^ (k & 1))
            cp = pltpu.make_async_remote_copy(src_ref=v_ref, dst_ref=all_ref.at[me], send_sem=send_sems.at[k - 1],
                                              recv_sem=recv_sems.at[k - 1], device_id=peer, device_id_type=MESH)
            cp.start()
            sends.append(cp)
        for k in range(1, 8):
            peer = (x ^ (k >> 2), y ^ ((k >> 1) & 1), c ^ (k & 1))
            pltpu.make_async_remote_copy(src_ref=v_ref, dst_ref=all_ref.at[me ^ k], send_sem=send_sems.at[k - 1],
                                         recv_sem=recv_sems.at[k - 1], device_id=peer, device_id_type=MESH).wait_recv()
        for cp in sends:
            cp.wait_send()
        local.wait()
        if with_sum:
            tot = all_ref[0]
            for d in range(1, 8):
                tot = tot + all_ref[d]
            sum_ref[...] = tot

    out_shape = [jax.ShapeDtypeStruct((8, R, LANES), F32)]
    if with_sum:
        out_shape.append(jax.ShapeDtypeStruct((R, LANES), F32))
    vm = pl.BlockSpec(memory_space=pltpu.VMEM)
    return pl.pallas_call(
        body, name=name, out_shape=out_shape, in_specs=[vm], out_specs=[vm] * len(out_shape),
        scratch_shapes=[pltpu.SemaphoreType.DMA((7,)), pltpu.SemaphoreType.DMA((7,)), pltpu.SemaphoreType.DMA],
    )(v)


CHIP_FLIPS = ((1, 0), (0, 1), (1, 1))


HBM_SPEC = pl.BlockSpec(memory_space=pltpu.HBM)
SEM_SPEC = pl.BlockSpec(memory_space=pltpu.SEMAPHORE)
ANY_SPEC = pl.BlockSpec(memory_space=pl.ANY)
DATAFLOW = pltpu.SideEffectType.DATAFLOW_SIDE_EFFECTING


def _chip_copy(land, sems, idx, slot_src, slot_dst, peer):
    send_sems, recv_sems = sems
    return pltpu.make_async_remote_copy(src_ref=land.at[slot_src], dst_ref=land.at[slot_dst], send_sem=send_sems.at[idx],
                                        recv_sem=recv_sems.at[idx], device_id=peer, device_id_type=MESH)


def _gather_start(stacks, groups, after):
    n = len(stacks)
    ng = len(groups)
    after = tuple(after)

    def body(*refs):
        lands = refs[:n]
        first = n + len(after)
        sems = [(refs[first + 2 * g], refs[first + 2 * g + 1]) for g in range(ng)]
        token = refs[-1]
        x, y, c = _coords()
        s = 2 * x + y
        for g, members in enumerate(groups):
            for i, t in enumerate(members):
                for j, (fx, fy) in enumerate(CHIP_FLIPS):
                    _chip_copy(lands[t], sems[g], 3 * i + j, s, s, (x ^ fx, y ^ fy, c)).start()
        token[...] = jnp.zeros_like(token)

    out_shape = []
    for members in groups:
        out_shape += [pltpu.SemaphoreType.DMA((3 * len(members),))] * 2
    out_shape += [pltpu.HBM(w.shape, w.dtype) for w in stacks]
    out_shape.append(jax.ShapeDtypeStruct((8, 128), F32))
    res = pl.pallas_call(
        body, name="gather_start", out_shape=out_shape, in_specs=[HBM_SPEC] * n + [ANY_SPEC] * len(after),
        out_specs=[SEM_SPEC] * (2 * ng) + [HBM_SPEC] * n + [pl.BlockSpec(memory_space=pltpu.VMEM)],
        input_output_aliases={t: 2 * ng + t for t in range(n)},
        compiler_params=pltpu.CompilerParams(has_side_effects=DATAFLOW),
    )(*[pltpu.with_memory_space_constraint(w, pltpu.HBM) for w in stacks], *after)
    sems = [(res[2 * g], res[2 * g + 1]) for g in range(ng)]
    return sems, list(res[2 * ng:2 * ng + n]), res[-1]


def _gather_wait(name, stacks, sems, after):
    n = len(stacks)
    after = tuple(after)

    def body(*refs):
        lands = refs[:n]
        group_sems = (refs[n], refs[n + 1])
        x, y, c = _coords()
        s = 2 * x + y
        for i in range(n):
            for j, (fx, fy) in enumerate(CHIP_FLIPS):
                cp = _chip_copy(lands[i], group_sems, 3 * i + j, s, 2 * (x ^ fx) + (y ^ fy), (x ^ fx, y ^ fy, c))
                cp.wait_send()
                cp.wait_recv()

    return pl.pallas_call(
        body, name=name, out_shape=[pltpu.HBM(w.shape, w.dtype) for w in stacks],
        in_specs=[HBM_SPEC] * n + [SEM_SPEC, SEM_SPEC] + [ANY_SPEC] * len(after), out_specs=[HBM_SPEC] * n,
        input_output_aliases={t: t for t in range(n)},
        compiler_params=pltpu.CompilerParams(has_side_effects=DATAFLOW),
    )(*stacks, sems[0], sems[1], *after)


N_PEERS = 7


def _peer(x, y, c, k):
    return x ^ (k >> 2), y ^ ((k >> 1) & 1), c ^ (k & 1)


def _reduce_copy(grad, land, sems, idx, x, y, c, k):
    px, py, pc = _peer(x, y, c, k)
    rh = grad.shape[1] // 2
    return pltpu.make_async_remote_copy(src_ref=grad.at[2 * px + py, pl.ds(pc * rh, rh), :], dst_ref=land.at[k - 1],
                                        send_sem=sems[0].at[idx], recv_sem=sems[1].at[idx], device_id=(px, py, pc),
                                        device_id_type=MESH)


def _reduce_start(name, grads):
    n = len(grads)

    def body(*refs):
        src, lands, sems, token = refs[:n], refs[n:2 * n], (refs[2 * n], refs[2 * n + 1]), refs[-1]
        x, y, c = _coords()
        for t in range(n):
            for k in range(1, N_PEERS + 1):
                _reduce_copy(src[t], lands[t], sems, N_PEERS * t + k - 1, x, y, c, k).start()
        token[...] = jnp.zeros_like(token)

    lands = [lax.empty((N_PEERS, g.shape[1] // 2, g.shape[2]), g.dtype) for g in grads]
    out_shape = [pltpu.SemaphoreType.DMA((N_PEERS * n,))] * 2
    out_shape += [pltpu.HBM(a.shape, a.dtype) for a in list(grads) + lands]
    out_shape.append(jax.ShapeDtypeStruct((8, 128), F32))
    res = pl.pallas_call(
        body, name=name, out_shape=out_shape, in_specs=[HBM_SPEC] * (2 * n),
        out_specs=[SEM_SPEC] * 2 + [HBM_SPEC] * (2 * n) + [pl.BlockSpec(memory_space=pltpu.VMEM)],
        input_output_aliases={t: 2 + t for t in range(2 * n)},
        compiler_params=pltpu.CompilerParams(has_side_effects=DATAFLOW),
    )(*[pltpu.with_memory_space_constraint(a, pltpu.HBM) for a in list(grads) + lands])
    return (res[0], res[1]), list(res[2:2 + n]), list(res[2 + n:2 + 2 * n]), res[-1]


def _reduce_wait(name, grads, lands, sems, after):
    n = len(grads)
    after = tuple(after)

    def body(*refs):
        src, dst, group_sems = refs[:n], refs[n:2 * n], (refs[2 * n], refs[2 * n + 1])
        x, y, c = _coords()
        for t in range(n):
            for k in range(1, N_PEERS + 1):
                cp = _reduce_copy(src[t], dst[t], group_sems, N_PEERS * t + k - 1, x, y, c, k)
                cp.wait_send()
                cp.wait_recv()

    res = pl.pallas_call(
        body, name=name, out_shape=[pltpu.HBM(a.shape, a.dtype) for a in list(grads) + list(lands)],
        in_specs=[HBM_SPEC] * (2 * n) + [SEM_SPEC, SEM_SPEC] + [ANY_SPEC] * len(after), out_specs=[HBM_SPEC] * (2 * n),
        input_output_aliases={t: t for t in range(2 * n)},
        compiler_params=pltpu.CompilerParams(has_side_effects=DATAFLOW),
    )(*grads, *lands, sems[0], sems[1], *after)
    return list(res[:n]), list(res[n:])


def _join_halves(name, halves):
    n = len(halves)

    def body(*refs):
        src, dst = refs[:n], refs[n:2 * n]
        send_sems, recv_sems = refs[2 * n:]
        x, y, c = _coords()
        cps = []
        for t in range(n):
            cp = pltpu.make_async_remote_copy(src_ref=src[t], dst_ref=dst[t], send_sem=send_sems.at[t],
                                              recv_sem=recv_sems.at[t], device_id=(x, y, 1 - c), device_id_type=MESH)
            cp.start()
            cps.append(cp)
        for cp in cps:
            cp.wait()

    anyspec = pl.BlockSpec(memory_space=pl.ANY)
    return pl.pallas_call(
        body, name=name, out_shape=[jax.ShapeDtypeStruct(h.shape, h.dtype) for h in halves],
        in_specs=[anyspec] * n, out_specs=[anyspec] * n,
        scratch_shapes=[pltpu.SemaphoreType.DMA((n,)), pltpu.SemaphoreType.DMA((n,))],
    )(*halves)


def _row_block(rows):
    for rb in (512, 256, 128, 64, 32, 16):
        if rows % rb == 0:
            return rb
    raise ValueError(rows)


def _sum_devices(name, grad, land, place):
    S, R, C = grad.shape
    rh = R // 2
    rb = _row_block(rh)
    nbh = rh // rb

    def body(place_ref, g_ref, l_ref, o_ref):
        tot = g_ref[...].astype(F32)
        for k in range(N_PEERS):
            tot = tot + l_ref[k].astype(F32)
        o_ref[...] = tot

    return pl.pallas_call(
        body, name=name,
        grid_spec=pltpu.PrefetchScalarGridSpec(
            num_scalar_prefetch=1, grid=(nbh,),
            in_specs=[pl.BlockSpec((None, rb, C), lambda r, place: (place[0], place[1] * nbh + r, 0)),
                      pl.BlockSpec((N_PEERS, rb, C), lambda r, place: (0, r, 0))],
            out_specs=pl.BlockSpec((rb, C), lambda r, place: (r, 0))),
        out_shape=jax.ShapeDtypeStruct((rh, C), F32), compiler_params=_params(1),
    )(place, grad, land)


def _adamw_math(w, g, m, v):
    m2 = ADAM_B1 * m + (1.0 - ADAM_B1) * g
    v2 = ADAM_B2 * v + (1.0 - ADAM_B2) * (g * g)
    m_hat = m2 / (1.0 - ADAM_B1 ** ADAM_STEP)
    v_hat = v2 / (1.0 - ADAM_B2 ** ADAM_STEP)
    delta = -ADAM_LR * (m_hat / (jnp.sqrt(v_hat) + ADAM_EPS) + ADAM_WD * w)
    return delta, m2, v2


def _adamw(name, w, m, v, gs):
    L, R, C = w.shape
    Rh = R // 2
    rb = _row_block(Rh)
    nbh = Rh // rb
    assert len(gs) == L

    def body(core_ref, w_ref, m_ref, v_ref, *rest):
        g_refs, (go_ref, d_ref, m2_ref, v2_ref) = rest[:2 * L], rest[2 * L:]
        layer, half = pl.program_id(0), pl.program_id(1)
        mine = half == core_ref[0]
        g = jnp.where(mine, g_refs[0][...], g_refs[1][...])
        for t in range(1, L):
            g = jnp.where(layer == t, jnp.where(mine, g_refs[2 * t][...], g_refs[2 * t + 1][...]), g)
        delta, m2, v2 = _adamw_math(w_ref[...], g, m_ref[...], v_ref[...])
        go_ref[...] = g
        d_ref[...] = delta
        m2_ref[...] = m2
        v2_ref[...] = v2

    wspec = pl.BlockSpec((None, rb, C), lambda l, h, r, core: (l, h * nbh + r, 0))
    gspec = pl.BlockSpec((rb, C), lambda l, h, r, core: (r, 0))
    return pl.pallas_call(
        body, name=name,
        grid_spec=pltpu.PrefetchScalarGridSpec(num_scalar_prefetch=1, grid=(L, 2, nbh),
                                               in_specs=[wspec] * 3 + [gspec] * (2 * L), out_specs=[wspec] * 4),
        out_shape=[jax.ShapeDtypeStruct((L, R, C), F32)] * 4, compiler_params=_params(3),
    )(lax.axis_index("c").astype(jnp.int32).reshape(1), w, m, v, *[g for pair in gs for g in pair])


def _adamw_small(w, g, m, v):
    def body(w_ref, g_ref, m_ref, v_ref, d_ref, m2_ref, v2_ref):
        delta, m2, v2 = _adamw_math(w_ref[...], g_ref[...], m_ref[...], v_ref[...])
        d_ref[...] = delta
        m2_ref[...] = m2
        v2_ref[...] = v2

    return pl.pallas_call(body, name="adamw_small", out_shape=[jax.ShapeDtypeStruct(w.shape, F32)] * 3)(w, g, m, v)


def _packed_rows(shape):
    c = shape[-1]
    return (int(np.prod(shape)) // c) * -(-c // LANES)


def _pack(arrays):
    total = sum(_packed_rows(a.shape) for a in arrays)
    total += -total % 8
    buf, r0 = None, 0
    for a in arrays:
        a = a.astype(F32).reshape(-1, a.shape[-1])
        r, c = a.shape
        k = -(-c // LANES)
        a = jnp.pad(a, ((0, 0), (0, k * LANES - c))).reshape(r * k, LANES)
        a = jnp.pad(a, ((r0, total - r0 - r * k), (0, 0)))
        buf = a if buf is None else buf + a
        r0 += r * k
    return buf


def _unpack(buf, shapes):
    out, r0 = [], 0
    for shp in shapes:
        c = shp[-1]
        rows = _packed_rows(shp)
        out.append(buf[r0:r0 + rows].reshape(-1, -(-c // LANES) * LANES)[:, :c].reshape(shp))
        r0 += rows
    return out


def _rms(x, g):
    return x * lax.rsqrt(jnp.mean(x * x, axis=-1, keepdims=True) + NORM_EPS) * g


def _residual_norm_ep(acc, *rest):
    *bias, res, gain = rest
    x = acc + res + (bias[0] if bias else 0.0)
    return x, _rms(x, gain)


RESIDUAL_NORM_OUTS = (("tile", F32), ("tile", BF))


def _mlp_up(tag, h, w_up_sm):
    (up,) = _mm(f"mlp{tag}_up", h, w_up_sm, nt=False, b_sm=True, tm=1024, tn=1024, rows=256,
                ep_fn=lambda acc: (acc,), outs=(("tile", BF),))
    return up


RMS_BWD_OUTS = (("tile", F32), ("tile", BF), ("colsum", F32), ("colsum", F32))


def _mlp_bwd(tag, dy, dy_bf, x, g, up, w_up_sm, w_down):
    (dup,) = _mm(f"mlp{tag}_dup", dy_bf, w_down, nt=True, tm=1024, tn=1024, rows=256, ep_in=((up, "tile"),),
                 ep_fn=lambda acc, u: (acc * (2.0 * jnp.maximum(u.astype(F32), 0.0)),), outs=(("tile", BF),))
    dx, dx_bf, dg, dx_sum = _mm(f"mlp{tag}_dx", dup, w_up_sm, nt=True, b_sm=True, tm=512, tn=1024, rows=256,
                                ep_in=((x, "tile"), (g, "row"), (dy, "tile")), ep_fn=_rms_bwd_ep, outs=RMS_BWD_OUTS)
    return dx, dx_bf, dg, dx_sum, dup


class _Reduction:
    def __init__(self, tag, grads, place):
        self.tag, self.place = tag, place
        self.sems, self.grads, self.lands, self.token = _reduce_start(f"reduce_start_{tag}", grads)

    def finish(self, after):
        grads, lands = _reduce_wait(f"reduce_wait_{self.tag}", self.grads, self.lands, self.sems, after)
        return [_sum_devices(f"reduce_sum_{self.tag}{i}", g, l, self.place) for i, (g, l) in enumerate(zip(grads, lands))]


def kernel(x, conv_norm_g, conv_w_in, conv_b_in, conv_dw, conv_dw_b, conv_ln_g, conv_ln_b, conv_w_out, conv_b_out, attn_norm_g, w_qkv, b_qkv, q_norm_g, k_norm_g, sinks, w_o, b_o, rel_bias, mlp_norm_g, w_up, w_down, loss_target, m_conv_norm_g, m_conv_w_in, m_conv_b_in, m_conv_dw, m_conv_dw_b, m_conv_ln_g, m_conv_ln_b, m_conv_w_out, m_conv_b_out, m_attn_norm_g, m_w_qkv, m_b_qkv, m_q_norm_g, m_k_norm_g, m_sinks, m_w_o, m_b_o, m_rel_bias, m_mlp_norm_g, m_w_up, m_w_down, v_conv_norm_g, v_conv_w_in, v_conv_b_in, v_conv_dw, v_conv_dw_b, v_conv_ln_g, v_conv_ln_b, v_conv_w_out, v_conv_b_out, v_attn_norm_g, v_w_qkv, v_b_qkv, v_q_norm_g, v_k_norm_g, v_sinks, v_w_o, v_b_o, v_rel_bias, v_mlp_norm_g, v_w_up, v_w_down):
    Dm = D_MODEL
    x2d = x[0]
    tgt = loss_target[0]
    T = x2d.shape[0]
    shard = 2 * lax.axis_index("x") + lax.axis_index("y")

    sharded_small = [conv_dw[0], attn_norm_g, b_qkv, b_o]
    (gathered,) = _gather8("gather_small_weights", _pack(sharded_small), with_sum=False)
    chips = [_unpack(gathered[2 * s], [a.shape for a in sharded_small]) for s in range(N_SHARD)]
    dw_f, attn_norm_f, b_qkv_f, b_o_f = (jnp.concatenate([chips[s][t] for s in range(N_SHARD)], axis=-1)
                                         for t in range(len(sharded_small)))
    dw_pad = jnp.pad(dw_f, ((0, HALO - CONV_W), (0, 0)))

    big = [conv_w_in[0], conv_w_out[0], w_qkv[0], w_o[0], w_up[0], w_up[1], w_down[0], w_down[1]]
    stacks = [lax.dynamic_update_slice(lax.empty((N_SHARD,) + w.shape, BF), w.astype(BF)[None], (shard, 0, 0))
              for w in big]
    groups = ((0,), (1,), (4, 6), (2, 3), (5, 7))
    gather_sems, stacks, gather_token = _gather_start(stacks, groups, after=(gathered,))

    def gathered_group(g, name, after):
        return _gather_wait(name, [stacks[t] for t in groups[g]], gather_sems[g], after)

    bucket = _bucket_table()
    bias = _bias_table(rel_bias, bucket)

    h0 = _rms_fwd("conv_norm", x2d, conv_norm_g, deps=(gather_token,))
    (w_in_sm,) = gathered_group(0, "gather_wait_conv_in", (h0, dw_pad, bias))
    (u,) = _mm("conv_in", h0, w_in_sm, nt=False, b_sm=True, tm=1024, tn=512, rows=256, ep_in=((conv_b_in, "row"),),
               ep_fn=lambda acc, b: (acc + b,), outs=(("tile", BF),))
    cv, s_act = _conv_fwd(u, dw_pad, conv_dw_b, conv_ln_g, conv_ln_b)
    (g_out,) = gathered_group(1, "gather_wait_conv_out", (s_act,))
    w_out_f = g_out.reshape(Dm, Dm)
    x1, h1 = _mm("conv_out", s_act, w_out_f, nt=False, tm=1024, tn=1024, rows=256,
                 ep_in=((conv_b_out, "row"), (x2d, "tile"), (mlp_norm_g[0:1], "row")), ep_fn=_residual_norm_ep,
                 outs=RESIDUAL_NORM_OUTS)

    g_up0, g_down0 = gathered_group(2, "gather_wait_mlp0", (x1,))
    w_up_sm = [g_up0, None]
    w_down_f = [g_down0.reshape(D_FF, Dm), None]
    up0 = _mlp_up(0, h1, w_up_sm[0])
    x2, h2 = _mm("mlp0_down", up0, w_down_f[0], nt=False, tm=512, tn=1024, rows=256, a_fn=_relu2,
                 ep_in=((x1, "tile"), (attn_norm_f, "row")), ep_fn=_residual_norm_ep, outs=RESIDUAL_NORM_OUTS)

    g_qkv, g_o = gathered_group(3, "gather_wait_attn", (x2,))
    w_qkv_f = jnp.transpose(g_qkv, (1, 0, 2)).reshape(Dm, QKV_DIM)
    w_o_f = g_o.reshape(ATTN_DIM, Dm)
    (qkv,) = _mm("attn_qkv", h2, w_qkv_f, nt=False, tm=1024, tn=QKV_DIM, rows=256, ep_in=((b_qkv_f, "row"),),
                 ep_fn=lambda acc, b: (acc + b,), outs=(("tile", F32),))
    qg_t = jnp.tile(q_norm_g, (1, N_HEADS))
    kg_t = jnp.tile(k_norm_g, (1, N_KV))
    qn, kn, vv = _qk_norm_fwd(qkv, qg_t, kg_t)
    sinks1 = sinks[0]
    att = _attn_fwd(qn, kn, vv, bias, sinks1)
    x3, h3 = _mm("attn_out", att, w_o_f, nt=False, tm=1024, tn=1024, rows=256,
                 ep_in=((b_o_f, "row"), (x2, "tile"), (mlp_norm_g[1:2], "row")), ep_fn=_residual_norm_ep,
                 outs=RESIDUAL_NORM_OUTS)

    g_up1, g_down1 = gathered_group(4, "gather_wait_mlp1", (x3,))
    w_up_sm[1] = g_up1
    w_down_f[1] = g_down1.reshape(D_FF, Dm)
    up1 = _mlp_up(1, h3, w_up_sm[1])

    def loss_ep(acc, r, t):
        diff = acc + r - t
        dy = diff * (1.0 / Dm)
        return dy, dy, jnp.sum(diff * diff, axis=0, keepdims=True)

    dy, dy_bf, sq = _mm("mlp1_down_loss", up1, w_down_f[1], nt=False, tm=512, tn=1024, rows=256, a_fn=_relu2,
                        ep_in=((x3, "tile"), (tgt, "tile")), ep_fn=loss_ep,
                        outs=(("tile", F32), ("tile", BF), ("colsum", F32)))

    place = jnp.stack([shard, lax.axis_index("c")]).astype(jnp.int32)
    dx3, dx3_bf, dg_mlp1, db_o, dup1 = _mlp_bwd(1, dy, dy_bf, x3, mlp_norm_g[1:2], up1, w_up_sm[1], w_down_f[1])
    dw_down1 = _mm_tn("mlp1_dw_down", up1, dy_bf, tm=1024, tn=1024, tk=2048, a_fn=_relu2)
    dw_up1 = _mm_tn("mlp1_dw_up", h3, dup1, tm=1024, tn=1024, tk=2048, out_sm=N_SHARD)
    red_mlp1 = _Reduction("mlp1", [dw_up1, dw_down1.reshape(N_SHARD, D_FF // N_SHARD, Dm)], place)

    ident = lambda acc: (acc,)
    (datt,) = _mm("attn_dout", dx3_bf, w_o_f, nt=True, tm=1024, tn=1024, rows=256, ep_fn=ident, outs=(("tile", BF),),
                  deps=(red_mlp1.token,))
    dw_o = _mm_tn("attn_dw_o", att, dx3_bf, tm=1024, tn=1024, tk=2048)
    dqn, dkn, dvv, dbias, dsinks = _attn_bwd(qn, kn, vv, bias, sinks1, datt)
    drel = _bias_grad(dbias, bucket)
    dqkv, db_qkv, dqg_t, dkg_t = _qk_norm_bwd(qkv, dqn, dkn, dvv, qg_t, kg_t)
    dw_qkv = _mm_tn("attn_dw_qkv", h2, dqkv, tm=1024, tn=QKV_DIM, tk=2048)
    red_attn = _Reduction("attn", [jnp.transpose(dw_qkv.reshape(Dm, N_SHARD, QKV_DIM // N_SHARD), (1, 0, 2)),
                                   dw_o.reshape(N_SHARD, ATTN_DIM // N_SHARD, Dm)], place)
    dx2, dx2_bf, dg_attn, _ = _mm("attn_dx", dqkv, w_qkv_f, nt=True, tm=512, tn=1024, rows=256,
                                  ep_in=((x2, "tile"), (attn_norm_f, "row"), (dx3, "tile")), ep_fn=_rms_bwd_ep,
                                  outs=RMS_BWD_OUTS, deps=(red_attn.token,))

    dx1, dx1_bf, dg_mlp0, db_out, dup0 = _mlp_bwd(0, dx2, dx2_bf, x1, mlp_norm_g[0:1], up0, w_up_sm[0], w_down_f[0])
    dw_down0 = _mm_tn("mlp0_dw_down", up0, dx2_bf, tm=1024, tn=1024, tk=2048, a_fn=_relu2)
    dw_up0 = _mm_tn("mlp0_dw_up", h1, dup0, tm=1024, tn=1024, tk=2048, out_sm=N_SHARD)
    red_mlp0 = _Reduction("mlp0", [dw_up0, dw_down0.reshape(N_SHARD, D_FF // N_SHARD, Dm)], place)
    (r_qkv, r_o) = red_attn.finish((dx1,))
    (r_up1, r_down1) = red_mlp1.finish((dx1,))

    dcv, dln_g, dln_b, ddw_b = _mm("conv_ds", dx1_bf, w_out_f, nt=True, tm=512, tn=1024, rows=256,
                                   ep_in=((cv, "tile"), (conv_ln_g, "row"), (conv_ln_b, "row")),
                                   ep_fn=_ln_silu_bwd_ep,
                                   outs=(("tile", F32), ("colsum", F32), ("colsum", F32), ("colsum", F32)),
                                   deps=(red_mlp0.token,))
    dw_out = _mm_tn("conv_dw_out", s_act, dx1_bf, tm=1024, tn=1024, tk=2048)
    du, db_in, ddw8 = _conv_bwd(u, dcv, dw_pad)
    (r_up0, r_down0) = red_mlp0.finish((du,))
    dw_in = _mm_tn("conv_dw_in", h0, du, tm=1024, tn=512, tk=2048, out_sm=N_SHARD)
    red_conv = _Reduction("conv", [dw_in, dw_out.reshape(N_SHARD, Dm // N_SHARD, Dm)], place)
    def first_layer_ep(*args):
        tot, _, dg, _ = _rms_bwd_ep(*args)
        return tot, dg

    gx, dg_conv = _mm("conv_dx", du, w_in_sm, nt=True, b_sm=True, tm=512, tn=1024, rows=256,
                      ep_in=((x2d, "tile"), (conv_norm_g, "row"), (dx1, "tile")), ep_fn=first_layer_ep,
                      outs=(("tile", F32), ("colsum", F32)), deps=(red_conv.token,))
    (r_in, r_out) = red_conv.finish((gx,))
    mine = [r_in, r_out, r_qkv, r_o, r_up0, r_up1, r_down0, r_down1]
    r_in, r_out, r_qkv, r_o, r_up0, r_up1, r_down0, r_down1 = zip(mine, _join_halves("join_halves", mine))

    big_out = {}
    for nm, w, m, v, gs in (("conv_w_in", conv_w_in, m_conv_w_in, v_conv_w_in, (r_in,)),
                            ("conv_w_out", conv_w_out, m_conv_w_out, v_conv_w_out, (r_out,)),
                            ("w_qkv", w_qkv, m_w_qkv, v_w_qkv, (r_qkv,)),
                            ("w_o", w_o, m_w_o, v_w_o, (r_o,)),
                            ("w_up", w_up, m_w_up, v_w_up, (r_up0, r_up1)),
                            ("w_down", w_down, m_w_down, v_w_down, (r_down0, r_down1))):
        big_out[nm] = _adamw(f"adamw_{nm}", w, m, v, gs)

    dqg = dqg_t.reshape(N_HEADS, HEAD_DIM).sum(axis=0, keepdims=True)
    dkg = dkg_t.reshape(N_KV, HEAD_DIM).sum(axis=0, keepdims=True)
    small_full = [dg_conv, db_in, ddw8.sum(axis=1)[:CONV_W], ddw_b, dln_g, dln_b, db_out, dg_attn, db_qkv, dqg, dkg,
                  dsinks[None, :], db_o, drel.reshape(1, REL_BUCKETS * N_HEADS),
                  jnp.pad(dg_mlp0, ((0, 1), (0, 0))) + jnp.pad(dg_mlp1, ((1, 0), (0, 0))), sq]
    _, small_sum = _gather8("reduce_small_grads", _pack(small_full), with_sum=True)
    (r_norm, r_b_in, r_dw, r_dw_b, r_ln_g, r_ln_b, r_b_out, r_attn_norm, r_b_qkv, r_qg, r_kg, r_sinks, r_b_o, r_rel,
     r_mlp_norm, r_sq) = _unpack(small_sum, [a.shape for a in small_full])
    loss = 0.5 * jnp.sum(r_sq) * (1.0 / Dm)

    def cols(a, width):
        return lax.dynamic_slice_in_dim(a, shard * width, width, axis=a.ndim - 1)

    small_names = ["conv_norm_g", "conv_b_in", "conv_dw", "conv_dw_b", "conv_ln_g", "conv_ln_b", "conv_b_out",
                   "attn_norm_g", "b_qkv", "q_norm_g", "k_norm_g", "sinks", "b_o", "rel_bias", "mlp_norm_g"]
    small_g = [r_norm, r_b_in, cols(r_dw, Dm // N_SHARD)[None], r_dw_b, r_ln_g, r_ln_b, r_b_out,
               cols(r_attn_norm, Dm // N_SHARD), cols(r_b_qkv, QKV_DIM // N_SHARD), r_qg, r_kg, r_sinks,
               cols(r_b_o, Dm // N_SHARD), r_rel.reshape(REL_BUCKETS, N_HEADS), r_mlp_norm]
    small_w = [conv_norm_g, conv_b_in, conv_dw, conv_dw_b, conv_ln_g, conv_ln_b, conv_b_out, attn_norm_g, b_qkv,
               q_norm_g, k_norm_g, sinks, b_o, rel_bias, mlp_norm_g]
    small_m = [m_conv_norm_g, m_conv_b_in, m_conv_dw, m_conv_dw_b, m_conv_ln_g, m_conv_ln_b, m_conv_b_out,
               m_attn_norm_g, m_b_qkv, m_q_norm_g, m_k_norm_g, m_sinks, m_b_o, m_rel_bias, m_mlp_norm_g]
    small_v = [v_conv_norm_g, v_conv_b_in, v_conv_dw, v_conv_dw_b, v_conv_ln_g, v_conv_ln_b, v_conv_b_out,
               v_attn_norm_g, v_b_qkv, v_q_norm_g, v_k_norm_g, v_sinks, v_b_o, v_rel_bias, v_mlp_norm_g]
    flat2 = lambda a: a.reshape(-1, a.shape[-1])
    shapes2 = [flat2(w).shape for w in small_w]
    pk = lambda arrs: _pack([flat2(a) for a in arrs])
    packed_g = pk(small_g)
    d_s, m_s, v_s = _adamw_small(pk(small_w), packed_g, pk(small_m), pk(small_v))
    small_out = {}
    for nm, w, g, d, m2, v2 in zip(small_names, small_w, _unpack(packed_g, shapes2), _unpack(d_s, shapes2),
                                   _unpack(m_s, shapes2), _unpack(v_s, shapes2)):
        small_out[nm] = tuple(a.reshape(w.shape) for a in (g, d, m2, v2))

    order = ["conv_norm_g", "conv_w_in", "conv_b_in", "conv_dw", "conv_dw_b", "conv_ln_g", "conv_ln_b", "conv_w_out",
             "conv_b_out", "attn_norm_g", "w_qkv", "b_qkv", "q_norm_g", "k_norm_g", "sinks", "w_o", "b_o", "rel_bias",
             "mlp_norm_g", "w_up", "w_down"]
    res = {**small_out, **big_out}
    outs = [loss, gx[None]]
    for part in range(4):
        outs += [res[nm][part] for nm in order]
    return tuple(outs)
```

```python
import math

import numpy as np
import jax
import jax.numpy as jnp
from jax import lax
from jax.experimental import pallas as pl
from jax.experimental.pallas import tpu as pltpu

F32 = jnp.float32
BF = jnp.bfloat16
MESH = pl.DeviceIdType.MESH

D_MODEL = 1024
D_FF = 4096
N_HEADS = 16
N_KV = 2
GROUP = N_HEADS // N_KV
HEAD_DIM = 64
ATTN_DIM = N_HEADS * HEAD_DIM
KV_DIM = N_KV * HEAD_DIM
QKV_DIM = ATTN_DIM + 2 * KV_DIM
BLOCK = 128
CONV_W = 31
HALO = 32
REL_BUCKETS = 32
REL_MAX_DIST = 128
NORM_EPS = 1e-6
NEG_INF = -1e30
N_SHARD = 4
LANES = 1024

ADAM_LR = 0.001
ADAM_B1 = 0.9
ADAM_B2 = 0.999
ADAM_EPS = 1e-08
ADAM_WD = 0.01
ADAM_STEP = 10

VMEM_LIMIT = 56 * 1024 * 1024


def _params(n_axes):
    return pltpu.CompilerParams(dimension_semantics=("arbitrary",) * n_axes, vmem_limit_bytes=VMEM_LIMIT)


def _dot(a, b, ca, cb):
    return lax.dot_general(a, b, (((ca,), (cb,)), ((), ())), preferred_element_type=F32)


def _mm(name, a, b, *, nt, tm, tn, ep_fn, outs, b_sm=False, ep_in=(), deps=(), rows=None):
    M, K = a.shape
    rows = tm if rows is None else rows
    if b_sm:
        S, ks = b.shape[0], b.shape[2]
        N, per = (b.shape[1], None) if nt else (S * b.shape[2], b.shape[2] // tn)
        assert (S * ks == K) if nt else (b.shape[1] == K)
    else:
        N = b.shape[0] if nt else b.shape[1]
        assert (b.shape[1] if nt else b.shape[0]) == K
    assert M % tm == 0 and N % tn == 0 and tm % rows == 0
    ne, no, nd = len(ep_in), len(outs), len(deps)

    def body(a_ref, b_ref, *rest):
        ep_refs, out_refs = rest[:ne], rest[ne + nd:ne + nd + no]
        i = pl.program_id(1)
        sums = [None] * no
        for r in range(tm // rows):
            rs = pl.ds(r * rows, rows)

            def lhs(cols):
                return a_ref[rs, cols].astype(BF)

            if b_sm and nt:
                acc = None
                for s in range(S):
                    part = _dot(lhs(pl.ds(s * ks, ks)), b_ref[s].astype(BF), 1, 1)
                    acc = part if acc is None else acc + part
            else:
                acc = _dot(lhs(slice(None)), b_ref[...].astype(BF), 1, 1 if nt else 0)
            ep_vals = [ref[rs, :] if kind == "tile" else ref[...] for ref, (_, kind) in zip(ep_refs, ep_in)]
            vals = ep_fn(acc, *ep_vals)
            for o, ((kind, dt), ref, val) in enumerate(zip(outs, out_refs, vals)):
                if kind == "tile":
                    ref[rs, :] = val.astype(dt)
                else:
                    sums[o] = val if sums[o] is None else sums[o] + val
        for (kind, dt), ref, val in zip(outs, out_refs, sums):
            if kind == "colsum":
                @pl.when(i == 0)
                def _():
                    ref[...] = val

                @pl.when(i > 0)
                def _():
                    ref[...] += val

    if b_sm and nt:
        b_spec = pl.BlockSpec((S, tn, ks), lambda j, i: (0, j, 0))
    elif b_sm:
        b_spec = pl.BlockSpec((None, K, tn), lambda j, i: (j // per, 0, j % per))
    elif nt:
        b_spec = pl.BlockSpec((tn, K), lambda j, i: (j, 0))
    else:
        b_spec = pl.BlockSpec((K, tn), lambda j, i: (0, j))
    in_specs = [pl.BlockSpec((tm, K), lambda j, i: (i, 0)), b_spec]
    for arr, kind in ep_in:
        if kind == "tile":
            assert arr.shape == (M, N)
            in_specs.append(pl.BlockSpec((tm, tn), lambda j, i: (i, j)))
        else:
            assert arr.shape == (1, N)
            in_specs.append(pl.BlockSpec((1, tn), lambda j, i: (0, j)))
    in_specs += [pl.BlockSpec(memory_space=pl.ANY)] * nd
    out_shape, out_specs = [], []
    for kind, dt in outs:
        if kind == "tile":
            out_shape.append(jax.ShapeDtypeStruct((M, N), dt))
            out_specs.append(pl.BlockSpec((tm, tn), lambda j, i: (i, j)))
        else:
            out_shape.append(jax.ShapeDtypeStruct((1, N), F32))
            out_specs.append(pl.BlockSpec((1, tn), lambda j, i: (0, j)))
    return pl.pallas_call(
        body, name=name, grid=(N // tn, M // tm), in_specs=in_specs, out_specs=out_specs, out_shape=out_shape,
        compiler_params=_params(2),
    )(a, b, *[arr for arr, _ in ep_in], *deps)


def _mm_tn(name, a, b, *, tm, tn, tk, out_sm=None):
    T, Ka = a.shape
    N = b.shape[1]
    assert b.shape[0] == T and T % tk == 0 and Ka % tm == 0 and N % tn == 0
    nk = T // tk

    def body(a_ref, b_ref, o_ref, acc_ref):
        k = pl.program_id(2)

        @pl.when(k == 0)
        def _():
            acc_ref[...] = jnp.zeros_like(acc_ref)

        acc_ref[...] += _dot(a_ref[...].astype(BF), b_ref[...].astype(BF), 0, 0)

        @pl.when(k == nk - 1)
        def _():
            o_ref[...] = acc_ref[...].astype(BF)

    if out_sm is None:
        out_shape = jax.ShapeDtypeStruct((Ka, N), BF)
        out_spec = pl.BlockSpec((tm, tn), lambda i, j, k: (i, j))
    else:
        per = (N // out_sm) // tn
        assert per * tn * out_sm == N
        out_shape = jax.ShapeDtypeStruct((out_sm, Ka, N // out_sm), BF)
        out_spec = pl.BlockSpec((None, tm, tn), lambda i, j, k: (j // per, i, j % per))
    return pl.pallas_call(
        body, name=name, grid=(Ka // tm, N // tn, nk),
        in_specs=[pl.BlockSpec((tk, tm), lambda i, j, k: (k, i)), pl.BlockSpec((tk, tn), lambda i, j, k: (k, j))],
        out_specs=out_spec, out_shape=out_shape, scratch_shapes=[pltpu.VMEM((tm, tn), F32)],
        compiler_params=_params(3),
    )(a, b)


def _relu2(v):
    r = jnp.maximum(v.astype(F32), 0.0)
    return r * r


def _rms_bwd_ep(dh, x, g, dres):
    rstd = lax.rsqrt(jnp.mean(x * x, axis=-1, keepdims=True) + NORM_EPS)
    xh = x * rstd
    dxh = dh * g
    dx = rstd * (dxh - xh * jnp.mean(dxh * xh, axis=-1, keepdims=True))
    tot = dres + dx
    return tot, tot, jnp.sum(dh * xh, axis=0, keepdims=True), jnp.sum(tot, axis=0, keepdims=True)


def _rms_fwd(name, x, g, tm=512, deps=()):
    T, Dm = x.shape

    def body(x_ref, g_ref, *rest):
        o_ref = rest[-1]
        xv = x_ref[...]
        rstd = lax.rsqrt(jnp.mean(xv * xv, axis=-1, keepdims=True) + NORM_EPS)
        o_ref[...] = (xv * rstd * g_ref[...]).astype(BF)

    return pl.pallas_call(
        body, name=name, grid=(T // tm,),
        in_specs=[pl.BlockSpec((tm, Dm), lambda i: (i, 0)), pl.BlockSpec((1, Dm), lambda i: (0, 0))]
        + [pl.BlockSpec(memory_space=pl.ANY)] * len(deps),
        out_specs=pl.BlockSpec((tm, Dm), lambda i: (i, 0)), out_shape=jax.ShapeDtypeStruct((T, Dm), BF),
        compiler_params=_params(1),
    )(x, g, *deps)


HEAD_COLS = 128


def _two_term_dot(v, m):
    hi = v.astype(BF)
    lo = (v - hi.astype(F32)).astype(BF)
    return _dot(hi, m, 1, 0) + _dot(lo, m, 1, 0)


def _head_sum(v, select):
    sel, sel_t = select
    return _two_term_dot(_two_term_dot(v, sel), sel_t)


def _head_select(n):
    sel = (np.arange(n)[:, None] // HEAD_DIM == np.arange(HEAD_COLS)[None, :]).astype(np.float32)
    return jnp.asarray(sel, dtype=BF), jnp.asarray(sel.T, dtype=BF)


def _qk_norm_fwd(qkv, qg_t, kg_t, tm=256):
    T = qkv.shape[0]
    scale = 1.0 / math.sqrt(HEAD_DIM)

    def body(x_ref, qg_ref, kg_ref, sq_ref, sqt_ref, sk_ref, skt_ref, q_ref, k_ref, v_ref):
        q = x_ref[:, pl.ds(0, ATTN_DIM)]
        rq = lax.rsqrt(_head_sum(q * q, (sq_ref[...], sqt_ref[...])) * (1.0 / HEAD_DIM) + NORM_EPS)
        q_ref[...] = (q * rq * qg_ref[...] * scale).astype(BF)
        k = x_ref[:, pl.ds(ATTN_DIM, KV_DIM)]
        rk = lax.rsqrt(_head_sum(k * k, (sk_ref[...], skt_ref[...])) * (1.0 / HEAD_DIM) + NORM_EPS)
        k_ref[...] = (k * rk * kg_ref[...]).astype(BF)
        v_ref[...] = x_ref[:, pl.ds(ATTN_DIM + KV_DIM, KV_DIM)].astype(BF)

    full = lambda shape: pl.BlockSpec(shape, lambda i: (0, 0))
    return pl.pallas_call(
        body, name="qk_norm_fwd", grid=(T // tm,),
        in_specs=[pl.BlockSpec((tm, QKV_DIM), lambda i: (i, 0)), full((1, ATTN_DIM)), full((1, KV_DIM)),
                  full((ATTN_DIM, HEAD_COLS)), full((HEAD_COLS, ATTN_DIM)), full((KV_DIM, HEAD_COLS)), full((HEAD_COLS, KV_DIM))],
        out_specs=[pl.BlockSpec((tm, ATTN_DIM), lambda i: (i, 0)), pl.BlockSpec((tm, KV_DIM), lambda i: (i, 0)),
                   pl.BlockSpec((tm, KV_DIM), lambda i: (i, 0))],
        out_shape=[jax.ShapeDtypeStruct((T, ATTN_DIM), BF), jax.ShapeDtypeStruct((T, KV_DIM), BF),
                   jax.ShapeDtypeStruct((T, KV_DIM), BF)],
        compiler_params=_params(1),
    )(qkv, qg_t, kg_t, *_head_select(ATTN_DIM), *_head_select(KV_DIM))


def _qk_norm_bwd(qkv, dqn, dkn, dv, qg_t, kg_t, tm=256):
    T = qkv.shape[0]

    def body(x_ref, dq_ref, dk_ref, dv_ref, qg_ref, kg_ref, sq_ref, sqt_ref, sk_ref, skt_ref,
             o_ref, db_ref, dqg_ref, dkg_ref):
        i = pl.program_id(0)

        def one(x, dy, g, select):
            r = lax.rsqrt(_head_sum(x * x, select) * (1.0 / HEAD_DIM) + NORM_EPS)
            xh = x * r
            dxh = dy * g
            dx = r * (dxh - xh * (_head_sum(dxh * xh, select) * (1.0 / HEAD_DIM)))
            return dx, jnp.sum(dy * xh, axis=0, keepdims=True)

        dq, dqg = one(x_ref[:, pl.ds(0, ATTN_DIM)], dq_ref[...], qg_ref[...], (sq_ref[...], sqt_ref[...]))
        dk, dkg = one(x_ref[:, pl.ds(ATTN_DIM, KV_DIM)], dk_ref[...], kg_ref[...], (sk_ref[...], skt_ref[...]))
        dvv = dv_ref[...]
        o_ref[:, pl.ds(0, ATTN_DIM)] = dq.astype(BF)
        o_ref[:, pl.ds(ATTN_DIM, KV_DIM)] = dk.astype(BF)
        o_ref[:, pl.ds(ATTN_DIM + KV_DIM, KV_DIM)] = dvv.astype(BF)
        sq, sk, sv = (jnp.sum(t, axis=0, keepdims=True) for t in (dq, dk, dvv))

        @pl.when(i == 0)
        def _():
            db_ref[:, pl.ds(0, ATTN_DIM)] = sq
            db_ref[:, pl.ds(ATTN_DIM, KV_DIM)] = sk
            db_ref[:, pl.ds(ATTN_DIM + KV_DIM, KV_DIM)] = sv
            dqg_ref[...] = dqg
            dkg_ref[...] = dkg

        @pl.when(i > 0)
        def _():
            db_ref[:, pl.ds(0, ATTN_DIM)] += sq
            db_ref[:, pl.ds(ATTN_DIM, KV_DIM)] += sk
            db_ref[:, pl.ds(ATTN_DIM + KV_DIM, KV_DIM)] += sv
            dqg_ref[...] += dqg
            dkg_ref[...] += dkg

    full = lambda shape: pl.BlockSpec(shape, lambda i: (0, 0))
    row = lambda n: pl.BlockSpec((tm, n), lambda i: (i, 0))
    return pl.pallas_call(
        body, name="qk_norm_bwd", grid=(T // tm,),
        in_specs=[row(QKV_DIM), row(ATTN_DIM), row(KV_DIM), row(KV_DIM), full((1, ATTN_DIM)), full((1, KV_DIM)),
                  full((ATTN_DIM, HEAD_COLS)), full((HEAD_COLS, ATTN_DIM)), full((KV_DIM, HEAD_COLS)), full((HEAD_COLS, KV_DIM))],
        out_specs=[row(QKV_DIM), full((1, QKV_DIM)), full((1, ATTN_DIM)), full((1, KV_DIM))],
        out_shape=[jax.ShapeDtypeStruct((T, QKV_DIM), BF), jax.ShapeDtypeStruct((1, QKV_DIM), F32),
                   jax.ShapeDtypeStruct((1, ATTN_DIM), F32), jax.ShapeDtypeStruct((1, KV_DIM), F32)],
        compiler_params=_params(1),
    )(qkv, dqn, dkn, dv, qg_t, kg_t, *_head_select(ATTN_DIM), *_head_select(KV_DIM))


ROWS = 64
COLS = 128


SUBLANES = 8
FIRST_TAP = HALO - (CONV_W - 1)


def _glu(a, g):
    return a.astype(F32) * jax.nn.sigmoid(g.astype(F32))


def _shifted(xe, s):
    return xe if s == 0 else pltpu.roll(xe, ROWS + HALO - s, axis=0)


def _conv_fwd(u, dw_pad, dw_b, ln_g, ln_b, tm=256):
    T = u.shape[0]
    Dm = D_MODEL
    hpt = tm // HALO

    def body(ac_ref, gc_ref, ap_ref, gp_ref, w_ref, wb_ref, lg_ref, lb_ref, cv_ref, s_ref, ext):
        i = pl.program_id(0)
        ext[pl.ds(0, HALO), :] = jnp.where(i > 0, _glu(ap_ref[...], gp_ref[...]), 0.0)
        ext[pl.ds(HALO, tm), :] = _glu(ac_ref[...], gc_ref[...])

        def rows(r, carry):
            r0 = pl.multiple_of(r * ROWS, ROWS)
            for c in range(Dm // COLS):
                cs = pl.ds(c * COLS, COLS)
                xe = ext[pl.ds(r0, ROWS + HALO), cs]
                acc = jnp.zeros((ROWS, COLS), F32)
                for s in range(SUBLANES):
                    xs = _shifted(xe, s)
                    for j in range(CONV_W):
                        off = FIRST_TAP + j
                        if off % SUBLANES == s:
                            acc = acc + xs[off - s:off - s + ROWS, :] * w_ref[pl.ds(j, 1), cs]
                cv_ref[pl.ds(r0, ROWS), cs] = acc + wb_ref[:, cs]
            return carry

        lax.fori_loop(0, tm // ROWS, rows, 0)
        cv = cv_ref[...]
        xc = cv - jnp.mean(cv, axis=-1, keepdims=True)
        y = xc * lax.rsqrt(jnp.mean(xc * xc, axis=-1, keepdims=True) + NORM_EPS) * lg_ref[...] + lb_ref[...]
        s_ref[...] = (y * jax.nn.sigmoid(y)).astype(BF)

    full = lambda shape: pl.BlockSpec(shape, lambda i: (0, 0))
    return pl.pallas_call(
        body, name="conv_fwd", grid=(T // tm,),
        in_specs=[pl.BlockSpec((tm, Dm), lambda i: (i, 0)), pl.BlockSpec((tm, Dm), lambda i: (i, 1)),
                  pl.BlockSpec((HALO, Dm), lambda i: (jnp.maximum(i * hpt - 1, 0), 0)),
                  pl.BlockSpec((HALO, Dm), lambda i: (jnp.maximum(i * hpt - 1, 0), 1)),
                  full((HALO, Dm)), full((1, Dm)), full((1, Dm)), full((1, Dm))],
        out_specs=[pl.BlockSpec((tm, Dm), lambda i: (i, 0)), pl.BlockSpec((tm, Dm), lambda i: (i, 0))],
        out_shape=[jax.ShapeDtypeStruct((T, Dm), F32), jax.ShapeDtypeStruct((T, Dm), BF)],
        scratch_shapes=[pltpu.VMEM((tm + HALO, Dm), F32)],
        compiler_params=_params(1),
    )(u, u, u, u, dw_pad, dw_b, ln_g, ln_b)


def _ln_silu_bwd_ep(ds, cv, lg, lb):
    xc = cv - jnp.mean(cv, axis=-1, keepdims=True)
    rstd = lax.rsqrt(jnp.mean(xc * xc, axis=-1, keepdims=True) + NORM_EPS)
    xh = xc * rstd
    y = xh * lg + lb
    sg = jax.nn.sigmoid(y)
    dy = ds * (sg * (1.0 + y * (1.0 - sg)))
    dxh = dy * lg
    dcv = rstd * (dxh - jnp.mean(dxh, axis=-1, keepdims=True) - xh * jnp.mean(dxh * xh, axis=-1, keepdims=True))
    return (dcv, jnp.sum(dy * xh, axis=0, keepdims=True), jnp.sum(dy, axis=0, keepdims=True),
            jnp.sum(dcv, axis=0, keepdims=True))


def _conv_bwd(u, dcv, dw_pad, tm=256):
    T = u.shape[0]
    Dm = D_MODEL
    hpt = tm // HALO
    last = T // HALO - 1
    nt = T // tm

    def body(ac_ref, gc_ref, ap_ref, gp_ref, dc_ref, dn_ref, w_ref, du_ref, db_ref, dw_ref, ext_g, ext_d):
        i = pl.program_id(0)
        ext_g[pl.ds(0, HALO), :] = jnp.where(i > 0, _glu(ap_ref[...], gp_ref[...]), 0.0)
        ext_g[pl.ds(HALO, tm), :] = _glu(ac_ref[...], gc_ref[...])
        ext_d[pl.ds(0, tm), :] = dc_ref[...]
        ext_d[pl.ds(tm, HALO), :] = jnp.where(i < nt - 1, dn_ref[...], 0.0)

        @pl.when(i == 0)
        def _():
            db_ref[...] = jnp.zeros_like(db_ref)
            dw_ref[...] = jnp.zeros_like(dw_ref)

        def rows(r, carry):
            r0 = pl.multiple_of(r * ROWS, ROWS)
            rs = pl.ds(r0, ROWS)
            for c in range(Dm // COLS):
                cs = pl.ds(c * COLS, COLS)
                cs2 = pl.ds(Dm + c * COLS, COLS)
                de = ext_d[pl.ds(r0, ROWS + HALO), cs]
                ge = ext_g[pl.ds(r0, ROWS + HALO), cs]
                dcur = de[0:ROWS, :]
                acc = jnp.zeros((ROWS, COLS), F32)
                for s in range(SUBLANES):
                    ds_, gs_ = _shifted(de, s), _shifted(ge, s)
                    for j in range(CONV_W):
                        off = CONV_W - 1 - j
                        if off % SUBLANES == s:
                            acc = acc + ds_[off - s:off - s + ROWS, :] * w_ref[pl.ds(j, 1), cs]
                        goff = FIRST_TAP + j
                        if goff % SUBLANES == s:
                            prod = dcur * gs_[goff - s:goff - s + ROWS, :]
                            dw_ref[j, :, cs] += jnp.sum(prod.reshape(ROWS // SUBLANES, SUBLANES, COLS), axis=0)
                a = ac_ref[rs, cs].astype(F32)
                sg = jax.nn.sigmoid(gc_ref[rs, cs].astype(F32))
                da = acc * sg
                dg = acc * a * sg * (1.0 - sg)
                du_ref[rs, cs] = da.astype(BF)
                du_ref[rs, cs2] = dg.astype(BF)
                db_ref[:, cs] += jnp.sum(da, axis=0, keepdims=True)
                db_ref[:, cs2] += jnp.sum(dg, axis=0, keepdims=True)
            return carry

        lax.fori_loop(0, tm // ROWS, rows, 0)

    return pl.pallas_call(
        body, name="conv_bwd", grid=(nt,),
        in_specs=[pl.BlockSpec((tm, Dm), lambda i: (i, 0)), pl.BlockSpec((tm, Dm), lambda i: (i, 1)),
                  pl.BlockSpec((HALO, Dm), lambda i: (jnp.maximum(i * hpt - 1, 0), 0)),
                  pl.BlockSpec((HALO, Dm), lambda i: (jnp.maximum(i * hpt - 1, 0), 1)),
                  pl.BlockSpec((tm, Dm), lambda i: (i, 0)),
                  pl.BlockSpec((HALO, Dm), lambda i: (jnp.minimum((i + 1) * hpt, last), 0)),
                  pl.BlockSpec((HALO, Dm), lambda i: (0, 0))],
        out_specs=[pl.BlockSpec((tm, 2 * Dm), lambda i: (i, 0)), pl.BlockSpec((1, 2 * Dm), lambda i: (0, 0)),
                   pl.BlockSpec((HALO, 8, Dm), lambda i: (0, 0, 0))],
        out_shape=[jax.ShapeDtypeStruct((T, 2 * Dm), BF), jax.ShapeDtypeStruct((1, 2 * Dm), F32),
                   jax.ShapeDtypeStruct((HALO, 8, Dm), F32)],
        scratch_shapes=[pltpu.VMEM((tm + HALO, Dm), F32), pltpu.VMEM((tm + HALO, Dm), F32)],
        compiler_params=_params(1),
    )(u, u, u, u, dcv, dcv, dw_pad)


def _bucket_table():
    q_loc = np.arange(BLOCK)[:, None]
    k_loc = np.arange(2 * BLOCK)[None, :]
    dist = q_loc + BLOCK - k_loc
    n = np.maximum(dist, 0)
    max_exact = REL_BUCKETS // 2
    large = max_exact + (np.log(np.maximum(n, 1).astype(np.float32) / max_exact)
                         / math.log(REL_MAX_DIST / max_exact) * (REL_BUCKETS - max_exact)).astype(np.int32)
    large = np.minimum(large, REL_BUCKETS - 1)
    bucket = np.where(n < max_exact, n, large).astype(np.int32)
    return jnp.asarray(np.where((dist >= 0) & (dist < BLOCK), bucket, -1).astype(np.int32))


def _bias_table(rel_bias, bucket):
    def body(rb_ref, bk_ref, o_ref):
        bk = bk_ref[...]
        for h in range(N_HEADS):
            acc = jnp.full((BLOCK, 2 * BLOCK), NEG_INF, F32)
            for b in range(REL_BUCKETS):
                acc = jnp.where(bk == b, rb_ref[b, h], acc)
            o_ref[h] = acc

    return pl.pallas_call(
        body, name="bias_table", out_shape=jax.ShapeDtypeStruct((N_HEADS, BLOCK, 2 * BLOCK), F32),
        in_specs=[pl.BlockSpec(memory_space=pltpu.SMEM), pl.BlockSpec(memory_space=pltpu.VMEM)],
        out_specs=pl.BlockSpec(memory_space=pltpu.VMEM),
    )(rel_bias, bucket)


def _bias_grad(dbias, bucket):
    def body(db_ref, bk_ref, o_ref):
        bk = bk_ref[...]
        for b in range(REL_BUCKETS):
            sel = bk == b
            for h in range(N_HEADS):
                o_ref[b, h] = jnp.sum(jnp.where(sel, db_ref[h], 0.0))

    return pl.pallas_call(
        body, name="bias_grad", out_shape=jax.ShapeDtypeStruct((REL_BUCKETS, N_HEADS), F32),
        in_specs=[pl.BlockSpec(memory_space=pltpu.VMEM), pl.BlockSpec(memory_space=pltpu.VMEM)],
        out_specs=pl.BlockSpec(memory_space=pltpu.SMEM),
    )(dbias, bucket)


GROUP_ROWS = GROUP * BLOCK


def _head_probs(qk, bias_h, sink, first):
    s = jnp.where(first, NEG_INF, qk + bias_h)
    m = jnp.maximum(jnp.max(s, axis=-1, keepdims=True), sink)
    p = jnp.exp(s - m)
    ps = jnp.exp(sink - m)
    inv = 1.0 / (jnp.sum(p, axis=-1, keepdims=True) + ps)
    return p * inv, ps * inv


def _band(prev_ref, cur_ref, g):
    hs = pl.ds(g * HEAD_DIM, HEAD_DIM)
    return jnp.concatenate([prev_ref[:, hs], cur_ref[:, hs]], axis=0)


def _stack_heads(ref, g):
    return jnp.concatenate([ref[:, pl.ds((g * GROUP + hh) * HEAD_DIM, HEAD_DIM)] for hh in range(GROUP)], axis=0)


def _unstack_heads(ref, g, stacked, dtype):
    for hh in range(GROUP):
        ref[:, pl.ds((g * GROUP + hh) * HEAD_DIM, HEAD_DIM)] = stacked[hh * BLOCK:(hh + 1) * BLOCK, :].astype(dtype)


def _first_mask(n):
    col = lax.broadcasted_iota(jnp.int32, (1, 2 * BLOCK), 1)
    return jnp.logical_and(n == 0, col < BLOCK)


def _head_rows(hh):
    return pl.ds(hh * BLOCK, BLOCK)


def _attn_fwd(qn, kn, vv, bias, sinks):
    T = qn.shape[0]
    nb = T // BLOCK

    def body(sk_ref, q_ref, kc_ref, kp_ref, vc_ref, vp_ref, b_ref, o_ref, qk_buf, p_buf):
        first = _first_mask(pl.program_id(0))
        for g in range(N_KV):
            k = _band(kp_ref, kc_ref, g)
            v = _band(vp_ref, vc_ref, g)
            qk_buf[...] = _dot(_stack_heads(q_ref, g), k, 1, 1)
            for hh in range(GROUP):
                h = g * GROUP + hh
                pn, _ = _head_probs(qk_buf[_head_rows(hh), :], b_ref[h], sk_ref[h], first)
                p_buf[_head_rows(hh), :] = pn.astype(BF)
            _unstack_heads(o_ref, g, _dot(p_buf[...], v, 1, 0), BF)

    cur = lambda n: (n, 0)
    prev = lambda n: (jnp.maximum(n - 1, 0), 0)
    return pl.pallas_call(
        body, name="attn_fwd", grid=(nb,),
        in_specs=[pl.BlockSpec(memory_space=pltpu.SMEM), pl.BlockSpec((BLOCK, ATTN_DIM), cur),
                  pl.BlockSpec((BLOCK, KV_DIM), cur), pl.BlockSpec((BLOCK, KV_DIM), prev),
                  pl.BlockSpec((BLOCK, KV_DIM), cur), pl.BlockSpec((BLOCK, KV_DIM), prev),
                  pl.BlockSpec((N_HEADS, BLOCK, 2 * BLOCK), lambda n: (0, 0, 0))],
        out_specs=pl.BlockSpec((BLOCK, ATTN_DIM), cur), out_shape=jax.ShapeDtypeStruct((T, ATTN_DIM), BF),
        scratch_shapes=[pltpu.VMEM((GROUP_ROWS, 2 * BLOCK), F32), pltpu.VMEM((GROUP_ROWS, 2 * BLOCK), BF)],
        compiler_params=_params(1),
    )(sinks, qn, kn, kn, vv, vv, bias)


def _attn_bwd(qn, kn, vv, bias, sinks, do):
    T = qn.shape[0]
    nb = T // BLOCK
    scale = 1.0 / math.sqrt(HEAD_DIM)

    def body(sk_ref, q_ref, kc_ref, kp_ref, vc_ref, vp_ref, b_ref, do_ref,
             dq_ref, dk_ref, dv_ref, db_ref, dsk_ref, dk_full, dv_full, dk_carry, dv_carry, qk_buf, dp_buf, p_buf, ds_buf):
        n = pl.program_id(0)

        @pl.when(n == 0)
        def _():
            db_ref[...] = jnp.zeros_like(db_ref)
            dk_carry[...] = jnp.zeros_like(dk_carry)
            dv_carry[...] = jnp.zeros_like(dv_carry)
            for h in range(N_HEADS):
                dsk_ref[h] = 0.0

        @pl.when(n < nb)
        def _():
            first = _first_mask(n)
            for g in range(N_KV):
                k = _band(kp_ref, kc_ref, g)
                v = _band(vp_ref, vc_ref, g)
                q = _stack_heads(q_ref, g)
                dout = _stack_heads(do_ref, g)
                qk_buf[...] = _dot(q, k, 1, 1)
                dp_buf[...] = _dot(dout, v, 1, 1)
                for hh in range(GROUP):
                    h = g * GROUP + hh
                    rows = _head_rows(hh)
                    pn, psink = _head_probs(qk_buf[rows, :], b_ref[h], sk_ref[h], first)
                    dp = dp_buf[rows, :]
                    delta = jnp.sum(pn * dp, axis=-1, keepdims=True)
                    ds = pn * (dp - delta)
                    dsk_ref[h] += -jnp.sum(psink * delta)
                    db_ref[h] += ds
                    ds_buf[rows, :] = ds.astype(BF)
                    p_buf[rows, :] = pn.astype(BF)
                dsb = ds_buf[...]
                _unstack_heads(dq_ref, g, _dot(dsb, k, 1, 0) * scale, F32)
                gs = pl.ds(g * HEAD_DIM, HEAD_DIM)
                dk_full[:, gs] = _dot(dsb, q, 0, 0)
                dv_full[:, gs] = _dot(p_buf[...], dout, 0, 0)

        @pl.when(n == nb)
        def _():
            dk_full[...] = jnp.zeros_like(dk_full)
            dv_full[...] = jnp.zeros_like(dv_full)

        dk_ref[...] = dk_carry[...] + dk_full[pl.ds(0, BLOCK), :]
        dv_ref[...] = dv_carry[...] + dv_full[pl.ds(0, BLOCK), :]
        dk_carry[...] = dk_full[pl.ds(BLOCK, BLOCK), :]
        dv_carry[...] = dv_full[pl.ds(BLOCK, BLOCK), :]

    cur = lambda n: (jnp.minimum(n, nb - 1), 0)
    prev = lambda n: (jnp.maximum(jnp.minimum(n, nb - 1) - 1, 0), 0)
    out_kv = lambda n: (jnp.maximum(n - 1, 0), 0)
    return pl.pallas_call(
        body, name="attn_bwd", grid=(nb + 1,),
        in_specs=[pl.BlockSpec(memory_space=pltpu.SMEM), pl.BlockSpec((BLOCK, ATTN_DIM), cur),
                  pl.BlockSpec((BLOCK, KV_DIM), cur), pl.BlockSpec((BLOCK, KV_DIM), prev),
                  pl.BlockSpec((BLOCK, KV_DIM), cur), pl.BlockSpec((BLOCK, KV_DIM), prev),
                  pl.BlockSpec((N_HEADS, BLOCK, 2 * BLOCK), lambda n: (0, 0, 0)),
                  pl.BlockSpec((BLOCK, ATTN_DIM), cur)],
        out_specs=[pl.BlockSpec((BLOCK, ATTN_DIM), cur), pl.BlockSpec((BLOCK, KV_DIM), out_kv),
                   pl.BlockSpec((BLOCK, KV_DIM), out_kv),
                   pl.BlockSpec((N_HEADS, BLOCK, 2 * BLOCK), lambda n: (0, 0, 0)),
                   pl.BlockSpec(memory_space=pltpu.SMEM)],
        out_shape=[jax.ShapeDtypeStruct((T, ATTN_DIM), F32), jax.ShapeDtypeStruct((T, KV_DIM), F32),
                   jax.ShapeDtypeStruct((T, KV_DIM), F32),
                   jax.ShapeDtypeStruct((N_HEADS, BLOCK, 2 * BLOCK), F32), jax.ShapeDtypeStruct((N_HEADS,), F32)],
        scratch_shapes=[pltpu.VMEM((2 * BLOCK, KV_DIM), F32), pltpu.VMEM((2 * BLOCK, KV_DIM), F32),
                        pltpu.VMEM((BLOCK, KV_DIM), F32), pltpu.VMEM((BLOCK, KV_DIM), F32),
                        pltpu.VMEM((GROUP_ROWS, 2 * BLOCK), F32), pltpu.VMEM((GROUP_ROWS, 2 * BLOCK), F32),
                        pltpu.VMEM((GROUP_ROWS, 2 * BLOCK), BF), pltpu.VMEM((GROUP_ROWS, 2 * BLOCK), BF)],
        compiler_params=_params(1),
    )(sinks, qn, kn, kn, vv, vv, bias, do)


def _coords():
    return lax.axis_index("x"), lax.axis_index("y"), lax.axis_index("c")


def _gather8(name, v, with_sum, after=()):
    R = v.shape[0]
    after = tuple(after)

    def body(v_ref, *rest):
        all_ref = rest[len(after)]
        sum_ref = rest[len(after) + 1] if with_sum else None
        send_sems, recv_sems, local_sem = rest[-3:]
        x, y, c = _coords()
        me = 4 * x + 2 * y + c
        local = pltpu.make_async_copy(v_ref, all_ref.at[me], local_sem)
        local.start()
        sends = []
        for k in range(1, 8):
            peer = (x ^ (k >> 2), y ^ ((k >> 1) & 1), c ^ (k & 1))
            cp = pltpu.make_async_remote_copy(src_ref=v_ref, dst_ref=all_ref.at[me], send_sem=send_sems.at[k - 1],
                                              recv_sem=recv_sems.at[k - 1], device_id=peer, device_id_type=MESH)
            cp.start()
            sends.append(cp)
        for k in range(1, 8):
            peer = (x ^ (k >> 2), y ^ ((k >> 1) & 1), c ^ (k & 1))
            pltpu.make_async_remote_copy(src_ref=v_ref, dst_ref=all_ref.at[me ^ k], send_sem=send_sems.at[k - 1],
                                         recv_sem=recv_sems.at[k - 1], device_id=peer, device_id_type=MESH).wait_recv()
        for cp in sends:
            cp.wait_send()
        local.wait()
        if with_sum:
            tot = all_ref[0]
            for d in range(1, 8):
                tot = tot + all_ref[d]
            sum_ref[...] = tot

    out_shape = [jax.ShapeDtypeStruct((8, R, LANES), F32)]
    if with_sum:
        out_shape.append(jax.ShapeDtypeStruct((R, LANES), F32))
    vm = pl.BlockSpec(memory_space=pltpu.VMEM)
    return pl.pallas_call(
        body, name=name, out_shape=out_shape, in_specs=[vm] + [ANY_SPEC] * len(after), out_specs=[vm] * len(out_shape),
        scratch_shapes=[pltpu.SemaphoreType.DMA((7,)), pltpu.SemaphoreType.DMA((7,)), pltpu.SemaphoreType.DMA],
    )(v, *after)


CHIP_FLIPS = ((1, 0), (0, 1), (1, 1))


HBM_SPEC = pl.BlockSpec(memory_space=pltpu.HBM)
SEM_SPEC = pl.BlockSpec(memory_space=pltpu.SEMAPHORE)
ANY_SPEC = pl.BlockSpec(memory_space=pl.ANY)
DATAFLOW = pltpu.SideEffectType.DATAFLOW_SIDE_EFFECTING


def _chip_copy(land, sems, idx, slot_src, slot_dst, peer):
    send_sems, recv_sems = sems
    return pltpu.make_async_remote_copy(src_ref=land.at[slot_src], dst_ref=land.at[slot_dst], send_sem=send_sems.at[idx],
                                        recv_sem=recv_sems.at[idx], device_id=peer, device_id_type=MESH)


def _gather_start(stacks, groups, after):
    n = len(stacks)
    ng = len(groups)
    after = tuple(after)

    def body(*refs):
        lands = refs[:n]
        first = n + len(after)
        sems = [(refs[first + 2 * g], refs[first + 2 * g + 1]) for g in range(ng)]
        token = refs[-1]
        x, y, c = _coords()
        s = 2 * x + y
        for g, members in enumerate(groups):
            for i, t in enumerate(members):
                for j, (fx, fy) in enumerate(CHIP_FLIPS):
                    _chip_copy(lands[t], sems[g], 3 * i + j, s, s, (x ^ fx, y ^ fy, c)).start()
        token[...] = jnp.zeros_like(token)

    out_shape = []
    for members in groups:
        out_shape += [pltpu.SemaphoreType.DMA((3 * len(members),))] * 2
    out_shape += [pltpu.HBM(w.shape, w.dtype) for w in stacks]
    out_shape.append(jax.ShapeDtypeStruct((8, 128), F32))
    res = pl.pallas_call(
        body, name="gather_start", out_shape=out_shape, in_specs=[HBM_SPEC] * n + [ANY_SPEC] * len(after),
        out_specs=[SEM_SPEC] * (2 * ng) + [HBM_SPEC] * n + [pl.BlockSpec(memory_space=pltpu.VMEM)],
        input_output_aliases={t: 2 * ng + t for t in range(n)},
        compiler_params=pltpu.CompilerParams(has_side_effects=DATAFLOW),
    )(*[pltpu.with_memory_space_constraint(w, pltpu.HBM) for w in stacks], *after)
    sems = [(res[2 * g], res[2 * g + 1]) for g in range(ng)]
    return sems, list(res[2 * ng:2 * ng + n]), res[-1]


def _gather_wait(name, stacks, sems, after):
    n = len(stacks)
    after = tuple(after)

    def body(*refs):
        lands = refs[:n]
        group_sems = (refs[n], refs[n + 1])
        x, y, c = _coords()
        s = 2 * x + y
        for i in range(n):
            for j, (fx, fy) in enumerate(CHIP_FLIPS):
                cp = _chip_copy(lands[i], group_sems, 3 * i + j, s, 2 * (x ^ fx) + (y ^ fy), (x ^ fx, y ^ fy, c))
                cp.wait_send()
                cp.wait_recv()

    return pl.pallas_call(
        body, name=name, out_shape=[pltpu.HBM(w.shape, w.dtype) for w in stacks],
        in_specs=[HBM_SPEC] * n + [SEM_SPEC, SEM_SPEC] + [ANY_SPEC] * len(after), out_specs=[HBM_SPEC] * n,
        input_output_aliases={t: t for t in range(n)},
        compiler_params=pltpu.CompilerParams(has_side_effects=DATAFLOW),
    )(*stacks, sems[0], sems[1], *after)


N_PEERS = 7


def _peer(x, y, c, k):
    return x ^ (k >> 2), y ^ ((k >> 1) & 1), c ^ (k & 1)


def _reduce_copy(grad, land, sems, idx, x, y, c, k):
    px, py, pc = _peer(x, y, c, k)
    rh = grad.shape[1] // 2
    return pltpu.make_async_remote_copy(src_ref=grad.at[2 * px + py, pl.ds(pc * rh, rh), :], dst_ref=land.at[k - 1],
                                        send_sem=sems[0].at[idx], recv_sem=sems[1].at[idx], device_id=(px, py, pc),
                                        device_id_type=MESH)


def _reduce_start(name, grads):
    n = len(grads)

    def body(*refs):
        src, lands, sems, token = refs[:n], refs[n:2 * n], (refs[2 * n], refs[2 * n + 1]), refs[-1]
        x, y, c = _coords()
        for t in range(n):
            for k in range(1, N_PEERS + 1):
                _reduce_copy(src[t], lands[t], sems, N_PEERS * t + k - 1, x, y, c, k).start()
        token[...] = jnp.zeros_like(token)

    lands = [lax.empty((N_PEERS, g.shape[1] // 2, g.shape[2]), g.dtype) for g in grads]
    out_shape = [pltpu.SemaphoreType.DMA((N_PEERS * n,))] * 2
    out_shape += [pltpu.HBM(a.shape, a.dtype) for a in list(grads) + lands]
    out_shape.append(jax.ShapeDtypeStruct((8, 128), F32))
    res = pl.pallas_call(
        body, name=name, out_shape=out_shape, in_specs=[HBM_SPEC] * (2 * n),
        out_specs=[SEM_SPEC] * 2 + [HBM_SPEC] * (2 * n) + [pl.BlockSpec(memory_space=pltpu.VMEM)],
        input_output_aliases={t: 2 + t for t in range(2 * n)},
        compiler_params=pltpu.CompilerParams(has_side_effects=DATAFLOW),
    )(*[pltpu.with_memory_space_constraint(a, pltpu.HBM) for a in list(grads) + lands])
    return (res[0], res[1]), list(res[2:2 + n]), list(res[2 + n:2 + 2 * n]), res[-1]


def _reduce_wait(name, grads, lands, sems, after):
    n = len(grads)
    after = tuple(after)

    def body(*refs):
        src, dst, group_sems = refs[:n], refs[n:2 * n], (refs[2 * n], refs[2 * n + 1])
        x, y, c = _coords()
        for t in range(n):
            for k in range(1, N_PEERS + 1):
                cp = _reduce_copy(src[t], dst[t], group_sems, N_PEERS * t + k - 1, x, y, c, k)
                cp.wait_send()
                cp.wait_recv()

    res = pl.pallas_call(
        body, name=name, out_shape=[pltpu.HBM(a.shape, a.dtype) for a in list(grads) + list(lands)],
        in_specs=[HBM_SPEC] * (2 * n) + [SEM_SPEC, SEM_SPEC] + [ANY_SPEC] * len(after), out_specs=[HBM_SPEC] * (2 * n),
        input_output_aliases={t: t for t in range(2 * n)},
        compiler_params=pltpu.CompilerParams(has_side_effects=DATAFLOW),
    )(*grads, *lands, sems[0], sems[1], *after)
    return list(res[:n]), list(res[n:])


def _join_halves(name, halves):
    n = len(halves)

    def body(*refs):
        src, dst = refs[:n], refs[n:2 * n]
        send_sems, recv_sems = refs[2 * n:]
        x, y, c = _coords()
        cps = []
        for t in range(n):
            cp = pltpu.make_async_remote_copy(src_ref=src[t], dst_ref=dst[t], send_sem=send_sems.at[t],
                                              recv_sem=recv_sems.at[t], device_id=(x, y, 1 - c), device_id_type=MESH)
            cp.start()
            cps.append(cp)
        for cp in cps:
            cp.wait()

    anyspec = pl.BlockSpec(memory_space=pl.ANY)
    return pl.pallas_call(
        body, name=name, out_shape=[jax.ShapeDtypeStruct(h.shape, h.dtype) for h in halves],
        in_specs=[anyspec] * n, out_specs=[anyspec] * n,
        scratch_shapes=[pltpu.SemaphoreType.DMA((n,)), pltpu.SemaphoreType.DMA((n,))],
    )(*halves)


def _row_block(rows):
    for rb in (512, 256, 128, 64, 32, 16):
        if rows % rb == 0:
            return rb
    raise ValueError(rows)


def _sum_devices(name, grad, land, place):
    S, R, C = grad.shape
    rh = R // 2
    rb = _row_block(rh)
    nbh = rh // rb

    def body(place_ref, g_ref, l_ref, o_ref):
        tot = g_ref[...].astype(F32)
        for k in range(N_PEERS):
            tot = tot + l_ref[k].astype(F32)
        o_ref[...] = tot

    return pl.pallas_call(
        body, name=name,
        grid_spec=pltpu.PrefetchScalarGridSpec(
            num_scalar_prefetch=1, grid=(nbh,),
            in_specs=[pl.BlockSpec((None, rb, C), lambda r, place: (place[0], place[1] * nbh + r, 0)),
                      pl.BlockSpec((N_PEERS, rb, C), lambda r, place: (0, r, 0))],
            out_specs=pl.BlockSpec((rb, C), lambda r, place: (r, 0))),
        out_shape=jax.ShapeDtypeStruct((rh, C), F32), compiler_params=_params(1),
    )(place, grad, land)


def _adamw_math(w, g, m, v):
    m2 = ADAM_B1 * m + (1.0 - ADAM_B1) * g
    v2 = ADAM_B2 * v + (1.0 - ADAM_B2) * (g * g)
    m_hat = m2 / (1.0 - ADAM_B1 ** ADAM_STEP)
    v_hat = v2 / (1.0 - ADAM_B2 ** ADAM_STEP)
    delta = -ADAM_LR * (m_hat / (jnp.sqrt(v_hat) + ADAM_EPS) + ADAM_WD * w)
    return delta, m2, v2


def _adamw(name, w, m, v, gs):
    L, R, C = w.shape
    Rh = R // 2
    rb = _row_block(Rh)
    nbh = Rh // rb
    assert len(gs) == L

    def body(core_ref, w_ref, m_ref, v_ref, *rest):
        g_refs, (go_ref, d_ref, m2_ref, v2_ref) = rest[:2 * L], rest[2 * L:]
        layer, half = pl.program_id(0), pl.program_id(1)
        mine = half == core_ref[0]
        g = jnp.where(mine, g_refs[0][...], g_refs[1][...])
        for t in range(1, L):
            g = jnp.where(layer == t, jnp.where(mine, g_refs[2 * t][...], g_refs[2 * t + 1][...]), g)
        delta, m2, v2 = _adamw_math(w_ref[...], g, m_ref[...], v_ref[...])
        go_ref[...] = g
        d_ref[...] = delta
        m2_ref[...] = m2
        v2_ref[...] = v2

    wspec = pl.BlockSpec((None, rb, C), lambda l, h, r, core: (l, h * nbh + r, 0))
    gspec = pl.BlockSpec((rb, C), lambda l, h, r, core: (r, 0))
    return pl.pallas_call(
        body, name=name,
        grid_spec=pltpu.PrefetchScalarGridSpec(num_scalar_prefetch=1, grid=(L, 2, nbh),
                                               in_specs=[wspec] * 3 + [gspec] * (2 * L), out_specs=[wspec] * 4),
        out_shape=[jax.ShapeDtypeStruct((L, R, C), F32)] * 4, compiler_params=_params(3),
    )(lax.axis_index("c").astype(jnp.int32).reshape(1), w, m, v, *[g for pair in gs for g in pair])


def _adamw_small(w, g, m, v):
    def body(w_ref, g_ref, m_ref, v_ref, d_ref, m2_ref, v2_ref):
        delta, m2, v2 = _adamw_math(w_ref[...], g_ref[...], m_ref[...], v_ref[...])
        d_ref[...] = delta
        m2_ref[...] = m2
        v2_ref[...] = v2

    return pl.pallas_call(body, name="adamw_small", out_shape=[jax.ShapeDtypeStruct(w.shape, F32)] * 3)(w, g, m, v)


def _packed_rows(shape):
    c = shape[-1]
    return (int(np.prod(shape)) // c) * -(-c // LANES)


def _pack(arrays):
    total = sum(_packed_rows(a.shape) for a in arrays)
    total += -total % 8
    buf, r0 = None, 0
    for a in arrays:
        a = a.astype(F32).reshape(-1, a.shape[-1])
        r, c = a.shape
        k = -(-c // LANES)
        a = jnp.pad(a, ((0, 0), (0, k * LANES - c))).reshape(r * k, LANES)
        a = jnp.pad(a, ((r0, total - r0 - r * k), (0, 0)))
        buf = a if buf is None else buf + a
        r0 += r * k
    return buf


def _unpack(buf, shapes):
    out, r0 = [], 0
    for shp in shapes:
        c = shp[-1]
        rows = _packed_rows(shp)
        out.append(buf[r0:r0 + rows].reshape(-1, -(-c // LANES) * LANES)[:, :c].reshape(shp))
        r0 += rows
    return out


def _rms(x, g):
    return x * lax.rsqrt(jnp.mean(x * x, axis=-1, keepdims=True) + NORM_EPS) * g


def _residual_norm_ep(acc, *rest):
    *bias, res, gain = rest
    x = acc + res + (bias[0] if bias else 0.0)
    return x, _rms(x, gain)


RESIDUAL_NORM_OUTS = (("tile", F32), ("tile", BF))


def _mlp_up(tag, h, w_up_sm):
    return _mm(f"mlp{tag}_up", h, w_up_sm, nt=False, b_sm=True, tm=1024, tn=1024, rows=256,
               ep_fn=lambda acc: (acc, _relu2(acc.astype(BF))), outs=(("tile", BF), ("tile", BF)))


RMS_BWD_OUTS = (("tile", F32), ("tile", BF), ("colsum", F32), ("colsum", F32))


def _mlp_bwd(tag, dy, dy_bf, x, g, up, w_up_sm, w_down):
    (dup,) = _mm(f"mlp{tag}_dup", dy_bf, w_down, nt=True, tm=1024, tn=1024, rows=256, ep_in=((up, "tile"),),
                 ep_fn=lambda acc, u: (acc * (2.0 * jnp.maximum(u.astype(F32), 0.0)),), outs=(("tile", BF),))
    dx, dx_bf, dg, dx_sum = _mm(f"mlp{tag}_dx", dup, w_up_sm, nt=True, b_sm=True, tm=512, tn=1024, rows=256,
                                ep_in=((x, "tile"), (g, "row"), (dy, "tile")), ep_fn=_rms_bwd_ep, outs=RMS_BWD_OUTS)
    return dx, dx_bf, dg, dx_sum, dup


class _Reduction:
    def __init__(self, tag, grads, place):
        self.tag, self.place = tag, place
        self.sems, self.grads, self.lands, self.token = _reduce_start(f"reduce_start_{tag}", grads)

    def finish(self, after):
        grads, lands = _reduce_wait(f"reduce_wait_{self.tag}", self.grads, self.lands, self.sems, after)
        return [_sum_devices(f"reduce_sum_{self.tag}{i}", g, l, self.place) for i, (g, l) in enumerate(zip(grads, lands))]


def kernel(x, conv_norm_g, conv_w_in, conv_b_in, conv_dw, conv_dw_b, conv_ln_g, conv_ln_b, conv_w_out, conv_b_out, attn_norm_g, w_qkv, b_qkv, q_norm_g, k_norm_g, sinks, w_o, b_o, rel_bias, mlp_norm_g, w_up, w_down, loss_target, m_conv_norm_g, m_conv_w_in, m_conv_b_in, m_conv_dw, m_conv_dw_b, m_conv_ln_g, m_conv_ln_b, m_conv_w_out, m_conv_b_out, m_attn_norm_g, m_w_qkv, m_b_qkv, m_q_norm_g, m_k_norm_g, m_sinks, m_w_o, m_b_o, m_rel_bias, m_mlp_norm_g, m_w_up, m_w_down, v_conv_norm_g, v_conv_w_in, v_conv_b_in, v_conv_dw, v_conv_dw_b, v_conv_ln_g, v_conv_ln_b, v_conv_w_out, v_conv_b_out, v_attn_norm_g, v_w_qkv, v_b_qkv, v_q_norm_g, v_k_norm_g, v_sinks, v_w_o, v_b_o, v_rel_bias, v_mlp_norm_g, v_w_up, v_w_down):
    Dm = D_MODEL
    x2d = x[0]
    tgt = loss_target[0]
    T = x2d.shape[0]
    shard = 2 * lax.axis_index("x") + lax.axis_index("y")

    sharded_small = [conv_dw[0], attn_norm_g, b_qkv, b_o]
    (gathered,) = _gather8("gather_small_weights", _pack(sharded_small), with_sum=False)
    chips = [_unpack(gathered[2 * s], [a.shape for a in sharded_small]) for s in range(N_SHARD)]
    dw_f, attn_norm_f, b_qkv_f, b_o_f = (jnp.concatenate([chips[s][t] for s in range(N_SHARD)], axis=-1)
                                         for t in range(len(sharded_small)))
    dw_pad = jnp.pad(dw_f, ((0, HALO - CONV_W), (0, 0)))

    big = [conv_w_in[0], conv_w_out[0], w_qkv[0], w_o[0], w_up[0], w_up[1], w_down[0], w_down[1]]
    stacks = [lax.dynamic_update_slice(lax.empty((N_SHARD,) + w.shape, BF), w.astype(BF)[None], (shard, 0, 0))
              for w in big]
    groups = ((0,), (1,), (4, 6), (2, 3), (5, 7))
    gather_sems, stacks, gather_token = _gather_start(stacks, groups, after=(gathered,))

    def gathered_group(g, name, after):
        return _gather_wait(name, [stacks[t] for t in groups[g]], gather_sems[g], after)

    bucket = _bucket_table()
    bias = _bias_table(rel_bias, bucket)

    h0 = _rms_fwd("conv_norm", x2d, conv_norm_g, deps=(gather_token,))
    (w_in_sm,) = gathered_group(0, "gather_wait_conv_in", (h0, dw_pad, bias))
    (u,) = _mm("conv_in", h0, w_in_sm, nt=False, b_sm=True, tm=1024, tn=512, rows=256, ep_in=((conv_b_in, "row"),),
               ep_fn=lambda acc, b: (acc + b,), outs=(("tile", BF),))
    cv, s_act = _conv_fwd(u, dw_pad, conv_dw_b, conv_ln_g, conv_ln_b)
    (g_out,) = gathered_group(1, "gather_wait_conv_out", (s_act,))
    w_out_f = g_out.reshape(Dm, Dm)
    x1, h1 = _mm("conv_out", s_act, w_out_f, nt=False, tm=1024, tn=1024, rows=256,
                 ep_in=((conv_b_out, "row"), (x2d, "tile"), (mlp_norm_g[0:1], "row")), ep_fn=_residual_norm_ep,
                 outs=RESIDUAL_NORM_OUTS)

    g_up0, g_down0 = gathered_group(2, "gather_wait_mlp0", (x1,))
    w_up_sm = [g_up0, None]
    w_down_f = [g_down0.reshape(D_FF, Dm), None]
    up0, act0 = _mlp_up(0, h1, w_up_sm[0])
    x2, h2 = _mm("mlp0_down", act0, w_down_f[0], nt=False, tm=512, tn=1024, rows=256,
                 ep_in=((x1, "tile"), (attn_norm_f, "row")), ep_fn=_residual_norm_ep, outs=RESIDUAL_NORM_OUTS)

    g_qkv, g_o = gathered_group(3, "gather_wait_attn", (x2,))
    w_qkv_f = jnp.transpose(g_qkv, (1, 0, 2)).reshape(Dm, QKV_DIM)
    w_o_f = g_o.reshape(ATTN_DIM, Dm)
    (qkv,) = _mm("attn_qkv", h2, w_qkv_f, nt=False, tm=1024, tn=QKV_DIM, rows=256, ep_in=((b_qkv_f, "row"),),
                 ep_fn=lambda acc, b: (acc + b,), outs=(("tile", F32),))
    qg_t = jnp.tile(q_norm_g, (1, N_HEADS))
    kg_t = jnp.tile(k_norm_g, (1, N_KV))
    qn, kn, vv = _qk_norm_fwd(qkv, qg_t, kg_t)
    sinks1 = sinks[0]
    att = _attn_fwd(qn, kn, vv, bias, sinks1)
    x3, h3 = _mm("attn_out", att, w_o_f, nt=False, tm=1024, tn=1024, rows=256,
                 ep_in=((b_o_f, "row"), (x2, "tile"), (mlp_norm_g[1:2], "row")), ep_fn=_residual_norm_ep,
                 outs=RESIDUAL_NORM_OUTS)

    g_up1, g_down1 = gathered_group(4, "gather_wait_mlp1", (x3,))
    w_up_sm[1] = g_up1
    w_down_f[1] = g_down1.reshape(D_FF, Dm)
    up1, act1 = _mlp_up(1, h3, w_up_sm[1])

    def loss_ep(acc, r, t):
        diff = acc + r - t
        dy = diff * (1.0 / Dm)
        return dy, dy, jnp.sum(diff * diff, axis=0, keepdims=True)

    dy, dy_bf, sq = _mm("mlp1_down_loss", act1, w_down_f[1], nt=False, tm=512, tn=1024, rows=256,
                        ep_in=((x3, "tile"), (tgt, "tile")), ep_fn=loss_ep,
                        outs=(("tile", F32), ("tile", BF), ("colsum", F32)))

    place = jnp.stack([shard, lax.axis_index("c")]).astype(jnp.int32)
    dx3, dx3_bf, dg_mlp1, db_o, dup1 = _mlp_bwd(1, dy, dy_bf, x3, mlp_norm_g[1:2], up1, w_up_sm[1], w_down_f[1])
    dw_down1 = _mm_tn("mlp1_dw_down", act1, dy_bf, tm=1024, tn=1024, tk=2048)
    dw_up1 = _mm_tn("mlp1_dw_up", h3, dup1, tm=1024, tn=1024, tk=2048, out_sm=N_SHARD)
    red_mlp1 = _Reduction("mlp1", [dw_up1, dw_down1.reshape(N_SHARD, D_FF // N_SHARD, Dm)], place)

    ident = lambda acc: (acc,)
    (datt,) = _mm("attn_dout", dx3_bf, w_o_f, nt=True, tm=1024, tn=1024, rows=256, ep_fn=ident, outs=(("tile", BF),),
                  deps=(red_mlp1.token,))
    dw_o = _mm_tn("attn_dw_o", att, dx3_bf, tm=1024, tn=1024, tk=2048)
    dqn, dkn, dvv, dbias, dsinks = _attn_bwd(qn, kn, vv, bias, sinks1, datt)
    drel = _bias_grad(dbias, bucket)
    dqkv, db_qkv, dqg_t, dkg_t = _qk_norm_bwd(qkv, dqn, dkn, dvv, qg_t, kg_t)
    dw_qkv = _mm_tn("attn_dw_qkv", h2, dqkv, tm=1024, tn=QKV_DIM, tk=2048)
    red_attn = _Reduction("attn", [jnp.transpose(dw_qkv.reshape(Dm, N_SHARD, QKV_DIM // N_SHARD), (1, 0, 2)),
                                   dw_o.reshape(N_SHARD, ATTN_DIM // N_SHARD, Dm)], place)
    dx2, dx2_bf, dg_attn, _ = _mm("attn_dx", dqkv, w_qkv_f, nt=True, tm=512, tn=1024, rows=256,
                                  ep_in=((x2, "tile"), (attn_norm_f, "row"), (dx3, "tile")), ep_fn=_rms_bwd_ep,
                                  outs=RMS_BWD_OUTS, deps=(red_attn.token,))

    dx1, dx1_bf, dg_mlp0, db_out, dup0 = _mlp_bwd(0, dx2, dx2_bf, x1, mlp_norm_g[0:1], up0, w_up_sm[0], w_down_f[0])
    dw_down0 = _mm_tn("mlp0_dw_down", act0, dx2_bf, tm=1024, tn=1024, tk=2048)
    dw_up0 = _mm_tn("mlp0_dw_up", h1, dup0, tm=1024, tn=1024, tk=2048, out_sm=N_SHARD)
    red_mlp0 = _Reduction("mlp0", [dw_up0, dw_down0.reshape(N_SHARD, D_FF // N_SHARD, Dm)], place)
    (r_qkv, r_o) = red_attn.finish((dx1,))
    (r_up1, r_down1) = red_mlp1.finish((dx1,))

    dcv, dln_g, dln_b, ddw_b = _mm("conv_ds", dx1_bf, w_out_f, nt=True, tm=512, tn=1024, rows=256,
                                   ep_in=((cv, "tile"), (conv_ln_g, "row"), (conv_ln_b, "row")),
                                   ep_fn=_ln_silu_bwd_ep,
                                   outs=(("tile", F32), ("colsum", F32), ("colsum", F32), ("colsum", F32)),
                                   deps=(red_mlp0.token,))
    dw_out = _mm_tn("conv_dw_out", s_act, dx1_bf, tm=1024, tn=1024, tk=2048)
    du, db_in, ddw8 = _conv_bwd(u, dcv, dw_pad)
    (r_up0, r_down0) = red_mlp0.finish((du,))
    dw_in = _mm_tn("conv_dw_in", h0, du, tm=1024, tn=512, tk=2048, out_sm=N_SHARD)
    red_conv = _Reduction("conv", [dw_in, dw_out.reshape(N_SHARD, Dm // N_SHARD, Dm)], place)
    def first_layer_ep(*args):
        tot, _, dg, _ = _rms_bwd_ep(*args)
        return tot, dg

    gx, dg_conv = _mm("conv_dx", du, w_in_sm, nt=True, b_sm=True, tm=512, tn=1024, rows=256,
                      ep_in=((x2d, "tile"), (conv_norm_g, "row"), (dx1, "tile")), ep_fn=first_layer_ep,
                      outs=(("tile", F32), ("colsum", F32)), deps=(red_conv.token,))
    (r_in, r_out) = red_conv.finish((gx,))
    mine = [r_in, r_out, r_qkv, r_o, r_up0, r_up1, r_down0, r_down1]
    r_in, r_out, r_qkv, r_o, r_up0, r_up1, r_down0, r_down1 = zip(mine, _join_halves("join_halves", mine))

    big_out = {}
    for nm, w, m, v, gs in (("conv_w_in", conv_w_in, m_conv_w_in, v_conv_w_in, (r_in,)),
                            ("conv_w_out", conv_w_out, m_conv_w_out, v_conv_w_out, (r_out,)),
                            ("w_qkv", w_qkv, m_w_qkv, v_w_qkv, (r_qkv,)),
                            ("w_o", w_o, m_w_o, v_w_o, (r_o,)),
                            ("w_up", w_up, m_w_up, v_w_up, (r_up0, r_up1)),
                            ("w_down", w_down, m_w_down, v_w_down, (r_down0, r_down1))):
        big_out[nm] = _adamw(f"adamw_{nm}", w, m, v, gs)

    dqg = dqg_t.reshape(N_HEADS, HEAD_DIM).sum(axis=0, keepdims=True)
    dkg = dkg_t.reshape(N_KV, HEAD_DIM).sum(axis=0, keepdims=True)
    small_full = [dg_conv, db_in, ddw8.sum(axis=1)[:CONV_W], ddw_b, dln_g, dln_b, db_out, dg_attn, db_qkv, dqg, dkg,
                  dsinks[None, :], db_o, drel.reshape(1, REL_BUCKETS * N_HEADS),
                  jnp.pad(dg_mlp0, ((0, 1), (0, 0))) + jnp.pad(dg_mlp1, ((1, 0), (0, 0))), sq]
    _, small_sum = _gather8("reduce_small_grads", _pack(small_full), with_sum=True, after=(big_out["w_down"][0],))
    (r_norm, r_b_in, r_dw, r_dw_b, r_ln_g, r_ln_b, r_b_out, r_attn_norm, r_b_qkv, r_qg, r_kg, r_sinks, r_b_o, r_rel,
     r_mlp_norm, r_sq) = _unpack(small_sum, [a.shape for a in small_full])
    loss = 0.5 * jnp.sum(r_sq) * (1.0 / Dm)

    def cols(a, width):
        return lax.dynamic_slice_in_dim(a, shard * width, width, axis=a.ndim - 1)

    small_names = ["conv_norm_g", "conv_b_in", "conv_dw", "conv_dw_b", "conv_ln_g", "conv_ln_b", "conv_b_out",
                   "attn_norm_g", "b_qkv", "q_norm_g", "k_norm_g", "sinks", "b_o", "rel_bias", "mlp_norm_g"]
    small_g = [r_norm, r_b_in, cols(r_dw, Dm // N_SHARD)[None], r_dw_b, r_ln_g, r_ln_b, r_b_out,
               cols(r_attn_norm, Dm // N_SHARD), cols(r_b_qkv, QKV_DIM // N_SHARD), r_qg, r_kg, r_sinks,
               cols(r_b_o, Dm // N_SHARD), r_rel.reshape(REL_BUCKETS, N_HEADS), r_mlp_norm]
    small_w = [conv_norm_g, conv_b_in, conv_dw, conv_dw_b, conv_ln_g, conv_ln_b, conv_b_out, attn_norm_g, b_qkv,
               q_norm_g, k_norm_g, sinks, b_o, rel_bias, mlp_norm_g]
    small_m = [m_conv_norm_g, m_conv_b_in, m_conv_dw, m_conv_dw_b, m_conv_ln_g, m_conv_ln_b, m_conv_b_out,
               m_attn_norm_g, m_b_qkv, m_q_norm_g, m_k_norm_g, m_sinks, m_b_o, m_rel_bias, m_mlp_norm_g]
    small_v = [v_conv_norm_g, v_conv_b_in, v_conv_dw, v_conv_dw_b, v_conv_ln_g, v_conv_ln_b, v_conv_b_out,
               v_attn_norm_g, v_b_qkv, v_q_norm_g, v_k_norm_g, v_sinks, v_b_o, v_rel_bias, v_mlp_norm_g]
    flat2 = lambda a: a.reshape(-1, a.shape[-1])
    shapes2 = [flat2(w).shape for w in small_w]
    pk = lambda arrs: _pack([flat2(a) for a in arrs])
    packed_g = pk(small_g)
    d_s, m_s, v_s = _adamw_small(pk(small_w), packed_g, pk(small_m), pk(small_v))
    small_out = {}
    for nm, w, g, d, m2, v2 in zip(small_names, small_w, _unpack(packed_g, shapes2), _unpack(d_s, shapes2),
                                   _unpack(m_s, shapes2), _unpack(v_s, shapes2)):
        small_out[nm] = tuple(a.reshape(w.shape) for a in (g, d, m2, v2))

    order = ["conv_norm_g", "conv_w_in", "conv_b_in", "conv_dw", "conv_dw_b", "conv_ln_g", "conv_ln_b", "conv_w_out",
             "conv_b_out", "attn_norm_g", "w_qkv", "b_qkv", "q_norm_g", "k_norm_g", "sinks", "w_o", "b_o", "rel_bias",
             "mlp_norm_g", "w_up", "w_down"]
    res = {**small_out, **big_out}
    outs = [loss, gx[None]]
    for part in range(4):
        outs += [res[nm][part] for nm in order]
    return tuple(outs)
```

```python
import math

import numpy as np
import jax
import jax.numpy as jnp
from jax import lax
from jax.experimental import pallas as pl
from jax.experimental.pallas import tpu as pltpu

F32 = jnp.float32
BF = jnp.bfloat16
MESH = pl.DeviceIdType.MESH

D_MODEL = 1024
D_FF = 4096
N_HEADS = 16
N_KV = 2
GROUP = N_HEADS // N_KV
HEAD_DIM = 64
ATTN_DIM = N_HEADS * HEAD_DIM
KV_DIM = N_KV * HEAD_DIM
QKV_DIM = ATTN_DIM + 2 * KV_DIM
BLOCK = 128
CONV_W = 31
HALO = 32
REL_BUCKETS = 32
REL_MAX_DIST = 128
NORM_EPS = 1e-6
NEG_INF = -1e30
N_SHARD = 4
LANES = 1024

ADAM_LR = 0.001
ADAM_B1 = 0.9
ADAM_B2 = 0.999
ADAM_EPS = 1e-08
ADAM_WD = 0.01
ADAM_STEP = 10

VMEM_LIMIT = 56 * 1024 * 1024


def _params(n_axes):
    return pltpu.CompilerParams(dimension_semantics=("arbitrary",) * n_axes, vmem_limit_bytes=VMEM_LIMIT)


def _dot(a, b, ca, cb):
    return lax.dot_general(a, b, (((ca,), (cb,)), ((), ())), preferred_element_type=F32)


def _mm(name, a, b, *, nt, tm, tn, ep_fn, outs, a_fn=None, b_sm=False, ep_in=(), deps=(), rows=None):
    M, K = a.shape
    rows = tm if rows is None else rows
    if b_sm:
        S, ks = b.shape[0], b.shape[2]
        N, per = (b.shape[1], None) if nt else (S * b.shape[2], b.shape[2] // tn)
        assert (S * ks == K) if nt else (b.shape[1] == K)
    else:
        N = b.shape[0] if nt else b.shape[1]
        assert (b.shape[1] if nt else b.shape[0]) == K
    assert M % tm == 0 and N % tn == 0 and tm % rows == 0
    ne, no, nd = len(ep_in), len(outs), len(deps)

    def body(a_ref, b_ref, *rest):
        ep_refs, out_refs = rest[:ne], rest[ne + nd:ne + nd + no]
        i = pl.program_id(1)
        sums = [None] * no
        for r in range(tm // rows):
            rs = pl.ds(r * rows, rows)

            def lhs(cols):
                av = a_ref[rs, cols]
                return (av if a_fn is None else a_fn(av)).astype(BF)

            if b_sm and nt:
                acc = None
                for s in range(S):
                    part = _dot(lhs(pl.ds(s * ks, ks)), b_ref[s].astype(BF), 1, 1)
                    acc = part if acc is None else acc + part
            else:
                acc = _dot(lhs(slice(None)), b_ref[...].astype(BF), 1, 1 if nt else 0)
            ep_vals = [ref[rs, :] if kind == "tile" else ref[...] for ref, (_, kind) in zip(ep_refs, ep_in)]
            vals = ep_fn(acc, *ep_vals)
            for o, ((kind, dt), ref, val) in enumerate(zip(outs, out_refs, vals)):
                if kind == "tile":
                    ref[rs, :] = val.astype(dt)
                else:
                    sums[o] = val if sums[o] is None else sums[o] + val
        for (kind, dt), ref, val in zip(outs, out_refs, sums):
            if kind == "colsum":
                @pl.when(i == 0)
                def _():
                    ref[...] = val

                @pl.when(i > 0)
                def _():
                    ref[...] += val

    if b_sm and nt:
        b_spec = pl.BlockSpec((S, tn, ks), lambda j, i: (0, j, 0))
    elif b_sm:
        b_spec = pl.BlockSpec((None, K, tn), lambda j, i: (j // per, 0, j % per))
    elif nt:
        b_spec = pl.BlockSpec((tn, K), lambda j, i: (j, 0))
    else:
        b_spec = pl.BlockSpec((K, tn), lambda j, i: (0, j))
    in_specs = [pl.BlockSpec((tm, K), lambda j, i: (i, 0)), b_spec]
    for arr, kind in ep_in:
        if kind == "tile":
            assert arr.shape == (M, N)
            in_specs.append(pl.BlockSpec((tm, tn), lambda j, i: (i, j)))
        else:
            assert arr.shape == (1, N)
            in_specs.append(pl.BlockSpec((1, tn), lambda j, i: (0, j)))
    in_specs += [pl.BlockSpec(memory_space=pl.ANY)] * nd
    out_shape, out_specs = [], []
    for kind, dt in outs:
        if kind == "tile":
            out_shape.append(jax.ShapeDtypeStruct((M, N), dt))
            out_specs.append(pl.BlockSpec((tm, tn), lambda j, i: (i, j)))
        else:
            out_shape.append(jax.ShapeDtypeStruct((1, N), F32))
            out_specs.append(pl.BlockSpec((1, tn), lambda j, i: (0, j)))
    return pl.pallas_call(
        body, name=name, grid=(N // tn, M // tm), in_specs=in_specs, out_specs=out_specs, out_shape=out_shape,
        compiler_params=_params(2),
    )(a, b, *[arr for arr, _ in ep_in], *deps)


def _mm_tn(name, a, b, *, tm, tn, tk, a_fn=None, out_sm=None):
    T, Ka = a.shape
    N = b.shape[1]
    assert b.shape[0] == T and T % tk == 0 and Ka % tm == 0 and N % tn == 0
    nk = T // tk

    def body(a_ref, b_ref, o_ref, acc_ref):
        k = pl.program_id(2)

        @pl.when(k == 0)
        def _():
            acc_ref[...] = jnp.zeros_like(acc_ref)

        av = a_ref[...]
        if a_fn is not None:
            av = a_fn(av)
        acc_ref[...] += _dot(av.astype(BF), b_ref[...].astype(BF), 0, 0)

        @pl.when(k == nk - 1)
        def _():
            o_ref[...] = acc_ref[...].astype(BF)

    if out_sm is None:
        out_shape = jax.ShapeDtypeStruct((Ka, N), BF)
        out_spec = pl.BlockSpec((tm, tn), lambda i, j, k: (i, j))
    else:
        per = (N // out_sm) // tn
        assert per * tn * out_sm == N
        out_shape = jax.ShapeDtypeStruct((out_sm, Ka, N // out_sm), BF)
        out_spec = pl.BlockSpec((None, tm, tn), lambda i, j, k: (j // per, i, j % per))
    return pl.pallas_call(
        body, name=name, grid=(Ka // tm, N // tn, nk),
        in_specs=[pl.BlockSpec((tk, tm), lambda i, j, k: (k, i)), pl.BlockSpec((tk, tn), lambda i, j, k: (k, j))],
        out_specs=out_spec, out_shape=out_shape, scratch_shapes=[pltpu.VMEM((tm, tn), F32)],
        compiler_params=_params(3),
    )(a, b)


def _relu2(v):
    r = jnp.maximum(v.astype(F32), 0.0)
    return r * r


def _rms_bwd_ep(dh, x, g, dres):
    rstd = lax.rsqrt(jnp.mean(x * x, axis=-1, keepdims=True) + NORM_EPS)
    xh = x * rstd
    dxh = dh * g
    dx = rstd * (dxh - xh * jnp.mean(dxh * xh, axis=-1, keepdims=True))
    tot = dres + dx
    return tot, tot, jnp.sum(dh * xh, axis=0, keepdims=True), jnp.sum(tot, axis=0, keepdims=True)


def _rms_fwd(name, x, g, tm=512, deps=()):
    T, Dm = x.shape

    def body(x_ref, g_ref, *rest):
        o_ref = rest[-1]
        xv = x_ref[...]
        rstd = lax.rsqrt(jnp.mean(xv * xv, axis=-1, keepdims=True) + NORM_EPS)
        o_ref[...] = (xv * rstd * g_ref[...]).astype(BF)

    return pl.pallas_call(
        body, name=name, grid=(T // tm,),
        in_specs=[pl.BlockSpec((tm, Dm), lambda i: (i, 0)), pl.BlockSpec((1, Dm), lambda i: (0, 0))]
        + [pl.BlockSpec(memory_space=pl.ANY)] * len(deps),
        out_specs=pl.BlockSpec((tm, Dm), lambda i: (i, 0)), out_shape=jax.ShapeDtypeStruct((T, Dm), BF),
        compiler_params=_params(1),
    )(x, g, *deps)


HEAD_COLS = 128


def _two_term_dot(v, m):
    hi = v.astype(BF)
    lo = (v - hi.astype(F32)).astype(BF)
    return _dot(hi, m, 1, 0) + _dot(lo, m, 1, 0)


def _head_sum(v, select):
    sel, sel_t = select
    return _two_term_dot(_two_term_dot(v, sel), sel_t)


def _head_select(n):
    sel = (np.arange(n)[:, None] // HEAD_DIM == np.arange(HEAD_COLS)[None, :]).astype(np.float32)
    return jnp.asarray(sel, dtype=BF), jnp.asarray(sel.T, dtype=BF)


def _qk_norm_fwd(qkv, qg_t, kg_t, tm=256):
    T = qkv.shape[0]
    scale = 1.0 / math.sqrt(HEAD_DIM)

    def body(x_ref, qg_ref, kg_ref, sq_ref, sqt_ref, sk_ref, skt_ref, q_ref, k_ref, v_ref):
        q = x_ref[:, pl.ds(0, ATTN_DIM)]
        rq = lax.rsqrt(_head_sum(q * q, (sq_ref[...], sqt_ref[...])) * (1.0 / HEAD_DIM) + NORM_EPS)
        q_ref[...] = (q * rq * qg_ref[...] * scale).astype(BF)
        k = x_ref[:, pl.ds(ATTN_DIM, KV_DIM)]
        rk = lax.rsqrt(_head_sum(k * k, (sk_ref[...], skt_ref[...])) * (1.0 / HEAD_DIM) + NORM_EPS)
        k_ref[...] = (k * rk * kg_ref[...]).astype(BF)
        v_ref[...] = x_ref[:, pl.ds(ATTN_DIM + KV_DIM, KV_DIM)].astype(BF)

    full = lambda shape: pl.BlockSpec(shape, lambda i: (0, 0))
    return pl.pallas_call(
        body, name="qk_norm_fwd", grid=(T // tm,),
        in_specs=[pl.BlockSpec((tm, QKV_DIM), lambda i: (i, 0)), full((1, ATTN_DIM)), full((1, KV_DIM)),
                  full((ATTN_DIM, HEAD_COLS)), full((HEAD_COLS, ATTN_DIM)), full((KV_DIM, HEAD_COLS)), full((HEAD_COLS, KV_DIM))],
        out_specs=[pl.BlockSpec((tm, ATTN_DIM), lambda i: (i, 0)), pl.BlockSpec((tm, KV_DIM), lambda i: (i, 0)),
                   pl.BlockSpec((tm, KV_DIM), lambda i: (i, 0))],
        out_shape=[jax.ShapeDtypeStruct((T, ATTN_DIM), BF), jax.ShapeDtypeStruct((T, KV_DIM), BF),
                   jax.ShapeDtypeStruct((T, KV_DIM), BF)],
        compiler_params=_params(1),
    )(qkv, qg_t, kg_t, *_head_select(ATTN_DIM), *_head_select(KV_DIM))


def _qk_norm_bwd(qkv, dqn, dkn, dv, qg_t, kg_t, tm=256):
    T = qkv.shape[0]

    def body(x_ref, dq_ref, dk_ref, dv_ref, qg_ref, kg_ref, sq_ref, sqt_ref, sk_ref, skt_ref,
             o_ref, db_ref, dqg_ref, dkg_ref):
        i = pl.program_id(0)

        def one(x, dy, g, select):
            r = lax.rsqrt(_head_sum(x * x, select) * (1.0 / HEAD_DIM) + NORM_EPS)
            xh = x * r
            dxh = dy * g
            dx = r * (dxh - xh * (_head_sum(dxh * xh, select) * (1.0 / HEAD_DIM)))
            return dx, jnp.sum(dy * xh, axis=0, keepdims=True)

        dq, dqg = one(x_ref[:, pl.ds(0, ATTN_DIM)], dq_ref[...], qg_ref[...], (sq_ref[...], sqt_ref[...]))
        dk, dkg = one(x_ref[:, pl.ds(ATTN_DIM, KV_DIM)], dk_ref[...], kg_ref[...], (sk_ref[...], skt_ref[...]))
        dvv = dv_ref[...]
        o_ref[:, pl.ds(0, ATTN_DIM)] = dq.astype(BF)
        o_ref[:, pl.ds(ATTN_DIM, KV_DIM)] = dk.astype(BF)
        o_ref[:, pl.ds(ATTN_DIM + KV_DIM, KV_DIM)] = dvv.astype(BF)
        sq, sk, sv = (jnp.sum(t, axis=0, keepdims=True) for t in (dq, dk, dvv))

        @pl.when(i == 0)
        def _():
            db_ref[:, pl.ds(0, ATTN_DIM)] = sq
            db_ref[:, pl.ds(ATTN_DIM, KV_DIM)] = sk
            db_ref[:, pl.ds(ATTN_DIM + KV_DIM, KV_DIM)] = sv
            dqg_ref[...] = dqg
            dkg_ref[...] = dkg

        @pl.when(i > 0)
        def _():
            db_ref[:, pl.ds(0, ATTN_DIM)] += sq
            db_ref[:, pl.ds(ATTN_DIM, KV_DIM)] += sk
            db_ref[:, pl.ds(ATTN_DIM + KV_DIM, KV_DIM)] += sv
            dqg_ref[...] += dqg
            dkg_ref[...] += dkg

    full = lambda shape: pl.BlockSpec(shape, lambda i: (0, 0))
    row = lambda n: pl.BlockSpec((tm, n), lambda i: (i, 0))
    return pl.pallas_call(
        body, name="qk_norm_bwd", grid=(T // tm,),
        in_specs=[row(QKV_DIM), row(ATTN_DIM), row(KV_DIM), row(KV_DIM), full((1, ATTN_DIM)), full((1, KV_DIM)),
                  full((ATTN_DIM, HEAD_COLS)), full((HEAD_COLS, ATTN_DIM)), full((KV_DIM, HEAD_COLS)), full((HEAD_COLS, KV_DIM))],
        out_specs=[row(QKV_DIM), full((1, QKV_DIM)), full((1, ATTN_DIM)), full((1, KV_DIM))],
        out_shape=[jax.ShapeDtypeStruct((T, QKV_DIM), BF), jax.ShapeDtypeStruct((1, QKV_DIM), F32),
                   jax.ShapeDtypeStruct((1, ATTN_DIM), F32), jax.ShapeDtypeStruct((1, KV_DIM), F32)],
        compiler_params=_params(1),
    )(qkv, dqn, dkn, dv, qg_t, kg_t, *_head_select(ATTN_DIM), *_head_select(KV_DIM))


ROWS = 64
COLS = 128


SUBLANES = 8
FIRST_TAP = HALO - (CONV_W - 1)


def _glu(a, g):
    return a.astype(F32) * jax.nn.sigmoid(g.astype(F32))


def _shifted(xe, s):
    return xe if s == 0 else pltpu.roll(xe, ROWS + HALO - s, axis=0)


def _conv_fwd(u, dw_pad, dw_b, ln_g, ln_b, tm=256):
    T = u.shape[0]
    Dm = D_MODEL
    hpt = tm // HALO

    def body(ac_ref, gc_ref, ap_ref, gp_ref, w_ref, wb_ref, lg_ref, lb_ref, cv_ref, s_ref, ext):
        i = pl.program_id(0)
        ext[pl.ds(0, HALO), :] = jnp.where(i > 0, _glu(ap_ref[...], gp_ref[...]), 0.0)
        ext[pl.ds(HALO, tm), :] = _glu(ac_ref[...], gc_ref[...])

        def rows(r, carry):
            r0 = pl.multiple_of(r * ROWS, ROWS)
            for c in range(Dm // COLS):
                cs = pl.ds(c * COLS, COLS)
                xe = ext[pl.ds(r0, ROWS + HALO), cs]
                acc = jnp.zeros((ROWS, COLS), F32)
                for s in range(SUBLANES):
                    xs = _shifted(xe, s)
                    for j in range(CONV_W):
                        off = FIRST_TAP + j
                        if off % SUBLANES == s:
                            acc = acc + xs[off - s:off - s + ROWS, :] * w_ref[pl.ds(j, 1), cs]
                cv_ref[pl.ds(r0, ROWS), cs] = acc + wb_ref[:, cs]
            return carry

        lax.fori_loop(0, tm // ROWS, rows, 0)
        cv = cv_ref[...]
        xc = cv - jnp.mean(cv, axis=-1, keepdims=True)
        y = xc * lax.rsqrt(jnp.mean(xc * xc, axis=-1, keepdims=True) + NORM_EPS) * lg_ref[...] + lb_ref[...]
        s_ref[...] = (y * jax.nn.sigmoid(y)).astype(BF)

    full = lambda shape: pl.BlockSpec(shape, lambda i: (0, 0))
    return pl.pallas_call(
        body, name="conv_fwd", grid=(T // tm,),
        in_specs=[pl.BlockSpec((tm, Dm), lambda i: (i, 0)), pl.BlockSpec((tm, Dm), lambda i: (i, 1)),
                  pl.BlockSpec((HALO, Dm), lambda i: (jnp.maximum(i * hpt - 1, 0), 0)),
                  pl.BlockSpec((HALO, Dm), lambda i: (jnp.maximum(i * hpt - 1, 0), 1)),
                  full((HALO, Dm)), full((1, Dm)), full((1, Dm)), full((1, Dm))],
        out_specs=[pl.BlockSpec((tm, Dm), lambda i: (i, 0)), pl.BlockSpec((tm, Dm), lambda i: (i, 0))],
        out_shape=[jax.ShapeDtypeStruct((T, Dm), F32), jax.ShapeDtypeStruct((T, Dm), BF)],
        scratch_shapes=[pltpu.VMEM((tm + HALO, Dm), F32)],
        compiler_params=_params(1),
    )(u, u, u, u, dw_pad, dw_b, ln_g, ln_b)


def _ln_silu_bwd_ep(ds, cv, lg, lb):
    xc = cv - jnp.mean(cv, axis=-1, keepdims=True)
    rstd = lax.rsqrt(jnp.mean(xc * xc, axis=-1, keepdims=True) + NORM_EPS)
    xh = xc * rstd
    y = xh * lg + lb
    sg = jax.nn.sigmoid(y)
    dy = ds * (sg * (1.0 + y * (1.0 - sg)))
    dxh = dy * lg
    dcv = rstd * (dxh - jnp.mean(dxh, axis=-1, keepdims=True) - xh * jnp.mean(dxh * xh, axis=-1, keepdims=True))
    return (dcv, jnp.sum(dy * xh, axis=0, keepdims=True), jnp.sum(dy, axis=0, keepdims=True),
            jnp.sum(dcv, axis=0, keepdims=True))


def _conv_bwd(u, dcv, dw_pad, tm=256):
    T = u.shape[0]
    Dm = D_MODEL
    hpt = tm // HALO
    last = T // HALO - 1
    nt = T // tm

    def body(ac_ref, gc_ref, ap_ref, gp_ref, dc_ref, dn_ref, w_ref, du_ref, db_ref, dw_ref, ext_g, ext_d):
        i = pl.program_id(0)
        ext_g[pl.ds(0, HALO), :] = jnp.where(i > 0, _glu(ap_ref[...], gp_ref[...]), 0.0)
        ext_g[pl.ds(HALO, tm), :] = _glu(ac_ref[...], gc_ref[...])
        ext_d[pl.ds(0, tm), :] = dc_ref[...]
        ext_d[pl.ds(tm, HALO), :] = jnp.where(i < nt - 1, dn_ref[...], 0.0)

        @pl.when(i == 0)
        def _():
            db_ref[...] = jnp.zeros_like(db_ref)
            dw_ref[...] = jnp.zeros_like(dw_ref)

        def rows(r, carry):
            r0 = pl.multiple_of(r * ROWS, ROWS)
            rs = pl.ds(r0, ROWS)
            for c in range(Dm // COLS):
                cs = pl.ds(c * COLS, COLS)
                cs2 = pl.ds(Dm + c * COLS, COLS)
                de = ext_d[pl.ds(r0, ROWS + HALO), cs]
                ge = ext_g[pl.ds(r0, ROWS + HALO), cs]
                dcur = de[0:ROWS, :]
                acc = jnp.zeros((ROWS, COLS), F32)
                for s in range(SUBLANES):
                    ds_, gs_ = _shifted(de, s), _shifted(ge, s)
                    for j in range(CONV_W):
                        off = CONV_W - 1 - j
                        if off % SUBLANES == s:
                            acc = acc + ds_[off - s:off - s + ROWS, :] * w_ref[pl.ds(j, 1), cs]
                        goff = FIRST_TAP + j
                        if goff % SUBLANES == s:
                            prod = dcur * gs_[goff - s:goff - s + ROWS, :]
                            dw_ref[j, :, cs] += jnp.sum(prod.reshape(ROWS // SUBLANES, SUBLANES, COLS), axis=0)
                a = ac_ref[rs, cs].astype(F32)
                sg = jax.nn.sigmoid(gc_ref[rs, cs].astype(F32))
                da = acc * sg
                dg = acc * a * sg * (1.0 - sg)
                du_ref[rs, cs] = da.astype(BF)
                du_ref[rs, cs2] = dg.astype(BF)
                db_ref[:, cs] += jnp.sum(da, axis=0, keepdims=True)
                db_ref[:, cs2] += jnp.sum(dg, axis=0, keepdims=True)
            return carry

        lax.fori_loop(0, tm // ROWS, rows, 0)

    return pl.pallas_call(
        body, name="conv_bwd", grid=(nt,),
        in_specs=[pl.BlockSpec((tm, Dm), lambda i: (i, 0)), pl.BlockSpec((tm, Dm), lambda i: (i, 1)),
                  pl.BlockSpec((HALO, Dm), lambda i: (jnp.maximum(i * hpt - 1, 0), 0)),
                  pl.BlockSpec((HALO, Dm), lambda i: (jnp.maximum(i * hpt - 1, 0), 1)),
                  pl.BlockSpec((tm, Dm), lambda i: (i, 0)),
                  pl.BlockSpec((HALO, Dm), lambda i: (jnp.minimum((i + 1) * hpt, last), 0)),
                  pl.BlockSpec((HALO, Dm), lambda i: (0, 0))],
        out_specs=[pl.BlockSpec((tm, 2 * Dm), lambda i: (i, 0)), pl.BlockSpec((1, 2 * Dm), lambda i: (0, 0)),
                   pl.BlockSpec((HALO, 8, Dm), lambda i: (0, 0, 0))],
        out_shape=[jax.ShapeDtypeStruct((T, 2 * Dm), BF), jax.ShapeDtypeStruct((1, 2 * Dm), F32),
                   jax.ShapeDtypeStruct((HALO, 8, Dm), F32)],
        scratch_shapes=[pltpu.VMEM((tm + HALO, Dm), F32), pltpu.VMEM((tm + HALO, Dm), F32)],
        compiler_params=_params(1),
    )(u, u, u, u, dcv, dcv, dw_pad)


def _bucket_table():
    q_loc = np.arange(BLOCK)[:, None]
    k_loc = np.arange(2 * BLOCK)[None, :]
    dist = q_loc + BLOCK - k_loc
    n = np.maximum(dist, 0)
    max_exact = REL_BUCKETS // 2
    large = max_exact + (np.log(np.maximum(n, 1).astype(np.float32) / max_exact)
                         / math.log(REL_MAX_DIST / max_exact) * (REL_BUCKETS - max_exact)).astype(np.int32)
    large = np.minimum(large, REL_BUCKETS - 1)
    bucket = np.where(n < max_exact, n, large).astype(np.int32)
    return jnp.asarray(np.where((dist >= 0) & (dist < BLOCK), bucket, -1).astype(np.int32))


def _bias_table(rel_bias, bucket):
    def body(rb_ref, bk_ref, o_ref):
        bk = bk_ref[...]
        for h in range(N_HEADS):
            acc = jnp.full((BLOCK, 2 * BLOCK), NEG_INF, F32)
            for b in range(REL_BUCKETS):
                acc = jnp.where(bk == b, rb_ref[b, h], acc)
            o_ref[h] = acc

    return pl.pallas_call(
        body, name="bias_table", out_shape=jax.ShapeDtypeStruct((N_HEADS, BLOCK, 2 * BLOCK), F32),
        in_specs=[pl.BlockSpec(memory_space=pltpu.SMEM), pl.BlockSpec(memory_space=pltpu.VMEM)],
        out_specs=pl.BlockSpec(memory_space=pltpu.VMEM),
    )(rel_bias, bucket)


def _bias_grad(dbias, bucket):
    def body(db_ref, bk_ref, o_ref):
        bk = bk_ref[...]
        for b in range(REL_BUCKETS):
            sel = bk == b
            for h in range(N_HEADS):
                o_ref[b, h] = jnp.sum(jnp.where(sel, db_ref[h], 0.0))

    return pl.pallas_call(
        body, name="bias_grad", out_shape=jax.ShapeDtypeStruct((REL_BUCKETS, N_HEADS), F32),
        in_specs=[pl.BlockSpec(memory_space=pltpu.VMEM), pl.BlockSpec(memory_space=pltpu.VMEM)],
        out_specs=pl.BlockSpec(memory_space=pltpu.SMEM),
    )(dbias, bucket)


GROUP_ROWS = GROUP * BLOCK


def _head_probs(qk, bias_h, sink, first):
    s = jnp.where(first, NEG_INF, qk + bias_h)
    m = jnp.maximum(jnp.max(s, axis=-1, keepdims=True), sink)
    p = jnp.exp(s - m)
    ps = jnp.exp(sink - m)
    inv = 1.0 / (jnp.sum(p, axis=-1, keepdims=True) + ps)
    return p * inv, ps * inv


def _band(prev_ref, cur_ref, g):
    hs = pl.ds(g * HEAD_DIM, HEAD_DIM)
    return jnp.concatenate([prev_ref[:, hs], cur_ref[:, hs]], axis=0)


def _stack_heads(ref, g):
    return jnp.concatenate([ref[:, pl.ds((g * GROUP + hh) * HEAD_DIM, HEAD_DIM)] for hh in range(GROUP)], axis=0)


def _unstack_heads(ref, g, stacked, dtype):
    for hh in range(GROUP):
        ref[:, pl.ds((g * GROUP + hh) * HEAD_DIM, HEAD_DIM)] = stacked[hh * BLOCK:(hh + 1) * BLOCK, :].astype(dtype)


def _first_mask(n):
    col = lax.broadcasted_iota(jnp.int32, (1, 2 * BLOCK), 1)
    return jnp.logical_and(n == 0, col < BLOCK)


def _head_rows(hh):
    return pl.ds(hh * BLOCK, BLOCK)


def _attn_fwd(qn, kn, vv, bias, sinks):
    T = qn.shape[0]
    nb = T // BLOCK

    def body(sk_ref, q_ref, kc_ref, kp_ref, vc_ref, vp_ref, b_ref, o_ref, qk_buf, p_buf):
        first = _first_mask(pl.program_id(0))
        for g in range(N_KV):
            k = _band(kp_ref, kc_ref, g)
            v = _band(vp_ref, vc_ref, g)
            qk_buf[g] = _dot(_stack_heads(q_ref, g), k, 1, 1)
            for hh in range(GROUP):
                h = g * GROUP + hh
                pn, _ = _head_probs(qk_buf[g, _head_rows(hh), :], b_ref[h], sk_ref[h], first)
                p_buf[g, _head_rows(hh), :] = pn.astype(BF)
            _unstack_heads(o_ref, g, _dot(p_buf[g], v, 1, 0), BF)

    cur = lambda n: (n, 0)
    prev = lambda n: (jnp.maximum(n - 1, 0), 0)
    return pl.pallas_call(
        body, name="attn_fwd", grid=(nb,),
        in_specs=[pl.BlockSpec(memory_space=pltpu.SMEM), pl.BlockSpec((BLOCK, ATTN_DIM), cur),
                  pl.BlockSpec((BLOCK, KV_DIM), cur), pl.BlockSpec((BLOCK, KV_DIM), prev),
                  pl.BlockSpec((BLOCK, KV_DIM), cur), pl.BlockSpec((BLOCK, KV_DIM), prev),
                  pl.BlockSpec((N_HEADS, BLOCK, 2 * BLOCK), lambda n: (0, 0, 0))],
        out_specs=pl.BlockSpec((BLOCK, ATTN_DIM), cur), out_shape=jax.ShapeDtypeStruct((T, ATTN_DIM), BF),
        scratch_shapes=[pltpu.VMEM((N_KV, GROUP_ROWS, 2 * BLOCK), F32), pltpu.VMEM((N_KV, GROUP_ROWS, 2 * BLOCK), BF)],
        compiler_params=_params(1),
    )(sinks, qn, kn, kn, vv, vv, bias)


def _attn_bwd(qn, kn, vv, bias, sinks, do):
    T = qn.shape[0]
    nb = T // BLOCK
    scale = 1.0 / math.sqrt(HEAD_DIM)

    def body(sk_ref, q_ref, kc_ref, kp_ref, vc_ref, vp_ref, b_ref, do_ref,
             dq_ref, dk_ref, dv_ref, db_ref, dsk_ref, dk_full, dv_full, dk_carry, dv_carry, qk_buf, dp_buf, p_buf, ds_buf):
        n = pl.program_id(0)

        @pl.when(n == 0)
        def _():
            db_ref[...] = jnp.zeros_like(db_ref)
            dk_carry[...] = jnp.zeros_like(dk_carry)
            dv_carry[...] = jnp.zeros_like(dv_carry)
            for h in range(N_HEADS):
                dsk_ref[h] = 0.0

        @pl.when(n < nb)
        def _():
            first = _first_mask(n)
            for g in range(N_KV):
                k = _band(kp_ref, kc_ref, g)
                v = _band(vp_ref, vc_ref, g)
                q = _stack_heads(q_ref, g)
                dout = _stack_heads(do_ref, g)
                qk_buf[g] = _dot(q, k, 1, 1)
                dp_buf[g] = _dot(dout, v, 1, 1)
                for hh in range(GROUP):
                    h = g * GROUP + hh
                    rows = _head_rows(hh)
                    pn, psink = _head_probs(qk_buf[g, rows, :], b_ref[h], sk_ref[h], first)
                    dp = dp_buf[g, rows, :]
                    delta = jnp.sum(pn * dp, axis=-1, keepdims=True)
                    ds = pn * (dp - delta)
                    dsk_ref[h] += -jnp.sum(psink * delta)
                    db_ref[h] += ds
                    ds_buf[g, rows, :] = ds.astype(BF)
                    p_buf[g, rows, :] = pn.astype(BF)
                dsb = ds_buf[g]
                _unstack_heads(dq_ref, g, _dot(dsb, k, 1, 0) * scale, F32)
                gs = pl.ds(g * HEAD_DIM, HEAD_DIM)
                dk_full[:, gs] = _dot(dsb, q, 0, 0)
                dv_full[:, gs] = _dot(p_buf[g], dout, 0, 0)

        @pl.when(n == nb)
        def _():
            dk_full[...] = jnp.zeros_like(dk_full)
            dv_full[...] = jnp.zeros_like(dv_full)

        dk_ref[...] = dk_carry[...] + dk_full[pl.ds(0, BLOCK), :]
        dv_ref[...] = dv_carry[...] + dv_full[pl.ds(0, BLOCK), :]
        dk_carry[...] = dk_full[pl.ds(BLOCK, BLOCK), :]
        dv_carry[...] = dv_full[pl.ds(BLOCK, BLOCK), :]

    cur = lambda n: (jnp.minimum(n, nb - 1), 0)
    prev = lambda n: (jnp.maximum(jnp.minimum(n, nb - 1) - 1, 0), 0)
    out_kv = lambda n: (jnp.maximum(n - 1, 0), 0)
    return pl.pallas_call(
        body, name="attn_bwd", grid=(nb + 1,),
        in_specs=[pl.BlockSpec(memory_space=pltpu.SMEM), pl.BlockSpec((BLOCK, ATTN_DIM), cur),
                  pl.BlockSpec((BLOCK, KV_DIM), cur), pl.BlockSpec((BLOCK, KV_DIM), prev),
                  pl.BlockSpec((BLOCK, KV_DIM), cur), pl.BlockSpec((BLOCK, KV_DIM), prev),
                  pl.BlockSpec((N_HEADS, BLOCK, 2 * BLOCK), lambda n: (0, 0, 0)),
                  pl.BlockSpec((BLOCK, ATTN_DIM), cur)],
        out_specs=[pl.BlockSpec((BLOCK, ATTN_DIM), cur), pl.BlockSpec((BLOCK, KV_DIM), out_kv),
                   pl.BlockSpec((BLOCK, KV_DIM), out_kv),
                   pl.BlockSpec((N_HEADS, BLOCK, 2 * BLOCK), lambda n: (0, 0, 0)),
                   pl.BlockSpec(memory_space=pltpu.SMEM)],
        out_shape=[jax.ShapeDtypeStruct((T, ATTN_DIM), F32), jax.ShapeDtypeStruct((T, KV_DIM), F32),
                   jax.ShapeDtypeStruct((T, KV_DIM), F32),
                   jax.ShapeDtypeStruct((N_HEADS, BLOCK, 2 * BLOCK), F32), jax.ShapeDtypeStruct((N_HEADS,), F32)],
        scratch_shapes=[pltpu.VMEM((2 * BLOCK, KV_DIM), F32), pltpu.VMEM((2 * BLOCK, KV_DIM), F32),
                        pltpu.VMEM((BLOCK, KV_DIM), F32), pltpu.VMEM((BLOCK, KV_DIM), F32),
                        pltpu.VMEM((N_KV, GROUP_ROWS, 2 * BLOCK), F32), pltpu.VMEM((N_KV, GROUP_ROWS, 2 * BLOCK), F32),
                        pltpu.VMEM((N_KV, GROUP_ROWS, 2 * BLOCK), BF), pltpu.VMEM((N_KV, GROUP_ROWS, 2 * BLOCK), BF)],
        compiler_params=_params(1),
    )(sinks, qn, kn, kn, vv, vv, bias, do)


def _coords():
    return lax.axis_index("x"), lax.axis_index("y"), lax.axis_index("c")


def _gather8(name, v, with_sum):
    R = v.shape[0]

    def body(v_ref, all_ref, *rest):
        sum_ref = rest[0] if with_sum else None
        send_sems, recv_sems, local_sem = rest[-3:]
        x, y, c = _coords()
        me = 4 * x + 2 * y + c
        local = pltpu.make_async_copy(v_ref, all_ref.at[me], local_sem)
        local.start()
        sends = []
        for k in range(1, 8):
            peer = (x ^ (k >> 2), y ^ ((k >> 1) & 1), c ^ (k & 1))
            cp = pltpu.make_async_remote_copy(src_ref=v_ref, dst_ref=all_ref.at[me], send_sem=send_sems.at[k - 1],
                                              recv_sem=recv_sems.at[k - 1], device_id=peer, device_id_type=MESH)
            cp.start()
            sends.append(cp)
        for k in range(1, 8):
            peer = (x ^ (k >> 2), y ^ ((k >> 1) & 1), c ^ (k & 1))
            pltpu.make_async_remote_copy(src_ref=v_ref, dst_ref=all_ref.at[me ^ k], send_sem=send_sems.at[k - 1],
                                         recv_sem=recv_sems.at[k - 1], device_id=peer, device_id_type=MESH).wait_recv()
        for cp in sends:
            cp.wait_send()
        local.wait()
        if with_sum:
            tot = all_ref[0]
            for d in range(1, 8):
                tot = tot + all_ref[d]
            sum_ref[...] = tot

    out_shape = [jax.ShapeDtypeStruct((8, R, LANES), F32)]
    if with_sum:
        out_shape.append(jax.ShapeDtypeStruct((R, LANES), F32))
    vm = pl.BlockSpec(memory_space=pltpu.VMEM)
    return pl.pallas_call(
        body, name=name, out_shape=out_shape, in_specs=[vm], out_specs=[vm] * len(out_shape),
        scratch_shapes=[pltpu.SemaphoreType.DMA((7,)), pltpu.SemaphoreType.DMA((7,)), pltpu.SemaphoreType.DMA],
    )(v)


CHIP_FLIPS = ((1, 0), (0, 1), (1, 1))


HBM_SPEC = pl.BlockSpec(memory_space=pltpu.HBM)
SEM_SPEC = pl.BlockSpec(memory_space=pltpu.SEMAPHORE)
ANY_SPEC = pl.BlockSpec(memory_space=pl.ANY)
DATAFLOW = pltpu.SideEffectType.DATAFLOW_SIDE_EFFECTING


def _chip_copy(land, sems, idx, slot_src, slot_dst, peer):
    send_sems, recv_sems = sems
    return pltpu.make_async_remote_copy(src_ref=land.at[slot_src], dst_ref=land.at[slot_dst], send_sem=send_sems.at[idx],
                                        recv_sem=recv_sems.at[idx], device_id=peer, device_id_type=MESH)


def _gather_start(stacks, groups, after):
    n = len(stacks)
    ng = len(groups)
    after = tuple(after)

    def body(*refs):
        lands = refs[:n]
        first = n + len(after)
        sems = [(refs[first + 2 * g], refs[first + 2 * g + 1]) for g in range(ng)]
        token = refs[-1]
        x, y, c = _coords()
        s = 2 * x + y
        for g, members in enumerate(groups):
            for i, t in enumerate(members):
                for j, (fx, fy) in enumerate(CHIP_FLIPS):
                    _chip_copy(lands[t], sems[g], 3 * i + j, s, s, (x ^ fx, y ^ fy, c)).start()
        token[...] = jnp.zeros_like(token)

    out_shape = []
    for members in groups:
        out_shape += [pltpu.SemaphoreType.DMA((3 * len(members),))] * 2
    out_shape += [pltpu.HBM(w.shape, w.dtype) for w in stacks]
    out_shape.append(jax.ShapeDtypeStruct((8, 128), F32))
    res = pl.pallas_call(
        body, name="gather_start", out_shape=out_shape, in_specs=[HBM_SPEC] * n + [ANY_SPEC] * len(after),
        out_specs=[SEM_SPEC] * (2 * ng) + [HBM_SPEC] * n + [pl.BlockSpec(memory_space=pltpu.VMEM)],
        input_output_aliases={t: 2 * ng + t for t in range(n)},
        compiler_params=pltpu.CompilerParams(has_side_effects=DATAFLOW),
    )(*[pltpu.with_memory_space_constraint(w, pltpu.HBM) for w in stacks], *after)
    sems = [(res[2 * g], res[2 * g + 1]) for g in range(ng)]
    return sems, list(res[2 * ng:2 * ng + n]), res[-1]


def _gather_wait(name, stacks, sems, after):
    n = len(stacks)
    after = tuple(after)

    def body(*refs):
        lands = refs[:n]
        group_sems = (refs[n], refs[n + 1])
        x, y, c = _coords()
        s = 2 * x + y
        for i in range(n):
            for j, (fx, fy) in enumerate(CHIP_FLIPS):
                cp = _chip_copy(lands[i], group_sems, 3 * i + j, s, 2 * (x ^ fx) + (y ^ fy), (x ^ fx, y ^ fy, c))
                cp.wait_send()
                cp.wait_recv()

    return pl.pallas_call(
        body, name=name, out_shape=[pltpu.HBM(w.shape, w.dtype) for w in stacks],
        in_specs=[HBM_SPEC] * n + [SEM_SPEC, SEM_SPEC] + [ANY_SPEC] * len(after), out_specs=[HBM_SPEC] * n,
        input_output_aliases={t: t for t in range(n)},
        compiler_params=pltpu.CompilerParams(has_side_effects=DATAFLOW),
    )(*stacks, sems[0], sems[1], *after)


N_PEERS = 7


def _peer(x, y, c, k):
    return x ^ (k >> 2), y ^ ((k >> 1) & 1), c ^ (k & 1)


def _reduce_copy(grad, land, sems, idx, x, y, c, k):
    px, py, pc = _peer(x, y, c, k)
    rh = grad.shape[1] // 2
    return pltpu.make_async_remote_copy(src_ref=grad.at[2 * px + py, pl.ds(pc * rh, rh), :], dst_ref=land.at[k - 1],
                                        send_sem=sems[0].at[idx], recv_sem=sems[1].at[idx], device_id=(px, py, pc),
                                        device_id_type=MESH)


def _reduce_start(name, grads):
    n = len(grads)

    def body(*refs):
        src, lands, sems, token = refs[:n], refs[n:2 * n], (refs[2 * n], refs[2 * n + 1]), refs[-1]
        x, y, c = _coords()
        for t in range(n):
            for k in range(1, N_PEERS + 1):
                _reduce_copy(src[t], lands[t], sems, N_PEERS * t + k - 1, x, y, c, k).start()
        token[...] = jnp.zeros_like(token)

    lands = [lax.empty((N_PEERS, g.shape[1] // 2, g.shape[2]), g.dtype) for g in grads]
    out_shape = [pltpu.SemaphoreType.DMA((N_PEERS * n,))] * 2
    out_shape += [pltpu.HBM(a.shape, a.dtype) for a in list(grads) + lands]
    out_shape.append(jax.ShapeDtypeStruct((8, 128), F32))
    res = pl.pallas_call(
        body, name=name, out_shape=out_shape, in_specs=[HBM_SPEC] * (2 * n),
        out_specs=[SEM_SPEC] * 2 + [HBM_SPEC] * (2 * n) + [pl.BlockSpec(memory_space=pltpu.VMEM)],
        input_output_aliases={t: 2 + t for t in range(2 * n)},
        compiler_params=pltpu.CompilerParams(has_side_effects=DATAFLOW),
    )(*[pltpu.with_memory_space_constraint(a, pltpu.HBM) for a in list(grads) + lands])
    return (res[0], res[1]), list(res[2:2 + n]), list(res[2 + n:2 + 2 * n]), res[-1]


def _reduce_wait(name, grads, lands, sems, after):
    n = len(grads)
    after = tuple(after)

    def body(*refs):
        src, dst, group_sems = refs[:n], refs[n:2 * n], (refs[2 * n], refs[2 * n + 1])
        x, y, c = _coords()
        for t in range(n):
            for k in range(1, N_PEERS + 1):
                cp = _reduce_copy(src[t], dst[t], group_sems, N_PEERS * t + k - 1, x, y, c, k)
                cp.wait_send()
                cp.wait_recv()

    res = pl.pallas_call(
        body, name=name, out_shape=[pltpu.HBM(a.shape, a.dtype) for a in list(grads) + list(lands)],
        in_specs=[HBM_SPEC] * (2 * n) + [SEM_SPEC, SEM_SPEC] + [ANY_SPEC] * len(after), out_specs=[HBM_SPEC] * (2 * n),
        input_output_aliases={t: t for t in range(2 * n)},
        compiler_params=pltpu.CompilerParams(has_side_effects=DATAFLOW),
    )(*grads, *lands, sems[0], sems[1], *after)
    return list(res[:n]), list(res[n:])


def _join_halves(name, halves):
    n = len(halves)

    def body(*refs):
        src, dst = refs[:n], refs[n:2 * n]
        send_sems, recv_sems = refs[2 * n:]
        x, y, c = _coords()
        cps = []
        for t in range(n):
            cp = pltpu.make_async_remote_copy(src_ref=src[t], dst_ref=dst[t], send_sem=send_sems.at[t],
                                              recv_sem=recv_sems.at[t], device_id=(x, y, 1 - c), device_id_type=MESH)
            cp.start()
            cps.append(cp)
        for cp in cps:
            cp.wait()

    anyspec = pl.BlockSpec(memory_space=pl.ANY)
    return pl.pallas_call(
        body, name=name, out_shape=[jax.ShapeDtypeStruct(h.shape, h.dtype) for h in halves],
        in_specs=[anyspec] * n, out_specs=[anyspec] * n,
        scratch_shapes=[pltpu.SemaphoreType.DMA((n,)), pltpu.SemaphoreType.DMA((n,))],
    )(*halves)


def _row_block(rows):
    for rb in (512, 256, 128, 64, 32, 16):
        if rows % rb == 0:
            return rb
    raise ValueError(rows)


def _sum_devices(name, grad, land, place):
    S, R, C = grad.shape
    rh = R // 2
    rb = _row_block(rh)
    nbh = rh // rb

    def body(place_ref, g_ref, l_ref, o_ref):
        tot = g_ref[...].astype(F32)
        for k in range(N_PEERS):
            tot = tot + l_ref[k].astype(F32)
        o_ref[...] = tot

    return pl.pallas_call(
        body, name=name,
        grid_spec=pltpu.PrefetchScalarGridSpec(
            num_scalar_prefetch=1, grid=(nbh,),
            in_specs=[pl.BlockSpec((None, rb, C), lambda r, place: (place[0], place[1] * nbh + r, 0)),
                      pl.BlockSpec((N_PEERS, rb, C), lambda r, place: (0, r, 0))],
            out_specs=pl.BlockSpec((rb, C), lambda r, place: (r, 0))),
        out_shape=jax.ShapeDtypeStruct((rh, C), F32), compiler_params=_params(1),
    )(place, grad, land)


def _adamw_math(w, g, m, v):
    m2 = ADAM_B1 * m + (1.0 - ADAM_B1) * g
    v2 = ADAM_B2 * v + (1.0 - ADAM_B2) * (g * g)
    m_hat = m2 / (1.0 - ADAM_B1 ** ADAM_STEP)
    v_hat = v2 / (1.0 - ADAM_B2 ** ADAM_STEP)
    delta = -ADAM_LR * (m_hat / (jnp.sqrt(v_hat) + ADAM_EPS) + ADAM_WD * w)
    return delta, m2, v2


def _adamw(name, w, m, v, gs):
    L, R, C = w.shape
    Rh = R // 2
    rb = _row_block(Rh)
    nbh = Rh // rb
    assert len(gs) == L

    def body(core_ref, w_ref, m_ref, v_ref, *rest):
        g_refs, (go_ref, d_ref, m2_ref, v2_ref) = rest[:2 * L], rest[2 * L:]
        layer, half = pl.program_id(0), pl.program_id(1)
        mine = half == core_ref[0]
        g = jnp.where(mine, g_refs[0][...], g_refs[1][...])
        for t in range(1, L):
            g = jnp.where(layer == t, jnp.where(mine, g_refs[2 * t][...], g_refs[2 * t + 1][...]), g)
        delta, m2, v2 = _adamw_math(w_ref[...], g, m_ref[...], v_ref[...])
        go_ref[...] = g
        d_ref[...] = delta
        m2_ref[...] = m2
        v2_ref[...] = v2

    wspec = pl.BlockSpec((None, rb, C), lambda l, h, r, core: (l, h * nbh + r, 0))
    gspec = pl.BlockSpec((rb, C), lambda l, h, r, core: (r, 0))
    return pl.pallas_call(
        body, name=name,
        grid_spec=pltpu.PrefetchScalarGridSpec(num_scalar_prefetch=1, grid=(L, 2, nbh),
                                               in_specs=[wspec] * 3 + [gspec] * (2 * L), out_specs=[wspec] * 4),
        out_shape=[jax.ShapeDtypeStruct((L, R, C), F32)] * 4, compiler_params=_params(3),
    )(lax.axis_index("c").astype(jnp.int32).reshape(1), w, m, v, *[g for pair in gs for g in pair])


def _adamw_small(w, g, m, v):
    def body(w_ref, g_ref, m_ref, v_ref, d_ref, m2_ref, v2_ref):
        delta, m2, v2 = _adamw_math(w_ref[...], g_ref[...], m_ref[...], v_ref[...])
        d_ref[...] = delta
        m2_ref[...] = m2
        v2_ref[...] = v2

    return pl.pallas_call(body, name="adamw_small", out_shape=[jax.ShapeDtypeStruct(w.shape, F32)] * 3)(w, g, m, v)


def _packed_rows(shape):
    c = shape[-1]
    return (int(np.prod(shape)) // c) * -(-c // LANES)


def _pack(arrays):
    total = sum(_packed_rows(a.shape) for a in arrays)
    total += -total % 8
    buf, r0 = None, 0
    for a in arrays:
        a = a.astype(F32).reshape(-1, a.shape[-1])
        r, c = a.shape
        k = -(-c // LANES)
        a = jnp.pad(a, ((0, 0), (0, k * LANES - c))).reshape(r * k, LANES)
        a = jnp.pad(a, ((r0, total - r0 - r * k), (0, 0)))
        buf = a if buf is None else buf + a
        r0 += r * k
    return buf


def _unpack(buf, shapes):
    out, r0 = [], 0
    for shp in shapes:
        c = shp[-1]
        rows = _packed_rows(shp)
        out.append(buf[r0:r0 + rows].reshape(-1, -(-c // LANES) * LANES)[:, :c].reshape(shp))
        r0 += rows
    return out


def _rms(x, g):
    return x * lax.rsqrt(jnp.mean(x * x, axis=-1, keepdims=True) + NORM_EPS) * g


def _residual_norm_ep(acc, *rest):
    *bias, res, gain = rest
    x = acc + res + (bias[0] if bias else 0.0)
    return x, _rms(x, gain)


RESIDUAL_NORM_OUTS = (("tile", F32), ("tile", BF))


def _mlp_up(tag, h, w_up_sm):
    (up,) = _mm(f"mlp{tag}_up", h, w_up_sm, nt=False, b_sm=True, tm=2048, tn=1024, rows=256,
                ep_fn=lambda acc: (acc,), outs=(("tile", BF),))
    return up


RMS_BWD_OUTS = (("tile", F32), ("tile", BF), ("colsum", F32), ("colsum", F32))


def _mlp_bwd(tag, dy, dy_bf, x, g, up, w_up_sm, w_down):
    (dup,) = _mm(f"mlp{tag}_dup", dy_bf, w_down, nt=True, tm=2048, tn=1024, rows=256, ep_in=((up, "tile"),),
                 ep_fn=lambda acc, u: (acc * (2.0 * jnp.maximum(u.astype(F32), 0.0)),), outs=(("tile", BF),))
    dx, dx_bf, dg, dx_sum = _mm(f"mlp{tag}_dx", dup, w_up_sm, nt=True, b_sm=True, tm=512, tn=1024, rows=256,
                                ep_in=((x, "tile"), (g, "row"), (dy, "tile")), ep_fn=_rms_bwd_ep, outs=RMS_BWD_OUTS)
    return dx, dx_bf, dg, dx_sum, dup


class _Reduction:
    def __init__(self, tag, grads, place):
        self.tag, self.place = tag, place
        self.sems, self.grads, self.lands, self.token = _reduce_start(f"reduce_start_{tag}", grads)

    def finish(self, after):
        grads, lands = _reduce_wait(f"reduce_wait_{self.tag}", self.grads, self.lands, self.sems, after)
        return [_sum_devices(f"reduce_sum_{self.tag}{i}", g, l, self.place) for i, (g, l) in enumerate(zip(grads, lands))]


def kernel(x, conv_norm_g, conv_w_in, conv_b_in, conv_dw, conv_dw_b, conv_ln_g, conv_ln_b, conv_w_out, conv_b_out, attn_norm_g, w_qkv, b_qkv, q_norm_g, k_norm_g, sinks, w_o, b_o, rel_bias, mlp_norm_g, w_up, w_down, loss_target, m_conv_norm_g, m_conv_w_in, m_conv_b_in, m_conv_dw, m_conv_dw_b, m_conv_ln_g, m_conv_ln_b, m_conv_w_out, m_conv_b_out, m_attn_norm_g, m_w_qkv, m_b_qkv, m_q_norm_g, m_k_norm_g, m_sinks, m_w_o, m_b_o, m_rel_bias, m_mlp_norm_g, m_w_up, m_w_down, v_conv_norm_g, v_conv_w_in, v_conv_b_in, v_conv_dw, v_conv_dw_b, v_conv_ln_g, v_conv_ln_b, v_conv_w_out, v_conv_b_out, v_attn_norm_g, v_w_qkv, v_b_qkv, v_q_norm_g, v_k_norm_g, v_sinks, v_w_o, v_b_o, v_rel_bias, v_mlp_norm_g, v_w_up, v_w_down):
    Dm = D_MODEL
    x2d = x[0]
    tgt = loss_target[0]
    T = x2d.shape[0]
    shard = 2 * lax.axis_index("x") + lax.axis_index("y")

    sharded_small = [conv_dw[0], attn_norm_g, b_qkv, b_o]
    (gathered,) = _gather8("gather_small_weights", _pack(sharded_small), with_sum=False)
    chips = [_unpack(gathered[2 * s], [a.shape for a in sharded_small]) for s in range(N_SHARD)]
    dw_f, attn_norm_f, b_qkv_f, b_o_f = (jnp.concatenate([chips[s][t] for s in range(N_SHARD)], axis=-1)
                                         for t in range(len(sharded_small)))
    dw_pad = jnp.pad(dw_f, ((0, HALO - CONV_W), (0, 0)))

    big = [conv_w_in[0], conv_w_out[0], w_qkv[0], w_o[0], w_up[0], w_up[1], w_down[0], w_down[1]]
    stacks = [lax.dynamic_update_slice(lax.empty((N_SHARD,) + w.shape, BF), w.astype(BF)[None], (shard, 0, 0))
              for w in big]
    groups = ((0,), (1,), (4, 6), (2, 3), (5, 7))
    gather_sems, stacks, gather_token = _gather_start(stacks, groups, after=(gathered,))

    def gathered_group(g, name, after):
        return _gather_wait(name, [stacks[t] for t in groups[g]], gather_sems[g], after)

    bucket = _bucket_table()
    bias = _bias_table(rel_bias, bucket)

    h0 = _rms_fwd("conv_norm", x2d, conv_norm_g, deps=(gather_token,))
    (w_in_sm,) = gathered_group(0, "gather_wait_conv_in", (h0, dw_pad, bias))
    (u,) = _mm("conv_in", h0, w_in_sm, nt=False, b_sm=True, tm=2048, tn=512, rows=256, ep_in=((conv_b_in, "row"),),
               ep_fn=lambda acc, b: (acc + b,), outs=(("tile", BF),))
    cv, s_act = _conv_fwd(u, dw_pad, conv_dw_b, conv_ln_g, conv_ln_b)
    (g_out,) = gathered_group(1, "gather_wait_conv_out", (s_act,))
    w_out_f = g_out.reshape(Dm, Dm)
    x1, h1 = _mm("conv_out", s_act, w_out_f, nt=False, tm=1024, tn=1024, rows=256,
                 ep_in=((conv_b_out, "row"), (x2d, "tile"), (mlp_norm_g[0:1], "row")), ep_fn=_residual_norm_ep,
                 outs=RESIDUAL_NORM_OUTS)

    g_up0, g_down0 = gathered_group(2, "gather_wait_mlp0", (x1,))
    w_up_sm = [g_up0, None]
    w_down_f = [g_down0.reshape(D_FF, Dm), None]
    up0 = _mlp_up(0, h1, w_up_sm[0])
    x2, h2 = _mm("mlp0_down", up0, w_down_f[0], nt=False, tm=512, tn=1024, rows=256, a_fn=_relu2,
                 ep_in=((x1, "tile"), (attn_norm_f, "row")), ep_fn=_residual_norm_ep, outs=RESIDUAL_NORM_OUTS)

    g_qkv, g_o = gathered_group(3, "gather_wait_attn", (x2,))
    w_qkv_f = jnp.transpose(g_qkv, (1, 0, 2)).reshape(Dm, QKV_DIM)
    w_o_f = g_o.reshape(ATTN_DIM, Dm)
    (qkv,) = _mm("attn_qkv", h2, w_qkv_f, nt=False, tm=1024, tn=QKV_DIM, rows=256, ep_in=((b_qkv_f, "row"),),
                 ep_fn=lambda acc, b: (acc + b,), outs=(("tile", F32),))
    qg_t = jnp.tile(q_norm_g, (1, N_HEADS))
    kg_t = jnp.tile(k_norm_g, (1, N_KV))
    qn, kn, vv = _qk_norm_fwd(qkv, qg_t, kg_t)
    sinks1 = sinks[0]
    att = _attn_fwd(qn, kn, vv, bias, sinks1)
    x3, h3 = _mm("attn_out", att, w_o_f, nt=False, tm=1024, tn=1024, rows=256,
                 ep_in=((b_o_f, "row"), (x2, "tile"), (mlp_norm_g[1:2], "row")), ep_fn=_residual_norm_ep,
                 outs=RESIDUAL_NORM_OUTS)

    g_up1, g_down1 = gathered_group(4, "gather_wait_mlp1", (x3,))
    w_up_sm[1] = g_up1
    w_down_f[1] = g_down1.reshape(D_FF, Dm)
    up1 = _mlp_up(1, h3, w_up_sm[1])

    def loss_ep(acc, r, t):
        diff = acc + r - t
        dy = diff * (1.0 / Dm)
        return dy, dy, jnp.sum(diff * diff, axis=0, keepdims=True)

    dy, dy_bf, sq = _mm("mlp1_down_loss", up1, w_down_f[1], nt=False, tm=512, tn=1024, rows=256, a_fn=_relu2,
                        ep_in=((x3, "tile"), (tgt, "tile")), ep_fn=loss_ep,
                        outs=(("tile", F32), ("tile", BF), ("colsum", F32)))

    place = jnp.stack([shard, lax.axis_index("c")]).astype(jnp.int32)
    dx3, dx3_bf, dg_mlp1, db_o, dup1 = _mlp_bwd(1, dy, dy_bf, x3, mlp_norm_g[1:2], up1, w_up_sm[1], w_down_f[1])
    dw_down1 = _mm_tn("mlp1_dw_down", up1, dy_bf, tm=1024, tn=1024, tk=2048, a_fn=_relu2)
    dw_up1 = _mm_tn("mlp1_dw_up", h3, dup1, tm=1024, tn=1024, tk=2048, out_sm=N_SHARD)
    red_mlp1 = _Reduction("mlp1", [dw_up1, dw_down1.reshape(N_SHARD, D_FF // N_SHARD, Dm)], place)

    ident = lambda acc: (acc,)
    (datt,) = _mm("attn_dout", dx3_bf, w_o_f, nt=True, tm=1024, tn=1024, rows=256, ep_fn=ident, outs=(("tile", BF),),
                  deps=(red_mlp1.token,))
    dw_o = _mm_tn("attn_dw_o", att, dx3_bf, tm=1024, tn=1024, tk=2048)
    dqn, dkn, dvv, dbias, dsinks = _attn_bwd(qn, kn, vv, bias, sinks1, datt)
    drel = _bias_grad(dbias, bucket)
    dqkv, db_qkv, dqg_t, dkg_t = _qk_norm_bwd(qkv, dqn, dkn, dvv, qg_t, kg_t)
    dw_qkv = _mm_tn("attn_dw_qkv", h2, dqkv, tm=1024, tn=QKV_DIM, tk=2048)
    red_attn = _Reduction("attn", [jnp.transpose(dw_qkv.reshape(Dm, N_SHARD, QKV_DIM // N_SHARD), (1, 0, 2)),
                                   dw_o.reshape(N_SHARD, ATTN_DIM // N_SHARD, Dm)], place)
    dx2, dx2_bf, dg_attn, _ = _mm("attn_dx", dqkv, w_qkv_f, nt=True, tm=512, tn=1024, rows=256,
                                  ep_in=((x2, "tile"), (attn_norm_f, "row"), (dx3, "tile")), ep_fn=_rms_bwd_ep,
                                  outs=RMS_BWD_OUTS, deps=(red_attn.token,))

    dx1, dx1_bf, dg_mlp0, db_out, dup0 = _mlp_bwd(0, dx2, dx2_bf, x1, mlp_norm_g[0:1], up0, w_up_sm[0], w_down_f[0])
    dw_down0 = _mm_tn("mlp0_dw_down", up0, dx2_bf, tm=1024, tn=1024, tk=2048, a_fn=_relu2)
    dw_up0 = _mm_tn("mlp0_dw_up", h1, dup0, tm=1024, tn=1024, tk=2048, out_sm=N_SHARD)
    red_mlp0 = _Reduction("mlp0", [dw_up0, dw_down0.reshape(N_SHARD, D_FF // N_SHARD, Dm)], place)
    (r_qkv, r_o) = red_attn.finish((dx1,))
    (r_up1, r_down1) = red_mlp1.finish((dx1,))

    dcv, dln_g, dln_b, ddw_b = _mm("conv_ds", dx1_bf, w_out_f, nt=True, tm=512, tn=1024, rows=256,
                                   ep_in=((cv, "tile"), (conv_ln_g, "row"), (conv_ln_b, "row")),
                                   ep_fn=_ln_silu_bwd_ep,
                                   outs=(("tile", F32), ("colsum", F32), ("colsum", F32), ("colsum", F32)),
                                   deps=(red_mlp0.token,))
    dw_out = _mm_tn("conv_dw_out", s_act, dx1_bf, tm=1024, tn=1024, tk=2048)
    du, db_in, ddw8 = _conv_bwd(u, dcv, dw_pad)
    (r_up0, r_down0) = red_mlp0.finish((du,))
    dw_in = _mm_tn("conv_dw_in", h0, du, tm=1024, tn=512, tk=2048, out_sm=N_SHARD)
    red_conv = _Reduction("conv", [dw_in, dw_out.reshape(N_SHARD, Dm // N_SHARD, Dm)], place)
    def first_layer_ep(*args):
        tot, _, dg, _ = _rms_bwd_ep(*args)
        return tot, dg

    gx, dg_conv = _mm("conv_dx", du, w_in_sm, nt=True, b_sm=True, tm=512, tn=1024, rows=256,
                      ep_in=((x2d, "tile"), (conv_norm_g, "row"), (dx1, "tile")), ep_fn=first_layer_ep,
                      outs=(("tile", F32), ("colsum", F32)), deps=(red_conv.token,))
    (r_in, r_out) = red_conv.finish((gx,))
    mine = [r_in, r_out, r_qkv, r_o, r_up0, r_up1, r_down0, r_down1]
    r_in, r_out, r_qkv, r_o, r_up0, r_up1, r_down0, r_down1 = zip(mine, _join_halves("join_halves", mine))

    big_out = {}
    for nm, w, m, v, gs in (("conv_w_in", conv_w_in, m_conv_w_in, v_conv_w_in, (r_in,)),
                            ("conv_w_out", conv_w_out, m_conv_w_out, v_conv_w_out, (r_out,)),
                            ("w_qkv", w_qkv, m_w_qkv, v_w_qkv, (r_qkv,)),
                            ("w_o", w_o, m_w_o, v_w_o, (r_o,)),
                            ("w_up", w_up, m_w_up, v_w_up, (r_up0, r_up1)),
                            ("w_down", w_down, m_w_down, v_w_down, (r_down0, r_down1))):
        big_out[nm] = _adamw(f"adamw_{nm}", w, m, v, gs)

    dqg = dqg_t.reshape(N_HEADS, HEAD_DIM).sum(axis=0, keepdims=True)
    dkg = dkg_t.reshape(N_KV, HEAD_DIM).sum(axis=0, keepdims=True)
    small_full = [dg_conv, db_in, ddw8.sum(axis=1)[:CONV_W], ddw_b, dln_g, dln_b, db_out, dg_attn, db_qkv, dqg, dkg,
                  dsinks[None, :], db_o, drel.reshape(1, REL_BUCKETS * N_HEADS),
                  jnp.pad(dg_mlp0, ((0, 1), (0, 0))) + jnp.pad(dg_mlp1, ((1, 0), (0, 0))), sq]
    _, small_sum = _gather8("reduce_small_grads", _pack(small_full), with_sum=True)
    (r_norm, r_b_in, r_dw, r_dw_b, r_ln_g, r_ln_b, r_b_out, r_attn_norm, r_b_qkv, r_qg, r_kg, r_sinks, r_b_o, r_rel,
     r_mlp_norm, r_sq) = _unpack(small_sum, [a.shape for a in small_full])
    loss = 0.5 * jnp.sum(r_sq) * (1.0 / Dm)

    def cols(a, width):
        return lax.dynamic_slice_in_dim(a, shard * width, width, axis=a.ndim - 1)

    small_names = ["conv_norm_g", "conv_b_in", "conv_dw", "conv_dw_b", "conv_ln_g", "conv_ln_b", "conv_b_out",
                   "attn_norm_g", "b_qkv", "q_norm_g", "k_norm_g", "sinks", "b_o", "rel_bias", "mlp_norm_g"]
    small_g = [r_norm, r_b_in, cols(r_dw, Dm // N_SHARD)[None], r_dw_b, r_ln_g, r_ln_b, r_b_out,
               cols(r_attn_norm, Dm // N_SHARD), cols(r_b_qkv, QKV_DIM // N_SHARD), r_qg, r_kg, r_sinks,
               cols(r_b_o, Dm // N_SHARD), r_rel.reshape(REL_BUCKETS, N_HEADS), r_mlp_norm]
    small_w = [conv_norm_g, conv_b_in, conv_dw, conv_dw_b, conv_ln_g, conv_ln_b, conv_b_out, attn_norm_g, b_qkv,
               q_norm_g, k_norm_g, sinks, b_o, rel_bias, mlp_norm_g]
    small_m = [m_conv_norm_g, m_conv_b_in, m_conv_dw, m_conv_dw_b, m_conv_ln_g, m_conv_ln_b, m_conv_b_out,
               m_attn_norm_g, m_b_qkv, m_q_norm_g, m_k_norm_g, m_sinks, m_b_o, m_rel_bias, m_mlp_norm_g]
    small_v = [v_conv_norm_g, v_conv_b_in, v_conv_dw, v_conv_dw_b, v_conv_ln_g, v_conv_ln_b, v_conv_b_out,
               v_attn_norm_g, v_b_qkv, v_q_norm_g, v_k_norm_g, v_sinks, v_b_o, v_rel_bias, v_mlp_norm_g]
    flat2 = lambda a: a.reshape(-1, a.shape[-1])
    shapes2 = [flat2(w).shape for w in small_w]
    pk = lambda arrs: _pack([flat2(a) for a in arrs])
    packed_g = pk(small_g)
    d_s, m_s, v_s = _adamw_small(pk(small_w), packed_g, pk(small_m), pk(small_v))
    small_out = {}
    for nm, w, g, d, m2, v2 in zip(small_names, small_w, _unpack(packed_g, shapes2), _unpack(d_s, shapes2),
                                   _unpack(m_s, shapes2), _unpack(v_s, shapes2)):
        small_out[nm] = tuple(a.reshape(w.shape) for a in (g, d, m2, v2))

    order = ["conv_norm_g", "conv_w_in", "conv_b_in", "conv_dw", "conv_dw_b", "conv_ln_g", "conv_ln_b", "conv_w_out",
             "conv_b_out", "attn_norm_g", "w_qkv", "b_qkv", "q_norm_g", "k_norm_g", "sinks", "w_o", "b_o", "rel_bias",
             "mlp_norm_g", "w_up", "w_down"]
    res = {**small_out, **big_out}
    outs = [loss, gx[None]]
    for part in range(4):
        outs += [res[nm][part] for nm in order]
    return tuple(outs)
```

```python
import math

import numpy as np
import jax
import jax.numpy as jnp
from jax import lax
from jax.experimental import pallas as pl
from jax.experimental.pallas import tpu as pltpu

F32 = jnp.float32
BF = jnp.bfloat16
MESH = pl.DeviceIdType.MESH

D_MODEL = 1024
D_FF = 4096
N_HEADS = 16
N_KV = 2
GROUP = N_HEADS // N_KV
HEAD_DIM = 64
ATTN_DIM = N_HEADS * HEAD_DIM
KV_DIM = N_KV * HEAD_DIM
QKV_DIM = ATTN_DIM + 2 * KV_DIM
BLOCK = 128
CONV_W = 31
HALO = 32
REL_BUCKETS = 32
REL_MAX_DIST = 128
NORM_EPS = 1e-6
NEG_INF = -1e30
N_SHARD = 4
LANES = 1024

ADAM_LR = 0.001
ADAM_B1 = 0.9
ADAM_B2 = 0.999
ADAM_EPS = 1e-08
ADAM_WD = 0.01
ADAM_STEP = 10

VMEM_LIMIT = 56 * 1024 * 1024


def _params(n_axes):
    return pltpu.CompilerParams(dimension_semantics=("arbitrary",) * n_axes, vmem_limit_bytes=VMEM_LIMIT)


def _dot(a, b, ca, cb):
    return lax.dot_general(a, b, (((ca,), (cb,)), ((), ())), preferred_element_type=F32)


def _mm(name, a, b, *, nt, tm, tn, ep_fn, outs, a_fn=None, b_sm=False, ep_in=(), deps=(), rows=None):
    M, K = a.shape
    rows = tm if rows is None else rows
    if b_sm:
        S, ks = b.shape[0], b.shape[2]
        N, per = (b.shape[1], None) if nt else (S * b.shape[2], b.shape[2] // tn)
        assert (S * ks == K) if nt else (b.shape[1] == K)
    else:
        N = b.shape[0] if nt else b.shape[1]
        assert (b.shape[1] if nt else b.shape[0]) == K
    assert M % tm == 0 and N % tn == 0 and tm % rows == 0
    ne, no, nd = len(ep_in), len(outs), len(deps)

    def body(a_ref, b_ref, *rest):
        ep_refs, out_refs = rest[:ne], rest[ne + nd:ne + nd + no]
        i = pl.program_id(1)
        sums = [None] * no
        for r in range(tm // rows):
            rs = pl.ds(r * rows, rows)

            def lhs(cols):
                av = a_ref[rs, cols]
                return (av if a_fn is None else a_fn(av)).astype(BF)

            if b_sm and nt:
                acc = None
                for s in range(S):
                    part = _dot(lhs(pl.ds(s * ks, ks)), b_ref[s].astype(BF), 1, 1)
                    acc = part if acc is None else acc + part
            else:
                acc = _dot(lhs(slice(None)), b_ref[...].astype(BF), 1, 1 if nt else 0)
            ep_vals = [ref[rs, :] if kind == "tile" else ref[...] for ref, (_, kind) in zip(ep_refs, ep_in)]
            vals = ep_fn(acc, *ep_vals)
            for o, ((kind, dt), ref, val) in enumerate(zip(outs, out_refs, vals)):
                if kind == "tile":
                    ref[rs, :] = val.astype(dt)
                else:
                    sums[o] = val if sums[o] is None else sums[o] + val
        for (kind, dt), ref, val in zip(outs, out_refs, sums):
            if kind == "colsum":
                @pl.when(i == 0)
                def _():
                    ref[...] = val

                @pl.when(i > 0)
                def _():
                    ref[...] += val

    if b_sm and nt:
        b_spec = pl.BlockSpec((S, tn, ks), lambda j, i: (0, j, 0))
    elif b_sm:
        b_spec = pl.BlockSpec((None, K, tn), lambda j, i: (j // per, 0, j % per))
    elif nt:
        b_spec = pl.BlockSpec((tn, K), lambda j, i: (j, 0))
    else:
        b_spec = pl.BlockSpec((K, tn), lambda j, i: (0, j))
    in_specs = [pl.BlockSpec((tm, K), lambda j, i: (i, 0)), b_spec]
    for arr, kind in ep_in:
        if kind == "tile":
            assert arr.shape == (M, N)
            in_specs.append(pl.BlockSpec((tm, tn), lambda j, i: (i, j)))
        else:
            assert arr.shape == (1, N)
            in_specs.append(pl.BlockSpec((1, tn), lambda j, i: (0, j)))
    in_specs += [pl.BlockSpec(memory_space=pl.ANY)] * nd
    out_shape, out_specs = [], []
    for kind, dt in outs:
        if kind == "tile":
            out_shape.append(jax.ShapeDtypeStruct((M, N), dt))
            out_specs.append(pl.BlockSpec((tm, tn), lambda j, i: (i, j)))
        else:
            out_shape.append(jax.ShapeDtypeStruct((1, N), F32))
            out_specs.append(pl.BlockSpec((1, tn), lambda j, i: (0, j)))
    return pl.pallas_call(
        body, name=name, grid=(N // tn, M // tm), in_specs=in_specs, out_specs=out_specs, out_shape=out_shape,
        compiler_params=_params(2),
    )(a, b, *[arr for arr, _ in ep_in], *deps)


def _mm_tn(name, a, b, *, tm, tn, tk, a_fn=None, out_sm=None):
    T, Ka = a.shape
    N = b.shape[1]
    assert b.shape[0] == T and T % tk == 0 and Ka % tm == 0 and N % tn == 0
    nk = T // tk

    def body(a_ref, b_ref, o_ref, acc_ref):
        k = pl.program_id(2)

        @pl.when(k == 0)
        def _():
            acc_ref[...] = jnp.zeros_like(acc_ref)

        av = a_ref[...]
        if a_fn is not None:
            av = a_fn(av)
        acc_ref[...] += _dot(av.astype(BF), b_ref[...].astype(BF), 0, 0)

        @pl.when(k == nk - 1)
        def _():
            o_ref[...] = acc_ref[...].astype(BF)

    if out_sm is None:
        out_shape = jax.ShapeDtypeStruct((Ka, N), BF)
        out_spec = pl.BlockSpec((tm, tn), lambda i, j, k: (i, j))
    else:
        per = (N // out_sm) // tn
        assert per * tn * out_sm == N
        out_shape = jax.ShapeDtypeStruct((out_sm, Ka, N // out_sm), BF)
        out_spec = pl.BlockSpec((None, tm, tn), lambda i, j, k: (j // per, i, j % per))
    return pl.pallas_call(
        body, name=name, grid=(Ka // tm, N // tn, nk),
        in_specs=[pl.BlockSpec((tk, tm), lambda i, j, k: (k, i)), pl.BlockSpec((tk, tn), lambda i, j, k: (k, j))],
        out_specs=out_spec, out_shape=out_shape, scratch_shapes=[pltpu.VMEM((tm, tn), F32)],
        compiler_params=_params(3),
    )(a, b)


def _relu2(v):
    r = jnp.maximum(v.astype(F32), 0.0)
    return r * r


def _rms_bwd_ep(dh, x, g, dres):
    rstd = lax.rsqrt(jnp.mean(x * x, axis=-1, keepdims=True) + NORM_EPS)
    xh = x * rstd
    dxh = dh * g
    dx = rstd * (dxh - xh * jnp.mean(dxh * xh, axis=-1, keepdims=True))
    tot = dres + dx
    return tot, tot, jnp.sum(dh * xh, axis=0, keepdims=True), jnp.sum(tot, axis=0, keepdims=True)


def _rms_fwd(name, x, g, tm=512, deps=()):
    T, Dm = x.shape

    def body(x_ref, g_ref, *rest):
        o_ref = rest[-1]
        xv = x_ref[...]
        rstd = lax.rsqrt(jnp.mean(xv * xv, axis=-1, keepdims=True) + NORM_EPS)
        o_ref[...] = (xv * rstd * g_ref[...]).astype(BF)

    return pl.pallas_call(
        body, name=name, grid=(T // tm,),
        in_specs=[pl.BlockSpec((tm, Dm), lambda i: (i, 0)), pl.BlockSpec((1, Dm), lambda i: (0, 0))]
        + [pl.BlockSpec(memory_space=pl.ANY)] * len(deps),
        out_specs=pl.BlockSpec((tm, Dm), lambda i: (i, 0)), out_shape=jax.ShapeDtypeStruct((T, Dm), BF),
        compiler_params=_params(1),
    )(x, g, *deps)


HEAD_COLS = 128


def _two_term_dot(v, m):
    hi = v.astype(BF)
    lo = (v - hi.astype(F32)).astype(BF)
    return _dot(hi, m, 1, 0) + _dot(lo, m, 1, 0)


def _head_sum(v, select):
    sel, sel_t = select
    return _two_term_dot(_two_term_dot(v, sel), sel_t)


def _head_select(n):
    sel = (np.arange(n)[:, None] // HEAD_DIM == np.arange(HEAD_COLS)[None, :]).astype(np.float32)
    return jnp.asarray(sel, dtype=BF), jnp.asarray(sel.T, dtype=BF)


def _qk_norm_fwd(qkv, qg_t, kg_t, tm=256):
    T = qkv.shape[0]
    scale = 1.0 / math.sqrt(HEAD_DIM)

    def body(x_ref, qg_ref, kg_ref, sq_ref, sqt_ref, sk_ref, skt_ref, q_ref, k_ref, v_ref):
        q = x_ref[:, pl.ds(0, ATTN_DIM)]
        rq = lax.rsqrt(_head_sum(q * q, (sq_ref[...], sqt_ref[...])) * (1.0 / HEAD_DIM) + NORM_EPS)
        q_ref[...] = (q * rq * qg_ref[...] * scale).astype(BF)
        k = x_ref[:, pl.ds(ATTN_DIM, KV_DIM)]
        rk = lax.rsqrt(_head_sum(k * k, (sk_ref[...], skt_ref[...])) * (1.0 / HEAD_DIM) + NORM_EPS)
        k_ref[...] = (k * rk * kg_ref[...]).astype(BF)
        v_ref[...] = x_ref[:, pl.ds(ATTN_DIM + KV_DIM, KV_DIM)].astype(BF)

    full = lambda shape: pl.BlockSpec(shape, lambda i: (0, 0))
    return pl.pallas_call(
        body, name="qk_norm_fwd", grid=(T // tm,),
        in_specs=[pl.BlockSpec((tm, QKV_DIM), lambda i: (i, 0)), full((1, ATTN_DIM)), full((1, KV_DIM)),
                  full((ATTN_DIM, HEAD_COLS)), full((HEAD_COLS, ATTN_DIM)), full((KV_DIM, HEAD_COLS)), full((HEAD_COLS, KV_DIM))],
        out_specs=[pl.BlockSpec((tm, ATTN_DIM), lambda i: (i, 0)), pl.BlockSpec((tm, KV_DIM), lambda i: (i, 0)),
                   pl.BlockSpec((tm, KV_DIM), lambda i: (i, 0))],
        out_shape=[jax.ShapeDtypeStruct((T, ATTN_DIM), BF), jax.ShapeDtypeStruct((T, KV_DIM), BF),
                   jax.ShapeDtypeStruct((T, KV_DIM), BF)],
        compiler_params=_params(1),
    )(qkv, qg_t, kg_t, *_head_select(ATTN_DIM), *_head_select(KV_DIM))


def _qk_norm_bwd(qkv, dqn, dkn, dv, qg_t, kg_t, tm=256):
    T = qkv.shape[0]

    def body(x_ref, dq_ref, dk_ref, dv_ref, qg_ref, kg_ref, sq_ref, sqt_ref, sk_ref, skt_ref,
             o_ref, db_ref, dqg_ref, dkg_ref):
        i = pl.program_id(0)

        def one(x, dy, g, select):
            r = lax.rsqrt(_head_sum(x * x, select) * (1.0 / HEAD_DIM) + NORM_EPS)
            xh = x * r
            dxh = dy * g
            dx = r * (dxh - xh * (_head_sum(dxh * xh, select) * (1.0 / HEAD_DIM)))
            return dx, jnp.sum(dy * xh, axis=0, keepdims=True)

        dq, dqg = one(x_ref[:, pl.ds(0, ATTN_DIM)], dq_ref[...], qg_ref[...], (sq_ref[...], sqt_ref[...]))
        dk, dkg = one(x_ref[:, pl.ds(ATTN_DIM, KV_DIM)], dk_ref[...], kg_ref[...], (sk_ref[...], skt_ref[...]))
        dvv = dv_ref[...]
        o_ref[:, pl.ds(0, ATTN_DIM)] = dq.astype(BF)
        o_ref[:, pl.ds(ATTN_DIM, KV_DIM)] = dk.astype(BF)
        o_ref[:, pl.ds(ATTN_DIM + KV_DIM, KV_DIM)] = dvv.astype(BF)
        sq, sk, sv = (jnp.sum(t, axis=0, keepdims=True) for t in (dq, dk, dvv))

        @pl.when(i == 0)
        def _():
            db_ref[:, pl.ds(0, ATTN_DIM)] = sq
            db_ref[:, pl.ds(ATTN_DIM, KV_DIM)] = sk
            db_ref[:, pl.ds(ATTN_DIM + KV_DIM, KV_DIM)] = sv
            dqg_ref[...] = dqg
            dkg_ref[...] = dkg

        @pl.when(i > 0)
        def _():
            db_ref[:, pl.ds(0, ATTN_DIM)] += sq
            db_ref[:, pl.ds(ATTN_DIM, KV_DIM)] += sk
            db_ref[:, pl.ds(ATTN_DIM + KV_DIM, KV_DIM)] += sv
            dqg_ref[...] += dqg
            dkg_ref[...] += dkg

    full = lambda shape: pl.BlockSpec(shape, lambda i: (0, 0))
    row = lambda n: pl.BlockSpec((tm, n), lambda i: (i, 0))
    return pl.pallas_call(
        body, name="qk_norm_bwd", grid=(T // tm,),
        in_specs=[row(QKV_DIM), row(ATTN_DIM), row(KV_DIM), row(KV_DIM), full((1, ATTN_DIM)), full((1, KV_DIM)),
                  full((ATTN_DIM, HEAD_COLS)), full((HEAD_COLS, ATTN_DIM)), full((KV_DIM, HEAD_COLS)), full((HEAD_COLS, KV_DIM))],
        out_specs=[row(QKV_DIM), full((1, QKV_DIM)), full((1, ATTN_DIM)), full((1, KV_DIM))],
        out_shape=[jax.ShapeDtypeStruct((T, QKV_DIM), BF), jax.ShapeDtypeStruct((1, QKV_DIM), F32),
                   jax.ShapeDtypeStruct((1, ATTN_DIM), F32), jax.ShapeDtypeStruct((1, KV_DIM), F32)],
        compiler_params=_params(1),
    )(qkv, dqn, dkn, dv, qg_t, kg_t, *_head_select(ATTN_DIM), *_head_select(KV_DIM))


ROWS = 64
COLS = 128


SUBLANES = 8
FIRST_TAP = HALO - (CONV_W - 1)


def _glu(a, g):
    return a.astype(F32) * jax.nn.sigmoid(g.astype(F32))


def _shifted(xe, s):
    return xe if s == 0 else pltpu.roll(xe, ROWS + HALO - s, axis=0)


def _conv_fwd(u, dw_pad, dw_b, ln_g, ln_b, tm=256):
    T = u.shape[0]
    Dm = D_MODEL
    hpt = tm // HALO

    def body(ac_ref, gc_ref, ap_ref, gp_ref, w_ref, wb_ref, lg_ref, lb_ref, cv_ref, s_ref, ext):
        i = pl.program_id(0)
        ext[pl.ds(0, HALO), :] = jnp.where(i > 0, _glu(ap_ref[...], gp_ref[...]), 0.0)
        ext[pl.ds(HALO, tm), :] = _glu(ac_ref[...], gc_ref[...])

        def rows(r, carry):
            r0 = pl.multiple_of(r * ROWS, ROWS)
            for c in range(Dm // COLS):
                cs = pl.ds(c * COLS, COLS)
                xe = ext[pl.ds(r0, ROWS + HALO), cs]
                acc = jnp.zeros((ROWS, COLS), F32)
                for s in range(SUBLANES):
                    xs = _shifted(xe, s)
                    for j in range(CONV_W):
                        off = FIRST_TAP + j
                        if off % SUBLANES == s:
                            acc = acc + xs[off - s:off - s + ROWS, :] * w_ref[pl.ds(j, 1), cs]
                cv_ref[pl.ds(r0, ROWS), cs] = acc + wb_ref[:, cs]
            return carry

        lax.fori_loop(0, tm // ROWS, rows, 0)
        cv = cv_ref[...]
        xc = cv - jnp.mean(cv, axis=-1, keepdims=True)
        y = xc * lax.rsqrt(jnp.mean(xc * xc, axis=-1, keepdims=True) + NORM_EPS) * lg_ref[...] + lb_ref[...]
        s_ref[...] = (y * jax.nn.sigmoid(y)).astype(BF)

    full = lambda shape: pl.BlockSpec(shape, lambda i: (0, 0))
    return pl.pallas_call(
        body, name="conv_fwd", grid=(T // tm,),
        in_specs=[pl.BlockSpec((tm, Dm), lambda i: (i, 0)), pl.BlockSpec((tm, Dm), lambda i: (i, 1)),
                  pl.BlockSpec((HALO, Dm), lambda i: (jnp.maximum(i * hpt - 1, 0), 0)),
                  pl.BlockSpec((HALO, Dm), lambda i: (jnp.maximum(i * hpt - 1, 0), 1)),
                  full((HALO, Dm)), full((1, Dm)), full((1, Dm)), full((1, Dm))],
        out_specs=[pl.BlockSpec((tm, Dm), lambda i: (i, 0)), pl.BlockSpec((tm, Dm), lambda i: (i, 0))],
        out_shape=[jax.ShapeDtypeStruct((T, Dm), F32), jax.ShapeDtypeStruct((T, Dm), BF)],
        scratch_shapes=[pltpu.VMEM((tm + HALO, Dm), F32)],
        compiler_params=_params(1),
    )(u, u, u, u, dw_pad, dw_b, ln_g, ln_b)


def _ln_silu_bwd_ep(ds, cv, lg, lb):
    xc = cv - jnp.mean(cv, axis=-1, keepdims=True)
    rstd = lax.rsqrt(jnp.mean(xc * xc, axis=-1, keepdims=True) + NORM_EPS)
    xh = xc * rstd
    y = xh * lg + lb
    sg = jax.nn.sigmoid(y)
    dy = ds * (sg * (1.0 + y * (1.0 - sg)))
    dxh = dy * lg
    dcv = rstd * (dxh - jnp.mean(dxh, axis=-1, keepdims=True) - xh * jnp.mean(dxh * xh, axis=-1, keepdims=True))
    return (dcv, jnp.sum(dy * xh, axis=0, keepdims=True), jnp.sum(dy, axis=0, keepdims=True),
            jnp.sum(dcv, axis=0, keepdims=True))


def _conv_bwd(u, dcv, dw_pad, tm=256):
    T = u.shape[0]
    Dm = D_MODEL
    hpt = tm // HALO
    last = T // HALO - 1
    nt = T // tm

    def body(ac_ref, gc_ref, ap_ref, gp_ref, dc_ref, dn_ref, w_ref, du_ref, db_ref, dw_ref, ext_g, ext_d):
        i = pl.program_id(0)
        ext_g[pl.ds(0, HALO), :] = jnp.where(i > 0, _glu(ap_ref[...], gp_ref[...]), 0.0)
        ext_g[pl.ds(HALO, tm), :] = _glu(ac_ref[...], gc_ref[...])
        ext_d[pl.ds(0, tm), :] = dc_ref[...]
        ext_d[pl.ds(tm, HALO), :] = jnp.where(i < nt - 1, dn_ref[...], 0.0)

        @pl.when(i == 0)
        def _():
            db_ref[...] = jnp.zeros_like(db_ref)
            dw_ref[...] = jnp.zeros_like(dw_ref)

        def rows(r, carry):
            r0 = pl.multiple_of(r * ROWS, ROWS)
            rs = pl.ds(r0, ROWS)
            for c in range(Dm // COLS):
                cs = pl.ds(c * COLS, COLS)
                cs2 = pl.ds(Dm + c * COLS, COLS)
                de = ext_d[pl.ds(r0, ROWS + HALO), cs]
                ge = ext_g[pl.ds(r0, ROWS + HALO), cs]
                dcur = de[0:ROWS, :]
                acc = jnp.zeros((ROWS, COLS), F32)
                for s in range(SUBLANES):
                    ds_, gs_ = _shifted(de, s), _shifted(ge, s)
                    for j in range(CONV_W):
                        off = CONV_W - 1 - j
                        if off % SUBLANES == s:
                            acc = acc + ds_[off - s:off - s + ROWS, :] * w_ref[pl.ds(j, 1), cs]
                        goff = FIRST_TAP + j
                        if goff % SUBLANES == s:
                            prod = dcur * gs_[goff - s:goff - s + ROWS, :]
                            dw_ref[j, :, cs] += jnp.sum(prod.reshape(ROWS // SUBLANES, SUBLANES, COLS), axis=0)
                a = ac_ref[rs, cs].astype(F32)
                sg = jax.nn.sigmoid(gc_ref[rs, cs].astype(F32))
                da = acc * sg
                dg = acc * a * sg * (1.0 - sg)
                du_ref[rs, cs] = da.astype(BF)
                du_ref[rs, cs2] = dg.astype(BF)
                db_ref[:, cs] += jnp.sum(da, axis=0, keepdims=True)
                db_ref[:, cs2] += jnp.sum(dg, axis=0, keepdims=True)
            return carry

        lax.fori_loop(0, tm // ROWS, rows, 0)

    return pl.pallas_call(
        body, name="conv_bwd", grid=(nt,),
        in_specs=[pl.BlockSpec((tm, Dm), lambda i: (i, 0)), pl.BlockSpec((tm, Dm), lambda i: (i, 1)),
                  pl.BlockSpec((HALO, Dm), lambda i: (jnp.maximum(i * hpt - 1, 0), 0)),
                  pl.BlockSpec((HALO, Dm), lambda i: (jnp.maximum(i * hpt - 1, 0), 1)),
                  pl.BlockSpec((tm, Dm), lambda i: (i, 0)),
                  pl.BlockSpec((HALO, Dm), lambda i: (jnp.minimum((i + 1) * hpt, last), 0)),
                  pl.BlockSpec((HALO, Dm), lambda i: (0, 0))],
        out_specs=[pl.BlockSpec((tm, 2 * Dm), lambda i: (i, 0)), pl.BlockSpec((1, 2 * Dm), lambda i: (0, 0)),
                   pl.BlockSpec((HALO, 8, Dm), lambda i: (0, 0, 0))],
        out_shape=[jax.ShapeDtypeStruct((T, 2 * Dm), BF), jax.ShapeDtypeStruct((1, 2 * Dm), F32),
                   jax.ShapeDtypeStruct((HALO, 8, Dm), F32)],
        scratch_shapes=[pltpu.VMEM((tm + HALO, Dm), F32), pltpu.VMEM((tm + HALO, Dm), F32)],
        compiler_params=_params(1),
    )(u, u, u, u, dcv, dcv, dw_pad)


def _bucket_table():
    q_loc = np.arange(BLOCK)[:, None]
    k_loc = np.arange(2 * BLOCK)[None, :]
    dist = q_loc + BLOCK - k_loc
    n = np.maximum(dist, 0)
    max_exact = REL_BUCKETS // 2
    large = max_exact + (np.log(np.maximum(n, 1).astype(np.float32) / max_exact)
                         / math.log(REL_MAX_DIST / max_exact) * (REL_BUCKETS - max_exact)).astype(np.int32)
    large = np.minimum(large, REL_BUCKETS - 1)
    bucket = np.where(n < max_exact, n, large).astype(np.int32)
    return jnp.asarray(np.where((dist >= 0) & (dist < BLOCK), bucket, -1).astype(np.int32))


def _bias_table(rel_bias, bucket):
    def body(rb_ref, bk_ref, o_ref):
        bk = bk_ref[...]
        for h in range(N_HEADS):
            acc = jnp.full((BLOCK, 2 * BLOCK), NEG_INF, F32)
            for b in range(REL_BUCKETS):
                acc = jnp.where(bk == b, rb_ref[b, h], acc)
            o_ref[h] = acc

    return pl.pallas_call(
        body, name="bias_table", out_shape=jax.ShapeDtypeStruct((N_HEADS, BLOCK, 2 * BLOCK), F32),
        in_specs=[pl.BlockSpec(memory_space=pltpu.SMEM), pl.BlockSpec(memory_space=pltpu.VMEM)],
        out_specs=pl.BlockSpec(memory_space=pltpu.VMEM),
    )(rel_bias, bucket)


def _bias_grad(dbias, bucket):
    def body(db_ref, bk_ref, o_ref):
        bk = bk_ref[...]
        for b in range(REL_BUCKETS):
            sel = bk == b
            for h in range(N_HEADS):
                o_ref[b, h] = jnp.sum(jnp.where(sel, db_ref[h], 0.0))

    return pl.pallas_call(
        body, name="bias_grad", out_shape=jax.ShapeDtypeStruct((REL_BUCKETS, N_HEADS), F32),
        in_specs=[pl.BlockSpec(memory_space=pltpu.VMEM), pl.BlockSpec(memory_space=pltpu.VMEM)],
        out_specs=pl.BlockSpec(memory_space=pltpu.SMEM),
    )(dbias, bucket)


GROUP_ROWS = GROUP * BLOCK


def _head_probs(qk, bias_h, sink, first):
    s = jnp.where(first, NEG_INF, qk + bias_h)
    m = jnp.maximum(jnp.max(s, axis=-1, keepdims=True), sink)
    p = jnp.exp(s - m)
    ps = jnp.exp(sink - m)
    inv = 1.0 / (jnp.sum(p, axis=-1, keepdims=True) + ps)
    return p * inv, ps * inv


def _band(prev_ref, cur_ref, g):
    hs = pl.ds(g * HEAD_DIM, HEAD_DIM)
    return jnp.concatenate([prev_ref[:, hs], cur_ref[:, hs]], axis=0)


def _stack_heads(ref, g):
    return jnp.concatenate([ref[:, pl.ds((g * GROUP + hh) * HEAD_DIM, HEAD_DIM)] for hh in range(GROUP)], axis=0)


def _unstack_heads(ref, g, stacked, dtype):
    for hh in range(GROUP):
        ref[:, pl.ds((g * GROUP + hh) * HEAD_DIM, HEAD_DIM)] = stacked[hh * BLOCK:(hh + 1) * BLOCK, :].astype(dtype)


def _first_mask(n):
    col = lax.broadcasted_iota(jnp.int32, (1, 2 * BLOCK), 1)
    return jnp.logical_and(n == 0, col < BLOCK)


def _head_rows(hh):
    return pl.ds(hh * BLOCK, BLOCK)


def _attn_fwd(qn, kn, vv, bias, sinks):
    T = qn.shape[0]
    nb = T // BLOCK

    def body(sk_ref, q_ref, kc_ref, kp_ref, vc_ref, vp_ref, b_ref, o_ref, qk_buf, p_buf):
        first = _first_mask(pl.program_id(0))
        for g in range(N_KV):
            k = _band(kp_ref, kc_ref, g)
            v = _band(vp_ref, vc_ref, g)
            qk_buf[g] = _dot(_stack_heads(q_ref, g), k, 1, 1)
            for hh in range(GROUP):
                h = g * GROUP + hh
                pn, _ = _head_probs(qk_buf[g, _head_rows(hh), :], b_ref[h], sk_ref[h], first)
                p_buf[g, _head_rows(hh), :] = pn.astype(BF)
            _unstack_heads(o_ref, g, _dot(p_buf[g], v, 1, 0), BF)

    cur = lambda n: (n, 0)
    prev = lambda n: (jnp.maximum(n - 1, 0), 0)
    return pl.pallas_call(
        body, name="attn_fwd", grid=(nb,),
        in_specs=[pl.BlockSpec(memory_space=pltpu.SMEM), pl.BlockSpec((BLOCK, ATTN_DIM), cur),
                  pl.BlockSpec((BLOCK, KV_DIM), cur), pl.BlockSpec((BLOCK, KV_DIM), prev),
                  pl.BlockSpec((BLOCK, KV_DIM), cur), pl.BlockSpec((BLOCK, KV_DIM), prev),
                  pl.BlockSpec((N_HEADS, BLOCK, 2 * BLOCK), lambda n: (0, 0, 0))],
        out_specs=pl.BlockSpec((BLOCK, ATTN_DIM), cur), out_shape=jax.ShapeDtypeStruct((T, ATTN_DIM), BF),
        scratch_shapes=[pltpu.VMEM((N_KV, GROUP_ROWS, 2 * BLOCK), F32), pltpu.VMEM((N_KV, GROUP_ROWS, 2 * BLOCK), BF)],
        compiler_params=_params(1),
    )(sinks, qn, kn, kn, vv, vv, bias)


def _attn_bwd(qn, kn, vv, bias, sinks, do):
    T = qn.shape[0]
    nb = T // BLOCK
    scale = 1.0 / math.sqrt(HEAD_DIM)

    def body(sk_ref, q_ref, kc_ref, kp_ref, vc_ref, vp_ref, b_ref, do_ref,
             dq_ref, dk_ref, dv_ref, db_ref, dsk_ref, dk_full, dv_full, dk_carry, dv_carry, qk_buf, dp_buf, p_buf, ds_buf):
        n = pl.program_id(0)

        @pl.when(n == 0)
        def _():
            db_ref[...] = jnp.zeros_like(db_ref)
            dk_carry[...] = jnp.zeros_like(dk_carry)
            dv_carry[...] = jnp.zeros_like(dv_carry)
            for h in range(N_HEADS):
                dsk_ref[h] = 0.0

        @pl.when(n < nb)
        def _():
            first = _first_mask(n)
            for g in range(N_KV):
                k = _band(kp_ref, kc_ref, g)
                v = _band(vp_ref, vc_ref, g)
                q = _stack_heads(q_ref, g)
                dout = _stack_heads(do_ref, g)
                qk_buf[g] = _dot(q, k, 1, 1)
                dp_buf[g] = _dot(dout, v, 1, 1)
                for hh in range(GROUP):
                    h = g * GROUP + hh
                    rows = _head_rows(hh)
                    pn, psink = _head_probs(qk_buf[g, rows, :], b_ref[h], sk_ref[h], first)
                    dp = dp_buf[g, rows, :]
                    delta = jnp.sum(pn * dp, axis=-1, keepdims=True)
                    ds = pn * (dp - delta)
                    dsk_ref[h] += -jnp.sum(psink * delta)
                    db_ref[h] += ds
                    ds_buf[g, rows, :] = ds.astype(BF)
                    p_buf[g, rows, :] = pn.astype(BF)
                dsb = ds_buf[g]
                _unstack_heads(dq_ref, g, _dot(dsb, k, 1, 0) * scale, F32)
                gs = pl.ds(g * HEAD_DIM, HEAD_DIM)
                dk_full[:, gs] = _dot(dsb, q, 0, 0)
                dv_full[:, gs] = _dot(p_buf[g], dout, 0, 0)

        @pl.when(n == nb)
        def _():
            dk_full[...] = jnp.zeros_like(dk_full)
            dv_full[...] = jnp.zeros_like(dv_full)

        dk_ref[...] = dk_carry[...] + dk_full[pl.ds(0, BLOCK), :]
        dv_ref[...] = dv_carry[...] + dv_full[pl.ds(0, BLOCK), :]
        dk_carry[...] = dk_full[pl.ds(BLOCK, BLOCK), :]
        dv_carry[...] = dv_full[pl.ds(BLOCK, BLOCK), :]

    cur = lambda n: (jnp.minimum(n, nb - 1), 0)
    prev = lambda n: (jnp.maximum(jnp.minimum(n, nb - 1) - 1, 0), 0)
    out_kv = lambda n: (jnp.maximum(n - 1, 0), 0)
    return pl.pallas_call(
        body, name="attn_bwd", grid=(nb + 1,),
        in_specs=[pl.BlockSpec(memory_space=pltpu.SMEM), pl.BlockSpec((BLOCK, ATTN_DIM), cur),
                  pl.BlockSpec((BLOCK, KV_DIM), cur), pl.BlockSpec((BLOCK, KV_DIM), prev),
                  pl.BlockSpec((BLOCK, KV_DIM), cur), pl.BlockSpec((BLOCK, KV_DIM), prev),
                  pl.BlockSpec((N_HEADS, BLOCK, 2 * BLOCK), lambda n: (0, 0, 0)),
                  pl.BlockSpec((BLOCK, ATTN_DIM), cur)],
        out_specs=[pl.BlockSpec((BLOCK, ATTN_DIM), cur), pl.BlockSpec((BLOCK, KV_DIM), out_kv),
                   pl.BlockSpec((BLOCK, KV_DIM), out_kv),
                   pl.BlockSpec((N_HEADS, BLOCK, 2 * BLOCK), lambda n: (0, 0, 0)),
                   pl.BlockSpec(memory_space=pltpu.SMEM)],
        out_shape=[jax.ShapeDtypeStruct((T, ATTN_DIM), F32), jax.ShapeDtypeStruct((T, KV_DIM), F32),
                   jax.ShapeDtypeStruct((T, KV_DIM), F32),
                   jax.ShapeDtypeStruct((N_HEADS, BLOCK, 2 * BLOCK), F32), jax.ShapeDtypeStruct((N_HEADS,), F32)],
        scratch_shapes=[pltpu.VMEM((2 * BLOCK, KV_DIM), F32), pltpu.VMEM((2 * BLOCK, KV_DIM), F32),
                        pltpu.VMEM((BLOCK, KV_DIM), F32), pltpu.VMEM((BLOCK, KV_DIM), F32),
                        pltpu.VMEM((N_KV, GROUP_ROWS, 2 * BLOCK), F32), pltpu.VMEM((N_KV, GROUP_ROWS, 2 * BLOCK), F32),
                        pltpu.VMEM((N_KV, GROUP_ROWS, 2 * BLOCK), BF), pltpu.VMEM((N_KV, GROUP_ROWS, 2 * BLOCK), BF)],
        compiler_params=_params(1),
    )(sinks, qn, kn, kn, vv, vv, bias, do)


def _coords():
    return lax.axis_index("x"), lax.axis_index("y"), lax.axis_index("c")


def _sum8(name, blocks):
    def body(b_ref, o_ref):
        tot = b_ref[0]
        for d in range(1, 8):
            tot = tot + b_ref[d]
        o_ref[...] = tot

    return pl.pallas_call(body, name=name, out_shape=jax.ShapeDtypeStruct(blocks.shape[1:], F32))(blocks)


HBM_SPEC = pl.BlockSpec(memory_space=pltpu.HBM)
SEM_SPEC = pl.BlockSpec(memory_space=pltpu.SEMAPHORE)
ANY_SPEC = pl.BlockSpec(memory_space=pl.ANY)
DATAFLOW = pltpu.SideEffectType.DATAFLOW_SIDE_EFFECTING


OTHER_CHIPS = (4, 2, 6)
ALL_OTHERS = (1, 2, 3, 4, 5, 6, 7)


def _slot(x, y, c, peers):
    return 2 * x + y if peers is OTHER_CHIPS else 4 * x + 2 * y + c


def _slot_copy(land, sems, idx, x, y, c, k, peers, arriving):
    send_sems, recv_sems = sems
    px, py, pc = x ^ (k >> 2), y ^ ((k >> 1) & 1), c ^ (k & 1)
    mine = _slot(x, y, c, peers)
    dst = _slot(px, py, pc, peers) if arriving else mine
    return pltpu.make_async_remote_copy(src_ref=land.at[mine], dst_ref=land.at[dst], send_sem=send_sems.at[idx],
                                        recv_sem=recv_sems.at[idx], device_id=(px, py, pc), device_id_type=MESH)


def _gather_start(name, stacks, groups, peers, after):
    n = len(stacks)
    ng = len(groups)
    np_ = len(peers)
    after = tuple(after)

    def body(*refs):
        lands = refs[:n]
        first = n + len(after)
        sems = [(refs[first + 2 * g], refs[first + 2 * g + 1]) for g in range(ng)]
        token = refs[-1]
        x, y, c = _coords()
        for g, members in enumerate(groups):
            for i, t in enumerate(members):
                for j, k in enumerate(peers):
                    _slot_copy(lands[t], sems[g], np_ * i + j, x, y, c, k, peers, arriving=False).start()
        token[...] = jnp.zeros_like(token)

    out_shape = []
    for members in groups:
        out_shape += [pltpu.SemaphoreType.DMA((np_ * len(members),))] * 2
    out_shape += [pltpu.HBM(w.shape, w.dtype) for w in stacks]
    out_shape.append(jax.ShapeDtypeStruct((8, 128), F32))
    res = pl.pallas_call(
        body, name=name, out_shape=out_shape, in_specs=[HBM_SPEC] * n + [ANY_SPEC] * len(after),
        out_specs=[SEM_SPEC] * (2 * ng) + [HBM_SPEC] * n + [pl.BlockSpec(memory_space=pltpu.VMEM)],
        input_output_aliases={t: 2 * ng + t for t in range(n)},
        compiler_params=pltpu.CompilerParams(has_side_effects=DATAFLOW),
    )(*[pltpu.with_memory_space_constraint(w, pltpu.HBM) for w in stacks], *after)
    sems = [(res[2 * g], res[2 * g + 1]) for g in range(ng)]
    return sems, list(res[2 * ng:2 * ng + n]), res[-1]


def _gather_wait(name, stacks, sems, peers, after):
    n = len(stacks)
    after = tuple(after)

    def body(*refs):
        lands = refs[:n]
        group_sems = (refs[n], refs[n + 1])
        x, y, c = _coords()
        for i in range(n):
            for j, k in enumerate(peers):
                cp = _slot_copy(lands[i], group_sems, len(peers) * i + j, x, y, c, k, peers, arriving=True)
                cp.wait_send()
                cp.wait_recv()

    return pl.pallas_call(
        body, name=name, out_shape=[pltpu.HBM(w.shape, w.dtype) for w in stacks],
        in_specs=[HBM_SPEC] * n + [SEM_SPEC, SEM_SPEC] + [ANY_SPEC] * len(after), out_specs=[HBM_SPEC] * n,
        input_output_aliases={t: t for t in range(n)},
        compiler_params=pltpu.CompilerParams(has_side_effects=DATAFLOW),
    )(*stacks, sems[0], sems[1], *after)


N_PEERS = 7


def _peer(x, y, c, k):
    return x ^ (k >> 2), y ^ ((k >> 1) & 1), c ^ (k & 1)


def _reduce_copy(grad, land, sems, idx, x, y, c, k):
    px, py, pc = _peer(x, y, c, k)
    rh = grad.shape[1] // 2
    return pltpu.make_async_remote_copy(src_ref=grad.at[2 * px + py, pl.ds(pc * rh, rh), :], dst_ref=land.at[k - 1],
                                        send_sem=sems[0].at[idx], recv_sem=sems[1].at[idx], device_id=(px, py, pc),
                                        device_id_type=MESH)


def _reduce_start(name, grads):
    n = len(grads)

    def body(*refs):
        src, lands, sems, token = refs[:n], refs[n:2 * n], (refs[2 * n], refs[2 * n + 1]), refs[-1]
        x, y, c = _coords()
        for t in range(n):
            for k in range(1, N_PEERS + 1):
                _reduce_copy(src[t], lands[t], sems, N_PEERS * t + k - 1, x, y, c, k).start()
        token[...] = jnp.zeros_like(token)

    lands = [lax.empty((N_PEERS, g.shape[1] // 2, g.shape[2]), g.dtype) for g in grads]
    out_shape = [pltpu.SemaphoreType.DMA((N_PEERS * n,))] * 2
    out_shape += [pltpu.HBM(a.shape, a.dtype) for a in list(grads) + lands]
    out_shape.append(jax.ShapeDtypeStruct((8, 128), F32))
    res = pl.pallas_call(
        body, name=name, out_shape=out_shape, in_specs=[HBM_SPEC] * (2 * n),
        out_specs=[SEM_SPEC] * 2 + [HBM_SPEC] * (2 * n) + [pl.BlockSpec(memory_space=pltpu.VMEM)],
        input_output_aliases={t: 2 + t for t in range(2 * n)},
        compiler_params=pltpu.CompilerParams(has_side_effects=DATAFLOW),
    )(*[pltpu.with_memory_space_constraint(a, pltpu.HBM) for a in list(grads) + lands])
    return (res[0], res[1]), list(res[2:2 + n]), list(res[2 + n:2 + 2 * n]), res[-1]


def _reduce_wait(name, grads, lands, sems, after):
    n = len(grads)
    after = tuple(after)

    def body(*refs):
        src, dst, group_sems = refs[:n], refs[n:2 * n], (refs[2 * n], refs[2 * n + 1])
        x, y, c = _coords()
        for t in range(n):
            for k in range(1, N_PEERS + 1):
                cp = _reduce_copy(src[t], dst[t], group_sems, N_PEERS * t + k - 1, x, y, c, k)
                cp.wait_send()
                cp.wait_recv()

    res = pl.pallas_call(
        body, name=name, out_shape=[pltpu.HBM(a.shape, a.dtype) for a in list(grads) + list(lands)],
        in_specs=[HBM_SPEC] * (2 * n) + [SEM_SPEC, SEM_SPEC] + [ANY_SPEC] * len(after), out_specs=[HBM_SPEC] * (2 * n),
        input_output_aliases={t: t for t in range(2 * n)},
        compiler_params=pltpu.CompilerParams(has_side_effects=DATAFLOW),
    )(*grads, *lands, sems[0], sems[1], *after)
    return list(res[:n]), list(res[n:])


def _join_halves(name, halves):
    n = len(halves)

    def body(*refs):
        src, dst = refs[:n], refs[n:2 * n]
        send_sems, recv_sems = refs[2 * n:]
        x, y, c = _coords()
        cps = []
        for t in range(n):
            cp = pltpu.make_async_remote_copy(src_ref=src[t], dst_ref=dst[t], send_sem=send_sems.at[t],
                                              recv_sem=recv_sems.at[t], device_id=(x, y, 1 - c), device_id_type=MESH)
            cp.start()
            cps.append(cp)
        for cp in cps:
            cp.wait()

    anyspec = pl.BlockSpec(memory_space=pl.ANY)
    return pl.pallas_call(
        body, name=name, out_shape=[jax.ShapeDtypeStruct(h.shape, h.dtype) for h in halves],
        in_specs=[anyspec] * n, out_specs=[anyspec] * n,
        scratch_shapes=[pltpu.SemaphoreType.DMA((n,)), pltpu.SemaphoreType.DMA((n,))],
    )(*halves)


def _row_block(rows):
    for rb in (512, 256, 128, 64, 32, 16):
        if rows % rb == 0:
            return rb
    raise ValueError(rows)


def _sum_devices(name, grad, land, place):
    S, R, C = grad.shape
    rh = R // 2
    rb = _row_block(rh)
    nbh = rh // rb

    def body(place_ref, g_ref, l_ref, o_ref):
        tot = g_ref[...].astype(F32)
        for k in range(N_PEERS):
            tot = tot + l_ref[k].astype(F32)
        o_ref[...] = tot

    return pl.pallas_call(
        body, name=name,
        grid_spec=pltpu.PrefetchScalarGridSpec(
            num_scalar_prefetch=1, grid=(nbh,),
            in_specs=[pl.BlockSpec((None, rb, C), lambda r, place: (place[0], place[1] * nbh + r, 0)),
                      pl.BlockSpec((N_PEERS, rb, C), lambda r, place: (0, r, 0))],
            out_specs=pl.BlockSpec((rb, C), lambda r, place: (r, 0))),
        out_shape=jax.ShapeDtypeStruct((rh, C), F32), compiler_params=_params(1),
    )(place, grad, land)


def _adamw_math(w, g, m, v):
    m2 = ADAM_B1 * m + (1.0 - ADAM_B1) * g
    v2 = ADAM_B2 * v + (1.0 - ADAM_B2) * (g * g)
    m_hat = m2 / (1.0 - ADAM_B1 ** ADAM_STEP)
    v_hat = v2 / (1.0 - ADAM_B2 ** ADAM_STEP)
    delta = -ADAM_LR * (m_hat / (jnp.sqrt(v_hat) + ADAM_EPS) + ADAM_WD * w)
    return delta, m2, v2


def _adamw(name, w, m, v, gs):
    L, R, C = w.shape
    Rh = R // 2
    rb = _row_block(Rh)
    nbh = Rh // rb
    assert len(gs) == L

    def body(core_ref, w_ref, m_ref, v_ref, *rest):
        g_refs, (go_ref, d_ref, m2_ref, v2_ref) = rest[:2 * L], rest[2 * L:]
        layer, half = pl.program_id(0), pl.program_id(1)
        mine = half == core_ref[0]
        g = jnp.where(mine, g_refs[0][...], g_refs[1][...])
        for t in range(1, L):
            g = jnp.where(layer == t, jnp.where(mine, g_refs[2 * t][...], g_refs[2 * t + 1][...]), g)
        delta, m2, v2 = _adamw_math(w_ref[...], g, m_ref[...], v_ref[...])
        go_ref[...] = g
        d_ref[...] = delta
        m2_ref[...] = m2
        v2_ref[...] = v2

    wspec = pl.BlockSpec((None, rb, C), lambda l, h, r, core: (l, h * nbh + r, 0))
    gspec = pl.BlockSpec((rb, C), lambda l, h, r, core: (r, 0))
    return pl.pallas_call(
        body, name=name,
        grid_spec=pltpu.PrefetchScalarGridSpec(num_scalar_prefetch=1, grid=(L, 2, nbh),
                                               in_specs=[wspec] * 3 + [gspec] * (2 * L), out_specs=[wspec] * 4),
        out_shape=[jax.ShapeDtypeStruct((L, R, C), F32)] * 4, compiler_params=_params(3),
    )(lax.axis_index("c").astype(jnp.int32).reshape(1), w, m, v, *[g for pair in gs for g in pair])


def _adamw_small(w, g, m, v):
    def body(w_ref, g_ref, m_ref, v_ref, d_ref, m2_ref, v2_ref):
        delta, m2, v2 = _adamw_math(w_ref[...], g_ref[...], m_ref[...], v_ref[...])
        d_ref[...] = delta
        m2_ref[...] = m2
        v2_ref[...] = v2

    return pl.pallas_call(body, name="adamw_small", out_shape=[jax.ShapeDtypeStruct(w.shape, F32)] * 3)(w, g, m, v)


def _packed_rows(shape):
    c = shape[-1]
    return (int(np.prod(shape)) // c) * -(-c // LANES)


def _pack(arrays):
    total = sum(_packed_rows(a.shape) for a in arrays)
    total += -total % 8
    buf, r0 = None, 0
    for a in arrays:
        a = a.astype(F32).reshape(-1, a.shape[-1])
        r, c = a.shape
        k = -(-c // LANES)
        a = jnp.pad(a, ((0, 0), (0, k * LANES - c))).reshape(r * k, LANES)
        a = jnp.pad(a, ((r0, total - r0 - r * k), (0, 0)))
        buf = a if buf is None else buf + a
        r0 += r * k
    return buf


def _unpack(buf, shapes):
    out, r0 = [], 0
    for shp in shapes:
        c = shp[-1]
        rows = _packed_rows(shp)
        out.append(buf[r0:r0 + rows].reshape(-1, -(-c // LANES) * LANES)[:, :c].reshape(shp))
        r0 += rows
    return out


def _rms(x, g):
    return x * lax.rsqrt(jnp.mean(x * x, axis=-1, keepdims=True) + NORM_EPS) * g


def _residual_norm_ep(acc, *rest):
    *bias, res, gain = rest
    x = acc + res + (bias[0] if bias else 0.0)
    return x, _rms(x, gain)


RESIDUAL_NORM_OUTS = (("tile", F32), ("tile", BF))


def _mlp_up(tag, h, w_up_sm):
    (up,) = _mm(f"mlp{tag}_up", h, w_up_sm, nt=False, b_sm=True, tm=2048, tn=1024, rows=256,
                ep_fn=lambda acc: (acc,), outs=(("tile", BF),))
    return up


RMS_BWD_OUTS = (("tile", F32), ("tile", BF), ("colsum", F32), ("colsum", F32))


def _mlp_bwd(tag, dy, dy_bf, x, g, up, w_up_sm, w_down):
    (dup,) = _mm(f"mlp{tag}_dup", dy_bf, w_down, nt=True, tm=2048, tn=1024, rows=256, ep_in=((up, "tile"),),
                 ep_fn=lambda acc, u: (acc * (2.0 * jnp.maximum(u.astype(F32), 0.0)),), outs=(("tile", BF),))
    dx, dx_bf, dg, dx_sum = _mm(f"mlp{tag}_dx", dup, w_up_sm, nt=True, b_sm=True, tm=512, tn=1024, rows=256,
                                ep_in=((x, "tile"), (g, "row"), (dy, "tile")), ep_fn=_rms_bwd_ep, outs=RMS_BWD_OUTS)
    return dx, dx_bf, dg, dx_sum, dup


class _Reduction:
    def __init__(self, tag, grads, place):
        self.tag, self.place = tag, place
        self.sems, self.grads, self.lands, self.token = _reduce_start(f"reduce_start_{tag}", grads)

    def finish(self, after):
        grads, lands = _reduce_wait(f"reduce_wait_{self.tag}", self.grads, self.lands, self.sems, after)
        return [_sum_devices(f"reduce_sum_{self.tag}{i}", g, l, self.place) for i, (g, l) in enumerate(zip(grads, lands))]


def kernel(x, conv_norm_g, conv_w_in, conv_b_in, conv_dw, conv_dw_b, conv_ln_g, conv_ln_b, conv_w_out, conv_b_out, attn_norm_g, w_qkv, b_qkv, q_norm_g, k_norm_g, sinks, w_o, b_o, rel_bias, mlp_norm_g, w_up, w_down, loss_target, m_conv_norm_g, m_conv_w_in, m_conv_b_in, m_conv_dw, m_conv_dw_b, m_conv_ln_g, m_conv_ln_b, m_conv_w_out, m_conv_b_out, m_attn_norm_g, m_w_qkv, m_b_qkv, m_q_norm_g, m_k_norm_g, m_sinks, m_w_o, m_b_o, m_rel_bias, m_mlp_norm_g, m_w_up, m_w_down, v_conv_norm_g, v_conv_w_in, v_conv_b_in, v_conv_dw, v_conv_dw_b, v_conv_ln_g, v_conv_ln_b, v_conv_w_out, v_conv_b_out, v_attn_norm_g, v_w_qkv, v_b_qkv, v_q_norm_g, v_k_norm_g, v_sinks, v_w_o, v_b_o, v_rel_bias, v_mlp_norm_g, v_w_up, v_w_down):
    Dm = D_MODEL
    x2d = x[0]
    tgt = loss_target[0]
    T = x2d.shape[0]
    shard = 2 * lax.axis_index("x") + lax.axis_index("y")

    me = 2 * shard + lax.axis_index("c")

    def own_slot(block, slots, index):
        return lax.dynamic_update_slice(lax.empty((slots,) + block.shape, block.dtype), block[None],
                                        (index,) + (0,) * block.ndim)

    sharded_small = [conv_dw[0], attn_norm_g, b_qkv, b_o]
    (small_sems,), (small_land,), small_token = _gather_start(
        "small_weights_start", [own_slot(_pack(sharded_small), 8, me)], ((0,),), ALL_OTHERS, after=())

    big = [conv_w_in[0], conv_w_out[0], w_qkv[0], w_o[0], w_up[0], w_up[1], w_down[0], w_down[1]]
    stacks = [own_slot(w.astype(BF), N_SHARD, shard) for w in big]
    groups = ((0,), (1,), (4, 6), (2, 3), (5, 7))
    gather_sems, stacks, gather_token = _gather_start("gather_start", stacks, groups, OTHER_CHIPS, after=(small_token,))

    def gathered_group(g, name, after):
        return _gather_wait(name, [stacks[t] for t in groups[g]], gather_sems[g], OTHER_CHIPS, after)

    bucket = _bucket_table()
    bias = _bias_table(rel_bias, bucket)

    h0 = _rms_fwd("conv_norm", x2d, conv_norm_g, deps=(gather_token,))
    (w_in_sm,) = gathered_group(0, "gather_wait_conv_in", (h0, bias))
    (u,) = _mm("conv_in", h0, w_in_sm, nt=False, b_sm=True, tm=2048, tn=512, rows=256, ep_in=((conv_b_in, "row"),),
               ep_fn=lambda acc, b: (acc + b,), outs=(("tile", BF),))
    (gathered,) = _gather_wait("small_weights_wait", [small_land], small_sems, ALL_OTHERS, (u,))
    chips = [_unpack(gathered[2 * s], [a.shape for a in sharded_small]) for s in range(N_SHARD)]
    dw_f, attn_norm_f, b_qkv_f, b_o_f = (jnp.concatenate([chips[s][t] for s in range(N_SHARD)], axis=-1)
                                         for t in range(len(sharded_small)))
    dw_pad = jnp.pad(dw_f, ((0, HALO - CONV_W), (0, 0)))
    cv, s_act = _conv_fwd(u, dw_pad, conv_dw_b, conv_ln_g, conv_ln_b)
    (g_out,) = gathered_group(1, "gather_wait_conv_out", (s_act,))
    w_out_f = g_out.reshape(Dm, Dm)
    x1, h1 = _mm("conv_out", s_act, w_out_f, nt=False, tm=1024, tn=1024, rows=256,
                 ep_in=((conv_b_out, "row"), (x2d, "tile"), (mlp_norm_g[0:1], "row")), ep_fn=_residual_norm_ep,
                 outs=RESIDUAL_NORM_OUTS)

    g_up0, g_down0 = gathered_group(2, "gather_wait_mlp0", (x1,))
    w_up_sm = [g_up0, None]
    w_down_f = [g_down0.reshape(D_FF, Dm), None]
    up0 = _mlp_up(0, h1, w_up_sm[0])
    x2, h2 = _mm("mlp0_down", up0, w_down_f[0], nt=False, tm=512, tn=1024, rows=256, a_fn=_relu2,
                 ep_in=((x1, "tile"), (attn_norm_f, "row")), ep_fn=_residual_norm_ep, outs=RESIDUAL_NORM_OUTS)

    g_qkv, g_o = gathered_group(3, "gather_wait_attn", (x2,))
    w_qkv_f = jnp.transpose(g_qkv, (1, 0, 2)).reshape(Dm, QKV_DIM)
    w_o_f = g_o.reshape(ATTN_DIM, Dm)
    (qkv,) = _mm("attn_qkv", h2, w_qkv_f, nt=False, tm=1024, tn=QKV_DIM, rows=256, ep_in=((b_qkv_f, "row"),),
                 ep_fn=lambda acc, b: (acc + b,), outs=(("tile", F32),))
    qg_t = jnp.tile(q_norm_g, (1, N_HEADS))
    kg_t = jnp.tile(k_norm_g, (1, N_KV))
    qn, kn, vv = _qk_norm_fwd(qkv, qg_t, kg_t)
    sinks1 = sinks[0]
    att = _attn_fwd(qn, kn, vv, bias, sinks1)
    x3, h3 = _mm("attn_out", att, w_o_f, nt=False, tm=1024, tn=1024, rows=256,
                 ep_in=((b_o_f, "row"), (x2, "tile"), (mlp_norm_g[1:2], "row")), ep_fn=_residual_norm_ep,
                 outs=RESIDUAL_NORM_OUTS)

    g_up1, g_down1 = gathered_group(4, "gather_wait_mlp1", (x3,))
    w_up_sm[1] = g_up1
    w_down_f[1] = g_down1.reshape(D_FF, Dm)
    up1 = _mlp_up(1, h3, w_up_sm[1])

    def loss_ep(acc, r, t):
        diff = acc + r - t
        dy = diff * (1.0 / Dm)
        return dy, dy, jnp.sum(diff * diff, axis=0, keepdims=True)

    dy, dy_bf, sq = _mm("mlp1_down_loss", up1, w_down_f[1], nt=False, tm=512, tn=1024, rows=256, a_fn=_relu2,
                        ep_in=((x3, "tile"), (tgt, "tile")), ep_fn=loss_ep,
                        outs=(("tile", F32), ("tile", BF), ("colsum", F32)))

    place = jnp.stack([shard, lax.axis_index("c")]).astype(jnp.int32)
    dx3, dx3_bf, dg_mlp1, db_o, dup1 = _mlp_bwd(1, dy, dy_bf, x3, mlp_norm_g[1:2], up1, w_up_sm[1], w_down_f[1])
    dw_down1 = _mm_tn("mlp1_dw_down", up1, dy_bf, tm=1024, tn=1024, tk=2048, a_fn=_relu2)
    dw_up1 = _mm_tn("mlp1_dw_up", h3, dup1, tm=1024, tn=1024, tk=2048, out_sm=N_SHARD)
    red_mlp1 = _Reduction("mlp1", [dw_up1, dw_down1.reshape(N_SHARD, D_FF // N_SHARD, Dm)], place)

    ident = lambda acc: (acc,)
    (datt,) = _mm("attn_dout", dx3_bf, w_o_f, nt=True, tm=1024, tn=1024, rows=256, ep_fn=ident, outs=(("tile", BF),),
                  deps=(red_mlp1.token,))
    dw_o = _mm_tn("attn_dw_o", att, dx3_bf, tm=1024, tn=1024, tk=2048)
    dqn, dkn, dvv, dbias, dsinks = _attn_bwd(qn, kn, vv, bias, sinks1, datt)
    drel = _bias_grad(dbias, bucket)
    dqkv, db_qkv, dqg_t, dkg_t = _qk_norm_bwd(qkv, dqn, dkn, dvv, qg_t, kg_t)
    dw_qkv = _mm_tn("attn_dw_qkv", h2, dqkv, tm=1024, tn=QKV_DIM, tk=2048)
    red_attn = _Reduction("attn", [jnp.transpose(dw_qkv.reshape(Dm, N_SHARD, QKV_DIM // N_SHARD), (1, 0, 2)),
                                   dw_o.reshape(N_SHARD, ATTN_DIM // N_SHARD, Dm)], place)
    dx2, dx2_bf, dg_attn, _ = _mm("attn_dx", dqkv, w_qkv_f, nt=True, tm=512, tn=1024, rows=256,
                                  ep_in=((x2, "tile"), (attn_norm_f, "row"), (dx3, "tile")), ep_fn=_rms_bwd_ep,
                                  outs=RMS_BWD_OUTS, deps=(red_attn.token,))

    dx1, dx1_bf, dg_mlp0, db_out, dup0 = _mlp_bwd(0, dx2, dx2_bf, x1, mlp_norm_g[0:1], up0, w_up_sm[0], w_down_f[0])
    dw_down0 = _mm_tn("mlp0_dw_down", up0, dx2_bf, tm=1024, tn=1024, tk=2048, a_fn=_relu2)
    dw_up0 = _mm_tn("mlp0_dw_up", h1, dup0, tm=1024, tn=1024, tk=2048, out_sm=N_SHARD)
    red_mlp0 = _Reduction("mlp0", [dw_up0, dw_down0.reshape(N_SHARD, D_FF // N_SHARD, Dm)], place)
    (r_qkv, r_o) = red_attn.finish((dx1,))
    (r_up1, r_down1) = red_mlp1.finish((dx1,))

    dcv, dln_g, dln_b, ddw_b = _mm("conv_ds", dx1_bf, w_out_f, nt=True, tm=512, tn=1024, rows=256,
                                   ep_in=((cv, "tile"), (conv_ln_g, "row"), (conv_ln_b, "row")),
                                   ep_fn=_ln_silu_bwd_ep,
                                   outs=(("tile", F32), ("colsum", F32), ("colsum", F32), ("colsum", F32)),
                                   deps=(red_mlp0.token,))
    dw_out = _mm_tn("conv_dw_out", s_act, dx1_bf, tm=1024, tn=1024, tk=2048)
    du, db_in, ddw8 = _conv_bwd(u, dcv, dw_pad)
    (r_up0, r_down0) = red_mlp0.finish((du,))
    dw_in = _mm_tn("conv_dw_in", h0, du, tm=1024, tn=512, tk=2048, out_sm=N_SHARD)
    red_conv = _Reduction("conv", [dw_in, dw_out.reshape(N_SHARD, Dm // N_SHARD, Dm)], place)
    def first_layer_ep(*args):
        tot, _, dg, _ = _rms_bwd_ep(*args)
        return tot, dg

    gx, dg_conv = _mm("conv_dx", du, w_in_sm, nt=True, b_sm=True, tm=512, tn=1024, rows=256,
                      ep_in=((x2d, "tile"), (conv_norm_g, "row"), (dx1, "tile")), ep_fn=first_layer_ep,
                      outs=(("tile", F32), ("colsum", F32)), deps=(red_conv.token,))
    (r_in, r_out) = red_conv.finish((gx,))
    mine = [r_in, r_out, r_qkv, r_o, r_up0, r_up1, r_down0, r_down1]
    r_in, r_out, r_qkv, r_o, r_up0, r_up1, r_down0, r_down1 = zip(mine, _join_halves("join_halves", mine))

    big_out = {}
    for nm, w, m, v, gs in (("conv_w_in", conv_w_in, m_conv_w_in, v_conv_w_in, (r_in,)),
                            ("conv_w_out", conv_w_out, m_conv_w_out, v_conv_w_out, (r_out,)),
                            ("w_qkv", w_qkv, m_w_qkv, v_w_qkv, (r_qkv,)),
                            ("w_o", w_o, m_w_o, v_w_o, (r_o,)),
                            ("w_up", w_up, m_w_up, v_w_up, (r_up0, r_up1)),
                            ("w_down", w_down, m_w_down, v_w_down, (r_down0, r_down1))):
        big_out[nm] = _adamw(f"adamw_{nm}", w, m, v, gs)

    dqg = dqg_t.reshape(N_HEADS, HEAD_DIM).sum(axis=0, keepdims=True)
    dkg = dkg_t.reshape(N_KV, HEAD_DIM).sum(axis=0, keepdims=True)
    small_full = [dg_conv, db_in, ddw8.sum(axis=1)[:CONV_W], ddw_b, dln_g, dln_b, db_out, dg_attn, db_qkv, dqg, dkg,
                  dsinks[None, :], db_o, drel.reshape(1, REL_BUCKETS * N_HEADS),
                  jnp.pad(dg_mlp0, ((0, 1), (0, 0))) + jnp.pad(dg_mlp1, ((1, 0), (0, 0))), sq]
    (sg_sems,), (sg_land,), _ = _gather_start(
        "small_grads_start", [own_slot(_pack(small_full), 8, me)], ((0,),), ALL_OTHERS, after=())
    (sg_land,) = _gather_wait("small_grads_wait", [sg_land], sg_sems, ALL_OTHERS, (big_out["w_down"][0],))
    small_sum = _sum8("small_grads_sum", sg_land)
    (r_norm, r_b_in, r_dw, r_dw_b, r_ln_g, r_ln_b, r_b_out, r_attn_norm, r_b_qkv, r_qg, r_kg, r_sinks, r_b_o, r_rel,
     r_mlp_norm, r_sq) = _unpack(small_sum, [a.shape for a in small_full])
    loss = 0.5 * jnp.sum(r_sq) * (1.0 / Dm)

    def cols(a, width):
        return lax.dynamic_slice_in_dim(a, shard * width, width, axis=a.ndim - 1)

    small_names = ["conv_norm_g", "conv_b_in", "conv_dw", "conv_dw_b", "conv_ln_g", "conv_ln_b", "conv_b_out",
                   "attn_norm_g", "b_qkv", "q_norm_g", "k_norm_g", "sinks", "b_o", "rel_bias", "mlp_norm_g"]
    small_g = [r_norm, r_b_in, cols(r_dw, Dm // N_SHARD)[None], r_dw_b, r_ln_g, r_ln_b, r_b_out,
               cols(r_attn_norm, Dm // N_SHARD), cols(r_b_qkv, QKV_DIM // N_SHARD), r_qg, r_kg, r_sinks,
               cols(r_b_o, Dm // N_SHARD), r_rel.reshape(REL_BUCKETS, N_HEADS), r_mlp_norm]
    small_w = [conv_norm_g, conv_b_in, conv_dw, conv_dw_b, conv_ln_g, conv_ln_b, conv_b_out, attn_norm_g, b_qkv,
               q_norm_g, k_norm_g, sinks, b_o, rel_bias, mlp_norm_g]
    small_m = [m_conv_norm_g, m_conv_b_in, m_conv_dw, m_conv_dw_b, m_conv_ln_g, m_conv_ln_b, m_conv_b_out,
               m_attn_norm_g, m_b_qkv, m_q_norm_g, m_k_norm_g, m_sinks, m_b_o, m_rel_bias, m_mlp_norm_g]
    small_v = [v_conv_norm_g, v_conv_b_in, v_conv_dw, v_conv_dw_b, v_conv_ln_g, v_conv_ln_b, v_conv_b_out,
               v_attn_norm_g, v_b_qkv, v_q_norm_g, v_k_norm_g, v_sinks, v_b_o, v_rel_bias, v_mlp_norm_g]
    flat2 = lambda a: a.reshape(-1, a.shape[-1])
    shapes2 = [flat2(w).shape for w in small_w]
    pk = lambda arrs: _pack([flat2(a) for a in arrs])
    packed_g = pk(small_g)
    d_s, m_s, v_s = _adamw_small(pk(small_w), packed_g, pk(small_m), pk(small_v))
    small_out = {}
    for nm, w, g, d, m2, v2 in zip(small_names, small_w, _unpack(packed_g, shapes2), _unpack(d_s, shapes2),
                                   _unpack(m_s, shapes2), _unpack(v_s, shapes2)):
        small_out[nm] = tuple(a.reshape(w.shape) for a in (g, d, m2, v2))

    order = ["conv_norm_g", "conv_w_in", "conv_b_in", "conv_dw", "conv_dw_b", "conv_ln_g", "conv_ln_b", "conv_w_out",
             "conv_b_out", "attn_norm_g", "w_qkv", "b_qkv", "q_norm_g", "k_norm_g", "sinks", "w_o", "b_o", "rel_bias",
             "mlp_norm_g", "w_up", "w_down"]
    res = {**small_out, **big_out}
    outs = [loss, gx[None]]
    for part in range(4):
        outs += [res[nm][part] for nm in order]
    return tuple(outs)
```

```python
import math

import numpy as np
import jax
import jax.numpy as jnp
from jax import lax
from jax.experimental import pallas as pl
from jax.experimental.pallas import tpu as pltpu

F32 = jnp.float32
BF = jnp.bfloat16
MESH = pl.DeviceIdType.MESH

D_MODEL = 1024
D_FF = 4096
N_HEADS = 16
N_KV = 2
GROUP = N_HEADS // N_KV
HEAD_DIM = 64
ATTN_DIM = N_HEADS * HEAD_DIM
KV_DIM = N_KV * HEAD_DIM
QKV_DIM = ATTN_DIM + 2 * KV_DIM
BLOCK = 128
CONV_W = 31
HALO = 32
REL_BUCKETS = 32
REL_MAX_DIST = 128
NORM_EPS = 1e-6
NEG_INF = -1e30
N_SHARD = 4
LANES = 1024

ADAM_LR = 0.001
ADAM_B1 = 0.9
ADAM_B2 = 0.999
ADAM_EPS = 1e-08
ADAM_WD = 0.01
ADAM_STEP = 10

VMEM_LIMIT = 56 * 1024 * 1024


def _params(n_axes):
    return pltpu.CompilerParams(dimension_semantics=("arbitrary",) * n_axes, vmem_limit_bytes=VMEM_LIMIT)


def _dot(a, b, ca, cb):
    return lax.dot_general(a, b, (((ca,), (cb,)), ((), ())), preferred_element_type=F32)


def _mm(name, a, b, *, nt, tm, tn, ep_fn, outs, a_fn=None, b_sm=False, ep_in=(), deps=(), rows=None):
    M, K = a.shape
    rows = tm if rows is None else rows
    if b_sm:
        S, ks = b.shape[0], b.shape[2]
        N, per = (b.shape[1], None) if nt else (S * b.shape[2], b.shape[2] // tn)
        assert (S * ks == K) if nt else (b.shape[1] == K)
    else:
        N = b.shape[0] if nt else b.shape[1]
        assert (b.shape[1] if nt else b.shape[0]) == K
    assert M % tm == 0 and N % tn == 0 and tm % rows == 0
    ne, no, nd = len(ep_in), len(outs), len(deps)

    def body(a_ref, b_ref, *rest):
        ep_refs, out_refs = rest[:ne], rest[ne + nd:ne + nd + no]
        i = pl.program_id(1)
        sums = [None] * no
        for r in range(tm // rows):
            rs = pl.ds(r * rows, rows)

            def lhs(cols):
                av = a_ref[rs, cols]
                return (av if a_fn is None else a_fn(av)).astype(BF)

            if b_sm and nt:
                acc = None
                for s in range(S):
                    part = _dot(lhs(pl.ds(s * ks, ks)), b_ref[s].astype(BF), 1, 1)
                    acc = part if acc is None else acc + part
            else:
                acc = _dot(lhs(slice(None)), b_ref[...].astype(BF), 1, 1 if nt else 0)
            ep_vals = [ref[rs, :] if kind == "tile" else ref[...] for ref, (_, kind) in zip(ep_refs, ep_in)]
            vals = ep_fn(acc, *ep_vals)
            for o, ((kind, dt), ref, val) in enumerate(zip(outs, out_refs, vals)):
                if kind == "tile":
                    ref[rs, :] = val.astype(dt)
                else:
                    sums[o] = val if sums[o] is None else sums[o] + val
        for (kind, dt), ref, val in zip(outs, out_refs, sums):
            if kind == "colsum":
                @pl.when(i == 0)
                def _():
                    ref[...] = val

                @pl.when(i > 0)
                def _():
                    ref[...] += val

    if b_sm and nt:
        b_spec = pl.BlockSpec((S, tn, ks), lambda j, i: (0, j, 0))
    elif b_sm:
        b_spec = pl.BlockSpec((None, K, tn), lambda j, i: (j // per, 0, j % per))
    elif nt:
        b_spec = pl.BlockSpec((tn, K), lambda j, i: (j, 0))
    else:
        b_spec = pl.BlockSpec((K, tn), lambda j, i: (0, j))
    in_specs = [pl.BlockSpec((tm, K), lambda j, i: (i, 0)), b_spec]
    for arr, kind in ep_in:
        if kind == "tile":
            assert arr.shape == (M, N)
            in_specs.append(pl.BlockSpec((tm, tn), lambda j, i: (i, j)))
        else:
            assert arr.shape == (1, N)
            in_specs.append(pl.BlockSpec((1, tn), lambda j, i: (0, j)))
    in_specs += [pl.BlockSpec(memory_space=pl.ANY)] * nd
    out_shape, out_specs = [], []
    for kind, dt in outs:
        if kind == "tile":
            out_shape.append(jax.ShapeDtypeStruct((M, N), dt))
            out_specs.append(pl.BlockSpec((tm, tn), lambda j, i: (i, j)))
        else:
            out_shape.append(jax.ShapeDtypeStruct((1, N), F32))
            out_specs.append(pl.BlockSpec((1, tn), lambda j, i: (0, j)))
    return pl.pallas_call(
        body, name=name, grid=(N // tn, M // tm), in_specs=in_specs, out_specs=out_specs, out_shape=out_shape,
        compiler_params=_params(2),
    )(a, b, *[arr for arr, _ in ep_in], *deps)


def _mm_tn(name, a, b, *, tm, tn, tk, a_fn=None, out_sm=None):
    T, Ka = a.shape
    N = b.shape[1]
    assert b.shape[0] == T and T % tk == 0 and Ka % tm == 0 and N % tn == 0
    nk = T // tk

    def body(a_ref, b_ref, o_ref, acc_ref):
        k = pl.program_id(2)

        @pl.when(k == 0)
        def _():
            acc_ref[...] = jnp.zeros_like(acc_ref)

        av = a_ref[...]
        if a_fn is not None:
            av = a_fn(av)
        acc_ref[...] += _dot(av.astype(BF), b_ref[...].astype(BF), 0, 0)

        @pl.when(k == nk - 1)
        def _():
            o_ref[...] = acc_ref[...].astype(BF)

    if out_sm is None:
        out_shape = jax.ShapeDtypeStruct((Ka, N), BF)
        out_spec = pl.BlockSpec((tm, tn), lambda i, j, k: (i, j))
    else:
        per = (N // out_sm) // tn
        assert per * tn * out_sm == N
        out_shape = jax.ShapeDtypeStruct((out_sm, Ka, N // out_sm), BF)
        out_spec = pl.BlockSpec((None, tm, tn), lambda i, j, k: (j // per, i, j % per))
    return pl.pallas_call(
        body, name=name, grid=(Ka // tm, N // tn, nk),
        in_specs=[pl.BlockSpec((tk, tm), lambda i, j, k: (k, i)), pl.BlockSpec((tk, tn), lambda i, j, k: (k, j))],
        out_specs=out_spec, out_shape=out_shape, scratch_shapes=[pltpu.VMEM((tm, tn), F32)],
        compiler_params=_params(3),
    )(a, b)


def _relu2(v):
    r = jnp.maximum(v.astype(F32), 0.0)
    return r * r


def _rms_bwd_ep(dh, x, g, dres):
    rstd = lax.rsqrt(jnp.mean(x * x, axis=-1, keepdims=True) + NORM_EPS)
    xh = x * rstd
    dxh = dh * g
    dx = rstd * (dxh - xh * jnp.mean(dxh * xh, axis=-1, keepdims=True))
    tot = dres + dx
    return tot, tot, jnp.sum(dh * xh, axis=0, keepdims=True), jnp.sum(tot, axis=0, keepdims=True)


def _rms_fwd(name, x, g, tm=512, deps=()):
    T, Dm = x.shape

    def body(x_ref, g_ref, *rest):
        o_ref = rest[-1]
        xv = x_ref[...]
        rstd = lax.rsqrt(jnp.mean(xv * xv, axis=-1, keepdims=True) + NORM_EPS)
        o_ref[...] = (xv * rstd * g_ref[...]).astype(BF)

    return pl.pallas_call(
        body, name=name, grid=(T // tm,),
        in_specs=[pl.BlockSpec((tm, Dm), lambda i: (i, 0)), pl.BlockSpec((1, Dm), lambda i: (0, 0))]
        + [pl.BlockSpec(memory_space=pl.ANY)] * len(deps),
        out_specs=pl.BlockSpec((tm, Dm), lambda i: (i, 0)), out_shape=jax.ShapeDtypeStruct((T, Dm), BF),
        compiler_params=_params(1),
    )(x, g, *deps)


HEAD_COLS = 128


def _two_term_dot(v, m):
    hi = v.astype(BF)
    lo = (v - hi.astype(F32)).astype(BF)
    return _dot(hi, m, 1, 0) + _dot(lo, m, 1, 0)


def _head_sum(v, select):
    sel, sel_t = select
    return _two_term_dot(_two_term_dot(v, sel), sel_t)


def _head_select(n):
    sel = (np.arange(n)[:, None] // HEAD_DIM == np.arange(HEAD_COLS)[None, :]).astype(np.float32)
    return jnp.asarray(sel, dtype=BF), jnp.asarray(sel.T, dtype=BF)


def _qk_norm_fwd(qkv, qg_t, kg_t, tm=256):
    T = qkv.shape[0]
    scale = 1.0 / math.sqrt(HEAD_DIM)

    def body(x_ref, qg_ref, kg_ref, sq_ref, sqt_ref, sk_ref, skt_ref, q_ref, k_ref, v_ref):
        q = x_ref[:, pl.ds(0, ATTN_DIM)]
        rq = lax.rsqrt(_head_sum(q * q, (sq_ref[...], sqt_ref[...])) * (1.0 / HEAD_DIM) + NORM_EPS)
        q_ref[...] = (q * rq * qg_ref[...] * scale).astype(BF)
        k = x_ref[:, pl.ds(ATTN_DIM, KV_DIM)]
        rk = lax.rsqrt(_head_sum(k * k, (sk_ref[...], skt_ref[...])) * (1.0 / HEAD_DIM) + NORM_EPS)
        k_ref[...] = (k * rk * kg_ref[...]).astype(BF)
        v_ref[...] = x_ref[:, pl.ds(ATTN_DIM + KV_DIM, KV_DIM)].astype(BF)

    full = lambda shape: pl.BlockSpec(shape, lambda i: (0, 0))
    return pl.pallas_call(
        body, name="qk_norm_fwd", grid=(T // tm,),
        in_specs=[pl.BlockSpec((tm, QKV_DIM), lambda i: (i, 0)), full((1, ATTN_DIM)), full((1, KV_DIM)),
                  full((ATTN_DIM, HEAD_COLS)), full((HEAD_COLS, ATTN_DIM)), full((KV_DIM, HEAD_COLS)), full((HEAD_COLS, KV_DIM))],
        out_specs=[pl.BlockSpec((tm, ATTN_DIM), lambda i: (i, 0)), pl.BlockSpec((tm, KV_DIM), lambda i: (i, 0)),
                   pl.BlockSpec((tm, KV_DIM), lambda i: (i, 0))],
        out_shape=[jax.ShapeDtypeStruct((T, ATTN_DIM), BF), jax.ShapeDtypeStruct((T, KV_DIM), BF),
                   jax.ShapeDtypeStruct((T, KV_DIM), BF)],
        compiler_params=_params(1),
    )(qkv, qg_t, kg_t, *_head_select(ATTN_DIM), *_head_select(KV_DIM))


def _qk_norm_bwd(qkv, dqn, dkn, dv, qg_t, kg_t, tm=256):
    T = qkv.shape[0]

    def body(x_ref, dq_ref, dk_ref, dv_ref, qg_ref, kg_ref, sq_ref, sqt_ref, sk_ref, skt_ref,
             o_ref, db_ref, dqg_ref, dkg_ref):
        i = pl.program_id(0)

        def one(x, dy, g, select):
            r = lax.rsqrt(_head_sum(x * x, select) * (1.0 / HEAD_DIM) + NORM_EPS)
            xh = x * r
            dxh = dy * g
            dx = r * (dxh - xh * (_head_sum(dxh * xh, select) * (1.0 / HEAD_DIM)))
            return dx, jnp.sum(dy * xh, axis=0, keepdims=True)

        dq, dqg = one(x_ref[:, pl.ds(0, ATTN_DIM)], dq_ref[...], qg_ref[...], (sq_ref[...], sqt_ref[...]))
        dk, dkg = one(x_ref[:, pl.ds(ATTN_DIM, KV_DIM)], dk_ref[...], kg_ref[...], (sk_ref[...], skt_ref[...]))
        dvv = dv_ref[...]
        o_ref[:, pl.ds(0, ATTN_DIM)] = dq.astype(BF)
        o_ref[:, pl.ds(ATTN_DIM, KV_DIM)] = dk.astype(BF)
        o_ref[:, pl.ds(ATTN_DIM + KV_DIM, KV_DIM)] = dvv.astype(BF)
        sq, sk, sv = (jnp.sum(t, axis=0, keepdims=True) for t in (dq, dk, dvv))

        @pl.when(i == 0)
        def _():
            db_ref[:, pl.ds(0, ATTN_DIM)] = sq
            db_ref[:, pl.ds(ATTN_DIM, KV_DIM)] = sk
            db_ref[:, pl.ds(ATTN_DIM + KV_DIM, KV_DIM)] = sv
            dqg_ref[...] = dqg
            dkg_ref[...] = dkg

        @pl.when(i > 0)
        def _():
            db_ref[:, pl.ds(0, ATTN_DIM)] += sq
            db_ref[:, pl.ds(ATTN_DIM, KV_DIM)] += sk
            db_ref[:, pl.ds(ATTN_DIM + KV_DIM, KV_DIM)] += sv
            dqg_ref[...] += dqg
            dkg_ref[...] += dkg

    full = lambda shape: pl.BlockSpec(shape, lambda i: (0, 0))
    row = lambda n: pl.BlockSpec((tm, n), lambda i: (i, 0))
    return pl.pallas_call(
        body, name="qk_norm_bwd", grid=(T // tm,),
        in_specs=[row(QKV_DIM), row(ATTN_DIM), row(KV_DIM), row(KV_DIM), full((1, ATTN_DIM)), full((1, KV_DIM)),
                  full((ATTN_DIM, HEAD_COLS)), full((HEAD_COLS, ATTN_DIM)), full((KV_DIM, HEAD_COLS)), full((HEAD_COLS, KV_DIM))],
        out_specs=[row(QKV_DIM), full((1, QKV_DIM)), full((1, ATTN_DIM)), full((1, KV_DIM))],
        out_shape=[jax.ShapeDtypeStruct((T, QKV_DIM), BF), jax.ShapeDtypeStruct((1, QKV_DIM), F32),
                   jax.ShapeDtypeStruct((1, ATTN_DIM), F32), jax.ShapeDtypeStruct((1, KV_DIM), F32)],
        compiler_params=_params(1),
    )(qkv, dqn, dkn, dv, qg_t, kg_t, *_head_select(ATTN_DIM), *_head_select(KV_DIM))


ROWS = 64
COLS = 128


SUBLANES = 8
FIRST_TAP = HALO - (CONV_W - 1)


def _glu(a, g):
    return a.astype(F32) * jax.nn.sigmoid(g.astype(F32))


def _shifted(xe, s):
    return xe if s == 0 else pltpu.roll(xe, ROWS + HALO - s, axis=0)


def _conv_fwd(u, dw_pad, dw_b, ln_g, ln_b, tm=256):
    T = u.shape[0]
    Dm = D_MODEL
    hpt = tm // HALO

    def body(ac_ref, gc_ref, ap_ref, gp_ref, w_ref, wb_ref, lg_ref, lb_ref, cv_ref, s_ref, ext):
        i = pl.program_id(0)
        ext[pl.ds(0, HALO), :] = jnp.where(i > 0, _glu(ap_ref[...], gp_ref[...]), 0.0)
        ext[pl.ds(HALO, tm), :] = _glu(ac_ref[...], gc_ref[...])

        def rows(r, carry):
            r0 = pl.multiple_of(r * ROWS, ROWS)
            for c in range(Dm // COLS):
                cs = pl.ds(c * COLS, COLS)
                xe = ext[pl.ds(r0, ROWS + HALO), cs]
                acc = jnp.zeros((ROWS, COLS), F32)
                for s in range(SUBLANES):
                    xs = _shifted(xe, s)
                    for j in range(CONV_W):
                        off = FIRST_TAP + j
                        if off % SUBLANES == s:
                            acc = acc + xs[off - s:off - s + ROWS, :] * w_ref[pl.ds(j, 1), cs]
                cv_ref[pl.ds(r0, ROWS), cs] = acc + wb_ref[:, cs]
            return carry

        lax.fori_loop(0, tm // ROWS, rows, 0)
        cv = cv_ref[...]
        xc = cv - jnp.mean(cv, axis=-1, keepdims=True)
        y = xc * lax.rsqrt(jnp.mean(xc * xc, axis=-1, keepdims=True) + NORM_EPS) * lg_ref[...] + lb_ref[...]
        s_ref[...] = (y * jax.nn.sigmoid(y)).astype(BF)

    full = lambda shape: pl.BlockSpec(shape, lambda i: (0, 0))
    return pl.pallas_call(
        body, name="conv_fwd", grid=(T // tm,),
        in_specs=[pl.BlockSpec((tm, Dm), lambda i: (i, 0)), pl.BlockSpec((tm, Dm), lambda i: (i, 1)),
                  pl.BlockSpec((HALO, Dm), lambda i: (jnp.maximum(i * hpt - 1, 0), 0)),
                  pl.BlockSpec((HALO, Dm), lambda i: (jnp.maximum(i * hpt - 1, 0), 1)),
                  full((HALO, Dm)), full((1, Dm)), full((1, Dm)), full((1, Dm))],
        out_specs=[pl.BlockSpec((tm, Dm), lambda i: (i, 0)), pl.BlockSpec((tm, Dm), lambda i: (i, 0))],
        out_shape=[jax.ShapeDtypeStruct((T, Dm), F32), jax.ShapeDtypeStruct((T, Dm), BF)],
        scratch_shapes=[pltpu.VMEM((tm + HALO, Dm), F32)],
        compiler_params=_params(1),
    )(u, u, u, u, dw_pad, dw_b, ln_g, ln_b)


def _ln_silu_bwd_ep(ds, cv, lg, lb):
    xc = cv - jnp.mean(cv, axis=-1, keepdims=True)
    rstd = lax.rsqrt(jnp.mean(xc * xc, axis=-1, keepdims=True) + NORM_EPS)
    xh = xc * rstd
    y = xh * lg + lb
    sg = jax.nn.sigmoid(y)
    dy = ds * (sg * (1.0 + y * (1.0 - sg)))
    dxh = dy * lg
    dcv = rstd * (dxh - jnp.mean(dxh, axis=-1, keepdims=True) - xh * jnp.mean(dxh * xh, axis=-1, keepdims=True))
    return (dcv, jnp.sum(dy * xh, axis=0, keepdims=True), jnp.sum(dy, axis=0, keepdims=True),
            jnp.sum(dcv, axis=0, keepdims=True))


def _conv_bwd(u, dcv, dw_pad, tm=256):
    T = u.shape[0]
    Dm = D_MODEL
    hpt = tm // HALO
    last = T // HALO - 1
    nt = T // tm

    def body(ac_ref, gc_ref, ap_ref, gp_ref, dc_ref, dn_ref, w_ref, du_ref, db_ref, dw_ref, ext_g, ext_d):
        i = pl.program_id(0)
        ext_g[pl.ds(0, HALO), :] = jnp.where(i > 0, _glu(ap_ref[...], gp_ref[...]), 0.0)
        ext_g[pl.ds(HALO, tm), :] = _glu(ac_ref[...], gc_ref[...])
        ext_d[pl.ds(0, tm), :] = dc_ref[...]
        ext_d[pl.ds(tm, HALO), :] = jnp.where(i < nt - 1, dn_ref[...], 0.0)

        @pl.when(i == 0)
        def _():
            db_ref[...] = jnp.zeros_like(db_ref)
            dw_ref[...] = jnp.zeros_like(dw_ref)

        def rows(r, carry):
            r0 = pl.multiple_of(r * ROWS, ROWS)
            rs = pl.ds(r0, ROWS)
            for c in range(Dm // COLS):
                cs = pl.ds(c * COLS, COLS)
                cs2 = pl.ds(Dm + c * COLS, COLS)
                de = ext_d[pl.ds(r0, ROWS + HALO), cs]
                ge = ext_g[pl.ds(r0, ROWS + HALO), cs]
                dcur = de[0:ROWS, :]
                acc = jnp.zeros((ROWS, COLS), F32)
                for s in range(SUBLANES):
                    ds_, gs_ = _shifted(de, s), _shifted(ge, s)
                    for j in range(CONV_W):
                        off = CONV_W - 1 - j
                        if off % SUBLANES == s:
                            acc = acc + ds_[off - s:off - s + ROWS, :] * w_ref[pl.ds(j, 1), cs]
                        goff = FIRST_TAP + j
                        if goff % SUBLANES == s:
                            prod = dcur * gs_[goff - s:goff - s + ROWS, :]
                            dw_ref[j, :, cs] += jnp.sum(prod.reshape(ROWS // SUBLANES, SUBLANES, COLS), axis=0)
                a = ac_ref[rs, cs].astype(F32)
                sg = jax.nn.sigmoid(gc_ref[rs, cs].astype(F32))
                da = acc * sg
                dg = acc * a * sg * (1.0 - sg)
                du_ref[rs, cs] = da.astype(BF)
                du_ref[rs, cs2] = dg.astype(BF)
                db_ref[:, cs] += jnp.sum(da, axis=0, keepdims=True)
                db_ref[:, cs2] += jnp.sum(dg, axis=0, keepdims=True)
            return carry

        lax.fori_loop(0, tm // ROWS, rows, 0)

    return pl.pallas_call(
        body, name="conv_bwd", grid=(nt,),
        in_specs=[pl.BlockSpec((tm, Dm), lambda i: (i, 0)), pl.BlockSpec((tm, Dm), lambda i: (i, 1)),
                  pl.BlockSpec((HALO, Dm), lambda i: (jnp.maximum(i * hpt - 1, 0), 0)),
                  pl.BlockSpec((HALO, Dm), lambda i: (jnp.maximum(i * hpt - 1, 0), 1)),
                  pl.BlockSpec((tm, Dm), lambda i: (i, 0)),
                  pl.BlockSpec((HALO, Dm), lambda i: (jnp.minimum((i + 1) * hpt, last), 0)),
                  pl.BlockSpec((HALO, Dm), lambda i: (0, 0))],
        out_specs=[pl.BlockSpec((tm, 2 * Dm), lambda i: (i, 0)), pl.BlockSpec((1, 2 * Dm), lambda i: (0, 0)),
                   pl.BlockSpec((HALO, 8, Dm), lambda i: (0, 0, 0))],
        out_shape=[jax.ShapeDtypeStruct((T, 2 * Dm), BF), jax.ShapeDtypeStruct((1, 2 * Dm), F32),
                   jax.ShapeDtypeStruct((HALO, 8, Dm), F32)],
        scratch_shapes=[pltpu.VMEM((tm + HALO, Dm), F32), pltpu.VMEM((tm + HALO, Dm), F32)],
        compiler_params=_params(1),
    )(u, u, u, u, dcv, dcv, dw_pad)


def _bucket_table():
    q_loc = np.arange(BLOCK)[:, None]
    k_loc = np.arange(2 * BLOCK)[None, :]
    dist = q_loc + BLOCK - k_loc
    n = np.maximum(dist, 0)
    max_exact = REL_BUCKETS // 2
    large = max_exact + (np.log(np.maximum(n, 1).astype(np.float32) / max_exact)
                         / math.log(REL_MAX_DIST / max_exact) * (REL_BUCKETS - max_exact)).astype(np.int32)
    large = np.minimum(large, REL_BUCKETS - 1)
    bucket = np.where(n < max_exact, n, large).astype(np.int32)
    return jnp.asarray(np.where((dist >= 0) & (dist < BLOCK), bucket, -1).astype(np.int32))


def _bias_table(rel_bias, bucket):
    def body(rb_ref, bk_ref, o_ref):
        bk = bk_ref[...]
        for h in range(N_HEADS):
            acc = jnp.full((BLOCK, 2 * BLOCK), NEG_INF, F32)
            for b in range(REL_BUCKETS):
                acc = jnp.where(bk == b, rb_ref[b, h], acc)
            o_ref[h] = acc

    return pl.pallas_call(
        body, name="bias_table", out_shape=jax.ShapeDtypeStruct((N_HEADS, BLOCK, 2 * BLOCK), F32),
        in_specs=[pl.BlockSpec(memory_space=pltpu.SMEM), pl.BlockSpec(memory_space=pltpu.VMEM)],
        out_specs=pl.BlockSpec(memory_space=pltpu.VMEM),
    )(rel_bias, bucket)


def _bias_grad(dbias, bucket):
    def body(db_ref, bk_ref, o_ref):
        bk = bk_ref[...]
        for b in range(REL_BUCKETS):
            sel = bk == b
            for h in range(N_HEADS):
                o_ref[b, h] = jnp.sum(jnp.where(sel, db_ref[h], 0.0))

    return pl.pallas_call(
        body, name="bias_grad", out_shape=jax.ShapeDtypeStruct((REL_BUCKETS, N_HEADS), F32),
        in_specs=[pl.BlockSpec(memory_space=pltpu.VMEM), pl.BlockSpec(memory_space=pltpu.VMEM)],
        out_specs=pl.BlockSpec(memory_space=pltpu.SMEM),
    )(dbias, bucket)


GROUP_ROWS = GROUP * BLOCK


def _head_probs(qk, bias_h, sink, first):
    s = jnp.where(first, NEG_INF, qk + bias_h)
    m = jnp.maximum(jnp.max(s, axis=-1, keepdims=True), sink)
    p = jnp.exp(s - m)
    ps = jnp.exp(sink - m)
    inv = 1.0 / (jnp.sum(p, axis=-1, keepdims=True) + ps)
    return p * inv, ps * inv


def _band(prev_ref, cur_ref, g):
    hs = pl.ds(g * HEAD_DIM, HEAD_DIM)
    return jnp.concatenate([prev_ref[:, hs], cur_ref[:, hs]], axis=0)


def _stack_heads(ref, g):
    return jnp.concatenate([ref[:, pl.ds((g * GROUP + hh) * HEAD_DIM, HEAD_DIM)] for hh in range(GROUP)], axis=0)


def _unstack_heads(ref, g, stacked, dtype):
    for hh in range(GROUP):
        ref[:, pl.ds((g * GROUP + hh) * HEAD_DIM, HEAD_DIM)] = stacked[hh * BLOCK:(hh + 1) * BLOCK, :].astype(dtype)


def _first_mask(n):
    col = lax.broadcasted_iota(jnp.int32, (1, 2 * BLOCK), 1)
    return jnp.logical_and(n == 0, col < BLOCK)


def _head_rows(hh):
    return pl.ds(hh * BLOCK, BLOCK)


def _attn_fwd(qn, kn, vv, bias, sinks):
    T = qn.shape[0]
    nb = T // BLOCK

    def body(sk_ref, q_ref, kc_ref, kp_ref, vc_ref, vp_ref, b_ref, o_ref, qk_buf, p_buf):
        first = _first_mask(pl.program_id(0))
        for g in range(N_KV):
            k = _band(kp_ref, kc_ref, g)
            v = _band(vp_ref, vc_ref, g)
            qk_buf[g] = _dot(_stack_heads(q_ref, g), k, 1, 1)
            for hh in range(GROUP):
                h = g * GROUP + hh
                pn, _ = _head_probs(qk_buf[g, _head_rows(hh), :], b_ref[h], sk_ref[h], first)
                p_buf[g, _head_rows(hh), :] = pn.astype(BF)
            _unstack_heads(o_ref, g, _dot(p_buf[g], v, 1, 0), BF)

    cur = lambda n: (n, 0)
    prev = lambda n: (jnp.maximum(n - 1, 0), 0)
    return pl.pallas_call(
        body, name="attn_fwd", grid=(nb,),
        in_specs=[pl.BlockSpec(memory_space=pltpu.SMEM), pl.BlockSpec((BLOCK, ATTN_DIM), cur),
                  pl.BlockSpec((BLOCK, KV_DIM), cur), pl.BlockSpec((BLOCK, KV_DIM), prev),
                  pl.BlockSpec((BLOCK, KV_DIM), cur), pl.BlockSpec((BLOCK, KV_DIM), prev),
                  pl.BlockSpec((N_HEADS, BLOCK, 2 * BLOCK), lambda n: (0, 0, 0))],
        out_specs=pl.BlockSpec((BLOCK, ATTN_DIM), cur), out_shape=jax.ShapeDtypeStruct((T, ATTN_DIM), BF),
        scratch_shapes=[pltpu.VMEM((N_KV, GROUP_ROWS, 2 * BLOCK), F32), pltpu.VMEM((N_KV, GROUP_ROWS, 2 * BLOCK), BF)],
        compiler_params=_params(1),
    )(sinks, qn, kn, kn, vv, vv, bias)


def _attn_bwd(qn, kn, vv, bias, sinks, do):
    T = qn.shape[0]
    nb = T // BLOCK
    scale = 1.0 / math.sqrt(HEAD_DIM)

    def body(sk_ref, q_ref, kc_ref, kp_ref, vc_ref, vp_ref, b_ref, do_ref,
             dq_ref, dk_ref, dv_ref, db_ref, dsk_ref, dk_full, dv_full, dk_carry, dv_carry, qk_buf, dp_buf, p_buf, ds_buf):
        n = pl.program_id(0)

        @pl.when(n == 0)
        def _():
            db_ref[...] = jnp.zeros_like(db_ref)
            dk_carry[...] = jnp.zeros_like(dk_carry)
            dv_carry[...] = jnp.zeros_like(dv_carry)
            for h in range(N_HEADS):
                dsk_ref[h] = 0.0

        @pl.when(n < nb)
        def _():
            first = _first_mask(n)
            for g in range(N_KV):
                k = _band(kp_ref, kc_ref, g)
                v = _band(vp_ref, vc_ref, g)
                q = _stack_heads(q_ref, g)
                dout = _stack_heads(do_ref, g)
                qk_buf[g] = _dot(q, k, 1, 1)
                dp_buf[g] = _dot(dout, v, 1, 1)
                for hh in range(GROUP):
                    h = g * GROUP + hh
                    rows = _head_rows(hh)
                    pn, psink = _head_probs(qk_buf[g, rows, :], b_ref[h], sk_ref[h], first)
                    dp = dp_buf[g, rows, :]
                    delta = jnp.sum(pn * dp, axis=-1, keepdims=True)
                    ds = pn * (dp - delta)
                    dsk_ref[h] += -jnp.sum(psink * delta)
                    db_ref[h] += ds
                    ds_buf[g, rows, :] = ds.astype(BF)
                    p_buf[g, rows, :] = pn.astype(BF)
                dsb = ds_buf[g]
                _unstack_heads(dq_ref, g, _dot(dsb, k, 1, 0) * scale, F32)
                gs = pl.ds(g * HEAD_DIM, HEAD_DIM)
                dk_full[:, gs] = _dot(dsb, q, 0, 0)
                dv_full[:, gs] = _dot(p_buf[g], dout, 0, 0)

        @pl.when(n == nb)
        def _():
            dk_full[...] = jnp.zeros_like(dk_full)
            dv_full[...] = jnp.zeros_like(dv_full)

        dk_ref[...] = dk_carry[...] + dk_full[pl.ds(0, BLOCK), :]
        dv_ref[...] = dv_carry[...] + dv_full[pl.ds(0, BLOCK), :]
        dk_carry[...] = dk_full[pl.ds(BLOCK, BLOCK), :]
        dv_carry[...] = dv_full[pl.ds(BLOCK, BLOCK), :]

    cur = lambda n: (jnp.minimum(n, nb - 1), 0)
    prev = lambda n: (jnp.maximum(jnp.minimum(n, nb - 1) - 1, 0), 0)
    out_kv = lambda n: (jnp.maximum(n - 1, 0), 0)
    return pl.pallas_call(
        body, name="attn_bwd", grid=(nb + 1,),
        in_specs=[pl.BlockSpec(memory_space=pltpu.SMEM), pl.BlockSpec((BLOCK, ATTN_DIM), cur),
                  pl.BlockSpec((BLOCK, KV_DIM), cur), pl.BlockSpec((BLOCK, KV_DIM), prev),
                  pl.BlockSpec((BLOCK, KV_DIM), cur), pl.BlockSpec((BLOCK, KV_DIM), prev),
                  pl.BlockSpec((N_HEADS, BLOCK, 2 * BLOCK), lambda n: (0, 0, 0)),
                  pl.BlockSpec((BLOCK, ATTN_DIM), cur)],
        out_specs=[pl.BlockSpec((BLOCK, ATTN_DIM), cur), pl.BlockSpec((BLOCK, KV_DIM), out_kv),
                   pl.BlockSpec((BLOCK, KV_DIM), out_kv),
                   pl.BlockSpec((N_HEADS, BLOCK, 2 * BLOCK), lambda n: (0, 0, 0)),
                   pl.BlockSpec(memory_space=pltpu.SMEM)],
        out_shape=[jax.ShapeDtypeStruct((T, ATTN_DIM), F32), jax.ShapeDtypeStruct((T, KV_DIM), F32),
                   jax.ShapeDtypeStruct((T, KV_DIM), F32),
                   jax.ShapeDtypeStruct((N_HEADS, BLOCK, 2 * BLOCK), F32), jax.ShapeDtypeStruct((N_HEADS,), F32)],
        scratch_shapes=[pltpu.VMEM((2 * BLOCK, KV_DIM), F32), pltpu.VMEM((2 * BLOCK, KV_DIM), F32),
                        pltpu.VMEM((BLOCK, KV_DIM), F32), pltpu.VMEM((BLOCK, KV_DIM), F32),
                        pltpu.VMEM((N_KV, GROUP_ROWS, 2 * BLOCK), F32), pltpu.VMEM((N_KV, GROUP_ROWS, 2 * BLOCK), F32),
                        pltpu.VMEM((N_KV, GROUP_ROWS, 2 * BLOCK), BF), pltpu.VMEM((N_KV, GROUP_ROWS, 2 * BLOCK), BF)],
        compiler_params=_params(1),
    )(sinks, qn, kn, kn, vv, vv, bias, do)


def _coords():
    return lax.axis_index("x"), lax.axis_index("y"), lax.axis_index("c")


def _sum8(name, blocks):
    def body(b_ref, o_ref):
        tot = b_ref[0]
        for d in range(1, 8):
            tot = tot + b_ref[d]
        o_ref[...] = tot

    return pl.pallas_call(body, name=name, out_shape=jax.ShapeDtypeStruct(blocks.shape[1:], F32))(blocks)


HBM_SPEC = pl.BlockSpec(memory_space=pltpu.HBM)
SEM_SPEC = pl.BlockSpec(memory_space=pltpu.SEMAPHORE)
ANY_SPEC = pl.BlockSpec(memory_space=pl.ANY)
DATAFLOW = pltpu.SideEffectType.DATAFLOW_SIDE_EFFECTING


OTHER_CHIPS = (4, 2, 6)
ALL_OTHERS = (1, 2, 3, 4, 5, 6, 7)


def _slot(x, y, c, peers):
    return 2 * x + y if peers is OTHER_CHIPS else 4 * x + 2 * y + c


def _slot_copy(land, sems, idx, x, y, c, k, peers, arriving):
    send_sems, recv_sems = sems
    px, py, pc = x ^ (k >> 2), y ^ ((k >> 1) & 1), c ^ (k & 1)
    mine = _slot(x, y, c, peers)
    dst = _slot(px, py, pc, peers) if arriving else mine
    return pltpu.make_async_remote_copy(src_ref=land.at[mine], dst_ref=land.at[dst], send_sem=send_sems.at[idx],
                                        recv_sem=recv_sems.at[idx], device_id=(px, py, pc), device_id_type=MESH)


def _gather_start(name, stacks, groups, peers, after):
    n = len(stacks)
    ng = len(groups)
    np_ = len(peers)
    after = tuple(after)

    def body(*refs):
        lands = refs[:n]
        first = n + len(after)
        sems = [(refs[first + 2 * g], refs[first + 2 * g + 1]) for g in range(ng)]
        token = refs[-1]
        x, y, c = _coords()
        for g, members in enumerate(groups):
            for i, t in enumerate(members):
                for j, k in enumerate(peers):
                    _slot_copy(lands[t], sems[g], np_ * i + j, x, y, c, k, peers, arriving=False).start()
        token[...] = jnp.zeros_like(token)

    out_shape = []
    for members in groups:
        out_shape += [pltpu.SemaphoreType.DMA((np_ * len(members),))] * 2
    out_shape += [pltpu.HBM(w.shape, w.dtype) for w in stacks]
    out_shape.append(jax.ShapeDtypeStruct((8, 128), F32))
    res = pl.pallas_call(
        body, name=name, out_shape=out_shape, in_specs=[HBM_SPEC] * n + [ANY_SPEC] * len(after),
        out_specs=[SEM_SPEC] * (2 * ng) + [HBM_SPEC] * n + [pl.BlockSpec(memory_space=pltpu.VMEM)],
        input_output_aliases={t: 2 * ng + t for t in range(n)},
        compiler_params=pltpu.CompilerParams(has_side_effects=DATAFLOW),
    )(*[pltpu.with_memory_space_constraint(w, pltpu.HBM) for w in stacks], *after)
    sems = [(res[2 * g], res[2 * g + 1]) for g in range(ng)]
    return sems, list(res[2 * ng:2 * ng + n]), res[-1]


def _gather_wait(name, stacks, sems, peers, after):
    n = len(stacks)
    after = tuple(after)

    def body(*refs):
        lands = refs[:n]
        group_sems = (refs[n], refs[n + 1])
        x, y, c = _coords()
        for i in range(n):
            for j, k in enumerate(peers):
                cp = _slot_copy(lands[i], group_sems, len(peers) * i + j, x, y, c, k, peers, arriving=True)
                cp.wait_send()
                cp.wait_recv()

    return pl.pallas_call(
        body, name=name, out_shape=[pltpu.HBM(w.shape, w.dtype) for w in stacks],
        in_specs=[HBM_SPEC] * n + [SEM_SPEC, SEM_SPEC] + [ANY_SPEC] * len(after), out_specs=[HBM_SPEC] * n,
        input_output_aliases={t: t for t in range(n)},
        compiler_params=pltpu.CompilerParams(has_side_effects=DATAFLOW),
    )(*stacks, sems[0], sems[1], *after)


N_PEERS = 7


def _peer(x, y, c, k):
    return x ^ (k >> 2), y ^ ((k >> 1) & 1), c ^ (k & 1)


def _reduce_copy(grad, land, sems, idx, x, y, c, k):
    px, py, pc = _peer(x, y, c, k)
    rh = grad.shape[1] // 2
    return pltpu.make_async_remote_copy(src_ref=grad.at[2 * px + py, pl.ds(pc * rh, rh), :], dst_ref=land.at[k - 1],
                                        send_sem=sems[0].at[idx], recv_sem=sems[1].at[idx], device_id=(px, py, pc),
                                        device_id_type=MESH)


def _reduce_start(name, grads):
    n = len(grads)

    def body(*refs):
        src, lands, sems, token = refs[:n], refs[n:2 * n], (refs[2 * n], refs[2 * n + 1]), refs[-1]
        x, y, c = _coords()
        for t in range(n):
            for k in range(1, N_PEERS + 1):
                _reduce_copy(src[t], lands[t], sems, N_PEERS * t + k - 1, x, y, c, k).start()
        token[...] = jnp.zeros_like(token)

    lands = [lax.empty((N_PEERS, g.shape[1] // 2, g.shape[2]), g.dtype) for g in grads]
    out_shape = [pltpu.SemaphoreType.DMA((N_PEERS * n,))] * 2
    out_shape += [pltpu.HBM(a.shape, a.dtype) for a in list(grads) + lands]
    out_shape.append(jax.ShapeDtypeStruct((8, 128), F32))
    res = pl.pallas_call(
        body, name=name, out_shape=out_shape, in_specs=[HBM_SPEC] * (2 * n),
        out_specs=[SEM_SPEC] * 2 + [HBM_SPEC] * (2 * n) + [pl.BlockSpec(memory_space=pltpu.VMEM)],
        input_output_aliases={t: 2 + t for t in range(2 * n)},
        compiler_params=pltpu.CompilerParams(has_side_effects=DATAFLOW),
    )(*[pltpu.with_memory_space_constraint(a, pltpu.HBM) for a in list(grads) + lands])
    return (res[0], res[1]), list(res[2:2 + n]), list(res[2 + n:2 + 2 * n]), res[-1]


def _reduce_wait(name, grads, lands, sems, after):
    n = len(grads)
    after = tuple(after)

    def body(*refs):
        src, dst, group_sems = refs[:n], refs[n:2 * n], (refs[2 * n], refs[2 * n + 1])
        x, y, c = _coords()
        for t in range(n):
            for k in range(1, N_PEERS + 1):
                cp = _reduce_copy(src[t], dst[t], group_sems, N_PEERS * t + k - 1, x, y, c, k)
                cp.wait_send()
                cp.wait_recv()

    res = pl.pallas_call(
        body, name=name, out_shape=[pltpu.HBM(a.shape, a.dtype) for a in list(grads) + list(lands)],
        in_specs=[HBM_SPEC] * (2 * n) + [SEM_SPEC, SEM_SPEC] + [ANY_SPEC] * len(after), out_specs=[HBM_SPEC] * (2 * n),
        input_output_aliases={t: t for t in range(2 * n)},
        compiler_params=pltpu.CompilerParams(has_side_effects=DATAFLOW),
    )(*grads, *lands, sems[0], sems[1], *after)
    return list(res[:n]), list(res[n:])


def _join_halves(name, halves, deps=()):
    n = len(halves)

    def body(*refs):
        src, dst = refs[:n], refs[n + len(deps):2 * n + len(deps)]
        send_sems, recv_sems = refs[-2:]
        x, y, c = _coords()
        cps = []
        for t in range(n):
            cp = pltpu.make_async_remote_copy(src_ref=src[t], dst_ref=dst[t], send_sem=send_sems.at[t],
                                              recv_sem=recv_sems.at[t], device_id=(x, y, 1 - c), device_id_type=MESH)
            cp.start()
            cps.append(cp)
        for cp in cps:
            cp.wait()

    anyspec = pl.BlockSpec(memory_space=pl.ANY)
    return pl.pallas_call(
        body, name=name, out_shape=[jax.ShapeDtypeStruct(h.shape, h.dtype) for h in halves],
        in_specs=[anyspec] * (n + len(deps)), out_specs=[anyspec] * n,
        scratch_shapes=[pltpu.SemaphoreType.DMA((n,)), pltpu.SemaphoreType.DMA((n,))],
    )(*halves, *deps)


def _row_block(rows):
    for rb in (512, 256, 128, 64, 32, 16):
        if rows % rb == 0:
            return rb
    raise ValueError(rows)


def _sum_devices(name, grad, land, place):
    S, R, C = grad.shape
    rh = R // 2
    rb = _row_block(rh)
    nbh = rh // rb

    def body(place_ref, g_ref, l_ref, o_ref):
        tot = g_ref[...].astype(F32)
        for k in range(N_PEERS):
            tot = tot + l_ref[k].astype(F32)
        o_ref[...] = tot

    return pl.pallas_call(
        body, name=name,
        grid_spec=pltpu.PrefetchScalarGridSpec(
            num_scalar_prefetch=1, grid=(nbh,),
            in_specs=[pl.BlockSpec((None, rb, C), lambda r, place: (place[0], place[1] * nbh + r, 0)),
                      pl.BlockSpec((N_PEERS, rb, C), lambda r, place: (0, r, 0))],
            out_specs=pl.BlockSpec((rb, C), lambda r, place: (r, 0))),
        out_shape=jax.ShapeDtypeStruct((rh, C), F32), compiler_params=_params(1),
    )(place, grad, land)


def _adamw_math(w, g, m, v):
    m2 = ADAM_B1 * m + (1.0 - ADAM_B1) * g
    v2 = ADAM_B2 * v + (1.0 - ADAM_B2) * (g * g)
    m_hat = m2 / (1.0 - ADAM_B1 ** ADAM_STEP)
    v_hat = v2 / (1.0 - ADAM_B2 ** ADAM_STEP)
    delta = -ADAM_LR * (m_hat / (jnp.sqrt(v_hat) + ADAM_EPS) + ADAM_WD * w)
    return delta, m2, v2


def _adamw(name, w, m, v, gs):
    L, R, C = w.shape
    Rh = R // 2
    rb = _row_block(Rh)
    nbh = Rh // rb
    assert len(gs) == L

    def body(core_ref, w_ref, m_ref, v_ref, *rest):
        g_refs, (go_ref, d_ref, m2_ref, v2_ref) = rest[:2 * L], rest[2 * L:]
        layer, half = pl.program_id(0), pl.program_id(1)
        mine = half == core_ref[0]
        g = jnp.where(mine, g_refs[0][...], g_refs[1][...])
        for t in range(1, L):
            g = jnp.where(layer == t, jnp.where(mine, g_refs[2 * t][...], g_refs[2 * t + 1][...]), g)
        delta, m2, v2 = _adamw_math(w_ref[...], g, m_ref[...], v_ref[...])
        go_ref[...] = g
        d_ref[...] = delta
        m2_ref[...] = m2
        v2_ref[...] = v2

    wspec = pl.BlockSpec((None, rb, C), lambda l, h, r, core: (l, h * nbh + r, 0))
    gspec = pl.BlockSpec((rb, C), lambda l, h, r, core: (r, 0))
    return pl.pallas_call(
        body, name=name,
        grid_spec=pltpu.PrefetchScalarGridSpec(num_scalar_prefetch=1, grid=(L, 2, nbh),
                                               in_specs=[wspec] * 3 + [gspec] * (2 * L), out_specs=[wspec] * 4),
        out_shape=[jax.ShapeDtypeStruct((L, R, C), F32)] * 4, compiler_params=_params(3),
    )(lax.axis_index("c").astype(jnp.int32).reshape(1), w, m, v, *[g for pair in gs for g in pair])


def _adamw_small(ws, gs, ms, vs):
    n = len(ws)

    def body(*refs):
        w_refs, g_refs, m_refs, v_refs = (refs[k * n:(k + 1) * n] for k in range(4))
        d_refs, m2_refs, v2_refs = (refs[(4 + k) * n:(5 + k) * n] for k in range(3))
        for t in range(n):
            delta, m2, v2 = _adamw_math(w_refs[t][...], g_refs[t][...], m_refs[t][...], v_refs[t][...])
            d_refs[t][...] = delta
            m2_refs[t][...] = m2
            v2_refs[t][...] = v2

    res = pl.pallas_call(body, name="adamw_small", out_shape=[jax.ShapeDtypeStruct(w.shape, F32) for w in ws] * 3)(
        *ws, *gs, *ms, *vs)
    return res[:n], res[n:2 * n], res[2 * n:]


def _packed_rows(shape):
    c = shape[-1]
    return (int(np.prod(shape)) // c) * -(-c // LANES)


def _pack(arrays):
    total = sum(_packed_rows(a.shape) for a in arrays)
    total += -total % 8
    buf, r0 = None, 0
    for a in arrays:
        a = a.astype(F32).reshape(-1, a.shape[-1])
        r, c = a.shape
        k = -(-c // LANES)
        a = jnp.pad(a, ((0, 0), (0, k * LANES - c))).reshape(r * k, LANES)
        a = jnp.pad(a, ((r0, total - r0 - r * k), (0, 0)))
        buf = a if buf is None else buf + a
        r0 += r * k
    return buf


def _unpack(buf, shapes):
    out, r0 = [], 0
    for shp in shapes:
        c = shp[-1]
        rows = _packed_rows(shp)
        out.append(buf[r0:r0 + rows].reshape(-1, -(-c // LANES) * LANES)[:, :c].reshape(shp))
        r0 += rows
    return out


def _rms(x, g):
    return x * lax.rsqrt(jnp.mean(x * x, axis=-1, keepdims=True) + NORM_EPS) * g


def _residual_norm_ep(acc, *rest):
    *bias, res, gain = rest
    x = acc + res + (bias[0] if bias else 0.0)
    return x, _rms(x, gain)


RESIDUAL_NORM_OUTS = (("tile", F32), ("tile", BF))


def _mlp_up(tag, h, w_up_sm):
    (up,) = _mm(f"mlp{tag}_up", h, w_up_sm, nt=False, b_sm=True, tm=2048, tn=1024, rows=256,
                ep_fn=lambda acc: (acc,), outs=(("tile", BF),))
    return up


RMS_BWD_OUTS = (("tile", F32), ("tile", BF), ("colsum", F32), ("colsum", F32))


def _mlp_bwd(tag, dy, dy_bf, x, g, up, w_up_sm, w_down):
    (dup,) = _mm(f"mlp{tag}_dup", dy_bf, w_down, nt=True, tm=2048, tn=1024, rows=256, ep_in=((up, "tile"),),
                 ep_fn=lambda acc, u: (acc * (2.0 * jnp.maximum(u.astype(F32), 0.0)),), outs=(("tile", BF),))
    dx, dx_bf, dg, dx_sum = _mm(f"mlp{tag}_dx", dup, w_up_sm, nt=True, b_sm=True, tm=512, tn=1024, rows=256,
                                ep_in=((x, "tile"), (g, "row"), (dy, "tile")), ep_fn=_rms_bwd_ep, outs=RMS_BWD_OUTS)
    return dx, dx_bf, dg, dx_sum, dup


class _Reduction:
    def __init__(self, tag, grads, place):
        self.tag, self.place = tag, place
        self.sems, self.grads, self.lands, self.token = _reduce_start(f"reduce_start_{tag}", grads)

    def finish(self, after):
        grads, lands = _reduce_wait(f"reduce_wait_{self.tag}", self.grads, self.lands, self.sems, after)
        return [_sum_devices(f"reduce_sum_{self.tag}{i}", g, l, self.place) for i, (g, l) in enumerate(zip(grads, lands))]


def kernel(x, conv_norm_g, conv_w_in, conv_b_in, conv_dw, conv_dw_b, conv_ln_g, conv_ln_b, conv_w_out, conv_b_out, attn_norm_g, w_qkv, b_qkv, q_norm_g, k_norm_g, sinks, w_o, b_o, rel_bias, mlp_norm_g, w_up, w_down, loss_target, m_conv_norm_g, m_conv_w_in, m_conv_b_in, m_conv_dw, m_conv_dw_b, m_conv_ln_g, m_conv_ln_b, m_conv_w_out, m_conv_b_out, m_attn_norm_g, m_w_qkv, m_b_qkv, m_q_norm_g, m_k_norm_g, m_sinks, m_w_o, m_b_o, m_rel_bias, m_mlp_norm_g, m_w_up, m_w_down, v_conv_norm_g, v_conv_w_in, v_conv_b_in, v_conv_dw, v_conv_dw_b, v_conv_ln_g, v_conv_ln_b, v_conv_w_out, v_conv_b_out, v_attn_norm_g, v_w_qkv, v_b_qkv, v_q_norm_g, v_k_norm_g, v_sinks, v_w_o, v_b_o, v_rel_bias, v_mlp_norm_g, v_w_up, v_w_down):
    Dm = D_MODEL
    x2d = x[0]
    tgt = loss_target[0]
    T = x2d.shape[0]
    shard = 2 * lax.axis_index("x") + lax.axis_index("y")

    me = 2 * shard + lax.axis_index("c")

    def own_slot(block, slots, index):
        return lax.dynamic_update_slice(lax.empty((slots,) + block.shape, block.dtype), block[None],
                                        (index,) + (0,) * block.ndim)

    sharded_small = [conv_dw[0], attn_norm_g, b_qkv, b_o]
    (small_sems,), (small_land,), small_token = _gather_start(
        "small_weights_start", [own_slot(_pack(sharded_small), 8, me)], ((0,),), ALL_OTHERS, after=())

    big = [conv_w_in[0], conv_w_out[0], w_qkv[0], w_o[0], w_up[0], w_up[1], w_down[0], w_down[1]]
    stacks = [own_slot(w.astype(BF), N_SHARD, shard) for w in big]
    groups = ((0,), (1,), (4, 6), (2, 3), (5, 7))
    gather_sems, stacks, gather_token = _gather_start("gather_start", stacks, groups, OTHER_CHIPS, after=(small_token,))

    def gathered_group(g, name, after):
        return _gather_wait(name, [stacks[t] for t in groups[g]], gather_sems[g], OTHER_CHIPS, after)

    bucket = _bucket_table()
    bias = _bias_table(rel_bias, bucket)

    h0 = _rms_fwd("conv_norm", x2d, conv_norm_g, deps=(gather_token,))
    (w_in_sm,) = gathered_group(0, "gather_wait_conv_in", (h0, bias))
    (u,) = _mm("conv_in", h0, w_in_sm, nt=False, b_sm=True, tm=2048, tn=512, rows=256, ep_in=((conv_b_in, "row"),),
               ep_fn=lambda acc, b: (acc + b,), outs=(("tile", BF),))
    (gathered,) = _gather_wait("small_weights_wait", [small_land], small_sems, ALL_OTHERS, (u,))
    chips = [_unpack(gathered[2 * s], [a.shape for a in sharded_small]) for s in range(N_SHARD)]
    dw_f, attn_norm_f, b_qkv_f, b_o_f = (jnp.concatenate([chips[s][t] for s in range(N_SHARD)], axis=-1)
                                         for t in range(len(sharded_small)))
    dw_pad = jnp.pad(dw_f, ((0, HALO - CONV_W), (0, 0)))
    cv, s_act = _conv_fwd(u, dw_pad, conv_dw_b, conv_ln_g, conv_ln_b)
    (g_out,) = gathered_group(1, "gather_wait_conv_out", (s_act,))
    w_out_f = g_out.reshape(Dm, Dm)
    x1, h1 = _mm("conv_out", s_act, w_out_f, nt=False, tm=1024, tn=1024, rows=256,
                 ep_in=((conv_b_out, "row"), (x2d, "tile"), (mlp_norm_g[0:1], "row")), ep_fn=_residual_norm_ep,
                 outs=RESIDUAL_NORM_OUTS)

    g_up0, g_down0 = gathered_group(2, "gather_wait_mlp0", (x1,))
    w_up_sm = [g_up0, None]
    w_down_f = [g_down0.reshape(D_FF, Dm), None]
    up0 = _mlp_up(0, h1, w_up_sm[0])
    x2, h2 = _mm("mlp0_down", up0, w_down_f[0], nt=False, tm=512, tn=1024, rows=256, a_fn=_relu2,
                 ep_in=((x1, "tile"), (attn_norm_f, "row")), ep_fn=_residual_norm_ep, outs=RESIDUAL_NORM_OUTS)

    g_qkv, g_o = gathered_group(3, "gather_wait_attn", (x2,))
    w_qkv_f = jnp.transpose(g_qkv, (1, 0, 2)).reshape(Dm, QKV_DIM)
    w_o_f = g_o.reshape(ATTN_DIM, Dm)
    (qkv,) = _mm("attn_qkv", h2, w_qkv_f, nt=False, tm=1024, tn=QKV_DIM, rows=256, ep_in=((b_qkv_f, "row"),),
                 ep_fn=lambda acc, b: (acc + b,), outs=(("tile", F32),))
    qg_t = jnp.tile(q_norm_g, (1, N_HEADS))
    kg_t = jnp.tile(k_norm_g, (1, N_KV))
    qn, kn, vv = _qk_norm_fwd(qkv, qg_t, kg_t)
    sinks1 = sinks[0]
    att = _attn_fwd(qn, kn, vv, bias, sinks1)
    x3, h3 = _mm("attn_out", att, w_o_f, nt=False, tm=1024, tn=1024, rows=256,
                 ep_in=((b_o_f, "row"), (x2, "tile"), (mlp_norm_g[1:2], "row")), ep_fn=_residual_norm_ep,
                 outs=RESIDUAL_NORM_OUTS)

    g_up1, g_down1 = gathered_group(4, "gather_wait_mlp1", (x3,))
    w_up_sm[1] = g_up1
    w_down_f[1] = g_down1.reshape(D_FF, Dm)
    up1 = _mlp_up(1, h3, w_up_sm[1])

    def loss_ep(acc, r, t):
        diff = acc + r - t
        dy = diff * (1.0 / Dm)
        return dy, dy, jnp.sum(diff * diff, axis=0, keepdims=True)

    dy, dy_bf, sq = _mm("mlp1_down_loss", up1, w_down_f[1], nt=False, tm=512, tn=1024, rows=256, a_fn=_relu2,
                        ep_in=((x3, "tile"), (tgt, "tile")), ep_fn=loss_ep,
                        outs=(("tile", F32), ("tile", BF), ("colsum", F32)))

    place = jnp.stack([shard, lax.axis_index("c")]).astype(jnp.int32)
    dx3, dx3_bf, dg_mlp1, db_o, dup1 = _mlp_bwd(1, dy, dy_bf, x3, mlp_norm_g[1:2], up1, w_up_sm[1], w_down_f[1])
    dw_down1 = _mm_tn("mlp1_dw_down", up1, dy_bf, tm=1024, tn=1024, tk=2048, a_fn=_relu2)
    dw_up1 = _mm_tn("mlp1_dw_up", h3, dup1, tm=1024, tn=1024, tk=2048, out_sm=N_SHARD)
    red_mlp1 = _Reduction("mlp1", [dw_up1, dw_down1.reshape(N_SHARD, D_FF // N_SHARD, Dm)], place)

    ident = lambda acc: (acc,)
    (datt,) = _mm("attn_dout", dx3_bf, w_o_f, nt=True, tm=1024, tn=1024, rows=256, ep_fn=ident, outs=(("tile", BF),),
                  deps=(red_mlp1.token,))
    dw_o = _mm_tn("attn_dw_o", att, dx3_bf, tm=1024, tn=1024, tk=2048)
    dqn, dkn, dvv, dbias, dsinks = _attn_bwd(qn, kn, vv, bias, sinks1, datt)
    drel = _bias_grad(dbias, bucket)
    dqkv, db_qkv, dqg_t, dkg_t = _qk_norm_bwd(qkv, dqn, dkn, dvv, qg_t, kg_t)
    dw_qkv = _mm_tn("attn_dw_qkv", h2, dqkv, tm=1024, tn=QKV_DIM, tk=2048)
    red_attn = _Reduction("attn", [jnp.transpose(dw_qkv.reshape(Dm, N_SHARD, QKV_DIM // N_SHARD), (1, 0, 2)),
                                   dw_o.reshape(N_SHARD, ATTN_DIM // N_SHARD, Dm)], place)
    dx2, dx2_bf, dg_attn, _ = _mm("attn_dx", dqkv, w_qkv_f, nt=True, tm=512, tn=1024, rows=256,
                                  ep_in=((x2, "tile"), (attn_norm_f, "row"), (dx3, "tile")), ep_fn=_rms_bwd_ep,
                                  outs=RMS_BWD_OUTS, deps=(red_attn.token,))

    dx1, dx1_bf, dg_mlp0, db_out, dup0 = _mlp_bwd(0, dx2, dx2_bf, x1, mlp_norm_g[0:1], up0, w_up_sm[0], w_down_f[0])
    dw_down0 = _mm_tn("mlp0_dw_down", up0, dx2_bf, tm=1024, tn=1024, tk=2048, a_fn=_relu2)
    dw_up0 = _mm_tn("mlp0_dw_up", h1, dup0, tm=1024, tn=1024, tk=2048, out_sm=N_SHARD)
    red_mlp0 = _Reduction("mlp0", [dw_up0, dw_down0.reshape(N_SHARD, D_FF // N_SHARD, Dm)], place)
    (r_qkv, r_o) = red_attn.finish((dx1,))
    (r_up1, r_down1) = red_mlp1.finish((dx1,))

    dcv, dln_g, dln_b, ddw_b = _mm("conv_ds", dx1_bf, w_out_f, nt=True, tm=512, tn=1024, rows=256,
                                   ep_in=((cv, "tile"), (conv_ln_g, "row"), (conv_ln_b, "row")),
                                   ep_fn=_ln_silu_bwd_ep,
                                   outs=(("tile", F32), ("colsum", F32), ("colsum", F32), ("colsum", F32)),
                                   deps=(red_mlp0.token,))
    dw_out = _mm_tn("conv_dw_out", s_act, dx1_bf, tm=1024, tn=1024, tk=2048)
    du, db_in, ddw8 = _conv_bwd(u, dcv, dw_pad)
    (r_up0, r_down0) = red_mlp0.finish((du,))
    dw_in = _mm_tn("conv_dw_in", h0, du, tm=1024, tn=512, tk=2048, out_sm=N_SHARD)
    red_conv = _Reduction("conv", [dw_in, dw_out.reshape(N_SHARD, Dm // N_SHARD, Dm)], place)
    def first_layer_ep(*args):
        tot, _, dg, _ = _rms_bwd_ep(*args)
        return tot, dg

    gx, dg_conv = _mm("conv_dx", du, w_in_sm, nt=True, b_sm=True, tm=512, tn=1024, rows=256,
                      ep_in=((x2d, "tile"), (conv_norm_g, "row"), (dx1, "tile")), ep_fn=first_layer_ep,
                      outs=(("tile", F32), ("colsum", F32)), deps=(red_conv.token,))
    (r_in, r_out) = red_conv.finish((gx,))

    dqg = dqg_t.reshape(N_HEADS, HEAD_DIM).sum(axis=0, keepdims=True)
    dkg = dkg_t.reshape(N_KV, HEAD_DIM).sum(axis=0, keepdims=True)
    small_full = [dg_conv, db_in, ddw8.sum(axis=1)[:CONV_W], ddw_b, dln_g, dln_b, db_out, dg_attn, db_qkv, dqg, dkg,
                  dsinks[None, :], db_o, drel.reshape(1, REL_BUCKETS * N_HEADS),
                  jnp.pad(dg_mlp0, ((0, 1), (0, 0))) + jnp.pad(dg_mlp1, ((1, 0), (0, 0))), sq]
    (sg_sems,), (sg_land,), sg_token = _gather_start(
        "small_grads_start", [own_slot(_pack(small_full), 8, me)], ((0,),), ALL_OTHERS, after=())

    mine = [r_in, r_out, r_qkv, r_o, r_up0, r_up1, r_down0, r_down1]
    r_in, r_out, r_qkv, r_o, r_up0, r_up1, r_down0, r_down1 = zip(
        mine, _join_halves("join_halves", mine, deps=(sg_token,)))

    big_out = {}
    for nm, w, m, v, gs in (("conv_w_in", conv_w_in, m_conv_w_in, v_conv_w_in, (r_in,)),
                            ("conv_w_out", conv_w_out, m_conv_w_out, v_conv_w_out, (r_out,)),
                            ("w_qkv", w_qkv, m_w_qkv, v_w_qkv, (r_qkv,)),
                            ("w_o", w_o, m_w_o, v_w_o, (r_o,)),
                            ("w_up", w_up, m_w_up, v_w_up, (r_up0, r_up1)),
                            ("w_down", w_down, m_w_down, v_w_down, (r_down0, r_down1))):
        big_out[nm] = _adamw(f"adamw_{nm}", w, m, v, gs)

    (sg_land,) = _gather_wait("small_grads_wait", [sg_land], sg_sems, ALL_OTHERS,
                              [big_out[nm][0] for nm in big_out])
    small_sum = _sum8("small_grads_sum", sg_land)
    (r_norm, r_b_in, r_dw, r_dw_b, r_ln_g, r_ln_b, r_b_out, r_attn_norm, r_b_qkv, r_qg, r_kg, r_sinks, r_b_o, r_rel,
     r_mlp_norm, r_sq) = _unpack(small_sum, [a.shape for a in small_full])
    loss = 0.5 * jnp.sum(r_sq) * (1.0 / Dm)

    def cols(a, width):
        return lax.dynamic_slice_in_dim(a, shard * width, width, axis=a.ndim - 1)

    small_names = ["conv_norm_g", "conv_b_in", "conv_dw", "conv_dw_b", "conv_ln_g", "conv_ln_b", "conv_b_out",
                   "attn_norm_g", "b_qkv", "q_norm_g", "k_norm_g", "sinks", "b_o", "rel_bias", "mlp_norm_g"]
    small_g = [r_norm, r_b_in, cols(r_dw, Dm // N_SHARD)[None], r_dw_b, r_ln_g, r_ln_b, r_b_out,
               cols(r_attn_norm, Dm // N_SHARD), cols(r_b_qkv, QKV_DIM // N_SHARD), r_qg, r_kg, r_sinks,
               cols(r_b_o, Dm // N_SHARD), r_rel.reshape(REL_BUCKETS, N_HEADS), r_mlp_norm]
    small_w = [conv_norm_g, conv_b_in, conv_dw, conv_dw_b, conv_ln_g, conv_ln_b, conv_b_out, attn_norm_g, b_qkv,
               q_norm_g, k_norm_g, sinks, b_o, rel_bias, mlp_norm_g]
    small_m = [m_conv_norm_g, m_conv_b_in, m_conv_dw, m_conv_dw_b, m_conv_ln_g, m_conv_ln_b, m_conv_b_out,
               m_attn_norm_g, m_b_qkv, m_q_norm_g, m_k_norm_g, m_sinks, m_b_o, m_rel_bias, m_mlp_norm_g]
    small_v = [v_conv_norm_g, v_conv_b_in, v_conv_dw, v_conv_dw_b, v_conv_ln_g, v_conv_ln_b, v_conv_b_out,
               v_attn_norm_g, v_b_qkv, v_q_norm_g, v_k_norm_g, v_sinks, v_b_o, v_rel_bias, v_mlp_norm_g]
    flat2 = lambda a: a.reshape(-1, a.shape[-1])
    small_g = [flat2(g) for g in small_g]
    d_s, m_s, v_s = _adamw_small([flat2(w) for w in small_w], small_g, [flat2(m) for m in small_m],
                                 [flat2(v) for v in small_v])
    small_out = {}
    for nm, w, g, d, m2, v2 in zip(small_names, small_w, small_g, d_s, m_s, v_s):
        small_out[nm] = tuple(a.reshape(w.shape) for a in (g, d, m2, v2))

    order = ["conv_norm_g", "conv_w_in", "conv_b_in", "conv_dw", "conv_dw_b", "conv_ln_g", "conv_ln_b", "conv_w_out",
             "conv_b_out", "attn_norm_g", "w_qkv", "b_qkv", "q_norm_g", "k_norm_g", "sinks", "w_o", "b_o", "rel_bias",
             "mlp_norm_g", "w_up", "w_down"]
    res = {**small_out, **big_out}
    outs = [loss, gx[None]]
    for part in range(4):
        outs += [res[nm][part] for nm in order]
    return tuple(outs)
```

```python
import math

import numpy as np
import jax
import jax.numpy as jnp
from jax import lax
from jax.experimental import pallas as pl
from jax.experimental.pallas import tpu as pltpu

F32 = jnp.float32
BF = jnp.bfloat16
MESH = pl.DeviceIdType.MESH

D_MODEL = 1024
D_FF = 4096
N_HEADS = 16
N_KV = 2
GROUP = N_HEADS // N_KV
HEAD_DIM = 64
ATTN_DIM = N_HEADS * HEAD_DIM
KV_DIM = N_KV * HEAD_DIM
QKV_DIM = ATTN_DIM + 2 * KV_DIM
BLOCK = 128
CONV_W = 31
HALO = 32
REL_BUCKETS = 32
REL_MAX_DIST = 128
NORM_EPS = 1e-6
NEG_INF = -1e30
N_SHARD = 4
LANES = 1024

ADAM_LR = 0.001
ADAM_B1 = 0.9
ADAM_B2 = 0.999
ADAM_EPS = 1e-08
ADAM_WD = 0.01
ADAM_STEP = 10

VMEM_LIMIT = 56 * 1024 * 1024


def _params(n_axes):
    return pltpu.CompilerParams(dimension_semantics=("arbitrary",) * n_axes, vmem_limit_bytes=VMEM_LIMIT)


def _dot(a, b, ca, cb):
    return lax.dot_general(a, b, (((ca,), (cb,)), ((), ())), preferred_element_type=F32)


def _mm(name, a, b, *, nt, tm, tn, ep_fn, outs, a_fn=None, b_sm=False, ep_in=(), deps=(), rows=None):
    M, K = a.shape
    rows = tm if rows is None else rows
    if b_sm:
        S, ks = b.shape[0], b.shape[2]
        N, per = (b.shape[1], None) if nt else (S * b.shape[2], b.shape[2] // tn)
        assert (S * ks == K) if nt else (b.shape[1] == K)
    else:
        N = b.shape[0] if nt else b.shape[1]
        assert (b.shape[1] if nt else b.shape[0]) == K
    assert M % tm == 0 and N % tn == 0 and tm % rows == 0
    ne, no, nd = len(ep_in), len(outs), len(deps)

    def body(a_ref, b_ref, *rest):
        ep_refs, out_refs = rest[:ne], rest[ne + nd:ne + nd + no]
        i = pl.program_id(1)
        sums = [None] * no
        for r in range(tm // rows):
            rs = pl.ds(r * rows, rows)

            def lhs(cols):
                av = a_ref[rs, cols]
                return (av if a_fn is None else a_fn(av)).astype(BF)

            if b_sm and nt:
                acc = None
                for s in range(S):
                    part = _dot(lhs(pl.ds(s * ks, ks)), b_ref[s].astype(BF), 1, 1)
                    acc = part if acc is None else acc + part
            else:
                acc = _dot(lhs(slice(None)), b_ref[...].astype(BF), 1, 1 if nt else 0)
            ep_vals = [ref[rs, :] if kind == "tile" else ref[...] for ref, (_, kind) in zip(ep_refs, ep_in)]
            vals = ep_fn(acc, *ep_vals)
            for o, ((kind, dt), ref, val) in enumerate(zip(outs, out_refs, vals)):
                if kind == "tile":
                    ref[rs, :] = val.astype(dt)
                else:
                    sums[o] = val if sums[o] is None else sums[o] + val
        for (kind, dt), ref, val in zip(outs, out_refs, sums):
            if kind == "colsum":
                @pl.when(i == 0)
                def _():
                    ref[...] = val

                @pl.when(i > 0)
                def _():
                    ref[...] += val

    if b_sm and nt:
        b_spec = pl.BlockSpec((S, tn, ks), lambda j, i: (0, j, 0))
    elif b_sm:
        b_spec = pl.BlockSpec((None, K, tn), lambda j, i: (j // per, 0, j % per))
    elif nt:
        b_spec = pl.BlockSpec((tn, K), lambda j, i: (j, 0))
    else:
        b_spec = pl.BlockSpec((K, tn), lambda j, i: (0, j))
    in_specs = [pl.BlockSpec((tm, K), lambda j, i: (i, 0)), b_spec]
    for arr, kind in ep_in:
        if kind == "tile":
            assert arr.shape == (M, N)
            in_specs.append(pl.BlockSpec((tm, tn), lambda j, i: (i, j)))
        else:
            assert arr.shape == (1, N)
            in_specs.append(pl.BlockSpec((1, tn), lambda j, i: (0, j)))
    in_specs += [pl.BlockSpec(memory_space=pl.ANY)] * nd
    out_shape, out_specs = [], []
    for kind, dt in outs:
        if kind == "tile":
            out_shape.append(jax.ShapeDtypeStruct((M, N), dt))
            out_specs.append(pl.BlockSpec((tm, tn), lambda j, i: (i, j)))
        else:
            out_shape.append(jax.ShapeDtypeStruct((1, N), F32))
            out_specs.append(pl.BlockSpec((1, tn), lambda j, i: (0, j)))
    return pl.pallas_call(
        body, name=name, grid=(N // tn, M // tm), in_specs=in_specs, out_specs=out_specs, out_shape=out_shape,
        compiler_params=_params(2),
    )(a, b, *[arr for arr, _ in ep_in], *deps)


def _mm_tn(name, a, b, *, tm, tn, tk, a_fn=None, out_sm=None):
    T, Ka = a.shape
    N = b.shape[1]
    assert b.shape[0] == T and T % tk == 0 and Ka % tm == 0 and N % tn == 0
    nk = T // tk

    def body(a_ref, b_ref, o_ref, acc_ref):
        k = pl.program_id(2)

        @pl.when(k == 0)
        def _():
            acc_ref[...] = jnp.zeros_like(acc_ref)

        av = a_ref[...]
        if a_fn is not None:
            av = a_fn(av)
        acc_ref[...] += _dot(av.astype(BF), b_ref[...].astype(BF), 0, 0)

        @pl.when(k == nk - 1)
        def _():
            o_ref[...] = acc_ref[...].astype(BF)

    if out_sm is None:
        out_shape = jax.ShapeDtypeStruct((Ka, N), BF)
        out_spec = pl.BlockSpec((tm, tn), lambda i, j, k: (i, j))
    else:
        per = (N // out_sm) // tn
        assert per * tn * out_sm == N
        out_shape = jax.ShapeDtypeStruct((out_sm, Ka, N // out_sm), BF)
        out_spec = pl.BlockSpec((None, tm, tn), lambda i, j, k: (j // per, i, j % per))
    return pl.pallas_call(
        body, name=name, grid=(Ka // tm, N // tn, nk),
        in_specs=[pl.BlockSpec((tk, tm), lambda i, j, k: (k, i)), pl.BlockSpec((tk, tn), lambda i, j, k: (k, j))],
        out_specs=out_spec, out_shape=out_shape, scratch_shapes=[pltpu.VMEM((tm, tn), F32)],
        compiler_params=_params(3),
    )(a, b)


def _relu2(v):
    r = jnp.maximum(v.astype(F32), 0.0)
    return r * r


def _rms_bwd_ep(dh, x, g, dres):
    rstd = lax.rsqrt(jnp.mean(x * x, axis=-1, keepdims=True) + NORM_EPS)
    xh = x * rstd
    dxh = dh * g
    dx = rstd * (dxh - xh * jnp.mean(dxh * xh, axis=-1, keepdims=True))
    tot = dres + dx
    return tot, tot, jnp.sum(dh * xh, axis=0, keepdims=True), jnp.sum(tot, axis=0, keepdims=True)


def _rms_fwd(name, x, g, tm=512, deps=()):
    T, Dm = x.shape

    def body(x_ref, g_ref, *rest):
        o_ref = rest[-1]
        xv = x_ref[...]
        rstd = lax.rsqrt(jnp.mean(xv * xv, axis=-1, keepdims=True) + NORM_EPS)
        o_ref[...] = (xv * rstd * g_ref[...]).astype(BF)

    return pl.pallas_call(
        body, name=name, grid=(T // tm,),
        in_specs=[pl.BlockSpec((tm, Dm), lambda i: (i, 0)), pl.BlockSpec((1, Dm), lambda i: (0, 0))]
        + [pl.BlockSpec(memory_space=pl.ANY)] * len(deps),
        out_specs=pl.BlockSpec((tm, Dm), lambda i: (i, 0)), out_shape=jax.ShapeDtypeStruct((T, Dm), BF),
        compiler_params=_params(1),
    )(x, g, *deps)


HEAD_COLS = 128


def _two_term_dot(v, m):
    hi = v.astype(BF)
    lo = (v - hi.astype(F32)).astype(BF)
    return _dot(hi, m, 1, 0) + _dot(lo, m, 1, 0)


def _head_sum(v, select):
    sel, sel_t = select
    return _two_term_dot(_two_term_dot(v, sel), sel_t)


def _head_select(n):
    sel = (np.arange(n)[:, None] // HEAD_DIM == np.arange(HEAD_COLS)[None, :]).astype(np.float32)
    return jnp.asarray(sel, dtype=BF), jnp.asarray(sel.T, dtype=BF)


def _qk_norm_fwd(qkv, qg_t, kg_t, tm=256):
    T = qkv.shape[0]
    scale = 1.0 / math.sqrt(HEAD_DIM)

    def body(x_ref, qg_ref, kg_ref, sq_ref, sqt_ref, sk_ref, skt_ref, q_ref, k_ref, v_ref):
        q = x_ref[:, pl.ds(0, ATTN_DIM)]
        rq = lax.rsqrt(_head_sum(q * q, (sq_ref[...], sqt_ref[...])) * (1.0 / HEAD_DIM) + NORM_EPS)
        q_ref[...] = (q * rq * qg_ref[...] * scale).astype(BF)
        k = x_ref[:, pl.ds(ATTN_DIM, KV_DIM)]
        rk = lax.rsqrt(_head_sum(k * k, (sk_ref[...], skt_ref[...])) * (1.0 / HEAD_DIM) + NORM_EPS)
        k_ref[...] = (k * rk * kg_ref[...]).astype(BF)
        v_ref[...] = x_ref[:, pl.ds(ATTN_DIM + KV_DIM, KV_DIM)].astype(BF)

    full = lambda shape: pl.BlockSpec(shape, lambda i: (0, 0))
    return pl.pallas_call(
        body, name="qk_norm_fwd", grid=(T // tm,),
        in_specs=[pl.BlockSpec((tm, QKV_DIM), lambda i: (i, 0)), full((1, ATTN_DIM)), full((1, KV_DIM)),
                  full((ATTN_DIM, HEAD_COLS)), full((HEAD_COLS, ATTN_DIM)), full((KV_DIM, HEAD_COLS)), full((HEAD_COLS, KV_DIM))],
        out_specs=[pl.BlockSpec((tm, ATTN_DIM), lambda i: (i, 0)), pl.BlockSpec((tm, KV_DIM), lambda i: (i, 0)),
                   pl.BlockSpec((tm, KV_DIM), lambda i: (i, 0))],
        out_shape=[jax.ShapeDtypeStruct((T, ATTN_DIM), BF), jax.ShapeDtypeStruct((T, KV_DIM), BF),
                   jax.ShapeDtypeStruct((T, KV_DIM), BF)],
        compiler_params=_params(1),
    )(qkv, qg_t, kg_t, *_head_select(ATTN_DIM), *_head_select(KV_DIM))


def _qk_norm_bwd(qkv, dqn, dkn, dv, qg_t, kg_t, tm=256):
    T = qkv.shape[0]

    def body(x_ref, dq_ref, dk_ref, dv_ref, qg_ref, kg_ref, sq_ref, sqt_ref, sk_ref, skt_ref,
             o_ref, db_ref, dqg_ref, dkg_ref):
        i = pl.program_id(0)

        def one(x, dy, g, select):
            r = lax.rsqrt(_head_sum(x * x, select) * (1.0 / HEAD_DIM) + NORM_EPS)
            xh = x * r
            dxh = dy * g
            dx = r * (dxh - xh * (_head_sum(dxh * xh, select) * (1.0 / HEAD_DIM)))
            return dx, jnp.sum(dy * xh, axis=0, keepdims=True)

        dq, dqg = one(x_ref[:, pl.ds(0, ATTN_DIM)], dq_ref[...], qg_ref[...], (sq_ref[...], sqt_ref[...]))
        dk, dkg = one(x_ref[:, pl.ds(ATTN_DIM, KV_DIM)], dk_ref[...], kg_ref[...], (sk_ref[...], skt_ref[...]))
        dvv = dv_ref[...]
        o_ref[:, pl.ds(0, ATTN_DIM)] = dq.astype(BF)
        o_ref[:, pl.ds(ATTN_DIM, KV_DIM)] = dk.astype(BF)
        o_ref[:, pl.ds(ATTN_DIM + KV_DIM, KV_DIM)] = dvv.astype(BF)
        sq, sk, sv = (jnp.sum(t, axis=0, keepdims=True) for t in (dq, dk, dvv))

        @pl.when(i == 0)
        def _():
            db_ref[:, pl.ds(0, ATTN_DIM)] = sq
            db_ref[:, pl.ds(ATTN_DIM, KV_DIM)] = sk
            db_ref[:, pl.ds(ATTN_DIM + KV_DIM, KV_DIM)] = sv
            dqg_ref[...] = dqg
            dkg_ref[...] = dkg

        @pl.when(i > 0)
        def _():
            db_ref[:, pl.ds(0, ATTN_DIM)] += sq
            db_ref[:, pl.ds(ATTN_DIM, KV_DIM)] += sk
            db_ref[:, pl.ds(ATTN_DIM + KV_DIM, KV_DIM)] += sv
            dqg_ref[...] += dqg
            dkg_ref[...] += dkg

    full = lambda shape: pl.BlockSpec(shape, lambda i: (0, 0))
    row = lambda n: pl.BlockSpec((tm, n), lambda i: (i, 0))
    return pl.pallas_call(
        body, name="qk_norm_bwd", grid=(T // tm,),
        in_specs=[row(QKV_DIM), row(ATTN_DIM), row(KV_DIM), row(KV_DIM), full((1, ATTN_DIM)), full((1, KV_DIM)),
                  full((ATTN_DIM, HEAD_COLS)), full((HEAD_COLS, ATTN_DIM)), full((KV_DIM, HEAD_COLS)), full((HEAD_COLS, KV_DIM))],
        out_specs=[row(QKV_DIM), full((1, QKV_DIM)), full((1, ATTN_DIM)), full((1, KV_DIM))],
        out_shape=[jax.ShapeDtypeStruct((T, QKV_DIM), BF), jax.ShapeDtypeStruct((1, QKV_DIM), F32),
                   jax.ShapeDtypeStruct((1, ATTN_DIM), F32), jax.ShapeDtypeStruct((1, KV_DIM), F32)],
        compiler_params=_params(1),
    )(qkv, dqn, dkn, dv, qg_t, kg_t, *_head_select(ATTN_DIM), *_head_select(KV_DIM))


ROWS = 64
COLS = 128


SUBLANES = 8
FIRST_TAP = HALO - (CONV_W - 1)


def _glu(a, g):
    return a.astype(F32) * jax.nn.sigmoid(g.astype(F32))


def _shifted(xe, s):
    return xe if s == 0 else pltpu.roll(xe, ROWS + HALO - s, axis=0)


def _conv_fwd(u, dw_pad, dw_b, ln_g, ln_b, tm=256):
    T = u.shape[0]
    Dm = D_MODEL
    hpt = tm // HALO

    def body(ac_ref, gc_ref, ap_ref, gp_ref, w_ref, wb_ref, lg_ref, lb_ref, cv_ref, s_ref, ext):
        i = pl.program_id(0)
        ext[pl.ds(0, HALO), :] = jnp.where(i > 0, _glu(ap_ref[...], gp_ref[...]), 0.0)
        ext[pl.ds(HALO, tm), :] = _glu(ac_ref[...], gc_ref[...])

        def rows(r, carry):
            r0 = pl.multiple_of(r * ROWS, ROWS)
            for c in range(Dm // COLS):
                cs = pl.ds(c * COLS, COLS)
                xe = ext[pl.ds(r0, ROWS + HALO), cs]
                acc = jnp.zeros((ROWS, COLS), F32)
                for s in range(SUBLANES):
                    xs = _shifted(xe, s)
                    for j in range(CONV_W):
                        off = FIRST_TAP + j
                        if off % SUBLANES == s:
                            acc = acc + xs[off - s:off - s + ROWS, :] * w_ref[pl.ds(j, 1), cs]
                cv_ref[pl.ds(r0, ROWS), cs] = acc + wb_ref[:, cs]
            return carry

        lax.fori_loop(0, tm // ROWS, rows, 0)
        cv = cv_ref[...]
        xc = cv - jnp.mean(cv, axis=-1, keepdims=True)
        y = xc * lax.rsqrt(jnp.mean(xc * xc, axis=-1, keepdims=True) + NORM_EPS) * lg_ref[...] + lb_ref[...]
        s_ref[...] = (y * jax.nn.sigmoid(y)).astype(BF)

    full = lambda shape: pl.BlockSpec(shape, lambda i: (0, 0))
    return pl.pallas_call(
        body, name="conv_fwd", grid=(T // tm,),
        in_specs=[pl.BlockSpec((tm, Dm), lambda i: (i, 0)), pl.BlockSpec((tm, Dm), lambda i: (i, 1)),
                  pl.BlockSpec((HALO, Dm), lambda i: (jnp.maximum(i * hpt - 1, 0), 0)),
                  pl.BlockSpec((HALO, Dm), lambda i: (jnp.maximum(i * hpt - 1, 0), 1)),
                  full((HALO, Dm)), full((1, Dm)), full((1, Dm)), full((1, Dm))],
        out_specs=[pl.BlockSpec((tm, Dm), lambda i: (i, 0)), pl.BlockSpec((tm, Dm), lambda i: (i, 0))],
        out_shape=[jax.ShapeDtypeStruct((T, Dm), F32), jax.ShapeDtypeStruct((T, Dm), BF)],
        scratch_shapes=[pltpu.VMEM((tm + HALO, Dm), F32)],
        compiler_params=_params(1),
    )(u, u, u, u, dw_pad, dw_b, ln_g, ln_b)


def _ln_silu_bwd_ep(ds, cv, lg, lb):
    xc = cv - jnp.mean(cv, axis=-1, keepdims=True)
    rstd = lax.rsqrt(jnp.mean(xc * xc, axis=-1, keepdims=True) + NORM_EPS)
    xh = xc * rstd
    y = xh * lg + lb
    sg = jax.nn.sigmoid(y)
    dy = ds * (sg * (1.0 + y * (1.0 - sg)))
    dxh = dy * lg
    dcv = rstd * (dxh - jnp.mean(dxh, axis=-1, keepdims=True) - xh * jnp.mean(dxh * xh, axis=-1, keepdims=True))
    return (dcv, jnp.sum(dy * xh, axis=0, keepdims=True), jnp.sum(dy, axis=0, keepdims=True),
            jnp.sum(dcv, axis=0, keepdims=True))


def _conv_bwd(u, dcv, dw_pad, tm=256):
    T = u.shape[0]
    Dm = D_MODEL
    hpt = tm // HALO
    last = T // HALO - 1
    nt = T // tm

    def body(ac_ref, gc_ref, ap_ref, gp_ref, dc_ref, dn_ref, w_ref, du_ref, db_ref, dw_ref, ext_g, ext_d):
        i = pl.program_id(0)
        ext_g[pl.ds(0, HALO), :] = jnp.where(i > 0, _glu(ap_ref[...], gp_ref[...]), 0.0)
        ext_g[pl.ds(HALO, tm), :] = _glu(ac_ref[...], gc_ref[...])
        ext_d[pl.ds(0, tm), :] = dc_ref[...]
        ext_d[pl.ds(tm, HALO), :] = jnp.where(i < nt - 1, dn_ref[...], 0.0)

        @pl.when(i == 0)
        def _():
            db_ref[...] = jnp.zeros_like(db_ref)
            dw_ref[...] = jnp.zeros_like(dw_ref)

        def rows(r, carry):
            r0 = pl.multiple_of(r * ROWS, ROWS)
            rs = pl.ds(r0, ROWS)
            for c in range(Dm // COLS):
                cs = pl.ds(c * COLS, COLS)
                cs2 = pl.ds(Dm + c * COLS, COLS)
                de = ext_d[pl.ds(r0, ROWS + HALO), cs]
                ge = ext_g[pl.ds(r0, ROWS + HALO), cs]
                dcur = de[0:ROWS, :]
                acc = jnp.zeros((ROWS, COLS), F32)
                for s in range(SUBLANES):
                    ds_, gs_ = _shifted(de, s), _shifted(ge, s)
                    for j in range(CONV_W):
                        off = CONV_W - 1 - j
                        if off % SUBLANES == s:
                            acc = acc + ds_[off - s:off - s + ROWS, :] * w_ref[pl.ds(j, 1), cs]
                        goff = FIRST_TAP + j
                        if goff % SUBLANES == s:
                            prod = dcur * gs_[goff - s:goff - s + ROWS, :]
                            dw_ref[j, :, cs] += jnp.sum(prod.reshape(ROWS // SUBLANES, SUBLANES, COLS), axis=0)
                a = ac_ref[rs, cs].astype(F32)
                sg = jax.nn.sigmoid(gc_ref[rs, cs].astype(F32))
                da = acc * sg
                dg = acc * a * sg * (1.0 - sg)
                du_ref[rs, cs] = da.astype(BF)
                du_ref[rs, cs2] = dg.astype(BF)
                db_ref[:, cs] += jnp.sum(da, axis=0, keepdims=True)
                db_ref[:, cs2] += jnp.sum(dg, axis=0, keepdims=True)
            return carry

        lax.fori_loop(0, tm // ROWS, rows, 0)

    return pl.pallas_call(
        body, name="conv_bwd", grid=(nt,),
        in_specs=[pl.BlockSpec((tm, Dm), lambda i: (i, 0)), pl.BlockSpec((tm, Dm), lambda i: (i, 1)),
                  pl.BlockSpec((HALO, Dm), lambda i: (jnp.maximum(i * hpt - 1, 0), 0)),
                  pl.BlockSpec((HALO, Dm), lambda i: (jnp.maximum(i * hpt - 1, 0), 1)),
                  pl.BlockSpec((tm, Dm), lambda i: (i, 0)),
                  pl.BlockSpec((HALO, Dm), lambda i: (jnp.minimum((i + 1) * hpt, last), 0)),
                  pl.BlockSpec((HALO, Dm), lambda i: (0, 0))],
        out_specs=[pl.BlockSpec((tm, 2 * Dm), lambda i: (i, 0)), pl.BlockSpec((1, 2 * Dm), lambda i: (0, 0)),
                   pl.BlockSpec((HALO, 8, Dm), lambda i: (0, 0, 0))],
        out_shape=[jax.ShapeDtypeStruct((T, 2 * Dm), BF), jax.ShapeDtypeStruct((1, 2 * Dm), F32),
                   jax.ShapeDtypeStruct((HALO, 8, Dm), F32)],
        scratch_shapes=[pltpu.VMEM((tm + HALO, Dm), F32), pltpu.VMEM((tm + HALO, Dm), F32)],
        compiler_params=_params(1),
    )(u, u, u, u, dcv, dcv, dw_pad)


def _bucket_table():
    q_loc = np.arange(BLOCK)[:, None]
    k_loc = np.arange(2 * BLOCK)[None, :]
    dist = q_loc + BLOCK - k_loc
    n = np.maximum(dist, 0)
    max_exact = REL_BUCKETS // 2
    large = max_exact + (np.log(np.maximum(n, 1).astype(np.float32) / max_exact)
                         / math.log(REL_MAX_DIST / max_exact) * (REL_BUCKETS - max_exact)).astype(np.int32)
    large = np.minimum(large, REL_BUCKETS - 1)
    bucket = np.where(n < max_exact, n, large).astype(np.int32)
    return jnp.asarray(np.where((dist >= 0) & (dist < BLOCK), bucket, -1).astype(np.int32))


def _bias_table(rel_bias, bucket):
    def body(rb_ref, bk_ref, o_ref):
        bk = bk_ref[...]
        for h in range(N_HEADS):
            acc = jnp.full((BLOCK, 2 * BLOCK), NEG_INF, F32)
            for b in range(REL_BUCKETS):
                acc = jnp.where(bk == b, rb_ref[b, h], acc)
            o_ref[h] = acc

    return pl.pallas_call(
        body, name="bias_table", out_shape=jax.ShapeDtypeStruct((N_HEADS, BLOCK, 2 * BLOCK), F32),
        in_specs=[pl.BlockSpec(memory_space=pltpu.SMEM), pl.BlockSpec(memory_space=pltpu.VMEM)],
        out_specs=pl.BlockSpec(memory_space=pltpu.VMEM),
    )(rel_bias, bucket)


def _bias_grad(dbias, bucket):
    def body(db_ref, bk_ref, o_ref):
        bk = bk_ref[...]
        for b in range(REL_BUCKETS):
            sel = bk == b
            for h in range(N_HEADS):
                o_ref[b, h] = jnp.sum(jnp.where(sel, db_ref[h], 0.0))

    return pl.pallas_call(
        body, name="bias_grad", out_shape=jax.ShapeDtypeStruct((REL_BUCKETS, N_HEADS), F32),
        in_specs=[pl.BlockSpec(memory_space=pltpu.VMEM), pl.BlockSpec(memory_space=pltpu.VMEM)],
        out_specs=pl.BlockSpec(memory_space=pltpu.SMEM),
    )(dbias, bucket)


GROUP_ROWS = GROUP * BLOCK


def _head_probs(qk, bias_h, sink, first):
    s = jnp.where(first, NEG_INF, qk + bias_h)
    m = jnp.maximum(jnp.max(s, axis=-1, keepdims=True), sink)
    p = jnp.exp(s - m)
    ps = jnp.exp(sink - m)
    inv = 1.0 / (jnp.sum(p, axis=-1, keepdims=True) + ps)
    return p * inv, ps * inv


def _band(prev_ref, cur_ref, g):
    hs = pl.ds(g * HEAD_DIM, HEAD_DIM)
    return jnp.concatenate([prev_ref[:, hs], cur_ref[:, hs]], axis=0)


def _stack_heads(ref, g):
    return jnp.concatenate([ref[:, pl.ds((g * GROUP + hh) * HEAD_DIM, HEAD_DIM)] for hh in range(GROUP)], axis=0)


def _unstack_heads(ref, g, stacked, dtype):
    for hh in range(GROUP):
        ref[:, pl.ds((g * GROUP + hh) * HEAD_DIM, HEAD_DIM)] = stacked[hh * BLOCK:(hh + 1) * BLOCK, :].astype(dtype)


def _first_mask(n):
    col = lax.broadcasted_iota(jnp.int32, (1, 2 * BLOCK), 1)
    return jnp.logical_and(n == 0, col < BLOCK)


def _head_rows(hh):
    return pl.ds(hh * BLOCK, BLOCK)


def _attn_fwd(qn, kn, vv, bias, sinks):
    T = qn.shape[0]
    nb = T // BLOCK

    def body(sk_ref, q_ref, kc_ref, kp_ref, vc_ref, vp_ref, b_ref, o_ref, qk_buf, p_buf):
        first = _first_mask(pl.program_id(0))
        for g in range(N_KV):
            k = _band(kp_ref, kc_ref, g)
            v = _band(vp_ref, vc_ref, g)
            qk_buf[g] = _dot(_stack_heads(q_ref, g), k, 1, 1)
            for hh in range(GROUP):
                h = g * GROUP + hh
                pn, _ = _head_probs(qk_buf[g, _head_rows(hh), :], b_ref[h], sk_ref[h], first)
                p_buf[g, _head_rows(hh), :] = pn.astype(BF)
            _unstack_heads(o_ref, g, _dot(p_buf[g], v, 1, 0), BF)

    cur = lambda n: (n, 0)
    prev = lambda n: (jnp.maximum(n - 1, 0), 0)
    return pl.pallas_call(
        body, name="attn_fwd", grid=(nb,),
        in_specs=[pl.BlockSpec(memory_space=pltpu.SMEM), pl.BlockSpec((BLOCK, ATTN_DIM), cur),
                  pl.BlockSpec((BLOCK, KV_DIM), cur), pl.BlockSpec((BLOCK, KV_DIM), prev),
                  pl.BlockSpec((BLOCK, KV_DIM), cur), pl.BlockSpec((BLOCK, KV_DIM), prev),
                  pl.BlockSpec((N_HEADS, BLOCK, 2 * BLOCK), lambda n: (0, 0, 0))],
        out_specs=pl.BlockSpec((BLOCK, ATTN_DIM), cur), out_shape=jax.ShapeDtypeStruct((T, ATTN_DIM), BF),
        scratch_shapes=[pltpu.VMEM((N_KV, GROUP_ROWS, 2 * BLOCK), F32), pltpu.VMEM((N_KV, GROUP_ROWS, 2 * BLOCK), BF)],
        compiler_params=_params(1),
    )(sinks, qn, kn, kn, vv, vv, bias)


def _attn_bwd(qn, kn, vv, bias, sinks, do):
    T = qn.shape[0]
    nb = T // BLOCK
    scale = 1.0 / math.sqrt(HEAD_DIM)

    def body(sk_ref, q_ref, kc_ref, kp_ref, vc_ref, vp_ref, b_ref, do_ref,
             dq_ref, dk_ref, dv_ref, db_ref, dsk_ref, dk_full, dv_full, dk_carry, dv_carry, qk_buf, dp_buf, p_buf, ds_buf):
        n = pl.program_id(0)

        @pl.when(n == 0)
        def _():
            db_ref[...] = jnp.zeros_like(db_ref)
            dk_carry[...] = jnp.zeros_like(dk_carry)
            dv_carry[...] = jnp.zeros_like(dv_carry)
            for h in range(N_HEADS):
                dsk_ref[h] = 0.0

        @pl.when(n < nb)
        def _():
            first = _first_mask(n)
            for g in range(N_KV):
                k = _band(kp_ref, kc_ref, g)
                v = _band(vp_ref, vc_ref, g)
                q = _stack_heads(q_ref, g)
                dout = _stack_heads(do_ref, g)
                qk_buf[g] = _dot(q, k, 1, 1)
                dp_buf[g] = _dot(dout, v, 1, 1)
                for hh in range(GROUP):
                    h = g * GROUP + hh
                    rows = _head_rows(hh)
                    pn, psink = _head_probs(qk_buf[g, rows, :], b_ref[h], sk_ref[h], first)
                    dp = dp_buf[g, rows, :]
                    delta = jnp.sum(pn * dp, axis=-1, keepdims=True)
                    ds = pn * (dp - delta)
                    dsk_ref[h] += -jnp.sum(psink * delta)
                    db_ref[h] += ds
                    ds_buf[g, rows, :] = ds.astype(BF)
                    p_buf[g, rows, :] = pn.astype(BF)
                dsb = ds_buf[g]
                _unstack_heads(dq_ref, g, _dot(dsb, k, 1, 0) * scale, F32)
                gs = pl.ds(g * HEAD_DIM, HEAD_DIM)
                dk_full[:, gs] = _dot(dsb, q, 0, 0)
                dv_full[:, gs] = _dot(p_buf[g], dout, 0, 0)

        @pl.when(n == nb)
        def _():
            dk_full[...] = jnp.zeros_like(dk_full)
            dv_full[...] = jnp.zeros_like(dv_full)

        dk_ref[...] = dk_carry[...] + dk_full[pl.ds(0, BLOCK), :]
        dv_ref[...] = dv_carry[...] + dv_full[pl.ds(0, BLOCK), :]
        dk_carry[...] = dk_full[pl.ds(BLOCK, BLOCK), :]
        dv_carry[...] = dv_full[pl.ds(BLOCK, BLOCK), :]

    cur = lambda n: (jnp.minimum(n, nb - 1), 0)
    prev = lambda n: (jnp.maximum(jnp.minimum(n, nb - 1) - 1, 0), 0)
    out_kv = lambda n: (jnp.maximum(n - 1, 0), 0)
    return pl.pallas_call(
        body, name="attn_bwd", grid=(nb + 1,),
        in_specs=[pl.BlockSpec(memory_space=pltpu.SMEM), pl.BlockSpec((BLOCK, ATTN_DIM), cur),
                  pl.BlockSpec((BLOCK, KV_DIM), cur), pl.BlockSpec((BLOCK, KV_DIM), prev),
                  pl.BlockSpec((BLOCK, KV_DIM), cur), pl.BlockSpec((BLOCK, KV_DIM), prev),
                  pl.BlockSpec((N_HEADS, BLOCK, 2 * BLOCK), lambda n: (0, 0, 0)),
                  pl.BlockSpec((BLOCK, ATTN_DIM), cur)],
        out_specs=[pl.BlockSpec((BLOCK, ATTN_DIM), cur), pl.BlockSpec((BLOCK, KV_DIM), out_kv),
                   pl.BlockSpec((BLOCK, KV_DIM), out_kv),
                   pl.BlockSpec((N_HEADS, BLOCK, 2 * BLOCK), lambda n: (0, 0, 0)),
                   pl.BlockSpec(memory_space=pltpu.SMEM)],
        out_shape=[jax.ShapeDtypeStruct((T, ATTN_DIM), F32), jax.ShapeDtypeStruct((T, KV_DIM), F32),
                   jax.ShapeDtypeStruct((T, KV_DIM), F32),
                   jax.ShapeDtypeStruct((N_HEADS, BLOCK, 2 * BLOCK), F32), jax.ShapeDtypeStruct((N_HEADS,), F32)],
        scratch_shapes=[pltpu.VMEM((2 * BLOCK, KV_DIM), F32), pltpu.VMEM((2 * BLOCK, KV_DIM), F32),
                        pltpu.VMEM((BLOCK, KV_DIM), F32), pltpu.VMEM((BLOCK, KV_DIM), F32),
                        pltpu.VMEM((N_KV, GROUP_ROWS, 2 * BLOCK), F32), pltpu.VMEM((N_KV, GROUP_ROWS, 2 * BLOCK), F32),
                        pltpu.VMEM((N_KV, GROUP_ROWS, 2 * BLOCK), BF), pltpu.VMEM((N_KV, GROUP_ROWS, 2 * BLOCK), BF)],
        compiler_params=_params(1),
    )(sinks, qn, kn, kn, vv, vv, bias, do)


def _coords():
    return lax.axis_index("x"), lax.axis_index("y"), lax.axis_index("c")


def _sum8(name, blocks):
    def body(b_ref, o_ref):
        tot = b_ref[0]
        for d in range(1, 8):
            tot = tot + b_ref[d]
        o_ref[...] = tot

    return pl.pallas_call(body, name=name, out_shape=jax.ShapeDtypeStruct(blocks.shape[1:], F32))(blocks)


HBM_SPEC = pl.BlockSpec(memory_space=pltpu.HBM)
SEM_SPEC = pl.BlockSpec(memory_space=pltpu.SEMAPHORE)
ANY_SPEC = pl.BlockSpec(memory_space=pl.ANY)
DATAFLOW = pltpu.SideEffectType.DATAFLOW_SIDE_EFFECTING


OTHER_CHIPS = (4, 2, 6)
ALL_OTHERS = (1, 2, 3, 4, 5, 6, 7)


def _slot(x, y, c, peers):
    return 2 * x + y if peers is OTHER_CHIPS else 4 * x + 2 * y + c


def _slot_copy(land, sems, idx, x, y, c, k, peers, arriving):
    send_sems, recv_sems = sems
    px, py, pc = x ^ (k >> 2), y ^ ((k >> 1) & 1), c ^ (k & 1)
    mine = _slot(x, y, c, peers)
    dst = _slot(px, py, pc, peers) if arriving else mine
    return pltpu.make_async_remote_copy(src_ref=land.at[mine], dst_ref=land.at[dst], send_sem=send_sems.at[idx],
                                        recv_sem=recv_sems.at[idx], device_id=(px, py, pc), device_id_type=MESH)


def _gather_start(name, stacks, groups, peers, after):
    n = len(stacks)
    ng = len(groups)
    np_ = len(peers)
    after = tuple(after)

    def body(*refs):
        lands = refs[:n]
        first = n + len(after)
        sems = [(refs[first + 2 * g], refs[first + 2 * g + 1]) for g in range(ng)]
        token = refs[-1]
        x, y, c = _coords()
        for g, members in enumerate(groups):
            for i, t in enumerate(members):
                for j, k in enumerate(peers):
                    _slot_copy(lands[t], sems[g], np_ * i + j, x, y, c, k, peers, arriving=False).start()
        token[...] = jnp.zeros_like(token)

    out_shape = []
    for members in groups:
        out_shape += [pltpu.SemaphoreType.DMA((np_ * len(members),))] * 2
    out_shape += [pltpu.HBM(w.shape, w.dtype) for w in stacks]
    out_shape.append(jax.ShapeDtypeStruct((8, 128), F32))
    res = pl.pallas_call(
        body, name=name, out_shape=out_shape, in_specs=[HBM_SPEC] * n + [ANY_SPEC] * len(after),
        out_specs=[SEM_SPEC] * (2 * ng) + [HBM_SPEC] * n + [pl.BlockSpec(memory_space=pltpu.VMEM)],
        input_output_aliases={t: 2 * ng + t for t in range(n)},
        compiler_params=pltpu.CompilerParams(has_side_effects=DATAFLOW),
    )(*[pltpu.with_memory_space_constraint(w, pltpu.HBM) for w in stacks], *after)
    sems = [(res[2 * g], res[2 * g + 1]) for g in range(ng)]
    return sems, list(res[2 * ng:2 * ng + n]), res[-1]


def _gather_wait(name, stacks, sems, peers, after):
    n = len(stacks)
    after = tuple(after)

    def body(*refs):
        lands = refs[:n]
        group_sems = (refs[n], refs[n + 1])
        x, y, c = _coords()
        for i in range(n):
            for j, k in enumerate(peers):
                cp = _slot_copy(lands[i], group_sems, len(peers) * i + j, x, y, c, k, peers, arriving=True)
                cp.wait_send()
                cp.wait_recv()

    return pl.pallas_call(
        body, name=name, out_shape=[pltpu.HBM(w.shape, w.dtype) for w in stacks],
        in_specs=[HBM_SPEC] * n + [SEM_SPEC, SEM_SPEC] + [ANY_SPEC] * len(after), out_specs=[HBM_SPEC] * n,
        input_output_aliases={t: t for t in range(n)},
        compiler_params=pltpu.CompilerParams(has_side_effects=DATAFLOW),
    )(*stacks, sems[0], sems[1], *after)


N_PEERS = 7


def _peer(x, y, c, k):
    return x ^ (k >> 2), y ^ ((k >> 1) & 1), c ^ (k & 1)


def _reduce_copy(grad, land, sems, idx, x, y, c, k):
    px, py, pc = _peer(x, y, c, k)
    rh = grad.shape[1] // 2
    return pltpu.make_async_remote_copy(src_ref=grad.at[2 * px + py, pl.ds(pc * rh, rh), :], dst_ref=land.at[k - 1],
                                        send_sem=sems[0].at[idx], recv_sem=sems[1].at[idx], device_id=(px, py, pc),
                                        device_id_type=MESH)


def _reduce_start(name, grads):
    n = len(grads)

    def body(*refs):
        src, lands, sems, token = refs[:n], refs[n:2 * n], (refs[2 * n], refs[2 * n + 1]), refs[-1]
        x, y, c = _coords()
        for t in range(n):
            for k in range(1, N_PEERS + 1):
                _reduce_copy(src[t], lands[t], sems, N_PEERS * t + k - 1, x, y, c, k).start()
        token[...] = jnp.zeros_like(token)

    lands = [lax.empty((N_PEERS, g.shape[1] // 2, g.shape[2]), g.dtype) for g in grads]
    out_shape = [pltpu.SemaphoreType.DMA((N_PEERS * n,))] * 2
    out_shape += [pltpu.HBM(a.shape, a.dtype) for a in list(grads) + lands]
    out_shape.append(jax.ShapeDtypeStruct((8, 128), F32))
    res = pl.pallas_call(
        body, name=name, out_shape=out_shape, in_specs=[HBM_SPEC] * (2 * n),
        out_specs=[SEM_SPEC] * 2 + [HBM_SPEC] * (2 * n) + [pl.BlockSpec(memory_space=pltpu.VMEM)],
        input_output_aliases={t: 2 + t for t in range(2 * n)},
        compiler_params=pltpu.CompilerParams(has_side_effects=DATAFLOW),
    )(*[pltpu.with_memory_space_constraint(a, pltpu.HBM) for a in list(grads) + lands])
    return (res[0], res[1]), list(res[2:2 + n]), list(res[2 + n:2 + 2 * n]), res[-1]


def _reduce_wait(name, grads, lands, sems, after):
    n = len(grads)
    after = tuple(after)

    def body(*refs):
        src, dst, group_sems = refs[:n], refs[n:2 * n], (refs[2 * n], refs[2 * n + 1])
        x, y, c = _coords()
        for t in range(n):
            for k in range(1, N_PEERS + 1):
                cp = _reduce_copy(src[t], dst[t], group_sems, N_PEERS * t + k - 1, x, y, c, k)
                cp.wait_send()
                cp.wait_recv()

    res = pl.pallas_call(
        body, name=name, out_shape=[pltpu.HBM(a.shape, a.dtype) for a in list(grads) + list(lands)],
        in_specs=[HBM_SPEC] * (2 * n) + [SEM_SPEC, SEM_SPEC] + [ANY_SPEC] * len(after), out_specs=[HBM_SPEC] * (2 * n),
        input_output_aliases={t: t for t in range(2 * n)},
        compiler_params=pltpu.CompilerParams(has_side_effects=DATAFLOW),
    )(*grads, *lands, sems[0], sems[1], *after)
    return list(res[:n]), list(res[n:])


def _join_halves(name, halves, deps=()):
    n = len(halves)

    def body(*refs):
        src, dst = refs[:n], refs[n + len(deps):2 * n + len(deps)]
        send_sems, recv_sems = refs[-2:]
        x, y, c = _coords()
        cps = []
        for t in range(n):
            cp = pltpu.make_async_remote_copy(src_ref=src[t], dst_ref=dst[t], send_sem=send_sems.at[t],
                                              recv_sem=recv_sems.at[t], device_id=(x, y, 1 - c), device_id_type=MESH)
            cp.start()
            cps.append(cp)
        for cp in cps:
            cp.wait()

    anyspec = pl.BlockSpec(memory_space=pl.ANY)
    return pl.pallas_call(
        body, name=name, out_shape=[jax.ShapeDtypeStruct(h.shape, h.dtype) for h in halves],
        in_specs=[anyspec] * (n + len(deps)), out_specs=[anyspec] * n,
        scratch_shapes=[pltpu.SemaphoreType.DMA((n,)), pltpu.SemaphoreType.DMA((n,))],
    )(*halves, *deps)


def _row_block(rows):
    for rb in (512, 256, 128, 64, 32, 16):
        if rows % rb == 0:
            return rb
    raise ValueError(rows)


def _sum_devices(name, grad, land, place):
    S, R, C = grad.shape
    rh = R // 2
    rb = _row_block(rh)
    nbh = rh // rb

    def body(place_ref, g_ref, l_ref, o_ref):
        tot = g_ref[...].astype(F32)
        for k in range(N_PEERS):
            tot = tot + l_ref[k].astype(F32)
        o_ref[...] = tot

    return pl.pallas_call(
        body, name=name,
        grid_spec=pltpu.PrefetchScalarGridSpec(
            num_scalar_prefetch=1, grid=(nbh,),
            in_specs=[pl.BlockSpec((None, rb, C), lambda r, place: (place[0], place[1] * nbh + r, 0)),
                      pl.BlockSpec((N_PEERS, rb, C), lambda r, place: (0, r, 0))],
            out_specs=pl.BlockSpec((rb, C), lambda r, place: (r, 0))),
        out_shape=jax.ShapeDtypeStruct((rh, C), F32), compiler_params=_params(1),
    )(place, grad, land)


def _adamw_math(w, g, m, v):
    m2 = ADAM_B1 * m + (1.0 - ADAM_B1) * g
    v2 = ADAM_B2 * v + (1.0 - ADAM_B2) * (g * g)
    m_hat = m2 / (1.0 - ADAM_B1 ** ADAM_STEP)
    v_hat = v2 / (1.0 - ADAM_B2 ** ADAM_STEP)
    delta = -ADAM_LR * (m_hat / (jnp.sqrt(v_hat) + ADAM_EPS) + ADAM_WD * w)
    return delta, m2, v2


def _adamw(name, w, m, v, gs):
    L, R, C = w.shape
    Rh = R // 2
    rb = _row_block(Rh)
    nbh = Rh // rb
    assert len(gs) == L

    def body(core_ref, w_ref, m_ref, v_ref, *rest):
        g_refs, (go_ref, d_ref, m2_ref, v2_ref) = rest[:2 * L], rest[2 * L:]
        layer, half = pl.program_id(0), pl.program_id(1)
        mine = half == core_ref[0]
        g = jnp.where(mine, g_refs[0][...], g_refs[1][...])
        for t in range(1, L):
            g = jnp.where(layer == t, jnp.where(mine, g_refs[2 * t][...], g_refs[2 * t + 1][...]), g)
        delta, m2, v2 = _adamw_math(w_ref[...], g, m_ref[...], v_ref[...])
        go_ref[...] = g
        d_ref[...] = delta
        m2_ref[...] = m2
        v2_ref[...] = v2

    wspec = pl.BlockSpec((None, rb, C), lambda l, h, r, core: (l, h * nbh + r, 0))
    gspec = pl.BlockSpec((rb, C), lambda l, h, r, core: (r, 0))
    return pl.pallas_call(
        body, name=name,
        grid_spec=pltpu.PrefetchScalarGridSpec(num_scalar_prefetch=1, grid=(L, 2, nbh),
                                               in_specs=[wspec] * 3 + [gspec] * (2 * L), out_specs=[wspec] * 4),
        out_shape=[jax.ShapeDtypeStruct((L, R, C), F32)] * 4, compiler_params=_params(3),
    )(lax.axis_index("c").astype(jnp.int32).reshape(1), w, m, v, *[g for pair in gs for g in pair])


def _adamw_small(ws, gs, ms, vs):
    n = len(ws)

    def body(*refs):
        w_refs, g_refs, m_refs, v_refs = (refs[k * n:(k + 1) * n] for k in range(4))
        d_refs, m2_refs, v2_refs = (refs[(4 + k) * n:(5 + k) * n] for k in range(3))
        for t in range(n):
            delta, m2, v2 = _adamw_math(w_refs[t][...], g_refs[t][...], m_refs[t][...], v_refs[t][...])
            d_refs[t][...] = delta
            m2_refs[t][...] = m2
            v2_refs[t][...] = v2

    res = pl.pallas_call(body, name="adamw_small", out_shape=[jax.ShapeDtypeStruct(w.shape, F32) for w in ws] * 3)(
        *ws, *gs, *ms, *vs)
    return res[:n], res[n:2 * n], res[2 * n:]


def _packed_rows(shape):
    c = shape[-1]
    return (int(np.prod(shape)) // c) * -(-c // LANES)


def _pack(arrays):
    total = sum(_packed_rows(a.shape) for a in arrays)
    total += -total % 8
    buf, r0 = None, 0
    for a in arrays:
        a = a.astype(F32).reshape(-1, a.shape[-1])
        r, c = a.shape
        k = -(-c // LANES)
        a = jnp.pad(a, ((0, 0), (0, k * LANES - c))).reshape(r * k, LANES)
        a = jnp.pad(a, ((r0, total - r0 - r * k), (0, 0)))
        buf = a if buf is None else buf + a
        r0 += r * k
    return buf


def _unpack(buf, shapes):
    out, r0 = [], 0
    for shp in shapes:
        c = shp[-1]
        rows = _packed_rows(shp)
        out.append(buf[r0:r0 + rows].reshape(-1, -(-c // LANES) * LANES)[:, :c].reshape(shp))
        r0 += rows
    return out


def _rms(x, g):
    return x * lax.rsqrt(jnp.mean(x * x, axis=-1, keepdims=True) + NORM_EPS) * g


def _residual_norm_ep(acc, *rest):
    *bias, res, gain = rest
    x = acc + res + (bias[0] if bias else 0.0)
    return x, _rms(x, gain)


RESIDUAL_NORM_OUTS = (("tile", F32), ("tile", BF))


def _mlp_up(tag, h, w_up_sm):
    (up,) = _mm(f"mlp{tag}_up", h, w_up_sm, nt=False, b_sm=True, tm=4096, tn=1024, rows=256,
                ep_fn=lambda acc: (acc,), outs=(("tile", BF),))
    return up


RMS_BWD_OUTS = (("tile", F32), ("tile", BF), ("colsum", F32), ("colsum", F32))


def _mlp_bwd(tag, dy, dy_bf, x, g, up, w_up_sm, w_down):
    (dup,) = _mm(f"mlp{tag}_dup", dy_bf, w_down, nt=True, tm=2048, tn=1024, rows=256, ep_in=((up, "tile"),),
                 ep_fn=lambda acc, u: (acc * (2.0 * jnp.maximum(u.astype(F32), 0.0)),), outs=(("tile", BF),))
    dx, dx_bf, dg, dx_sum = _mm(f"mlp{tag}_dx", dup, w_up_sm, nt=True, b_sm=True, tm=512, tn=1024, rows=256,
                                ep_in=((x, "tile"), (g, "row"), (dy, "tile")), ep_fn=_rms_bwd_ep, outs=RMS_BWD_OUTS)
    return dx, dx_bf, dg, dx_sum, dup


class _Reduction:
    def __init__(self, tag, grads, place):
        self.tag, self.place = tag, place
        self.sems, self.grads, self.lands, self.token = _reduce_start(f"reduce_start_{tag}", grads)

    def finish(self, after):
        grads, lands = _reduce_wait(f"reduce_wait_{self.tag}", self.grads, self.lands, self.sems, after)
        return [_sum_devices(f"reduce_sum_{self.tag}{i}", g, l, self.place) for i, (g, l) in enumerate(zip(grads, lands))]


def kernel(x, conv_norm_g, conv_w_in, conv_b_in, conv_dw, conv_dw_b, conv_ln_g, conv_ln_b, conv_w_out, conv_b_out, attn_norm_g, w_qkv, b_qkv, q_norm_g, k_norm_g, sinks, w_o, b_o, rel_bias, mlp_norm_g, w_up, w_down, loss_target, m_conv_norm_g, m_conv_w_in, m_conv_b_in, m_conv_dw, m_conv_dw_b, m_conv_ln_g, m_conv_ln_b, m_conv_w_out, m_conv_b_out, m_attn_norm_g, m_w_qkv, m_b_qkv, m_q_norm_g, m_k_norm_g, m_sinks, m_w_o, m_b_o, m_rel_bias, m_mlp_norm_g, m_w_up, m_w_down, v_conv_norm_g, v_conv_w_in, v_conv_b_in, v_conv_dw, v_conv_dw_b, v_conv_ln_g, v_conv_ln_b, v_conv_w_out, v_conv_b_out, v_attn_norm_g, v_w_qkv, v_b_qkv, v_q_norm_g, v_k_norm_g, v_sinks, v_w_o, v_b_o, v_rel_bias, v_mlp_norm_g, v_w_up, v_w_down):
    Dm = D_MODEL
    x2d = x[0]
    tgt = loss_target[0]
    T = x2d.shape[0]
    shard = 2 * lax.axis_index("x") + lax.axis_index("y")

    me = 2 * shard + lax.axis_index("c")

    def own_slot(block, slots, index):
        return lax.dynamic_update_slice(lax.empty((slots,) + block.shape, block.dtype), block[None],
                                        (index,) + (0,) * block.ndim)

    sharded_small = [conv_dw[0], attn_norm_g, b_qkv, b_o]
    (small_sems,), (small_land,), small_token = _gather_start(
        "small_weights_start", [own_slot(_pack(sharded_small), 8, me)], ((0,),), ALL_OTHERS, after=())

    big = [conv_w_in[0], conv_w_out[0], w_qkv[0], w_o[0], w_up[0], w_up[1], w_down[0], w_down[1]]
    stacks = [own_slot(w.astype(BF), N_SHARD, shard) for w in big]
    groups = ((0,), (1,), (4, 6), (2, 3), (5, 7))
    gather_sems, stacks, gather_token = _gather_start("gather_start", stacks, groups, OTHER_CHIPS, after=(small_token,))

    def gathered_group(g, name, after):
        return _gather_wait(name, [stacks[t] for t in groups[g]], gather_sems[g], OTHER_CHIPS, after)

    bucket = _bucket_table()
    bias = _bias_table(rel_bias, bucket)

    h0 = _rms_fwd("conv_norm", x2d, conv_norm_g, deps=(gather_token,))
    (w_in_sm,) = gathered_group(0, "gather_wait_conv_in", (h0, bias))
    (u,) = _mm("conv_in", h0, w_in_sm, nt=False, b_sm=True, tm=2048, tn=512, rows=256, ep_in=((conv_b_in, "row"),),
               ep_fn=lambda acc, b: (acc + b,), outs=(("tile", BF),))
    (gathered,) = _gather_wait("small_weights_wait", [small_land], small_sems, ALL_OTHERS, (u,))
    chips = [_unpack(gathered[2 * s], [a.shape for a in sharded_small]) for s in range(N_SHARD)]
    dw_f, attn_norm_f, b_qkv_f, b_o_f = (jnp.concatenate([chips[s][t] for s in range(N_SHARD)], axis=-1)
                                         for t in range(len(sharded_small)))
    dw_pad = jnp.pad(dw_f, ((0, HALO - CONV_W), (0, 0)))
    cv, s_act = _conv_fwd(u, dw_pad, conv_dw_b, conv_ln_g, conv_ln_b)
    (g_out,) = gathered_group(1, "gather_wait_conv_out", (s_act,))
    w_out_f = g_out.reshape(Dm, Dm)
    x1, h1 = _mm("conv_out", s_act, w_out_f, nt=False, tm=1024, tn=1024, rows=256,
                 ep_in=((conv_b_out, "row"), (x2d, "tile"), (mlp_norm_g[0:1], "row")), ep_fn=_residual_norm_ep,
                 outs=RESIDUAL_NORM_OUTS)

    g_up0, g_down0 = gathered_group(2, "gather_wait_mlp0", (x1,))
    w_up_sm = [g_up0, None]
    w_down_f = [g_down0.reshape(D_FF, Dm), None]
    up0 = _mlp_up(0, h1, w_up_sm[0])
    x2, h2 = _mm("mlp0_down", up0, w_down_f[0], nt=False, tm=512, tn=1024, rows=256, a_fn=_relu2,
                 ep_in=((x1, "tile"), (attn_norm_f, "row")), ep_fn=_residual_norm_ep, outs=RESIDUAL_NORM_OUTS)

    g_qkv, g_o = gathered_group(3, "gather_wait_attn", (x2,))
    w_qkv_f = jnp.transpose(g_qkv, (1, 0, 2)).reshape(Dm, QKV_DIM)
    w_o_f = g_o.reshape(ATTN_DIM, Dm)
    (qkv,) = _mm("attn_qkv", h2, w_qkv_f, nt=False, tm=1024, tn=QKV_DIM, rows=256, ep_in=((b_qkv_f, "row"),),
                 ep_fn=lambda acc, b: (acc + b,), outs=(("tile", F32),))
    qg_t = jnp.tile(q_norm_g, (1, N_HEADS))
    kg_t = jnp.tile(k_norm_g, (1, N_KV))
    qn, kn, vv = _qk_norm_fwd(qkv, qg_t, kg_t)
    sinks1 = sinks[0]
    att = _attn_fwd(qn, kn, vv, bias, sinks1)
    x3, h3 = _mm("attn_out", att, w_o_f, nt=False, tm=1024, tn=1024, rows=256,
                 ep_in=((b_o_f, "row"), (x2, "tile"), (mlp_norm_g[1:2], "row")), ep_fn=_residual_norm_ep,
                 outs=RESIDUAL_NORM_OUTS)

    g_up1, g_down1 = gathered_group(4, "gather_wait_mlp1", (x3,))
    w_up_sm[1] = g_up1
    w_down_f[1] = g_down1.reshape(D_FF, Dm)
    up1 = _mlp_up(1, h3, w_up_sm[1])

    def loss_ep(acc, r, t):
        diff = acc + r - t
        dy = diff * (1.0 / Dm)
        return dy, dy, jnp.sum(diff * diff, axis=0, keepdims=True)

    dy, dy_bf, sq = _mm("mlp1_down_loss", up1, w_down_f[1], nt=False, tm=512, tn=1024, rows=256, a_fn=_relu2,
                        ep_in=((x3, "tile"), (tgt, "tile")), ep_fn=loss_ep,
                        outs=(("tile", F32), ("tile", BF), ("colsum", F32)))

    place = jnp.stack([shard, lax.axis_index("c")]).astype(jnp.int32)
    dx3, dx3_bf, dg_mlp1, db_o, dup1 = _mlp_bwd(1, dy, dy_bf, x3, mlp_norm_g[1:2], up1, w_up_sm[1], w_down_f[1])
    dw_down1 = _mm_tn("mlp1_dw_down", up1, dy_bf, tm=1024, tn=1024, tk=2048, a_fn=_relu2)
    dw_up1 = _mm_tn("mlp1_dw_up", h3, dup1, tm=1024, tn=1024, tk=4096, out_sm=N_SHARD)
    red_mlp1 = _Reduction("mlp1", [dw_up1, dw_down1.reshape(N_SHARD, D_FF // N_SHARD, Dm)], place)

    ident = lambda acc: (acc,)
    (datt,) = _mm("attn_dout", dx3_bf, w_o_f, nt=True, tm=1024, tn=1024, rows=256, ep_fn=ident, outs=(("tile", BF),),
                  deps=(red_mlp1.token,))
    dw_o = _mm_tn("attn_dw_o", att, dx3_bf, tm=1024, tn=1024, tk=4096)
    dqn, dkn, dvv, dbias, dsinks = _attn_bwd(qn, kn, vv, bias, sinks1, datt)
    drel = _bias_grad(dbias, bucket)
    dqkv, db_qkv, dqg_t, dkg_t = _qk_norm_bwd(qkv, dqn, dkn, dvv, qg_t, kg_t)
    dw_qkv = _mm_tn("attn_dw_qkv", h2, dqkv, tm=1024, tn=QKV_DIM, tk=2048)
    red_attn = _Reduction("attn", [jnp.transpose(dw_qkv.reshape(Dm, N_SHARD, QKV_DIM // N_SHARD), (1, 0, 2)),
                                   dw_o.reshape(N_SHARD, ATTN_DIM // N_SHARD, Dm)], place)
    dx2, dx2_bf, dg_attn, _ = _mm("attn_dx", dqkv, w_qkv_f, nt=True, tm=512, tn=1024, rows=256,
                                  ep_in=((x2, "tile"), (attn_norm_f, "row"), (dx3, "tile")), ep_fn=_rms_bwd_ep,
                                  outs=RMS_BWD_OUTS, deps=(red_attn.token,))

    dx1, dx1_bf, dg_mlp0, db_out, dup0 = _mlp_bwd(0, dx2, dx2_bf, x1, mlp_norm_g[0:1], up0, w_up_sm[0], w_down_f[0])
    dw_down0 = _mm_tn("mlp0_dw_down", up0, dx2_bf, tm=1024, tn=1024, tk=2048, a_fn=_relu2)
    dw_up0 = _mm_tn("mlp0_dw_up", h1, dup0, tm=1024, tn=1024, tk=4096, out_sm=N_SHARD)
    dw_out = _mm_tn("conv_dw_out", s_act, dx1_bf, tm=1024, tn=1024, tk=4096)
    red_mlp0 = _Reduction("mlp0", [dw_up0, dw_down0.reshape(N_SHARD, D_FF // N_SHARD, Dm),
                                   dw_out.reshape(N_SHARD, Dm // N_SHARD, Dm)], place)
    (r_qkv, r_o) = red_attn.finish((dx1,))
    (r_up1, r_down1) = red_mlp1.finish((dx1,))

    dcv, dln_g, dln_b, ddw_b = _mm("conv_ds", dx1_bf, w_out_f, nt=True, tm=512, tn=1024, rows=256,
                                   ep_in=((cv, "tile"), (conv_ln_g, "row"), (conv_ln_b, "row")),
                                   ep_fn=_ln_silu_bwd_ep,
                                   outs=(("tile", F32), ("colsum", F32), ("colsum", F32), ("colsum", F32)),
                                   deps=(red_mlp0.token,))
    du, db_in, ddw8 = _conv_bwd(u, dcv, dw_pad)
    (r_up0, r_down0, r_out) = red_mlp0.finish((du,))
    dw_in = _mm_tn("conv_dw_in", h0, du, tm=1024, tn=512, tk=4096, out_sm=N_SHARD)
    red_conv = _Reduction("conv", [dw_in], place)
    def first_layer_ep(*args):
        tot, _, dg, _ = _rms_bwd_ep(*args)
        return tot, dg

    gx, dg_conv = _mm("conv_dx", du, w_in_sm, nt=True, b_sm=True, tm=512, tn=1024, rows=256,
                      ep_in=((x2d, "tile"), (conv_norm_g, "row"), (dx1, "tile")), ep_fn=first_layer_ep,
                      outs=(("tile", F32), ("colsum", F32)), deps=(red_conv.token,))
    (r_in,) = red_conv.finish((gx,))

    dqg = dqg_t.reshape(N_HEADS, HEAD_DIM).sum(axis=0, keepdims=True)
    dkg = dkg_t.reshape(N_KV, HEAD_DIM).sum(axis=0, keepdims=True)
    small_full = [dg_conv, db_in, ddw8.sum(axis=1)[:CONV_W], ddw_b, dln_g, dln_b, db_out, dg_attn, db_qkv, dqg, dkg,
                  dsinks[None, :], db_o, drel.reshape(1, REL_BUCKETS * N_HEADS),
                  jnp.pad(dg_mlp0, ((0, 1), (0, 0))) + jnp.pad(dg_mlp1, ((1, 0), (0, 0))), sq]
    (sg_sems,), (sg_land,), sg_token = _gather_start(
        "small_grads_start", [own_slot(_pack(small_full), 8, me)], ((0,),), ALL_OTHERS, after=())

    mine = [r_in, r_out, r_qkv, r_o, r_up0, r_up1, r_down0, r_down1]
    r_in, r_out, r_qkv, r_o, r_up0, r_up1, r_down0, r_down1 = zip(
        mine, _join_halves("join_halves", mine, deps=(sg_token,)))

    big_out = {}
    for nm, w, m, v, gs in (("conv_w_in", conv_w_in, m_conv_w_in, v_conv_w_in, (r_in,)),
                            ("conv_w_out", conv_w_out, m_conv_w_out, v_conv_w_out, (r_out,)),
                            ("w_qkv", w_qkv, m_w_qkv, v_w_qkv, (r_qkv,)),
                            ("w_o", w_o, m_w_o, v_w_o, (r_o,)),
                            ("w_up", w_up, m_w_up, v_w_up, (r_up0, r_up1)),
                            ("w_down", w_down, m_w_down, v_w_down, (r_down0, r_down1))):
        big_out[nm] = _adamw(f"adamw_{nm}", w, m, v, gs)

    (sg_land,) = _gather_wait("small_grads_wait", [sg_land], sg_sems, ALL_OTHERS,
                              [big_out[nm][0] for nm in big_out])
    small_sum = _sum8("small_grads_sum", sg_land)
    (r_norm, r_b_in, r_dw, r_dw_b, r_ln_g, r_ln_b, r_b_out, r_attn_norm, r_b_qkv, r_qg, r_kg, r_sinks, r_b_o, r_rel,
     r_mlp_norm, r_sq) = _unpack(small_sum, [a.shape for a in small_full])
    loss = 0.5 * jnp.sum(r_sq) * (1.0 / Dm)

    def cols(a, width):
        return lax.dynamic_slice_in_dim(a, shard * width, width, axis=a.ndim - 1)

    small_names = ["conv_norm_g", "conv_b_in", "conv_dw", "conv_dw_b", "conv_ln_g", "conv_ln_b", "conv_b_out",
                   "attn_norm_g", "b_qkv", "q_norm_g", "k_norm_g", "sinks", "b_o", "rel_bias", "mlp_norm_g"]
    small_g = [r_norm, r_b_in, cols(r_dw, Dm // N_SHARD)[None], r_dw_b, r_ln_g, r_ln_b, r_b_out,
               cols(r_attn_norm, Dm // N_SHARD), cols(r_b_qkv, QKV_DIM // N_SHARD), r_qg, r_kg, r_sinks,
               cols(r_b_o, Dm // N_SHARD), r_rel.reshape(REL_BUCKETS, N_HEADS), r_mlp_norm]
    small_w = [conv_norm_g, conv_b_in, conv_dw, conv_dw_b, conv_ln_g, conv_ln_b, conv_b_out, attn_norm_g, b_qkv,
               q_norm_g, k_norm_g, sinks, b_o, rel_bias, mlp_norm_g]
    small_m = [m_conv_norm_g, m_conv_b_in, m_conv_dw, m_conv_dw_b, m_conv_ln_g, m_conv_ln_b, m_conv_b_out,
               m_attn_norm_g, m_b_qkv, m_q_norm_g, m_k_norm_g, m_sinks, m_b_o, m_rel_bias, m_mlp_norm_g]
    small_v = [v_conv_norm_g, v_conv_b_in, v_conv_dw, v_conv_dw_b, v_conv_ln_g, v_conv_ln_b, v_conv_b_out,
               v_attn_norm_g, v_b_qkv, v_q_norm_g, v_k_norm_g, v_sinks, v_b_o, v_rel_bias, v_mlp_norm_g]
    flat2 = lambda a: a.reshape(-1, a.shape[-1])
    small_g = [flat2(g) for g in small_g]
    d_s, m_s, v_s = _adamw_small([flat2(w) for w in small_w], small_g, [flat2(m) for m in small_m],
                                 [flat2(v) for v in small_v])
    small_out = {}
    for nm, w, g, d, m2, v2 in zip(small_names, small_w, small_g, d_s, m_s, v_s):
        small_out[nm] = tuple(a.reshape(w.shape) for a in (g, d, m2, v2))

    order = ["conv_norm_g", "conv_w_in", "conv_b_in", "conv_dw", "conv_dw_b", "conv_ln_g", "conv_ln_b", "conv_w_out",
             "conv_b_out", "attn_norm_g", "w_qkv", "b_qkv", "q_norm_g", "k_norm_g", "sinks", "w_o", "b_o", "rel_bias",
             "mlp_norm_g", "w_up", "w_down"]
    res = {**small_out, **big_out}
    outs = [loss, gx[None]]
    for part in range(4):
        outs += [res[nm][part] for nm in order]
    return tuple(outs)
```

```python
import math

import numpy as np
import jax
import jax.numpy as jnp
from jax import lax
from jax.experimental import pallas as pl
from jax.experimental.pallas import tpu as pltpu

F32 = jnp.float32
BF = jnp.bfloat16
MESH = pl.DeviceIdType.MESH

D_MODEL = 1024
D_FF = 4096
N_HEADS = 16
N_KV = 2
GROUP = N_HEADS // N_KV
HEAD_DIM = 64
ATTN_DIM = N_HEADS * HEAD_DIM
KV_DIM = N_KV * HEAD_DIM
QKV_DIM = ATTN_DIM + 2 * KV_DIM
BLOCK = 128
CONV_W = 31
HALO = 32
REL_BUCKETS = 32
REL_MAX_DIST = 128
NORM_EPS = 1e-6
NEG_INF = -1e30
N_SHARD = 4
LANES = 1024

ADAM_LR = 0.001
ADAM_B1 = 0.9
ADAM_B2 = 0.999
ADAM_EPS = 1e-08
ADAM_WD = 0.01
ADAM_STEP = 10

VMEM_LIMIT = 56 * 1024 * 1024


def _params(n_axes):
    return pltpu.CompilerParams(dimension_semantics=("arbitrary",) * n_axes, vmem_limit_bytes=VMEM_LIMIT)


def _dot(a, b, ca, cb):
    return lax.dot_general(a, b, (((ca,), (cb,)), ((), ())), preferred_element_type=F32)


def _mm(name, a, b, *, nt, tm, tn, ep_fn, outs, a_fn=None, b_sm=False, ep_in=(), deps=(), rows=None):
    M, K = a.shape
    rows = tm if rows is None else rows
    if b_sm:
        S, ks = b.shape[0], b.shape[2]
        N, per = (b.shape[1], None) if nt else (S * b.shape[2], b.shape[2] // tn)
        assert (S * ks == K) if nt else (b.shape[1] == K)
    else:
        N = b.shape[0] if nt else b.shape[1]
        assert (b.shape[1] if nt else b.shape[0]) == K
    assert M % tm == 0 and N % tn == 0 and tm % rows == 0
    ne, no, nd = len(ep_in), len(outs), len(deps)

    def body(a_ref, b_ref, *rest):
        ep_refs, out_refs = rest[:ne], rest[ne + nd:ne + nd + no]
        i = pl.program_id(1)
        sums = [None] * no
        for r in range(tm // rows):
            rs = pl.ds(r * rows, rows)

            def lhs(cols):
                av = a_ref[rs, cols]
                return (av if a_fn is None else a_fn(av)).astype(BF)

            if b_sm and nt:
                acc = None
                for s in range(S):
                    part = _dot(lhs(pl.ds(s * ks, ks)), b_ref[s].astype(BF), 1, 1)
                    acc = part if acc is None else acc + part
            else:
                acc = _dot(lhs(slice(None)), b_ref[...].astype(BF), 1, 1 if nt else 0)
            ep_vals = [ref[rs, :] if kind == "tile" else ref[...] for ref, (_, kind) in zip(ep_refs, ep_in)]
            vals = ep_fn(acc, *ep_vals)
            for o, ((kind, dt), ref, val) in enumerate(zip(outs, out_refs, vals)):
                if kind == "tile":
                    ref[rs, :] = val.astype(dt)
                else:
                    sums[o] = val if sums[o] is None else sums[o] + val
        for (kind, dt), ref, val in zip(outs, out_refs, sums):
            if kind == "colsum":
                @pl.when(i == 0)
                def _():
                    ref[...] = val

                @pl.when(i > 0)
                def _():
                    ref[...] += val

    if b_sm and nt:
        b_spec = pl.BlockSpec((S, tn, ks), lambda j, i: (0, j, 0))
    elif b_sm:
        b_spec = pl.BlockSpec((None, K, tn), lambda j, i: (j // per, 0, j % per))
    elif nt:
        b_spec = pl.BlockSpec((tn, K), lambda j, i: (j, 0))
    else:
        b_spec = pl.BlockSpec((K, tn), lambda j, i: (0, j))
    in_specs = [pl.BlockSpec((tm, K), lambda j, i: (i, 0)), b_spec]
    for arr, kind in ep_in:
        if kind == "tile":
            assert arr.shape == (M, N)
            in_specs.append(pl.BlockSpec((tm, tn), lambda j, i: (i, j)))
        else:
            assert arr.shape == (1, N)
            in_specs.append(pl.BlockSpec((1, tn), lambda j, i: (0, j)))
    in_specs += [pl.BlockSpec(memory_space=pl.ANY)] * nd
    out_shape, out_specs = [], []
    for kind, dt in outs:
        if kind == "tile":
            out_shape.append(jax.ShapeDtypeStruct((M, N), dt))
            out_specs.append(pl.BlockSpec((tm, tn), lambda j, i: (i, j)))
        else:
            out_shape.append(jax.ShapeDtypeStruct((1, N), F32))
            out_specs.append(pl.BlockSpec((1, tn), lambda j, i: (0, j)))
    return pl.pallas_call(
        body, name=name, grid=(N // tn, M // tm), in_specs=in_specs, out_specs=out_specs, out_shape=out_shape,
        compiler_params=_params(2),
    )(a, b, *[arr for arr, _ in ep_in], *deps)


def _mm_tn(name, a, b, *, tm, tn, tk, a_fn=None, out_sm=None):
    T, Ka = a.shape
    N = b.shape[1]
    assert b.shape[0] == T and T % tk == 0 and Ka % tm == 0 and N % tn == 0
    nk = T // tk

    def body(a_ref, b_ref, o_ref, acc_ref):
        k = pl.program_id(2)

        @pl.when(k == 0)
        def _():
            acc_ref[...] = jnp.zeros_like(acc_ref)

        av = a_ref[...]
        if a_fn is not None:
            av = a_fn(av)
        acc_ref[...] += _dot(av.astype(BF), b_ref[...].astype(BF), 0, 0)

        @pl.when(k == nk - 1)
        def _():
            o_ref[...] = acc_ref[...].astype(BF)

    if out_sm is None:
        out_shape = jax.ShapeDtypeStruct((Ka, N), BF)
        out_spec = pl.BlockSpec((tm, tn), lambda i, j, k: (i, j))
    else:
        per = (N // out_sm) // tn
        assert per * tn * out_sm == N
        out_shape = jax.ShapeDtypeStruct((out_sm, Ka, N // out_sm), BF)
        out_spec = pl.BlockSpec((None, tm, tn), lambda i, j, k: (j // per, i, j % per))
    return pl.pallas_call(
        body, name=name, grid=(Ka // tm, N // tn, nk),
        in_specs=[pl.BlockSpec((tk, tm), lambda i, j, k: (k, i)), pl.BlockSpec((tk, tn), lambda i, j, k: (k, j))],
        out_specs=out_spec, out_shape=out_shape, scratch_shapes=[pltpu.VMEM((tm, tn), F32)],
        compiler_params=_params(3),
    )(a, b)


def _relu2(v):
    r = jnp.maximum(v.astype(F32), 0.0)
    return r * r


def _rms_bwd_ep(dh, x, g, dres):
    rstd = lax.rsqrt(jnp.mean(x * x, axis=-1, keepdims=True) + NORM_EPS)
    xh = x * rstd
    dxh = dh * g
    dx = rstd * (dxh - xh * jnp.mean(dxh * xh, axis=-1, keepdims=True))
    tot = dres + dx
    return tot, tot, jnp.sum(dh * xh, axis=0, keepdims=True), jnp.sum(tot, axis=0, keepdims=True)


def _rms_fwd(name, x, g, tm=512, deps=()):
    T, Dm = x.shape

    def body(x_ref, g_ref, *rest):
        o_ref = rest[-1]
        xv = x_ref[...]
        rstd = lax.rsqrt(jnp.mean(xv * xv, axis=-1, keepdims=True) + NORM_EPS)
        o_ref[...] = (xv * rstd * g_ref[...]).astype(BF)

    return pl.pallas_call(
        body, name=name, grid=(T // tm,),
        in_specs=[pl.BlockSpec((tm, Dm), lambda i: (i, 0)), pl.BlockSpec((1, Dm), lambda i: (0, 0))]
        + [pl.BlockSpec(memory_space=pl.ANY)] * len(deps),
        out_specs=pl.BlockSpec((tm, Dm), lambda i: (i, 0)), out_shape=jax.ShapeDtypeStruct((T, Dm), BF),
        compiler_params=_params(1),
    )(x, g, *deps)


HEAD_COLS = 128


def _two_term_dot(v, m):
    hi = v.astype(BF)
    lo = (v - hi.astype(F32)).astype(BF)
    return _dot(hi, m, 1, 0) + _dot(lo, m, 1, 0)


def _head_sum(v, select):
    sel, sel_t = select
    return _two_term_dot(_two_term_dot(v, sel), sel_t)


def _head_select(n):
    sel = (np.arange(n)[:, None] // HEAD_DIM == np.arange(HEAD_COLS)[None, :]).astype(np.float32)
    return jnp.asarray(sel, dtype=BF), jnp.asarray(sel.T, dtype=BF)


def _qk_norm_fwd(qkv, qg_t, kg_t, tm=256):
    T = qkv.shape[0]
    scale = 1.0 / math.sqrt(HEAD_DIM)

    def body(x_ref, qg_ref, kg_ref, sq_ref, sqt_ref, sk_ref, skt_ref, q_ref, k_ref, v_ref):
        q = x_ref[:, pl.ds(0, ATTN_DIM)]
        rq = lax.rsqrt(_head_sum(q * q, (sq_ref[...], sqt_ref[...])) * (1.0 / HEAD_DIM) + NORM_EPS)
        q_ref[...] = (q * rq * qg_ref[...] * scale).astype(BF)
        k = x_ref[:, pl.ds(ATTN_DIM, KV_DIM)]
        rk = lax.rsqrt(_head_sum(k * k, (sk_ref[...], skt_ref[...])) * (1.0 / HEAD_DIM) + NORM_EPS)
        k_ref[...] = (k * rk * kg_ref[...]).astype(BF)
        v_ref[...] = x_ref[:, pl.ds(ATTN_DIM + KV_DIM, KV_DIM)].astype(BF)

    full = lambda shape: pl.BlockSpec(shape, lambda i: (0, 0))
    return pl.pallas_call(
        body, name="qk_norm_fwd", grid=(T // tm,),
        in_specs=[pl.BlockSpec((tm, QKV_DIM), lambda i: (i, 0)), full((1, ATTN_DIM)), full((1, KV_DIM)),
                  full((ATTN_DIM, HEAD_COLS)), full((HEAD_COLS, ATTN_DIM)), full((KV_DIM, HEAD_COLS)), full((HEAD_COLS, KV_DIM))],
        out_specs=[pl.BlockSpec((tm, ATTN_DIM), lambda i: (i, 0)), pl.BlockSpec((tm, KV_DIM), lambda i: (i, 0)),
                   pl.BlockSpec((tm, KV_DIM), lambda i: (i, 0))],
        out_shape=[jax.ShapeDtypeStruct((T, ATTN_DIM), BF), jax.ShapeDtypeStruct((T, KV_DIM), BF),
                   jax.ShapeDtypeStruct((T, KV_DIM), BF)],
        compiler_params=_params(1),
    )(qkv, qg_t, kg_t, *_head_select(ATTN_DIM), *_head_select(KV_DIM))


def _qk_norm_bwd(qkv, dqn, dkn, dv, qg_t, kg_t, tm=256):
    T = qkv.shape[0]

    def body(x_ref, dq_ref, dk_ref, dv_ref, qg_ref, kg_ref, sq_ref, sqt_ref, sk_ref, skt_ref,
             o_ref, db_ref, dqg_ref, dkg_ref):
        i = pl.program_id(0)

        def one(x, dy, g, select):
            r = lax.rsqrt(_head_sum(x * x, select) * (1.0 / HEAD_DIM) + NORM_EPS)
            xh = x * r
            dxh = dy * g
            dx = r * (dxh - xh * (_head_sum(dxh * xh, select) * (1.0 / HEAD_DIM)))
            return dx, jnp.sum(dy * xh, axis=0, keepdims=True)

        dq, dqg = one(x_ref[:, pl.ds(0, ATTN_DIM)], dq_ref[...], qg_ref[...], (sq_ref[...], sqt_ref[...]))
        dk, dkg = one(x_ref[:, pl.ds(ATTN_DIM, KV_DIM)], dk_ref[...], kg_ref[...], (sk_ref[...], skt_ref[...]))
        dvv = dv_ref[...]
        o_ref[:, pl.ds(0, ATTN_DIM)] = dq.astype(BF)
        o_ref[:, pl.ds(ATTN_DIM, KV_DIM)] = dk.astype(BF)
        o_ref[:, pl.ds(ATTN_DIM + KV_DIM, KV_DIM)] = dvv.astype(BF)
        sq, sk, sv = (jnp.sum(t, axis=0, keepdims=True) for t in (dq, dk, dvv))

        @pl.when(i == 0)
        def _():
            db_ref[:, pl.ds(0, ATTN_DIM)] = sq
            db_ref[:, pl.ds(ATTN_DIM, KV_DIM)] = sk
            db_ref[:, pl.ds(ATTN_DIM + KV_DIM, KV_DIM)] = sv
            dqg_ref[...] = dqg
            dkg_ref[...] = dkg

        @pl.when(i > 0)
        def _():
            db_ref[:, pl.ds(0, ATTN_DIM)] += sq
            db_ref[:, pl.ds(ATTN_DIM, KV_DIM)] += sk
            db_ref[:, pl.ds(ATTN_DIM + KV_DIM, KV_DIM)] += sv
            dqg_ref[...] += dqg
            dkg_ref[...] += dkg

    full = lambda shape: pl.BlockSpec(shape, lambda i: (0, 0))
    row = lambda n: pl.BlockSpec((tm, n), lambda i: (i, 0))
    return pl.pallas_call(
        body, name="qk_norm_bwd", grid=(T // tm,),
        in_specs=[row(QKV_DIM), row(ATTN_DIM), row(KV_DIM), row(KV_DIM), full((1, ATTN_DIM)), full((1, KV_DIM)),
                  full((ATTN_DIM, HEAD_COLS)), full((HEAD_COLS, ATTN_DIM)), full((KV_DIM, HEAD_COLS)), full((HEAD_COLS, KV_DIM))],
        out_specs=[row(QKV_DIM), full((1, QKV_DIM)), full((1, ATTN_DIM)), full((1, KV_DIM))],
        out_shape=[jax.ShapeDtypeStruct((T, QKV_DIM), BF), jax.ShapeDtypeStruct((1, QKV_DIM), F32),
                   jax.ShapeDtypeStruct((1, ATTN_DIM), F32), jax.ShapeDtypeStruct((1, KV_DIM), F32)],
        compiler_params=_params(1),
    )(qkv, dqn, dkn, dv, qg_t, kg_t, *_head_select(ATTN_DIM), *_head_select(KV_DIM))


ROWS = 64
COLS = 128


SUBLANES = 8
FIRST_TAP = HALO - (CONV_W - 1)


def _glu(a, g):
    return a.astype(F32) * jax.nn.sigmoid(g.astype(F32))


def _shifted(xe, s):
    return xe if s == 0 else pltpu.roll(xe, ROWS + HALO - s, axis=0)


def _conv_fwd(u, dw_pad, dw_b, ln_g, ln_b, tm=256):
    T = u.shape[0]
    Dm = D_MODEL
    hpt = tm // HALO

    def body(ac_ref, gc_ref, ap_ref, gp_ref, w_ref, wb_ref, lg_ref, lb_ref, cv_ref, s_ref, ext):
        i = pl.program_id(0)
        ext[pl.ds(0, HALO), :] = jnp.where(i > 0, _glu(ap_ref[...], gp_ref[...]), 0.0)
        ext[pl.ds(HALO, tm), :] = _glu(ac_ref[...], gc_ref[...])

        def rows(r, carry):
            r0 = pl.multiple_of(r * ROWS, ROWS)
            for c in range(Dm // COLS):
                cs = pl.ds(c * COLS, COLS)
                xe = ext[pl.ds(r0, ROWS + HALO), cs]
                acc = jnp.zeros((ROWS, COLS), F32)
                for s in range(SUBLANES):
                    xs = _shifted(xe, s)
                    for j in range(CONV_W):
                        off = FIRST_TAP + j
                        if off % SUBLANES == s:
                            acc = acc + xs[off - s:off - s + ROWS, :] * w_ref[pl.ds(j, 1), cs]
                cv_ref[pl.ds(r0, ROWS), cs] = acc + wb_ref[:, cs]
            return carry

        lax.fori_loop(0, tm // ROWS, rows, 0)
        cv = cv_ref[...]
        xc = cv - jnp.mean(cv, axis=-1, keepdims=True)
        y = xc * lax.rsqrt(jnp.mean(xc * xc, axis=-1, keepdims=True) + NORM_EPS) * lg_ref[...] + lb_ref[...]
        s_ref[...] = (y * jax.nn.sigmoid(y)).astype(BF)

    full = lambda shape: pl.BlockSpec(shape, lambda i: (0, 0))
    return pl.pallas_call(
        body, name="conv_fwd", grid=(T // tm,),
        in_specs=[pl.BlockSpec((tm, Dm), lambda i: (i, 0)), pl.BlockSpec((tm, Dm), lambda i: (i, 1)),
                  pl.BlockSpec((HALO, Dm), lambda i: (jnp.maximum(i * hpt - 1, 0), 0)),
                  pl.BlockSpec((HALO, Dm), lambda i: (jnp.maximum(i * hpt - 1, 0), 1)),
                  full((HALO, Dm)), full((1, Dm)), full((1, Dm)), full((1, Dm))],
        out_specs=[pl.BlockSpec((tm, Dm), lambda i: (i, 0)), pl.BlockSpec((tm, Dm), lambda i: (i, 0))],
        out_shape=[jax.ShapeDtypeStruct((T, Dm), F32), jax.ShapeDtypeStruct((T, Dm), BF)],
        scratch_shapes=[pltpu.VMEM((tm + HALO, Dm), F32)],
        compiler_params=_params(1),
    )(u, u, u, u, dw_pad, dw_b, ln_g, ln_b)


def _ln_silu_bwd_ep(ds, cv, lg, lb):
    xc = cv - jnp.mean(cv, axis=-1, keepdims=True)
    rstd = lax.rsqrt(jnp.mean(xc * xc, axis=-1, keepdims=True) + NORM_EPS)
    xh = xc * rstd
    y = xh * lg + lb
    sg = jax.nn.sigmoid(y)
    dy = ds * (sg * (1.0 + y * (1.0 - sg)))
    dxh = dy * lg
    dcv = rstd * (dxh - jnp.mean(dxh, axis=-1, keepdims=True) - xh * jnp.mean(dxh * xh, axis=-1, keepdims=True))
    return (dcv, jnp.sum(dy * xh, axis=0, keepdims=True), jnp.sum(dy, axis=0, keepdims=True),
            jnp.sum(dcv, axis=0, keepdims=True))


def _conv_bwd(u, dcv, dw_pad, tm=256):
    T = u.shape[0]
    Dm = D_MODEL
    hpt = tm // HALO
    last = T // HALO - 1
    nt = T // tm

    def body(ac_ref, gc_ref, ap_ref, gp_ref, dc_ref, dn_ref, w_ref, du_ref, db_ref, dw_ref, ext_g, ext_d):
        i = pl.program_id(0)
        ext_g[pl.ds(0, HALO), :] = jnp.where(i > 0, _glu(ap_ref[...], gp_ref[...]), 0.0)
        ext_g[pl.ds(HALO, tm), :] = _glu(ac_ref[...], gc_ref[...])
        ext_d[pl.ds(0, tm), :] = dc_ref[...]
        ext_d[pl.ds(tm, HALO), :] = jnp.where(i < nt - 1, dn_ref[...], 0.0)

        @pl.when(i == 0)
        def _():
            db_ref[...] = jnp.zeros_like(db_ref)
            dw_ref[...] = jnp.zeros_like(dw_ref)

        def rows(r, carry):
            r0 = pl.multiple_of(r * ROWS, ROWS)
            rs = pl.ds(r0, ROWS)
            for c in range(Dm // COLS):
                cs = pl.ds(c * COLS, COLS)
                cs2 = pl.ds(Dm + c * COLS, COLS)
                de = ext_d[pl.ds(r0, ROWS + HALO), cs]
                ge = ext_g[pl.ds(r0, ROWS + HALO), cs]
                dcur = de[0:ROWS, :]
                acc = jnp.zeros((ROWS, COLS), F32)
                for s in range(SUBLANES):
                    ds_, gs_ = _shifted(de, s), _shifted(ge, s)
                    for j in range(CONV_W):
                        off = CONV_W - 1 - j
                        if off % SUBLANES == s:
                            acc = acc + ds_[off - s:off - s + ROWS, :] * w_ref[pl.ds(j, 1), cs]
                        goff = FIRST_TAP + j
                        if goff % SUBLANES == s:
                            prod = dcur * gs_[goff - s:goff - s + ROWS, :]
                            dw_ref[j, :, cs] += jnp.sum(prod.reshape(ROWS // SUBLANES, SUBLANES, COLS), axis=0)
                a = ac_ref[rs, cs].astype(F32)
                sg = jax.nn.sigmoid(gc_ref[rs, cs].astype(F32))
                da = acc * sg
                dg = acc * a * sg * (1.0 - sg)
                du_ref[rs, cs] = da.astype(BF)
                du_ref[rs, cs2] = dg.astype(BF)
                db_ref[:, cs] += jnp.sum(da, axis=0, keepdims=True)
                db_ref[:, cs2] += jnp.sum(dg, axis=0, keepdims=True)
            return carry

        lax.fori_loop(0, tm // ROWS, rows, 0)

    return pl.pallas_call(
        body, name="conv_bwd", grid=(nt,),
        in_specs=[pl.BlockSpec((tm, Dm), lambda i: (i, 0)), pl.BlockSpec((tm, Dm), lambda i: (i, 1)),
                  pl.BlockSpec((HALO, Dm), lambda i: (jnp.maximum(i * hpt - 1, 0), 0)),
                  pl.BlockSpec((HALO, Dm), lambda i: (jnp.maximum(i * hpt - 1, 0), 1)),
                  pl.BlockSpec((tm, Dm), lambda i: (i, 0)),
                  pl.BlockSpec((HALO, Dm), lambda i: (jnp.minimum((i + 1) * hpt, last), 0)),
                  pl.BlockSpec((HALO, Dm), lambda i: (0, 0))],
        out_specs=[pl.BlockSpec((tm, 2 * Dm), lambda i: (i, 0)), pl.BlockSpec((1, 2 * Dm), lambda i: (0, 0)),
                   pl.BlockSpec((HALO, 8, Dm), lambda i: (0, 0, 0))],
        out_shape=[jax.ShapeDtypeStruct((T, 2 * Dm), BF), jax.ShapeDtypeStruct((1, 2 * Dm), F32),
                   jax.ShapeDtypeStruct((HALO, 8, Dm), F32)],
        scratch_shapes=[pltpu.VMEM((tm + HALO, Dm), F32), pltpu.VMEM((tm + HALO, Dm), F32)],
        compiler_params=_params(1),
    )(u, u, u, u, dcv, dcv, dw_pad)


def _bucket_table():
    q_loc = np.arange(BLOCK)[:, None]
    k_loc = np.arange(2 * BLOCK)[None, :]
    dist = q_loc + BLOCK - k_loc
    n = np.maximum(dist, 0)
    max_exact = REL_BUCKETS // 2
    large = max_exact + (np.log(np.maximum(n, 1).astype(np.float32) / max_exact)
                         / math.log(REL_MAX_DIST / max_exact) * (REL_BUCKETS - max_exact)).astype(np.int32)
    large = np.minimum(large, REL_BUCKETS - 1)
    bucket = np.where(n < max_exact, n, large).astype(np.int32)
    return jnp.asarray(np.where((dist >= 0) & (dist < BLOCK), bucket, -1).astype(np.int32))


def _bias_table(rel_bias, bucket):
    def body(rb_ref, bk_ref, o_ref):
        bk = bk_ref[...]
        for h in range(N_HEADS):
            acc = jnp.full((BLOCK, 2 * BLOCK), NEG_INF, F32)
            for b in range(REL_BUCKETS):
                acc = jnp.where(bk == b, rb_ref[b, h], acc)
            o_ref[h] = acc

    return pl.pallas_call(
        body, name="bias_table", out_shape=jax.ShapeDtypeStruct((N_HEADS, BLOCK, 2 * BLOCK), F32),
        in_specs=[pl.BlockSpec(memory_space=pltpu.SMEM), pl.BlockSpec(memory_space=pltpu.VMEM)],
        out_specs=pl.BlockSpec(memory_space=pltpu.VMEM),
    )(rel_bias, bucket)


def _bias_grad(dbias, bucket):
    def body(db_ref, bk_ref, o_ref):
        bk = bk_ref[...]
        for b in range(REL_BUCKETS):
            sel = bk == b
            for h in range(N_HEADS):
                o_ref[b, h] = jnp.sum(jnp.where(sel, db_ref[h], 0.0))

    return pl.pallas_call(
        body, name="bias_grad", out_shape=jax.ShapeDtypeStruct((REL_BUCKETS, N_HEADS), F32),
        in_specs=[pl.BlockSpec(memory_space=pltpu.VMEM), pl.BlockSpec(memory_space=pltpu.VMEM)],
        out_specs=pl.BlockSpec(memory_space=pltpu.SMEM),
    )(dbias, bucket)


GROUP_ROWS = GROUP * BLOCK


def _head_probs(qk, bias_h, sink, first):
    s = jnp.where(first, NEG_INF, qk + bias_h)
    m = jnp.maximum(jnp.max(s, axis=-1, keepdims=True), sink)
    p = jnp.exp(s - m)
    ps = jnp.exp(sink - m)
    inv = 1.0 / (jnp.sum(p, axis=-1, keepdims=True) + ps)
    return p * inv, ps * inv


def _band(prev_ref, cur_ref, g):
    hs = pl.ds(g * HEAD_DIM, HEAD_DIM)
    return jnp.concatenate([prev_ref[:, hs], cur_ref[:, hs]], axis=0)


def _stack_heads(ref, g):
    return jnp.concatenate([ref[:, pl.ds((g * GROUP + hh) * HEAD_DIM, HEAD_DIM)] for hh in range(GROUP)], axis=0)


def _unstack_heads(ref, g, stacked, dtype):
    for hh in range(GROUP):
        ref[:, pl.ds((g * GROUP + hh) * HEAD_DIM, HEAD_DIM)] = stacked[hh * BLOCK:(hh + 1) * BLOCK, :].astype(dtype)


def _first_mask(n):
    col = lax.broadcasted_iota(jnp.int32, (1, 2 * BLOCK), 1)
    return jnp.logical_and(n == 0, col < BLOCK)


def _head_rows(hh):
    return pl.ds(hh * BLOCK, BLOCK)


def _attn_fwd(qn, kn, vv, bias, sinks):
    T = qn.shape[0]
    nb = T // BLOCK

    def body(sk_ref, q_ref, kc_ref, kp_ref, vc_ref, vp_ref, b_ref, o_ref, qk_buf, p_buf):
        first = _first_mask(pl.program_id(0))
        for g in range(N_KV):
            k = _band(kp_ref, kc_ref, g)
            v = _band(vp_ref, vc_ref, g)
            qk_buf[g] = _dot(_stack_heads(q_ref, g), k, 1, 1)
            for hh in range(GROUP):
                h = g * GROUP + hh
                pn, _ = _head_probs(qk_buf[g, _head_rows(hh), :], b_ref[h], sk_ref[h], first)
                p_buf[g, _head_rows(hh), :] = pn.astype(BF)
            _unstack_heads(o_ref, g, _dot(p_buf[g], v, 1, 0), BF)

    cur = lambda n: (n, 0)
    prev = lambda n: (jnp.maximum(n - 1, 0), 0)
    return pl.pallas_call(
        body, name="attn_fwd", grid=(nb,),
        in_specs=[pl.BlockSpec(memory_space=pltpu.SMEM), pl.BlockSpec((BLOCK, ATTN_DIM), cur),
                  pl.BlockSpec((BLOCK, KV_DIM), cur), pl.BlockSpec((BLOCK, KV_DIM), prev),
                  pl.BlockSpec((BLOCK, KV_DIM), cur), pl.BlockSpec((BLOCK, KV_DIM), prev),
                  pl.BlockSpec((N_HEADS, BLOCK, 2 * BLOCK), lambda n: (0, 0, 0))],
        out_specs=pl.BlockSpec((BLOCK, ATTN_DIM), cur), out_shape=jax.ShapeDtypeStruct((T, ATTN_DIM), BF),
        scratch_shapes=[pltpu.VMEM((N_KV, GROUP_ROWS, 2 * BLOCK), F32), pltpu.VMEM((N_KV, GROUP_ROWS, 2 * BLOCK), BF)],
        compiler_params=_params(1),
    )(sinks, qn, kn, kn, vv, vv, bias)


def _attn_bwd(qn, kn, vv, bias, sinks, do):
    T = qn.shape[0]
    nb = T // BLOCK
    scale = 1.0 / math.sqrt(HEAD_DIM)

    def body(sk_ref, q_ref, kc_ref, kp_ref, vc_ref, vp_ref, b_ref, do_ref,
             dq_ref, dk_ref, dv_ref, db_ref, dsk_ref, dk_full, dv_full, dk_carry, dv_carry, qk_buf, dp_buf, p_buf, ds_buf):
        n = pl.program_id(0)

        @pl.when(n == 0)
        def _():
            db_ref[...] = jnp.zeros_like(db_ref)
            dk_carry[...] = jnp.zeros_like(dk_carry)
            dv_carry[...] = jnp.zeros_like(dv_carry)
            for h in range(N_HEADS):
                dsk_ref[h] = 0.0

        @pl.when(n < nb)
        def _():
            first = _first_mask(n)
            for g in range(N_KV):
                k = _band(kp_ref, kc_ref, g)
                v = _band(vp_ref, vc_ref, g)
                q = _stack_heads(q_ref, g)
                dout = _stack_heads(do_ref, g)
                qk_buf[g] = _dot(q, k, 1, 1)
                dp_buf[g] = _dot(dout, v, 1, 1)
                for hh in range(GROUP):
                    h = g * GROUP + hh
                    rows = _head_rows(hh)
                    pn, psink = _head_probs(qk_buf[g, rows, :], b_ref[h], sk_ref[h], first)
                    dp = dp_buf[g, rows, :]
                    delta = jnp.sum(pn * dp, axis=-1, keepdims=True)
                    ds = pn * (dp - delta)
                    dsk_ref[h] += -jnp.sum(psink * delta)
                    db_ref[h] += ds
                    ds_buf[g, rows, :] = ds.astype(BF)
                    p_buf[g, rows, :] = pn.astype(BF)
                dsb = ds_buf[g]
                _unstack_heads(dq_ref, g, _dot(dsb, k, 1, 0) * scale, F32)
                gs = pl.ds(g * HEAD_DIM, HEAD_DIM)
                dk_full[:, gs] = _dot(dsb, q, 0, 0)
                dv_full[:, gs] = _dot(p_buf[g], dout, 0, 0)

        @pl.when(n == nb)
        def _():
            dk_full[...] = jnp.zeros_like(dk_full)
            dv_full[...] = jnp.zeros_like(dv_full)

        dk_ref[...] = dk_carry[...] + dk_full[pl.ds(0, BLOCK), :]
        dv_ref[...] = dv_carry[...] + dv_full[pl.ds(0, BLOCK), :]
        dk_carry[...] = dk_full[pl.ds(BLOCK, BLOCK), :]
        dv_carry[...] = dv_full[pl.ds(BLOCK, BLOCK), :]

    cur = lambda n: (jnp.minimum(n, nb - 1), 0)
    prev = lambda n: (jnp.maximum(jnp.minimum(n, nb - 1) - 1, 0), 0)
    out_kv = lambda n: (jnp.maximum(n - 1, 0), 0)
    return pl.pallas_call(
        body, name="attn_bwd", grid=(nb + 1,),
        in_specs=[pl.BlockSpec(memory_space=pltpu.SMEM), pl.BlockSpec((BLOCK, ATTN_DIM), cur),
                  pl.BlockSpec((BLOCK, KV_DIM), cur), pl.BlockSpec((BLOCK, KV_DIM), prev),
                  pl.BlockSpec((BLOCK, KV_DIM), cur), pl.BlockSpec((BLOCK, KV_DIM), prev),
                  pl.BlockSpec((N_HEADS, BLOCK, 2 * BLOCK), lambda n: (0, 0, 0)),
                  pl.BlockSpec((BLOCK, ATTN_DIM), cur)],
        out_specs=[pl.BlockSpec((BLOCK, ATTN_DIM), cur), pl.BlockSpec((BLOCK, KV_DIM), out_kv),
                   pl.BlockSpec((BLOCK, KV_DIM), out_kv),
                   pl.BlockSpec((N_HEADS, BLOCK, 2 * BLOCK), lambda n: (0, 0, 0)),
                   pl.BlockSpec(memory_space=pltpu.SMEM)],
        out_shape=[jax.ShapeDtypeStruct((T, ATTN_DIM), F32), jax.ShapeDtypeStruct((T, KV_DIM), F32),
                   jax.ShapeDtypeStruct((T, KV_DIM), F32),
                   jax.ShapeDtypeStruct((N_HEADS, BLOCK, 2 * BLOCK), F32), jax.ShapeDtypeStruct((N_HEADS,), F32)],
        scratch_shapes=[pltpu.VMEM((2 * BLOCK, KV_DIM), F32), pltpu.VMEM((2 * BLOCK, KV_DIM), F32),
                        pltpu.VMEM((BLOCK, KV_DIM), F32), pltpu.VMEM((BLOCK, KV_DIM), F32),
                        pltpu.VMEM((N_KV, GROUP_ROWS, 2 * BLOCK), F32), pltpu.VMEM((N_KV, GROUP_ROWS, 2 * BLOCK), F32),
                        pltpu.VMEM((N_KV, GROUP_ROWS, 2 * BLOCK), BF), pltpu.VMEM((N_KV, GROUP_ROWS, 2 * BLOCK), BF)],
        compiler_params=_params(1),
    )(sinks, qn, kn, kn, vv, vv, bias, do)


def _coords():
    return lax.axis_index("x"), lax.axis_index("y"), lax.axis_index("c")


def _sum8(name, blocks):
    def body(b_ref, o_ref):
        tot = b_ref[0]
        for d in range(1, 8):
            tot = tot + b_ref[d]
        o_ref[...] = tot

    return pl.pallas_call(body, name=name, out_shape=jax.ShapeDtypeStruct(blocks.shape[1:], F32))(blocks)


HBM_SPEC = pl.BlockSpec(memory_space=pltpu.HBM)
SEM_SPEC = pl.BlockSpec(memory_space=pltpu.SEMAPHORE)
ANY_SPEC = pl.BlockSpec(memory_space=pl.ANY)
DATAFLOW = pltpu.SideEffectType.DATAFLOW_SIDE_EFFECTING


OTHER_CHIPS = (4, 2, 6)
ALL_OTHERS = (1, 2, 3, 4, 5, 6, 7)


def _slot(x, y, c, peers):
    return 2 * x + y if peers is OTHER_CHIPS else 4 * x + 2 * y + c


def _slot_copy(land, sems, idx, x, y, c, k, peers, arriving):
    send_sems, recv_sems = sems
    px, py, pc = x ^ (k >> 2), y ^ ((k >> 1) & 1), c ^ (k & 1)
    mine = _slot(x, y, c, peers)
    dst = _slot(px, py, pc, peers) if arriving else mine
    return pltpu.make_async_remote_copy(src_ref=land.at[mine], dst_ref=land.at[dst], send_sem=send_sems.at[idx],
                                        recv_sem=recv_sems.at[idx], device_id=(px, py, pc), device_id_type=MESH)


def _gather_start(name, stacks, groups, peers, after):
    n = len(stacks)
    ng = len(groups)
    np_ = len(peers)
    after = tuple(after)

    def body(*refs):
        lands = refs[:n]
        first = n + len(after)
        sems = [(refs[first + 2 * g], refs[first + 2 * g + 1]) for g in range(ng)]
        token = refs[-1]
        x, y, c = _coords()
        for g, members in enumerate(groups):
            for i, t in enumerate(members):
                for j, k in enumerate(peers):
                    _slot_copy(lands[t], sems[g], np_ * i + j, x, y, c, k, peers, arriving=False).start()
        token[...] = jnp.zeros_like(token)

    out_shape = []
    for members in groups:
        out_shape += [pltpu.SemaphoreType.DMA((np_ * len(members),))] * 2
    out_shape += [pltpu.HBM(w.shape, w.dtype) for w in stacks]
    out_shape.append(jax.ShapeDtypeStruct((8, 128), F32))
    res = pl.pallas_call(
        body, name=name, out_shape=out_shape, in_specs=[HBM_SPEC] * n + [ANY_SPEC] * len(after),
        out_specs=[SEM_SPEC] * (2 * ng) + [HBM_SPEC] * n + [pl.BlockSpec(memory_space=pltpu.VMEM)],
        input_output_aliases={t: 2 * ng + t for t in range(n)},
        compiler_params=pltpu.CompilerParams(has_side_effects=DATAFLOW),
    )(*[pltpu.with_memory_space_constraint(w, pltpu.HBM) for w in stacks], *after)
    sems = [(res[2 * g], res[2 * g + 1]) for g in range(ng)]
    return sems, list(res[2 * ng:2 * ng + n]), res[-1]


def _gather_wait(name, stacks, sems, peers, after):
    n = len(stacks)
    after = tuple(after)

    def body(*refs):
        lands = refs[:n]
        group_sems = (refs[n], refs[n + 1])
        x, y, c = _coords()
        for i in range(n):
            for j, k in enumerate(peers):
                cp = _slot_copy(lands[i], group_sems, len(peers) * i + j, x, y, c, k, peers, arriving=True)
                cp.wait_send()
                cp.wait_recv()

    return pl.pallas_call(
        body, name=name, out_shape=[pltpu.HBM(w.shape, w.dtype) for w in stacks],
        in_specs=[HBM_SPEC] * n + [SEM_SPEC, SEM_SPEC] + [ANY_SPEC] * len(after), out_specs=[HBM_SPEC] * n,
        input_output_aliases={t: t for t in range(n)},
        compiler_params=pltpu.CompilerParams(has_side_effects=DATAFLOW),
    )(*stacks, sems[0], sems[1], *after)


N_PEERS = 7


def _peer(x, y, c, k):
    return x ^ (k >> 2), y ^ ((k >> 1) & 1), c ^ (k & 1)


def _reduce_copy(grad, land, sems, idx, x, y, c, k):
    px, py, pc = _peer(x, y, c, k)
    rh = grad.shape[1] // 2
    return pltpu.make_async_remote_copy(src_ref=grad.at[2 * px + py, pl.ds(pc * rh, rh), :], dst_ref=land.at[k - 1],
                                        send_sem=sems[0].at[idx], recv_sem=sems[1].at[idx], device_id=(px, py, pc),
                                        device_id_type=MESH)


def _reduce_start(name, grads):
    n = len(grads)

    def body(*refs):
        src, lands, sems, token = refs[:n], refs[n:2 * n], (refs[2 * n], refs[2 * n + 1]), refs[-1]
        x, y, c = _coords()
        for t in range(n):
            for k in range(1, N_PEERS + 1):
                _reduce_copy(src[t], lands[t], sems, N_PEERS * t + k - 1, x, y, c, k).start()
        token[...] = jnp.zeros_like(token)

    lands = [lax.empty((N_PEERS, g.shape[1] // 2, g.shape[2]), g.dtype) for g in grads]
    out_shape = [pltpu.SemaphoreType.DMA((N_PEERS * n,))] * 2
    out_shape += [pltpu.HBM(a.shape, a.dtype) for a in list(grads) + lands]
    out_shape.append(jax.ShapeDtypeStruct((8, 128), F32))
    res = pl.pallas_call(
        body, name=name, out_shape=out_shape, in_specs=[HBM_SPEC] * (2 * n),
        out_specs=[SEM_SPEC] * 2 + [HBM_SPEC] * (2 * n) + [pl.BlockSpec(memory_space=pltpu.VMEM)],
        input_output_aliases={t: 2 + t for t in range(2 * n)},
        compiler_params=pltpu.CompilerParams(has_side_effects=DATAFLOW),
    )(*[pltpu.with_memory_space_constraint(a, pltpu.HBM) for a in list(grads) + lands])
    return (res[0], res[1]), list(res[2:2 + n]), list(res[2 + n:2 + 2 * n]), res[-1]


def _reduce_wait(name, grads, lands, sems, after):
    n = len(grads)
    after = tuple(after)

    def body(*refs):
        src, dst, group_sems = refs[:n], refs[n:2 * n], (refs[2 * n], refs[2 * n + 1])
        x, y, c = _coords()
        for t in range(n):
            for k in range(1, N_PEERS + 1):
                cp = _reduce_copy(src[t], dst[t], group_sems, N_PEERS * t + k - 1, x, y, c, k)
                cp.wait_send()
                cp.wait_recv()

    res = pl.pallas_call(
        body, name=name, out_shape=[pltpu.HBM(a.shape, a.dtype) for a in list(grads) + list(lands)],
        in_specs=[HBM_SPEC] * (2 * n) + [SEM_SPEC, SEM_SPEC] + [ANY_SPEC] * len(after), out_specs=[HBM_SPEC] * (2 * n),
        input_output_aliases={t: t for t in range(2 * n)},
        compiler_params=pltpu.CompilerParams(has_side_effects=DATAFLOW),
    )(*grads, *lands, sems[0], sems[1], *after)
    return list(res[:n]), list(res[n:])


def _join_halves(name, halves, deps=()):
    n = len(halves)

    def body(*refs):
        src, dst = refs[:n], refs[n + len(deps):2 * n + len(deps)]
        send_sems, recv_sems = refs[-2:]
        x, y, c = _coords()
        cps = []
        for t in range(n):
            cp = pltpu.make_async_remote_copy(src_ref=src[t], dst_ref=dst[t], send_sem=send_sems.at[t],
                                              recv_sem=recv_sems.at[t], device_id=(x, y, 1 - c), device_id_type=MESH)
            cp.start()
            cps.append(cp)
        for cp in cps:
            cp.wait()

    anyspec = pl.BlockSpec(memory_space=pl.ANY)
    return pl.pallas_call(
        body, name=name, out_shape=[jax.ShapeDtypeStruct(h.shape, h.dtype) for h in halves],
        in_specs=[anyspec] * (n + len(deps)), out_specs=[anyspec] * n,
        scratch_shapes=[pltpu.SemaphoreType.DMA((n,)), pltpu.SemaphoreType.DMA((n,))],
    )(*halves, *deps)


def _row_block(rows):
    for rb in (512, 256, 128, 64, 32, 16):
        if rows % rb == 0:
            return rb
    raise ValueError(rows)


def _sum_devices(name, grad, land, place):
    S, R, C = grad.shape
    rh = R // 2
    rb = _row_block(rh)
    nbh = rh // rb

    def body(place_ref, g_ref, l_ref, o_ref):
        tot = g_ref[...].astype(F32)
        for k in range(N_PEERS):
            tot = tot + l_ref[k].astype(F32)
        o_ref[...] = tot

    return pl.pallas_call(
        body, name=name,
        grid_spec=pltpu.PrefetchScalarGridSpec(
            num_scalar_prefetch=1, grid=(nbh,),
            in_specs=[pl.BlockSpec((None, rb, C), lambda r, place: (place[0], place[1] * nbh + r, 0)),
                      pl.BlockSpec((N_PEERS, rb, C), lambda r, place: (0, r, 0))],
            out_specs=pl.BlockSpec((rb, C), lambda r, place: (r, 0))),
        out_shape=jax.ShapeDtypeStruct((rh, C), F32), compiler_params=_params(1),
    )(place, grad, land)


def _adamw_math(w, g, m, v):
    m2 = ADAM_B1 * m + (1.0 - ADAM_B1) * g
    v2 = ADAM_B2 * v + (1.0 - ADAM_B2) * (g * g)
    m_hat = m2 / (1.0 - ADAM_B1 ** ADAM_STEP)
    v_hat = v2 / (1.0 - ADAM_B2 ** ADAM_STEP)
    delta = -ADAM_LR * (m_hat / (jnp.sqrt(v_hat) + ADAM_EPS) + ADAM_WD * w)
    return delta, m2, v2


def _adamw(name, w, m, v, gs):
    L, R, C = w.shape
    Rh = R // 2
    rb = _row_block(Rh)
    nbh = Rh // rb
    assert len(gs) == L

    def body(core_ref, w_ref, m_ref, v_ref, *rest):
        g_refs, (go_ref, d_ref, m2_ref, v2_ref) = rest[:2 * L], rest[2 * L:]
        layer, half = pl.program_id(0), pl.program_id(1)
        mine = half == core_ref[0]
        g = jnp.where(mine, g_refs[0][...], g_refs[1][...])
        for t in range(1, L):
            g = jnp.where(layer == t, jnp.where(mine, g_refs[2 * t][...], g_refs[2 * t + 1][...]), g)
        delta, m2, v2 = _adamw_math(w_ref[...], g, m_ref[...], v_ref[...])
        go_ref[...] = g
        d_ref[...] = delta
        m2_ref[...] = m2
        v2_ref[...] = v2

    wspec = pl.BlockSpec((None, rb, C), lambda l, h, r, core: (l, h * nbh + r, 0))
    gspec = pl.BlockSpec((rb, C), lambda l, h, r, core: (r, 0))
    return pl.pallas_call(
        body, name=name,
        grid_spec=pltpu.PrefetchScalarGridSpec(num_scalar_prefetch=1, grid=(L, 2, nbh),
                                               in_specs=[wspec] * 3 + [gspec] * (2 * L), out_specs=[wspec] * 4),
        out_shape=[jax.ShapeDtypeStruct((L, R, C), F32)] * 4, compiler_params=_params(3),
    )(lax.axis_index("c").astype(jnp.int32).reshape(1), w, m, v, *[g for pair in gs for g in pair])


def _adamw_small(ws, gs, ms, vs):
    n = len(ws)

    def body(*refs):
        w_refs, g_refs, m_refs, v_refs = (refs[k * n:(k + 1) * n] for k in range(4))
        d_refs, m2_refs, v2_refs = (refs[(4 + k) * n:(5 + k) * n] for k in range(3))
        for t in range(n):
            delta, m2, v2 = _adamw_math(w_refs[t][...], g_refs[t][...], m_refs[t][...], v_refs[t][...])
            d_refs[t][...] = delta
            m2_refs[t][...] = m2
            v2_refs[t][...] = v2

    res = pl.pallas_call(body, name="adamw_small", out_shape=[jax.ShapeDtypeStruct(w.shape, F32) for w in ws] * 3)(
        *ws, *gs, *ms, *vs)
    return res[:n], res[n:2 * n], res[2 * n:]


def _packed_rows(shape):
    c = shape[-1]
    return (int(np.prod(shape)) // c) * -(-c // LANES)


def _pack(arrays):
    total = sum(_packed_rows(a.shape) for a in arrays)
    total += -total % 8
    buf, r0 = None, 0
    for a in arrays:
        a = a.astype(F32).reshape(-1, a.shape[-1])
        r, c = a.shape
        k = -(-c // LANES)
        a = jnp.pad(a, ((0, 0), (0, k * LANES - c))).reshape(r * k, LANES)
        a = jnp.pad(a, ((r0, total - r0 - r * k), (0, 0)))
        buf = a if buf is None else buf + a
        r0 += r * k
    return buf


def _unpack(buf, shapes):
    out, r0 = [], 0
    for shp in shapes:
        c = shp[-1]
        rows = _packed_rows(shp)
        out.append(buf[r0:r0 + rows].reshape(-1, -(-c // LANES) * LANES)[:, :c].reshape(shp))
        r0 += rows
    return out


def _rms(x, g):
    return x * lax.rsqrt(jnp.mean(x * x, axis=-1, keepdims=True) + NORM_EPS) * g


def _residual_norm_ep(acc, *rest):
    *bias, res, gain = rest
    x = acc + res + (bias[0] if bias else 0.0)
    return x, _rms(x, gain)


RESIDUAL_NORM_OUTS = (("tile", F32), ("tile", BF))


def _mlp_up(tag, h, w_up_sm):
    (up,) = _mm(f"mlp{tag}_up", h, w_up_sm, nt=False, b_sm=True, tm=2048, tn=1024, rows=256,
                ep_fn=lambda acc: (acc,), outs=(("tile", BF),))
    return up


RMS_BWD_OUTS = (("tile", F32), ("tile", BF), ("colsum", F32), ("colsum", F32))


def _mlp_bwd(tag, dy, dy_bf, x, g, up, w_up_sm, w_down):
    (dup,) = _mm(f"mlp{tag}_dup", dy_bf, w_down, nt=True, tm=2048, tn=1024, rows=256, ep_in=((up, "tile"),),
                 ep_fn=lambda acc, u: (acc * (2.0 * jnp.maximum(u.astype(F32), 0.0)),), outs=(("tile", BF),))
    dx, dx_bf, dg, dx_sum = _mm(f"mlp{tag}_dx", dup, w_up_sm, nt=True, b_sm=True, tm=512, tn=1024, rows=256,
                                ep_in=((x, "tile"), (g, "row"), (dy, "tile")), ep_fn=_rms_bwd_ep, outs=RMS_BWD_OUTS)
    return dx, dx_bf, dg, dx_sum, dup


class _Reduction:
    def __init__(self, tag, grads, place):
        self.tag, self.place = tag, place
        self.sems, self.grads, self.lands, self.token = _reduce_start(f"reduce_start_{tag}", grads)

    def finish(self, after):
        grads, lands = _reduce_wait(f"reduce_wait_{self.tag}", self.grads, self.lands, self.sems, after)
        return [_sum_devices(f"reduce_sum_{self.tag}{i}", g, l, self.place) for i, (g, l) in enumerate(zip(grads, lands))]


def kernel(x, conv_norm_g, conv_w_in, conv_b_in, conv_dw, conv_dw_b, conv_ln_g, conv_ln_b, conv_w_out, conv_b_out, attn_norm_g, w_qkv, b_qkv, q_norm_g, k_norm_g, sinks, w_o, b_o, rel_bias, mlp_norm_g, w_up, w_down, loss_target, m_conv_norm_g, m_conv_w_in, m_conv_b_in, m_conv_dw, m_conv_dw_b, m_conv_ln_g, m_conv_ln_b, m_conv_w_out, m_conv_b_out, m_attn_norm_g, m_w_qkv, m_b_qkv, m_q_norm_g, m_k_norm_g, m_sinks, m_w_o, m_b_o, m_rel_bias, m_mlp_norm_g, m_w_up, m_w_down, v_conv_norm_g, v_conv_w_in, v_conv_b_in, v_conv_dw, v_conv_dw_b, v_conv_ln_g, v_conv_ln_b, v_conv_w_out, v_conv_b_out, v_attn_norm_g, v_w_qkv, v_b_qkv, v_q_norm_g, v_k_norm_g, v_sinks, v_w_o, v_b_o, v_rel_bias, v_mlp_norm_g, v_w_up, v_w_down):
    Dm = D_MODEL
    x2d = x[0]
    tgt = loss_target[0]
    T = x2d.shape[0]
    shard = 2 * lax.axis_index("x") + lax.axis_index("y")

    me = 2 * shard + lax.axis_index("c")

    def own_slot(block, slots, index):
        return lax.dynamic_update_slice(lax.empty((slots,) + block.shape, block.dtype), block[None],
                                        (index,) + (0,) * block.ndim)

    sharded_small = [conv_dw[0], attn_norm_g, b_qkv, b_o]
    (small_sems,), (small_land,), small_token = _gather_start(
        "small_weights_start", [own_slot(_pack(sharded_small), 8, me)], ((0,),), ALL_OTHERS, after=())

    big = [conv_w_in[0], conv_w_out[0], w_qkv[0], w_o[0], w_up[0], w_up[1], w_down[0], w_down[1]]
    stacks = [own_slot(w.astype(BF), N_SHARD, shard) for w in big]
    groups = ((0,), (1,), (4, 6), (2, 3), (5, 7))
    gather_sems, stacks, gather_token = _gather_start("gather_start", stacks, groups, OTHER_CHIPS, after=(small_token,))

    def gathered_group(g, name, after):
        return _gather_wait(name, [stacks[t] for t in groups[g]], gather_sems[g], OTHER_CHIPS, after)

    bucket = _bucket_table()
    bias = _bias_table(rel_bias, bucket)

    h0 = _rms_fwd("conv_norm", x2d, conv_norm_g, deps=(gather_token,))
    (w_in_sm,) = gathered_group(0, "gather_wait_conv_in", (h0, bias))
    (u,) = _mm("conv_in", h0, w_in_sm, nt=False, b_sm=True, tm=2048, tn=512, rows=256, ep_in=((conv_b_in, "row"),),
               ep_fn=lambda acc, b: (acc + b,), outs=(("tile", BF),))
    (gathered,) = _gather_wait("small_weights_wait", [small_land], small_sems, ALL_OTHERS, (u,))
    chips = [_unpack(gathered[2 * s], [a.shape for a in sharded_small]) for s in range(N_SHARD)]
    dw_f, attn_norm_f, b_qkv_f, b_o_f = (jnp.concatenate([chips[s][t] for s in range(N_SHARD)], axis=-1)
                                         for t in range(len(sharded_small)))
    dw_pad = jnp.pad(dw_f, ((0, HALO - CONV_W), (0, 0)))
    cv, s_act = _conv_fwd(u, dw_pad, conv_dw_b, conv_ln_g, conv_ln_b)
    (g_out,) = gathered_group(1, "gather_wait_conv_out", (s_act,))
    w_out_f = g_out.reshape(Dm, Dm)
    x1, h1 = _mm("conv_out", s_act, w_out_f, nt=False, tm=1024, tn=1024, rows=256,
                 ep_in=((conv_b_out, "row"), (x2d, "tile"), (mlp_norm_g[0:1], "row")), ep_fn=_residual_norm_ep,
                 outs=RESIDUAL_NORM_OUTS)

    g_up0, g_down0 = gathered_group(2, "gather_wait_mlp0", (x1,))
    w_up_sm = [g_up0, None]
    w_down_f = [g_down0.reshape(D_FF, Dm), None]
    up0 = _mlp_up(0, h1, w_up_sm[0])
    x2, h2 = _mm("mlp0_down", up0, w_down_f[0], nt=False, tm=512, tn=1024, rows=256, a_fn=_relu2,
                 ep_in=((x1, "tile"), (attn_norm_f, "row")), ep_fn=_residual_norm_ep, outs=RESIDUAL_NORM_OUTS)

    g_qkv, g_o = gathered_group(3, "gather_wait_attn", (x2,))
    w_qkv_f = jnp.transpose(g_qkv, (1, 0, 2)).reshape(Dm, QKV_DIM)
    w_o_f = g_o.reshape(ATTN_DIM, Dm)
    (qkv,) = _mm("attn_qkv", h2, w_qkv_f, nt=False, tm=1024, tn=QKV_DIM, rows=256, ep_in=((b_qkv_f, "row"),),
                 ep_fn=lambda acc, b: (acc + b,), outs=(("tile", F32),))
    qg_t = jnp.tile(q_norm_g, (1, N_HEADS))
    kg_t = jnp.tile(k_norm_g, (1, N_KV))
    qn, kn, vv = _qk_norm_fwd(qkv, qg_t, kg_t)
    sinks1 = sinks[0]
    att = _attn_fwd(qn, kn, vv, bias, sinks1)
    x3, h3 = _mm("attn_out", att, w_o_f, nt=False, tm=1024, tn=1024, rows=256,
                 ep_in=((b_o_f, "row"), (x2, "tile"), (mlp_norm_g[1:2], "row")), ep_fn=_residual_norm_ep,
                 outs=RESIDUAL_NORM_OUTS)

    g_up1, g_down1 = gathered_group(4, "gather_wait_mlp1", (x3,))
    w_up_sm[1] = g_up1
    w_down_f[1] = g_down1.reshape(D_FF, Dm)
    up1 = _mlp_up(1, h3, w_up_sm[1])

    def loss_ep(acc, r, t):
        diff = acc + r - t
        dy = diff * (1.0 / Dm)
        return dy, dy, jnp.sum(diff * diff, axis=0, keepdims=True)

    dy, dy_bf, sq = _mm("mlp1_down_loss", up1, w_down_f[1], nt=False, tm=512, tn=1024, rows=256, a_fn=_relu2,
                        ep_in=((x3, "tile"), (tgt, "tile")), ep_fn=loss_ep,
                        outs=(("tile", F32), ("tile", BF), ("colsum", F32)))

    place = jnp.stack([shard, lax.axis_index("c")]).astype(jnp.int32)
    dx3, dx3_bf, dg_mlp1, db_o, dup1 = _mlp_bwd(1, dy, dy_bf, x3, mlp_norm_g[1:2], up1, w_up_sm[1], w_down_f[1])
    dw_down1 = _mm_tn("mlp1_dw_down", up1, dy_bf, tm=1024, tn=1024, tk=2048, a_fn=_relu2)
    dw_up1 = _mm_tn("mlp1_dw_up", h3, dup1, tm=1024, tn=1024, tk=2048, out_sm=N_SHARD)
    red_mlp1 = _Reduction("mlp1", [dw_up1, dw_down1.reshape(N_SHARD, D_FF // N_SHARD, Dm)], place)

    ident = lambda acc: (acc,)
    (datt,) = _mm("attn_dout", dx3_bf, w_o_f, nt=True, tm=1024, tn=1024, rows=256, ep_fn=ident, outs=(("tile", BF),),
                  deps=(red_mlp1.token,))
    dw_o = _mm_tn("attn_dw_o", att, dx3_bf, tm=1024, tn=1024, tk=2048)
    dqn, dkn, dvv, dbias, dsinks = _attn_bwd(qn, kn, vv, bias, sinks1, datt)
    drel = _bias_grad(dbias, bucket)
    dqkv, db_qkv, dqg_t, dkg_t = _qk_norm_bwd(qkv, dqn, dkn, dvv, qg_t, kg_t)
    dw_qkv = _mm_tn("attn_dw_qkv", h2, dqkv, tm=1024, tn=QKV_DIM, tk=2048)
    red_attn = _Reduction("attn", [jnp.transpose(dw_qkv.reshape(Dm, N_SHARD, QKV_DIM // N_SHARD), (1, 0, 2)),
                                   dw_o.reshape(N_SHARD, ATTN_DIM // N_SHARD, Dm)], place)
    dx2, dx2_bf, dg_attn, _ = _mm("attn_dx", dqkv, w_qkv_f, nt=True, tm=512, tn=1024, rows=256,
                                  ep_in=((x2, "tile"), (attn_norm_f, "row"), (dx3, "tile")), ep_fn=_rms_bwd_ep,
                                  outs=RMS_BWD_OUTS, deps=(red_attn.token,))

    dx1, dx1_bf, dg_mlp0, db_out, dup0 = _mlp_bwd(0, dx2, dx2_bf, x1, mlp_norm_g[0:1], up0, w_up_sm[0], w_down_f[0])
    dw_down0 = _mm_tn("mlp0_dw_down", up0, dx2_bf, tm=1024, tn=1024, tk=2048, a_fn=_relu2)
    dw_up0 = _mm_tn("mlp0_dw_up", h1, dup0, tm=1024, tn=1024, tk=2048, out_sm=N_SHARD)
    dw_out = _mm_tn("conv_dw_out", s_act, dx1_bf, tm=1024, tn=1024, tk=2048)
    red_mlp0 = _Reduction("mlp0", [dw_up0, dw_down0.reshape(N_SHARD, D_FF // N_SHARD, Dm),
                                   dw_out.reshape(N_SHARD, Dm // N_SHARD, Dm)], place)
    (r_qkv, r_o) = red_attn.finish((dx1,))
    (r_up1, r_down1) = red_mlp1.finish((dx1,))

    dcv, dln_g, dln_b, ddw_b = _mm("conv_ds", dx1_bf, w_out_f, nt=True, tm=512, tn=1024, rows=256,
                                   ep_in=((cv, "tile"), (conv_ln_g, "row"), (conv_ln_b, "row")),
                                   ep_fn=_ln_silu_bwd_ep,
                                   outs=(("tile", F32), ("colsum", F32), ("colsum", F32), ("colsum", F32)),
                                   deps=(red_mlp0.token,))
    du, db_in, ddw8 = _conv_bwd(u, dcv, dw_pad)
    (r_up0, r_down0, r_out) = red_mlp0.finish((du,))
    dw_in = _mm_tn("conv_dw_in", h0, du, tm=1024, tn=512, tk=4096, out_sm=N_SHARD)
    red_conv = _Reduction("conv", [dw_in], place)
    def first_layer_ep(*args):
        tot, _, dg, _ = _rms_bwd_ep(*args)
        return tot, dg

    gx, dg_conv = _mm("conv_dx", du, w_in_sm, nt=True, b_sm=True, tm=512, tn=1024, rows=256,
                      ep_in=((x2d, "tile"), (conv_norm_g, "row"), (dx1, "tile")), ep_fn=first_layer_ep,
                      outs=(("tile", F32), ("colsum", F32)), deps=(red_conv.token,))
    (r_in,) = red_conv.finish((gx,))

    dqg = dqg_t.reshape(N_HEADS, HEAD_DIM).sum(axis=0, keepdims=True)
    dkg = dkg_t.reshape(N_KV, HEAD_DIM).sum(axis=0, keepdims=True)
    small_full = [dg_conv, db_in, ddw8.sum(axis=1)[:CONV_W], ddw_b, dln_g, dln_b, db_out, dg_attn, db_qkv, dqg, dkg,
                  dsinks[None, :], db_o, drel.reshape(1, REL_BUCKETS * N_HEADS),
                  jnp.pad(dg_mlp0, ((0, 1), (0, 0))) + jnp.pad(dg_mlp1, ((1, 0), (0, 0))), sq]
    (sg_sems,), (sg_land,), sg_token = _gather_start(
        "small_grads_start", [own_slot(_pack(small_full), 8, me)], ((0,),), ALL_OTHERS, after=())

    mine = [r_in, r_out, r_qkv, r_o, r_up0, r_up1, r_down0, r_down1]
    r_in, r_out, r_qkv, r_o, r_up0, r_up1, r_down0, r_down1 = zip(
        mine, _join_halves("join_halves", mine, deps=(sg_token,)))

    big_out = {}
    for nm, w, m, v, gs in (("conv_w_in", conv_w_in, m_conv_w_in, v_conv_w_in, (r_in,)),
                            ("conv_w_out", conv_w_out, m_conv_w_out, v_conv_w_out, (r_out,)),
                            ("w_qkv", w_qkv, m_w_qkv, v_w_qkv, (r_qkv,)),
                            ("w_o", w_o, m_w_o, v_w_o, (r_o,)),
                            ("w_up", w_up, m_w_up, v_w_up, (r_up0, r_up1)),
                            ("w_down", w_down, m_w_down, v_w_down, (r_down0, r_down1))):
        big_out[nm] = _adamw(f"adamw_{nm}", w, m, v, gs)

    (sg_land,) = _gather_wait("small_grads_wait", [sg_land], sg_sems, ALL_OTHERS,
                              [big_out[nm][0] for nm in big_out])
    small_sum = _sum8("small_grads_sum", sg_land)
    (r_norm, r_b_in, r_dw, r_dw_b, r_ln_g, r_ln_b, r_b_out, r_attn_norm, r_b_qkv, r_qg, r_kg, r_sinks, r_b_o, r_rel,
     r_mlp_norm, r_sq) = _unpack(small_sum, [a.shape for a in small_full])
    loss = 0.5 * jnp.sum(r_sq) * (1.0 / Dm)

    def cols(a, width):
        return lax.dynamic_slice_in_dim(a, shard * width, width, axis=a.ndim - 1)

    small_names = ["conv_norm_g", "conv_b_in", "conv_dw", "conv_dw_b", "conv_ln_g", "conv_ln_b", "conv_b_out",
                   "attn_norm_g", "b_qkv", "q_norm_g", "k_norm_g", "sinks", "b_o", "rel_bias", "mlp_norm_g"]
    small_g = [r_norm, r_b_in, cols(r_dw, Dm // N_SHARD)[None], r_dw_b, r_ln_g, r_ln_b, r_b_out,
               cols(r_attn_norm, Dm // N_SHARD), cols(r_b_qkv, QKV_DIM // N_SHARD), r_qg, r_kg, r_sinks,
               cols(r_b_o, Dm // N_SHARD), r_rel.reshape(REL_BUCKETS, N_HEADS), r_mlp_norm]
    small_w = [conv_norm_g, conv_b_in, conv_dw, conv_dw_b, conv_ln_g, conv_ln_b, conv_b_out, attn_norm_g, b_qkv,
               q_norm_g, k_norm_g, sinks, b_o, rel_bias, mlp_norm_g]
    small_m = [m_conv_norm_g, m_conv_b_in, m_conv_dw, m_conv_dw_b, m_conv_ln_g, m_conv_ln_b, m_conv_b_out,
               m_attn_norm_g, m_b_qkv, m_q_norm_g, m_k_norm_g, m_sinks, m_b_o, m_rel_bias, m_mlp_norm_g]
    small_v = [v_conv_norm_g, v_conv_b_in, v_conv_dw, v_conv_dw_b, v_conv_ln_g, v_conv_ln_b, v_conv_b_out,
               v_attn_norm_g, v_b_qkv, v_q_norm_g, v_k_norm_g, v_sinks, v_b_o, v_rel_bias, v_mlp_norm_g]
    flat2 = lambda a: a.reshape(-1, a.shape[-1])
    small_g = [flat2(g) for g in small_g]
    d_s, m_s, v_s = _adamw_small([flat2(w) for w in small_w], small_g, [flat2(m) for m in small_m],
                                 [flat2(v) for v in small_v])
    small_out = {}
    for nm, w, g, d, m2, v2 in zip(small_names, small_w, small_g, d_s, m_s, v_s):
        small_out[nm] = tuple(a.reshape(w.shape) for a in (g, d, m2, v2))

    order = ["conv_norm_g", "conv_w_in", "conv_b_in", "conv_dw", "conv_dw_b", "conv_ln_g", "conv_ln_b", "conv_w_out",
             "conv_b_out", "attn_norm_g", "w_qkv", "b_qkv", "q_norm_g", "k_norm_g", "sinks", "w_o", "b_o", "rel_bias",
             "mlp_norm_g", "w_up", "w_down"]
    res = {**small_out, **big_out}
    outs = [loss, gx[None]]
    for part in range(4):
        outs += [res[nm][part] for nm in order]
    return tuple(outs)
```

```python
import math

import numpy as np
import jax
import jax.numpy as jnp
from jax import lax
from jax.experimental import pallas as pl
from jax.experimental.pallas import tpu as pltpu

F32 = jnp.float32
BF = jnp.bfloat16
MESH = pl.DeviceIdType.MESH

D_MODEL = 1024
D_FF = 4096
N_HEADS = 16
N_KV = 2
GROUP = N_HEADS // N_KV
HEAD_DIM = 64
ATTN_DIM = N_HEADS * HEAD_DIM
KV_DIM = N_KV * HEAD_DIM
QKV_DIM = ATTN_DIM + 2 * KV_DIM
BLOCK = 128
CONV_W = 31
HALO = 32
REL_BUCKETS = 32
REL_MAX_DIST = 128
NORM_EPS = 1e-6
NEG_INF = -1e30
N_SHARD = 4
LANES = 1024

ADAM_LR = 0.001
ADAM_B1 = 0.9
ADAM_B2 = 0.999
ADAM_EPS = 1e-08
ADAM_WD = 0.01
ADAM_STEP = 10

VMEM_LIMIT = 56 * 1024 * 1024


def _params(n_axes):
    return pltpu.CompilerParams(dimension_semantics=("arbitrary",) * n_axes, vmem_limit_bytes=VMEM_LIMIT)


def _dot(a, b, ca, cb):
    return lax.dot_general(a, b, (((ca,), (cb,)), ((), ())), preferred_element_type=F32)


def _mm(name, a, b, *, nt, tm, tn, ep_fn, outs, a_fn=None, b_sm=False, ep_in=(), deps=(), rows=None):
    M, K = a.shape
    rows = tm if rows is None else rows
    if b_sm:
        S, ks = b.shape[0], b.shape[2]
        N, per = (b.shape[1], None) if nt else (S * b.shape[2], b.shape[2] // tn)
        assert (S * ks == K) if nt else (b.shape[1] == K)
    else:
        N = b.shape[0] if nt else b.shape[1]
        assert (b.shape[1] if nt else b.shape[0]) == K
    assert M % tm == 0 and N % tn == 0 and tm % rows == 0
    ne, no, nd = len(ep_in), len(outs), len(deps)

    def body(a_ref, b_ref, *rest):
        ep_refs, out_refs = rest[:ne], rest[ne + nd:ne + nd + no]
        i = pl.program_id(1)
        sums = [None] * no
        for r in range(tm // rows):
            rs = pl.ds(r * rows, rows)

            def lhs(cols):
                av = a_ref[rs, cols]
                return (av if a_fn is None else a_fn(av)).astype(BF)

            if b_sm and nt:
                acc = None
                for s in range(S):
                    part = _dot(lhs(pl.ds(s * ks, ks)), b_ref[s].astype(BF), 1, 1)
                    acc = part if acc is None else acc + part
            else:
                acc = _dot(lhs(slice(None)), b_ref[...].astype(BF), 1, 1 if nt else 0)
            ep_vals = [ref[rs, :] if kind == "tile" else ref[...] for ref, (_, kind) in zip(ep_refs, ep_in)]
            vals = ep_fn(acc, *ep_vals)
            for o, ((kind, dt, *_), ref, val) in enumerate(zip(outs, out_refs, vals)):
                if kind == "tile":
                    ref[rs, :] = val.astype(dt)
                else:
                    sums[o] = val if sums[o] is None else sums[o] + val
        for (kind, *_), ref, val in zip(outs, out_refs, sums):
            if kind == "colsum":
                @pl.when(i == 0)
                def _():
                    ref[...] = val

                @pl.when(i > 0)
                def _():
                    ref[...] += val

    if b_sm and nt:
        b_spec = pl.BlockSpec((S, tn, ks), lambda j, i: (0, j, 0))
    elif b_sm:
        b_spec = pl.BlockSpec((None, K, tn), lambda j, i: (j // per, 0, j % per))
    elif nt:
        b_spec = pl.BlockSpec((tn, K), lambda j, i: (j, 0))
    else:
        b_spec = pl.BlockSpec((K, tn), lambda j, i: (0, j))
    in_specs = [pl.BlockSpec((tm, K), lambda j, i: (i, 0)), b_spec]
    for arr, kind in ep_in:
        if kind == "tile":
            assert arr.shape == (M, N)
            in_specs.append(pl.BlockSpec((tm, tn), lambda j, i: (i, j)))
        elif kind == "whole":
            in_specs.append(pl.BlockSpec(arr.shape, lambda j, i, rank=arr.ndim: (0,) * rank))
        else:
            assert arr.shape == (1, N)
            in_specs.append(pl.BlockSpec((1, tn), lambda j, i: (0, j)))
    in_specs += [pl.BlockSpec(memory_space=pl.ANY)] * nd
    out_shape, out_specs = [], []
    for kind, dt, *width in outs:
        if kind == "tile" and width:
            assert tn == N
            out_shape.append(jax.ShapeDtypeStruct((M, width[0]), dt))
            out_specs.append(pl.BlockSpec((tm, width[0]), lambda j, i: (i, 0)))
        elif kind == "tile":
            out_shape.append(jax.ShapeDtypeStruct((M, N), dt))
            out_specs.append(pl.BlockSpec((tm, tn), lambda j, i: (i, j)))
        else:
            out_shape.append(jax.ShapeDtypeStruct((1, N), F32))
            out_specs.append(pl.BlockSpec((1, tn), lambda j, i: (0, j)))
    return pl.pallas_call(
        body, name=name, grid=(N // tn, M // tm), in_specs=in_specs, out_specs=out_specs, out_shape=out_shape,
        compiler_params=_params(2),
    )(a, b, *[arr for arr, _ in ep_in], *deps)


def _mm_tn(name, a, b, *, tm, tn, tk, a_fn=None, out_sm=None):
    T, Ka = a.shape
    N = b.shape[1]
    assert b.shape[0] == T and T % tk == 0 and Ka % tm == 0 and N % tn == 0
    nk = T // tk

    def body(a_ref, b_ref, o_ref, acc_ref):
        k = pl.program_id(2)

        @pl.when(k == 0)
        def _():
            acc_ref[...] = jnp.zeros_like(acc_ref)

        av = a_ref[...]
        if a_fn is not None:
            av = a_fn(av)
        acc_ref[...] += _dot(av.astype(BF), b_ref[...].astype(BF), 0, 0)

        @pl.when(k == nk - 1)
        def _():
            o_ref[...] = acc_ref[...].astype(BF)

    if out_sm is None:
        out_shape = jax.ShapeDtypeStruct((Ka, N), BF)
        out_spec = pl.BlockSpec((tm, tn), lambda i, j, k: (i, j))
    else:
        per = (N // out_sm) // tn
        assert per * tn * out_sm == N
        out_shape = jax.ShapeDtypeStruct((out_sm, Ka, N // out_sm), BF)
        out_spec = pl.BlockSpec((None, tm, tn), lambda i, j, k: (j // per, i, j % per))
    return pl.pallas_call(
        body, name=name, grid=(Ka // tm, N // tn, nk),
        in_specs=[pl.BlockSpec((tk, tm), lambda i, j, k: (k, i)), pl.BlockSpec((tk, tn), lambda i, j, k: (k, j))],
        out_specs=out_spec, out_shape=out_shape, scratch_shapes=[pltpu.VMEM((tm, tn), F32)],
        compiler_params=_params(3),
    )(a, b)


def _relu2(v):
    r = jnp.maximum(v.astype(F32), 0.0)
    return r * r


def _rms_bwd_ep(dh, x, g, dres):
    rstd = lax.rsqrt(jnp.mean(x * x, axis=-1, keepdims=True) + NORM_EPS)
    xh = x * rstd
    dxh = dh * g
    dx = rstd * (dxh - xh * jnp.mean(dxh * xh, axis=-1, keepdims=True))
    tot = dres + dx
    return tot, tot, jnp.sum(dh * xh, axis=0, keepdims=True), jnp.sum(tot, axis=0, keepdims=True)


def _rms_fwd(name, x, g, tm=512, deps=()):
    T, Dm = x.shape

    def body(x_ref, g_ref, *rest):
        o_ref = rest[-1]
        xv = x_ref[...]
        rstd = lax.rsqrt(jnp.mean(xv * xv, axis=-1, keepdims=True) + NORM_EPS)
        o_ref[...] = (xv * rstd * g_ref[...]).astype(BF)

    return pl.pallas_call(
        body, name=name, grid=(T // tm,),
        in_specs=[pl.BlockSpec((tm, Dm), lambda i: (i, 0)), pl.BlockSpec((1, Dm), lambda i: (0, 0))]
        + [pl.BlockSpec(memory_space=pl.ANY)] * len(deps),
        out_specs=pl.BlockSpec((tm, Dm), lambda i: (i, 0)), out_shape=jax.ShapeDtypeStruct((T, Dm), BF),
        compiler_params=_params(1),
    )(x, g, *deps)


HEAD_COLS = 128


def _two_term_dot(v, m):
    hi = v.astype(BF)
    lo = (v - hi.astype(F32)).astype(BF)
    return _dot(hi, m, 1, 0) + _dot(lo, m, 1, 0)


def _head_sum(v, select):
    sel, sel_t = select
    return _two_term_dot(_two_term_dot(v, sel), sel_t)


def _head_select(n):
    sel = (np.arange(n)[:, None] // HEAD_DIM == np.arange(HEAD_COLS)[None, :]).astype(np.float32)
    return jnp.asarray(sel, dtype=BF), jnp.asarray(sel.T, dtype=BF)


def _qk_normed(x, g, select, scale):
    r = lax.rsqrt(_head_sum(x * x, select) * (1.0 / HEAD_DIM) + NORM_EPS)
    return x * r * g * scale


def _qk_norm_bwd(qkv, dqn, dkn, dv, qg_t, kg_t, tm=256):
    T = qkv.shape[0]

    def body(x_ref, dq_ref, dk_ref, dv_ref, qg_ref, kg_ref, sq_ref, sqt_ref, sk_ref, skt_ref,
             o_ref, db_ref, dqg_ref, dkg_ref):
        i = pl.program_id(0)

        def one(x, dy, g, select):
            r = lax.rsqrt(_head_sum(x * x, select) * (1.0 / HEAD_DIM) + NORM_EPS)
            xh = x * r
            dxh = dy * g
            dx = r * (dxh - xh * (_head_sum(dxh * xh, select) * (1.0 / HEAD_DIM)))
            return dx, jnp.sum(dy * xh, axis=0, keepdims=True)

        dq, dqg = one(x_ref[:, pl.ds(0, ATTN_DIM)], dq_ref[...], qg_ref[...], (sq_ref[...], sqt_ref[...]))
        dk, dkg = one(x_ref[:, pl.ds(ATTN_DIM, KV_DIM)], dk_ref[...], kg_ref[...], (sk_ref[...], skt_ref[...]))
        dvv = dv_ref[...]
        o_ref[:, pl.ds(0, ATTN_DIM)] = dq.astype(BF)
        o_ref[:, pl.ds(ATTN_DIM, KV_DIM)] = dk.astype(BF)
        o_ref[:, pl.ds(ATTN_DIM + KV_DIM, KV_DIM)] = dvv.astype(BF)
        sq, sk, sv = (jnp.sum(t, axis=0, keepdims=True) for t in (dq, dk, dvv))

        @pl.when(i == 0)
        def _():
            db_ref[:, pl.ds(0, ATTN_DIM)] = sq
            db_ref[:, pl.ds(ATTN_DIM, KV_DIM)] = sk
            db_ref[:, pl.ds(ATTN_DIM + KV_DIM, KV_DIM)] = sv
            dqg_ref[...] = dqg
            dkg_ref[...] = dkg

        @pl.when(i > 0)
        def _():
            db_ref[:, pl.ds(0, ATTN_DIM)] += sq
            db_ref[:, pl.ds(ATTN_DIM, KV_DIM)] += sk
            db_ref[:, pl.ds(ATTN_DIM + KV_DIM, KV_DIM)] += sv
            dqg_ref[...] += dqg
            dkg_ref[...] += dkg

    full = lambda shape: pl.BlockSpec(shape, lambda i: (0, 0))
    row = lambda n: pl.BlockSpec((tm, n), lambda i: (i, 0))
    return pl.pallas_call(
        body, name="qk_norm_bwd", grid=(T // tm,),
        in_specs=[row(QKV_DIM), row(ATTN_DIM), row(KV_DIM), row(KV_DIM), full((1, ATTN_DIM)), full((1, KV_DIM)),
                  full((ATTN_DIM, HEAD_COLS)), full((HEAD_COLS, ATTN_DIM)), full((KV_DIM, HEAD_COLS)), full((HEAD_COLS, KV_DIM))],
        out_specs=[row(QKV_DIM), full((1, QKV_DIM)), full((1, ATTN_DIM)), full((1, KV_DIM))],
        out_shape=[jax.ShapeDtypeStruct((T, QKV_DIM), BF), jax.ShapeDtypeStruct((1, QKV_DIM), F32),
                   jax.ShapeDtypeStruct((1, ATTN_DIM), F32), jax.ShapeDtypeStruct((1, KV_DIM), F32)],
        compiler_params=_params(1),
    )(qkv, dqn, dkn, dv, qg_t, kg_t, *_head_select(ATTN_DIM), *_head_select(KV_DIM))


ROWS = 64
COLS = 128


SUBLANES = 8
FIRST_TAP = HALO - (CONV_W - 1)


def _glu(a, g):
    return a.astype(F32) * jax.nn.sigmoid(g.astype(F32))


def _shifted(xe, s):
    return xe if s == 0 else pltpu.roll(xe, ROWS + HALO - s, axis=0)


def _conv_fwd(u, dw_pad, dw_b, ln_g, ln_b, tm=256):
    T = u.shape[0]
    Dm = D_MODEL
    hpt = tm // HALO

    def body(ac_ref, gc_ref, ap_ref, gp_ref, w_ref, wb_ref, lg_ref, lb_ref, cv_ref, s_ref, ext):
        i = pl.program_id(0)
        ext[pl.ds(0, HALO), :] = jnp.where(i > 0, _glu(ap_ref[...], gp_ref[...]), 0.0)
        ext[pl.ds(HALO, tm), :] = _glu(ac_ref[...], gc_ref[...])

        def rows(r, carry):
            r0 = pl.multiple_of(r * ROWS, ROWS)
            for c in range(Dm // COLS):
                cs = pl.ds(c * COLS, COLS)
                xe = ext[pl.ds(r0, ROWS + HALO), cs]
                acc = jnp.zeros((ROWS, COLS), F32)
                for s in range(SUBLANES):
                    xs = _shifted(xe, s)
                    for j in range(CONV_W):
                        off = FIRST_TAP + j
                        if off % SUBLANES == s:
                            acc = acc + xs[off - s:off - s + ROWS, :] * w_ref[pl.ds(j, 1), cs]
                cv_ref[pl.ds(r0, ROWS), cs] = acc + wb_ref[:, cs]
            return carry

        lax.fori_loop(0, tm // ROWS, rows, 0)
        cv = cv_ref[...]
        xc = cv - jnp.mean(cv, axis=-1, keepdims=True)
        y = xc * lax.rsqrt(jnp.mean(xc * xc, axis=-1, keepdims=True) + NORM_EPS) * lg_ref[...] + lb_ref[...]
        s_ref[...] = (y * jax.nn.sigmoid(y)).astype(BF)

    full = lambda shape: pl.BlockSpec(shape, lambda i: (0, 0))
    return pl.pallas_call(
        body, name="conv_fwd", grid=(T // tm,),
        in_specs=[pl.BlockSpec((tm, Dm), lambda i: (i, 0)), pl.BlockSpec((tm, Dm), lambda i: (i, 1)),
                  pl.BlockSpec((HALO, Dm), lambda i: (jnp.maximum(i * hpt - 1, 0), 0)),
                  pl.BlockSpec((HALO, Dm), lambda i: (jnp.maximum(i * hpt - 1, 0), 1)),
                  full((HALO, Dm)), full((1, Dm)), full((1, Dm)), full((1, Dm))],
        out_specs=[pl.BlockSpec((tm, Dm), lambda i: (i, 0)), pl.BlockSpec((tm, Dm), lambda i: (i, 0))],
        out_shape=[jax.ShapeDtypeStruct((T, Dm), F32), jax.ShapeDtypeStruct((T, Dm), BF)],
        scratch_shapes=[pltpu.VMEM((tm + HALO, Dm), F32)],
        compiler_params=_params(1),
    )(u, u, u, u, dw_pad, dw_b, ln_g, ln_b)


def _ln_silu_bwd_ep(ds, cv, lg, lb):
    xc = cv - jnp.mean(cv, axis=-1, keepdims=True)
    rstd = lax.rsqrt(jnp.mean(xc * xc, axis=-1, keepdims=True) + NORM_EPS)
    xh = xc * rstd
    y = xh * lg + lb
    sg = jax.nn.sigmoid(y)
    dy = ds * (sg * (1.0 + y * (1.0 - sg)))
    dxh = dy * lg
    dcv = rstd * (dxh - jnp.mean(dxh, axis=-1, keepdims=True) - xh * jnp.mean(dxh * xh, axis=-1, keepdims=True))
    return (dcv, jnp.sum(dy * xh, axis=0, keepdims=True), jnp.sum(dy, axis=0, keepdims=True),
            jnp.sum(dcv, axis=0, keepdims=True))


def _conv_bwd(u, dcv, dw_pad, tm=256):
    T = u.shape[0]
    Dm = D_MODEL
    hpt = tm // HALO
    last = T // HALO - 1
    nt = T // tm

    def body(ac_ref, gc_ref, ap_ref, gp_ref, dc_ref, dn_ref, w_ref, du_ref, db_ref, dw_ref, ext_g, ext_d):
        i = pl.program_id(0)
        ext_g[pl.ds(0, HALO), :] = jnp.where(i > 0, _glu(ap_ref[...], gp_ref[...]), 0.0)
        ext_g[pl.ds(HALO, tm), :] = _glu(ac_ref[...], gc_ref[...])
        ext_d[pl.ds(0, tm), :] = dc_ref[...]
        ext_d[pl.ds(tm, HALO), :] = jnp.where(i < nt - 1, dn_ref[...], 0.0)

        @pl.when(i == 0)
        def _():
            db_ref[...] = jnp.zeros_like(db_ref)
            dw_ref[...] = jnp.zeros_like(dw_ref)

        def rows(r, carry):
            r0 = pl.multiple_of(r * ROWS, ROWS)
            rs = pl.ds(r0, ROWS)
            for c in range(Dm // COLS):
                cs = pl.ds(c * COLS, COLS)
                cs2 = pl.ds(Dm + c * COLS, COLS)
                de = ext_d[pl.ds(r0, ROWS + HALO), cs]
                ge = ext_g[pl.ds(r0, ROWS + HALO), cs]
                dcur = de[0:ROWS, :]
                acc = jnp.zeros((ROWS, COLS), F32)
                for s in range(SUBLANES):
                    ds_, gs_ = _shifted(de, s), _shifted(ge, s)
                    for j in range(CONV_W):
                        off = CONV_W - 1 - j
                        if off % SUBLANES == s:
                            acc = acc + ds_[off - s:off - s + ROWS, :] * w_ref[pl.ds(j, 1), cs]
                        goff = FIRST_TAP + j
                        if goff % SUBLANES == s:
                            prod = dcur * gs_[goff - s:goff - s + ROWS, :]
                            dw_ref[j, :, cs] += jnp.sum(prod.reshape(ROWS // SUBLANES, SUBLANES, COLS), axis=0)
                a = ac_ref[rs, cs].astype(F32)
                sg = jax.nn.sigmoid(gc_ref[rs, cs].astype(F32))
                da = acc * sg
                dg = acc * a * sg * (1.0 - sg)
                du_ref[rs, cs] = da.astype(BF)
                du_ref[rs, cs2] = dg.astype(BF)
                db_ref[:, cs] += jnp.sum(da, axis=0, keepdims=True)
                db_ref[:, cs2] += jnp.sum(dg, axis=0, keepdims=True)
            return carry

        lax.fori_loop(0, tm // ROWS, rows, 0)

    return pl.pallas_call(
        body, name="conv_bwd", grid=(nt,),
        in_specs=[pl.BlockSpec((tm, Dm), lambda i: (i, 0)), pl.BlockSpec((tm, Dm), lambda i: (i, 1)),
                  pl.BlockSpec((HALO, Dm), lambda i: (jnp.maximum(i * hpt - 1, 0), 0)),
                  pl.BlockSpec((HALO, Dm), lambda i: (jnp.maximum(i * hpt - 1, 0), 1)),
                  pl.BlockSpec((tm, Dm), lambda i: (i, 0)),
                  pl.BlockSpec((HALO, Dm), lambda i: (jnp.minimum((i + 1) * hpt, last), 0)),
                  pl.BlockSpec((HALO, Dm), lambda i: (0, 0))],
        out_specs=[pl.BlockSpec((tm, 2 * Dm), lambda i: (i, 0)), pl.BlockSpec((1, 2 * Dm), lambda i: (0, 0)),
                   pl.BlockSpec((HALO, 8, Dm), lambda i: (0, 0, 0))],
        out_shape=[jax.ShapeDtypeStruct((T, 2 * Dm), BF), jax.ShapeDtypeStruct((1, 2 * Dm), F32),
                   jax.ShapeDtypeStruct((HALO, 8, Dm), F32)],
        scratch_shapes=[pltpu.VMEM((tm + HALO, Dm), F32), pltpu.VMEM((tm + HALO, Dm), F32)],
        compiler_params=_params(1),
    )(u, u, u, u, dcv, dcv, dw_pad)


def _bucket_table():
    q_loc = np.arange(BLOCK)[:, None]
    k_loc = np.arange(2 * BLOCK)[None, :]
    dist = q_loc + BLOCK - k_loc
    n = np.maximum(dist, 0)
    max_exact = REL_BUCKETS // 2
    large = max_exact + (np.log(np.maximum(n, 1).astype(np.float32) / max_exact)
                         / math.log(REL_MAX_DIST / max_exact) * (REL_BUCKETS - max_exact)).astype(np.int32)
    large = np.minimum(large, REL_BUCKETS - 1)
    bucket = np.where(n < max_exact, n, large).astype(np.int32)
    return jnp.asarray(np.where((dist >= 0) & (dist < BLOCK), bucket, -1).astype(np.int32))


def _bias_table(rel_bias, bucket):
    def body(rb_ref, bk_ref, o_ref):
        bk = bk_ref[...]
        for h in range(N_HEADS):
            acc = jnp.full((BLOCK, 2 * BLOCK), NEG_INF, F32)
            for b in range(REL_BUCKETS):
                acc = jnp.where(bk == b, rb_ref[b, h], acc)
            o_ref[h] = acc

    return pl.pallas_call(
        body, name="bias_table", out_shape=jax.ShapeDtypeStruct((N_HEADS, BLOCK, 2 * BLOCK), F32),
        in_specs=[pl.BlockSpec(memory_space=pltpu.SMEM), pl.BlockSpec(memory_space=pltpu.VMEM)],
        out_specs=pl.BlockSpec(memory_space=pltpu.VMEM),
    )(rel_bias, bucket)


def _bias_grad(dbias, bucket):
    def body(db_ref, bk_ref, o_ref):
        bk = bk_ref[...]
        for b in range(REL_BUCKETS):
            sel = bk == b
            for h in range(N_HEADS):
                o_ref[b, h] = jnp.sum(jnp.where(sel, db_ref[h], 0.0))

    return pl.pallas_call(
        body, name="bias_grad", out_shape=jax.ShapeDtypeStruct((REL_BUCKETS, N_HEADS), F32),
        in_specs=[pl.BlockSpec(memory_space=pltpu.VMEM), pl.BlockSpec(memory_space=pltpu.VMEM)],
        out_specs=pl.BlockSpec(memory_space=pltpu.SMEM),
    )(dbias, bucket)


GROUP_ROWS = GROUP * BLOCK


def _head_probs(qk, bias_h, sink, first):
    s = jnp.where(first, NEG_INF, qk + bias_h)
    m = jnp.maximum(jnp.max(s, axis=-1, keepdims=True), sink)
    p = jnp.exp(s - m)
    ps = jnp.exp(sink - m)
    inv = 1.0 / (jnp.sum(p, axis=-1, keepdims=True) + ps)
    return p * inv, ps * inv


def _band(prev_ref, cur_ref, g):
    hs = pl.ds(g * HEAD_DIM, HEAD_DIM)
    return jnp.concatenate([prev_ref[:, hs], cur_ref[:, hs]], axis=0)


def _stack_heads(ref, g):
    return jnp.concatenate([ref[:, pl.ds((g * GROUP + hh) * HEAD_DIM, HEAD_DIM)] for hh in range(GROUP)], axis=0)


def _unstack_heads(ref, g, stacked, dtype):
    for hh in range(GROUP):
        ref[:, pl.ds((g * GROUP + hh) * HEAD_DIM, HEAD_DIM)] = stacked[hh * BLOCK:(hh + 1) * BLOCK, :].astype(dtype)


def _first_mask(n):
    col = lax.broadcasted_iota(jnp.int32, (1, 2 * BLOCK), 1)
    return jnp.logical_and(n == 0, col < BLOCK)


def _head_rows(hh):
    return pl.ds(hh * BLOCK, BLOCK)


def _attn_fwd(qn, kn, vv, bias, sinks):
    T = qn.shape[0]
    nb = T // BLOCK

    def body(sk_ref, q_ref, kc_ref, kp_ref, vc_ref, vp_ref, b_ref, o_ref, qk_buf, p_buf):
        first = _first_mask(pl.program_id(0))
        for g in range(N_KV):
            k = _band(kp_ref, kc_ref, g)
            v = _band(vp_ref, vc_ref, g)
            qk_buf[g] = _dot(_stack_heads(q_ref, g), k, 1, 1)
            for hh in range(GROUP):
                h = g * GROUP + hh
                pn, _ = _head_probs(qk_buf[g, _head_rows(hh), :], b_ref[h], sk_ref[h], first)
                p_buf[g, _head_rows(hh), :] = pn.astype(BF)
            _unstack_heads(o_ref, g, _dot(p_buf[g], v, 1, 0), BF)

    cur = lambda n: (n, 0)
    prev = lambda n: (jnp.maximum(n - 1, 0), 0)
    return pl.pallas_call(
        body, name="attn_fwd", grid=(nb,),
        in_specs=[pl.BlockSpec(memory_space=pltpu.SMEM), pl.BlockSpec((BLOCK, ATTN_DIM), cur),
                  pl.BlockSpec((BLOCK, KV_DIM), cur), pl.BlockSpec((BLOCK, KV_DIM), prev),
                  pl.BlockSpec((BLOCK, KV_DIM), cur), pl.BlockSpec((BLOCK, KV_DIM), prev),
                  pl.BlockSpec((N_HEADS, BLOCK, 2 * BLOCK), lambda n: (0, 0, 0))],
        out_specs=pl.BlockSpec((BLOCK, ATTN_DIM), cur), out_shape=jax.ShapeDtypeStruct((T, ATTN_DIM), BF),
        scratch_shapes=[pltpu.VMEM((N_KV, GROUP_ROWS, 2 * BLOCK), F32), pltpu.VMEM((N_KV, GROUP_ROWS, 2 * BLOCK), BF)],
        compiler_params=_params(1),
    )(sinks, qn, kn, kn, vv, vv, bias)


def _attn_bwd(qn, kn, vv, bias, sinks, do):
    T = qn.shape[0]
    nb = T // BLOCK
    scale = 1.0 / math.sqrt(HEAD_DIM)

    def body(sk_ref, q_ref, kc_ref, kp_ref, vc_ref, vp_ref, b_ref, do_ref,
             dq_ref, dk_ref, dv_ref, db_ref, dsk_ref, dk_full, dv_full, dk_carry, dv_carry, qk_buf, dp_buf, p_buf, ds_buf):
        n = pl.program_id(0)

        @pl.when(n == 0)
        def _():
            db_ref[...] = jnp.zeros_like(db_ref)
            dk_carry[...] = jnp.zeros_like(dk_carry)
            dv_carry[...] = jnp.zeros_like(dv_carry)
            for h in range(N_HEADS):
                dsk_ref[h] = 0.0

        @pl.when(n < nb)
        def _():
            first = _first_mask(n)
            for g in range(N_KV):
                k = _band(kp_ref, kc_ref, g)
                v = _band(vp_ref, vc_ref, g)
                q = _stack_heads(q_ref, g)
                dout = _stack_heads(do_ref, g)
                qk_buf[g] = _dot(q, k, 1, 1)
                dp_buf[g] = _dot(dout, v, 1, 1)
                for hh in range(GROUP):
                    h = g * GROUP + hh
                    rows = _head_rows(hh)
                    pn, psink = _head_probs(qk_buf[g, rows, :], b_ref[h], sk_ref[h], first)
                    dp = dp_buf[g, rows, :]
                    delta = jnp.sum(pn * dp, axis=-1, keepdims=True)
                    ds = pn * (dp - delta)
                    dsk_ref[h] += -jnp.sum(psink * delta)
                    db_ref[h] += ds
                    ds_buf[g, rows, :] = ds.astype(BF)
                    p_buf[g, rows, :] = pn.astype(BF)
                dsb = ds_buf[g]
                _unstack_heads(dq_ref, g, _dot(dsb, k, 1, 0) * scale, F32)
                gs = pl.ds(g * HEAD_DIM, HEAD_DIM)
                dk_full[:, gs] = _dot(dsb, q, 0, 0)
                dv_full[:, gs] = _dot(p_buf[g], dout, 0, 0)

        @pl.when(n == nb)
        def _():
            dk_full[...] = jnp.zeros_like(dk_full)
            dv_full[...] = jnp.zeros_like(dv_full)

        dk_ref[...] = dk_carry[...] + dk_full[pl.ds(0, BLOCK), :]
        dv_ref[...] = dv_carry[...] + dv_full[pl.ds(0, BLOCK), :]
        dk_carry[...] = dk_full[pl.ds(BLOCK, BLOCK), :]
        dv_carry[...] = dv_full[pl.ds(BLOCK, BLOCK), :]

    cur = lambda n: (jnp.minimum(n, nb - 1), 0)
    prev = lambda n: (jnp.maximum(jnp.minimum(n, nb - 1) - 1, 0), 0)
    out_kv = lambda n: (jnp.maximum(n - 1, 0), 0)
    return pl.pallas_call(
        body, name="attn_bwd", grid=(nb + 1,),
        in_specs=[pl.BlockSpec(memory_space=pltpu.SMEM), pl.BlockSpec((BLOCK, ATTN_DIM), cur),
                  pl.BlockSpec((BLOCK, KV_DIM), cur), pl.BlockSpec((BLOCK, KV_DIM), prev),
                  pl.BlockSpec((BLOCK, KV_DIM), cur), pl.BlockSpec((BLOCK, KV_DIM), prev),
                  pl.BlockSpec((N_HEADS, BLOCK, 2 * BLOCK), lambda n: (0, 0, 0)),
                  pl.BlockSpec((BLOCK, ATTN_DIM), cur)],
        out_specs=[pl.BlockSpec((BLOCK, ATTN_DIM), cur), pl.BlockSpec((BLOCK, KV_DIM), out_kv),
                   pl.BlockSpec((BLOCK, KV_DIM), out_kv),
                   pl.BlockSpec((N_HEADS, BLOCK, 2 * BLOCK), lambda n: (0, 0, 0)),
                   pl.BlockSpec(memory_space=pltpu.SMEM)],
        out_shape=[jax.ShapeDtypeStruct((T, ATTN_DIM), F32), jax.ShapeDtypeStruct((T, KV_DIM), F32),
                   jax.ShapeDtypeStruct((T, KV_DIM), F32),
                   jax.ShapeDtypeStruct((N_HEADS, BLOCK, 2 * BLOCK), F32), jax.ShapeDtypeStruct((N_HEADS,), F32)],
        scratch_shapes=[pltpu.VMEM((2 * BLOCK, KV_DIM), F32), pltpu.VMEM((2 * BLOCK, KV_DIM), F32),
                        pltpu.VMEM((BLOCK, KV_DIM), F32), pltpu.VMEM((BLOCK, KV_DIM), F32),
                        pltpu.VMEM((N_KV, GROUP_ROWS, 2 * BLOCK), F32), pltpu.VMEM((N_KV, GROUP_ROWS, 2 * BLOCK), F32),
                        pltpu.VMEM((N_KV, GROUP_ROWS, 2 * BLOCK), BF), pltpu.VMEM((N_KV, GROUP_ROWS, 2 * BLOCK), BF)],
        compiler_params=_params(1),
    )(sinks, qn, kn, kn, vv, vv, bias, do)


def _coords():
    return lax.axis_index("x"), lax.axis_index("y"), lax.axis_index("c")


def _sum8(name, blocks):
    def body(b_ref, o_ref):
        tot = b_ref[0]
        for d in range(1, 8):
            tot = tot + b_ref[d]
        o_ref[...] = tot

    return pl.pallas_call(body, name=name, out_shape=jax.ShapeDtypeStruct(blocks.shape[1:], F32))(blocks)


HBM_SPEC = pl.BlockSpec(memory_space=pltpu.HBM)
SEM_SPEC = pl.BlockSpec(memory_space=pltpu.SEMAPHORE)
ANY_SPEC = pl.BlockSpec(memory_space=pl.ANY)
DATAFLOW = pltpu.SideEffectType.DATAFLOW_SIDE_EFFECTING


OTHER_CHIPS = (4, 2, 6)
ALL_OTHERS = (1, 2, 3, 4, 5, 6, 7)


def _slot(x, y, c, peers):
    return 2 * x + y if peers is OTHER_CHIPS else 4 * x + 2 * y + c


def _slot_copy(land, sems, idx, x, y, c, k, peers, arriving):
    send_sems, recv_sems = sems
    px, py, pc = x ^ (k >> 2), y ^ ((k >> 1) & 1), c ^ (k & 1)
    mine = _slot(x, y, c, peers)
    dst = _slot(px, py, pc, peers) if arriving else mine
    return pltpu.make_async_remote_copy(src_ref=land.at[mine], dst_ref=land.at[dst], send_sem=send_sems.at[idx],
                                        recv_sem=recv_sems.at[idx], device_id=(px, py, pc), device_id_type=MESH)


def _gather_start(name, stacks, groups, peers, after):
    n = len(stacks)
    ng = len(groups)
    np_ = len(peers)
    after = tuple(after)

    def body(*refs):
        lands = refs[:n]
        first = n + len(after)
        sems = [(refs[first + 2 * g], refs[first + 2 * g + 1]) for g in range(ng)]
        token = refs[-1]
        x, y, c = _coords()
        for g, members in enumerate(groups):
            for i, t in enumerate(members):
                for j, k in enumerate(peers):
                    _slot_copy(lands[t], sems[g], np_ * i + j, x, y, c, k, peers, arriving=False).start()
        token[...] = jnp.zeros_like(token)

    out_shape = []
    for members in groups:
        out_shape += [pltpu.SemaphoreType.DMA((np_ * len(members),))] * 2
    out_shape += [pltpu.HBM(w.shape, w.dtype) for w in stacks]
    out_shape.append(jax.ShapeDtypeStruct((8, 128), F32))
    res = pl.pallas_call(
        body, name=name, out_shape=out_shape, in_specs=[HBM_SPEC] * n + [ANY_SPEC] * len(after),
        out_specs=[SEM_SPEC] * (2 * ng) + [HBM_SPEC] * n + [pl.BlockSpec(memory_space=pltpu.VMEM)],
        input_output_aliases={t: 2 * ng + t for t in range(n)},
        compiler_params=pltpu.CompilerParams(has_side_effects=DATAFLOW),
    )(*[pltpu.with_memory_space_constraint(w, pltpu.HBM) for w in stacks], *after)
    sems = [(res[2 * g], res[2 * g + 1]) for g in range(ng)]
    return sems, list(res[2 * ng:2 * ng + n]), res[-1]


def _gather_wait(name, stacks, sems, peers, after):
    n = len(stacks)
    after = tuple(after)

    def body(*refs):
        lands = refs[:n]
        group_sems = (refs[n], refs[n + 1])
        x, y, c = _coords()
        for i in range(n):
            for j, k in enumerate(peers):
                cp = _slot_copy(lands[i], group_sems, len(peers) * i + j, x, y, c, k, peers, arriving=True)
                cp.wait_send()
                cp.wait_recv()

    return pl.pallas_call(
        body, name=name, out_shape=[pltpu.HBM(w.shape, w.dtype) for w in stacks],
        in_specs=[HBM_SPEC] * n + [SEM_SPEC, SEM_SPEC] + [ANY_SPEC] * len(after), out_specs=[HBM_SPEC] * n,
        input_output_aliases={t: t for t in range(n)},
        compiler_params=pltpu.CompilerParams(has_side_effects=DATAFLOW),
    )(*stacks, sems[0], sems[1], *after)


N_PEERS = 7


def _peer(x, y, c, k):
    return x ^ (k >> 2), y ^ ((k >> 1) & 1), c ^ (k & 1)


def _reduce_copy(grad, land, sems, idx, x, y, c, k):
    px, py, pc = _peer(x, y, c, k)
    rh = grad.shape[1] // 2
    return pltpu.make_async_remote_copy(src_ref=grad.at[2 * px + py, pl.ds(pc * rh, rh), :], dst_ref=land.at[k - 1],
                                        send_sem=sems[0].at[idx], recv_sem=sems[1].at[idx], device_id=(px, py, pc),
                                        device_id_type=MESH)


def _reduce_start(name, grads):
    n = len(grads)

    def body(*refs):
        src, lands, sems, token = refs[:n], refs[n:2 * n], (refs[2 * n], refs[2 * n + 1]), refs[-1]
        x, y, c = _coords()
        for t in range(n):
            for k in range(1, N_PEERS + 1):
                _reduce_copy(src[t], lands[t], sems, N_PEERS * t + k - 1, x, y, c, k).start()
        token[...] = jnp.zeros_like(token)

    lands = [lax.empty((N_PEERS, g.shape[1] // 2, g.shape[2]), g.dtype) for g in grads]
    out_shape = [pltpu.SemaphoreType.DMA((N_PEERS * n,))] * 2
    out_shape += [pltpu.HBM(a.shape, a.dtype) for a in list(grads) + lands]
    out_shape.append(jax.ShapeDtypeStruct((8, 128), F32))
    res = pl.pallas_call(
        body, name=name, out_shape=out_shape, in_specs=[HBM_SPEC] * (2 * n),
        out_specs=[SEM_SPEC] * 2 + [HBM_SPEC] * (2 * n) + [pl.BlockSpec(memory_space=pltpu.VMEM)],
        input_output_aliases={t: 2 + t for t in range(2 * n)},
        compiler_params=pltpu.CompilerParams(has_side_effects=DATAFLOW),
    )(*[pltpu.with_memory_space_constraint(a, pltpu.HBM) for a in list(grads) + lands])
    return (res[0], res[1]), list(res[2:2 + n]), list(res[2 + n:2 + 2 * n]), res[-1]


def _reduce_wait(name, grads, lands, sems, after):
    n = len(grads)
    after = tuple(after)

    def body(*refs):
        src, dst, group_sems = refs[:n], refs[n:2 * n], (refs[2 * n], refs[2 * n + 1])
        x, y, c = _coords()
        for t in range(n):
            for k in range(1, N_PEERS + 1):
                cp = _reduce_copy(src[t], dst[t], group_sems, N_PEERS * t + k - 1, x, y, c, k)
                cp.wait_send()
                cp.wait_recv()

    res = pl.pallas_call(
        body, name=name, out_shape=[pltpu.HBM(a.shape, a.dtype) for a in list(grads) + list(lands)],
        in_specs=[HBM_SPEC] * (2 * n) + [SEM_SPEC, SEM_SPEC] + [ANY_SPEC] * len(after), out_specs=[HBM_SPEC] * (2 * n),
        input_output_aliases={t: t for t in range(2 * n)},
        compiler_params=pltpu.CompilerParams(has_side_effects=DATAFLOW),
    )(*grads, *lands, sems[0], sems[1], *after)
    return list(res[:n]), list(res[n:])


def _join_halves(name, halves, deps=()):
    n = len(halves)

    def body(*refs):
        src, dst = refs[:n], refs[n + len(deps):2 * n + len(deps)]
        send_sems, recv_sems = refs[-2:]
        x, y, c = _coords()
        cps = []
        for t in range(n):
            cp = pltpu.make_async_remote_copy(src_ref=src[t], dst_ref=dst[t], send_sem=send_sems.at[t],
                                              recv_sem=recv_sems.at[t], device_id=(x, y, 1 - c), device_id_type=MESH)
            cp.start()
            cps.append(cp)
        for cp in cps:
            cp.wait()

    anyspec = pl.BlockSpec(memory_space=pl.ANY)
    return pl.pallas_call(
        body, name=name, out_shape=[jax.ShapeDtypeStruct(h.shape, h.dtype) for h in halves],
        in_specs=[anyspec] * (n + len(deps)), out_specs=[anyspec] * n,
        scratch_shapes=[pltpu.SemaphoreType.DMA((n,)), pltpu.SemaphoreType.DMA((n,))],
    )(*halves, *deps)


def _row_block(rows):
    for rb in (512, 256, 128, 64, 32, 16):
        if rows % rb == 0:
            return rb
    raise ValueError(rows)


def _sum_devices(name, grad, land, place):
    S, R, C = grad.shape
    rh = R // 2
    rb = _row_block(rh)
    nbh = rh // rb

    def body(place_ref, g_ref, l_ref, o_ref):
        tot = g_ref[...].astype(F32)
        for k in range(N_PEERS):
            tot = tot + l_ref[k].astype(F32)
        o_ref[...] = tot

    return pl.pallas_call(
        body, name=name,
        grid_spec=pltpu.PrefetchScalarGridSpec(
            num_scalar_prefetch=1, grid=(nbh,),
            in_specs=[pl.BlockSpec((None, rb, C), lambda r, place: (place[0], place[1] * nbh + r, 0)),
                      pl.BlockSpec((N_PEERS, rb, C), lambda r, place: (0, r, 0))],
            out_specs=pl.BlockSpec((rb, C), lambda r, place: (r, 0))),
        out_shape=jax.ShapeDtypeStruct((rh, C), F32), compiler_params=_params(1),
    )(place, grad, land)


def _adamw_math(w, g, m, v):
    m2 = ADAM_B1 * m + (1.0 - ADAM_B1) * g
    v2 = ADAM_B2 * v + (1.0 - ADAM_B2) * (g * g)
    m_hat = m2 / (1.0 - ADAM_B1 ** ADAM_STEP)
    v_hat = v2 / (1.0 - ADAM_B2 ** ADAM_STEP)
    delta = -ADAM_LR * (m_hat / (jnp.sqrt(v_hat) + ADAM_EPS) + ADAM_WD * w)
    return delta, m2, v2


def _adamw(name, w, m, v, gs):
    L, R, C = w.shape
    Rh = R // 2
    rb = _row_block(Rh)
    nbh = Rh // rb
    assert len(gs) == L

    def body(core_ref, w_ref, m_ref, v_ref, *rest):
        g_refs, (go_ref, d_ref, m2_ref, v2_ref) = rest[:2 * L], rest[2 * L:]
        layer, half = pl.program_id(0), pl.program_id(1)
        mine = half == core_ref[0]
        g = jnp.where(mine, g_refs[0][...], g_refs[1][...])
        for t in range(1, L):
            g = jnp.where(layer == t, jnp.where(mine, g_refs[2 * t][...], g_refs[2 * t + 1][...]), g)
        delta, m2, v2 = _adamw_math(w_ref[...], g, m_ref[...], v_ref[...])
        go_ref[...] = g
        d_ref[...] = delta
        m2_ref[...] = m2
        v2_ref[...] = v2

    wspec = pl.BlockSpec((None, rb, C), lambda l, h, r, core: (l, h * nbh + r, 0))
    gspec = pl.BlockSpec((rb, C), lambda l, h, r, core: (r, 0))
    return pl.pallas_call(
        body, name=name,
        grid_spec=pltpu.PrefetchScalarGridSpec(num_scalar_prefetch=1, grid=(L, 2, nbh),
                                               in_specs=[wspec] * 3 + [gspec] * (2 * L), out_specs=[wspec] * 4),
        out_shape=[jax.ShapeDtypeStruct((L, R, C), F32)] * 4, compiler_params=_params(3),
    )(lax.axis_index("c").astype(jnp.int32).reshape(1), w, m, v, *[g for pair in gs for g in pair])


def _adamw_small(ws, gs, ms, vs):
    n = len(ws)

    def body(*refs):
        w_refs, g_refs, m_refs, v_refs = (refs[k * n:(k + 1) * n] for k in range(4))
        d_refs, m2_refs, v2_refs = (refs[(4 + k) * n:(5 + k) * n] for k in range(3))
        for t in range(n):
            delta, m2, v2 = _adamw_math(w_refs[t][...], g_refs[t][...], m_refs[t][...], v_refs[t][...])
            d_refs[t][...] = delta
            m2_refs[t][...] = m2
            v2_refs[t][...] = v2

    res = pl.pallas_call(body, name="adamw_small", out_shape=[jax.ShapeDtypeStruct(w.shape, F32) for w in ws] * 3)(
        *ws, *gs, *ms, *vs)
    return res[:n], res[n:2 * n], res[2 * n:]


def _packed_rows(shape):
    c = shape[-1]
    return (int(np.prod(shape)) // c) * -(-c // LANES)


def _pack(arrays):
    total = sum(_packed_rows(a.shape) for a in arrays)
    total += -total % 8
    buf, r0 = None, 0
    for a in arrays:
        a = a.astype(F32).reshape(-1, a.shape[-1])
        r, c = a.shape
        k = -(-c // LANES)
        a = jnp.pad(a, ((0, 0), (0, k * LANES - c))).reshape(r * k, LANES)
        a = jnp.pad(a, ((r0, total - r0 - r * k), (0, 0)))
        buf = a if buf is None else buf + a
        r0 += r * k
    return buf


def _unpack(buf, shapes):
    out, r0 = [], 0
    for shp in shapes:
        c = shp[-1]
        rows = _packed_rows(shp)
        out.append(buf[r0:r0 + rows].reshape(-1, -(-c // LANES) * LANES)[:, :c].reshape(shp))
        r0 += rows
    return out


def _rms(x, g):
    return x * lax.rsqrt(jnp.mean(x * x, axis=-1, keepdims=True) + NORM_EPS) * g


def _residual_norm_ep(acc, *rest):
    *bias, res, gain = rest
    x = acc + res + (bias[0] if bias else 0.0)
    return x, _rms(x, gain)


RESIDUAL_NORM_OUTS = (("tile", F32), ("tile", BF))


def _mlp_up(tag, h, w_up_sm):
    (up,) = _mm(f"mlp{tag}_up", h, w_up_sm, nt=False, b_sm=True, tm=2048, tn=1024, rows=256,
                ep_fn=lambda acc: (acc,), outs=(("tile", BF),))
    return up


RMS_BWD_OUTS = (("tile", F32), ("tile", BF), ("colsum", F32), ("colsum", F32))


def _mlp_bwd(tag, dy, dy_bf, x, g, up, w_up_sm, w_down):
    (dup,) = _mm(f"mlp{tag}_dup", dy_bf, w_down, nt=True, tm=2048, tn=1024, rows=256, ep_in=((up, "tile"),),
                 ep_fn=lambda acc, u: (acc * (2.0 * jnp.maximum(u.astype(F32), 0.0)),), outs=(("tile", BF),))
    dx, dx_bf, dg, dx_sum = _mm(f"mlp{tag}_dx", dup, w_up_sm, nt=True, b_sm=True, tm=512, tn=1024, rows=256,
                                ep_in=((x, "tile"), (g, "row"), (dy, "tile")), ep_fn=_rms_bwd_ep, outs=RMS_BWD_OUTS)
    return dx, dx_bf, dg, dx_sum, dup


class _Reduction:
    def __init__(self, tag, grads, place):
        self.tag, self.place = tag, place
        self.sems, self.grads, self.lands, self.token = _reduce_start(f"reduce_start_{tag}", grads)

    def finish(self, after):
        grads, lands = _reduce_wait(f"reduce_wait_{self.tag}", self.grads, self.lands, self.sems, after)
        return [_sum_devices(f"reduce_sum_{self.tag}{i}", g, l, self.place) for i, (g, l) in enumerate(zip(grads, lands))]


def kernel(x, conv_norm_g, conv_w_in, conv_b_in, conv_dw, conv_dw_b, conv_ln_g, conv_ln_b, conv_w_out, conv_b_out, attn_norm_g, w_qkv, b_qkv, q_norm_g, k_norm_g, sinks, w_o, b_o, rel_bias, mlp_norm_g, w_up, w_down, loss_target, m_conv_norm_g, m_conv_w_in, m_conv_b_in, m_conv_dw, m_conv_dw_b, m_conv_ln_g, m_conv_ln_b, m_conv_w_out, m_conv_b_out, m_attn_norm_g, m_w_qkv, m_b_qkv, m_q_norm_g, m_k_norm_g, m_sinks, m_w_o, m_b_o, m_rel_bias, m_mlp_norm_g, m_w_up, m_w_down, v_conv_norm_g, v_conv_w_in, v_conv_b_in, v_conv_dw, v_conv_dw_b, v_conv_ln_g, v_conv_ln_b, v_conv_w_out, v_conv_b_out, v_attn_norm_g, v_w_qkv, v_b_qkv, v_q_norm_g, v_k_norm_g, v_sinks, v_w_o, v_b_o, v_rel_bias, v_mlp_norm_g, v_w_up, v_w_down):
    Dm = D_MODEL
    x2d = x[0]
    tgt = loss_target[0]
    T = x2d.shape[0]
    shard = 2 * lax.axis_index("x") + lax.axis_index("y")

    me = 2 * shard + lax.axis_index("c")

    def own_slot(block, slots, index):
        return lax.dynamic_update_slice(lax.empty((slots,) + block.shape, block.dtype), block[None],
                                        (index,) + (0,) * block.ndim)

    sharded_small = [conv_dw[0], attn_norm_g, b_qkv, b_o]
    (small_sems,), (small_land,), small_token = _gather_start(
        "small_weights_start", [own_slot(_pack(sharded_small), 8, me)], ((0,),), ALL_OTHERS, after=())

    big = [conv_w_in[0], conv_w_out[0], w_qkv[0], w_o[0], w_up[0], w_up[1], w_down[0], w_down[1]]
    stacks = [own_slot(w.astype(BF), N_SHARD, shard) for w in big]
    groups = ((0,), (1,), (4, 6), (2, 3), (5, 7))
    gather_sems, stacks, gather_token = _gather_start("gather_start", stacks, groups, OTHER_CHIPS, after=(small_token,))

    def gathered_group(g, name, after):
        return _gather_wait(name, [stacks[t] for t in groups[g]], gather_sems[g], OTHER_CHIPS, after)

    bucket = _bucket_table()
    bias = _bias_table(rel_bias, bucket)

    h0 = _rms_fwd("conv_norm", x2d, conv_norm_g, deps=(gather_token,))
    (w_in_sm,) = gathered_group(0, "gather_wait_conv_in", (h0, bias))
    (u,) = _mm("conv_in", h0, w_in_sm, nt=False, b_sm=True, tm=2048, tn=512, rows=256, ep_in=((conv_b_in, "row"),),
               ep_fn=lambda acc, b: (acc + b,), outs=(("tile", BF),))
    (gathered,) = _gather_wait("small_weights_wait", [small_land], small_sems, ALL_OTHERS, (u,))
    chips = [_unpack(gathered[2 * s], [a.shape for a in sharded_small]) for s in range(N_SHARD)]
    dw_f, attn_norm_f, b_qkv_f, b_o_f = (jnp.concatenate([chips[s][t] for s in range(N_SHARD)], axis=-1)
                                         for t in range(len(sharded_small)))
    dw_pad = jnp.pad(dw_f, ((0, HALO - CONV_W), (0, 0)))
    cv, s_act = _conv_fwd(u, dw_pad, conv_dw_b, conv_ln_g, conv_ln_b)
    (g_out,) = gathered_group(1, "gather_wait_conv_out", (s_act,))
    w_out_f = g_out.reshape(Dm, Dm)
    x1, h1 = _mm("conv_out", s_act, w_out_f, nt=False, tm=1024, tn=1024, rows=256,
                 ep_in=((conv_b_out, "row"), (x2d, "tile"), (mlp_norm_g[0:1], "row")), ep_fn=_residual_norm_ep,
                 outs=RESIDUAL_NORM_OUTS)

    g_up0, g_down0 = gathered_group(2, "gather_wait_mlp0", (x1,))
    w_up_sm = [g_up0, None]
    w_down_f = [g_down0.reshape(D_FF, Dm), None]
    up0 = _mlp_up(0, h1, w_up_sm[0])
    x2, h2 = _mm("mlp0_down", up0, w_down_f[0], nt=False, tm=512, tn=1024, rows=256, a_fn=_relu2,
                 ep_in=((x1, "tile"), (attn_norm_f, "row")), ep_fn=_residual_norm_ep, outs=RESIDUAL_NORM_OUTS)

    g_qkv, g_o = gathered_group(3, "gather_wait_attn", (x2,))
    w_qkv_f = jnp.transpose(g_qkv, (1, 0, 2)).reshape(Dm, QKV_DIM)
    w_o_f = g_o.reshape(ATTN_DIM, Dm)
    qg_t = jnp.tile(q_norm_g, (1, N_HEADS))
    kg_t = jnp.tile(k_norm_g, (1, N_KV))

    def qkv_ep(acc, b, qg, kg, sel_q, sel_q_t, sel_k, sel_k_t):
        proj = acc + b
        q, k, v = proj[:, :ATTN_DIM], proj[:, ATTN_DIM:ATTN_DIM + KV_DIM], proj[:, ATTN_DIM + KV_DIM:]
        return proj, _qk_normed(q, qg, (sel_q, sel_q_t), 1.0 / math.sqrt(HEAD_DIM)), _qk_normed(k, kg, (sel_k, sel_k_t), 1.0), v

    qkv, qn, kn, vv = _mm(
        "attn_qkv", h2, w_qkv_f, nt=False, tm=1024, tn=QKV_DIM, rows=256, ep_fn=qkv_ep,
        ep_in=((b_qkv_f, "row"), (qg_t, "whole"), (kg_t, "whole"))
        + tuple((m, "whole") for m in _head_select(ATTN_DIM) + _head_select(KV_DIM)),
        outs=(("tile", F32), ("tile", BF, ATTN_DIM), ("tile", BF, KV_DIM), ("tile", BF, KV_DIM)))
    sinks1 = sinks[0]
    att = _attn_fwd(qn, kn, vv, bias, sinks1)
    x3, h3 = _mm("attn_out", att, w_o_f, nt=False, tm=1024, tn=1024, rows=256,
                 ep_in=((b_o_f, "row"), (x2, "tile"), (mlp_norm_g[1:2], "row")), ep_fn=_residual_norm_ep,
                 outs=RESIDUAL_NORM_OUTS)

    g_up1, g_down1 = gathered_group(4, "gather_wait_mlp1", (x3,))
    w_up_sm[1] = g_up1
    w_down_f[1] = g_down1.reshape(D_FF, Dm)
    up1 = _mlp_up(1, h3, w_up_sm[1])

    def loss_ep(acc, r, t):
        diff = acc + r - t
        dy = diff * (1.0 / Dm)
        return dy, dy, jnp.sum(diff * diff, axis=0, keepdims=True)

    dy, dy_bf, sq = _mm("mlp1_down_loss", up1, w_down_f[1], nt=False, tm=512, tn=1024, rows=256, a_fn=_relu2,
                        ep_in=((x3, "tile"), (tgt, "tile")), ep_fn=loss_ep,
                        outs=(("tile", F32), ("tile", BF), ("colsum", F32)))

    place = jnp.stack([shard, lax.axis_index("c")]).astype(jnp.int32)
    dx3, dx3_bf, dg_mlp1, db_o, dup1 = _mlp_bwd(1, dy, dy_bf, x3, mlp_norm_g[1:2], up1, w_up_sm[1], w_down_f[1])
    dw_down1 = _mm_tn("mlp1_dw_down", up1, dy_bf, tm=1024, tn=1024, tk=2048, a_fn=_relu2)
    dw_up1 = _mm_tn("mlp1_dw_up", h3, dup1, tm=1024, tn=1024, tk=2048, out_sm=N_SHARD)
    red_mlp1 = _Reduction("mlp1", [dw_up1, dw_down1.reshape(N_SHARD, D_FF // N_SHARD, Dm)], place)

    ident = lambda acc: (acc,)
    (datt,) = _mm("attn_dout", dx3_bf, w_o_f, nt=True, tm=1024, tn=1024, rows=256, ep_fn=ident, outs=(("tile", BF),),
                  deps=(red_mlp1.token,))
    dw_o = _mm_tn("attn_dw_o", att, dx3_bf, tm=1024, tn=1024, tk=2048)
    dqn, dkn, dvv, dbias, dsinks = _attn_bwd(qn, kn, vv, bias, sinks1, datt)
    drel = _bias_grad(dbias, bucket)
    dqkv, db_qkv, dqg_t, dkg_t = _qk_norm_bwd(qkv, dqn, dkn, dvv, qg_t, kg_t)
    dw_qkv = _mm_tn("attn_dw_qkv", h2, dqkv, tm=1024, tn=QKV_DIM, tk=2048)
    red_attn = _Reduction("attn", [jnp.transpose(dw_qkv.reshape(Dm, N_SHARD, QKV_DIM // N_SHARD), (1, 0, 2)),
                                   dw_o.reshape(N_SHARD, ATTN_DIM // N_SHARD, Dm)], place)
    dx2, dx2_bf, dg_attn, _ = _mm("attn_dx", dqkv, w_qkv_f, nt=True, tm=512, tn=1024, rows=256,
                                  ep_in=((x2, "tile"), (attn_norm_f, "row"), (dx3, "tile")), ep_fn=_rms_bwd_ep,
                                  outs=RMS_BWD_OUTS, deps=(red_attn.token,))

    dx1, dx1_bf, dg_mlp0, db_out, dup0 = _mlp_bwd(0, dx2, dx2_bf, x1, mlp_norm_g[0:1], up0, w_up_sm[0], w_down_f[0])
    dw_down0 = _mm_tn("mlp0_dw_down", up0, dx2_bf, tm=1024, tn=1024, tk=2048, a_fn=_relu2)
    dw_up0 = _mm_tn("mlp0_dw_up", h1, dup0, tm=1024, tn=1024, tk=2048, out_sm=N_SHARD)
    dw_out = _mm_tn("conv_dw_out", s_act, dx1_bf, tm=1024, tn=1024, tk=2048)
    red_mlp0 = _Reduction("mlp0", [dw_up0, dw_down0.reshape(N_SHARD, D_FF // N_SHARD, Dm),
                                   dw_out.reshape(N_SHARD, Dm // N_SHARD, Dm)], place)
    (r_qkv, r_o) = red_attn.finish((dx1,))
    (r_up1, r_down1) = red_mlp1.finish((dx1,))

    dcv, dln_g, dln_b, ddw_b = _mm("conv_ds", dx1_bf, w_out_f, nt=True, tm=512, tn=1024, rows=256,
                                   ep_in=((cv, "tile"), (conv_ln_g, "row"), (conv_ln_b, "row")),
                                   ep_fn=_ln_silu_bwd_ep,
                                   outs=(("tile", F32), ("colsum", F32), ("colsum", F32), ("colsum", F32)),
                                   deps=(red_mlp0.token,))
    du, db_in, ddw8 = _conv_bwd(u, dcv, dw_pad)
    (r_up0, r_down0, r_out) = red_mlp0.finish((du,))
    dw_in = _mm_tn("conv_dw_in", h0, du, tm=1024, tn=512, tk=4096, out_sm=N_SHARD)
    red_conv = _Reduction("conv", [dw_in], place)
    def first_layer_ep(*args):
        tot, _, dg, _ = _rms_bwd_ep(*args)
        return tot, dg

    gx, dg_conv = _mm("conv_dx", du, w_in_sm, nt=True, b_sm=True, tm=512, tn=1024, rows=256,
                      ep_in=((x2d, "tile"), (conv_norm_g, "row"), (dx1, "tile")), ep_fn=first_layer_ep,
                      outs=(("tile", F32), ("colsum", F32)), deps=(red_conv.token,))
    (r_in,) = red_conv.finish((gx,))

    dqg = dqg_t.reshape(N_HEADS, HEAD_DIM).sum(axis=0, keepdims=True)
    dkg = dkg_t.reshape(N_KV, HEAD_DIM).sum(axis=0, keepdims=True)
    small_full = [dg_conv, db_in, ddw8.sum(axis=1)[:CONV_W], ddw_b, dln_g, dln_b, db_out, dg_attn, db_qkv, dqg, dkg,
                  dsinks[None, :], db_o, drel.reshape(1, REL_BUCKETS * N_HEADS),
                  jnp.pad(dg_mlp0, ((0, 1), (0, 0))) + jnp.pad(dg_mlp1, ((1, 0), (0, 0))), sq]
    (sg_sems,), (sg_land,), sg_token = _gather_start(
        "small_grads_start", [own_slot(_pack(small_full), 8, me)], ((0,),), ALL_OTHERS, after=())

    mine = [r_in, r_out, r_qkv, r_o, r_up0, r_up1, r_down0, r_down1]
    r_in, r_out, r_qkv, r_o, r_up0, r_up1, r_down0, r_down1 = zip(
        mine, _join_halves("join_halves", mine, deps=(sg_token,)))

    big_out = {}
    for nm, w, m, v, gs in (("conv_w_in", conv_w_in, m_conv_w_in, v_conv_w_in, (r_in,)),
                            ("conv_w_out", conv_w_out, m_conv_w_out, v_conv_w_out, (r_out,)),
                            ("w_qkv", w_qkv, m_w_qkv, v_w_qkv, (r_qkv,)),
                            ("w_o", w_o, m_w_o, v_w_o, (r_o,)),
                            ("w_up", w_up, m_w_up, v_w_up, (r_up0, r_up1)),
                            ("w_down", w_down, m_w_down, v_w_down, (r_down0, r_down1))):
        big_out[nm] = _adamw(f"adamw_{nm}", w, m, v, gs)

    (sg_land,) = _gather_wait("small_grads_wait", [sg_land], sg_sems, ALL_OTHERS,
                              [big_out[nm][0] for nm in big_out])
    small_sum = _sum8("small_grads_sum", sg_land)
    (r_norm, r_b_in, r_dw, r_dw_b, r_ln_g, r_ln_b, r_b_out, r_attn_norm, r_b_qkv, r_qg, r_kg, r_sinks, r_b_o, r_rel,
     r_mlp_norm, r_sq) = _unpack(small_sum, [a.shape for a in small_full])
    loss = 0.5 * jnp.sum(r_sq) * (1.0 / Dm)

    def cols(a, width):
        return lax.dynamic_slice_in_dim(a, shard * width, width, axis=a.ndim - 1)

    small_names = ["conv_norm_g", "conv_b_in", "conv_dw", "conv_dw_b", "conv_ln_g", "conv_ln_b", "conv_b_out",
                   "attn_norm_g", "b_qkv", "q_norm_g", "k_norm_g", "sinks", "b_o", "rel_bias", "mlp_norm_g"]
    small_g = [r_norm, r_b_in, cols(r_dw, Dm // N_SHARD)[None], r_dw_b, r_ln_g, r_ln_b, r_b_out,
               cols(r_attn_norm, Dm // N_SHARD), cols(r_b_qkv, QKV_DIM // N_SHARD), r_qg, r_kg, r_sinks,
               cols(r_b_o, Dm // N_SHARD), r_rel.reshape(REL_BUCKETS, N_HEADS), r_mlp_norm]
    small_w = [conv_norm_g, conv_b_in, conv_dw, conv_dw_b, conv_ln_g, conv_ln_b, conv_b_out, attn_norm_g, b_qkv,
               q_norm_g, k_norm_g, sinks, b_o, rel_bias, mlp_norm_g]
    small_m = [m_conv_norm_g, m_conv_b_in, m_conv_dw, m_conv_dw_b, m_conv_ln_g, m_conv_ln_b, m_conv_b_out,
               m_attn_norm_g, m_b_qkv, m_q_norm_g, m_k_norm_g, m_sinks, m_b_o, m_rel_bias, m_mlp_norm_g]
    small_v = [v_conv_norm_g, v_conv_b_in, v_conv_dw, v_conv_dw_b, v_conv_ln_g, v_conv_ln_b, v_conv_b_out,
               v_attn_norm_g, v_b_qkv, v_q_norm_g, v_k_norm_g, v_sinks, v_b_o, v_rel_bias, v_mlp_norm_g]
    flat2 = lambda a: a.reshape(-1, a.shape[-1])
    small_g = [flat2(g) for g in small_g]
    d_s, m_s, v_s = _adamw_small([flat2(w) for w in small_w], small_g, [flat2(m) for m in small_m],
                                 [flat2(v) for v in small_v])
    small_out = {}
    for nm, w, g, d, m2, v2 in zip(small_names, small_w, small_g, d_s, m_s, v_s):
        small_out[nm] = tuple(a.reshape(w.shape) for a in (g, d, m2, v2))

    order = ["conv_norm_g", "conv_w_in", "conv_b_in", "conv_dw", "conv_dw_b", "conv_ln_g", "conv_ln_b", "conv_w_out",
             "conv_b_out", "attn_norm_g", "w_qkv", "b_qkv", "q_norm_g", "k_norm_g", "sinks", "w_o", "b_o", "rel_bias",
             "mlp_norm_g", "w_up", "w_down"]
    res = {**small_out, **big_out}
    outs = [loss, gx[None]]
    for part in range(4):
        outs += [res[nm][part] for nm in order]
    return tuple(outs)
```

```python
import math

import numpy as np
import jax
import jax.numpy as jnp
from jax import lax
from jax.experimental import pallas as pl
from jax.experimental.pallas import tpu as pltpu

F32 = jnp.float32
BF = jnp.bfloat16
MESH = pl.DeviceIdType.MESH

D_MODEL = 1024
D_FF = 4096
N_HEADS = 16
N_KV = 2
GROUP = N_HEADS // N_KV
HEAD_DIM = 64
ATTN_DIM = N_HEADS * HEAD_DIM
KV_DIM = N_KV * HEAD_DIM
QKV_DIM = ATTN_DIM + 2 * KV_DIM
BLOCK = 128
CONV_W = 31
HALO = 32
REL_BUCKETS = 32
REL_MAX_DIST = 128
NORM_EPS = 1e-6
NEG_INF = -1e30
N_SHARD = 4
LANES = 1024

ADAM_LR = 0.001
ADAM_B1 = 0.9
ADAM_B2 = 0.999
ADAM_EPS = 1e-08
ADAM_WD = 0.01
ADAM_STEP = 10

VMEM_LIMIT = 56 * 1024 * 1024


def _params(n_axes):
    return pltpu.CompilerParams(dimension_semantics=("arbitrary",) * n_axes, vmem_limit_bytes=VMEM_LIMIT)


def _dot(a, b, ca, cb):
    return lax.dot_general(a, b, (((ca,), (cb,)), ((), ())), preferred_element_type=F32)


def _mm(name, a, b, *, nt, tm, tn, ep_fn, outs, a_fn=None, b_sm=False, ep_in=(), deps=(), rows=None):
    M, K = a.shape
    rows = tm if rows is None else rows
    if b_sm:
        S, ks = b.shape[0], b.shape[2]
        N, per = (b.shape[1], None) if nt else (S * b.shape[2], b.shape[2] // tn)
        assert (S * ks == K) if nt else (b.shape[1] == K)
    else:
        N = b.shape[0] if nt else b.shape[1]
        assert (b.shape[1] if nt else b.shape[0]) == K
    assert M % tm == 0 and N % tn == 0 and tm % rows == 0
    ne, no, nd = len(ep_in), len(outs), len(deps)

    def body(a_ref, b_ref, *rest):
        ep_refs, out_refs = rest[:ne], rest[ne + nd:ne + nd + no]
        i = pl.program_id(1)
        sums = [None] * no
        for r in range(tm // rows):
            rs = pl.ds(r * rows, rows)

            def lhs(cols):
                av = a_ref[rs, cols]
                return (av if a_fn is None else a_fn(av)).astype(BF)

            if b_sm and nt:
                acc = None
                for s in range(S):
                    part = _dot(lhs(pl.ds(s * ks, ks)), b_ref[s].astype(BF), 1, 1)
                    acc = part if acc is None else acc + part
            else:
                acc = _dot(lhs(slice(None)), b_ref[...].astype(BF), 1, 1 if nt else 0)
            ep_vals = [ref[rs, :] if kind == "tile" else ref[...] for ref, (_, kind) in zip(ep_refs, ep_in)]
            vals = ep_fn(acc, *ep_vals)
            for o, ((kind, dt, *_), ref, val) in enumerate(zip(outs, out_refs, vals)):
                if kind == "tile":
                    ref[rs, :] = val.astype(dt)
                else:
                    sums[o] = val if sums[o] is None else sums[o] + val
        for (kind, *_), ref, val in zip(outs, out_refs, sums):
            if kind == "colsum":
                @pl.when(i == 0)
                def _():
                    ref[...] = val

                @pl.when(i > 0)
                def _():
                    ref[...] += val

    if b_sm and nt:
        b_spec = pl.BlockSpec((S, tn, ks), lambda j, i: (0, j, 0))
    elif b_sm:
        b_spec = pl.BlockSpec((None, K, tn), lambda j, i: (j // per, 0, j % per))
    elif nt:
        b_spec = pl.BlockSpec((tn, K), lambda j, i: (j, 0))
    else:
        b_spec = pl.BlockSpec((K, tn), lambda j, i: (0, j))
    in_specs = [pl.BlockSpec((tm, K), lambda j, i: (i, 0)), b_spec]
    for arr, kind in ep_in:
        if kind == "tile":
            assert arr.shape == (M, N)
            in_specs.append(pl.BlockSpec((tm, tn), lambda j, i: (i, j)))
        elif kind == "whole":
            in_specs.append(pl.BlockSpec(arr.shape, lambda j, i, rank=arr.ndim: (0,) * rank))
        else:
            assert arr.shape == (1, N)
            in_specs.append(pl.BlockSpec((1, tn), lambda j, i: (0, j)))
    in_specs += [pl.BlockSpec(memory_space=pl.ANY)] * nd
    out_shape, out_specs = [], []
    for kind, dt, *width in outs:
        if kind == "tile" and width:
            assert tn == N
            out_shape.append(jax.ShapeDtypeStruct((M, width[0]), dt))
            out_specs.append(pl.BlockSpec((tm, width[0]), lambda j, i: (i, 0)))
        elif kind == "tile":
            out_shape.append(jax.ShapeDtypeStruct((M, N), dt))
            out_specs.append(pl.BlockSpec((tm, tn), lambda j, i: (i, j)))
        else:
            out_shape.append(jax.ShapeDtypeStruct((1, N), F32))
            out_specs.append(pl.BlockSpec((1, tn), lambda j, i: (0, j)))
    return pl.pallas_call(
        body, name=name, grid=(N // tn, M // tm), in_specs=in_specs, out_specs=out_specs, out_shape=out_shape,
        compiler_params=_params(2),
    )(a, b, *[arr for arr, _ in ep_in], *deps)


def _mm_tn(name, a, b, *, tm, tn, tk, a_fn=None, out_sm=None):
    T, Ka = a.shape
    N = b.shape[1]
    assert b.shape[0] == T and T % tk == 0 and Ka % tm == 0 and N % tn == 0
    nk = T // tk

    def body(a_ref, b_ref, o_ref, acc_ref):
        k = pl.program_id(2)

        @pl.when(k == 0)
        def _():
            acc_ref[...] = jnp.zeros_like(acc_ref)

        av = a_ref[...]
        if a_fn is not None:
            av = a_fn(av)
        acc_ref[...] += _dot(av.astype(BF), b_ref[...].astype(BF), 0, 0)

        @pl.when(k == nk - 1)
        def _():
            o_ref[...] = acc_ref[...].astype(BF)

    if out_sm is None:
        out_shape = jax.ShapeDtypeStruct((Ka, N), BF)
        out_spec = pl.BlockSpec((tm, tn), lambda i, j, k: (i, j))
    else:
        per = (N // out_sm) // tn
        assert per * tn * out_sm == N
        out_shape = jax.ShapeDtypeStruct((out_sm, Ka, N // out_sm), BF)
        out_spec = pl.BlockSpec((None, tm, tn), lambda i, j, k: (j // per, i, j % per))
    return pl.pallas_call(
        body, name=name, grid=(Ka // tm, N // tn, nk),
        in_specs=[pl.BlockSpec((tk, tm), lambda i, j, k: (k, i)), pl.BlockSpec((tk, tn), lambda i, j, k: (k, j))],
        out_specs=out_spec, out_shape=out_shape, scratch_shapes=[pltpu.VMEM((tm, tn), F32)],
        compiler_params=_params(3),
    )(a, b)


def _relu2(v):
    r = jnp.maximum(v.astype(F32), 0.0)
    return r * r


def _rms_bwd_ep(dh, x, g, dres):
    rstd = lax.rsqrt(jnp.mean(x * x, axis=-1, keepdims=True) + NORM_EPS)
    xh = x * rstd
    dxh = dh * g
    dx = rstd * (dxh - xh * jnp.mean(dxh * xh, axis=-1, keepdims=True))
    tot = dres + dx
    return tot, tot, jnp.sum(dh * xh, axis=0, keepdims=True), jnp.sum(tot, axis=0, keepdims=True)


def _rms_fwd(name, x, g, tm=512, deps=()):
    T, Dm = x.shape

    def body(x_ref, g_ref, *rest):
        o_ref = rest[-1]
        xv = x_ref[...]
        rstd = lax.rsqrt(jnp.mean(xv * xv, axis=-1, keepdims=True) + NORM_EPS)
        o_ref[...] = (xv * rstd * g_ref[...]).astype(BF)

    return pl.pallas_call(
        body, name=name, grid=(T // tm,),
        in_specs=[pl.BlockSpec((tm, Dm), lambda i: (i, 0)), pl.BlockSpec((1, Dm), lambda i: (0, 0))]
        + [pl.BlockSpec(memory_space=pl.ANY)] * len(deps),
        out_specs=pl.BlockSpec((tm, Dm), lambda i: (i, 0)), out_shape=jax.ShapeDtypeStruct((T, Dm), BF),
        compiler_params=_params(1),
    )(x, g, *deps)


HEAD_COLS = 128


def _two_term_dot(v, m):
    hi = v.astype(BF)
    lo = (v - hi.astype(F32)).astype(BF)
    return _dot(hi, m, 1, 0) + _dot(lo, m, 1, 0)


def _head_sum(v, select):
    sel, sel_t = select
    return _two_term_dot(_two_term_dot(v, sel), sel_t)


def _head_select(n):
    sel = (np.arange(n)[:, None] // HEAD_DIM == np.arange(HEAD_COLS)[None, :]).astype(np.float32)
    return jnp.asarray(sel, dtype=BF), jnp.asarray(sel.T, dtype=BF)


def _qk_normed(x, g, select, scale):
    r = lax.rsqrt(_head_sum(x * x, select) * (1.0 / HEAD_DIM) + NORM_EPS)
    return x * r * g * scale


def _qk_norm_bwd(qkv, dqn, dkn, dv, qg_t, kg_t, tm=256):
    T = qkv.shape[0]

    def body(x_ref, dq_ref, dk_ref, dv_ref, qg_ref, kg_ref, sq_ref, sqt_ref, sk_ref, skt_ref,
             o_ref, db_ref, dqg_ref, dkg_ref):
        i = pl.program_id(0)

        def one(x, dy, g, select):
            r = lax.rsqrt(_head_sum(x * x, select) * (1.0 / HEAD_DIM) + NORM_EPS)
            xh = x * r
            dxh = dy * g
            dx = r * (dxh - xh * (_head_sum(dxh * xh, select) * (1.0 / HEAD_DIM)))
            return dx, jnp.sum(dy * xh, axis=0, keepdims=True)

        dq, dqg = one(x_ref[:, pl.ds(0, ATTN_DIM)], dq_ref[...], qg_ref[...], (sq_ref[...], sqt_ref[...]))
        dk, dkg = one(x_ref[:, pl.ds(ATTN_DIM, KV_DIM)], dk_ref[...], kg_ref[...], (sk_ref[...], skt_ref[...]))
        dvv = dv_ref[...]
        o_ref[:, pl.ds(0, ATTN_DIM)] = dq.astype(BF)
        o_ref[:, pl.ds(ATTN_DIM, KV_DIM)] = dk.astype(BF)
        o_ref[:, pl.ds(ATTN_DIM + KV_DIM, KV_DIM)] = dvv.astype(BF)
        sq, sk, sv = (jnp.sum(t, axis=0, keepdims=True) for t in (dq, dk, dvv))

        @pl.when(i == 0)
        def _():
            db_ref[:, pl.ds(0, ATTN_DIM)] = sq
            db_ref[:, pl.ds(ATTN_DIM, KV_DIM)] = sk
            db_ref[:, pl.ds(ATTN_DIM + KV_DIM, KV_DIM)] = sv
            dqg_ref[...] = dqg
            dkg_ref[...] = dkg

        @pl.when(i > 0)
        def _():
            db_ref[:, pl.ds(0, ATTN_DIM)] += sq
            db_ref[:, pl.ds(ATTN_DIM, KV_DIM)] += sk
            db_ref[:, pl.ds(ATTN_DIM + KV_DIM, KV_DIM)] += sv
            dqg_ref[...] += dqg
            dkg_ref[...] += dkg

    full = lambda shape: pl.BlockSpec(shape, lambda i: (0, 0))
    row = lambda n: pl.BlockSpec((tm, n), lambda i: (i, 0))
    return pl.pallas_call(
        body, name="qk_norm_bwd", grid=(T // tm,),
        in_specs=[row(QKV_DIM), row(ATTN_DIM), row(KV_DIM), row(KV_DIM), full((1, ATTN_DIM)), full((1, KV_DIM)),
                  full((ATTN_DIM, HEAD_COLS)), full((HEAD_COLS, ATTN_DIM)), full((KV_DIM, HEAD_COLS)), full((HEAD_COLS, KV_DIM))],
        out_specs=[row(QKV_DIM), full((1, QKV_DIM)), full((1, ATTN_DIM)), full((1, KV_DIM))],
        out_shape=[jax.ShapeDtypeStruct((T, QKV_DIM), BF), jax.ShapeDtypeStruct((1, QKV_DIM), F32),
                   jax.ShapeDtypeStruct((1, ATTN_DIM), F32), jax.ShapeDtypeStruct((1, KV_DIM), F32)],
        compiler_params=_params(1),
    )(qkv, dqn, dkn, dv, qg_t, kg_t, *_head_select(ATTN_DIM), *_head_select(KV_DIM))


ROWS = 64
COLS = 128


SUBLANES = 8
FIRST_TAP = HALO - (CONV_W - 1)


def _glu(a, g):
    return a.astype(F32) * jax.nn.sigmoid(g.astype(F32))


def _shifted(xe, s):
    return xe if s == 0 else pltpu.roll(xe, ROWS + HALO - s, axis=0)


def _conv_fwd(u, dw_pad, dw_b, ln_g, ln_b, tm=256):
    T = u.shape[0]
    Dm = D_MODEL
    hpt = tm // HALO

    def body(ac_ref, gc_ref, ap_ref, gp_ref, w_ref, wb_ref, lg_ref, lb_ref, cv_ref, s_ref, ext):
        i = pl.program_id(0)
        ext[pl.ds(0, HALO), :] = jnp.where(i > 0, _glu(ap_ref[...], gp_ref[...]), 0.0)
        ext[pl.ds(HALO, tm), :] = _glu(ac_ref[...], gc_ref[...])

        def rows(r, carry):
            r0 = pl.multiple_of(r * ROWS, ROWS)
            for c in range(Dm // COLS):
                cs = pl.ds(c * COLS, COLS)
                xe = ext[pl.ds(r0, ROWS + HALO), cs]
                acc = jnp.zeros((ROWS, COLS), F32)
                for s in range(SUBLANES):
                    xs = _shifted(xe, s)
                    for j in range(CONV_W):
                        off = FIRST_TAP + j
                        if off % SUBLANES == s:
                            acc = acc + xs[off - s:off - s + ROWS, :] * w_ref[pl.ds(j, 1), cs]
                cv_ref[pl.ds(r0, ROWS), cs] = acc + wb_ref[:, cs]
            return carry

        lax.fori_loop(0, tm // ROWS, rows, 0)
        cv = cv_ref[...]
        xc = cv - jnp.mean(cv, axis=-1, keepdims=True)
        y = xc * lax.rsqrt(jnp.mean(xc * xc, axis=-1, keepdims=True) + NORM_EPS) * lg_ref[...] + lb_ref[...]
        s_ref[...] = (y * jax.nn.sigmoid(y)).astype(BF)

    full = lambda shape: pl.BlockSpec(shape, lambda i: (0, 0))
    return pl.pallas_call(
        body, name="conv_fwd", grid=(T // tm,),
        in_specs=[pl.BlockSpec((tm, Dm), lambda i: (i, 0)), pl.BlockSpec((tm, Dm), lambda i: (i, 1)),
                  pl.BlockSpec((HALO, Dm), lambda i: (jnp.maximum(i * hpt - 1, 0), 0)),
                  pl.BlockSpec((HALO, Dm), lambda i: (jnp.maximum(i * hpt - 1, 0), 1)),
                  full((HALO, Dm)), full((1, Dm)), full((1, Dm)), full((1, Dm))],
        out_specs=[pl.BlockSpec((tm, Dm), lambda i: (i, 0)), pl.BlockSpec((tm, Dm), lambda i: (i, 0))],
        out_shape=[jax.ShapeDtypeStruct((T, Dm), F32), jax.ShapeDtypeStruct((T, Dm), BF)],
        scratch_shapes=[pltpu.VMEM((tm + HALO, Dm), F32)],
        compiler_params=_params(1),
    )(u, u, u, u, dw_pad, dw_b, ln_g, ln_b)


def _ln_silu_bwd_ep(ds, cv, lg, lb):
    xc = cv - jnp.mean(cv, axis=-1, keepdims=True)
    rstd = lax.rsqrt(jnp.mean(xc * xc, axis=-1, keepdims=True) + NORM_EPS)
    xh = xc * rstd
    y = xh * lg + lb
    sg = jax.nn.sigmoid(y)
    dy = ds * (sg * (1.0 + y * (1.0 - sg)))
    dxh = dy * lg
    dcv = rstd * (dxh - jnp.mean(dxh, axis=-1, keepdims=True) - xh * jnp.mean(dxh * xh, axis=-1, keepdims=True))
    return (dcv, jnp.sum(dy * xh, axis=0, keepdims=True), jnp.sum(dy, axis=0, keepdims=True),
            jnp.sum(dcv, axis=0, keepdims=True))


def _conv_bwd(u, dcv, dw_pad, tm=256):
    T = u.shape[0]
    Dm = D_MODEL
    hpt = tm // HALO
    last = T // HALO - 1
    nt = T // tm

    def body(ac_ref, gc_ref, ap_ref, gp_ref, dc_ref, dn_ref, w_ref, du_ref, db_ref, dw_ref, ext_g, ext_d):
        i = pl.program_id(0)
        ext_g[pl.ds(0, HALO), :] = jnp.where(i > 0, _glu(ap_ref[...], gp_ref[...]), 0.0)
        ext_g[pl.ds(HALO, tm), :] = _glu(ac_ref[...], gc_ref[...])
        ext_d[pl.ds(0, tm), :] = dc_ref[...]
        ext_d[pl.ds(tm, HALO), :] = jnp.where(i < nt - 1, dn_ref[...], 0.0)

        @pl.when(i == 0)
        def _():
            db_ref[...] = jnp.zeros_like(db_ref)
            dw_ref[...] = jnp.zeros_like(dw_ref)

        def rows(r, carry):
            r0 = pl.multiple_of(r * ROWS, ROWS)
            rs = pl.ds(r0, ROWS)
            for c in range(Dm // COLS):
                cs = pl.ds(c * COLS, COLS)
                cs2 = pl.ds(Dm + c * COLS, COLS)
                de = ext_d[pl.ds(r0, ROWS + HALO), cs]
                ge = ext_g[pl.ds(r0, ROWS + HALO), cs]
                dcur = de[0:ROWS, :]
                acc = jnp.zeros((ROWS, COLS), F32)
                for s in range(SUBLANES):
                    ds_, gs_ = _shifted(de, s), _shifted(ge, s)
                    for j in range(CONV_W):
                        off = CONV_W - 1 - j
                        if off % SUBLANES == s:
                            acc = acc + ds_[off - s:off - s + ROWS, :] * w_ref[pl.ds(j, 1), cs]
                        goff = FIRST_TAP + j
                        if goff % SUBLANES == s:
                            prod = dcur * gs_[goff - s:goff - s + ROWS, :]
                            dw_ref[j, :, cs] += jnp.sum(prod.reshape(ROWS // SUBLANES, SUBLANES, COLS), axis=0)
                a = ac_ref[rs, cs].astype(F32)
                sg = jax.nn.sigmoid(gc_ref[rs, cs].astype(F32))
                da = acc * sg
                dg = acc * a * sg * (1.0 - sg)
                du_ref[rs, cs] = da.astype(BF)
                du_ref[rs, cs2] = dg.astype(BF)
                db_ref[:, cs] += jnp.sum(da, axis=0, keepdims=True)
                db_ref[:, cs2] += jnp.sum(dg, axis=0, keepdims=True)
            return carry

        lax.fori_loop(0, tm // ROWS, rows, 0)

    return pl.pallas_call(
        body, name="conv_bwd", grid=(nt,),
        in_specs=[pl.BlockSpec((tm, Dm), lambda i: (i, 0)), pl.BlockSpec((tm, Dm), lambda i: (i, 1)),
                  pl.BlockSpec((HALO, Dm), lambda i: (jnp.maximum(i * hpt - 1, 0), 0)),
                  pl.BlockSpec((HALO, Dm), lambda i: (jnp.maximum(i * hpt - 1, 0), 1)),
                  pl.BlockSpec((tm, Dm), lambda i: (i, 0)),
                  pl.BlockSpec((HALO, Dm), lambda i: (jnp.minimum((i + 1) * hpt, last), 0)),
                  pl.BlockSpec((HALO, Dm), lambda i: (0, 0))],
        out_specs=[pl.BlockSpec((tm, 2 * Dm), lambda i: (i, 0)), pl.BlockSpec((1, 2 * Dm), lambda i: (0, 0)),
                   pl.BlockSpec((HALO, 8, Dm), lambda i: (0, 0, 0))],
        out_shape=[jax.ShapeDtypeStruct((T, 2 * Dm), BF), jax.ShapeDtypeStruct((1, 2 * Dm), F32),
                   jax.ShapeDtypeStruct((HALO, 8, Dm), F32)],
        scratch_shapes=[pltpu.VMEM((tm + HALO, Dm), F32), pltpu.VMEM((tm + HALO, Dm), F32)],
        compiler_params=_params(1),
    )(u, u, u, u, dcv, dcv, dw_pad)


def _bucket_table():
    q_loc = np.arange(BLOCK)[:, None]
    k_loc = np.arange(2 * BLOCK)[None, :]
    dist = q_loc + BLOCK - k_loc
    n = np.maximum(dist, 0)
    max_exact = REL_BUCKETS // 2
    large = max_exact + (np.log(np.maximum(n, 1).astype(np.float32) / max_exact)
                         / math.log(REL_MAX_DIST / max_exact) * (REL_BUCKETS - max_exact)).astype(np.int32)
    large = np.minimum(large, REL_BUCKETS - 1)
    bucket = np.where(n < max_exact, n, large).astype(np.int32)
    return jnp.asarray(np.where((dist >= 0) & (dist < BLOCK), bucket, -1).astype(np.int32))


def _bias_table(rel_bias, bucket):
    def body(rb_ref, bk_ref, o_ref):
        bk = bk_ref[...]
        for h in range(N_HEADS):
            acc = jnp.full((BLOCK, 2 * BLOCK), NEG_INF, F32)
            for b in range(REL_BUCKETS):
                acc = jnp.where(bk == b, rb_ref[b, h], acc)
            o_ref[h] = acc

    return pl.pallas_call(
        body, name="bias_table", out_shape=jax.ShapeDtypeStruct((N_HEADS, BLOCK, 2 * BLOCK), F32),
        in_specs=[pl.BlockSpec(memory_space=pltpu.SMEM), pl.BlockSpec(memory_space=pltpu.VMEM)],
        out_specs=pl.BlockSpec(memory_space=pltpu.VMEM),
    )(rel_bias, bucket)


def _bias_grad(dbias, bucket):
    def body(db_ref, bk_ref, o_ref):
        bk = bk_ref[...]
        for b in range(REL_BUCKETS):
            sel = bk == b
            for h in range(N_HEADS):
                o_ref[b, h] = jnp.sum(jnp.where(sel, db_ref[h], 0.0))

    return pl.pallas_call(
        body, name="bias_grad", out_shape=jax.ShapeDtypeStruct((REL_BUCKETS, N_HEADS), F32),
        in_specs=[pl.BlockSpec(memory_space=pltpu.VMEM), pl.BlockSpec(memory_space=pltpu.VMEM)],
        out_specs=pl.BlockSpec(memory_space=pltpu.SMEM),
    )(dbias, bucket)


GROUP_ROWS = GROUP * BLOCK


def _head_probs(qk, bias_h, sink, first):
    s = jnp.where(first, NEG_INF, qk + bias_h)
    m = jnp.maximum(jnp.max(s, axis=-1, keepdims=True), sink)
    p = jnp.exp(s - m)
    ps = jnp.exp(sink - m)
    inv = 1.0 / (jnp.sum(p, axis=-1, keepdims=True) + ps)
    return p * inv, ps * inv


def _band(prev_ref, cur_ref, g):
    hs = pl.ds(g * HEAD_DIM, HEAD_DIM)
    return jnp.concatenate([prev_ref[:, hs], cur_ref[:, hs]], axis=0)


def _stack_heads(ref, g):
    return jnp.concatenate([ref[:, pl.ds((g * GROUP + hh) * HEAD_DIM, HEAD_DIM)] for hh in range(GROUP)], axis=0)


def _unstack_heads(ref, g, stacked, dtype):
    for hh in range(GROUP):
        ref[:, pl.ds((g * GROUP + hh) * HEAD_DIM, HEAD_DIM)] = stacked[hh * BLOCK:(hh + 1) * BLOCK, :].astype(dtype)


def _first_mask(n):
    col = lax.broadcasted_iota(jnp.int32, (1, 2 * BLOCK), 1)
    return jnp.logical_and(n == 0, col < BLOCK)


def _head_rows(hh):
    return pl.ds(hh * BLOCK, BLOCK)


def _attn_fwd(qn, kn, vv, bias, sinks):
    T = qn.shape[0]
    nb = T // BLOCK

    def body(sk_ref, q_ref, kc_ref, kp_ref, vc_ref, vp_ref, b_ref, o_ref, qk_buf, p_buf):
        first = _first_mask(pl.program_id(0))
        for g in range(N_KV):
            k = _band(kp_ref, kc_ref, g)
            v = _band(vp_ref, vc_ref, g)
            qk_buf[g] = _dot(_stack_heads(q_ref, g), k, 1, 1)
            for hh in range(GROUP):
                h = g * GROUP + hh
                pn, _ = _head_probs(qk_buf[g, _head_rows(hh), :], b_ref[h], sk_ref[h], first)
                p_buf[g, _head_rows(hh), :] = pn.astype(BF)
            _unstack_heads(o_ref, g, _dot(p_buf[g], v, 1, 0), BF)

    cur = lambda n: (n, 0)
    prev = lambda n: (jnp.maximum(n - 1, 0), 0)
    return pl.pallas_call(
        body, name="attn_fwd", grid=(nb,),
        in_specs=[pl.BlockSpec(memory_space=pltpu.SMEM), pl.BlockSpec((BLOCK, ATTN_DIM), cur),
                  pl.BlockSpec((BLOCK, KV_DIM), cur), pl.BlockSpec((BLOCK, KV_DIM), prev),
                  pl.BlockSpec((BLOCK, KV_DIM), cur), pl.BlockSpec((BLOCK, KV_DIM), prev),
                  pl.BlockSpec((N_HEADS, BLOCK, 2 * BLOCK), lambda n: (0, 0, 0))],
        out_specs=pl.BlockSpec((BLOCK, ATTN_DIM), cur), out_shape=jax.ShapeDtypeStruct((T, ATTN_DIM), BF),
        scratch_shapes=[pltpu.VMEM((N_KV, GROUP_ROWS, 2 * BLOCK), F32), pltpu.VMEM((N_KV, GROUP_ROWS, 2 * BLOCK), BF)],
        compiler_params=_params(1),
    )(sinks, qn, kn, kn, vv, vv, bias)


def _attn_bwd(qn, kn, vv, bias, sinks, do):
    T = qn.shape[0]
    nb = T // BLOCK
    scale = 1.0 / math.sqrt(HEAD_DIM)

    def body(sk_ref, q_ref, kc_ref, kp_ref, vc_ref, vp_ref, b_ref, do_ref,
             dq_ref, dk_ref, dv_ref, db_ref, dsk_ref, dk_full, dv_full, dk_carry, dv_carry, qk_buf, dp_buf, p_buf, ds_buf):
        n = pl.program_id(0)

        @pl.when(n == 0)
        def _():
            db_ref[...] = jnp.zeros_like(db_ref)
            dk_carry[...] = jnp.zeros_like(dk_carry)
            dv_carry[...] = jnp.zeros_like(dv_carry)
            for h in range(N_HEADS):
                dsk_ref[h] = 0.0

        @pl.when(n < nb)
        def _():
            first = _first_mask(n)
            for g in range(N_KV):
                k = _band(kp_ref, kc_ref, g)
                v = _band(vp_ref, vc_ref, g)
                q = _stack_heads(q_ref, g)
                dout = _stack_heads(do_ref, g)
                qk_buf[g] = _dot(q, k, 1, 1)
                dp_buf[g] = _dot(dout, v, 1, 1)
                for hh in range(GROUP):
                    h = g * GROUP + hh
                    rows = _head_rows(hh)
                    pn, psink = _head_probs(qk_buf[g, rows, :], b_ref[h], sk_ref[h], first)
                    dp = dp_buf[g, rows, :]
                    delta = jnp.sum(pn * dp, axis=-1, keepdims=True)
                    ds = pn * (dp - delta)
                    dsk_ref[h] += -jnp.sum(psink * delta)
                    db_ref[h] += ds
                    ds_buf[g, rows, :] = ds.astype(BF)
                    p_buf[g, rows, :] = pn.astype(BF)
                dsb = ds_buf[g]
                _unstack_heads(dq_ref, g, _dot(dsb, k, 1, 0) * scale, F32)
                gs = pl.ds(g * HEAD_DIM, HEAD_DIM)
                dk_full[:, gs] = _dot(dsb, q, 0, 0)
                dv_full[:, gs] = _dot(p_buf[g], dout, 0, 0)

        @pl.when(n == nb)
        def _():
            dk_full[...] = jnp.zeros_like(dk_full)
            dv_full[...] = jnp.zeros_like(dv_full)

        dk_ref[...] = dk_carry[...] + dk_full[pl.ds(0, BLOCK), :]
        dv_ref[...] = dv_carry[...] + dv_full[pl.ds(0, BLOCK), :]
        dk_carry[...] = dk_full[pl.ds(BLOCK, BLOCK), :]
        dv_carry[...] = dv_full[pl.ds(BLOCK, BLOCK), :]

    cur = lambda n: (jnp.minimum(n, nb - 1), 0)
    prev = lambda n: (jnp.maximum(jnp.minimum(n, nb - 1) - 1, 0), 0)
    out_kv = lambda n: (jnp.maximum(n - 1, 0), 0)
    return pl.pallas_call(
        body, name="attn_bwd", grid=(nb + 1,),
        in_specs=[pl.BlockSpec(memory_space=pltpu.SMEM), pl.BlockSpec((BLOCK, ATTN_DIM), cur),
                  pl.BlockSpec((BLOCK, KV_DIM), cur), pl.BlockSpec((BLOCK, KV_DIM), prev),
                  pl.BlockSpec((BLOCK, KV_DIM), cur), pl.BlockSpec((BLOCK, KV_DIM), prev),
                  pl.BlockSpec((N_HEADS, BLOCK, 2 * BLOCK), lambda n: (0, 0, 0)),
                  pl.BlockSpec((BLOCK, ATTN_DIM), cur)],
        out_specs=[pl.BlockSpec((BLOCK, ATTN_DIM), cur), pl.BlockSpec((BLOCK, KV_DIM), out_kv),
                   pl.BlockSpec((BLOCK, KV_DIM), out_kv),
                   pl.BlockSpec((N_HEADS, BLOCK, 2 * BLOCK), lambda n: (0, 0, 0)),
                   pl.BlockSpec(memory_space=pltpu.SMEM)],
        out_shape=[jax.ShapeDtypeStruct((T, ATTN_DIM), F32), jax.ShapeDtypeStruct((T, KV_DIM), F32),
                   jax.ShapeDtypeStruct((T, KV_DIM), F32),
                   jax.ShapeDtypeStruct((N_HEADS, BLOCK, 2 * BLOCK), F32), jax.ShapeDtypeStruct((N_HEADS,), F32)],
        scratch_shapes=[pltpu.VMEM((2 * BLOCK, KV_DIM), F32), pltpu.VMEM((2 * BLOCK, KV_DIM), F32),
                        pltpu.VMEM((BLOCK, KV_DIM), F32), pltpu.VMEM((BLOCK, KV_DIM), F32),
                        pltpu.VMEM((N_KV, GROUP_ROWS, 2 * BLOCK), F32), pltpu.VMEM((N_KV, GROUP_ROWS, 2 * BLOCK), F32),
                        pltpu.VMEM((N_KV, GROUP_ROWS, 2 * BLOCK), BF), pltpu.VMEM((N_KV, GROUP_ROWS, 2 * BLOCK), BF)],
        compiler_params=_params(1),
    )(sinks, qn, kn, kn, vv, vv, bias, do)


def _coords():
    return lax.axis_index("x"), lax.axis_index("y"), lax.axis_index("c")


def _sum8(name, blocks):
    def body(b_ref, o_ref):
        tot = b_ref[0]
        for d in range(1, 8):
            tot = tot + b_ref[d]
        o_ref[...] = tot

    return pl.pallas_call(body, name=name, out_shape=jax.ShapeDtypeStruct(blocks.shape[1:], F32))(blocks)


HBM_SPEC = pl.BlockSpec(memory_space=pltpu.HBM)
SEM_SPEC = pl.BlockSpec(memory_space=pltpu.SEMAPHORE)
ANY_SPEC = pl.BlockSpec(memory_space=pl.ANY)
DATAFLOW = pltpu.SideEffectType.DATAFLOW_SIDE_EFFECTING


OTHER_CHIPS = (4, 2, 6)
ALL_OTHERS = (1, 2, 3, 4, 5, 6, 7)


def _slot(x, y, c, peers):
    return 2 * x + y if peers is OTHER_CHIPS else 4 * x + 2 * y + c


def _slot_copy(land, sems, idx, x, y, c, k, peers, arriving):
    send_sems, recv_sems = sems
    px, py, pc = x ^ (k >> 2), y ^ ((k >> 1) & 1), c ^ (k & 1)
    mine = _slot(x, y, c, peers)
    dst = _slot(px, py, pc, peers) if arriving else mine
    return pltpu.make_async_remote_copy(src_ref=land.at[mine], dst_ref=land.at[dst], send_sem=send_sems.at[idx],
                                        recv_sem=recv_sems.at[idx], device_id=(px, py, pc), device_id_type=MESH)


def _gather_start(name, stacks, groups, peers, after):
    n = len(stacks)
    ng = len(groups)
    np_ = len(peers)
    after = tuple(after)

    def body(*refs):
        lands = refs[:n]
        first = n + len(after)
        sems = [(refs[first + 2 * g], refs[first + 2 * g + 1]) for g in range(ng)]
        token = refs[-1]
        x, y, c = _coords()
        for g, members in enumerate(groups):
            for i, t in enumerate(members):
                for j, k in enumerate(peers):
                    _slot_copy(lands[t], sems[g], np_ * i + j, x, y, c, k, peers, arriving=False).start()
        token[...] = jnp.zeros_like(token)

    out_shape = []
    for members in groups:
        out_shape += [pltpu.SemaphoreType.DMA((np_ * len(members),))] * 2
    out_shape += [pltpu.HBM(w.shape, w.dtype) for w in stacks]
    out_shape.append(jax.ShapeDtypeStruct((8, 128), F32))
    res = pl.pallas_call(
        body, name=name, out_shape=out_shape, in_specs=[HBM_SPEC] * n + [ANY_SPEC] * len(after),
        out_specs=[SEM_SPEC] * (2 * ng) + [HBM_SPEC] * n + [pl.BlockSpec(memory_space=pltpu.VMEM)],
        input_output_aliases={t: 2 * ng + t for t in range(n)},
        compiler_params=pltpu.CompilerParams(has_side_effects=DATAFLOW),
    )(*[pltpu.with_memory_space_constraint(w, pltpu.HBM) for w in stacks], *after)
    sems = [(res[2 * g], res[2 * g + 1]) for g in range(ng)]
    return sems, list(res[2 * ng:2 * ng + n]), res[-1]


def _gather_wait(name, stacks, sems, peers, after):
    n = len(stacks)
    after = tuple(after)

    def body(*refs):
        lands = refs[:n]
        group_sems = (refs[n], refs[n + 1])
        x, y, c = _coords()
        for i in range(n):
            for j, k in enumerate(peers):
                cp = _slot_copy(lands[i], group_sems, len(peers) * i + j, x, y, c, k, peers, arriving=True)
                cp.wait_send()
                cp.wait_recv()

    return pl.pallas_call(
        body, name=name, out_shape=[pltpu.HBM(w.shape, w.dtype) for w in stacks],
        in_specs=[HBM_SPEC] * n + [SEM_SPEC, SEM_SPEC] + [ANY_SPEC] * len(after), out_specs=[HBM_SPEC] * n,
        input_output_aliases={t: t for t in range(n)},
        compiler_params=pltpu.CompilerParams(has_side_effects=DATAFLOW),
    )(*stacks, sems[0], sems[1], *after)


N_PEERS = 7


def _peer(x, y, c, k):
    return x ^ (k >> 2), y ^ ((k >> 1) & 1), c ^ (k & 1)


def _reduce_copy(grad, land, sems, idx, x, y, c, k):
    px, py, pc = _peer(x, y, c, k)
    rh = grad.shape[1] // 2
    return pltpu.make_async_remote_copy(src_ref=grad.at[2 * px + py, pl.ds(pc * rh, rh), :], dst_ref=land.at[k - 1],
                                        send_sem=sems[0].at[idx], recv_sem=sems[1].at[idx], device_id=(px, py, pc),
                                        device_id_type=MESH)


def _reduce_start(name, grads):
    n = len(grads)

    def body(*refs):
        src, lands, sems, token = refs[:n], refs[n:2 * n], (refs[2 * n], refs[2 * n + 1]), refs[-1]
        x, y, c = _coords()
        for t in range(n):
            for k in range(1, N_PEERS + 1):
                _reduce_copy(src[t], lands[t], sems, N_PEERS * t + k - 1, x, y, c, k).start()
        token[...] = jnp.zeros_like(token)

    lands = [lax.empty((N_PEERS, g.shape[1] // 2, g.shape[2]), g.dtype) for g in grads]
    out_shape = [pltpu.SemaphoreType.DMA((N_PEERS * n,))] * 2
    out_shape += [pltpu.HBM(a.shape, a.dtype) for a in list(grads) + lands]
    out_shape.append(jax.ShapeDtypeStruct((8, 128), F32))
    res = pl.pallas_call(
        body, name=name, out_shape=out_shape, in_specs=[HBM_SPEC] * (2 * n),
        out_specs=[SEM_SPEC] * 2 + [HBM_SPEC] * (2 * n) + [pl.BlockSpec(memory_space=pltpu.VMEM)],
        input_output_aliases={t: 2 + t for t in range(2 * n)},
        compiler_params=pltpu.CompilerParams(has_side_effects=DATAFLOW),
    )(*[pltpu.with_memory_space_constraint(a, pltpu.HBM) for a in list(grads) + lands])
    return (res[0], res[1]), list(res[2:2 + n]), list(res[2 + n:2 + 2 * n]), res[-1]


def _reduce_wait(name, grads, lands, sems, after):
    n = len(grads)
    after = tuple(after)

    def body(*refs):
        src, dst, group_sems = refs[:n], refs[n:2 * n], (refs[2 * n], refs[2 * n + 1])
        x, y, c = _coords()
        for t in range(n):
            for k in range(1, N_PEERS + 1):
                cp = _reduce_copy(src[t], dst[t], group_sems, N_PEERS * t + k - 1, x, y, c, k)
                cp.wait_send()
                cp.wait_recv()

    res = pl.pallas_call(
        body, name=name, out_shape=[pltpu.HBM(a.shape, a.dtype) for a in list(grads) + list(lands)],
        in_specs=[HBM_SPEC] * (2 * n) + [SEM_SPEC, SEM_SPEC] + [ANY_SPEC] * len(after), out_specs=[HBM_SPEC] * (2 * n),
        input_output_aliases={t: t for t in range(2 * n)},
        compiler_params=pltpu.CompilerParams(has_side_effects=DATAFLOW),
    )(*grads, *lands, sems[0], sems[1], *after)
    return list(res[:n]), list(res[n:])


def _join_halves(name, halves, deps=()):
    n = len(halves)

    def body(*refs):
        src, dst = refs[:n], refs[n + len(deps):2 * n + len(deps)]
        send_sems, recv_sems = refs[-2:]
        x, y, c = _coords()
        cps = []
        for t in range(n):
            cp = pltpu.make_async_remote_copy(src_ref=src[t], dst_ref=dst[t], send_sem=send_sems.at[t],
                                              recv_sem=recv_sems.at[t], device_id=(x, y, 1 - c), device_id_type=MESH)
            cp.start()
            cps.append(cp)
        for cp in cps:
            cp.wait()

    anyspec = pl.BlockSpec(memory_space=pl.ANY)
    return pl.pallas_call(
        body, name=name, out_shape=[jax.ShapeDtypeStruct(h.shape, h.dtype) for h in halves],
        in_specs=[anyspec] * (n + len(deps)), out_specs=[anyspec] * n,
        scratch_shapes=[pltpu.SemaphoreType.DMA((n,)), pltpu.SemaphoreType.DMA((n,))],
    )(*halves, *deps)


def _row_block(rows):
    for rb in (512, 256, 128, 64, 32, 16):
        if rows % rb == 0:
            return rb
    raise ValueError(rows)


def _sum_devices(name, grad, land, place):
    S, R, C = grad.shape
    rh = R // 2
    rb = _row_block(rh)
    nbh = rh // rb

    def body(place_ref, g_ref, l_ref, o_ref):
        tot = g_ref[...].astype(F32)
        for k in range(N_PEERS):
            tot = tot + l_ref[k].astype(F32)
        o_ref[...] = tot

    return pl.pallas_call(
        body, name=name,
        grid_spec=pltpu.PrefetchScalarGridSpec(
            num_scalar_prefetch=1, grid=(nbh,),
            in_specs=[pl.BlockSpec((None, rb, C), lambda r, place: (place[0], place[1] * nbh + r, 0)),
                      pl.BlockSpec((N_PEERS, rb, C), lambda r, place: (0, r, 0))],
            out_specs=pl.BlockSpec((rb, C), lambda r, place: (r, 0))),
        out_shape=jax.ShapeDtypeStruct((rh, C), F32), compiler_params=_params(1),
    )(place, grad, land)


def _adamw_math(w, g, m, v):
    m2 = ADAM_B1 * m + (1.0 - ADAM_B1) * g
    v2 = ADAM_B2 * v + (1.0 - ADAM_B2) * (g * g)
    m_hat = m2 / (1.0 - ADAM_B1 ** ADAM_STEP)
    v_hat = v2 / (1.0 - ADAM_B2 ** ADAM_STEP)
    delta = -ADAM_LR * (m_hat / (jnp.sqrt(v_hat) + ADAM_EPS) + ADAM_WD * w)
    return delta, m2, v2


def _adamw(name, w, m, v, gs):
    L, R, C = w.shape
    Rh = R // 2
    rb = _row_block(Rh)
    nbh = Rh // rb
    assert len(gs) == L

    def body(core_ref, w_ref, m_ref, v_ref, *rest):
        g_refs, (go_ref, d_ref, m2_ref, v2_ref) = rest[:2 * L], rest[2 * L:]
        layer, half = pl.program_id(0), pl.program_id(1)
        mine = half == core_ref[0]
        g = jnp.where(mine, g_refs[0][...], g_refs[1][...])
        for t in range(1, L):
            g = jnp.where(layer == t, jnp.where(mine, g_refs[2 * t][...], g_refs[2 * t + 1][...]), g)
        delta, m2, v2 = _adamw_math(w_ref[...], g, m_ref[...], v_ref[...])
        go_ref[...] = g
        d_ref[...] = delta
        m2_ref[...] = m2
        v2_ref[...] = v2

    wspec = pl.BlockSpec((None, rb, C), lambda l, h, r, core: (l, h * nbh + r, 0))
    gspec = pl.BlockSpec((rb, C), lambda l, h, r, core: (r, 0))
    return pl.pallas_call(
        body, name=name,
        grid_spec=pltpu.PrefetchScalarGridSpec(num_scalar_prefetch=1, grid=(L, 2, nbh),
                                               in_specs=[wspec] * 3 + [gspec] * (2 * L), out_specs=[wspec] * 4),
        out_shape=[jax.ShapeDtypeStruct((L, R, C), F32)] * 4, compiler_params=_params(3),
    )(lax.axis_index("c").astype(jnp.int32).reshape(1), w, m, v, *[g for pair in gs for g in pair])


def _adamw_small(ws, gs, ms, vs):
    n = len(ws)

    def body(*refs):
        w_refs, g_refs, m_refs, v_refs = (refs[k * n:(k + 1) * n] for k in range(4))
        d_refs, m2_refs, v2_refs = (refs[(4 + k) * n:(5 + k) * n] for k in range(3))
        for t in range(n):
            delta, m2, v2 = _adamw_math(w_refs[t][...], g_refs[t][...], m_refs[t][...], v_refs[t][...])
            d_refs[t][...] = delta
            m2_refs[t][...] = m2
            v2_refs[t][...] = v2

    res = pl.pallas_call(body, name="adamw_small", out_shape=[jax.ShapeDtypeStruct(w.shape, F32) for w in ws] * 3)(
        *ws, *gs, *ms, *vs)
    return res[:n], res[n:2 * n], res[2 * n:]


def _packed_rows(shape):
    c = shape[-1]
    return (int(np.prod(shape)) // c) * -(-c // LANES)


def _pack(arrays):
    total = sum(_packed_rows(a.shape) for a in arrays)
    total += -total % 8
    buf, r0 = None, 0
    for a in arrays:
        a = a.astype(F32).reshape(-1, a.shape[-1])
        r, c = a.shape
        k = -(-c // LANES)
        a = jnp.pad(a, ((0, 0), (0, k * LANES - c))).reshape(r * k, LANES)
        a = jnp.pad(a, ((r0, total - r0 - r * k), (0, 0)))
        buf = a if buf is None else buf + a
        r0 += r * k
    return buf


def _unpack(buf, shapes):
    out, r0 = [], 0
    for shp in shapes:
        c = shp[-1]
        rows = _packed_rows(shp)
        out.append(buf[r0:r0 + rows].reshape(-1, -(-c // LANES) * LANES)[:, :c].reshape(shp))
        r0 += rows
    return out


def _rms(x, g):
    return x * lax.rsqrt(jnp.mean(x * x, axis=-1, keepdims=True) + NORM_EPS) * g


def _residual_norm_ep(acc, *rest):
    *bias, res, gain = rest
    x = acc + res + (bias[0] if bias else 0.0)
    return x, _rms(x, gain)


RESIDUAL_NORM_OUTS = (("tile", F32), ("tile", BF))


def _mlp_up(tag, h, w_up_sm):
    (up,) = _mm(f"mlp{tag}_up", h, w_up_sm, nt=False, b_sm=True, tm=2048, tn=1024, rows=256,
                ep_fn=lambda acc: (acc,), outs=(("tile", BF),))
    return up


RMS_BWD_OUTS = (("tile", F32), ("tile", BF), ("colsum", F32), ("colsum", F32))


def _mlp_bwd(tag, dy, dy_bf, x, g, up, w_up_sm, w_down):
    (dup,) = _mm(f"mlp{tag}_dup", dy_bf, w_down, nt=True, tm=2048, tn=1024, rows=256, ep_in=((up, "tile"),),
                 ep_fn=lambda acc, u: (acc * (2.0 * jnp.maximum(u.astype(F32), 0.0)),), outs=(("tile", BF),))
    dx, dx_bf, dg, dx_sum = _mm(f"mlp{tag}_dx", dup, w_up_sm, nt=True, b_sm=True, tm=512, tn=1024, rows=256,
                                ep_in=((x, "tile"), (g, "row"), (dy, "tile")), ep_fn=_rms_bwd_ep, outs=RMS_BWD_OUTS)
    return dx, dx_bf, dg, dx_sum, dup


class _Reduction:
    def __init__(self, tag, grads, place):
        self.tag, self.place = tag, place
        self.sems, self.grads, self.lands, self.token = _reduce_start(f"reduce_start_{tag}", grads)

    def finish(self, after):
        grads, lands = _reduce_wait(f"reduce_wait_{self.tag}", self.grads, self.lands, self.sems, after)
        return [_sum_devices(f"reduce_sum_{self.tag}{i}", g, l, self.place) for i, (g, l) in enumerate(zip(grads, lands))]


def kernel(x, conv_norm_g, conv_w_in, conv_b_in, conv_dw, conv_dw_b, conv_ln_g, conv_ln_b, conv_w_out, conv_b_out, attn_norm_g, w_qkv, b_qkv, q_norm_g, k_norm_g, sinks, w_o, b_o, rel_bias, mlp_norm_g, w_up, w_down, loss_target, m_conv_norm_g, m_conv_w_in, m_conv_b_in, m_conv_dw, m_conv_dw_b, m_conv_ln_g, m_conv_ln_b, m_conv_w_out, m_conv_b_out, m_attn_norm_g, m_w_qkv, m_b_qkv, m_q_norm_g, m_k_norm_g, m_sinks, m_w_o, m_b_o, m_rel_bias, m_mlp_norm_g, m_w_up, m_w_down, v_conv_norm_g, v_conv_w_in, v_conv_b_in, v_conv_dw, v_conv_dw_b, v_conv_ln_g, v_conv_ln_b, v_conv_w_out, v_conv_b_out, v_attn_norm_g, v_w_qkv, v_b_qkv, v_q_norm_g, v_k_norm_g, v_sinks, v_w_o, v_b_o, v_rel_bias, v_mlp_norm_g, v_w_up, v_w_down):
    Dm = D_MODEL
    x2d = x[0]
    tgt = loss_target[0]
    T = x2d.shape[0]
    shard = 2 * lax.axis_index("x") + lax.axis_index("y")

    me = 2 * shard + lax.axis_index("c")

    def own_slot(block, slots, index):
        return lax.dynamic_update_slice(lax.empty((slots,) + block.shape, block.dtype), block[None],
                                        (index,) + (0,) * block.ndim)

    sharded_small = [conv_dw[0], attn_norm_g, b_qkv, b_o]
    (small_sems,), (small_land,), small_token = _gather_start(
        "small_weights_start", [own_slot(_pack(sharded_small), 8, me)], ((0,),), ALL_OTHERS, after=())

    big = [conv_w_in[0], conv_w_out[0], w_qkv[0], w_o[0], w_up[0], w_up[1], w_down[0], w_down[1]]
    stacks = [own_slot(w.astype(BF), N_SHARD, shard) for w in big]
    groups = ((0,), (1,), (4, 6), (2, 3), (5, 7))
    gather_sems, stacks, gather_token = _gather_start("gather_start", stacks, groups, OTHER_CHIPS, after=(small_token,))

    def gathered_group(g, name, after):
        return _gather_wait(name, [stacks[t] for t in groups[g]], gather_sems[g], OTHER_CHIPS, after)

    bucket = _bucket_table()
    bias = _bias_table(rel_bias, bucket)

    h0 = _rms_fwd("conv_norm", x2d, conv_norm_g, deps=(gather_token,))
    (w_in_sm,) = gathered_group(0, "gather_wait_conv_in", (h0, bias))
    (u,) = _mm("conv_in", h0, w_in_sm, nt=False, b_sm=True, tm=2048, tn=512, rows=256, ep_in=((conv_b_in, "row"),),
               ep_fn=lambda acc, b: (acc + b,), outs=(("tile", BF),))
    (gathered,) = _gather_wait("small_weights_wait", [small_land], small_sems, ALL_OTHERS, (u,))
    chips = [_unpack(gathered[2 * s], [a.shape for a in sharded_small]) for s in range(N_SHARD)]
    dw_f, attn_norm_f, b_qkv_f, b_o_f = (jnp.concatenate([chips[s][t] for s in range(N_SHARD)], axis=-1)
                                         for t in range(len(sharded_small)))
    dw_pad = jnp.pad(dw_f, ((0, HALO - CONV_W), (0, 0)))
    cv, s_act = _conv_fwd(u, dw_pad, conv_dw_b, conv_ln_g, conv_ln_b)
    (g_out,) = gathered_group(1, "gather_wait_conv_out", (s_act,))
    w_out_f = g_out.reshape(Dm, Dm)
    x1, h1 = _mm("conv_out", s_act, w_out_f, nt=False, tm=1024, tn=1024, rows=256,
                 ep_in=((conv_b_out, "row"), (x2d, "tile"), (mlp_norm_g[0:1], "row")), ep_fn=_residual_norm_ep,
                 outs=RESIDUAL_NORM_OUTS)

    g_up0, g_down0 = gathered_group(2, "gather_wait_mlp0", (x1,))
    w_up_sm = [g_up0, None]
    w_down_f = [g_down0.reshape(D_FF, Dm), None]
    up0 = _mlp_up(0, h1, w_up_sm[0])
    x2, h2 = _mm("mlp0_down", up0, w_down_f[0], nt=False, tm=512, tn=1024, rows=256, a_fn=_relu2,
                 ep_in=((x1, "tile"), (attn_norm_f, "row")), ep_fn=_residual_norm_ep, outs=RESIDUAL_NORM_OUTS)

    g_qkv, g_o = gathered_group(3, "gather_wait_attn", (x2,))
    w_qkv_f = jnp.transpose(g_qkv, (1, 0, 2)).reshape(Dm, QKV_DIM)
    w_o_f = g_o.reshape(ATTN_DIM, Dm)
    qg_t = jnp.tile(q_norm_g, (1, N_HEADS))
    kg_t = jnp.tile(k_norm_g, (1, N_KV))

    def qkv_ep(acc, b, qg, kg, sel_q, sel_q_t, sel_k, sel_k_t):
        proj = acc + b
        q, k, v = proj[:, :ATTN_DIM], proj[:, ATTN_DIM:ATTN_DIM + KV_DIM], proj[:, ATTN_DIM + KV_DIM:]
        return proj, _qk_normed(q, qg, (sel_q, sel_q_t), 1.0 / math.sqrt(HEAD_DIM)), _qk_normed(k, kg, (sel_k, sel_k_t), 1.0), v

    qkv, qn, kn, vv = _mm(
        "attn_qkv", h2, w_qkv_f, nt=False, tm=1024, tn=QKV_DIM, rows=256, ep_fn=qkv_ep,
        ep_in=((b_qkv_f, "row"), (qg_t, "whole"), (kg_t, "whole"))
        + tuple((m, "whole") for m in _head_select(ATTN_DIM) + _head_select(KV_DIM)),
        outs=(("tile", F32), ("tile", BF, ATTN_DIM), ("tile", BF, KV_DIM), ("tile", BF, KV_DIM)))
    sinks1 = sinks[0]
    att = _attn_fwd(qn, kn, vv, bias, sinks1)
    x3, h3 = _mm("attn_out", att, w_o_f, nt=False, tm=1024, tn=1024, rows=256,
                 ep_in=((b_o_f, "row"), (x2, "tile"), (mlp_norm_g[1:2], "row")), ep_fn=_residual_norm_ep,
                 outs=RESIDUAL_NORM_OUTS)

    g_up1, g_down1 = gathered_group(4, "gather_wait_mlp1", (x3,))
    w_up_sm[1] = g_up1
    w_down_f[1] = g_down1.reshape(D_FF, Dm)
    up1 = _mlp_up(1, h3, w_up_sm[1])

    def loss_ep(acc, r, t):
        diff = acc + r - t
        dy = diff * (1.0 / Dm)
        return dy, dy, jnp.sum(diff * diff, axis=0, keepdims=True)

    dy, dy_bf, sq = _mm("mlp1_down_loss", up1, w_down_f[1], nt=False, tm=512, tn=1024, rows=256, a_fn=_relu2,
                        ep_in=((x3, "tile"), (tgt, "tile")), ep_fn=loss_ep,
                        outs=(("tile", F32), ("tile", BF), ("colsum", F32)))

    place = jnp.stack([shard, lax.axis_index("c")]).astype(jnp.int32)
    dx3, dx3_bf, dg_mlp1, db_o, dup1 = _mlp_bwd(1, dy, dy_bf, x3, mlp_norm_g[1:2], up1, w_up_sm[1], w_down_f[1])
    dw_down1 = _mm_tn("mlp1_dw_down", up1, dy_bf, tm=1024, tn=1024, tk=2048, a_fn=_relu2)
    dw_up1 = _mm_tn("mlp1_dw_up", h3, dup1, tm=1024, tn=1024, tk=2048, out_sm=N_SHARD)
    red_mlp1 = _Reduction("mlp1", [dw_up1, dw_down1.reshape(N_SHARD, D_FF // N_SHARD, Dm)], place)

    ident = lambda acc: (acc,)
    (datt,) = _mm("attn_dout", dx3_bf, w_o_f, nt=True, tm=1024, tn=1024, rows=256, ep_fn=ident, outs=(("tile", BF),),
                  deps=(red_mlp1.token,))
    dw_o = _mm_tn("attn_dw_o", att, dx3_bf, tm=1024, tn=1024, tk=2048)
    dqn, dkn, dvv, dbias, dsinks = _attn_bwd(qn, kn, vv, bias, sinks1, datt)
    drel = _bias_grad(dbias, bucket)
    dqkv, db_qkv, dqg_t, dkg_t = _qk_norm_bwd(qkv, dqn, dkn, dvv, qg_t, kg_t)
    dw_qkv = _mm_tn("attn_dw_qkv", h2, dqkv, tm=1024, tn=QKV_DIM, tk=2048)
    red_attn = _Reduction("attn", [jnp.transpose(dw_qkv.reshape(Dm, N_SHARD, QKV_DIM // N_SHARD), (1, 0, 2)),
                                   dw_o.reshape(N_SHARD, ATTN_DIM // N_SHARD, Dm)], place)
    dx2, dx2_bf, dg_attn, _ = _mm("attn_dx", dqkv, w_qkv_f, nt=True, tm=1024, tn=1024, rows=256,
                                  ep_in=((x2, "tile"), (attn_norm_f, "row"), (dx3, "tile")), ep_fn=_rms_bwd_ep,
                                  outs=RMS_BWD_OUTS, deps=(red_attn.token,))

    dx1, dx1_bf, dg_mlp0, db_out, dup0 = _mlp_bwd(0, dx2, dx2_bf, x1, mlp_norm_g[0:1], up0, w_up_sm[0], w_down_f[0])
    dw_down0 = _mm_tn("mlp0_dw_down", up0, dx2_bf, tm=1024, tn=1024, tk=2048, a_fn=_relu2)
    dw_up0 = _mm_tn("mlp0_dw_up", h1, dup0, tm=1024, tn=1024, tk=2048, out_sm=N_SHARD)
    dw_out = _mm_tn("conv_dw_out", s_act, dx1_bf, tm=1024, tn=1024, tk=2048)
    red_mlp0 = _Reduction("mlp0", [dw_up0, dw_down0.reshape(N_SHARD, D_FF // N_SHARD, Dm),
                                   dw_out.reshape(N_SHARD, Dm // N_SHARD, Dm)], place)
    (r_qkv, r_o) = red_attn.finish((dx1,))
    (r_up1, r_down1) = red_mlp1.finish((dx1,))

    dcv, dln_g, dln_b, ddw_b = _mm("conv_ds", dx1_bf, w_out_f, nt=True, tm=1024, tn=1024, rows=256,
                                   ep_in=((cv, "tile"), (conv_ln_g, "row"), (conv_ln_b, "row")),
                                   ep_fn=_ln_silu_bwd_ep,
                                   outs=(("tile", F32), ("colsum", F32), ("colsum", F32), ("colsum", F32)),
                                   deps=(red_mlp0.token,))
    du, db_in, ddw8 = _conv_bwd(u, dcv, dw_pad)
    (r_up0, r_down0, r_out) = red_mlp0.finish((du,))
    dw_in = _mm_tn("conv_dw_in", h0, du, tm=1024, tn=512, tk=4096, out_sm=N_SHARD)
    red_conv = _Reduction("conv", [dw_in], place)
    def first_layer_ep(*args):
        tot, _, dg, _ = _rms_bwd_ep(*args)
        return tot, dg

    gx, dg_conv = _mm("conv_dx", du, w_in_sm, nt=True, b_sm=True, tm=1024, tn=1024, rows=256,
                      ep_in=((x2d, "tile"), (conv_norm_g, "row"), (dx1, "tile")), ep_fn=first_layer_ep,
                      outs=(("tile", F32), ("colsum", F32)), deps=(red_conv.token,))
    (r_in,) = red_conv.finish((gx,))

    dqg = dqg_t.reshape(N_HEADS, HEAD_DIM).sum(axis=0, keepdims=True)
    dkg = dkg_t.reshape(N_KV, HEAD_DIM).sum(axis=0, keepdims=True)
    small_full = [dg_conv, db_in, ddw8.sum(axis=1)[:CONV_W], ddw_b, dln_g, dln_b, db_out, dg_attn, db_qkv, dqg, dkg,
                  dsinks[None, :], db_o, drel.reshape(1, REL_BUCKETS * N_HEADS),
                  jnp.pad(dg_mlp0, ((0, 1), (0, 0))) + jnp.pad(dg_mlp1, ((1, 0), (0, 0))), sq]
    (sg_sems,), (sg_land,), sg_token = _gather_start(
        "small_grads_start", [own_slot(_pack(small_full), 8, me)], ((0,),), ALL_OTHERS, after=())

    mine = [r_in, r_out, r_qkv, r_o, r_up0, r_up1, r_down0, r_down1]
    r_in, r_out, r_qkv, r_o, r_up0, r_up1, r_down0, r_down1 = zip(
        mine, _join_halves("join_halves", mine, deps=(sg_token,)))

    big_out = {}
    for nm, w, m, v, gs in (("conv_w_in", conv_w_in, m_conv_w_in, v_conv_w_in, (r_in,)),
                            ("conv_w_out", conv_w_out, m_conv_w_out, v_conv_w_out, (r_out,)),
                            ("w_qkv", w_qkv, m_w_qkv, v_w_qkv, (r_qkv,)),
                            ("w_o", w_o, m_w_o, v_w_o, (r_o,)),
                            ("w_up", w_up, m_w_up, v_w_up, (r_up0, r_up1)),
                            ("w_down", w_down, m_w_down, v_w_down, (r_down0, r_down1))):
        big_out[nm] = _adamw(f"adamw_{nm}", w, m, v, gs)

    (sg_land,) = _gather_wait("small_grads_wait", [sg_land], sg_sems, ALL_OTHERS,
                              [big_out[nm][0] for nm in big_out])
    small_sum = _sum8("small_grads_sum", sg_land)
    (r_norm, r_b_in, r_dw, r_dw_b, r_ln_g, r_ln_b, r_b_out, r_attn_norm, r_b_qkv, r_qg, r_kg, r_sinks, r_b_o, r_rel,
     r_mlp_norm, r_sq) = _unpack(small_sum, [a.shape for a in small_full])
    loss = 0.5 * jnp.sum(r_sq) * (1.0 / Dm)

    def cols(a, width):
        return lax.dynamic_slice_in_dim(a, shard * width, width, axis=a.ndim - 1)

    small_names = ["conv_norm_g", "conv_b_in", "conv_dw", "conv_dw_b", "conv_ln_g", "conv_ln_b", "conv_b_out",
                   "attn_norm_g", "b_qkv", "q_norm_g", "k_norm_g", "sinks", "b_o", "rel_bias", "mlp_norm_g"]
    small_g = [r_norm, r_b_in, cols(r_dw, Dm // N_SHARD)[None], r_dw_b, r_ln_g, r_ln_b, r_b_out,
               cols(r_attn_norm, Dm // N_SHARD), cols(r_b_qkv, QKV_DIM // N_SHARD), r_qg, r_kg, r_sinks,
               cols(r_b_o, Dm // N_SHARD), r_rel.reshape(REL_BUCKETS, N_HEADS), r_mlp_norm]
    small_w = [conv_norm_g, conv_b_in, conv_dw, conv_dw_b, conv_ln_g, conv_ln_b, conv_b_out, attn_norm_g, b_qkv,
               q_norm_g, k_norm_g, sinks, b_o, rel_bias, mlp_norm_g]
    small_m = [m_conv_norm_g, m_conv_b_in, m_conv_dw, m_conv_dw_b, m_conv_ln_g, m_conv_ln_b, m_conv_b_out,
               m_attn_norm_g, m_b_qkv, m_q_norm_g, m_k_norm_g, m_sinks, m_b_o, m_rel_bias, m_mlp_norm_g]
    small_v = [v_conv_norm_g, v_conv_b_in, v_conv_dw, v_conv_dw_b, v_conv_ln_g, v_conv_ln_b, v_conv_b_out,
               v_attn_norm_g, v_b_qkv, v_q_norm_g, v_k_norm_g, v_sinks, v_b_o, v_rel_bias, v_mlp_norm_g]
    flat2 = lambda a: a.reshape(-1, a.shape[-1])
    small_g = [flat2(g) for g in small_g]
    d_s, m_s, v_s = _adamw_small([flat2(w) for w in small_w], small_g, [flat2(m) for m in small_m],
                                 [flat2(v) for v in small_v])
    small_out = {}
    for nm, w, g, d, m2, v2 in zip(small_names, small_w, small_g, d_s, m_s, v_s):
        small_out[nm] = tuple(a.reshape(w.shape) for a in (g, d, m2, v2))

    order = ["conv_norm_g", "conv_w_in", "conv_b_in", "conv_dw", "conv_dw_b", "conv_ln_g", "conv_ln_b", "conv_w_out",
             "conv_b_out", "attn_norm_g", "w_qkv", "b_qkv", "q_norm_g", "k_norm_g", "sinks", "w_o", "b_o", "rel_bias",
             "mlp_norm_g", "w_up", "w_down"]
    res = {**small_out, **big_out}
    outs = [loss, gx[None]]
    for part in range(4):
        outs += [res[nm][part] for nm in order]
    return tuple(outs)
```

```python
import math

import numpy as np
import jax
import jax.numpy as jnp
from jax import lax
from jax.experimental import pallas as pl
from jax.experimental.pallas import tpu as pltpu

F32 = jnp.float32
BF = jnp.bfloat16
MESH = pl.DeviceIdType.MESH

D_MODEL = 1024
D_FF = 4096
N_HEADS = 16
N_KV = 2
GROUP = N_HEADS // N_KV
HEAD_DIM = 64
ATTN_DIM = N_HEADS * HEAD_DIM
KV_DIM = N_KV * HEAD_DIM
QKV_DIM = ATTN_DIM + 2 * KV_DIM
BLOCK = 128
CONV_W = 31
HALO = 32
REL_BUCKETS = 32
REL_MAX_DIST = 128
NORM_EPS = 1e-6
NEG_INF = -1e30
N_SHARD = 4
LANES = 1024

ADAM_LR = 0.001
ADAM_B1 = 0.9
ADAM_B2 = 0.999
ADAM_EPS = 1e-08
ADAM_WD = 0.01
ADAM_STEP = 10

VMEM_LIMIT = 56 * 1024 * 1024


def _params(n_axes):
    return pltpu.CompilerParams(dimension_semantics=("arbitrary",) * n_axes, vmem_limit_bytes=VMEM_LIMIT)


def _dot(a, b, ca, cb):
    return lax.dot_general(a, b, (((ca,), (cb,)), ((), ())), preferred_element_type=F32)


def _mm(name, a, b, *, nt, tm, tn, ep_fn, outs, a_fn=None, b_sm=False, ep_in=(), deps=(), rows=None):
    M, K = a.shape
    rows = tm if rows is None else rows
    if b_sm:
        S, ks = b.shape[0], b.shape[2]
        N, per = (b.shape[1], None) if nt else (S * b.shape[2], b.shape[2] // tn)
        assert (S * ks == K) if nt else (b.shape[1] == K)
    else:
        N = b.shape[0] if nt else b.shape[1]
        assert (b.shape[1] if nt else b.shape[0]) == K
    assert M % tm == 0 and N % tn == 0 and tm % rows == 0
    ne, no, nd = len(ep_in), len(outs), len(deps)

    def body(a_ref, b_ref, *rest):
        ep_refs, out_refs = rest[:ne], rest[ne + nd:ne + nd + no]
        i = pl.program_id(1)
        sums = [None] * no
        for r in range(tm // rows):
            rs = pl.ds(r * rows, rows)

            def lhs(cols):
                av = a_ref[rs, cols]
                return (av if a_fn is None else a_fn(av)).astype(BF)

            if b_sm and nt:
                acc = None
                for s in range(S):
                    part = _dot(lhs(pl.ds(s * ks, ks)), b_ref[s].astype(BF), 1, 1)
                    acc = part if acc is None else acc + part
            else:
                acc = _dot(lhs(slice(None)), b_ref[...].astype(BF), 1, 1 if nt else 0)
            ep_vals = [ref[rs, :] if kind == "tile" else ref[...] for ref, (_, kind) in zip(ep_refs, ep_in)]
            vals = ep_fn(acc, *ep_vals)
            for o, ((kind, dt, *_), ref, val) in enumerate(zip(outs, out_refs, vals)):
                if kind == "tile":
                    ref[rs, :] = val.astype(dt)
                else:
                    sums[o] = val if sums[o] is None else sums[o] + val
        for (kind, *_), ref, val in zip(outs, out_refs, sums):
            if kind == "colsum":
                @pl.when(i == 0)
                def _():
                    ref[...] = val

                @pl.when(i > 0)
                def _():
                    ref[...] += val

    if b_sm and nt:
        b_spec = pl.BlockSpec((S, tn, ks), lambda j, i: (0, j, 0))
    elif b_sm:
        b_spec = pl.BlockSpec((None, K, tn), lambda j, i: (j // per, 0, j % per))
    elif nt:
        b_spec = pl.BlockSpec((tn, K), lambda j, i: (j, 0))
    else:
        b_spec = pl.BlockSpec((K, tn), lambda j, i: (0, j))
    in_specs = [pl.BlockSpec((tm, K), lambda j, i: (i, 0)), b_spec]
    for arr, kind in ep_in:
        if kind == "tile":
            assert arr.shape == (M, N)
            in_specs.append(pl.BlockSpec((tm, tn), lambda j, i: (i, j)))
        elif kind == "whole":
            in_specs.append(pl.BlockSpec(arr.shape, lambda j, i, rank=arr.ndim: (0,) * rank))
        else:
            assert arr.shape == (1, N)
            in_specs.append(pl.BlockSpec((1, tn), lambda j, i: (0, j)))
    in_specs += [pl.BlockSpec(memory_space=pl.ANY)] * nd
    out_shape, out_specs = [], []
    for kind, dt, *width in outs:
        if kind == "tile" and width:
            assert tn == N
            out_shape.append(jax.ShapeDtypeStruct((M, width[0]), dt))
            out_specs.append(pl.BlockSpec((tm, width[0]), lambda j, i: (i, 0)))
        elif kind == "tile":
            out_shape.append(jax.ShapeDtypeStruct((M, N), dt))
            out_specs.append(pl.BlockSpec((tm, tn), lambda j, i: (i, j)))
        else:
            out_shape.append(jax.ShapeDtypeStruct((1, N), F32))
            out_specs.append(pl.BlockSpec((1, tn), lambda j, i: (0, j)))
    return pl.pallas_call(
        body, name=name, grid=(N // tn, M // tm), in_specs=in_specs, out_specs=out_specs, out_shape=out_shape,
        compiler_params=_params(2),
    )(a, b, *[arr for arr, _ in ep_in], *deps)


def _mm_tn(name, a, b, *, tm, tn, tk, a_fn=None, out_sm=None):
    T, Ka = a.shape
    N = b.shape[1]
    assert b.shape[0] == T and T % tk == 0 and Ka % tm == 0 and N % tn == 0
    nk = T // tk

    def body(a_ref, b_ref, o_ref, acc_ref):
        k = pl.program_id(2)

        @pl.when(k == 0)
        def _():
            acc_ref[...] = jnp.zeros_like(acc_ref)

        av = a_ref[...]
        if a_fn is not None:
            av = a_fn(av)
        acc_ref[...] += _dot(av.astype(BF), b_ref[...].astype(BF), 0, 0)

        @pl.when(k == nk - 1)
        def _():
            o_ref[...] = acc_ref[...].astype(BF)

    if out_sm is None:
        out_shape = jax.ShapeDtypeStruct((Ka, N), BF)
        out_spec = pl.BlockSpec((tm, tn), lambda i, j, k: (i, j))
    else:
        per = (N // out_sm) // tn
        assert per * tn * out_sm == N
        out_shape = jax.ShapeDtypeStruct((out_sm, Ka, N // out_sm), BF)
        out_spec = pl.BlockSpec((None, tm, tn), lambda i, j, k: (j // per, i, j % per))
    return pl.pallas_call(
        body, name=name, grid=(Ka // tm, N // tn, nk),
        in_specs=[pl.BlockSpec((tk, tm), lambda i, j, k: (k, i)), pl.BlockSpec((tk, tn), lambda i, j, k: (k, j))],
        out_specs=out_spec, out_shape=out_shape, scratch_shapes=[pltpu.VMEM((tm, tn), F32)],
        compiler_params=_params(3),
    )(a, b)


def _relu2(v):
    r = jnp.maximum(v.astype(F32), 0.0)
    return r * r


def _rms_bwd_ep(dh, x, g, dres):
    rstd = lax.rsqrt(jnp.mean(x * x, axis=-1, keepdims=True) + NORM_EPS)
    xh = x * rstd
    dxh = dh * g
    dx = rstd * (dxh - xh * jnp.mean(dxh * xh, axis=-1, keepdims=True))
    tot = dres + dx
    return tot, tot, jnp.sum(dh * xh, axis=0, keepdims=True), jnp.sum(tot, axis=0, keepdims=True)


def _rms_fwd(name, x, g, tm=512, deps=()):
    T, Dm = x.shape

    def body(x_ref, g_ref, *rest):
        o_ref = rest[-1]
        xv = x_ref[...]
        rstd = lax.rsqrt(jnp.mean(xv * xv, axis=-1, keepdims=True) + NORM_EPS)
        o_ref[...] = (xv * rstd * g_ref[...]).astype(BF)

    return pl.pallas_call(
        body, name=name, grid=(T // tm,),
        in_specs=[pl.BlockSpec((tm, Dm), lambda i: (i, 0)), pl.BlockSpec((1, Dm), lambda i: (0, 0))]
        + [pl.BlockSpec(memory_space=pl.ANY)] * len(deps),
        out_specs=pl.BlockSpec((tm, Dm), lambda i: (i, 0)), out_shape=jax.ShapeDtypeStruct((T, Dm), BF),
        compiler_params=_params(1),
    )(x, g, *deps)


HEAD_COLS = 128


def _two_term_dot(v, m):
    hi = v.astype(BF)
    lo = (v - hi.astype(F32)).astype(BF)
    return _dot(hi, m, 1, 0) + _dot(lo, m, 1, 0)


def _head_sum(v, select):
    sel, sel_t = select
    return _two_term_dot(_two_term_dot(v, sel), sel_t)


def _head_select(n):
    sel = (np.arange(n)[:, None] // HEAD_DIM == np.arange(HEAD_COLS)[None, :]).astype(np.float32)
    return jnp.asarray(sel, dtype=BF), jnp.asarray(sel.T, dtype=BF)


def _qk_normed(x, g, select, scale):
    r = lax.rsqrt(_head_sum(x * x, select) * (1.0 / HEAD_DIM) + NORM_EPS)
    return x * r * g * scale


def _qk_norm_bwd(qkv, dqn, dkn, dv, qg_t, kg_t, tm=256):
    T = qkv.shape[0]

    def body(x_ref, dq_ref, dk_ref, dv_ref, qg_ref, kg_ref, sq_ref, sqt_ref, sk_ref, skt_ref,
             o_ref, db_ref, dqg_ref, dkg_ref):
        i = pl.program_id(0)

        def one(x, dy, g, select):
            r = lax.rsqrt(_head_sum(x * x, select) * (1.0 / HEAD_DIM) + NORM_EPS)
            xh = x * r
            dxh = dy * g
            dx = r * (dxh - xh * (_head_sum(dxh * xh, select) * (1.0 / HEAD_DIM)))
            return dx, jnp.sum(dy * xh, axis=0, keepdims=True)

        dq, dqg = one(x_ref[:, pl.ds(0, ATTN_DIM)], dq_ref[...], qg_ref[...], (sq_ref[...], sqt_ref[...]))
        dk, dkg = one(x_ref[:, pl.ds(ATTN_DIM, KV_DIM)], dk_ref[...], kg_ref[...], (sk_ref[...], skt_ref[...]))
        dvv = dv_ref[...]
        o_ref[:, pl.ds(0, ATTN_DIM)] = dq.astype(BF)
        o_ref[:, pl.ds(ATTN_DIM, KV_DIM)] = dk.astype(BF)
        o_ref[:, pl.ds(ATTN_DIM + KV_DIM, KV_DIM)] = dvv.astype(BF)
        sq, sk, sv = (jnp.sum(t, axis=0, keepdims=True) for t in (dq, dk, dvv))

        @pl.when(i == 0)
        def _():
            db_ref[:, pl.ds(0, ATTN_DIM)] = sq
            db_ref[:, pl.ds(ATTN_DIM, KV_DIM)] = sk
            db_ref[:, pl.ds(ATTN_DIM + KV_DIM, KV_DIM)] = sv
            dqg_ref[...] = dqg
            dkg_ref[...] = dkg

        @pl.when(i > 0)
        def _():
            db_ref[:, pl.ds(0, ATTN_DIM)] += sq
            db_ref[:, pl.ds(ATTN_DIM, KV_DIM)] += sk
            db_ref[:, pl.ds(ATTN_DIM + KV_DIM, KV_DIM)] += sv
            dqg_ref[...] += dqg
            dkg_ref[...] += dkg

    full = lambda shape: pl.BlockSpec(shape, lambda i: (0, 0))
    row = lambda n: pl.BlockSpec((tm, n), lambda i: (i, 0))
    return pl.pallas_call(
        body, name="qk_norm_bwd", grid=(T // tm,),
        in_specs=[row(QKV_DIM), row(ATTN_DIM), row(KV_DIM), row(KV_DIM), full((1, ATTN_DIM)), full((1, KV_DIM)),
                  full((ATTN_DIM, HEAD_COLS)), full((HEAD_COLS, ATTN_DIM)), full((KV_DIM, HEAD_COLS)), full((HEAD_COLS, KV_DIM))],
        out_specs=[row(QKV_DIM), full((1, QKV_DIM)), full((1, ATTN_DIM)), full((1, KV_DIM))],
        out_shape=[jax.ShapeDtypeStruct((T, QKV_DIM), BF), jax.ShapeDtypeStruct((1, QKV_DIM), F32),
                   jax.ShapeDtypeStruct((1, ATTN_DIM), F32), jax.ShapeDtypeStruct((1, KV_DIM), F32)],
        compiler_params=_params(1),
    )(qkv, dqn, dkn, dv, qg_t, kg_t, *_head_select(ATTN_DIM), *_head_select(KV_DIM))


ROWS = 64
COLS = 128


SUBLANES = 8
FIRST_TAP = HALO - (CONV_W - 1)


def _glu(a, g):
    return a.astype(F32) * jax.nn.sigmoid(g.astype(F32))


def _shifted(xe, s):
    return xe if s == 0 else pltpu.roll(xe, ROWS + HALO - s, axis=0)


def _conv_fwd(u, dw_pad, dw_b, ln_g, ln_b, tm=256):
    T = u.shape[0]
    Dm = D_MODEL
    hpt = tm // HALO

    def body(ac_ref, gc_ref, ap_ref, gp_ref, w_ref, wb_ref, lg_ref, lb_ref, cv_ref, s_ref, ext):
        i = pl.program_id(0)
        ext[pl.ds(0, HALO), :] = jnp.where(i > 0, _glu(ap_ref[...], gp_ref[...]), 0.0)
        ext[pl.ds(HALO, tm), :] = _glu(ac_ref[...], gc_ref[...])

        def rows(r, carry):
            r0 = pl.multiple_of(r * ROWS, ROWS)
            for c in range(Dm // COLS):
                cs = pl.ds(c * COLS, COLS)
                xe = ext[pl.ds(r0, ROWS + HALO), cs]
                acc = jnp.zeros((ROWS, COLS), F32)
                for s in range(SUBLANES):
                    xs = _shifted(xe, s)
                    for j in range(CONV_W):
                        off = FIRST_TAP + j
                        if off % SUBLANES == s:
                            acc = acc + xs[off - s:off - s + ROWS, :] * w_ref[pl.ds(j, 1), cs]
                cv_ref[pl.ds(r0, ROWS), cs] = acc + wb_ref[:, cs]
            return carry

        lax.fori_loop(0, tm // ROWS, rows, 0)
        cv = cv_ref[...]
        xc = cv - jnp.mean(cv, axis=-1, keepdims=True)
        y = xc * lax.rsqrt(jnp.mean(xc * xc, axis=-1, keepdims=True) + NORM_EPS) * lg_ref[...] + lb_ref[...]
        s_ref[...] = (y * jax.nn.sigmoid(y)).astype(BF)

    full = lambda shape: pl.BlockSpec(shape, lambda i: (0, 0))
    return pl.pallas_call(
        body, name="conv_fwd", grid=(T // tm,),
        in_specs=[pl.BlockSpec((tm, Dm), lambda i: (i, 0)), pl.BlockSpec((tm, Dm), lambda i: (i, 1)),
                  pl.BlockSpec((HALO, Dm), lambda i: (jnp.maximum(i * hpt - 1, 0), 0)),
                  pl.BlockSpec((HALO, Dm), lambda i: (jnp.maximum(i * hpt - 1, 0), 1)),
                  full((HALO, Dm)), full((1, Dm)), full((1, Dm)), full((1, Dm))],
        out_specs=[pl.BlockSpec((tm, Dm), lambda i: (i, 0)), pl.BlockSpec((tm, Dm), lambda i: (i, 0))],
        out_shape=[jax.ShapeDtypeStruct((T, Dm), F32), jax.ShapeDtypeStruct((T, Dm), BF)],
        scratch_shapes=[pltpu.VMEM((tm + HALO, Dm), F32)],
        compiler_params=_params(1),
    )(u, u, u, u, dw_pad, dw_b, ln_g, ln_b)


def _ln_silu_bwd_ep(ds, cv, lg, lb):
    xc = cv - jnp.mean(cv, axis=-1, keepdims=True)
    rstd = lax.rsqrt(jnp.mean(xc * xc, axis=-1, keepdims=True) + NORM_EPS)
    xh = xc * rstd
    y = xh * lg + lb
    sg = jax.nn.sigmoid(y)
    dy = ds * (sg * (1.0 + y * (1.0 - sg)))
    dxh = dy * lg
    dcv = rstd * (dxh - jnp.mean(dxh, axis=-1, keepdims=True) - xh * jnp.mean(dxh * xh, axis=-1, keepdims=True))
    return (dcv, jnp.sum(dy * xh, axis=0, keepdims=True), jnp.sum(dy, axis=0, keepdims=True),
            jnp.sum(dcv, axis=0, keepdims=True))


def _conv_bwd(u, dcv, dw_pad, tm=256):
    T = u.shape[0]
    Dm = D_MODEL
    hpt = tm // HALO
    last = T // HALO - 1
    nt = T // tm

    def body(ac_ref, gc_ref, ap_ref, gp_ref, dc_ref, dn_ref, w_ref, du_ref, db_ref, dw_ref, ext_g, ext_d):
        i = pl.program_id(0)
        ext_g[pl.ds(0, HALO), :] = jnp.where(i > 0, _glu(ap_ref[...], gp_ref[...]), 0.0)
        ext_g[pl.ds(HALO, tm), :] = _glu(ac_ref[...], gc_ref[...])
        ext_d[pl.ds(0, tm), :] = dc_ref[...]
        ext_d[pl.ds(tm, HALO), :] = jnp.where(i < nt - 1, dn_ref[...], 0.0)

        @pl.when(i == 0)
        def _():
            db_ref[...] = jnp.zeros_like(db_ref)
            dw_ref[...] = jnp.zeros_like(dw_ref)

        def rows(r, carry):
            r0 = pl.multiple_of(r * ROWS, ROWS)
            rs = pl.ds(r0, ROWS)
            for c in range(Dm // COLS):
                cs = pl.ds(c * COLS, COLS)
                cs2 = pl.ds(Dm + c * COLS, COLS)
                de = ext_d[pl.ds(r0, ROWS + HALO), cs]
                ge = ext_g[pl.ds(r0, ROWS + HALO), cs]
                dcur = de[0:ROWS, :]
                acc = jnp.zeros((ROWS, COLS), F32)
                for s in range(SUBLANES):
                    ds_, gs_ = _shifted(de, s), _shifted(ge, s)
                    for j in range(CONV_W):
                        off = CONV_W - 1 - j
                        if off % SUBLANES == s:
                            acc = acc + ds_[off - s:off - s + ROWS, :] * w_ref[pl.ds(j, 1), cs]
                        goff = FIRST_TAP + j
                        if goff % SUBLANES == s:
                            prod = dcur * gs_[goff - s:goff - s + ROWS, :]
                            dw_ref[j, :, cs] += jnp.sum(prod.reshape(ROWS // SUBLANES, SUBLANES, COLS), axis=0)
                a = ac_ref[rs, cs].astype(F32)
                sg = jax.nn.sigmoid(gc_ref[rs, cs].astype(F32))
                da = acc * sg
                dg = acc * a * sg * (1.0 - sg)
                du_ref[rs, cs] = da.astype(BF)
                du_ref[rs, cs2] = dg.astype(BF)
                db_ref[:, cs] += jnp.sum(da, axis=0, keepdims=True)
                db_ref[:, cs2] += jnp.sum(dg, axis=0, keepdims=True)
            return carry

        lax.fori_loop(0, tm // ROWS, rows, 0)

    return pl.pallas_call(
        body, name="conv_bwd", grid=(nt,),
        in_specs=[pl.BlockSpec((tm, Dm), lambda i: (i, 0)), pl.BlockSpec((tm, Dm), lambda i: (i, 1)),
                  pl.BlockSpec((HALO, Dm), lambda i: (jnp.maximum(i * hpt - 1, 0), 0)),
                  pl.BlockSpec((HALO, Dm), lambda i: (jnp.maximum(i * hpt - 1, 0), 1)),
                  pl.BlockSpec((tm, Dm), lambda i: (i, 0)),
                  pl.BlockSpec((HALO, Dm), lambda i: (jnp.minimum((i + 1) * hpt, last), 0)),
                  pl.BlockSpec((HALO, Dm), lambda i: (0, 0))],
        out_specs=[pl.BlockSpec((tm, 2 * Dm), lambda i: (i, 0)), pl.BlockSpec((1, 2 * Dm), lambda i: (0, 0)),
                   pl.BlockSpec((HALO, 8, Dm), lambda i: (0, 0, 0))],
        out_shape=[jax.ShapeDtypeStruct((T, 2 * Dm), BF), jax.ShapeDtypeStruct((1, 2 * Dm), F32),
                   jax.ShapeDtypeStruct((HALO, 8, Dm), F32)],
        scratch_shapes=[pltpu.VMEM((tm + HALO, Dm), F32), pltpu.VMEM((tm + HALO, Dm), F32)],
        compiler_params=_params(1),
    )(u, u, u, u, dcv, dcv, dw_pad)


def _bucket_table():
    q_loc = np.arange(BLOCK)[:, None]
    k_loc = np.arange(2 * BLOCK)[None, :]
    dist = q_loc + BLOCK - k_loc
    n = np.maximum(dist, 0)
    max_exact = REL_BUCKETS // 2
    large = max_exact + (np.log(np.maximum(n, 1).astype(np.float32) / max_exact)
                         / math.log(REL_MAX_DIST / max_exact) * (REL_BUCKETS - max_exact)).astype(np.int32)
    large = np.minimum(large, REL_BUCKETS - 1)
    bucket = np.where(n < max_exact, n, large).astype(np.int32)
    return jnp.asarray(np.where((dist >= 0) & (dist < BLOCK), bucket, -1).astype(np.int32))


def _bias_table(rel_bias, bucket):
    def body(rb_ref, bk_ref, o_ref):
        bk = bk_ref[...]
        for h in range(N_HEADS):
            acc = jnp.full((BLOCK, 2 * BLOCK), NEG_INF, F32)
            for b in range(REL_BUCKETS):
                acc = jnp.where(bk == b, rb_ref[b, h], acc)
            o_ref[h] = acc

    return pl.pallas_call(
        body, name="bias_table", out_shape=jax.ShapeDtypeStruct((N_HEADS, BLOCK, 2 * BLOCK), F32),
        in_specs=[pl.BlockSpec(memory_space=pltpu.SMEM), pl.BlockSpec(memory_space=pltpu.VMEM)],
        out_specs=pl.BlockSpec(memory_space=pltpu.VMEM),
    )(rel_bias, bucket)


def _bias_grad(dbias, bucket):
    def body(db_ref, bk_ref, o_ref):
        bk = bk_ref[...]
        for b in range(REL_BUCKETS):
            sel = bk == b
            for h in range(N_HEADS):
                o_ref[b, h] = jnp.sum(jnp.where(sel, db_ref[h], 0.0))

    return pl.pallas_call(
        body, name="bias_grad", out_shape=jax.ShapeDtypeStruct((REL_BUCKETS, N_HEADS), F32),
        in_specs=[pl.BlockSpec(memory_space=pltpu.VMEM), pl.BlockSpec(memory_space=pltpu.VMEM)],
        out_specs=pl.BlockSpec(memory_space=pltpu.SMEM),
    )(dbias, bucket)


GROUP_ROWS = GROUP * BLOCK


def _head_probs(qk, bias_h, sink, first):
    s = jnp.where(first, NEG_INF, qk + bias_h)
    m = jnp.maximum(jnp.max(s, axis=-1, keepdims=True), sink)
    p = jnp.exp(s - m)
    ps = jnp.exp(sink - m)
    inv = 1.0 / (jnp.sum(p, axis=-1, keepdims=True) + ps)
    return p * inv, ps * inv


def _band(prev_ref, cur_ref, g):
    hs = pl.ds(g * HEAD_DIM, HEAD_DIM)
    return jnp.concatenate([prev_ref[:, hs], cur_ref[:, hs]], axis=0)


def _stack_heads(ref, g):
    return jnp.concatenate([ref[:, pl.ds((g * GROUP + hh) * HEAD_DIM, HEAD_DIM)] for hh in range(GROUP)], axis=0)


def _unstack_heads(ref, g, stacked, dtype):
    for hh in range(GROUP):
        ref[:, pl.ds((g * GROUP + hh) * HEAD_DIM, HEAD_DIM)] = stacked[hh * BLOCK:(hh + 1) * BLOCK, :].astype(dtype)


def _first_mask(n):
    col = lax.broadcasted_iota(jnp.int32, (1, 2 * BLOCK), 1)
    return jnp.logical_and(n == 0, col < BLOCK)


def _head_rows(hh):
    return pl.ds(hh * BLOCK, BLOCK)


def _attn_fwd(qn, kn, vv, bias, sinks):
    T = qn.shape[0]
    nb = T // BLOCK

    def body(sk_ref, q_ref, kc_ref, kp_ref, vc_ref, vp_ref, b_ref, o_ref, qk_buf, p_buf):
        first = _first_mask(pl.program_id(0))
        for g in range(N_KV):
            qk_buf[g] = _dot(_stack_heads(q_ref, g), _band(kp_ref, kc_ref, g), 1, 1)
        for g in range(N_KV):
            for hh in range(GROUP):
                h = g * GROUP + hh
                pn, _ = _head_probs(qk_buf[g, _head_rows(hh), :], b_ref[h], sk_ref[h], first)
                p_buf[g, _head_rows(hh), :] = pn.astype(BF)
        for g in range(N_KV):
            _unstack_heads(o_ref, g, _dot(p_buf[g], _band(vp_ref, vc_ref, g), 1, 0), BF)

    cur = lambda n: (n, 0)
    prev = lambda n: (jnp.maximum(n - 1, 0), 0)
    return pl.pallas_call(
        body, name="attn_fwd", grid=(nb,),
        in_specs=[pl.BlockSpec(memory_space=pltpu.SMEM), pl.BlockSpec((BLOCK, ATTN_DIM), cur),
                  pl.BlockSpec((BLOCK, KV_DIM), cur), pl.BlockSpec((BLOCK, KV_DIM), prev),
                  pl.BlockSpec((BLOCK, KV_DIM), cur), pl.BlockSpec((BLOCK, KV_DIM), prev),
                  pl.BlockSpec((N_HEADS, BLOCK, 2 * BLOCK), lambda n: (0, 0, 0))],
        out_specs=pl.BlockSpec((BLOCK, ATTN_DIM), cur), out_shape=jax.ShapeDtypeStruct((T, ATTN_DIM), BF),
        scratch_shapes=[pltpu.VMEM((N_KV, GROUP_ROWS, 2 * BLOCK), F32), pltpu.VMEM((N_KV, GROUP_ROWS, 2 * BLOCK), BF)],
        compiler_params=_params(1),
    )(sinks, qn, kn, kn, vv, vv, bias)


def _attn_bwd(qn, kn, vv, bias, sinks, do):
    T = qn.shape[0]
    nb = T // BLOCK
    scale = 1.0 / math.sqrt(HEAD_DIM)

    def body(sk_ref, q_ref, kc_ref, kp_ref, vc_ref, vp_ref, b_ref, do_ref,
             dq_ref, dk_ref, dv_ref, db_ref, dsk_ref, dk_full, dv_full, dk_carry, dv_carry, qk_buf, dp_buf, p_buf, ds_buf):
        n = pl.program_id(0)

        @pl.when(n == 0)
        def _():
            db_ref[...] = jnp.zeros_like(db_ref)
            dk_carry[...] = jnp.zeros_like(dk_carry)
            dv_carry[...] = jnp.zeros_like(dv_carry)
            for h in range(N_HEADS):
                dsk_ref[h] = 0.0

        @pl.when(n < nb)
        def _():
            first = _first_mask(n)
            ks = [_band(kp_ref, kc_ref, g) for g in range(N_KV)]
            qs = [_stack_heads(q_ref, g) for g in range(N_KV)]
            douts = [_stack_heads(do_ref, g) for g in range(N_KV)]
            for g in range(N_KV):
                qk_buf[g] = _dot(qs[g], ks[g], 1, 1)
                dp_buf[g] = _dot(douts[g], _band(vp_ref, vc_ref, g), 1, 1)
            for g in range(N_KV):
                for hh in range(GROUP):
                    h = g * GROUP + hh
                    rows = _head_rows(hh)
                    pn, psink = _head_probs(qk_buf[g, rows, :], b_ref[h], sk_ref[h], first)
                    dp = dp_buf[g, rows, :]
                    delta = jnp.sum(pn * dp, axis=-1, keepdims=True)
                    ds = pn * (dp - delta)
                    dsk_ref[h] += -jnp.sum(psink * delta)
                    db_ref[h] += ds
                    ds_buf[g, rows, :] = ds.astype(BF)
                    p_buf[g, rows, :] = pn.astype(BF)
            for g in range(N_KV):
                dsb = ds_buf[g]
                _unstack_heads(dq_ref, g, _dot(dsb, ks[g], 1, 0) * scale, F32)
                gs = pl.ds(g * HEAD_DIM, HEAD_DIM)
                dk_full[:, gs] = _dot(dsb, qs[g], 0, 0)
                dv_full[:, gs] = _dot(p_buf[g], douts[g], 0, 0)

        @pl.when(n == nb)
        def _():
            dk_full[...] = jnp.zeros_like(dk_full)
            dv_full[...] = jnp.zeros_like(dv_full)

        dk_ref[...] = dk_carry[...] + dk_full[pl.ds(0, BLOCK), :]
        dv_ref[...] = dv_carry[...] + dv_full[pl.ds(0, BLOCK), :]
        dk_carry[...] = dk_full[pl.ds(BLOCK, BLOCK), :]
        dv_carry[...] = dv_full[pl.ds(BLOCK, BLOCK), :]

    cur = lambda n: (jnp.minimum(n, nb - 1), 0)
    prev = lambda n: (jnp.maximum(jnp.minimum(n, nb - 1) - 1, 0), 0)
    out_kv = lambda n: (jnp.maximum(n - 1, 0), 0)
    return pl.pallas_call(
        body, name="attn_bwd", grid=(nb + 1,),
        in_specs=[pl.BlockSpec(memory_space=pltpu.SMEM), pl.BlockSpec((BLOCK, ATTN_DIM), cur),
                  pl.BlockSpec((BLOCK, KV_DIM), cur), pl.BlockSpec((BLOCK, KV_DIM), prev),
                  pl.BlockSpec((BLOCK, KV_DIM), cur), pl.BlockSpec((BLOCK, KV_DIM), prev),
                  pl.BlockSpec((N_HEADS, BLOCK, 2 * BLOCK), lambda n: (0, 0, 0)),
                  pl.BlockSpec((BLOCK, ATTN_DIM), cur)],
        out_specs=[pl.BlockSpec((BLOCK, ATTN_DIM), cur), pl.BlockSpec((BLOCK, KV_DIM), out_kv),
                   pl.BlockSpec((BLOCK, KV_DIM), out_kv),
                   pl.BlockSpec((N_HEADS, BLOCK, 2 * BLOCK), lambda n: (0, 0, 0)),
                   pl.BlockSpec(memory_space=pltpu.SMEM)],
        out_shape=[jax.ShapeDtypeStruct((T, ATTN_DIM), F32), jax.ShapeDtypeStruct((T, KV_DIM), F32),
                   jax.ShapeDtypeStruct((T, KV_DIM), F32),
                   jax.ShapeDtypeStruct((N_HEADS, BLOCK, 2 * BLOCK), F32), jax.ShapeDtypeStruct((N_HEADS,), F32)],
        scratch_shapes=[pltpu.VMEM((2 * BLOCK, KV_DIM), F32), pltpu.VMEM((2 * BLOCK, KV_DIM), F32),
                        pltpu.VMEM((BLOCK, KV_DIM), F32), pltpu.VMEM((BLOCK, KV_DIM), F32),
                        pltpu.VMEM((N_KV, GROUP_ROWS, 2 * BLOCK), F32), pltpu.VMEM((N_KV, GROUP_ROWS, 2 * BLOCK), F32),
                        pltpu.VMEM((N_KV, GROUP_ROWS, 2 * BLOCK), BF), pltpu.VMEM((N_KV, GROUP_ROWS, 2 * BLOCK), BF)],
        compiler_params=_params(1),
    )(sinks, qn, kn, kn, vv, vv, bias, do)


def _coords():
    return lax.axis_index("x"), lax.axis_index("y"), lax.axis_index("c")


def _sum8(name, blocks):
    def body(b_ref, o_ref):
        tot = b_ref[0]
        for d in range(1, 8):
            tot = tot + b_ref[d]
        o_ref[...] = tot

    return pl.pallas_call(body, name=name, out_shape=jax.ShapeDtypeStruct(blocks.shape[1:], F32))(blocks)


HBM_SPEC = pl.BlockSpec(memory_space=pltpu.HBM)
SEM_SPEC = pl.BlockSpec(memory_space=pltpu.SEMAPHORE)
ANY_SPEC = pl.BlockSpec(memory_space=pl.ANY)
DATAFLOW = pltpu.SideEffectType.DATAFLOW_SIDE_EFFECTING


OTHER_CHIPS = (4, 2, 6)
ALL_OTHERS = (1, 2, 3, 4, 5, 6, 7)


def _slot(x, y, c, peers):
    return 2 * x + y if peers is OTHER_CHIPS else 4 * x + 2 * y + c


def _slot_copy(land, sems, idx, x, y, c, k, peers, arriving):
    send_sems, recv_sems = sems
    px, py, pc = x ^ (k >> 2), y ^ ((k >> 1) & 1), c ^ (k & 1)
    mine = _slot(x, y, c, peers)
    dst = _slot(px, py, pc, peers) if arriving else mine
    return pltpu.make_async_remote_copy(src_ref=land.at[mine], dst_ref=land.at[dst], send_sem=send_sems.at[idx],
                                        recv_sem=recv_sems.at[idx], device_id=(px, py, pc), device_id_type=MESH)


def _gather_start(name, stacks, groups, peers, after):
    n = len(stacks)
    ng = len(groups)
    np_ = len(peers)
    after = tuple(after)

    def body(*refs):
        lands = refs[:n]
        first = n + len(after)
        sems = [(refs[first + 2 * g], refs[first + 2 * g + 1]) for g in range(ng)]
        token = refs[-1]
        x, y, c = _coords()
        for g, members in enumerate(groups):
            for i, t in enumerate(members):
                for j, k in enumerate(peers):
                    _slot_copy(lands[t], sems[g], np_ * i + j, x, y, c, k, peers, arriving=False).start()
        token[...] = jnp.zeros_like(token)

    out_shape = []
    for members in groups:
        out_shape += [pltpu.SemaphoreType.DMA((np_ * len(members),))] * 2
    out_shape += [pltpu.HBM(w.shape, w.dtype) for w in stacks]
    out_shape.append(jax.ShapeDtypeStruct((8, 128), F32))
    res = pl.pallas_call(
        body, name=name, out_shape=out_shape, in_specs=[HBM_SPEC] * n + [ANY_SPEC] * len(after),
        out_specs=[SEM_SPEC] * (2 * ng) + [HBM_SPEC] * n + [pl.BlockSpec(memory_space=pltpu.VMEM)],
        input_output_aliases={t: 2 * ng + t for t in range(n)},
        compiler_params=pltpu.CompilerParams(has_side_effects=DATAFLOW),
    )(*[pltpu.with_memory_space_constraint(w, pltpu.HBM) for w in stacks], *after)
    sems = [(res[2 * g], res[2 * g + 1]) for g in range(ng)]
    return sems, list(res[2 * ng:2 * ng + n]), res[-1]


def _gather_wait(name, stacks, sems, peers, after):
    n = len(stacks)
    after = tuple(after)

    def body(*refs):
        lands = refs[:n]
        group_sems = (refs[n], refs[n + 1])
        x, y, c = _coords()
        for i in range(n):
            for j, k in enumerate(peers):
                cp = _slot_copy(lands[i], group_sems, len(peers) * i + j, x, y, c, k, peers, arriving=True)
                cp.wait_send()
                cp.wait_recv()

    return pl.pallas_call(
        body, name=name, out_shape=[pltpu.HBM(w.shape, w.dtype) for w in stacks],
        in_specs=[HBM_SPEC] * n + [SEM_SPEC, SEM_SPEC] + [ANY_SPEC] * len(after), out_specs=[HBM_SPEC] * n,
        input_output_aliases={t: t for t in range(n)},
        compiler_params=pltpu.CompilerParams(has_side_effects=DATAFLOW),
    )(*stacks, sems[0], sems[1], *after)


N_PEERS = 7


def _peer(x, y, c, k):
    return x ^ (k >> 2), y ^ ((k >> 1) & 1), c ^ (k & 1)


def _reduce_copy(grad, land, sems, idx, x, y, c, k):
    px, py, pc = _peer(x, y, c, k)
    rh = grad.shape[1] // 2
    return pltpu.make_async_remote_copy(src_ref=grad.at[2 * px + py, pl.ds(pc * rh, rh), :], dst_ref=land.at[k - 1],
                                        send_sem=sems[0].at[idx], recv_sem=sems[1].at[idx], device_id=(px, py, pc),
                                        device_id_type=MESH)


def _reduce_start(name, grads):
    n = len(grads)

    def body(*refs):
        src, lands, sems, token = refs[:n], refs[n:2 * n], (refs[2 * n], refs[2 * n + 1]), refs[-1]
        x, y, c = _coords()
        for t in range(n):
            for k in range(1, N_PEERS + 1):
                _reduce_copy(src[t], lands[t], sems, N_PEERS * t + k - 1, x, y, c, k).start()
        token[...] = jnp.zeros_like(token)

    lands = [lax.empty((N_PEERS, g.shape[1] // 2, g.shape[2]), g.dtype) for g in grads]
    out_shape = [pltpu.SemaphoreType.DMA((N_PEERS * n,))] * 2
    out_shape += [pltpu.HBM(a.shape, a.dtype) for a in list(grads) + lands]
    out_shape.append(jax.ShapeDtypeStruct((8, 128), F32))
    res = pl.pallas_call(
        body, name=name, out_shape=out_shape, in_specs=[HBM_SPEC] * (2 * n),
        out_specs=[SEM_SPEC] * 2 + [HBM_SPEC] * (2 * n) + [pl.BlockSpec(memory_space=pltpu.VMEM)],
        input_output_aliases={t: 2 + t for t in range(2 * n)},
        compiler_params=pltpu.CompilerParams(has_side_effects=DATAFLOW),
    )(*[pltpu.with_memory_space_constraint(a, pltpu.HBM) for a in list(grads) + lands])
    return (res[0], res[1]), list(res[2:2 + n]), list(res[2 + n:2 + 2 * n]), res[-1]


def _reduce_wait(name, grads, lands, sems, after):
    n = len(grads)
    after = tuple(after)

    def body(*refs):
        src, dst, group_sems = refs[:n], refs[n:2 * n], (refs[2 * n], refs[2 * n + 1])
        x, y, c = _coords()
        for t in range(n):
            for k in range(1, N_PEERS + 1):
                cp = _reduce_copy(src[t], dst[t], group_sems, N_PEERS * t + k - 1, x, y, c, k)
                cp.wait_send()
                cp.wait_recv()

    res = pl.pallas_call(
        body, name=name, out_shape=[pltpu.HBM(a.shape, a.dtype) for a in list(grads) + list(lands)],
        in_specs=[HBM_SPEC] * (2 * n) + [SEM_SPEC, SEM_SPEC] + [ANY_SPEC] * len(after), out_specs=[HBM_SPEC] * (2 * n),
        input_output_aliases={t: t for t in range(2 * n)},
        compiler_params=pltpu.CompilerParams(has_side_effects=DATAFLOW),
    )(*grads, *lands, sems[0], sems[1], *after)
    return list(res[:n]), list(res[n:])


def _join_halves(name, halves, deps=()):
    n = len(halves)

    def body(*refs):
        src, dst = refs[:n], refs[n + len(deps):2 * n + len(deps)]
        send_sems, recv_sems = refs[-2:]
        x, y, c = _coords()
        cps = []
        for t in range(n):
            cp = pltpu.make_async_remote_copy(src_ref=src[t], dst_ref=dst[t], send_sem=send_sems.at[t],
                                              recv_sem=recv_sems.at[t], device_id=(x, y, 1 - c), device_id_type=MESH)
            cp.start()
            cps.append(cp)
        for cp in cps:
            cp.wait()

    anyspec = pl.BlockSpec(memory_space=pl.ANY)
    return pl.pallas_call(
        body, name=name, out_shape=[jax.ShapeDtypeStruct(h.shape, h.dtype) for h in halves],
        in_specs=[anyspec] * (n + len(deps)), out_specs=[anyspec] * n,
        scratch_shapes=[pltpu.SemaphoreType.DMA((n,)), pltpu.SemaphoreType.DMA((n,))],
    )(*halves, *deps)


def _row_block(rows):
    for rb in (512, 256, 128, 64, 32, 16):
        if rows % rb == 0:
            return rb
    raise ValueError(rows)


def _sum_devices(name, grad, land, place):
    S, R, C = grad.shape
    rh = R // 2
    rb = _row_block(rh)
    nbh = rh // rb

    def body(place_ref, g_ref, l_ref, o_ref):
        tot = g_ref[...].astype(F32)
        for k in range(N_PEERS):
            tot = tot + l_ref[k].astype(F32)
        o_ref[...] = tot

    return pl.pallas_call(
        body, name=name,
        grid_spec=pltpu.PrefetchScalarGridSpec(
            num_scalar_prefetch=1, grid=(nbh,),
            in_specs=[pl.BlockSpec((None, rb, C), lambda r, place: (place[0], place[1] * nbh + r, 0)),
                      pl.BlockSpec((N_PEERS, rb, C), lambda r, place: (0, r, 0))],
            out_specs=pl.BlockSpec((rb, C), lambda r, place: (r, 0))),
        out_shape=jax.ShapeDtypeStruct((rh, C), F32), compiler_params=_params(1),
    )(place, grad, land)


def _adamw_math(w, g, m, v):
    m2 = ADAM_B1 * m + (1.0 - ADAM_B1) * g
    v2 = ADAM_B2 * v + (1.0 - ADAM_B2) * (g * g)
    m_hat = m2 / (1.0 - ADAM_B1 ** ADAM_STEP)
    v_hat = v2 / (1.0 - ADAM_B2 ** ADAM_STEP)
    delta = -ADAM_LR * (m_hat / (jnp.sqrt(v_hat) + ADAM_EPS) + ADAM_WD * w)
    return delta, m2, v2


def _adamw(name, w, m, v, gs):
    L, R, C = w.shape
    Rh = R // 2
    rb = _row_block(Rh)
    nbh = Rh // rb
    assert len(gs) == L

    def body(core_ref, w_ref, m_ref, v_ref, *rest):
        g_refs, (go_ref, d_ref, m2_ref, v2_ref) = rest[:2 * L], rest[2 * L:]
        layer, half = pl.program_id(0), pl.program_id(1)
        mine = half == core_ref[0]
        g = jnp.where(mine, g_refs[0][...], g_refs[1][...])
        for t in range(1, L):
            g = jnp.where(layer == t, jnp.where(mine, g_refs[2 * t][...], g_refs[2 * t + 1][...]), g)
        delta, m2, v2 = _adamw_math(w_ref[...], g, m_ref[...], v_ref[...])
        go_ref[...] = g
        d_ref[...] = delta
        m2_ref[...] = m2
        v2_ref[...] = v2

    wspec = pl.BlockSpec((None, rb, C), lambda l, h, r, core: (l, h * nbh + r, 0))
    gspec = pl.BlockSpec((rb, C), lambda l, h, r, core: (r, 0))
    return pl.pallas_call(
        body, name=name,
        grid_spec=pltpu.PrefetchScalarGridSpec(num_scalar_prefetch=1, grid=(L, 2, nbh),
                                               in_specs=[wspec] * 3 + [gspec] * (2 * L), out_specs=[wspec] * 4),
        out_shape=[jax.ShapeDtypeStruct((L, R, C), F32)] * 4, compiler_params=_params(3),
    )(lax.axis_index("c").astype(jnp.int32).reshape(1), w, m, v, *[g for pair in gs for g in pair])


def _adamw_small(ws, gs, ms, vs):
    n = len(ws)

    def body(*refs):
        w_refs, g_refs, m_refs, v_refs = (refs[k * n:(k + 1) * n] for k in range(4))
        d_refs, m2_refs, v2_refs = (refs[(4 + k) * n:(5 + k) * n] for k in range(3))
        for t in range(n):
            delta, m2, v2 = _adamw_math(w_refs[t][...], g_refs[t][...], m_refs[t][...], v_refs[t][...])
            d_refs[t][...] = delta
            m2_refs[t][...] = m2
            v2_refs[t][...] = v2

    res = pl.pallas_call(body, name="adamw_small", out_shape=[jax.ShapeDtypeStruct(w.shape, F32) for w in ws] * 3)(
        *ws, *gs, *ms, *vs)
    return res[:n], res[n:2 * n], res[2 * n:]


def _packed_rows(shape):
    c = shape[-1]
    return (int(np.prod(shape)) // c) * -(-c // LANES)


def _pack(arrays):
    total = sum(_packed_rows(a.shape) for a in arrays)
    total += -total % 8
    buf, r0 = None, 0
    for a in arrays:
        a = a.astype(F32).reshape(-1, a.shape[-1])
        r, c = a.shape
        k = -(-c // LANES)
        a = jnp.pad(a, ((0, 0), (0, k * LANES - c))).reshape(r * k, LANES)
        a = jnp.pad(a, ((r0, total - r0 - r * k), (0, 0)))
        buf = a if buf is None else buf + a
        r0 += r * k
    return buf


def _unpack(buf, shapes):
    out, r0 = [], 0
    for shp in shapes:
        c = shp[-1]
        rows = _packed_rows(shp)
        out.append(buf[r0:r0 + rows].reshape(-1, -(-c // LANES) * LANES)[:, :c].reshape(shp))
        r0 += rows
    return out


def _rms(x, g):
    return x * lax.rsqrt(jnp.mean(x * x, axis=-1, keepdims=True) + NORM_EPS) * g


def _residual_norm_ep(acc, *rest):
    *bias, res, gain = rest
    x = acc + res + (bias[0] if bias else 0.0)
    return x, _rms(x, gain)


RESIDUAL_NORM_OUTS = (("tile", F32), ("tile", BF))


def _mlp_up(tag, h, w_up_sm):
    (up,) = _mm(f"mlp{tag}_up", h, w_up_sm, nt=False, b_sm=True, tm=2048, tn=1024, rows=256,
                ep_fn=lambda acc: (acc,), outs=(("tile", BF),))
    return up


RMS_BWD_OUTS = (("tile", F32), ("tile", BF), ("colsum", F32), ("colsum", F32))


def _mlp_bwd(tag, dy, dy_bf, x, g, up, w_up_sm, w_down):
    (dup,) = _mm(f"mlp{tag}_dup", dy_bf, w_down, nt=True, tm=2048, tn=1024, rows=256, ep_in=((up, "tile"),),
                 ep_fn=lambda acc, u: (acc * (2.0 * jnp.maximum(u.astype(F32), 0.0)),), outs=(("tile", BF),))
    dx, dx_bf, dg, dx_sum = _mm(f"mlp{tag}_dx", dup, w_up_sm, nt=True, b_sm=True, tm=512, tn=1024, rows=256,
                                ep_in=((x, "tile"), (g, "row"), (dy, "tile")), ep_fn=_rms_bwd_ep, outs=RMS_BWD_OUTS)
    return dx, dx_bf, dg, dx_sum, dup


class _Reduction:
    def __init__(self, tag, grads, place):
        self.tag, self.place = tag, place
        self.sems, self.grads, self.lands, self.token = _reduce_start(f"reduce_start_{tag}", grads)

    def finish(self, after):
        grads, lands = _reduce_wait(f"reduce_wait_{self.tag}", self.grads, self.lands, self.sems, after)
        return [_sum_devices(f"reduce_sum_{self.tag}{i}", g, l, self.place) for i, (g, l) in enumerate(zip(grads, lands))]


def kernel(x, conv_norm_g, conv_w_in, conv_b_in, conv_dw, conv_dw_b, conv_ln_g, conv_ln_b, conv_w_out, conv_b_out, attn_norm_g, w_qkv, b_qkv, q_norm_g, k_norm_g, sinks, w_o, b_o, rel_bias, mlp_norm_g, w_up, w_down, loss_target, m_conv_norm_g, m_conv_w_in, m_conv_b_in, m_conv_dw, m_conv_dw_b, m_conv_ln_g, m_conv_ln_b, m_conv_w_out, m_conv_b_out, m_attn_norm_g, m_w_qkv, m_b_qkv, m_q_norm_g, m_k_norm_g, m_sinks, m_w_o, m_b_o, m_rel_bias, m_mlp_norm_g, m_w_up, m_w_down, v_conv_norm_g, v_conv_w_in, v_conv_b_in, v_conv_dw, v_conv_dw_b, v_conv_ln_g, v_conv_ln_b, v_conv_w_out, v_conv_b_out, v_attn_norm_g, v_w_qkv, v_b_qkv, v_q_norm_g, v_k_norm_g, v_sinks, v_w_o, v_b_o, v_rel_bias, v_mlp_norm_g, v_w_up, v_w_down):
    Dm = D_MODEL
    x2d = x[0]
    tgt = loss_target[0]
    T = x2d.shape[0]
    shard = 2 * lax.axis_index("x") + lax.axis_index("y")

    me = 2 * shard + lax.axis_index("c")

    def own_slot(block, slots, index):
        return lax.dynamic_update_slice(lax.empty((slots,) + block.shape, block.dtype), block[None],
                                        (index,) + (0,) * block.ndim)

    sharded_small = [conv_dw[0], attn_norm_g, b_qkv, b_o]
    (small_sems,), (small_land,), small_token = _gather_start(
        "small_weights_start", [own_slot(_pack(sharded_small), 8, me)], ((0,),), ALL_OTHERS, after=())

    big = [conv_w_in[0], conv_w_out[0], w_qkv[0], w_o[0], w_up[0], w_up[1], w_down[0], w_down[1]]
    stacks = [own_slot(w.astype(BF), N_SHARD, shard) for w in big]
    groups = ((0,), (1,), (4, 6), (2, 3), (5, 7))
    gather_sems, stacks, gather_token = _gather_start("gather_start", stacks, groups, OTHER_CHIPS, after=(small_token,))

    def gathered_group(g, name, after):
        return _gather_wait(name, [stacks[t] for t in groups[g]], gather_sems[g], OTHER_CHIPS, after)

    bucket = _bucket_table()
    bias = _bias_table(rel_bias, bucket)

    h0 = _rms_fwd("conv_norm", x2d, conv_norm_g, deps=(gather_token,))
    (w_in_sm,) = gathered_group(0, "gather_wait_conv_in", (h0, bias))
    (u,) = _mm("conv_in", h0, w_in_sm, nt=False, b_sm=True, tm=2048, tn=512, rows=256, ep_in=((conv_b_in, "row"),),
               ep_fn=lambda acc, b: (acc + b,), outs=(("tile", BF),))
    (gathered,) = _gather_wait("small_weights_wait", [small_land], small_sems, ALL_OTHERS, (u,))
    chips = [_unpack(gathered[2 * s], [a.shape for a in sharded_small]) for s in range(N_SHARD)]
    dw_f, attn_norm_f, b_qkv_f, b_o_f = (jnp.concatenate([chips[s][t] for s in range(N_SHARD)], axis=-1)
                                         for t in range(len(sharded_small)))
    dw_pad = jnp.pad(dw_f, ((0, HALO - CONV_W), (0, 0)))
    cv, s_act = _conv_fwd(u, dw_pad, conv_dw_b, conv_ln_g, conv_ln_b)
    (g_out,) = gathered_group(1, "gather_wait_conv_out", (s_act,))
    w_out_f = g_out.reshape(Dm, Dm)
    x1, h1 = _mm("conv_out", s_act, w_out_f, nt=False, tm=1024, tn=1024, rows=256,
                 ep_in=((conv_b_out, "row"), (x2d, "tile"), (mlp_norm_g[0:1], "row")), ep_fn=_residual_norm_ep,
                 outs=RESIDUAL_NORM_OUTS)

    g_up0, g_down0 = gathered_group(2, "gather_wait_mlp0", (x1,))
    w_up_sm = [g_up0, None]
    w_down_f = [g_down0.reshape(D_FF, Dm), None]
    up0 = _mlp_up(0, h1, w_up_sm[0])
    x2, h2 = _mm("mlp0_down", up0, w_down_f[0], nt=False, tm=512, tn=1024, rows=256, a_fn=_relu2,
                 ep_in=((x1, "tile"), (attn_norm_f, "row")), ep_fn=_residual_norm_ep, outs=RESIDUAL_NORM_OUTS)

    g_qkv, g_o = gathered_group(3, "gather_wait_attn", (x2,))
    w_qkv_f = jnp.transpose(g_qkv, (1, 0, 2)).reshape(Dm, QKV_DIM)
    w_o_f = g_o.reshape(ATTN_DIM, Dm)
    qg_t = jnp.tile(q_norm_g, (1, N_HEADS))
    kg_t = jnp.tile(k_norm_g, (1, N_KV))

    def qkv_ep(acc, b, qg, kg, sel_q, sel_q_t, sel_k, sel_k_t):
        proj = acc + b
        q, k, v = proj[:, :ATTN_DIM], proj[:, ATTN_DIM:ATTN_DIM + KV_DIM], proj[:, ATTN_DIM + KV_DIM:]
        return proj, _qk_normed(q, qg, (sel_q, sel_q_t), 1.0 / math.sqrt(HEAD_DIM)), _qk_normed(k, kg, (sel_k, sel_k_t), 1.0), v

    qkv, qn, kn, vv = _mm(
        "attn_qkv", h2, w_qkv_f, nt=False, tm=1024, tn=QKV_DIM, rows=256, ep_fn=qkv_ep,
        ep_in=((b_qkv_f, "row"), (qg_t, "whole"), (kg_t, "whole"))
        + tuple((m, "whole") for m in _head_select(ATTN_DIM) + _head_select(KV_DIM)),
        outs=(("tile", F32), ("tile", BF, ATTN_DIM), ("tile", BF, KV_DIM), ("tile", BF, KV_DIM)))
    sinks1 = sinks[0]
    att = _attn_fwd(qn, kn, vv, bias, sinks1)
    x3, h3 = _mm("attn_out", att, w_o_f, nt=False, tm=1024, tn=1024, rows=256,
                 ep_in=((b_o_f, "row"), (x2, "tile"), (mlp_norm_g[1:2], "row")), ep_fn=_residual_norm_ep,
                 outs=RESIDUAL_NORM_OUTS)

    g_up1, g_down1 = gathered_group(4, "gather_wait_mlp1", (x3,))
    w_up_sm[1] = g_up1
    w_down_f[1] = g_down1.reshape(D_FF, Dm)
    up1 = _mlp_up(1, h3, w_up_sm[1])

    def loss_ep(acc, r, t):
        diff = acc + r - t
        dy = diff * (1.0 / Dm)
        return dy, dy, jnp.sum(diff * diff, axis=0, keepdims=True)

    dy, dy_bf, sq = _mm("mlp1_down_loss", up1, w_down_f[1], nt=False, tm=512, tn=1024, rows=256, a_fn=_relu2,
                        ep_in=((x3, "tile"), (tgt, "tile")), ep_fn=loss_ep,
                        outs=(("tile", F32), ("tile", BF), ("colsum", F32)))

    place = jnp.stack([shard, lax.axis_index("c")]).astype(jnp.int32)
    dx3, dx3_bf, dg_mlp1, db_o, dup1 = _mlp_bwd(1, dy, dy_bf, x3, mlp_norm_g[1:2], up1, w_up_sm[1], w_down_f[1])
    dw_down1 = _mm_tn("mlp1_dw_down", up1, dy_bf, tm=1024, tn=1024, tk=2048, a_fn=_relu2)
    dw_up1 = _mm_tn("mlp1_dw_up", h3, dup1, tm=1024, tn=1024, tk=2048, out_sm=N_SHARD)
    red_mlp1 = _Reduction("mlp1", [dw_up1, dw_down1.reshape(N_SHARD, D_FF // N_SHARD, Dm)], place)

    ident = lambda acc: (acc,)
    (datt,) = _mm("attn_dout", dx3_bf, w_o_f, nt=True, tm=1024, tn=1024, rows=256, ep_fn=ident, outs=(("tile", BF),),
                  deps=(red_mlp1.token,))
    dw_o = _mm_tn("attn_dw_o", att, dx3_bf, tm=1024, tn=1024, tk=2048)
    dqn, dkn, dvv, dbias, dsinks = _attn_bwd(qn, kn, vv, bias, sinks1, datt)
    drel = _bias_grad(dbias, bucket)
    dqkv, db_qkv, dqg_t, dkg_t = _qk_norm_bwd(qkv, dqn, dkn, dvv, qg_t, kg_t)
    dw_qkv = _mm_tn("attn_dw_qkv", h2, dqkv, tm=1024, tn=QKV_DIM, tk=2048)
    red_attn = _Reduction("attn", [jnp.transpose(dw_qkv.reshape(Dm, N_SHARD, QKV_DIM // N_SHARD), (1, 0, 2)),
                                   dw_o.reshape(N_SHARD, ATTN_DIM // N_SHARD, Dm)], place)
    dx2, dx2_bf, dg_attn, _ = _mm("attn_dx", dqkv, w_qkv_f, nt=True, tm=1024, tn=1024, rows=256,
                                  ep_in=((x2, "tile"), (attn_norm_f, "row"), (dx3, "tile")), ep_fn=_rms_bwd_ep,
                                  outs=RMS_BWD_OUTS, deps=(red_attn.token,))

    dx1, dx1_bf, dg_mlp0, db_out, dup0 = _mlp_bwd(0, dx2, dx2_bf, x1, mlp_norm_g[0:1], up0, w_up_sm[0], w_down_f[0])
    dw_down0 = _mm_tn("mlp0_dw_down", up0, dx2_bf, tm=1024, tn=1024, tk=2048, a_fn=_relu2)
    dw_up0 = _mm_tn("mlp0_dw_up", h1, dup0, tm=1024, tn=1024, tk=2048, out_sm=N_SHARD)
    dw_out = _mm_tn("conv_dw_out", s_act, dx1_bf, tm=1024, tn=1024, tk=2048)
    red_mlp0 = _Reduction("mlp0", [dw_up0, dw_down0.reshape(N_SHARD, D_FF // N_SHARD, Dm),
                                   dw_out.reshape(N_SHARD, Dm // N_SHARD, Dm)], place)
    (r_qkv, r_o) = red_attn.finish((dx1,))
    (r_up1, r_down1) = red_mlp1.finish((dx1,))

    dcv, dln_g, dln_b, ddw_b = _mm("conv_ds", dx1_bf, w_out_f, nt=True, tm=1024, tn=1024, rows=256,
                                   ep_in=((cv, "tile"), (conv_ln_g, "row"), (conv_ln_b, "row")),
                                   ep_fn=_ln_silu_bwd_ep,
                                   outs=(("tile", F32), ("colsum", F32), ("colsum", F32), ("colsum", F32)),
                                   deps=(red_mlp0.token,))
    du, db_in, ddw8 = _conv_bwd(u, dcv, dw_pad)
    (r_up0, r_down0, r_out) = red_mlp0.finish((du,))
    dw_in = _mm_tn("conv_dw_in", h0, du, tm=1024, tn=512, tk=4096, out_sm=N_SHARD)
    red_conv = _Reduction("conv", [dw_in], place)
    def first_layer_ep(*args):
        tot, _, dg, _ = _rms_bwd_ep(*args)
        return tot, dg

    gx, dg_conv = _mm("conv_dx", du, w_in_sm, nt=True, b_sm=True, tm=1024, tn=1024, rows=256,
                      ep_in=((x2d, "tile"), (conv_norm_g, "row"), (dx1, "tile")), ep_fn=first_layer_ep,
                      outs=(("tile", F32), ("colsum", F32)), deps=(red_conv.token,))
    (r_in,) = red_conv.finish((gx,))

    dqg = dqg_t.reshape(N_HEADS, HEAD_DIM).sum(axis=0, keepdims=True)
    dkg = dkg_t.reshape(N_KV, HEAD_DIM).sum(axis=0, keepdims=True)
    small_full = [dg_conv, db_in, ddw8.sum(axis=1)[:CONV_W], ddw_b, dln_g, dln_b, db_out, dg_attn, db_qkv, dqg, dkg,
                  dsinks[None, :], db_o, drel.reshape(1, REL_BUCKETS * N_HEADS),
                  jnp.pad(dg_mlp0, ((0, 1), (0, 0))) + jnp.pad(dg_mlp1, ((1, 0), (0, 0))), sq]
    (sg_sems,), (sg_land,), sg_token = _gather_start(
        "small_grads_start", [own_slot(_pack(small_full), 8, me)], ((0,),), ALL_OTHERS, after=())

    mine = [r_in, r_out, r_qkv, r_o, r_up0, r_up1, r_down0, r_down1]
    r_in, r_out, r_qkv, r_o, r_up0, r_up1, r_down0, r_down1 = zip(
        mine, _join_halves("join_halves", mine, deps=(sg_token,)))

    big_out = {}
    for nm, w, m, v, gs in (("conv_w_in", conv_w_in, m_conv_w_in, v_conv_w_in, (r_in,)),
                            ("conv_w_out", conv_w_out, m_conv_w_out, v_conv_w_out, (r_out,)),
                            ("w_qkv", w_qkv, m_w_qkv, v_w_qkv, (r_qkv,)),
                            ("w_o", w_o, m_w_o, v_w_o, (r_o,)),
                            ("w_up", w_up, m_w_up, v_w_up, (r_up0, r_up1)),
                            ("w_down", w_down, m_w_down, v_w_down, (r_down0, r_down1))):
        big_out[nm] = _adamw(f"adamw_{nm}", w, m, v, gs)

    (sg_land,) = _gather_wait("small_grads_wait", [sg_land], sg_sems, ALL_OTHERS,
                              [big_out[nm][0] for nm in big_out])
    small_sum = _sum8("small_grads_sum", sg_land)
    (r_norm, r_b_in, r_dw, r_dw_b, r_ln_g, r_ln_b, r_b_out, r_attn_norm, r_b_qkv, r_qg, r_kg, r_sinks, r_b_o, r_rel,
     r_mlp_norm, r_sq) = _unpack(small_sum, [a.shape for a in small_full])
    loss = 0.5 * jnp.sum(r_sq) * (1.0 / Dm)

    def cols(a, width):
        return lax.dynamic_slice_in_dim(a, shard * width, width, axis=a.ndim - 1)

    small_names = ["conv_norm_g", "conv_b_in", "conv_dw", "conv_dw_b", "conv_ln_g", "conv_ln_b", "conv_b_out",
                   "attn_norm_g", "b_qkv", "q_norm_g", "k_norm_g", "sinks", "b_o", "rel_bias", "mlp_norm_g"]
    small_g = [r_norm, r_b_in, cols(r_dw, Dm // N_SHARD)[None], r_dw_b, r_ln_g, r_ln_b, r_b_out,
               cols(r_attn_norm, Dm // N_SHARD), cols(r_b_qkv, QKV_DIM // N_SHARD), r_qg, r_kg, r_sinks,
               cols(r_b_o, Dm // N_SHARD), r_rel.reshape(REL_BUCKETS, N_HEADS), r_mlp_norm]
    small_w = [conv_norm_g, conv_b_in, conv_dw, conv_dw_b, conv_ln_g, conv_ln_b, conv_b_out, attn_norm_g, b_qkv,
               q_norm_g, k_norm_g, sinks, b_o, rel_bias, mlp_norm_g]
    small_m = [m_conv_norm_g, m_conv_b_in, m_conv_dw, m_conv_dw_b, m_conv_ln_g, m_conv_ln_b, m_conv_b_out,
               m_attn_norm_g, m_b_qkv, m_q_norm_g, m_k_norm_g, m_sinks, m_b_o, m_rel_bias, m_mlp_norm_g]
    small_v = [v_conv_norm_g, v_conv_b_in, v_conv_dw, v_conv_dw_b, v_conv_ln_g, v_conv_ln_b, v_conv_b_out,
               v_attn_norm_g, v_b_qkv, v_q_norm_g, v_k_norm_g, v_sinks, v_b_o, v_rel_bias, v_mlp_norm_g]
    flat2 = lambda a: a.reshape(-1, a.shape[-1])
    small_g = [flat2(g) for g in small_g]
    d_s, m_s, v_s = _adamw_small([flat2(w) for w in small_w], small_g, [flat2(m) for m in small_m],
                                 [flat2(v) for v in small_v])
    small_out = {}
    for nm, w, g, d, m2, v2 in zip(small_names, small_w, small_g, d_s, m_s, v_s):
        small_out[nm] = tuple(a.reshape(w.shape) for a in (g, d, m2, v2))

    order = ["conv_norm_g", "conv_w_in", "conv_b_in", "conv_dw", "conv_dw_b", "conv_ln_g", "conv_ln_b", "conv_w_out",
             "conv_b_out", "attn_norm_g", "w_qkv", "b_qkv", "q_norm_g", "k_norm_g", "sinks", "w_o", "b_o", "rel_bias",
             "mlp_norm_g", "w_up", "w_down"]
    res = {**small_out, **big_out}
    outs = [loss, gx[None]]
    for part in range(4):
        outs += [res[nm][part] for nm in order]
    return tuple(outs)
```

```python
import math

import numpy as np
import jax
import jax.numpy as jnp
from jax import lax
from jax.experimental import pallas as pl
from jax.experimental.pallas import tpu as pltpu

F32 = jnp.float32
BF = jnp.bfloat16
MESH = pl.DeviceIdType.MESH

D_MODEL = 1024
D_FF = 4096
N_HEADS = 16
N_KV = 2
GROUP = N_HEADS // N_KV
HEAD_DIM = 64
ATTN_DIM = N_HEADS * HEAD_DIM
KV_DIM = N_KV * HEAD_DIM
QKV_DIM = ATTN_DIM + 2 * KV_DIM
BLOCK = 128
CONV_W = 31
HALO = 32
REL_BUCKETS = 32
REL_MAX_DIST = 128
NORM_EPS = 1e-6
NEG_INF = -1e30
N_SHARD = 4
LANES = 1024

ADAM_LR = 0.001
ADAM_B1 = 0.9
ADAM_B2 = 0.999
ADAM_EPS = 1e-08
ADAM_WD = 0.01
ADAM_STEP = 10

VMEM_LIMIT = 56 * 1024 * 1024


def _params(n_axes):
    return pltpu.CompilerParams(dimension_semantics=("arbitrary",) * n_axes, vmem_limit_bytes=VMEM_LIMIT)


def _dot(a, b, ca, cb):
    return lax.dot_general(a, b, (((ca,), (cb,)), ((), ())), preferred_element_type=F32)


def _mm(name, a, b, *, nt, tm, tn, ep_fn, outs, a_fn=None, b_sm=False, ep_in=(), deps=(), rows=None):
    M, K = a.shape
    rows = tm if rows is None else rows
    if b_sm:
        S, ks = b.shape[0], b.shape[2]
        N, per = (b.shape[1], None) if nt else (S * b.shape[2], b.shape[2] // tn)
        assert (S * ks == K) if nt else (b.shape[1] == K)
    else:
        N = b.shape[0] if nt else b.shape[1]
        assert (b.shape[1] if nt else b.shape[0]) == K
    assert M % tm == 0 and N % tn == 0 and tm % rows == 0
    ne, no, nd = len(ep_in), len(outs), len(deps)

    def body(a_ref, b_ref, *rest):
        ep_refs, out_refs = rest[:ne], rest[ne + nd:ne + nd + no]
        i = pl.program_id(1)
        sums = [None] * no
        for r in range(tm // rows):
            rs = pl.ds(r * rows, rows)

            def lhs(cols):
                av = a_ref[rs, cols]
                return (av if a_fn is None else a_fn(av)).astype(BF)

            if b_sm and nt:
                acc = None
                for s in range(S):
                    part = _dot(lhs(pl.ds(s * ks, ks)), b_ref[s].astype(BF), 1, 1)
                    acc = part if acc is None else acc + part
            else:
                acc = _dot(lhs(slice(None)), b_ref[...].astype(BF), 1, 1 if nt else 0)
            ep_vals = [ref[rs, :] if kind == "tile" else ref[...] for ref, (_, kind) in zip(ep_refs, ep_in)]
            vals = ep_fn(acc, *ep_vals)
            for o, ((kind, dt, *_), ref, val) in enumerate(zip(outs, out_refs, vals)):
                if kind == "tile":
                    ref[rs, :] = val.astype(dt)
                else:
                    sums[o] = val if sums[o] is None else sums[o] + val
        for (kind, *_), ref, val in zip(outs, out_refs, sums):
            if kind == "colsum":
                @pl.when(i == 0)
                def _():
                    ref[...] = val

                @pl.when(i > 0)
                def _():
                    ref[...] += val

    if b_sm and nt:
        b_spec = pl.BlockSpec((S, tn, ks), lambda j, i: (0, j, 0))
    elif b_sm:
        b_spec = pl.BlockSpec((None, K, tn), lambda j, i: (j // per, 0, j % per))
    elif nt:
        b_spec = pl.BlockSpec((tn, K), lambda j, i: (j, 0))
    else:
        b_spec = pl.BlockSpec((K, tn), lambda j, i: (0, j))
    in_specs = [pl.BlockSpec((tm, K), lambda j, i: (i, 0)), b_spec]
    for arr, kind in ep_in:
        if kind == "tile":
            assert arr.shape == (M, N)
            in_specs.append(pl.BlockSpec((tm, tn), lambda j, i: (i, j)))
        elif kind == "whole":
            in_specs.append(pl.BlockSpec(arr.shape, lambda j, i, rank=arr.ndim: (0,) * rank))
        else:
            assert arr.shape == (1, N)
            in_specs.append(pl.BlockSpec((1, tn), lambda j, i: (0, j)))
    in_specs += [pl.BlockSpec(memory_space=pl.ANY)] * nd
    out_shape, out_specs = [], []
    for kind, dt, *width in outs:
        if kind == "tile" and width:
            assert tn == N
            out_shape.append(jax.ShapeDtypeStruct((M, width[0]), dt))
            out_specs.append(pl.BlockSpec((tm, width[0]), lambda j, i: (i, 0)))
        elif kind == "tile":
            out_shape.append(jax.ShapeDtypeStruct((M, N), dt))
            out_specs.append(pl.BlockSpec((tm, tn), lambda j, i: (i, j)))
        else:
            out_shape.append(jax.ShapeDtypeStruct((1, N), F32))
            out_specs.append(pl.BlockSpec((1, tn), lambda j, i: (0, j)))
    return pl.pallas_call(
        body, name=name, grid=(N // tn, M // tm), in_specs=in_specs, out_specs=out_specs, out_shape=out_shape,
        compiler_params=_params(2),
    )(a, b, *[arr for arr, _ in ep_in], *deps)


def _mm_tn(name, a, b, *, tm, tn, tk, a_fn=None, out_sm=None):
    T, Ka = a.shape
    N = b.shape[1]
    assert b.shape[0] == T and T % tk == 0 and Ka % tm == 0 and N % tn == 0
    nk = T // tk

    def body(a_ref, b_ref, o_ref, acc_ref):
        k = pl.program_id(2)

        @pl.when(k == 0)
        def _():
            acc_ref[...] = jnp.zeros_like(acc_ref)

        av = a_ref[...]
        if a_fn is not None:
            av = a_fn(av)
        acc_ref[...] += _dot(av.astype(BF), b_ref[...].astype(BF), 0, 0)

        @pl.when(k == nk - 1)
        def _():
            o_ref[...] = acc_ref[...].astype(BF)

    if out_sm is None:
        out_shape = jax.ShapeDtypeStruct((Ka, N), BF)
        out_spec = pl.BlockSpec((tm, tn), lambda i, j, k: (i, j))
    else:
        per = (N // out_sm) // tn
        assert per * tn * out_sm == N
        out_shape = jax.ShapeDtypeStruct((out_sm, Ka, N // out_sm), BF)
        out_spec = pl.BlockSpec((None, tm, tn), lambda i, j, k: (j // per, i, j % per))
    return pl.pallas_call(
        body, name=name, grid=(Ka // tm, N // tn, nk),
        in_specs=[pl.BlockSpec((tk, tm), lambda i, j, k: (k, i)), pl.BlockSpec((tk, tn), lambda i, j, k: (k, j))],
        out_specs=out_spec, out_shape=out_shape, scratch_shapes=[pltpu.VMEM((tm, tn), F32)],
        compiler_params=_params(3),
    )(a, b)


def _relu2(v):
    r = jnp.maximum(v.astype(F32), 0.0)
    return r * r


def _rms_bwd_ep(dh, x, g, dres):
    rstd = lax.rsqrt(jnp.mean(x * x, axis=-1, keepdims=True) + NORM_EPS)
    xh = x * rstd
    dxh = dh * g
    dx = rstd * (dxh - xh * jnp.mean(dxh * xh, axis=-1, keepdims=True))
    tot = dres + dx
    return tot, tot, jnp.sum(dh * xh, axis=0, keepdims=True), jnp.sum(tot, axis=0, keepdims=True)


def _rms_fwd(name, x, g, tm=512, deps=()):
    T, Dm = x.shape

    def body(x_ref, g_ref, *rest):
        o_ref = rest[-1]
        xv = x_ref[...]
        rstd = lax.rsqrt(jnp.mean(xv * xv, axis=-1, keepdims=True) + NORM_EPS)
        o_ref[...] = (xv * rstd * g_ref[...]).astype(BF)

    return pl.pallas_call(
        body, name=name, grid=(T // tm,),
        in_specs=[pl.BlockSpec((tm, Dm), lambda i: (i, 0)), pl.BlockSpec((1, Dm), lambda i: (0, 0))]
        + [pl.BlockSpec(memory_space=pl.ANY)] * len(deps),
        out_specs=pl.BlockSpec((tm, Dm), lambda i: (i, 0)), out_shape=jax.ShapeDtypeStruct((T, Dm), BF),
        compiler_params=_params(1),
    )(x, g, *deps)


HEAD_COLS = 128


def _two_term_dot(v, m):
    hi = v.astype(BF)
    lo = (v - hi.astype(F32)).astype(BF)
    return _dot(hi, m, 1, 0) + _dot(lo, m, 1, 0)


def _head_sum(v, select):
    sel, sel_t = select
    return _two_term_dot(_two_term_dot(v, sel), sel_t)


def _head_select(n):
    sel = (np.arange(n)[:, None] // HEAD_DIM == np.arange(HEAD_COLS)[None, :]).astype(np.float32)
    return jnp.asarray(sel, dtype=BF), jnp.asarray(sel.T, dtype=BF)


def _qk_normed(x, g, select, scale):
    r = lax.rsqrt(_head_sum(x * x, select) * (1.0 / HEAD_DIM) + NORM_EPS)
    return x * r * g * scale


def _qk_norm_bwd(qkv, dqn, dkn, dv, qg_t, kg_t, tm=256):
    T = qkv.shape[0]

    def body(x_ref, dq_ref, dk_ref, dv_ref, qg_ref, kg_ref, sq_ref, sqt_ref, sk_ref, skt_ref,
             o_ref, db_ref, dqg_ref, dkg_ref):
        i = pl.program_id(0)

        def one(x, dy, g, select):
            r = lax.rsqrt(_head_sum(x * x, select) * (1.0 / HEAD_DIM) + NORM_EPS)
            xh = x * r
            dxh = dy * g
            dx = r * (dxh - xh * (_head_sum(dxh * xh, select) * (1.0 / HEAD_DIM)))
            return dx, jnp.sum(dy * xh, axis=0, keepdims=True)

        dq, dqg = one(x_ref[:, pl.ds(0, ATTN_DIM)], dq_ref[...], qg_ref[...], (sq_ref[...], sqt_ref[...]))
        dk, dkg = one(x_ref[:, pl.ds(ATTN_DIM, KV_DIM)], dk_ref[...], kg_ref[...], (sk_ref[...], skt_ref[...]))
        dvv = dv_ref[...]
        o_ref[:, pl.ds(0, ATTN_DIM)] = dq.astype(BF)
        o_ref[:, pl.ds(ATTN_DIM, KV_DIM)] = dk.astype(BF)
        o_ref[:, pl.ds(ATTN_DIM + KV_DIM, KV_DIM)] = dvv.astype(BF)
        sq, sk, sv = (jnp.sum(t, axis=0, keepdims=True) for t in (dq, dk, dvv))

        @pl.when(i == 0)
        def _():
            db_ref[:, pl.ds(0, ATTN_DIM)] = sq
            db_ref[:, pl.ds(ATTN_DIM, KV_DIM)] = sk
            db_ref[:, pl.ds(ATTN_DIM + KV_DIM, KV_DIM)] = sv
            dqg_ref[...] = dqg
            dkg_ref[...] = dkg

        @pl.when(i > 0)
        def _():
            db_ref[:, pl.ds(0, ATTN_DIM)] += sq
            db_ref[:, pl.ds(ATTN_DIM, KV_DIM)] += sk
            db_ref[:, pl.ds(ATTN_DIM + KV_DIM, KV_DIM)] += sv
            dqg_ref[...] += dqg
            dkg_ref[...] += dkg

    full = lambda shape: pl.BlockSpec(shape, lambda i: (0, 0))
    row = lambda n: pl.BlockSpec((tm, n), lambda i: (i, 0))
    return pl.pallas_call(
        body, name="qk_norm_bwd", grid=(T // tm,),
        in_specs=[row(QKV_DIM), row(ATTN_DIM), row(KV_DIM), row(KV_DIM), full((1, ATTN_DIM)), full((1, KV_DIM)),
                  full((ATTN_DIM, HEAD_COLS)), full((HEAD_COLS, ATTN_DIM)), full((KV_DIM, HEAD_COLS)), full((HEAD_COLS, KV_DIM))],
        out_specs=[row(QKV_DIM), full((1, QKV_DIM)), full((1, ATTN_DIM)), full((1, KV_DIM))],
        out_shape=[jax.ShapeDtypeStruct((T, QKV_DIM), BF), jax.ShapeDtypeStruct((1, QKV_DIM), F32),
                   jax.ShapeDtypeStruct((1, ATTN_DIM), F32), jax.ShapeDtypeStruct((1, KV_DIM), F32)],
        compiler_params=_params(1),
    )(qkv, dqn, dkn, dv, qg_t, kg_t, *_head_select(ATTN_DIM), *_head_select(KV_DIM))


ROWS = 128
COLS = 128


SUBLANES = 8
FIRST_TAP = HALO - (CONV_W - 1)


def _glu(a, g):
    return a.astype(F32) * jax.nn.sigmoid(g.astype(F32))


def _shifted(xe, s):
    return xe if s == 0 else pltpu.roll(xe, ROWS + HALO - s, axis=0)


def _conv_fwd(u, dw_pad, dw_b, ln_g, ln_b, tm=512):
    T = u.shape[0]
    Dm = D_MODEL
    hpt = tm // HALO

    def body(ac_ref, gc_ref, ap_ref, gp_ref, w_ref, wb_ref, lg_ref, lb_ref, cv_ref, s_ref, ext):
        i = pl.program_id(0)
        ext[pl.ds(0, HALO), :] = jnp.where(i > 0, _glu(ap_ref[...], gp_ref[...]), 0.0)
        ext[pl.ds(HALO, tm), :] = _glu(ac_ref[...], gc_ref[...])

        def rows(r, carry):
            r0 = pl.multiple_of(r * ROWS, ROWS)
            for c in range(Dm // COLS):
                cs = pl.ds(c * COLS, COLS)
                xe = ext[pl.ds(r0, ROWS + HALO), cs]
                acc = jnp.zeros((ROWS, COLS), F32)
                for s in range(SUBLANES):
                    xs = _shifted(xe, s)
                    for j in range(CONV_W):
                        off = FIRST_TAP + j
                        if off % SUBLANES == s:
                            acc = acc + xs[off - s:off - s + ROWS, :] * w_ref[pl.ds(j, 1), cs]
                cv_ref[pl.ds(r0, ROWS), cs] = acc + wb_ref[:, cs]
            return carry

        lax.fori_loop(0, tm // ROWS, rows, 0)
        cv = cv_ref[...]
        xc = cv - jnp.mean(cv, axis=-1, keepdims=True)
        y = xc * lax.rsqrt(jnp.mean(xc * xc, axis=-1, keepdims=True) + NORM_EPS) * lg_ref[...] + lb_ref[...]
        s_ref[...] = (y * jax.nn.sigmoid(y)).astype(BF)

    full = lambda shape: pl.BlockSpec(shape, lambda i: (0, 0))
    return pl.pallas_call(
        body, name="conv_fwd", grid=(T // tm,),
        in_specs=[pl.BlockSpec((tm, Dm), lambda i: (i, 0)), pl.BlockSpec((tm, Dm), lambda i: (i, 1)),
                  pl.BlockSpec((HALO, Dm), lambda i: (jnp.maximum(i * hpt - 1, 0), 0)),
                  pl.BlockSpec((HALO, Dm), lambda i: (jnp.maximum(i * hpt - 1, 0), 1)),
                  full((HALO, Dm)), full((1, Dm)), full((1, Dm)), full((1, Dm))],
        out_specs=[pl.BlockSpec((tm, Dm), lambda i: (i, 0)), pl.BlockSpec((tm, Dm), lambda i: (i, 0))],
        out_shape=[jax.ShapeDtypeStruct((T, Dm), F32), jax.ShapeDtypeStruct((T, Dm), BF)],
        scratch_shapes=[pltpu.VMEM((tm + HALO, Dm), F32)],
        compiler_params=_params(1),
    )(u, u, u, u, dw_pad, dw_b, ln_g, ln_b)


def _ln_silu_bwd_ep(ds, cv, lg, lb):
    xc = cv - jnp.mean(cv, axis=-1, keepdims=True)
    rstd = lax.rsqrt(jnp.mean(xc * xc, axis=-1, keepdims=True) + NORM_EPS)
    xh = xc * rstd
    y = xh * lg + lb
    sg = jax.nn.sigmoid(y)
    dy = ds * (sg * (1.0 + y * (1.0 - sg)))
    dxh = dy * lg
    dcv = rstd * (dxh - jnp.mean(dxh, axis=-1, keepdims=True) - xh * jnp.mean(dxh * xh, axis=-1, keepdims=True))
    return (dcv, jnp.sum(dy * xh, axis=0, keepdims=True), jnp.sum(dy, axis=0, keepdims=True),
            jnp.sum(dcv, axis=0, keepdims=True))


def _conv_bwd(u, dcv, dw_pad, tm=512):
    T = u.shape[0]
    Dm = D_MODEL
    hpt = tm // HALO
    last = T // HALO - 1
    nt = T // tm

    def body(ac_ref, gc_ref, ap_ref, gp_ref, dc_ref, dn_ref, w_ref, du_ref, db_ref, dw_ref, ext_g, ext_d):
        i = pl.program_id(0)
        ext_g[pl.ds(0, HALO), :] = jnp.where(i > 0, _glu(ap_ref[...], gp_ref[...]), 0.0)
        ext_g[pl.ds(HALO, tm), :] = _glu(ac_ref[...], gc_ref[...])
        ext_d[pl.ds(0, tm), :] = dc_ref[...]
        ext_d[pl.ds(tm, HALO), :] = jnp.where(i < nt - 1, dn_ref[...], 0.0)

        @pl.when(i == 0)
        def _():
            db_ref[...] = jnp.zeros_like(db_ref)
            dw_ref[...] = jnp.zeros_like(dw_ref)

        def rows(r, carry):
            r0 = pl.multiple_of(r * ROWS, ROWS)
            rs = pl.ds(r0, ROWS)
            for c in range(Dm // COLS):
                cs = pl.ds(c * COLS, COLS)
                cs2 = pl.ds(Dm + c * COLS, COLS)
                de = ext_d[pl.ds(r0, ROWS + HALO), cs]
                ge = ext_g[pl.ds(r0, ROWS + HALO), cs]
                dcur = de[0:ROWS, :]
                acc = jnp.zeros((ROWS, COLS), F32)
                for s in range(SUBLANES):
                    ds_, gs_ = _shifted(de, s), _shifted(ge, s)
                    for j in range(CONV_W):
                        off = CONV_W - 1 - j
                        if off % SUBLANES == s:
                            acc = acc + ds_[off - s:off - s + ROWS, :] * w_ref[pl.ds(j, 1), cs]
                        goff = FIRST_TAP + j
                        if goff % SUBLANES == s:
                            prod = dcur * gs_[goff - s:goff - s + ROWS, :]
                            dw_ref[j, :, cs] += jnp.sum(prod.reshape(ROWS // SUBLANES, SUBLANES, COLS), axis=0)
                a = ac_ref[rs, cs].astype(F32)
                sg = jax.nn.sigmoid(gc_ref[rs, cs].astype(F32))
                da = acc * sg
                dg = acc * a * sg * (1.0 - sg)
                du_ref[rs, cs] = da.astype(BF)
                du_ref[rs, cs2] = dg.astype(BF)
                db_ref[:, cs] += jnp.sum(da, axis=0, keepdims=True)
                db_ref[:, cs2] += jnp.sum(dg, axis=0, keepdims=True)
            return carry

        lax.fori_loop(0, tm // ROWS, rows, 0)

    return pl.pallas_call(
        body, name="conv_bwd", grid=(nt,),
        in_specs=[pl.BlockSpec((tm, Dm), lambda i: (i, 0)), pl.BlockSpec((tm, Dm), lambda i: (i, 1)),
                  pl.BlockSpec((HALO, Dm), lambda i: (jnp.maximum(i * hpt - 1, 0), 0)),
                  pl.BlockSpec((HALO, Dm), lambda i: (jnp.maximum(i * hpt - 1, 0), 1)),
                  pl.BlockSpec((tm, Dm), lambda i: (i, 0)),
                  pl.BlockSpec((HALO, Dm), lambda i: (jnp.minimum((i + 1) * hpt, last), 0)),
                  pl.BlockSpec((HALO, Dm), lambda i: (0, 0))],
        out_specs=[pl.BlockSpec((tm, 2 * Dm), lambda i: (i, 0)), pl.BlockSpec((1, 2 * Dm), lambda i: (0, 0)),
                   pl.BlockSpec((HALO, 8, Dm), lambda i: (0, 0, 0))],
        out_shape=[jax.ShapeDtypeStruct((T, 2 * Dm), BF), jax.ShapeDtypeStruct((1, 2 * Dm), F32),
                   jax.ShapeDtypeStruct((HALO, 8, Dm), F32)],
        scratch_shapes=[pltpu.VMEM((tm + HALO, Dm), F32), pltpu.VMEM((tm + HALO, Dm), F32)],
        compiler_params=_params(1),
    )(u, u, u, u, dcv, dcv, dw_pad)


def _bucket_table():
    q_loc = np.arange(BLOCK)[:, None]
    k_loc = np.arange(2 * BLOCK)[None, :]
    dist = q_loc + BLOCK - k_loc
    n = np.maximum(dist, 0)
    max_exact = REL_BUCKETS // 2
    large = max_exact + (np.log(np.maximum(n, 1).astype(np.float32) / max_exact)
                         / math.log(REL_MAX_DIST / max_exact) * (REL_BUCKETS - max_exact)).astype(np.int32)
    large = np.minimum(large, REL_BUCKETS - 1)
    bucket = np.where(n < max_exact, n, large).astype(np.int32)
    return jnp.asarray(np.where((dist >= 0) & (dist < BLOCK), bucket, -1).astype(np.int32))


def _bias_table(rel_bias, bucket):
    def body(rb_ref, bk_ref, o_ref):
        bk = bk_ref[...]
        for h in range(N_HEADS):
            acc = jnp.full((BLOCK, 2 * BLOCK), NEG_INF, F32)
            for b in range(REL_BUCKETS):
                acc = jnp.where(bk == b, rb_ref[b, h], acc)
            o_ref[h] = acc

    return pl.pallas_call(
        body, name="bias_table", out_shape=jax.ShapeDtypeStruct((N_HEADS, BLOCK, 2 * BLOCK), F32),
        in_specs=[pl.BlockSpec(memory_space=pltpu.SMEM), pl.BlockSpec(memory_space=pltpu.VMEM)],
        out_specs=pl.BlockSpec(memory_space=pltpu.VMEM),
    )(rel_bias, bucket)


def _bias_grad(dbias, bucket):
    def body(db_ref, bk_ref, o_ref):
        bk = bk_ref[...]
        for b in range(REL_BUCKETS):
            sel = bk == b
            for h in range(N_HEADS):
                o_ref[b, h] = jnp.sum(jnp.where(sel, db_ref[h], 0.0))

    return pl.pallas_call(
        body, name="bias_grad", out_shape=jax.ShapeDtypeStruct((REL_BUCKETS, N_HEADS), F32),
        in_specs=[pl.BlockSpec(memory_space=pltpu.VMEM), pl.BlockSpec(memory_space=pltpu.VMEM)],
        out_specs=pl.BlockSpec(memory_space=pltpu.SMEM),
    )(dbias, bucket)


GROUP_ROWS = GROUP * BLOCK


def _head_probs(qk, bias_h, sink, first):
    s = jnp.where(first, NEG_INF, qk + bias_h)
    m = jnp.maximum(jnp.max(s, axis=-1, keepdims=True), sink)
    p = jnp.exp(s - m)
    ps = jnp.exp(sink - m)
    inv = 1.0 / (jnp.sum(p, axis=-1, keepdims=True) + ps)
    return p * inv, ps * inv


def _band(prev_ref, cur_ref, g):
    hs = pl.ds(g * HEAD_DIM, HEAD_DIM)
    return jnp.concatenate([prev_ref[:, hs], cur_ref[:, hs]], axis=0)


def _stack_heads(ref, g):
    return jnp.concatenate([ref[:, pl.ds((g * GROUP + hh) * HEAD_DIM, HEAD_DIM)] for hh in range(GROUP)], axis=0)


def _unstack_heads(ref, g, stacked, dtype):
    for hh in range(GROUP):
        ref[:, pl.ds((g * GROUP + hh) * HEAD_DIM, HEAD_DIM)] = stacked[hh * BLOCK:(hh + 1) * BLOCK, :].astype(dtype)


def _first_mask(n):
    col = lax.broadcasted_iota(jnp.int32, (1, 2 * BLOCK), 1)
    return jnp.logical_and(n == 0, col < BLOCK)


def _head_rows(hh):
    return pl.ds(hh * BLOCK, BLOCK)


def _attn_fwd(qn, kn, vv, bias, sinks):
    T = qn.shape[0]
    nb = T // BLOCK

    def body(sk_ref, q_ref, kc_ref, kp_ref, vc_ref, vp_ref, b_ref, o_ref, qk_buf, p_buf):
        first = _first_mask(pl.program_id(0))
        for g in range(N_KV):
            qk_buf[g] = _dot(_stack_heads(q_ref, g), _band(kp_ref, kc_ref, g), 1, 1)
        for g in range(N_KV):
            for hh in range(GROUP):
                h = g * GROUP + hh
                pn, _ = _head_probs(qk_buf[g, _head_rows(hh), :], b_ref[h], sk_ref[h], first)
                p_buf[g, _head_rows(hh), :] = pn.astype(BF)
        for g in range(N_KV):
            _unstack_heads(o_ref, g, _dot(p_buf[g], _band(vp_ref, vc_ref, g), 1, 0), BF)

    cur = lambda n: (n, 0)
    prev = lambda n: (jnp.maximum(n - 1, 0), 0)
    return pl.pallas_call(
        body, name="attn_fwd", grid=(nb,),
        in_specs=[pl.BlockSpec(memory_space=pltpu.SMEM), pl.BlockSpec((BLOCK, ATTN_DIM), cur),
                  pl.BlockSpec((BLOCK, KV_DIM), cur), pl.BlockSpec((BLOCK, KV_DIM), prev),
                  pl.BlockSpec((BLOCK, KV_DIM), cur), pl.BlockSpec((BLOCK, KV_DIM), prev),
                  pl.BlockSpec((N_HEADS, BLOCK, 2 * BLOCK), lambda n: (0, 0, 0))],
        out_specs=pl.BlockSpec((BLOCK, ATTN_DIM), cur), out_shape=jax.ShapeDtypeStruct((T, ATTN_DIM), BF),
        scratch_shapes=[pltpu.VMEM((N_KV, GROUP_ROWS, 2 * BLOCK), F32), pltpu.VMEM((N_KV, GROUP_ROWS, 2 * BLOCK), BF)],
        compiler_params=_params(1),
    )(sinks, qn, kn, kn, vv, vv, bias)


def _attn_bwd(qn, kn, vv, bias, sinks, do):
    T = qn.shape[0]
    nb = T // BLOCK
    scale = 1.0 / math.sqrt(HEAD_DIM)

    def body(sk_ref, q_ref, kc_ref, kp_ref, vc_ref, vp_ref, b_ref, do_ref,
             dq_ref, dk_ref, dv_ref, db_ref, dsk_ref, dk_full, dv_full, dk_carry, dv_carry, qk_buf, dp_buf, p_buf, ds_buf):
        n = pl.program_id(0)

        @pl.when(n == 0)
        def _():
            db_ref[...] = jnp.zeros_like(db_ref)
            dk_carry[...] = jnp.zeros_like(dk_carry)
            dv_carry[...] = jnp.zeros_like(dv_carry)
            for h in range(N_HEADS):
                dsk_ref[h] = 0.0

        @pl.when(n < nb)
        def _():
            first = _first_mask(n)
            ks = [_band(kp_ref, kc_ref, g) for g in range(N_KV)]
            qs = [_stack_heads(q_ref, g) for g in range(N_KV)]
            douts = [_stack_heads(do_ref, g) for g in range(N_KV)]
            for g in range(N_KV):
                qk_buf[g] = _dot(qs[g], ks[g], 1, 1)
                dp_buf[g] = _dot(douts[g], _band(vp_ref, vc_ref, g), 1, 1)
            for g in range(N_KV):
                for hh in range(GROUP):
                    h = g * GROUP + hh
                    rows = _head_rows(hh)
                    pn, psink = _head_probs(qk_buf[g, rows, :], b_ref[h], sk_ref[h], first)
                    dp = dp_buf[g, rows, :]
                    delta = jnp.sum(pn * dp, axis=-1, keepdims=True)
                    ds = pn * (dp - delta)
                    dsk_ref[h] += -jnp.sum(psink * delta)
                    db_ref[h] += ds
                    ds_buf[g, rows, :] = ds.astype(BF)
                    p_buf[g, rows, :] = pn.astype(BF)
            for g in range(N_KV):
                dsb = ds_buf[g]
                _unstack_heads(dq_ref, g, _dot(dsb, ks[g], 1, 0) * scale, F32)
                gs = pl.ds(g * HEAD_DIM, HEAD_DIM)
                dk_full[:, gs] = _dot(dsb, qs[g], 0, 0)
                dv_full[:, gs] = _dot(p_buf[g], douts[g], 0, 0)

        @pl.when(n == nb)
        def _():
            dk_full[...] = jnp.zeros_like(dk_full)
            dv_full[...] = jnp.zeros_like(dv_full)

        dk_ref[...] = dk_carry[...] + dk_full[pl.ds(0, BLOCK), :]
        dv_ref[...] = dv_carry[...] + dv_full[pl.ds(0, BLOCK), :]
        dk_carry[...] = dk_full[pl.ds(BLOCK, BLOCK), :]
        dv_carry[...] = dv_full[pl.ds(BLOCK, BLOCK), :]

    cur = lambda n: (jnp.minimum(n, nb - 1), 0)
    prev = lambda n: (jnp.maximum(jnp.minimum(n, nb - 1) - 1, 0), 0)
    out_kv = lambda n: (jnp.maximum(n - 1, 0), 0)
    return pl.pallas_call(
        body, name="attn_bwd", grid=(nb + 1,),
        in_specs=[pl.BlockSpec(memory_space=pltpu.SMEM), pl.BlockSpec((BLOCK, ATTN_DIM), cur),
                  pl.BlockSpec((BLOCK, KV_DIM), cur), pl.BlockSpec((BLOCK, KV_DIM), prev),
                  pl.BlockSpec((BLOCK, KV_DIM), cur), pl.BlockSpec((BLOCK, KV_DIM), prev),
                  pl.BlockSpec((N_HEADS, BLOCK, 2 * BLOCK), lambda n: (0, 0, 0)),
                  pl.BlockSpec((BLOCK, ATTN_DIM), cur)],
        out_specs=[pl.BlockSpec((BLOCK, ATTN_DIM), cur), pl.BlockSpec((BLOCK, KV_DIM), out_kv),
                   pl.BlockSpec((BLOCK, KV_DIM), out_kv),
                   pl.BlockSpec((N_HEADS, BLOCK, 2 * BLOCK), lambda n: (0, 0, 0)),
                   pl.BlockSpec(memory_space=pltpu.SMEM)],
        out_shape=[jax.ShapeDtypeStruct((T, ATTN_DIM), F32), jax.ShapeDtypeStruct((T, KV_DIM), F32),
                   jax.ShapeDtypeStruct((T, KV_DIM), F32),
                   jax.ShapeDtypeStruct((N_HEADS, BLOCK, 2 * BLOCK), F32), jax.ShapeDtypeStruct((N_HEADS,), F32)],
        scratch_shapes=[pltpu.VMEM((2 * BLOCK, KV_DIM), F32), pltpu.VMEM((2 * BLOCK, KV_DIM), F32),
                        pltpu.VMEM((BLOCK, KV_DIM), F32), pltpu.VMEM((BLOCK, KV_DIM), F32),
                        pltpu.VMEM((N_KV, GROUP_ROWS, 2 * BLOCK), F32), pltpu.VMEM((N_KV, GROUP_ROWS, 2 * BLOCK), F32),
                        pltpu.VMEM((N_KV, GROUP_ROWS, 2 * BLOCK), BF), pltpu.VMEM((N_KV, GROUP_ROWS, 2 * BLOCK), BF)],
        compiler_params=_params(1),
    )(sinks, qn, kn, kn, vv, vv, bias, do)


def _coords():
    return lax.axis_index("x"), lax.axis_index("y"), lax.axis_index("c")


def _sum8(name, blocks):
    def body(b_ref, o_ref):
        tot = b_ref[0]
        for d in range(1, 8):
            tot = tot + b_ref[d]
        o_ref[...] = tot

    return pl.pallas_call(body, name=name, out_shape=jax.ShapeDtypeStruct(blocks.shape[1:], F32))(blocks)


HBM_SPEC = pl.BlockSpec(memory_space=pltpu.HBM)
SEM_SPEC = pl.BlockSpec(memory_space=pltpu.SEMAPHORE)
ANY_SPEC = pl.BlockSpec(memory_space=pl.ANY)
DATAFLOW = pltpu.SideEffectType.DATAFLOW_SIDE_EFFECTING


OTHER_CHIPS = (4, 2, 6)
ALL_OTHERS = (1, 2, 3, 4, 5, 6, 7)


def _slot(x, y, c, peers):
    return 2 * x + y if peers is OTHER_CHIPS else 4 * x + 2 * y + c


def _slot_copy(land, sems, idx, x, y, c, k, peers, arriving):
    send_sems, recv_sems = sems
    px, py, pc = x ^ (k >> 2), y ^ ((k >> 1) & 1), c ^ (k & 1)
    mine = _slot(x, y, c, peers)
    dst = _slot(px, py, pc, peers) if arriving else mine
    return pltpu.make_async_remote_copy(src_ref=land.at[mine], dst_ref=land.at[dst], send_sem=send_sems.at[idx],
                                        recv_sem=recv_sems.at[idx], device_id=(px, py, pc), device_id_type=MESH)


def _gather_start(name, stacks, groups, peers, after):
    n = len(stacks)
    ng = len(groups)
    np_ = len(peers)
    after = tuple(after)

    def body(*refs):
        lands = refs[:n]
        first = n + len(after)
        sems = [(refs[first + 2 * g], refs[first + 2 * g + 1]) for g in range(ng)]
        token = refs[-1]
        x, y, c = _coords()
        for g, members in enumerate(groups):
            for i, t in enumerate(members):
                for j, k in enumerate(peers):
                    _slot_copy(lands[t], sems[g], np_ * i + j, x, y, c, k, peers, arriving=False).start()
        token[...] = jnp.zeros_like(token)

    out_shape = []
    for members in groups:
        out_shape += [pltpu.SemaphoreType.DMA((np_ * len(members),))] * 2
    out_shape += [pltpu.HBM(w.shape, w.dtype) for w in stacks]
    out_shape.append(jax.ShapeDtypeStruct((8, 128), F32))
    res = pl.pallas_call(
        body, name=name, out_shape=out_shape, in_specs=[HBM_SPEC] * n + [ANY_SPEC] * len(after),
        out_specs=[SEM_SPEC] * (2 * ng) + [HBM_SPEC] * n + [pl.BlockSpec(memory_space=pltpu.VMEM)],
        input_output_aliases={t: 2 * ng + t for t in range(n)},
        compiler_params=pltpu.CompilerParams(has_side_effects=DATAFLOW),
    )(*[pltpu.with_memory_space_constraint(w, pltpu.HBM) for w in stacks], *after)
    sems = [(res[2 * g], res[2 * g + 1]) for g in range(ng)]
    return sems, list(res[2 * ng:2 * ng + n]), res[-1]


def _gather_wait(name, stacks, sems, peers, after):
    n = len(stacks)
    after = tuple(after)

    def body(*refs):
        lands = refs[:n]
        group_sems = (refs[n], refs[n + 1])
        x, y, c = _coords()
        for i in range(n):
            for j, k in enumerate(peers):
                cp = _slot_copy(lands[i], group_sems, len(peers) * i + j, x, y, c, k, peers, arriving=True)
                cp.wait_send()
                cp.wait_recv()

    return pl.pallas_call(
        body, name=name, out_shape=[pltpu.HBM(w.shape, w.dtype) for w in stacks],
        in_specs=[HBM_SPEC] * n + [SEM_SPEC, SEM_SPEC] + [ANY_SPEC] * len(after), out_specs=[HBM_SPEC] * n,
        input_output_aliases={t: t for t in range(n)},
        compiler_params=pltpu.CompilerParams(has_side_effects=DATAFLOW),
    )(*stacks, sems[0], sems[1], *after)


N_PEERS = 7


def _peer(x, y, c, k):
    return x ^ (k >> 2), y ^ ((k >> 1) & 1), c ^ (k & 1)


def _reduce_copy(grad, land, sems, idx, x, y, c, k):
    px, py, pc = _peer(x, y, c, k)
    rh = grad.shape[1] // 2
    return pltpu.make_async_remote_copy(src_ref=grad.at[2 * px + py, pl.ds(pc * rh, rh), :], dst_ref=land.at[k - 1],
                                        send_sem=sems[0].at[idx], recv_sem=sems[1].at[idx], device_id=(px, py, pc),
                                        device_id_type=MESH)


def _reduce_start(name, grads):
    n = len(grads)

    def body(*refs):
        src, lands, sems, token = refs[:n], refs[n:2 * n], (refs[2 * n], refs[2 * n + 1]), refs[-1]
        x, y, c = _coords()
        for t in range(n):
            for k in range(1, N_PEERS + 1):
                _reduce_copy(src[t], lands[t], sems, N_PEERS * t + k - 1, x, y, c, k).start()
        token[...] = jnp.zeros_like(token)

    lands = [lax.empty((N_PEERS, g.shape[1] // 2, g.shape[2]), g.dtype) for g in grads]
    out_shape = [pltpu.SemaphoreType.DMA((N_PEERS * n,))] * 2
    out_shape += [pltpu.HBM(a.shape, a.dtype) for a in list(grads) + lands]
    out_shape.append(jax.ShapeDtypeStruct((8, 128), F32))
    res = pl.pallas_call(
        body, name=name, out_shape=out_shape, in_specs=[HBM_SPEC] * (2 * n),
        out_specs=[SEM_SPEC] * 2 + [HBM_SPEC] * (2 * n) + [pl.BlockSpec(memory_space=pltpu.VMEM)],
        input_output_aliases={t: 2 + t for t in range(2 * n)},
        compiler_params=pltpu.CompilerParams(has_side_effects=DATAFLOW),
    )(*[pltpu.with_memory_space_constraint(a, pltpu.HBM) for a in list(grads) + lands])
    return (res[0], res[1]), list(res[2:2 + n]), list(res[2 + n:2 + 2 * n]), res[-1]


def _reduce_wait(name, grads, lands, sems, after):
    n = len(grads)
    after = tuple(after)

    def body(*refs):
        src, dst, group_sems = refs[:n], refs[n:2 * n], (refs[2 * n], refs[2 * n + 1])
        x, y, c = _coords()
        for t in range(n):
            for k in range(1, N_PEERS + 1):
                cp = _reduce_copy(src[t], dst[t], group_sems, N_PEERS * t + k - 1, x, y, c, k)
                cp.wait_send()
                cp.wait_recv()

    res = pl.pallas_call(
        body, name=name, out_shape=[pltpu.HBM(a.shape, a.dtype) for a in list(grads) + list(lands)],
        in_specs=[HBM_SPEC] * (2 * n) + [SEM_SPEC, SEM_SPEC] + [ANY_SPEC] * len(after), out_specs=[HBM_SPEC] * (2 * n),
        input_output_aliases={t: t for t in range(2 * n)},
        compiler_params=pltpu.CompilerParams(has_side_effects=DATAFLOW),
    )(*grads, *lands, sems[0], sems[1], *after)
    return list(res[:n]), list(res[n:])


def _join_halves(name, halves, deps=()):
    n = len(halves)

    def body(*refs):
        src, dst = refs[:n], refs[n + len(deps):2 * n + len(deps)]
        send_sems, recv_sems = refs[-2:]
        x, y, c = _coords()
        cps = []
        for t in range(n):
            cp = pltpu.make_async_remote_copy(src_ref=src[t], dst_ref=dst[t], send_sem=send_sems.at[t],
                                              recv_sem=recv_sems.at[t], device_id=(x, y, 1 - c), device_id_type=MESH)
            cp.start()
            cps.append(cp)
        for cp in cps:
            cp.wait()

    anyspec = pl.BlockSpec(memory_space=pl.ANY)
    return pl.pallas_call(
        body, name=name, out_shape=[jax.ShapeDtypeStruct(h.shape, h.dtype) for h in halves],
        in_specs=[anyspec] * (n + len(deps)), out_specs=[anyspec] * n,
        scratch_shapes=[pltpu.SemaphoreType.DMA((n,)), pltpu.SemaphoreType.DMA((n,))],
    )(*halves, *deps)


def _row_block(rows):
    for rb in (512, 256, 128, 64, 32, 16):
        if rows % rb == 0:
            return rb
    raise ValueError(rows)


def _sum_devices(name, grad, land, place):
    S, R, C = grad.shape
    rh = R // 2
    rb = _row_block(rh)
    nbh = rh // rb

    def body(place_ref, g_ref, l_ref, o_ref):
        tot = g_ref[...].astype(F32)
        for k in range(N_PEERS):
            tot = tot + l_ref[k].astype(F32)
        o_ref[...] = tot

    return pl.pallas_call(
        body, name=name,
        grid_spec=pltpu.PrefetchScalarGridSpec(
            num_scalar_prefetch=1, grid=(nbh,),
            in_specs=[pl.BlockSpec((None, rb, C), lambda r, place: (place[0], place[1] * nbh + r, 0)),
                      pl.BlockSpec((N_PEERS, rb, C), lambda r, place: (0, r, 0))],
            out_specs=pl.BlockSpec((rb, C), lambda r, place: (r, 0))),
        out_shape=jax.ShapeDtypeStruct((rh, C), F32), compiler_params=_params(1),
    )(place, grad, land)


def _adamw_math(w, g, m, v):
    m2 = ADAM_B1 * m + (1.0 - ADAM_B1) * g
    v2 = ADAM_B2 * v + (1.0 - ADAM_B2) * (g * g)
    m_hat = m2 / (1.0 - ADAM_B1 ** ADAM_STEP)
    v_hat = v2 / (1.0 - ADAM_B2 ** ADAM_STEP)
    delta = -ADAM_LR * (m_hat / (jnp.sqrt(v_hat) + ADAM_EPS) + ADAM_WD * w)
    return delta, m2, v2


def _adamw(name, w, m, v, gs):
    L, R, C = w.shape
    Rh = R // 2
    rb = _row_block(Rh)
    nbh = Rh // rb
    assert len(gs) == L

    def body(core_ref, w_ref, m_ref, v_ref, *rest):
        g_refs, (go_ref, d_ref, m2_ref, v2_ref) = rest[:2 * L], rest[2 * L:]
        layer, half = pl.program_id(0), pl.program_id(1)
        mine = half == core_ref[0]
        g = jnp.where(mine, g_refs[0][...], g_refs[1][...])
        for t in range(1, L):
            g = jnp.where(layer == t, jnp.where(mine, g_refs[2 * t][...], g_refs[2 * t + 1][...]), g)
        delta, m2, v2 = _adamw_math(w_ref[...], g, m_ref[...], v_ref[...])
        go_ref[...] = g
        d_ref[...] = delta
        m2_ref[...] = m2
        v2_ref[...] = v2

    wspec = pl.BlockSpec((None, rb, C), lambda l, h, r, core: (l, h * nbh + r, 0))
    gspec = pl.BlockSpec((rb, C), lambda l, h, r, core: (r, 0))
    return pl.pallas_call(
        body, name=name,
        grid_spec=pltpu.PrefetchScalarGridSpec(num_scalar_prefetch=1, grid=(L, 2, nbh),
                                               in_specs=[wspec] * 3 + [gspec] * (2 * L), out_specs=[wspec] * 4),
        out_shape=[jax.ShapeDtypeStruct((L, R, C), F32)] * 4, compiler_params=_params(3),
    )(lax.axis_index("c").astype(jnp.int32).reshape(1), w, m, v, *[g for pair in gs for g in pair])


def _adamw_small(ws, gs, ms, vs):
    n = len(ws)

    def body(*refs):
        w_refs, g_refs, m_refs, v_refs = (refs[k * n:(k + 1) * n] for k in range(4))
        d_refs, m2_refs, v2_refs = (refs[(4 + k) * n:(5 + k) * n] for k in range(3))
        for t in range(n):
            delta, m2, v2 = _adamw_math(w_refs[t][...], g_refs[t][...], m_refs[t][...], v_refs[t][...])
            d_refs[t][...] = delta
            m2_refs[t][...] = m2
            v2_refs[t][...] = v2

    res = pl.pallas_call(body, name="adamw_small", out_shape=[jax.ShapeDtypeStruct(w.shape, F32) for w in ws] * 3)(
        *ws, *gs, *ms, *vs)
    return res[:n], res[n:2 * n], res[2 * n:]


def _packed_rows(shape):
    c = shape[-1]
    return (int(np.prod(shape)) // c) * -(-c // LANES)


def _pack(arrays):
    total = sum(_packed_rows(a.shape) for a in arrays)
    total += -total % 8
    buf, r0 = None, 0
    for a in arrays:
        a = a.astype(F32).reshape(-1, a.shape[-1])
        r, c = a.shape
        k = -(-c // LANES)
        a = jnp.pad(a, ((0, 0), (0, k * LANES - c))).reshape(r * k, LANES)
        a = jnp.pad(a, ((r0, total - r0 - r * k), (0, 0)))
        buf = a if buf is None else buf + a
        r0 += r * k
    return buf


def _unpack(buf, shapes):
    out, r0 = [], 0
    for shp in shapes:
        c = shp[-1]
        rows = _packed_rows(shp)
        out.append(buf[r0:r0 + rows].reshape(-1, -(-c // LANES) * LANES)[:, :c].reshape(shp))
        r0 += rows
    return out


def _rms(x, g):
    return x * lax.rsqrt(jnp.mean(x * x, axis=-1, keepdims=True) + NORM_EPS) * g


def _residual_norm_ep(acc, *rest):
    *bias, res, gain = rest
    x = acc + res + (bias[0] if bias else 0.0)
    return x, _rms(x, gain)


RESIDUAL_NORM_OUTS = (("tile", F32), ("tile", BF))


def _mlp_up(tag, h, w_up_sm):
    (up,) = _mm(f"mlp{tag}_up", h, w_up_sm, nt=False, b_sm=True, tm=2048, tn=1024, rows=256,
                ep_fn=lambda acc: (acc,), outs=(("tile", BF),))
    return up


RMS_BWD_OUTS = (("tile", F32), ("tile", BF), ("colsum", F32), ("colsum", F32))


def _mlp_bwd(tag, dy, dy_bf, x, g, up, w_up_sm, w_down):
    (dup,) = _mm(f"mlp{tag}_dup", dy_bf, w_down, nt=True, tm=2048, tn=1024, rows=256, ep_in=((up, "tile"),),
                 ep_fn=lambda acc, u: (acc * (2.0 * jnp.maximum(u.astype(F32), 0.0)),), outs=(("tile", BF),))
    dx, dx_bf, dg, dx_sum = _mm(f"mlp{tag}_dx", dup, w_up_sm, nt=True, b_sm=True, tm=512, tn=1024, rows=256,
                                ep_in=((x, "tile"), (g, "row"), (dy, "tile")), ep_fn=_rms_bwd_ep, outs=RMS_BWD_OUTS)
    return dx, dx_bf, dg, dx_sum, dup


class _Reduction:
    def __init__(self, tag, grads, place):
        self.tag, self.place = tag, place
        self.sems, self.grads, self.lands, self.token = _reduce_start(f"reduce_start_{tag}", grads)

    def finish(self, after):
        grads, lands = _reduce_wait(f"reduce_wait_{self.tag}", self.grads, self.lands, self.sems, after)
        return [_sum_devices(f"reduce_sum_{self.tag}{i}", g, l, self.place) for i, (g, l) in enumerate(zip(grads, lands))]


def kernel(x, conv_norm_g, conv_w_in, conv_b_in, conv_dw, conv_dw_b, conv_ln_g, conv_ln_b, conv_w_out, conv_b_out, attn_norm_g, w_qkv, b_qkv, q_norm_g, k_norm_g, sinks, w_o, b_o, rel_bias, mlp_norm_g, w_up, w_down, loss_target, m_conv_norm_g, m_conv_w_in, m_conv_b_in, m_conv_dw, m_conv_dw_b, m_conv_ln_g, m_conv_ln_b, m_conv_w_out, m_conv_b_out, m_attn_norm_g, m_w_qkv, m_b_qkv, m_q_norm_g, m_k_norm_g, m_sinks, m_w_o, m_b_o, m_rel_bias, m_mlp_norm_g, m_w_up, m_w_down, v_conv_norm_g, v_conv_w_in, v_conv_b_in, v_conv_dw, v_conv_dw_b, v_conv_ln_g, v_conv_ln_b, v_conv_w_out, v_conv_b_out, v_attn_norm_g, v_w_qkv, v_b_qkv, v_q_norm_g, v_k_norm_g, v_sinks, v_w_o, v_b_o, v_rel_bias, v_mlp_norm_g, v_w_up, v_w_down):
    Dm = D_MODEL
    x2d = x[0]
    tgt = loss_target[0]
    T = x2d.shape[0]
    shard = 2 * lax.axis_index("x") + lax.axis_index("y")

    me = 2 * shard + lax.axis_index("c")

    def own_slot(block, slots, index):
        return lax.dynamic_update_slice(lax.empty((slots,) + block.shape, block.dtype), block[None],
                                        (index,) + (0,) * block.ndim)

    sharded_small = [conv_dw[0], attn_norm_g, b_qkv, b_o]
    (small_sems,), (small_land,), small_token = _gather_start(
        "small_weights_start", [own_slot(_pack(sharded_small), 8, me)], ((0,),), ALL_OTHERS, after=())

    big = [conv_w_in[0], conv_w_out[0], w_qkv[0], w_o[0], w_up[0], w_up[1], w_down[0], w_down[1]]
    stacks = [own_slot(w.astype(BF), N_SHARD, shard) for w in big]
    groups = ((0,), (1,), (4, 6), (2, 3), (5, 7))
    gather_sems, stacks, gather_token = _gather_start("gather_start", stacks, groups, OTHER_CHIPS, after=(small_token,))

    def gathered_group(g, name, after):
        return _gather_wait(name, [stacks[t] for t in groups[g]], gather_sems[g], OTHER_CHIPS, after)

    bucket = _bucket_table()
    bias = _bias_table(rel_bias, bucket)

    h0 = _rms_fwd("conv_norm", x2d, conv_norm_g, deps=(gather_token,))
    (w_in_sm,) = gathered_group(0, "gather_wait_conv_in", (h0, bias))
    (u,) = _mm("conv_in", h0, w_in_sm, nt=False, b_sm=True, tm=2048, tn=512, rows=256, ep_in=((conv_b_in, "row"),),
               ep_fn=lambda acc, b: (acc + b,), outs=(("tile", BF),))
    (gathered,) = _gather_wait("small_weights_wait", [small_land], small_sems, ALL_OTHERS, (u,))
    chips = [_unpack(gathered[2 * s], [a.shape for a in sharded_small]) for s in range(N_SHARD)]
    dw_f, attn_norm_f, b_qkv_f, b_o_f = (jnp.concatenate([chips[s][t] for s in range(N_SHARD)], axis=-1)
                                         for t in range(len(sharded_small)))
    dw_pad = jnp.pad(dw_f, ((0, HALO - CONV_W), (0, 0)))
    cv, s_act = _conv_fwd(u, dw_pad, conv_dw_b, conv_ln_g, conv_ln_b)
    (g_out,) = gathered_group(1, "gather_wait_conv_out", (s_act,))
    w_out_f = g_out.reshape(Dm, Dm)
    x1, h1 = _mm("conv_out", s_act, w_out_f, nt=False, tm=1024, tn=1024, rows=256,
                 ep_in=((conv_b_out, "row"), (x2d, "tile"), (mlp_norm_g[0:1], "row")), ep_fn=_residual_norm_ep,
                 outs=RESIDUAL_NORM_OUTS)

    g_up0, g_down0 = gathered_group(2, "gather_wait_mlp0", (x1,))
    w_up_sm = [g_up0, None]
    w_down_f = [g_down0.reshape(D_FF, Dm), None]
    up0 = _mlp_up(0, h1, w_up_sm[0])
    x2, h2 = _mm("mlp0_down", up0, w_down_f[0], nt=False, tm=512, tn=1024, rows=256, a_fn=_relu2,
                 ep_in=((x1, "tile"), (attn_norm_f, "row")), ep_fn=_residual_norm_ep, outs=RESIDUAL_NORM_OUTS)

    g_qkv, g_o = gathered_group(3, "gather_wait_attn", (x2,))
    w_qkv_f = jnp.transpose(g_qkv, (1, 0, 2)).reshape(Dm, QKV_DIM)
    w_o_f = g_o.reshape(ATTN_DIM, Dm)
    qg_t = jnp.tile(q_norm_g, (1, N_HEADS))
    kg_t = jnp.tile(k_norm_g, (1, N_KV))

    def qkv_ep(acc, b, qg, kg, sel_q, sel_q_t, sel_k, sel_k_t):
        proj = acc + b
        q, k, v = proj[:, :ATTN_DIM], proj[:, ATTN_DIM:ATTN_DIM + KV_DIM], proj[:, ATTN_DIM + KV_DIM:]
        return proj, _qk_normed(q, qg, (sel_q, sel_q_t), 1.0 / math.sqrt(HEAD_DIM)), _qk_normed(k, kg, (sel_k, sel_k_t), 1.0), v

    qkv, qn, kn, vv = _mm(
        "attn_qkv", h2, w_qkv_f, nt=False, tm=1024, tn=QKV_DIM, rows=256, ep_fn=qkv_ep,
        ep_in=((b_qkv_f, "row"), (qg_t, "whole"), (kg_t, "whole"))
        + tuple((m, "whole") for m in _head_select(ATTN_DIM) + _head_select(KV_DIM)),
        outs=(("tile", F32), ("tile", BF, ATTN_DIM), ("tile", BF, KV_DIM), ("tile", BF, KV_DIM)))
    sinks1 = sinks[0]
    att = _attn_fwd(qn, kn, vv, bias, sinks1)
    x3, h3 = _mm("attn_out", att, w_o_f, nt=False, tm=1024, tn=1024, rows=256,
                 ep_in=((b_o_f, "row"), (x2, "tile"), (mlp_norm_g[1:2], "row")), ep_fn=_residual_norm_ep,
                 outs=RESIDUAL_NORM_OUTS)

    g_up1, g_down1 = gathered_group(4, "gather_wait_mlp1", (x3,))
    w_up_sm[1] = g_up1
    w_down_f[1] = g_down1.reshape(D_FF, Dm)
    up1 = _mlp_up(1, h3, w_up_sm[1])

    def loss_ep(acc, r, t):
        diff = acc + r - t
        dy = diff * (1.0 / Dm)
        return dy, dy, jnp.sum(diff * diff, axis=0, keepdims=True)

    dy, dy_bf, sq = _mm("mlp1_down_loss", up1, w_down_f[1], nt=False, tm=512, tn=1024, rows=256, a_fn=_relu2,
                        ep_in=((x3, "tile"), (tgt, "tile")), ep_fn=loss_ep,
                        outs=(("tile", F32), ("tile", BF), ("colsum", F32)))

    place = jnp.stack([shard, lax.axis_index("c")]).astype(jnp.int32)
    dx3, dx3_bf, dg_mlp1, db_o, dup1 = _mlp_bwd(1, dy, dy_bf, x3, mlp_norm_g[1:2], up1, w_up_sm[1], w_down_f[1])
    dw_down1 = _mm_tn("mlp1_dw_down", up1, dy_bf, tm=1024, tn=1024, tk=2048, a_fn=_relu2)
    dw_up1 = _mm_tn("mlp1_dw_up", h3, dup1, tm=1024, tn=1024, tk=2048, out_sm=N_SHARD)
    red_mlp1 = _Reduction("mlp1", [dw_up1, dw_down1.reshape(N_SHARD, D_FF // N_SHARD, Dm)], place)

    ident = lambda acc: (acc,)
    (datt,) = _mm("attn_dout", dx3_bf, w_o_f, nt=True, tm=1024, tn=1024, rows=256, ep_fn=ident, outs=(("tile", BF),),
                  deps=(red_mlp1.token,))
    dw_o = _mm_tn("attn_dw_o", att, dx3_bf, tm=1024, tn=1024, tk=2048)
    dqn, dkn, dvv, dbias, dsinks = _attn_bwd(qn, kn, vv, bias, sinks1, datt)
    drel = _bias_grad(dbias, bucket)
    dqkv, db_qkv, dqg_t, dkg_t = _qk_norm_bwd(qkv, dqn, dkn, dvv, qg_t, kg_t)
    dw_qkv = _mm_tn("attn_dw_qkv", h2, dqkv, tm=1024, tn=QKV_DIM, tk=2048)
    red_attn = _Reduction("attn", [jnp.transpose(dw_qkv.reshape(Dm, N_SHARD, QKV_DIM // N_SHARD), (1, 0, 2)),
                                   dw_o.reshape(N_SHARD, ATTN_DIM // N_SHARD, Dm)], place)
    dx2, dx2_bf, dg_attn, _ = _mm("attn_dx", dqkv, w_qkv_f, nt=True, tm=1024, tn=1024, rows=256,
                                  ep_in=((x2, "tile"), (attn_norm_f, "row"), (dx3, "tile")), ep_fn=_rms_bwd_ep,
                                  outs=RMS_BWD_OUTS, deps=(red_attn.token,))

    dx1, dx1_bf, dg_mlp0, db_out, dup0 = _mlp_bwd(0, dx2, dx2_bf, x1, mlp_norm_g[0:1], up0, w_up_sm[0], w_down_f[0])
    dw_down0 = _mm_tn("mlp0_dw_down", up0, dx2_bf, tm=1024, tn=1024, tk=2048, a_fn=_relu2)
    dw_up0 = _mm_tn("mlp0_dw_up", h1, dup0, tm=1024, tn=1024, tk=2048, out_sm=N_SHARD)
    dw_out = _mm_tn("conv_dw_out", s_act, dx1_bf, tm=1024, tn=1024, tk=2048)
    red_mlp0 = _Reduction("mlp0", [dw_up0, dw_down0.reshape(N_SHARD, D_FF // N_SHARD, Dm),
                                   dw_out.reshape(N_SHARD, Dm // N_SHARD, Dm)], place)
    (r_qkv, r_o) = red_attn.finish((dx1,))
    (r_up1, r_down1) = red_mlp1.finish((dx1,))

    dcv, dln_g, dln_b, ddw_b = _mm("conv_ds", dx1_bf, w_out_f, nt=True, tm=1024, tn=1024, rows=256,
                                   ep_in=((cv, "tile"), (conv_ln_g, "row"), (conv_ln_b, "row")),
                                   ep_fn=_ln_silu_bwd_ep,
                                   outs=(("tile", F32), ("colsum", F32), ("colsum", F32), ("colsum", F32)),
                                   deps=(red_mlp0.token,))
    du, db_in, ddw8 = _conv_bwd(u, dcv, dw_pad)
    (r_up0, r_down0, r_out) = red_mlp0.finish((du,))
    dw_in = _mm_tn("conv_dw_in", h0, du, tm=1024, tn=512, tk=4096, out_sm=N_SHARD)
    red_conv = _Reduction("conv", [dw_in], place)
    def first_layer_ep(*args):
        tot, _, dg, _ = _rms_bwd_ep(*args)
        return tot, dg

    gx, dg_conv = _mm("conv_dx", du, w_in_sm, nt=True, b_sm=True, tm=1024, tn=1024, rows=256,
                      ep_in=((x2d, "tile"), (conv_norm_g, "row"), (dx1, "tile")), ep_fn=first_layer_ep,
                      outs=(("tile", F32), ("colsum", F32)), deps=(red_conv.token,))
    (r_in,) = red_conv.finish((gx,))

    dqg = dqg_t.reshape(N_HEADS, HEAD_DIM).sum(axis=0, keepdims=True)
    dkg = dkg_t.reshape(N_KV, HEAD_DIM).sum(axis=0, keepdims=True)
    small_full = [dg_conv, db_in, ddw8.sum(axis=1)[:CONV_W], ddw_b, dln_g, dln_b, db_out, dg_attn, db_qkv, dqg, dkg,
                  dsinks[None, :], db_o, drel.reshape(1, REL_BUCKETS * N_HEADS),
                  jnp.pad(dg_mlp0, ((0, 1), (0, 0))) + jnp.pad(dg_mlp1, ((1, 0), (0, 0))), sq]
    (sg_sems,), (sg_land,), sg_token = _gather_start(
        "small_grads_start", [own_slot(_pack(small_full), 8, me)], ((0,),), ALL_OTHERS, after=())

    mine = [r_in, r_out, r_qkv, r_o, r_up0, r_up1, r_down0, r_down1]
    r_in, r_out, r_qkv, r_o, r_up0, r_up1, r_down0, r_down1 = zip(
        mine, _join_halves("join_halves", mine, deps=(sg_token,)))

    big_out = {}
    for nm, w, m, v, gs in (("conv_w_in", conv_w_in, m_conv_w_in, v_conv_w_in, (r_in,)),
                            ("conv_w_out", conv_w_out, m_conv_w_out, v_conv_w_out, (r_out,)),
                            ("w_qkv", w_qkv, m_w_qkv, v_w_qkv, (r_qkv,)),
                            ("w_o", w_o, m_w_o, v_w_o, (r_o,)),
                            ("w_up", w_up, m_w_up, v_w_up, (r_up0, r_up1)),
                            ("w_down", w_down, m_w_down, v_w_down, (r_down0, r_down1))):
        big_out[nm] = _adamw(f"adamw_{nm}", w, m, v, gs)

    (sg_land,) = _gather_wait("small_grads_wait", [sg_land], sg_sems, ALL_OTHERS,
                              [big_out[nm][0] for nm in big_out])
    small_sum = _sum8("small_grads_sum", sg_land)
    (r_norm, r_b_in, r_dw, r_dw_b, r_ln_g, r_ln_b, r_b_out, r_attn_norm, r_b_qkv, r_qg, r_kg, r_sinks, r_b_o, r_rel,
     r_mlp_norm, r_sq) = _unpack(small_sum, [a.shape for a in small_full])
    loss = 0.5 * jnp.sum(r_sq) * (1.0 / Dm)

    def cols(a, width):
        return lax.dynamic_slice_in_dim(a, shard * width, width, axis=a.ndim - 1)

    small_names = ["conv_norm_g", "conv_b_in", "conv_dw", "conv_dw_b", "conv_ln_g", "conv_ln_b", "conv_b_out",
                   "attn_norm_g", "b_qkv", "q_norm_g", "k_norm_g", "sinks", "b_o", "rel_bias", "mlp_norm_g"]
    small_g = [r_norm, r_b_in, cols(r_dw, Dm // N_SHARD)[None], r_dw_b, r_ln_g, r_ln_b, r_b_out,
               cols(r_attn_norm, Dm // N_SHARD), cols(r_b_qkv, QKV_DIM // N_SHARD), r_qg, r_kg, r_sinks,
               cols(r_b_o, Dm // N_SHARD), r_rel.reshape(REL_BUCKETS, N_HEADS), r_mlp_norm]
    small_w = [conv_norm_g, conv_b_in, conv_dw, conv_dw_b, conv_ln_g, conv_ln_b, conv_b_out, attn_norm_g, b_qkv,
               q_norm_g, k_norm_g, sinks, b_o, rel_bias, mlp_norm_g]
    small_m = [m_conv_norm_g, m_conv_b_in, m_conv_dw, m_conv_dw_b, m_conv_ln_g, m_conv_ln_b, m_conv_b_out,
               m_attn_norm_g, m_b_qkv, m_q_norm_g, m_k_norm_g, m_sinks, m_b_o, m_rel_bias, m_mlp_norm_g]
    small_v = [v_conv_norm_g, v_conv_b_in, v_conv_dw, v_conv_dw_b, v_conv_ln_g, v_conv_ln_b, v_conv_b_out,
               v_attn_norm_g, v_b_qkv, v_q_norm_g, v_k_norm_g, v_sinks, v_b_o, v_rel_bias, v_mlp_norm_g]
    flat2 = lambda a: a.reshape(-1, a.shape[-1])
    small_g = [flat2(g) for g in small_g]
    d_s, m_s, v_s = _adamw_small([flat2(w) for w in small_w], small_g, [flat2(m) for m in small_m],
                                 [flat2(v) for v in small_v])
    small_out = {}
    for nm, w, g, d, m2, v2 in zip(small_names, small_w, small_g, d_s, m_s, v_s):
        small_out[nm] = tuple(a.reshape(w.shape) for a in (g, d, m2, v2))

    order = ["conv_norm_g", "conv_w_in", "conv_b_in", "conv_dw", "conv_dw_b", "conv_ln_g", "conv_ln_b", "conv_w_out",
             "conv_b_out", "attn_norm_g", "w_qkv", "b_qkv", "q_norm_g", "k_norm_g", "sinks", "w_o", "b_o", "rel_bias",
             "mlp_norm_g", "w_up", "w_down"]
    res = {**small_out, **big_out}
    outs = [loss, gx[None]]
    for part in range(4):
        outs += [res[nm][part] for nm in order]
    return tuple(outs)
```

```python
import math

import numpy as np
import jax
import jax.numpy as jnp
from jax import lax
from jax.experimental import pallas as pl
from jax.experimental.pallas import tpu as pltpu

F32 = jnp.float32
BF = jnp.bfloat16
MESH = pl.DeviceIdType.MESH

D_MODEL = 1024
D_FF = 4096
N_HEADS = 16
N_KV = 2
GROUP = N_HEADS // N_KV
HEAD_DIM = 64
ATTN_DIM = N_HEADS * HEAD_DIM
KV_DIM = N_KV * HEAD_DIM
QKV_DIM = ATTN_DIM + 2 * KV_DIM
BLOCK = 128
CONV_W = 31
HALO = 32
REL_BUCKETS = 32
REL_MAX_DIST = 128
NORM_EPS = 1e-6
NEG_INF = -1e30
N_SHARD = 4
LANES = 1024

ADAM_LR = 0.001
ADAM_B1 = 0.9
ADAM_B2 = 0.999
ADAM_EPS = 1e-08
ADAM_WD = 0.01
ADAM_STEP = 10

VMEM_LIMIT = 56 * 1024 * 1024


def _params(n_axes):
    return pltpu.CompilerParams(dimension_semantics=("arbitrary",) * n_axes, vmem_limit_bytes=VMEM_LIMIT)


def _dot(a, b, ca, cb):
    return lax.dot_general(a, b, (((ca,), (cb,)), ((), ())), preferred_element_type=F32)


def _mm(name, a, b, *, nt, tm, tn, ep_fn, outs, a_fn=None, b_sm=False, ep_in=(), deps=(), rows=None):
    M, K = a.shape
    rows = tm if rows is None else rows
    if b_sm:
        S, ks = b.shape[0], b.shape[2]
        N, per = (b.shape[1], None) if nt else (S * b.shape[2], b.shape[2] // tn)
        assert (S * ks == K) if nt else (b.shape[1] == K)
    else:
        N = b.shape[0] if nt else b.shape[1]
        assert (b.shape[1] if nt else b.shape[0]) == K
    assert M % tm == 0 and N % tn == 0 and tm % rows == 0
    ne, no, nd = len(ep_in), len(outs), len(deps)

    def body(a_ref, b_ref, *rest):
        ep_refs, out_refs = rest[:ne], rest[ne + nd:ne + nd + no]
        i = pl.program_id(1)
        sums = [None] * no
        for r in range(tm // rows):
            rs = pl.ds(r * rows, rows)

            def lhs(cols):
                av = a_ref[rs, cols]
                return (av if a_fn is None else a_fn(av)).astype(BF)

            if b_sm and nt:
                acc = None
                for s in range(S):
                    part = _dot(lhs(pl.ds(s * ks, ks)), b_ref[s].astype(BF), 1, 1)
                    acc = part if acc is None else acc + part
            else:
                acc = _dot(lhs(slice(None)), b_ref[...].astype(BF), 1, 1 if nt else 0)
            ep_vals = [ref[rs, :] if kind == "tile" else ref[...] for ref, (_, kind) in zip(ep_refs, ep_in)]
            vals = ep_fn(acc, *ep_vals)
            for o, ((kind, dt, *_), ref, val) in enumerate(zip(outs, out_refs, vals)):
                if kind == "tile":
                    ref[rs, :] = val.astype(dt)
                else:
                    sums[o] = val if sums[o] is None else sums[o] + val
        for (kind, *_), ref, val in zip(outs, out_refs, sums):
            if kind == "colsum":
                @pl.when(i == 0)
                def _():
                    ref[...] = val

                @pl.when(i > 0)
                def _():
                    ref[...] += val

    if b_sm and nt:
        b_spec = pl.BlockSpec((S, tn, ks), lambda j, i: (0, j, 0))
    elif b_sm:
        b_spec = pl.BlockSpec((None, K, tn), lambda j, i: (j // per, 0, j % per))
    elif nt:
        b_spec = pl.BlockSpec((tn, K), lambda j, i: (j, 0))
    else:
        b_spec = pl.BlockSpec((K, tn), lambda j, i: (0, j))
    in_specs = [pl.BlockSpec((tm, K), lambda j, i: (i, 0)), b_spec]
    for arr, kind in ep_in:
        if kind == "tile":
            assert arr.shape == (M, N)
            in_specs.append(pl.BlockSpec((tm, tn), lambda j, i: (i, j)))
        elif kind == "whole":
            in_specs.append(pl.BlockSpec(arr.shape, lambda j, i, rank=arr.ndim: (0,) * rank))
        else:
            assert arr.shape == (1, N)
            in_specs.append(pl.BlockSpec((1, tn), lambda j, i: (0, j)))
    in_specs += [pl.BlockSpec(memory_space=pl.ANY)] * nd
    out_shape, out_specs = [], []
    for kind, dt, *width in outs:
        if kind == "tile" and width:
            assert tn == N
            out_shape.append(jax.ShapeDtypeStruct((M, width[0]), dt))
            out_specs.append(pl.BlockSpec((tm, width[0]), lambda j, i: (i, 0)))
        elif kind == "tile":
            out_shape.append(jax.ShapeDtypeStruct((M, N), dt))
            out_specs.append(pl.BlockSpec((tm, tn), lambda j, i: (i, j)))
        else:
            out_shape.append(jax.ShapeDtypeStruct((1, N), F32))
            out_specs.append(pl.BlockSpec((1, tn), lambda j, i: (0, j)))
    return pl.pallas_call(
        body, name=name, grid=(N // tn, M // tm), in_specs=in_specs, out_specs=out_specs, out_shape=out_shape,
        compiler_params=_params(2),
    )(a, b, *[arr for arr, _ in ep_in], *deps)


def _mm_tn(name, a, b, *, tm, tn, tk, a_fn=None, out_sm=None):
    T, Ka = a.shape
    N = b.shape[1]
    assert b.shape[0] == T and T % tk == 0 and Ka % tm == 0 and N % tn == 0
    nk = T // tk

    def body(a_ref, b_ref, o_ref, acc_ref):
        k = pl.program_id(2)

        @pl.when(k == 0)
        def _():
            acc_ref[...] = jnp.zeros_like(acc_ref)

        av = a_ref[...]
        if a_fn is not None:
            av = a_fn(av)
        acc_ref[...] += _dot(av.astype(BF), b_ref[...].astype(BF), 0, 0)

        @pl.when(k == nk - 1)
        def _():
            o_ref[...] = acc_ref[...].astype(BF)

    if out_sm is None:
        out_shape = jax.ShapeDtypeStruct((Ka, N), BF)
        out_spec = pl.BlockSpec((tm, tn), lambda i, j, k: (i, j))
    else:
        per = (N // out_sm) // tn
        assert per * tn * out_sm == N
        out_shape = jax.ShapeDtypeStruct((out_sm, Ka, N // out_sm), BF)
        out_spec = pl.BlockSpec((None, tm, tn), lambda i, j, k: (j // per, i, j % per))
    return pl.pallas_call(
        body, name=name, grid=(Ka // tm, N // tn, nk),
        in_specs=[pl.BlockSpec((tk, tm), lambda i, j, k: (k, i)), pl.BlockSpec((tk, tn), lambda i, j, k: (k, j))],
        out_specs=out_spec, out_shape=out_shape, scratch_shapes=[pltpu.VMEM((tm, tn), F32)],
        compiler_params=_params(3),
    )(a, b)


def _relu2(v):
    r = jnp.maximum(v.astype(F32), 0.0)
    return r * r


def _rms_bwd_ep(dh, x, g, dres):
    rstd = lax.rsqrt(jnp.mean(x * x, axis=-1, keepdims=True) + NORM_EPS)
    xh = x * rstd
    dxh = dh * g
    dx = rstd * (dxh - xh * jnp.mean(dxh * xh, axis=-1, keepdims=True))
    tot = dres + dx
    return tot, tot, jnp.sum(dh * xh, axis=0, keepdims=True), jnp.sum(tot, axis=0, keepdims=True)


def _rms_fwd(name, x, g, tm=512, deps=()):
    T, Dm = x.shape

    def body(x_ref, g_ref, *rest):
        o_ref = rest[-1]
        xv = x_ref[...]
        rstd = lax.rsqrt(jnp.mean(xv * xv, axis=-1, keepdims=True) + NORM_EPS)
        o_ref[...] = (xv * rstd * g_ref[...]).astype(BF)

    return pl.pallas_call(
        body, name=name, grid=(T // tm,),
        in_specs=[pl.BlockSpec((tm, Dm), lambda i: (i, 0)), pl.BlockSpec((1, Dm), lambda i: (0, 0))]
        + [pl.BlockSpec(memory_space=pl.ANY)] * len(deps),
        out_specs=pl.BlockSpec((tm, Dm), lambda i: (i, 0)), out_shape=jax.ShapeDtypeStruct((T, Dm), BF),
        compiler_params=_params(1),
    )(x, g, *deps)


HEAD_GROUP = 256


def _two_term_dot(v, m):
    hi = v.astype(BF)
    lo = (v - hi.astype(F32)).astype(BF)
    return _dot(hi, m, 1, 0) + _dot(lo, m, 1, 0)


def _head_sum(v, ones):
    n = v.shape[1]
    w = min(n, HEAD_GROUP)
    blk = ones[:w, :w]
    parts = [_two_term_dot(v[:, c:c + w], blk) for c in range(0, n, w)]
    return parts[0] if len(parts) == 1 else jnp.concatenate(parts, axis=1)


def _head_ones():
    idx = np.arange(HEAD_GROUP) // HEAD_DIM
    return jnp.asarray((idx[:, None] == idx[None, :]).astype(np.float32), dtype=BF)


def _qk_normed(x, g, ones, scale):
    r = lax.rsqrt(_head_sum(x * x, ones) * (1.0 / HEAD_DIM) + NORM_EPS)
    return x * r * g * scale


def _qk_norm_bwd(qkv, dqn, dkn, dv, qg_t, kg_t, tm=256):
    T = qkv.shape[0]

    def body(x_ref, dq_ref, dk_ref, dv_ref, qg_ref, kg_ref, ones_ref, o_ref, db_ref, dqg_ref, dkg_ref):
        i = pl.program_id(0)
        ones = ones_ref[...]

        def one(x, dy, g):
            r = lax.rsqrt(_head_sum(x * x, ones) * (1.0 / HEAD_DIM) + NORM_EPS)
            xh = x * r
            dxh = dy * g
            dx = r * (dxh - xh * (_head_sum(dxh * xh, ones) * (1.0 / HEAD_DIM)))
            return dx, jnp.sum(dy * xh, axis=0, keepdims=True)

        dq, dqg = one(x_ref[:, pl.ds(0, ATTN_DIM)], dq_ref[...], qg_ref[...])
        dk, dkg = one(x_ref[:, pl.ds(ATTN_DIM, KV_DIM)], dk_ref[...], kg_ref[...])
        dvv = dv_ref[...]
        o_ref[:, pl.ds(0, ATTN_DIM)] = dq.astype(BF)
        o_ref[:, pl.ds(ATTN_DIM, KV_DIM)] = dk.astype(BF)
        o_ref[:, pl.ds(ATTN_DIM + KV_DIM, KV_DIM)] = dvv.astype(BF)
        sq, sk, sv = (jnp.sum(t, axis=0, keepdims=True) for t in (dq, dk, dvv))

        @pl.when(i == 0)
        def _():
            db_ref[:, pl.ds(0, ATTN_DIM)] = sq
            db_ref[:, pl.ds(ATTN_DIM, KV_DIM)] = sk
            db_ref[:, pl.ds(ATTN_DIM + KV_DIM, KV_DIM)] = sv
            dqg_ref[...] = dqg
            dkg_ref[...] = dkg

        @pl.when(i > 0)
        def _():
            db_ref[:, pl.ds(0, ATTN_DIM)] += sq
            db_ref[:, pl.ds(ATTN_DIM, KV_DIM)] += sk
            db_ref[:, pl.ds(ATTN_DIM + KV_DIM, KV_DIM)] += sv
            dqg_ref[...] += dqg
            dkg_ref[...] += dkg

    full = lambda shape: pl.BlockSpec(shape, lambda i: (0, 0))
    row = lambda n: pl.BlockSpec((tm, n), lambda i: (i, 0))
    return pl.pallas_call(
        body, name="qk_norm_bwd", grid=(T // tm,),
        in_specs=[row(QKV_DIM), row(ATTN_DIM), row(KV_DIM), row(KV_DIM), full((1, ATTN_DIM)), full((1, KV_DIM)),
                  full((HEAD_GROUP, HEAD_GROUP))],
        out_specs=[row(QKV_DIM), full((1, QKV_DIM)), full((1, ATTN_DIM)), full((1, KV_DIM))],
        out_shape=[jax.ShapeDtypeStruct((T, QKV_DIM), BF), jax.ShapeDtypeStruct((1, QKV_DIM), F32),
                   jax.ShapeDtypeStruct((1, ATTN_DIM), F32), jax.ShapeDtypeStruct((1, KV_DIM), F32)],
        compiler_params=_params(1),
    )(qkv, dqn, dkn, dv, qg_t, kg_t, _head_ones())


ROWS = 128
COLS = 128


SUBLANES = 8
FIRST_TAP = HALO - (CONV_W - 1)


def _glu(a, g):
    return a.astype(F32) * jax.nn.sigmoid(g.astype(F32))


def _shifted(xe, s):
    return xe if s == 0 else pltpu.roll(xe, ROWS + HALO - s, axis=0)


def _conv_fwd(u, dw_pad, dw_b, ln_g, ln_b, tm=512):
    T = u.shape[0]
    Dm = D_MODEL
    hpt = tm // HALO

    def body(ac_ref, gc_ref, ap_ref, gp_ref, w_ref, wb_ref, lg_ref, lb_ref, cv_ref, s_ref, ext):
        i = pl.program_id(0)
        ext[pl.ds(0, HALO), :] = jnp.where(i > 0, _glu(ap_ref[...], gp_ref[...]), 0.0)
        ext[pl.ds(HALO, tm), :] = _glu(ac_ref[...], gc_ref[...])

        def rows(r, carry):
            r0 = pl.multiple_of(r * ROWS, ROWS)
            for c in range(Dm // COLS):
                cs = pl.ds(c * COLS, COLS)
                xe = ext[pl.ds(r0, ROWS + HALO), cs]
                acc = jnp.zeros((ROWS, COLS), F32)
                for s in range(SUBLANES):
                    xs = _shifted(xe, s)
                    for j in range(CONV_W):
                        off = FIRST_TAP + j
                        if off % SUBLANES == s:
                            acc = acc + xs[off - s:off - s + ROWS, :] * w_ref[pl.ds(j, 1), cs]
                cv_ref[pl.ds(r0, ROWS), cs] = acc + wb_ref[:, cs]
            return carry

        lax.fori_loop(0, tm // ROWS, rows, 0)
        cv = cv_ref[...]
        xc = cv - jnp.mean(cv, axis=-1, keepdims=True)
        y = xc * lax.rsqrt(jnp.mean(xc * xc, axis=-1, keepdims=True) + NORM_EPS) * lg_ref[...] + lb_ref[...]
        s_ref[...] = (y * jax.nn.sigmoid(y)).astype(BF)

    full = lambda shape: pl.BlockSpec(shape, lambda i: (0, 0))
    return pl.pallas_call(
        body, name="conv_fwd", grid=(T // tm,),
        in_specs=[pl.BlockSpec((tm, Dm), lambda i: (i, 0)), pl.BlockSpec((tm, Dm), lambda i: (i, 1)),
                  pl.BlockSpec((HALO, Dm), lambda i: (jnp.maximum(i * hpt - 1, 0), 0)),
                  pl.BlockSpec((HALO, Dm), lambda i: (jnp.maximum(i * hpt - 1, 0), 1)),
                  full((HALO, Dm)), full((1, Dm)), full((1, Dm)), full((1, Dm))],
        out_specs=[pl.BlockSpec((tm, Dm), lambda i: (i, 0)), pl.BlockSpec((tm, Dm), lambda i: (i, 0))],
        out_shape=[jax.ShapeDtypeStruct((T, Dm), F32), jax.ShapeDtypeStruct((T, Dm), BF)],
        scratch_shapes=[pltpu.VMEM((tm + HALO, Dm), F32)],
        compiler_params=_params(1),
    )(u, u, u, u, dw_pad, dw_b, ln_g, ln_b)


def _ln_silu_bwd_ep(ds, cv, lg, lb):
    xc = cv - jnp.mean(cv, axis=-1, keepdims=True)
    rstd = lax.rsqrt(jnp.mean(xc * xc, axis=-1, keepdims=True) + NORM_EPS)
    xh = xc * rstd
    y = xh * lg + lb
    sg = jax.nn.sigmoid(y)
    dy = ds * (sg * (1.0 + y * (1.0 - sg)))
    dxh = dy * lg
    dcv = rstd * (dxh - jnp.mean(dxh, axis=-1, keepdims=True) - xh * jnp.mean(dxh * xh, axis=-1, keepdims=True))
    return (dcv, jnp.sum(dy * xh, axis=0, keepdims=True), jnp.sum(dy, axis=0, keepdims=True),
            jnp.sum(dcv, axis=0, keepdims=True))


def _conv_bwd(u, dcv, dw_pad, tm=512):
    T = u.shape[0]
    Dm = D_MODEL
    hpt = tm // HALO
    last = T // HALO - 1
    nt = T // tm

    def body(ac_ref, gc_ref, ap_ref, gp_ref, dc_ref, dn_ref, w_ref, du_ref, db_ref, dw_ref, ext_g, ext_d):
        i = pl.program_id(0)
        ext_g[pl.ds(0, HALO), :] = jnp.where(i > 0, _glu(ap_ref[...], gp_ref[...]), 0.0)
        ext_g[pl.ds(HALO, tm), :] = _glu(ac_ref[...], gc_ref[...])
        ext_d[pl.ds(0, tm), :] = dc_ref[...]
        ext_d[pl.ds(tm, HALO), :] = jnp.where(i < nt - 1, dn_ref[...], 0.0)

        @pl.when(i == 0)
        def _():
            db_ref[...] = jnp.zeros_like(db_ref)
            dw_ref[...] = jnp.zeros_like(dw_ref)

        def rows(r, carry):
            r0 = pl.multiple_of(r * ROWS, ROWS)
            rs = pl.ds(r0, ROWS)
            for c in range(Dm // COLS):
                cs = pl.ds(c * COLS, COLS)
                cs2 = pl.ds(Dm + c * COLS, COLS)
                de = ext_d[pl.ds(r0, ROWS + HALO), cs]
                ge = ext_g[pl.ds(r0, ROWS + HALO), cs]
                dcur = de[0:ROWS, :]
                acc = jnp.zeros((ROWS, COLS), F32)
                for s in range(SUBLANES):
                    ds_, gs_ = _shifted(de, s), _shifted(ge, s)
                    for j in range(CONV_W):
                        off = CONV_W - 1 - j
                        if off % SUBLANES == s:
                            acc = acc + ds_[off - s:off - s + ROWS, :] * w_ref[pl.ds(j, 1), cs]
                        goff = FIRST_TAP + j
                        if goff % SUBLANES == s:
                            prod = dcur * gs_[goff - s:goff - s + ROWS, :]
                            dw_ref[j, :, cs] += jnp.sum(prod.reshape(ROWS // SUBLANES, SUBLANES, COLS), axis=0)
                a = ac_ref[rs, cs].astype(F32)
                sg = jax.nn.sigmoid(gc_ref[rs, cs].astype(F32))
                da = acc * sg
                dg = acc * a * sg * (1.0 - sg)
                du_ref[rs, cs] = da.astype(BF)
                du_ref[rs, cs2] = dg.astype(BF)
                db_ref[:, cs] += jnp.sum(da, axis=0, keepdims=True)
                db_ref[:, cs2] += jnp.sum(dg, axis=0, keepdims=True)
            return carry

        lax.fori_loop(0, tm // ROWS, rows, 0)

    return pl.pallas_call(
        body, name="conv_bwd", grid=(nt,),
        in_specs=[pl.BlockSpec((tm, Dm), lambda i: (i, 0)), pl.BlockSpec((tm, Dm), lambda i: (i, 1)),
                  pl.BlockSpec((HALO, Dm), lambda i: (jnp.maximum(i * hpt - 1, 0), 0)),
                  pl.BlockSpec((HALO, Dm), lambda i: (jnp.maximum(i * hpt - 1, 0), 1)),
                  pl.BlockSpec((tm, Dm), lambda i: (i, 0)),
                  pl.BlockSpec((HALO, Dm), lambda i: (jnp.minimum((i + 1) * hpt, last), 0)),
                  pl.BlockSpec((HALO, Dm), lambda i: (0, 0))],
        out_specs=[pl.BlockSpec((tm, 2 * Dm), lambda i: (i, 0)), pl.BlockSpec((1, 2 * Dm), lambda i: (0, 0)),
                   pl.BlockSpec((HALO, 8, Dm), lambda i: (0, 0, 0))],
        out_shape=[jax.ShapeDtypeStruct((T, 2 * Dm), BF), jax.ShapeDtypeStruct((1, 2 * Dm), F32),
                   jax.ShapeDtypeStruct((HALO, 8, Dm), F32)],
        scratch_shapes=[pltpu.VMEM((tm + HALO, Dm), F32), pltpu.VMEM((tm + HALO, Dm), F32)],
        compiler_params=_params(1),
    )(u, u, u, u, dcv, dcv, dw_pad)


def _bucket_table():
    q_loc = np.arange(BLOCK)[:, None]
    k_loc = np.arange(2 * BLOCK)[None, :]
    dist = q_loc + BLOCK - k_loc
    n = np.maximum(dist, 0)
    max_exact = REL_BUCKETS // 2
    large = max_exact + (np.log(np.maximum(n, 1).astype(np.float32) / max_exact)
                         / math.log(REL_MAX_DIST / max_exact) * (REL_BUCKETS - max_exact)).astype(np.int32)
    large = np.minimum(large, REL_BUCKETS - 1)
    bucket = np.where(n < max_exact, n, large).astype(np.int32)
    return jnp.asarray(np.where((dist >= 0) & (dist < BLOCK), bucket, -1).astype(np.int32))


def _bias_table(rel_bias, bucket):
    def body(rb_ref, bk_ref, o_ref):
        bk = bk_ref[...]
        for h in range(N_HEADS):
            acc = jnp.full((BLOCK, 2 * BLOCK), NEG_INF, F32)
            for b in range(REL_BUCKETS):
                acc = jnp.where(bk == b, rb_ref[b, h], acc)
            o_ref[h] = acc

    return pl.pallas_call(
        body, name="bias_table", out_shape=jax.ShapeDtypeStruct((N_HEADS, BLOCK, 2 * BLOCK), F32),
        in_specs=[pl.BlockSpec(memory_space=pltpu.SMEM), pl.BlockSpec(memory_space=pltpu.VMEM)],
        out_specs=pl.BlockSpec(memory_space=pltpu.VMEM),
    )(rel_bias, bucket)


def _bias_grad(dbias, bucket):
    def body(db_ref, bk_ref, o_ref):
        bk = bk_ref[...]
        for b in range(REL_BUCKETS):
            sel = bk == b
            for h in range(N_HEADS):
                o_ref[b, h] = jnp.sum(jnp.where(sel, db_ref[h], 0.0))

    return pl.pallas_call(
        body, name="bias_grad", out_shape=jax.ShapeDtypeStruct((REL_BUCKETS, N_HEADS), F32),
        in_specs=[pl.BlockSpec(memory_space=pltpu.VMEM), pl.BlockSpec(memory_space=pltpu.VMEM)],
        out_specs=pl.BlockSpec(memory_space=pltpu.SMEM),
    )(dbias, bucket)


GROUP_ROWS = GROUP * BLOCK


def _head_probs(qk, bias_h, sink, first):
    s = jnp.where(first, NEG_INF, qk + bias_h)
    m = jnp.maximum(jnp.max(s, axis=-1, keepdims=True), sink)
    p = jnp.exp(s - m)
    ps = jnp.exp(sink - m)
    inv = 1.0 / (jnp.sum(p, axis=-1, keepdims=True) + ps)
    return p * inv, ps * inv


def _band(prev_ref, cur_ref, g):
    hs = pl.ds(g * HEAD_DIM, HEAD_DIM)
    return jnp.concatenate([prev_ref[:, hs], cur_ref[:, hs]], axis=0)


def _stack_heads(ref, g):
    return jnp.concatenate([ref[:, pl.ds((g * GROUP + hh) * HEAD_DIM, HEAD_DIM)] for hh in range(GROUP)], axis=0)


def _unstack_heads(ref, g, stacked, dtype):
    for hh in range(GROUP):
        ref[:, pl.ds((g * GROUP + hh) * HEAD_DIM, HEAD_DIM)] = stacked[hh * BLOCK:(hh + 1) * BLOCK, :].astype(dtype)


def _first_mask(n):
    col = lax.broadcasted_iota(jnp.int32, (1, 2 * BLOCK), 1)
    return jnp.logical_and(n == 0, col < BLOCK)


def _head_rows(hh):
    return pl.ds(hh * BLOCK, BLOCK)


def _attn_fwd(qn, kn, vv, bias, sinks):
    T = qn.shape[0]
    nb = T // BLOCK

    def body(sk_ref, q_ref, kc_ref, kp_ref, vc_ref, vp_ref, b_ref, o_ref, qk_buf, p_buf):
        first = _first_mask(pl.program_id(0))
        for g in range(N_KV):
            qk_buf[g] = _dot(_stack_heads(q_ref, g), _band(kp_ref, kc_ref, g), 1, 1)
        for g in range(N_KV):
            for hh in range(GROUP):
                h = g * GROUP + hh
                pn, _ = _head_probs(qk_buf[g, _head_rows(hh), :], b_ref[h], sk_ref[h], first)
                p_buf[g, _head_rows(hh), :] = pn.astype(BF)
        for g in range(N_KV):
            _unstack_heads(o_ref, g, _dot(p_buf[g], _band(vp_ref, vc_ref, g), 1, 0), BF)

    cur = lambda n: (n, 0)
    prev = lambda n: (jnp.maximum(n - 1, 0), 0)
    return pl.pallas_call(
        body, name="attn_fwd", grid=(nb,),
        in_specs=[pl.BlockSpec(memory_space=pltpu.SMEM), pl.BlockSpec((BLOCK, ATTN_DIM), cur),
                  pl.BlockSpec((BLOCK, KV_DIM), cur), pl.BlockSpec((BLOCK, KV_DIM), prev),
                  pl.BlockSpec((BLOCK, KV_DIM), cur), pl.BlockSpec((BLOCK, KV_DIM), prev),
                  pl.BlockSpec((N_HEADS, BLOCK, 2 * BLOCK), lambda n: (0, 0, 0))],
        out_specs=pl.BlockSpec((BLOCK, ATTN_DIM), cur), out_shape=jax.ShapeDtypeStruct((T, ATTN_DIM), BF),
        scratch_shapes=[pltpu.VMEM((N_KV, GROUP_ROWS, 2 * BLOCK), F32), pltpu.VMEM((N_KV, GROUP_ROWS, 2 * BLOCK), BF)],
        compiler_params=_params(1),
    )(sinks, qn, kn, kn, vv, vv, bias)


def _attn_bwd(qn, kn, vv, bias, sinks, do):
    T = qn.shape[0]
    nb = T // BLOCK
    scale = 1.0 / math.sqrt(HEAD_DIM)

    def body(sk_ref, q_ref, kc_ref, kp_ref, vc_ref, vp_ref, b_ref, do_ref,
             dq_ref, dk_ref, dv_ref, db_ref, dsk_ref, dk_full, dv_full, dk_carry, dv_carry, qk_buf, dp_buf, p_buf, ds_buf):
        n = pl.program_id(0)

        @pl.when(n == 0)
        def _():
            db_ref[...] = jnp.zeros_like(db_ref)
            dk_carry[...] = jnp.zeros_like(dk_carry)
            dv_carry[...] = jnp.zeros_like(dv_carry)
            for h in range(N_HEADS):
                dsk_ref[h] = 0.0

        @pl.when(n < nb)
        def _():
            first = _first_mask(n)
            ks = [_band(kp_ref, kc_ref, g) for g in range(N_KV)]
            qs = [_stack_heads(q_ref, g) for g in range(N_KV)]
            douts = [_stack_heads(do_ref, g) for g in range(N_KV)]
            for g in range(N_KV):
                qk_buf[g] = _dot(qs[g], ks[g], 1, 1)
                dp_buf[g] = _dot(douts[g], _band(vp_ref, vc_ref, g), 1, 1)
            for g in range(N_KV):
                for hh in range(GROUP):
                    h = g * GROUP + hh
                    rows = _head_rows(hh)
                    pn, psink = _head_probs(qk_buf[g, rows, :], b_ref[h], sk_ref[h], first)
                    dp = dp_buf[g, rows, :]
                    delta = jnp.sum(pn * dp, axis=-1, keepdims=True)
                    ds = pn * (dp - delta)
                    dsk_ref[h] += -jnp.sum(psink * delta)
                    db_ref[h] += ds
                    ds_buf[g, rows, :] = ds.astype(BF)
                    p_buf[g, rows, :] = pn.astype(BF)
            for g in range(N_KV):
                dsb = ds_buf[g]
                _unstack_heads(dq_ref, g, _dot(dsb, ks[g], 1, 0) * scale, F32)
                gs = pl.ds(g * HEAD_DIM, HEAD_DIM)
                dk_full[:, gs] = _dot(dsb, qs[g], 0, 0)
                dv_full[:, gs] = _dot(p_buf[g], douts[g], 0, 0)

        @pl.when(n == nb)
        def _():
            dk_full[...] = jnp.zeros_like(dk_full)
            dv_full[...] = jnp.zeros_like(dv_full)

        dk_ref[...] = dk_carry[...] + dk_full[pl.ds(0, BLOCK), :]
        dv_ref[...] = dv_carry[...] + dv_full[pl.ds(0, BLOCK), :]
        dk_carry[...] = dk_full[pl.ds(BLOCK, BLOCK), :]
        dv_carry[...] = dv_full[pl.ds(BLOCK, BLOCK), :]

    cur = lambda n: (jnp.minimum(n, nb - 1), 0)
    prev = lambda n: (jnp.maximum(jnp.minimum(n, nb - 1) - 1, 0), 0)
    out_kv = lambda n: (jnp.maximum(n - 1, 0), 0)
    return pl.pallas_call(
        body, name="attn_bwd", grid=(nb + 1,),
        in_specs=[pl.BlockSpec(memory_space=pltpu.SMEM), pl.BlockSpec((BLOCK, ATTN_DIM), cur),
                  pl.BlockSpec((BLOCK, KV_DIM), cur), pl.BlockSpec((BLOCK, KV_DIM), prev),
                  pl.BlockSpec((BLOCK, KV_DIM), cur), pl.BlockSpec((BLOCK, KV_DIM), prev),
                  pl.BlockSpec((N_HEADS, BLOCK, 2 * BLOCK), lambda n: (0, 0, 0)),
                  pl.BlockSpec((BLOCK, ATTN_DIM), cur)],
        out_specs=[pl.BlockSpec((BLOCK, ATTN_DIM), cur), pl.BlockSpec((BLOCK, KV_DIM), out_kv),
                   pl.BlockSpec((BLOCK, KV_DIM), out_kv),
                   pl.BlockSpec((N_HEADS, BLOCK, 2 * BLOCK), lambda n: (0, 0, 0)),
                   pl.BlockSpec(memory_space=pltpu.SMEM)],
        out_shape=[jax.ShapeDtypeStruct((T, ATTN_DIM), F32), jax.ShapeDtypeStruct((T, KV_DIM), F32),
                   jax.ShapeDtypeStruct((T, KV_DIM), F32),
                   jax.ShapeDtypeStruct((N_HEADS, BLOCK, 2 * BLOCK), F32), jax.ShapeDtypeStruct((N_HEADS,), F32)],
        scratch_shapes=[pltpu.VMEM((2 * BLOCK, KV_DIM), F32), pltpu.VMEM((2 * BLOCK, KV_DIM), F32),
                        pltpu.VMEM((BLOCK, KV_DIM), F32), pltpu.VMEM((BLOCK, KV_DIM), F32),
                        pltpu.VMEM((N_KV, GROUP_ROWS, 2 * BLOCK), F32), pltpu.VMEM((N_KV, GROUP_ROWS, 2 * BLOCK), F32),
                        pltpu.VMEM((N_KV, GROUP_ROWS, 2 * BLOCK), BF), pltpu.VMEM((N_KV, GROUP_ROWS, 2 * BLOCK), BF)],
        compiler_params=_params(1),
    )(sinks, qn, kn, kn, vv, vv, bias, do)


def _coords():
    return lax.axis_index("x"), lax.axis_index("y"), lax.axis_index("c")


def _sum8(name, blocks):
    def body(b_ref, o_ref):
        tot = b_ref[0]
        for d in range(1, 8):
            tot = tot + b_ref[d]
        o_ref[...] = tot

    return pl.pallas_call(body, name=name, out_shape=jax.ShapeDtypeStruct(blocks.shape[1:], F32))(blocks)


HBM_SPEC = pl.BlockSpec(memory_space=pltpu.HBM)
SEM_SPEC = pl.BlockSpec(memory_space=pltpu.SEMAPHORE)
ANY_SPEC = pl.BlockSpec(memory_space=pl.ANY)
DATAFLOW = pltpu.SideEffectType.DATAFLOW_SIDE_EFFECTING


OTHER_CHIPS = (4, 2, 6)
ALL_OTHERS = (1, 2, 3, 4, 5, 6, 7)


def _slot(x, y, c, peers):
    return 2 * x + y if peers is OTHER_CHIPS else 4 * x + 2 * y + c


def _slot_copy(land, sems, idx, x, y, c, k, peers, arriving):
    send_sems, recv_sems = sems
    px, py, pc = x ^ (k >> 2), y ^ ((k >> 1) & 1), c ^ (k & 1)
    mine = _slot(x, y, c, peers)
    dst = _slot(px, py, pc, peers) if arriving else mine
    return pltpu.make_async_remote_copy(src_ref=land.at[mine], dst_ref=land.at[dst], send_sem=send_sems.at[idx],
                                        recv_sem=recv_sems.at[idx], device_id=(px, py, pc), device_id_type=MESH)


def _gather_start(name, stacks, groups, peers, after):
    n = len(stacks)
    ng = len(groups)
    np_ = len(peers)
    after = tuple(after)

    def body(*refs):
        lands = refs[:n]
        first = n + len(after)
        sems = [(refs[first + 2 * g], refs[first + 2 * g + 1]) for g in range(ng)]
        token = refs[-1]
        x, y, c = _coords()
        for g, members in enumerate(groups):
            for i, t in enumerate(members):
                for j, k in enumerate(peers):
                    _slot_copy(lands[t], sems[g], np_ * i + j, x, y, c, k, peers, arriving=False).start()
        token[...] = jnp.zeros_like(token)

    out_shape = []
    for members in groups:
        out_shape += [pltpu.SemaphoreType.DMA((np_ * len(members),))] * 2
    out_shape += [pltpu.HBM(w.shape, w.dtype) for w in stacks]
    out_shape.append(jax.ShapeDtypeStruct((8, 128), F32))
    res = pl.pallas_call(
        body, name=name, out_shape=out_shape, in_specs=[HBM_SPEC] * n + [ANY_SPEC] * len(after),
        out_specs=[SEM_SPEC] * (2 * ng) + [HBM_SPEC] * n + [pl.BlockSpec(memory_space=pltpu.VMEM)],
        input_output_aliases={t: 2 * ng + t for t in range(n)},
        compiler_params=pltpu.CompilerParams(has_side_effects=DATAFLOW),
    )(*[pltpu.with_memory_space_constraint(w, pltpu.HBM) for w in stacks], *after)
    sems = [(res[2 * g], res[2 * g + 1]) for g in range(ng)]
    return sems, list(res[2 * ng:2 * ng + n]), res[-1]


def _gather_wait(name, stacks, sems, peers, after):
    n = len(stacks)
    after = tuple(after)

    def body(*refs):
        lands = refs[:n]
        group_sems = (refs[n], refs[n + 1])
        x, y, c = _coords()
        for i in range(n):
            for j, k in enumerate(peers):
                cp = _slot_copy(lands[i], group_sems, len(peers) * i + j, x, y, c, k, peers, arriving=True)
                cp.wait_send()
                cp.wait_recv()

    return pl.pallas_call(
        body, name=name, out_shape=[pltpu.HBM(w.shape, w.dtype) for w in stacks],
        in_specs=[HBM_SPEC] * n + [SEM_SPEC, SEM_SPEC] + [ANY_SPEC] * len(after), out_specs=[HBM_SPEC] * n,
        input_output_aliases={t: t for t in range(n)},
        compiler_params=pltpu.CompilerParams(has_side_effects=DATAFLOW),
    )(*stacks, sems[0], sems[1], *after)


N_PEERS = 7


def _peer(x, y, c, k):
    return x ^ (k >> 2), y ^ ((k >> 1) & 1), c ^ (k & 1)


def _reduce_copy(grad, land, sems, idx, x, y, c, k):
    px, py, pc = _peer(x, y, c, k)
    rh = grad.shape[1] // 2
    return pltpu.make_async_remote_copy(src_ref=grad.at[2 * px + py, pl.ds(pc * rh, rh), :], dst_ref=land.at[k - 1],
                                        send_sem=sems[0].at[idx], recv_sem=sems[1].at[idx], device_id=(px, py, pc),
                                        device_id_type=MESH)


def _reduce_start(name, grads):
    n = len(grads)

    def body(*refs):
        src, lands, sems, token = refs[:n], refs[n:2 * n], (refs[2 * n], refs[2 * n + 1]), refs[-1]
        x, y, c = _coords()
        for t in range(n):
            for k in range(1, N_PEERS + 1):
                _reduce_copy(src[t], lands[t], sems, N_PEERS * t + k - 1, x, y, c, k).start()
        token[...] = jnp.zeros_like(token)

    lands = [lax.empty((N_PEERS, g.shape[1] // 2, g.shape[2]), g.dtype) for g in grads]
    out_shape = [pltpu.SemaphoreType.DMA((N_PEERS * n,))] * 2
    out_shape += [pltpu.HBM(a.shape, a.dtype) for a in list(grads) + lands]
    out_shape.append(jax.ShapeDtypeStruct((8, 128), F32))
    res = pl.pallas_call(
        body, name=name, out_shape=out_shape, in_specs=[HBM_SPEC] * (2 * n),
        out_specs=[SEM_SPEC] * 2 + [HBM_SPEC] * (2 * n) + [pl.BlockSpec(memory_space=pltpu.VMEM)],
        input_output_aliases={t: 2 + t for t in range(2 * n)},
        compiler_params=pltpu.CompilerParams(has_side_effects=DATAFLOW),
    )(*[pltpu.with_memory_space_constraint(a, pltpu.HBM) for a in list(grads) + lands])
    return (res[0], res[1]), list(res[2:2 + n]), list(res[2 + n:2 + 2 * n]), res[-1]


def _reduce_wait(name, grads, lands, sems, after):
    n = len(grads)
    after = tuple(after)

    def body(*refs):
        src, dst, group_sems = refs[:n], refs[n:2 * n], (refs[2 * n], refs[2 * n + 1])
        x, y, c = _coords()
        for t in range(n):
            for k in range(1, N_PEERS + 1):
                cp = _reduce_copy(src[t], dst[t], group_sems, N_PEERS * t + k - 1, x, y, c, k)
                cp.wait_send()
                cp.wait_recv()

    res = pl.pallas_call(
        body, name=name, out_shape=[pltpu.HBM(a.shape, a.dtype) for a in list(grads) + list(lands)],
        in_specs=[HBM_SPEC] * (2 * n) + [SEM_SPEC, SEM_SPEC] + [ANY_SPEC] * len(after), out_specs=[HBM_SPEC] * (2 * n),
        input_output_aliases={t: t for t in range(2 * n)},
        compiler_params=pltpu.CompilerParams(has_side_effects=DATAFLOW),
    )(*grads, *lands, sems[0], sems[1], *after)
    return list(res[:n]), list(res[n:])


def _join_halves(name, halves, deps=()):
    n = len(halves)

    def body(*refs):
        src, dst = refs[:n], refs[n + len(deps):2 * n + len(deps)]
        send_sems, recv_sems = refs[-2:]
        x, y, c = _coords()
        cps = []
        for t in range(n):
            cp = pltpu.make_async_remote_copy(src_ref=src[t], dst_ref=dst[t], send_sem=send_sems.at[t],
                                              recv_sem=recv_sems.at[t], device_id=(x, y, 1 - c), device_id_type=MESH)
            cp.start()
            cps.append(cp)
        for cp in cps:
            cp.wait()

    anyspec = pl.BlockSpec(memory_space=pl.ANY)
    return pl.pallas_call(
        body, name=name, out_shape=[jax.ShapeDtypeStruct(h.shape, h.dtype) for h in halves],
        in_specs=[anyspec] * (n + len(deps)), out_specs=[anyspec] * n,
        scratch_shapes=[pltpu.SemaphoreType.DMA((n,)), pltpu.SemaphoreType.DMA((n,))],
    )(*halves, *deps)


def _row_block(rows):
    for rb in (512, 256, 128, 64, 32, 16):
        if rows % rb == 0:
            return rb
    raise ValueError(rows)


def _sum_devices(name, grad, land, place):
    S, R, C = grad.shape
    rh = R // 2
    rb = _row_block(rh)
    nbh = rh // rb

    def body(place_ref, g_ref, l_ref, o_ref):
        tot = g_ref[...].astype(F32)
        for k in range(N_PEERS):
            tot = tot + l_ref[k].astype(F32)
        o_ref[...] = tot

    return pl.pallas_call(
        body, name=name,
        grid_spec=pltpu.PrefetchScalarGridSpec(
            num_scalar_prefetch=1, grid=(nbh,),
            in_specs=[pl.BlockSpec((None, rb, C), lambda r, place: (place[0], place[1] * nbh + r, 0)),
                      pl.BlockSpec((N_PEERS, rb, C), lambda r, place: (0, r, 0))],
            out_specs=pl.BlockSpec((rb, C), lambda r, place: (r, 0))),
        out_shape=jax.ShapeDtypeStruct((rh, C), F32), compiler_params=_params(1),
    )(place, grad, land)


def _adamw_math(w, g, m, v):
    m2 = ADAM_B1 * m + (1.0 - ADAM_B1) * g
    v2 = ADAM_B2 * v + (1.0 - ADAM_B2) * (g * g)
    m_hat = m2 / (1.0 - ADAM_B1 ** ADAM_STEP)
    v_hat = v2 / (1.0 - ADAM_B2 ** ADAM_STEP)
    delta = -ADAM_LR * (m_hat / (jnp.sqrt(v_hat) + ADAM_EPS) + ADAM_WD * w)
    return delta, m2, v2


def _adamw(name, w, m, v, gs):
    L, R, C = w.shape
    Rh = R // 2
    rb = _row_block(Rh)
    nbh = Rh // rb
    assert len(gs) == L

    def body(core_ref, w_ref, m_ref, v_ref, *rest):
        g_refs, (go_ref, d_ref, m2_ref, v2_ref) = rest[:2 * L], rest[2 * L:]
        layer, half = pl.program_id(0), pl.program_id(1)
        mine = half == core_ref[0]
        g = jnp.where(mine, g_refs[0][...], g_refs[1][...])
        for t in range(1, L):
            g = jnp.where(layer == t, jnp.where(mine, g_refs[2 * t][...], g_refs[2 * t + 1][...]), g)
        delta, m2, v2 = _adamw_math(w_ref[...], g, m_ref[...], v_ref[...])
        go_ref[...] = g
        d_ref[...] = delta
        m2_ref[...] = m2
        v2_ref[...] = v2

    wspec = pl.BlockSpec((None, rb, C), lambda l, h, r, core: (l, h * nbh + r, 0))
    gspec = pl.BlockSpec((rb, C), lambda l, h, r, core: (r, 0))
    return pl.pallas_call(
        body, name=name,
        grid_spec=pltpu.PrefetchScalarGridSpec(num_scalar_prefetch=1, grid=(L, 2, nbh),
                                               in_specs=[wspec] * 3 + [gspec] * (2 * L), out_specs=[wspec] * 4),
        out_shape=[jax.ShapeDtypeStruct((L, R, C), F32)] * 4, compiler_params=_params(3),
    )(lax.axis_index("c").astype(jnp.int32).reshape(1), w, m, v, *[g for pair in gs for g in pair])


def _adamw_small(ws, gs, ms, vs):
    n = len(ws)

    def body(*refs):
        w_refs, g_refs, m_refs, v_refs = (refs[k * n:(k + 1) * n] for k in range(4))
        d_refs, m2_refs, v2_refs = (refs[(4 + k) * n:(5 + k) * n] for k in range(3))
        for t in range(n):
            delta, m2, v2 = _adamw_math(w_refs[t][...], g_refs[t][...], m_refs[t][...], v_refs[t][...])
            d_refs[t][...] = delta
            m2_refs[t][...] = m2
            v2_refs[t][...] = v2

    res = pl.pallas_call(body, name="adamw_small", out_shape=[jax.ShapeDtypeStruct(w.shape, F32) for w in ws] * 3)(
        *ws, *gs, *ms, *vs)
    return res[:n], res[n:2 * n], res[2 * n:]


def _packed_rows(shape):
    c = shape[-1]
    return (int(np.prod(shape)) // c) * -(-c // LANES)


def _pack(arrays):
    total = sum(_packed_rows(a.shape) for a in arrays)
    total += -total % 8
    buf, r0 = None, 0
    for a in arrays:
        a = a.astype(F32).reshape(-1, a.shape[-1])
        r, c = a.shape
        k = -(-c // LANES)
        a = jnp.pad(a, ((0, 0), (0, k * LANES - c))).reshape(r * k, LANES)
        a = jnp.pad(a, ((r0, total - r0 - r * k), (0, 0)))
        buf = a if buf is None else buf + a
        r0 += r * k
    return buf


def _unpack(buf, shapes):
    out, r0 = [], 0
    for shp in shapes:
        c = shp[-1]
        rows = _packed_rows(shp)
        out.append(buf[r0:r0 + rows].reshape(-1, -(-c // LANES) * LANES)[:, :c].reshape(shp))
        r0 += rows
    return out


def _rms(x, g):
    return x * lax.rsqrt(jnp.mean(x * x, axis=-1, keepdims=True) + NORM_EPS) * g


def _residual_norm_ep(acc, *rest):
    *bias, res, gain = rest
    x = acc + res + (bias[0] if bias else 0.0)
    return x, _rms(x, gain)


RESIDUAL_NORM_OUTS = (("tile", F32), ("tile", BF))


def _mlp_up(tag, h, w_up_sm):
    (up,) = _mm(f"mlp{tag}_up", h, w_up_sm, nt=False, b_sm=True, tm=2048, tn=1024, rows=256,
                ep_fn=lambda acc: (acc,), outs=(("tile", BF),))
    return up


RMS_BWD_OUTS = (("tile", F32), ("tile", BF), ("colsum", F32), ("colsum", F32))


def _mlp_bwd(tag, dy, dy_bf, x, g, up, w_up_sm, w_down):
    (dup,) = _mm(f"mlp{tag}_dup", dy_bf, w_down, nt=True, tm=2048, tn=1024, rows=256, ep_in=((up, "tile"),),
                 ep_fn=lambda acc, u: (acc * (2.0 * jnp.maximum(u.astype(F32), 0.0)),), outs=(("tile", BF),))
    dx, dx_bf, dg, dx_sum = _mm(f"mlp{tag}_dx", dup, w_up_sm, nt=True, b_sm=True, tm=512, tn=1024, rows=256,
                                ep_in=((x, "tile"), (g, "row"), (dy, "tile")), ep_fn=_rms_bwd_ep, outs=RMS_BWD_OUTS)
    return dx, dx_bf, dg, dx_sum, dup


class _Reduction:
    def __init__(self, tag, grads, place):
        self.tag, self.place = tag, place
        self.sems, self.grads, self.lands, self.token = _reduce_start(f"reduce_start_{tag}", grads)

    def finish(self, after):
        grads, lands = _reduce_wait(f"reduce_wait_{self.tag}", self.grads, self.lands, self.sems, after)
        return [_sum_devices(f"reduce_sum_{self.tag}{i}", g, l, self.place) for i, (g, l) in enumerate(zip(grads, lands))]


def kernel(x, conv_norm_g, conv_w_in, conv_b_in, conv_dw, conv_dw_b, conv_ln_g, conv_ln_b, conv_w_out, conv_b_out, attn_norm_g, w_qkv, b_qkv, q_norm_g, k_norm_g, sinks, w_o, b_o, rel_bias, mlp_norm_g, w_up, w_down, loss_target, m_conv_norm_g, m_conv_w_in, m_conv_b_in, m_conv_dw, m_conv_dw_b, m_conv_ln_g, m_conv_ln_b, m_conv_w_out, m_conv_b_out, m_attn_norm_g, m_w_qkv, m_b_qkv, m_q_norm_g, m_k_norm_g, m_sinks, m_w_o, m_b_o, m_rel_bias, m_mlp_norm_g, m_w_up, m_w_down, v_conv_norm_g, v_conv_w_in, v_conv_b_in, v_conv_dw, v_conv_dw_b, v_conv_ln_g, v_conv_ln_b, v_conv_w_out, v_conv_b_out, v_attn_norm_g, v_w_qkv, v_b_qkv, v_q_norm_g, v_k_norm_g, v_sinks, v_w_o, v_b_o, v_rel_bias, v_mlp_norm_g, v_w_up, v_w_down):
    Dm = D_MODEL
    x2d = x[0]
    tgt = loss_target[0]
    T = x2d.shape[0]
    shard = 2 * lax.axis_index("x") + lax.axis_index("y")

    me = 2 * shard + lax.axis_index("c")

    def own_slot(block, slots, index):
        return lax.dynamic_update_slice(lax.empty((slots,) + block.shape, block.dtype), block[None],
                                        (index,) + (0,) * block.ndim)

    sharded_small = [conv_dw[0], attn_norm_g, b_qkv, b_o]
    (small_sems,), (small_land,), small_token = _gather_start(
        "small_weights_start", [own_slot(_pack(sharded_small), 8, me)], ((0,),), ALL_OTHERS, after=())

    big = [conv_w_in[0], conv_w_out[0], w_qkv[0], w_o[0], w_up[0], w_up[1], w_down[0], w_down[1]]
    stacks = [own_slot(w.astype(BF), N_SHARD, shard) for w in big]
    groups = ((0,), (1,), (4, 6), (2, 3), (5, 7))
    gather_sems, stacks, gather_token = _gather_start("gather_start", stacks, groups, OTHER_CHIPS, after=(small_token,))

    def gathered_group(g, name, after):
        return _gather_wait(name, [stacks[t] for t in groups[g]], gather_sems[g], OTHER_CHIPS, after)

    bucket = _bucket_table()
    bias = _bias_table(rel_bias, bucket)

    h0 = _rms_fwd("conv_norm", x2d, conv_norm_g, deps=(gather_token,))
    (w_in_sm,) = gathered_group(0, "gather_wait_conv_in", (h0, bias))
    (u,) = _mm("conv_in", h0, w_in_sm, nt=False, b_sm=True, tm=2048, tn=512, rows=256, ep_in=((conv_b_in, "row"),),
               ep_fn=lambda acc, b: (acc + b,), outs=(("tile", BF),))
    (gathered,) = _gather_wait("small_weights_wait", [small_land], small_sems, ALL_OTHERS, (u,))
    chips = [_unpack(gathered[2 * s], [a.shape for a in sharded_small]) for s in range(N_SHARD)]
    dw_f, attn_norm_f, b_qkv_f, b_o_f = (jnp.concatenate([chips[s][t] for s in range(N_SHARD)], axis=-1)
                                         for t in range(len(sharded_small)))
    dw_pad = jnp.pad(dw_f, ((0, HALO - CONV_W), (0, 0)))
    cv, s_act = _conv_fwd(u, dw_pad, conv_dw_b, conv_ln_g, conv_ln_b)
    (g_out,) = gathered_group(1, "gather_wait_conv_out", (s_act,))
    w_out_f = g_out.reshape(Dm, Dm)
    x1, h1 = _mm("conv_out", s_act, w_out_f, nt=False, tm=1024, tn=1024, rows=256,
                 ep_in=((conv_b_out, "row"), (x2d, "tile"), (mlp_norm_g[0:1], "row")), ep_fn=_residual_norm_ep,
                 outs=RESIDUAL_NORM_OUTS)

    g_up0, g_down0 = gathered_group(2, "gather_wait_mlp0", (x1,))
    w_up_sm = [g_up0, None]
    w_down_f = [g_down0.reshape(D_FF, Dm), None]
    up0 = _mlp_up(0, h1, w_up_sm[0])
    x2, h2 = _mm("mlp0_down", up0, w_down_f[0], nt=False, tm=512, tn=1024, rows=256, a_fn=_relu2,
                 ep_in=((x1, "tile"), (attn_norm_f, "row")), ep_fn=_residual_norm_ep, outs=RESIDUAL_NORM_OUTS)

    g_qkv, g_o = gathered_group(3, "gather_wait_attn", (x2,))
    w_qkv_f = jnp.transpose(g_qkv, (1, 0, 2)).reshape(Dm, QKV_DIM)
    w_o_f = g_o.reshape(ATTN_DIM, Dm)
    qg_t = jnp.tile(q_norm_g, (1, N_HEADS))
    kg_t = jnp.tile(k_norm_g, (1, N_KV))

    def qkv_ep(acc, b, qg, kg, ones):
        proj = acc + b
        q, k, v = proj[:, :ATTN_DIM], proj[:, ATTN_DIM:ATTN_DIM + KV_DIM], proj[:, ATTN_DIM + KV_DIM:]
        return proj, _qk_normed(q, qg, ones, 1.0 / math.sqrt(HEAD_DIM)), _qk_normed(k, kg, ones, 1.0), v

    qkv, qn, kn, vv = _mm(
        "attn_qkv", h2, w_qkv_f, nt=False, tm=1024, tn=QKV_DIM, rows=256, ep_fn=qkv_ep,
        ep_in=((b_qkv_f, "row"), (qg_t, "whole"), (kg_t, "whole"), (_head_ones(), "whole")),
        outs=(("tile", F32), ("tile", BF, ATTN_DIM), ("tile", BF, KV_DIM), ("tile", BF, KV_DIM)))
    sinks1 = sinks[0]
    att = _attn_fwd(qn, kn, vv, bias, sinks1)
    x3, h3 = _mm("attn_out", att, w_o_f, nt=False, tm=1024, tn=1024, rows=256,
                 ep_in=((b_o_f, "row"), (x2, "tile"), (mlp_norm_g[1:2], "row")), ep_fn=_residual_norm_ep,
                 outs=RESIDUAL_NORM_OUTS)

    g_up1, g_down1 = gathered_group(4, "gather_wait_mlp1", (x3,))
    w_up_sm[1] = g_up1
    w_down_f[1] = g_down1.reshape(D_FF, Dm)
    up1 = _mlp_up(1, h3, w_up_sm[1])

    def loss_ep(acc, r, t):
        diff = acc + r - t
        dy = diff * (1.0 / Dm)
        return dy, dy, jnp.sum(diff * diff, axis=0, keepdims=True)

    dy, dy_bf, sq = _mm("mlp1_down_loss", up1, w_down_f[1], nt=False, tm=512, tn=1024, rows=256, a_fn=_relu2,
                        ep_in=((x3, "tile"), (tgt, "tile")), ep_fn=loss_ep,
                        outs=(("tile", F32), ("tile", BF), ("colsum", F32)))

    place = jnp.stack([shard, lax.axis_index("c")]).astype(jnp.int32)
    dx3, dx3_bf, dg_mlp1, db_o, dup1 = _mlp_bwd(1, dy, dy_bf, x3, mlp_norm_g[1:2], up1, w_up_sm[1], w_down_f[1])
    dw_down1 = _mm_tn("mlp1_dw_down", up1, dy_bf, tm=1024, tn=1024, tk=2048, a_fn=_relu2)
    dw_up1 = _mm_tn("mlp1_dw_up", h3, dup1, tm=1024, tn=1024, tk=2048, out_sm=N_SHARD)
    red_mlp1 = _Reduction("mlp1", [dw_up1, dw_down1.reshape(N_SHARD, D_FF // N_SHARD, Dm)], place)

    ident = lambda acc: (acc,)
    (datt,) = _mm("attn_dout", dx3_bf, w_o_f, nt=True, tm=1024, tn=1024, rows=256, ep_fn=ident, outs=(("tile", BF),),
                  deps=(red_mlp1.token,))
    dw_o = _mm_tn("attn_dw_o", att, dx3_bf, tm=1024, tn=1024, tk=2048)
    dqn, dkn, dvv, dbias, dsinks = _attn_bwd(qn, kn, vv, bias, sinks1, datt)
    drel = _bias_grad(dbias, bucket)
    dqkv, db_qkv, dqg_t, dkg_t = _qk_norm_bwd(qkv, dqn, dkn, dvv, qg_t, kg_t)
    dw_qkv = _mm_tn("attn_dw_qkv", h2, dqkv, tm=1024, tn=QKV_DIM, tk=2048)
    red_attn = _Reduction("attn", [jnp.transpose(dw_qkv.reshape(Dm, N_SHARD, QKV_DIM // N_SHARD), (1, 0, 2)),
                                   dw_o.reshape(N_SHARD, ATTN_DIM // N_SHARD, Dm)], place)
    dx2, dx2_bf, dg_attn, _ = _mm("attn_dx", dqkv, w_qkv_f, nt=True, tm=1024, tn=1024, rows=256,
                                  ep_in=((x2, "tile"), (attn_norm_f, "row"), (dx3, "tile")), ep_fn=_rms_bwd_ep,
                                  outs=RMS_BWD_OUTS, deps=(red_attn.token,))

    dx1, dx1_bf, dg_mlp0, db_out, dup0 = _mlp_bwd(0, dx2, dx2_bf, x1, mlp_norm_g[0:1], up0, w_up_sm[0], w_down_f[0])
    dw_down0 = _mm_tn("mlp0_dw_down", up0, dx2_bf, tm=1024, tn=1024, tk=2048, a_fn=_relu2)
    dw_up0 = _mm_tn("mlp0_dw_up", h1, dup0, tm=1024, tn=1024, tk=2048, out_sm=N_SHARD)
    dw_out = _mm_tn("conv_dw_out", s_act, dx1_bf, tm=1024, tn=1024, tk=2048)
    red_mlp0 = _Reduction("mlp0", [dw_up0, dw_down0.reshape(N_SHARD, D_FF // N_SHARD, Dm),
                                   dw_out.reshape(N_SHARD, Dm // N_SHARD, Dm)], place)
    (r_qkv, r_o) = red_attn.finish((dx1,))
    (r_up1, r_down1) = red_mlp1.finish((dx1,))

    dcv, dln_g, dln_b, ddw_b = _mm("conv_ds", dx1_bf, w_out_f, nt=True, tm=1024, tn=1024, rows=256,
                                   ep_in=((cv, "tile"), (conv_ln_g, "row"), (conv_ln_b, "row")),
                                   ep_fn=_ln_silu_bwd_ep,
                                   outs=(("tile", F32), ("colsum", F32), ("colsum", F32), ("colsum", F32)),
                                   deps=(red_mlp0.token,))
    du, db_in, ddw8 = _conv_bwd(u, dcv, dw_pad)
    (r_up0, r_down0, r_out) = red_mlp0.finish((du,))
    dw_in = _mm_tn("conv_dw_in", h0, du, tm=1024, tn=512, tk=4096, out_sm=N_SHARD)
    red_conv = _Reduction("conv", [dw_in], place)
    def first_layer_ep(*args):
        tot, _, dg, _ = _rms_bwd_ep(*args)
        return tot, dg

    gx, dg_conv = _mm("conv_dx", du, w_in_sm, nt=True, b_sm=True, tm=1024, tn=1024, rows=256,
                      ep_in=((x2d, "tile"), (conv_norm_g, "row"), (dx1, "tile")), ep_fn=first_layer_ep,
                      outs=(("tile", F32), ("colsum", F32)), deps=(red_conv.token,))
    (r_in,) = red_conv.finish((gx,))

    dqg = dqg_t.reshape(N_HEADS, HEAD_DIM).sum(axis=0, keepdims=True)
    dkg = dkg_t.reshape(N_KV, HEAD_DIM).sum(axis=0, keepdims=True)
    small_full = [dg_conv, db_in, ddw8.sum(axis=1)[:CONV_W], ddw_b, dln_g, dln_b, db_out, dg_attn, db_qkv, dqg, dkg,
                  dsinks[None, :], db_o, drel.reshape(1, REL_BUCKETS * N_HEADS),
                  jnp.pad(dg_mlp0, ((0, 1), (0, 0))) + jnp.pad(dg_mlp1, ((1, 0), (0, 0))), sq]
    (sg_sems,), (sg_land,), sg_token = _gather_start(
        "small_grads_start", [own_slot(_pack(small_full), 8, me)], ((0,),), ALL_OTHERS, after=())

    mine = [r_in, r_out, r_qkv, r_o, r_up0, r_up1, r_down0, r_down1]
    r_in, r_out, r_qkv, r_o, r_up0, r_up1, r_down0, r_down1 = zip(
        mine, _join_halves("join_halves", mine, deps=(sg_token,)))

    big_out = {}
    for nm, w, m, v, gs in (("conv_w_in", conv_w_in, m_conv_w_in, v_conv_w_in, (r_in,)),
                            ("conv_w_out", conv_w_out, m_conv_w_out, v_conv_w_out, (r_out,)),
                            ("w_qkv", w_qkv, m_w_qkv, v_w_qkv, (r_qkv,)),
                            ("w_o", w_o, m_w_o, v_w_o, (r_o,)),
                            ("w_up", w_up, m_w_up, v_w_up, (r_up0, r_up1)),
                            ("w_down", w_down, m_w_down, v_w_down, (r_down0, r_down1))):
        big_out[nm] = _adamw(f"adamw_{nm}", w, m, v, gs)

    (sg_land,) = _gather_wait("small_grads_wait", [sg_land], sg_sems, ALL_OTHERS,
                              [big_out[nm][0] for nm in big_out])
    small_sum = _sum8("small_grads_sum", sg_land)
    (r_norm, r_b_in, r_dw, r_dw_b, r_ln_g, r_ln_b, r_b_out, r_attn_norm, r_b_qkv, r_qg, r_kg, r_sinks, r_b_o, r_rel,
     r_mlp_norm, r_sq) = _unpack(small_sum, [a.shape for a in small_full])
    loss = 0.5 * jnp.sum(r_sq) * (1.0 / Dm)

    def cols(a, width):
        return lax.dynamic_slice_in_dim(a, shard * width, width, axis=a.ndim - 1)

    small_names = ["conv_norm_g", "conv_b_in", "conv_dw", "conv_dw_b", "conv_ln_g", "conv_ln_b", "conv_b_out",
                   "attn_norm_g", "b_qkv", "q_norm_g", "k_norm_g", "sinks", "b_o", "rel_bias", "mlp_norm_g"]
    small_g = [r_norm, r_b_in, cols(r_dw, Dm // N_SHARD)[None], r_dw_b, r_ln_g, r_ln_b, r_b_out,
               cols(r_attn_norm, Dm // N_SHARD), cols(r_b_qkv, QKV_DIM // N_SHARD), r_qg, r_kg, r_sinks,
               cols(r_b_o, Dm // N_SHARD), r_rel.reshape(REL_BUCKETS, N_HEADS), r_mlp_norm]
    small_w = [conv_norm_g, conv_b_in, conv_dw, conv_dw_b, conv_ln_g, conv_ln_b, conv_b_out, attn_norm_g, b_qkv,
               q_norm_g, k_norm_g, sinks, b_o, rel_bias, mlp_norm_g]
    small_m = [m_conv_norm_g, m_conv_b_in, m_conv_dw, m_conv_dw_b, m_conv_ln_g, m_conv_ln_b, m_conv_b_out,
               m_attn_norm_g, m_b_qkv, m_q_norm_g, m_k_norm_g, m_sinks, m_b_o, m_rel_bias, m_mlp_norm_g]
    small_v = [v_conv_norm_g, v_conv_b_in, v_conv_dw, v_conv_dw_b, v_conv_ln_g, v_conv_ln_b, v_conv_b_out,
               v_attn_norm_g, v_b_qkv, v_q_norm_g, v_k_norm_g, v_sinks, v_b_o, v_rel_bias, v_mlp_norm_g]
    flat2 = lambda a: a.reshape(-1, a.shape[-1])
    small_g = [flat2(g) for g in small_g]
    d_s, m_s, v_s = _adamw_small([flat2(w) for w in small_w], small_g, [flat2(m) for m in small_m],
                                 [flat2(v) for v in small_v])
    small_out = {}
    for nm, w, g, d, m2, v2 in zip(small_names, small_w, small_g, d_s, m_s, v_s):
        small_out[nm] = tuple(a.reshape(w.shape) for a in (g, d, m2, v2))

    order = ["conv_norm_g", "conv_w_in", "conv_b_in", "conv_dw", "conv_dw_b", "conv_ln_g", "conv_ln_b", "conv_w_out",
             "conv_b_out", "attn_norm_g", "w_qkv", "b_qkv", "q_norm_g", "k_norm_g", "sinks", "w_o", "b_o", "rel_bias",
             "mlp_norm_g", "w_up", "w_down"]
    res = {**small_out, **big_out}
    outs = [loss, gx[None]]
    for part in range(4):
        outs += [res[nm][part] for nm in order]
    return tuple(outs)
```

```python
import math

import numpy as np
import jax
import jax.numpy as jnp
from jax import lax
from jax.experimental import pallas as pl
from jax.experimental.pallas import tpu as pltpu

F32 = jnp.float32
BF = jnp.bfloat16
MESH = pl.DeviceIdType.MESH

D_MODEL = 1024
D_FF = 4096
N_HEADS = 16
N_KV = 2
GROUP = N_HEADS // N_KV
HEAD_DIM = 64
ATTN_DIM = N_HEADS * HEAD_DIM
KV_DIM = N_KV * HEAD_DIM
QKV_DIM = ATTN_DIM + 2 * KV_DIM
BLOCK = 128
CONV_W = 31
HALO = 32
REL_BUCKETS = 32
REL_MAX_DIST = 128
NORM_EPS = 1e-6
NEG_INF = -1e30
N_SHARD = 4
LANES = 1024

ADAM_LR = 0.001
ADAM_B1 = 0.9
ADAM_B2 = 0.999
ADAM_EPS = 1e-08
ADAM_WD = 0.01
ADAM_STEP = 10

VMEM_LIMIT = 56 * 1024 * 1024


def _params(n_axes):
    return pltpu.CompilerParams(dimension_semantics=("arbitrary",) * n_axes, vmem_limit_bytes=VMEM_LIMIT)


def _dot(a, b, ca, cb):
    return lax.dot_general(a, b, (((ca,), (cb,)), ((), ())), preferred_element_type=F32)


def _mm(name, a, b, *, nt, tm, tn, ep_fn, outs, a_fn=None, b_sm=False, ep_in=(), deps=(), rows=None):
    M, K = a.shape
    rows = tm if rows is None else rows
    if b_sm:
        S, ks = b.shape[0], b.shape[2]
        N, per = (b.shape[1], None) if nt else (S * b.shape[2], b.shape[2] // tn)
        assert (S * ks == K) if nt else (b.shape[1] == K)
    else:
        N = b.shape[0] if nt else b.shape[1]
        assert (b.shape[1] if nt else b.shape[0]) == K
    assert M % tm == 0 and N % tn == 0 and tm % rows == 0
    ne, no, nd = len(ep_in), len(outs), len(deps)

    def body(a_ref, b_ref, *rest):
        ep_refs, out_refs = rest[:ne], rest[ne + nd:ne + nd + no]
        i = pl.program_id(1)
        sums = [None] * no
        for r in range(tm // rows):
            rs = pl.ds(r * rows, rows)

            def lhs(cols):
                av = a_ref[rs, cols]
                return (av if a_fn is None else a_fn(av)).astype(BF)

            if b_sm and nt:
                acc = None
                for s in range(S):
                    part = _dot(lhs(pl.ds(s * ks, ks)), b_ref[s].astype(BF), 1, 1)
                    acc = part if acc is None else acc + part
            else:
                acc = _dot(lhs(slice(None)), b_ref[...].astype(BF), 1, 1 if nt else 0)
            ep_vals = [ref[rs, :] if kind == "tile" else ref[...] for ref, (_, kind) in zip(ep_refs, ep_in)]
            vals = ep_fn(acc, *ep_vals)
            for o, ((kind, dt, *_), ref, val) in enumerate(zip(outs, out_refs, vals)):
                if kind == "tile":
                    ref[rs, :] = val.astype(dt)
                else:
                    sums[o] = val if sums[o] is None else sums[o] + val
        for (kind, *_), ref, val in zip(outs, out_refs, sums):
            if kind == "colsum":
                @pl.when(i == 0)
                def _():
                    ref[...] = val

                @pl.when(i > 0)
                def _():
                    ref[...] += val

    if b_sm and nt:
        b_spec = pl.BlockSpec((S, tn, ks), lambda j, i: (0, j, 0))
    elif b_sm:
        b_spec = pl.BlockSpec((None, K, tn), lambda j, i: (j // per, 0, j % per))
    elif nt:
        b_spec = pl.BlockSpec((tn, K), lambda j, i: (j, 0))
    else:
        b_spec = pl.BlockSpec((K, tn), lambda j, i: (0, j))
    in_specs = [pl.BlockSpec((tm, K), lambda j, i: (i, 0)), b_spec]
    for arr, kind in ep_in:
        if kind == "tile":
            assert arr.shape == (M, N)
            in_specs.append(pl.BlockSpec((tm, tn), lambda j, i: (i, j)))
        elif kind == "whole":
            in_specs.append(pl.BlockSpec(arr.shape, lambda j, i, rank=arr.ndim: (0,) * rank))
        else:
            assert arr.shape == (1, N)
            in_specs.append(pl.BlockSpec((1, tn), lambda j, i: (0, j)))
    in_specs += [pl.BlockSpec(memory_space=pl.ANY)] * nd
    out_shape, out_specs = [], []
    for kind, dt, *width in outs:
        if kind == "tile" and width:
            assert tn == N
            out_shape.append(jax.ShapeDtypeStruct((M, width[0]), dt))
            out_specs.append(pl.BlockSpec((tm, width[0]), lambda j, i: (i, 0)))
        elif kind == "tile":
            out_shape.append(jax.ShapeDtypeStruct((M, N), dt))
            out_specs.append(pl.BlockSpec((tm, tn), lambda j, i: (i, j)))
        else:
            out_shape.append(jax.ShapeDtypeStruct((1, N), F32))
            out_specs.append(pl.BlockSpec((1, tn), lambda j, i: (0, j)))
    return pl.pallas_call(
        body, name=name, grid=(N // tn, M // tm), in_specs=in_specs, out_specs=out_specs, out_shape=out_shape,
        compiler_params=_params(2),
    )(a, b, *[arr for arr, _ in ep_in], *deps)


def _mm_tn(name, a, b, *, tm, tn, tk, a_fn=None, out_sm=None):
    T, Ka = a.shape
    N = b.shape[1]
    assert b.shape[0] == T and T % tk == 0 and Ka % tm == 0 and N % tn == 0
    nk = T // tk

    def body(a_ref, b_ref, o_ref, acc_ref):
        k = pl.program_id(2)

        @pl.when(k == 0)
        def _():
            acc_ref[...] = jnp.zeros_like(acc_ref)

        av = a_ref[...]
        if a_fn is not None:
            av = a_fn(av)
        acc_ref[...] += _dot(av.astype(BF), b_ref[...].astype(BF), 0, 0)

        @pl.when(k == nk - 1)
        def _():
            o_ref[...] = acc_ref[...].astype(BF)

    if out_sm is None:
        out_shape = jax.ShapeDtypeStruct((Ka, N), BF)
        out_spec = pl.BlockSpec((tm, tn), lambda i, j, k: (i, j))
    else:
        per = (N // out_sm) // tn
        assert per * tn * out_sm == N
        out_shape = jax.ShapeDtypeStruct((out_sm, Ka, N // out_sm), BF)
        out_spec = pl.BlockSpec((None, tm, tn), lambda i, j, k: (j // per, i, j % per))
    return pl.pallas_call(
        body, name=name, grid=(Ka // tm, N // tn, nk),
        in_specs=[pl.BlockSpec((tk, tm), lambda i, j, k: (k, i)), pl.BlockSpec((tk, tn), lambda i, j, k: (k, j))],
        out_specs=out_spec, out_shape=out_shape, scratch_shapes=[pltpu.VMEM((tm, tn), F32)],
        compiler_params=_params(3),
    )(a, b)


def _relu2(v):
    r = jnp.maximum(v.astype(F32), 0.0)
    return r * r


def _rms_bwd_ep(dh, x, g, dres):
    rstd = lax.rsqrt(jnp.mean(x * x, axis=-1, keepdims=True) + NORM_EPS)
    xh = x * rstd
    dxh = dh * g
    dx = rstd * (dxh - xh * jnp.mean(dxh * xh, axis=-1, keepdims=True))
    tot = dres + dx
    return tot, tot, jnp.sum(dh * xh, axis=0, keepdims=True), jnp.sum(tot, axis=0, keepdims=True)


def _rms_fwd(name, x, g, tm=512, deps=()):
    T, Dm = x.shape

    def body(x_ref, g_ref, *rest):
        o_ref = rest[-1]
        xv = x_ref[...]
        rstd = lax.rsqrt(jnp.mean(xv * xv, axis=-1, keepdims=True) + NORM_EPS)
        o_ref[...] = (xv * rstd * g_ref[...]).astype(BF)

    return pl.pallas_call(
        body, name=name, grid=(T // tm,),
        in_specs=[pl.BlockSpec((tm, Dm), lambda i: (i, 0)), pl.BlockSpec((1, Dm), lambda i: (0, 0))]
        + [pl.BlockSpec(memory_space=pl.ANY)] * len(deps),
        out_specs=pl.BlockSpec((tm, Dm), lambda i: (i, 0)), out_shape=jax.ShapeDtypeStruct((T, Dm), BF),
        compiler_params=_params(1),
    )(x, g, *deps)


HEAD_GROUP = 256


def _two_term_dot(v, m):
    hi = v.astype(BF)
    lo = (v - hi.astype(F32)).astype(BF)
    return _dot(hi, m, 1, 0) + _dot(lo, m, 1, 0)


def _head_sum(v, ones):
    n = v.shape[1]
    w = min(n, HEAD_GROUP)
    blk = ones[:w, :w]
    parts = [_two_term_dot(v[:, c:c + w], blk) for c in range(0, n, w)]
    return parts[0] if len(parts) == 1 else jnp.concatenate(parts, axis=1)


def _head_ones():
    idx = np.arange(HEAD_GROUP) // HEAD_DIM
    return jnp.asarray((idx[:, None] == idx[None, :]).astype(np.float32), dtype=BF)


def _qk_normed(x, g, ones, scale):
    r = lax.rsqrt(_head_sum(x * x, ones) * (1.0 / HEAD_DIM) + NORM_EPS)
    return x * r * g * scale


def _qk_norm_bwd(qkv, dqn, dkn, dv, qg_t, kg_t, tm=256):
    T = qkv.shape[0]

    def body(x_ref, dq_ref, dk_ref, dv_ref, qg_ref, kg_ref, ones_ref, o_ref, db_ref, dqg_ref, dkg_ref):
        i = pl.program_id(0)
        ones = ones_ref[...]

        def one(x, dy, g):
            r = lax.rsqrt(_head_sum(x * x, ones) * (1.0 / HEAD_DIM) + NORM_EPS)
            xh = x * r
            dxh = dy * g
            dx = r * (dxh - xh * (_head_sum(dxh * xh, ones) * (1.0 / HEAD_DIM)))
            return dx, jnp.sum(dy * xh, axis=0, keepdims=True)

        dq, dqg = one(x_ref[:, pl.ds(0, ATTN_DIM)], dq_ref[...], qg_ref[...])
        dk, dkg = one(x_ref[:, pl.ds(ATTN_DIM, KV_DIM)], dk_ref[...], kg_ref[...])
        dvv = dv_ref[...]
        o_ref[:, pl.ds(0, ATTN_DIM)] = dq.astype(BF)
        o_ref[:, pl.ds(ATTN_DIM, KV_DIM)] = dk.astype(BF)
        o_ref[:, pl.ds(ATTN_DIM + KV_DIM, KV_DIM)] = dvv.astype(BF)
        sq, sk, sv = (jnp.sum(t, axis=0, keepdims=True) for t in (dq, dk, dvv))

        @pl.when(i == 0)
        def _():
            db_ref[:, pl.ds(0, ATTN_DIM)] = sq
            db_ref[:, pl.ds(ATTN_DIM, KV_DIM)] = sk
            db_ref[:, pl.ds(ATTN_DIM + KV_DIM, KV_DIM)] = sv
            dqg_ref[...] = dqg
            dkg_ref[...] = dkg

        @pl.when(i > 0)
        def _():
            db_ref[:, pl.ds(0, ATTN_DIM)] += sq
            db_ref[:, pl.ds(ATTN_DIM, KV_DIM)] += sk
            db_ref[:, pl.ds(ATTN_DIM + KV_DIM, KV_DIM)] += sv
            dqg_ref[...] += dqg
            dkg_ref[...] += dkg

    full = lambda shape: pl.BlockSpec(shape, lambda i: (0, 0))
    row = lambda n: pl.BlockSpec((tm, n), lambda i: (i, 0))
    return pl.pallas_call(
        body, name="qk_norm_bwd", grid=(T // tm,),
        in_specs=[row(QKV_DIM), row(ATTN_DIM), row(KV_DIM), row(KV_DIM), full((1, ATTN_DIM)), full((1, KV_DIM)),
                  full((HEAD_GROUP, HEAD_GROUP))],
        out_specs=[row(QKV_DIM), full((1, QKV_DIM)), full((1, ATTN_DIM)), full((1, KV_DIM))],
        out_shape=[jax.ShapeDtypeStruct((T, QKV_DIM), BF), jax.ShapeDtypeStruct((1, QKV_DIM), F32),
                   jax.ShapeDtypeStruct((1, ATTN_DIM), F32), jax.ShapeDtypeStruct((1, KV_DIM), F32)],
        compiler_params=_params(1),
    )(qkv, dqn, dkn, dv, qg_t, kg_t, _head_ones())


ROWS = 128
COLS = 128


SUBLANES = 8
FIRST_TAP = HALO - (CONV_W - 1)


def _glu(a, g):
    return a.astype(F32) * jax.nn.sigmoid(g.astype(F32))


def _shifted(xe, s):
    return xe if s == 0 else pltpu.roll(xe, ROWS + HALO - s, axis=0)


def _conv_fwd(u, dw_pad, dw_b, ln_g, ln_b, tm=512):
    T = u.shape[0]
    Dm = D_MODEL
    hpt = tm // HALO

    def body(ac_ref, gc_ref, ap_ref, gp_ref, w_ref, wb_ref, lg_ref, lb_ref, cv_ref, s_ref, ext):
        i = pl.program_id(0)
        ext[pl.ds(0, HALO), :] = jnp.where(i > 0, _glu(ap_ref[...], gp_ref[...]), 0.0)
        ext[pl.ds(HALO, tm), :] = _glu(ac_ref[...], gc_ref[...])

        def rows(r, carry):
            r0 = pl.multiple_of(r * ROWS, ROWS)
            for c in range(Dm // COLS):
                cs = pl.ds(c * COLS, COLS)
                xe = ext[pl.ds(r0, ROWS + HALO), cs]
                acc = jnp.zeros((ROWS, COLS), F32)
                for s in range(SUBLANES):
                    xs = _shifted(xe, s)
                    for j in range(CONV_W):
                        off = FIRST_TAP + j
                        if off % SUBLANES == s:
                            acc = acc + xs[off - s:off - s + ROWS, :] * w_ref[pl.ds(j, 1), cs]
                cv_ref[pl.ds(r0, ROWS), cs] = acc + wb_ref[:, cs]
            return carry

        lax.fori_loop(0, tm // ROWS, rows, 0)
        cv = cv_ref[...]
        xc = cv - jnp.mean(cv, axis=-1, keepdims=True)
        y = xc * lax.rsqrt(jnp.mean(xc * xc, axis=-1, keepdims=True) + NORM_EPS) * lg_ref[...] + lb_ref[...]
        s_ref[...] = (y * jax.nn.sigmoid(y)).astype(BF)

    full = lambda shape: pl.BlockSpec(shape, lambda i: (0, 0))
    return pl.pallas_call(
        body, name="conv_fwd", grid=(T // tm,),
        in_specs=[pl.BlockSpec((tm, Dm), lambda i: (i, 0)), pl.BlockSpec((tm, Dm), lambda i: (i, 1)),
                  pl.BlockSpec((HALO, Dm), lambda i: (jnp.maximum(i * hpt - 1, 0), 0)),
                  pl.BlockSpec((HALO, Dm), lambda i: (jnp.maximum(i * hpt - 1, 0), 1)),
                  full((HALO, Dm)), full((1, Dm)), full((1, Dm)), full((1, Dm))],
        out_specs=[pl.BlockSpec((tm, Dm), lambda i: (i, 0)), pl.BlockSpec((tm, Dm), lambda i: (i, 0))],
        out_shape=[jax.ShapeDtypeStruct((T, Dm), F32), jax.ShapeDtypeStruct((T, Dm), BF)],
        scratch_shapes=[pltpu.VMEM((tm + HALO, Dm), F32)],
        compiler_params=_params(1),
    )(u, u, u, u, dw_pad, dw_b, ln_g, ln_b)


def _ln_silu_bwd_ep(ds, cv, lg, lb):
    xc = cv - jnp.mean(cv, axis=-1, keepdims=True)
    rstd = lax.rsqrt(jnp.mean(xc * xc, axis=-1, keepdims=True) + NORM_EPS)
    xh = xc * rstd
    y = xh * lg + lb
    sg = jax.nn.sigmoid(y)
    dy = ds * (sg * (1.0 + y * (1.0 - sg)))
    dxh = dy * lg
    dcv = rstd * (dxh - jnp.mean(dxh, axis=-1, keepdims=True) - xh * jnp.mean(dxh * xh, axis=-1, keepdims=True))
    return (dcv, jnp.sum(dy * xh, axis=0, keepdims=True), jnp.sum(dy, axis=0, keepdims=True),
            jnp.sum(dcv, axis=0, keepdims=True))


def _conv_bwd(u, dcv, dw_pad, tm=512):
    T = u.shape[0]
    Dm = D_MODEL
    hpt = tm // HALO
    last = T // HALO - 1
    nt = T // tm

    def body(ac_ref, gc_ref, ap_ref, gp_ref, dc_ref, dn_ref, w_ref, du_ref, db_ref, dw_ref, ext_g, ext_d):
        i = pl.program_id(0)
        ext_g[pl.ds(0, HALO), :] = jnp.where(i > 0, _glu(ap_ref[...], gp_ref[...]), 0.0)
        ext_g[pl.ds(HALO, tm), :] = _glu(ac_ref[...], gc_ref[...])
        ext_d[pl.ds(0, tm), :] = dc_ref[...]
        ext_d[pl.ds(tm, HALO), :] = jnp.where(i < nt - 1, dn_ref[...], 0.0)

        @pl.when(i == 0)
        def _():
            db_ref[...] = jnp.zeros_like(db_ref)
            dw_ref[...] = jnp.zeros_like(dw_ref)

        def rows(r, carry):
            r0 = pl.multiple_of(r * ROWS, ROWS)
            rs = pl.ds(r0, ROWS)
            for c in range(Dm // COLS):
                cs = pl.ds(c * COLS, COLS)
                cs2 = pl.ds(Dm + c * COLS, COLS)
                de = ext_d[pl.ds(r0, ROWS + HALO), cs]
                ge = ext_g[pl.ds(r0, ROWS + HALO), cs]
                dcur = de[0:ROWS, :]
                acc = jnp.zeros((ROWS, COLS), F32)
                for s in range(SUBLANES):
                    ds_, gs_ = _shifted(de, s), _shifted(ge, s)
                    for j in range(CONV_W):
                        off = CONV_W - 1 - j
                        if off % SUBLANES == s:
                            acc = acc + ds_[off - s:off - s + ROWS, :] * w_ref[pl.ds(j, 1), cs]
                        goff = FIRST_TAP + j
                        if goff % SUBLANES == s:
                            prod = dcur * gs_[goff - s:goff - s + ROWS, :]
                            dw_ref[j, :, cs] += jnp.sum(prod.reshape(ROWS // SUBLANES, SUBLANES, COLS), axis=0)
                a = ac_ref[rs, cs].astype(F32)
                sg = jax.nn.sigmoid(gc_ref[rs, cs].astype(F32))
                da = acc * sg
                dg = acc * a * sg * (1.0 - sg)
                du_ref[rs, cs] = da.astype(BF)
                du_ref[rs, cs2] = dg.astype(BF)
                db_ref[:, cs] += jnp.sum(da, axis=0, keepdims=True)
                db_ref[:, cs2] += jnp.sum(dg, axis=0, keepdims=True)
            return carry

        lax.fori_loop(0, tm // ROWS, rows, 0)

    return pl.pallas_call(
        body, name="conv_bwd", grid=(nt,),
        in_specs=[pl.BlockSpec((tm, Dm), lambda i: (i, 0)), pl.BlockSpec((tm, Dm), lambda i: (i, 1)),
                  pl.BlockSpec((HALO, Dm), lambda i: (jnp.maximum(i * hpt - 1, 0), 0)),
                  pl.BlockSpec((HALO, Dm), lambda i: (jnp.maximum(i * hpt - 1, 0), 1)),
                  pl.BlockSpec((tm, Dm), lambda i: (i, 0)),
                  pl.BlockSpec((HALO, Dm), lambda i: (jnp.minimum((i + 1) * hpt, last), 0)),
                  pl.BlockSpec((HALO, Dm), lambda i: (0, 0))],
        out_specs=[pl.BlockSpec((tm, 2 * Dm), lambda i: (i, 0)), pl.BlockSpec((1, 2 * Dm), lambda i: (0, 0)),
                   pl.BlockSpec((HALO, 8, Dm), lambda i: (0, 0, 0))],
        out_shape=[jax.ShapeDtypeStruct((T, 2 * Dm), BF), jax.ShapeDtypeStruct((1, 2 * Dm), F32),
                   jax.ShapeDtypeStruct((HALO, 8, Dm), F32)],
        scratch_shapes=[pltpu.VMEM((tm + HALO, Dm), F32), pltpu.VMEM((tm + HALO, Dm), F32)],
        compiler_params=_params(1),
    )(u, u, u, u, dcv, dcv, dw_pad)


def _bucket_table():
    q_loc = np.arange(BLOCK)[:, None]
    k_loc = np.arange(2 * BLOCK)[None, :]
    dist = q_loc + BLOCK - k_loc
    n = np.maximum(dist, 0)
    max_exact = REL_BUCKETS // 2
    large = max_exact + (np.log(np.maximum(n, 1).astype(np.float32) / max_exact)
                         / math.log(REL_MAX_DIST / max_exact) * (REL_BUCKETS - max_exact)).astype(np.int32)
    large = np.minimum(large, REL_BUCKETS - 1)
    bucket = np.where(n < max_exact, n, large).astype(np.int32)
    return jnp.asarray(np.where((dist >= 0) & (dist < BLOCK), bucket, -1).astype(np.int32))


def _bias_table(rel_bias, bucket):
    def body(rb_ref, bk_ref, o_ref):
        bk = bk_ref[...]
        for h in range(N_HEADS):
            acc = jnp.full((BLOCK, 2 * BLOCK), NEG_INF, F32)
            for b in range(REL_BUCKETS):
                acc = jnp.where(bk == b, rb_ref[b, h], acc)
            o_ref[h] = acc

    return pl.pallas_call(
        body, name="bias_table", out_shape=jax.ShapeDtypeStruct((N_HEADS, BLOCK, 2 * BLOCK), F32),
        in_specs=[pl.BlockSpec(memory_space=pltpu.SMEM), pl.BlockSpec(memory_space=pltpu.VMEM)],
        out_specs=pl.BlockSpec(memory_space=pltpu.VMEM),
    )(rel_bias, bucket)


def _bias_grad(dbias, bucket):
    def body(db_ref, bk_ref, o_ref):
        bk = bk_ref[...]
        for b in range(REL_BUCKETS):
            sel = bk == b
            for h in range(N_HEADS):
                o_ref[b, h] = jnp.sum(jnp.where(sel, db_ref[h], 0.0))

    return pl.pallas_call(
        body, name="bias_grad", out_shape=jax.ShapeDtypeStruct((REL_BUCKETS, N_HEADS), F32),
        in_specs=[pl.BlockSpec(memory_space=pltpu.VMEM), pl.BlockSpec(memory_space=pltpu.VMEM)],
        out_specs=pl.BlockSpec(memory_space=pltpu.SMEM),
    )(dbias, bucket)


GROUP_ROWS = GROUP * BLOCK


def _head_probs(qk, bias_h, sink, first):
    s = jnp.where(first, NEG_INF, qk + bias_h)
    m = jnp.maximum(jnp.max(s, axis=-1, keepdims=True), sink)
    p = jnp.exp(s - m)
    ps = jnp.exp(sink - m)
    inv = 1.0 / (jnp.sum(p, axis=-1, keepdims=True) + ps)
    return p * inv, ps * inv


def _band(prev_ref, cur_ref, g):
    hs = pl.ds(g * HEAD_DIM, HEAD_DIM)
    return jnp.concatenate([prev_ref[:, hs], cur_ref[:, hs]], axis=0)


def _stack_heads(ref, g):
    return jnp.concatenate([ref[:, pl.ds((g * GROUP + hh) * HEAD_DIM, HEAD_DIM)] for hh in range(GROUP)], axis=0)


def _unstack_heads(ref, g, stacked, dtype):
    for hh in range(GROUP):
        ref[:, pl.ds((g * GROUP + hh) * HEAD_DIM, HEAD_DIM)] = stacked[hh * BLOCK:(hh + 1) * BLOCK, :].astype(dtype)


def _first_mask(n):
    col = lax.broadcasted_iota(jnp.int32, (1, 2 * BLOCK), 1)
    return jnp.logical_and(n == 0, col < BLOCK)


def _head_rows(hh):
    return pl.ds(hh * BLOCK, BLOCK)


def _attn_fwd(qn, kn, vv, bias, sinks):
    T = qn.shape[0]
    nb = T // BLOCK

    def body(sk_ref, q_ref, kc_ref, kp_ref, vc_ref, vp_ref, b_ref, o_ref, qk_buf, p_buf):
        first = _first_mask(pl.program_id(0))
        for g in range(N_KV):
            qk_buf[g] = _dot(_stack_heads(q_ref, g), _band(kp_ref, kc_ref, g), 1, 1)
        for g in range(N_KV):
            for hh in range(GROUP):
                h = g * GROUP + hh
                pn, _ = _head_probs(qk_buf[g, _head_rows(hh), :], b_ref[h], sk_ref[h], first)
                p_buf[g, _head_rows(hh), :] = pn.astype(BF)
        for g in range(N_KV):
            _unstack_heads(o_ref, g, _dot(p_buf[g], _band(vp_ref, vc_ref, g), 1, 0), BF)

    cur = lambda n: (n, 0)
    prev = lambda n: (jnp.maximum(n - 1, 0), 0)
    return pl.pallas_call(
        body, name="attn_fwd", grid=(nb,),
        in_specs=[pl.BlockSpec(memory_space=pltpu.SMEM), pl.BlockSpec((BLOCK, ATTN_DIM), cur),
                  pl.BlockSpec((BLOCK, KV_DIM), cur), pl.BlockSpec((BLOCK, KV_DIM), prev),
                  pl.BlockSpec((BLOCK, KV_DIM), cur), pl.BlockSpec((BLOCK, KV_DIM), prev),
                  pl.BlockSpec((N_HEADS, BLOCK, 2 * BLOCK), lambda n: (0, 0, 0))],
        out_specs=pl.BlockSpec((BLOCK, ATTN_DIM), cur), out_shape=jax.ShapeDtypeStruct((T, ATTN_DIM), BF),
        scratch_shapes=[pltpu.VMEM((N_KV, GROUP_ROWS, 2 * BLOCK), F32), pltpu.VMEM((N_KV, GROUP_ROWS, 2 * BLOCK), BF)],
        compiler_params=_params(1),
    )(sinks, qn, kn, kn, vv, vv, bias)


def _attn_bwd(qn, kn, vv, bias, sinks, do):
    T = qn.shape[0]
    nb = T // BLOCK
    scale = 1.0 / math.sqrt(HEAD_DIM)

    def body(sk_ref, q_ref, kc_ref, kp_ref, vc_ref, vp_ref, b_ref, do_ref,
             dq_ref, dk_ref, dv_ref, db_ref, dsk_ref, dk_full, dv_full, dk_carry, dv_carry, qk_buf, dp_buf, p_buf, ds_buf):
        n = pl.program_id(0)

        @pl.when(n == 0)
        def _():
            db_ref[...] = jnp.zeros_like(db_ref)
            dk_carry[...] = jnp.zeros_like(dk_carry)
            dv_carry[...] = jnp.zeros_like(dv_carry)
            for h in range(N_HEADS):
                dsk_ref[h] = 0.0

        @pl.when(n < nb)
        def _():
            first = _first_mask(n)
            ks = [_band(kp_ref, kc_ref, g) for g in range(N_KV)]
            qs = [_stack_heads(q_ref, g) for g in range(N_KV)]
            douts = [_stack_heads(do_ref, g) for g in range(N_KV)]
            for g in range(N_KV):
                qk_buf[g] = _dot(qs[g], ks[g], 1, 1)
                dp_buf[g] = _dot(douts[g], _band(vp_ref, vc_ref, g), 1, 1)
            for g in range(N_KV):
                for hh in range(GROUP):
                    h = g * GROUP + hh
                    rows = _head_rows(hh)
                    pn, psink = _head_probs(qk_buf[g, rows, :], b_ref[h], sk_ref[h], first)
                    dp = dp_buf[g, rows, :]
                    delta = jnp.sum(pn * dp, axis=-1, keepdims=True)
                    ds = pn * (dp - delta)
                    dsk_ref[h] += -jnp.sum(psink * delta)
                    db_ref[h] += ds
                    ds_buf[g, rows, :] = ds.astype(BF)
                    p_buf[g, rows, :] = pn.astype(BF)
            for g in range(N_KV):
                dsb = ds_buf[g]
                _unstack_heads(dq_ref, g, _dot(dsb, ks[g], 1, 0) * scale, F32)
                gs = pl.ds(g * HEAD_DIM, HEAD_DIM)
                dk_full[:, gs] = _dot(dsb, qs[g], 0, 0)
                dv_full[:, gs] = _dot(p_buf[g], douts[g], 0, 0)

        @pl.when(n == nb)
        def _():
            dk_full[...] = jnp.zeros_like(dk_full)
            dv_full[...] = jnp.zeros_like(dv_full)

        dk_ref[...] = dk_carry[...] + dk_full[pl.ds(0, BLOCK), :]
        dv_ref[...] = dv_carry[...] + dv_full[pl.ds(0, BLOCK), :]
        dk_carry[...] = dk_full[pl.ds(BLOCK, BLOCK), :]
        dv_carry[...] = dv_full[pl.ds(BLOCK, BLOCK), :]

    cur = lambda n: (jnp.minimum(n, nb - 1), 0)
    prev = lambda n: (jnp.maximum(jnp.minimum(n, nb - 1) - 1, 0), 0)
    out_kv = lambda n: (jnp.maximum(n - 1, 0), 0)
    return pl.pallas_call(
        body, name="attn_bwd", grid=(nb + 1,),
        in_specs=[pl.BlockSpec(memory_space=pltpu.SMEM), pl.BlockSpec((BLOCK, ATTN_DIM), cur),
                  pl.BlockSpec((BLOCK, KV_DIM), cur), pl.BlockSpec((BLOCK, KV_DIM), prev),
                  pl.BlockSpec((BLOCK, KV_DIM), cur), pl.BlockSpec((BLOCK, KV_DIM), prev),
                  pl.BlockSpec((N_HEADS, BLOCK, 2 * BLOCK), lambda n: (0, 0, 0)),
                  pl.BlockSpec((BLOCK, ATTN_DIM), cur)],
        out_specs=[pl.BlockSpec((BLOCK, ATTN_DIM), cur), pl.BlockSpec((BLOCK, KV_DIM), out_kv),
                   pl.BlockSpec((BLOCK, KV_DIM), out_kv),
                   pl.BlockSpec((N_HEADS, BLOCK, 2 * BLOCK), lambda n: (0, 0, 0)),
                   pl.BlockSpec(memory_space=pltpu.SMEM)],
        out_shape=[jax.ShapeDtypeStruct((T, ATTN_DIM), F32), jax.ShapeDtypeStruct((T, KV_DIM), F32),
                   jax.ShapeDtypeStruct((T, KV_DIM), F32),
                   jax.ShapeDtypeStruct((N_HEADS, BLOCK, 2 * BLOCK), F32), jax.ShapeDtypeStruct((N_HEADS,), F32)],
        scratch_shapes=[pltpu.VMEM((2 * BLOCK, KV_DIM), F32), pltpu.VMEM((2 * BLOCK, KV_DIM), F32),
                        pltpu.VMEM((BLOCK, KV_DIM), F32), pltpu.VMEM((BLOCK, KV_DIM), F32),
                        pltpu.VMEM((N_KV, GROUP_ROWS, 2 * BLOCK), F32), pltpu.VMEM((N_KV, GROUP_ROWS, 2 * BLOCK), F32),
                        pltpu.VMEM((N_KV, GROUP_ROWS, 2 * BLOCK), BF), pltpu.VMEM((N_KV, GROUP_ROWS, 2 * BLOCK), BF)],
        compiler_params=_params(1),
    )(sinks, qn, kn, kn, vv, vv, bias, do)


def _coords():
    return lax.axis_index("x"), lax.axis_index("y"), lax.axis_index("c")


def _sum8(name, blocks):
    def body(b_ref, o_ref):
        tot = b_ref[0]
        for d in range(1, 8):
            tot = tot + b_ref[d]
        o_ref[...] = tot

    return pl.pallas_call(body, name=name, out_shape=jax.ShapeDtypeStruct(blocks.shape[1:], F32))(blocks)


HBM_SPEC = pl.BlockSpec(memory_space=pltpu.HBM)
SEM_SPEC = pl.BlockSpec(memory_space=pltpu.SEMAPHORE)
ANY_SPEC = pl.BlockSpec(memory_space=pl.ANY)
DATAFLOW = pltpu.SideEffectType.DATAFLOW_SIDE_EFFECTING


OTHER_CHIPS = (4, 2, 6)
ALL_OTHERS = (1, 2, 3, 4, 5, 6, 7)


def _slot(x, y, c, peers):
    return 2 * x + y if peers is OTHER_CHIPS else 4 * x + 2 * y + c


def _slot_copy(land, sems, idx, x, y, c, k, peers, arriving):
    send_sems, recv_sems = sems
    px, py, pc = x ^ (k >> 2), y ^ ((k >> 1) & 1), c ^ (k & 1)
    mine = _slot(x, y, c, peers)
    dst = _slot(px, py, pc, peers) if arriving else mine
    return pltpu.make_async_remote_copy(src_ref=land.at[mine], dst_ref=land.at[dst], send_sem=send_sems.at[idx],
                                        recv_sem=recv_sems.at[idx], device_id=(px, py, pc), device_id_type=MESH)


def _gather_start(name, stacks, groups, peers, after):
    n = len(stacks)
    ng = len(groups)
    np_ = len(peers)
    after = tuple(after)

    def body(*refs):
        lands = refs[:n]
        first = n + len(after)
        sems = [(refs[first + 2 * g], refs[first + 2 * g + 1]) for g in range(ng)]
        token = refs[-1]
        x, y, c = _coords()
        for g, members in enumerate(groups):
            for i, t in enumerate(members):
                for j, k in enumerate(peers):
                    _slot_copy(lands[t], sems[g], np_ * i + j, x, y, c, k, peers, arriving=False).start()
        token[...] = jnp.zeros_like(token)

    out_shape = []
    for members in groups:
        out_shape += [pltpu.SemaphoreType.DMA((np_ * len(members),))] * 2
    out_shape += [pltpu.HBM(w.shape, w.dtype) for w in stacks]
    out_shape.append(jax.ShapeDtypeStruct((8, 128), F32))
    res = pl.pallas_call(
        body, name=name, out_shape=out_shape, in_specs=[HBM_SPEC] * n + [ANY_SPEC] * len(after),
        out_specs=[SEM_SPEC] * (2 * ng) + [HBM_SPEC] * n + [pl.BlockSpec(memory_space=pltpu.VMEM)],
        input_output_aliases={t: 2 * ng + t for t in range(n)},
        compiler_params=pltpu.CompilerParams(has_side_effects=DATAFLOW),
    )(*[pltpu.with_memory_space_constraint(w, pltpu.HBM) for w in stacks], *after)
    sems = [(res[2 * g], res[2 * g + 1]) for g in range(ng)]
    return sems, list(res[2 * ng:2 * ng + n]), res[-1]


def _gather_wait(name, stacks, sems, peers, after):
    n = len(stacks)
    after = tuple(after)

    def body(*refs):
        lands = refs[:n]
        group_sems = (refs[n], refs[n + 1])
        x, y, c = _coords()
        for i in range(n):
            for j, k in enumerate(peers):
                cp = _slot_copy(lands[i], group_sems, len(peers) * i + j, x, y, c, k, peers, arriving=True)
                cp.wait_send()
                cp.wait_recv()

    return pl.pallas_call(
        body, name=name, out_shape=[pltpu.HBM(w.shape, w.dtype) for w in stacks],
        in_specs=[HBM_SPEC] * n + [SEM_SPEC, SEM_SPEC] + [ANY_SPEC] * len(after), out_specs=[HBM_SPEC] * n,
        input_output_aliases={t: t for t in range(n)},
        compiler_params=pltpu.CompilerParams(has_side_effects=DATAFLOW),
    )(*stacks, sems[0], sems[1], *after)


N_PEERS = 7


def _peer(x, y, c, k):
    return x ^ (k >> 2), y ^ ((k >> 1) & 1), c ^ (k & 1)


def _reduce_copy(grad, land, sems, idx, x, y, c, k):
    px, py, pc = _peer(x, y, c, k)
    rh = grad.shape[1] // 2
    return pltpu.make_async_remote_copy(src_ref=grad.at[2 * px + py, pl.ds(pc * rh, rh), :], dst_ref=land.at[k - 1],
                                        send_sem=sems[0].at[idx], recv_sem=sems[1].at[idx], device_id=(px, py, pc),
                                        device_id_type=MESH)


def _reduce_start(name, grads):
    n = len(grads)

    def body(*refs):
        src, lands, sems, token = refs[:n], refs[n:2 * n], (refs[2 * n], refs[2 * n + 1]), refs[-1]
        x, y, c = _coords()
        for t in range(n):
            for k in range(1, N_PEERS + 1):
                _reduce_copy(src[t], lands[t], sems, N_PEERS * t + k - 1, x, y, c, k).start()
        token[...] = jnp.zeros_like(token)

    lands = [lax.empty((N_PEERS, g.shape[1] // 2, g.shape[2]), g.dtype) for g in grads]
    out_shape = [pltpu.SemaphoreType.DMA((N_PEERS * n,))] * 2
    out_shape += [pltpu.HBM(a.shape, a.dtype) for a in list(grads) + lands]
    out_shape.append(jax.ShapeDtypeStruct((8, 128), F32))
    res = pl.pallas_call(
        body, name=name, out_shape=out_shape, in_specs=[HBM_SPEC] * (2 * n),
        out_specs=[SEM_SPEC] * 2 + [HBM_SPEC] * (2 * n) + [pl.BlockSpec(memory_space=pltpu.VMEM)],
        input_output_aliases={t: 2 + t for t in range(2 * n)},
        compiler_params=pltpu.CompilerParams(has_side_effects=DATAFLOW),
    )(*[pltpu.with_memory_space_constraint(a, pltpu.HBM) for a in list(grads) + lands])
    return (res[0], res[1]), list(res[2:2 + n]), list(res[2 + n:2 + 2 * n]), res[-1]


def _reduce_wait(name, grads, lands, sems, after):
    n = len(grads)
    after = tuple(after)

    def body(*refs):
        src, dst, group_sems = refs[:n], refs[n:2 * n], (refs[2 * n], refs[2 * n + 1])
        x, y, c = _coords()
        for t in range(n):
            for k in range(1, N_PEERS + 1):
                cp = _reduce_copy(src[t], dst[t], group_sems, N_PEERS * t + k - 1, x, y, c, k)
                cp.wait_send()
                cp.wait_recv()

    res = pl.pallas_call(
        body, name=name, out_shape=[pltpu.HBM(a.shape, a.dtype) for a in list(grads) + list(lands)],
        in_specs=[HBM_SPEC] * (2 * n) + [SEM_SPEC, SEM_SPEC] + [ANY_SPEC] * len(after), out_specs=[HBM_SPEC] * (2 * n),
        input_output_aliases={t: t for t in range(2 * n)},
        compiler_params=pltpu.CompilerParams(has_side_effects=DATAFLOW),
    )(*grads, *lands, sems[0], sems[1], *after)
    return list(res[:n]), list(res[n:])


def _join_halves(name, halves, deps=()):
    n = len(halves)

    def body(*refs):
        src, dst = refs[:n], refs[n + len(deps):2 * n + len(deps)]
        send_sems, recv_sems = refs[-2:]
        x, y, c = _coords()
        cps = []
        for t in range(n):
            cp = pltpu.make_async_remote_copy(src_ref=src[t], dst_ref=dst[t], send_sem=send_sems.at[t],
                                              recv_sem=recv_sems.at[t], device_id=(x, y, 1 - c), device_id_type=MESH)
            cp.start()
            cps.append(cp)
        for cp in cps:
            cp.wait()

    anyspec = pl.BlockSpec(memory_space=pl.ANY)
    return pl.pallas_call(
        body, name=name, out_shape=[jax.ShapeDtypeStruct(h.shape, h.dtype) for h in halves],
        in_specs=[anyspec] * (n + len(deps)), out_specs=[anyspec] * n,
        scratch_shapes=[pltpu.SemaphoreType.DMA((n,)), pltpu.SemaphoreType.DMA((n,))],
    )(*halves, *deps)


def _row_block(rows):
    for rb in (512, 256, 128, 64, 32, 16):
        if rows % rb == 0:
            return rb
    raise ValueError(rows)


def _sum_devices(name, grad, land, place):
    S, R, C = grad.shape
    rh = R // 2
    rb = _row_block(rh)
    nbh = rh // rb

    def body(place_ref, g_ref, l_ref, o_ref):
        tot = g_ref[...].astype(F32)
        for k in range(N_PEERS):
            tot = tot + l_ref[k].astype(F32)
        o_ref[...] = tot

    return pl.pallas_call(
        body, name=name,
        grid_spec=pltpu.PrefetchScalarGridSpec(
            num_scalar_prefetch=1, grid=(nbh,),
            in_specs=[pl.BlockSpec((None, rb, C), lambda r, place: (place[0], place[1] * nbh + r, 0)),
                      pl.BlockSpec((N_PEERS, rb, C), lambda r, place: (0, r, 0))],
            out_specs=pl.BlockSpec((rb, C), lambda r, place: (r, 0))),
        out_shape=jax.ShapeDtypeStruct((rh, C), F32), compiler_params=_params(1),
    )(place, grad, land)


def _adamw_math(w, g, m, v):
    m2 = ADAM_B1 * m + (1.0 - ADAM_B1) * g
    v2 = ADAM_B2 * v + (1.0 - ADAM_B2) * (g * g)
    m_hat = m2 / (1.0 - ADAM_B1 ** ADAM_STEP)
    v_hat = v2 / (1.0 - ADAM_B2 ** ADAM_STEP)
    delta = -ADAM_LR * (m_hat / (jnp.sqrt(v_hat) + ADAM_EPS) + ADAM_WD * w)
    return delta, m2, v2


def _adamw(name, w, m, v, gs):
    L, R, C = w.shape
    Rh = R // 2
    rb = _row_block(Rh)
    nbh = Rh // rb
    assert len(gs) == L

    def body(core_ref, w_ref, m_ref, v_ref, *rest):
        g_refs, (go_ref, d_ref, m2_ref, v2_ref) = rest[:2 * L], rest[2 * L:]
        layer, half = pl.program_id(0), pl.program_id(1)
        mine = half == core_ref[0]
        g = jnp.where(mine, g_refs[0][...], g_refs[1][...])
        for t in range(1, L):
            g = jnp.where(layer == t, jnp.where(mine, g_refs[2 * t][...], g_refs[2 * t + 1][...]), g)
        delta, m2, v2 = _adamw_math(w_ref[...], g, m_ref[...], v_ref[...])
        go_ref[...] = g
        d_ref[...] = delta
        m2_ref[...] = m2
        v2_ref[...] = v2

    wspec = pl.BlockSpec((None, rb, C), lambda l, h, r, core: (l, h * nbh + r, 0))
    gspec = pl.BlockSpec((rb, C), lambda l, h, r, core: (r, 0))
    return pl.pallas_call(
        body, name=name,
        grid_spec=pltpu.PrefetchScalarGridSpec(num_scalar_prefetch=1, grid=(L, 2, nbh),
                                               in_specs=[wspec] * 3 + [gspec] * (2 * L), out_specs=[wspec] * 4),
        out_shape=[jax.ShapeDtypeStruct((L, R, C), F32)] * 4, compiler_params=_params(3),
    )(lax.axis_index("c").astype(jnp.int32).reshape(1), w, m, v, *[g for pair in gs for g in pair])


def _adamw_small(ws, gs, ms, vs):
    n = len(ws)

    def body(*refs):
        w_refs, g_refs, m_refs, v_refs = (refs[k * n:(k + 1) * n] for k in range(4))
        d_refs, m2_refs, v2_refs = (refs[(4 + k) * n:(5 + k) * n] for k in range(3))
        for t in range(n):
            delta, m2, v2 = _adamw_math(w_refs[t][...], g_refs[t][...], m_refs[t][...], v_refs[t][...])
            d_refs[t][...] = delta
            m2_refs[t][...] = m2
            v2_refs[t][...] = v2

    res = pl.pallas_call(body, name="adamw_small", out_shape=[jax.ShapeDtypeStruct(w.shape, F32) for w in ws] * 3)(
        *ws, *gs, *ms, *vs)
    return res[:n], res[n:2 * n], res[2 * n:]


def _packed_rows(shape):
    c = shape[-1]
    return (int(np.prod(shape)) // c) * -(-c // LANES)


def _pack(arrays):
    total = sum(_packed_rows(a.shape) for a in arrays)
    total += -total % 8
    buf, r0 = None, 0
    for a in arrays:
        a = a.astype(F32).reshape(-1, a.shape[-1])
        r, c = a.shape
        k = -(-c // LANES)
        a = jnp.pad(a, ((0, 0), (0, k * LANES - c))).reshape(r * k, LANES)
        a = jnp.pad(a, ((r0, total - r0 - r * k), (0, 0)))
        buf = a if buf is None else buf + a
        r0 += r * k
    return buf


def _unpack(buf, shapes):
    out, r0 = [], 0
    for shp in shapes:
        c = shp[-1]
        rows = _packed_rows(shp)
        out.append(buf[r0:r0 + rows].reshape(-1, -(-c // LANES) * LANES)[:, :c].reshape(shp))
        r0 += rows
    return out


def _rms(x, g):
    return x * lax.rsqrt(jnp.mean(x * x, axis=-1, keepdims=True) + NORM_EPS) * g


def _residual_norm_ep(acc, *rest):
    *bias, res, gain = rest
    x = acc + res + (bias[0] if bias else 0.0)
    return x, _rms(x, gain)


RESIDUAL_NORM_OUTS = (("tile", F32), ("tile", BF))


def _mlp_up(tag, h, w_up_sm):
    (up,) = _mm(f"mlp{tag}_up", h, w_up_sm, nt=False, b_sm=True, tm=2048, tn=1024, rows=256,
                ep_fn=lambda acc: (acc,), outs=(("tile", BF),))
    return up


RMS_BWD_OUTS = (("tile", F32), ("tile", BF), ("colsum", F32), ("colsum", F32))


def _mlp_bwd(tag, dy, dy_bf, x, g, up, w_up_sm, w_down):
    (dup,) = _mm(f"mlp{tag}_dup", dy_bf, w_down, nt=True, tm=2048, tn=1024, rows=256, ep_in=((up, "tile"),),
                 ep_fn=lambda acc, u: (acc * (2.0 * jnp.maximum(u.astype(F32), 0.0)),), outs=(("tile", BF),))
    dx, dx_bf, dg, dx_sum = _mm(f"mlp{tag}_dx", dup, w_up_sm, nt=True, b_sm=True, tm=512, tn=1024, rows=256,
                                ep_in=((x, "tile"), (g, "row"), (dy, "tile")), ep_fn=_rms_bwd_ep, outs=RMS_BWD_OUTS)
    return dx, dx_bf, dg, dx_sum, dup


class _Reduction:
    def __init__(self, tag, grads, place):
        self.tag, self.place = tag, place
        self.sems, self.grads, self.lands, self.token = _reduce_start(f"reduce_start_{tag}", grads)

    def finish(self, after):
        grads, lands = _reduce_wait(f"reduce_wait_{self.tag}", self.grads, self.lands, self.sems, after)
        return [_sum_devices(f"reduce_sum_{self.tag}{i}", g, l, self.place) for i, (g, l) in enumerate(zip(grads, lands))]


def kernel(x, conv_norm_g, conv_w_in, conv_b_in, conv_dw, conv_dw_b, conv_ln_g, conv_ln_b, conv_w_out, conv_b_out, attn_norm_g, w_qkv, b_qkv, q_norm_g, k_norm_g, sinks, w_o, b_o, rel_bias, mlp_norm_g, w_up, w_down, loss_target, m_conv_norm_g, m_conv_w_in, m_conv_b_in, m_conv_dw, m_conv_dw_b, m_conv_ln_g, m_conv_ln_b, m_conv_w_out, m_conv_b_out, m_attn_norm_g, m_w_qkv, m_b_qkv, m_q_norm_g, m_k_norm_g, m_sinks, m_w_o, m_b_o, m_rel_bias, m_mlp_norm_g, m_w_up, m_w_down, v_conv_norm_g, v_conv_w_in, v_conv_b_in, v_conv_dw, v_conv_dw_b, v_conv_ln_g, v_conv_ln_b, v_conv_w_out, v_conv_b_out, v_attn_norm_g, v_w_qkv, v_b_qkv, v_q_norm_g, v_k_norm_g, v_sinks, v_w_o, v_b_o, v_rel_bias, v_mlp_norm_g, v_w_up, v_w_down):
    Dm = D_MODEL
    x2d = x[0]
    tgt = loss_target[0]
    T = x2d.shape[0]
    shard = 2 * lax.axis_index("x") + lax.axis_index("y")

    me = 2 * shard + lax.axis_index("c")

    def own_slot(block, slots, index):
        return lax.dynamic_update_slice(lax.empty((slots,) + block.shape, block.dtype), block[None],
                                        (index,) + (0,) * block.ndim)

    (conv_in_sems,), (stack_in,), first_token = _gather_start(
        "gather_start_conv_in", [own_slot(conv_w_in[0].astype(BF), N_SHARD, shard)], ((0,),), OTHER_CHIPS, after=())
    sharded_small = [conv_dw[0], attn_norm_g, b_qkv, b_o]
    (small_sems,), (small_land,), small_token = _gather_start(
        "small_weights_start", [own_slot(_pack(sharded_small), 8, me)], ((0,),), ALL_OTHERS, after=(first_token,))

    big = [conv_w_out[0], w_qkv[0], w_o[0], w_up[0], w_up[1], w_down[0], w_down[1]]
    stacks = [own_slot(w.astype(BF), N_SHARD, shard) for w in big]
    groups = ((0,), (3, 5), (1, 2), (4, 6))
    gather_sems, stacks, gather_token = _gather_start("gather_start", stacks, groups, OTHER_CHIPS, after=(small_token,))

    def gathered_group(g, name, after):
        return _gather_wait(name, [stacks[t] for t in groups[g]], gather_sems[g], OTHER_CHIPS, after)

    bucket = _bucket_table()
    bias = _bias_table(rel_bias, bucket)

    h0 = _rms_fwd("conv_norm", x2d, conv_norm_g, deps=(gather_token,))
    (w_in_sm,) = _gather_wait("gather_wait_conv_in", [stack_in], conv_in_sems, OTHER_CHIPS, (h0, bias))
    (u,) = _mm("conv_in", h0, w_in_sm, nt=False, b_sm=True, tm=2048, tn=512, rows=256, ep_in=((conv_b_in, "row"),),
               ep_fn=lambda acc, b: (acc + b,), outs=(("tile", BF),))
    (gathered,) = _gather_wait("small_weights_wait", [small_land], small_sems, ALL_OTHERS, (u,))
    chips = [_unpack(gathered[2 * s], [a.shape for a in sharded_small]) for s in range(N_SHARD)]
    dw_f, attn_norm_f, b_qkv_f, b_o_f = (jnp.concatenate([chips[s][t] for s in range(N_SHARD)], axis=-1)
                                         for t in range(len(sharded_small)))
    dw_pad = jnp.pad(dw_f, ((0, HALO - CONV_W), (0, 0)))
    cv, s_act = _conv_fwd(u, dw_pad, conv_dw_b, conv_ln_g, conv_ln_b)
    (g_out,) = gathered_group(0, "gather_wait_conv_out", (s_act,))
    w_out_f = g_out.reshape(Dm, Dm)
    x1, h1 = _mm("conv_out", s_act, w_out_f, nt=False, tm=1024, tn=1024, rows=256,
                 ep_in=((conv_b_out, "row"), (x2d, "tile"), (mlp_norm_g[0:1], "row")), ep_fn=_residual_norm_ep,
                 outs=RESIDUAL_NORM_OUTS)

    g_up0, g_down0 = gathered_group(1, "gather_wait_mlp0", (x1,))
    w_up_sm = [g_up0, None]
    w_down_f = [g_down0.reshape(D_FF, Dm), None]
    up0 = _mlp_up(0, h1, w_up_sm[0])
    x2, h2 = _mm("mlp0_down", up0, w_down_f[0], nt=False, tm=512, tn=1024, rows=256, a_fn=_relu2,
                 ep_in=((x1, "tile"), (attn_norm_f, "row")), ep_fn=_residual_norm_ep, outs=RESIDUAL_NORM_OUTS)

    g_qkv, g_o = gathered_group(2, "gather_wait_attn", (x2,))
    w_qkv_f = jnp.transpose(g_qkv, (1, 0, 2)).reshape(Dm, QKV_DIM)
    w_o_f = g_o.reshape(ATTN_DIM, Dm)
    qg_t = jnp.tile(q_norm_g, (1, N_HEADS))
    kg_t = jnp.tile(k_norm_g, (1, N_KV))

    def qkv_ep(acc, b, qg, kg, ones):
        proj = acc + b
        q, k, v = proj[:, :ATTN_DIM], proj[:, ATTN_DIM:ATTN_DIM + KV_DIM], proj[:, ATTN_DIM + KV_DIM:]
        return proj, _qk_normed(q, qg, ones, 1.0 / math.sqrt(HEAD_DIM)), _qk_normed(k, kg, ones, 1.0), v

    qkv, qn, kn, vv = _mm(
        "attn_qkv", h2, w_qkv_f, nt=False, tm=1024, tn=QKV_DIM, rows=256, ep_fn=qkv_ep,
        ep_in=((b_qkv_f, "row"), (qg_t, "whole"), (kg_t, "whole"), (_head_ones(), "whole")),
        outs=(("tile", F32), ("tile", BF, ATTN_DIM), ("tile", BF, KV_DIM), ("tile", BF, KV_DIM)))
    sinks1 = sinks[0]
    att = _attn_fwd(qn, kn, vv, bias, sinks1)
    x3, h3 = _mm("attn_out", att, w_o_f, nt=False, tm=1024, tn=1024, rows=256,
                 ep_in=((b_o_f, "row"), (x2, "tile"), (mlp_norm_g[1:2], "row")), ep_fn=_residual_norm_ep,
                 outs=RESIDUAL_NORM_OUTS)

    g_up1, g_down1 = gathered_group(3, "gather_wait_mlp1", (x3,))
    w_up_sm[1] = g_up1
    w_down_f[1] = g_down1.reshape(D_FF, Dm)
    up1 = _mlp_up(1, h3, w_up_sm[1])

    def loss_ep(acc, r, t):
        diff = acc + r - t
        dy = diff * (1.0 / Dm)
        return dy, dy, jnp.sum(diff * diff, axis=0, keepdims=True)

    dy, dy_bf, sq = _mm("mlp1_down_loss", up1, w_down_f[1], nt=False, tm=512, tn=1024, rows=256, a_fn=_relu2,
                        ep_in=((x3, "tile"), (tgt, "tile")), ep_fn=loss_ep,
                        outs=(("tile", F32), ("tile", BF), ("colsum", F32)))

    place = jnp.stack([shard, lax.axis_index("c")]).astype(jnp.int32)
    dx3, dx3_bf, dg_mlp1, db_o, dup1 = _mlp_bwd(1, dy, dy_bf, x3, mlp_norm_g[1:2], up1, w_up_sm[1], w_down_f[1])
    dw_down1 = _mm_tn("mlp1_dw_down", up1, dy_bf, tm=1024, tn=1024, tk=2048, a_fn=_relu2)
    dw_up1 = _mm_tn("mlp1_dw_up", h3, dup1, tm=1024, tn=1024, tk=2048, out_sm=N_SHARD)
    red_mlp1 = _Reduction("mlp1", [dw_up1, dw_down1.reshape(N_SHARD, D_FF // N_SHARD, Dm)], place)

    ident = lambda acc: (acc,)
    (datt,) = _mm("attn_dout", dx3_bf, w_o_f, nt=True, tm=1024, tn=1024, rows=256, ep_fn=ident, outs=(("tile", BF),),
                  deps=(red_mlp1.token,))
    dw_o = _mm_tn("attn_dw_o", att, dx3_bf, tm=1024, tn=1024, tk=2048)
    dqn, dkn, dvv, dbias, dsinks = _attn_bwd(qn, kn, vv, bias, sinks1, datt)
    drel = _bias_grad(dbias, bucket)
    dqkv, db_qkv, dqg_t, dkg_t = _qk_norm_bwd(qkv, dqn, dkn, dvv, qg_t, kg_t)
    dw_qkv = _mm_tn("attn_dw_qkv", h2, dqkv, tm=1024, tn=QKV_DIM, tk=2048)
    red_attn = _Reduction("attn", [jnp.transpose(dw_qkv.reshape(Dm, N_SHARD, QKV_DIM // N_SHARD), (1, 0, 2)),
                                   dw_o.reshape(N_SHARD, ATTN_DIM // N_SHARD, Dm)], place)
    dx2, dx2_bf, dg_attn, _ = _mm("attn_dx", dqkv, w_qkv_f, nt=True, tm=1024, tn=1024, rows=256,
                                  ep_in=((x2, "tile"), (attn_norm_f, "row"), (dx3, "tile")), ep_fn=_rms_bwd_ep,
                                  outs=RMS_BWD_OUTS, deps=(red_attn.token,))

    dx1, dx1_bf, dg_mlp0, db_out, dup0 = _mlp_bwd(0, dx2, dx2_bf, x1, mlp_norm_g[0:1], up0, w_up_sm[0], w_down_f[0])
    dw_down0 = _mm_tn("mlp0_dw_down", up0, dx2_bf, tm=1024, tn=1024, tk=2048, a_fn=_relu2)
    dw_up0 = _mm_tn("mlp0_dw_up", h1, dup0, tm=1024, tn=1024, tk=2048, out_sm=N_SHARD)
    dw_out = _mm_tn("conv_dw_out", s_act, dx1_bf, tm=1024, tn=1024, tk=2048)
    red_mlp0 = _Reduction("mlp0", [dw_up0, dw_down0.reshape(N_SHARD, D_FF // N_SHARD, Dm),
                                   dw_out.reshape(N_SHARD, Dm // N_SHARD, Dm)], place)
    (r_qkv, r_o) = red_attn.finish((dx1,))
    (r_up1, r_down1) = red_mlp1.finish((dx1,))

    dcv, dln_g, dln_b, ddw_b = _mm("conv_ds", dx1_bf, w_out_f, nt=True, tm=1024, tn=1024, rows=256,
                                   ep_in=((cv, "tile"), (conv_ln_g, "row"), (conv_ln_b, "row")),
                                   ep_fn=_ln_silu_bwd_ep,
                                   outs=(("tile", F32), ("colsum", F32), ("colsum", F32), ("colsum", F32)),
                                   deps=(red_mlp0.token,))
    du, db_in, ddw8 = _conv_bwd(u, dcv, dw_pad)
    (r_up0, r_down0, r_out) = red_mlp0.finish((du,))
    dw_in = _mm_tn("conv_dw_in", h0, du, tm=1024, tn=512, tk=4096, out_sm=N_SHARD)
    red_conv = _Reduction("conv", [dw_in], place)
    def first_layer_ep(*args):
        tot, _, dg, _ = _rms_bwd_ep(*args)
        return tot, dg

    gx, dg_conv = _mm("conv_dx", du, w_in_sm, nt=True, b_sm=True, tm=1024, tn=1024, rows=256,
                      ep_in=((x2d, "tile"), (conv_norm_g, "row"), (dx1, "tile")), ep_fn=first_layer_ep,
                      outs=(("tile", F32), ("colsum", F32)), deps=(red_conv.token,))
    (r_in,) = red_conv.finish((gx,))

    dqg = dqg_t.reshape(N_HEADS, HEAD_DIM).sum(axis=0, keepdims=True)
    dkg = dkg_t.reshape(N_KV, HEAD_DIM).sum(axis=0, keepdims=True)
    small_full = [dg_conv, db_in, ddw8.sum(axis=1)[:CONV_W], ddw_b, dln_g, dln_b, db_out, dg_attn, db_qkv, dqg, dkg,
                  dsinks[None, :], db_o, drel.reshape(1, REL_BUCKETS * N_HEADS),
                  jnp.pad(dg_mlp0, ((0, 1), (0, 0))) + jnp.pad(dg_mlp1, ((1, 0), (0, 0))), sq]
    (sg_sems,), (sg_land,), sg_token = _gather_start(
        "small_grads_start", [own_slot(_pack(small_full), 8, me)], ((0,),), ALL_OTHERS, after=())

    mine = [r_in, r_out, r_qkv, r_o, r_up0, r_up1, r_down0, r_down1]
    r_in, r_out, r_qkv, r_o, r_up0, r_up1, r_down0, r_down1 = zip(
        mine, _join_halves("join_halves", mine, deps=(sg_token,)))

    big_out = {}
    for nm, w, m, v, gs in (("conv_w_in", conv_w_in, m_conv_w_in, v_conv_w_in, (r_in,)),
                            ("conv_w_out", conv_w_out, m_conv_w_out, v_conv_w_out, (r_out,)),
                            ("w_qkv", w_qkv, m_w_qkv, v_w_qkv, (r_qkv,)),
                            ("w_o", w_o, m_w_o, v_w_o, (r_o,)),
                            ("w_up", w_up, m_w_up, v_w_up, (r_up0, r_up1)),
                            ("w_down", w_down, m_w_down, v_w_down, (r_down0, r_down1))):
        big_out[nm] = _adamw(f"adamw_{nm}", w, m, v, gs)

    (sg_land,) = _gather_wait("small_grads_wait", [sg_land], sg_sems, ALL_OTHERS,
                              [big_out[nm][0] for nm in big_out])
    small_sum = _sum8("small_grads_sum", sg_land)
    (r_norm, r_b_in, r_dw, r_dw_b, r_ln_g, r_ln_b, r_b_out, r_attn_norm, r_b_qkv, r_qg, r_kg, r_sinks, r_b_o, r_rel,
     r_mlp_norm, r_sq) = _unpack(small_sum, [a.shape for a in small_full])
    loss = 0.5 * jnp.sum(r_sq) * (1.0 / Dm)

    def cols(a, width):
        return lax.dynamic_slice_in_dim(a, shard * width, width, axis=a.ndim - 1)

    small_names = ["conv_norm_g", "conv_b_in", "conv_dw", "conv_dw_b", "conv_ln_g", "conv_ln_b", "conv_b_out",
                   "attn_norm_g", "b_qkv", "q_norm_g", "k_norm_g", "sinks", "b_o", "rel_bias", "mlp_norm_g"]
    small_g = [r_norm, r_b_in, cols(r_dw, Dm // N_SHARD)[None], r_dw_b, r_ln_g, r_ln_b, r_b_out,
               cols(r_attn_norm, Dm // N_SHARD), cols(r_b_qkv, QKV_DIM // N_SHARD), r_qg, r_kg, r_sinks,
               cols(r_b_o, Dm // N_SHARD), r_rel.reshape(REL_BUCKETS, N_HEADS), r_mlp_norm]
    small_w = [conv_norm_g, conv_b_in, conv_dw, conv_dw_b, conv_ln_g, conv_ln_b, conv_b_out, attn_norm_g, b_qkv,
               q_norm_g, k_norm_g, sinks, b_o, rel_bias, mlp_norm_g]
    small_m = [m_conv_norm_g, m_conv_b_in, m_conv_dw, m_conv_dw_b, m_conv_ln_g, m_conv_ln_b, m_conv_b_out,
               m_attn_norm_g, m_b_qkv, m_q_norm_g, m_k_norm_g, m_sinks, m_b_o, m_rel_bias, m_mlp_norm_g]
    small_v = [v_conv_norm_g, v_conv_b_in, v_conv_dw, v_conv_dw_b, v_conv_ln_g, v_conv_ln_b, v_conv_b_out,
               v_attn_norm_g, v_b_qkv, v_q_norm_g, v_k_norm_g, v_sinks, v_b_o, v_rel_bias, v_mlp_norm_g]
    flat2 = lambda a: a.reshape(-1, a.shape[-1])
    small_g = [flat2(g) for g in small_g]
    d_s, m_s, v_s = _adamw_small([flat2(w) for w in small_w], small_g, [flat2(m) for m in small_m],
                                 [flat2(v) for v in small_v])
    small_out = {}
    for nm, w, g, d, m2, v2 in zip(small_names, small_w, small_g, d_s, m_s, v_s):
        small_out[nm] = tuple(a.reshape(w.shape) for a in (g, d, m2, v2))

    order = ["conv_norm_g", "conv_w_in", "conv_b_in", "conv_dw", "conv_dw_b", "conv_ln_g", "conv_ln_b", "conv_w_out",
             "conv_b_out", "attn_norm_g", "w_qkv", "b_qkv", "q_norm_g", "k_norm_g", "sinks", "w_o", "b_o", "rel_bias",
             "mlp_norm_g", "w_up", "w_down"]
    res = {**small_out, **big_out}
    outs = [loss, gx[None]]
    for part in range(4):
        outs += [res[nm][part] for nm in order]
    return tuple(outs)
```

```python
import math

import numpy as np
import jax
import jax.numpy as jnp
from jax import lax
from jax.experimental import pallas as pl
from jax.experimental.pallas import tpu as pltpu

F32 = jnp.float32
BF = jnp.bfloat16
MESH = pl.DeviceIdType.MESH

D_MODEL = 1024
D_FF = 4096
N_HEADS = 16
N_KV = 2
GROUP = N_HEADS // N_KV
HEAD_DIM = 64
ATTN_DIM = N_HEADS * HEAD_DIM
KV_DIM = N_KV * HEAD_DIM
QKV_DIM = ATTN_DIM + 2 * KV_DIM
BLOCK = 128
CONV_W = 31
HALO = 32
REL_BUCKETS = 32
REL_MAX_DIST = 128
NORM_EPS = 1e-6
NEG_INF = -1e30
N_SHARD = 4
LANES = 1024

ADAM_LR = 0.001
ADAM_B1 = 0.9
ADAM_B2 = 0.999
ADAM_EPS = 1e-08
ADAM_WD = 0.01
ADAM_STEP = 10

VMEM_LIMIT = 56 * 1024 * 1024


def _params(n_axes):
    return pltpu.CompilerParams(dimension_semantics=("arbitrary",) * n_axes, vmem_limit_bytes=VMEM_LIMIT)


def _dot(a, b, ca, cb):
    return lax.dot_general(a, b, (((ca,), (cb,)), ((), ())), preferred_element_type=F32)


def _mm(name, a, b, *, nt, tm, tn, ep_fn, outs, a_fn=None, b_sm=False, ep_in=(), deps=(), rows=None):
    M, K = a.shape
    rows = tm if rows is None else rows
    if b_sm:
        S, ks = b.shape[0], b.shape[2]
        N, per = (b.shape[1], None) if nt else (S * b.shape[2], b.shape[2] // tn)
        assert (S * ks == K) if nt else (b.shape[1] == K)
    else:
        N = b.shape[0] if nt else b.shape[1]
        assert (b.shape[1] if nt else b.shape[0]) == K
    assert M % tm == 0 and N % tn == 0 and tm % rows == 0
    ne, no, nd = len(ep_in), len(outs), len(deps)

    def body(a_ref, b_ref, *rest):
        ep_refs, out_refs = rest[:ne], rest[ne + nd:ne + nd + no]
        i = pl.program_id(1)
        sums = [None] * no
        for r in range(tm // rows):
            rs = pl.ds(r * rows, rows)

            def lhs(cols):
                av = a_ref[rs, cols]
                return (av if a_fn is None else a_fn(av)).astype(BF)

            if b_sm and nt:
                acc = None
                for s in range(S):
                    part = _dot(lhs(pl.ds(s * ks, ks)), b_ref[s].astype(BF), 1, 1)
                    acc = part if acc is None else acc + part
            else:
                acc = _dot(lhs(slice(None)), b_ref[...].astype(BF), 1, 1 if nt else 0)
            ep_vals = [ref[rs, :] if kind == "tile" else ref[...] for ref, (_, kind) in zip(ep_refs, ep_in)]
            vals = ep_fn(acc, *ep_vals)
            for o, ((kind, dt, *_), ref, val) in enumerate(zip(outs, out_refs, vals)):
                if kind == "tile":
                    ref[rs, :] = val.astype(dt)
                else:
                    sums[o] = val if sums[o] is None else sums[o] + val
        for (kind, *_), ref, val in zip(outs, out_refs, sums):
            if kind == "colsum":
                @pl.when(i == 0)
                def _():
                    ref[...] = val

                @pl.when(i > 0)
                def _():
                    ref[...] += val

    if b_sm and nt:
        b_spec = pl.BlockSpec((S, tn, ks), lambda j, i: (0, j, 0))
    elif b_sm:
        b_spec = pl.BlockSpec((None, K, tn), lambda j, i: (j // per, 0, j % per))
    elif nt:
        b_spec = pl.BlockSpec((tn, K), lambda j, i: (j, 0))
    else:
        b_spec = pl.BlockSpec((K, tn), lambda j, i: (0, j))
    in_specs = [pl.BlockSpec((tm, K), lambda j, i: (i, 0)), b_spec]
    for arr, kind in ep_in:
        if kind == "tile":
            assert arr.shape == (M, N)
            in_specs.append(pl.BlockSpec((tm, tn), lambda j, i: (i, j)))
        elif kind == "whole":
            in_specs.append(pl.BlockSpec(arr.shape, lambda j, i, rank=arr.ndim: (0,) * rank))
        else:
            assert arr.shape == (1, N)
            in_specs.append(pl.BlockSpec((1, tn), lambda j, i: (0, j)))
    in_specs += [pl.BlockSpec(memory_space=pl.ANY)] * nd
    out_shape, out_specs = [], []
    for kind, dt, *width in outs:
        if kind == "tile" and width:
            assert tn == N
            out_shape.append(jax.ShapeDtypeStruct((M, width[0]), dt))
            out_specs.append(pl.BlockSpec((tm, width[0]), lambda j, i: (i, 0)))
        elif kind == "tile":
            out_shape.append(jax.ShapeDtypeStruct((M, N), dt))
            out_specs.append(pl.BlockSpec((tm, tn), lambda j, i: (i, j)))
        else:
            out_shape.append(jax.ShapeDtypeStruct((1, N), F32))
            out_specs.append(pl.BlockSpec((1, tn), lambda j, i: (0, j)))
    return pl.pallas_call(
        body, name=name, grid=(N // tn, M // tm), in_specs=in_specs, out_specs=out_specs, out_shape=out_shape,
        compiler_params=_params(2),
    )(a, b, *[arr for arr, _ in ep_in], *deps)


def _mm_tn(name, a, b, *, tm, tn, tk, a_fn=None, out_sm=None):
    T, Ka = a.shape
    N = b.shape[1]
    assert b.shape[0] == T and T % tk == 0 and Ka % tm == 0 and N % tn == 0
    nk = T // tk

    def body(a_ref, b_ref, o_ref, acc_ref):
        k = pl.program_id(2)

        @pl.when(k == 0)
        def _():
            acc_ref[...] = jnp.zeros_like(acc_ref)

        av = a_ref[...]
        if a_fn is not None:
            av = a_fn(av)
        acc_ref[...] += _dot(av.astype(BF), b_ref[...].astype(BF), 0, 0)

        @pl.when(k == nk - 1)
        def _():
            o_ref[...] = acc_ref[...].astype(BF)

    if out_sm is None:
        out_shape = jax.ShapeDtypeStruct((Ka, N), BF)
        out_spec = pl.BlockSpec((tm, tn), lambda i, j, k: (i, j))
    else:
        per = (N // out_sm) // tn
        assert per * tn * out_sm == N
        out_shape = jax.ShapeDtypeStruct((out_sm, Ka, N // out_sm), BF)
        out_spec = pl.BlockSpec((None, tm, tn), lambda i, j, k: (j // per, i, j % per))
    return pl.pallas_call(
        body, name=name, grid=(Ka // tm, N // tn, nk),
        in_specs=[pl.BlockSpec((tk, tm), lambda i, j, k: (k, i)), pl.BlockSpec((tk, tn), lambda i, j, k: (k, j))],
        out_specs=out_spec, out_shape=out_shape, scratch_shapes=[pltpu.VMEM((tm, tn), F32)],
        compiler_params=_params(3),
    )(a, b)


def _relu2(v):
    r = jnp.maximum(v.astype(F32), 0.0)
    return r * r


def _rms_bwd_ep(dh, x, g, dres):
    rstd = lax.rsqrt(jnp.mean(x * x, axis=-1, keepdims=True) + NORM_EPS)
    xh = x * rstd
    dxh = dh * g
    dx = rstd * (dxh - xh * jnp.mean(dxh * xh, axis=-1, keepdims=True))
    tot = dres + dx
    return tot, tot, jnp.sum(dh * xh, axis=0, keepdims=True), jnp.sum(tot, axis=0, keepdims=True)


def _rms_fwd(name, x, g, tm=512, deps=()):
    T, Dm = x.shape

    def body(x_ref, g_ref, *rest):
        o_ref = rest[-1]
        xv = x_ref[...]
        rstd = lax.rsqrt(jnp.mean(xv * xv, axis=-1, keepdims=True) + NORM_EPS)
        o_ref[...] = (xv * rstd * g_ref[...]).astype(BF)

    return pl.pallas_call(
        body, name=name, grid=(T // tm,),
        in_specs=[pl.BlockSpec((tm, Dm), lambda i: (i, 0)), pl.BlockSpec((1, Dm), lambda i: (0, 0))]
        + [pl.BlockSpec(memory_space=pl.ANY)] * len(deps),
        out_specs=pl.BlockSpec((tm, Dm), lambda i: (i, 0)), out_shape=jax.ShapeDtypeStruct((T, Dm), BF),
        compiler_params=_params(1),
    )(x, g, *deps)


HEAD_GROUP = 256


def _two_term_dot(v, m):
    hi = v.astype(BF)
    lo = (v - hi.astype(F32)).astype(BF)
    return _dot(hi, m, 1, 0) + _dot(lo, m, 1, 0)


def _head_sum(v, ones):
    n = v.shape[1]
    w = min(n, HEAD_GROUP)
    blk = ones[:w, :w]
    parts = [_two_term_dot(v[:, c:c + w], blk) for c in range(0, n, w)]
    return parts[0] if len(parts) == 1 else jnp.concatenate(parts, axis=1)


def _head_ones():
    idx = np.arange(HEAD_GROUP) // HEAD_DIM
    return jnp.asarray((idx[:, None] == idx[None, :]).astype(np.float32), dtype=BF)


def _qk_normed(x, g, ones, scale):
    r = lax.rsqrt(_head_sum(x * x, ones) * (1.0 / HEAD_DIM) + NORM_EPS)
    return x * r * g * scale


def _qk_norm_bwd(qkv, dqn, dkn, dv, qg_t, kg_t, tm=256):
    T = qkv.shape[0]

    def body(x_ref, dq_ref, dk_ref, dv_ref, qg_ref, kg_ref, ones_ref, o_ref, db_ref, dqg_ref, dkg_ref):
        i = pl.program_id(0)
        ones = ones_ref[...]

        def one(x, dy, g):
            r = lax.rsqrt(_head_sum(x * x, ones) * (1.0 / HEAD_DIM) + NORM_EPS)
            xh = x * r
            dxh = dy * g
            dx = r * (dxh - xh * (_head_sum(dxh * xh, ones) * (1.0 / HEAD_DIM)))
            return dx, jnp.sum(dy * xh, axis=0, keepdims=True)

        dq, dqg = one(x_ref[:, pl.ds(0, ATTN_DIM)], dq_ref[...], qg_ref[...])
        dk, dkg = one(x_ref[:, pl.ds(ATTN_DIM, KV_DIM)], dk_ref[...], kg_ref[...])
        dvv = dv_ref[...]
        o_ref[:, pl.ds(0, ATTN_DIM)] = dq.astype(BF)
        o_ref[:, pl.ds(ATTN_DIM, KV_DIM)] = dk.astype(BF)
        o_ref[:, pl.ds(ATTN_DIM + KV_DIM, KV_DIM)] = dvv.astype(BF)
        sq, sk, sv = (jnp.sum(t, axis=0, keepdims=True) for t in (dq, dk, dvv))

        @pl.when(i == 0)
        def _():
            db_ref[:, pl.ds(0, ATTN_DIM)] = sq
            db_ref[:, pl.ds(ATTN_DIM, KV_DIM)] = sk
            db_ref[:, pl.ds(ATTN_DIM + KV_DIM, KV_DIM)] = sv
            dqg_ref[...] = dqg
            dkg_ref[...] = dkg

        @pl.when(i > 0)
        def _():
            db_ref[:, pl.ds(0, ATTN_DIM)] += sq
            db_ref[:, pl.ds(ATTN_DIM, KV_DIM)] += sk
            db_ref[:, pl.ds(ATTN_DIM + KV_DIM, KV_DIM)] += sv
            dqg_ref[...] += dqg
            dkg_ref[...] += dkg

    full = lambda shape: pl.BlockSpec(shape, lambda i: (0, 0))
    row = lambda n: pl.BlockSpec((tm, n), lambda i: (i, 0))
    return pl.pallas_call(
        body, name="qk_norm_bwd", grid=(T // tm,),
        in_specs=[row(QKV_DIM), row(ATTN_DIM), row(KV_DIM), row(KV_DIM), full((1, ATTN_DIM)), full((1, KV_DIM)),
                  full((HEAD_GROUP, HEAD_GROUP))],
        out_specs=[row(QKV_DIM), full((1, QKV_DIM)), full((1, ATTN_DIM)), full((1, KV_DIM))],
        out_shape=[jax.ShapeDtypeStruct((T, QKV_DIM), BF), jax.ShapeDtypeStruct((1, QKV_DIM), F32),
                   jax.ShapeDtypeStruct((1, ATTN_DIM), F32), jax.ShapeDtypeStruct((1, KV_DIM), F32)],
        compiler_params=_params(1),
    )(qkv, dqn, dkn, dv, qg_t, kg_t, _head_ones())


ROWS = 128
COLS = 128


SUBLANES = 8
FIRST_TAP = HALO - (CONV_W - 1)


def _glu(a, g):
    return a.astype(F32) * jax.nn.sigmoid(g.astype(F32))


def _shifted(xe, s):
    return xe if s == 0 else pltpu.roll(xe, ROWS + HALO - s, axis=0)


def _conv_fwd(u, dw_pad, dw_b, ln_g, ln_b, tm=512):
    T = u.shape[0]
    Dm = D_MODEL
    hpt = tm // HALO

    def body(ac_ref, gc_ref, ap_ref, gp_ref, w_ref, wb_ref, lg_ref, lb_ref, cv_ref, s_ref, ext):
        i = pl.program_id(0)
        ext[pl.ds(0, HALO), :] = jnp.where(i > 0, _glu(ap_ref[...], gp_ref[...]), 0.0)
        ext[pl.ds(HALO, tm), :] = _glu(ac_ref[...], gc_ref[...])

        def rows(r, carry):
            r0 = pl.multiple_of(r * ROWS, ROWS)
            for c in range(Dm // COLS):
                cs = pl.ds(c * COLS, COLS)
                xe = ext[pl.ds(r0, ROWS + HALO), cs]
                acc = jnp.zeros((ROWS, COLS), F32)
                for s in range(SUBLANES):
                    xs = _shifted(xe, s)
                    for j in range(CONV_W):
                        off = FIRST_TAP + j
                        if off % SUBLANES == s:
                            acc = acc + xs[off - s:off - s + ROWS, :] * w_ref[pl.ds(j, 1), cs]
                cv_ref[pl.ds(r0, ROWS), cs] = acc + wb_ref[:, cs]
            return carry

        lax.fori_loop(0, tm // ROWS, rows, 0)
        cv = cv_ref[...]
        xc = cv - jnp.mean(cv, axis=-1, keepdims=True)
        y = xc * lax.rsqrt(jnp.mean(xc * xc, axis=-1, keepdims=True) + NORM_EPS) * lg_ref[...] + lb_ref[...]
        s_ref[...] = (y * jax.nn.sigmoid(y)).astype(BF)

    full = lambda shape: pl.BlockSpec(shape, lambda i: (0, 0))
    return pl.pallas_call(
        body, name="conv_fwd", grid=(T // tm,),
        in_specs=[pl.BlockSpec((tm, Dm), lambda i: (i, 0)), pl.BlockSpec((tm, Dm), lambda i: (i, 1)),
                  pl.BlockSpec((HALO, Dm), lambda i: (jnp.maximum(i * hpt - 1, 0), 0)),
                  pl.BlockSpec((HALO, Dm), lambda i: (jnp.maximum(i * hpt - 1, 0), 1)),
                  full((HALO, Dm)), full((1, Dm)), full((1, Dm)), full((1, Dm))],
        out_specs=[pl.BlockSpec((tm, Dm), lambda i: (i, 0)), pl.BlockSpec((tm, Dm), lambda i: (i, 0))],
        out_shape=[jax.ShapeDtypeStruct((T, Dm), F32), jax.ShapeDtypeStruct((T, Dm), BF)],
        scratch_shapes=[pltpu.VMEM((tm + HALO, Dm), F32)],
        compiler_params=_params(1),
    )(u, u, u, u, dw_pad, dw_b, ln_g, ln_b)


def _ln_silu_bwd_ep(ds, cv, lg, lb):
    xc = cv - jnp.mean(cv, axis=-1, keepdims=True)
    rstd = lax.rsqrt(jnp.mean(xc * xc, axis=-1, keepdims=True) + NORM_EPS)
    xh = xc * rstd
    y = xh * lg + lb
    sg = jax.nn.sigmoid(y)
    dy = ds * (sg * (1.0 + y * (1.0 - sg)))
    dxh = dy * lg
    dcv = rstd * (dxh - jnp.mean(dxh, axis=-1, keepdims=True) - xh * jnp.mean(dxh * xh, axis=-1, keepdims=True))
    return (dcv, jnp.sum(dy * xh, axis=0, keepdims=True), jnp.sum(dy, axis=0, keepdims=True),
            jnp.sum(dcv, axis=0, keepdims=True))


def _conv_bwd(u, dcv, dw_pad, tm=512):
    T = u.shape[0]
    Dm = D_MODEL
    hpt = tm // HALO
    last = T // HALO - 1
    nt = T // tm

    def body(ac_ref, gc_ref, ap_ref, gp_ref, dc_ref, dn_ref, w_ref, du_ref, db_ref, dw_ref, ext_g, ext_d):
        i = pl.program_id(0)
        ext_g[pl.ds(0, HALO), :] = jnp.where(i > 0, _glu(ap_ref[...], gp_ref[...]), 0.0)
        ext_g[pl.ds(HALO, tm), :] = _glu(ac_ref[...], gc_ref[...])
        ext_d[pl.ds(0, tm), :] = dc_ref[...]
        ext_d[pl.ds(tm, HALO), :] = jnp.where(i < nt - 1, dn_ref[...], 0.0)

        @pl.when(i == 0)
        def _():
            db_ref[...] = jnp.zeros_like(db_ref)
            dw_ref[...] = jnp.zeros_like(dw_ref)

        def rows(r, carry):
            r0 = pl.multiple_of(r * ROWS, ROWS)
            rs = pl.ds(r0, ROWS)
            for c in range(Dm // COLS):
                cs = pl.ds(c * COLS, COLS)
                cs2 = pl.ds(Dm + c * COLS, COLS)
                de = ext_d[pl.ds(r0, ROWS + HALO), cs]
                ge = ext_g[pl.ds(r0, ROWS + HALO), cs]
                dcur = de[0:ROWS, :]
                acc = jnp.zeros((ROWS, COLS), F32)
                for s in range(SUBLANES):
                    ds_, gs_ = _shifted(de, s), _shifted(ge, s)
                    for j in range(CONV_W):
                        off = CONV_W - 1 - j
                        if off % SUBLANES == s:
                            acc = acc + ds_[off - s:off - s + ROWS, :] * w_ref[pl.ds(j, 1), cs]
                        goff = FIRST_TAP + j
                        if goff % SUBLANES == s:
                            prod = dcur * gs_[goff - s:goff - s + ROWS, :]
                            dw_ref[j, :, cs] += jnp.sum(prod.reshape(ROWS // SUBLANES, SUBLANES, COLS), axis=0)
                a = ac_ref[rs, cs].astype(F32)
                sg = jax.nn.sigmoid(gc_ref[rs, cs].astype(F32))
                da = acc * sg
                dg = acc * a * sg * (1.0 - sg)
                du_ref[rs, cs] = da.astype(BF)
                du_ref[rs, cs2] = dg.astype(BF)
                db_ref[:, cs] += jnp.sum(da, axis=0, keepdims=True)
                db_ref[:, cs2] += jnp.sum(dg, axis=0, keepdims=True)
            return carry

        lax.fori_loop(0, tm // ROWS, rows, 0)

    return pl.pallas_call(
        body, name="conv_bwd", grid=(nt,),
        in_specs=[pl.BlockSpec((tm, Dm), lambda i: (i, 0)), pl.BlockSpec((tm, Dm), lambda i: (i, 1)),
                  pl.BlockSpec((HALO, Dm), lambda i: (jnp.maximum(i * hpt - 1, 0), 0)),
                  pl.BlockSpec((HALO, Dm), lambda i: (jnp.maximum(i * hpt - 1, 0), 1)),
                  pl.BlockSpec((tm, Dm), lambda i: (i, 0)),
                  pl.BlockSpec((HALO, Dm), lambda i: (jnp.minimum((i + 1) * hpt, last), 0)),
                  pl.BlockSpec((HALO, Dm), lambda i: (0, 0))],
        out_specs=[pl.BlockSpec((tm, 2 * Dm), lambda i: (i, 0)), pl.BlockSpec((1, 2 * Dm), lambda i: (0, 0)),
                   pl.BlockSpec((HALO, 8, Dm), lambda i: (0, 0, 0))],
        out_shape=[jax.ShapeDtypeStruct((T, 2 * Dm), BF), jax.ShapeDtypeStruct((1, 2 * Dm), F32),
                   jax.ShapeDtypeStruct((HALO, 8, Dm), F32)],
        scratch_shapes=[pltpu.VMEM((tm + HALO, Dm), F32), pltpu.VMEM((tm + HALO, Dm), F32)],
        compiler_params=_params(1),
    )(u, u, u, u, dcv, dcv, dw_pad)


def _bucket_table():
    q_loc = np.arange(BLOCK)[:, None]
    k_loc = np.arange(2 * BLOCK)[None, :]
    dist = q_loc + BLOCK - k_loc
    n = np.maximum(dist, 0)
    max_exact = REL_BUCKETS // 2
    large = max_exact + (np.log(np.maximum(n, 1).astype(np.float32) / max_exact)
                         / math.log(REL_MAX_DIST / max_exact) * (REL_BUCKETS - max_exact)).astype(np.int32)
    large = np.minimum(large, REL_BUCKETS - 1)
    bucket = np.where(n < max_exact, n, large).astype(np.int32)
    band = np.where((dist >= 0) & (dist < BLOCK), bucket, -1)
    folded = np.where(np.arange(BLOCK)[None, :] > q_loc, band[:, :BLOCK], band[:, BLOCK:])
    assert (folded >= 0).all() and ((band[:, :BLOCK] >= 0) != (band[:, BLOCK:] >= 0)).all()
    return jnp.asarray(folded.astype(np.int32))


def _prev_mask():
    row = lax.broadcasted_iota(jnp.int32, (BLOCK, BLOCK), 0)
    col = lax.broadcasted_iota(jnp.int32, (BLOCK, BLOCK), 1)
    return col > row


def _fold(band, prev_mask):
    return jnp.where(prev_mask, band[:, :BLOCK], band[:, BLOCK:])


def _unfold(ref, g, rows, folded, prev_mask):
    ref[g, rows, pl.ds(0, BLOCK)] = jnp.where(prev_mask, folded, 0.0).astype(ref.dtype)
    ref[g, rows, pl.ds(BLOCK, BLOCK)] = jnp.where(prev_mask, 0.0, folded).astype(ref.dtype)


def _bias_table(rel_bias, bucket):
    def body(rb_ref, bk_ref, o_ref):
        bk = bk_ref[...]
        prev_mask = _prev_mask()
        for h in range(N_HEADS):
            acc = jnp.zeros((BLOCK, BLOCK), F32)
            for b in range(REL_BUCKETS):
                acc = jnp.where(bk == b, rb_ref[b, h], acc)
            o_ref[0, h] = acc
            o_ref[1, h] = jnp.where(prev_mask, NEG_INF, acc)

    return pl.pallas_call(
        body, name="bias_table", out_shape=jax.ShapeDtypeStruct((2, N_HEADS, BLOCK, BLOCK), F32),
        in_specs=[pl.BlockSpec(memory_space=pltpu.SMEM), pl.BlockSpec(memory_space=pltpu.VMEM)],
        out_specs=pl.BlockSpec(memory_space=pltpu.VMEM),
    )(rel_bias, bucket)


def _bias_grad(dbias, bucket):
    def body(db_ref, bk_ref, o_ref):
        bk = bk_ref[...]
        for b in range(REL_BUCKETS):
            sel = bk == b
            for h in range(N_HEADS):
                o_ref[b, h] = jnp.sum(jnp.where(sel, db_ref[h], 0.0))

    return pl.pallas_call(
        body, name="bias_grad", out_shape=jax.ShapeDtypeStruct((REL_BUCKETS, N_HEADS), F32),
        in_specs=[pl.BlockSpec(memory_space=pltpu.VMEM), pl.BlockSpec(memory_space=pltpu.VMEM)],
        out_specs=pl.BlockSpec(memory_space=pltpu.SMEM),
    )(dbias, bucket)


GROUP_ROWS = GROUP * BLOCK
BIAS_SPEC = pl.BlockSpec((2, N_HEADS, BLOCK, BLOCK), lambda n: (0, 0, 0, 0))


def _head_probs(qk, bias_h, sink, prev_mask):
    s = _fold(qk, prev_mask) + bias_h
    m = jnp.maximum(jnp.max(s, axis=-1, keepdims=True), sink)
    p = jnp.exp(s - m)
    ps = jnp.exp(sink - m)
    inv = 1.0 / (jnp.sum(p, axis=-1, keepdims=True) + ps)
    return p * inv, ps * inv


def _band(prev_ref, cur_ref, g):
    hs = pl.ds(g * HEAD_DIM, HEAD_DIM)
    return jnp.concatenate([prev_ref[:, hs], cur_ref[:, hs]], axis=0)


def _stack_heads(ref, g):
    return jnp.concatenate([ref[:, pl.ds((g * GROUP + hh) * HEAD_DIM, HEAD_DIM)] for hh in range(GROUP)], axis=0)


def _unstack_heads(ref, g, stacked, dtype):
    for hh in range(GROUP):
        ref[:, pl.ds((g * GROUP + hh) * HEAD_DIM, HEAD_DIM)] = stacked[hh * BLOCK:(hh + 1) * BLOCK, :].astype(dtype)


def _head_rows(hh):
    return pl.ds(hh * BLOCK, BLOCK)


def _attn_fwd(qn, kn, vv, bias, sinks):
    T = qn.shape[0]
    nb = T // BLOCK

    def body(sk_ref, q_ref, kc_ref, kp_ref, vc_ref, vp_ref, b_ref, o_ref, qk_buf, p_buf):
        table = (pl.program_id(0) == 0).astype(jnp.int32)
        prev_mask = _prev_mask()
        for g in range(N_KV):
            qk_buf[g] = _dot(_stack_heads(q_ref, g), _band(kp_ref, kc_ref, g), 1, 1)
        for g in range(N_KV):
            for hh in range(GROUP):
                h = g * GROUP + hh
                pn, _ = _head_probs(qk_buf[g, _head_rows(hh), :], b_ref[table, h], sk_ref[h], prev_mask)
                _unfold(p_buf, g, _head_rows(hh), pn, prev_mask)
        for g in range(N_KV):
            _unstack_heads(o_ref, g, _dot(p_buf[g], _band(vp_ref, vc_ref, g), 1, 0), BF)

    cur = lambda n: (n, 0)
    prev = lambda n: (jnp.maximum(n - 1, 0), 0)
    return pl.pallas_call(
        body, name="attn_fwd", grid=(nb,),
        in_specs=[pl.BlockSpec(memory_space=pltpu.SMEM), pl.BlockSpec((BLOCK, ATTN_DIM), cur),
                  pl.BlockSpec((BLOCK, KV_DIM), cur), pl.BlockSpec((BLOCK, KV_DIM), prev),
                  pl.BlockSpec((BLOCK, KV_DIM), cur), pl.BlockSpec((BLOCK, KV_DIM), prev), BIAS_SPEC],
        out_specs=pl.BlockSpec((BLOCK, ATTN_DIM), cur), out_shape=jax.ShapeDtypeStruct((T, ATTN_DIM), BF),
        scratch_shapes=[pltpu.VMEM((N_KV, GROUP_ROWS, 2 * BLOCK), F32), pltpu.VMEM((N_KV, GROUP_ROWS, 2 * BLOCK), BF)],
        compiler_params=_params(1),
    )(sinks, qn, kn, kn, vv, vv, bias)


def _attn_bwd(qn, kn, vv, bias, sinks, do):
    T = qn.shape[0]
    nb = T // BLOCK
    scale = 1.0 / math.sqrt(HEAD_DIM)

    def body(sk_ref, q_ref, kc_ref, kp_ref, vc_ref, vp_ref, b_ref, do_ref,
             dq_ref, dk_ref, dv_ref, db_ref, dsk_ref, dk_full, dv_full, dk_carry, dv_carry, qk_buf, dp_buf, p_buf, ds_buf):
        n = pl.program_id(0)

        @pl.when(n == 0)
        def _():
            db_ref[...] = jnp.zeros_like(db_ref)
            dk_carry[...] = jnp.zeros_like(dk_carry)
            dv_carry[...] = jnp.zeros_like(dv_carry)
            for h in range(N_HEADS):
                dsk_ref[h] = 0.0

        @pl.when(n < nb)
        def _():
            table = (n == 0).astype(jnp.int32)
            prev_mask = _prev_mask()
            ks = [_band(kp_ref, kc_ref, g) for g in range(N_KV)]
            qs = [_stack_heads(q_ref, g) for g in range(N_KV)]
            douts = [_stack_heads(do_ref, g) for g in range(N_KV)]
            for g in range(N_KV):
                qk_buf[g] = _dot(qs[g], ks[g], 1, 1)
                dp_buf[g] = _dot(douts[g], _band(vp_ref, vc_ref, g), 1, 1)
            for g in range(N_KV):
                for hh in range(GROUP):
                    h = g * GROUP + hh
                    rows = _head_rows(hh)
                    pn, psink = _head_probs(qk_buf[g, rows, :], b_ref[table, h], sk_ref[h], prev_mask)
                    dp = _fold(dp_buf[g, rows, :], prev_mask)
                    delta = jnp.sum(pn * dp, axis=-1, keepdims=True)
                    ds = pn * (dp - delta)
                    dsk_ref[h] += -jnp.sum(psink * delta)
                    db_ref[h] += ds
                    _unfold(ds_buf, g, rows, ds, prev_mask)
                    _unfold(p_buf, g, rows, pn, prev_mask)
            for g in range(N_KV):
                dsb = ds_buf[g]
                _unstack_heads(dq_ref, g, _dot(dsb, ks[g], 1, 0) * scale, F32)
                gs = pl.ds(g * HEAD_DIM, HEAD_DIM)
                dk_full[:, gs] = _dot(dsb, qs[g], 0, 0)
                dv_full[:, gs] = _dot(p_buf[g], douts[g], 0, 0)

        @pl.when(n == nb)
        def _():
            dk_full[...] = jnp.zeros_like(dk_full)
            dv_full[...] = jnp.zeros_like(dv_full)

        dk_ref[...] = dk_carry[...] + dk_full[pl.ds(0, BLOCK), :]
        dv_ref[...] = dv_carry[...] + dv_full[pl.ds(0, BLOCK), :]
        dk_carry[...] = dk_full[pl.ds(BLOCK, BLOCK), :]
        dv_carry[...] = dv_full[pl.ds(BLOCK, BLOCK), :]

    cur = lambda n: (jnp.minimum(n, nb - 1), 0)
    prev = lambda n: (jnp.maximum(jnp.minimum(n, nb - 1) - 1, 0), 0)
    out_kv = lambda n: (jnp.maximum(n - 1, 0), 0)
    return pl.pallas_call(
        body, name="attn_bwd", grid=(nb + 1,),
        in_specs=[pl.BlockSpec(memory_space=pltpu.SMEM), pl.BlockSpec((BLOCK, ATTN_DIM), cur),
                  pl.BlockSpec((BLOCK, KV_DIM), cur), pl.BlockSpec((BLOCK, KV_DIM), prev),
                  pl.BlockSpec((BLOCK, KV_DIM), cur), pl.BlockSpec((BLOCK, KV_DIM), prev), BIAS_SPEC,
                  pl.BlockSpec((BLOCK, ATTN_DIM), cur)],
        out_specs=[pl.BlockSpec((BLOCK, ATTN_DIM), cur), pl.BlockSpec((BLOCK, KV_DIM), out_kv),
                   pl.BlockSpec((BLOCK, KV_DIM), out_kv),
                   pl.BlockSpec((N_HEADS, BLOCK, BLOCK), lambda n: (0, 0, 0)),
                   pl.BlockSpec(memory_space=pltpu.SMEM)],
        out_shape=[jax.ShapeDtypeStruct((T, ATTN_DIM), F32), jax.ShapeDtypeStruct((T, KV_DIM), F32),
                   jax.ShapeDtypeStruct((T, KV_DIM), F32),
                   jax.ShapeDtypeStruct((N_HEADS, BLOCK, BLOCK), F32), jax.ShapeDtypeStruct((N_HEADS,), F32)],
        scratch_shapes=[pltpu.VMEM((2 * BLOCK, KV_DIM), F32), pltpu.VMEM((2 * BLOCK, KV_DIM), F32),
                        pltpu.VMEM((BLOCK, KV_DIM), F32), pltpu.VMEM((BLOCK, KV_DIM), F32),
                        pltpu.VMEM((N_KV, GROUP_ROWS, 2 * BLOCK), F32), pltpu.VMEM((N_KV, GROUP_ROWS, 2 * BLOCK), F32),
                        pltpu.VMEM((N_KV, GROUP_ROWS, 2 * BLOCK), BF), pltpu.VMEM((N_KV, GROUP_ROWS, 2 * BLOCK), BF)],
        compiler_params=_params(1),
    )(sinks, qn, kn, kn, vv, vv, bias, do)


def _coords():
    return lax.axis_index("x"), lax.axis_index("y"), lax.axis_index("c")


def _sum8(name, blocks):
    def body(b_ref, o_ref):
        tot = b_ref[0]
        for d in range(1, 8):
            tot = tot + b_ref[d]
        o_ref[...] = tot

    return pl.pallas_call(body, name=name, out_shape=jax.ShapeDtypeStruct(blocks.shape[1:], F32))(blocks)


HBM_SPEC = pl.BlockSpec(memory_space=pltpu.HBM)
SEM_SPEC = pl.BlockSpec(memory_space=pltpu.SEMAPHORE)
ANY_SPEC = pl.BlockSpec(memory_space=pl.ANY)
DATAFLOW = pltpu.SideEffectType.DATAFLOW_SIDE_EFFECTING


OTHER_CHIPS = (4, 2, 6)
ALL_OTHERS = (1, 2, 3, 4, 5, 6, 7)


def _slot(x, y, c, peers):
    return 2 * x + y if peers is OTHER_CHIPS else 4 * x + 2 * y + c


def _slot_copy(land, sems, idx, x, y, c, k, peers, arriving):
    send_sems, recv_sems = sems
    px, py, pc = x ^ (k >> 2), y ^ ((k >> 1) & 1), c ^ (k & 1)
    mine = _slot(x, y, c, peers)
    dst = _slot(px, py, pc, peers) if arriving else mine
    return pltpu.make_async_remote_copy(src_ref=land.at[mine], dst_ref=land.at[dst], send_sem=send_sems.at[idx],
                                        recv_sem=recv_sems.at[idx], device_id=(px, py, pc), device_id_type=MESH)


def _gather_start(name, stacks, groups, peers, after):
    n = len(stacks)
    ng = len(groups)
    np_ = len(peers)
    after = tuple(after)

    def body(*refs):
        lands = refs[:n]
        first = n + len(after)
        sems = [(refs[first + 2 * g], refs[first + 2 * g + 1]) for g in range(ng)]
        token = refs[-1]
        x, y, c = _coords()
        for g, members in enumerate(groups):
            for i, t in enumerate(members):
                for j, k in enumerate(peers):
                    _slot_copy(lands[t], sems[g], np_ * i + j, x, y, c, k, peers, arriving=False).start()
        token[...] = jnp.zeros_like(token)

    out_shape = []
    for members in groups:
        out_shape += [pltpu.SemaphoreType.DMA((np_ * len(members),))] * 2
    out_shape += [pltpu.HBM(w.shape, w.dtype) for w in stacks]
    out_shape.append(jax.ShapeDtypeStruct((8, 128), F32))
    res = pl.pallas_call(
        body, name=name, out_shape=out_shape, in_specs=[HBM_SPEC] * n + [ANY_SPEC] * len(after),
        out_specs=[SEM_SPEC] * (2 * ng) + [HBM_SPEC] * n + [pl.BlockSpec(memory_space=pltpu.VMEM)],
        input_output_aliases={t: 2 * ng + t for t in range(n)},
        compiler_params=pltpu.CompilerParams(has_side_effects=DATAFLOW),
    )(*[pltpu.with_memory_space_constraint(w, pltpu.HBM) for w in stacks], *after)
    sems = [(res[2 * g], res[2 * g + 1]) for g in range(ng)]
    return sems, list(res[2 * ng:2 * ng + n]), res[-1]


def _gather_wait(name, stacks, sems, peers, after):
    n = len(stacks)
    after = tuple(after)

    def body(*refs):
        lands = refs[:n]
        group_sems = (refs[n], refs[n + 1])
        x, y, c = _coords()
        for i in range(n):
            for j, k in enumerate(peers):
                cp = _slot_copy(lands[i], group_sems, len(peers) * i + j, x, y, c, k, peers, arriving=True)
                cp.wait_send()
                cp.wait_recv()

    return pl.pallas_call(
        body, name=name, out_shape=[pltpu.HBM(w.shape, w.dtype) for w in stacks],
        in_specs=[HBM_SPEC] * n + [SEM_SPEC, SEM_SPEC] + [ANY_SPEC] * len(after), out_specs=[HBM_SPEC] * n,
        input_output_aliases={t: t for t in range(n)},
        compiler_params=pltpu.CompilerParams(has_side_effects=DATAFLOW),
    )(*stacks, sems[0], sems[1], *after)


N_PEERS = 7


def _peer(x, y, c, k):
    return x ^ (k >> 2), y ^ ((k >> 1) & 1), c ^ (k & 1)


def _reduce_copy(grad, land, sems, idx, x, y, c, k):
    px, py, pc = _peer(x, y, c, k)
    rh = grad.shape[1] // 2
    return pltpu.make_async_remote_copy(src_ref=grad.at[2 * px + py, pl.ds(pc * rh, rh), :], dst_ref=land.at[k - 1],
                                        send_sem=sems[0].at[idx], recv_sem=sems[1].at[idx], device_id=(px, py, pc),
                                        device_id_type=MESH)


def _reduce_start(name, grads):
    n = len(grads)

    def body(*refs):
        src, lands, sems, token = refs[:n], refs[n:2 * n], (refs[2 * n], refs[2 * n + 1]), refs[-1]
        x, y, c = _coords()
        for t in range(n):
            for k in range(1, N_PEERS + 1):
                _reduce_copy(src[t], lands[t], sems, N_PEERS * t + k - 1, x, y, c, k).start()
        token[...] = jnp.zeros_like(token)

    lands = [lax.empty((N_PEERS, g.shape[1] // 2, g.shape[2]), g.dtype) for g in grads]
    out_shape = [pltpu.SemaphoreType.DMA((N_PEERS * n,))] * 2
    out_shape += [pltpu.HBM(a.shape, a.dtype) for a in list(grads) + lands]
    out_shape.append(jax.ShapeDtypeStruct((8, 128), F32))
    res = pl.pallas_call(
        body, name=name, out_shape=out_shape, in_specs=[HBM_SPEC] * (2 * n),
        out_specs=[SEM_SPEC] * 2 + [HBM_SPEC] * (2 * n) + [pl.BlockSpec(memory_space=pltpu.VMEM)],
        input_output_aliases={t: 2 + t for t in range(2 * n)},
        compiler_params=pltpu.CompilerParams(has_side_effects=DATAFLOW),
    )(*[pltpu.with_memory_space_constraint(a, pltpu.HBM) for a in list(grads) + lands])
    return (res[0], res[1]), list(res[2:2 + n]), list(res[2 + n:2 + 2 * n]), res[-1]


def _reduce_wait(name, grads, lands, sems, after):
    n = len(grads)
    after = tuple(after)

    def body(*refs):
        src, dst, group_sems = refs[:n], refs[n:2 * n], (refs[2 * n], refs[2 * n + 1])
        x, y, c = _coords()
        for t in range(n):
            for k in range(1, N_PEERS + 1):
                cp = _reduce_copy(src[t], dst[t], group_sems, N_PEERS * t + k - 1, x, y, c, k)
                cp.wait_send()
                cp.wait_recv()

    res = pl.pallas_call(
        body, name=name, out_shape=[pltpu.HBM(a.shape, a.dtype) for a in list(grads) + list(lands)],
        in_specs=[HBM_SPEC] * (2 * n) + [SEM_SPEC, SEM_SPEC] + [ANY_SPEC] * len(after), out_specs=[HBM_SPEC] * (2 * n),
        input_output_aliases={t: t for t in range(2 * n)},
        compiler_params=pltpu.CompilerParams(has_side_effects=DATAFLOW),
    )(*grads, *lands, sems[0], sems[1], *after)
    return list(res[:n]), list(res[n:])


def _join_halves(name, halves, deps=()):
    n = len(halves)

    def body(*refs):
        src, dst = refs[:n], refs[n + len(deps):2 * n + len(deps)]
        send_sems, recv_sems = refs[-2:]
        x, y, c = _coords()
        cps = []
        for t in range(n):
            cp = pltpu.make_async_remote_copy(src_ref=src[t], dst_ref=dst[t], send_sem=send_sems.at[t],
                                              recv_sem=recv_sems.at[t], device_id=(x, y, 1 - c), device_id_type=MESH)
            cp.start()
            cps.append(cp)
        for cp in cps:
            cp.wait()

    anyspec = pl.BlockSpec(memory_space=pl.ANY)
    return pl.pallas_call(
        body, name=name, out_shape=[jax.ShapeDtypeStruct(h.shape, h.dtype) for h in halves],
        in_specs=[anyspec] * (n + len(deps)), out_specs=[anyspec] * n,
        scratch_shapes=[pltpu.SemaphoreType.DMA((n,)), pltpu.SemaphoreType.DMA((n,))],
    )(*halves, *deps)


def _row_block(rows):
    for rb in (512, 256, 128, 64, 32, 16):
        if rows % rb == 0:
            return rb
    raise ValueError(rows)


def _sum_devices(name, grad, land, place):
    S, R, C = grad.shape
    rh = R // 2
    rb = _row_block(rh)
    nbh = rh // rb

    def body(place_ref, g_ref, l_ref, o_ref):
        tot = g_ref[...].astype(F32)
        for k in range(N_PEERS):
            tot = tot + l_ref[k].astype(F32)
        o_ref[...] = tot

    return pl.pallas_call(
        body, name=name,
        grid_spec=pltpu.PrefetchScalarGridSpec(
            num_scalar_prefetch=1, grid=(nbh,),
            in_specs=[pl.BlockSpec((None, rb, C), lambda r, place: (place[0], place[1] * nbh + r, 0)),
                      pl.BlockSpec((N_PEERS, rb, C), lambda r, place: (0, r, 0))],
            out_specs=pl.BlockSpec((rb, C), lambda r, place: (r, 0))),
        out_shape=jax.ShapeDtypeStruct((rh, C), F32), compiler_params=_params(1),
    )(place, grad, land)


def _adamw_math(w, g, m, v):
    m2 = ADAM_B1 * m + (1.0 - ADAM_B1) * g
    v2 = ADAM_B2 * v + (1.0 - ADAM_B2) * (g * g)
    m_hat = m2 / (1.0 - ADAM_B1 ** ADAM_STEP)
    v_hat = v2 / (1.0 - ADAM_B2 ** ADAM_STEP)
    delta = -ADAM_LR * (m_hat / (jnp.sqrt(v_hat) + ADAM_EPS) + ADAM_WD * w)
    return delta, m2, v2


def _adamw(name, w, m, v, gs):
    L, R, C = w.shape
    Rh = R // 2
    rb = _row_block(Rh)
    nbh = Rh // rb
    assert len(gs) == L

    def body(core_ref, w_ref, m_ref, v_ref, *rest):
        g_refs, (go_ref, d_ref, m2_ref, v2_ref) = rest[:2 * L], rest[2 * L:]
        layer, half = pl.program_id(0), pl.program_id(1)
        mine = half == core_ref[0]
        g = jnp.where(mine, g_refs[0][...], g_refs[1][...])
        for t in range(1, L):
            g = jnp.where(layer == t, jnp.where(mine, g_refs[2 * t][...], g_refs[2 * t + 1][...]), g)
        delta, m2, v2 = _adamw_math(w_ref[...], g, m_ref[...], v_ref[...])
        go_ref[...] = g
        d_ref[...] = delta
        m2_ref[...] = m2
        v2_ref[...] = v2

    wspec = pl.BlockSpec((None, rb, C), lambda l, h, r, core: (l, h * nbh + r, 0))
    gspec = pl.BlockSpec((rb, C), lambda l, h, r, core: (r, 0))
    return pl.pallas_call(
        body, name=name,
        grid_spec=pltpu.PrefetchScalarGridSpec(num_scalar_prefetch=1, grid=(L, 2, nbh),
                                               in_specs=[wspec] * 3 + [gspec] * (2 * L), out_specs=[wspec] * 4),
        out_shape=[jax.ShapeDtypeStruct((L, R, C), F32)] * 4, compiler_params=_params(3),
    )(lax.axis_index("c").astype(jnp.int32).reshape(1), w, m, v, *[g for pair in gs for g in pair])


def _adamw_small(ws, gs, ms, vs):
    n = len(ws)

    def body(*refs):
        w_refs, g_refs, m_refs, v_refs = (refs[k * n:(k + 1) * n] for k in range(4))
        d_refs, m2_refs, v2_refs = (refs[(4 + k) * n:(5 + k) * n] for k in range(3))
        for t in range(n):
            delta, m2, v2 = _adamw_math(w_refs[t][...], g_refs[t][...], m_refs[t][...], v_refs[t][...])
            d_refs[t][...] = delta
            m2_refs[t][...] = m2
            v2_refs[t][...] = v2

    res = pl.pallas_call(body, name="adamw_small", out_shape=[jax.ShapeDtypeStruct(w.shape, F32) for w in ws] * 3)(
        *ws, *gs, *ms, *vs)
    return res[:n], res[n:2 * n], res[2 * n:]


def _packed_rows(shape):
    c = shape[-1]
    return (int(np.prod(shape)) // c) * -(-c // LANES)


def _pack(arrays):
    total = sum(_packed_rows(a.shape) for a in arrays)
    total += -total % 8
    buf, r0 = None, 0
    for a in arrays:
        a = a.astype(F32).reshape(-1, a.shape[-1])
        r, c = a.shape
        k = -(-c // LANES)
        a = jnp.pad(a, ((0, 0), (0, k * LANES - c))).reshape(r * k, LANES)
        a = jnp.pad(a, ((r0, total - r0 - r * k), (0, 0)))
        buf = a if buf is None else buf + a
        r0 += r * k
    return buf


def _unpack(buf, shapes):
    out, r0 = [], 0
    for shp in shapes:
        c = shp[-1]
        rows = _packed_rows(shp)
        out.append(buf[r0:r0 + rows].reshape(-1, -(-c // LANES) * LANES)[:, :c].reshape(shp))
        r0 += rows
    return out


def _rms(x, g):
    return x * lax.rsqrt(jnp.mean(x * x, axis=-1, keepdims=True) + NORM_EPS) * g


def _residual_norm_ep(acc, *rest):
    *bias, res, gain = rest
    x = acc + res + (bias[0] if bias else 0.0)
    return x, _rms(x, gain)


RESIDUAL_NORM_OUTS = (("tile", F32), ("tile", BF))


def _mlp_up(tag, h, w_up_sm):
    (up,) = _mm(f"mlp{tag}_up", h, w_up_sm, nt=False, b_sm=True, tm=2048, tn=1024, rows=256,
                ep_fn=lambda acc: (acc,), outs=(("tile", BF),))
    return up


RMS_BWD_OUTS = (("tile", F32), ("tile", BF), ("colsum", F32), ("colsum", F32))


def _mlp_bwd(tag, dy, dy_bf, x, g, up, w_up_sm, w_down):
    (dup,) = _mm(f"mlp{tag}_dup", dy_bf, w_down, nt=True, tm=2048, tn=1024, rows=256, ep_in=((up, "tile"),),
                 ep_fn=lambda acc, u: (acc * (2.0 * jnp.maximum(u.astype(F32), 0.0)),), outs=(("tile", BF),))
    dx, dx_bf, dg, dx_sum = _mm(f"mlp{tag}_dx", dup, w_up_sm, nt=True, b_sm=True, tm=512, tn=1024, rows=256,
                                ep_in=((x, "tile"), (g, "row"), (dy, "tile")), ep_fn=_rms_bwd_ep, outs=RMS_BWD_OUTS)
    return dx, dx_bf, dg, dx_sum, dup


class _Reduction:
    def __init__(self, tag, grads, place):
        self.tag, self.place = tag, place
        self.sems, self.grads, self.lands, self.token = _reduce_start(f"reduce_start_{tag}", grads)

    def finish(self, after):
        grads, lands = _reduce_wait(f"reduce_wait_{self.tag}", self.grads, self.lands, self.sems, after)
        return [_sum_devices(f"reduce_sum_{self.tag}{i}", g, l, self.place) for i, (g, l) in enumerate(zip(grads, lands))]


def kernel(x, conv_norm_g, conv_w_in, conv_b_in, conv_dw, conv_dw_b, conv_ln_g, conv_ln_b, conv_w_out, conv_b_out, attn_norm_g, w_qkv, b_qkv, q_norm_g, k_norm_g, sinks, w_o, b_o, rel_bias, mlp_norm_g, w_up, w_down, loss_target, m_conv_norm_g, m_conv_w_in, m_conv_b_in, m_conv_dw, m_conv_dw_b, m_conv_ln_g, m_conv_ln_b, m_conv_w_out, m_conv_b_out, m_attn_norm_g, m_w_qkv, m_b_qkv, m_q_norm_g, m_k_norm_g, m_sinks, m_w_o, m_b_o, m_rel_bias, m_mlp_norm_g, m_w_up, m_w_down, v_conv_norm_g, v_conv_w_in, v_conv_b_in, v_conv_dw, v_conv_dw_b, v_conv_ln_g, v_conv_ln_b, v_conv_w_out, v_conv_b_out, v_attn_norm_g, v_w_qkv, v_b_qkv, v_q_norm_g, v_k_norm_g, v_sinks, v_w_o, v_b_o, v_rel_bias, v_mlp_norm_g, v_w_up, v_w_down):
    Dm = D_MODEL
    x2d = x[0]
    tgt = loss_target[0]
    T = x2d.shape[0]
    shard = 2 * lax.axis_index("x") + lax.axis_index("y")

    me = 2 * shard + lax.axis_index("c")

    def own_slot(block, slots, index):
        return lax.dynamic_update_slice(lax.empty((slots,) + block.shape, block.dtype), block[None],
                                        (index,) + (0,) * block.ndim)

    (conv_in_sems,), (stack_in,), first_token = _gather_start(
        "gather_start_conv_in", [own_slot(conv_w_in[0].astype(BF), N_SHARD, shard)], ((0,),), OTHER_CHIPS, after=())
    sharded_small = [conv_dw[0], attn_norm_g, b_qkv, b_o]
    (small_sems,), (small_land,), small_token = _gather_start(
        "small_weights_start", [own_slot(_pack(sharded_small), 8, me)], ((0,),), ALL_OTHERS, after=(first_token,))

    big = [conv_w_out[0], w_qkv[0], w_o[0], w_up[0], w_up[1], w_down[0], w_down[1]]
    stacks = [own_slot(w.astype(BF), N_SHARD, shard) for w in big]
    groups = ((0,), (3, 5), (1, 2), (4, 6))
    gather_sems, stacks, gather_token = _gather_start("gather_start", stacks, groups, OTHER_CHIPS, after=(small_token,))

    def gathered_group(g, name, after):
        return _gather_wait(name, [stacks[t] for t in groups[g]], gather_sems[g], OTHER_CHIPS, after)

    bucket = _bucket_table()
    bias = _bias_table(rel_bias, bucket)

    h0 = _rms_fwd("conv_norm", x2d, conv_norm_g, deps=(gather_token,))
    (w_in_sm,) = _gather_wait("gather_wait_conv_in", [stack_in], conv_in_sems, OTHER_CHIPS, (h0, bias))
    (u,) = _mm("conv_in", h0, w_in_sm, nt=False, b_sm=True, tm=2048, tn=512, rows=256, ep_in=((conv_b_in, "row"),),
               ep_fn=lambda acc, b: (acc + b,), outs=(("tile", BF),))
    (gathered,) = _gather_wait("small_weights_wait", [small_land], small_sems, ALL_OTHERS, (u,))
    chips = [_unpack(gathered[2 * s], [a.shape for a in sharded_small]) for s in range(N_SHARD)]
    dw_f, attn_norm_f, b_qkv_f, b_o_f = (jnp.concatenate([chips[s][t] for s in range(N_SHARD)], axis=-1)
                                         for t in range(len(sharded_small)))
    dw_pad = jnp.pad(dw_f, ((0, HALO - CONV_W), (0, 0)))
    cv, s_act = _conv_fwd(u, dw_pad, conv_dw_b, conv_ln_g, conv_ln_b)
    (g_out,) = gathered_group(0, "gather_wait_conv_out", (s_act,))
    w_out_f = g_out.reshape(Dm, Dm)
    x1, h1 = _mm("conv_out", s_act, w_out_f, nt=False, tm=1024, tn=1024, rows=256,
                 ep_in=((conv_b_out, "row"), (x2d, "tile"), (mlp_norm_g[0:1], "row")), ep_fn=_residual_norm_ep,
                 outs=RESIDUAL_NORM_OUTS)

    g_up0, g_down0 = gathered_group(1, "gather_wait_mlp0", (x1,))
    w_up_sm = [g_up0, None]
    w_down_f = [g_down0.reshape(D_FF, Dm), None]
    up0 = _mlp_up(0, h1, w_up_sm[0])
    x2, h2 = _mm("mlp0_down", up0, w_down_f[0], nt=False, tm=512, tn=1024, rows=256, a_fn=_relu2,
                 ep_in=((x1, "tile"), (attn_norm_f, "row")), ep_fn=_residual_norm_ep, outs=RESIDUAL_NORM_OUTS)

    g_qkv, g_o = gathered_group(2, "gather_wait_attn", (x2,))
    w_qkv_f = jnp.transpose(g_qkv, (1, 0, 2)).reshape(Dm, QKV_DIM)
    w_o_f = g_o.reshape(ATTN_DIM, Dm)
    qg_t = jnp.tile(q_norm_g, (1, N_HEADS))
    kg_t = jnp.tile(k_norm_g, (1, N_KV))

    def qkv_ep(acc, b, qg, kg, ones):
        proj = acc + b
        q, k, v = proj[:, :ATTN_DIM], proj[:, ATTN_DIM:ATTN_DIM + KV_DIM], proj[:, ATTN_DIM + KV_DIM:]
        return proj, _qk_normed(q, qg, ones, 1.0 / math.sqrt(HEAD_DIM)), _qk_normed(k, kg, ones, 1.0), v

    qkv, qn, kn, vv = _mm(
        "attn_qkv", h2, w_qkv_f, nt=False, tm=1024, tn=QKV_DIM, rows=256, ep_fn=qkv_ep,
        ep_in=((b_qkv_f, "row"), (qg_t, "whole"), (kg_t, "whole"), (_head_ones(), "whole")),
        outs=(("tile", F32), ("tile", BF, ATTN_DIM), ("tile", BF, KV_DIM), ("tile", BF, KV_DIM)))
    sinks1 = sinks[0]
    att = _attn_fwd(qn, kn, vv, bias, sinks1)
    x3, h3 = _mm("attn_out", att, w_o_f, nt=False, tm=1024, tn=1024, rows=256,
                 ep_in=((b_o_f, "row"), (x2, "tile"), (mlp_norm_g[1:2], "row")), ep_fn=_residual_norm_ep,
                 outs=RESIDUAL_NORM_OUTS)

    g_up1, g_down1 = gathered_group(3, "gather_wait_mlp1", (x3,))
    w_up_sm[1] = g_up1
    w_down_f[1] = g_down1.reshape(D_FF, Dm)
    up1 = _mlp_up(1, h3, w_up_sm[1])

    def loss_ep(acc, r, t):
        diff = acc + r - t
        dy = diff * (1.0 / Dm)
        return dy, dy, jnp.sum(diff * diff, axis=0, keepdims=True)

    dy, dy_bf, sq = _mm("mlp1_down_loss", up1, w_down_f[1], nt=False, tm=512, tn=1024, rows=256, a_fn=_relu2,
                        ep_in=((x3, "tile"), (tgt, "tile")), ep_fn=loss_ep,
                        outs=(("tile", F32), ("tile", BF), ("colsum", F32)))

    place = jnp.stack([shard, lax.axis_index("c")]).astype(jnp.int32)
    dx3, dx3_bf, dg_mlp1, db_o, dup1 = _mlp_bwd(1, dy, dy_bf, x3, mlp_norm_g[1:2], up1, w_up_sm[1], w_down_f[1])
    dw_down1 = _mm_tn("mlp1_dw_down", up1, dy_bf, tm=1024, tn=1024, tk=2048, a_fn=_relu2)
    dw_up1 = _mm_tn("mlp1_dw_up", h3, dup1, tm=1024, tn=1024, tk=2048, out_sm=N_SHARD)
    red_mlp1 = _Reduction("mlp1", [dw_up1, dw_down1.reshape(N_SHARD, D_FF // N_SHARD, Dm)], place)

    ident = lambda acc: (acc,)
    (datt,) = _mm("attn_dout", dx3_bf, w_o_f, nt=True, tm=1024, tn=1024, rows=256, ep_fn=ident, outs=(("tile", BF),),
                  deps=(red_mlp1.token,))
    dw_o = _mm_tn("attn_dw_o", att, dx3_bf, tm=1024, tn=1024, tk=2048)
    dqn, dkn, dvv, dbias, dsinks = _attn_bwd(qn, kn, vv, bias, sinks1, datt)
    drel = _bias_grad(dbias, bucket)
    dqkv, db_qkv, dqg_t, dkg_t = _qk_norm_bwd(qkv, dqn, dkn, dvv, qg_t, kg_t)
    dw_qkv = _mm_tn("attn_dw_qkv", h2, dqkv, tm=1024, tn=QKV_DIM, tk=2048)
    red_attn = _Reduction("attn", [jnp.transpose(dw_qkv.reshape(Dm, N_SHARD, QKV_DIM // N_SHARD), (1, 0, 2)),
                                   dw_o.reshape(N_SHARD, ATTN_DIM // N_SHARD, Dm)], place)
    dx2, dx2_bf, dg_attn, _ = _mm("attn_dx", dqkv, w_qkv_f, nt=True, tm=1024, tn=1024, rows=256,
                                  ep_in=((x2, "tile"), (attn_norm_f, "row"), (dx3, "tile")), ep_fn=_rms_bwd_ep,
                                  outs=RMS_BWD_OUTS, deps=(red_attn.token,))

    dx1, dx1_bf, dg_mlp0, db_out, dup0 = _mlp_bwd(0, dx2, dx2_bf, x1, mlp_norm_g[0:1], up0, w_up_sm[0], w_down_f[0])
    dw_down0 = _mm_tn("mlp0_dw_down", up0, dx2_bf, tm=1024, tn=1024, tk=2048, a_fn=_relu2)
    dw_up0 = _mm_tn("mlp0_dw_up", h1, dup0, tm=1024, tn=1024, tk=2048, out_sm=N_SHARD)
    dw_out = _mm_tn("conv_dw_out", s_act, dx1_bf, tm=1024, tn=1024, tk=2048)
    red_mlp0 = _Reduction("mlp0", [dw_up0, dw_down0.reshape(N_SHARD, D_FF // N_SHARD, Dm),
                                   dw_out.reshape(N_SHARD, Dm // N_SHARD, Dm)], place)
    (r_qkv, r_o) = red_attn.finish((dx1,))
    (r_up1, r_down1) = red_mlp1.finish((dx1,))

    dcv, dln_g, dln_b, ddw_b = _mm("conv_ds", dx1_bf, w_out_f, nt=True, tm=1024, tn=1024, rows=256,
                                   ep_in=((cv, "tile"), (conv_ln_g, "row"), (conv_ln_b, "row")),
                                   ep_fn=_ln_silu_bwd_ep,
                                   outs=(("tile", F32), ("colsum", F32), ("colsum", F32), ("colsum", F32)),
                                   deps=(red_mlp0.token,))
    du, db_in, ddw8 = _conv_bwd(u, dcv, dw_pad)
    (r_up0, r_down0, r_out) = red_mlp0.finish((du,))
    dw_in = _mm_tn("conv_dw_in", h0, du, tm=1024, tn=512, tk=4096, out_sm=N_SHARD)
    red_conv = _Reduction("conv", [dw_in], place)
    def first_layer_ep(*args):
        tot, _, dg, _ = _rms_bwd_ep(*args)
        return tot, dg

    gx, dg_conv = _mm("conv_dx", du, w_in_sm, nt=True, b_sm=True, tm=1024, tn=1024, rows=256,
                      ep_in=((x2d, "tile"), (conv_norm_g, "row"), (dx1, "tile")), ep_fn=first_layer_ep,
                      outs=(("tile", F32), ("colsum", F32)), deps=(red_conv.token,))
    (r_in,) = red_conv.finish((gx,))

    dqg = dqg_t.reshape(N_HEADS, HEAD_DIM).sum(axis=0, keepdims=True)
    dkg = dkg_t.reshape(N_KV, HEAD_DIM).sum(axis=0, keepdims=True)
    small_full = [dg_conv, db_in, ddw8.sum(axis=1)[:CONV_W], ddw_b, dln_g, dln_b, db_out, dg_attn, db_qkv, dqg, dkg,
                  dsinks[None, :], db_o, drel.reshape(1, REL_BUCKETS * N_HEADS),
                  jnp.pad(dg_mlp0, ((0, 1), (0, 0))) + jnp.pad(dg_mlp1, ((1, 0), (0, 0))), sq]
    (sg_sems,), (sg_land,), sg_token = _gather_start(
        "small_grads_start", [own_slot(_pack(small_full), 8, me)], ((0,),), ALL_OTHERS, after=())

    mine = [r_in, r_out, r_qkv, r_o, r_up0, r_up1, r_down0, r_down1]
    r_in, r_out, r_qkv, r_o, r_up0, r_up1, r_down0, r_down1 = zip(
        mine, _join_halves("join_halves", mine, deps=(sg_token,)))

    big_out = {}
    for nm, w, m, v, gs in (("conv_w_in", conv_w_in, m_conv_w_in, v_conv_w_in, (r_in,)),
                            ("conv_w_out", conv_w_out, m_conv_w_out, v_conv_w_out, (r_out,)),
                            ("w_qkv", w_qkv, m_w_qkv, v_w_qkv, (r_qkv,)),
                            ("w_o", w_o, m_w_o, v_w_o, (r_o,)),
                            ("w_up", w_up, m_w_up, v_w_up, (r_up0, r_up1)),
                            ("w_down", w_down, m_w_down, v_w_down, (r_down0, r_down1))):
        big_out[nm] = _adamw(f"adamw_{nm}", w, m, v, gs)

    (sg_land,) = _gather_wait("small_grads_wait", [sg_land], sg_sems, ALL_OTHERS,
                              [big_out[nm][0] for nm in big_out])
    small_sum = _sum8("small_grads_sum", sg_land)
    (r_norm, r_b_in, r_dw, r_dw_b, r_ln_g, r_ln_b, r_b_out, r_attn_norm, r_b_qkv, r_qg, r_kg, r_sinks, r_b_o, r_rel,
     r_mlp_norm, r_sq) = _unpack(small_sum, [a.shape for a in small_full])
    loss = 0.5 * jnp.sum(r_sq) * (1.0 / Dm)

    def cols(a, width):
        return lax.dynamic_slice_in_dim(a, shard * width, width, axis=a.ndim - 1)

    small_names = ["conv_norm_g", "conv_b_in", "conv_dw", "conv_dw_b", "conv_ln_g", "conv_ln_b", "conv_b_out",
                   "attn_norm_g", "b_qkv", "q_norm_g", "k_norm_g", "sinks", "b_o", "rel_bias", "mlp_norm_g"]
    small_g = [r_norm, r_b_in, cols(r_dw, Dm // N_SHARD)[None], r_dw_b, r_ln_g, r_ln_b, r_b_out,
               cols(r_attn_norm, Dm // N_SHARD), cols(r_b_qkv, QKV_DIM // N_SHARD), r_qg, r_kg, r_sinks,
               cols(r_b_o, Dm // N_SHARD), r_rel.reshape(REL_BUCKETS, N_HEADS), r_mlp_norm]
    small_w = [conv_norm_g, conv_b_in, conv_dw, conv_dw_b, conv_ln_g, conv_ln_b, conv_b_out, attn_norm_g, b_qkv,
               q_norm_g, k_norm_g, sinks, b_o, rel_bias, mlp_norm_g]
    small_m = [m_conv_norm_g, m_conv_b_in, m_conv_dw, m_conv_dw_b, m_conv_ln_g, m_conv_ln_b, m_conv_b_out,
               m_attn_norm_g, m_b_qkv, m_q_norm_g, m_k_norm_g, m_sinks, m_b_o, m_rel_bias, m_mlp_norm_g]
    small_v = [v_conv_norm_g, v_conv_b_in, v_conv_dw, v_conv_dw_b, v_conv_ln_g, v_conv_ln_b, v_conv_b_out,
               v_attn_norm_g, v_b_qkv, v_q_norm_g, v_k_norm_g, v_sinks, v_b_o, v_rel_bias, v_mlp_norm_g]
    flat2 = lambda a: a.reshape(-1, a.shape[-1])
    small_g = [flat2(g) for g in small_g]
    d_s, m_s, v_s = _adamw_small([flat2(w) for w in small_w], small_g, [flat2(m) for m in small_m],
                                 [flat2(v) for v in small_v])
    small_out = {}
    for nm, w, g, d, m2, v2 in zip(small_names, small_w, small_g, d_s, m_s, v_s):
        small_out[nm] = tuple(a.reshape(w.shape) for a in (g, d, m2, v2))

    order = ["conv_norm_g", "conv_w_in", "conv_b_in", "conv_dw", "conv_dw_b", "conv_ln_g", "conv_ln_b", "conv_w_out",
             "conv_b_out", "attn_norm_g", "w_qkv", "b_qkv", "q_norm_g", "k_norm_g", "sinks", "w_o", "b_o", "rel_bias",
             "mlp_norm_g", "w_up", "w_down"]
    res = {**small_out, **big_out}
    outs = [loss, gx[None]]
    for part in range(4):
        outs += [res[nm][part] for nm in order]
    return tuple(outs)
```

```python
import math

import numpy as np
import jax
import jax.numpy as jnp
from jax import lax
from jax.experimental import pallas as pl
from jax.experimental.pallas import tpu as pltpu

F32 = jnp.float32
BF = jnp.bfloat16
MESH = pl.DeviceIdType.MESH

D_MODEL = 1024
D_FF = 4096
N_HEADS = 16
N_KV = 2
GROUP = N_HEADS // N_KV
HEAD_DIM = 64
ATTN_DIM = N_HEADS * HEAD_DIM
KV_DIM = N_KV * HEAD_DIM
QKV_DIM = ATTN_DIM + 2 * KV_DIM
BLOCK = 128
CONV_W = 31
HALO = 32
REL_BUCKETS = 32
REL_MAX_DIST = 128
NORM_EPS = 1e-6
NEG_INF = -1e30
N_SHARD = 4
LANES = 1024

ADAM_LR = 0.001
ADAM_B1 = 0.9
ADAM_B2 = 0.999
ADAM_EPS = 1e-08
ADAM_WD = 0.01
ADAM_STEP = 10

VMEM_LIMIT = 56 * 1024 * 1024


def _params(n_axes):
    return pltpu.CompilerParams(dimension_semantics=("arbitrary",) * n_axes, vmem_limit_bytes=VMEM_LIMIT)


def _dot(a, b, ca, cb):
    return lax.dot_general(a, b, (((ca,), (cb,)), ((), ())), preferred_element_type=F32)


def _mm(name, a, b, *, nt, tm, tn, ep_fn, outs, a_fn=None, b_sm=False, ep_in=(), deps=(), rows=None):
    M, K = a.shape
    rows = tm if rows is None else rows
    if b_sm:
        S, ks = b.shape[0], b.shape[2]
        N, per = (b.shape[1], None) if nt else (S * b.shape[2], b.shape[2] // tn)
        assert (S * ks == K) if nt else (b.shape[1] == K)
    else:
        N = b.shape[0] if nt else b.shape[1]
        assert (b.shape[1] if nt else b.shape[0]) == K
    assert M % tm == 0 and N % tn == 0 and tm % rows == 0
    ne, no, nd = len(ep_in), len(outs), len(deps)

    def body(a_ref, b_ref, *rest):
        ep_refs, out_refs = rest[:ne], rest[ne + nd:ne + nd + no]
        i = pl.program_id(1)
        sums = [None] * no
        for r in range(tm // rows):
            rs = pl.ds(r * rows, rows)

            def lhs(cols):
                av = a_ref[rs, cols]
                return (av if a_fn is None else a_fn(av)).astype(BF)

            if b_sm and nt:
                acc = None
                for s in range(S):
                    part = _dot(lhs(pl.ds(s * ks, ks)), b_ref[s].astype(BF), 1, 1)
                    acc = part if acc is None else acc + part
            else:
                acc = _dot(lhs(slice(None)), b_ref[...].astype(BF), 1, 1 if nt else 0)
            ep_vals = [ref[rs, :] if kind == "tile" else ref[...] for ref, (_, kind) in zip(ep_refs, ep_in)]
            vals = ep_fn(acc, *ep_vals)
            for o, ((kind, dt, *_), ref, val) in enumerate(zip(outs, out_refs, vals)):
                if kind == "tile":
                    ref[rs, :] = val.astype(dt)
                else:
                    sums[o] = val if sums[o] is None else sums[o] + val
        for (kind, *_), ref, val in zip(outs, out_refs, sums):
            if kind == "colsum":
                @pl.when(i == 0)
                def _():
                    ref[...] = val

                @pl.when(i > 0)
                def _():
                    ref[...] += val

    if b_sm and nt:
        b_spec = pl.BlockSpec((S, tn, ks), lambda j, i: (0, j, 0))
    elif b_sm:
        b_spec = pl.BlockSpec((None, K, tn), lambda j, i: (j // per, 0, j % per))
    elif nt:
        b_spec = pl.BlockSpec((tn, K), lambda j, i: (j, 0))
    else:
        b_spec = pl.BlockSpec((K, tn), lambda j, i: (0, j))
    in_specs = [pl.BlockSpec((tm, K), lambda j, i: (i, 0)), b_spec]
    for arr, kind in ep_in:
        if kind == "tile":
            assert arr.shape == (M, N)
            in_specs.append(pl.BlockSpec((tm, tn), lambda j, i: (i, j)))
        elif kind == "whole":
            in_specs.append(pl.BlockSpec(arr.shape, lambda j, i, rank=arr.ndim: (0,) * rank))
        else:
            assert arr.shape == (1, N)
            in_specs.append(pl.BlockSpec((1, tn), lambda j, i: (0, j)))
    in_specs += [pl.BlockSpec(memory_space=pl.ANY)] * nd
    out_shape, out_specs = [], []
    for kind, dt, *width in outs:
        if kind == "tile" and width:
            assert tn == N
            out_shape.append(jax.ShapeDtypeStruct((M, width[0]), dt))
            out_specs.append(pl.BlockSpec((tm, width[0]), lambda j, i: (i, 0)))
        elif kind == "tile":
            out_shape.append(jax.ShapeDtypeStruct((M, N), dt))
            out_specs.append(pl.BlockSpec((tm, tn), lambda j, i: (i, j)))
        else:
            out_shape.append(jax.ShapeDtypeStruct((1, N), F32))
            out_specs.append(pl.BlockSpec((1, tn), lambda j, i: (0, j)))
    return pl.pallas_call(
        body, name=name, grid=(N // tn, M // tm), in_specs=in_specs, out_specs=out_specs, out_shape=out_shape,
        compiler_params=_params(2),
    )(a, b, *[arr for arr, _ in ep_in], *deps)


def _mm_tn(name, a, b, *, tm, tn, tk, a_fn=None, out_sm=None):
    T, Ka = a.shape
    N = b.shape[1]
    assert b.shape[0] == T and T % tk == 0 and Ka % tm == 0 and N % tn == 0
    nk = T // tk

    def body(a_ref, b_ref, o_ref, acc_ref):
        k = pl.program_id(2)

        @pl.when(k == 0)
        def _():
            acc_ref[...] = jnp.zeros_like(acc_ref)

        av = a_ref[...]
        if a_fn is not None:
            av = a_fn(av)
        acc_ref[...] += _dot(av.astype(BF), b_ref[...].astype(BF), 0, 0)

        @pl.when(k == nk - 1)
        def _():
            o_ref[...] = acc_ref[...].astype(BF)

    if out_sm is None:
        out_shape = jax.ShapeDtypeStruct((Ka, N), BF)
        out_spec = pl.BlockSpec((tm, tn), lambda i, j, k: (i, j))
    else:
        per = (N // out_sm) // tn
        assert per * tn * out_sm == N
        out_shape = jax.ShapeDtypeStruct((out_sm, Ka, N // out_sm), BF)
        out_spec = pl.BlockSpec((None, tm, tn), lambda i, j, k: (j // per, i, j % per))
    return pl.pallas_call(
        body, name=name, grid=(Ka // tm, N // tn, nk),
        in_specs=[pl.BlockSpec((tk, tm), lambda i, j, k: (k, i)), pl.BlockSpec((tk, tn), lambda i, j, k: (k, j))],
        out_specs=out_spec, out_shape=out_shape, scratch_shapes=[pltpu.VMEM((tm, tn), F32)],
        compiler_params=_params(3),
    )(a, b)


def _relu2(v):
    r = jnp.maximum(v.astype(F32), 0.0)
    return r * r


def _rms_bwd_ep(dh, x, g, dres):
    rstd = lax.rsqrt(jnp.mean(x * x, axis=-1, keepdims=True) + NORM_EPS)
    xh = x * rstd
    dxh = dh * g
    dx = rstd * (dxh - xh * jnp.mean(dxh * xh, axis=-1, keepdims=True))
    tot = dres + dx
    return tot, tot, jnp.sum(dh * xh, axis=0, keepdims=True), jnp.sum(tot, axis=0, keepdims=True)


def _rms_fwd(name, x, g, tm=512, deps=()):
    T, Dm = x.shape

    def body(x_ref, g_ref, *rest):
        o_ref = rest[-1]
        xv = x_ref[...]
        rstd = lax.rsqrt(jnp.mean(xv * xv, axis=-1, keepdims=True) + NORM_EPS)
        o_ref[...] = (xv * rstd * g_ref[...]).astype(BF)

    return pl.pallas_call(
        body, name=name, grid=(T // tm,),
        in_specs=[pl.BlockSpec((tm, Dm), lambda i: (i, 0)), pl.BlockSpec((1, Dm), lambda i: (0, 0))]
        + [pl.BlockSpec(memory_space=pl.ANY)] * len(deps),
        out_specs=pl.BlockSpec((tm, Dm), lambda i: (i, 0)), out_shape=jax.ShapeDtypeStruct((T, Dm), BF),
        compiler_params=_params(1),
    )(x, g, *deps)


HEAD_GROUP = 256


def _head_sum(v, ones):
    n = v.shape[1]
    w = min(n, HEAD_GROUP)
    blk = ones[:w, :w]
    parts = [_dot(v[:, c:c + w].astype(BF), blk, 1, 0) for c in range(0, n, w)]
    return parts[0] if len(parts) == 1 else jnp.concatenate(parts, axis=1)


def _head_ones():
    idx = np.arange(HEAD_GROUP) // HEAD_DIM
    return jnp.asarray((idx[:, None] == idx[None, :]).astype(np.float32), dtype=BF)


def _qk_normed(x, g, ones, scale):
    r = lax.rsqrt(_head_sum(x * x, ones) * (1.0 / HEAD_DIM) + NORM_EPS)
    return x * r * g * scale


def _qk_norm_bwd(qkv, dqn, dkn, dv, qg_t, kg_t, tm=256):
    T = qkv.shape[0]

    def body(x_ref, dq_ref, dk_ref, dv_ref, qg_ref, kg_ref, ones_ref, o_ref, db_ref, dqg_ref, dkg_ref):
        i = pl.program_id(0)
        ones = ones_ref[...]

        def one(x, dy, g):
            r = lax.rsqrt(_head_sum(x * x, ones) * (1.0 / HEAD_DIM) + NORM_EPS)
            xh = x * r
            dxh = dy * g
            dx = r * (dxh - xh * (_head_sum(dxh * xh, ones) * (1.0 / HEAD_DIM)))
            return dx, jnp.sum(dy * xh, axis=0, keepdims=True)

        dq, dqg = one(x_ref[:, pl.ds(0, ATTN_DIM)], dq_ref[...], qg_ref[...])
        dk, dkg = one(x_ref[:, pl.ds(ATTN_DIM, KV_DIM)], dk_ref[...], kg_ref[...])
        dvv = dv_ref[...]
        o_ref[:, pl.ds(0, ATTN_DIM)] = dq.astype(BF)
        o_ref[:, pl.ds(ATTN_DIM, KV_DIM)] = dk.astype(BF)
        o_ref[:, pl.ds(ATTN_DIM + KV_DIM, KV_DIM)] = dvv.astype(BF)
        sq, sk, sv = (jnp.sum(t, axis=0, keepdims=True) for t in (dq, dk, dvv))

        @pl.when(i == 0)
        def _():
            db_ref[:, pl.ds(0, ATTN_DIM)] = sq
            db_ref[:, pl.ds(ATTN_DIM, KV_DIM)] = sk
            db_ref[:, pl.ds(ATTN_DIM + KV_DIM, KV_DIM)] = sv
            dqg_ref[...] = dqg
            dkg_ref[...] = dkg

        @pl.when(i > 0)
        def _():
            db_ref[:, pl.ds(0, ATTN_DIM)] += sq
            db_ref[:, pl.ds(ATTN_DIM, KV_DIM)] += sk
            db_ref[:, pl.ds(ATTN_DIM + KV_DIM, KV_DIM)] += sv
            dqg_ref[...] += dqg
            dkg_ref[...] += dkg

    full = lambda shape: pl.BlockSpec(shape, lambda i: (0, 0))
    row = lambda n: pl.BlockSpec((tm, n), lambda i: (i, 0))
    return pl.pallas_call(
        body, name="qk_norm_bwd", grid=(T // tm,),
        in_specs=[row(QKV_DIM), row(ATTN_DIM), row(KV_DIM), row(KV_DIM), full((1, ATTN_DIM)), full((1, KV_DIM)),
                  full((HEAD_GROUP, HEAD_GROUP))],
        out_specs=[row(QKV_DIM), full((1, QKV_DIM)), full((1, ATTN_DIM)), full((1, KV_DIM))],
        out_shape=[jax.ShapeDtypeStruct((T, QKV_DIM), BF), jax.ShapeDtypeStruct((1, QKV_DIM), F32),
                   jax.ShapeDtypeStruct((1, ATTN_DIM), F32), jax.ShapeDtypeStruct((1, KV_DIM), F32)],
        compiler_params=_params(1),
    )(qkv, dqn, dkn, dv, qg_t, kg_t, _head_ones())


ROWS = 128
COLS = 128


SUBLANES = 8
FIRST_TAP = HALO - (CONV_W - 1)


def _glu(a, g):
    return a.astype(F32) * jax.nn.sigmoid(g.astype(F32))


def _shifted(xe, s):
    return xe if s == 0 else pltpu.roll(xe, ROWS + HALO - s, axis=0)


def _conv_fwd(u, dw_pad, dw_b, ln_g, ln_b, tm=512):
    T = u.shape[0]
    Dm = D_MODEL
    hpt = tm // HALO

    def body(ac_ref, gc_ref, ap_ref, gp_ref, w_ref, wb_ref, lg_ref, lb_ref, cv_ref, s_ref, ext):
        i = pl.program_id(0)
        ext[pl.ds(0, HALO), :] = jnp.where(i > 0, _glu(ap_ref[...], gp_ref[...]), 0.0)
        ext[pl.ds(HALO, tm), :] = _glu(ac_ref[...], gc_ref[...])

        def rows(r, carry):
            r0 = pl.multiple_of(r * ROWS, ROWS)
            for c in range(Dm // COLS):
                cs = pl.ds(c * COLS, COLS)
                xe = ext[pl.ds(r0, ROWS + HALO), cs]
                acc = jnp.zeros((ROWS, COLS), F32)
                for s in range(SUBLANES):
                    xs = _shifted(xe, s)
                    for j in range(CONV_W):
                        off = FIRST_TAP + j
                        if off % SUBLANES == s:
                            acc = acc + xs[off - s:off - s + ROWS, :] * w_ref[pl.ds(j, 1), cs]
                cv_ref[pl.ds(r0, ROWS), cs] = acc + wb_ref[:, cs]
            return carry

        lax.fori_loop(0, tm // ROWS, rows, 0)
        cv = cv_ref[...]
        xc = cv - jnp.mean(cv, axis=-1, keepdims=True)
        y = xc * lax.rsqrt(jnp.mean(xc * xc, axis=-1, keepdims=True) + NORM_EPS) * lg_ref[...] + lb_ref[...]
        s_ref[...] = (y * jax.nn.sigmoid(y)).astype(BF)

    full = lambda shape: pl.BlockSpec(shape, lambda i: (0, 0))
    return pl.pallas_call(
        body, name="conv_fwd", grid=(T // tm,),
        in_specs=[pl.BlockSpec((tm, Dm), lambda i: (i, 0)), pl.BlockSpec((tm, Dm), lambda i: (i, 1)),
                  pl.BlockSpec((HALO, Dm), lambda i: (jnp.maximum(i * hpt - 1, 0), 0)),
                  pl.BlockSpec((HALO, Dm), lambda i: (jnp.maximum(i * hpt - 1, 0), 1)),
                  full((HALO, Dm)), full((1, Dm)), full((1, Dm)), full((1, Dm))],
        out_specs=[pl.BlockSpec((tm, Dm), lambda i: (i, 0)), pl.BlockSpec((tm, Dm), lambda i: (i, 0))],
        out_shape=[jax.ShapeDtypeStruct((T, Dm), F32), jax.ShapeDtypeStruct((T, Dm), BF)],
        scratch_shapes=[pltpu.VMEM((tm + HALO, Dm), F32)],
        compiler_params=_params(1),
    )(u, u, u, u, dw_pad, dw_b, ln_g, ln_b)


def _ln_silu_bwd_ep(ds, cv, lg, lb):
    xc = cv - jnp.mean(cv, axis=-1, keepdims=True)
    rstd = lax.rsqrt(jnp.mean(xc * xc, axis=-1, keepdims=True) + NORM_EPS)
    xh = xc * rstd
    y = xh * lg + lb
    sg = jax.nn.sigmoid(y)
    dy = ds * (sg * (1.0 + y * (1.0 - sg)))
    dxh = dy * lg
    dcv = rstd * (dxh - jnp.mean(dxh, axis=-1, keepdims=True) - xh * jnp.mean(dxh * xh, axis=-1, keepdims=True))
    return (dcv, jnp.sum(dy * xh, axis=0, keepdims=True), jnp.sum(dy, axis=0, keepdims=True),
            jnp.sum(dcv, axis=0, keepdims=True))


def _conv_bwd(u, dcv, dw_pad, tm=512):
    T = u.shape[0]
    Dm = D_MODEL
    hpt = tm // HALO
    last = T // HALO - 1
    nt = T // tm

    def body(ac_ref, gc_ref, ap_ref, gp_ref, dc_ref, dn_ref, w_ref, du_ref, db_ref, dw_ref, ext_g, ext_d):
        i = pl.program_id(0)
        ext_g[pl.ds(0, HALO), :] = jnp.where(i > 0, _glu(ap_ref[...], gp_ref[...]), 0.0)
        ext_g[pl.ds(HALO, tm), :] = _glu(ac_ref[...], gc_ref[...])
        ext_d[pl.ds(0, tm), :] = dc_ref[...]
        ext_d[pl.ds(tm, HALO), :] = jnp.where(i < nt - 1, dn_ref[...], 0.0)

        @pl.when(i == 0)
        def _():
            db_ref[...] = jnp.zeros_like(db_ref)
            dw_ref[...] = jnp.zeros_like(dw_ref)

        def rows(r, carry):
            r0 = pl.multiple_of(r * ROWS, ROWS)
            rs = pl.ds(r0, ROWS)
            for c in range(Dm // COLS):
                cs = pl.ds(c * COLS, COLS)
                cs2 = pl.ds(Dm + c * COLS, COLS)
                de = ext_d[pl.ds(r0, ROWS + HALO), cs]
                ge = ext_g[pl.ds(r0, ROWS + HALO), cs]
                dcur = de[0:ROWS, :]
                acc = jnp.zeros((ROWS, COLS), F32)
                for s in range(SUBLANES):
                    ds_, gs_ = _shifted(de, s), _shifted(ge, s)
                    for j in range(CONV_W):
                        off = CONV_W - 1 - j
                        if off % SUBLANES == s:
                            acc = acc + ds_[off - s:off - s + ROWS, :] * w_ref[pl.ds(j, 1), cs]
                        goff = FIRST_TAP + j
                        if goff % SUBLANES == s:
                            prod = dcur * gs_[goff - s:goff - s + ROWS, :]
                            dw_ref[j, :, cs] += jnp.sum(prod.reshape(ROWS // SUBLANES, SUBLANES, COLS), axis=0)
                a = ac_ref[rs, cs].astype(F32)
                sg = jax.nn.sigmoid(gc_ref[rs, cs].astype(F32))
                da = acc * sg
                dg = acc * a * sg * (1.0 - sg)
                du_ref[rs, cs] = da.astype(BF)
                du_ref[rs, cs2] = dg.astype(BF)
                db_ref[:, cs] += jnp.sum(da, axis=0, keepdims=True)
                db_ref[:, cs2] += jnp.sum(dg, axis=0, keepdims=True)
            return carry

        lax.fori_loop(0, tm // ROWS, rows, 0)

    return pl.pallas_call(
        body, name="conv_bwd", grid=(nt,),
        in_specs=[pl.BlockSpec((tm, Dm), lambda i: (i, 0)), pl.BlockSpec((tm, Dm), lambda i: (i, 1)),
                  pl.BlockSpec((HALO, Dm), lambda i: (jnp.maximum(i * hpt - 1, 0), 0)),
                  pl.BlockSpec((HALO, Dm), lambda i: (jnp.maximum(i * hpt - 1, 0), 1)),
                  pl.BlockSpec((tm, Dm), lambda i: (i, 0)),
                  pl.BlockSpec((HALO, Dm), lambda i: (jnp.minimum((i + 1) * hpt, last), 0)),
                  pl.BlockSpec((HALO, Dm), lambda i: (0, 0))],
        out_specs=[pl.BlockSpec((tm, 2 * Dm), lambda i: (i, 0)), pl.BlockSpec((1, 2 * Dm), lambda i: (0, 0)),
                   pl.BlockSpec((HALO, 8, Dm), lambda i: (0, 0, 0))],
        out_shape=[jax.ShapeDtypeStruct((T, 2 * Dm), BF), jax.ShapeDtypeStruct((1, 2 * Dm), F32),
                   jax.ShapeDtypeStruct((HALO, 8, Dm), F32)],
        scratch_shapes=[pltpu.VMEM((tm + HALO, Dm), F32), pltpu.VMEM((tm + HALO, Dm), F32)],
        compiler_params=_params(1),
    )(u, u, u, u, dcv, dcv, dw_pad)


def _bucket_table():
    q_loc = np.arange(BLOCK)[:, None]
    k_loc = np.arange(2 * BLOCK)[None, :]
    dist = q_loc + BLOCK - k_loc
    n = np.maximum(dist, 0)
    max_exact = REL_BUCKETS // 2
    large = max_exact + (np.log(np.maximum(n, 1).astype(np.float32) / max_exact)
                         / math.log(REL_MAX_DIST / max_exact) * (REL_BUCKETS - max_exact)).astype(np.int32)
    large = np.minimum(large, REL_BUCKETS - 1)
    bucket = np.where(n < max_exact, n, large).astype(np.int32)
    band = np.where((dist >= 0) & (dist < BLOCK), bucket, -1)
    folded = np.where(np.arange(BLOCK)[None, :] > q_loc, band[:, :BLOCK], band[:, BLOCK:])
    assert (folded >= 0).all() and ((band[:, :BLOCK] >= 0) != (band[:, BLOCK:] >= 0)).all()
    return jnp.asarray(folded.astype(np.int32))


def _prev_mask():
    row = lax.broadcasted_iota(jnp.int32, (BLOCK, BLOCK), 0)
    col = lax.broadcasted_iota(jnp.int32, (BLOCK, BLOCK), 1)
    return col > row


def _fold(band, prev_mask):
    return jnp.where(prev_mask, band[:, :BLOCK], band[:, BLOCK:])


def _unfold(ref, g, rows, folded, prev_mask):
    ref[g, rows, pl.ds(0, BLOCK)] = jnp.where(prev_mask, folded, 0.0).astype(ref.dtype)
    ref[g, rows, pl.ds(BLOCK, BLOCK)] = jnp.where(prev_mask, 0.0, folded).astype(ref.dtype)


def _bias_table(rel_bias, bucket):
    def body(rb_ref, bk_ref, o_ref):
        bk = bk_ref[...]
        prev_mask = _prev_mask()
        for h in range(N_HEADS):
            acc = jnp.zeros((BLOCK, BLOCK), F32)
            for b in range(REL_BUCKETS):
                acc = jnp.where(bk == b, rb_ref[b, h], acc)
            o_ref[0, h] = acc
            o_ref[1, h] = jnp.where(prev_mask, NEG_INF, acc)

    return pl.pallas_call(
        body, name="bias_table", out_shape=jax.ShapeDtypeStruct((2, N_HEADS, BLOCK, BLOCK), F32),
        in_specs=[pl.BlockSpec(memory_space=pltpu.SMEM), pl.BlockSpec(memory_space=pltpu.VMEM)],
        out_specs=pl.BlockSpec(memory_space=pltpu.VMEM),
    )(rel_bias, bucket)


def _bias_grad(dbias, bucket):
    def body(db_ref, bk_ref, o_ref):
        bk = bk_ref[...]
        for b in range(REL_BUCKETS):
            sel = bk == b
            for h in range(N_HEADS):
                o_ref[b, h] = jnp.sum(jnp.where(sel, db_ref[h], 0.0))

    return pl.pallas_call(
        body, name="bias_grad", out_shape=jax.ShapeDtypeStruct((REL_BUCKETS, N_HEADS), F32),
        in_specs=[pl.BlockSpec(memory_space=pltpu.VMEM), pl.BlockSpec(memory_space=pltpu.VMEM)],
        out_specs=pl.BlockSpec(memory_space=pltpu.SMEM),
    )(dbias, bucket)


GROUP_ROWS = GROUP * BLOCK
BIAS_SPEC = pl.BlockSpec((2, N_HEADS, BLOCK, BLOCK), lambda n: (0, 0, 0, 0))


def _head_probs(qk, bias_h, sink, prev_mask):
    s = _fold(qk, prev_mask) + bias_h
    m = jnp.maximum(jnp.max(s, axis=-1, keepdims=True), sink)
    p = jnp.exp(s - m)
    ps = jnp.exp(sink - m)
    inv = 1.0 / (jnp.sum(p, axis=-1, keepdims=True) + ps)
    return p * inv, ps * inv


def _band(prev_ref, cur_ref, g):
    hs = pl.ds(g * HEAD_DIM, HEAD_DIM)
    return jnp.concatenate([prev_ref[:, hs], cur_ref[:, hs]], axis=0)


def _stack_heads(ref, g):
    return jnp.concatenate([ref[:, pl.ds((g * GROUP + hh) * HEAD_DIM, HEAD_DIM)] for hh in range(GROUP)], axis=0)


def _unstack_heads(ref, g, stacked, dtype):
    for hh in range(GROUP):
        ref[:, pl.ds((g * GROUP + hh) * HEAD_DIM, HEAD_DIM)] = stacked[hh * BLOCK:(hh + 1) * BLOCK, :].astype(dtype)


def _head_rows(hh):
    return pl.ds(hh * BLOCK, BLOCK)


def _attn_fwd(qn, kn, vv, bias, sinks):
    T = qn.shape[0]
    nb = T // BLOCK

    def body(sk_ref, q_ref, kc_ref, kp_ref, vc_ref, vp_ref, b_ref, o_ref, qk_buf, p_buf):
        table = (pl.program_id(0) == 0).astype(jnp.int32)
        prev_mask = _prev_mask()
        for g in range(N_KV):
            qk_buf[g] = _dot(_stack_heads(q_ref, g), _band(kp_ref, kc_ref, g), 1, 1)
        for g in range(N_KV):
            for hh in range(GROUP):
                h = g * GROUP + hh
                pn, _ = _head_probs(qk_buf[g, _head_rows(hh), :], b_ref[table, h], sk_ref[h], prev_mask)
                _unfold(p_buf, g, _head_rows(hh), pn, prev_mask)
        for g in range(N_KV):
            _unstack_heads(o_ref, g, _dot(p_buf[g], _band(vp_ref, vc_ref, g), 1, 0), BF)

    cur = lambda n: (n, 0)
    prev = lambda n: (jnp.maximum(n - 1, 0), 0)
    return pl.pallas_call(
        body, name="attn_fwd", grid=(nb,),
        in_specs=[pl.BlockSpec(memory_space=pltpu.SMEM), pl.BlockSpec((BLOCK, ATTN_DIM), cur),
                  pl.BlockSpec((BLOCK, KV_DIM), cur), pl.BlockSpec((BLOCK, KV_DIM), prev),
                  pl.BlockSpec((BLOCK, KV_DIM), cur), pl.BlockSpec((BLOCK, KV_DIM), prev), BIAS_SPEC],
        out_specs=pl.BlockSpec((BLOCK, ATTN_DIM), cur), out_shape=jax.ShapeDtypeStruct((T, ATTN_DIM), BF),
        scratch_shapes=[pltpu.VMEM((N_KV, GROUP_ROWS, 2 * BLOCK), F32), pltpu.VMEM((N_KV, GROUP_ROWS, 2 * BLOCK), BF)],
        compiler_params=_params(1),
    )(sinks, qn, kn, kn, vv, vv, bias)


def _attn_bwd(qn, kn, vv, bias, sinks, do):
    T = qn.shape[0]
    nb = T // BLOCK
    scale = 1.0 / math.sqrt(HEAD_DIM)

    def body(sk_ref, q_ref, kc_ref, kp_ref, vc_ref, vp_ref, b_ref, do_ref,
             dq_ref, dk_ref, dv_ref, db_ref, dsk_ref, dk_full, dv_full, dk_carry, dv_carry, qk_buf, dp_buf, p_buf, ds_buf):
        n = pl.program_id(0)

        @pl.when(n == 0)
        def _():
            db_ref[...] = jnp.zeros_like(db_ref)
            dk_carry[...] = jnp.zeros_like(dk_carry)
            dv_carry[...] = jnp.zeros_like(dv_carry)
            for h in range(N_HEADS):
                dsk_ref[h] = 0.0

        @pl.when(n < nb)
        def _():
            table = (n == 0).astype(jnp.int32)
            prev_mask = _prev_mask()
            ks = [_band(kp_ref, kc_ref, g) for g in range(N_KV)]
            qs = [_stack_heads(q_ref, g) for g in range(N_KV)]
            douts = [_stack_heads(do_ref, g) for g in range(N_KV)]
            for g in range(N_KV):
                qk_buf[g] = _dot(qs[g], ks[g], 1, 1)
                dp_buf[g] = _dot(douts[g], _band(vp_ref, vc_ref, g), 1, 1)
            for g in range(N_KV):
                for hh in range(GROUP):
                    h = g * GROUP + hh
                    rows = _head_rows(hh)
                    pn, psink = _head_probs(qk_buf[g, rows, :], b_ref[table, h], sk_ref[h], prev_mask)
                    dp = _fold(dp_buf[g, rows, :], prev_mask)
                    delta = jnp.sum(pn * dp, axis=-1, keepdims=True)
                    ds = pn * (dp - delta)
                    dsk_ref[h] += -jnp.sum(psink * delta)
                    db_ref[h] += ds
                    _unfold(ds_buf, g, rows, ds, prev_mask)
                    _unfold(p_buf, g, rows, pn, prev_mask)
            for g in range(N_KV):
                dsb = ds_buf[g]
                _unstack_heads(dq_ref, g, _dot(dsb, ks[g], 1, 0) * scale, F32)
                gs = pl.ds(g * HEAD_DIM, HEAD_DIM)
                dk_full[:, gs] = _dot(dsb, qs[g], 0, 0)
                dv_full[:, gs] = _dot(p_buf[g], douts[g], 0, 0)

        @pl.when(n == nb)
        def _():
            dk_full[...] = jnp.zeros_like(dk_full)
            dv_full[...] = jnp.zeros_like(dv_full)

        dk_ref[...] = dk_carry[...] + dk_full[pl.ds(0, BLOCK), :]
        dv_ref[...] = dv_carry[...] + dv_full[pl.ds(0, BLOCK), :]
        dk_carry[...] = dk_full[pl.ds(BLOCK, BLOCK), :]
        dv_carry[...] = dv_full[pl.ds(BLOCK, BLOCK), :]

    cur = lambda n: (jnp.minimum(n, nb - 1), 0)
    prev = lambda n: (jnp.maximum(jnp.minimum(n, nb - 1) - 1, 0), 0)
    out_kv = lambda n: (jnp.maximum(n - 1, 0), 0)
    return pl.pallas_call(
        body, name="attn_bwd", grid=(nb + 1,),
        in_specs=[pl.BlockSpec(memory_space=pltpu.SMEM), pl.BlockSpec((BLOCK, ATTN_DIM), cur),
                  pl.BlockSpec((BLOCK, KV_DIM), cur), pl.BlockSpec((BLOCK, KV_DIM), prev),
                  pl.BlockSpec((BLOCK, KV_DIM), cur), pl.BlockSpec((BLOCK, KV_DIM), prev), BIAS_SPEC,
                  pl.BlockSpec((BLOCK, ATTN_DIM), cur)],
        out_specs=[pl.BlockSpec((BLOCK, ATTN_DIM), cur), pl.BlockSpec((BLOCK, KV_DIM), out_kv),
                   pl.BlockSpec((BLOCK, KV_DIM), out_kv),
                   pl.BlockSpec((N_HEADS, BLOCK, BLOCK), lambda n: (0, 0, 0)),
                   pl.BlockSpec(memory_space=pltpu.SMEM)],
        out_shape=[jax.ShapeDtypeStruct((T, ATTN_DIM), F32), jax.ShapeDtypeStruct((T, KV_DIM), F32),
                   jax.ShapeDtypeStruct((T, KV_DIM), F32),
                   jax.ShapeDtypeStruct((N_HEADS, BLOCK, BLOCK), F32), jax.ShapeDtypeStruct((N_HEADS,), F32)],
        scratch_shapes=[pltpu.VMEM((2 * BLOCK, KV_DIM), F32), pltpu.VMEM((2 * BLOCK, KV_DIM), F32),
                        pltpu.VMEM((BLOCK, KV_DIM), F32), pltpu.VMEM((BLOCK, KV_DIM), F32),
                        pltpu.VMEM((N_KV, GROUP_ROWS, 2 * BLOCK), F32), pltpu.VMEM((N_KV, GROUP_ROWS, 2 * BLOCK), F32),
                        pltpu.VMEM((N_KV, GROUP_ROWS, 2 * BLOCK), BF), pltpu.VMEM((N_KV, GROUP_ROWS, 2 * BLOCK), BF)],
        compiler_params=_params(1),
    )(sinks, qn, kn, kn, vv, vv, bias, do)


def _coords():
    return lax.axis_index("x"), lax.axis_index("y"), lax.axis_index("c")


def _sum8(name, blocks):
    def body(b_ref, o_ref):
        tot = b_ref[0]
        for d in range(1, 8):
            tot = tot + b_ref[d]
        o_ref[...] = tot

    return pl.pallas_call(body, name=name, out_shape=jax.ShapeDtypeStruct(blocks.shape[1:], F32))(blocks)


HBM_SPEC = pl.BlockSpec(memory_space=pltpu.HBM)
SEM_SPEC = pl.BlockSpec(memory_space=pltpu.SEMAPHORE)
ANY_SPEC = pl.BlockSpec(memory_space=pl.ANY)
DATAFLOW = pltpu.SideEffectType.DATAFLOW_SIDE_EFFECTING


OTHER_CHIPS = (4, 2, 6)
ALL_OTHERS = (1, 2, 3, 4, 5, 6, 7)


def _slot(x, y, c, peers):
    return 2 * x + y if peers is OTHER_CHIPS else 4 * x + 2 * y + c


def _slot_copy(land, sems, idx, x, y, c, k, peers, arriving):
    send_sems, recv_sems = sems
    px, py, pc = x ^ (k >> 2), y ^ ((k >> 1) & 1), c ^ (k & 1)
    mine = _slot(x, y, c, peers)
    dst = _slot(px, py, pc, peers) if arriving else mine
    return pltpu.make_async_remote_copy(src_ref=land.at[mine], dst_ref=land.at[dst], send_sem=send_sems.at[idx],
                                        recv_sem=recv_sems.at[idx], device_id=(px, py, pc), device_id_type=MESH)


def _gather_start(name, stacks, groups, peers, after):
    n = len(stacks)
    ng = len(groups)
    np_ = len(peers)
    after = tuple(after)

    def body(*refs):
        lands = refs[:n]
        first = n + len(after)
        sems = [(refs[first + 2 * g], refs[first + 2 * g + 1]) for g in range(ng)]
        token = refs[-1]
        x, y, c = _coords()
        for g, members in enumerate(groups):
            for i, t in enumerate(members):
                for j, k in enumerate(peers):
                    _slot_copy(lands[t], sems[g], np_ * i + j, x, y, c, k, peers, arriving=False).start()
        token[...] = jnp.zeros_like(token)

    out_shape = []
    for members in groups:
        out_shape += [pltpu.SemaphoreType.DMA((np_ * len(members),))] * 2
    out_shape += [pltpu.HBM(w.shape, w.dtype) for w in stacks]
    out_shape.append(jax.ShapeDtypeStruct((8, 128), F32))
    res = pl.pallas_call(
        body, name=name, out_shape=out_shape, in_specs=[HBM_SPEC] * n + [ANY_SPEC] * len(after),
        out_specs=[SEM_SPEC] * (2 * ng) + [HBM_SPEC] * n + [pl.BlockSpec(memory_space=pltpu.VMEM)],
        input_output_aliases={t: 2 * ng + t for t in range(n)},
        compiler_params=pltpu.CompilerParams(has_side_effects=DATAFLOW),
    )(*[pltpu.with_memory_space_constraint(w, pltpu.HBM) for w in stacks], *after)
    sems = [(res[2 * g], res[2 * g + 1]) for g in range(ng)]
    return sems, list(res[2 * ng:2 * ng + n]), res[-1]


def _gather_wait(name, stacks, sems, peers, after):
    n = len(stacks)
    after = tuple(after)

    def body(*refs):
        lands = refs[:n]
        group_sems = (refs[n], refs[n + 1])
        x, y, c = _coords()
        for i in range(n):
            for j, k in enumerate(peers):
                cp = _slot_copy(lands[i], group_sems, len(peers) * i + j, x, y, c, k, peers, arriving=True)
                cp.wait_send()
                cp.wait_recv()

    return pl.pallas_call(
        body, name=name, out_shape=[pltpu.HBM(w.shape, w.dtype) for w in stacks],
        in_specs=[HBM_SPEC] * n + [SEM_SPEC, SEM_SPEC] + [ANY_SPEC] * len(after), out_specs=[HBM_SPEC] * n,
        input_output_aliases={t: t for t in range(n)},
        compiler_params=pltpu.CompilerParams(has_side_effects=DATAFLOW),
    )(*stacks, sems[0], sems[1], *after)


N_PEERS = 7


def _peer(x, y, c, k):
    return x ^ (k >> 2), y ^ ((k >> 1) & 1), c ^ (k & 1)


def _reduce_copy(grad, land, sems, idx, x, y, c, k):
    px, py, pc = _peer(x, y, c, k)
    rh = grad.shape[1] // 2
    return pltpu.make_async_remote_copy(src_ref=grad.at[2 * px + py, pl.ds(pc * rh, rh), :], dst_ref=land.at[k - 1],
                                        send_sem=sems[0].at[idx], recv_sem=sems[1].at[idx], device_id=(px, py, pc),
                                        device_id_type=MESH)


def _reduce_start(name, grads):
    n = len(grads)

    def body(*refs):
        src, lands, sems, token = refs[:n], refs[n:2 * n], (refs[2 * n], refs[2 * n + 1]), refs[-1]
        x, y, c = _coords()
        for t in range(n):
            for k in range(1, N_PEERS + 1):
                _reduce_copy(src[t], lands[t], sems, N_PEERS * t + k - 1, x, y, c, k).start()
        token[...] = jnp.zeros_like(token)

    lands = [lax.empty((N_PEERS, g.shape[1] // 2, g.shape[2]), g.dtype) for g in grads]
    out_shape = [pltpu.SemaphoreType.DMA((N_PEERS * n,))] * 2
    out_shape += [pltpu.HBM(a.shape, a.dtype) for a in list(grads) + lands]
    out_shape.append(jax.ShapeDtypeStruct((8, 128), F32))
    res = pl.pallas_call(
        body, name=name, out_shape=out_shape, in_specs=[HBM_SPEC] * (2 * n),
        out_specs=[SEM_SPEC] * 2 + [HBM_SPEC] * (2 * n) + [pl.BlockSpec(memory_space=pltpu.VMEM)],
        input_output_aliases={t: 2 + t for t in range(2 * n)},
        compiler_params=pltpu.CompilerParams(has_side_effects=DATAFLOW),
    )(*[pltpu.with_memory_space_constraint(a, pltpu.HBM) for a in list(grads) + lands])
    return (res[0], res[1]), list(res[2:2 + n]), list(res[2 + n:2 + 2 * n]), res[-1]


def _reduce_wait(name, grads, lands, sems, after):
    n = len(grads)
    after = tuple(after)

    def body(*refs):
        src, dst, group_sems = refs[:n], refs[n:2 * n], (refs[2 * n], refs[2 * n + 1])
        x, y, c = _coords()
        for t in range(n):
            for k in range(1, N_PEERS + 1):
                cp = _reduce_copy(src[t], dst[t], group_sems, N_PEERS * t + k - 1, x, y, c, k)
                cp.wait_send()
                cp.wait_recv()

    res = pl.pallas_call(
        body, name=name, out_shape=[pltpu.HBM(a.shape, a.dtype) for a in list(grads) + list(lands)],
        in_specs=[HBM_SPEC] * (2 * n) + [SEM_SPEC, SEM_SPEC] + [ANY_SPEC] * len(after), out_specs=[HBM_SPEC] * (2 * n),
        input_output_aliases={t: t for t in range(2 * n)},
        compiler_params=pltpu.CompilerParams(has_side_effects=DATAFLOW),
    )(*grads, *lands, sems[0], sems[1], *after)
    return list(res[:n]), list(res[n:])


def _join_copy(half, land, sems, idx, x, y, c):
    return pltpu.make_async_remote_copy(src_ref=half, dst_ref=land, send_sem=sems[0].at[idx], recv_sem=sems[1].at[idx],
                                        device_id=(x, y, 1 - c), device_id_type=MESH)


def _join_start(name, halves):
    n = len(halves)

    def body(*refs):
        src, lands, sems, token = refs[:n], refs[n:2 * n], (refs[2 * n], refs[2 * n + 1]), refs[-1]
        x, y, c = _coords()
        for t in range(n):
            _join_copy(src[t], lands[t], sems, t, x, y, c).start()
        token[...] = jnp.zeros_like(token)

    lands = [lax.empty(h.shape, h.dtype) for h in halves]
    out_shape = [pltpu.SemaphoreType.DMA((n,))] * 2
    out_shape += [pltpu.HBM(a.shape, a.dtype) for a in list(halves) + lands]
    out_shape.append(jax.ShapeDtypeStruct((8, 128), F32))
    res = pl.pallas_call(
        body, name=name, out_shape=out_shape, in_specs=[HBM_SPEC] * (2 * n),
        out_specs=[SEM_SPEC] * 2 + [HBM_SPEC] * (2 * n) + [pl.BlockSpec(memory_space=pltpu.VMEM)],
        input_output_aliases={t: 2 + t for t in range(2 * n)},
        compiler_params=pltpu.CompilerParams(has_side_effects=DATAFLOW),
    )(*[pltpu.with_memory_space_constraint(a, pltpu.HBM) for a in list(halves) + lands])
    return (res[0], res[1]), list(res[2:2 + n]), list(res[2 + n:2 + 2 * n]), res[-1]


def _join_wait(name, halves, lands, sems, after):
    n = len(halves)
    after = tuple(after)

    def body(*refs):
        src, dst, group_sems = refs[:n], refs[n:2 * n], (refs[2 * n], refs[2 * n + 1])
        x, y, c = _coords()
        for t in range(n):
            cp = _join_copy(src[t], dst[t], group_sems, t, x, y, c)
            cp.wait_send()
            cp.wait_recv()

    res = pl.pallas_call(
        body, name=name, out_shape=[pltpu.HBM(a.shape, a.dtype) for a in list(halves) + list(lands)],
        in_specs=[HBM_SPEC] * (2 * n) + [SEM_SPEC, SEM_SPEC] + [ANY_SPEC] * len(after), out_specs=[HBM_SPEC] * (2 * n),
        input_output_aliases={t: t for t in range(2 * n)},
        compiler_params=pltpu.CompilerParams(has_side_effects=DATAFLOW),
    )(*halves, *lands, sems[0], sems[1], *after)
    return list(res[:n]), list(res[n:])


def _join_halves(name, halves, deps=()):
    n = len(halves)

    def body(*refs):
        src, dst = refs[:n], refs[n + len(deps):2 * n + len(deps)]
        send_sems, recv_sems = refs[-2:]
        x, y, c = _coords()
        cps = []
        for t in range(n):
            cp = pltpu.make_async_remote_copy(src_ref=src[t], dst_ref=dst[t], send_sem=send_sems.at[t],
                                              recv_sem=recv_sems.at[t], device_id=(x, y, 1 - c), device_id_type=MESH)
            cp.start()
            cps.append(cp)
        for cp in cps:
            cp.wait()

    anyspec = pl.BlockSpec(memory_space=pl.ANY)
    return pl.pallas_call(
        body, name=name, out_shape=[jax.ShapeDtypeStruct(h.shape, h.dtype) for h in halves],
        in_specs=[anyspec] * (n + len(deps)), out_specs=[anyspec] * n,
        scratch_shapes=[pltpu.SemaphoreType.DMA((n,)), pltpu.SemaphoreType.DMA((n,))],
    )(*halves, *deps)


def _row_block(rows):
    for rb in (512, 256, 128, 64, 32, 16):
        if rows % rb == 0:
            return rb
    raise ValueError(rows)


def _sum_devices(name, grad, land, place):
    S, R, C = grad.shape
    rh = R // 2
    rb = _row_block(rh)
    nbh = rh // rb

    def body(place_ref, g_ref, l_ref, o_ref):
        tot = g_ref[...].astype(F32)
        for k in range(N_PEERS):
            tot = tot + l_ref[k].astype(F32)
        o_ref[...] = tot

    return pl.pallas_call(
        body, name=name,
        grid_spec=pltpu.PrefetchScalarGridSpec(
            num_scalar_prefetch=1, grid=(nbh,),
            in_specs=[pl.BlockSpec((None, rb, C), lambda r, place: (place[0], place[1] * nbh + r, 0)),
                      pl.BlockSpec((N_PEERS, rb, C), lambda r, place: (0, r, 0))],
            out_specs=pl.BlockSpec((rb, C), lambda r, place: (r, 0))),
        out_shape=jax.ShapeDtypeStruct((rh, C), F32), compiler_params=_params(1),
    )(place, grad, land)


def _adamw_math(w, g, m, v):
    m2 = ADAM_B1 * m + (1.0 - ADAM_B1) * g
    v2 = ADAM_B2 * v + (1.0 - ADAM_B2) * (g * g)
    m_hat = m2 / (1.0 - ADAM_B1 ** ADAM_STEP)
    v_hat = v2 / (1.0 - ADAM_B2 ** ADAM_STEP)
    delta = -ADAM_LR * (m_hat / (jnp.sqrt(v_hat) + ADAM_EPS) + ADAM_WD * w)
    return delta, m2, v2


def _adamw(name, w, m, v, gs):
    L, R, C = w.shape
    Rh = R // 2
    rb = _row_block(Rh)
    nbh = Rh // rb
    assert len(gs) == L

    def body(core_ref, w_ref, m_ref, v_ref, *rest):
        g_refs, (go_ref, d_ref, m2_ref, v2_ref) = rest[:2 * L], rest[2 * L:]
        layer, half = pl.program_id(0), pl.program_id(1)
        mine = half == core_ref[0]
        g = jnp.where(mine, g_refs[0][...], g_refs[1][...])
        for t in range(1, L):
            g = jnp.where(layer == t, jnp.where(mine, g_refs[2 * t][...], g_refs[2 * t + 1][...]), g)
        delta, m2, v2 = _adamw_math(w_ref[...], g, m_ref[...], v_ref[...])
        go_ref[...] = g
        d_ref[...] = delta
        m2_ref[...] = m2
        v2_ref[...] = v2

    wspec = pl.BlockSpec((None, rb, C), lambda l, h, r, core: (l, h * nbh + r, 0))
    gspec = pl.BlockSpec((rb, C), lambda l, h, r, core: (r, 0))
    return pl.pallas_call(
        body, name=name,
        grid_spec=pltpu.PrefetchScalarGridSpec(num_scalar_prefetch=1, grid=(L, 2, nbh),
                                               in_specs=[wspec] * 3 + [gspec] * (2 * L), out_specs=[wspec] * 4),
        out_shape=[jax.ShapeDtypeStruct((L, R, C), F32)] * 4, compiler_params=_params(3),
    )(lax.axis_index("c").astype(jnp.int32).reshape(1), w, m, v, *[g for pair in gs for g in pair])


def _adamw_small(ws, gs, ms, vs):
    n = len(ws)

    def body(*refs):
        w_refs, g_refs, m_refs, v_refs = (refs[k * n:(k + 1) * n] for k in range(4))
        d_refs, m2_refs, v2_refs = (refs[(4 + k) * n:(5 + k) * n] for k in range(3))
        for t in range(n):
            delta, m2, v2 = _adamw_math(w_refs[t][...], g_refs[t][...], m_refs[t][...], v_refs[t][...])
            d_refs[t][...] = delta
            m2_refs[t][...] = m2
            v2_refs[t][...] = v2

    res = pl.pallas_call(body, name="adamw_small", out_shape=[jax.ShapeDtypeStruct(w.shape, F32) for w in ws] * 3)(
        *ws, *gs, *ms, *vs)
    return res[:n], res[n:2 * n], res[2 * n:]


def _packed_rows(shape):
    c = shape[-1]
    return (int(np.prod(shape)) // c) * -(-c // LANES)


def _pack(arrays):
    total = sum(_packed_rows(a.shape) for a in arrays)
    total += -total % 8
    buf, r0 = None, 0
    for a in arrays:
        a = a.astype(F32).reshape(-1, a.shape[-1])
        r, c = a.shape
        k = -(-c // LANES)
        a = jnp.pad(a, ((0, 0), (0, k * LANES - c))).reshape(r * k, LANES)
        a = jnp.pad(a, ((r0, total - r0 - r * k), (0, 0)))
        buf = a if buf is None else buf + a
        r0 += r * k
    return buf


def _unpack(buf, shapes):
    out, r0 = [], 0
    for shp in shapes:
        c = shp[-1]
        rows = _packed_rows(shp)
        out.append(buf[r0:r0 + rows].reshape(-1, -(-c // LANES) * LANES)[:, :c].reshape(shp))
        r0 += rows
    return out


def _rms(x, g):
    return x * lax.rsqrt(jnp.mean(x * x, axis=-1, keepdims=True) + NORM_EPS) * g


def _residual_norm_ep(acc, *rest):
    *bias, res, gain = rest
    x = acc + res + (bias[0] if bias else 0.0)
    return x, _rms(x, gain)


RESIDUAL_NORM_OUTS = (("tile", F32), ("tile", BF))


def _mlp_up(tag, h, w_up_sm):
    (up,) = _mm(f"mlp{tag}_up", h, w_up_sm, nt=False, b_sm=True, tm=2048, tn=1024, rows=256,
                ep_fn=lambda acc: (acc,), outs=(("tile", BF),))
    return up


RMS_BWD_OUTS = (("tile", F32), ("tile", BF), ("colsum", F32), ("colsum", F32))


def _mlp_bwd(tag, dy, dy_bf, x, g, up, w_up_sm, w_down):
    (dup,) = _mm(f"mlp{tag}_dup", dy_bf, w_down, nt=True, tm=2048, tn=1024, rows=256, ep_in=((up, "tile"),),
                 ep_fn=lambda acc, u: (acc * (2.0 * jnp.maximum(u.astype(F32), 0.0)),), outs=(("tile", BF),))
    dx, dx_bf, dg, dx_sum = _mm(f"mlp{tag}_dx", dup, w_up_sm, nt=True, b_sm=True, tm=512, tn=1024, rows=256,
                                ep_in=((x, "tile"), (g, "row"), (dy, "tile")), ep_fn=_rms_bwd_ep, outs=RMS_BWD_OUTS)
    return dx, dx_bf, dg, dx_sum, dup


class _Reduction:
    def __init__(self, tag, grads, place):
        self.tag, self.place = tag, place
        self.sems, self.grads, self.lands, self.token = _reduce_start(f"reduce_start_{tag}", grads)

    def finish(self, after):
        grads, lands = _reduce_wait(f"reduce_wait_{self.tag}", self.grads, self.lands, self.sems, after)
        return [_sum_devices(f"reduce_sum_{self.tag}{i}", g, l, self.place) for i, (g, l) in enumerate(zip(grads, lands))]


def kernel(x, conv_norm_g, conv_w_in, conv_b_in, conv_dw, conv_dw_b, conv_ln_g, conv_ln_b, conv_w_out, conv_b_out, attn_norm_g, w_qkv, b_qkv, q_norm_g, k_norm_g, sinks, w_o, b_o, rel_bias, mlp_norm_g, w_up, w_down, loss_target, m_conv_norm_g, m_conv_w_in, m_conv_b_in, m_conv_dw, m_conv_dw_b, m_conv_ln_g, m_conv_ln_b, m_conv_w_out, m_conv_b_out, m_attn_norm_g, m_w_qkv, m_b_qkv, m_q_norm_g, m_k_norm_g, m_sinks, m_w_o, m_b_o, m_rel_bias, m_mlp_norm_g, m_w_up, m_w_down, v_conv_norm_g, v_conv_w_in, v_conv_b_in, v_conv_dw, v_conv_dw_b, v_conv_ln_g, v_conv_ln_b, v_conv_w_out, v_conv_b_out, v_attn_norm_g, v_w_qkv, v_b_qkv, v_q_norm_g, v_k_norm_g, v_sinks, v_w_o, v_b_o, v_rel_bias, v_mlp_norm_g, v_w_up, v_w_down):
    Dm = D_MODEL
    x2d = x[0]
    tgt = loss_target[0]
    T = x2d.shape[0]
    shard = 2 * lax.axis_index("x") + lax.axis_index("y")

    me = 2 * shard + lax.axis_index("c")

    def own_slot(block, slots, index):
        return lax.dynamic_update_slice(lax.empty((slots,) + block.shape, block.dtype), block[None],
                                        (index,) + (0,) * block.ndim)

    (conv_in_sems,), (stack_in,), first_token = _gather_start(
        "gather_start_conv_in", [own_slot(conv_w_in[0].astype(BF), N_SHARD, shard)], ((0,),), OTHER_CHIPS, after=())
    sharded_small = [conv_dw[0], attn_norm_g, b_qkv, b_o]
    (small_sems,), (small_land,), small_token = _gather_start(
        "small_weights_start", [own_slot(_pack(sharded_small), 8, me)], ((0,),), ALL_OTHERS, after=(first_token,))

    big = [conv_w_out[0], w_qkv[0], w_o[0], w_up[0], w_up[1], w_down[0], w_down[1]]
    stacks = [own_slot(w.astype(BF), N_SHARD, shard) for w in big]
    groups = ((0,), (3, 5), (1, 2), (4, 6))
    gather_sems, stacks, gather_token = _gather_start("gather_start", stacks, groups, OTHER_CHIPS, after=(small_token,))

    def gathered_group(g, name, after):
        return _gather_wait(name, [stacks[t] for t in groups[g]], gather_sems[g], OTHER_CHIPS, after)

    bucket = _bucket_table()
    bias = _bias_table(rel_bias, bucket)

    h0 = _rms_fwd("conv_norm", x2d, conv_norm_g, deps=(gather_token,))
    (w_in_sm,) = _gather_wait("gather_wait_conv_in", [stack_in], conv_in_sems, OTHER_CHIPS, (h0, bias))
    (u,) = _mm("conv_in", h0, w_in_sm, nt=False, b_sm=True, tm=2048, tn=512, rows=256, ep_in=((conv_b_in, "row"),),
               ep_fn=lambda acc, b: (acc + b,), outs=(("tile", BF),))
    (gathered,) = _gather_wait("small_weights_wait", [small_land], small_sems, ALL_OTHERS, (u,))
    chips = [_unpack(gathered[2 * s], [a.shape for a in sharded_small]) for s in range(N_SHARD)]
    dw_f, attn_norm_f, b_qkv_f, b_o_f = (jnp.concatenate([chips[s][t] for s in range(N_SHARD)], axis=-1)
                                         for t in range(len(sharded_small)))
    dw_pad = jnp.pad(dw_f, ((0, HALO - CONV_W), (0, 0)))
    cv, s_act = _conv_fwd(u, dw_pad, conv_dw_b, conv_ln_g, conv_ln_b)
    (g_out,) = gathered_group(0, "gather_wait_conv_out", (s_act,))
    w_out_f = g_out.reshape(Dm, Dm)
    x1, h1 = _mm("conv_out", s_act, w_out_f, nt=False, tm=1024, tn=1024, rows=256,
                 ep_in=((conv_b_out, "row"), (x2d, "tile"), (mlp_norm_g[0:1], "row")), ep_fn=_residual_norm_ep,
                 outs=RESIDUAL_NORM_OUTS)

    g_up0, g_down0 = gathered_group(1, "gather_wait_mlp0", (x1,))
    w_up_sm = [g_up0, None]
    w_down_f = [g_down0.reshape(D_FF, Dm), None]
    up0 = _mlp_up(0, h1, w_up_sm[0])
    x2, h2 = _mm("mlp0_down", up0, w_down_f[0], nt=False, tm=512, tn=1024, rows=256, a_fn=_relu2,
                 ep_in=((x1, "tile"), (attn_norm_f, "row")), ep_fn=_residual_norm_ep, outs=RESIDUAL_NORM_OUTS)

    g_qkv, g_o = gathered_group(2, "gather_wait_attn", (x2,))
    w_qkv_f = jnp.transpose(g_qkv, (1, 0, 2)).reshape(Dm, QKV_DIM)
    w_o_f = g_o.reshape(ATTN_DIM, Dm)
    qg_t = jnp.tile(q_norm_g, (1, N_HEADS))
    kg_t = jnp.tile(k_norm_g, (1, N_KV))

    def qkv_ep(acc, b, qg, kg, ones):
        proj = acc + b
        q, k, v = proj[:, :ATTN_DIM], proj[:, ATTN_DIM:ATTN_DIM + KV_DIM], proj[:, ATTN_DIM + KV_DIM:]
        return proj, _qk_normed(q, qg, ones, 1.0 / math.sqrt(HEAD_DIM)), _qk_normed(k, kg, ones, 1.0), v

    qkv, qn, kn, vv = _mm(
        "attn_qkv", h2, w_qkv_f, nt=False, tm=1024, tn=QKV_DIM, rows=256, ep_fn=qkv_ep,
        ep_in=((b_qkv_f, "row"), (qg_t, "whole"), (kg_t, "whole"), (_head_ones(), "whole")),
        outs=(("tile", F32), ("tile", BF, ATTN_DIM), ("tile", BF, KV_DIM), ("tile", BF, KV_DIM)))
    sinks1 = sinks[0]
    att = _attn_fwd(qn, kn, vv, bias, sinks1)
    x3, h3 = _mm("attn_out", att, w_o_f, nt=False, tm=1024, tn=1024, rows=256,
                 ep_in=((b_o_f, "row"), (x2, "tile"), (mlp_norm_g[1:2], "row")), ep_fn=_residual_norm_ep,
                 outs=RESIDUAL_NORM_OUTS)

    g_up1, g_down1 = gathered_group(3, "gather_wait_mlp1", (x3,))
    w_up_sm[1] = g_up1
    w_down_f[1] = g_down1.reshape(D_FF, Dm)
    up1 = _mlp_up(1, h3, w_up_sm[1])

    def loss_ep(acc, r, t):
        diff = acc + r - t
        dy = diff * (1.0 / Dm)
        return dy, dy, jnp.sum(diff * diff, axis=0, keepdims=True)

    dy, dy_bf, sq = _mm("mlp1_down_loss", up1, w_down_f[1], nt=False, tm=512, tn=1024, rows=256, a_fn=_relu2,
                        ep_in=((x3, "tile"), (tgt, "tile")), ep_fn=loss_ep,
                        outs=(("tile", F32), ("tile", BF), ("colsum", F32)))

    place = jnp.stack([shard, lax.axis_index("c")]).astype(jnp.int32)
    dx3, dx3_bf, dg_mlp1, db_o, dup1 = _mlp_bwd(1, dy, dy_bf, x3, mlp_norm_g[1:2], up1, w_up_sm[1], w_down_f[1])
    dw_down1 = _mm_tn("mlp1_dw_down", up1, dy_bf, tm=1024, tn=1024, tk=2048, a_fn=_relu2)
    dw_up1 = _mm_tn("mlp1_dw_up", h3, dup1, tm=1024, tn=1024, tk=2048, out_sm=N_SHARD)
    red_mlp1 = _Reduction("mlp1", [dw_up1, dw_down1.reshape(N_SHARD, D_FF // N_SHARD, Dm)], place)

    ident = lambda acc: (acc,)
    (datt,) = _mm("attn_dout", dx3_bf, w_o_f, nt=True, tm=1024, tn=1024, rows=256, ep_fn=ident, outs=(("tile", BF),),
                  deps=(red_mlp1.token,))
    dw_o = _mm_tn("attn_dw_o", att, dx3_bf, tm=1024, tn=1024, tk=2048)
    dqn, dkn, dvv, dbias, dsinks = _attn_bwd(qn, kn, vv, bias, sinks1, datt)
    drel = _bias_grad(dbias, bucket)
    dqkv, db_qkv, dqg_t, dkg_t = _qk_norm_bwd(qkv, dqn, dkn, dvv, qg_t, kg_t)
    dw_qkv = _mm_tn("attn_dw_qkv", h2, dqkv, tm=1024, tn=QKV_DIM, tk=2048)
    red_attn = _Reduction("attn", [jnp.transpose(dw_qkv.reshape(Dm, N_SHARD, QKV_DIM // N_SHARD), (1, 0, 2)),
                                   dw_o.reshape(N_SHARD, ATTN_DIM // N_SHARD, Dm)], place)
    dx2, dx2_bf, dg_attn, _ = _mm("attn_dx", dqkv, w_qkv_f, nt=True, tm=1024, tn=1024, rows=256,
                                  ep_in=((x2, "tile"), (attn_norm_f, "row"), (dx3, "tile")), ep_fn=_rms_bwd_ep,
                                  outs=RMS_BWD_OUTS, deps=(red_attn.token,))

    dx1, dx1_bf, dg_mlp0, db_out, dup0 = _mlp_bwd(0, dx2, dx2_bf, x1, mlp_norm_g[0:1], up0, w_up_sm[0], w_down_f[0])
    dw_down0 = _mm_tn("mlp0_dw_down", up0, dx2_bf, tm=1024, tn=1024, tk=2048, a_fn=_relu2)
    dw_up0 = _mm_tn("mlp0_dw_up", h1, dup0, tm=1024, tn=1024, tk=2048, out_sm=N_SHARD)
    dw_out = _mm_tn("conv_dw_out", s_act, dx1_bf, tm=1024, tn=1024, tk=2048)
    red_mlp0 = _Reduction("mlp0", [dw_up0, dw_down0.reshape(N_SHARD, D_FF // N_SHARD, Dm),
                                   dw_out.reshape(N_SHARD, Dm // N_SHARD, Dm)], place)
    (r_qkv, r_o) = red_attn.finish((dx1,))
    (r_up1, r_down1) = red_mlp1.finish((dx1,))

    dcv, dln_g, dln_b, ddw_b = _mm("conv_ds", dx1_bf, w_out_f, nt=True, tm=1024, tn=1024, rows=256,
                                   ep_in=((cv, "tile"), (conv_ln_g, "row"), (conv_ln_b, "row")),
                                   ep_fn=_ln_silu_bwd_ep,
                                   outs=(("tile", F32), ("colsum", F32), ("colsum", F32), ("colsum", F32)),
                                   deps=(red_mlp0.token,))
    du, db_in, ddw8 = _conv_bwd(u, dcv, dw_pad)
    (r_up0, r_down0, r_out) = red_mlp0.finish((du,))
    early = [r_out, r_qkv, r_o, r_up0, r_up1, r_down0, r_down1]
    join_sems, early, early_lands, join_token = _join_start("join_start", early)
    dw_in = _mm_tn("conv_dw_in", h0, du, tm=1024, tn=512, tk=4096, out_sm=N_SHARD)
    red_conv = _Reduction("conv", [dw_in], place)
    def first_layer_ep(*args):
        tot, _, dg, _ = _rms_bwd_ep(*args)
        return tot, dg

    gx, dg_conv = _mm("conv_dx", du, w_in_sm, nt=True, b_sm=True, tm=1024, tn=1024, rows=256,
                      ep_in=((x2d, "tile"), (conv_norm_g, "row"), (dx1, "tile")), ep_fn=first_layer_ep,
                      outs=(("tile", F32), ("colsum", F32)), deps=(red_conv.token, join_token))
    (r_in,) = red_conv.finish((gx,))

    dqg = dqg_t.reshape(N_HEADS, HEAD_DIM).sum(axis=0, keepdims=True)
    dkg = dkg_t.reshape(N_KV, HEAD_DIM).sum(axis=0, keepdims=True)
    small_full = [dg_conv, db_in, ddw8.sum(axis=1)[:CONV_W], ddw_b, dln_g, dln_b, db_out, dg_attn, db_qkv, dqg, dkg,
                  dsinks[None, :], db_o, drel.reshape(1, REL_BUCKETS * N_HEADS),
                  jnp.pad(dg_mlp0, ((0, 1), (0, 0))) + jnp.pad(dg_mlp1, ((1, 0), (0, 0))), sq]
    (sg_sems,), (sg_land,), sg_token = _gather_start(
        "small_grads_start", [own_slot(_pack(small_full), 8, me)], ((0,),), ALL_OTHERS, after=())

    early, early_sibling = _join_wait("join_wait", early, early_lands, join_sems, (gx, sg_token))
    r_out, r_qkv, r_o, r_up0, r_up1, r_down0, r_down1 = zip(early, early_sibling)
    r_in = (r_in,) + tuple(_join_halves("join_halves", [r_in], deps=(sg_token,)))

    big_out = {}
    for nm, w, m, v, gs in (("conv_w_in", conv_w_in, m_conv_w_in, v_conv_w_in, (r_in,)),
                            ("conv_w_out", conv_w_out, m_conv_w_out, v_conv_w_out, (r_out,)),
                            ("w_qkv", w_qkv, m_w_qkv, v_w_qkv, (r_qkv,)),
                            ("w_o", w_o, m_w_o, v_w_o, (r_o,)),
                            ("w_up", w_up, m_w_up, v_w_up, (r_up0, r_up1)),
                            ("w_down", w_down, m_w_down, v_w_down, (r_down0, r_down1))):
        big_out[nm] = _adamw(f"adamw_{nm}", w, m, v, gs)

    (sg_land,) = _gather_wait("small_grads_wait", [sg_land], sg_sems, ALL_OTHERS,
                              [big_out[nm][0] for nm in big_out])
    small_sum = _sum8("small_grads_sum", sg_land)
    (r_norm, r_b_in, r_dw, r_dw_b, r_ln_g, r_ln_b, r_b_out, r_attn_norm, r_b_qkv, r_qg, r_kg, r_sinks, r_b_o, r_rel,
     r_mlp_norm, r_sq) = _unpack(small_sum, [a.shape for a in small_full])
    loss = 0.5 * jnp.sum(r_sq) * (1.0 / Dm)

    def cols(a, width):
        return lax.dynamic_slice_in_dim(a, shard * width, width, axis=a.ndim - 1)

    small_names = ["conv_norm_g", "conv_b_in", "conv_dw", "conv_dw_b", "conv_ln_g", "conv_ln_b", "conv_b_out",
                   "attn_norm_g", "b_qkv", "q_norm_g", "k_norm_g", "sinks", "b_o", "rel_bias", "mlp_norm_g"]
    small_g = [r_norm, r_b_in, cols(r_dw, Dm // N_SHARD)[None], r_dw_b, r_ln_g, r_ln_b, r_b_out,
               cols(r_attn_norm, Dm // N_SHARD), cols(r_b_qkv, QKV_DIM // N_SHARD), r_qg, r_kg, r_sinks,
               cols(r_b_o, Dm // N_SHARD), r_rel.reshape(REL_BUCKETS, N_HEADS), r_mlp_norm]
    small_w = [conv_norm_g, conv_b_in, conv_dw, conv_dw_b, conv_ln_g, conv_ln_b, conv_b_out, attn_norm_g, b_qkv,
               q_norm_g, k_norm_g, sinks, b_o, rel_bias, mlp_norm_g]
    small_m = [m_conv_norm_g, m_conv_b_in, m_conv_dw, m_conv_dw_b, m_conv_ln_g, m_conv_ln_b, m_conv_b_out,
               m_attn_norm_g, m_b_qkv, m_q_norm_g, m_k_norm_g, m_sinks, m_b_o, m_rel_bias, m_mlp_norm_g]
    small_v = [v_conv_norm_g, v_conv_b_in, v_conv_dw, v_conv_dw_b, v_conv_ln_g, v_conv_ln_b, v_conv_b_out,
               v_attn_norm_g, v_b_qkv, v_q_norm_g, v_k_norm_g, v_sinks, v_b_o, v_rel_bias, v_mlp_norm_g]
    flat2 = lambda a: a.reshape(-1, a.shape[-1])
    small_g = [flat2(g) for g in small_g]
    d_s, m_s, v_s = _adamw_small([flat2(w) for w in small_w], small_g, [flat2(m) for m in small_m],
                                 [flat2(v) for v in small_v])
    small_out = {}
    for nm, w, g, d, m2, v2 in zip(small_names, small_w, small_g, d_s, m_s, v_s):
        small_out[nm] = tuple(a.reshape(w.shape) for a in (g, d, m2, v2))

    order = ["conv_norm_g", "conv_w_in", "conv_b_in", "conv_dw", "conv_dw_b", "conv_ln_g", "conv_ln_b", "conv_w_out",
             "conv_b_out", "attn_norm_g", "w_qkv", "b_qkv", "q_norm_g", "k_norm_g", "sinks", "w_o", "b_o", "rel_bias",
             "mlp_norm_g", "w_up", "w_down"]
    res = {**small_out, **big_out}
    outs = [loss, gx[None]]
    for part in range(4):
        outs += [res[nm][part] for nm in order]
    return tuple(outs)
```

```python
import math

import numpy as np
import jax
import jax.numpy as jnp
from jax import lax
from jax.experimental import pallas as pl
from jax.experimental.pallas import tpu as pltpu

F32 = jnp.float32
BF = jnp.bfloat16
MESH = pl.DeviceIdType.MESH

D_MODEL = 1024
D_FF = 4096
N_HEADS = 16
N_KV = 2
GROUP = N_HEADS // N_KV
HEAD_DIM = 64
ATTN_DIM = N_HEADS * HEAD_DIM
KV_DIM = N_KV * HEAD_DIM
QKV_DIM = ATTN_DIM + 2 * KV_DIM
BLOCK = 128
CONV_W = 31
HALO = 32
REL_BUCKETS = 32
REL_MAX_DIST = 128
NORM_EPS = 1e-6
NEG_INF = -1e30
N_SHARD = 4
LANES = 1024

ADAM_LR = 0.001
ADAM_B1 = 0.9
ADAM_B2 = 0.999
ADAM_EPS = 1e-08
ADAM_WD = 0.01
ADAM_STEP = 10

VMEM_LIMIT = 56 * 1024 * 1024


def _params(n_axes):
    return pltpu.CompilerParams(dimension_semantics=("arbitrary",) * n_axes, vmem_limit_bytes=VMEM_LIMIT)


def _dot(a, b, ca, cb):
    return lax.dot_general(a, b, (((ca,), (cb,)), ((), ())), preferred_element_type=F32)


def _mm(name, a, b, *, nt, tm, tn, ep_fn, outs, a_fn=None, b_sm=False, ep_in=(), deps=(), rows=None):
    M, K = a.shape
    rows = tm if rows is None else rows
    if b_sm:
        S, ks = b.shape[0], b.shape[2]
        N, per = (b.shape[1], None) if nt else (S * b.shape[2], b.shape[2] // tn)
        assert (S * ks == K) if nt else (b.shape[1] == K)
    else:
        N = b.shape[0] if nt else b.shape[1]
        assert (b.shape[1] if nt else b.shape[0]) == K
    assert M % tm == 0 and N % tn == 0 and tm % rows == 0
    ne, no, nd = len(ep_in), len(outs), len(deps)

    def body(a_ref, b_ref, *rest):
        ep_refs, out_refs = rest[:ne], rest[ne + nd:ne + nd + no]
        i = pl.program_id(1)
        sums = [None] * no
        for r in range(tm // rows):
            rs = pl.ds(r * rows, rows)

            def lhs(cols):
                av = a_ref[rs, cols]
                return (av if a_fn is None else a_fn(av)).astype(BF)

            if b_sm and nt:
                acc = None
                for s in range(S):
                    part = _dot(lhs(pl.ds(s * ks, ks)), b_ref[s].astype(BF), 1, 1)
                    acc = part if acc is None else acc + part
            else:
                acc = _dot(lhs(slice(None)), b_ref[...].astype(BF), 1, 1 if nt else 0)
            ep_vals = [ref[rs, :] if kind == "tile" else ref[...] for ref, (_, kind) in zip(ep_refs, ep_in)]
            vals = ep_fn(acc, *ep_vals)
            for o, ((kind, dt, *_), ref, val) in enumerate(zip(outs, out_refs, vals)):
                if kind == "tile":
                    ref[rs, :] = val.astype(dt)
                else:
                    sums[o] = val if sums[o] is None else sums[o] + val
        for (kind, *_), ref, val in zip(outs, out_refs, sums):
            if kind == "colsum":
                @pl.when(i == 0)
                def _():
                    ref[...] = val

                @pl.when(i > 0)
                def _():
                    ref[...] += val

    if b_sm and nt:
        b_spec = pl.BlockSpec((S, tn, ks), lambda j, i: (0, j, 0))
    elif b_sm:
        b_spec = pl.BlockSpec((None, K, tn), lambda j, i: (j // per, 0, j % per))
    elif nt:
        b_spec = pl.BlockSpec((tn, K), lambda j, i: (j, 0))
    else:
        b_spec = pl.BlockSpec((K, tn), lambda j, i: (0, j))
    in_specs = [pl.BlockSpec((tm, K), lambda j, i: (i, 0)), b_spec]
    for arr, kind in ep_in:
        if kind == "tile":
            assert arr.shape == (M, N)
            in_specs.append(pl.BlockSpec((tm, tn), lambda j, i: (i, j)))
        elif kind == "whole":
            in_specs.append(pl.BlockSpec(arr.shape, lambda j, i, rank=arr.ndim: (0,) * rank))
        else:
            assert arr.shape == (1, N)
            in_specs.append(pl.BlockSpec((1, tn), lambda j, i: (0, j)))
    in_specs += [pl.BlockSpec(memory_space=pl.ANY)] * nd
    out_shape, out_specs = [], []
    for kind, dt, *width in outs:
        if kind == "tile" and width:
            assert tn == N
            out_shape.append(jax.ShapeDtypeStruct((M, width[0]), dt))
            out_specs.append(pl.BlockSpec((tm, width[0]), lambda j, i: (i, 0)))
        elif kind == "tile":
            out_shape.append(jax.ShapeDtypeStruct((M, N), dt))
            out_specs.append(pl.BlockSpec((tm, tn), lambda j, i: (i, j)))
        else:
            out_shape.append(jax.ShapeDtypeStruct((1, N), F32))
            out_specs.append(pl.BlockSpec((1, tn), lambda j, i: (0, j)))
    return pl.pallas_call(
        body, name=name, grid=(N // tn, M // tm), in_specs=in_specs, out_specs=out_specs, out_shape=out_shape,
        compiler_params=_params(2),
    )(a, b, *[arr for arr, _ in ep_in], *deps)


def _mm_tn(name, a, b, *, tm, tn, tk, a_fn=None, out_sm=None):
    T, Ka = a.shape
    N = b.shape[1]
    assert b.shape[0] == T and T % tk == 0 and Ka % tm == 0 and N % tn == 0
    nk = T // tk

    def body(a_ref, b_ref, o_ref, acc_ref):
        k = pl.program_id(2)

        @pl.when(k == 0)
        def _():
            acc_ref[...] = jnp.zeros_like(acc_ref)

        av = a_ref[...]
        if a_fn is not None:
            av = a_fn(av)
        acc_ref[...] += _dot(av.astype(BF), b_ref[...].astype(BF), 0, 0)

        @pl.when(k == nk - 1)
        def _():
            o_ref[...] = acc_ref[...].astype(BF)

    if out_sm is None:
        out_shape = jax.ShapeDtypeStruct((Ka, N), BF)
        out_spec = pl.BlockSpec((tm, tn), lambda i, j, k: (i, j))
    else:
        per = (N // out_sm) // tn
        assert per * tn * out_sm == N
        out_shape = jax.ShapeDtypeStruct((out_sm, Ka, N // out_sm), BF)
        out_spec = pl.BlockSpec((None, tm, tn), lambda i, j, k: (j // per, i, j % per))
    return pl.pallas_call(
        body, name=name, grid=(Ka // tm, N // tn, nk),
        in_specs=[pl.BlockSpec((tk, tm), lambda i, j, k: (k, i)), pl.BlockSpec((tk, tn), lambda i, j, k: (k, j))],
        out_specs=out_spec, out_shape=out_shape, scratch_shapes=[pltpu.VMEM((tm, tn), F32)],
        compiler_params=_params(3),
    )(a, b)


def _relu2(v):
    r = jnp.maximum(v.astype(F32), 0.0)
    return r * r


def _rms_bwd_ep(dh, x, g, dres):
    rstd = lax.rsqrt(jnp.mean(x * x, axis=-1, keepdims=True) + NORM_EPS)
    xh = x * rstd
    dxh = dh * g
    dx = rstd * (dxh - xh * jnp.mean(dxh * xh, axis=-1, keepdims=True))
    tot = dres + dx
    return tot, tot, jnp.sum(dh * xh, axis=0, keepdims=True), jnp.sum(tot, axis=0, keepdims=True)


def _rms_fwd(name, x, g, tm=512, deps=()):
    T, Dm = x.shape

    def body(x_ref, g_ref, *rest):
        o_ref = rest[-1]
        xv = x_ref[...]
        rstd = lax.rsqrt(jnp.mean(xv * xv, axis=-1, keepdims=True) + NORM_EPS)
        o_ref[...] = (xv * rstd * g_ref[...]).astype(BF)

    return pl.pallas_call(
        body, name=name, grid=(T // tm,),
        in_specs=[pl.BlockSpec((tm, Dm), lambda i: (i, 0)), pl.BlockSpec((1, Dm), lambda i: (0, 0))]
        + [pl.BlockSpec(memory_space=pl.ANY)] * len(deps),
        out_specs=pl.BlockSpec((tm, Dm), lambda i: (i, 0)), out_shape=jax.ShapeDtypeStruct((T, Dm), BF),
        compiler_params=_params(1),
    )(x, g, *deps)


HEAD_GROUP = 256


def _head_sum(v, ones):
    n = v.shape[1]
    w = min(n, HEAD_GROUP)
    blk = ones[:w, :w]
    parts = [_dot(v[:, c:c + w].astype(BF), blk, 1, 0) for c in range(0, n, w)]
    return parts[0] if len(parts) == 1 else jnp.concatenate(parts, axis=1)


def _head_ones():
    idx = np.arange(HEAD_GROUP) // HEAD_DIM
    return jnp.asarray((idx[:, None] == idx[None, :]).astype(np.float32), dtype=BF)


def _qk_normed(x, g, ones, scale):
    r = lax.rsqrt(_head_sum(x * x, ones) * (1.0 / HEAD_DIM) + NORM_EPS)
    return x * r * g * scale


def _qk_norm_bwd(qkv, dqn, dkn, dv, qg_t, kg_t, tm=256):
    T = qkv.shape[0]

    def body(x_ref, dq_ref, dk_ref, dv_ref, qg_ref, kg_ref, ones_ref, o_ref, db_ref, dqg_ref, dkg_ref):
        i = pl.program_id(0)
        ones = ones_ref[...]

        def one(x, dy, g):
            r = lax.rsqrt(_head_sum(x * x, ones) * (1.0 / HEAD_DIM) + NORM_EPS)
            xh = x * r
            dxh = dy * g
            dx = r * (dxh - xh * (_head_sum(dxh * xh, ones) * (1.0 / HEAD_DIM)))
            return dx, jnp.sum(dy * xh, axis=0, keepdims=True)

        dq, dqg = one(x_ref[:, pl.ds(0, ATTN_DIM)], dq_ref[...], qg_ref[...])
        dk, dkg = one(x_ref[:, pl.ds(ATTN_DIM, KV_DIM)], dk_ref[...], kg_ref[...])
        dvv = dv_ref[...]
        o_ref[:, pl.ds(0, ATTN_DIM)] = dq.astype(BF)
        o_ref[:, pl.ds(ATTN_DIM, KV_DIM)] = dk.astype(BF)
        o_ref[:, pl.ds(ATTN_DIM + KV_DIM, KV_DIM)] = dvv.astype(BF)
        sq, sk, sv = (jnp.sum(t, axis=0, keepdims=True) for t in (dq, dk, dvv))

        @pl.when(i == 0)
        def _():
            db_ref[:, pl.ds(0, ATTN_DIM)] = sq
            db_ref[:, pl.ds(ATTN_DIM, KV_DIM)] = sk
            db_ref[:, pl.ds(ATTN_DIM + KV_DIM, KV_DIM)] = sv
            dqg_ref[...] = dqg
            dkg_ref[...] = dkg

        @pl.when(i > 0)
        def _():
            db_ref[:, pl.ds(0, ATTN_DIM)] += sq
            db_ref[:, pl.ds(ATTN_DIM, KV_DIM)] += sk
            db_ref[:, pl.ds(ATTN_DIM + KV_DIM, KV_DIM)] += sv
            dqg_ref[...] += dqg
            dkg_ref[...] += dkg

    full = lambda shape: pl.BlockSpec(shape, lambda i: (0, 0))
    row = lambda n: pl.BlockSpec((tm, n), lambda i: (i, 0))
    return pl.pallas_call(
        body, name="qk_norm_bwd", grid=(T // tm,),
        in_specs=[row(QKV_DIM), row(ATTN_DIM), row(KV_DIM), row(KV_DIM), full((1, ATTN_DIM)), full((1, KV_DIM)),
                  full((HEAD_GROUP, HEAD_GROUP))],
        out_specs=[row(QKV_DIM), full((1, QKV_DIM)), full((1, ATTN_DIM)), full((1, KV_DIM))],
        out_shape=[jax.ShapeDtypeStruct((T, QKV_DIM), BF), jax.ShapeDtypeStruct((1, QKV_DIM), F32),
                   jax.ShapeDtypeStruct((1, ATTN_DIM), F32), jax.ShapeDtypeStruct((1, KV_DIM), F32)],
        compiler_params=_params(1),
    )(qkv, dqn, dkn, dv, qg_t, kg_t, _head_ones())


ROWS = 128
COLS = 128


SUBLANES = 8
FIRST_TAP = HALO - (CONV_W - 1)


def _glu(a, g):
    return a.astype(F32) * jax.nn.sigmoid(g.astype(F32))


def _shifted(xe, s):
    return xe if s == 0 else pltpu.roll(xe, ROWS + HALO - s, axis=0)


def _conv_fwd(u, dw_pad, dw_b, ln_g, ln_b, tm=512):
    T = u.shape[0]
    Dm = D_MODEL
    hpt = tm // HALO

    def body(ac_ref, gc_ref, ap_ref, gp_ref, w_ref, wb_ref, lg_ref, lb_ref, cv_ref, s_ref, ext):
        i = pl.program_id(0)
        ext[pl.ds(0, HALO), :] = jnp.where(i > 0, _glu(ap_ref[...], gp_ref[...]), 0.0)
        ext[pl.ds(HALO, tm), :] = _glu(ac_ref[...], gc_ref[...])

        def rows(r, carry):
            r0 = pl.multiple_of(r * ROWS, ROWS)
            for c in range(Dm // COLS):
                cs = pl.ds(c * COLS, COLS)
                xe = ext[pl.ds(r0, ROWS + HALO), cs]
                acc = jnp.zeros((ROWS, COLS), F32)
                for s in range(SUBLANES):
                    xs = _shifted(xe, s)
                    for j in range(CONV_W):
                        off = FIRST_TAP + j
                        if off % SUBLANES == s:
                            acc = acc + xs[off - s:off - s + ROWS, :] * w_ref[pl.ds(j, 1), cs]
                cv_ref[pl.ds(r0, ROWS), cs] = acc + wb_ref[:, cs]
            return carry

        lax.fori_loop(0, tm // ROWS, rows, 0)
        cv = cv_ref[...]
        xc = cv - jnp.mean(cv, axis=-1, keepdims=True)
        y = xc * lax.rsqrt(jnp.mean(xc * xc, axis=-1, keepdims=True) + NORM_EPS) * lg_ref[...] + lb_ref[...]
        s_ref[...] = (y * jax.nn.sigmoid(y)).astype(BF)

    full = lambda shape: pl.BlockSpec(shape, lambda i: (0, 0))
    return pl.pallas_call(
        body, name="conv_fwd", grid=(T // tm,),
        in_specs=[pl.BlockSpec((tm, Dm), lambda i: (i, 0)), pl.BlockSpec((tm, Dm), lambda i: (i, 1)),
                  pl.BlockSpec((HALO, Dm), lambda i: (jnp.maximum(i * hpt - 1, 0), 0)),
                  pl.BlockSpec((HALO, Dm), lambda i: (jnp.maximum(i * hpt - 1, 0), 1)),
                  full((HALO, Dm)), full((1, Dm)), full((1, Dm)), full((1, Dm))],
        out_specs=[pl.BlockSpec((tm, Dm), lambda i: (i, 0)), pl.BlockSpec((tm, Dm), lambda i: (i, 0))],
        out_shape=[jax.ShapeDtypeStruct((T, Dm), F32), jax.ShapeDtypeStruct((T, Dm), BF)],
        scratch_shapes=[pltpu.VMEM((tm + HALO, Dm), F32)],
        compiler_params=_params(1),
    )(u, u, u, u, dw_pad, dw_b, ln_g, ln_b)


def _ln_silu_bwd_ep(ds, cv, lg, lb):
    xc = cv - jnp.mean(cv, axis=-1, keepdims=True)
    rstd = lax.rsqrt(jnp.mean(xc * xc, axis=-1, keepdims=True) + NORM_EPS)
    xh = xc * rstd
    y = xh * lg + lb
    sg = jax.nn.sigmoid(y)
    dy = ds * (sg * (1.0 + y * (1.0 - sg)))
    dxh = dy * lg
    dcv = rstd * (dxh - jnp.mean(dxh, axis=-1, keepdims=True) - xh * jnp.mean(dxh * xh, axis=-1, keepdims=True))
    return (dcv, jnp.sum(dy * xh, axis=0, keepdims=True), jnp.sum(dy, axis=0, keepdims=True),
            jnp.sum(dcv, axis=0, keepdims=True))


def _conv_bwd(u, dcv, dw_pad, tm=512):
    T = u.shape[0]
    Dm = D_MODEL
    hpt = tm // HALO
    last = T // HALO - 1
    nt = T // tm

    def body(ac_ref, gc_ref, ap_ref, gp_ref, dc_ref, dn_ref, w_ref, du_ref, db_ref, dw_ref, ext_g, ext_d):
        i = pl.program_id(0)
        ext_g[pl.ds(0, HALO), :] = jnp.where(i > 0, _glu(ap_ref[...], gp_ref[...]), 0.0)
        ext_g[pl.ds(HALO, tm), :] = _glu(ac_ref[...], gc_ref[...])
        ext_d[pl.ds(0, tm), :] = dc_ref[...]
        ext_d[pl.ds(tm, HALO), :] = jnp.where(i < nt - 1, dn_ref[...], 0.0)

        @pl.when(i == 0)
        def _():
            db_ref[...] = jnp.zeros_like(db_ref)
            dw_ref[...] = jnp.zeros_like(dw_ref)

        def rows(r, carry):
            r0 = pl.multiple_of(r * ROWS, ROWS)
            rs = pl.ds(r0, ROWS)
            for c in range(Dm // COLS):
                cs = pl.ds(c * COLS, COLS)
                cs2 = pl.ds(Dm + c * COLS, COLS)
                de = ext_d[pl.ds(r0, ROWS + HALO), cs]
                ge = ext_g[pl.ds(r0, ROWS + HALO), cs]
                dcur = de[0:ROWS, :]
                acc = jnp.zeros((ROWS, COLS), F32)
                for s in range(SUBLANES):
                    ds_, gs_ = _shifted(de, s), _shifted(ge, s)
                    for j in range(CONV_W):
                        off = CONV_W - 1 - j
                        if off % SUBLANES == s:
                            acc = acc + ds_[off - s:off - s + ROWS, :] * w_ref[pl.ds(j, 1), cs]
                        goff = FIRST_TAP + j
                        if goff % SUBLANES == s:
                            prod = dcur * gs_[goff - s:goff - s + ROWS, :]
                            dw_ref[j, :, cs] += jnp.sum(prod.reshape(ROWS // SUBLANES, SUBLANES, COLS), axis=0)
                a = ac_ref[rs, cs].astype(F32)
                sg = jax.nn.sigmoid(gc_ref[rs, cs].astype(F32))
                da = acc * sg
                dg = acc * a * sg * (1.0 - sg)
                du_ref[rs, cs] = da.astype(BF)
                du_ref[rs, cs2] = dg.astype(BF)
                db_ref[:, cs] += jnp.sum(da, axis=0, keepdims=True)
                db_ref[:, cs2] += jnp.sum(dg, axis=0, keepdims=True)
            return carry

        lax.fori_loop(0, tm // ROWS, rows, 0)

    return pl.pallas_call(
        body, name="conv_bwd", grid=(nt,),
        in_specs=[pl.BlockSpec((tm, Dm), lambda i: (i, 0)), pl.BlockSpec((tm, Dm), lambda i: (i, 1)),
                  pl.BlockSpec((HALO, Dm), lambda i: (jnp.maximum(i * hpt - 1, 0), 0)),
                  pl.BlockSpec((HALO, Dm), lambda i: (jnp.maximum(i * hpt - 1, 0), 1)),
                  pl.BlockSpec((tm, Dm), lambda i: (i, 0)),
                  pl.BlockSpec((HALO, Dm), lambda i: (jnp.minimum((i + 1) * hpt, last), 0)),
                  pl.BlockSpec((HALO, Dm), lambda i: (0, 0))],
        out_specs=[pl.BlockSpec((tm, 2 * Dm), lambda i: (i, 0)), pl.BlockSpec((1, 2 * Dm), lambda i: (0, 0)),
                   pl.BlockSpec((HALO, 8, Dm), lambda i: (0, 0, 0))],
        out_shape=[jax.ShapeDtypeStruct((T, 2 * Dm), BF), jax.ShapeDtypeStruct((1, 2 * Dm), F32),
                   jax.ShapeDtypeStruct((HALO, 8, Dm), F32)],
        scratch_shapes=[pltpu.VMEM((tm + HALO, Dm), F32), pltpu.VMEM((tm + HALO, Dm), F32)],
        compiler_params=_params(1),
    )(u, u, u, u, dcv, dcv, dw_pad)


def _bucket_table():
    q_loc = np.arange(BLOCK)[:, None]
    k_loc = np.arange(2 * BLOCK)[None, :]
    dist = q_loc + BLOCK - k_loc
    n = np.maximum(dist, 0)
    max_exact = REL_BUCKETS // 2
    large = max_exact + (np.log(np.maximum(n, 1).astype(np.float32) / max_exact)
                         / math.log(REL_MAX_DIST / max_exact) * (REL_BUCKETS - max_exact)).astype(np.int32)
    large = np.minimum(large, REL_BUCKETS - 1)
    bucket = np.where(n < max_exact, n, large).astype(np.int32)
    band = np.where((dist >= 0) & (dist < BLOCK), bucket, -1)
    folded = np.where(np.arange(BLOCK)[None, :] > q_loc, band[:, :BLOCK], band[:, BLOCK:])
    assert (folded >= 0).all() and ((band[:, :BLOCK] >= 0) != (band[:, BLOCK:] >= 0)).all()
    return jnp.asarray(folded.astype(np.int32))


def _prev_mask():
    row = lax.broadcasted_iota(jnp.int32, (BLOCK, BLOCK), 0)
    col = lax.broadcasted_iota(jnp.int32, (BLOCK, BLOCK), 1)
    return col > row


def _fold(band, prev_mask):
    return jnp.where(prev_mask, band[:, :BLOCK], band[:, BLOCK:])


def _unfold(ref, g, rows, folded, prev_mask):
    ref[g, rows, pl.ds(0, BLOCK)] = jnp.where(prev_mask, folded, 0.0).astype(ref.dtype)
    ref[g, rows, pl.ds(BLOCK, BLOCK)] = jnp.where(prev_mask, 0.0, folded).astype(ref.dtype)


def _bias_table(rel_bias_t, bucket):
    def body(rb_ref, bk_ref, o_ref):
        bk = bk_ref[...]
        prev_mask = _prev_mask()
        for h in range(N_HEADS):
            acc = jnp.zeros((BLOCK, BLOCK), F32)
            for b in range(REL_BUCKETS):
                acc = jnp.where(bk == b, rb_ref[h, b], acc)
            o_ref[0, h] = acc
            o_ref[1, h] = jnp.where(prev_mask, NEG_INF, acc)

    return pl.pallas_call(
        body, name="bias_table", out_shape=jax.ShapeDtypeStruct((2, N_HEADS, BLOCK, BLOCK), F32),
        in_specs=[pl.BlockSpec(memory_space=pltpu.SMEM), pl.BlockSpec(memory_space=pltpu.VMEM)],
        out_specs=pl.BlockSpec(memory_space=pltpu.VMEM),
    )(rel_bias_t, bucket)


def _bias_grad(dbias, bucket):
    def body(db_ref, bk_ref, o_ref):
        bk = bk_ref[...]
        for b in range(REL_BUCKETS):
            sel = bk == b
            for h in range(N_HEADS):
                o_ref[h, b] = jnp.sum(jnp.where(sel, db_ref[h], 0.0))

    return pl.pallas_call(
        body, name="bias_grad", out_shape=jax.ShapeDtypeStruct((N_HEADS, REL_BUCKETS), F32),
        in_specs=[pl.BlockSpec(memory_space=pltpu.VMEM), pl.BlockSpec(memory_space=pltpu.VMEM)],
        out_specs=pl.BlockSpec(memory_space=pltpu.SMEM),
    )(dbias, bucket)


GROUP_ROWS = GROUP * BLOCK
BIAS_SPEC = pl.BlockSpec((2, N_HEADS, BLOCK, BLOCK), lambda n: (0, 0, 0, 0))


def _head_probs(qk, bias_h, sink, prev_mask):
    s = _fold(qk, prev_mask) + bias_h
    m = jnp.maximum(jnp.max(s, axis=-1, keepdims=True), sink)
    p = jnp.exp(s - m)
    ps = jnp.exp(sink - m)
    inv = 1.0 / (jnp.sum(p, axis=-1, keepdims=True) + ps)
    return p * inv, ps * inv


def _band(prev_ref, cur_ref, g):
    hs = pl.ds(g * HEAD_DIM, HEAD_DIM)
    return jnp.concatenate([prev_ref[:, hs], cur_ref[:, hs]], axis=0)


def _stack_heads(ref, g):
    return jnp.concatenate([ref[:, pl.ds((g * GROUP + hh) * HEAD_DIM, HEAD_DIM)] for hh in range(GROUP)], axis=0)


def _unstack_heads(ref, g, stacked, dtype):
    for hh in range(GROUP):
        ref[:, pl.ds((g * GROUP + hh) * HEAD_DIM, HEAD_DIM)] = stacked[hh * BLOCK:(hh + 1) * BLOCK, :].astype(dtype)


def _head_rows(hh):
    return pl.ds(hh * BLOCK, BLOCK)


def _attn_fwd(qn, kn, vv, bias, sinks):
    T = qn.shape[0]
    nb = T // BLOCK

    def body(sk_ref, q_ref, kc_ref, kp_ref, vc_ref, vp_ref, b_ref, o_ref, qk_buf, p_buf):
        table = (pl.program_id(0) == 0).astype(jnp.int32)
        prev_mask = _prev_mask()
        for g in range(N_KV):
            qk_buf[g] = _dot(_stack_heads(q_ref, g), _band(kp_ref, kc_ref, g), 1, 1)
        for g in range(N_KV):
            for hh in range(GROUP):
                h = g * GROUP + hh
                pn, _ = _head_probs(qk_buf[g, _head_rows(hh), :], b_ref[table, h], sk_ref[h], prev_mask)
                _unfold(p_buf, g, _head_rows(hh), pn, prev_mask)
        for g in range(N_KV):
            _unstack_heads(o_ref, g, _dot(p_buf[g], _band(vp_ref, vc_ref, g), 1, 0), BF)

    cur = lambda n: (n, 0)
    prev = lambda n: (jnp.maximum(n - 1, 0), 0)
    return pl.pallas_call(
        body, name="attn_fwd", grid=(nb,),
        in_specs=[pl.BlockSpec(memory_space=pltpu.SMEM), pl.BlockSpec((BLOCK, ATTN_DIM), cur),
                  pl.BlockSpec((BLOCK, KV_DIM), cur), pl.BlockSpec((BLOCK, KV_DIM), prev),
                  pl.BlockSpec((BLOCK, KV_DIM), cur), pl.BlockSpec((BLOCK, KV_DIM), prev), BIAS_SPEC],
        out_specs=pl.BlockSpec((BLOCK, ATTN_DIM), cur), out_shape=jax.ShapeDtypeStruct((T, ATTN_DIM), BF),
        scratch_shapes=[pltpu.VMEM((N_KV, GROUP_ROWS, 2 * BLOCK), F32), pltpu.VMEM((N_KV, GROUP_ROWS, 2 * BLOCK), BF)],
        compiler_params=_params(1),
    )(sinks, qn, kn, kn, vv, vv, bias)


def _attn_bwd(qn, kn, vv, bias, sinks, do):
    T = qn.shape[0]
    nb = T // BLOCK
    scale = 1.0 / math.sqrt(HEAD_DIM)

    def body(sk_ref, q_ref, kc_ref, kp_ref, vc_ref, vp_ref, b_ref, do_ref,
             dq_ref, dk_ref, dv_ref, db_ref, dsk_ref, dk_full, dv_full, dk_carry, dv_carry, qk_buf, dp_buf, p_buf, ds_buf):
        n = pl.program_id(0)

        @pl.when(n == 0)
        def _():
            db_ref[...] = jnp.zeros_like(db_ref)
            dk_carry[...] = jnp.zeros_like(dk_carry)
            dv_carry[...] = jnp.zeros_like(dv_carry)
            for h in range(N_HEADS):
                dsk_ref[h] = 0.0

        @pl.when(n < nb)
        def _():
            table = (n == 0).astype(jnp.int32)
            prev_mask = _prev_mask()
            ks = [_band(kp_ref, kc_ref, g) for g in range(N_KV)]
            qs = [_stack_heads(q_ref, g) for g in range(N_KV)]
            douts = [_stack_heads(do_ref, g) for g in range(N_KV)]
            for g in range(N_KV):
                qk_buf[g] = _dot(qs[g], ks[g], 1, 1)
                dp_buf[g] = _dot(douts[g], _band(vp_ref, vc_ref, g), 1, 1)
            for g in range(N_KV):
                for hh in range(GROUP):
                    h = g * GROUP + hh
                    rows = _head_rows(hh)
                    pn, psink = _head_probs(qk_buf[g, rows, :], b_ref[table, h], sk_ref[h], prev_mask)
                    dp = _fold(dp_buf[g, rows, :], prev_mask)
                    delta = jnp.sum(pn * dp, axis=-1, keepdims=True)
                    ds = pn * (dp - delta)
                    dsk_ref[h] += -jnp.sum(psink * delta)
                    db_ref[h] += ds
                    _unfold(ds_buf, g, rows, ds, prev_mask)
                    _unfold(p_buf, g, rows, pn, prev_mask)
            for g in range(N_KV):
                dsb = ds_buf[g]
                _unstack_heads(dq_ref, g, _dot(dsb, ks[g], 1, 0) * scale, F32)
                gs = pl.ds(g * HEAD_DIM, HEAD_DIM)
                dk_full[:, gs] = _dot(dsb, qs[g], 0, 0)
                dv_full[:, gs] = _dot(p_buf[g], douts[g], 0, 0)

        @pl.when(n == nb)
        def _():
            dk_full[...] = jnp.zeros_like(dk_full)
            dv_full[...] = jnp.zeros_like(dv_full)

        dk_ref[...] = dk_carry[...] + dk_full[pl.ds(0, BLOCK), :]
        dv_ref[...] = dv_carry[...] + dv_full[pl.ds(0, BLOCK), :]
        dk_carry[...] = dk_full[pl.ds(BLOCK, BLOCK), :]
        dv_carry[...] = dv_full[pl.ds(BLOCK, BLOCK), :]

    cur = lambda n: (jnp.minimum(n, nb - 1), 0)
    prev = lambda n: (jnp.maximum(jnp.minimum(n, nb - 1) - 1, 0), 0)
    out_kv = lambda n: (jnp.maximum(n - 1, 0), 0)
    return pl.pallas_call(
        body, name="attn_bwd", grid=(nb + 1,),
        in_specs=[pl.BlockSpec(memory_space=pltpu.SMEM), pl.BlockSpec((BLOCK, ATTN_DIM), cur),
                  pl.BlockSpec((BLOCK, KV_DIM), cur), pl.BlockSpec((BLOCK, KV_DIM), prev),
                  pl.BlockSpec((BLOCK, KV_DIM), cur), pl.BlockSpec((BLOCK, KV_DIM), prev), BIAS_SPEC,
                  pl.BlockSpec((BLOCK, ATTN_DIM), cur)],
        out_specs=[pl.BlockSpec((BLOCK, ATTN_DIM), cur), pl.BlockSpec((BLOCK, KV_DIM), out_kv),
                   pl.BlockSpec((BLOCK, KV_DIM), out_kv),
                   pl.BlockSpec((N_HEADS, BLOCK, BLOCK), lambda n: (0, 0, 0)),
                   pl.BlockSpec(memory_space=pltpu.SMEM)],
        out_shape=[jax.ShapeDtypeStruct((T, ATTN_DIM), F32), jax.ShapeDtypeStruct((T, KV_DIM), F32),
                   jax.ShapeDtypeStruct((T, KV_DIM), F32),
                   jax.ShapeDtypeStruct((N_HEADS, BLOCK, BLOCK), F32), jax.ShapeDtypeStruct((N_HEADS,), F32)],
        scratch_shapes=[pltpu.VMEM((2 * BLOCK, KV_DIM), F32), pltpu.VMEM((2 * BLOCK, KV_DIM), F32),
                        pltpu.VMEM((BLOCK, KV_DIM), F32), pltpu.VMEM((BLOCK, KV_DIM), F32),
                        pltpu.VMEM((N_KV, GROUP_ROWS, 2 * BLOCK), F32), pltpu.VMEM((N_KV, GROUP_ROWS, 2 * BLOCK), F32),
                        pltpu.VMEM((N_KV, GROUP_ROWS, 2 * BLOCK), BF), pltpu.VMEM((N_KV, GROUP_ROWS, 2 * BLOCK), BF)],
        compiler_params=_params(1),
    )(sinks, qn, kn, kn, vv, vv, bias, do)


def _coords():
    return lax.axis_index("x"), lax.axis_index("y"), lax.axis_index("c")


def _sum8(name, blocks):
    def body(b_ref, o_ref):
        tot = b_ref[0]
        for d in range(1, 8):
            tot = tot + b_ref[d]
        o_ref[...] = tot

    return pl.pallas_call(body, name=name, out_shape=jax.ShapeDtypeStruct(blocks.shape[1:], F32))(blocks)


HBM_SPEC = pl.BlockSpec(memory_space=pltpu.HBM)
SEM_SPEC = pl.BlockSpec(memory_space=pltpu.SEMAPHORE)
ANY_SPEC = pl.BlockSpec(memory_space=pl.ANY)
DATAFLOW = pltpu.SideEffectType.DATAFLOW_SIDE_EFFECTING


OTHER_CHIPS = (4, 2, 6)
ALL_OTHERS = (1, 2, 3, 4, 5, 6, 7)


def _slot(x, y, c, peers):
    return 2 * x + y if peers is OTHER_CHIPS else 4 * x + 2 * y + c


def _slot_copy(land, sems, idx, x, y, c, k, peers, arriving):
    send_sems, recv_sems = sems
    px, py, pc = x ^ (k >> 2), y ^ ((k >> 1) & 1), c ^ (k & 1)
    mine = _slot(x, y, c, peers)
    dst = _slot(px, py, pc, peers) if arriving else mine
    return pltpu.make_async_remote_copy(src_ref=land.at[mine], dst_ref=land.at[dst], send_sem=send_sems.at[idx],
                                        recv_sem=recv_sems.at[idx], device_id=(px, py, pc), device_id_type=MESH)


def _gather_start(name, stacks, groups, peers, after):
    n = len(stacks)
    ng = len(groups)
    np_ = len(peers)
    after = tuple(after)

    def body(*refs):
        lands = refs[:n]
        first = n + len(after)
        sems = [(refs[first + 2 * g], refs[first + 2 * g + 1]) for g in range(ng)]
        token = refs[-1]
        x, y, c = _coords()
        for g, members in enumerate(groups):
            for i, t in enumerate(members):
                for j, k in enumerate(peers):
                    _slot_copy(lands[t], sems[g], np_ * i + j, x, y, c, k, peers, arriving=False).start()
        token[...] = jnp.zeros_like(token)

    out_shape = []
    for members in groups:
        out_shape += [pltpu.SemaphoreType.DMA((np_ * len(members),))] * 2
    out_shape += [pltpu.HBM(w.shape, w.dtype) for w in stacks]
    out_shape.append(jax.ShapeDtypeStruct((8, 128), F32))
    res = pl.pallas_call(
        body, name=name, out_shape=out_shape, in_specs=[HBM_SPEC] * n + [ANY_SPEC] * len(after),
        out_specs=[SEM_SPEC] * (2 * ng) + [HBM_SPEC] * n + [pl.BlockSpec(memory_space=pltpu.VMEM)],
        input_output_aliases={t: 2 * ng + t for t in range(n)},
        compiler_params=pltpu.CompilerParams(has_side_effects=DATAFLOW),
    )(*[pltpu.with_memory_space_constraint(w, pltpu.HBM) for w in stacks], *after)
    sems = [(res[2 * g], res[2 * g + 1]) for g in range(ng)]
    return sems, list(res[2 * ng:2 * ng + n]), res[-1]


def _gather_wait(name, stacks, sems, peers, after):
    n = len(stacks)
    after = tuple(after)

    def body(*refs):
        lands = refs[:n]
        group_sems = (refs[n], refs[n + 1])
        x, y, c = _coords()
        for i in range(n):
            for j, k in enumerate(peers):
                cp = _slot_copy(lands[i], group_sems, len(peers) * i + j, x, y, c, k, peers, arriving=True)
                cp.wait_send()
                cp.wait_recv()

    return pl.pallas_call(
        body, name=name, out_shape=[pltpu.HBM(w.shape, w.dtype) for w in stacks],
        in_specs=[HBM_SPEC] * n + [SEM_SPEC, SEM_SPEC] + [ANY_SPEC] * len(after), out_specs=[HBM_SPEC] * n,
        input_output_aliases={t: t for t in range(n)},
        compiler_params=pltpu.CompilerParams(has_side_effects=DATAFLOW),
    )(*stacks, sems[0], sems[1], *after)


N_PEERS = 7


def _peer(x, y, c, k):
    return x ^ (k >> 2), y ^ ((k >> 1) & 1), c ^ (k & 1)


def _reduce_copy(grad, land, sems, idx, x, y, c, k):
    px, py, pc = _peer(x, y, c, k)
    rh = grad.shape[1] // 2
    return pltpu.make_async_remote_copy(src_ref=grad.at[2 * px + py, pl.ds(pc * rh, rh), :], dst_ref=land.at[k - 1],
                                        send_sem=sems[0].at[idx], recv_sem=sems[1].at[idx], device_id=(px, py, pc),
                                        device_id_type=MESH)


def _reduce_start(name, grads):
    n = len(grads)

    def body(*refs):
        src, lands, sems, token = refs[:n], refs[n:2 * n], (refs[2 * n], refs[2 * n + 1]), refs[-1]
        x, y, c = _coords()
        for t in range(n):
            for k in range(1, N_PEERS + 1):
                _reduce_copy(src[t], lands[t], sems, N_PEERS * t + k - 1, x, y, c, k).start()
        token[...] = jnp.zeros_like(token)

    lands = [lax.empty((N_PEERS, g.shape[1] // 2, g.shape[2]), g.dtype) for g in grads]
    out_shape = [pltpu.SemaphoreType.DMA((N_PEERS * n,))] * 2
    out_shape += [pltpu.HBM(a.shape, a.dtype) for a in list(grads) + lands]
    out_shape.append(jax.ShapeDtypeStruct((8, 128), F32))
    res = pl.pallas_call(
        body, name=name, out_shape=out_shape, in_specs=[HBM_SPEC] * (2 * n),
        out_specs=[SEM_SPEC] * 2 + [HBM_SPEC] * (2 * n) + [pl.BlockSpec(memory_space=pltpu.VMEM)],
        input_output_aliases={t: 2 + t for t in range(2 * n)},
        compiler_params=pltpu.CompilerParams(has_side_effects=DATAFLOW),
    )(*[pltpu.with_memory_space_constraint(a, pltpu.HBM) for a in list(grads) + lands])
    return (res[0], res[1]), list(res[2:2 + n]), list(res[2 + n:2 + 2 * n]), res[-1]


def _reduce_wait(name, grads, lands, sems, after):
    n = len(grads)
    after = tuple(after)

    def body(*refs):
        src, dst, group_sems = refs[:n], refs[n:2 * n], (refs[2 * n], refs[2 * n + 1])
        x, y, c = _coords()
        for t in range(n):
            for k in range(1, N_PEERS + 1):
                cp = _reduce_copy(src[t], dst[t], group_sems, N_PEERS * t + k - 1, x, y, c, k)
                cp.wait_send()
                cp.wait_recv()

    res = pl.pallas_call(
        body, name=name, out_shape=[pltpu.HBM(a.shape, a.dtype) for a in list(grads) + list(lands)],
        in_specs=[HBM_SPEC] * (2 * n) + [SEM_SPEC, SEM_SPEC] + [ANY_SPEC] * len(after), out_specs=[HBM_SPEC] * (2 * n),
        input_output_aliases={t: t for t in range(2 * n)},
        compiler_params=pltpu.CompilerParams(has_side_effects=DATAFLOW),
    )(*grads, *lands, sems[0], sems[1], *after)
    return list(res[:n]), list(res[n:])


def _join_copy(half, land, sems, idx, x, y, c):
    return pltpu.make_async_remote_copy(src_ref=half, dst_ref=land, send_sem=sems[0].at[idx], recv_sem=sems[1].at[idx],
                                        device_id=(x, y, 1 - c), device_id_type=MESH)


def _join_start(name, halves):
    n = len(halves)

    def body(*refs):
        src, lands, sems, token = refs[:n], refs[n:2 * n], (refs[2 * n], refs[2 * n + 1]), refs[-1]
        x, y, c = _coords()
        for t in range(n):
            _join_copy(src[t], lands[t], sems, t, x, y, c).start()
        token[...] = jnp.zeros_like(token)

    lands = [lax.empty(h.shape, h.dtype) for h in halves]
    out_shape = [pltpu.SemaphoreType.DMA((n,))] * 2
    out_shape += [pltpu.HBM(a.shape, a.dtype) for a in list(halves) + lands]
    out_shape.append(jax.ShapeDtypeStruct((8, 128), F32))
    res = pl.pallas_call(
        body, name=name, out_shape=out_shape, in_specs=[HBM_SPEC] * (2 * n),
        out_specs=[SEM_SPEC] * 2 + [HBM_SPEC] * (2 * n) + [pl.BlockSpec(memory_space=pltpu.VMEM)],
        input_output_aliases={t: 2 + t for t in range(2 * n)},
        compiler_params=pltpu.CompilerParams(has_side_effects=DATAFLOW),
    )(*[pltpu.with_memory_space_constraint(a, pltpu.HBM) for a in list(halves) + lands])
    return (res[0], res[1]), list(res[2:2 + n]), list(res[2 + n:2 + 2 * n]), res[-1]


def _join_wait(name, halves, lands, sems, after):
    n = len(halves)
    after = tuple(after)

    def body(*refs):
        src, dst, group_sems = refs[:n], refs[n:2 * n], (refs[2 * n], refs[2 * n + 1])
        x, y, c = _coords()
        for t in range(n):
            cp = _join_copy(src[t], dst[t], group_sems, t, x, y, c)
            cp.wait_send()
            cp.wait_recv()

    res = pl.pallas_call(
        body, name=name, out_shape=[pltpu.HBM(a.shape, a.dtype) for a in list(halves) + list(lands)],
        in_specs=[HBM_SPEC] * (2 * n) + [SEM_SPEC, SEM_SPEC] + [ANY_SPEC] * len(after), out_specs=[HBM_SPEC] * (2 * n),
        input_output_aliases={t: t for t in range(2 * n)},
        compiler_params=pltpu.CompilerParams(has_side_effects=DATAFLOW),
    )(*halves, *lands, sems[0], sems[1], *after)
    return list(res[:n]), list(res[n:])


def _join_halves(name, halves, deps=()):
    n = len(halves)

    def body(*refs):
        src, dst = refs[:n], refs[n + len(deps):2 * n + len(deps)]
        send_sems, recv_sems = refs[-2:]
        x, y, c = _coords()
        cps = []
        for t in range(n):
            cp = pltpu.make_async_remote_copy(src_ref=src[t], dst_ref=dst[t], send_sem=send_sems.at[t],
                                              recv_sem=recv_sems.at[t], device_id=(x, y, 1 - c), device_id_type=MESH)
            cp.start()
            cps.append(cp)
        for cp in cps:
            cp.wait()

    anyspec = pl.BlockSpec(memory_space=pl.ANY)
    return pl.pallas_call(
        body, name=name, out_shape=[jax.ShapeDtypeStruct(h.shape, h.dtype) for h in halves],
        in_specs=[anyspec] * (n + len(deps)), out_specs=[anyspec] * n,
        scratch_shapes=[pltpu.SemaphoreType.DMA((n,)), pltpu.SemaphoreType.DMA((n,))],
    )(*halves, *deps)


def _row_block(rows):
    for rb in (512, 256, 128, 64, 32, 16):
        if rows % rb == 0:
            return rb
    raise ValueError(rows)


def _sum_devices(name, grad, land, place):
    S, R, C = grad.shape
    rh = R // 2
    rb = _row_block(rh)
    nbh = rh // rb

    def body(place_ref, g_ref, l_ref, o_ref):
        tot = g_ref[...].astype(F32)
        for k in range(N_PEERS):
            tot = tot + l_ref[k].astype(F32)
        o_ref[...] = tot

    return pl.pallas_call(
        body, name=name,
        grid_spec=pltpu.PrefetchScalarGridSpec(
            num_scalar_prefetch=1, grid=(nbh,),
            in_specs=[pl.BlockSpec((None, rb, C), lambda r, place: (place[0], place[1] * nbh + r, 0)),
                      pl.BlockSpec((N_PEERS, rb, C), lambda r, place: (0, r, 0))],
            out_specs=pl.BlockSpec((rb, C), lambda r, place: (r, 0))),
        out_shape=jax.ShapeDtypeStruct((rh, C), F32), compiler_params=_params(1),
    )(place, grad, land)


def _adamw_math(w, g, m, v):
    m2 = ADAM_B1 * m + (1.0 - ADAM_B1) * g
    v2 = ADAM_B2 * v + (1.0 - ADAM_B2) * (g * g)
    m_hat = m2 / (1.0 - ADAM_B1 ** ADAM_STEP)
    v_hat = v2 / (1.0 - ADAM_B2 ** ADAM_STEP)
    delta = -ADAM_LR * (m_hat / (jnp.sqrt(v_hat) + ADAM_EPS) + ADAM_WD * w)
    return delta, m2, v2


def _adamw(name, w, m, v, gs):
    L, R, C = w.shape
    Rh = R // 2
    rb = _row_block(Rh)
    nbh = Rh // rb
    assert len(gs) == L

    def body(core_ref, w_ref, m_ref, v_ref, *rest):
        g_refs, (go_ref, d_ref, m2_ref, v2_ref) = rest[:2 * L], rest[2 * L:]
        layer, half = pl.program_id(0), pl.program_id(1)
        mine = half == core_ref[0]
        g = jnp.where(mine, g_refs[0][...], g_refs[1][...])
        for t in range(1, L):
            g = jnp.where(layer == t, jnp.where(mine, g_refs[2 * t][...], g_refs[2 * t + 1][...]), g)
        delta, m2, v2 = _adamw_math(w_ref[...], g, m_ref[...], v_ref[...])
        go_ref[...] = g
        d_ref[...] = delta
        m2_ref[...] = m2
        v2_ref[...] = v2

    wspec = pl.BlockSpec((None, rb, C), lambda l, h, r, core: (l, h * nbh + r, 0))
    gspec = pl.BlockSpec((rb, C), lambda l, h, r, core: (r, 0))
    return pl.pallas_call(
        body, name=name,
        grid_spec=pltpu.PrefetchScalarGridSpec(num_scalar_prefetch=1, grid=(L, 2, nbh),
                                               in_specs=[wspec] * 3 + [gspec] * (2 * L), out_specs=[wspec] * 4),
        out_shape=[jax.ShapeDtypeStruct((L, R, C), F32)] * 4, compiler_params=_params(3),
    )(lax.axis_index("c").astype(jnp.int32).reshape(1), w, m, v, *[g for pair in gs for g in pair])


def _adamw_small(ws, gs, ms, vs):
    n = len(ws)

    def body(*refs):
        w_refs, g_refs, m_refs, v_refs = (refs[k * n:(k + 1) * n] for k in range(4))
        d_refs, m2_refs, v2_refs = (refs[(4 + k) * n:(5 + k) * n] for k in range(3))
        for t in range(n):
            delta, m2, v2 = _adamw_math(w_refs[t][...], g_refs[t][...], m_refs[t][...], v_refs[t][...])
            d_refs[t][...] = delta
            m2_refs[t][...] = m2
            v2_refs[t][...] = v2

    res = pl.pallas_call(body, name="adamw_small", out_shape=[jax.ShapeDtypeStruct(w.shape, F32) for w in ws] * 3)(
        *ws, *gs, *ms, *vs)
    return res[:n], res[n:2 * n], res[2 * n:]


def _packed_rows(shape):
    c = shape[-1]
    return (int(np.prod(shape)) // c) * -(-c // LANES)


def _pack(arrays):
    total = sum(_packed_rows(a.shape) for a in arrays)
    total += -total % 8
    buf, r0 = None, 0
    for a in arrays:
        a = a.astype(F32).reshape(-1, a.shape[-1])
        r, c = a.shape
        k = -(-c // LANES)
        a = jnp.pad(a, ((0, 0), (0, k * LANES - c))).reshape(r * k, LANES)
        a = jnp.pad(a, ((r0, total - r0 - r * k), (0, 0)))
        buf = a if buf is None else buf + a
        r0 += r * k
    return buf


def _unpack(buf, shapes):
    out, r0 = [], 0
    for shp in shapes:
        c = shp[-1]
        rows = _packed_rows(shp)
        out.append(buf[r0:r0 + rows].reshape(-1, -(-c // LANES) * LANES)[:, :c].reshape(shp))
        r0 += rows
    return out


def _rms(x, g):
    return x * lax.rsqrt(jnp.mean(x * x, axis=-1, keepdims=True) + NORM_EPS) * g


def _residual_norm_ep(acc, *rest):
    *bias, res, gain = rest
    x = acc + res + (bias[0] if bias else 0.0)
    return x, _rms(x, gain)


RESIDUAL_NORM_OUTS = (("tile", F32), ("tile", BF))


def _mlp_up(tag, h, w_up_sm):
    (up,) = _mm(f"mlp{tag}_up", h, w_up_sm, nt=False, b_sm=True, tm=2048, tn=1024, rows=256,
                ep_fn=lambda acc: (acc,), outs=(("tile", BF),))
    return up


RMS_BWD_OUTS = (("tile", F32), ("tile", BF), ("colsum", F32), ("colsum", F32))


def _mlp_bwd(tag, dy, dy_bf, x, g, up, w_up_sm, w_down):
    (dup,) = _mm(f"mlp{tag}_dup", dy_bf, w_down, nt=True, tm=2048, tn=1024, rows=256, ep_in=((up, "tile"),),
                 ep_fn=lambda acc, u: (acc * (2.0 * jnp.maximum(u.astype(F32), 0.0)),), outs=(("tile", BF),))
    dx, dx_bf, dg, dx_sum = _mm(f"mlp{tag}_dx", dup, w_up_sm, nt=True, b_sm=True, tm=512, tn=1024, rows=256,
                                ep_in=((x, "tile"), (g, "row"), (dy, "tile")), ep_fn=_rms_bwd_ep, outs=RMS_BWD_OUTS)
    return dx, dx_bf, dg, dx_sum, dup


class _Reduction:
    def __init__(self, tag, grads, place):
        self.tag, self.place = tag, place
        self.sems, self.grads, self.lands, self.token = _reduce_start(f"reduce_start_{tag}", grads)

    def finish(self, after):
        grads, lands = _reduce_wait(f"reduce_wait_{self.tag}", self.grads, self.lands, self.sems, after)
        return [_sum_devices(f"reduce_sum_{self.tag}{i}", g, l, self.place) for i, (g, l) in enumerate(zip(grads, lands))]


def kernel(x, conv_norm_g, conv_w_in, conv_b_in, conv_dw, conv_dw_b, conv_ln_g, conv_ln_b, conv_w_out, conv_b_out, attn_norm_g, w_qkv, b_qkv, q_norm_g, k_norm_g, sinks, w_o, b_o, rel_bias, mlp_norm_g, w_up, w_down, loss_target, m_conv_norm_g, m_conv_w_in, m_conv_b_in, m_conv_dw, m_conv_dw_b, m_conv_ln_g, m_conv_ln_b, m_conv_w_out, m_conv_b_out, m_attn_norm_g, m_w_qkv, m_b_qkv, m_q_norm_g, m_k_norm_g, m_sinks, m_w_o, m_b_o, m_rel_bias, m_mlp_norm_g, m_w_up, m_w_down, v_conv_norm_g, v_conv_w_in, v_conv_b_in, v_conv_dw, v_conv_dw_b, v_conv_ln_g, v_conv_ln_b, v_conv_w_out, v_conv_b_out, v_attn_norm_g, v_w_qkv, v_b_qkv, v_q_norm_g, v_k_norm_g, v_sinks, v_w_o, v_b_o, v_rel_bias, v_mlp_norm_g, v_w_up, v_w_down):
    Dm = D_MODEL
    x2d = x[0]
    tgt = loss_target[0]
    T = x2d.shape[0]
    shard = 2 * lax.axis_index("x") + lax.axis_index("y")

    me = 2 * shard + lax.axis_index("c")

    def own_slot(block, slots, index):
        return lax.dynamic_update_slice(lax.empty((slots,) + block.shape, block.dtype), block[None],
                                        (index,) + (0,) * block.ndim)

    (conv_in_sems,), (stack_in,), first_token = _gather_start(
        "gather_start_conv_in", [own_slot(conv_w_in[0].astype(BF), N_SHARD, shard)], ((0,),), OTHER_CHIPS, after=())
    sharded_small = [conv_dw[0], attn_norm_g, b_qkv, b_o]
    (small_sems,), (small_land,), small_token = _gather_start(
        "small_weights_start", [own_slot(_pack(sharded_small), 8, me)], ((0,),), ALL_OTHERS, after=(first_token,))

    big = [conv_w_out[0], jnp.swapaxes(w_qkv, 1, 2)[0], w_o[0], w_up[0], w_up[1], w_down[0], w_down[1]]
    stacks = [own_slot(w.astype(BF), N_SHARD, shard) for w in big]
    groups = ((0,), (3, 5), (1, 2), (4, 6))
    gather_sems, stacks, gather_token = _gather_start("gather_start", stacks, groups, OTHER_CHIPS, after=(small_token,))

    def gathered_group(g, name, after):
        return _gather_wait(name, [stacks[t] for t in groups[g]], gather_sems[g], OTHER_CHIPS, after)

    bucket = _bucket_table()
    bias = _bias_table(rel_bias.T, bucket)

    h0 = _rms_fwd("conv_norm", x2d, conv_norm_g, deps=(gather_token,))
    (w_in_sm,) = _gather_wait("gather_wait_conv_in", [stack_in], conv_in_sems, OTHER_CHIPS, (h0, bias))
    (u,) = _mm("conv_in", h0, w_in_sm, nt=False, b_sm=True, tm=2048, tn=512, rows=256, ep_in=((conv_b_in, "row"),),
               ep_fn=lambda acc, b: (acc + b,), outs=(("tile", BF),))
    (gathered,) = _gather_wait("small_weights_wait", [small_land], small_sems, ALL_OTHERS, (u,))
    chips = [_unpack(gathered[2 * s], [a.shape for a in sharded_small]) for s in range(N_SHARD)]
    dw_f, attn_norm_f, b_qkv_f, b_o_f = (jnp.concatenate([chips[s][t] for s in range(N_SHARD)], axis=-1)
                                         for t in range(len(sharded_small)))
    dw_pad = jnp.pad(dw_f, ((0, HALO - CONV_W), (0, 0)))
    cv, s_act = _conv_fwd(u, dw_pad, conv_dw_b, conv_ln_g, conv_ln_b)
    (g_out,) = gathered_group(0, "gather_wait_conv_out", (s_act,))
    w_out_f = g_out.reshape(Dm, Dm)
    x1, h1 = _mm("conv_out", s_act, w_out_f, nt=False, tm=1024, tn=1024, rows=256,
                 ep_in=((conv_b_out, "row"), (x2d, "tile"), (mlp_norm_g[0:1], "row")), ep_fn=_residual_norm_ep,
                 outs=RESIDUAL_NORM_OUTS)

    g_up0, g_down0 = gathered_group(1, "gather_wait_mlp0", (x1,))
    w_up_sm = [g_up0, None]
    w_down_f = [g_down0.reshape(D_FF, Dm), None]
    up0 = _mlp_up(0, h1, w_up_sm[0])
    x2, h2 = _mm("mlp0_down", up0, w_down_f[0], nt=False, tm=512, tn=1024, rows=256, a_fn=_relu2,
                 ep_in=((x1, "tile"), (attn_norm_f, "row")), ep_fn=_residual_norm_ep, outs=RESIDUAL_NORM_OUTS)

    g_qkv, g_o = gathered_group(2, "gather_wait_attn", (x2,))
    w_qkv_t = g_qkv.reshape(QKV_DIM, Dm)
    w_o_f = g_o.reshape(ATTN_DIM, Dm)
    qg_t = jnp.tile(q_norm_g, (1, N_HEADS))
    kg_t = jnp.tile(k_norm_g, (1, N_KV))

    def qkv_ep(acc, b, qg, kg, ones):
        proj = acc + b
        q, k, v = proj[:, :ATTN_DIM], proj[:, ATTN_DIM:ATTN_DIM + KV_DIM], proj[:, ATTN_DIM + KV_DIM:]
        return proj, _qk_normed(q, qg, ones, 1.0 / math.sqrt(HEAD_DIM)), _qk_normed(k, kg, ones, 1.0), v

    qkv, qn, kn, vv = _mm(
        "attn_qkv", h2, w_qkv_t, nt=True, tm=1024, tn=QKV_DIM, rows=256, ep_fn=qkv_ep,
        ep_in=((b_qkv_f, "row"), (qg_t, "whole"), (kg_t, "whole"), (_head_ones(), "whole")),
        outs=(("tile", F32), ("tile", BF, ATTN_DIM), ("tile", BF, KV_DIM), ("tile", BF, KV_DIM)))
    sinks1 = sinks[0]
    att = _attn_fwd(qn, kn, vv, bias, sinks1)
    x3, h3 = _mm("attn_out", att, w_o_f, nt=False, tm=1024, tn=1024, rows=256,
                 ep_in=((b_o_f, "row"), (x2, "tile"), (mlp_norm_g[1:2], "row")), ep_fn=_residual_norm_ep,
                 outs=RESIDUAL_NORM_OUTS)

    g_up1, g_down1 = gathered_group(3, "gather_wait_mlp1", (x3,))
    w_up_sm[1] = g_up1
    w_down_f[1] = g_down1.reshape(D_FF, Dm)
    up1 = _mlp_up(1, h3, w_up_sm[1])

    def loss_ep(acc, r, t):
        diff = acc + r - t
        dy = diff * (1.0 / Dm)
        return dy, dy, jnp.sum(diff * diff, axis=0, keepdims=True)

    dy, dy_bf, sq = _mm("mlp1_down_loss", up1, w_down_f[1], nt=False, tm=512, tn=1024, rows=256, a_fn=_relu2,
                        ep_in=((x3, "tile"), (tgt, "tile")), ep_fn=loss_ep,
                        outs=(("tile", F32), ("tile", BF), ("colsum", F32)))

    place = jnp.stack([shard, lax.axis_index("c")]).astype(jnp.int32)
    dx3, dx3_bf, dg_mlp1, db_o, dup1 = _mlp_bwd(1, dy, dy_bf, x3, mlp_norm_g[1:2], up1, w_up_sm[1], w_down_f[1])
    dw_down1 = _mm_tn("mlp1_dw_down", up1, dy_bf, tm=1024, tn=1024, tk=2048, a_fn=_relu2)
    dw_up1 = _mm_tn("mlp1_dw_up", h3, dup1, tm=1024, tn=1024, tk=2048, out_sm=N_SHARD)
    red_mlp1 = _Reduction("mlp1", [dw_up1, dw_down1.reshape(N_SHARD, D_FF // N_SHARD, Dm)], place)

    ident = lambda acc: (acc,)
    (datt,) = _mm("attn_dout", dx3_bf, w_o_f, nt=True, tm=1024, tn=1024, rows=256, ep_fn=ident, outs=(("tile", BF),),
                  deps=(red_mlp1.token,))
    dw_o = _mm_tn("attn_dw_o", att, dx3_bf, tm=1024, tn=1024, tk=2048)
    dqn, dkn, dvv, dbias, dsinks = _attn_bwd(qn, kn, vv, bias, sinks1, datt)
    drel = _bias_grad(dbias, bucket)
    dqkv, db_qkv, dqg_t, dkg_t = _qk_norm_bwd(qkv, dqn, dkn, dvv, qg_t, kg_t)
    dw_qkv_t = _mm_tn("attn_dw_qkv", dqkv, h2, tm=QKV_DIM, tn=1024, tk=2048)
    red_attn = _Reduction("attn", [dw_qkv_t.reshape(N_SHARD, QKV_DIM // N_SHARD, Dm),
                                   dw_o.reshape(N_SHARD, ATTN_DIM // N_SHARD, Dm)], place)
    dx2, dx2_bf, dg_attn, _ = _mm("attn_dx", dqkv, w_qkv_t, nt=False, tm=1024, tn=1024, rows=256,
                                  ep_in=((x2, "tile"), (attn_norm_f, "row"), (dx3, "tile")), ep_fn=_rms_bwd_ep,
                                  outs=RMS_BWD_OUTS, deps=(red_attn.token,))

    dx1, dx1_bf, dg_mlp0, db_out, dup0 = _mlp_bwd(0, dx2, dx2_bf, x1, mlp_norm_g[0:1], up0, w_up_sm[0], w_down_f[0])
    dw_down0 = _mm_tn("mlp0_dw_down", up0, dx2_bf, tm=1024, tn=1024, tk=2048, a_fn=_relu2)
    dw_up0 = _mm_tn("mlp0_dw_up", h1, dup0, tm=1024, tn=1024, tk=2048, out_sm=N_SHARD)
    dw_out = _mm_tn("conv_dw_out", s_act, dx1_bf, tm=1024, tn=1024, tk=2048)
    red_mlp0 = _Reduction("mlp0", [dw_up0, dw_down0.reshape(N_SHARD, D_FF // N_SHARD, Dm),
                                   dw_out.reshape(N_SHARD, Dm // N_SHARD, Dm)], place)
    (r_qkv, r_o) = red_attn.finish((dx1,))
    (r_up1, r_down1) = red_mlp1.finish((dx1,))

    dcv, dln_g, dln_b, ddw_b = _mm("conv_ds", dx1_bf, w_out_f, nt=True, tm=1024, tn=1024, rows=256,
                                   ep_in=((cv, "tile"), (conv_ln_g, "row"), (conv_ln_b, "row")),
                                   ep_fn=_ln_silu_bwd_ep,
                                   outs=(("tile", F32), ("colsum", F32), ("colsum", F32), ("colsum", F32)),
                                   deps=(red_mlp0.token,))
    du, db_in, ddw8 = _conv_bwd(u, dcv, dw_pad)
    (r_up0, r_down0, r_out) = red_mlp0.finish((du,))
    early = [r_out, r_qkv, r_o, r_up0, r_up1, r_down0, r_down1]
    join_sems, early, early_lands, join_token = _join_start("join_start", early)
    dw_in = _mm_tn("conv_dw_in", h0, du, tm=1024, tn=512, tk=4096, out_sm=N_SHARD)
    red_conv = _Reduction("conv", [dw_in], place)
    def first_layer_ep(*args):
        tot, _, dg, _ = _rms_bwd_ep(*args)
        return tot, dg

    gx, dg_conv = _mm("conv_dx", du, w_in_sm, nt=True, b_sm=True, tm=1024, tn=1024, rows=256,
                      ep_in=((x2d, "tile"), (conv_norm_g, "row"), (dx1, "tile")), ep_fn=first_layer_ep,
                      outs=(("tile", F32), ("colsum", F32)), deps=(red_conv.token, join_token))
    (r_in,) = red_conv.finish((gx,))

    dqg = dqg_t.reshape(N_HEADS, HEAD_DIM).sum(axis=0, keepdims=True)
    dkg = dkg_t.reshape(N_KV, HEAD_DIM).sum(axis=0, keepdims=True)
    small_full = [dg_conv, db_in, ddw8.sum(axis=1)[:CONV_W], ddw_b, dln_g, dln_b, db_out, dg_attn, db_qkv, dqg, dkg,
                  dsinks[None, :], db_o, drel.reshape(1, REL_BUCKETS * N_HEADS),
                  jnp.pad(dg_mlp0, ((0, 1), (0, 0))) + jnp.pad(dg_mlp1, ((1, 0), (0, 0))), sq]
    (sg_sems,), (sg_land,), sg_token = _gather_start(
        "small_grads_start", [own_slot(_pack(small_full), 8, me)], ((0,),), ALL_OTHERS, after=())

    early, early_sibling = _join_wait("join_wait", early, early_lands, join_sems, (gx, sg_token))
    r_out, r_qkv, r_o, r_up0, r_up1, r_down0, r_down1 = zip(early, early_sibling)
    r_in = (r_in,) + tuple(_join_halves("join_halves", [r_in], deps=(sg_token,)))

    big_out = {}
    qkv_t = [jnp.swapaxes(a, 1, 2) for a in (w_qkv, m_w_qkv, v_w_qkv)]
    for nm, w, m, v, gs in (("conv_w_in", conv_w_in, m_conv_w_in, v_conv_w_in, (r_in,)),
                            ("conv_w_out", conv_w_out, m_conv_w_out, v_conv_w_out, (r_out,)),
                            ("w_qkv", *qkv_t, (r_qkv,)),
                            ("w_o", w_o, m_w_o, v_w_o, (r_o,)),
                            ("w_up", w_up, m_w_up, v_w_up, (r_up0, r_up1)),
                            ("w_down", w_down, m_w_down, v_w_down, (r_down0, r_down1))):
        big_out[nm] = _adamw(f"adamw_{nm}", w, m, v, gs)
    big_out["w_qkv"] = tuple(jnp.swapaxes(a, 1, 2) for a in big_out["w_qkv"])

    (sg_land,) = _gather_wait("small_grads_wait", [sg_land], sg_sems, ALL_OTHERS,
                              [big_out[nm][0] for nm in big_out])
    small_sum = _sum8("small_grads_sum", sg_land)
    (r_norm, r_b_in, r_dw, r_dw_b, r_ln_g, r_ln_b, r_b_out, r_attn_norm, r_b_qkv, r_qg, r_kg, r_sinks, r_b_o, r_rel,
     r_mlp_norm, r_sq) = _unpack(small_sum, [a.shape for a in small_full])
    loss = 0.5 * jnp.sum(r_sq) * (1.0 / Dm)

    def cols(a, width):
        return lax.dynamic_slice_in_dim(a, shard * width, width, axis=a.ndim - 1)

    small_names = ["conv_norm_g", "conv_b_in", "conv_dw", "conv_dw_b", "conv_ln_g", "conv_ln_b", "conv_b_out",
                   "attn_norm_g", "b_qkv", "q_norm_g", "k_norm_g", "sinks", "b_o", "rel_bias", "mlp_norm_g"]
    small_g = [r_norm, r_b_in, cols(r_dw, Dm // N_SHARD)[None], r_dw_b, r_ln_g, r_ln_b, r_b_out,
               cols(r_attn_norm, Dm // N_SHARD), cols(r_b_qkv, QKV_DIM // N_SHARD), r_qg, r_kg, r_sinks,
               cols(r_b_o, Dm // N_SHARD), r_rel.reshape(N_HEADS, REL_BUCKETS), r_mlp_norm]
    small_w = [conv_norm_g, conv_b_in, conv_dw, conv_dw_b, conv_ln_g, conv_ln_b, conv_b_out, attn_norm_g, b_qkv,
               q_norm_g, k_norm_g, sinks, b_o, rel_bias.T, mlp_norm_g]
    small_m = [m_conv_norm_g, m_conv_b_in, m_conv_dw, m_conv_dw_b, m_conv_ln_g, m_conv_ln_b, m_conv_b_out,
               m_attn_norm_g, m_b_qkv, m_q_norm_g, m_k_norm_g, m_sinks, m_b_o, m_rel_bias.T, m_mlp_norm_g]
    small_v = [v_conv_norm_g, v_conv_b_in, v_conv_dw, v_conv_dw_b, v_conv_ln_g, v_conv_ln_b, v_conv_b_out,
               v_attn_norm_g, v_b_qkv, v_q_norm_g, v_k_norm_g, v_sinks, v_b_o, v_rel_bias.T, v_mlp_norm_g]
    flat2 = lambda a: a.reshape(-1, a.shape[-1])
    small_g = [flat2(g) for g in small_g]
    d_s, m_s, v_s = _adamw_small([flat2(w) for w in small_w], small_g, [flat2(m) for m in small_m],
                                 [flat2(v) for v in small_v])
    small_out = {}
    for nm, w, g, d, m2, v2 in zip(small_names, small_w, small_g, d_s, m_s, v_s):
        small_out[nm] = tuple(a.reshape(w.shape) for a in (g, d, m2, v2))
    small_out["rel_bias"] = tuple(a.T for a in small_out["rel_bias"])

    order = ["conv_norm_g", "conv_w_in", "conv_b_in", "conv_dw", "conv_dw_b", "conv_ln_g", "conv_ln_b", "conv_w_out",
             "conv_b_out", "attn_norm_g", "w_qkv", "b_qkv", "q_norm_g", "k_norm_g", "sinks", "w_o", "b_o", "rel_bias",
             "mlp_norm_g", "w_up", "w_down"]
    res = {**small_out, **big_out}
    outs = [loss, gx[None]]
    for part in range(4):
        outs += [res[nm][part] for nm in order]
    return tuple(outs)
```

```python
import math

import numpy as np
import jax
import jax.numpy as jnp
from jax import lax
from jax.experimental import pallas as pl
from jax.experimental.pallas import tpu as pltpu

F32 = jnp.float32
BF = jnp.bfloat16
MESH = pl.DeviceIdType.MESH

D_MODEL = 1024
D_FF = 4096
N_HEADS = 16
N_KV = 2
GROUP = N_HEADS // N_KV
HEAD_DIM = 64
ATTN_DIM = N_HEADS * HEAD_DIM
KV_DIM = N_KV * HEAD_DIM
QKV_DIM = ATTN_DIM + 2 * KV_DIM
BLOCK = 128
CONV_W = 31
HALO = 32
REL_BUCKETS = 32
REL_MAX_DIST = 128
NORM_EPS = 1e-6
NEG_INF = -1e30
N_SHARD = 4
LANES = 1024

ADAM_LR = 0.001
ADAM_B1 = 0.9
ADAM_B2 = 0.999
ADAM_EPS = 1e-08
ADAM_WD = 0.01
ADAM_STEP = 10

VMEM_LIMIT = 56 * 1024 * 1024


def _params(n_axes):
    return pltpu.CompilerParams(dimension_semantics=("arbitrary",) * n_axes, vmem_limit_bytes=VMEM_LIMIT)


def _dot(a, b, ca, cb):
    return lax.dot_general(a, b, (((ca,), (cb,)), ((), ())), preferred_element_type=F32)


def _mm(name, a, b, *, nt, tm, tn, ep_fn, outs, a_fn=None, b_sm=False, ep_in=(), deps=(), rows=None):
    M, K = a.shape
    rows = tm if rows is None else rows
    if b_sm:
        S, ks = b.shape[0], b.shape[2]
        N, per = (b.shape[1], None) if nt else (S * b.shape[2], b.shape[2] // tn)
        assert (S * ks == K) if nt else (b.shape[1] == K)
    else:
        N = b.shape[0] if nt else b.shape[1]
        assert (b.shape[1] if nt else b.shape[0]) == K
    assert M % tm == 0 and N % tn == 0 and tm % rows == 0
    ne, no, nd = len(ep_in), len(outs), len(deps)

    def body(a_ref, b_ref, *rest):
        ep_refs, out_refs = rest[:ne], rest[ne + nd:ne + nd + no]
        i = pl.program_id(1)
        sums = [None] * no
        for r in range(tm // rows):
            rs = pl.ds(r * rows, rows)

            def lhs(cols):
                av = a_ref[rs, cols]
                return (av if a_fn is None else a_fn(av)).astype(BF)

            if b_sm and nt:
                acc = None
                for s in range(S):
                    part = _dot(lhs(pl.ds(s * ks, ks)), b_ref[s].astype(BF), 1, 1)
                    acc = part if acc is None else acc + part
            else:
                acc = _dot(lhs(slice(None)), b_ref[...].astype(BF), 1, 1 if nt else 0)
            ep_vals = [ref[rs, :] if kind == "tile" else ref[...] for ref, (_, kind) in zip(ep_refs, ep_in)]
            vals = ep_fn(acc, *ep_vals)
            for o, ((kind, dt, *_), ref, val) in enumerate(zip(outs, out_refs, vals)):
                if kind == "tile":
                    ref[rs, :] = val.astype(dt)
                else:
                    sums[o] = val if sums[o] is None else sums[o] + val
        for (kind, *_), ref, val in zip(outs, out_refs, sums):
            if kind == "colsum":
                @pl.when(i == 0)
                def _():
                    ref[...] = val

                @pl.when(i > 0)
                def _():
                    ref[...] += val

    if b_sm and nt:
        b_spec = pl.BlockSpec((S, tn, ks), lambda j, i: (0, j, 0))
    elif b_sm:
        b_spec = pl.BlockSpec((None, K, tn), lambda j, i: (j // per, 0, j % per))
    elif nt:
        b_spec = pl.BlockSpec((tn, K), lambda j, i: (j, 0))
    else:
        b_spec = pl.BlockSpec((K, tn), lambda j, i: (0, j))
    in_specs = [pl.BlockSpec((tm, K), lambda j, i: (i, 0)), b_spec]
    for arr, kind in ep_in:
        if kind == "tile":
            assert arr.shape == (M, N)
            in_specs.append(pl.BlockSpec((tm, tn), lambda j, i: (i, j)))
        elif kind == "whole":
            in_specs.append(pl.BlockSpec(arr.shape, lambda j, i, rank=arr.ndim: (0,) * rank))
        else:
            assert arr.shape == (1, N)
            in_specs.append(pl.BlockSpec((1, tn), lambda j, i: (0, j)))
    in_specs += [pl.BlockSpec(memory_space=pl.ANY)] * nd
    out_shape, out_specs = [], []
    for kind, dt, *width in outs:
        if kind == "tile" and width:
            assert tn == N
            out_shape.append(jax.ShapeDtypeStruct((M, width[0]), dt))
            out_specs.append(pl.BlockSpec((tm, width[0]), lambda j, i: (i, 0)))
        elif kind == "tile":
            out_shape.append(jax.ShapeDtypeStruct((M, N), dt))
            out_specs.append(pl.BlockSpec((tm, tn), lambda j, i: (i, j)))
        else:
            out_shape.append(jax.ShapeDtypeStruct((1, N), F32))
            out_specs.append(pl.BlockSpec((1, tn), lambda j, i: (0, j)))
    return pl.pallas_call(
        body, name=name, grid=(N // tn, M // tm), in_specs=in_specs, out_specs=out_specs, out_shape=out_shape,
        compiler_params=_params(2),
    )(a, b, *[arr for arr, _ in ep_in], *deps)


def _mm_tn(name, a, b, *, tm, tn, tk, a_fn=None, out_sm=None):
    T, Ka = a.shape
    N = b.shape[1]
    assert b.shape[0] == T and T % tk == 0 and Ka % tm == 0 and N % tn == 0
    nk = T // tk

    def body(a_ref, b_ref, o_ref, acc_ref):
        k = pl.program_id(2)

        @pl.when(k == 0)
        def _():
            acc_ref[...] = jnp.zeros_like(acc_ref)

        av = a_ref[...]
        if a_fn is not None:
            av = a_fn(av)
        acc_ref[...] += _dot(av.astype(BF), b_ref[...].astype(BF), 0, 0)

        @pl.when(k == nk - 1)
        def _():
            o_ref[...] = acc_ref[...].astype(BF)

    if out_sm is None:
        out_shape = jax.ShapeDtypeStruct((Ka, N), BF)
        out_spec = pl.BlockSpec((tm, tn), lambda i, j, k: (i, j))
    else:
        per = (N // out_sm) // tn
        assert per * tn * out_sm == N
        out_shape = jax.ShapeDtypeStruct((out_sm, Ka, N // out_sm), BF)
        out_spec = pl.BlockSpec((None, tm, tn), lambda i, j, k: (j // per, i, j % per))
    return pl.pallas_call(
        body, name=name, grid=(Ka // tm, N // tn, nk),
        in_specs=[pl.BlockSpec((tk, tm), lambda i, j, k: (k, i)), pl.BlockSpec((tk, tn), lambda i, j, k: (k, j))],
        out_specs=out_spec, out_shape=out_shape, scratch_shapes=[pltpu.VMEM((tm, tn), F32)],
        compiler_params=_params(3),
    )(a, b)


def _relu2(v):
    r = jnp.maximum(v.astype(F32), 0.0)
    return r * r


def _rms_bwd_ep(dh, x, g, dres):
    rstd = lax.rsqrt(jnp.mean(x * x, axis=-1, keepdims=True) + NORM_EPS)
    xh = x * rstd
    dxh = dh * g
    dx = rstd * (dxh - xh * jnp.mean(dxh * xh, axis=-1, keepdims=True))
    tot = dres + dx
    return tot, tot, jnp.sum(dh * xh, axis=0, keepdims=True), jnp.sum(tot, axis=0, keepdims=True)


def _rms_fwd(name, x, g, tm=512, deps=()):
    T, Dm = x.shape

    def body(x_ref, g_ref, *rest):
        o_ref = rest[-1]
        xv = x_ref[...]
        rstd = lax.rsqrt(jnp.mean(xv * xv, axis=-1, keepdims=True) + NORM_EPS)
        o_ref[...] = (xv * rstd * g_ref[...]).astype(BF)

    return pl.pallas_call(
        body, name=name, grid=(T // tm,),
        in_specs=[pl.BlockSpec((tm, Dm), lambda i: (i, 0)), pl.BlockSpec((1, Dm), lambda i: (0, 0))]
        + [pl.BlockSpec(memory_space=pl.ANY)] * len(deps),
        out_specs=pl.BlockSpec((tm, Dm), lambda i: (i, 0)), out_shape=jax.ShapeDtypeStruct((T, Dm), BF),
        compiler_params=_params(1),
    )(x, g, *deps)


HEAD_GROUP = 256


def _head_sum(v, ones):
    n = v.shape[1]
    w = min(n, HEAD_GROUP)
    blk = ones[:w, :w]
    parts = [_dot(v[:, c:c + w].astype(BF), blk, 1, 0) for c in range(0, n, w)]
    return parts[0] if len(parts) == 1 else jnp.concatenate(parts, axis=1)


def _head_ones():
    idx = np.arange(HEAD_GROUP) // HEAD_DIM
    return jnp.asarray((idx[:, None] == idx[None, :]).astype(np.float32), dtype=BF)


def _qk_normed(x, g, ones, scale):
    r = lax.rsqrt(_head_sum(x * x, ones) * (1.0 / HEAD_DIM) + NORM_EPS)
    return x * r * g * scale


def _qk_norm_bwd(qkv, dqn, dkn, dv, qg_t, kg_t, tm=256):
    T = qkv.shape[0]

    def body(x_ref, dq_ref, dk_ref, dv_ref, qg_ref, kg_ref, ones_ref, o_ref, db_ref, dqg_ref, dkg_ref):
        i = pl.program_id(0)
        ones = ones_ref[...]

        def one(x, dy, g):
            r = lax.rsqrt(_head_sum(x * x, ones) * (1.0 / HEAD_DIM) + NORM_EPS)
            xh = x * r
            dxh = dy * g
            dx = r * (dxh - xh * (_head_sum(dxh * xh, ones) * (1.0 / HEAD_DIM)))
            return dx, jnp.sum(dy * xh, axis=0, keepdims=True)

        dq, dqg = one(x_ref[:, pl.ds(0, ATTN_DIM)], dq_ref[...], qg_ref[...])
        dk, dkg = one(x_ref[:, pl.ds(ATTN_DIM, KV_DIM)], dk_ref[...], kg_ref[...])
        dvv = dv_ref[...]
        o_ref[:, pl.ds(0, ATTN_DIM)] = dq.astype(BF)
        o_ref[:, pl.ds(ATTN_DIM, KV_DIM)] = dk.astype(BF)
        o_ref[:, pl.ds(ATTN_DIM + KV_DIM, KV_DIM)] = dvv.astype(BF)
        sq, sk, sv = (jnp.sum(t, axis=0, keepdims=True) for t in (dq, dk, dvv))

        @pl.when(i == 0)
        def _():
            db_ref[:, pl.ds(0, ATTN_DIM)] = sq
            db_ref[:, pl.ds(ATTN_DIM, KV_DIM)] = sk
            db_ref[:, pl.ds(ATTN_DIM + KV_DIM, KV_DIM)] = sv
            dqg_ref[...] = dqg
            dkg_ref[...] = dkg

        @pl.when(i > 0)
        def _():
            db_ref[:, pl.ds(0, ATTN_DIM)] += sq
            db_ref[:, pl.ds(ATTN_DIM, KV_DIM)] += sk
            db_ref[:, pl.ds(ATTN_DIM + KV_DIM, KV_DIM)] += sv
            dqg_ref[...] += dqg
            dkg_ref[...] += dkg

    full = lambda shape: pl.BlockSpec(shape, lambda i: (0, 0))
    row = lambda n: pl.BlockSpec((tm, n), lambda i: (i, 0))
    return pl.pallas_call(
        body, name="qk_norm_bwd", grid=(T // tm,),
        in_specs=[row(QKV_DIM), row(ATTN_DIM), row(KV_DIM), row(KV_DIM), full((1, ATTN_DIM)), full((1, KV_DIM)),
                  full((HEAD_GROUP, HEAD_GROUP))],
        out_specs=[row(QKV_DIM), full((1, QKV_DIM)), full((1, ATTN_DIM)), full((1, KV_DIM))],
        out_shape=[jax.ShapeDtypeStruct((T, QKV_DIM), BF), jax.ShapeDtypeStruct((1, QKV_DIM), F32),
                   jax.ShapeDtypeStruct((1, ATTN_DIM), F32), jax.ShapeDtypeStruct((1, KV_DIM), F32)],
        compiler_params=_params(1),
    )(qkv, dqn, dkn, dv, qg_t, kg_t, _head_ones())


ROWS = 128
COLS = 128


SUBLANES = 8
FIRST_TAP = HALO - (CONV_W - 1)


def _glu(a, g):
    return a.astype(F32) * jax.nn.sigmoid(g.astype(F32))


def _shifted(xe, s):
    return xe if s == 0 else pltpu.roll(xe, ROWS + HALO - s, axis=0)


def _conv_fwd(u, dw_pad, dw_b, ln_g, ln_b, tm=512):
    T = u.shape[0]
    Dm = D_MODEL
    hpt = tm // HALO

    def body(ac_ref, gc_ref, ap_ref, gp_ref, w_ref, wb_ref, lg_ref, lb_ref, cv_ref, s_ref, ext):
        i = pl.program_id(0)
        ext[pl.ds(0, HALO), :] = jnp.where(i > 0, _glu(ap_ref[...], gp_ref[...]), 0.0)
        ext[pl.ds(HALO, tm), :] = _glu(ac_ref[...], gc_ref[...])

        def rows(r, carry):
            r0 = pl.multiple_of(r * ROWS, ROWS)
            for c in range(Dm // COLS):
                cs = pl.ds(c * COLS, COLS)
                xe = ext[pl.ds(r0, ROWS + HALO), cs]
                acc = jnp.zeros((ROWS, COLS), F32)
                for s in range(SUBLANES):
                    xs = _shifted(xe, s)
                    for j in range(CONV_W):
                        off = FIRST_TAP + j
                        if off % SUBLANES == s:
                            acc = acc + xs[off - s:off - s + ROWS, :] * w_ref[pl.ds(j, 1), cs]
                cv_ref[pl.ds(r0, ROWS), cs] = acc + wb_ref[:, cs]
            return carry

        lax.fori_loop(0, tm // ROWS, rows, 0)
        cv = cv_ref[...]
        xc = cv - jnp.mean(cv, axis=-1, keepdims=True)
        y = xc * lax.rsqrt(jnp.mean(xc * xc, axis=-1, keepdims=True) + NORM_EPS) * lg_ref[...] + lb_ref[...]
        s_ref[...] = (y * jax.nn.sigmoid(y)).astype(BF)

    full = lambda shape: pl.BlockSpec(shape, lambda i: (0, 0))
    return pl.pallas_call(
        body, name="conv_fwd", grid=(T // tm,),
        in_specs=[pl.BlockSpec((tm, Dm), lambda i: (i, 0)), pl.BlockSpec((tm, Dm), lambda i: (i, 1)),
                  pl.BlockSpec((HALO, Dm), lambda i: (jnp.maximum(i * hpt - 1, 0), 0)),
                  pl.BlockSpec((HALO, Dm), lambda i: (jnp.maximum(i * hpt - 1, 0), 1)),
                  full((HALO, Dm)), full((1, Dm)), full((1, Dm)), full((1, Dm))],
        out_specs=[pl.BlockSpec((tm, Dm), lambda i: (i, 0)), pl.BlockSpec((tm, Dm), lambda i: (i, 0))],
        out_shape=[jax.ShapeDtypeStruct((T, Dm), F32), jax.ShapeDtypeStruct((T, Dm), BF)],
        scratch_shapes=[pltpu.VMEM((tm + HALO, Dm), F32)],
        compiler_params=_params(1),
    )(u, u, u, u, dw_pad, dw_b, ln_g, ln_b)


def _ln_silu_bwd_ep(ds, cv, lg, lb):
    xc = cv - jnp.mean(cv, axis=-1, keepdims=True)
    rstd = lax.rsqrt(jnp.mean(xc * xc, axis=-1, keepdims=True) + NORM_EPS)
    xh = xc * rstd
    y = xh * lg + lb
    sg = jax.nn.sigmoid(y)
    dy = ds * (sg * (1.0 + y * (1.0 - sg)))
    dxh = dy * lg
    dcv = rstd * (dxh - jnp.mean(dxh, axis=-1, keepdims=True) - xh * jnp.mean(dxh * xh, axis=-1, keepdims=True))
    return (dcv, jnp.sum(dy * xh, axis=0, keepdims=True), jnp.sum(dy, axis=0, keepdims=True),
            jnp.sum(dcv, axis=0, keepdims=True))


def _conv_bwd(u, dcv, dw_pad, tm=512):
    T = u.shape[0]
    Dm = D_MODEL
    hpt = tm // HALO
    last = T // HALO - 1
    nt = T // tm

    def body(ac_ref, gc_ref, ap_ref, gp_ref, dc_ref, dn_ref, w_ref, du_ref, db_ref, dw_ref, ext_g, ext_d):
        i = pl.program_id(0)
        ext_g[pl.ds(0, HALO), :] = jnp.where(i > 0, _glu(ap_ref[...], gp_ref[...]), 0.0)
        ext_g[pl.ds(HALO, tm), :] = _glu(ac_ref[...], gc_ref[...])
        ext_d[pl.ds(0, tm), :] = dc_ref[...]
        ext_d[pl.ds(tm, HALO), :] = jnp.where(i < nt - 1, dn_ref[...], 0.0)

        @pl.when(i == 0)
        def _():
            db_ref[...] = jnp.zeros_like(db_ref)
            dw_ref[...] = jnp.zeros_like(dw_ref)

        def rows(r, carry):
            r0 = pl.multiple_of(r * ROWS, ROWS)
            rs = pl.ds(r0, ROWS)
            for c in range(Dm // COLS):
                cs = pl.ds(c * COLS, COLS)
                cs2 = pl.ds(Dm + c * COLS, COLS)
                de = ext_d[pl.ds(r0, ROWS + HALO), cs]
                ge = ext_g[pl.ds(r0, ROWS + HALO), cs]
                dcur = de[0:ROWS, :]
                acc = jnp.zeros((ROWS, COLS), F32)
                for s in range(SUBLANES):
                    ds_, gs_ = _shifted(de, s), _shifted(ge, s)
                    for j in range(CONV_W):
                        off = CONV_W - 1 - j
                        if off % SUBLANES == s:
                            acc = acc + ds_[off - s:off - s + ROWS, :] * w_ref[pl.ds(j, 1), cs]
                        goff = FIRST_TAP + j
                        if goff % SUBLANES == s:
                            prod = dcur * gs_[goff - s:goff - s + ROWS, :]
                            dw_ref[j, :, cs] += jnp.sum(prod.reshape(ROWS // SUBLANES, SUBLANES, COLS), axis=0)
                a = ac_ref[rs, cs].astype(F32)
                sg = jax.nn.sigmoid(gc_ref[rs, cs].astype(F32))
                da = acc * sg
                dg = acc * a * sg * (1.0 - sg)
                du_ref[rs, cs] = da.astype(BF)
                du_ref[rs, cs2] = dg.astype(BF)
                db_ref[:, cs] += jnp.sum(da, axis=0, keepdims=True)
                db_ref[:, cs2] += jnp.sum(dg, axis=0, keepdims=True)
            return carry

        lax.fori_loop(0, tm // ROWS, rows, 0)

    return pl.pallas_call(
        body, name="conv_bwd", grid=(nt,),
        in_specs=[pl.BlockSpec((tm, Dm), lambda i: (i, 0)), pl.BlockSpec((tm, Dm), lambda i: (i, 1)),
                  pl.BlockSpec((HALO, Dm), lambda i: (jnp.maximum(i * hpt - 1, 0), 0)),
                  pl.BlockSpec((HALO, Dm), lambda i: (jnp.maximum(i * hpt - 1, 0), 1)),
                  pl.BlockSpec((tm, Dm), lambda i: (i, 0)),
                  pl.BlockSpec((HALO, Dm), lambda i: (jnp.minimum((i + 1) * hpt, last), 0)),
                  pl.BlockSpec((HALO, Dm), lambda i: (0, 0))],
        out_specs=[pl.BlockSpec((tm, 2 * Dm), lambda i: (i, 0)), pl.BlockSpec((1, 2 * Dm), lambda i: (0, 0)),
                   pl.BlockSpec((HALO, 8, Dm), lambda i: (0, 0, 0))],
        out_shape=[jax.ShapeDtypeStruct((T, 2 * Dm), BF), jax.ShapeDtypeStruct((1, 2 * Dm), F32),
                   jax.ShapeDtypeStruct((HALO, 8, Dm), F32)],
        scratch_shapes=[pltpu.VMEM((tm + HALO, Dm), F32), pltpu.VMEM((tm + HALO, Dm), F32)],
        compiler_params=_params(1),
    )(u, u, u, u, dcv, dcv, dw_pad)


def _bucket_table():
    q_loc = np.arange(BLOCK)[:, None]
    k_loc = np.arange(2 * BLOCK)[None, :]
    dist = q_loc + BLOCK - k_loc
    n = np.maximum(dist, 0)
    max_exact = REL_BUCKETS // 2
    large = max_exact + (np.log(np.maximum(n, 1).astype(np.float32) / max_exact)
                         / math.log(REL_MAX_DIST / max_exact) * (REL_BUCKETS - max_exact)).astype(np.int32)
    large = np.minimum(large, REL_BUCKETS - 1)
    bucket = np.where(n < max_exact, n, large).astype(np.int32)
    band = np.where((dist >= 0) & (dist < BLOCK), bucket, -1)
    folded = np.where(np.arange(BLOCK)[None, :] > q_loc, band[:, :BLOCK], band[:, BLOCK:])
    assert (folded >= 0).all() and ((band[:, :BLOCK] >= 0) != (band[:, BLOCK:] >= 0)).all()
    return jnp.asarray(folded.astype(np.int32))


def _prev_mask():
    row = lax.broadcasted_iota(jnp.int32, (BLOCK, BLOCK), 0)
    col = lax.broadcasted_iota(jnp.int32, (BLOCK, BLOCK), 1)
    return col > row


def _fold(band, prev_mask):
    return jnp.where(prev_mask, band[:, :BLOCK], band[:, BLOCK:])


def _unfold(ref, g, rows, folded, prev_mask):
    ref[g, rows, pl.ds(0, BLOCK)] = jnp.where(prev_mask, folded, 0.0).astype(ref.dtype)
    ref[g, rows, pl.ds(BLOCK, BLOCK)] = jnp.where(prev_mask, 0.0, folded).astype(ref.dtype)


def _bias_table(rel_bias_t, bucket):
    def body(rb_ref, bk_ref, o_ref):
        bk = bk_ref[...]
        prev_mask = _prev_mask()
        for h in range(N_HEADS):
            acc = jnp.zeros((BLOCK, BLOCK), F32)
            for b in range(REL_BUCKETS):
                acc = jnp.where(bk == b, rb_ref[h, b], acc)
            o_ref[0, h] = acc
            o_ref[1, h] = jnp.where(prev_mask, NEG_INF, acc)

    return pl.pallas_call(
        body, name="bias_table", out_shape=jax.ShapeDtypeStruct((2, N_HEADS, BLOCK, BLOCK), F32),
        in_specs=[pl.BlockSpec(memory_space=pltpu.SMEM), pl.BlockSpec(memory_space=pltpu.VMEM)],
        out_specs=pl.BlockSpec(memory_space=pltpu.VMEM),
    )(rel_bias_t, bucket)


def _bias_grad(dbias, bucket):
    def body(db_ref, bk_ref, o_ref):
        bk = bk_ref[...]
        for b in range(REL_BUCKETS):
            sel = bk == b
            for h in range(N_HEADS):
                o_ref[h, b] = jnp.sum(jnp.where(sel, db_ref[h], 0.0))

    return pl.pallas_call(
        body, name="bias_grad", out_shape=jax.ShapeDtypeStruct((N_HEADS, REL_BUCKETS), F32),
        in_specs=[pl.BlockSpec(memory_space=pltpu.VMEM), pl.BlockSpec(memory_space=pltpu.VMEM)],
        out_specs=pl.BlockSpec(memory_space=pltpu.SMEM),
    )(dbias, bucket)


GROUP_ROWS = GROUP * BLOCK
BIAS_SPEC = pl.BlockSpec((2, N_HEADS, BLOCK, BLOCK), lambda n: (0, 0, 0, 0))


def _head_probs(qk, bias_h, sink, prev_mask):
    s = _fold(qk, prev_mask) + bias_h
    m = jnp.maximum(jnp.max(s, axis=-1, keepdims=True), sink)
    p = jnp.exp(s - m)
    ps = jnp.exp(sink - m)
    inv = 1.0 / (jnp.sum(p, axis=-1, keepdims=True) + ps)
    return p * inv, ps * inv


def _band(prev_ref, cur_ref, g):
    hs = pl.ds(g * HEAD_DIM, HEAD_DIM)
    return jnp.concatenate([prev_ref[:, hs], cur_ref[:, hs]], axis=0)


def _stack_heads(ref, g):
    return jnp.concatenate([ref[:, pl.ds((g * GROUP + hh) * HEAD_DIM, HEAD_DIM)] for hh in range(GROUP)], axis=0)


def _unstack_heads(ref, g, stacked, dtype):
    for hh in range(GROUP):
        ref[:, pl.ds((g * GROUP + hh) * HEAD_DIM, HEAD_DIM)] = stacked[hh * BLOCK:(hh + 1) * BLOCK, :].astype(dtype)


def _head_rows(hh):
    return pl.ds(hh * BLOCK, BLOCK)


def _attn_fwd(qn, kn, vv, bias, sinks):
    T = qn.shape[0]
    nb = T // BLOCK

    def body(sk_ref, q_ref, kc_ref, kp_ref, vc_ref, vp_ref, b_ref, o_ref, qk_buf, p_buf):
        table = (pl.program_id(0) == 0).astype(jnp.int32)
        prev_mask = _prev_mask()
        for g in range(N_KV):
            qk_buf[g] = _dot(_stack_heads(q_ref, g), _band(kp_ref, kc_ref, g), 1, 1)
        for g in range(N_KV):
            for hh in range(GROUP):
                h = g * GROUP + hh
                pn, _ = _head_probs(qk_buf[g, _head_rows(hh), :], b_ref[table, h], sk_ref[h], prev_mask)
                _unfold(p_buf, g, _head_rows(hh), pn, prev_mask)
        for g in range(N_KV):
            _unstack_heads(o_ref, g, _dot(p_buf[g], _band(vp_ref, vc_ref, g), 1, 0), BF)

    cur = lambda n: (n, 0)
    prev = lambda n: (jnp.maximum(n - 1, 0), 0)
    return pl.pallas_call(
        body, name="attn_fwd", grid=(nb,),
        in_specs=[pl.BlockSpec(memory_space=pltpu.SMEM), pl.BlockSpec((BLOCK, ATTN_DIM), cur),
                  pl.BlockSpec((BLOCK, KV_DIM), cur), pl.BlockSpec((BLOCK, KV_DIM), prev),
                  pl.BlockSpec((BLOCK, KV_DIM), cur), pl.BlockSpec((BLOCK, KV_DIM), prev), BIAS_SPEC],
        out_specs=pl.BlockSpec((BLOCK, ATTN_DIM), cur), out_shape=jax.ShapeDtypeStruct((T, ATTN_DIM), BF),
        scratch_shapes=[pltpu.VMEM((N_KV, GROUP_ROWS, 2 * BLOCK), F32), pltpu.VMEM((N_KV, GROUP_ROWS, 2 * BLOCK), BF)],
        compiler_params=_params(1),
    )(sinks, qn, kn, kn, vv, vv, bias)


def _attn_bwd(qn, kn, vv, bias, sinks, do):
    T = qn.shape[0]
    nb = T // BLOCK
    scale = 1.0 / math.sqrt(HEAD_DIM)

    def body(sk_ref, q_ref, kc_ref, kp_ref, vc_ref, vp_ref, b_ref, do_ref,
             dq_ref, dk_ref, dv_ref, db_ref, dsk_ref, dk_full, dv_full, dk_carry, dv_carry, qk_buf, dp_buf, p_buf, ds_buf):
        n = pl.program_id(0)

        @pl.when(n == 0)
        def _():
            db_ref[...] = jnp.zeros_like(db_ref)
            dk_carry[...] = jnp.zeros_like(dk_carry)
            dv_carry[...] = jnp.zeros_like(dv_carry)
            for h in range(N_HEADS):
                dsk_ref[h] = 0.0

        @pl.when(n < nb)
        def _():
            table = (n == 0).astype(jnp.int32)
            prev_mask = _prev_mask()
            ks = [_band(kp_ref, kc_ref, g) for g in range(N_KV)]
            qs = [_stack_heads(q_ref, g) for g in range(N_KV)]
            douts = [_stack_heads(do_ref, g) for g in range(N_KV)]
            for g in range(N_KV):
                qk_buf[g] = _dot(qs[g], ks[g], 1, 1)
                dp_buf[g] = _dot(douts[g], _band(vp_ref, vc_ref, g), 1, 1)
            for g in range(N_KV):
                for hh in range(GROUP):
                    h = g * GROUP + hh
                    rows = _head_rows(hh)
                    pn, psink = _head_probs(qk_buf[g, rows, :], b_ref[table, h], sk_ref[h], prev_mask)
                    dp = _fold(dp_buf[g, rows, :], prev_mask)
                    delta = jnp.sum(pn * dp, axis=-1, keepdims=True)
                    ds = pn * (dp - delta)
                    dsk_ref[h] += -jnp.sum(psink * delta)
                    db_ref[h] += ds
                    _unfold(ds_buf, g, rows, ds, prev_mask)
                    _unfold(p_buf, g, rows, pn, prev_mask)
            for g in range(N_KV):
                dsb = ds_buf[g]
                _unstack_heads(dq_ref, g, _dot(dsb, ks[g], 1, 0) * scale, F32)
                gs = pl.ds(g * HEAD_DIM, HEAD_DIM)
                dk_full[:, gs] = _dot(dsb, qs[g], 0, 0)
                dv_full[:, gs] = _dot(p_buf[g], douts[g], 0, 0)

        @pl.when(n == nb)
        def _():
            dk_full[...] = jnp.zeros_like(dk_full)
            dv_full[...] = jnp.zeros_like(dv_full)

        dk_ref[...] = dk_carry[...] + dk_full[pl.ds(0, BLOCK), :]
        dv_ref[...] = dv_carry[...] + dv_full[pl.ds(0, BLOCK), :]
        dk_carry[...] = dk_full[pl.ds(BLOCK, BLOCK), :]
        dv_carry[...] = dv_full[pl.ds(BLOCK, BLOCK), :]

    cur = lambda n: (jnp.minimum(n, nb - 1), 0)
    prev = lambda n: (jnp.maximum(jnp.minimum(n, nb - 1) - 1, 0), 0)
    out_kv = lambda n: (jnp.maximum(n - 1, 0), 0)
    return pl.pallas_call(
        body, name="attn_bwd", grid=(nb + 1,),
        in_specs=[pl.BlockSpec(memory_space=pltpu.SMEM), pl.BlockSpec((BLOCK, ATTN_DIM), cur),
                  pl.BlockSpec((BLOCK, KV_DIM), cur), pl.BlockSpec((BLOCK, KV_DIM), prev),
                  pl.BlockSpec((BLOCK, KV_DIM), cur), pl.BlockSpec((BLOCK, KV_DIM), prev), BIAS_SPEC,
                  pl.BlockSpec((BLOCK, ATTN_DIM), cur)],
        out_specs=[pl.BlockSpec((BLOCK, ATTN_DIM), cur), pl.BlockSpec((BLOCK, KV_DIM), out_kv),
                   pl.BlockSpec((BLOCK, KV_DIM), out_kv),
                   pl.BlockSpec((N_HEADS, BLOCK, BLOCK), lambda n: (0, 0, 0)),
                   pl.BlockSpec(memory_space=pltpu.SMEM)],
        out_shape=[jax.ShapeDtypeStruct((T, ATTN_DIM), F32), jax.ShapeDtypeStruct((T, KV_DIM), F32),
                   jax.ShapeDtypeStruct((T, KV_DIM), F32),
                   jax.ShapeDtypeStruct((N_HEADS, BLOCK, BLOCK), F32), jax.ShapeDtypeStruct((N_HEADS,), F32)],
        scratch_shapes=[pltpu.VMEM((2 * BLOCK, KV_DIM), F32), pltpu.VMEM((2 * BLOCK, KV_DIM), F32),
                        pltpu.VMEM((BLOCK, KV_DIM), F32), pltpu.VMEM((BLOCK, KV_DIM), F32),
                        pltpu.VMEM((N_KV, GROUP_ROWS, 2 * BLOCK), F32), pltpu.VMEM((N_KV, GROUP_ROWS, 2 * BLOCK), F32),
                        pltpu.VMEM((N_KV, GROUP_ROWS, 2 * BLOCK), BF), pltpu.VMEM((N_KV, GROUP_ROWS, 2 * BLOCK), BF)],
        compiler_params=_params(1),
    )(sinks, qn, kn, kn, vv, vv, bias, do)


def _coords():
    return lax.axis_index("x"), lax.axis_index("y"), lax.axis_index("c")


def _sum8(name, blocks):
    def body(b_ref, o_ref):
        tot = b_ref[0]
        for d in range(1, 8):
            tot = tot + b_ref[d]
        o_ref[...] = tot

    return pl.pallas_call(body, name=name, out_shape=jax.ShapeDtypeStruct(blocks.shape[1:], F32))(blocks)


HBM_SPEC = pl.BlockSpec(memory_space=pltpu.HBM)
SEM_SPEC = pl.BlockSpec(memory_space=pltpu.SEMAPHORE)
ANY_SPEC = pl.BlockSpec(memory_space=pl.ANY)
DATAFLOW = pltpu.SideEffectType.DATAFLOW_SIDE_EFFECTING


OTHER_CHIPS = (4, 2, 6)
ALL_OTHERS = (1, 2, 3, 4, 5, 6, 7)


def _slot(x, y, c, peers):
    return 2 * x + y if peers is OTHER_CHIPS else 4 * x + 2 * y + c


def _slot_copy(land, sems, idx, x, y, c, k, peers, arriving):
    send_sems, recv_sems = sems
    px, py, pc = x ^ (k >> 2), y ^ ((k >> 1) & 1), c ^ (k & 1)
    mine = _slot(x, y, c, peers)
    dst = _slot(px, py, pc, peers) if arriving else mine
    return pltpu.make_async_remote_copy(src_ref=land.at[mine], dst_ref=land.at[dst], send_sem=send_sems.at[idx],
                                        recv_sem=recv_sems.at[idx], device_id=(px, py, pc), device_id_type=MESH)


def _gather_start(name, stacks, groups, peers, after):
    n = len(stacks)
    ng = len(groups)
    np_ = len(peers)
    after = tuple(after)

    def body(*refs):
        lands = refs[:n]
        first = n + len(after)
        sems = [(refs[first + 2 * g], refs[first + 2 * g + 1]) for g in range(ng)]
        token = refs[-1]
        x, y, c = _coords()
        for g, members in enumerate(groups):
            for i, t in enumerate(members):
                for j, k in enumerate(peers):
                    _slot_copy(lands[t], sems[g], np_ * i + j, x, y, c, k, peers, arriving=False).start()
        token[...] = jnp.zeros_like(token)

    out_shape = []
    for members in groups:
        out_shape += [pltpu.SemaphoreType.DMA((np_ * len(members),))] * 2
    out_shape += [pltpu.HBM(w.shape, w.dtype) for w in stacks]
    out_shape.append(jax.ShapeDtypeStruct((8, 128), F32))
    res = pl.pallas_call(
        body, name=name, out_shape=out_shape, in_specs=[HBM_SPEC] * n + [ANY_SPEC] * len(after),
        out_specs=[SEM_SPEC] * (2 * ng) + [HBM_SPEC] * n + [pl.BlockSpec(memory_space=pltpu.VMEM)],
        input_output_aliases={t: 2 * ng + t for t in range(n)},
        compiler_params=pltpu.CompilerParams(has_side_effects=DATAFLOW),
    )(*[pltpu.with_memory_space_constraint(w, pltpu.HBM) for w in stacks], *after)
    sems = [(res[2 * g], res[2 * g + 1]) for g in range(ng)]
    return sems, list(res[2 * ng:2 * ng + n]), res[-1]


def _gather_wait(name, stacks, sems, peers, after):
    n = len(stacks)
    after = tuple(after)

    def body(*refs):
        lands = refs[:n]
        group_sems = (refs[n], refs[n + 1])
        x, y, c = _coords()
        for i in range(n):
            for j, k in enumerate(peers):
                cp = _slot_copy(lands[i], group_sems, len(peers) * i + j, x, y, c, k, peers, arriving=True)
                cp.wait_send()
                cp.wait_recv()

    return pl.pallas_call(
        body, name=name, out_shape=[pltpu.HBM(w.shape, w.dtype) for w in stacks],
        in_specs=[HBM_SPEC] * n + [SEM_SPEC, SEM_SPEC] + [ANY_SPEC] * len(after), out_specs=[HBM_SPEC] * n,
        input_output_aliases={t: t for t in range(n)},
        compiler_params=pltpu.CompilerParams(has_side_effects=DATAFLOW),
    )(*stacks, sems[0], sems[1], *after)


N_PEERS = 7


def _peer(x, y, c, k):
    return x ^ (k >> 2), y ^ ((k >> 1) & 1), c ^ (k & 1)


def _reduce_copy(grad, land, sems, idx, x, y, c, k):
    px, py, pc = _peer(x, y, c, k)
    rh = grad.shape[1] // 2
    return pltpu.make_async_remote_copy(src_ref=grad.at[2 * px + py, pl.ds(pc * rh, rh), :], dst_ref=land.at[k - 1],
                                        send_sem=sems[0].at[idx], recv_sem=sems[1].at[idx], device_id=(px, py, pc),
                                        device_id_type=MESH)


def _reduce_start(name, grads):
    n = len(grads)

    def body(*refs):
        src, lands, sems, token = refs[:n], refs[n:2 * n], (refs[2 * n], refs[2 * n + 1]), refs[-1]
        x, y, c = _coords()
        for t in range(n):
            for k in range(1, N_PEERS + 1):
                _reduce_copy(src[t], lands[t], sems, N_PEERS * t + k - 1, x, y, c, k).start()
        token[...] = jnp.zeros_like(token)

    lands = [lax.empty((N_PEERS, g.shape[1] // 2, g.shape[2]), g.dtype) for g in grads]
    out_shape = [pltpu.SemaphoreType.DMA((N_PEERS * n,))] * 2
    out_shape += [pltpu.HBM(a.shape, a.dtype) for a in list(grads) + lands]
    out_shape.append(jax.ShapeDtypeStruct((8, 128), F32))
    res = pl.pallas_call(
        body, name=name, out_shape=out_shape, in_specs=[HBM_SPEC] * (2 * n),
        out_specs=[SEM_SPEC] * 2 + [HBM_SPEC] * (2 * n) + [pl.BlockSpec(memory_space=pltpu.VMEM)],
        input_output_aliases={t: 2 + t for t in range(2 * n)},
        compiler_params=pltpu.CompilerParams(has_side_effects=DATAFLOW),
    )(*[pltpu.with_memory_space_constraint(a, pltpu.HBM) for a in list(grads) + lands])
    return (res[0], res[1]), list(res[2:2 + n]), list(res[2 + n:2 + 2 * n]), res[-1]


def _reduce_wait(name, grads, lands, sems, after):
    n = len(grads)
    after = tuple(after)

    def body(*refs):
        src, dst, group_sems = refs[:n], refs[n:2 * n], (refs[2 * n], refs[2 * n + 1])
        x, y, c = _coords()
        for t in range(n):
            for k in range(1, N_PEERS + 1):
                cp = _reduce_copy(src[t], dst[t], group_sems, N_PEERS * t + k - 1, x, y, c, k)
                cp.wait_send()
                cp.wait_recv()

    res = pl.pallas_call(
        body, name=name, out_shape=[pltpu.HBM(a.shape, a.dtype) for a in list(grads) + list(lands)],
        in_specs=[HBM_SPEC] * (2 * n) + [SEM_SPEC, SEM_SPEC] + [ANY_SPEC] * len(after), out_specs=[HBM_SPEC] * (2 * n),
        input_output_aliases={t: t for t in range(2 * n)},
        compiler_params=pltpu.CompilerParams(has_side_effects=DATAFLOW),
    )(*grads, *lands, sems[0], sems[1], *after)
    return list(res[:n]), list(res[n:])


def _join_copy(half, land, sems, idx, x, y, c):
    return pltpu.make_async_remote_copy(src_ref=half, dst_ref=land, send_sem=sems[0].at[idx], recv_sem=sems[1].at[idx],
                                        device_id=(x, y, 1 - c), device_id_type=MESH)


def _join_start(name, halves):
    n = len(halves)

    def body(*refs):
        src, lands, sems, token = refs[:n], refs[n:2 * n], (refs[2 * n], refs[2 * n + 1]), refs[-1]
        x, y, c = _coords()
        for t in range(n):
            _join_copy(src[t], lands[t], sems, t, x, y, c).start()
        token[...] = jnp.zeros_like(token)

    lands = [lax.empty(h.shape, h.dtype) for h in halves]
    out_shape = [pltpu.SemaphoreType.DMA((n,))] * 2
    out_shape += [pltpu.HBM(a.shape, a.dtype) for a in list(halves) + lands]
    out_shape.append(jax.ShapeDtypeStruct((8, 128), F32))
    res = pl.pallas_call(
        body, name=name, out_shape=out_shape, in_specs=[HBM_SPEC] * (2 * n),
        out_specs=[SEM_SPEC] * 2 + [HBM_SPEC] * (2 * n) + [pl.BlockSpec(memory_space=pltpu.VMEM)],
        input_output_aliases={t: 2 + t for t in range(2 * n)},
        compiler_params=pltpu.CompilerParams(has_side_effects=DATAFLOW),
    )(*[pltpu.with_memory_space_constraint(a, pltpu.HBM) for a in list(halves) + lands])
    return (res[0], res[1]), list(res[2:2 + n]), list(res[2 + n:2 + 2 * n]), res[-1]


def _join_wait(name, halves, lands, sems, after):
    n = len(halves)
    after = tuple(after)

    def body(*refs):
        src, dst, group_sems = refs[:n], refs[n:2 * n], (refs[2 * n], refs[2 * n + 1])
        x, y, c = _coords()
        for t in range(n):
            cp = _join_copy(src[t], dst[t], group_sems, t, x, y, c)
            cp.wait_send()
            cp.wait_recv()

    res = pl.pallas_call(
        body, name=name, out_shape=[pltpu.HBM(a.shape, a.dtype) for a in list(halves) + list(lands)],
        in_specs=[HBM_SPEC] * (2 * n) + [SEM_SPEC, SEM_SPEC] + [ANY_SPEC] * len(after), out_specs=[HBM_SPEC] * (2 * n),
        input_output_aliases={t: t for t in range(2 * n)},
        compiler_params=pltpu.CompilerParams(has_side_effects=DATAFLOW),
    )(*halves, *lands, sems[0], sems[1], *after)
    return list(res[:n]), list(res[n:])


def _join_halves(name, halves, deps=()):
    n = len(halves)

    def body(*refs):
        src, dst = refs[:n], refs[n + len(deps):2 * n + len(deps)]
        send_sems, recv_sems = refs[-2:]
        x, y, c = _coords()
        cps = []
        for t in range(n):
            cp = pltpu.make_async_remote_copy(src_ref=src[t], dst_ref=dst[t], send_sem=send_sems.at[t],
                                              recv_sem=recv_sems.at[t], device_id=(x, y, 1 - c), device_id_type=MESH)
            cp.start()
            cps.append(cp)
        for cp in cps:
            cp.wait()

    anyspec = pl.BlockSpec(memory_space=pl.ANY)
    return pl.pallas_call(
        body, name=name, out_shape=[jax.ShapeDtypeStruct(h.shape, h.dtype) for h in halves],
        in_specs=[anyspec] * (n + len(deps)), out_specs=[anyspec] * n,
        scratch_shapes=[pltpu.SemaphoreType.DMA((n,)), pltpu.SemaphoreType.DMA((n,))],
    )(*halves, *deps)


def _row_block(rows):
    for rb in (512, 256, 128, 64, 32, 16):
        if rows % rb == 0:
            return rb
    raise ValueError(rows)


def _sum_devices(name, grad, land, place):
    S, R, C = grad.shape
    rh = R // 2
    rb = _row_block(rh)
    nbh = rh // rb

    def body(place_ref, g_ref, l_ref, o_ref):
        tot = g_ref[...].astype(F32)
        for k in range(N_PEERS):
            tot = tot + l_ref[k].astype(F32)
        o_ref[...] = tot

    return pl.pallas_call(
        body, name=name,
        grid_spec=pltpu.PrefetchScalarGridSpec(
            num_scalar_prefetch=1, grid=(nbh,),
            in_specs=[pl.BlockSpec((None, rb, C), lambda r, place: (place[0], place[1] * nbh + r, 0)),
                      pl.BlockSpec((N_PEERS, rb, C), lambda r, place: (0, r, 0))],
            out_specs=pl.BlockSpec((rb, C), lambda r, place: (r, 0))),
        out_shape=jax.ShapeDtypeStruct((rh, C), F32), compiler_params=_params(1),
    )(place, grad, land)


def _adamw_math(w, g, m, v):
    m2 = ADAM_B1 * m + (1.0 - ADAM_B1) * g
    v2 = ADAM_B2 * v + (1.0 - ADAM_B2) * (g * g)
    m_hat = m2 / (1.0 - ADAM_B1 ** ADAM_STEP)
    v_hat = v2 / (1.0 - ADAM_B2 ** ADAM_STEP)
    delta = -ADAM_LR * (m_hat / (jnp.sqrt(v_hat) + ADAM_EPS) + ADAM_WD * w)
    return delta, m2, v2


def _adamw(name, w, m, v, gs):
    L, R, C = w.shape
    Rh = R // 2
    rb = _row_block(Rh)
    nbh = Rh // rb
    assert len(gs) == L

    def body(core_ref, w_ref, m_ref, v_ref, *rest):
        g_refs, (go_ref, d_ref, m2_ref, v2_ref) = rest[:2 * L], rest[2 * L:]
        layer, half = pl.program_id(0), pl.program_id(1)
        mine = half == core_ref[0]
        g = jnp.where(mine, g_refs[0][...], g_refs[1][...])
        for t in range(1, L):
            g = jnp.where(layer == t, jnp.where(mine, g_refs[2 * t][...], g_refs[2 * t + 1][...]), g)
        delta, m2, v2 = _adamw_math(w_ref[...], g, m_ref[...], v_ref[...])
        go_ref[...] = g
        d_ref[...] = delta
        m2_ref[...] = m2
        v2_ref[...] = v2

    wspec = pl.BlockSpec((None, rb, C), lambda l, h, r, core: (l, h * nbh + r, 0))
    gspec = pl.BlockSpec((rb, C), lambda l, h, r, core: (r, 0))
    return pl.pallas_call(
        body, name=name,
        grid_spec=pltpu.PrefetchScalarGridSpec(num_scalar_prefetch=1, grid=(L, 2, nbh),
                                               in_specs=[wspec] * 3 + [gspec] * (2 * L), out_specs=[wspec] * 4),
        out_shape=[jax.ShapeDtypeStruct((L, R, C), F32)] * 4, compiler_params=_params(3),
    )(lax.axis_index("c").astype(jnp.int32).reshape(1), w, m, v, *[g for pair in gs for g in pair])


def _adamw_small(ws, gs, ms, vs):
    n = len(ws)

    def body(*refs):
        w_refs, g_refs, m_refs, v_refs = (refs[k * n:(k + 1) * n] for k in range(4))
        d_refs, m2_refs, v2_refs = (refs[(4 + k) * n:(5 + k) * n] for k in range(3))
        for t in range(n):
            delta, m2, v2 = _adamw_math(w_refs[t][...], g_refs[t][...], m_refs[t][...], v_refs[t][...])
            d_refs[t][...] = delta
            m2_refs[t][...] = m2
            v2_refs[t][...] = v2

    res = pl.pallas_call(body, name="adamw_small", out_shape=[jax.ShapeDtypeStruct(w.shape, F32) for w in ws] * 3)(
        *ws, *gs, *ms, *vs)
    return res[:n], res[n:2 * n], res[2 * n:]


def _packed_rows(shape):
    c = shape[-1]
    return (int(np.prod(shape)) // c) * -(-c // LANES)


def _pack(arrays):
    total = sum(_packed_rows(a.shape) for a in arrays)
    total += -total % 8
    buf, r0 = None, 0
    for a in arrays:
        a = a.astype(F32).reshape(-1, a.shape[-1])
        r, c = a.shape
        k = -(-c // LANES)
        a = jnp.pad(a, ((0, 0), (0, k * LANES - c))).reshape(r * k, LANES)
        a = jnp.pad(a, ((r0, total - r0 - r * k), (0, 0)))
        buf = a if buf is None else buf + a
        r0 += r * k
    return buf


def _unpack(buf, shapes):
    out, r0 = [], 0
    for shp in shapes:
        c = shp[-1]
        rows = _packed_rows(shp)
        out.append(buf[r0:r0 + rows].reshape(-1, -(-c // LANES) * LANES)[:, :c].reshape(shp))
        r0 += rows
    return out


def _rms(x, g):
    return x * lax.rsqrt(jnp.mean(x * x, axis=-1, keepdims=True) + NORM_EPS) * g


def _residual_norm_ep(acc, *rest):
    *bias, res, gain = rest
    x = acc + res + (bias[0] if bias else 0.0)
    return x, _rms(x, gain)


RESIDUAL_NORM_OUTS = (("tile", F32), ("tile", BF))


def _mlp_up(tag, h, w_up_sm):
    (up,) = _mm(f"mlp{tag}_up", h, w_up_sm, nt=False, b_sm=True, tm=2048, tn=1024, rows=256,
                ep_fn=lambda acc: (acc,), outs=(("tile", BF),))
    return up


RMS_BWD_OUTS = (("tile", F32), ("tile", BF), ("colsum", F32), ("colsum", F32))


def _mlp_bwd(tag, dy, dy_bf, x, g, up, w_up_sm, w_down):
    (dup,) = _mm(f"mlp{tag}_dup", dy_bf, w_down, nt=True, tm=2048, tn=1024, rows=256, ep_in=((up, "tile"),),
                 ep_fn=lambda acc, u: (acc * (2.0 * jnp.maximum(u.astype(F32), 0.0)),), outs=(("tile", BF),))
    dx, dx_bf, dg, dx_sum = _mm(f"mlp{tag}_dx", dup, w_up_sm, nt=True, b_sm=True, tm=512, tn=1024, rows=256,
                                ep_in=((x, "tile"), (g, "row"), (dy, "tile")), ep_fn=_rms_bwd_ep, outs=RMS_BWD_OUTS)
    return dx, dx_bf, dg, dx_sum, dup


class _Reduction:
    def __init__(self, tag, grads, place):
        self.tag, self.place = tag, place
        self.sems, self.grads, self.lands, self.token = _reduce_start(f"reduce_start_{tag}", grads)

    def finish(self, after):
        grads, lands = _reduce_wait(f"reduce_wait_{self.tag}", self.grads, self.lands, self.sems, after)
        return [_sum_devices(f"reduce_sum_{self.tag}{i}", g, l, self.place) for i, (g, l) in enumerate(zip(grads, lands))]


def kernel(x, conv_norm_g, conv_w_in, conv_b_in, conv_dw, conv_dw_b, conv_ln_g, conv_ln_b, conv_w_out, conv_b_out, attn_norm_g, w_qkv, b_qkv, q_norm_g, k_norm_g, sinks, w_o, b_o, rel_bias, mlp_norm_g, w_up, w_down, loss_target, m_conv_norm_g, m_conv_w_in, m_conv_b_in, m_conv_dw, m_conv_dw_b, m_conv_ln_g, m_conv_ln_b, m_conv_w_out, m_conv_b_out, m_attn_norm_g, m_w_qkv, m_b_qkv, m_q_norm_g, m_k_norm_g, m_sinks, m_w_o, m_b_o, m_rel_bias, m_mlp_norm_g, m_w_up, m_w_down, v_conv_norm_g, v_conv_w_in, v_conv_b_in, v_conv_dw, v_conv_dw_b, v_conv_ln_g, v_conv_ln_b, v_conv_w_out, v_conv_b_out, v_attn_norm_g, v_w_qkv, v_b_qkv, v_q_norm_g, v_k_norm_g, v_sinks, v_w_o, v_b_o, v_rel_bias, v_mlp_norm_g, v_w_up, v_w_down):
    Dm = D_MODEL
    x2d = x[0]
    tgt = loss_target[0]
    T = x2d.shape[0]
    shard = 2 * lax.axis_index("x") + lax.axis_index("y")

    me = 2 * shard + lax.axis_index("c")

    def own_slot(block, slots, index):
        return lax.dynamic_update_slice(lax.empty((slots,) + block.shape, block.dtype), block[None],
                                        (index,) + (0,) * block.ndim)

    (conv_in_sems,), (stack_in,), first_token = _gather_start(
        "gather_start_conv_in", [own_slot(conv_w_in[0].astype(BF), N_SHARD, shard)], ((0,),), OTHER_CHIPS, after=())
    sharded_small = [conv_dw[0], attn_norm_g, b_qkv, b_o]
    (small_sems,), (small_land,), small_token = _gather_start(
        "small_weights_start", [own_slot(_pack(sharded_small), 8, me)], ((0,),), ALL_OTHERS, after=(first_token,))

    big = [conv_w_out[0], jnp.swapaxes(w_qkv, 1, 2)[0], w_o[0], w_up[0], w_up[1], w_down[0], w_down[1]]
    stacks = [own_slot(w.astype(BF), N_SHARD, shard) for w in big]
    groups = ((0,), (3, 5), (1, 2), (4, 6))
    gather_sems, stacks, gather_token = _gather_start("gather_start", stacks, groups, OTHER_CHIPS, after=(small_token,))

    def gathered_group(g, name, after):
        return _gather_wait(name, [stacks[t] for t in groups[g]], gather_sems[g], OTHER_CHIPS, after)

    bucket = _bucket_table()
    bias = _bias_table(rel_bias.T, bucket)

    h0 = _rms_fwd("conv_norm", x2d, conv_norm_g, deps=(gather_token,))
    (w_in_sm,) = _gather_wait("gather_wait_conv_in", [stack_in], conv_in_sems, OTHER_CHIPS, (h0, bias))
    (u,) = _mm("conv_in", h0, w_in_sm, nt=False, b_sm=True, tm=2048, tn=512, rows=256, ep_in=((conv_b_in, "row"),),
               ep_fn=lambda acc, b: (acc + b,), outs=(("tile", BF),))
    (gathered,) = _gather_wait("small_weights_wait", [small_land], small_sems, ALL_OTHERS, (u,))
    chips = [_unpack(gathered[2 * s], [a.shape for a in sharded_small]) for s in range(N_SHARD)]
    dw_f, attn_norm_f, b_qkv_f, b_o_f = (jnp.concatenate([chips[s][t] for s in range(N_SHARD)], axis=-1)
                                         for t in range(len(sharded_small)))
    dw_pad = jnp.pad(dw_f, ((0, HALO - CONV_W), (0, 0)))
    cv, s_act = _conv_fwd(u, dw_pad, conv_dw_b, conv_ln_g, conv_ln_b)
    (g_out,) = gathered_group(0, "gather_wait_conv_out", (s_act,))
    w_out_f = g_out.reshape(Dm, Dm)
    x1, h1 = _mm("conv_out", s_act, w_out_f, nt=False, tm=1024, tn=1024, rows=256,
                 ep_in=((conv_b_out, "row"), (x2d, "tile"), (mlp_norm_g[0:1], "row")), ep_fn=_residual_norm_ep,
                 outs=RESIDUAL_NORM_OUTS)

    g_up0, g_down0 = gathered_group(1, "gather_wait_mlp0", (x1,))
    w_up_sm = [g_up0, None]
    w_down_f = [g_down0.reshape(D_FF, Dm), None]
    up0 = _mlp_up(0, h1, w_up_sm[0])
    x2, h2 = _mm("mlp0_down", up0, w_down_f[0], nt=False, tm=512, tn=1024, rows=256, a_fn=_relu2,
                 ep_in=((x1, "tile"), (attn_norm_f, "row")), ep_fn=_residual_norm_ep, outs=RESIDUAL_NORM_OUTS)

    g_qkv, g_o = gathered_group(2, "gather_wait_attn", (x2,))
    w_qkv_t = g_qkv.reshape(QKV_DIM, Dm)
    w_o_f = g_o.reshape(ATTN_DIM, Dm)
    qg_t = jnp.tile(q_norm_g, (1, N_HEADS))
    kg_t = jnp.tile(k_norm_g, (1, N_KV))

    def qkv_ep(acc, b, qg, kg, ones):
        proj = acc + b
        q, k, v = proj[:, :ATTN_DIM], proj[:, ATTN_DIM:ATTN_DIM + KV_DIM], proj[:, ATTN_DIM + KV_DIM:]
        return proj, _qk_normed(q, qg, ones, 1.0 / math.sqrt(HEAD_DIM)), _qk_normed(k, kg, ones, 1.0), v

    qkv, qn, kn, vv = _mm(
        "attn_qkv", h2, w_qkv_t, nt=True, tm=1024, tn=QKV_DIM, rows=256, ep_fn=qkv_ep,
        ep_in=((b_qkv_f, "row"), (qg_t, "whole"), (kg_t, "whole"), (_head_ones(), "whole")),
        outs=(("tile", F32), ("tile", BF, ATTN_DIM), ("tile", BF, KV_DIM), ("tile", BF, KV_DIM)))
    sinks1 = sinks[0]
    att = _attn_fwd(qn, kn, vv, bias, sinks1)
    x3, h3 = _mm("attn_out", att, w_o_f, nt=False, tm=1024, tn=1024, rows=256,
                 ep_in=((b_o_f, "row"), (x2, "tile"), (mlp_norm_g[1:2], "row")), ep_fn=_residual_norm_ep,
                 outs=RESIDUAL_NORM_OUTS)

    g_up1, g_down1 = gathered_group(3, "gather_wait_mlp1", (x3,))
    w_up_sm[1] = g_up1
    w_down_f[1] = g_down1.reshape(D_FF, Dm)
    up1 = _mlp_up(1, h3, w_up_sm[1])

    def loss_ep(acc, r, t):
        diff = acc + r - t
        dy = diff * (1.0 / Dm)
        return dy, dy, jnp.sum(diff * diff, axis=0, keepdims=True)

    dy, dy_bf, sq = _mm("mlp1_down_loss", up1, w_down_f[1], nt=False, tm=512, tn=1024, rows=256, a_fn=_relu2,
                        ep_in=((x3, "tile"), (tgt, "tile")), ep_fn=loss_ep,
                        outs=(("tile", F32), ("tile", BF), ("colsum", F32)))

    place = jnp.stack([shard, lax.axis_index("c")]).astype(jnp.int32)
    dx3, dx3_bf, dg_mlp1, db_o, dup1 = _mlp_bwd(1, dy, dy_bf, x3, mlp_norm_g[1:2], up1, w_up_sm[1], w_down_f[1])
    dw_down1 = _mm_tn("mlp1_dw_down", up1, dy_bf, tm=1024, tn=1024, tk=2048, a_fn=_relu2)
    dw_up1 = _mm_tn("mlp1_dw_up", h3, dup1, tm=1024, tn=1024, tk=2048, out_sm=N_SHARD)
    red_mlp1 = _Reduction("mlp1", [dw_up1, dw_down1.reshape(N_SHARD, D_FF // N_SHARD, Dm)], place)

    ident = lambda acc: (acc,)
    (datt,) = _mm("attn_dout", dx3_bf, w_o_f, nt=True, tm=1024, tn=1024, rows=256, ep_fn=ident, outs=(("tile", BF),),
                  deps=(red_mlp1.token,))
    dw_o = _mm_tn("attn_dw_o", att, dx3_bf, tm=1024, tn=1024, tk=2048)
    dqn, dkn, dvv, dbias, dsinks = _attn_bwd(qn, kn, vv, bias, sinks1, datt)
    drel = _bias_grad(dbias, bucket)
    dqkv, db_qkv, dqg_t, dkg_t = _qk_norm_bwd(qkv, dqn, dkn, dvv, qg_t, kg_t)
    dw_qkv_t = _mm_tn("attn_dw_qkv", dqkv, h2, tm=QKV_DIM, tn=1024, tk=2048)
    red_attn = _Reduction("attn", [dw_qkv_t.reshape(N_SHARD, QKV_DIM // N_SHARD, Dm),
                                   dw_o.reshape(N_SHARD, ATTN_DIM // N_SHARD, Dm)], place)
    dx2, dx2_bf, dg_attn, _ = _mm("attn_dx", dqkv, w_qkv_t, nt=False, tm=1024, tn=1024, rows=256,
                                  ep_in=((x2, "tile"), (attn_norm_f, "row"), (dx3, "tile")), ep_fn=_rms_bwd_ep,
                                  outs=RMS_BWD_OUTS, deps=(red_attn.token,))

    dx1, dx1_bf, dg_mlp0, db_out, dup0 = _mlp_bwd(0, dx2, dx2_bf, x1, mlp_norm_g[0:1], up0, w_up_sm[0], w_down_f[0])
    dw_down0 = _mm_tn("mlp0_dw_down", up0, dx2_bf, tm=1024, tn=1024, tk=2048, a_fn=_relu2)
    dw_up0 = _mm_tn("mlp0_dw_up", h1, dup0, tm=1024, tn=1024, tk=2048, out_sm=N_SHARD)
    dw_out = _mm_tn("conv_dw_out", s_act, dx1_bf, tm=1024, tn=1024, tk=2048)
    red_mlp0 = _Reduction("mlp0", [dw_up0, dw_down0.reshape(N_SHARD, D_FF // N_SHARD, Dm),
                                   dw_out.reshape(N_SHARD, Dm // N_SHARD, Dm)], place)
    (r_qkv, r_o) = red_attn.finish((dx1,))
    (r_up1, r_down1) = red_mlp1.finish((dx1,))

    dcv, dln_g, dln_b, ddw_b = _mm("conv_ds", dx1_bf, w_out_f, nt=True, tm=1024, tn=1024, rows=256,
                                   ep_in=((cv, "tile"), (conv_ln_g, "row"), (conv_ln_b, "row")),
                                   ep_fn=_ln_silu_bwd_ep,
                                   outs=(("tile", F32), ("colsum", F32), ("colsum", F32), ("colsum", F32)),
                                   deps=(red_mlp0.token,))
    du, db_in, ddw8 = _conv_bwd(u, dcv, dw_pad)
    (r_up0, r_down0, r_out) = red_mlp0.finish((du,))
    early = [r_out, r_qkv, r_o, r_up0, r_up1, r_down0, r_down1]
    join_sems, early, early_lands, join_token = _join_start("join_start", early)
    dw_in = _mm_tn("conv_dw_in", h0, du, tm=1024, tn=512, tk=4096, out_sm=N_SHARD)
    red_conv = _Reduction("conv", [dw_in], place)
    def first_layer_ep(*args):
        tot, _, dg, _ = _rms_bwd_ep(*args)
        return tot, dg

    gx, dg_conv = _mm("conv_dx", du, w_in_sm, nt=True, b_sm=True, tm=1024, tn=1024, rows=256,
                      ep_in=((x2d, "tile"), (conv_norm_g, "row"), (dx1, "tile")), ep_fn=first_layer_ep,
                      outs=(("tile", F32), ("colsum", F32)), deps=(red_conv.token, join_token))
    (r_in,) = red_conv.finish((gx,))

    dqg = dqg_t.reshape(N_HEADS, HEAD_DIM).sum(axis=0, keepdims=True)
    dkg = dkg_t.reshape(N_KV, HEAD_DIM).sum(axis=0, keepdims=True)
    small_full = [dg_conv, db_in, ddw8.sum(axis=1)[:CONV_W], ddw_b, dln_g, dln_b, db_out, dg_attn, db_qkv, dqg, dkg,
                  dsinks[None, :], db_o, drel.reshape(1, REL_BUCKETS * N_HEADS),
                  jnp.pad(dg_mlp0, ((0, 1), (0, 0))) + jnp.pad(dg_mlp1, ((1, 0), (0, 0))), sq]
    (sg_sems,), (sg_land,), sg_token = _gather_start(
        "small_grads_start", [own_slot(_pack(small_full), 8, me)], ((0,),), ALL_OTHERS, after=())

    early, early_sibling = _join_wait("join_wait", early, early_lands, join_sems, (gx, sg_token))
    r_out, r_qkv, r_o, r_up0, r_up1, r_down0, r_down1 = zip(early, early_sibling)
    r_in = (r_in,) + tuple(_join_halves("join_halves", [r_in], deps=(sg_token,)))

    big_out = {}
    qkv_t = [jnp.swapaxes(a, 1, 2) for a in (w_qkv, m_w_qkv, v_w_qkv)]
    for nm, w, m, v, gs in (("conv_w_in", conv_w_in, m_conv_w_in, v_conv_w_in, (r_in,)),
                            ("conv_w_out", conv_w_out, m_conv_w_out, v_conv_w_out, (r_out,)),
                            ("w_qkv", *qkv_t, (r_qkv,)),
                            ("w_o", w_o, m_w_o, v_w_o, (r_o,)),
                            ("w_up", w_up, m_w_up, v_w_up, (r_up0, r_up1)),
                            ("w_down", w_down, m_w_down, v_w_down, (r_down0, r_down1))):
        big_out[nm] = _adamw(f"adamw_{nm}", w, m, v, gs)

    (sg_land,) = _gather_wait("small_grads_wait", [sg_land], sg_sems, ALL_OTHERS,
                              [big_out[nm][0] for nm in big_out])
    big_out["w_qkv"] = tuple(jnp.swapaxes(a, 1, 2) for a in big_out["w_qkv"])
    small_sum = _sum8("small_grads_sum", sg_land)
    (r_norm, r_b_in, r_dw, r_dw_b, r_ln_g, r_ln_b, r_b_out, r_attn_norm, r_b_qkv, r_qg, r_kg, r_sinks, r_b_o, r_rel,
     r_mlp_norm, r_sq) = _unpack(small_sum, [a.shape for a in small_full])
    loss = 0.5 * jnp.sum(r_sq) * (1.0 / Dm)

    def cols(a, width):
        return lax.dynamic_slice_in_dim(a, shard * width, width, axis=a.ndim - 1)

    small_names = ["conv_norm_g", "conv_b_in", "conv_dw", "conv_dw_b", "conv_ln_g", "conv_ln_b", "conv_b_out",
                   "attn_norm_g", "b_qkv", "q_norm_g", "k_norm_g", "sinks", "b_o", "rel_bias", "mlp_norm_g"]
    small_g = [r_norm, r_b_in, cols(r_dw, Dm // N_SHARD)[None], r_dw_b, r_ln_g, r_ln_b, r_b_out,
               cols(r_attn_norm, Dm // N_SHARD), cols(r_b_qkv, QKV_DIM // N_SHARD), r_qg, r_kg, r_sinks,
               cols(r_b_o, Dm // N_SHARD), r_rel.reshape(N_HEADS, REL_BUCKETS), r_mlp_norm]
    small_w = [conv_norm_g, conv_b_in, conv_dw, conv_dw_b, conv_ln_g, conv_ln_b, conv_b_out, attn_norm_g, b_qkv,
               q_norm_g, k_norm_g, sinks, b_o, rel_bias.T, mlp_norm_g]
    small_m = [m_conv_norm_g, m_conv_b_in, m_conv_dw, m_conv_dw_b, m_conv_ln_g, m_conv_ln_b, m_conv_b_out,
               m_attn_norm_g, m_b_qkv, m_q_norm_g, m_k_norm_g, m_sinks, m_b_o, m_rel_bias.T, m_mlp_norm_g]
    small_v = [v_conv_norm_g, v_conv_b_in, v_conv_dw, v_conv_dw_b, v_conv_ln_g, v_conv_ln_b, v_conv_b_out,
               v_attn_norm_g, v_b_qkv, v_q_norm_g, v_k_norm_g, v_sinks, v_b_o, v_rel_bias.T, v_mlp_norm_g]
    flat2 = lambda a: a.reshape(-1, a.shape[-1])
    small_g = [flat2(g) for g in small_g]
    d_s, m_s, v_s = _adamw_small([flat2(w) for w in small_w], small_g, [flat2(m) for m in small_m],
                                 [flat2(v) for v in small_v])
    small_out = {}
    for nm, w, g, d, m2, v2 in zip(small_names, small_w, small_g, d_s, m_s, v_s):
        small_out[nm] = tuple(a.reshape(w.shape) for a in (g, d, m2, v2))
    small_out["rel_bias"] = tuple(a.T for a in small_out["rel_bias"])

    order = ["conv_norm_g", "conv_w_in", "conv_b_in", "conv_dw", "conv_dw_b", "conv_ln_g", "conv_ln_b", "conv_w_out",
             "conv_b_out", "attn_norm_g", "w_qkv", "b_qkv", "q_norm_g", "k_norm_g", "sinks", "w_o", "b_o", "rel_bias",
             "mlp_norm_g", "w_up", "w_down"]
    res = {**small_out, **big_out}
    outs = [loss, gx[None]]
    for part in range(4):
        outs += [res[nm][part] for nm in order]
    return tuple(outs)
```

```python
import math

import numpy as np
import jax
import jax.numpy as jnp
from jax import lax
from jax.experimental import pallas as pl
from jax.experimental.pallas import tpu as pltpu

F32 = jnp.float32
BF = jnp.bfloat16
MESH = pl.DeviceIdType.MESH

D_MODEL = 1024
D_FF = 4096
N_HEADS = 16
N_KV = 2
GROUP = N_HEADS // N_KV
HEAD_DIM = 64
ATTN_DIM = N_HEADS * HEAD_DIM
KV_DIM = N_KV * HEAD_DIM
QKV_DIM = ATTN_DIM + 2 * KV_DIM
BLOCK = 128
CONV_W = 31
HALO = 32
REL_BUCKETS = 32
REL_MAX_DIST = 128
NORM_EPS = 1e-6
NEG_INF = -1e30
N_SHARD = 4
LANES = 1024

ADAM_LR = 0.001
ADAM_B1 = 0.9
ADAM_B2 = 0.999
ADAM_EPS = 1e-08
ADAM_WD = 0.01
ADAM_STEP = 10

VMEM_LIMIT = 56 * 1024 * 1024


def _params(n_axes):
    return pltpu.CompilerParams(dimension_semantics=("arbitrary",) * n_axes, vmem_limit_bytes=VMEM_LIMIT)


def _dot(a, b, ca, cb):
    return lax.dot_general(a, b, (((ca,), (cb,)), ((), ())), preferred_element_type=F32)


def _mm(name, a, b, *, nt, tm, tn, ep_fn, outs, a_fn=None, b_sm=False, ep_in=(), deps=(), rows=None):
    M, K = a.shape
    rows = tm if rows is None else rows
    if b_sm:
        S, ks = b.shape[0], b.shape[2]
        N, per = (b.shape[1], None) if nt else (S * b.shape[2], b.shape[2] // tn)
        assert (S * ks == K) if nt else (b.shape[1] == K)
    else:
        N = b.shape[0] if nt else b.shape[1]
        assert (b.shape[1] if nt else b.shape[0]) == K
    assert M % tm == 0 and N % tn == 0 and tm % rows == 0
    ne, no, nd = len(ep_in), len(outs), len(deps)

    def body(a_ref, b_ref, *rest):
        ep_refs, out_refs = rest[:ne], rest[ne + nd:ne + nd + no]
        i = pl.program_id(1)
        sums = [None] * no
        for r in range(tm // rows):
            rs = pl.ds(r * rows, rows)

            def lhs(cols):
                av = a_ref[rs, cols]
                return (av if a_fn is None else a_fn(av)).astype(BF)

            if b_sm and nt:
                acc = None
                for s in range(S):
                    part = _dot(lhs(pl.ds(s * ks, ks)), b_ref[s].astype(BF), 1, 1)
                    acc = part if acc is None else acc + part
            else:
                acc = _dot(lhs(slice(None)), b_ref[...].astype(BF), 1, 1 if nt else 0)
            ep_vals = [ref[rs, :] if kind == "tile" else ref[...] for ref, (_, kind) in zip(ep_refs, ep_in)]
            vals = ep_fn(acc, *ep_vals)
            for o, ((kind, dt, *_), ref, val) in enumerate(zip(outs, out_refs, vals)):
                if kind == "tile":
                    ref[rs, :] = val.astype(dt)
                else:
                    sums[o] = val if sums[o] is None else sums[o] + val
        for (kind, *_), ref, val in zip(outs, out_refs, sums):
            if kind == "colsum":
                @pl.when(i == 0)
                def _():
                    ref[...] = val

                @pl.when(i > 0)
                def _():
                    ref[...] += val

    if b_sm and nt:
        b_spec = pl.BlockSpec((S, tn, ks), lambda j, i: (0, j, 0))
    elif b_sm:
        b_spec = pl.BlockSpec((None, K, tn), lambda j, i: (j // per, 0, j % per))
    elif nt:
        b_spec = pl.BlockSpec((tn, K), lambda j, i: (j, 0))
    else:
        b_spec = pl.BlockSpec((K, tn), lambda j, i: (0, j))
    in_specs = [pl.BlockSpec((tm, K), lambda j, i: (i, 0)), b_spec]
    for arr, kind in ep_in:
        if kind == "tile":
            assert arr.shape == (M, N)
            in_specs.append(pl.BlockSpec((tm, tn), lambda j, i: (i, j)))
        elif kind == "whole":
            in_specs.append(pl.BlockSpec(arr.shape, lambda j, i, rank=arr.ndim: (0,) * rank))
        else:
            assert arr.shape == (1, N)
            in_specs.append(pl.BlockSpec((1, tn), lambda j, i: (0, j)))
    in_specs += [pl.BlockSpec(memory_space=pl.ANY)] * nd
    out_shape, out_specs = [], []
    for kind, dt, *width in outs:
        if kind == "tile" and width:
            assert tn == N
            out_shape.append(jax.ShapeDtypeStruct((M, width[0]), dt))
            out_specs.append(pl.BlockSpec((tm, width[0]), lambda j, i: (i, 0)))
        elif kind == "tile":
            out_shape.append(jax.ShapeDtypeStruct((M, N), dt))
            out_specs.append(pl.BlockSpec((tm, tn), lambda j, i: (i, j)))
        else:
            out_shape.append(jax.ShapeDtypeStruct((1, N), F32))
            out_specs.append(pl.BlockSpec((1, tn), lambda j, i: (0, j)))
    return pl.pallas_call(
        body, name=name, grid=(N // tn, M // tm), in_specs=in_specs, out_specs=out_specs, out_shape=out_shape,
        compiler_params=_params(2),
    )(a, b, *[arr for arr, _ in ep_in], *deps)


def _mm_tn(name, a, b, *, tm, tn, tk, a_fn=None, out_sm=None):
    T, Ka = a.shape
    N = b.shape[1]
    assert b.shape[0] == T and T % tk == 0 and Ka % tm == 0 and N % tn == 0
    nk = T // tk

    def body(a_ref, b_ref, o_ref, acc_ref):
        k = pl.program_id(2)

        @pl.when(k == 0)
        def _():
            acc_ref[...] = jnp.zeros_like(acc_ref)

        av = a_ref[...]
        if a_fn is not None:
            av = a_fn(av)
        acc_ref[...] += _dot(av.astype(BF), b_ref[...].astype(BF), 0, 0)

        @pl.when(k == nk - 1)
        def _():
            o_ref[...] = acc_ref[...].astype(BF)

    if out_sm is None:
        out_shape = jax.ShapeDtypeStruct((Ka, N), BF)
        out_spec = pl.BlockSpec((tm, tn), lambda i, j, k: (i, j))
    else:
        per = (N // out_sm) // tn
        assert per * tn * out_sm == N
        out_shape = jax.ShapeDtypeStruct((out_sm, Ka, N // out_sm), BF)
        out_spec = pl.BlockSpec((None, tm, tn), lambda i, j, k: (j // per, i, j % per))
    return pl.pallas_call(
        body, name=name, grid=(Ka // tm, N // tn, nk),
        in_specs=[pl.BlockSpec((tk, tm), lambda i, j, k: (k, i)), pl.BlockSpec((tk, tn), lambda i, j, k: (k, j))],
        out_specs=out_spec, out_shape=out_shape, scratch_shapes=[pltpu.VMEM((tm, tn), F32)],
        compiler_params=_params(3),
    )(a, b)


def _relu2(v):
    r = jnp.maximum(v.astype(F32), 0.0)
    return r * r


def _rms_bwd_ep(dh, x, g, dres):
    rstd = lax.rsqrt(jnp.mean(x * x, axis=-1, keepdims=True) + NORM_EPS)
    xh = x * rstd
    dxh = dh * g
    dx = rstd * (dxh - xh * jnp.mean(dxh * xh, axis=-1, keepdims=True))
    tot = dres + dx
    return tot, tot, jnp.sum(dh * xh, axis=0, keepdims=True), jnp.sum(tot, axis=0, keepdims=True)


def _rms_fwd(name, x, g, tm=512, deps=()):
    T, Dm = x.shape

    def body(x_ref, g_ref, *rest):
        o_ref = rest[-1]
        xv = x_ref[...]
        rstd = lax.rsqrt(jnp.mean(xv * xv, axis=-1, keepdims=True) + NORM_EPS)
        o_ref[...] = (xv * rstd * g_ref[...]).astype(BF)

    return pl.pallas_call(
        body, name=name, grid=(T // tm,),
        in_specs=[pl.BlockSpec((tm, Dm), lambda i: (i, 0)), pl.BlockSpec((1, Dm), lambda i: (0, 0))]
        + [pl.BlockSpec(memory_space=pl.ANY)] * len(deps),
        out_specs=pl.BlockSpec((tm, Dm), lambda i: (i, 0)), out_shape=jax.ShapeDtypeStruct((T, Dm), BF),
        compiler_params=_params(1),
    )(x, g, *deps)


HEAD_GROUP = 256


def _head_sum(v, ones):
    n = v.shape[1]
    w = min(n, HEAD_GROUP)
    blk = ones[:w, :w]
    parts = [_dot(v[:, c:c + w].astype(BF), blk, 1, 0) for c in range(0, n, w)]
    return parts[0] if len(parts) == 1 else jnp.concatenate(parts, axis=1)


def _head_ones():
    idx = np.arange(HEAD_GROUP) // HEAD_DIM
    return jnp.asarray((idx[:, None] == idx[None, :]).astype(np.float32), dtype=BF)


def _qk_normed(x, g, ones, scale):
    r = lax.rsqrt(_head_sum(x * x, ones) * (1.0 / HEAD_DIM) + NORM_EPS)
    return x * r * g * scale


def _qk_norm_bwd(qkv, dqn, dkn, dv, qg_t, kg_t, tm=256):
    T = qkv.shape[0]

    def body(x_ref, dq_ref, dk_ref, dv_ref, qg_ref, kg_ref, ones_ref, o_ref, db_ref, dqg_ref, dkg_ref):
        i = pl.program_id(0)
        ones = ones_ref[...]

        def one(x, dy, g):
            r = lax.rsqrt(_head_sum(x * x, ones) * (1.0 / HEAD_DIM) + NORM_EPS)
            xh = x * r
            dxh = dy * g
            dx = r * (dxh - xh * (_head_sum(dxh * xh, ones) * (1.0 / HEAD_DIM)))
            return dx, jnp.sum(dy * xh, axis=0, keepdims=True)

        dq, dqg = one(x_ref[:, pl.ds(0, ATTN_DIM)], dq_ref[...], qg_ref[...])
        dk, dkg = one(x_ref[:, pl.ds(ATTN_DIM, KV_DIM)], dk_ref[...], kg_ref[...])
        dvv = dv_ref[...]
        o_ref[:, pl.ds(0, ATTN_DIM)] = dq.astype(BF)
        o_ref[:, pl.ds(ATTN_DIM, KV_DIM)] = dk.astype(BF)
        o_ref[:, pl.ds(ATTN_DIM + KV_DIM, KV_DIM)] = dvv.astype(BF)
        sq, sk, sv = (jnp.sum(t, axis=0, keepdims=True) for t in (dq, dk, dvv))

        @pl.when(i == 0)
        def _():
            db_ref[:, pl.ds(0, ATTN_DIM)] = sq
            db_ref[:, pl.ds(ATTN_DIM, KV_DIM)] = sk
            db_ref[:, pl.ds(ATTN_DIM + KV_DIM, KV_DIM)] = sv
            dqg_ref[...] = dqg
            dkg_ref[...] = dkg

        @pl.when(i > 0)
        def _():
            db_ref[:, pl.ds(0, ATTN_DIM)] += sq
            db_ref[:, pl.ds(ATTN_DIM, KV_DIM)] += sk
            db_ref[:, pl.ds(ATTN_DIM + KV_DIM, KV_DIM)] += sv
            dqg_ref[...] += dqg
            dkg_ref[...] += dkg

    full = lambda shape: pl.BlockSpec(shape, lambda i: (0, 0))
    row = lambda n: pl.BlockSpec((tm, n), lambda i: (i, 0))
    return pl.pallas_call(
        body, name="qk_norm_bwd", grid=(T // tm,),
        in_specs=[row(QKV_DIM), row(ATTN_DIM), row(KV_DIM), row(KV_DIM), full((1, ATTN_DIM)), full((1, KV_DIM)),
                  full((HEAD_GROUP, HEAD_GROUP))],
        out_specs=[row(QKV_DIM), full((1, QKV_DIM)), full((1, ATTN_DIM)), full((1, KV_DIM))],
        out_shape=[jax.ShapeDtypeStruct((T, QKV_DIM), BF), jax.ShapeDtypeStruct((1, QKV_DIM), F32),
                   jax.ShapeDtypeStruct((1, ATTN_DIM), F32), jax.ShapeDtypeStruct((1, KV_DIM), F32)],
        compiler_params=_params(1),
    )(qkv, dqn, dkn, dv, qg_t, kg_t, _head_ones())


ROWS = 128
COLS = 128


SUBLANES = 8
FIRST_TAP = HALO - (CONV_W - 1)


def _glu(a, g):
    return a.astype(F32) * jax.nn.sigmoid(g.astype(F32))


def _shifted(xe, s):
    return xe if s == 0 else pltpu.roll(xe, ROWS + HALO - s, axis=0)


def _conv_fwd(u, dw_pad, dw_b, ln_g, ln_b, tm=512):
    T = u.shape[0]
    Dm = D_MODEL
    hpt = tm // HALO

    def body(ac_ref, gc_ref, ap_ref, gp_ref, w_ref, wb_ref, lg_ref, lb_ref, cv_ref, s_ref, ext):
        i = pl.program_id(0)
        ext[pl.ds(0, HALO), :] = jnp.where(i > 0, _glu(ap_ref[...], gp_ref[...]), 0.0)
        ext[pl.ds(HALO, tm), :] = _glu(ac_ref[...], gc_ref[...])

        def rows(r, carry):
            r0 = pl.multiple_of(r * ROWS, ROWS)
            for c in range(Dm // COLS):
                cs = pl.ds(c * COLS, COLS)
                xe = ext[pl.ds(r0, ROWS + HALO), cs]
                acc = jnp.zeros((ROWS, COLS), F32)
                for s in range(SUBLANES):
                    xs = _shifted(xe, s)
                    for j in range(CONV_W):
                        off = FIRST_TAP + j
                        if off % SUBLANES == s:
                            acc = acc + xs[off - s:off - s + ROWS, :] * w_ref[pl.ds(j, 1), cs]
                cv_ref[pl.ds(r0, ROWS), cs] = acc + wb_ref[:, cs]
            return carry

        lax.fori_loop(0, tm // ROWS, rows, 0)
        cv = cv_ref[...]
        xc = cv - jnp.mean(cv, axis=-1, keepdims=True)
        y = xc * lax.rsqrt(jnp.mean(xc * xc, axis=-1, keepdims=True) + NORM_EPS) * lg_ref[...] + lb_ref[...]
        s_ref[...] = (y * jax.nn.sigmoid(y)).astype(BF)

    full = lambda shape: pl.BlockSpec(shape, lambda i: (0, 0))
    return pl.pallas_call(
        body, name="conv_fwd", grid=(T // tm,),
        in_specs=[pl.BlockSpec((tm, Dm), lambda i: (i, 0)), pl.BlockSpec((tm, Dm), lambda i: (i, 1)),
                  pl.BlockSpec((HALO, Dm), lambda i: (jnp.maximum(i * hpt - 1, 0), 0)),
                  pl.BlockSpec((HALO, Dm), lambda i: (jnp.maximum(i * hpt - 1, 0), 1)),
                  full((HALO, Dm)), full((1, Dm)), full((1, Dm)), full((1, Dm))],
        out_specs=[pl.BlockSpec((tm, Dm), lambda i: (i, 0)), pl.BlockSpec((tm, Dm), lambda i: (i, 0))],
        out_shape=[jax.ShapeDtypeStruct((T, Dm), F32), jax.ShapeDtypeStruct((T, Dm), BF)],
        scratch_shapes=[pltpu.VMEM((tm + HALO, Dm), F32)],
        compiler_params=_params(1),
    )(u, u, u, u, dw_pad, dw_b, ln_g, ln_b)


def _ln_silu_bwd_ep(ds, cv, lg, lb):
    xc = cv - jnp.mean(cv, axis=-1, keepdims=True)
    rstd = lax.rsqrt(jnp.mean(xc * xc, axis=-1, keepdims=True) + NORM_EPS)
    xh = xc * rstd
    y = xh * lg + lb
    sg = jax.nn.sigmoid(y)
    dy = ds * (sg * (1.0 + y * (1.0 - sg)))
    dxh = dy * lg
    dcv = rstd * (dxh - jnp.mean(dxh, axis=-1, keepdims=True) - xh * jnp.mean(dxh * xh, axis=-1, keepdims=True))
    return (dcv, jnp.sum(dy * xh, axis=0, keepdims=True), jnp.sum(dy, axis=0, keepdims=True),
            jnp.sum(dcv, axis=0, keepdims=True))


def _conv_bwd(u, dcv, dw_pad, tm=512):
    T = u.shape[0]
    Dm = D_MODEL
    hpt = tm // HALO
    last = T // HALO - 1
    nt = T // tm

    def body(ac_ref, gc_ref, ap_ref, gp_ref, dc_ref, dn_ref, w_ref, du_ref, db_ref, dw_ref, ext_g, ext_d):
        i = pl.program_id(0)
        ext_g[pl.ds(0, HALO), :] = jnp.where(i > 0, _glu(ap_ref[...], gp_ref[...]), 0.0)
        ext_g[pl.ds(HALO, tm), :] = _glu(ac_ref[...], gc_ref[...])
        ext_d[pl.ds(0, tm), :] = dc_ref[...]
        ext_d[pl.ds(tm, HALO), :] = jnp.where(i < nt - 1, dn_ref[...], 0.0)

        @pl.when(i == 0)
        def _():
            db_ref[...] = jnp.zeros_like(db_ref)
            dw_ref[...] = jnp.zeros_like(dw_ref)

        def rows(r, carry):
            r0 = pl.multiple_of(r * ROWS, ROWS)
            rs = pl.ds(r0, ROWS)
            for c in range(Dm // COLS):
                cs = pl.ds(c * COLS, COLS)
                cs2 = pl.ds(Dm + c * COLS, COLS)
                de = ext_d[pl.ds(r0, ROWS + HALO), cs]
                ge = ext_g[pl.ds(r0, ROWS + HALO), cs]
                dcur = de[0:ROWS, :]
                acc = jnp.zeros((ROWS, COLS), F32)
                for s in range(SUBLANES):
                    ds_, gs_ = _shifted(de, s), _shifted(ge, s)
                    for j in range(CONV_W):
                        off = CONV_W - 1 - j
                        if off % SUBLANES == s:
                            acc = acc + ds_[off - s:off - s + ROWS, :] * w_ref[pl.ds(j, 1), cs]
                        goff = FIRST_TAP + j
                        if goff % SUBLANES == s:
                            prod = dcur * gs_[goff - s:goff - s + ROWS, :]
                            dw_ref[j, :, cs] += jnp.sum(prod.reshape(ROWS // SUBLANES, SUBLANES, COLS), axis=0)
                a = ac_ref[rs, cs].astype(F32)
                sg = jax.nn.sigmoid(gc_ref[rs, cs].astype(F32))
                da = acc * sg
                dg = acc * a * sg * (1.0 - sg)
                du_ref[rs, cs] = da.astype(BF)
                du_ref[rs, cs2] = dg.astype(BF)
                db_ref[:, cs] += jnp.sum(da, axis=0, keepdims=True)
                db_ref[:, cs2] += jnp.sum(dg, axis=0, keepdims=True)
            return carry

        lax.fori_loop(0, tm // ROWS, rows, 0)

    return pl.pallas_call(
        body, name="conv_bwd", grid=(nt,),
        in_specs=[pl.BlockSpec((tm, Dm), lambda i: (i, 0)), pl.BlockSpec((tm, Dm), lambda i: (i, 1)),
                  pl.BlockSpec((HALO, Dm), lambda i: (jnp.maximum(i * hpt - 1, 0), 0)),
                  pl.BlockSpec((HALO, Dm), lambda i: (jnp.maximum(i * hpt - 1, 0), 1)),
                  pl.BlockSpec((tm, Dm), lambda i: (i, 0)),
                  pl.BlockSpec((HALO, Dm), lambda i: (jnp.minimum((i + 1) * hpt, last), 0)),
                  pl.BlockSpec((HALO, Dm), lambda i: (0, 0))],
        out_specs=[pl.BlockSpec((tm, 2 * Dm), lambda i: (i, 0)), pl.BlockSpec((1, 2 * Dm), lambda i: (0, 0)),
                   pl.BlockSpec((HALO, 8, Dm), lambda i: (0, 0, 0))],
        out_shape=[jax.ShapeDtypeStruct((T, 2 * Dm), BF), jax.ShapeDtypeStruct((1, 2 * Dm), F32),
                   jax.ShapeDtypeStruct((HALO, 8, Dm), F32)],
        scratch_shapes=[pltpu.VMEM((tm + HALO, Dm), F32), pltpu.VMEM((tm + HALO, Dm), F32)],
        compiler_params=_params(1),
    )(u, u, u, u, dcv, dcv, dw_pad)


def _bucket_table():
    q_loc = np.arange(BLOCK)[:, None]
    k_loc = np.arange(2 * BLOCK)[None, :]
    dist = q_loc + BLOCK - k_loc
    n = np.maximum(dist, 0)
    max_exact = REL_BUCKETS // 2
    large = max_exact + (np.log(np.maximum(n, 1).astype(np.float32) / max_exact)
                         / math.log(REL_MAX_DIST / max_exact) * (REL_BUCKETS - max_exact)).astype(np.int32)
    large = np.minimum(large, REL_BUCKETS - 1)
    bucket = np.where(n < max_exact, n, large).astype(np.int32)
    band = np.where((dist >= 0) & (dist < BLOCK), bucket, -1)
    folded = np.where(np.arange(BLOCK)[None, :] > q_loc, band[:, :BLOCK], band[:, BLOCK:])
    assert (folded >= 0).all() and ((band[:, :BLOCK] >= 0) != (band[:, BLOCK:] >= 0)).all()
    return jnp.asarray(folded.astype(np.int32))


def _prev_mask():
    row = lax.broadcasted_iota(jnp.int32, (BLOCK, BLOCK), 0)
    col = lax.broadcasted_iota(jnp.int32, (BLOCK, BLOCK), 1)
    return col > row


def _fold(band, prev_mask):
    return jnp.where(prev_mask, band[:, :BLOCK], band[:, BLOCK:])


def _unfold(ref, g, rows, folded, prev_mask):
    ref[g, rows, pl.ds(0, BLOCK)] = jnp.where(prev_mask, folded, 0.0).astype(ref.dtype)
    ref[g, rows, pl.ds(BLOCK, BLOCK)] = jnp.where(prev_mask, 0.0, folded).astype(ref.dtype)


def _bias_table(rel_bias_t, bucket):
    def body(rb_ref, bk_ref, o_ref):
        bk = bk_ref[...]
        prev_mask = _prev_mask()
        for h in range(N_HEADS):
            acc = jnp.zeros((BLOCK, BLOCK), F32)
            for b in range(REL_BUCKETS):
                acc = jnp.where(bk == b, rb_ref[h, b], acc)
            o_ref[0, h] = acc
            o_ref[1, h] = jnp.where(prev_mask, NEG_INF, acc)

    return pl.pallas_call(
        body, name="bias_table", out_shape=jax.ShapeDtypeStruct((2, N_HEADS, BLOCK, BLOCK), F32),
        in_specs=[pl.BlockSpec(memory_space=pltpu.SMEM), pl.BlockSpec(memory_space=pltpu.VMEM)],
        out_specs=pl.BlockSpec(memory_space=pltpu.VMEM),
    )(rel_bias_t, bucket)


def _bias_grad(dbias, bucket):
    def body(db_ref, bk_ref, o_ref):
        bk = bk_ref[...]
        for b in range(REL_BUCKETS):
            sel = bk == b
            for h in range(N_HEADS):
                o_ref[h, b] = jnp.sum(jnp.where(sel, db_ref[h], 0.0))

    return pl.pallas_call(
        body, name="bias_grad", out_shape=jax.ShapeDtypeStruct((N_HEADS, REL_BUCKETS), F32),
        in_specs=[pl.BlockSpec(memory_space=pltpu.VMEM), pl.BlockSpec(memory_space=pltpu.VMEM)],
        out_specs=pl.BlockSpec(memory_space=pltpu.SMEM),
    )(dbias, bucket)


GROUP_ROWS = GROUP * BLOCK
BIAS_SPEC = pl.BlockSpec((2, N_HEADS, BLOCK, BLOCK), lambda n: (0, 0, 0, 0))


def _head_probs(qk, bias_h, sink, prev_mask):
    s = _fold(qk, prev_mask) + bias_h
    m = jnp.maximum(jnp.max(s, axis=-1, keepdims=True), sink)
    p = jnp.exp(s - m)
    ps = jnp.exp(sink - m)
    inv = 1.0 / (jnp.sum(p, axis=-1, keepdims=True) + ps)
    return p * inv, ps * inv


def _band(prev_ref, cur_ref, g):
    hs = pl.ds(g * HEAD_DIM, HEAD_DIM)
    return jnp.concatenate([prev_ref[:, hs], cur_ref[:, hs]], axis=0)


def _stack_heads(ref, g):
    return jnp.concatenate([ref[:, pl.ds((g * GROUP + hh) * HEAD_DIM, HEAD_DIM)] for hh in range(GROUP)], axis=0)


def _unstack_heads(ref, g, stacked, dtype):
    for hh in range(GROUP):
        ref[:, pl.ds((g * GROUP + hh) * HEAD_DIM, HEAD_DIM)] = stacked[hh * BLOCK:(hh + 1) * BLOCK, :].astype(dtype)


def _head_rows(hh):
    return pl.ds(hh * BLOCK, BLOCK)


def _attn_fwd(qn, kn, vv, bias, sinks):
    T = qn.shape[0]
    nb = T // BLOCK

    def body(sk_ref, q_ref, kc_ref, kp_ref, vc_ref, vp_ref, b_ref, o_ref, qk_buf, p_buf):
        table = (pl.program_id(0) == 0).astype(jnp.int32)
        prev_mask = _prev_mask()
        for g in range(N_KV):
            qk_buf[g] = _dot(_stack_heads(q_ref, g), _band(kp_ref, kc_ref, g), 1, 1)
        for g in range(N_KV):
            for hh in range(GROUP):
                h = g * GROUP + hh
                pn, _ = _head_probs(qk_buf[g, _head_rows(hh), :], b_ref[table, h], sk_ref[h], prev_mask)
                _unfold(p_buf, g, _head_rows(hh), pn, prev_mask)
        for g in range(N_KV):
            _unstack_heads(o_ref, g, _dot(p_buf[g], _band(vp_ref, vc_ref, g), 1, 0), BF)

    cur = lambda n: (n, 0)
    prev = lambda n: (jnp.maximum(n - 1, 0), 0)
    return pl.pallas_call(
        body, name="attn_fwd", grid=(nb,),
        in_specs=[pl.BlockSpec(memory_space=pltpu.SMEM), pl.BlockSpec((BLOCK, ATTN_DIM), cur),
                  pl.BlockSpec((BLOCK, KV_DIM), cur), pl.BlockSpec((BLOCK, KV_DIM), prev),
                  pl.BlockSpec((BLOCK, KV_DIM), cur), pl.BlockSpec((BLOCK, KV_DIM), prev), BIAS_SPEC],
        out_specs=pl.BlockSpec((BLOCK, ATTN_DIM), cur), out_shape=jax.ShapeDtypeStruct((T, ATTN_DIM), BF),
        scratch_shapes=[pltpu.VMEM((N_KV, GROUP_ROWS, 2 * BLOCK), F32), pltpu.VMEM((N_KV, GROUP_ROWS, 2 * BLOCK), BF)],
        compiler_params=_params(1),
    )(sinks, qn, kn, kn, vv, vv, bias)


def _attn_bwd(qn, kn, vv, bias, sinks, do):
    T = qn.shape[0]
    nb = T // BLOCK
    scale = 1.0 / math.sqrt(HEAD_DIM)

    def body(sk_ref, q_ref, kc_ref, kp_ref, vc_ref, vp_ref, b_ref, do_ref,
             dq_ref, dk_ref, dv_ref, db_ref, dsk_ref, dk_full, dv_full, dk_carry, dv_carry, qk_buf, dp_buf, p_buf, ds_buf):
        n = pl.program_id(0)

        @pl.when(n == 0)
        def _():
            db_ref[...] = jnp.zeros_like(db_ref)
            dk_carry[...] = jnp.zeros_like(dk_carry)
            dv_carry[...] = jnp.zeros_like(dv_carry)
            for h in range(N_HEADS):
                dsk_ref[h] = 0.0

        @pl.when(n < nb)
        def _():
            table = (n == 0).astype(jnp.int32)
            prev_mask = _prev_mask()
            ks = [_band(kp_ref, kc_ref, g) for g in range(N_KV)]
            qs = [_stack_heads(q_ref, g) for g in range(N_KV)]
            douts = [_stack_heads(do_ref, g) for g in range(N_KV)]
            for g in range(N_KV):
                qk_buf[g] = _dot(qs[g], ks[g], 1, 1)
                dp_buf[g] = _dot(douts[g], _band(vp_ref, vc_ref, g), 1, 1)
            for g in range(N_KV):
                for hh in range(GROUP):
                    h = g * GROUP + hh
                    rows = _head_rows(hh)
                    pn, psink = _head_probs(qk_buf[g, rows, :], b_ref[table, h], sk_ref[h], prev_mask)
                    dp = _fold(dp_buf[g, rows, :], prev_mask)
                    delta = jnp.sum(pn * dp, axis=-1, keepdims=True)
                    ds = pn * (dp - delta)
                    dsk_ref[h] += -jnp.sum(psink * delta)
                    db_ref[h] += ds
                    _unfold(ds_buf, g, rows, ds, prev_mask)
                    _unfold(p_buf, g, rows, pn, prev_mask)
            for g in range(N_KV):
                dsb = ds_buf[g]
                _unstack_heads(dq_ref, g, _dot(dsb, ks[g], 1, 0) * scale, F32)
                gs = pl.ds(g * HEAD_DIM, HEAD_DIM)
                dk_full[:, gs] = _dot(dsb, qs[g], 0, 0)
                dv_full[:, gs] = _dot(p_buf[g], douts[g], 0, 0)

        @pl.when(n == nb)
        def _():
            dk_full[...] = jnp.zeros_like(dk_full)
            dv_full[...] = jnp.zeros_like(dv_full)

        dk_ref[...] = dk_carry[...] + dk_full[pl.ds(0, BLOCK), :]
        dv_ref[...] = dv_carry[...] + dv_full[pl.ds(0, BLOCK), :]
        dk_carry[...] = dk_full[pl.ds(BLOCK, BLOCK), :]
        dv_carry[...] = dv_full[pl.ds(BLOCK, BLOCK), :]

    cur = lambda n: (jnp.minimum(n, nb - 1), 0)
    prev = lambda n: (jnp.maximum(jnp.minimum(n, nb - 1) - 1, 0), 0)
    out_kv = lambda n: (jnp.maximum(n - 1, 0), 0)
    return pl.pallas_call(
        body, name="attn_bwd", grid=(nb + 1,),
        in_specs=[pl.BlockSpec(memory_space=pltpu.SMEM), pl.BlockSpec((BLOCK, ATTN_DIM), cur),
                  pl.BlockSpec((BLOCK, KV_DIM), cur), pl.BlockSpec((BLOCK, KV_DIM), prev),
                  pl.BlockSpec((BLOCK, KV_DIM), cur), pl.BlockSpec((BLOCK, KV_DIM), prev), BIAS_SPEC,
                  pl.BlockSpec((BLOCK, ATTN_DIM), cur)],
        out_specs=[pl.BlockSpec((BLOCK, ATTN_DIM), cur), pl.BlockSpec((BLOCK, KV_DIM), out_kv),
                   pl.BlockSpec((BLOCK, KV_DIM), out_kv),
                   pl.BlockSpec((N_HEADS, BLOCK, BLOCK), lambda n: (0, 0, 0)),
                   pl.BlockSpec(memory_space=pltpu.SMEM)],
        out_shape=[jax.ShapeDtypeStruct((T, ATTN_DIM), F32), jax.ShapeDtypeStruct((T, KV_DIM), F32),
                   jax.ShapeDtypeStruct((T, KV_DIM), F32),
                   jax.ShapeDtypeStruct((N_HEADS, BLOCK, BLOCK), F32), jax.ShapeDtypeStruct((N_HEADS,), F32)],
        scratch_shapes=[pltpu.VMEM((2 * BLOCK, KV_DIM), F32), pltpu.VMEM((2 * BLOCK, KV_DIM), F32),
                        pltpu.VMEM((BLOCK, KV_DIM), F32), pltpu.VMEM((BLOCK, KV_DIM), F32),
                        pltpu.VMEM((N_KV, GROUP_ROWS, 2 * BLOCK), F32), pltpu.VMEM((N_KV, GROUP_ROWS, 2 * BLOCK), F32),
                        pltpu.VMEM((N_KV, GROUP_ROWS, 2 * BLOCK), BF), pltpu.VMEM((N_KV, GROUP_ROWS, 2 * BLOCK), BF)],
        compiler_params=_params(1),
    )(sinks, qn, kn, kn, vv, vv, bias, do)


def _coords():
    return lax.axis_index("x"), lax.axis_index("y"), lax.axis_index("c")


def _sum8(name, blocks):
    def body(b_ref, o_ref):
        tot = b_ref[0]
        for d in range(1, 8):
            tot = tot + b_ref[d]
        o_ref[...] = tot

    return pl.pallas_call(body, name=name, out_shape=jax.ShapeDtypeStruct(blocks.shape[1:], F32))(blocks)


HBM_SPEC = pl.BlockSpec(memory_space=pltpu.HBM)
SEM_SPEC = pl.BlockSpec(memory_space=pltpu.SEMAPHORE)
ANY_SPEC = pl.BlockSpec(memory_space=pl.ANY)
DATAFLOW = pltpu.SideEffectType.DATAFLOW_SIDE_EFFECTING


OTHER_CHIPS = (4, 2, 6)
ALL_OTHERS = (1, 2, 3, 4, 5, 6, 7)


def _slot(x, y, c, peers):
    return 2 * x + y if peers is OTHER_CHIPS else 4 * x + 2 * y + c


def _slot_copy(land, sems, idx, x, y, c, k, peers, arriving):
    send_sems, recv_sems = sems
    px, py, pc = x ^ (k >> 2), y ^ ((k >> 1) & 1), c ^ (k & 1)
    mine = _slot(x, y, c, peers)
    dst = _slot(px, py, pc, peers) if arriving else mine
    return pltpu.make_async_remote_copy(src_ref=land.at[mine], dst_ref=land.at[dst], send_sem=send_sems.at[idx],
                                        recv_sem=recv_sems.at[idx], device_id=(px, py, pc), device_id_type=MESH)


def _gather_start(name, stacks, groups, peers, after):
    n = len(stacks)
    ng = len(groups)
    np_ = len(peers)
    after = tuple(after)

    def body(*refs):
        lands = refs[:n]
        first = n + len(after)
        sems = [(refs[first + 2 * g], refs[first + 2 * g + 1]) for g in range(ng)]
        token = refs[-1]
        x, y, c = _coords()
        for g, members in enumerate(groups):
            for i, t in enumerate(members):
                for j, k in enumerate(peers):
                    _slot_copy(lands[t], sems[g], np_ * i + j, x, y, c, k, peers, arriving=False).start()
        token[...] = jnp.zeros_like(token)

    out_shape = []
    for members in groups:
        out_shape += [pltpu.SemaphoreType.DMA((np_ * len(members),))] * 2
    out_shape += [pltpu.HBM(w.shape, w.dtype) for w in stacks]
    out_shape.append(jax.ShapeDtypeStruct((8, 128), F32))
    res = pl.pallas_call(
        body, name=name, out_shape=out_shape, in_specs=[HBM_SPEC] * n + [ANY_SPEC] * len(after),
        out_specs=[SEM_SPEC] * (2 * ng) + [HBM_SPEC] * n + [pl.BlockSpec(memory_space=pltpu.VMEM)],
        input_output_aliases={t: 2 * ng + t for t in range(n)},
        compiler_params=pltpu.CompilerParams(has_side_effects=DATAFLOW),
    )(*[pltpu.with_memory_space_constraint(w, pltpu.HBM) for w in stacks], *after)
    sems = [(res[2 * g], res[2 * g + 1]) for g in range(ng)]
    return sems, list(res[2 * ng:2 * ng + n]), res[-1]


def _gather_wait(name, stacks, sems, peers, after):
    n = len(stacks)
    after = tuple(after)

    def body(*refs):
        lands = refs[:n]
        group_sems = (refs[n], refs[n + 1])
        x, y, c = _coords()
        for i in range(n):
            for j, k in enumerate(peers):
                cp = _slot_copy(lands[i], group_sems, len(peers) * i + j, x, y, c, k, peers, arriving=True)
                cp.wait_send()
                cp.wait_recv()

    return pl.pallas_call(
        body, name=name, out_shape=[pltpu.HBM(w.shape, w.dtype) for w in stacks],
        in_specs=[HBM_SPEC] * n + [SEM_SPEC, SEM_SPEC] + [ANY_SPEC] * len(after), out_specs=[HBM_SPEC] * n,
        input_output_aliases={t: t for t in range(n)},
        compiler_params=pltpu.CompilerParams(has_side_effects=DATAFLOW),
    )(*stacks, sems[0], sems[1], *after)


N_PEERS = 7


def _peer(x, y, c, k):
    return x ^ (k >> 2), y ^ ((k >> 1) & 1), c ^ (k & 1)


def _reduce_copy(grad, land, sems, idx, x, y, c, k):
    px, py, pc = _peer(x, y, c, k)
    rh = grad.shape[1] // 2
    return pltpu.make_async_remote_copy(src_ref=grad.at[2 * px + py, pl.ds(pc * rh, rh), :], dst_ref=land.at[k - 1],
                                        send_sem=sems[0].at[idx], recv_sem=sems[1].at[idx], device_id=(px, py, pc),
                                        device_id_type=MESH)


def _reduce_start(name, grads):
    n = len(grads)

    def body(*refs):
        src, lands, sems, token = refs[:n], refs[n:2 * n], (refs[2 * n], refs[2 * n + 1]), refs[-1]
        x, y, c = _coords()
        for t in range(n):
            for k in range(1, N_PEERS + 1):
                _reduce_copy(src[t], lands[t], sems, N_PEERS * t + k - 1, x, y, c, k).start()
        token[...] = jnp.zeros_like(token)

    lands = [lax.empty((N_PEERS, g.shape[1] // 2, g.shape[2]), g.dtype) for g in grads]
    out_shape = [pltpu.SemaphoreType.DMA((N_PEERS * n,))] * 2
    out_shape += [pltpu.HBM(a.shape, a.dtype) for a in list(grads) + lands]
    out_shape.append(jax.ShapeDtypeStruct((8, 128), F32))
    res = pl.pallas_call(
        body, name=name, out_shape=out_shape, in_specs=[HBM_SPEC] * (2 * n),
        out_specs=[SEM_SPEC] * 2 + [HBM_SPEC] * (2 * n) + [pl.BlockSpec(memory_space=pltpu.VMEM)],
        input_output_aliases={t: 2 + t for t in range(2 * n)},
        compiler_params=pltpu.CompilerParams(has_side_effects=DATAFLOW),
    )(*[pltpu.with_memory_space_constraint(a, pltpu.HBM) for a in list(grads) + lands])
    return (res[0], res[1]), list(res[2:2 + n]), list(res[2 + n:2 + 2 * n]), res[-1]


def _reduce_wait(name, grads, lands, sems, after):
    n = len(grads)
    after = tuple(after)

    def body(*refs):
        src, dst, group_sems = refs[:n], refs[n:2 * n], (refs[2 * n], refs[2 * n + 1])
        x, y, c = _coords()
        for t in range(n):
            for k in range(1, N_PEERS + 1):
                cp = _reduce_copy(src[t], dst[t], group_sems, N_PEERS * t + k - 1, x, y, c, k)
                cp.wait_send()
                cp.wait_recv()

    res = pl.pallas_call(
        body, name=name, out_shape=[pltpu.HBM(a.shape, a.dtype) for a in list(grads) + list(lands)],
        in_specs=[HBM_SPEC] * (2 * n) + [SEM_SPEC, SEM_SPEC] + [ANY_SPEC] * len(after), out_specs=[HBM_SPEC] * (2 * n),
        input_output_aliases={t: t for t in range(2 * n)},
        compiler_params=pltpu.CompilerParams(has_side_effects=DATAFLOW),
    )(*grads, *lands, sems[0], sems[1], *after)
    return list(res[:n]), list(res[n:])


def _join_copy(half, land, sems, idx, x, y, c):
    return pltpu.make_async_remote_copy(src_ref=half, dst_ref=land, send_sem=sems[0].at[idx], recv_sem=sems[1].at[idx],
                                        device_id=(x, y, 1 - c), device_id_type=MESH)


def _join_start(name, halves):
    n = len(halves)

    def body(*refs):
        src, lands, sems, token = refs[:n], refs[n:2 * n], (refs[2 * n], refs[2 * n + 1]), refs[-1]
        x, y, c = _coords()
        for t in range(n):
            _join_copy(src[t], lands[t], sems, t, x, y, c).start()
        token[...] = jnp.zeros_like(token)

    lands = [lax.empty(h.shape, h.dtype) for h in halves]
    out_shape = [pltpu.SemaphoreType.DMA((n,))] * 2
    out_shape += [pltpu.HBM(a.shape, a.dtype) for a in list(halves) + lands]
    out_shape.append(jax.ShapeDtypeStruct((8, 128), F32))
    res = pl.pallas_call(
        body, name=name, out_shape=out_shape, in_specs=[HBM_SPEC] * (2 * n),
        out_specs=[SEM_SPEC] * 2 + [HBM_SPEC] * (2 * n) + [pl.BlockSpec(memory_space=pltpu.VMEM)],
        input_output_aliases={t: 2 + t for t in range(2 * n)},
        compiler_params=pltpu.CompilerParams(has_side_effects=DATAFLOW),
    )(*[pltpu.with_memory_space_constraint(a, pltpu.HBM) for a in list(halves) + lands])
    return (res[0], res[1]), list(res[2:2 + n]), list(res[2 + n:2 + 2 * n]), res[-1]


def _join_wait(name, halves, lands, sems, after):
    n = len(halves)
    after = tuple(after)

    def body(*refs):
        src, dst, group_sems = refs[:n], refs[n:2 * n], (refs[2 * n], refs[2 * n + 1])
        x, y, c = _coords()
        for t in range(n):
            cp = _join_copy(src[t], dst[t], group_sems, t, x, y, c)
            cp.wait_send()
            cp.wait_recv()

    res = pl.pallas_call(
        body, name=name, out_shape=[pltpu.HBM(a.shape, a.dtype) for a in list(halves) + list(lands)],
        in_specs=[HBM_SPEC] * (2 * n) + [SEM_SPEC, SEM_SPEC] + [ANY_SPEC] * len(after), out_specs=[HBM_SPEC] * (2 * n),
        input_output_aliases={t: t for t in range(2 * n)},
        compiler_params=pltpu.CompilerParams(has_side_effects=DATAFLOW),
    )(*halves, *lands, sems[0], sems[1], *after)
    return list(res[:n]), list(res[n:])


def _join_halves(name, halves, deps=()):
    n = len(halves)

    def body(*refs):
        src, dst = refs[:n], refs[n + len(deps):2 * n + len(deps)]
        send_sems, recv_sems = refs[-2:]
        x, y, c = _coords()
        cps = []
        for t in range(n):
            cp = pltpu.make_async_remote_copy(src_ref=src[t], dst_ref=dst[t], send_sem=send_sems.at[t],
                                              recv_sem=recv_sems.at[t], device_id=(x, y, 1 - c), device_id_type=MESH)
            cp.start()
            cps.append(cp)
        for cp in cps:
            cp.wait()

    anyspec = pl.BlockSpec(memory_space=pl.ANY)
    return pl.pallas_call(
        body, name=name, out_shape=[jax.ShapeDtypeStruct(h.shape, h.dtype) for h in halves],
        in_specs=[anyspec] * (n + len(deps)), out_specs=[anyspec] * n,
        scratch_shapes=[pltpu.SemaphoreType.DMA((n,)), pltpu.SemaphoreType.DMA((n,))],
    )(*halves, *deps)


BF16_ROWS = 16


def _row_block(rows, steps):
    for rb in range(rows // steps, 0, -1):
        if rows % rb == 0 and rb % BF16_ROWS == 0:
            return rb
    raise ValueError(rows)


def _sum_devices(name, grad, land, place):
    S, R, C = grad.shape
    rh = R // 2
    rb = _row_block(rh, 4)
    nbh = rh // rb

    def body(place_ref, g_ref, l_ref, o_ref):
        tot = g_ref[...].astype(F32)
        for k in range(N_PEERS):
            tot = tot + l_ref[k].astype(F32)
        o_ref[...] = tot

    return pl.pallas_call(
        body, name=name,
        grid_spec=pltpu.PrefetchScalarGridSpec(
            num_scalar_prefetch=1, grid=(nbh,),
            in_specs=[pl.BlockSpec((None, rb, C), lambda r, place: (place[0], place[1] * nbh + r, 0)),
                      pl.BlockSpec((N_PEERS, rb, C), lambda r, place: (0, r, 0))],
            out_specs=pl.BlockSpec((rb, C), lambda r, place: (r, 0))),
        out_shape=jax.ShapeDtypeStruct((rh, C), F32), compiler_params=_params(1),
    )(place, grad, land)


def _adamw_math(w, g, m, v):
    m2 = ADAM_B1 * m + (1.0 - ADAM_B1) * g
    v2 = ADAM_B2 * v + (1.0 - ADAM_B2) * (g * g)
    m_hat = m2 / (1.0 - ADAM_B1 ** ADAM_STEP)
    v_hat = v2 / (1.0 - ADAM_B2 ** ADAM_STEP)
    delta = -ADAM_LR * (m_hat / (jnp.sqrt(v_hat) + ADAM_EPS) + ADAM_WD * w)
    return delta, m2, v2


def _adamw(name, w, m, v, gs):
    L, R, C = w.shape
    Rh = R // 2
    rb = _row_block(Rh, 2)
    nbh = Rh // rb
    assert len(gs) == L

    def body(core_ref, w_ref, m_ref, v_ref, *rest):
        g_refs, (go_ref, d_ref, m2_ref, v2_ref) = rest[:2 * L], rest[2 * L:]
        layer, half = pl.program_id(0), pl.program_id(1)
        mine = half == core_ref[0]
        g = jnp.where(mine, g_refs[0][...], g_refs[1][...])
        for t in range(1, L):
            g = jnp.where(layer == t, jnp.where(mine, g_refs[2 * t][...], g_refs[2 * t + 1][...]), g)
        delta, m2, v2 = _adamw_math(w_ref[...], g, m_ref[...], v_ref[...])
        go_ref[...] = g
        d_ref[...] = delta
        m2_ref[...] = m2
        v2_ref[...] = v2

    wspec = pl.BlockSpec((None, rb, C), lambda l, h, r, core: (l, h * nbh + r, 0))
    gspec = pl.BlockSpec((rb, C), lambda l, h, r, core: (r, 0))
    return pl.pallas_call(
        body, name=name,
        grid_spec=pltpu.PrefetchScalarGridSpec(num_scalar_prefetch=1, grid=(L, 2, nbh),
                                               in_specs=[wspec] * 3 + [gspec] * (2 * L), out_specs=[wspec] * 4),
        out_shape=[jax.ShapeDtypeStruct((L, R, C), F32)] * 4, compiler_params=_params(3),
    )(lax.axis_index("c").astype(jnp.int32).reshape(1), w, m, v, *[g for pair in gs for g in pair])


def _adamw_small(ws, gs, ms, vs):
    n = len(ws)

    def body(*refs):
        w_refs, g_refs, m_refs, v_refs = (refs[k * n:(k + 1) * n] for k in range(4))
        d_refs, m2_refs, v2_refs = (refs[(4 + k) * n:(5 + k) * n] for k in range(3))
        for t in range(n):
            delta, m2, v2 = _adamw_math(w_refs[t][...], g_refs[t][...], m_refs[t][...], v_refs[t][...])
            d_refs[t][...] = delta
            m2_refs[t][...] = m2
            v2_refs[t][...] = v2

    res = pl.pallas_call(body, name="adamw_small", out_shape=[jax.ShapeDtypeStruct(w.shape, F32) for w in ws] * 3)(
        *ws, *gs, *ms, *vs)
    return res[:n], res[n:2 * n], res[2 * n:]


def _packed_rows(shape):
    c = shape[-1]
    return (int(np.prod(shape)) // c) * -(-c // LANES)


def _pack(arrays):
    total = sum(_packed_rows(a.shape) for a in arrays)
    total += -total % 8
    buf, r0 = None, 0
    for a in arrays:
        a = a.astype(F32).reshape(-1, a.shape[-1])
        r, c = a.shape
        k = -(-c // LANES)
        a = jnp.pad(a, ((0, 0), (0, k * LANES - c))).reshape(r * k, LANES)
        a = jnp.pad(a, ((r0, total - r0 - r * k), (0, 0)))
        buf = a if buf is None else buf + a
        r0 += r * k
    return buf


def _unpack(buf, shapes):
    out, r0 = [], 0
    for shp in shapes:
        c = shp[-1]
        rows = _packed_rows(shp)
        out.append(buf[r0:r0 + rows].reshape(-1, -(-c // LANES) * LANES)[:, :c].reshape(shp))
        r0 += rows
    return out


def _rms(x, g):
    return x * lax.rsqrt(jnp.mean(x * x, axis=-1, keepdims=True) + NORM_EPS) * g


def _residual_norm_ep(acc, *rest):
    *bias, res, gain = rest
    x = acc + res + (bias[0] if bias else 0.0)
    return x, _rms(x, gain)


RESIDUAL_NORM_OUTS = (("tile", F32), ("tile", BF))


def _mlp_up(tag, h, w_up_sm):
    (up,) = _mm(f"mlp{tag}_up", h, w_up_sm, nt=False, b_sm=True, tm=2048, tn=1024, rows=256,
                ep_fn=lambda acc: (acc,), outs=(("tile", BF),))
    return up


RMS_BWD_OUTS = (("tile", F32), ("tile", BF), ("colsum", F32), ("colsum", F32))


def _mlp_bwd(tag, dy, dy_bf, x, g, up, w_up_sm, w_down):
    (dup,) = _mm(f"mlp{tag}_dup", dy_bf, w_down, nt=True, tm=2048, tn=1024, rows=256, ep_in=((up, "tile"),),
                 ep_fn=lambda acc, u: (acc * (2.0 * jnp.maximum(u.astype(F32), 0.0)),), outs=(("tile", BF),))
    dx, dx_bf, dg, dx_sum = _mm(f"mlp{tag}_dx", dup, w_up_sm, nt=True, b_sm=True, tm=512, tn=1024, rows=256,
                                ep_in=((x, "tile"), (g, "row"), (dy, "tile")), ep_fn=_rms_bwd_ep, outs=RMS_BWD_OUTS)
    return dx, dx_bf, dg, dx_sum, dup


class _Reduction:
    def __init__(self, tag, grads, place):
        self.tag, self.place = tag, place
        self.sems, self.grads, self.lands, self.token = _reduce_start(f"reduce_start_{tag}", grads)

    def finish(self, after):
        grads, lands = _reduce_wait(f"reduce_wait_{self.tag}", self.grads, self.lands, self.sems, after)
        return [_sum_devices(f"reduce_sum_{self.tag}{i}", g, l, self.place) for i, (g, l) in enumerate(zip(grads, lands))]


def kernel(x, conv_norm_g, conv_w_in, conv_b_in, conv_dw, conv_dw_b, conv_ln_g, conv_ln_b, conv_w_out, conv_b_out, attn_norm_g, w_qkv, b_qkv, q_norm_g, k_norm_g, sinks, w_o, b_o, rel_bias, mlp_norm_g, w_up, w_down, loss_target, m_conv_norm_g, m_conv_w_in, m_conv_b_in, m_conv_dw, m_conv_dw_b, m_conv_ln_g, m_conv_ln_b, m_conv_w_out, m_conv_b_out, m_attn_norm_g, m_w_qkv, m_b_qkv, m_q_norm_g, m_k_norm_g, m_sinks, m_w_o, m_b_o, m_rel_bias, m_mlp_norm_g, m_w_up, m_w_down, v_conv_norm_g, v_conv_w_in, v_conv_b_in, v_conv_dw, v_conv_dw_b, v_conv_ln_g, v_conv_ln_b, v_conv_w_out, v_conv_b_out, v_attn_norm_g, v_w_qkv, v_b_qkv, v_q_norm_g, v_k_norm_g, v_sinks, v_w_o, v_b_o, v_rel_bias, v_mlp_norm_g, v_w_up, v_w_down):
    Dm = D_MODEL
    x2d = x[0]
    tgt = loss_target[0]
    T = x2d.shape[0]
    shard = 2 * lax.axis_index("x") + lax.axis_index("y")

    me = 2 * shard + lax.axis_index("c")

    def own_slot(block, slots, index):
        return lax.dynamic_update_slice(lax.empty((slots,) + block.shape, block.dtype), block[None],
                                        (index,) + (0,) * block.ndim)

    (conv_in_sems,), (stack_in,), first_token = _gather_start(
        "gather_start_conv_in", [own_slot(conv_w_in[0].astype(BF), N_SHARD, shard)], ((0,),), OTHER_CHIPS, after=())
    sharded_small = [conv_dw[0], attn_norm_g, b_qkv, b_o]
    (small_sems,), (small_land,), small_token = _gather_start(
        "small_weights_start", [own_slot(_pack(sharded_small), 8, me)], ((0,),), ALL_OTHERS, after=(first_token,))

    big = [conv_w_out[0], jnp.swapaxes(w_qkv, 1, 2)[0], w_o[0], w_up[0], w_up[1], w_down[0], w_down[1]]
    stacks = [own_slot(w.astype(BF), N_SHARD, shard) for w in big]
    groups = ((0,), (3, 5), (1, 2), (4, 6))
    gather_sems, stacks, gather_token = _gather_start("gather_start", stacks, groups, OTHER_CHIPS, after=(small_token,))

    def gathered_group(g, name, after):
        return _gather_wait(name, [stacks[t] for t in groups[g]], gather_sems[g], OTHER_CHIPS, after)

    bucket = _bucket_table()
    bias = _bias_table(rel_bias.T, bucket)

    h0 = _rms_fwd("conv_norm", x2d, conv_norm_g, deps=(gather_token,))
    (w_in_sm,) = _gather_wait("gather_wait_conv_in", [stack_in], conv_in_sems, OTHER_CHIPS, (h0, bias))
    (u,) = _mm("conv_in", h0, w_in_sm, nt=False, b_sm=True, tm=2048, tn=512, rows=256, ep_in=((conv_b_in, "row"),),
               ep_fn=lambda acc, b: (acc + b,), outs=(("tile", BF),))
    (gathered,) = _gather_wait("small_weights_wait", [small_land], small_sems, ALL_OTHERS, (u,))
    chips = [_unpack(gathered[2 * s], [a.shape for a in sharded_small]) for s in range(N_SHARD)]
    dw_f, attn_norm_f, b_qkv_f, b_o_f = (jnp.concatenate([chips[s][t] for s in range(N_SHARD)], axis=-1)
                                         for t in range(len(sharded_small)))
    dw_pad = jnp.pad(dw_f, ((0, HALO - CONV_W), (0, 0)))
    cv, s_act = _conv_fwd(u, dw_pad, conv_dw_b, conv_ln_g, conv_ln_b)
    (g_out,) = gathered_group(0, "gather_wait_conv_out", (s_act,))
    w_out_f = g_out.reshape(Dm, Dm)
    x1, h1 = _mm("conv_out", s_act, w_out_f, nt=False, tm=1024, tn=1024, rows=256,
                 ep_in=((conv_b_out, "row"), (x2d, "tile"), (mlp_norm_g[0:1], "row")), ep_fn=_residual_norm_ep,
                 outs=RESIDUAL_NORM_OUTS)

    g_up0, g_down0 = gathered_group(1, "gather_wait_mlp0", (x1,))
    w_up_sm = [g_up0, None]
    w_down_f = [g_down0.reshape(D_FF, Dm), None]
    up0 = _mlp_up(0, h1, w_up_sm[0])
    x2, h2 = _mm("mlp0_down", up0, w_down_f[0], nt=False, tm=512, tn=1024, rows=256, a_fn=_relu2,
                 ep_in=((x1, "tile"), (attn_norm_f, "row")), ep_fn=_residual_norm_ep, outs=RESIDUAL_NORM_OUTS)

    g_qkv, g_o = gathered_group(2, "gather_wait_attn", (x2,))
    w_qkv_t = g_qkv.reshape(QKV_DIM, Dm)
    w_o_f = g_o.reshape(ATTN_DIM, Dm)
    qg_t = jnp.tile(q_norm_g, (1, N_HEADS))
    kg_t = jnp.tile(k_norm_g, (1, N_KV))

    def qkv_ep(acc, b, qg, kg, ones):
        proj = acc + b
        q, k, v = proj[:, :ATTN_DIM], proj[:, ATTN_DIM:ATTN_DIM + KV_DIM], proj[:, ATTN_DIM + KV_DIM:]
        return proj, _qk_normed(q, qg, ones, 1.0 / math.sqrt(HEAD_DIM)), _qk_normed(k, kg, ones, 1.0), v

    qkv, qn, kn, vv = _mm(
        "attn_qkv", h2, w_qkv_t, nt=True, tm=1024, tn=QKV_DIM, rows=256, ep_fn=qkv_ep,
        ep_in=((b_qkv_f, "row"), (qg_t, "whole"), (kg_t, "whole"), (_head_ones(), "whole")),
        outs=(("tile", F32), ("tile", BF, ATTN_DIM), ("tile", BF, KV_DIM), ("tile", BF, KV_DIM)))
    sinks1 = sinks[0]
    att = _attn_fwd(qn, kn, vv, bias, sinks1)
    x3, h3 = _mm("attn_out", att, w_o_f, nt=False, tm=1024, tn=1024, rows=256,
                 ep_in=((b_o_f, "row"), (x2, "tile"), (mlp_norm_g[1:2], "row")), ep_fn=_residual_norm_ep,
                 outs=RESIDUAL_NORM_OUTS)

    g_up1, g_down1 = gathered_group(3, "gather_wait_mlp1", (x3,))
    w_up_sm[1] = g_up1
    w_down_f[1] = g_down1.reshape(D_FF, Dm)
    up1 = _mlp_up(1, h3, w_up_sm[1])

    def loss_ep(acc, r, t):
        diff = acc + r - t
        dy = diff * (1.0 / Dm)
        return dy, dy, jnp.sum(diff * diff, axis=0, keepdims=True)

    dy, dy_bf, sq = _mm("mlp1_down_loss", up1, w_down_f[1], nt=False, tm=512, tn=1024, rows=256, a_fn=_relu2,
                        ep_in=((x3, "tile"), (tgt, "tile")), ep_fn=loss_ep,
                        outs=(("tile", F32), ("tile", BF), ("colsum", F32)))

    place = jnp.stack([shard, lax.axis_index("c")]).astype(jnp.int32)
    dx3, dx3_bf, dg_mlp1, db_o, dup1 = _mlp_bwd(1, dy, dy_bf, x3, mlp_norm_g[1:2], up1, w_up_sm[1], w_down_f[1])
    dw_down1 = _mm_tn("mlp1_dw_down", up1, dy_bf, tm=1024, tn=1024, tk=2048, a_fn=_relu2)
    dw_up1 = _mm_tn("mlp1_dw_up", h3, dup1, tm=1024, tn=1024, tk=2048, out_sm=N_SHARD)
    red_mlp1 = _Reduction("mlp1", [dw_up1, dw_down1.reshape(N_SHARD, D_FF // N_SHARD, Dm)], place)

    ident = lambda acc: (acc,)
    (datt,) = _mm("attn_dout", dx3_bf, w_o_f, nt=True, tm=1024, tn=1024, rows=256, ep_fn=ident, outs=(("tile", BF),),
                  deps=(red_mlp1.token,))
    dw_o = _mm_tn("attn_dw_o", att, dx3_bf, tm=1024, tn=1024, tk=2048)
    dqn, dkn, dvv, dbias, dsinks = _attn_bwd(qn, kn, vv, bias, sinks1, datt)
    drel = _bias_grad(dbias, bucket)
    dqkv, db_qkv, dqg_t, dkg_t = _qk_norm_bwd(qkv, dqn, dkn, dvv, qg_t, kg_t)
    dw_qkv_t = _mm_tn("attn_dw_qkv", dqkv, h2, tm=QKV_DIM, tn=1024, tk=2048)
    red_attn = _Reduction("attn", [dw_qkv_t.reshape(N_SHARD, QKV_DIM // N_SHARD, Dm),
                                   dw_o.reshape(N_SHARD, ATTN_DIM // N_SHARD, Dm)], place)
    dx2, dx2_bf, dg_attn, _ = _mm("attn_dx", dqkv, w_qkv_t, nt=False, tm=1024, tn=1024, rows=256,
                                  ep_in=((x2, "tile"), (attn_norm_f, "row"), (dx3, "tile")), ep_fn=_rms_bwd_ep,
                                  outs=RMS_BWD_OUTS, deps=(red_attn.token,))

    dx1, dx1_bf, dg_mlp0, db_out, dup0 = _mlp_bwd(0, dx2, dx2_bf, x1, mlp_norm_g[0:1], up0, w_up_sm[0], w_down_f[0])
    dw_down0 = _mm_tn("mlp0_dw_down", up0, dx2_bf, tm=1024, tn=1024, tk=2048, a_fn=_relu2)
    dw_up0 = _mm_tn("mlp0_dw_up", h1, dup0, tm=1024, tn=1024, tk=2048, out_sm=N_SHARD)
    dw_out = _mm_tn("conv_dw_out", s_act, dx1_bf, tm=1024, tn=1024, tk=2048)
    red_mlp0 = _Reduction("mlp0", [dw_up0, dw_down0.reshape(N_SHARD, D_FF // N_SHARD, Dm),
                                   dw_out.reshape(N_SHARD, Dm // N_SHARD, Dm)], place)
    (r_qkv, r_o) = red_attn.finish((dx1,))
    (r_up1, r_down1) = red_mlp1.finish((dx1,))

    dcv, dln_g, dln_b, ddw_b = _mm("conv_ds", dx1_bf, w_out_f, nt=True, tm=1024, tn=1024, rows=256,
                                   ep_in=((cv, "tile"), (conv_ln_g, "row"), (conv_ln_b, "row")),
                                   ep_fn=_ln_silu_bwd_ep,
                                   outs=(("tile", F32), ("colsum", F32), ("colsum", F32), ("colsum", F32)),
                                   deps=(red_mlp0.token,))
    du, db_in, ddw8 = _conv_bwd(u, dcv, dw_pad)
    (r_up0, r_down0, r_out) = red_mlp0.finish((du,))
    early = [r_out, r_qkv, r_o, r_up0, r_up1, r_down0, r_down1]
    join_sems, early, early_lands, join_token = _join_start("join_start", early)
    dw_in = _mm_tn("conv_dw_in", h0, du, tm=1024, tn=512, tk=4096, out_sm=N_SHARD)
    red_conv = _Reduction("conv", [dw_in], place)
    def first_layer_ep(*args):
        tot, _, dg, _ = _rms_bwd_ep(*args)
        return tot, dg

    gx, dg_conv = _mm("conv_dx", du, w_in_sm, nt=True, b_sm=True, tm=1024, tn=1024, rows=256,
                      ep_in=((x2d, "tile"), (conv_norm_g, "row"), (dx1, "tile")), ep_fn=first_layer_ep,
                      outs=(("tile", F32), ("colsum", F32)), deps=(red_conv.token, join_token))
    (r_in,) = red_conv.finish((gx,))

    dqg = dqg_t.reshape(N_HEADS, HEAD_DIM).sum(axis=0, keepdims=True)
    dkg = dkg_t.reshape(N_KV, HEAD_DIM).sum(axis=0, keepdims=True)
    small_full = [dg_conv, db_in, ddw8.sum(axis=1)[:CONV_W], ddw_b, dln_g, dln_b, db_out, dg_attn, db_qkv, dqg, dkg,
                  dsinks[None, :], db_o, drel.reshape(1, REL_BUCKETS * N_HEADS),
                  jnp.pad(dg_mlp0, ((0, 1), (0, 0))) + jnp.pad(dg_mlp1, ((1, 0), (0, 0))), sq]
    (sg_sems,), (sg_land,), sg_token = _gather_start(
        "small_grads_start", [own_slot(_pack(small_full), 8, me)], ((0,),), ALL_OTHERS, after=())

    early, early_sibling = _join_wait("join_wait", early, early_lands, join_sems, (gx, sg_token))
    r_out, r_qkv, r_o, r_up0, r_up1, r_down0, r_down1 = zip(early, early_sibling)
    r_in = (r_in,) + tuple(_join_halves("join_halves", [r_in], deps=(sg_token,)))

    big_out = {}
    qkv_t = [jnp.swapaxes(a, 1, 2) for a in (w_qkv, m_w_qkv, v_w_qkv)]
    for nm, w, m, v, gs in (("conv_w_in", conv_w_in, m_conv_w_in, v_conv_w_in, (r_in,)),
                            ("conv_w_out", conv_w_out, m_conv_w_out, v_conv_w_out, (r_out,)),
                            ("w_qkv", *qkv_t, (r_qkv,)),
                            ("w_o", w_o, m_w_o, v_w_o, (r_o,)),
                            ("w_up", w_up, m_w_up, v_w_up, (r_up0, r_up1)),
                            ("w_down", w_down, m_w_down, v_w_down, (r_down0, r_down1))):
        big_out[nm] = _adamw(f"adamw_{nm}", w, m, v, gs)

    (sg_land,) = _gather_wait("small_grads_wait", [sg_land], sg_sems, ALL_OTHERS,
                              [big_out[nm][0] for nm in big_out])
    big_out["w_qkv"] = tuple(jnp.swapaxes(a, 1, 2) for a in big_out["w_qkv"])
    small_sum = _sum8("small_grads_sum", sg_land)
    (r_norm, r_b_in, r_dw, r_dw_b, r_ln_g, r_ln_b, r_b_out, r_attn_norm, r_b_qkv, r_qg, r_kg, r_sinks, r_b_o, r_rel,
     r_mlp_norm, r_sq) = _unpack(small_sum, [a.shape for a in small_full])
    loss = 0.5 * jnp.sum(r_sq) * (1.0 / Dm)

    def cols(a, width):
        return lax.dynamic_slice_in_dim(a, shard * width, width, axis=a.ndim - 1)

    small_names = ["conv_norm_g", "conv_b_in", "conv_dw", "conv_dw_b", "conv_ln_g", "conv_ln_b", "conv_b_out",
                   "attn_norm_g", "b_qkv", "q_norm_g", "k_norm_g", "sinks", "b_o", "rel_bias", "mlp_norm_g"]
    small_g = [r_norm, r_b_in, cols(r_dw, Dm // N_SHARD)[None], r_dw_b, r_ln_g, r_ln_b, r_b_out,
               cols(r_attn_norm, Dm // N_SHARD), cols(r_b_qkv, QKV_DIM // N_SHARD), r_qg, r_kg, r_sinks,
               cols(r_b_o, Dm // N_SHARD), r_rel.reshape(N_HEADS, REL_BUCKETS), r_mlp_norm]
    small_w = [conv_norm_g, conv_b_in, conv_dw, conv_dw_b, conv_ln_g, conv_ln_b, conv_b_out, attn_norm_g, b_qkv,
               q_norm_g, k_norm_g, sinks, b_o, rel_bias.T, mlp_norm_g]
    small_m = [m_conv_norm_g, m_conv_b_in, m_conv_dw, m_conv_dw_b, m_conv_ln_g, m_conv_ln_b, m_conv_b_out,
               m_attn_norm_g, m_b_qkv, m_q_norm_g, m_k_norm_g, m_sinks, m_b_o, m_rel_bias.T, m_mlp_norm_g]
    small_v = [v_conv_norm_g, v_conv_b_in, v_conv_dw, v_conv_dw_b, v_conv_ln_g, v_conv_ln_b, v_conv_b_out,
               v_attn_norm_g, v_b_qkv, v_q_norm_g, v_k_norm_g, v_sinks, v_b_o, v_rel_bias.T, v_mlp_norm_g]
    flat2 = lambda a: a.reshape(-1, a.shape[-1])
    small_g = [flat2(g) for g in small_g]
    d_s, m_s, v_s = _adamw_small([flat2(w) for w in small_w], small_g, [flat2(m) for m in small_m],
                                 [flat2(v) for v in small_v])
    small_out = {}
    for nm, w, g, d, m2, v2 in zip(small_names, small_w, small_g, d_s, m_s, v_s):
        small_out[nm] = tuple(a.reshape(w.shape) for a in (g, d, m2, v2))
    small_out["rel_bias"] = tuple(a.T for a in small_out["rel_bias"])

    order = ["conv_norm_g", "conv_w_in", "conv_b_in", "conv_dw", "conv_dw_b", "conv_ln_g", "conv_ln_b", "conv_w_out",
             "conv_b_out", "attn_norm_g", "w_qkv", "b_qkv", "q_norm_g", "k_norm_g", "sinks", "w_o", "b_o", "rel_bias",
             "mlp_norm_g", "w_up", "w_down"]
    res = {**small_out, **big_out}
    outs = [loss, gx[None]]
    for part in range(4):
        outs += [res[nm][part] for nm in order]
    return tuple(outs)
```

```python
import math

import numpy as np
import jax
import jax.numpy as jnp
from jax import lax
from jax.experimental import pallas as pl
from jax.experimental.pallas import tpu as pltpu

F32 = jnp.float32
BF = jnp.bfloat16
MESH = pl.DeviceIdType.MESH

D_MODEL = 1024
D_FF = 4096
N_HEADS = 16
N_KV = 2
GROUP = N_HEADS // N_KV
HEAD_DIM = 64
ATTN_DIM = N_HEADS * HEAD_DIM
KV_DIM = N_KV * HEAD_DIM
QKV_DIM = ATTN_DIM + 2 * KV_DIM
BLOCK = 128
CONV_W = 31
HALO = 32
REL_BUCKETS = 32
REL_MAX_DIST = 128
NORM_EPS = 1e-6
NEG_INF = -1e30
N_SHARD = 4
LANES = 1024

ADAM_LR = 0.001
ADAM_B1 = 0.9
ADAM_B2 = 0.999
ADAM_EPS = 1e-08
ADAM_WD = 0.01
ADAM_STEP = 10

VMEM_LIMIT = 56 * 1024 * 1024


def _params(n_axes):
    return pltpu.CompilerParams(dimension_semantics=("arbitrary",) * n_axes, vmem_limit_bytes=VMEM_LIMIT)


def _dot(a, b, ca, cb):
    return lax.dot_general(a, b, (((ca,), (cb,)), ((), ())), preferred_element_type=F32)


def _mm(name, a, b, *, nt, tm, tn, ep_fn, outs, a_fn=None, b_sm=False, ep_in=(), deps=(), rows=None):
    M, K = a.shape
    rows = tm if rows is None else rows
    if b_sm:
        S, ks = b.shape[0], b.shape[2]
        N, per = (b.shape[1], None) if nt else (S * b.shape[2], b.shape[2] // tn)
        assert (S * ks == K) if nt else (b.shape[1] == K)
    else:
        N = b.shape[0] if nt else b.shape[1]
        assert (b.shape[1] if nt else b.shape[0]) == K
    assert M % tm == 0 and N % tn == 0 and tm % rows == 0
    ne, no, nd = len(ep_in), len(outs), len(deps)

    def body(a_ref, b_ref, *rest):
        ep_refs, out_refs = rest[:ne], rest[ne + nd:ne + nd + no]
        i = pl.program_id(1)
        sums = [None] * no
        for r in range(tm // rows):
            rs = pl.ds(r * rows, rows)

            def lhs(cols):
                av = a_ref[rs, cols]
                return (av if a_fn is None else a_fn(av)).astype(BF)

            if b_sm and nt:
                acc = None
                for s in range(S):
                    part = _dot(lhs(pl.ds(s * ks, ks)), b_ref[s].astype(BF), 1, 1)
                    acc = part if acc is None else acc + part
            else:
                acc = _dot(lhs(slice(None)), b_ref[...].astype(BF), 1, 1 if nt else 0)
            ep_vals = [ref[rs, :] if kind == "tile" else ref[...] for ref, (_, kind) in zip(ep_refs, ep_in)]
            vals = ep_fn(acc, *ep_vals)
            for o, ((kind, dt, *_), ref, val) in enumerate(zip(outs, out_refs, vals)):
                if kind == "tile":
                    ref[rs, :] = val.astype(dt)
                else:
                    sums[o] = val if sums[o] is None else sums[o] + val
        for (kind, *_), ref, val in zip(outs, out_refs, sums):
            if kind == "colsum":
                @pl.when(i == 0)
                def _():
                    ref[...] = val

                @pl.when(i > 0)
                def _():
                    ref[...] += val

    if b_sm and nt:
        b_spec = pl.BlockSpec((S, tn, ks), lambda j, i: (0, j, 0))
    elif b_sm:
        b_spec = pl.BlockSpec((None, K, tn), lambda j, i: (j // per, 0, j % per))
    elif nt:
        b_spec = pl.BlockSpec((tn, K), lambda j, i: (j, 0))
    else:
        b_spec = pl.BlockSpec((K, tn), lambda j, i: (0, j))
    in_specs = [pl.BlockSpec((tm, K), lambda j, i: (i, 0)), b_spec]
    for arr, kind in ep_in:
        if kind == "tile":
            assert arr.shape == (M, N)
            in_specs.append(pl.BlockSpec((tm, tn), lambda j, i: (i, j)))
        elif kind == "whole":
            in_specs.append(pl.BlockSpec(arr.shape, lambda j, i, rank=arr.ndim: (0,) * rank))
        else:
            assert arr.shape == (1, N)
            in_specs.append(pl.BlockSpec((1, tn), lambda j, i: (0, j)))
    in_specs += [pl.BlockSpec(memory_space=pl.ANY)] * nd
    out_shape, out_specs = [], []
    for kind, dt, *width in outs:
        if kind == "tile" and width:
            assert tn == N
            out_shape.append(jax.ShapeDtypeStruct((M, width[0]), dt))
            out_specs.append(pl.BlockSpec((tm, width[0]), lambda j, i: (i, 0)))
        elif kind == "tile":
            out_shape.append(jax.ShapeDtypeStruct((M, N), dt))
            out_specs.append(pl.BlockSpec((tm, tn), lambda j, i: (i, j)))
        else:
            out_shape.append(jax.ShapeDtypeStruct((1, N), F32))
            out_specs.append(pl.BlockSpec((1, tn), lambda j, i: (0, j)))
    return pl.pallas_call(
        body, name=name, grid=(N // tn, M // tm), in_specs=in_specs, out_specs=out_specs, out_shape=out_shape,
        compiler_params=_params(2),
    )(a, b, *[arr for arr, _ in ep_in], *deps)


def _mm_tn(name, a, b, *, tm, tn, tk, a_fn=None, out_sm=None):
    T, Ka = a.shape
    N = b.shape[1]
    assert b.shape[0] == T and T % tk == 0 and Ka % tm == 0 and N % tn == 0
    nk = T // tk

    def body(a_ref, b_ref, o_ref, acc_ref):
        k = pl.program_id(2)

        @pl.when(k == 0)
        def _():
            acc_ref[...] = jnp.zeros_like(acc_ref)

        av = a_ref[...]
        if a_fn is not None:
            av = a_fn(av)
        acc_ref[...] += _dot(av.astype(BF), b_ref[...].astype(BF), 0, 0)

        @pl.when(k == nk - 1)
        def _():
            o_ref[...] = acc_ref[...].astype(BF)

    if out_sm is None:
        out_shape = jax.ShapeDtypeStruct((Ka, N), BF)
        out_spec = pl.BlockSpec((tm, tn), lambda i, j, k: (i, j))
    else:
        per = (N // out_sm) // tn
        assert per * tn * out_sm == N
        out_shape = jax.ShapeDtypeStruct((out_sm, Ka, N // out_sm), BF)
        out_spec = pl.BlockSpec((None, tm, tn), lambda i, j, k: (j // per, i, j % per))
    return pl.pallas_call(
        body, name=name, grid=(Ka // tm, N // tn, nk),
        in_specs=[pl.BlockSpec((tk, tm), lambda i, j, k: (k, i)), pl.BlockSpec((tk, tn), lambda i, j, k: (k, j))],
        out_specs=out_spec, out_shape=out_shape, scratch_shapes=[pltpu.VMEM((tm, tn), F32)],
        compiler_params=_params(3),
    )(a, b)


def _relu2(v):
    r = jnp.maximum(v.astype(F32), 0.0)
    return r * r


def _rms_bwd_ep(dh, x, g, dres):
    rstd = lax.rsqrt(jnp.mean(x * x, axis=-1, keepdims=True) + NORM_EPS)
    xh = x * rstd
    dxh = dh * g
    dx = rstd * (dxh - xh * jnp.mean(dxh * xh, axis=-1, keepdims=True))
    tot = dres + dx
    return tot, tot, jnp.sum(dh * xh, axis=0, keepdims=True), jnp.sum(tot, axis=0, keepdims=True)


def _rms_fwd(name, x, g, tm=512, deps=()):
    T, Dm = x.shape

    def body(x_ref, g_ref, *rest):
        o_ref = rest[-1]
        xv = x_ref[...]
        rstd = lax.rsqrt(jnp.mean(xv * xv, axis=-1, keepdims=True) + NORM_EPS)
        o_ref[...] = (xv * rstd * g_ref[...]).astype(BF)

    return pl.pallas_call(
        body, name=name, grid=(T // tm,),
        in_specs=[pl.BlockSpec((tm, Dm), lambda i: (i, 0)), pl.BlockSpec((1, Dm), lambda i: (0, 0))]
        + [pl.BlockSpec(memory_space=pl.ANY)] * len(deps),
        out_specs=pl.BlockSpec((tm, Dm), lambda i: (i, 0)), out_shape=jax.ShapeDtypeStruct((T, Dm), BF),
        compiler_params=_params(1),
    )(x, g, *deps)


HEAD_GROUP = 256


def _head_sum(v, ones):
    n = v.shape[1]
    w = min(n, HEAD_GROUP)
    blk = ones[:w, :w]
    parts = [_dot(v[:, c:c + w].astype(BF), blk, 1, 0) for c in range(0, n, w)]
    return parts[0] if len(parts) == 1 else jnp.concatenate(parts, axis=1)


def _head_ones():
    idx = np.arange(HEAD_GROUP) // HEAD_DIM
    return jnp.asarray((idx[:, None] == idx[None, :]).astype(np.float32), dtype=BF)


def _qk_normed(x, g, ones, scale):
    r = lax.rsqrt(_head_sum(x * x, ones) * (1.0 / HEAD_DIM) + NORM_EPS)
    return x * r * g * scale


def _qk_norm_bwd(qkv, dqn, dkn, dv, qg_t, kg_t, tm=256):
    T = qkv.shape[0]

    def body(x_ref, dq_ref, dk_ref, dv_ref, qg_ref, kg_ref, ones_ref, o_ref, db_ref, dqg_ref, dkg_ref):
        i = pl.program_id(0)
        ones = ones_ref[...]

        def one(x, dy, g):
            r = lax.rsqrt(_head_sum(x * x, ones) * (1.0 / HEAD_DIM) + NORM_EPS)
            xh = x * r
            dxh = dy * g
            dx = r * (dxh - xh * (_head_sum(dxh * xh, ones) * (1.0 / HEAD_DIM)))
            return dx, jnp.sum(dy * xh, axis=0, keepdims=True)

        dq, dqg = one(x_ref[:, pl.ds(0, ATTN_DIM)], dq_ref[...], qg_ref[...])
        dk, dkg = one(x_ref[:, pl.ds(ATTN_DIM, KV_DIM)], dk_ref[...], kg_ref[...])
        dvv = dv_ref[...]
        o_ref[:, pl.ds(0, ATTN_DIM)] = dq.astype(BF)
        o_ref[:, pl.ds(ATTN_DIM, KV_DIM)] = dk.astype(BF)
        o_ref[:, pl.ds(ATTN_DIM + KV_DIM, KV_DIM)] = dvv.astype(BF)
        sq, sk, sv = (jnp.sum(t, axis=0, keepdims=True) for t in (dq, dk, dvv))

        @pl.when(i == 0)
        def _():
            db_ref[:, pl.ds(0, ATTN_DIM)] = sq
            db_ref[:, pl.ds(ATTN_DIM, KV_DIM)] = sk
            db_ref[:, pl.ds(ATTN_DIM + KV_DIM, KV_DIM)] = sv
            dqg_ref[...] = dqg
            dkg_ref[...] = dkg

        @pl.when(i > 0)
        def _():
            db_ref[:, pl.ds(0, ATTN_DIM)] += sq
            db_ref[:, pl.ds(ATTN_DIM, KV_DIM)] += sk
            db_ref[:, pl.ds(ATTN_DIM + KV_DIM, KV_DIM)] += sv
            dqg_ref[...] += dqg
            dkg_ref[...] += dkg

    full = lambda shape: pl.BlockSpec(shape, lambda i: (0, 0))
    row = lambda n: pl.BlockSpec((tm, n), lambda i: (i, 0))
    return pl.pallas_call(
        body, name="qk_norm_bwd", grid=(T // tm,),
        in_specs=[row(QKV_DIM), row(ATTN_DIM), row(KV_DIM), row(KV_DIM), full((1, ATTN_DIM)), full((1, KV_DIM)),
                  full((HEAD_GROUP, HEAD_GROUP))],
        out_specs=[row(QKV_DIM), full((1, QKV_DIM)), full((1, ATTN_DIM)), full((1, KV_DIM))],
        out_shape=[jax.ShapeDtypeStruct((T, QKV_DIM), BF), jax.ShapeDtypeStruct((1, QKV_DIM), F32),
                   jax.ShapeDtypeStruct((1, ATTN_DIM), F32), jax.ShapeDtypeStruct((1, KV_DIM), F32)],
        compiler_params=_params(1),
    )(qkv, dqn, dkn, dv, qg_t, kg_t, _head_ones())


ROWS = 128
COLS = 128


SUBLANES = 8
FIRST_TAP = HALO - (CONV_W - 1)


def _glu(a, g):
    return a.astype(F32) * jax.nn.sigmoid(g.astype(F32))


def _shifted(xe, s):
    return xe if s == 0 else pltpu.roll(xe, ROWS + HALO - s, axis=0)


def _conv_fwd(u, dw_pad, dw_b, ln_g, ln_b, tm=512):
    T = u.shape[0]
    Dm = D_MODEL
    hpt = tm // HALO

    def body(ac_ref, gc_ref, ap_ref, gp_ref, w_ref, wb_ref, lg_ref, lb_ref, cv_ref, s_ref, ext):
        i = pl.program_id(0)
        ext[pl.ds(0, HALO), :] = jnp.where(i > 0, _glu(ap_ref[...], gp_ref[...]), 0.0)
        ext[pl.ds(HALO, tm), :] = _glu(ac_ref[...], gc_ref[...])

        def rows(r, carry):
            r0 = pl.multiple_of(r * ROWS, ROWS)
            for c in range(Dm // COLS):
                cs = pl.ds(c * COLS, COLS)
                xe = ext[pl.ds(r0, ROWS + HALO), cs]
                acc = jnp.zeros((ROWS, COLS), F32)
                for s in range(SUBLANES):
                    xs = _shifted(xe, s)
                    for j in range(CONV_W):
                        off = FIRST_TAP + j
                        if off % SUBLANES == s:
                            acc = acc + xs[off - s:off - s + ROWS, :] * w_ref[pl.ds(j, 1), cs]
                cv_ref[pl.ds(r0, ROWS), cs] = acc + wb_ref[:, cs]
            return carry

        lax.fori_loop(0, tm // ROWS, rows, 0)
        cv = cv_ref[...]
        xc = cv - jnp.mean(cv, axis=-1, keepdims=True)
        y = xc * lax.rsqrt(jnp.mean(xc * xc, axis=-1, keepdims=True) + NORM_EPS) * lg_ref[...] + lb_ref[...]
        s_ref[...] = (y * jax.nn.sigmoid(y)).astype(BF)

    full = lambda shape: pl.BlockSpec(shape, lambda i: (0, 0))
    return pl.pallas_call(
        body, name="conv_fwd", grid=(T // tm,),
        in_specs=[pl.BlockSpec((tm, Dm), lambda i: (i, 0)), pl.BlockSpec((tm, Dm), lambda i: (i, 1)),
                  pl.BlockSpec((HALO, Dm), lambda i: (jnp.maximum(i * hpt - 1, 0), 0)),
                  pl.BlockSpec((HALO, Dm), lambda i: (jnp.maximum(i * hpt - 1, 0), 1)),
                  full((HALO, Dm)), full((1, Dm)), full((1, Dm)), full((1, Dm))],
        out_specs=[pl.BlockSpec((tm, Dm), lambda i: (i, 0)), pl.BlockSpec((tm, Dm), lambda i: (i, 0))],
        out_shape=[jax.ShapeDtypeStruct((T, Dm), F32), jax.ShapeDtypeStruct((T, Dm), BF)],
        scratch_shapes=[pltpu.VMEM((tm + HALO, Dm), F32)],
        compiler_params=_params(1),
    )(u, u, u, u, dw_pad, dw_b, ln_g, ln_b)


def _ln_silu_bwd_ep(ds, cv, lg, lb):
    xc = cv - jnp.mean(cv, axis=-1, keepdims=True)
    rstd = lax.rsqrt(jnp.mean(xc * xc, axis=-1, keepdims=True) + NORM_EPS)
    xh = xc * rstd
    y = xh * lg + lb
    sg = jax.nn.sigmoid(y)
    dy = ds * (sg * (1.0 + y * (1.0 - sg)))
    dxh = dy * lg
    dcv = rstd * (dxh - jnp.mean(dxh, axis=-1, keepdims=True) - xh * jnp.mean(dxh * xh, axis=-1, keepdims=True))
    return (dcv, jnp.sum(dy * xh, axis=0, keepdims=True), jnp.sum(dy, axis=0, keepdims=True),
            jnp.sum(dcv, axis=0, keepdims=True))


def _conv_bwd(u, dcv, dw_pad, tm=512):
    T = u.shape[0]
    Dm = D_MODEL
    hpt = tm // HALO
    last = T // HALO - 1
    nt = T // tm

    def body(ac_ref, gc_ref, ap_ref, gp_ref, dc_ref, dn_ref, w_ref, du_ref, db_ref, dw_ref, ext_g, ext_d):
        i = pl.program_id(0)
        ext_g[pl.ds(0, HALO), :] = jnp.where(i > 0, _glu(ap_ref[...], gp_ref[...]), 0.0)
        ext_g[pl.ds(HALO, tm), :] = _glu(ac_ref[...], gc_ref[...])
        ext_d[pl.ds(0, tm), :] = dc_ref[...]
        ext_d[pl.ds(tm, HALO), :] = jnp.where(i < nt - 1, dn_ref[...], 0.0)

        @pl.when(i == 0)
        def _():
            db_ref[...] = jnp.zeros_like(db_ref)
            dw_ref[...] = jnp.zeros_like(dw_ref)

        def rows(r, carry):
            r0 = pl.multiple_of(r * ROWS, ROWS)
            rs = pl.ds(r0, ROWS)
            for c in range(Dm // COLS):
                cs = pl.ds(c * COLS, COLS)
                cs2 = pl.ds(Dm + c * COLS, COLS)
                de = ext_d[pl.ds(r0, ROWS + HALO), cs]
                ge = ext_g[pl.ds(r0, ROWS + HALO), cs]
                dcur = de[0:ROWS, :]
                acc = jnp.zeros((ROWS, COLS), F32)
                for s in range(SUBLANES):
                    ds_, gs_ = _shifted(de, s), _shifted(ge, s)
                    for j in range(CONV_W):
                        off = CONV_W - 1 - j
                        if off % SUBLANES == s:
                            acc = acc + ds_[off - s:off - s + ROWS, :] * w_ref[pl.ds(j, 1), cs]
                        goff = FIRST_TAP + j
                        if goff % SUBLANES == s:
                            prod = dcur * gs_[goff - s:goff - s + ROWS, :]
                            dw_ref[j, :, cs] += jnp.sum(prod.reshape(ROWS // SUBLANES, SUBLANES, COLS), axis=0)
                a = ac_ref[rs, cs].astype(F32)
                sg = jax.nn.sigmoid(gc_ref[rs, cs].astype(F32))
                da = acc * sg
                dg = acc * a * sg * (1.0 - sg)
                du_ref[rs, cs] = da.astype(BF)
                du_ref[rs, cs2] = dg.astype(BF)
                db_ref[:, cs] += jnp.sum(da, axis=0, keepdims=True)
                db_ref[:, cs2] += jnp.sum(dg, axis=0, keepdims=True)
            return carry

        lax.fori_loop(0, tm // ROWS, rows, 0)

    return pl.pallas_call(
        body, name="conv_bwd", grid=(nt,),
        in_specs=[pl.BlockSpec((tm, Dm), lambda i: (i, 0)), pl.BlockSpec((tm, Dm), lambda i: (i, 1)),
                  pl.BlockSpec((HALO, Dm), lambda i: (jnp.maximum(i * hpt - 1, 0), 0)),
                  pl.BlockSpec((HALO, Dm), lambda i: (jnp.maximum(i * hpt - 1, 0), 1)),
                  pl.BlockSpec((tm, Dm), lambda i: (i, 0)),
                  pl.BlockSpec((HALO, Dm), lambda i: (jnp.minimum((i + 1) * hpt, last), 0)),
                  pl.BlockSpec((HALO, Dm), lambda i: (0, 0))],
        out_specs=[pl.BlockSpec((tm, 2 * Dm), lambda i: (i, 0)), pl.BlockSpec((1, 2 * Dm), lambda i: (0, 0)),
                   pl.BlockSpec((HALO, 8, Dm), lambda i: (0, 0, 0))],
        out_shape=[jax.ShapeDtypeStruct((T, 2 * Dm), BF), jax.ShapeDtypeStruct((1, 2 * Dm), F32),
                   jax.ShapeDtypeStruct((HALO, 8, Dm), F32)],
        scratch_shapes=[pltpu.VMEM((tm + HALO, Dm), F32), pltpu.VMEM((tm + HALO, Dm), F32)],
        compiler_params=_params(1),
    )(u, u, u, u, dcv, dcv, dw_pad)


def _bucket_table():
    q_loc = np.arange(BLOCK)[:, None]
    k_loc = np.arange(2 * BLOCK)[None, :]
    dist = q_loc + BLOCK - k_loc
    n = np.maximum(dist, 0)
    max_exact = REL_BUCKETS // 2
    large = max_exact + (np.log(np.maximum(n, 1).astype(np.float32) / max_exact)
                         / math.log(REL_MAX_DIST / max_exact) * (REL_BUCKETS - max_exact)).astype(np.int32)
    large = np.minimum(large, REL_BUCKETS - 1)
    bucket = np.where(n < max_exact, n, large).astype(np.int32)
    band = np.where((dist >= 0) & (dist < BLOCK), bucket, -1)
    folded = np.where(np.arange(BLOCK)[None, :] > q_loc, band[:, :BLOCK], band[:, BLOCK:])
    assert (folded >= 0).all() and ((band[:, :BLOCK] >= 0) != (band[:, BLOCK:] >= 0)).all()
    return jnp.asarray(folded.astype(np.int32))


def _prev_mask():
    row = lax.broadcasted_iota(jnp.int32, (BLOCK, BLOCK), 0)
    col = lax.broadcasted_iota(jnp.int32, (BLOCK, BLOCK), 1)
    return col > row


def _fold(band, prev_mask):
    return jnp.where(prev_mask, band[:, :BLOCK], band[:, BLOCK:])


def _unfold(ref, g, rows, folded, prev_mask):
    ref[g, rows, pl.ds(0, BLOCK)] = jnp.where(prev_mask, folded, 0.0).astype(ref.dtype)
    ref[g, rows, pl.ds(BLOCK, BLOCK)] = jnp.where(prev_mask, 0.0, folded).astype(ref.dtype)


def _bias_table(rel_bias_t, bucket):
    def body(rb_ref, bk_ref, o_ref):
        bk = bk_ref[...]
        prev_mask = _prev_mask()
        for h in range(N_HEADS):
            acc = jnp.zeros((BLOCK, BLOCK), F32)
            for b in range(REL_BUCKETS):
                acc = jnp.where(bk == b, rb_ref[h, b], acc)
            o_ref[0, h] = acc
            o_ref[1, h] = jnp.where(prev_mask, NEG_INF, acc)

    return pl.pallas_call(
        body, name="bias_table", out_shape=jax.ShapeDtypeStruct((2, N_HEADS, BLOCK, BLOCK), F32),
        in_specs=[pl.BlockSpec(memory_space=pltpu.SMEM), pl.BlockSpec(memory_space=pltpu.VMEM)],
        out_specs=pl.BlockSpec(memory_space=pltpu.VMEM),
    )(rel_bias_t, bucket)


def _bias_grad(dbias, bucket):
    def body(db_ref, bk_ref, o_ref):
        bk = bk_ref[...]
        for b in range(REL_BUCKETS):
            sel = bk == b
            for h in range(N_HEADS):
                o_ref[h, b] = jnp.sum(jnp.where(sel, db_ref[h], 0.0))

    return pl.pallas_call(
        body, name="bias_grad", out_shape=jax.ShapeDtypeStruct((N_HEADS, REL_BUCKETS), F32),
        in_specs=[pl.BlockSpec(memory_space=pltpu.VMEM), pl.BlockSpec(memory_space=pltpu.VMEM)],
        out_specs=pl.BlockSpec(memory_space=pltpu.SMEM),
    )(dbias, bucket)


GROUP_ROWS = GROUP * BLOCK
BIAS_SPEC = pl.BlockSpec((2, N_HEADS, BLOCK, BLOCK), lambda n: (0, 0, 0, 0))


def _head_probs(qk, bias_h, sink, prev_mask):
    s = _fold(qk, prev_mask) + bias_h
    m = jnp.maximum(jnp.max(s, axis=-1, keepdims=True), sink)
    p = jnp.exp(s - m)
    ps = jnp.exp(sink - m)
    inv = 1.0 / (jnp.sum(p, axis=-1, keepdims=True) + ps)
    return p * inv, ps * inv


def _band(prev_ref, cur_ref, g):
    hs = pl.ds(g * HEAD_DIM, HEAD_DIM)
    return jnp.concatenate([prev_ref[:, hs], cur_ref[:, hs]], axis=0)


def _stack_heads(ref, g):
    return jnp.concatenate([ref[:, pl.ds((g * GROUP + hh) * HEAD_DIM, HEAD_DIM)] for hh in range(GROUP)], axis=0)


def _unstack_heads(ref, g, stacked, dtype):
    for hh in range(GROUP):
        ref[:, pl.ds((g * GROUP + hh) * HEAD_DIM, HEAD_DIM)] = stacked[hh * BLOCK:(hh + 1) * BLOCK, :].astype(dtype)


def _head_rows(hh):
    return pl.ds(hh * BLOCK, BLOCK)


def _attn_fwd(qn, kn, vv, bias, sinks):
    T = qn.shape[0]
    nb = T // BLOCK

    def body(sk_ref, q_ref, kc_ref, kp_ref, vc_ref, vp_ref, b_ref, o_ref, qk_buf, p_buf):
        table = (pl.program_id(0) == 0).astype(jnp.int32)
        prev_mask = _prev_mask()
        for g in range(N_KV):
            qk_buf[g] = _dot(_stack_heads(q_ref, g), _band(kp_ref, kc_ref, g), 1, 1)
        for g in range(N_KV):
            for hh in range(GROUP):
                h = g * GROUP + hh
                pn, _ = _head_probs(qk_buf[g, _head_rows(hh), :], b_ref[table, h], sk_ref[h], prev_mask)
                _unfold(p_buf, g, _head_rows(hh), pn, prev_mask)
        for g in range(N_KV):
            _unstack_heads(o_ref, g, _dot(p_buf[g], _band(vp_ref, vc_ref, g), 1, 0), BF)

    cur = lambda n: (n, 0)
    prev = lambda n: (jnp.maximum(n - 1, 0), 0)
    return pl.pallas_call(
        body, name="attn_fwd", grid=(nb,),
        in_specs=[pl.BlockSpec(memory_space=pltpu.SMEM), pl.BlockSpec((BLOCK, ATTN_DIM), cur),
                  pl.BlockSpec((BLOCK, KV_DIM), cur), pl.BlockSpec((BLOCK, KV_DIM), prev),
                  pl.BlockSpec((BLOCK, KV_DIM), cur), pl.BlockSpec((BLOCK, KV_DIM), prev), BIAS_SPEC],
        out_specs=pl.BlockSpec((BLOCK, ATTN_DIM), cur), out_shape=jax.ShapeDtypeStruct((T, ATTN_DIM), BF),
        scratch_shapes=[pltpu.VMEM((N_KV, GROUP_ROWS, 2 * BLOCK), F32), pltpu.VMEM((N_KV, GROUP_ROWS, 2 * BLOCK), BF)],
        compiler_params=_params(1),
    )(sinks, qn, kn, kn, vv, vv, bias)


def _attn_bwd(qn, kn, vv, bias, sinks, do):
    T = qn.shape[0]
    nb = T // BLOCK
    scale = 1.0 / math.sqrt(HEAD_DIM)

    def body(sk_ref, q_ref, kc_ref, kp_ref, vc_ref, vp_ref, b_ref, do_ref,
             dq_ref, dk_ref, dv_ref, db_ref, dsk_ref, dk_full, dv_full, dk_carry, dv_carry, qk_buf, dp_buf, p_buf, ds_buf):
        n = pl.program_id(0)

        @pl.when(n == 0)
        def _():
            db_ref[...] = jnp.zeros_like(db_ref)
            dk_carry[...] = jnp.zeros_like(dk_carry)
            dv_carry[...] = jnp.zeros_like(dv_carry)
            for h in range(N_HEADS):
                dsk_ref[h] = 0.0

        @pl.when(n < nb)
        def _():
            table = (n == 0).astype(jnp.int32)
            prev_mask = _prev_mask()
            ks = [_band(kp_ref, kc_ref, g) for g in range(N_KV)]
            qs = [_stack_heads(q_ref, g) for g in range(N_KV)]
            douts = [_stack_heads(do_ref, g) for g in range(N_KV)]
            for g in range(N_KV):
                qk_buf[g] = _dot(qs[g], ks[g], 1, 1)
                dp_buf[g] = _dot(douts[g], _band(vp_ref, vc_ref, g), 1, 1)
            for g in range(N_KV):
                for hh in range(GROUP):
                    h = g * GROUP + hh
                    rows = _head_rows(hh)
                    pn, psink = _head_probs(qk_buf[g, rows, :], b_ref[table, h], sk_ref[h], prev_mask)
                    dp = _fold(dp_buf[g, rows, :], prev_mask)
                    delta = jnp.sum(pn * dp, axis=-1, keepdims=True)
                    ds = pn * (dp - delta)
                    dsk_ref[h] += -jnp.sum(psink * delta)
                    db_ref[h] += ds
                    _unfold(ds_buf, g, rows, ds, prev_mask)
                    _unfold(p_buf, g, rows, pn, prev_mask)
            for g in range(N_KV):
                dsb = ds_buf[g]
                _unstack_heads(dq_ref, g, _dot(dsb, ks[g], 1, 0) * scale, F32)
                gs = pl.ds(g * HEAD_DIM, HEAD_DIM)
                dk_full[:, gs] = _dot(dsb, qs[g], 0, 0)
                dv_full[:, gs] = _dot(p_buf[g], douts[g], 0, 0)

        @pl.when(n == nb)
        def _():
            dk_full[...] = jnp.zeros_like(dk_full)
            dv_full[...] = jnp.zeros_like(dv_full)

        dk_ref[...] = dk_carry[...] + dk_full[pl.ds(0, BLOCK), :]
        dv_ref[...] = dv_carry[...] + dv_full[pl.ds(0, BLOCK), :]
        dk_carry[...] = dk_full[pl.ds(BLOCK, BLOCK), :]
        dv_carry[...] = dv_full[pl.ds(BLOCK, BLOCK), :]

    cur = lambda n: (jnp.minimum(n, nb - 1), 0)
    prev = lambda n: (jnp.maximum(jnp.minimum(n, nb - 1) - 1, 0), 0)
    out_kv = lambda n: (jnp.maximum(n - 1, 0), 0)
    return pl.pallas_call(
        body, name="attn_bwd", grid=(nb + 1,),
        in_specs=[pl.BlockSpec(memory_space=pltpu.SMEM), pl.BlockSpec((BLOCK, ATTN_DIM), cur),
                  pl.BlockSpec((BLOCK, KV_DIM), cur), pl.BlockSpec((BLOCK, KV_DIM), prev),
                  pl.BlockSpec((BLOCK, KV_DIM), cur), pl.BlockSpec((BLOCK, KV_DIM), prev), BIAS_SPEC,
                  pl.BlockSpec((BLOCK, ATTN_DIM), cur)],
        out_specs=[pl.BlockSpec((BLOCK, ATTN_DIM), cur), pl.BlockSpec((BLOCK, KV_DIM), out_kv),
                   pl.BlockSpec((BLOCK, KV_DIM), out_kv),
                   pl.BlockSpec((N_HEADS, BLOCK, BLOCK), lambda n: (0, 0, 0)),
                   pl.BlockSpec(memory_space=pltpu.SMEM)],
        out_shape=[jax.ShapeDtypeStruct((T, ATTN_DIM), F32), jax.ShapeDtypeStruct((T, KV_DIM), F32),
                   jax.ShapeDtypeStruct((T, KV_DIM), F32),
                   jax.ShapeDtypeStruct((N_HEADS, BLOCK, BLOCK), F32), jax.ShapeDtypeStruct((N_HEADS,), F32)],
        scratch_shapes=[pltpu.VMEM((2 * BLOCK, KV_DIM), F32), pltpu.VMEM((2 * BLOCK, KV_DIM), F32),
                        pltpu.VMEM((BLOCK, KV_DIM), F32), pltpu.VMEM((BLOCK, KV_DIM), F32),
                        pltpu.VMEM((N_KV, GROUP_ROWS, 2 * BLOCK), F32), pltpu.VMEM((N_KV, GROUP_ROWS, 2 * BLOCK), F32),
                        pltpu.VMEM((N_KV, GROUP_ROWS, 2 * BLOCK), BF), pltpu.VMEM((N_KV, GROUP_ROWS, 2 * BLOCK), BF)],
        compiler_params=_params(1),
    )(sinks, qn, kn, kn, vv, vv, bias, do)


def _coords():
    return lax.axis_index("x"), lax.axis_index("y"), lax.axis_index("c")


def _sum8(name, blocks):
    def body(b_ref, o_ref):
        tot = b_ref[0]
        for d in range(1, 8):
            tot = tot + b_ref[d]
        o_ref[...] = tot

    return pl.pallas_call(body, name=name, out_shape=jax.ShapeDtypeStruct(blocks.shape[1:], F32))(blocks)


HBM_SPEC = pl.BlockSpec(memory_space=pltpu.HBM)
SEM_SPEC = pl.BlockSpec(memory_space=pltpu.SEMAPHORE)
ANY_SPEC = pl.BlockSpec(memory_space=pl.ANY)
DATAFLOW = pltpu.SideEffectType.DATAFLOW_SIDE_EFFECTING


OTHER_CHIPS = (4, 2, 6)
ALL_OTHERS = (1, 2, 3, 4, 5, 6, 7)


def _slot(x, y, c, peers):
    return 2 * x + y if peers is OTHER_CHIPS else 4 * x + 2 * y + c


def _slot_copy(land, sems, idx, x, y, c, k, peers, arriving):
    send_sems, recv_sems = sems
    px, py, pc = x ^ (k >> 2), y ^ ((k >> 1) & 1), c ^ (k & 1)
    mine = _slot(x, y, c, peers)
    dst = _slot(px, py, pc, peers) if arriving else mine
    return pltpu.make_async_remote_copy(src_ref=land.at[mine], dst_ref=land.at[dst], send_sem=send_sems.at[idx],
                                        recv_sem=recv_sems.at[idx], device_id=(px, py, pc), device_id_type=MESH)


def _gather_start(name, stacks, groups, peers, after):
    n = len(stacks)
    ng = len(groups)
    np_ = len(peers)
    after = tuple(after)

    def body(*refs):
        lands = refs[:n]
        first = n + len(after)
        sems = [(refs[first + 2 * g], refs[first + 2 * g + 1]) for g in range(ng)]
        token = refs[-1]
        x, y, c = _coords()
        for g, members in enumerate(groups):
            for i, t in enumerate(members):
                for j, k in enumerate(peers):
                    _slot_copy(lands[t], sems[g], np_ * i + j, x, y, c, k, peers, arriving=False).start()
        token[...] = jnp.zeros_like(token)

    out_shape = []
    for members in groups:
        out_shape += [pltpu.SemaphoreType.DMA((np_ * len(members),))] * 2
    out_shape += [pltpu.HBM(w.shape, w.dtype) for w in stacks]
    out_shape.append(jax.ShapeDtypeStruct((8, 128), F32))
    res = pl.pallas_call(
        body, name=name, out_shape=out_shape, in_specs=[HBM_SPEC] * n + [ANY_SPEC] * len(after),
        out_specs=[SEM_SPEC] * (2 * ng) + [HBM_SPEC] * n + [pl.BlockSpec(memory_space=pltpu.VMEM)],
        input_output_aliases={t: 2 * ng + t for t in range(n)},
        compiler_params=pltpu.CompilerParams(has_side_effects=DATAFLOW),
    )(*[pltpu.with_memory_space_constraint(w, pltpu.HBM) for w in stacks], *after)
    sems = [(res[2 * g], res[2 * g + 1]) for g in range(ng)]
    return sems, list(res[2 * ng:2 * ng + n]), res[-1]


def _gather_wait(name, stacks, sems, peers, after):
    n = len(stacks)
    after = tuple(after)

    def body(*refs):
        lands = refs[:n]
        group_sems = (refs[n], refs[n + 1])
        x, y, c = _coords()
        for i in range(n):
            for j, k in enumerate(peers):
                cp = _slot_copy(lands[i], group_sems, len(peers) * i + j, x, y, c, k, peers, arriving=True)
                cp.wait_send()
                cp.wait_recv()

    return pl.pallas_call(
        body, name=name, out_shape=[pltpu.HBM(w.shape, w.dtype) for w in stacks],
        in_specs=[HBM_SPEC] * n + [SEM_SPEC, SEM_SPEC] + [ANY_SPEC] * len(after), out_specs=[HBM_SPEC] * n,
        input_output_aliases={t: t for t in range(n)},
        compiler_params=pltpu.CompilerParams(has_side_effects=DATAFLOW),
    )(*stacks, sems[0], sems[1], *after)


N_PEERS = 7


def _peer(x, y, c, k):
    return x ^ (k >> 2), y ^ ((k >> 1) & 1), c ^ (k & 1)


def _reduce_copy(grad, land, sems, idx, x, y, c, k):
    px, py, pc = _peer(x, y, c, k)
    rh = grad.shape[1] // 2
    return pltpu.make_async_remote_copy(src_ref=grad.at[2 * px + py, pl.ds(pc * rh, rh), :], dst_ref=land.at[k - 1],
                                        send_sem=sems[0].at[idx], recv_sem=sems[1].at[idx], device_id=(px, py, pc),
                                        device_id_type=MESH)


def _reduce_start(name, grads):
    n = len(grads)

    def body(*refs):
        src, lands, sems, token = refs[:n], refs[n:2 * n], (refs[2 * n], refs[2 * n + 1]), refs[-1]
        x, y, c = _coords()
        for t in range(n):
            for k in range(1, N_PEERS + 1):
                _reduce_copy(src[t], lands[t], sems, N_PEERS * t + k - 1, x, y, c, k).start()
        token[...] = jnp.zeros_like(token)

    lands = [lax.empty((N_PEERS, g.shape[1] // 2, g.shape[2]), g.dtype) for g in grads]
    out_shape = [pltpu.SemaphoreType.DMA((N_PEERS * n,))] * 2
    out_shape += [pltpu.HBM(a.shape, a.dtype) for a in list(grads) + lands]
    out_shape.append(jax.ShapeDtypeStruct((8, 128), F32))
    res = pl.pallas_call(
        body, name=name, out_shape=out_shape, in_specs=[HBM_SPEC] * (2 * n),
        out_specs=[SEM_SPEC] * 2 + [HBM_SPEC] * (2 * n) + [pl.BlockSpec(memory_space=pltpu.VMEM)],
        input_output_aliases={t: 2 + t for t in range(2 * n)},
        compiler_params=pltpu.CompilerParams(has_side_effects=DATAFLOW),
    )(*[pltpu.with_memory_space_constraint(a, pltpu.HBM) for a in list(grads) + lands])
    return (res[0], res[1]), list(res[2:2 + n]), list(res[2 + n:2 + 2 * n]), res[-1]


def _reduce_wait(name, grads, lands, sems, after):
    n = len(grads)
    after = tuple(after)

    def body(*refs):
        src, dst, group_sems = refs[:n], refs[n:2 * n], (refs[2 * n], refs[2 * n + 1])
        x, y, c = _coords()
        for t in range(n):
            for k in range(1, N_PEERS + 1):
                cp = _reduce_copy(src[t], dst[t], group_sems, N_PEERS * t + k - 1, x, y, c, k)
                cp.wait_send()
                cp.wait_recv()

    res = pl.pallas_call(
        body, name=name, out_shape=[pltpu.HBM(a.shape, a.dtype) for a in list(grads) + list(lands)],
        in_specs=[HBM_SPEC] * (2 * n) + [SEM_SPEC, SEM_SPEC] + [ANY_SPEC] * len(after), out_specs=[HBM_SPEC] * (2 * n),
        input_output_aliases={t: t for t in range(2 * n)},
        compiler_params=pltpu.CompilerParams(has_side_effects=DATAFLOW),
    )(*grads, *lands, sems[0], sems[1], *after)
    return list(res[:n]), list(res[n:])


def _join_copy(half, land, sems, idx, x, y, c):
    return pltpu.make_async_remote_copy(src_ref=half, dst_ref=land, send_sem=sems[0].at[idx], recv_sem=sems[1].at[idx],
                                        device_id=(x, y, 1 - c), device_id_type=MESH)


def _join_start(name, halves):
    n = len(halves)

    def body(*refs):
        src, lands, sems, token = refs[:n], refs[n:2 * n], (refs[2 * n], refs[2 * n + 1]), refs[-1]
        x, y, c = _coords()
        for t in range(n):
            _join_copy(src[t], lands[t], sems, t, x, y, c).start()
        token[...] = jnp.zeros_like(token)

    lands = [lax.empty(h.shape, h.dtype) for h in halves]
    out_shape = [pltpu.SemaphoreType.DMA((n,))] * 2
    out_shape += [pltpu.HBM(a.shape, a.dtype) for a in list(halves) + lands]
    out_shape.append(jax.ShapeDtypeStruct((8, 128), F32))
    res = pl.pallas_call(
        body, name=name, out_shape=out_shape, in_specs=[HBM_SPEC] * (2 * n),
        out_specs=[SEM_SPEC] * 2 + [HBM_SPEC] * (2 * n) + [pl.BlockSpec(memory_space=pltpu.VMEM)],
        input_output_aliases={t: 2 + t for t in range(2 * n)},
        compiler_params=pltpu.CompilerParams(has_side_effects=DATAFLOW),
    )(*[pltpu.with_memory_space_constraint(a, pltpu.HBM) for a in list(halves) + lands])
    return (res[0], res[1]), list(res[2:2 + n]), list(res[2 + n:2 + 2 * n]), res[-1]


def _join_wait(name, halves, lands, sems, after):
    n = len(halves)
    after = tuple(after)

    def body(*refs):
        src, dst, group_sems = refs[:n], refs[n:2 * n], (refs[2 * n], refs[2 * n + 1])
        x, y, c = _coords()
        for t in range(n):
            cp = _join_copy(src[t], dst[t], group_sems, t, x, y, c)
            cp.wait_send()
            cp.wait_recv()

    res = pl.pallas_call(
        body, name=name, out_shape=[pltpu.HBM(a.shape, a.dtype) for a in list(halves) + list(lands)],
        in_specs=[HBM_SPEC] * (2 * n) + [SEM_SPEC, SEM_SPEC] + [ANY_SPEC] * len(after), out_specs=[HBM_SPEC] * (2 * n),
        input_output_aliases={t: t for t in range(2 * n)},
        compiler_params=pltpu.CompilerParams(has_side_effects=DATAFLOW),
    )(*halves, *lands, sems[0], sems[1], *after)
    return list(res[:n]), list(res[n:])


def _join_halves(name, halves, deps=()):
    n = len(halves)

    def body(*refs):
        src, dst = refs[:n], refs[n + len(deps):2 * n + len(deps)]
        send_sems, recv_sems = refs[-2:]
        x, y, c = _coords()
        cps = []
        for t in range(n):
            cp = pltpu.make_async_remote_copy(src_ref=src[t], dst_ref=dst[t], send_sem=send_sems.at[t],
                                              recv_sem=recv_sems.at[t], device_id=(x, y, 1 - c), device_id_type=MESH)
            cp.start()
            cps.append(cp)
        for cp in cps:
            cp.wait()

    anyspec = pl.BlockSpec(memory_space=pl.ANY)
    return pl.pallas_call(
        body, name=name, out_shape=[jax.ShapeDtypeStruct(h.shape, h.dtype) for h in halves],
        in_specs=[anyspec] * (n + len(deps)), out_specs=[anyspec] * n,
        scratch_shapes=[pltpu.SemaphoreType.DMA((n,)), pltpu.SemaphoreType.DMA((n,))],
    )(*halves, *deps)


BF16_ROWS = 16
MAX_ROW_BLOCK = 512


def _row_block(rows):
    for rb in range(min(rows, MAX_ROW_BLOCK), 0, -1):
        if rows % rb == 0 and rb % BF16_ROWS == 0:
            return rb
    raise ValueError(rows)


def _sum_devices(name, grad, land, place):
    S, R, C = grad.shape
    rh = R // 2
    rb = _row_block(rh)
    nbh = rh // rb

    def body(place_ref, g_ref, l_ref, o_ref):
        tot = g_ref[...].astype(F32)
        for k in range(N_PEERS):
            tot = tot + l_ref[k].astype(F32)
        o_ref[...] = tot

    return pl.pallas_call(
        body, name=name,
        grid_spec=pltpu.PrefetchScalarGridSpec(
            num_scalar_prefetch=1, grid=(nbh,),
            in_specs=[pl.BlockSpec((None, rb, C), lambda r, place: (place[0], place[1] * nbh + r, 0)),
                      pl.BlockSpec((N_PEERS, rb, C), lambda r, place: (0, r, 0))],
            out_specs=pl.BlockSpec((rb, C), lambda r, place: (r, 0))),
        out_shape=jax.ShapeDtypeStruct((rh, C), F32), compiler_params=_params(1),
    )(place, grad, land)


def _adamw_math(w, g, m, v):
    m2 = ADAM_B1 * m + (1.0 - ADAM_B1) * g
    v2 = ADAM_B2 * v + (1.0 - ADAM_B2) * (g * g)
    m_hat = m2 / (1.0 - ADAM_B1 ** ADAM_STEP)
    v_hat = v2 / (1.0 - ADAM_B2 ** ADAM_STEP)
    delta = -ADAM_LR * (m_hat / (jnp.sqrt(v_hat) + ADAM_EPS) + ADAM_WD * w)
    return delta, m2, v2


def _adamw(name, w, m, v, gs):
    L, R, C = w.shape
    Rh = R // 2
    rb = _row_block(Rh)
    nbh = Rh // rb
    assert len(gs) == L

    def body(core_ref, w_ref, m_ref, v_ref, *rest):
        g_refs, (go_ref, d_ref, m2_ref, v2_ref) = rest[:2 * L], rest[2 * L:]
        layer, half = pl.program_id(0), pl.program_id(1)
        mine = half == core_ref[0]
        g = jnp.where(mine, g_refs[0][...], g_refs[1][...])
        for t in range(1, L):
            g = jnp.where(layer == t, jnp.where(mine, g_refs[2 * t][...], g_refs[2 * t + 1][...]), g)
        delta, m2, v2 = _adamw_math(w_ref[...], g, m_ref[...], v_ref[...])
        go_ref[...] = g
        d_ref[...] = delta
        m2_ref[...] = m2
        v2_ref[...] = v2

    wspec = pl.BlockSpec((None, rb, C), lambda l, h, r, core: (l, h * nbh + r, 0))
    gspec = pl.BlockSpec((rb, C), lambda l, h, r, core: (r, 0))
    return pl.pallas_call(
        body, name=name,
        grid_spec=pltpu.PrefetchScalarGridSpec(num_scalar_prefetch=1, grid=(L, 2, nbh),
                                               in_specs=[wspec] * 3 + [gspec] * (2 * L), out_specs=[wspec] * 4),
        out_shape=[jax.ShapeDtypeStruct((L, R, C), F32)] * 4, compiler_params=_params(3),
    )(lax.axis_index("c").astype(jnp.int32).reshape(1), w, m, v, *[g for pair in gs for g in pair])


def _adamw_small(ws, gs, ms, vs):
    n = len(ws)

    def body(*refs):
        w_refs, g_refs, m_refs, v_refs = (refs[k * n:(k + 1) * n] for k in range(4))
        d_refs, m2_refs, v2_refs = (refs[(4 + k) * n:(5 + k) * n] for k in range(3))
        for t in range(n):
            delta, m2, v2 = _adamw_math(w_refs[t][...], g_refs[t][...], m_refs[t][...], v_refs[t][...])
            d_refs[t][...] = delta
            m2_refs[t][...] = m2
            v2_refs[t][...] = v2

    res = pl.pallas_call(body, name="adamw_small", out_shape=[jax.ShapeDtypeStruct(w.shape, F32) for w in ws] * 3)(
        *ws, *gs, *ms, *vs)
    return res[:n], res[n:2 * n], res[2 * n:]


def _packed_rows(shape):
    c = shape[-1]
    return (int(np.prod(shape)) // c) * -(-c // LANES)


def _pack(arrays):
    total = sum(_packed_rows(a.shape) for a in arrays)
    total += -total % 8
    buf, r0 = None, 0
    for a in arrays:
        a = a.astype(F32).reshape(-1, a.shape[-1])
        r, c = a.shape
        k = -(-c // LANES)
        a = jnp.pad(a, ((0, 0), (0, k * LANES - c))).reshape(r * k, LANES)
        a = jnp.pad(a, ((r0, total - r0 - r * k), (0, 0)))
        buf = a if buf is None else buf + a
        r0 += r * k
    return buf


def _unpack(buf, shapes):
    out, r0 = [], 0
    for shp in shapes:
        c = shp[-1]
        rows = _packed_rows(shp)
        out.append(buf[r0:r0 + rows].reshape(-1, -(-c // LANES) * LANES)[:, :c].reshape(shp))
        r0 += rows
    return out


def _rms(x, g):
    return x * lax.rsqrt(jnp.mean(x * x, axis=-1, keepdims=True) + NORM_EPS) * g


def _residual_norm_ep(acc, *rest):
    *bias, res, gain = rest
    x = acc + res + (bias[0] if bias else 0.0)
    return x, _rms(x, gain)


RESIDUAL_NORM_OUTS = (("tile", F32), ("tile", BF))


def _mlp_up(tag, h, w_up_sm):
    (up,) = _mm(f"mlp{tag}_up", h, w_up_sm, nt=False, b_sm=True, tm=2048, tn=1024, rows=256,
                ep_fn=lambda acc: (acc,), outs=(("tile", BF),))
    return up


RMS_BWD_OUTS = (("tile", F32), ("tile", BF), ("colsum", F32), ("colsum", F32))


def _mlp_bwd(tag, dy, dy_bf, x, g, up, w_up_sm, w_down):
    (dup,) = _mm(f"mlp{tag}_dup", dy_bf, w_down, nt=True, tm=2048, tn=1024, rows=256, ep_in=((up, "tile"),),
                 ep_fn=lambda acc, u: (acc * (2.0 * jnp.maximum(u.astype(F32), 0.0)),), outs=(("tile", BF),))
    dx, dx_bf, dg, dx_sum = _mm(f"mlp{tag}_dx", dup, w_up_sm, nt=True, b_sm=True, tm=512, tn=1024, rows=256,
                                ep_in=((x, "tile"), (g, "row"), (dy, "tile")), ep_fn=_rms_bwd_ep, outs=RMS_BWD_OUTS)
    return dx, dx_bf, dg, dx_sum, dup


class _Reduction:
    def __init__(self, tag, grads, place):
        self.tag, self.place = tag, place
        self.sems, self.grads, self.lands, self.token = _reduce_start(f"reduce_start_{tag}", grads)

    def finish(self, after):
        grads, lands = _reduce_wait(f"reduce_wait_{self.tag}", self.grads, self.lands, self.sems, after)
        return [_sum_devices(f"reduce_sum_{self.tag}{i}", g, l, self.place) for i, (g, l) in enumerate(zip(grads, lands))]


def kernel(x, conv_norm_g, conv_w_in, conv_b_in, conv_dw, conv_dw_b, conv_ln_g, conv_ln_b, conv_w_out, conv_b_out, attn_norm_g, w_qkv, b_qkv, q_norm_g, k_norm_g, sinks, w_o, b_o, rel_bias, mlp_norm_g, w_up, w_down, loss_target, m_conv_norm_g, m_conv_w_in, m_conv_b_in, m_conv_dw, m_conv_dw_b, m_conv_ln_g, m_conv_ln_b, m_conv_w_out, m_conv_b_out, m_attn_norm_g, m_w_qkv, m_b_qkv, m_q_norm_g, m_k_norm_g, m_sinks, m_w_o, m_b_o, m_rel_bias, m_mlp_norm_g, m_w_up, m_w_down, v_conv_norm_g, v_conv_w_in, v_conv_b_in, v_conv_dw, v_conv_dw_b, v_conv_ln_g, v_conv_ln_b, v_conv_w_out, v_conv_b_out, v_attn_norm_g, v_w_qkv, v_b_qkv, v_q_norm_g, v_k_norm_g, v_sinks, v_w_o, v_b_o, v_rel_bias, v_mlp_norm_g, v_w_up, v_w_down):
    Dm = D_MODEL
    x2d = x[0]
    tgt = loss_target[0]
    T = x2d.shape[0]
    shard = 2 * lax.axis_index("x") + lax.axis_index("y")

    me = 2 * shard + lax.axis_index("c")

    def own_slot(block, slots, index):
        return lax.dynamic_update_slice(lax.empty((slots,) + block.shape, block.dtype), block[None],
                                        (index,) + (0,) * block.ndim)

    (conv_in_sems,), (stack_in,), first_token = _gather_start(
        "gather_start_conv_in", [own_slot(conv_w_in[0].astype(BF), N_SHARD, shard)], ((0,),), OTHER_CHIPS, after=())
    sharded_small = [conv_dw[0], attn_norm_g, b_qkv, b_o]
    (small_sems,), (small_land,), small_token = _gather_start(
        "small_weights_start", [own_slot(_pack(sharded_small), 8, me)], ((0,),), ALL_OTHERS, after=(first_token,))

    big = [conv_w_out[0], jnp.swapaxes(w_qkv, 1, 2)[0], w_o[0], w_up[0], w_up[1], w_down[0], w_down[1]]
    stacks = [own_slot(w.astype(BF), N_SHARD, shard) for w in big]
    groups = ((0,), (3, 5), (1, 2), (4, 6))
    gather_sems, stacks, gather_token = _gather_start("gather_start", stacks, groups, OTHER_CHIPS, after=(small_token,))

    def gathered_group(g, name, after):
        return _gather_wait(name, [stacks[t] for t in groups[g]], gather_sems[g], OTHER_CHIPS, after)

    bucket = _bucket_table()
    bias = _bias_table(rel_bias.T, bucket)

    h0 = _rms_fwd("conv_norm", x2d, conv_norm_g, deps=(gather_token,))
    (w_in_sm,) = _gather_wait("gather_wait_conv_in", [stack_in], conv_in_sems, OTHER_CHIPS, (h0, bias))
    (u,) = _mm("conv_in", h0, w_in_sm, nt=False, b_sm=True, tm=2048, tn=512, rows=256, ep_in=((conv_b_in, "row"),),
               ep_fn=lambda acc, b: (acc + b,), outs=(("tile", BF),))
    (gathered,) = _gather_wait("small_weights_wait", [small_land], small_sems, ALL_OTHERS, (u,))
    chips = [_unpack(gathered[2 * s], [a.shape for a in sharded_small]) for s in range(N_SHARD)]
    dw_f, attn_norm_f, b_qkv_f, b_o_f = (jnp.concatenate([chips[s][t] for s in range(N_SHARD)], axis=-1)
                                         for t in range(len(sharded_small)))
    dw_pad = jnp.pad(dw_f, ((0, HALO - CONV_W), (0, 0)))
    cv, s_act = _conv_fwd(u, dw_pad, conv_dw_b, conv_ln_g, conv_ln_b)
    (g_out,) = gathered_group(0, "gather_wait_conv_out", (s_act,))
    w_out_f = g_out.reshape(Dm, Dm)
    x1, h1 = _mm("conv_out", s_act, w_out_f, nt=False, tm=1024, tn=1024, rows=256,
                 ep_in=((conv_b_out, "row"), (x2d, "tile"), (mlp_norm_g[0:1], "row")), ep_fn=_residual_norm_ep,
                 outs=RESIDUAL_NORM_OUTS)

    g_up0, g_down0 = gathered_group(1, "gather_wait_mlp0", (x1,))
    w_up_sm = [g_up0, None]
    w_down_f = [g_down0.reshape(D_FF, Dm), None]
    up0 = _mlp_up(0, h1, w_up_sm[0])
    x2, h2 = _mm("mlp0_down", up0, w_down_f[0], nt=False, tm=512, tn=1024, rows=256, a_fn=_relu2,
                 ep_in=((x1, "tile"), (attn_norm_f, "row")), ep_fn=_residual_norm_ep, outs=RESIDUAL_NORM_OUTS)

    g_qkv, g_o = gathered_group(2, "gather_wait_attn", (x2,))
    w_qkv_t = g_qkv.reshape(QKV_DIM, Dm)
    w_o_f = g_o.reshape(ATTN_DIM, Dm)
    qg_t = jnp.tile(q_norm_g, (1, N_HEADS))
    kg_t = jnp.tile(k_norm_g, (1, N_KV))

    def qkv_ep(acc, b, qg, kg, ones):
        proj = acc + b
        q, k, v = proj[:, :ATTN_DIM], proj[:, ATTN_DIM:ATTN_DIM + KV_DIM], proj[:, ATTN_DIM + KV_DIM:]
        return proj, _qk_normed(q, qg, ones, 1.0 / math.sqrt(HEAD_DIM)), _qk_normed(k, kg, ones, 1.0), v

    qkv, qn, kn, vv = _mm(
        "attn_qkv", h2, w_qkv_t, nt=True, tm=1024, tn=QKV_DIM, rows=256, ep_fn=qkv_ep,
        ep_in=((b_qkv_f, "row"), (qg_t, "whole"), (kg_t, "whole"), (_head_ones(), "whole")),
        outs=(("tile", F32), ("tile", BF, ATTN_DIM), ("tile", BF, KV_DIM), ("tile", BF, KV_DIM)))
    sinks1 = sinks[0]
    att = _attn_fwd(qn, kn, vv, bias, sinks1)
    x3, h3 = _mm("attn_out", att, w_o_f, nt=False, tm=1024, tn=1024, rows=256,
                 ep_in=((b_o_f, "row"), (x2, "tile"), (mlp_norm_g[1:2], "row")), ep_fn=_residual_norm_ep,
                 outs=RESIDUAL_NORM_OUTS)

    g_up1, g_down1 = gathered_group(3, "gather_wait_mlp1", (x3,))
    w_up_sm[1] = g_up1
    w_down_f[1] = g_down1.reshape(D_FF, Dm)
    up1 = _mlp_up(1, h3, w_up_sm[1])

    def loss_ep(acc, r, t):
        diff = acc + r - t
        dy = diff * (1.0 / Dm)
        return dy, dy, jnp.sum(diff * diff, axis=0, keepdims=True)

    dy, dy_bf, sq = _mm("mlp1_down_loss", up1, w_down_f[1], nt=False, tm=512, tn=1024, rows=256, a_fn=_relu2,
                        ep_in=((x3, "tile"), (tgt, "tile")), ep_fn=loss_ep,
                        outs=(("tile", F32), ("tile", BF), ("colsum", F32)))

    place = jnp.stack([shard, lax.axis_index("c")]).astype(jnp.int32)
    dx3, dx3_bf, dg_mlp1, db_o, dup1 = _mlp_bwd(1, dy, dy_bf, x3, mlp_norm_g[1:2], up1, w_up_sm[1], w_down_f[1])
    dw_down1 = _mm_tn("mlp1_dw_down", up1, dy_bf, tm=1024, tn=1024, tk=2048, a_fn=_relu2)
    dw_up1 = _mm_tn("mlp1_dw_up", h3, dup1, tm=1024, tn=1024, tk=2048, out_sm=N_SHARD)
    red_mlp1 = _Reduction("mlp1", [dw_up1, dw_down1.reshape(N_SHARD, D_FF // N_SHARD, Dm)], place)

    ident = lambda acc: (acc,)
    (datt,) = _mm("attn_dout", dx3_bf, w_o_f, nt=True, tm=1024, tn=1024, rows=256, ep_fn=ident, outs=(("tile", BF),),
                  deps=(red_mlp1.token,))
    dw_o = _mm_tn("attn_dw_o", att, dx3_bf, tm=1024, tn=1024, tk=2048)
    dqn, dkn, dvv, dbias, dsinks = _attn_bwd(qn, kn, vv, bias, sinks1, datt)
    drel = _bias_grad(dbias, bucket)
    dqkv, db_qkv, dqg_t, dkg_t = _qk_norm_bwd(qkv, dqn, dkn, dvv, qg_t, kg_t)
    dw_qkv_t = _mm_tn("attn_dw_qkv", dqkv, h2, tm=QKV_DIM, tn=1024, tk=2048)
    red_attn = _Reduction("attn", [dw_qkv_t.reshape(N_SHARD, QKV_DIM // N_SHARD, Dm),
                                   dw_o.reshape(N_SHARD, ATTN_DIM // N_SHARD, Dm)], place)
    dx2, dx2_bf, dg_attn, _ = _mm("attn_dx", dqkv, w_qkv_t, nt=False, tm=1024, tn=1024, rows=256,
                                  ep_in=((x2, "tile"), (attn_norm_f, "row"), (dx3, "tile")), ep_fn=_rms_bwd_ep,
                                  outs=RMS_BWD_OUTS, deps=(red_attn.token,))

    dx1, dx1_bf, dg_mlp0, db_out, dup0 = _mlp_bwd(0, dx2, dx2_bf, x1, mlp_norm_g[0:1], up0, w_up_sm[0], w_down_f[0])
    dw_down0 = _mm_tn("mlp0_dw_down", up0, dx2_bf, tm=1024, tn=1024, tk=2048, a_fn=_relu2)
    dw_up0 = _mm_tn("mlp0_dw_up", h1, dup0, tm=1024, tn=1024, tk=2048, out_sm=N_SHARD)
    dw_out = _mm_tn("conv_dw_out", s_act, dx1_bf, tm=1024, tn=1024, tk=2048)
    red_mlp0 = _Reduction("mlp0", [dw_up0, dw_down0.reshape(N_SHARD, D_FF // N_SHARD, Dm),
                                   dw_out.reshape(N_SHARD, Dm // N_SHARD, Dm)], place)
    (r_qkv, r_o) = red_attn.finish((dx1,))
    (r_up1, r_down1) = red_mlp1.finish((dx1,))

    dcv, dln_g, dln_b, ddw_b = _mm("conv_ds", dx1_bf, w_out_f, nt=True, tm=1024, tn=1024, rows=256,
                                   ep_in=((cv, "tile"), (conv_ln_g, "row"), (conv_ln_b, "row")),
                                   ep_fn=_ln_silu_bwd_ep,
                                   outs=(("tile", F32), ("colsum", F32), ("colsum", F32), ("colsum", F32)),
                                   deps=(red_mlp0.token,))
    du, db_in, ddw8 = _conv_bwd(u, dcv, dw_pad)
    (r_up0, r_down0, r_out) = red_mlp0.finish((du,))
    early = [r_out, r_qkv, r_o, r_up0, r_up1, r_down0, r_down1]
    join_sems, early, early_lands, join_token = _join_start("join_start", early)
    dw_in = _mm_tn("conv_dw_in", h0, du, tm=1024, tn=512, tk=4096, out_sm=N_SHARD)
    red_conv = _Reduction("conv", [dw_in], place)
    def first_layer_ep(*args):
        tot, _, dg, _ = _rms_bwd_ep(*args)
        return tot, dg

    gx, dg_conv = _mm("conv_dx", du, w_in_sm, nt=True, b_sm=True, tm=1024, tn=1024, rows=256,
                      ep_in=((x2d, "tile"), (conv_norm_g, "row"), (dx1, "tile")), ep_fn=first_layer_ep,
                      outs=(("tile", F32), ("colsum", F32)), deps=(red_conv.token, join_token))
    (r_in,) = red_conv.finish((gx,))

    dqg = dqg_t.reshape(N_HEADS, HEAD_DIM).sum(axis=0, keepdims=True)
    dkg = dkg_t.reshape(N_KV, HEAD_DIM).sum(axis=0, keepdims=True)
    small_full = [dg_conv, db_in, ddw8.sum(axis=1)[:CONV_W], ddw_b, dln_g, dln_b, db_out, dg_attn, db_qkv, dqg, dkg,
                  dsinks[None, :], db_o, drel.reshape(1, REL_BUCKETS * N_HEADS),
                  jnp.pad(dg_mlp0, ((0, 1), (0, 0))) + jnp.pad(dg_mlp1, ((1, 0), (0, 0))), sq]
    (sg_sems,), (sg_land,), sg_token = _gather_start(
        "small_grads_start", [own_slot(_pack(small_full), 8, me)], ((0,),), ALL_OTHERS, after=())

    early, early_sibling = _join_wait("join_wait", early, early_lands, join_sems, (gx, sg_token))
    r_out, r_qkv, r_o, r_up0, r_up1, r_down0, r_down1 = zip(early, early_sibling)
    r_in = (r_in,) + tuple(_join_halves("join_halves", [r_in], deps=(sg_token,)))

    big_out = {}
    qkv_t = [jnp.swapaxes(a, 1, 2) for a in (w_qkv, m_w_qkv, v_w_qkv)]
    for nm, w, m, v, gs in (("conv_w_in", conv_w_in, m_conv_w_in, v_conv_w_in, (r_in,)),
                            ("conv_w_out", conv_w_out, m_conv_w_out, v_conv_w_out, (r_out,)),
                            ("w_qkv", *qkv_t, (r_qkv,)),
                            ("w_o", w_o, m_w_o, v_w_o, (r_o,)),
                            ("w_up", w_up, m_w_up, v_w_up, (r_up0, r_up1)),
                            ("w_down", w_down, m_w_down, v_w_down, (r_down0, r_down1))):
        big_out[nm] = _adamw(f"adamw_{nm}", w, m, v, gs)

    (sg_land,) = _gather_wait("small_grads_wait", [sg_land], sg_sems, ALL_OTHERS,
                              [big_out[nm][0] for nm in big_out])
    big_out["w_qkv"] = tuple(jnp.swapaxes(a, 1, 2) for a in big_out["w_qkv"])
    small_sum = _sum8("small_grads_sum", sg_land)
    (r_norm, r_b_in, r_dw, r_dw_b, r_ln_g, r_ln_b, r_b_out, r_attn_norm, r_b_qkv, r_qg, r_kg, r_sinks, r_b_o, r_rel,
     r_mlp_norm, r_sq) = _unpack(small_sum, [a.shape for a in small_full])
    loss = 0.5 * jnp.sum(r_sq) * (1.0 / Dm)

    def cols(a, width):
        return lax.dynamic_slice_in_dim(a, shard * width, width, axis=a.ndim - 1)

    small_names = ["conv_norm_g", "conv_b_in", "conv_dw", "conv_dw_b", "conv_ln_g", "conv_ln_b", "conv_b_out",
                   "attn_norm_g", "b_qkv", "q_norm_g", "k_norm_g", "sinks", "b_o", "rel_bias", "mlp_norm_g"]
    small_g = [r_norm, r_b_in, cols(r_dw, Dm // N_SHARD)[None], r_dw_b, r_ln_g, r_ln_b, r_b_out,
               cols(r_attn_norm, Dm // N_SHARD), cols(r_b_qkv, QKV_DIM // N_SHARD), r_qg, r_kg, r_sinks,
               cols(r_b_o, Dm // N_SHARD), r_rel.reshape(N_HEADS, REL_BUCKETS), r_mlp_norm]
    small_w = [conv_norm_g, conv_b_in, conv_dw, conv_dw_b, conv_ln_g, conv_ln_b, conv_b_out, attn_norm_g, b_qkv,
               q_norm_g, k_norm_g, sinks, b_o, rel_bias.T, mlp_norm_g]
    small_m = [m_conv_norm_g, m_conv_b_in, m_conv_dw, m_conv_dw_b, m_conv_ln_g, m_conv_ln_b, m_conv_b_out,
               m_attn_norm_g, m_b_qkv, m_q_norm_g, m_k_norm_g, m_sinks, m_b_o, m_rel_bias.T, m_mlp_norm_g]
    small_v = [v_conv_norm_g, v_conv_b_in, v_conv_dw, v_conv_dw_b, v_conv_ln_g, v_conv_ln_b, v_conv_b_out,
               v_attn_norm_g, v_b_qkv, v_q_norm_g, v_k_norm_g, v_sinks, v_b_o, v_rel_bias.T, v_mlp_norm_g]
    flat2 = lambda a: a.reshape(-1, a.shape[-1])
    small_g = [flat2(g) for g in small_g]
    d_s, m_s, v_s = _adamw_small([flat2(w) for w in small_w], small_g, [flat2(m) for m in small_m],
                                 [flat2(v) for v in small_v])
    small_out = {}
    for nm, w, g, d, m2, v2 in zip(small_names, small_w, small_g, d_s, m_s, v_s):
        small_out[nm] = tuple(a.reshape(w.shape) for a in (g, d, m2, v2))
    small_out["rel_bias"] = tuple(a.T for a in small_out["rel_bias"])

    order = ["conv_norm_g", "conv_w_in", "conv_b_in", "conv_dw", "conv_dw_b", "conv_ln_g", "conv_ln_b", "conv_w_out",
             "conv_b_out", "attn_norm_g", "w_qkv", "b_qkv", "q_norm_g", "k_norm_g", "sinks", "w_o", "b_o", "rel_bias",
             "mlp_norm_g", "w_up", "w_down"]
    res = {**small_out, **big_out}
    outs = [loss, gx[None]]
    for part in range(4):
        outs += [res[nm][part] for nm in order]
    return tuple(outs)
```

```python
import math

import numpy as np
import jax
import jax.numpy as jnp
from jax import lax
from jax.experimental import pallas as pl
from jax.experimental.pallas import tpu as pltpu

F32 = jnp.float32
BF = jnp.bfloat16
MESH = pl.DeviceIdType.MESH

D_MODEL = 1024
D_FF = 4096
N_HEADS = 16
N_KV = 2
GROUP = N_HEADS // N_KV
HEAD_DIM = 64
ATTN_DIM = N_HEADS * HEAD_DIM
KV_DIM = N_KV * HEAD_DIM
QKV_DIM = ATTN_DIM + 2 * KV_DIM
BLOCK = 128
CONV_W = 31
HALO = 32
REL_BUCKETS = 32
REL_MAX_DIST = 128
NORM_EPS = 1e-6
NEG_INF = -1e30
N_SHARD = 4
LANES = 1024

ADAM_LR = 0.001
ADAM_B1 = 0.9
ADAM_B2 = 0.999
ADAM_EPS = 1e-08
ADAM_WD = 0.01
ADAM_STEP = 10

VMEM_LIMIT = 56 * 1024 * 1024


def _params(n_axes):
    return pltpu.CompilerParams(dimension_semantics=("arbitrary",) * n_axes, vmem_limit_bytes=VMEM_LIMIT)


def _dot(a, b, ca, cb):
    return lax.dot_general(a, b, (((ca,), (cb,)), ((), ())), preferred_element_type=F32)


def _mm(name, a, b, *, nt, tm, tn, ep_fn, outs, a_fn=None, b_sm=False, ep_in=(), deps=(), rows=None):
    M, K = a.shape
    rows = tm if rows is None else rows
    if b_sm:
        S, ks = b.shape[0], b.shape[2]
        N, per = (b.shape[1], None) if nt else (S * b.shape[2], b.shape[2] // tn)
        assert (S * ks == K) if nt else (b.shape[1] == K)
    else:
        N = b.shape[0] if nt else b.shape[1]
        assert (b.shape[1] if nt else b.shape[0]) == K
    assert M % tm == 0 and N % tn == 0 and tm % rows == 0
    ne, no, nd = len(ep_in), len(outs), len(deps)

    def body(a_ref, b_ref, *rest):
        ep_refs, out_refs = rest[:ne], rest[ne + nd:ne + nd + no]
        i = pl.program_id(1)
        sums = [None] * no
        for r in range(tm // rows):
            rs = pl.ds(r * rows, rows)

            def lhs(cols):
                av = a_ref[rs, cols]
                return (av if a_fn is None else a_fn(av)).astype(BF)

            if b_sm and nt:
                acc = None
                for s in range(S):
                    part = _dot(lhs(pl.ds(s * ks, ks)), b_ref[s].astype(BF), 1, 1)
                    acc = part if acc is None else acc + part
            else:
                acc = _dot(lhs(slice(None)), b_ref[...].astype(BF), 1, 1 if nt else 0)
            ep_vals = [ref[rs, :] if kind == "tile" else ref[...] for ref, (_, kind) in zip(ep_refs, ep_in)]
            vals = ep_fn(acc, *ep_vals)
            for o, ((kind, dt, *_), ref, val) in enumerate(zip(outs, out_refs, vals)):
                if kind == "tile":
                    ref[rs, :] = val.astype(dt)
                else:
                    sums[o] = val if sums[o] is None else sums[o] + val
        for (kind, *_), ref, val in zip(outs, out_refs, sums):
            if kind == "colsum":
                @pl.when(i == 0)
                def _():
                    ref[...] = val

                @pl.when(i > 0)
                def _():
                    ref[...] += val

    if b_sm and nt:
        b_spec = pl.BlockSpec((S, tn, ks), lambda j, i: (0, j, 0))
    elif b_sm:
        b_spec = pl.BlockSpec((None, K, tn), lambda j, i: (j // per, 0, j % per))
    elif nt:
        b_spec = pl.BlockSpec((tn, K), lambda j, i: (j, 0))
    else:
        b_spec = pl.BlockSpec((K, tn), lambda j, i: (0, j))
    in_specs = [pl.BlockSpec((tm, K), lambda j, i: (i, 0)), b_spec]
    for arr, kind in ep_in:
        if kind == "tile":
            assert arr.shape == (M, N)
            in_specs.append(pl.BlockSpec((tm, tn), lambda j, i: (i, j)))
        elif kind == "whole":
            in_specs.append(pl.BlockSpec(arr.shape, lambda j, i, rank=arr.ndim: (0,) * rank))
        else:
            assert arr.shape == (1, N)
            in_specs.append(pl.BlockSpec((1, tn), lambda j, i: (0, j)))
    in_specs += [pl.BlockSpec(memory_space=pl.ANY)] * nd
    out_shape, out_specs = [], []
    for kind, dt, *width in outs:
        if kind == "tile" and width:
            assert tn == N
            out_shape.append(jax.ShapeDtypeStruct((M, width[0]), dt))
            out_specs.append(pl.BlockSpec((tm, width[0]), lambda j, i: (i, 0)))
        elif kind == "tile":
            out_shape.append(jax.ShapeDtypeStruct((M, N), dt))
            out_specs.append(pl.BlockSpec((tm, tn), lambda j, i: (i, j)))
        else:
            out_shape.append(jax.ShapeDtypeStruct((1, N), F32))
            out_specs.append(pl.BlockSpec((1, tn), lambda j, i: (0, j)))
    return pl.pallas_call(
        body, name=name, grid=(N // tn, M // tm), in_specs=in_specs, out_specs=out_specs, out_shape=out_shape,
        compiler_params=_params(2),
    )(a, b, *[arr for arr, _ in ep_in], *deps)


def _mm_tn(name, a, b, *, tm, tn, tk, a_fn=None, out_sm=None):
    T, Ka = a.shape
    N = b.shape[1]
    assert b.shape[0] == T and T % tk == 0 and Ka % tm == 0 and N % tn == 0
    nk = T // tk

    def body(a_ref, b_ref, o_ref, acc_ref):
        k = pl.program_id(2)

        @pl.when(k == 0)
        def _():
            acc_ref[...] = jnp.zeros_like(acc_ref)

        av = a_ref[...]
        if a_fn is not None:
            av = a_fn(av)
        acc_ref[...] += _dot(av.astype(BF), b_ref[...].astype(BF), 0, 0)

        @pl.when(k == nk - 1)
        def _():
            o_ref[...] = acc_ref[...].astype(BF)

    if out_sm is None:
        out_shape = jax.ShapeDtypeStruct((Ka, N), BF)
        out_spec = pl.BlockSpec((tm, tn), lambda i, j, k: (i, j))
    else:
        per = (N // out_sm) // tn
        assert per * tn * out_sm == N
        out_shape = jax.ShapeDtypeStruct((out_sm, Ka, N // out_sm), BF)
        out_spec = pl.BlockSpec((None, tm, tn), lambda i, j, k: (j // per, i, j % per))
    return pl.pallas_call(
        body, name=name, grid=(Ka // tm, N // tn, nk),
        in_specs=[pl.BlockSpec((tk, tm), lambda i, j, k: (k, i)), pl.BlockSpec((tk, tn), lambda i, j, k: (k, j))],
        out_specs=out_spec, out_shape=out_shape, scratch_shapes=[pltpu.VMEM((tm, tn), F32)],
        compiler_params=_params(3),
    )(a, b)


def _relu2(v):
    r = jnp.maximum(v.astype(F32), 0.0)
    return r * r


def _rms_bwd_ep(dh, x, g, dres):
    rstd = lax.rsqrt(jnp.mean(x * x, axis=-1, keepdims=True) + NORM_EPS)
    xh = x * rstd
    dxh = dh * g
    dx = rstd * (dxh - xh * jnp.mean(dxh * xh, axis=-1, keepdims=True))
    tot = dres + dx
    return tot, tot, jnp.sum(dh * xh, axis=0, keepdims=True), jnp.sum(tot, axis=0, keepdims=True)


def _rms_fwd(name, x, g, tm=512, deps=()):
    T, Dm = x.shape

    def body(x_ref, g_ref, *rest):
        o_ref = rest[-1]
        xv = x_ref[...]
        rstd = lax.rsqrt(jnp.mean(xv * xv, axis=-1, keepdims=True) + NORM_EPS)
        o_ref[...] = (xv * rstd * g_ref[...]).astype(BF)

    return pl.pallas_call(
        body, name=name, grid=(T // tm,),
        in_specs=[pl.BlockSpec((tm, Dm), lambda i: (i, 0)), pl.BlockSpec((1, Dm), lambda i: (0, 0))]
        + [pl.BlockSpec(memory_space=pl.ANY)] * len(deps),
        out_specs=pl.BlockSpec((tm, Dm), lambda i: (i, 0)), out_shape=jax.ShapeDtypeStruct((T, Dm), BF),
        compiler_params=_params(1),
    )(x, g, *deps)


HEAD_GROUP = 256


def _head_sum(v, ones):
    n = v.shape[1]
    w = min(n, HEAD_GROUP)
    blk = ones[:w, :w]
    parts = [_dot(v[:, c:c + w].astype(BF), blk, 1, 0) for c in range(0, n, w)]
    return parts[0] if len(parts) == 1 else jnp.concatenate(parts, axis=1)


def _head_ones():
    idx = np.arange(HEAD_GROUP) // HEAD_DIM
    return jnp.asarray((idx[:, None] == idx[None, :]).astype(np.float32), dtype=BF)


def _qk_normed(x, g, ones, scale):
    r = lax.rsqrt(_head_sum(x * x, ones) * (1.0 / HEAD_DIM) + NORM_EPS)
    return x * r * g * scale


def _qk_norm_bwd(qkv, dqn, dkn, dv, qg_t, kg_t, tm=256):
    T = qkv.shape[0]

    def body(x_ref, dq_ref, dk_ref, dv_ref, qg_ref, kg_ref, ones_ref, o_ref, db_ref, dqg_ref, dkg_ref):
        i = pl.program_id(0)
        ones = ones_ref[...]

        def one(x, dy, g):
            r = lax.rsqrt(_head_sum(x * x, ones) * (1.0 / HEAD_DIM) + NORM_EPS)
            xh = x * r
            dxh = dy * g
            dx = r * (dxh - xh * (_head_sum(dxh * xh, ones) * (1.0 / HEAD_DIM)))
            return dx, jnp.sum(dy * xh, axis=0, keepdims=True)

        dq, dqg = one(x_ref[:, pl.ds(0, ATTN_DIM)], dq_ref[...], qg_ref[...])
        dk, dkg = one(x_ref[:, pl.ds(ATTN_DIM, KV_DIM)], dk_ref[...], kg_ref[...])
        dvv = dv_ref[...]
        o_ref[:, pl.ds(0, ATTN_DIM)] = dq.astype(BF)
        o_ref[:, pl.ds(ATTN_DIM, KV_DIM)] = dk.astype(BF)
        o_ref[:, pl.ds(ATTN_DIM + KV_DIM, KV_DIM)] = dvv.astype(BF)
        sq, sk, sv = (jnp.sum(t, axis=0, keepdims=True) for t in (dq, dk, dvv))

        @pl.when(i == 0)
        def _():
            db_ref[:, pl.ds(0, ATTN_DIM)] = sq
            db_ref[:, pl.ds(ATTN_DIM, KV_DIM)] = sk
            db_ref[:, pl.ds(ATTN_DIM + KV_DIM, KV_DIM)] = sv
            dqg_ref[...] = dqg
            dkg_ref[...] = dkg

        @pl.when(i > 0)
        def _():
            db_ref[:, pl.ds(0, ATTN_DIM)] += sq
            db_ref[:, pl.ds(ATTN_DIM, KV_DIM)] += sk
            db_ref[:, pl.ds(ATTN_DIM + KV_DIM, KV_DIM)] += sv
            dqg_ref[...] += dqg
            dkg_ref[...] += dkg

    full = lambda shape: pl.BlockSpec(shape, lambda i: (0, 0))
    row = lambda n: pl.BlockSpec((tm, n), lambda i: (i, 0))
    return pl.pallas_call(
        body, name="qk_norm_bwd", grid=(T // tm,),
        in_specs=[row(QKV_DIM), row(ATTN_DIM), row(KV_DIM), row(KV_DIM), full((1, ATTN_DIM)), full((1, KV_DIM)),
                  full((HEAD_GROUP, HEAD_GROUP))],
        out_specs=[row(QKV_DIM), full((1, QKV_DIM)), full((1, ATTN_DIM)), full((1, KV_DIM))],
        out_shape=[jax.ShapeDtypeStruct((T, QKV_DIM), BF), jax.ShapeDtypeStruct((1, QKV_DIM), F32),
                   jax.ShapeDtypeStruct((1, ATTN_DIM), F32), jax.ShapeDtypeStruct((1, KV_DIM), F32)],
        compiler_params=_params(1),
    )(qkv, dqn, dkn, dv, qg_t, kg_t, _head_ones())


ROWS = 128
COLS = 128


SUBLANES = 8
FIRST_TAP = HALO - (CONV_W - 1)


def _glu(a, g):
    return a.astype(F32) * jax.nn.sigmoid(g.astype(F32))


def _shifted(xe, s):
    return xe if s == 0 else pltpu.roll(xe, ROWS + HALO - s, axis=0)


def _conv_fwd(u, dw_pad, dw_b, ln_g, ln_b, tm=1024):
    T = u.shape[0]
    Dm = D_MODEL
    hpt = tm // HALO

    def body(ac_ref, gc_ref, ap_ref, gp_ref, w_ref, wb_ref, lg_ref, lb_ref, cv_ref, s_ref, ext):
        i = pl.program_id(0)
        ext[pl.ds(0, HALO), :] = jnp.where(i > 0, _glu(ap_ref[...], gp_ref[...]), 0.0)
        ext[pl.ds(HALO, tm), :] = _glu(ac_ref[...], gc_ref[...])

        def rows(r, carry):
            r0 = pl.multiple_of(r * ROWS, ROWS)
            for c in range(Dm // COLS):
                cs = pl.ds(c * COLS, COLS)
                xe = ext[pl.ds(r0, ROWS + HALO), cs]
                acc = jnp.zeros((ROWS, COLS), F32)
                for s in range(SUBLANES):
                    xs = _shifted(xe, s)
                    for j in range(CONV_W):
                        off = FIRST_TAP + j
                        if off % SUBLANES == s:
                            acc = acc + xs[off - s:off - s + ROWS, :] * w_ref[pl.ds(j, 1), cs]
                cv_ref[pl.ds(r0, ROWS), cs] = acc + wb_ref[:, cs]
            return carry

        lax.fori_loop(0, tm // ROWS, rows, 0)
        cv = cv_ref[...]
        xc = cv - jnp.mean(cv, axis=-1, keepdims=True)
        y = xc * lax.rsqrt(jnp.mean(xc * xc, axis=-1, keepdims=True) + NORM_EPS) * lg_ref[...] + lb_ref[...]
        s_ref[...] = (y * jax.nn.sigmoid(y)).astype(BF)

    full = lambda shape: pl.BlockSpec(shape, lambda i: (0, 0))
    return pl.pallas_call(
        body, name="conv_fwd", grid=(T // tm,),
        in_specs=[pl.BlockSpec((tm, Dm), lambda i: (i, 0)), pl.BlockSpec((tm, Dm), lambda i: (i, 1)),
                  pl.BlockSpec((HALO, Dm), lambda i: (jnp.maximum(i * hpt - 1, 0), 0)),
                  pl.BlockSpec((HALO, Dm), lambda i: (jnp.maximum(i * hpt - 1, 0), 1)),
                  full((HALO, Dm)), full((1, Dm)), full((1, Dm)), full((1, Dm))],
        out_specs=[pl.BlockSpec((tm, Dm), lambda i: (i, 0)), pl.BlockSpec((tm, Dm), lambda i: (i, 0))],
        out_shape=[jax.ShapeDtypeStruct((T, Dm), F32), jax.ShapeDtypeStruct((T, Dm), BF)],
        scratch_shapes=[pltpu.VMEM((tm + HALO, Dm), F32)],
        compiler_params=_params(1),
    )(u, u, u, u, dw_pad, dw_b, ln_g, ln_b)


def _ln_silu_bwd_ep(ds, cv, lg, lb):
    xc = cv - jnp.mean(cv, axis=-1, keepdims=True)
    rstd = lax.rsqrt(jnp.mean(xc * xc, axis=-1, keepdims=True) + NORM_EPS)
    xh = xc * rstd
    y = xh * lg + lb
    sg = jax.nn.sigmoid(y)
    dy = ds * (sg * (1.0 + y * (1.0 - sg)))
    dxh = dy * lg
    dcv = rstd * (dxh - jnp.mean(dxh, axis=-1, keepdims=True) - xh * jnp.mean(dxh * xh, axis=-1, keepdims=True))
    return (dcv, jnp.sum(dy * xh, axis=0, keepdims=True), jnp.sum(dy, axis=0, keepdims=True),
            jnp.sum(dcv, axis=0, keepdims=True))


def _conv_bwd(u, dcv, dw_pad, tm=1024):
    T = u.shape[0]
    Dm = D_MODEL
    hpt = tm // HALO
    last = T // HALO - 1
    nt = T // tm

    def body(ac_ref, gc_ref, ap_ref, gp_ref, dc_ref, dn_ref, w_ref, du_ref, db_ref, dw_ref, ext_g, ext_d):
        i = pl.program_id(0)
        ext_g[pl.ds(0, HALO), :] = jnp.where(i > 0, _glu(ap_ref[...], gp_ref[...]), 0.0)
        ext_g[pl.ds(HALO, tm), :] = _glu(ac_ref[...], gc_ref[...])
        ext_d[pl.ds(0, tm), :] = dc_ref[...]
        ext_d[pl.ds(tm, HALO), :] = jnp.where(i < nt - 1, dn_ref[...], 0.0)

        @pl.when(i == 0)
        def _():
            db_ref[...] = jnp.zeros_like(db_ref)
            dw_ref[...] = jnp.zeros_like(dw_ref)

        def rows(r, carry):
            r0 = pl.multiple_of(r * ROWS, ROWS)
            rs = pl.ds(r0, ROWS)
            for c in range(Dm // COLS):
                cs = pl.ds(c * COLS, COLS)
                cs2 = pl.ds(Dm + c * COLS, COLS)
                de = ext_d[pl.ds(r0, ROWS + HALO), cs]
                ge = ext_g[pl.ds(r0, ROWS + HALO), cs]
                dcur = de[0:ROWS, :]
                acc = jnp.zeros((ROWS, COLS), F32)
                for s in range(SUBLANES):
                    ds_, gs_ = _shifted(de, s), _shifted(ge, s)
                    for j in range(CONV_W):
                        off = CONV_W - 1 - j
                        if off % SUBLANES == s:
                            acc = acc + ds_[off - s:off - s + ROWS, :] * w_ref[pl.ds(j, 1), cs]
                        goff = FIRST_TAP + j
                        if goff % SUBLANES == s:
                            prod = dcur * gs_[goff - s:goff - s + ROWS, :]
                            dw_ref[j, :, cs] += jnp.sum(prod.reshape(ROWS // SUBLANES, SUBLANES, COLS), axis=0)
                a = ac_ref[rs, cs].astype(F32)
                sg = jax.nn.sigmoid(gc_ref[rs, cs].astype(F32))
                da = acc * sg
                dg = acc * a * sg * (1.0 - sg)
                du_ref[rs, cs] = da.astype(BF)
                du_ref[rs, cs2] = dg.astype(BF)
                db_ref[:, cs] += jnp.sum(da, axis=0, keepdims=True)
                db_ref[:, cs2] += jnp.sum(dg, axis=0, keepdims=True)
            return carry

        lax.fori_loop(0, tm // ROWS, rows, 0)

    return pl.pallas_call(
        body, name="conv_bwd", grid=(nt,),
        in_specs=[pl.BlockSpec((tm, Dm), lambda i: (i, 0)), pl.BlockSpec((tm, Dm), lambda i: (i, 1)),
                  pl.BlockSpec((HALO, Dm), lambda i: (jnp.maximum(i * hpt - 1, 0), 0)),
                  pl.BlockSpec((HALO, Dm), lambda i: (jnp.maximum(i * hpt - 1, 0), 1)),
                  pl.BlockSpec((tm, Dm), lambda i: (i, 0)),
                  pl.BlockSpec((HALO, Dm), lambda i: (jnp.minimum((i + 1) * hpt, last), 0)),
                  pl.BlockSpec((HALO, Dm), lambda i: (0, 0))],
        out_specs=[pl.BlockSpec((tm, 2 * Dm), lambda i: (i, 0)), pl.BlockSpec((1, 2 * Dm), lambda i: (0, 0)),
                   pl.BlockSpec((HALO, 8, Dm), lambda i: (0, 0, 0))],
        out_shape=[jax.ShapeDtypeStruct((T, 2 * Dm), BF), jax.ShapeDtypeStruct((1, 2 * Dm), F32),
                   jax.ShapeDtypeStruct((HALO, 8, Dm), F32)],
        scratch_shapes=[pltpu.VMEM((tm + HALO, Dm), F32), pltpu.VMEM((tm + HALO, Dm), F32)],
        compiler_params=_params(1),
    )(u, u, u, u, dcv, dcv, dw_pad)


def _bucket_table():
    q_loc = np.arange(BLOCK)[:, None]
    k_loc = np.arange(2 * BLOCK)[None, :]
    dist = q_loc + BLOCK - k_loc
    n = np.maximum(dist, 0)
    max_exact = REL_BUCKETS // 2
    large = max_exact + (np.log(np.maximum(n, 1).astype(np.float32) / max_exact)
                         / math.log(REL_MAX_DIST / max_exact) * (REL_BUCKETS - max_exact)).astype(np.int32)
    large = np.minimum(large, REL_BUCKETS - 1)
    bucket = np.where(n < max_exact, n, large).astype(np.int32)
    band = np.where((dist >= 0) & (dist < BLOCK), bucket, -1)
    folded = np.where(np.arange(BLOCK)[None, :] > q_loc, band[:, :BLOCK], band[:, BLOCK:])
    assert (folded >= 0).all() and ((band[:, :BLOCK] >= 0) != (band[:, BLOCK:] >= 0)).all()
    return jnp.asarray(folded.astype(np.int32))


def _prev_mask():
    row = lax.broadcasted_iota(jnp.int32, (BLOCK, BLOCK), 0)
    col = lax.broadcasted_iota(jnp.int32, (BLOCK, BLOCK), 1)
    return col > row


def _fold(band, prev_mask):
    return jnp.where(prev_mask, band[:, :BLOCK], band[:, BLOCK:])


def _unfold(ref, g, rows, folded, prev_mask):
    ref[g, rows, pl.ds(0, BLOCK)] = jnp.where(prev_mask, folded, 0.0).astype(ref.dtype)
    ref[g, rows, pl.ds(BLOCK, BLOCK)] = jnp.where(prev_mask, 0.0, folded).astype(ref.dtype)


def _bias_table(rel_bias_t, bucket):
    def body(rb_ref, bk_ref, o_ref):
        bk = bk_ref[...]
        prev_mask = _prev_mask()
        for h in range(N_HEADS):
            acc = jnp.zeros((BLOCK, BLOCK), F32)
            for b in range(REL_BUCKETS):
                acc = jnp.where(bk == b, rb_ref[h, b], acc)
            o_ref[0, h] = acc
            o_ref[1, h] = jnp.where(prev_mask, NEG_INF, acc)

    return pl.pallas_call(
        body, name="bias_table", out_shape=jax.ShapeDtypeStruct((2, N_HEADS, BLOCK, BLOCK), F32),
        in_specs=[pl.BlockSpec(memory_space=pltpu.SMEM), pl.BlockSpec(memory_space=pltpu.VMEM)],
        out_specs=pl.BlockSpec(memory_space=pltpu.VMEM),
    )(rel_bias_t, bucket)


def _bias_grad(dbias, bucket):
    def body(db_ref, bk_ref, o_ref):
        bk = bk_ref[...]
        for b in range(REL_BUCKETS):
            sel = bk == b
            for h in range(N_HEADS):
                o_ref[h, b] = jnp.sum(jnp.where(sel, db_ref[h], 0.0))

    return pl.pallas_call(
        body, name="bias_grad", out_shape=jax.ShapeDtypeStruct((N_HEADS, REL_BUCKETS), F32),
        in_specs=[pl.BlockSpec(memory_space=pltpu.VMEM), pl.BlockSpec(memory_space=pltpu.VMEM)],
        out_specs=pl.BlockSpec(memory_space=pltpu.SMEM),
    )(dbias, bucket)


GROUP_ROWS = GROUP * BLOCK
BIAS_SPEC = pl.BlockSpec((2, N_HEADS, BLOCK, BLOCK), lambda n: (0, 0, 0, 0))


def _head_probs(qk, bias_h, sink, prev_mask):
    s = _fold(qk, prev_mask) + bias_h
    m = jnp.maximum(jnp.max(s, axis=-1, keepdims=True), sink)
    p = jnp.exp(s - m)
    ps = jnp.exp(sink - m)
    inv = 1.0 / (jnp.sum(p, axis=-1, keepdims=True) + ps)
    return p * inv, ps * inv


def _band(prev_ref, cur_ref, g):
    hs = pl.ds(g * HEAD_DIM, HEAD_DIM)
    return jnp.concatenate([prev_ref[:, hs], cur_ref[:, hs]], axis=0)


def _stack_heads(ref, g):
    return jnp.concatenate([ref[:, pl.ds((g * GROUP + hh) * HEAD_DIM, HEAD_DIM)] for hh in range(GROUP)], axis=0)


def _unstack_heads(ref, g, stacked, dtype):
    for hh in range(GROUP):
        ref[:, pl.ds((g * GROUP + hh) * HEAD_DIM, HEAD_DIM)] = stacked[hh * BLOCK:(hh + 1) * BLOCK, :].astype(dtype)


def _head_rows(hh):
    return pl.ds(hh * BLOCK, BLOCK)


def _attn_fwd(qn, kn, vv, bias, sinks):
    T = qn.shape[0]
    nb = T // BLOCK

    def body(sk_ref, q_ref, kc_ref, kp_ref, vc_ref, vp_ref, b_ref, o_ref, qk_buf, p_buf):
        table = (pl.program_id(0) == 0).astype(jnp.int32)
        prev_mask = _prev_mask()
        for g in range(N_KV):
            qk_buf[g] = _dot(_stack_heads(q_ref, g), _band(kp_ref, kc_ref, g), 1, 1)
        for g in range(N_KV):
            for hh in range(GROUP):
                h = g * GROUP + hh
                pn, _ = _head_probs(qk_buf[g, _head_rows(hh), :], b_ref[table, h], sk_ref[h], prev_mask)
                _unfold(p_buf, g, _head_rows(hh), pn, prev_mask)
        for g in range(N_KV):
            _unstack_heads(o_ref, g, _dot(p_buf[g], _band(vp_ref, vc_ref, g), 1, 0), BF)

    cur = lambda n: (n, 0)
    prev = lambda n: (jnp.maximum(n - 1, 0), 0)
    return pl.pallas_call(
        body, name="attn_fwd", grid=(nb,),
        in_specs=[pl.BlockSpec(memory_space=pltpu.SMEM), pl.BlockSpec((BLOCK, ATTN_DIM), cur),
                  pl.BlockSpec((BLOCK, KV_DIM), cur), pl.BlockSpec((BLOCK, KV_DIM), prev),
                  pl.BlockSpec((BLOCK, KV_DIM), cur), pl.BlockSpec((BLOCK, KV_DIM), prev), BIAS_SPEC],
        out_specs=pl.BlockSpec((BLOCK, ATTN_DIM), cur), out_shape=jax.ShapeDtypeStruct((T, ATTN_DIM), BF),
        scratch_shapes=[pltpu.VMEM((N_KV, GROUP_ROWS, 2 * BLOCK), F32), pltpu.VMEM((N_KV, GROUP_ROWS, 2 * BLOCK), BF)],
        compiler_params=_params(1),
    )(sinks, qn, kn, kn, vv, vv, bias)


def _attn_bwd(qn, kn, vv, bias, sinks, do):
    T = qn.shape[0]
    nb = T // BLOCK
    scale = 1.0 / math.sqrt(HEAD_DIM)

    def body(sk_ref, q_ref, kc_ref, kp_ref, vc_ref, vp_ref, b_ref, do_ref,
             dq_ref, dk_ref, dv_ref, db_ref, dsk_ref, dk_full, dv_full, dk_carry, dv_carry, qk_buf, dp_buf, p_buf, ds_buf):
        n = pl.program_id(0)

        @pl.when(n == 0)
        def _():
            db_ref[...] = jnp.zeros_like(db_ref)
            dk_carry[...] = jnp.zeros_like(dk_carry)
            dv_carry[...] = jnp.zeros_like(dv_carry)
            for h in range(N_HEADS):
                dsk_ref[h] = 0.0

        @pl.when(n < nb)
        def _():
            table = (n == 0).astype(jnp.int32)
            prev_mask = _prev_mask()
            ks = [_band(kp_ref, kc_ref, g) for g in range(N_KV)]
            qs = [_stack_heads(q_ref, g) for g in range(N_KV)]
            douts = [_stack_heads(do_ref, g) for g in range(N_KV)]
            for g in range(N_KV):
                qk_buf[g] = _dot(qs[g], ks[g], 1, 1)
                dp_buf[g] = _dot(douts[g], _band(vp_ref, vc_ref, g), 1, 1)
            for g in range(N_KV):
                for hh in range(GROUP):
                    h = g * GROUP + hh
                    rows = _head_rows(hh)
                    pn, psink = _head_probs(qk_buf[g, rows, :], b_ref[table, h], sk_ref[h], prev_mask)
                    dp = _fold(dp_buf[g, rows, :], prev_mask)
                    delta = jnp.sum(pn * dp, axis=-1, keepdims=True)
                    ds = pn * (dp - delta)
                    dsk_ref[h] += -jnp.sum(psink * delta)
                    db_ref[h] += ds
                    _unfold(ds_buf, g, rows, ds, prev_mask)
                    _unfold(p_buf, g, rows, pn, prev_mask)
            for g in range(N_KV):
                dsb = ds_buf[g]
                _unstack_heads(dq_ref, g, _dot(dsb, ks[g], 1, 0) * scale, F32)
                gs = pl.ds(g * HEAD_DIM, HEAD_DIM)
                dk_full[:, gs] = _dot(dsb, qs[g], 0, 0)
                dv_full[:, gs] = _dot(p_buf[g], douts[g], 0, 0)

        @pl.when(n == nb)
        def _():
            dk_full[...] = jnp.zeros_like(dk_full)
            dv_full[...] = jnp.zeros_like(dv_full)

        dk_ref[...] = dk_carry[...] + dk_full[pl.ds(0, BLOCK), :]
        dv_ref[...] = dv_carry[...] + dv_full[pl.ds(0, BLOCK), :]
        dk_carry[...] = dk_full[pl.ds(BLOCK, BLOCK), :]
        dv_carry[...] = dv_full[pl.ds(BLOCK, BLOCK), :]

    cur = lambda n: (jnp.minimum(n, nb - 1), 0)
    prev = lambda n: (jnp.maximum(jnp.minimum(n, nb - 1) - 1, 0), 0)
    out_kv = lambda n: (jnp.maximum(n - 1, 0), 0)
    return pl.pallas_call(
        body, name="attn_bwd", grid=(nb + 1,),
        in_specs=[pl.BlockSpec(memory_space=pltpu.SMEM), pl.BlockSpec((BLOCK, ATTN_DIM), cur),
                  pl.BlockSpec((BLOCK, KV_DIM), cur), pl.BlockSpec((BLOCK, KV_DIM), prev),
                  pl.BlockSpec((BLOCK, KV_DIM), cur), pl.BlockSpec((BLOCK, KV_DIM), prev), BIAS_SPEC,
                  pl.BlockSpec((BLOCK, ATTN_DIM), cur)],
        out_specs=[pl.BlockSpec((BLOCK, ATTN_DIM), cur), pl.BlockSpec((BLOCK, KV_DIM), out_kv),
                   pl.BlockSpec((BLOCK, KV_DIM), out_kv),
                   pl.BlockSpec((N_HEADS, BLOCK, BLOCK), lambda n: (0, 0, 0)),
                   pl.BlockSpec(memory_space=pltpu.SMEM)],
        out_shape=[jax.ShapeDtypeStruct((T, ATTN_DIM), F32), jax.ShapeDtypeStruct((T, KV_DIM), F32),
                   jax.ShapeDtypeStruct((T, KV_DIM), F32),
                   jax.ShapeDtypeStruct((N_HEADS, BLOCK, BLOCK), F32), jax.ShapeDtypeStruct((N_HEADS,), F32)],
        scratch_shapes=[pltpu.VMEM((2 * BLOCK, KV_DIM), F32), pltpu.VMEM((2 * BLOCK, KV_DIM), F32),
                        pltpu.VMEM((BLOCK, KV_DIM), F32), pltpu.VMEM((BLOCK, KV_DIM), F32),
                        pltpu.VMEM((N_KV, GROUP_ROWS, 2 * BLOCK), F32), pltpu.VMEM((N_KV, GROUP_ROWS, 2 * BLOCK), F32),
                        pltpu.VMEM((N_KV, GROUP_ROWS, 2 * BLOCK), BF), pltpu.VMEM((N_KV, GROUP_ROWS, 2 * BLOCK), BF)],
        compiler_params=_params(1),
    )(sinks, qn, kn, kn, vv, vv, bias, do)


def _coords():
    return lax.axis_index("x"), lax.axis_index("y"), lax.axis_index("c")


def _sum8(name, blocks):
    def body(b_ref, o_ref):
        tot = b_ref[0]
        for d in range(1, 8):
            tot = tot + b_ref[d]
        o_ref[...] = tot

    return pl.pallas_call(body, name=name, out_shape=jax.ShapeDtypeStruct(blocks.shape[1:], F32))(blocks)


HBM_SPEC = pl.BlockSpec(memory_space=pltpu.HBM)
SEM_SPEC = pl.BlockSpec(memory_space=pltpu.SEMAPHORE)
ANY_SPEC = pl.BlockSpec(memory_space=pl.ANY)
DATAFLOW = pltpu.SideEffectType.DATAFLOW_SIDE_EFFECTING


OTHER_CHIPS = (4, 2, 6)
ALL_OTHERS = (1, 2, 3, 4, 5, 6, 7)


def _slot(x, y, c, peers):
    return 2 * x + y if peers is OTHER_CHIPS else 4 * x + 2 * y + c


def _slot_copy(land, sems, idx, x, y, c, k, peers, arriving):
    send_sems, recv_sems = sems
    px, py, pc = x ^ (k >> 2), y ^ ((k >> 1) & 1), c ^ (k & 1)
    mine = _slot(x, y, c, peers)
    dst = _slot(px, py, pc, peers) if arriving else mine
    return pltpu.make_async_remote_copy(src_ref=land.at[mine], dst_ref=land.at[dst], send_sem=send_sems.at[idx],
                                        recv_sem=recv_sems.at[idx], device_id=(px, py, pc), device_id_type=MESH)


def _gather_start(name, stacks, groups, peers, after):
    n = len(stacks)
    ng = len(groups)
    np_ = len(peers)
    after = tuple(after)

    def body(*refs):
        lands = refs[:n]
        first = n + len(after)
        sems = [(refs[first + 2 * g], refs[first + 2 * g + 1]) for g in range(ng)]
        token = refs[-1]
        x, y, c = _coords()
        for g, members in enumerate(groups):
            for i, t in enumerate(members):
                for j, k in enumerate(peers):
                    _slot_copy(lands[t], sems[g], np_ * i + j, x, y, c, k, peers, arriving=False).start()
        token[...] = jnp.zeros_like(token)

    out_shape = []
    for members in groups:
        out_shape += [pltpu.SemaphoreType.DMA((np_ * len(members),))] * 2
    out_shape += [pltpu.HBM(w.shape, w.dtype) for w in stacks]
    out_shape.append(jax.ShapeDtypeStruct((8, 128), F32))
    res = pl.pallas_call(
        body, name=name, out_shape=out_shape, in_specs=[HBM_SPEC] * n + [ANY_SPEC] * len(after),
        out_specs=[SEM_SPEC] * (2 * ng) + [HBM_SPEC] * n + [pl.BlockSpec(memory_space=pltpu.VMEM)],
        input_output_aliases={t: 2 * ng + t for t in range(n)},
        compiler_params=pltpu.CompilerParams(has_side_effects=DATAFLOW),
    )(*[pltpu.with_memory_space_constraint(w, pltpu.HBM) for w in stacks], *after)
    sems = [(res[2 * g], res[2 * g + 1]) for g in range(ng)]
    return sems, list(res[2 * ng:2 * ng + n]), res[-1]


def _gather_wait(name, stacks, sems, peers, after):
    n = len(stacks)
    after = tuple(after)

    def body(*refs):
        lands = refs[:n]
        group_sems = (refs[n], refs[n + 1])
        x, y, c = _coords()
        for i in range(n):
            for j, k in enumerate(peers):
                cp = _slot_copy(lands[i], group_sems, len(peers) * i + j, x, y, c, k, peers, arriving=True)
                cp.wait_send()
                cp.wait_recv()

    return pl.pallas_call(
        body, name=name, out_shape=[pltpu.HBM(w.shape, w.dtype) for w in stacks],
        in_specs=[HBM_SPEC] * n + [SEM_SPEC, SEM_SPEC] + [ANY_SPEC] * len(after), out_specs=[HBM_SPEC] * n,
        input_output_aliases={t: t for t in range(n)},
        compiler_params=pltpu.CompilerParams(has_side_effects=DATAFLOW),
    )(*stacks, sems[0], sems[1], *after)


N_PEERS = 7


def _peer(x, y, c, k):
    return x ^ (k >> 2), y ^ ((k >> 1) & 1), c ^ (k & 1)


def _reduce_copy(grad, land, sems, idx, x, y, c, k):
    px, py, pc = _peer(x, y, c, k)
    rh = grad.shape[1] // 2
    return pltpu.make_async_remote_copy(src_ref=grad.at[2 * px + py, pl.ds(pc * rh, rh), :], dst_ref=land.at[k - 1],
                                        send_sem=sems[0].at[idx], recv_sem=sems[1].at[idx], device_id=(px, py, pc),
                                        device_id_type=MESH)


def _reduce_start(name, grads):
    n = len(grads)

    def body(*refs):
        src, lands, sems, token = refs[:n], refs[n:2 * n], (refs[2 * n], refs[2 * n + 1]), refs[-1]
        x, y, c = _coords()
        for t in range(n):
            for k in range(1, N_PEERS + 1):
                _reduce_copy(src[t], lands[t], sems, N_PEERS * t + k - 1, x, y, c, k).start()
        token[...] = jnp.zeros_like(token)

    lands = [lax.empty((N_PEERS, g.shape[1] // 2, g.shape[2]), g.dtype) for g in grads]
    out_shape = [pltpu.SemaphoreType.DMA((N_PEERS * n,))] * 2
    out_shape += [pltpu.HBM(a.shape, a.dtype) for a in list(grads) + lands]
    out_shape.append(jax.ShapeDtypeStruct((8, 128), F32))
    res = pl.pallas_call(
        body, name=name, out_shape=out_shape, in_specs=[HBM_SPEC] * (2 * n),
        out_specs=[SEM_SPEC] * 2 + [HBM_SPEC] * (2 * n) + [pl.BlockSpec(memory_space=pltpu.VMEM)],
        input_output_aliases={t: 2 + t for t in range(2 * n)},
        compiler_params=pltpu.CompilerParams(has_side_effects=DATAFLOW),
    )(*[pltpu.with_memory_space_constraint(a, pltpu.HBM) for a in list(grads) + lands])
    return (res[0], res[1]), list(res[2:2 + n]), list(res[2 + n:2 + 2 * n]), res[-1]


def _reduce_wait(name, grads, lands, sems, after):
    n = len(grads)
    after = tuple(after)

    def body(*refs):
        src, dst, group_sems = refs[:n], refs[n:2 * n], (refs[2 * n], refs[2 * n + 1])
        x, y, c = _coords()
        for t in range(n):
            for k in range(1, N_PEERS + 1):
                cp = _reduce_copy(src[t], dst[t], group_sems, N_PEERS * t + k - 1, x, y, c, k)
                cp.wait_send()
                cp.wait_recv()

    res = pl.pallas_call(
        body, name=name, out_shape=[pltpu.HBM(a.shape, a.dtype) for a in list(grads) + list(lands)],
        in_specs=[HBM_SPEC] * (2 * n) + [SEM_SPEC, SEM_SPEC] + [ANY_SPEC] * len(after), out_specs=[HBM_SPEC] * (2 * n),
        input_output_aliases={t: t for t in range(2 * n)},
        compiler_params=pltpu.CompilerParams(has_side_effects=DATAFLOW),
    )(*grads, *lands, sems[0], sems[1], *after)
    return list(res[:n]), list(res[n:])


def _join_copy(half, land, sems, idx, x, y, c):
    return pltpu.make_async_remote_copy(src_ref=half, dst_ref=land, send_sem=sems[0].at[idx], recv_sem=sems[1].at[idx],
                                        device_id=(x, y, 1 - c), device_id_type=MESH)


def _join_start(name, halves):
    n = len(halves)

    def body(*refs):
        src, lands, sems, token = refs[:n], refs[n:2 * n], (refs[2 * n], refs[2 * n + 1]), refs[-1]
        x, y, c = _coords()
        for t in range(n):
            _join_copy(src[t], lands[t], sems, t, x, y, c).start()
        token[...] = jnp.zeros_like(token)

    lands = [lax.empty(h.shape, h.dtype) for h in halves]
    out_shape = [pltpu.SemaphoreType.DMA((n,))] * 2
    out_shape += [pltpu.HBM(a.shape, a.dtype) for a in list(halves) + lands]
    out_shape.append(jax.ShapeDtypeStruct((8, 128), F32))
    res = pl.pallas_call(
        body, name=name, out_shape=out_shape, in_specs=[HBM_SPEC] * (2 * n),
        out_specs=[SEM_SPEC] * 2 + [HBM_SPEC] * (2 * n) + [pl.BlockSpec(memory_space=pltpu.VMEM)],
        input_output_aliases={t: 2 + t for t in range(2 * n)},
        compiler_params=pltpu.CompilerParams(has_side_effects=DATAFLOW),
    )(*[pltpu.with_memory_space_constraint(a, pltpu.HBM) for a in list(halves) + lands])
    return (res[0], res[1]), list(res[2:2 + n]), list(res[2 + n:2 + 2 * n]), res[-1]


def _join_wait(name, halves, lands, sems, after):
    n = len(halves)
    after = tuple(after)

    def body(*refs):
        src, dst, group_sems = refs[:n], refs[n:2 * n], (refs[2 * n], refs[2 * n + 1])
        x, y, c = _coords()
        for t in range(n):
            cp = _join_copy(src[t], dst[t], group_sems, t, x, y, c)
            cp.wait_send()
            cp.wait_recv()

    res = pl.pallas_call(
        body, name=name, out_shape=[pltpu.HBM(a.shape, a.dtype) for a in list(halves) + list(lands)],
        in_specs=[HBM_SPEC] * (2 * n) + [SEM_SPEC, SEM_SPEC] + [ANY_SPEC] * len(after), out_specs=[HBM_SPEC] * (2 * n),
        input_output_aliases={t: t for t in range(2 * n)},
        compiler_params=pltpu.CompilerParams(has_side_effects=DATAFLOW),
    )(*halves, *lands, sems[0], sems[1], *after)
    return list(res[:n]), list(res[n:])


def _join_halves(name, halves, deps=()):
    n = len(halves)

    def body(*refs):
        src, dst = refs[:n], refs[n + len(deps):2 * n + len(deps)]
        send_sems, recv_sems = refs[-2:]
        x, y, c = _coords()
        cps = []
        for t in range(n):
            cp = pltpu.make_async_remote_copy(src_ref=src[t], dst_ref=dst[t], send_sem=send_sems.at[t],
                                              recv_sem=recv_sems.at[t], device_id=(x, y, 1 - c), device_id_type=MESH)
            cp.start()
            cps.append(cp)
        for cp in cps:
            cp.wait()

    anyspec = pl.BlockSpec(memory_space=pl.ANY)
    return pl.pallas_call(
        body, name=name, out_shape=[jax.ShapeDtypeStruct(h.shape, h.dtype) for h in halves],
        in_specs=[anyspec] * (n + len(deps)), out_specs=[anyspec] * n,
        scratch_shapes=[pltpu.SemaphoreType.DMA((n,)), pltpu.SemaphoreType.DMA((n,))],
    )(*halves, *deps)


BF16_ROWS = 16
MAX_ROW_BLOCK = 512


def _row_block(rows):
    for rb in range(min(rows, MAX_ROW_BLOCK), 0, -1):
        if rows % rb == 0 and rb % BF16_ROWS == 0:
            return rb
    raise ValueError(rows)


def _sum_devices(name, grad, land, place):
    S, R, C = grad.shape
    rh = R // 2
    rb = _row_block(rh)
    nbh = rh // rb

    def body(place_ref, g_ref, l_ref, o_ref):
        tot = g_ref[...].astype(F32)
        for k in range(N_PEERS):
            tot = tot + l_ref[k].astype(F32)
        o_ref[...] = tot

    return pl.pallas_call(
        body, name=name,
        grid_spec=pltpu.PrefetchScalarGridSpec(
            num_scalar_prefetch=1, grid=(nbh,),
            in_specs=[pl.BlockSpec((None, rb, C), lambda r, place: (place[0], place[1] * nbh + r, 0)),
                      pl.BlockSpec((N_PEERS, rb, C), lambda r, place: (0, r, 0))],
            out_specs=pl.BlockSpec((rb, C), lambda r, place: (r, 0))),
        out_shape=jax.ShapeDtypeStruct((rh, C), F32), compiler_params=_params(1),
    )(place, grad, land)


def _adamw_math(w, g, m, v):
    m2 = ADAM_B1 * m + (1.0 - ADAM_B1) * g
    v2 = ADAM_B2 * v + (1.0 - ADAM_B2) * (g * g)
    m_hat = m2 / (1.0 - ADAM_B1 ** ADAM_STEP)
    v_hat = v2 / (1.0 - ADAM_B2 ** ADAM_STEP)
    delta = -ADAM_LR * (m_hat / (jnp.sqrt(v_hat) + ADAM_EPS) + ADAM_WD * w)
    return delta, m2, v2


def _adamw(name, w, m, v, gs):
    L, R, C = w.shape
    Rh = R // 2
    rb = _row_block(Rh)
    nbh = Rh // rb
    assert len(gs) == L

    def body(core_ref, w_ref, m_ref, v_ref, *rest):
        g_refs, (go_ref, d_ref, m2_ref, v2_ref) = rest[:2 * L], rest[2 * L:]
        layer, half = pl.program_id(0), pl.program_id(1)
        mine = half == core_ref[0]
        g = jnp.where(mine, g_refs[0][...], g_refs[1][...])
        for t in range(1, L):
            g = jnp.where(layer == t, jnp.where(mine, g_refs[2 * t][...], g_refs[2 * t + 1][...]), g)
        delta, m2, v2 = _adamw_math(w_ref[...], g, m_ref[...], v_ref[...])
        go_ref[...] = g
        d_ref[...] = delta
        m2_ref[...] = m2
        v2_ref[...] = v2

    wspec = pl.BlockSpec((None, rb, C), lambda l, h, r, core: (l, h * nbh + r, 0))
    gspec = pl.BlockSpec((rb, C), lambda l, h, r, core: (r, 0))
    return pl.pallas_call(
        body, name=name,
        grid_spec=pltpu.PrefetchScalarGridSpec(num_scalar_prefetch=1, grid=(L, 2, nbh),
                                               in_specs=[wspec] * 3 + [gspec] * (2 * L), out_specs=[wspec] * 4),
        out_shape=[jax.ShapeDtypeStruct((L, R, C), F32)] * 4, compiler_params=_params(3),
    )(lax.axis_index("c").astype(jnp.int32).reshape(1), w, m, v, *[g for pair in gs for g in pair])


def _adamw_small(ws, gs, ms, vs):
    n = len(ws)

    def body(*refs):
        w_refs, g_refs, m_refs, v_refs = (refs[k * n:(k + 1) * n] for k in range(4))
        d_refs, m2_refs, v2_refs = (refs[(4 + k) * n:(5 + k) * n] for k in range(3))
        for t in range(n):
            delta, m2, v2 = _adamw_math(w_refs[t][...], g_refs[t][...], m_refs[t][...], v_refs[t][...])
            d_refs[t][...] = delta
            m2_refs[t][...] = m2
            v2_refs[t][...] = v2

    res = pl.pallas_call(body, name="adamw_small", out_shape=[jax.ShapeDtypeStruct(w.shape, F32) for w in ws] * 3)(
        *ws, *gs, *ms, *vs)
    return res[:n], res[n:2 * n], res[2 * n:]


def _packed_rows(shape):
    c = shape[-1]
    return (int(np.prod(shape)) // c) * -(-c // LANES)


def _pack(arrays):
    total = sum(_packed_rows(a.shape) for a in arrays)
    total += -total % 8
    buf, r0 = None, 0
    for a in arrays:
        a = a.astype(F32).reshape(-1, a.shape[-1])
        r, c = a.shape
        k = -(-c // LANES)
        a = jnp.pad(a, ((0, 0), (0, k * LANES - c))).reshape(r * k, LANES)
        a = jnp.pad(a, ((r0, total - r0 - r * k), (0, 0)))
        buf = a if buf is None else buf + a
        r0 += r * k
    return buf


def _unpack(buf, shapes):
    out, r0 = [], 0
    for shp in shapes:
        c = shp[-1]
        rows = _packed_rows(shp)
        out.append(buf[r0:r0 + rows].reshape(-1, -(-c // LANES) * LANES)[:, :c].reshape(shp))
        r0 += rows
    return out


def _rms(x, g):
    return x * lax.rsqrt(jnp.mean(x * x, axis=-1, keepdims=True) + NORM_EPS) * g


def _residual_norm_ep(acc, *rest):
    *bias, res, gain = rest
    x = acc + res + (bias[0] if bias else 0.0)
    return x, _rms(x, gain)


RESIDUAL_NORM_OUTS = (("tile", F32), ("tile", BF))


def _mlp_up(tag, h, w_up_sm):
    (up,) = _mm(f"mlp{tag}_up", h, w_up_sm, nt=False, b_sm=True, tm=2048, tn=1024, rows=256,
                ep_fn=lambda acc: (acc,), outs=(("tile", BF),))
    return up


RMS_BWD_OUTS = (("tile", F32), ("tile", BF), ("colsum", F32), ("colsum", F32))


def _mlp_bwd(tag, dy, dy_bf, x, g, up, w_up_sm, w_down):
    (dup,) = _mm(f"mlp{tag}_dup", dy_bf, w_down, nt=True, tm=2048, tn=1024, rows=256, ep_in=((up, "tile"),),
                 ep_fn=lambda acc, u: (acc * (2.0 * jnp.maximum(u.astype(F32), 0.0)),), outs=(("tile", BF),))
    dx, dx_bf, dg, dx_sum = _mm(f"mlp{tag}_dx", dup, w_up_sm, nt=True, b_sm=True, tm=512, tn=1024, rows=256,
                                ep_in=((x, "tile"), (g, "row"), (dy, "tile")), ep_fn=_rms_bwd_ep, outs=RMS_BWD_OUTS)
    return dx, dx_bf, dg, dx_sum, dup


class _Reduction:
    def __init__(self, tag, grads, place):
        self.tag, self.place = tag, place
        self.sems, self.grads, self.lands, self.token = _reduce_start(f"reduce_start_{tag}", grads)

    def finish(self, after):
        grads, lands = _reduce_wait(f"reduce_wait_{self.tag}", self.grads, self.lands, self.sems, after)
        return [_sum_devices(f"reduce_sum_{self.tag}{i}", g, l, self.place) for i, (g, l) in enumerate(zip(grads, lands))]


def kernel(x, conv_norm_g, conv_w_in, conv_b_in, conv_dw, conv_dw_b, conv_ln_g, conv_ln_b, conv_w_out, conv_b_out, attn_norm_g, w_qkv, b_qkv, q_norm_g, k_norm_g, sinks, w_o, b_o, rel_bias, mlp_norm_g, w_up, w_down, loss_target, m_conv_norm_g, m_conv_w_in, m_conv_b_in, m_conv_dw, m_conv_dw_b, m_conv_ln_g, m_conv_ln_b, m_conv_w_out, m_conv_b_out, m_attn_norm_g, m_w_qkv, m_b_qkv, m_q_norm_g, m_k_norm_g, m_sinks, m_w_o, m_b_o, m_rel_bias, m_mlp_norm_g, m_w_up, m_w_down, v_conv_norm_g, v_conv_w_in, v_conv_b_in, v_conv_dw, v_conv_dw_b, v_conv_ln_g, v_conv_ln_b, v_conv_w_out, v_conv_b_out, v_attn_norm_g, v_w_qkv, v_b_qkv, v_q_norm_g, v_k_norm_g, v_sinks, v_w_o, v_b_o, v_rel_bias, v_mlp_norm_g, v_w_up, v_w_down):
    Dm = D_MODEL
    x2d = x[0]
    tgt = loss_target[0]
    T = x2d.shape[0]
    shard = 2 * lax.axis_index("x") + lax.axis_index("y")

    me = 2 * shard + lax.axis_index("c")

    def own_slot(block, slots, index):
        return lax.dynamic_update_slice(lax.empty((slots,) + block.shape, block.dtype), block[None],
                                        (index,) + (0,) * block.ndim)

    (conv_in_sems,), (stack_in,), first_token = _gather_start(
        "gather_start_conv_in", [own_slot(conv_w_in[0].astype(BF), N_SHARD, shard)], ((0,),), OTHER_CHIPS, after=())
    sharded_small = [conv_dw[0], attn_norm_g, b_qkv, b_o]
    (small_sems,), (small_land,), small_token = _gather_start(
        "small_weights_start", [own_slot(_pack(sharded_small), 8, me)], ((0,),), ALL_OTHERS, after=(first_token,))

    big = [conv_w_out[0], jnp.swapaxes(w_qkv, 1, 2)[0], w_o[0], w_up[0], w_up[1], w_down[0], w_down[1]]
    stacks = [own_slot(w.astype(BF), N_SHARD, shard) for w in big]
    groups = ((0,), (3, 5), (1, 2), (4, 6))
    gather_sems, stacks, gather_token = _gather_start("gather_start", stacks, groups, OTHER_CHIPS, after=(small_token,))

    def gathered_group(g, name, after):
        return _gather_wait(name, [stacks[t] for t in groups[g]], gather_sems[g], OTHER_CHIPS, after)

    bucket = _bucket_table()
    bias = _bias_table(rel_bias.T, bucket)

    h0 = _rms_fwd("conv_norm", x2d, conv_norm_g, deps=(gather_token,))
    (w_in_sm,) = _gather_wait("gather_wait_conv_in", [stack_in], conv_in_sems, OTHER_CHIPS, (h0, bias))
    (u,) = _mm("conv_in", h0, w_in_sm, nt=False, b_sm=True, tm=2048, tn=512, rows=256, ep_in=((conv_b_in, "row"),),
               ep_fn=lambda acc, b: (acc + b,), outs=(("tile", BF),))
    (gathered,) = _gather_wait("small_weights_wait", [small_land], small_sems, ALL_OTHERS, (u,))
    chips = [_unpack(gathered[2 * s], [a.shape for a in sharded_small]) for s in range(N_SHARD)]
    dw_f, attn_norm_f, b_qkv_f, b_o_f = (jnp.concatenate([chips[s][t] for s in range(N_SHARD)], axis=-1)
                                         for t in range(len(sharded_small)))
    dw_pad = jnp.pad(dw_f, ((0, HALO - CONV_W), (0, 0)))
    cv, s_act = _conv_fwd(u, dw_pad, conv_dw_b, conv_ln_g, conv_ln_b)
    (g_out,) = gathered_group(0, "gather_wait_conv_out", (s_act,))
    w_out_f = g_out.reshape(Dm, Dm)
    x1, h1 = _mm("conv_out", s_act, w_out_f, nt=False, tm=1024, tn=1024, rows=256,
                 ep_in=((conv_b_out, "row"), (x2d, "tile"), (mlp_norm_g[0:1], "row")), ep_fn=_residual_norm_ep,
                 outs=RESIDUAL_NORM_OUTS)

    g_up0, g_down0 = gathered_group(1, "gather_wait_mlp0", (x1,))
    w_up_sm = [g_up0, None]
    w_down_f = [g_down0.reshape(D_FF, Dm), None]
    up0 = _mlp_up(0, h1, w_up_sm[0])
    x2, h2 = _mm("mlp0_down", up0, w_down_f[0], nt=False, tm=512, tn=1024, rows=256, a_fn=_relu2,
                 ep_in=((x1, "tile"), (attn_norm_f, "row")), ep_fn=_residual_norm_ep, outs=RESIDUAL_NORM_OUTS)

    g_qkv, g_o = gathered_group(2, "gather_wait_attn", (x2,))
    w_qkv_t = g_qkv.reshape(QKV_DIM, Dm)
    w_o_f = g_o.reshape(ATTN_DIM, Dm)
    qg_t = jnp.tile(q_norm_g, (1, N_HEADS))
    kg_t = jnp.tile(k_norm_g, (1, N_KV))

    def qkv_ep(acc, b, qg, kg, ones):
        proj = acc + b
        q, k, v = proj[:, :ATTN_DIM], proj[:, ATTN_DIM:ATTN_DIM + KV_DIM], proj[:, ATTN_DIM + KV_DIM:]
        return proj, _qk_normed(q, qg, ones, 1.0 / math.sqrt(HEAD_DIM)), _qk_normed(k, kg, ones, 1.0), v

    qkv, qn, kn, vv = _mm(
        "attn_qkv", h2, w_qkv_t, nt=True, tm=1024, tn=QKV_DIM, rows=256, ep_fn=qkv_ep,
        ep_in=((b_qkv_f, "row"), (qg_t, "whole"), (kg_t, "whole"), (_head_ones(), "whole")),
        outs=(("tile", F32), ("tile", BF, ATTN_DIM), ("tile", BF, KV_DIM), ("tile", BF, KV_DIM)))
    sinks1 = sinks[0]
    att = _attn_fwd(qn, kn, vv, bias, sinks1)
    x3, h3 = _mm("attn_out", att, w_o_f, nt=False, tm=1024, tn=1024, rows=256,
                 ep_in=((b_o_f, "row"), (x2, "tile"), (mlp_norm_g[1:2], "row")), ep_fn=_residual_norm_ep,
                 outs=RESIDUAL_NORM_OUTS)

    g_up1, g_down1 = gathered_group(3, "gather_wait_mlp1", (x3,))
    w_up_sm[1] = g_up1
    w_down_f[1] = g_down1.reshape(D_FF, Dm)
    up1 = _mlp_up(1, h3, w_up_sm[1])

    def loss_ep(acc, r, t):
        diff = acc + r - t
        dy = diff * (1.0 / Dm)
        return dy, dy, jnp.sum(diff * diff, axis=0, keepdims=True)

    dy, dy_bf, sq = _mm("mlp1_down_loss", up1, w_down_f[1], nt=False, tm=512, tn=1024, rows=256, a_fn=_relu2,
                        ep_in=((x3, "tile"), (tgt, "tile")), ep_fn=loss_ep,
                        outs=(("tile", F32), ("tile", BF), ("colsum", F32)))

    place = jnp.stack([shard, lax.axis_index("c")]).astype(jnp.int32)
    dx3, dx3_bf, dg_mlp1, db_o, dup1 = _mlp_bwd(1, dy, dy_bf, x3, mlp_norm_g[1:2], up1, w_up_sm[1], w_down_f[1])
    dw_down1 = _mm_tn("mlp1_dw_down", up1, dy_bf, tm=1024, tn=1024, tk=2048, a_fn=_relu2)
    dw_up1 = _mm_tn("mlp1_dw_up", h3, dup1, tm=1024, tn=1024, tk=2048, out_sm=N_SHARD)
    red_mlp1 = _Reduction("mlp1", [dw_up1, dw_down1.reshape(N_SHARD, D_FF // N_SHARD, Dm)], place)

    ident = lambda acc: (acc,)
    (datt,) = _mm("attn_dout", dx3_bf, w_o_f, nt=True, tm=1024, tn=1024, rows=256, ep_fn=ident, outs=(("tile", BF),),
                  deps=(red_mlp1.token,))
    dw_o = _mm_tn("attn_dw_o", att, dx3_bf, tm=1024, tn=1024, tk=2048)
    dqn, dkn, dvv, dbias, dsinks = _attn_bwd(qn, kn, vv, bias, sinks1, datt)
    drel = _bias_grad(dbias, bucket)
    dqkv, db_qkv, dqg_t, dkg_t = _qk_norm_bwd(qkv, dqn, dkn, dvv, qg_t, kg_t)
    dw_qkv_t = _mm_tn("attn_dw_qkv", dqkv, h2, tm=QKV_DIM, tn=1024, tk=2048)
    red_attn = _Reduction("attn", [dw_qkv_t.reshape(N_SHARD, QKV_DIM // N_SHARD, Dm),
                                   dw_o.reshape(N_SHARD, ATTN_DIM // N_SHARD, Dm)], place)
    dx2, dx2_bf, dg_attn, _ = _mm("attn_dx", dqkv, w_qkv_t, nt=False, tm=1024, tn=1024, rows=256,
                                  ep_in=((x2, "tile"), (attn_norm_f, "row"), (dx3, "tile")), ep_fn=_rms_bwd_ep,
                                  outs=RMS_BWD_OUTS, deps=(red_attn.token,))

    dx1, dx1_bf, dg_mlp0, db_out, dup0 = _mlp_bwd(0, dx2, dx2_bf, x1, mlp_norm_g[0:1], up0, w_up_sm[0], w_down_f[0])
    dw_down0 = _mm_tn("mlp0_dw_down", up0, dx2_bf, tm=1024, tn=1024, tk=2048, a_fn=_relu2)
    dw_up0 = _mm_tn("mlp0_dw_up", h1, dup0, tm=1024, tn=1024, tk=2048, out_sm=N_SHARD)
    dw_out = _mm_tn("conv_dw_out", s_act, dx1_bf, tm=1024, tn=1024, tk=2048)
    red_mlp0 = _Reduction("mlp0", [dw_up0, dw_down0.reshape(N_SHARD, D_FF // N_SHARD, Dm),
                                   dw_out.reshape(N_SHARD, Dm // N_SHARD, Dm)], place)
    (r_qkv, r_o) = red_attn.finish((dx1,))
    (r_up1, r_down1) = red_mlp1.finish((dx1,))

    dcv, dln_g, dln_b, ddw_b = _mm("conv_ds", dx1_bf, w_out_f, nt=True, tm=1024, tn=1024, rows=256,
                                   ep_in=((cv, "tile"), (conv_ln_g, "row"), (conv_ln_b, "row")),
                                   ep_fn=_ln_silu_bwd_ep,
                                   outs=(("tile", F32), ("colsum", F32), ("colsum", F32), ("colsum", F32)),
                                   deps=(red_mlp0.token,))
    du, db_in, ddw8 = _conv_bwd(u, dcv, dw_pad)
    (r_up0, r_down0, r_out) = red_mlp0.finish((du,))
    early = [r_out, r_qkv, r_o, r_up0, r_up1, r_down0, r_down1]
    join_sems, early, early_lands, join_token = _join_start("join_start", early)
    dw_in = _mm_tn("conv_dw_in", h0, du, tm=1024, tn=512, tk=4096, out_sm=N_SHARD)
    red_conv = _Reduction("conv", [dw_in], place)
    def first_layer_ep(*args):
        tot, _, dg, _ = _rms_bwd_ep(*args)
        return tot, dg

    gx, dg_conv = _mm("conv_dx", du, w_in_sm, nt=True, b_sm=True, tm=1024, tn=1024, rows=256,
                      ep_in=((x2d, "tile"), (conv_norm_g, "row"), (dx1, "tile")), ep_fn=first_layer_ep,
                      outs=(("tile", F32), ("colsum", F32)), deps=(red_conv.token, join_token))
    (r_in,) = red_conv.finish((gx,))

    dqg = dqg_t.reshape(N_HEADS, HEAD_DIM).sum(axis=0, keepdims=True)
    dkg = dkg_t.reshape(N_KV, HEAD_DIM).sum(axis=0, keepdims=True)
    small_full = [dg_conv, db_in, ddw8.sum(axis=1)[:CONV_W], ddw_b, dln_g, dln_b, db_out, dg_attn, db_qkv, dqg, dkg,
                  dsinks[None, :], db_o, drel.reshape(1, REL_BUCKETS * N_HEADS),
                  jnp.pad(dg_mlp0, ((0, 1), (0, 0))) + jnp.pad(dg_mlp1, ((1, 0), (0, 0))), sq]
    (sg_sems,), (sg_land,), sg_token = _gather_start(
        "small_grads_start", [own_slot(_pack(small_full), 8, me)], ((0,),), ALL_OTHERS, after=())

    early, early_sibling = _join_wait("join_wait", early, early_lands, join_sems, (gx, sg_token))
    r_out, r_qkv, r_o, r_up0, r_up1, r_down0, r_down1 = zip(early, early_sibling)
    r_in = (r_in,) + tuple(_join_halves("join_halves", [r_in], deps=(sg_token,)))

    big_out = {}
    qkv_t = [jnp.swapaxes(a, 1, 2) for a in (w_qkv, m_w_qkv, v_w_qkv)]
    for nm, w, m, v, gs in (("conv_w_in", conv_w_in, m_conv_w_in, v_conv_w_in, (r_in,)),
                            ("conv_w_out", conv_w_out, m_conv_w_out, v_conv_w_out, (r_out,)),
                            ("w_qkv", *qkv_t, (r_qkv,)),
                            ("w_o", w_o, m_w_o, v_w_o, (r_o,)),
                            ("w_up", w_up, m_w_up, v_w_up, (r_up0, r_up1)),
                            ("w_down", w_down, m_w_down, v_w_down, (r_down0, r_down1))):
        big_out[nm] = _adamw(f"adamw_{nm}", w, m, v, gs)

    (sg_land,) = _gather_wait("small_grads_wait", [sg_land], sg_sems, ALL_OTHERS,
                              [big_out[nm][0] for nm in big_out])
    big_out["w_qkv"] = tuple(jnp.swapaxes(a, 1, 2) for a in big_out["w_qkv"])
    small_sum = _sum8("small_grads_sum", sg_land)
    (r_norm, r_b_in, r_dw, r_dw_b, r_ln_g, r_ln_b, r_b_out, r_attn_norm, r_b_qkv, r_qg, r_kg, r_sinks, r_b_o, r_rel,
     r_mlp_norm, r_sq) = _unpack(small_sum, [a.shape for a in small_full])
    loss = 0.5 * jnp.sum(r_sq) * (1.0 / Dm)

    def cols(a, width):
        return lax.dynamic_slice_in_dim(a, shard * width, width, axis=a.ndim - 1)

    small_names = ["conv_norm_g", "conv_b_in", "conv_dw", "conv_dw_b", "conv_ln_g", "conv_ln_b", "conv_b_out",
                   "attn_norm_g", "b_qkv", "q_norm_g", "k_norm_g", "sinks", "b_o", "rel_bias", "mlp_norm_g"]
    small_g = [r_norm, r_b_in, cols(r_dw, Dm // N_SHARD)[None], r_dw_b, r_ln_g, r_ln_b, r_b_out,
               cols(r_attn_norm, Dm // N_SHARD), cols(r_b_qkv, QKV_DIM // N_SHARD), r_qg, r_kg, r_sinks,
               cols(r_b_o, Dm // N_SHARD), r_rel.reshape(N_HEADS, REL_BUCKETS), r_mlp_norm]
    small_w = [conv_norm_g, conv_b_in, conv_dw, conv_dw_b, conv_ln_g, conv_ln_b, conv_b_out, attn_norm_g, b_qkv,
               q_norm_g, k_norm_g, sinks, b_o, rel_bias.T, mlp_norm_g]
    small_m = [m_conv_norm_g, m_conv_b_in, m_conv_dw, m_conv_dw_b, m_conv_ln_g, m_conv_ln_b, m_conv_b_out,
               m_attn_norm_g, m_b_qkv, m_q_norm_g, m_k_norm_g, m_sinks, m_b_o, m_rel_bias.T, m_mlp_norm_g]
    small_v = [v_conv_norm_g, v_conv_b_in, v_conv_dw, v_conv_dw_b, v_conv_ln_g, v_conv_ln_b, v_conv_b_out,
               v_attn_norm_g, v_b_qkv, v_q_norm_g, v_k_norm_g, v_sinks, v_b_o, v_rel_bias.T, v_mlp_norm_g]
    flat2 = lambda a: a.reshape(-1, a.shape[-1])
    small_g = [flat2(g) for g in small_g]
    d_s, m_s, v_s = _adamw_small([flat2(w) for w in small_w], small_g, [flat2(m) for m in small_m],
                                 [flat2(v) for v in small_v])
    small_out = {}
    for nm, w, g, d, m2, v2 in zip(small_names, small_w, small_g, d_s, m_s, v_s):
        small_out[nm] = tuple(a.reshape(w.shape) for a in (g, d, m2, v2))
    small_out["rel_bias"] = tuple(a.T for a in small_out["rel_bias"])

    order = ["conv_norm_g", "conv_w_in", "conv_b_in", "conv_dw", "conv_dw_b", "conv_ln_g", "conv_ln_b", "conv_w_out",
             "conv_b_out", "attn_norm_g", "w_qkv", "b_qkv", "q_norm_g", "k_norm_g", "sinks", "w_o", "b_o", "rel_bias",
             "mlp_norm_g", "w_up", "w_down"]
    res = {**small_out, **big_out}
    outs = [loss, gx[None]]
    for part in range(4):
        outs += [res[nm][part] for nm in order]
    return tuple(outs)
```

```python
import math

import numpy as np
import jax
import jax.numpy as jnp
from jax import lax
from jax.experimental import pallas as pl
from jax.experimental.pallas import tpu as pltpu

F32 = jnp.float32
BF = jnp.bfloat16
MESH = pl.DeviceIdType.MESH

D_MODEL = 1024
D_FF = 4096
N_HEADS = 16
N_KV = 2
GROUP = N_HEADS // N_KV
HEAD_DIM = 64
ATTN_DIM = N_HEADS * HEAD_DIM
KV_DIM = N_KV * HEAD_DIM
QKV_DIM = ATTN_DIM + 2 * KV_DIM
BLOCK = 128
CONV_W = 31
HALO = 32
REL_BUCKETS = 32
REL_MAX_DIST = 128
NORM_EPS = 1e-6
NEG_INF = -1e30
N_SHARD = 4
LANES = 1024

ADAM_LR = 0.001
ADAM_B1 = 0.9
ADAM_B2 = 0.999
ADAM_EPS = 1e-08
ADAM_WD = 0.01
ADAM_STEP = 10

VMEM_LIMIT = 56 * 1024 * 1024


def _params(n_axes):
    return pltpu.CompilerParams(dimension_semantics=("arbitrary",) * n_axes, vmem_limit_bytes=VMEM_LIMIT)


def _dot(a, b, ca, cb):
    return lax.dot_general(a, b, (((ca,), (cb,)), ((), ())), preferred_element_type=F32)


def _mm(name, a, b, *, nt, tm, tn, ep_fn, outs, a_fn=None, b_sm=False, ep_in=(), deps=(), rows=None):
    M, K = a.shape
    rows = tm if rows is None else rows
    if b_sm:
        S, ks = b.shape[0], b.shape[2]
        N, per = (b.shape[1], None) if nt else (S * b.shape[2], b.shape[2] // tn)
        assert (S * ks == K) if nt else (b.shape[1] == K)
    else:
        N = b.shape[0] if nt else b.shape[1]
        assert (b.shape[1] if nt else b.shape[0]) == K
    assert M % tm == 0 and N % tn == 0 and tm % rows == 0
    ne, no, nd = len(ep_in), len(outs), len(deps)

    def body(a_ref, b_ref, *rest):
        ep_refs, out_refs = rest[:ne], rest[ne + nd:ne + nd + no]
        i = pl.program_id(1)
        sums = [None] * no
        for r in range(tm // rows):
            rs = pl.ds(r * rows, rows)

            def lhs(cols):
                av = a_ref[rs, cols]
                return (av if a_fn is None else a_fn(av)).astype(BF)

            if b_sm and nt:
                acc = None
                for s in range(S):
                    part = _dot(lhs(pl.ds(s * ks, ks)), b_ref[s].astype(BF), 1, 1)
                    acc = part if acc is None else acc + part
            else:
                acc = _dot(lhs(slice(None)), b_ref[...].astype(BF), 1, 1 if nt else 0)
            ep_vals = [ref[rs, :] if kind == "tile" else ref[...] for ref, (_, kind) in zip(ep_refs, ep_in)]
            vals = ep_fn(acc, *ep_vals)
            for o, ((kind, dt, *_), ref, val) in enumerate(zip(outs, out_refs, vals)):
                if kind == "tile":
                    ref[rs, :] = val.astype(dt)
                else:
                    sums[o] = val if sums[o] is None else sums[o] + val
        for (kind, *_), ref, val in zip(outs, out_refs, sums):
            if kind == "colsum":
                @pl.when(i == 0)
                def _():
                    ref[...] = val

                @pl.when(i > 0)
                def _():
                    ref[...] += val

    if b_sm and nt:
        b_spec = pl.BlockSpec((S, tn, ks), lambda j, i: (0, j, 0))
    elif b_sm:
        b_spec = pl.BlockSpec((None, K, tn), lambda j, i: (j // per, 0, j % per))
    elif nt:
        b_spec = pl.BlockSpec((tn, K), lambda j, i: (j, 0))
    else:
        b_spec = pl.BlockSpec((K, tn), lambda j, i: (0, j))
    in_specs = [pl.BlockSpec((tm, K), lambda j, i: (i, 0)), b_spec]
    for arr, kind in ep_in:
        if kind == "tile":
            assert arr.shape == (M, N)
            in_specs.append(pl.BlockSpec((tm, tn), lambda j, i: (i, j)))
        elif kind == "whole":
            in_specs.append(pl.BlockSpec(arr.shape, lambda j, i, rank=arr.ndim: (0,) * rank))
        else:
            assert arr.shape == (1, N)
            in_specs.append(pl.BlockSpec((1, tn), lambda j, i: (0, j)))
    in_specs += [pl.BlockSpec(memory_space=pl.ANY)] * nd
    out_shape, out_specs = [], []
    for kind, dt, *width in outs:
        if kind == "tile" and width:
            assert tn == N
            out_shape.append(jax.ShapeDtypeStruct((M, width[0]), dt))
            out_specs.append(pl.BlockSpec((tm, width[0]), lambda j, i: (i, 0)))
        elif kind == "tile":
            out_shape.append(jax.ShapeDtypeStruct((M, N), dt))
            out_specs.append(pl.BlockSpec((tm, tn), lambda j, i: (i, j)))
        else:
            out_shape.append(jax.ShapeDtypeStruct((1, N), F32))
            out_specs.append(pl.BlockSpec((1, tn), lambda j, i: (0, j)))
    return pl.pallas_call(
        body, name=name, grid=(N // tn, M // tm), in_specs=in_specs, out_specs=out_specs, out_shape=out_shape,
        compiler_params=_params(2),
    )(a, b, *[arr for arr, _ in ep_in], *deps)


def _mm_tn(name, a, b, *, tm, tn, tk, a_fn=None, out_sm=None):
    T, Ka = a.shape
    N = b.shape[1]
    assert b.shape[0] == T and T % tk == 0 and Ka % tm == 0 and N % tn == 0
    nk = T // tk

    def body(a_ref, b_ref, o_ref, acc_ref):
        k = pl.program_id(2)

        @pl.when(k == 0)
        def _():
            acc_ref[...] = jnp.zeros_like(acc_ref)

        av = a_ref[...]
        if a_fn is not None:
            av = a_fn(av)
        acc_ref[...] += _dot(av.astype(BF), b_ref[...].astype(BF), 0, 0)

        @pl.when(k == nk - 1)
        def _():
            o_ref[...] = acc_ref[...].astype(BF)

    if out_sm is None:
        out_shape = jax.ShapeDtypeStruct((Ka, N), BF)
        out_spec = pl.BlockSpec((tm, tn), lambda i, j, k: (i, j))
    else:
        per = (N // out_sm) // tn
        assert per * tn * out_sm == N
        out_shape = jax.ShapeDtypeStruct((out_sm, Ka, N // out_sm), BF)
        out_spec = pl.BlockSpec((None, tm, tn), lambda i, j, k: (j // per, i, j % per))
    return pl.pallas_call(
        body, name=name, grid=(Ka // tm, N // tn, nk),
        in_specs=[pl.BlockSpec((tk, tm), lambda i, j, k: (k, i)), pl.BlockSpec((tk, tn), lambda i, j, k: (k, j))],
        out_specs=out_spec, out_shape=out_shape, scratch_shapes=[pltpu.VMEM((tm, tn), F32)],
        compiler_params=_params(3),
    )(a, b)


def _relu2(v):
    r = jnp.maximum(v.astype(F32), 0.0)
    return r * r


def _rms_bwd_ep(dh, x, g, dres):
    rstd = lax.rsqrt(jnp.mean(x * x, axis=-1, keepdims=True) + NORM_EPS)
    xh = x * rstd
    dxh = dh * g
    dx = rstd * (dxh - xh * jnp.mean(dxh * xh, axis=-1, keepdims=True))
    tot = dres + dx
    return tot, tot, jnp.sum(dh * xh, axis=0, keepdims=True), jnp.sum(tot, axis=0, keepdims=True)


def _rms_fwd(name, x, g, tm=512, deps=()):
    T, Dm = x.shape

    def body(x_ref, g_ref, *rest):
        o_ref = rest[-1]
        xv = x_ref[...]
        rstd = lax.rsqrt(jnp.mean(xv * xv, axis=-1, keepdims=True) + NORM_EPS)
        o_ref[...] = (xv * rstd * g_ref[...]).astype(BF)

    return pl.pallas_call(
        body, name=name, grid=(T // tm,),
        in_specs=[pl.BlockSpec((tm, Dm), lambda i: (i, 0)), pl.BlockSpec((1, Dm), lambda i: (0, 0))]
        + [pl.BlockSpec(memory_space=pl.ANY)] * len(deps),
        out_specs=pl.BlockSpec((tm, Dm), lambda i: (i, 0)), out_shape=jax.ShapeDtypeStruct((T, Dm), BF),
        compiler_params=_params(1),
    )(x, g, *deps)


HEAD_GROUP = 256


def _head_sum(v, ones):
    n = v.shape[1]
    w = min(n, HEAD_GROUP)
    blk = ones[:w, :w]
    parts = [_dot(v[:, c:c + w].astype(BF), blk, 1, 0) for c in range(0, n, w)]
    return parts[0] if len(parts) == 1 else jnp.concatenate(parts, axis=1)


def _head_ones():
    idx = np.arange(HEAD_GROUP) // HEAD_DIM
    return jnp.asarray((idx[:, None] == idx[None, :]).astype(np.float32), dtype=BF)


def _qk_normed(x, g, ones, scale):
    r = lax.rsqrt(_head_sum(x * x, ones) * (1.0 / HEAD_DIM) + NORM_EPS)
    return x * r * g * scale


def _qk_norm_bwd(qkv, dqn, dkn, dv, qg_t, kg_t, tm=256):
    T = qkv.shape[0]

    def body(x_ref, dq_ref, dk_ref, dv_ref, qg_ref, kg_ref, ones_ref, o_ref, db_ref, dqg_ref, dkg_ref):
        i = pl.program_id(0)
        ones = ones_ref[...]

        def one(x, dy, g):
            r = lax.rsqrt(_head_sum(x * x, ones) * (1.0 / HEAD_DIM) + NORM_EPS)
            xh = x * r
            dxh = dy * g
            dx = r * (dxh - xh * (_head_sum(dxh * xh, ones) * (1.0 / HEAD_DIM)))
            return dx, jnp.sum(dy * xh, axis=0, keepdims=True)

        dq, dqg = one(x_ref[:, pl.ds(0, ATTN_DIM)], dq_ref[...], qg_ref[...])
        dk, dkg = one(x_ref[:, pl.ds(ATTN_DIM, KV_DIM)], dk_ref[...], kg_ref[...])
        dvv = dv_ref[...]
        o_ref[:, pl.ds(0, ATTN_DIM)] = dq.astype(BF)
        o_ref[:, pl.ds(ATTN_DIM, KV_DIM)] = dk.astype(BF)
        o_ref[:, pl.ds(ATTN_DIM + KV_DIM, KV_DIM)] = dvv.astype(BF)
        sq, sk, sv = (jnp.sum(t, axis=0, keepdims=True) for t in (dq, dk, dvv))

        @pl.when(i == 0)
        def _():
            db_ref[:, pl.ds(0, ATTN_DIM)] = sq
            db_ref[:, pl.ds(ATTN_DIM, KV_DIM)] = sk
            db_ref[:, pl.ds(ATTN_DIM + KV_DIM, KV_DIM)] = sv
            dqg_ref[...] = dqg
            dkg_ref[...] = dkg

        @pl.when(i > 0)
        def _():
            db_ref[:, pl.ds(0, ATTN_DIM)] += sq
            db_ref[:, pl.ds(ATTN_DIM, KV_DIM)] += sk
            db_ref[:, pl.ds(ATTN_DIM + KV_DIM, KV_DIM)] += sv
            dqg_ref[...] += dqg
            dkg_ref[...] += dkg

    full = lambda shape: pl.BlockSpec(shape, lambda i: (0, 0))
    row = lambda n: pl.BlockSpec((tm, n), lambda i: (i, 0))
    return pl.pallas_call(
        body, name="qk_norm_bwd", grid=(T // tm,),
        in_specs=[row(QKV_DIM), row(ATTN_DIM), row(KV_DIM), row(KV_DIM), full((1, ATTN_DIM)), full((1, KV_DIM)),
                  full((HEAD_GROUP, HEAD_GROUP))],
        out_specs=[row(QKV_DIM), full((1, QKV_DIM)), full((1, ATTN_DIM)), full((1, KV_DIM))],
        out_shape=[jax.ShapeDtypeStruct((T, QKV_DIM), BF), jax.ShapeDtypeStruct((1, QKV_DIM), F32),
                   jax.ShapeDtypeStruct((1, ATTN_DIM), F32), jax.ShapeDtypeStruct((1, KV_DIM), F32)],
        compiler_params=_params(1),
    )(qkv, dqn, dkn, dv, qg_t, kg_t, _head_ones())


ROWS = 128
COLS = 128


SUBLANES = 8
FIRST_TAP = HALO - (CONV_W - 1)


def _glu(a, g):
    return a.astype(F32) * jax.nn.sigmoid(g.astype(F32))


def _shifted(xe, s):
    return xe if s == 0 else pltpu.roll(xe, ROWS + HALO - s, axis=0)


def _conv_fwd(u, dw_pad, dw_b, ln_g, ln_b, tm=512):
    T = u.shape[0]
    Dm = D_MODEL
    hpt = tm // HALO

    def body(ac_ref, gc_ref, ap_ref, gp_ref, w_ref, wb_ref, lg_ref, lb_ref, cv_ref, s_ref, ext):
        i = pl.program_id(0)
        ext[pl.ds(0, HALO), :] = jnp.where(i > 0, _glu(ap_ref[...], gp_ref[...]), 0.0)
        ext[pl.ds(HALO, tm), :] = _glu(ac_ref[...], gc_ref[...])

        def rows(r, carry):
            r0 = pl.multiple_of(r * ROWS, ROWS)
            for c in range(Dm // COLS):
                cs = pl.ds(c * COLS, COLS)
                xe = ext[pl.ds(r0, ROWS + HALO), cs]
                acc = jnp.zeros((ROWS, COLS), F32)
                for s in range(SUBLANES):
                    xs = _shifted(xe, s)
                    for j in range(CONV_W):
                        off = FIRST_TAP + j
                        if off % SUBLANES == s:
                            acc = acc + xs[off - s:off - s + ROWS, :] * w_ref[pl.ds(j, 1), cs]
                cv_ref[pl.ds(r0, ROWS), cs] = acc + wb_ref[:, cs]
            return carry

        lax.fori_loop(0, tm // ROWS, rows, 0)
        cv = cv_ref[...]
        xc = cv - jnp.mean(cv, axis=-1, keepdims=True)
        y = xc * lax.rsqrt(jnp.mean(xc * xc, axis=-1, keepdims=True) + NORM_EPS) * lg_ref[...] + lb_ref[...]
        s_ref[...] = (y * jax.nn.sigmoid(y)).astype(BF)

    full = lambda shape: pl.BlockSpec(shape, lambda i: (0, 0))
    return pl.pallas_call(
        body, name="conv_fwd", grid=(T // tm,),
        in_specs=[pl.BlockSpec((tm, Dm), lambda i: (i, 0)), pl.BlockSpec((tm, Dm), lambda i: (i, 1)),
                  pl.BlockSpec((HALO, Dm), lambda i: (jnp.maximum(i * hpt - 1, 0), 0)),
                  pl.BlockSpec((HALO, Dm), lambda i: (jnp.maximum(i * hpt - 1, 0), 1)),
                  full((HALO, Dm)), full((1, Dm)), full((1, Dm)), full((1, Dm))],
        out_specs=[pl.BlockSpec((tm, Dm), lambda i: (i, 0)), pl.BlockSpec((tm, Dm), lambda i: (i, 0))],
        out_shape=[jax.ShapeDtypeStruct((T, Dm), F32), jax.ShapeDtypeStruct((T, Dm), BF)],
        scratch_shapes=[pltpu.VMEM((tm + HALO, Dm), F32)],
        compiler_params=_params(1),
    )(u, u, u, u, dw_pad, dw_b, ln_g, ln_b)


def _ln_silu_bwd_ep(ds, cv, lg, lb):
    xc = cv - jnp.mean(cv, axis=-1, keepdims=True)
    rstd = lax.rsqrt(jnp.mean(xc * xc, axis=-1, keepdims=True) + NORM_EPS)
    xh = xc * rstd
    y = xh * lg + lb
    sg = jax.nn.sigmoid(y)
    dy = ds * (sg * (1.0 + y * (1.0 - sg)))
    dxh = dy * lg
    dcv = rstd * (dxh - jnp.mean(dxh, axis=-1, keepdims=True) - xh * jnp.mean(dxh * xh, axis=-1, keepdims=True))
    return (dcv, jnp.sum(dy * xh, axis=0, keepdims=True), jnp.sum(dy, axis=0, keepdims=True),
            jnp.sum(dcv, axis=0, keepdims=True))


def _conv_bwd(u, dcv, dw_pad, tm=512):
    T = u.shape[0]
    Dm = D_MODEL
    hpt = tm // HALO
    last = T // HALO - 1
    nt = T // tm

    def body(ac_ref, gc_ref, ap_ref, gp_ref, dc_ref, dn_ref, w_ref, du_ref, db_ref, dw_ref, ext_g, ext_d):
        i = pl.program_id(0)
        ext_g[pl.ds(0, HALO), :] = jnp.where(i > 0, _glu(ap_ref[...], gp_ref[...]), 0.0)
        ext_g[pl.ds(HALO, tm), :] = _glu(ac_ref[...], gc_ref[...])
        ext_d[pl.ds(0, tm), :] = dc_ref[...]
        ext_d[pl.ds(tm, HALO), :] = jnp.where(i < nt - 1, dn_ref[...], 0.0)

        @pl.when(i == 0)
        def _():
            db_ref[...] = jnp.zeros_like(db_ref)
            dw_ref[...] = jnp.zeros_like(dw_ref)

        def rows(r, carry):
            r0 = pl.multiple_of(r * ROWS, ROWS)
            rs = pl.ds(r0, ROWS)
            for c in range(Dm // COLS):
                cs = pl.ds(c * COLS, COLS)
                cs2 = pl.ds(Dm + c * COLS, COLS)
                de = ext_d[pl.ds(r0, ROWS + HALO), cs]
                ge = ext_g[pl.ds(r0, ROWS + HALO), cs]
                dcur = de[0:ROWS, :]
                acc = jnp.zeros((ROWS, COLS), F32)
                for s in range(SUBLANES):
                    ds_, gs_ = _shifted(de, s), _shifted(ge, s)
                    for j in range(CONV_W):
                        off = CONV_W - 1 - j
                        if off % SUBLANES == s:
                            acc = acc + ds_[off - s:off - s + ROWS, :] * w_ref[pl.ds(j, 1), cs]
                        goff = FIRST_TAP + j
                        if goff % SUBLANES == s:
                            prod = dcur * gs_[goff - s:goff - s + ROWS, :]
                            dw_ref[j, :, cs] += jnp.sum(prod.reshape(ROWS // SUBLANES, SUBLANES, COLS), axis=0)
                a = ac_ref[rs, cs].astype(F32)
                sg = jax.nn.sigmoid(gc_ref[rs, cs].astype(F32))
                da = acc * sg
                dg = acc * a * sg * (1.0 - sg)
                du_ref[rs, cs] = da.astype(BF)
                du_ref[rs, cs2] = dg.astype(BF)
                db_ref[:, cs] += jnp.sum(da, axis=0, keepdims=True)
                db_ref[:, cs2] += jnp.sum(dg, axis=0, keepdims=True)
            return carry

        lax.fori_loop(0, tm // ROWS, rows, 0)

    return pl.pallas_call(
        body, name="conv_bwd", grid=(nt,),
        in_specs=[pl.BlockSpec((tm, Dm), lambda i: (i, 0)), pl.BlockSpec((tm, Dm), lambda i: (i, 1)),
                  pl.BlockSpec((HALO, Dm), lambda i: (jnp.maximum(i * hpt - 1, 0), 0)),
                  pl.BlockSpec((HALO, Dm), lambda i: (jnp.maximum(i * hpt - 1, 0), 1)),
                  pl.BlockSpec((tm, Dm), lambda i: (i, 0)),
                  pl.BlockSpec((HALO, Dm), lambda i: (jnp.minimum((i + 1) * hpt, last), 0)),
                  pl.BlockSpec((HALO, Dm), lambda i: (0, 0))],
        out_specs=[pl.BlockSpec((tm, 2 * Dm), lambda i: (i, 0)), pl.BlockSpec((1, 2 * Dm), lambda i: (0, 0)),
                   pl.BlockSpec((HALO, 8, Dm), lambda i: (0, 0, 0))],
        out_shape=[jax.ShapeDtypeStruct((T, 2 * Dm), BF), jax.ShapeDtypeStruct((1, 2 * Dm), F32),
                   jax.ShapeDtypeStruct((HALO, 8, Dm), F32)],
        scratch_shapes=[pltpu.VMEM((tm + HALO, Dm), F32), pltpu.VMEM((tm + HALO, Dm), F32)],
        compiler_params=_params(1),
    )(u, u, u, u, dcv, dcv, dw_pad)


def _bucket_table():
    q_loc = np.arange(BLOCK)[:, None]
    k_loc = np.arange(2 * BLOCK)[None, :]
    dist = q_loc + BLOCK - k_loc
    n = np.maximum(dist, 0)
    max_exact = REL_BUCKETS // 2
    large = max_exact + (np.log(np.maximum(n, 1).astype(np.float32) / max_exact)
                         / math.log(REL_MAX_DIST / max_exact) * (REL_BUCKETS - max_exact)).astype(np.int32)
    large = np.minimum(large, REL_BUCKETS - 1)
    bucket = np.where(n < max_exact, n, large).astype(np.int32)
    band = np.where((dist >= 0) & (dist < BLOCK), bucket, -1)
    folded = np.where(np.arange(BLOCK)[None, :] > q_loc, band[:, :BLOCK], band[:, BLOCK:])
    assert (folded >= 0).all() and ((band[:, :BLOCK] >= 0) != (band[:, BLOCK:] >= 0)).all()
    return jnp.asarray(folded.astype(np.int32))


def _prev_mask():
    row = lax.broadcasted_iota(jnp.int32, (BLOCK, BLOCK), 0)
    col = lax.broadcasted_iota(jnp.int32, (BLOCK, BLOCK), 1)
    return col > row


def _fold(band, prev_mask):
    return jnp.where(prev_mask, band[:, :BLOCK], band[:, BLOCK:])


def _unfold(ref, g, rows, folded, prev_mask):
    ref[g, rows, pl.ds(0, BLOCK)] = jnp.where(prev_mask, folded, 0.0).astype(ref.dtype)
    ref[g, rows, pl.ds(BLOCK, BLOCK)] = jnp.where(prev_mask, 0.0, folded).astype(ref.dtype)


def _bias_table(rel_bias_t, bucket):
    def body(rb_ref, bk_ref, o_ref):
        bk = bk_ref[...]
        prev_mask = _prev_mask()
        for h in range(N_HEADS):
            acc = jnp.zeros((BLOCK, BLOCK), F32)
            for b in range(REL_BUCKETS):
                acc = jnp.where(bk == b, rb_ref[h, b], acc)
            o_ref[0, h] = acc
            o_ref[1, h] = jnp.where(prev_mask, NEG_INF, acc)

    return pl.pallas_call(
        body, name="bias_table", out_shape=jax.ShapeDtypeStruct((2, N_HEADS, BLOCK, BLOCK), F32),
        in_specs=[pl.BlockSpec(memory_space=pltpu.SMEM), pl.BlockSpec(memory_space=pltpu.VMEM)],
        out_specs=pl.BlockSpec(memory_space=pltpu.VMEM),
    )(rel_bias_t, bucket)


def _bias_grad(dbias, bucket):
    def body(db_ref, bk_ref, o_ref):
        bk = bk_ref[...]
        for b in range(REL_BUCKETS):
            sel = bk == b
            for h in range(N_HEADS):
                o_ref[h, b] = jnp.sum(jnp.where(sel, db_ref[h], 0.0))

    return pl.pallas_call(
        body, name="bias_grad", out_shape=jax.ShapeDtypeStruct((N_HEADS, REL_BUCKETS), F32),
        in_specs=[pl.BlockSpec(memory_space=pltpu.VMEM), pl.BlockSpec(memory_space=pltpu.VMEM)],
        out_specs=pl.BlockSpec(memory_space=pltpu.SMEM),
    )(dbias, bucket)


GROUP_ROWS = GROUP * BLOCK
BIAS_SPEC = pl.BlockSpec((2, N_HEADS, BLOCK, BLOCK), lambda n: (0, 0, 0, 0))


def _head_probs(qk, bias_h, sink, prev_mask):
    s = _fold(qk, prev_mask) + bias_h
    m = jnp.maximum(jnp.max(s, axis=-1, keepdims=True), sink)
    p = jnp.exp(s - m)
    ps = jnp.exp(sink - m)
    inv = 1.0 / (jnp.sum(p, axis=-1, keepdims=True) + ps)
    return p * inv, ps * inv


def _band(prev_ref, cur_ref, g):
    hs = pl.ds(g * HEAD_DIM, HEAD_DIM)
    return jnp.concatenate([prev_ref[:, hs], cur_ref[:, hs]], axis=0)


def _stack_heads(ref, g):
    return jnp.concatenate([ref[:, pl.ds((g * GROUP + hh) * HEAD_DIM, HEAD_DIM)] for hh in range(GROUP)], axis=0)


def _unstack_heads(ref, g, stacked, dtype):
    for hh in range(GROUP):
        ref[:, pl.ds((g * GROUP + hh) * HEAD_DIM, HEAD_DIM)] = stacked[hh * BLOCK:(hh + 1) * BLOCK, :].astype(dtype)


def _head_rows(hh):
    return pl.ds(hh * BLOCK, BLOCK)


def _attn_fwd(qn, kn, vv, bias, sinks):
    T = qn.shape[0]
    nb = T // BLOCK

    def body(sk_ref, q_ref, kc_ref, kp_ref, vc_ref, vp_ref, b_ref, o_ref, qk_buf, p_buf):
        table = (pl.program_id(0) == 0).astype(jnp.int32)
        prev_mask = _prev_mask()
        for g in range(N_KV):
            qk_buf[g] = _dot(_stack_heads(q_ref, g), _band(kp_ref, kc_ref, g), 1, 1)
        for g in range(N_KV):
            for hh in range(GROUP):
                h = g * GROUP + hh
                pn, _ = _head_probs(qk_buf[g, _head_rows(hh), :], b_ref[table, h], sk_ref[h], prev_mask)
                _unfold(p_buf, g, _head_rows(hh), pn, prev_mask)
        for g in range(N_KV):
            _unstack_heads(o_ref, g, _dot(p_buf[g], _band(vp_ref, vc_ref, g), 1, 0), BF)

    cur = lambda n: (n, 0)
    prev = lambda n: (jnp.maximum(n - 1, 0), 0)
    return pl.pallas_call(
        body, name="attn_fwd", grid=(nb,),
        in_specs=[pl.BlockSpec(memory_space=pltpu.SMEM), pl.BlockSpec((BLOCK, ATTN_DIM), cur),
                  pl.BlockSpec((BLOCK, KV_DIM), cur), pl.BlockSpec((BLOCK, KV_DIM), prev),
                  pl.BlockSpec((BLOCK, KV_DIM), cur), pl.BlockSpec((BLOCK, KV_DIM), prev), BIAS_SPEC],
        out_specs=pl.BlockSpec((BLOCK, ATTN_DIM), cur), out_shape=jax.ShapeDtypeStruct((T, ATTN_DIM), BF),
        scratch_shapes=[pltpu.VMEM((N_KV, GROUP_ROWS, 2 * BLOCK), F32), pltpu.VMEM((N_KV, GROUP_ROWS, 2 * BLOCK), BF)],
        compiler_params=_params(1),
    )(sinks, qn, kn, kn, vv, vv, bias)


def _attn_bwd(qn, kn, vv, bias, sinks, do):
    T = qn.shape[0]
    nb = T // BLOCK
    scale = 1.0 / math.sqrt(HEAD_DIM)

    def body(sk_ref, q_ref, kc_ref, kp_ref, vc_ref, vp_ref, b_ref, do_ref,
             dq_ref, dk_ref, dv_ref, db_ref, dsk_ref, dk_full, dv_full, dk_carry, dv_carry, qk_buf, dp_buf, p_buf, ds_buf):
        n = pl.program_id(0)

        @pl.when(n == 0)
        def _():
            db_ref[...] = jnp.zeros_like(db_ref)
            dk_carry[...] = jnp.zeros_like(dk_carry)
            dv_carry[...] = jnp.zeros_like(dv_carry)
            for h in range(N_HEADS):
                dsk_ref[h] = 0.0

        @pl.when(n < nb)
        def _():
            table = (n == 0).astype(jnp.int32)
            prev_mask = _prev_mask()
            ks = [_band(kp_ref, kc_ref, g) for g in range(N_KV)]
            qs = [_stack_heads(q_ref, g) for g in range(N_KV)]
            douts = [_stack_heads(do_ref, g) for g in range(N_KV)]
            for g in range(N_KV):
                qk_buf[g] = _dot(qs[g], ks[g], 1, 1)
                dp_buf[g] = _dot(douts[g], _band(vp_ref, vc_ref, g), 1, 1)
            for g in range(N_KV):
                for hh in range(GROUP):
                    h = g * GROUP + hh
                    rows = _head_rows(hh)
                    pn, psink = _head_probs(qk_buf[g, rows, :], b_ref[table, h], sk_ref[h], prev_mask)
                    dp = _fold(dp_buf[g, rows, :], prev_mask)
                    delta = jnp.sum(pn * dp, axis=-1, keepdims=True)
                    ds = pn * (dp - delta)
                    dsk_ref[h] += -jnp.sum(psink * delta)
                    db_ref[h] += ds
                    _unfold(ds_buf, g, rows, ds, prev_mask)
                    _unfold(p_buf, g, rows, pn, prev_mask)
            for g in range(N_KV):
                dsb = ds_buf[g]
                _unstack_heads(dq_ref, g, _dot(dsb, ks[g], 1, 0) * scale, F32)
                gs = pl.ds(g * HEAD_DIM, HEAD_DIM)
                dk_full[:, gs] = _dot(dsb, qs[g], 0, 0)
                dv_full[:, gs] = _dot(p_buf[g], douts[g], 0, 0)

        @pl.when(n == nb)
        def _():
            dk_full[...] = jnp.zeros_like(dk_full)
            dv_full[...] = jnp.zeros_like(dv_full)

        dk_ref[...] = dk_carry[...] + dk_full[pl.ds(0, BLOCK), :]
        dv_ref[...] = dv_carry[...] + dv_full[pl.ds(0, BLOCK), :]
        dk_carry[...] = dk_full[pl.ds(BLOCK, BLOCK), :]
        dv_carry[...] = dv_full[pl.ds(BLOCK, BLOCK), :]

    cur = lambda n: (jnp.minimum(n, nb - 1), 0)
    prev = lambda n: (jnp.maximum(jnp.minimum(n, nb - 1) - 1, 0), 0)
    out_kv = lambda n: (jnp.maximum(n - 1, 0), 0)
    return pl.pallas_call(
        body, name="attn_bwd", grid=(nb + 1,),
        in_specs=[pl.BlockSpec(memory_space=pltpu.SMEM), pl.BlockSpec((BLOCK, ATTN_DIM), cur),
                  pl.BlockSpec((BLOCK, KV_DIM), cur), pl.BlockSpec((BLOCK, KV_DIM), prev),
                  pl.BlockSpec((BLOCK, KV_DIM), cur), pl.BlockSpec((BLOCK, KV_DIM), prev), BIAS_SPEC,
                  pl.BlockSpec((BLOCK, ATTN_DIM), cur)],
        out_specs=[pl.BlockSpec((BLOCK, ATTN_DIM), cur), pl.BlockSpec((BLOCK, KV_DIM), out_kv),
                   pl.BlockSpec((BLOCK, KV_DIM), out_kv),
                   pl.BlockSpec((N_HEADS, BLOCK, BLOCK), lambda n: (0, 0, 0)),
                   pl.BlockSpec(memory_space=pltpu.SMEM)],
        out_shape=[jax.ShapeDtypeStruct((T, ATTN_DIM), F32), jax.ShapeDtypeStruct((T, KV_DIM), F32),
                   jax.ShapeDtypeStruct((T, KV_DIM), F32),
                   jax.ShapeDtypeStruct((N_HEADS, BLOCK, BLOCK), F32), jax.ShapeDtypeStruct((N_HEADS,), F32)],
        scratch_shapes=[pltpu.VMEM((2 * BLOCK, KV_DIM), F32), pltpu.VMEM((2 * BLOCK, KV_DIM), F32),
                        pltpu.VMEM((BLOCK, KV_DIM), F32), pltpu.VMEM((BLOCK, KV_DIM), F32),
                        pltpu.VMEM((N_KV, GROUP_ROWS, 2 * BLOCK), F32), pltpu.VMEM((N_KV, GROUP_ROWS, 2 * BLOCK), F32),
                        pltpu.VMEM((N_KV, GROUP_ROWS, 2 * BLOCK), BF), pltpu.VMEM((N_KV, GROUP_ROWS, 2 * BLOCK), BF)],
        compiler_params=_params(1),
    )(sinks, qn, kn, kn, vv, vv, bias, do)


def _coords():
    return lax.axis_index("x"), lax.axis_index("y"), lax.axis_index("c")


def _sum8(name, blocks):
    def body(b_ref, o_ref):
        tot = b_ref[0]
        for d in range(1, 8):
            tot = tot + b_ref[d]
        o_ref[...] = tot

    return pl.pallas_call(body, name=name, out_shape=jax.ShapeDtypeStruct(blocks.shape[1:], F32))(blocks)


HBM_SPEC = pl.BlockSpec(memory_space=pltpu.HBM)
SEM_SPEC = pl.BlockSpec(memory_space=pltpu.SEMAPHORE)
ANY_SPEC = pl.BlockSpec(memory_space=pl.ANY)
DATAFLOW = pltpu.SideEffectType.DATAFLOW_SIDE_EFFECTING


OTHER_CHIPS = (4, 2, 6)
ALL_OTHERS = (1, 2, 3, 4, 5, 6, 7)


def _slot(x, y, c, peers):
    return 2 * x + y if peers is OTHER_CHIPS else 4 * x + 2 * y + c


def _slot_copy(land, sems, idx, x, y, c, k, peers, arriving):
    send_sems, recv_sems = sems
    px, py, pc = x ^ (k >> 2), y ^ ((k >> 1) & 1), c ^ (k & 1)
    mine = _slot(x, y, c, peers)
    dst = _slot(px, py, pc, peers) if arriving else mine
    return pltpu.make_async_remote_copy(src_ref=land.at[mine], dst_ref=land.at[dst], send_sem=send_sems.at[idx],
                                        recv_sem=recv_sems.at[idx], device_id=(px, py, pc), device_id_type=MESH)


def _gather_start(name, stacks, groups, peers, after):
    n = len(stacks)
    ng = len(groups)
    np_ = len(peers)
    after = tuple(after)

    def body(*refs):
        lands = refs[:n]
        first = n + len(after)
        sems = [(refs[first + 2 * g], refs[first + 2 * g + 1]) for g in range(ng)]
        token = refs[-1]
        x, y, c = _coords()
        for g, members in enumerate(groups):
            for i, t in enumerate(members):
                for j, k in enumerate(peers):
                    _slot_copy(lands[t], sems[g], np_ * i + j, x, y, c, k, peers, arriving=False).start()
        token[...] = jnp.zeros_like(token)

    out_shape = []
    for members in groups:
        out_shape += [pltpu.SemaphoreType.DMA((np_ * len(members),))] * 2
    out_shape += [pltpu.HBM(w.shape, w.dtype) for w in stacks]
    out_shape.append(jax.ShapeDtypeStruct((8, 128), F32))
    res = pl.pallas_call(
        body, name=name, out_shape=out_shape, in_specs=[HBM_SPEC] * n + [ANY_SPEC] * len(after),
        out_specs=[SEM_SPEC] * (2 * ng) + [HBM_SPEC] * n + [pl.BlockSpec(memory_space=pltpu.VMEM)],
        input_output_aliases={t: 2 * ng + t for t in range(n)},
        compiler_params=pltpu.CompilerParams(has_side_effects=DATAFLOW),
    )(*[pltpu.with_memory_space_constraint(w, pltpu.HBM) for w in stacks], *after)
    sems = [(res[2 * g], res[2 * g + 1]) for g in range(ng)]
    return sems, list(res[2 * ng:2 * ng + n]), res[-1]


def _gather_wait(name, stacks, sems, peers, after):
    n = len(stacks)
    after = tuple(after)

    def body(*refs):
        lands = refs[:n]
        group_sems = (refs[n], refs[n + 1])
        x, y, c = _coords()
        for i in range(n):
            for j, k in enumerate(peers):
                cp = _slot_copy(lands[i], group_sems, len(peers) * i + j, x, y, c, k, peers, arriving=True)
                cp.wait_send()
                cp.wait_recv()

    return pl.pallas_call(
        body, name=name, out_shape=[pltpu.HBM(w.shape, w.dtype) for w in stacks],
        in_specs=[HBM_SPEC] * n + [SEM_SPEC, SEM_SPEC] + [ANY_SPEC] * len(after), out_specs=[HBM_SPEC] * n,
        input_output_aliases={t: t for t in range(n)},
        compiler_params=pltpu.CompilerParams(has_side_effects=DATAFLOW),
    )(*stacks, sems[0], sems[1], *after)


N_PEERS = 7


def _peer(x, y, c, k):
    return x ^ (k >> 2), y ^ ((k >> 1) & 1), c ^ (k & 1)


def _reduce_copy(grad, land, sems, idx, x, y, c, k):
    px, py, pc = _peer(x, y, c, k)
    rh = grad.shape[1] // 2
    return pltpu.make_async_remote_copy(src_ref=grad.at[2 * px + py, pl.ds(pc * rh, rh), :], dst_ref=land.at[k - 1],
                                        send_sem=sems[0].at[idx], recv_sem=sems[1].at[idx], device_id=(px, py, pc),
                                        device_id_type=MESH)


def _reduce_start(name, grads):
    n = len(grads)

    def body(*refs):
        src, lands, sems, token = refs[:n], refs[n:2 * n], (refs[2 * n], refs[2 * n + 1]), refs[-1]
        x, y, c = _coords()
        for t in range(n):
            for k in range(1, N_PEERS + 1):
                _reduce_copy(src[t], lands[t], sems, N_PEERS * t + k - 1, x, y, c, k).start()
        token[...] = jnp.zeros_like(token)

    lands = [lax.empty((N_PEERS, g.shape[1] // 2, g.shape[2]), g.dtype) for g in grads]
    out_shape = [pltpu.SemaphoreType.DMA((N_PEERS * n,))] * 2
    out_shape += [pltpu.HBM(a.shape, a.dtype) for a in list(grads) + lands]
    out_shape.append(jax.ShapeDtypeStruct((8, 128), F32))
    res = pl.pallas_call(
        body, name=name, out_shape=out_shape, in_specs=[HBM_SPEC] * (2 * n),
        out_specs=[SEM_SPEC] * 2 + [HBM_SPEC] * (2 * n) + [pl.BlockSpec(memory_space=pltpu.VMEM)],
        input_output_aliases={t: 2 + t for t in range(2 * n)},
        compiler_params=pltpu.CompilerParams(has_side_effects=DATAFLOW),
    )(*[pltpu.with_memory_space_constraint(a, pltpu.HBM) for a in list(grads) + lands])
    return (res[0], res[1]), list(res[2:2 + n]), list(res[2 + n:2 + 2 * n]), res[-1]


def _reduce_wait(name, grads, lands, sems, after):
    n = len(grads)
    after = tuple(after)

    def body(*refs):
        src, dst, group_sems = refs[:n], refs[n:2 * n], (refs[2 * n], refs[2 * n + 1])
        x, y, c = _coords()
        for t in range(n):
            for k in range(1, N_PEERS + 1):
                cp = _reduce_copy(src[t], dst[t], group_sems, N_PEERS * t + k - 1, x, y, c, k)
                cp.wait_send()
                cp.wait_recv()

    res = pl.pallas_call(
        body, name=name, out_shape=[pltpu.HBM(a.shape, a.dtype) for a in list(grads) + list(lands)],
        in_specs=[HBM_SPEC] * (2 * n) + [SEM_SPEC, SEM_SPEC] + [ANY_SPEC] * len(after), out_specs=[HBM_SPEC] * (2 * n),
        input_output_aliases={t: t for t in range(2 * n)},
        compiler_params=pltpu.CompilerParams(has_side_effects=DATAFLOW),
    )(*grads, *lands, sems[0], sems[1], *after)
    return list(res[:n]), list(res[n:])


def _join_copy(half, land, sems, idx, x, y, c):
    return pltpu.make_async_remote_copy(src_ref=half, dst_ref=land, send_sem=sems[0].at[idx], recv_sem=sems[1].at[idx],
                                        device_id=(x, y, 1 - c), device_id_type=MESH)


def _join_start(name, halves):
    n = len(halves)

    def body(*refs):
        src, lands, sems, token = refs[:n], refs[n:2 * n], (refs[2 * n], refs[2 * n + 1]), refs[-1]
        x, y, c = _coords()
        for t in range(n):
            _join_copy(src[t], lands[t], sems, t, x, y, c).start()
        token[...] = jnp.zeros_like(token)

    lands = [lax.empty(h.shape, h.dtype) for h in halves]
    out_shape = [pltpu.SemaphoreType.DMA((n,))] * 2
    out_shape += [pltpu.HBM(a.shape, a.dtype) for a in list(halves) + lands]
    out_shape.append(jax.ShapeDtypeStruct((8, 128), F32))
    res = pl.pallas_call(
        body, name=name, out_shape=out_shape, in_specs=[HBM_SPEC] * (2 * n),
        out_specs=[SEM_SPEC] * 2 + [HBM_SPEC] * (2 * n) + [pl.BlockSpec(memory_space=pltpu.VMEM)],
        input_output_aliases={t: 2 + t for t in range(2 * n)},
        compiler_params=pltpu.CompilerParams(has_side_effects=DATAFLOW),
    )(*[pltpu.with_memory_space_constraint(a, pltpu.HBM) for a in list(halves) + lands])
    return (res[0], res[1]), list(res[2:2 + n]), list(res[2 + n:2 + 2 * n]), res[-1]


def _join_wait(name, halves, lands, sems, after):
    n = len(halves)
    after = tuple(after)

    def body(*refs):
        src, dst, group_sems = refs[:n], refs[n:2 * n], (refs[2 * n], refs[2 * n + 1])
        x, y, c = _coords()
        for t in range(n):
            cp = _join_copy(src[t], dst[t], group_sems, t, x, y, c)
            cp.wait_send()
            cp.wait_recv()

    res = pl.pallas_call(
        body, name=name, out_shape=[pltpu.HBM(a.shape, a.dtype) for a in list(halves) + list(lands)],
        in_specs=[HBM_SPEC] * (2 * n) + [SEM_SPEC, SEM_SPEC] + [ANY_SPEC] * len(after), out_specs=[HBM_SPEC] * (2 * n),
        input_output_aliases={t: t for t in range(2 * n)},
        compiler_params=pltpu.CompilerParams(has_side_effects=DATAFLOW),
    )(*halves, *lands, sems[0], sems[1], *after)
    return list(res[:n]), list(res[n:])


def _join_halves(name, halves, deps=()):
    n = len(halves)

    def body(*refs):
        src, dst = refs[:n], refs[n + len(deps):2 * n + len(deps)]
        send_sems, recv_sems = refs[-2:]
        x, y, c = _coords()
        cps = []
        for t in range(n):
            cp = pltpu.make_async_remote_copy(src_ref=src[t], dst_ref=dst[t], send_sem=send_sems.at[t],
                                              recv_sem=recv_sems.at[t], device_id=(x, y, 1 - c), device_id_type=MESH)
            cp.start()
            cps.append(cp)
        for cp in cps:
            cp.wait()

    anyspec = pl.BlockSpec(memory_space=pl.ANY)
    return pl.pallas_call(
        body, name=name, out_shape=[jax.ShapeDtypeStruct(h.shape, h.dtype) for h in halves],
        in_specs=[anyspec] * (n + len(deps)), out_specs=[anyspec] * n,
        scratch_shapes=[pltpu.SemaphoreType.DMA((n,)), pltpu.SemaphoreType.DMA((n,))],
    )(*halves, *deps)


BF16_ROWS = 16
MAX_ROW_BLOCK = 512


def _row_block(rows):
    for rb in range(min(rows, MAX_ROW_BLOCK), 0, -1):
        if rows % rb == 0 and rb % BF16_ROWS == 0:
            return rb
    raise ValueError(rows)


def _sum_devices(name, grad, land, place):
    S, R, C = grad.shape
    rh = R // 2
    rb = _row_block(rh)
    nbh = rh // rb

    def body(place_ref, g_ref, l_ref, o_ref):
        tot = g_ref[...].astype(F32)
        for k in range(N_PEERS):
            tot = tot + l_ref[k].astype(F32)
        o_ref[...] = tot

    return pl.pallas_call(
        body, name=name,
        grid_spec=pltpu.PrefetchScalarGridSpec(
            num_scalar_prefetch=1, grid=(nbh,),
            in_specs=[pl.BlockSpec((None, rb, C), lambda r, place: (place[0], place[1] * nbh + r, 0)),
                      pl.BlockSpec((N_PEERS, rb, C), lambda r, place: (0, r, 0))],
            out_specs=pl.BlockSpec((rb, C), lambda r, place: (r, 0))),
        out_shape=jax.ShapeDtypeStruct((rh, C), F32), compiler_params=_params(1),
    )(place, grad, land)


def _adamw_math(w, g, m, v):
    m2 = ADAM_B1 * m + (1.0 - ADAM_B1) * g
    v2 = ADAM_B2 * v + (1.0 - ADAM_B2) * (g * g)
    m_hat = m2 / (1.0 - ADAM_B1 ** ADAM_STEP)
    v_hat = v2 / (1.0 - ADAM_B2 ** ADAM_STEP)
    delta = -ADAM_LR * (m_hat / (jnp.sqrt(v_hat) + ADAM_EPS) + ADAM_WD * w)
    return delta, m2, v2


def _adamw(name, w, m, v, gs):
    L, R, C = w.shape
    Rh = R // 2
    rb = _row_block(Rh)
    nbh = Rh // rb
    assert len(gs) == L

    def body(core_ref, w_ref, m_ref, v_ref, *rest):
        g_refs, (go_ref, d_ref, m2_ref, v2_ref) = rest[:2 * L], rest[2 * L:]
        layer, half = pl.program_id(0), pl.program_id(1)
        mine = half == core_ref[0]
        g = jnp.where(mine, g_refs[0][...], g_refs[1][...])
        for t in range(1, L):
            g = jnp.where(layer == t, jnp.where(mine, g_refs[2 * t][...], g_refs[2 * t + 1][...]), g)
        delta, m2, v2 = _adamw_math(w_ref[...], g, m_ref[...], v_ref[...])
        go_ref[...] = g
        d_ref[...] = delta
        m2_ref[...] = m2
        v2_ref[...] = v2

    wspec = pl.BlockSpec((None, rb, C), lambda l, h, r, core: (l, h * nbh + r, 0))
    gspec = pl.BlockSpec((rb, C), lambda l, h, r, core: (r, 0))
    return pl.pallas_call(
        body, name=name,
        grid_spec=pltpu.PrefetchScalarGridSpec(num_scalar_prefetch=1, grid=(L, 2, nbh),
                                               in_specs=[wspec] * 3 + [gspec] * (2 * L), out_specs=[wspec] * 4),
        out_shape=[jax.ShapeDtypeStruct((L, R, C), F32)] * 4, compiler_params=_params(3),
    )(lax.axis_index("c").astype(jnp.int32).reshape(1), w, m, v, *[g for pair in gs for g in pair])


def _adamw_small(ws, gs, ms, vs):
    n = len(ws)

    def body(*refs):
        w_refs, g_refs, m_refs, v_refs = (refs[k * n:(k + 1) * n] for k in range(4))
        d_refs, m2_refs, v2_refs = (refs[(4 + k) * n:(5 + k) * n] for k in range(3))
        for t in range(n):
            delta, m2, v2 = _adamw_math(w_refs[t][...], g_refs[t][...], m_refs[t][...], v_refs[t][...])
            d_refs[t][...] = delta
            m2_refs[t][...] = m2
            v2_refs[t][...] = v2

    res = pl.pallas_call(body, name="adamw_small", out_shape=[jax.ShapeDtypeStruct(w.shape, F32) for w in ws] * 3)(
        *ws, *gs, *ms, *vs)
    return res[:n], res[n:2 * n], res[2 * n:]


def _packed_rows(shape):
    c = shape[-1]
    return (int(np.prod(shape)) // c) * -(-c // LANES)


def _pack(arrays):
    total = sum(_packed_rows(a.shape) for a in arrays)
    total += -total % 8
    buf, r0 = None, 0
    for a in arrays:
        a = a.astype(F32).reshape(-1, a.shape[-1])
        r, c = a.shape
        k = -(-c // LANES)
        a = jnp.pad(a, ((0, 0), (0, k * LANES - c))).reshape(r * k, LANES)
        a = jnp.pad(a, ((r0, total - r0 - r * k), (0, 0)))
        buf = a if buf is None else buf + a
        r0 += r * k
    return buf


def _unpack(buf, shapes):
    out, r0 = [], 0
    for shp in shapes:
        c = shp[-1]
        rows = _packed_rows(shp)
        out.append(buf[r0:r0 + rows].reshape(-1, -(-c // LANES) * LANES)[:, :c].reshape(shp))
        r0 += rows
    return out


def _rms(x, g):
    return x * lax.rsqrt(jnp.mean(x * x, axis=-1, keepdims=True) + NORM_EPS) * g


def _residual_norm_ep(acc, *rest):
    *bias, res, gain = rest
    x = acc + res + (bias[0] if bias else 0.0)
    return x, _rms(x, gain)


RESIDUAL_NORM_OUTS = (("tile", F32), ("tile", BF))


def _mlp_up(tag, h, w_up_sm):
    (up,) = _mm(f"mlp{tag}_up", h, w_up_sm, nt=False, b_sm=True, tm=2048, tn=1024, rows=256,
                ep_fn=lambda acc: (acc,), outs=(("tile", BF),))
    return up


RMS_BWD_OUTS = (("tile", F32), ("tile", BF), ("colsum", F32), ("colsum", F32))


def _mlp_bwd(tag, dy, dy_bf, x, g, up, w_up_sm, w_down):
    (dup,) = _mm(f"mlp{tag}_dup", dy_bf, w_down, nt=True, tm=2048, tn=1024, rows=256, ep_in=((up, "tile"),),
                 ep_fn=lambda acc, u: (acc * (2.0 * jnp.maximum(u.astype(F32), 0.0)),), outs=(("tile", BF),))
    dx, dx_bf, dg, dx_sum = _mm(f"mlp{tag}_dx", dup, w_up_sm, nt=True, b_sm=True, tm=512, tn=1024, rows=256,
                                ep_in=((x, "tile"), (g, "row"), (dy, "tile")), ep_fn=_rms_bwd_ep, outs=RMS_BWD_OUTS)
    return dx, dx_bf, dg, dx_sum, dup


class _Reduction:
    def __init__(self, tag, grads, place):
        self.tag, self.place = tag, place
        self.sems, self.grads, self.lands, self.token = _reduce_start(f"reduce_start_{tag}", grads)

    def finish(self, after):
        grads, lands = _reduce_wait(f"reduce_wait_{self.tag}", self.grads, self.lands, self.sems, after)
        return [_sum_devices(f"reduce_sum_{self.tag}{i}", g, l, self.place) for i, (g, l) in enumerate(zip(grads, lands))]


def kernel(x, conv_norm_g, conv_w_in, conv_b_in, conv_dw, conv_dw_b, conv_ln_g, conv_ln_b, conv_w_out, conv_b_out, attn_norm_g, w_qkv, b_qkv, q_norm_g, k_norm_g, sinks, w_o, b_o, rel_bias, mlp_norm_g, w_up, w_down, loss_target, m_conv_norm_g, m_conv_w_in, m_conv_b_in, m_conv_dw, m_conv_dw_b, m_conv_ln_g, m_conv_ln_b, m_conv_w_out, m_conv_b_out, m_attn_norm_g, m_w_qkv, m_b_qkv, m_q_norm_g, m_k_norm_g, m_sinks, m_w_o, m_b_o, m_rel_bias, m_mlp_norm_g, m_w_up, m_w_down, v_conv_norm_g, v_conv_w_in, v_conv_b_in, v_conv_dw, v_conv_dw_b, v_conv_ln_g, v_conv_ln_b, v_conv_w_out, v_conv_b_out, v_attn_norm_g, v_w_qkv, v_b_qkv, v_q_norm_g, v_k_norm_g, v_sinks, v_w_o, v_b_o, v_rel_bias, v_mlp_norm_g, v_w_up, v_w_down):
    Dm = D_MODEL
    x2d = x[0]
    tgt = loss_target[0]
    T = x2d.shape[0]
    shard = 2 * lax.axis_index("x") + lax.axis_index("y")

    me = 2 * shard + lax.axis_index("c")

    def own_slot(block, slots, index):
        return lax.dynamic_update_slice(lax.empty((slots,) + block.shape, block.dtype), block[None],
                                        (index,) + (0,) * block.ndim)

    (conv_in_sems,), (stack_in,), first_token = _gather_start(
        "gather_start_conv_in", [own_slot(conv_w_in[0].astype(BF), N_SHARD, shard)], ((0,),), OTHER_CHIPS, after=())
    sharded_small = [conv_dw[0], attn_norm_g, b_qkv, b_o]
    (small_sems,), (small_land,), small_token = _gather_start(
        "small_weights_start", [own_slot(_pack(sharded_small), 8, me)], ((0,),), ALL_OTHERS, after=(first_token,))

    big = [conv_w_out[0], jnp.swapaxes(w_qkv, 1, 2)[0], w_o[0], w_up[0], w_up[1], w_down[0], w_down[1]]
    stacks = [own_slot(w.astype(BF), N_SHARD, shard) for w in big]
    groups = ((0,), (3, 5), (1, 2), (4, 6))
    gather_sems, stacks, gather_token = _gather_start("gather_start", stacks, groups, OTHER_CHIPS, after=(small_token,))

    def gathered_group(g, name, after):
        return _gather_wait(name, [stacks[t] for t in groups[g]], gather_sems[g], OTHER_CHIPS, after)

    bucket = _bucket_table()
    bias = _bias_table(rel_bias.T, bucket)

    h0 = _rms_fwd("conv_norm", x2d, conv_norm_g, tm=1024, deps=(gather_token,))
    (w_in_sm,) = _gather_wait("gather_wait_conv_in", [stack_in], conv_in_sems, OTHER_CHIPS, (h0, bias))
    (u,) = _mm("conv_in", h0, w_in_sm, nt=False, b_sm=True, tm=2048, tn=512, rows=256, ep_in=((conv_b_in, "row"),),
               ep_fn=lambda acc, b: (acc + b,), outs=(("tile", BF),))
    (gathered,) = _gather_wait("small_weights_wait", [small_land], small_sems, ALL_OTHERS, (u,))
    chips = [_unpack(gathered[2 * s], [a.shape for a in sharded_small]) for s in range(N_SHARD)]
    dw_f, attn_norm_f, b_qkv_f, b_o_f = (jnp.concatenate([chips[s][t] for s in range(N_SHARD)], axis=-1)
                                         for t in range(len(sharded_small)))
    dw_pad = jnp.pad(dw_f, ((0, HALO - CONV_W), (0, 0)))
    cv, s_act = _conv_fwd(u, dw_pad, conv_dw_b, conv_ln_g, conv_ln_b)
    (g_out,) = gathered_group(0, "gather_wait_conv_out", (s_act,))
    w_out_f = g_out.reshape(Dm, Dm)
    x1, h1 = _mm("conv_out", s_act, w_out_f, nt=False, tm=1024, tn=1024, rows=256,
                 ep_in=((conv_b_out, "row"), (x2d, "tile"), (mlp_norm_g[0:1], "row")), ep_fn=_residual_norm_ep,
                 outs=RESIDUAL_NORM_OUTS)

    g_up0, g_down0 = gathered_group(1, "gather_wait_mlp0", (x1,))
    w_up_sm = [g_up0, None]
    w_down_f = [g_down0.reshape(D_FF, Dm), None]
    up0 = _mlp_up(0, h1, w_up_sm[0])
    x2, h2 = _mm("mlp0_down", up0, w_down_f[0], nt=False, tm=512, tn=1024, rows=256, a_fn=_relu2,
                 ep_in=((x1, "tile"), (attn_norm_f, "row")), ep_fn=_residual_norm_ep, outs=RESIDUAL_NORM_OUTS)

    g_qkv, g_o = gathered_group(2, "gather_wait_attn", (x2,))
    w_qkv_t = g_qkv.reshape(QKV_DIM, Dm)
    w_o_f = g_o.reshape(ATTN_DIM, Dm)
    qg_t = jnp.tile(q_norm_g, (1, N_HEADS))
    kg_t = jnp.tile(k_norm_g, (1, N_KV))

    def qkv_ep(acc, b, qg, kg, ones):
        proj = acc + b
        q, k, v = proj[:, :ATTN_DIM], proj[:, ATTN_DIM:ATTN_DIM + KV_DIM], proj[:, ATTN_DIM + KV_DIM:]
        return proj, _qk_normed(q, qg, ones, 1.0 / math.sqrt(HEAD_DIM)), _qk_normed(k, kg, ones, 1.0), v

    qkv, qn, kn, vv = _mm(
        "attn_qkv", h2, w_qkv_t, nt=True, tm=1024, tn=QKV_DIM, rows=256, ep_fn=qkv_ep,
        ep_in=((b_qkv_f, "row"), (qg_t, "whole"), (kg_t, "whole"), (_head_ones(), "whole")),
        outs=(("tile", F32), ("tile", BF, ATTN_DIM), ("tile", BF, KV_DIM), ("tile", BF, KV_DIM)))
    sinks1 = sinks[0]
    att = _attn_fwd(qn, kn, vv, bias, sinks1)
    x3, h3 = _mm("attn_out", att, w_o_f, nt=False, tm=1024, tn=1024, rows=256,
                 ep_in=((b_o_f, "row"), (x2, "tile"), (mlp_norm_g[1:2], "row")), ep_fn=_residual_norm_ep,
                 outs=RESIDUAL_NORM_OUTS)

    g_up1, g_down1 = gathered_group(3, "gather_wait_mlp1", (x3,))
    w_up_sm[1] = g_up1
    w_down_f[1] = g_down1.reshape(D_FF, Dm)
    up1 = _mlp_up(1, h3, w_up_sm[1])

    def loss_ep(acc, r, t):
        diff = acc + r - t
        dy = diff * (1.0 / Dm)
        return dy, dy, jnp.sum(diff * diff, axis=0, keepdims=True)

    dy, dy_bf, sq = _mm("mlp1_down_loss", up1, w_down_f[1], nt=False, tm=512, tn=1024, rows=256, a_fn=_relu2,
                        ep_in=((x3, "tile"), (tgt, "tile")), ep_fn=loss_ep,
                        outs=(("tile", F32), ("tile", BF), ("colsum", F32)))

    place = jnp.stack([shard, lax.axis_index("c")]).astype(jnp.int32)
    dx3, dx3_bf, dg_mlp1, db_o, dup1 = _mlp_bwd(1, dy, dy_bf, x3, mlp_norm_g[1:2], up1, w_up_sm[1], w_down_f[1])
    dw_down1 = _mm_tn("mlp1_dw_down", up1, dy_bf, tm=1024, tn=1024, tk=2048, a_fn=_relu2)
    dw_up1 = _mm_tn("mlp1_dw_up", h3, dup1, tm=1024, tn=1024, tk=2048, out_sm=N_SHARD)
    red_mlp1 = _Reduction("mlp1", [dw_up1, dw_down1.reshape(N_SHARD, D_FF // N_SHARD, Dm)], place)

    ident = lambda acc: (acc,)
    (datt,) = _mm("attn_dout", dx3_bf, w_o_f, nt=True, tm=1024, tn=1024, rows=256, ep_fn=ident, outs=(("tile", BF),),
                  deps=(red_mlp1.token,))
    dw_o = _mm_tn("attn_dw_o", att, dx3_bf, tm=1024, tn=1024, tk=2048)
    dqn, dkn, dvv, dbias, dsinks = _attn_bwd(qn, kn, vv, bias, sinks1, datt)
    drel = _bias_grad(dbias, bucket)
    dqkv, db_qkv, dqg_t, dkg_t = _qk_norm_bwd(qkv, dqn, dkn, dvv, qg_t, kg_t)
    dw_qkv_t = _mm_tn("attn_dw_qkv", dqkv, h2, tm=QKV_DIM, tn=1024, tk=2048)
    red_attn = _Reduction("attn", [dw_qkv_t.reshape(N_SHARD, QKV_DIM // N_SHARD, Dm),
                                   dw_o.reshape(N_SHARD, ATTN_DIM // N_SHARD, Dm)], place)
    dx2, dx2_bf, dg_attn, _ = _mm("attn_dx", dqkv, w_qkv_t, nt=False, tm=1024, tn=1024, rows=256,
                                  ep_in=((x2, "tile"), (attn_norm_f, "row"), (dx3, "tile")), ep_fn=_rms_bwd_ep,
                                  outs=RMS_BWD_OUTS, deps=(red_attn.token,))

    dx1, dx1_bf, dg_mlp0, db_out, dup0 = _mlp_bwd(0, dx2, dx2_bf, x1, mlp_norm_g[0:1], up0, w_up_sm[0], w_down_f[0])
    dw_down0 = _mm_tn("mlp0_dw_down", up0, dx2_bf, tm=1024, tn=1024, tk=2048, a_fn=_relu2)
    dw_up0 = _mm_tn("mlp0_dw_up", h1, dup0, tm=1024, tn=1024, tk=2048, out_sm=N_SHARD)
    dw_out = _mm_tn("conv_dw_out", s_act, dx1_bf, tm=1024, tn=1024, tk=2048)
    red_mlp0 = _Reduction("mlp0", [dw_up0, dw_down0.reshape(N_SHARD, D_FF // N_SHARD, Dm),
                                   dw_out.reshape(N_SHARD, Dm // N_SHARD, Dm)], place)
    (r_qkv, r_o) = red_attn.finish((dx1,))
    (r_up1, r_down1) = red_mlp1.finish((dx1,))

    dcv, dln_g, dln_b, ddw_b = _mm("conv_ds", dx1_bf, w_out_f, nt=True, tm=1024, tn=1024, rows=256,
                                   ep_in=((cv, "tile"), (conv_ln_g, "row"), (conv_ln_b, "row")),
                                   ep_fn=_ln_silu_bwd_ep,
                                   outs=(("tile", F32), ("colsum", F32), ("colsum", F32), ("colsum", F32)),
                                   deps=(red_mlp0.token,))
    du, db_in, ddw8 = _conv_bwd(u, dcv, dw_pad)
    (r_up0, r_down0, r_out) = red_mlp0.finish((du,))
    early = [r_out, r_qkv, r_o, r_up0, r_up1, r_down0, r_down1]
    join_sems, early, early_lands, join_token = _join_start("join_start", early)
    dw_in = _mm_tn("conv_dw_in", h0, du, tm=1024, tn=512, tk=4096, out_sm=N_SHARD)
    red_conv = _Reduction("conv", [dw_in], place)
    def first_layer_ep(*args):
        tot, _, dg, _ = _rms_bwd_ep(*args)
        return tot, dg

    gx, dg_conv = _mm("conv_dx", du, w_in_sm, nt=True, b_sm=True, tm=1024, tn=1024, rows=256,
                      ep_in=((x2d, "tile"), (conv_norm_g, "row"), (dx1, "tile")), ep_fn=first_layer_ep,
                      outs=(("tile", F32), ("colsum", F32)), deps=(red_conv.token, join_token))
    (r_in,) = red_conv.finish((gx,))

    dqg = dqg_t.reshape(N_HEADS, HEAD_DIM).sum(axis=0, keepdims=True)
    dkg = dkg_t.reshape(N_KV, HEAD_DIM).sum(axis=0, keepdims=True)
    small_full = [dg_conv, db_in, ddw8.sum(axis=1)[:CONV_W], ddw_b, dln_g, dln_b, db_out, dg_attn, db_qkv, dqg, dkg,
                  dsinks[None, :], db_o, drel.reshape(1, REL_BUCKETS * N_HEADS),
                  jnp.pad(dg_mlp0, ((0, 1), (0, 0))) + jnp.pad(dg_mlp1, ((1, 0), (0, 0))), sq]
    (sg_sems,), (sg_land,), sg_token = _gather_start(
        "small_grads_start", [own_slot(_pack(small_full), 8, me)], ((0,),), ALL_OTHERS, after=())

    early, early_sibling = _join_wait("join_wait", early, early_lands, join_sems, (gx, sg_token))
    r_out, r_qkv, r_o, r_up0, r_up1, r_down0, r_down1 = zip(early, early_sibling)
    r_in = (r_in,) + tuple(_join_halves("join_halves", [r_in], deps=(sg_token,)))

    big_out = {}
    qkv_t = [jnp.swapaxes(a, 1, 2) for a in (w_qkv, m_w_qkv, v_w_qkv)]
    for nm, w, m, v, gs in (("conv_w_in", conv_w_in, m_conv_w_in, v_conv_w_in, (r_in,)),
                            ("conv_w_out", conv_w_out, m_conv_w_out, v_conv_w_out, (r_out,)),
                            ("w_qkv", *qkv_t, (r_qkv,)),
                            ("w_o", w_o, m_w_o, v_w_o, (r_o,)),
                            ("w_up", w_up, m_w_up, v_w_up, (r_up0, r_up1)),
                            ("w_down", w_down, m_w_down, v_w_down, (r_down0, r_down1))):
        big_out[nm] = _adamw(f"adamw_{nm}", w, m, v, gs)

    (sg_land,) = _gather_wait("small_grads_wait", [sg_land], sg_sems, ALL_OTHERS,
                              [big_out[nm][0] for nm in big_out])
    big_out["w_qkv"] = tuple(jnp.swapaxes(a, 1, 2) for a in big_out["w_qkv"])
    small_sum = _sum8("small_grads_sum", sg_land)
    (r_norm, r_b_in, r_dw, r_dw_b, r_ln_g, r_ln_b, r_b_out, r_attn_norm, r_b_qkv, r_qg, r_kg, r_sinks, r_b_o, r_rel,
     r_mlp_norm, r_sq) = _unpack(small_sum, [a.shape for a in small_full])
    loss = 0.5 * jnp.sum(r_sq) * (1.0 / Dm)

    def cols(a, width):
        return lax.dynamic_slice_in_dim(a, shard * width, width, axis=a.ndim - 1)

    small_names = ["conv_norm_g", "conv_b_in", "conv_dw", "conv_dw_b", "conv_ln_g", "conv_ln_b", "conv_b_out",
                   "attn_norm_g", "b_qkv", "q_norm_g", "k_norm_g", "sinks", "b_o", "rel_bias", "mlp_norm_g"]
    small_g = [r_norm, r_b_in, cols(r_dw, Dm // N_SHARD)[None], r_dw_b, r_ln_g, r_ln_b, r_b_out,
               cols(r_attn_norm, Dm // N_SHARD), cols(r_b_qkv, QKV_DIM // N_SHARD), r_qg, r_kg, r_sinks,
               cols(r_b_o, Dm // N_SHARD), r_rel.reshape(N_HEADS, REL_BUCKETS), r_mlp_norm]
    small_w = [conv_norm_g, conv_b_in, conv_dw, conv_dw_b, conv_ln_g, conv_ln_b, conv_b_out, attn_norm_g, b_qkv,
               q_norm_g, k_norm_g, sinks, b_o, rel_bias.T, mlp_norm_g]
    small_m = [m_conv_norm_g, m_conv_b_in, m_conv_dw, m_conv_dw_b, m_conv_ln_g, m_conv_ln_b, m_conv_b_out,
               m_attn_norm_g, m_b_qkv, m_q_norm_g, m_k_norm_g, m_sinks, m_b_o, m_rel_bias.T, m_mlp_norm_g]
    small_v = [v_conv_norm_g, v_conv_b_in, v_conv_dw, v_conv_dw_b, v_conv_ln_g, v_conv_ln_b, v_conv_b_out,
               v_attn_norm_g, v_b_qkv, v_q_norm_g, v_k_norm_g, v_sinks, v_b_o, v_rel_bias.T, v_mlp_norm_g]
    flat2 = lambda a: a.reshape(-1, a.shape[-1])
    small_g = [flat2(g) for g in small_g]
    d_s, m_s, v_s = _adamw_small([flat2(w) for w in small_w], small_g, [flat2(m) for m in small_m],
                                 [flat2(v) for v in small_v])
    small_out = {}
    for nm, w, g, d, m2, v2 in zip(small_names, small_w, small_g, d_s, m_s, v_s):
        small_out[nm] = tuple(a.reshape(w.shape) for a in (g, d, m2, v2))
    small_out["rel_bias"] = tuple(a.T for a in small_out["rel_bias"])

    order = ["conv_norm_g", "conv_w_in", "conv_b_in", "conv_dw", "conv_dw_b", "conv_ln_g", "conv_ln_b", "conv_w_out",
             "conv_b_out", "attn_norm_g", "w_qkv", "b_qkv", "q_norm_g", "k_norm_g", "sinks", "w_o", "b_o", "rel_bias",
             "mlp_norm_g", "w_up", "w_down"]
    res = {**small_out, **big_out}
    outs = [loss, gx[None]]
    for part in range(4):
        outs += [res[nm][part] for nm in order]
    return tuple(outs)
```

```python
import math

import numpy as np
import jax
import jax.numpy as jnp
from jax import lax
from jax.experimental import pallas as pl
from jax.experimental.pallas import tpu as pltpu

F32 = jnp.float32
BF = jnp.bfloat16
MESH = pl.DeviceIdType.MESH

D_MODEL = 1024
D_FF = 4096
N_HEADS = 16
N_KV = 2
GROUP = N_HEADS // N_KV
HEAD_DIM = 64
ATTN_DIM = N_HEADS * HEAD_DIM
KV_DIM = N_KV * HEAD_DIM
QKV_DIM = ATTN_DIM + 2 * KV_DIM
BLOCK = 128
CONV_W = 31
HALO = 32
REL_BUCKETS = 32
REL_MAX_DIST = 128
NORM_EPS = 1e-6
NEG_INF = -1e30
N_SHARD = 4
LANES = 1024

ADAM_LR = 0.001
ADAM_B1 = 0.9
ADAM_B2 = 0.999
ADAM_EPS = 1e-08
ADAM_WD = 0.01
ADAM_STEP = 10

VMEM_LIMIT = 56 * 1024 * 1024


def _params(n_axes):
    return pltpu.CompilerParams(dimension_semantics=("arbitrary",) * n_axes, vmem_limit_bytes=VMEM_LIMIT)


def _dot(a, b, ca, cb):
    return lax.dot_general(a, b, (((ca,), (cb,)), ((), ())), preferred_element_type=F32)


def _mm(name, a, b, *, nt, tm, tn, ep_fn, outs, a_fn=None, b_sm=False, ep_in=(), deps=(), rows=None):
    M, K = a.shape
    rows = tm if rows is None else rows
    if b_sm:
        S, ks = b.shape[0], b.shape[2]
        N, per = (b.shape[1], None) if nt else (S * b.shape[2], b.shape[2] // tn)
        assert (S * ks == K) if nt else (b.shape[1] == K)
    else:
        N = b.shape[0] if nt else b.shape[1]
        assert (b.shape[1] if nt else b.shape[0]) == K
    assert M % tm == 0 and N % tn == 0 and tm % rows == 0
    ne, no, nd = len(ep_in), len(outs), len(deps)

    def body(a_ref, b_ref, *rest):
        ep_refs, out_refs = rest[:ne], rest[ne + nd:ne + nd + no]
        i = pl.program_id(1)
        sums = [None] * no
        for r in range(tm // rows):
            rs = pl.ds(r * rows, rows)

            def lhs(cols):
                av = a_ref[rs, cols]
                return (av if a_fn is None else a_fn(av)).astype(BF)

            if b_sm and nt:
                acc = None
                for s in range(S):
                    part = _dot(lhs(pl.ds(s * ks, ks)), b_ref[s].astype(BF), 1, 1)
                    acc = part if acc is None else acc + part
            else:
                acc = _dot(lhs(slice(None)), b_ref[...].astype(BF), 1, 1 if nt else 0)
            ep_vals = [ref[rs, :] if kind == "tile" else ref[...] for ref, (_, kind) in zip(ep_refs, ep_in)]
            vals = ep_fn(acc, *ep_vals)
            for o, ((kind, dt, *_), ref, val) in enumerate(zip(outs, out_refs, vals)):
                if kind == "tile":
                    ref[rs, :] = val.astype(dt)
                else:
                    sums[o] = val if sums[o] is None else sums[o] + val
        for (kind, *_), ref, val in zip(outs, out_refs, sums):
            if kind == "colsum":
                @pl.when(i == 0)
                def _():
                    ref[...] = val

                @pl.when(i > 0)
                def _():
                    ref[...] += val

    if b_sm and nt:
        b_spec = pl.BlockSpec((S, tn, ks), lambda j, i: (0, j, 0))
    elif b_sm:
        b_spec = pl.BlockSpec((None, K, tn), lambda j, i: (j // per, 0, j % per))
    elif nt:
        b_spec = pl.BlockSpec((tn, K), lambda j, i: (j, 0))
    else:
        b_spec = pl.BlockSpec((K, tn), lambda j, i: (0, j))
    in_specs = [pl.BlockSpec((tm, K), lambda j, i: (i, 0)), b_spec]
    for arr, kind in ep_in:
        if kind == "tile":
            assert arr.shape == (M, N)
            in_specs.append(pl.BlockSpec((tm, tn), lambda j, i: (i, j)))
        elif kind == "whole":
            in_specs.append(pl.BlockSpec(arr.shape, lambda j, i, rank=arr.ndim: (0,) * rank))
        else:
            assert arr.shape == (1, N)
            in_specs.append(pl.BlockSpec((1, tn), lambda j, i: (0, j)))
    in_specs += [pl.BlockSpec(memory_space=pl.ANY)] * nd
    out_shape, out_specs = [], []
    for kind, dt, *width in outs:
        if kind == "tile" and width:
            assert tn == N
            out_shape.append(jax.ShapeDtypeStruct((M, width[0]), dt))
            out_specs.append(pl.BlockSpec((tm, width[0]), lambda j, i: (i, 0)))
        elif kind == "tile":
            out_shape.append(jax.ShapeDtypeStruct((M, N), dt))
            out_specs.append(pl.BlockSpec((tm, tn), lambda j, i: (i, j)))
        else:
            out_shape.append(jax.ShapeDtypeStruct((1, N), F32))
            out_specs.append(pl.BlockSpec((1, tn), lambda j, i: (0, j)))
    return pl.pallas_call(
        body, name=name, grid=(N // tn, M // tm), in_specs=in_specs, out_specs=out_specs, out_shape=out_shape,
        compiler_params=_params(2),
    )(a, b, *[arr for arr, _ in ep_in], *deps)


def _mm_tn(name, a, b, *, tm, tn, tk, a_fn=None, out_sm=None):
    T, Ka = a.shape
    N = b.shape[1]
    assert b.shape[0] == T and T % tk == 0 and Ka % tm == 0 and N % tn == 0
    nk = T // tk

    def body(a_ref, b_ref, o_ref, acc_ref):
        k = pl.program_id(2)

        @pl.when(k == 0)
        def _():
            acc_ref[...] = jnp.zeros_like(acc_ref)

        av = a_ref[...]
        if a_fn is not None:
            av = a_fn(av)
        acc_ref[...] += _dot(av.astype(BF), b_ref[...].astype(BF), 0, 0)

        @pl.when(k == nk - 1)
        def _():
            o_ref[...] = acc_ref[...].astype(BF)

    if out_sm is None:
        out_shape = jax.ShapeDtypeStruct((Ka, N), BF)
        out_spec = pl.BlockSpec((tm, tn), lambda i, j, k: (i, j))
    else:
        per = (N // out_sm) // tn
        assert per * tn * out_sm == N
        out_shape = jax.ShapeDtypeStruct((out_sm, Ka, N // out_sm), BF)
        out_spec = pl.BlockSpec((None, tm, tn), lambda i, j, k: (j // per, i, j % per))
    return pl.pallas_call(
        body, name=name, grid=(Ka // tm, N // tn, nk),
        in_specs=[pl.BlockSpec((tk, tm), lambda i, j, k: (k, i)), pl.BlockSpec((tk, tn), lambda i, j, k: (k, j))],
        out_specs=out_spec, out_shape=out_shape, scratch_shapes=[pltpu.VMEM((tm, tn), F32)],
        compiler_params=_params(3),
    )(a, b)


def _relu2(v):
    r = jnp.maximum(v.astype(F32), 0.0)
    return r * r


def _rms_bwd_ep(dh, x, g, dres):
    rstd = lax.rsqrt(jnp.mean(x * x, axis=-1, keepdims=True) + NORM_EPS)
    xh = x * rstd
    dxh = dh * g
    dx = rstd * (dxh - xh * jnp.mean(dxh * xh, axis=-1, keepdims=True))
    tot = dres + dx
    return tot, tot, jnp.sum(dh * xh, axis=0, keepdims=True), jnp.sum(tot, axis=0, keepdims=True)


def _rms_fwd(name, x, g, tm=512, deps=()):
    T, Dm = x.shape

    def body(x_ref, g_ref, *rest):
        o_ref = rest[-1]
        xv = x_ref[...]
        rstd = lax.rsqrt(jnp.mean(xv * xv, axis=-1, keepdims=True) + NORM_EPS)
        o_ref[...] = (xv * rstd * g_ref[...]).astype(BF)

    return pl.pallas_call(
        body, name=name, grid=(T // tm,),
        in_specs=[pl.BlockSpec((tm, Dm), lambda i: (i, 0)), pl.BlockSpec((1, Dm), lambda i: (0, 0))]
        + [pl.BlockSpec(memory_space=pl.ANY)] * len(deps),
        out_specs=pl.BlockSpec((tm, Dm), lambda i: (i, 0)), out_shape=jax.ShapeDtypeStruct((T, Dm), BF),
        compiler_params=_params(1),
    )(x, g, *deps)


HEAD_GROUP = 256


def _head_sum(v, ones):
    n = v.shape[1]
    w = min(n, HEAD_GROUP)
    blk = ones[:w, :w]
    parts = [_dot(v[:, c:c + w].astype(BF), blk, 1, 0) for c in range(0, n, w)]
    return parts[0] if len(parts) == 1 else jnp.concatenate(parts, axis=1)


def _head_ones():
    idx = np.arange(HEAD_GROUP) // HEAD_DIM
    return jnp.asarray((idx[:, None] == idx[None, :]).astype(np.float32), dtype=BF)


def _qk_normed(x, g, ones, scale):
    r = lax.rsqrt(_head_sum(x * x, ones) * (1.0 / HEAD_DIM) + NORM_EPS)
    return x * r * g * scale


def _qk_norm_bwd(qkv, dqn, dkn, dv, qg_t, kg_t, tm=256):
    T = qkv.shape[0]

    def body(x_ref, dq_ref, dk_ref, dv_ref, qg_ref, kg_ref, ones_ref, o_ref, db_ref, dqg_ref, dkg_ref):
        i = pl.program_id(0)
        ones = ones_ref[...]

        def one(x, dy, g):
            r = lax.rsqrt(_head_sum(x * x, ones) * (1.0 / HEAD_DIM) + NORM_EPS)
            xh = x * r
            dxh = dy * g
            dx = r * (dxh - xh * (_head_sum(dxh * xh, ones) * (1.0 / HEAD_DIM)))
            return dx, jnp.sum(dy * xh, axis=0, keepdims=True)

        dq, dqg = one(x_ref[:, pl.ds(0, ATTN_DIM)], dq_ref[...], qg_ref[...])
        dk, dkg = one(x_ref[:, pl.ds(ATTN_DIM, KV_DIM)], dk_ref[...], kg_ref[...])
        dvv = dv_ref[...]
        o_ref[:, pl.ds(0, ATTN_DIM)] = dq.astype(BF)
        o_ref[:, pl.ds(ATTN_DIM, KV_DIM)] = dk.astype(BF)
        o_ref[:, pl.ds(ATTN_DIM + KV_DIM, KV_DIM)] = dvv.astype(BF)
        sq, sk, sv = (jnp.sum(t, axis=0, keepdims=True) for t in (dq, dk, dvv))

        @pl.when(i == 0)
        def _():
            db_ref[:, pl.ds(0, ATTN_DIM)] = sq
            db_ref[:, pl.ds(ATTN_DIM, KV_DIM)] = sk
            db_ref[:, pl.ds(ATTN_DIM + KV_DIM, KV_DIM)] = sv
            dqg_ref[...] = dqg
            dkg_ref[...] = dkg

        @pl.when(i > 0)
        def _():
            db_ref[:, pl.ds(0, ATTN_DIM)] += sq
            db_ref[:, pl.ds(ATTN_DIM, KV_DIM)] += sk
            db_ref[:, pl.ds(ATTN_DIM + KV_DIM, KV_DIM)] += sv
            dqg_ref[...] += dqg
            dkg_ref[...] += dkg

    full = lambda shape: pl.BlockSpec(shape, lambda i: (0, 0))
    row = lambda n: pl.BlockSpec((tm, n), lambda i: (i, 0))
    return pl.pallas_call(
        body, name="qk_norm_bwd", grid=(T // tm,),
        in_specs=[row(QKV_DIM), row(ATTN_DIM), row(KV_DIM), row(KV_DIM), full((1, ATTN_DIM)), full((1, KV_DIM)),
                  full((HEAD_GROUP, HEAD_GROUP))],
        out_specs=[row(QKV_DIM), full((1, QKV_DIM)), full((1, ATTN_DIM)), full((1, KV_DIM))],
        out_shape=[jax.ShapeDtypeStruct((T, QKV_DIM), BF), jax.ShapeDtypeStruct((1, QKV_DIM), F32),
                   jax.ShapeDtypeStruct((1, ATTN_DIM), F32), jax.ShapeDtypeStruct((1, KV_DIM), F32)],
        compiler_params=_params(1),
    )(qkv, dqn, dkn, dv, qg_t, kg_t, _head_ones())


ROWS = 128
COLS = 128


SUBLANES = 8
FIRST_TAP = HALO - (CONV_W - 1)


def _glu(a, g):
    return a.astype(F32) * jax.nn.sigmoid(g.astype(F32))


def _shifted(xe, s):
    return xe if s == 0 else pltpu.roll(xe, ROWS + HALO - s, axis=0)


def _conv_fwd(u, dw_pad, dw_b, ln_g, ln_b, tm=512):
    T = u.shape[0]
    Dm = D_MODEL
    hpt = tm // HALO

    def body(ac_ref, gc_ref, ap_ref, gp_ref, w_ref, wb_ref, lg_ref, lb_ref, cv_ref, s_ref, ext):
        i = pl.program_id(0)
        ext[pl.ds(0, HALO), :] = jnp.where(i > 0, _glu(ap_ref[...], gp_ref[...]), 0.0)
        ext[pl.ds(HALO, tm), :] = _glu(ac_ref[...], gc_ref[...])

        def rows(r, carry):
            r0 = pl.multiple_of(r * ROWS, ROWS)
            for c in range(Dm // COLS):
                cs = pl.ds(c * COLS, COLS)
                xe = ext[pl.ds(r0, ROWS + HALO), cs]
                acc = jnp.zeros((ROWS, COLS), F32)
                for s in range(SUBLANES):
                    xs = _shifted(xe, s)
                    for j in range(CONV_W):
                        off = FIRST_TAP + j
                        if off % SUBLANES == s:
                            acc = acc + xs[off - s:off - s + ROWS, :] * w_ref[pl.ds(j, 1), cs]
                cv_ref[pl.ds(r0, ROWS), cs] = acc + wb_ref[:, cs]
            return carry

        lax.fori_loop(0, tm // ROWS, rows, 0)
        cv = cv_ref[...]
        xc = cv - jnp.mean(cv, axis=-1, keepdims=True)
        y = xc * lax.rsqrt(jnp.mean(xc * xc, axis=-1, keepdims=True) + NORM_EPS) * lg_ref[...] + lb_ref[...]
        s_ref[...] = (y * jax.nn.sigmoid(y)).astype(BF)

    full = lambda shape: pl.BlockSpec(shape, lambda i: (0, 0))
    return pl.pallas_call(
        body, name="conv_fwd", grid=(T // tm,),
        in_specs=[pl.BlockSpec((tm, Dm), lambda i: (i, 0)), pl.BlockSpec((tm, Dm), lambda i: (i, 1)),
                  pl.BlockSpec((HALO, Dm), lambda i: (jnp.maximum(i * hpt - 1, 0), 0)),
                  pl.BlockSpec((HALO, Dm), lambda i: (jnp.maximum(i * hpt - 1, 0), 1)),
                  full((HALO, Dm)), full((1, Dm)), full((1, Dm)), full((1, Dm))],
        out_specs=[pl.BlockSpec((tm, Dm), lambda i: (i, 0)), pl.BlockSpec((tm, Dm), lambda i: (i, 0))],
        out_shape=[jax.ShapeDtypeStruct((T, Dm), F32), jax.ShapeDtypeStruct((T, Dm), BF)],
        scratch_shapes=[pltpu.VMEM((tm + HALO, Dm), F32)],
        compiler_params=_params(1),
    )(u, u, u, u, dw_pad, dw_b, ln_g, ln_b)


def _ln_silu_bwd_ep(ds, cv, lg, lb):
    xc = cv - jnp.mean(cv, axis=-1, keepdims=True)
    rstd = lax.rsqrt(jnp.mean(xc * xc, axis=-1, keepdims=True) + NORM_EPS)
    xh = xc * rstd
    y = xh * lg + lb
    sg = jax.nn.sigmoid(y)
    dy = ds * (sg * (1.0 + y * (1.0 - sg)))
    dxh = dy * lg
    dcv = rstd * (dxh - jnp.mean(dxh, axis=-1, keepdims=True) - xh * jnp.mean(dxh * xh, axis=-1, keepdims=True))
    return (dcv, jnp.sum(dy * xh, axis=0, keepdims=True), jnp.sum(dy, axis=0, keepdims=True),
            jnp.sum(dcv, axis=0, keepdims=True))


def _conv_bwd(u, dcv, dw_pad, tm=512):
    T = u.shape[0]
    Dm = D_MODEL
    hpt = tm // HALO
    last = T // HALO - 1
    nt = T // tm

    def body(ac_ref, gc_ref, ap_ref, gp_ref, dc_ref, dn_ref, w_ref, du_ref, db_ref, dw_ref, ext_g, ext_d):
        i = pl.program_id(0)
        ext_g[pl.ds(0, HALO), :] = jnp.where(i > 0, _glu(ap_ref[...], gp_ref[...]), 0.0)
        ext_g[pl.ds(HALO, tm), :] = _glu(ac_ref[...], gc_ref[...])
        ext_d[pl.ds(0, tm), :] = dc_ref[...]
        ext_d[pl.ds(tm, HALO), :] = jnp.where(i < nt - 1, dn_ref[...], 0.0)

        @pl.when(i == 0)
        def _():
            db_ref[...] = jnp.zeros_like(db_ref)
            dw_ref[...] = jnp.zeros_like(dw_ref)

        def rows(r, carry):
            r0 = pl.multiple_of(r * ROWS, ROWS)
            rs = pl.ds(r0, ROWS)
            for c in range(Dm // COLS):
                cs = pl.ds(c * COLS, COLS)
                cs2 = pl.ds(Dm + c * COLS, COLS)
                de = ext_d[pl.ds(r0, ROWS + HALO), cs]
                ge = ext_g[pl.ds(r0, ROWS + HALO), cs]
                dcur = de[0:ROWS, :]
                acc = jnp.zeros((ROWS, COLS), F32)
                for s in range(SUBLANES):
                    ds_, gs_ = _shifted(de, s), _shifted(ge, s)
                    for j in range(CONV_W):
                        off = CONV_W - 1 - j
                        if off % SUBLANES == s:
                            acc = acc + ds_[off - s:off - s + ROWS, :] * w_ref[pl.ds(j, 1), cs]
                        goff = FIRST_TAP + j
                        if goff % SUBLANES == s:
                            prod = dcur * gs_[goff - s:goff - s + ROWS, :]
                            dw_ref[j, :, cs] += jnp.sum(prod.reshape(ROWS // SUBLANES, SUBLANES, COLS), axis=0)
                a = ac_ref[rs, cs].astype(F32)
                sg = jax.nn.sigmoid(gc_ref[rs, cs].astype(F32))
                da = acc * sg
                dg = acc * a * sg * (1.0 - sg)
                du_ref[rs, cs] = da.astype(BF)
                du_ref[rs, cs2] = dg.astype(BF)
                db_ref[:, cs] += jnp.sum(da, axis=0, keepdims=True)
                db_ref[:, cs2] += jnp.sum(dg, axis=0, keepdims=True)
            return carry

        lax.fori_loop(0, tm // ROWS, rows, 0)

    return pl.pallas_call(
        body, name="conv_bwd", grid=(nt,),
        in_specs=[pl.BlockSpec((tm, Dm), lambda i: (i, 0)), pl.BlockSpec((tm, Dm), lambda i: (i, 1)),
                  pl.BlockSpec((HALO, Dm), lambda i: (jnp.maximum(i * hpt - 1, 0), 0)),
                  pl.BlockSpec((HALO, Dm), lambda i: (jnp.maximum(i * hpt - 1, 0), 1)),
                  pl.BlockSpec((tm, Dm), lambda i: (i, 0)),
                  pl.BlockSpec((HALO, Dm), lambda i: (jnp.minimum((i + 1) * hpt, last), 0)),
                  pl.BlockSpec((HALO, Dm), lambda i: (0, 0))],
        out_specs=[pl.BlockSpec((tm, 2 * Dm), lambda i: (i, 0)), pl.BlockSpec((1, 2 * Dm), lambda i: (0, 0)),
                   pl.BlockSpec((HALO, 8, Dm), lambda i: (0, 0, 0))],
        out_shape=[jax.ShapeDtypeStruct((T, 2 * Dm), BF), jax.ShapeDtypeStruct((1, 2 * Dm), F32),
                   jax.ShapeDtypeStruct((HALO, 8, Dm), F32)],
        scratch_shapes=[pltpu.VMEM((tm + HALO, Dm), F32), pltpu.VMEM((tm + HALO, Dm), F32)],
        compiler_params=_params(1),
    )(u, u, u, u, dcv, dcv, dw_pad)


def _bucket_table():
    q_loc = np.arange(BLOCK)[:, None]
    k_loc = np.arange(2 * BLOCK)[None, :]
    dist = q_loc + BLOCK - k_loc
    n = np.maximum(dist, 0)
    max_exact = REL_BUCKETS // 2
    large = max_exact + (np.log(np.maximum(n, 1).astype(np.float32) / max_exact)
                         / math.log(REL_MAX_DIST / max_exact) * (REL_BUCKETS - max_exact)).astype(np.int32)
    large = np.minimum(large, REL_BUCKETS - 1)
    bucket = np.where(n < max_exact, n, large).astype(np.int32)
    band = np.where((dist >= 0) & (dist < BLOCK), bucket, -1)
    folded = np.where(np.arange(BLOCK)[None, :] > q_loc, band[:, :BLOCK], band[:, BLOCK:])
    assert (folded >= 0).all() and ((band[:, :BLOCK] >= 0) != (band[:, BLOCK:] >= 0)).all()
    return jnp.asarray(folded.astype(np.int32))


def _prev_mask():
    row = lax.broadcasted_iota(jnp.int32, (BLOCK, BLOCK), 0)
    col = lax.broadcasted_iota(jnp.int32, (BLOCK, BLOCK), 1)
    return col > row


def _fold(band, prev_mask):
    return jnp.where(prev_mask, band[:, :BLOCK], band[:, BLOCK:])


def _unfold(ref, g, rows, folded, prev_mask):
    ref[g, rows, pl.ds(0, BLOCK)] = jnp.where(prev_mask, folded, 0.0).astype(ref.dtype)
    ref[g, rows, pl.ds(BLOCK, BLOCK)] = jnp.where(prev_mask, 0.0, folded).astype(ref.dtype)


def _bias_table(rel_bias_t, bucket):
    def body(rb_ref, bk_ref, o_ref):
        bk = bk_ref[...]
        prev_mask = _prev_mask()
        for h in range(N_HEADS):
            acc = jnp.zeros((BLOCK, BLOCK), F32)
            for b in range(REL_BUCKETS):
                acc = jnp.where(bk == b, rb_ref[h, b], acc)
            o_ref[0, h] = acc
            o_ref[1, h] = jnp.where(prev_mask, NEG_INF, acc)

    return pl.pallas_call(
        body, name="bias_table", out_shape=jax.ShapeDtypeStruct((2, N_HEADS, BLOCK, BLOCK), F32),
        in_specs=[pl.BlockSpec(memory_space=pltpu.SMEM), pl.BlockSpec(memory_space=pltpu.VMEM)],
        out_specs=pl.BlockSpec(memory_space=pltpu.VMEM),
    )(rel_bias_t, bucket)


def _bias_grad(dbias, bucket):
    def body(db_ref, bk_ref, o_ref):
        bk = bk_ref[...]
        for b in range(REL_BUCKETS):
            sel = bk == b
            for h in range(N_HEADS):
                o_ref[h, b] = jnp.sum(jnp.where(sel, db_ref[h], 0.0))

    return pl.pallas_call(
        body, name="bias_grad", out_shape=jax.ShapeDtypeStruct((N_HEADS, REL_BUCKETS), F32),
        in_specs=[pl.BlockSpec(memory_space=pltpu.VMEM), pl.BlockSpec(memory_space=pltpu.VMEM)],
        out_specs=pl.BlockSpec(memory_space=pltpu.SMEM),
    )(dbias, bucket)


GROUP_ROWS = GROUP * BLOCK
BIAS_SPEC = pl.BlockSpec((2, N_HEADS, BLOCK, BLOCK), lambda n: (0, 0, 0, 0))


def _head_probs(qk, bias_h, sink, prev_mask):
    s = _fold(qk, prev_mask) + bias_h
    m = jnp.maximum(jnp.max(s, axis=-1, keepdims=True), sink)
    p = jnp.exp(s - m)
    ps = jnp.exp(sink - m)
    inv = 1.0 / (jnp.sum(p, axis=-1, keepdims=True) + ps)
    return p * inv, ps * inv


def _band(prev_ref, cur_ref, g):
    hs = pl.ds(g * HEAD_DIM, HEAD_DIM)
    return jnp.concatenate([prev_ref[:, hs], cur_ref[:, hs]], axis=0)


def _stack_heads(ref, g):
    return jnp.concatenate([ref[:, pl.ds((g * GROUP + hh) * HEAD_DIM, HEAD_DIM)] for hh in range(GROUP)], axis=0)


def _unstack_heads(ref, g, stacked, dtype):
    for hh in range(GROUP):
        ref[:, pl.ds((g * GROUP + hh) * HEAD_DIM, HEAD_DIM)] = stacked[hh * BLOCK:(hh + 1) * BLOCK, :].astype(dtype)


def _head_rows(hh):
    return pl.ds(hh * BLOCK, BLOCK)


def _attn_fwd(qn, kn, vv, bias, sinks):
    T = qn.shape[0]
    nb = T // BLOCK

    def body(sk_ref, q_ref, kc_ref, kp_ref, vc_ref, vp_ref, b_ref, o_ref, qk_buf, p_buf):
        prev_mask = _prev_mask()
        first = jnp.logical_and(pl.program_id(0) == 0, prev_mask)
        for g in range(N_KV):
            qk_buf[g] = _dot(_stack_heads(q_ref, g), _band(kp_ref, kc_ref, g), 1, 1)
        for g in range(N_KV):
            for hh in range(GROUP):
                h = g * GROUP + hh
                bias_h = jnp.where(first, NEG_INF, b_ref[0, h])
                pn, _ = _head_probs(qk_buf[g, _head_rows(hh), :], bias_h, sk_ref[h], prev_mask)
                _unfold(p_buf, g, _head_rows(hh), pn, prev_mask)
        for g in range(N_KV):
            _unstack_heads(o_ref, g, _dot(p_buf[g], _band(vp_ref, vc_ref, g), 1, 0), BF)

    cur = lambda n: (n, 0)
    prev = lambda n: (jnp.maximum(n - 1, 0), 0)
    return pl.pallas_call(
        body, name="attn_fwd", grid=(nb,),
        in_specs=[pl.BlockSpec(memory_space=pltpu.SMEM), pl.BlockSpec((BLOCK, ATTN_DIM), cur),
                  pl.BlockSpec((BLOCK, KV_DIM), cur), pl.BlockSpec((BLOCK, KV_DIM), prev),
                  pl.BlockSpec((BLOCK, KV_DIM), cur), pl.BlockSpec((BLOCK, KV_DIM), prev), BIAS_SPEC],
        out_specs=pl.BlockSpec((BLOCK, ATTN_DIM), cur), out_shape=jax.ShapeDtypeStruct((T, ATTN_DIM), BF),
        scratch_shapes=[pltpu.VMEM((N_KV, GROUP_ROWS, 2 * BLOCK), F32), pltpu.VMEM((N_KV, GROUP_ROWS, 2 * BLOCK), BF)],
        compiler_params=_params(1),
    )(sinks, qn, kn, kn, vv, vv, bias)


def _attn_bwd(qn, kn, vv, bias, sinks, do):
    T = qn.shape[0]
    nb = T // BLOCK
    scale = 1.0 / math.sqrt(HEAD_DIM)

    def body(sk_ref, q_ref, kc_ref, kp_ref, vc_ref, vp_ref, b_ref, do_ref,
             dq_ref, dk_ref, dv_ref, db_ref, dsk_ref, dk_full, dv_full, dk_carry, dv_carry, qk_buf, dp_buf, p_buf, ds_buf):
        n = pl.program_id(0)

        @pl.when(n == 0)
        def _():
            db_ref[...] = jnp.zeros_like(db_ref)
            dk_carry[...] = jnp.zeros_like(dk_carry)
            dv_carry[...] = jnp.zeros_like(dv_carry)
            for h in range(N_HEADS):
                dsk_ref[h] = 0.0

        @pl.when(n < nb)
        def _():
            prev_mask = _prev_mask()
            first = jnp.logical_and(n == 0, prev_mask)
            ks = [_band(kp_ref, kc_ref, g) for g in range(N_KV)]
            qs = [_stack_heads(q_ref, g) for g in range(N_KV)]
            douts = [_stack_heads(do_ref, g) for g in range(N_KV)]
            for g in range(N_KV):
                qk_buf[g] = _dot(qs[g], ks[g], 1, 1)
                dp_buf[g] = _dot(douts[g], _band(vp_ref, vc_ref, g), 1, 1)
            for g in range(N_KV):
                for hh in range(GROUP):
                    h = g * GROUP + hh
                    rows = _head_rows(hh)
                    bias_h = jnp.where(first, NEG_INF, b_ref[0, h])
                    pn, psink = _head_probs(qk_buf[g, rows, :], bias_h, sk_ref[h], prev_mask)
                    dp = _fold(dp_buf[g, rows, :], prev_mask)
                    delta = jnp.sum(pn * dp, axis=-1, keepdims=True)
                    ds = pn * (dp - delta)
                    dsk_ref[h] += -jnp.sum(psink * delta)
                    db_ref[h] += ds
                    _unfold(ds_buf, g, rows, ds, prev_mask)
                    _unfold(p_buf, g, rows, pn, prev_mask)
            for g in range(N_KV):
                dsb = ds_buf[g]
                _unstack_heads(dq_ref, g, _dot(dsb, ks[g], 1, 0) * scale, F32)
                gs = pl.ds(g * HEAD_DIM, HEAD_DIM)
                dk_full[:, gs] = _dot(dsb, qs[g], 0, 0)
                dv_full[:, gs] = _dot(p_buf[g], douts[g], 0, 0)

        @pl.when(n == nb)
        def _():
            dk_full[...] = jnp.zeros_like(dk_full)
            dv_full[...] = jnp.zeros_like(dv_full)

        dk_ref[...] = dk_carry[...] + dk_full[pl.ds(0, BLOCK), :]
        dv_ref[...] = dv_carry[...] + dv_full[pl.ds(0, BLOCK), :]
        dk_carry[...] = dk_full[pl.ds(BLOCK, BLOCK), :]
        dv_carry[...] = dv_full[pl.ds(BLOCK, BLOCK), :]

    cur = lambda n: (jnp.minimum(n, nb - 1), 0)
    prev = lambda n: (jnp.maximum(jnp.minimum(n, nb - 1) - 1, 0), 0)
    out_kv = lambda n: (jnp.maximum(n - 1, 0), 0)
    return pl.pallas_call(
        body, name="attn_bwd", grid=(nb + 1,),
        in_specs=[pl.BlockSpec(memory_space=pltpu.SMEM), pl.BlockSpec((BLOCK, ATTN_DIM), cur),
                  pl.BlockSpec((BLOCK, KV_DIM), cur), pl.BlockSpec((BLOCK, KV_DIM), prev),
                  pl.BlockSpec((BLOCK, KV_DIM), cur), pl.BlockSpec((BLOCK, KV_DIM), prev), BIAS_SPEC,
                  pl.BlockSpec((BLOCK, ATTN_DIM), cur)],
        out_specs=[pl.BlockSpec((BLOCK, ATTN_DIM), cur), pl.BlockSpec((BLOCK, KV_DIM), out_kv),
                   pl.BlockSpec((BLOCK, KV_DIM), out_kv),
                   pl.BlockSpec((N_HEADS, BLOCK, BLOCK), lambda n: (0, 0, 0)),
                   pl.BlockSpec(memory_space=pltpu.SMEM)],
        out_shape=[jax.ShapeDtypeStruct((T, ATTN_DIM), F32), jax.ShapeDtypeStruct((T, KV_DIM), F32),
                   jax.ShapeDtypeStruct((T, KV_DIM), F32),
                   jax.ShapeDtypeStruct((N_HEADS, BLOCK, BLOCK), F32), jax.ShapeDtypeStruct((N_HEADS,), F32)],
        scratch_shapes=[pltpu.VMEM((2 * BLOCK, KV_DIM), F32), pltpu.VMEM((2 * BLOCK, KV_DIM), F32),
                        pltpu.VMEM((BLOCK, KV_DIM), F32), pltpu.VMEM((BLOCK, KV_DIM), F32),
                        pltpu.VMEM((N_KV, GROUP_ROWS, 2 * BLOCK), F32), pltpu.VMEM((N_KV, GROUP_ROWS, 2 * BLOCK), F32),
                        pltpu.VMEM((N_KV, GROUP_ROWS, 2 * BLOCK), BF), pltpu.VMEM((N_KV, GROUP_ROWS, 2 * BLOCK), BF)],
        compiler_params=_params(1),
    )(sinks, qn, kn, kn, vv, vv, bias, do)


def _coords():
    return lax.axis_index("x"), lax.axis_index("y"), lax.axis_index("c")


def _sum8(name, blocks):
    def body(b_ref, o_ref):
        tot = b_ref[0]
        for d in range(1, 8):
            tot = tot + b_ref[d]
        o_ref[...] = tot

    return pl.pallas_call(body, name=name, out_shape=jax.ShapeDtypeStruct(blocks.shape[1:], F32))(blocks)


HBM_SPEC = pl.BlockSpec(memory_space=pltpu.HBM)
SEM_SPEC = pl.BlockSpec(memory_space=pltpu.SEMAPHORE)
ANY_SPEC = pl.BlockSpec(memory_space=pl.ANY)
DATAFLOW = pltpu.SideEffectType.DATAFLOW_SIDE_EFFECTING


OTHER_CHIPS = (4, 2, 6)
ALL_OTHERS = (1, 2, 3, 4, 5, 6, 7)


def _slot(x, y, c, peers):
    return 2 * x + y if peers is OTHER_CHIPS else 4 * x + 2 * y + c


def _slot_copy(land, sems, idx, x, y, c, k, peers, arriving):
    send_sems, recv_sems = sems
    px, py, pc = x ^ (k >> 2), y ^ ((k >> 1) & 1), c ^ (k & 1)
    mine = _slot(x, y, c, peers)
    dst = _slot(px, py, pc, peers) if arriving else mine
    return pltpu.make_async_remote_copy(src_ref=land.at[mine], dst_ref=land.at[dst], send_sem=send_sems.at[idx],
                                        recv_sem=recv_sems.at[idx], device_id=(px, py, pc), device_id_type=MESH)


def _gather_start(name, stacks, groups, peers, after):
    n = len(stacks)
    ng = len(groups)
    np_ = len(peers)
    after = tuple(after)

    def body(*refs):
        lands = refs[:n]
        first = n + len(after)
        sems = [(refs[first + 2 * g], refs[first + 2 * g + 1]) for g in range(ng)]
        token = refs[-1]
        x, y, c = _coords()
        for g, members in enumerate(groups):
            for i, t in enumerate(members):
                for j, k in enumerate(peers):
                    _slot_copy(lands[t], sems[g], np_ * i + j, x, y, c, k, peers, arriving=False).start()
        token[...] = jnp.zeros_like(token)

    out_shape = []
    for members in groups:
        out_shape += [pltpu.SemaphoreType.DMA((np_ * len(members),))] * 2
    out_shape += [pltpu.HBM(w.shape, w.dtype) for w in stacks]
    out_shape.append(jax.ShapeDtypeStruct((8, 128), F32))
    res = pl.pallas_call(
        body, name=name, out_shape=out_shape, in_specs=[HBM_SPEC] * n + [ANY_SPEC] * len(after),
        out_specs=[SEM_SPEC] * (2 * ng) + [HBM_SPEC] * n + [pl.BlockSpec(memory_space=pltpu.VMEM)],
        input_output_aliases={t: 2 * ng + t for t in range(n)},
        compiler_params=pltpu.CompilerParams(has_side_effects=DATAFLOW),
    )(*[pltpu.with_memory_space_constraint(w, pltpu.HBM) for w in stacks], *after)
    sems = [(res[2 * g], res[2 * g + 1]) for g in range(ng)]
    return sems, list(res[2 * ng:2 * ng + n]), res[-1]


def _gather_wait(name, stacks, sems, peers, after):
    n = len(stacks)
    after = tuple(after)

    def body(*refs):
        lands = refs[:n]
        group_sems = (refs[n], refs[n + 1])
        x, y, c = _coords()
        for i in range(n):
            for j, k in enumerate(peers):
                cp = _slot_copy(lands[i], group_sems, len(peers) * i + j, x, y, c, k, peers, arriving=True)
                cp.wait_send()
                cp.wait_recv()

    return pl.pallas_call(
        body, name=name, out_shape=[pltpu.HBM(w.shape, w.dtype) for w in stacks],
        in_specs=[HBM_SPEC] * n + [SEM_SPEC, SEM_SPEC] + [ANY_SPEC] * len(after), out_specs=[HBM_SPEC] * n,
        input_output_aliases={t: t for t in range(n)},
        compiler_params=pltpu.CompilerParams(has_side_effects=DATAFLOW),
    )(*stacks, sems[0], sems[1], *after)


N_PEERS = 7


def _peer(x, y, c, k):
    return x ^ (k >> 2), y ^ ((k >> 1) & 1), c ^ (k & 1)


def _reduce_copy(grad, land, sems, idx, x, y, c, k):
    px, py, pc = _peer(x, y, c, k)
    rh = grad.shape[1] // 2
    return pltpu.make_async_remote_copy(src_ref=grad.at[2 * px + py, pl.ds(pc * rh, rh), :], dst_ref=land.at[k - 1],
                                        send_sem=sems[0].at[idx], recv_sem=sems[1].at[idx], device_id=(px, py, pc),
                                        device_id_type=MESH)


def _reduce_start(name, grads):
    n = len(grads)

    def body(*refs):
        src, lands, sems, token = refs[:n], refs[n:2 * n], (refs[2 * n], refs[2 * n + 1]), refs[-1]
        x, y, c = _coords()
        for t in range(n):
            for k in range(1, N_PEERS + 1):
                _reduce_copy(src[t], lands[t], sems, N_PEERS * t + k - 1, x, y, c, k).start()
        token[...] = jnp.zeros_like(token)

    lands = [lax.empty((N_PEERS, g.shape[1] // 2, g.shape[2]), g.dtype) for g in grads]
    out_shape = [pltpu.SemaphoreType.DMA((N_PEERS * n,))] * 2
    out_shape += [pltpu.HBM(a.shape, a.dtype) for a in list(grads) + lands]
    out_shape.append(jax.ShapeDtypeStruct((8, 128), F32))
    res = pl.pallas_call(
        body, name=name, out_shape=out_shape, in_specs=[HBM_SPEC] * (2 * n),
        out_specs=[SEM_SPEC] * 2 + [HBM_SPEC] * (2 * n) + [pl.BlockSpec(memory_space=pltpu.VMEM)],
        input_output_aliases={t: 2 + t for t in range(2 * n)},
        compiler_params=pltpu.CompilerParams(has_side_effects=DATAFLOW),
    )(*[pltpu.with_memory_space_constraint(a, pltpu.HBM) for a in list(grads) + lands])
    return (res[0], res[1]), list(res[2:2 + n]), list(res[2 + n:2 + 2 * n]), res[-1]


def _reduce_wait(name, grads, lands, sems, after):
    n = len(grads)
    after = tuple(after)

    def body(*refs):
        src, dst, group_sems = refs[:n], refs[n:2 * n], (refs[2 * n], refs[2 * n + 1])
        x, y, c = _coords()
        for t in range(n):
            for k in range(1, N_PEERS + 1):
                cp = _reduce_copy(src[t], dst[t], group_sems, N_PEERS * t + k - 1, x, y, c, k)
                cp.wait_send()
                cp.wait_recv()

    res = pl.pallas_call(
        body, name=name, out_shape=[pltpu.HBM(a.shape, a.dtype) for a in list(grads) + list(lands)],
        in_specs=[HBM_SPEC] * (2 * n) + [SEM_SPEC, SEM_SPEC] + [ANY_SPEC] * len(after), out_specs=[HBM_SPEC] * (2 * n),
        input_output_aliases={t: t for t in range(2 * n)},
        compiler_params=pltpu.CompilerParams(has_side_effects=DATAFLOW),
    )(*grads, *lands, sems[0], sems[1], *after)
    return list(res[:n]), list(res[n:])


def _join_copy(half, land, sems, idx, x, y, c):
    return pltpu.make_async_remote_copy(src_ref=half, dst_ref=land, send_sem=sems[0].at[idx], recv_sem=sems[1].at[idx],
                                        device_id=(x, y, 1 - c), device_id_type=MESH)


def _join_start(name, halves):
    n = len(halves)

    def body(*refs):
        src, lands, sems, token = refs[:n], refs[n:2 * n], (refs[2 * n], refs[2 * n + 1]), refs[-1]
        x, y, c = _coords()
        for t in range(n):
            _join_copy(src[t], lands[t], sems, t, x, y, c).start()
        token[...] = jnp.zeros_like(token)

    lands = [lax.empty(h.shape, h.dtype) for h in halves]
    out_shape = [pltpu.SemaphoreType.DMA((n,))] * 2
    out_shape += [pltpu.HBM(a.shape, a.dtype) for a in list(halves) + lands]
    out_shape.append(jax.ShapeDtypeStruct((8, 128), F32))
    res = pl.pallas_call(
        body, name=name, out_shape=out_shape, in_specs=[HBM_SPEC] * (2 * n),
        out_specs=[SEM_SPEC] * 2 + [HBM_SPEC] * (2 * n) + [pl.BlockSpec(memory_space=pltpu.VMEM)],
        input_output_aliases={t: 2 + t for t in range(2 * n)},
        compiler_params=pltpu.CompilerParams(has_side_effects=DATAFLOW),
    )(*[pltpu.with_memory_space_constraint(a, pltpu.HBM) for a in list(halves) + lands])
    return (res[0], res[1]), list(res[2:2 + n]), list(res[2 + n:2 + 2 * n]), res[-1]


def _join_wait(name, halves, lands, sems, after):
    n = len(halves)
    after = tuple(after)

    def body(*refs):
        src, dst, group_sems = refs[:n], refs[n:2 * n], (refs[2 * n], refs[2 * n + 1])
        x, y, c = _coords()
        for t in range(n):
            cp = _join_copy(src[t], dst[t], group_sems, t, x, y, c)
            cp.wait_send()
            cp.wait_recv()

    res = pl.pallas_call(
        body, name=name, out_shape=[pltpu.HBM(a.shape, a.dtype) for a in list(halves) + list(lands)],
        in_specs=[HBM_SPEC] * (2 * n) + [SEM_SPEC, SEM_SPEC] + [ANY_SPEC] * len(after), out_specs=[HBM_SPEC] * (2 * n),
        input_output_aliases={t: t for t in range(2 * n)},
        compiler_params=pltpu.CompilerParams(has_side_effects=DATAFLOW),
    )(*halves, *lands, sems[0], sems[1], *after)
    return list(res[:n]), list(res[n:])


def _join_halves(name, halves, deps=()):
    n = len(halves)

    def body(*refs):
        src, dst = refs[:n], refs[n + len(deps):2 * n + len(deps)]
        send_sems, recv_sems = refs[-2:]
        x, y, c = _coords()
        cps = []
        for t in range(n):
            cp = pltpu.make_async_remote_copy(src_ref=src[t], dst_ref=dst[t], send_sem=send_sems.at[t],
                                              recv_sem=recv_sems.at[t], device_id=(x, y, 1 - c), device_id_type=MESH)
            cp.start()
            cps.append(cp)
        for cp in cps:
            cp.wait()

    anyspec = pl.BlockSpec(memory_space=pl.ANY)
    return pl.pallas_call(
        body, name=name, out_shape=[jax.ShapeDtypeStruct(h.shape, h.dtype) for h in halves],
        in_specs=[anyspec] * (n + len(deps)), out_specs=[anyspec] * n,
        scratch_shapes=[pltpu.SemaphoreType.DMA((n,)), pltpu.SemaphoreType.DMA((n,))],
    )(*halves, *deps)


BF16_ROWS = 16
MAX_ROW_BLOCK = 512


def _row_block(rows):
    for rb in range(min(rows, MAX_ROW_BLOCK), 0, -1):
        if rows % rb == 0 and rb % BF16_ROWS == 0:
            return rb
    raise ValueError(rows)


def _sum_devices(name, grad, land, place):
    S, R, C = grad.shape
    rh = R // 2
    rb = _row_block(rh)
    nbh = rh // rb

    def body(place_ref, g_ref, l_ref, o_ref):
        tot = g_ref[...].astype(F32)
        for k in range(N_PEERS):
            tot = tot + l_ref[k].astype(F32)
        o_ref[...] = tot

    return pl.pallas_call(
        body, name=name,
        grid_spec=pltpu.PrefetchScalarGridSpec(
            num_scalar_prefetch=1, grid=(nbh,),
            in_specs=[pl.BlockSpec((None, rb, C), lambda r, place: (place[0], place[1] * nbh + r, 0)),
                      pl.BlockSpec((N_PEERS, rb, C), lambda r, place: (0, r, 0))],
            out_specs=pl.BlockSpec((rb, C), lambda r, place: (r, 0))),
        out_shape=jax.ShapeDtypeStruct((rh, C), F32), compiler_params=_params(1),
    )(place, grad, land)


def _adamw_math(w, g, m, v):
    m2 = ADAM_B1 * m + (1.0 - ADAM_B1) * g
    v2 = ADAM_B2 * v + (1.0 - ADAM_B2) * (g * g)
    m_hat = m2 / (1.0 - ADAM_B1 ** ADAM_STEP)
    v_hat = v2 / (1.0 - ADAM_B2 ** ADAM_STEP)
    delta = -ADAM_LR * (m_hat / (jnp.sqrt(v_hat) + ADAM_EPS) + ADAM_WD * w)
    return delta, m2, v2


def _adamw(name, w, m, v, gs):
    L, R, C = w.shape
    Rh = R // 2
    rb = _row_block(Rh)
    nbh = Rh // rb
    assert len(gs) == L

    def body(core_ref, w_ref, m_ref, v_ref, *rest):
        g_refs, (go_ref, d_ref, m2_ref, v2_ref) = rest[:2 * L], rest[2 * L:]
        layer, half = pl.program_id(0), pl.program_id(1)
        mine = half == core_ref[0]
        g = jnp.where(mine, g_refs[0][...], g_refs[1][...])
        for t in range(1, L):
            g = jnp.where(layer == t, jnp.where(mine, g_refs[2 * t][...], g_refs[2 * t + 1][...]), g)
        delta, m2, v2 = _adamw_math(w_ref[...], g, m_ref[...], v_ref[...])
        go_ref[...] = g
        d_ref[...] = delta
        m2_ref[...] = m2
        v2_ref[...] = v2

    wspec = pl.BlockSpec((None, rb, C), lambda l, h, r, core: (l, h * nbh + r, 0))
    gspec = pl.BlockSpec((rb, C), lambda l, h, r, core: (r, 0))
    return pl.pallas_call(
        body, name=name,
        grid_spec=pltpu.PrefetchScalarGridSpec(num_scalar_prefetch=1, grid=(L, 2, nbh),
                                               in_specs=[wspec] * 3 + [gspec] * (2 * L), out_specs=[wspec] * 4),
        out_shape=[jax.ShapeDtypeStruct((L, R, C), F32)] * 4, compiler_params=_params(3),
    )(lax.axis_index("c").astype(jnp.int32).reshape(1), w, m, v, *[g for pair in gs for g in pair])


def _adamw_small(ws, gs, ms, vs):
    n = len(ws)

    def body(*refs):
        w_refs, g_refs, m_refs, v_refs = (refs[k * n:(k + 1) * n] for k in range(4))
        d_refs, m2_refs, v2_refs = (refs[(4 + k) * n:(5 + k) * n] for k in range(3))
        for t in range(n):
            delta, m2, v2 = _adamw_math(w_refs[t][...], g_refs[t][...], m_refs[t][...], v_refs[t][...])
            d_refs[t][...] = delta
            m2_refs[t][...] = m2
            v2_refs[t][...] = v2

    res = pl.pallas_call(body, name="adamw_small", out_shape=[jax.ShapeDtypeStruct(w.shape, F32) for w in ws] * 3)(
        *ws, *gs, *ms, *vs)
    return res[:n], res[n:2 * n], res[2 * n:]


def _packed_rows(shape):
    c = shape[-1]
    return (int(np.prod(shape)) // c) * -(-c // LANES)


def _pack(arrays):
    total = sum(_packed_rows(a.shape) for a in arrays)
    total += -total % 8
    buf, r0 = None, 0
    for a in arrays:
        a = a.astype(F32).reshape(-1, a.shape[-1])
        r, c = a.shape
        k = -(-c // LANES)
        a = jnp.pad(a, ((0, 0), (0, k * LANES - c))).reshape(r * k, LANES)
        a = jnp.pad(a, ((r0, total - r0 - r * k), (0, 0)))
        buf = a if buf is None else buf + a
        r0 += r * k
    return buf


def _unpack(buf, shapes):
    out, r0 = [], 0
    for shp in shapes:
        c = shp[-1]
        rows = _packed_rows(shp)
        out.append(buf[r0:r0 + rows].reshape(-1, -(-c // LANES) * LANES)[:, :c].reshape(shp))
        r0 += rows
    return out


def _rms(x, g):
    return x * lax.rsqrt(jnp.mean(x * x, axis=-1, keepdims=True) + NORM_EPS) * g


def _residual_norm_ep(acc, *rest):
    *bias, res, gain = rest
    x = acc + res + (bias[0] if bias else 0.0)
    return x, _rms(x, gain)


RESIDUAL_NORM_OUTS = (("tile", F32), ("tile", BF))


def _mlp_up(tag, h, w_up_sm):
    (up,) = _mm(f"mlp{tag}_up", h, w_up_sm, nt=False, b_sm=True, tm=2048, tn=1024, rows=256,
                ep_fn=lambda acc: (acc,), outs=(("tile", BF),))
    return up


RMS_BWD_OUTS = (("tile", F32), ("tile", BF), ("colsum", F32), ("colsum", F32))


def _mlp_bwd(tag, dy, dy_bf, x, g, up, w_up_sm, w_down):
    (dup,) = _mm(f"mlp{tag}_dup", dy_bf, w_down, nt=True, tm=2048, tn=1024, rows=256, ep_in=((up, "tile"),),
                 ep_fn=lambda acc, u: (acc * (2.0 * jnp.maximum(u.astype(F32), 0.0)),), outs=(("tile", BF),))
    dx, dx_bf, dg, dx_sum = _mm(f"mlp{tag}_dx", dup, w_up_sm, nt=True, b_sm=True, tm=512, tn=1024, rows=256,
                                ep_in=((x, "tile"), (g, "row"), (dy, "tile")), ep_fn=_rms_bwd_ep, outs=RMS_BWD_OUTS)
    return dx, dx_bf, dg, dx_sum, dup


class _Reduction:
    def __init__(self, tag, grads, place):
        self.tag, self.place = tag, place
        self.sems, self.grads, self.lands, self.token = _reduce_start(f"reduce_start_{tag}", grads)

    def finish(self, after):
        grads, lands = _reduce_wait(f"reduce_wait_{self.tag}", self.grads, self.lands, self.sems, after)
        return [_sum_devices(f"reduce_sum_{self.tag}{i}", g, l, self.place) for i, (g, l) in enumerate(zip(grads, lands))]


def kernel(x, conv_norm_g, conv_w_in, conv_b_in, conv_dw, conv_dw_b, conv_ln_g, conv_ln_b, conv_w_out, conv_b_out, attn_norm_g, w_qkv, b_qkv, q_norm_g, k_norm_g, sinks, w_o, b_o, rel_bias, mlp_norm_g, w_up, w_down, loss_target, m_conv_norm_g, m_conv_w_in, m_conv_b_in, m_conv_dw, m_conv_dw_b, m_conv_ln_g, m_conv_ln_b, m_conv_w_out, m_conv_b_out, m_attn_norm_g, m_w_qkv, m_b_qkv, m_q_norm_g, m_k_norm_g, m_sinks, m_w_o, m_b_o, m_rel_bias, m_mlp_norm_g, m_w_up, m_w_down, v_conv_norm_g, v_conv_w_in, v_conv_b_in, v_conv_dw, v_conv_dw_b, v_conv_ln_g, v_conv_ln_b, v_conv_w_out, v_conv_b_out, v_attn_norm_g, v_w_qkv, v_b_qkv, v_q_norm_g, v_k_norm_g, v_sinks, v_w_o, v_b_o, v_rel_bias, v_mlp_norm_g, v_w_up, v_w_down):
    Dm = D_MODEL
    x2d = x[0]
    tgt = loss_target[0]
    T = x2d.shape[0]
    shard = 2 * lax.axis_index("x") + lax.axis_index("y")

    me = 2 * shard + lax.axis_index("c")

    def own_slot(block, slots, index):
        return lax.dynamic_update_slice(lax.empty((slots,) + block.shape, block.dtype), block[None],
                                        (index,) + (0,) * block.ndim)

    (conv_in_sems,), (stack_in,), first_token = _gather_start(
        "gather_start_conv_in", [own_slot(conv_w_in[0].astype(BF), N_SHARD, shard)], ((0,),), OTHER_CHIPS, after=())
    sharded_small = [conv_dw[0], attn_norm_g, b_qkv, b_o]
    (small_sems,), (small_land,), small_token = _gather_start(
        "small_weights_start", [own_slot(_pack(sharded_small), 8, me)], ((0,),), ALL_OTHERS, after=(first_token,))

    big = [conv_w_out[0], jnp.swapaxes(w_qkv, 1, 2)[0], w_o[0], w_up[0], w_up[1], w_down[0], w_down[1]]
    stacks = [own_slot(w.astype(BF), N_SHARD, shard) for w in big]
    groups = ((0,), (3, 5), (1, 2), (4, 6))
    gather_sems, stacks, gather_token = _gather_start("gather_start", stacks, groups, OTHER_CHIPS, after=(small_token,))

    def gathered_group(g, name, after):
        return _gather_wait(name, [stacks[t] for t in groups[g]], gather_sems[g], OTHER_CHIPS, after)

    bucket = _bucket_table()
    bias = _bias_table(rel_bias.T, bucket)

    h0 = _rms_fwd("conv_norm", x2d, conv_norm_g, deps=(gather_token,))
    (w_in_sm,) = _gather_wait("gather_wait_conv_in", [stack_in], conv_in_sems, OTHER_CHIPS, (h0, bias))
    (u,) = _mm("conv_in", h0, w_in_sm, nt=False, b_sm=True, tm=2048, tn=512, rows=256, ep_in=((conv_b_in, "row"),),
               ep_fn=lambda acc, b: (acc + b,), outs=(("tile", BF),))
    (gathered,) = _gather_wait("small_weights_wait", [small_land], small_sems, ALL_OTHERS, (u,))
    chips = [_unpack(gathered[2 * s], [a.shape for a in sharded_small]) for s in range(N_SHARD)]
    dw_f, attn_norm_f, b_qkv_f, b_o_f = (jnp.concatenate([chips[s][t] for s in range(N_SHARD)], axis=-1)
                                         for t in range(len(sharded_small)))
    dw_pad = jnp.pad(dw_f, ((0, HALO - CONV_W), (0, 0)))
    cv, s_act = _conv_fwd(u, dw_pad, conv_dw_b, conv_ln_g, conv_ln_b)
    (g_out,) = gathered_group(0, "gather_wait_conv_out", (s_act,))
    w_out_f = g_out.reshape(Dm, Dm)
    x1, h1 = _mm("conv_out", s_act, w_out_f, nt=False, tm=1024, tn=1024, rows=256,
                 ep_in=((conv_b_out, "row"), (x2d, "tile"), (mlp_norm_g[0:1], "row")), ep_fn=_residual_norm_ep,
                 outs=RESIDUAL_NORM_OUTS)

    g_up0, g_down0 = gathered_group(1, "gather_wait_mlp0", (x1,))
    w_up_sm = [g_up0, None]
    w_down_f = [g_down0.reshape(D_FF, Dm), None]
    up0 = _mlp_up(0, h1, w_up_sm[0])
    x2, h2 = _mm("mlp0_down", up0, w_down_f[0], nt=False, tm=512, tn=1024, rows=256, a_fn=_relu2,
                 ep_in=((x1, "tile"), (attn_norm_f, "row")), ep_fn=_residual_norm_ep, outs=RESIDUAL_NORM_OUTS)

    g_qkv, g_o = gathered_group(2, "gather_wait_attn", (x2,))
    w_qkv_t = g_qkv.reshape(QKV_DIM, Dm)
    w_o_f = g_o.reshape(ATTN_DIM, Dm)
    qg_t = jnp.tile(q_norm_g, (1, N_HEADS))
    kg_t = jnp.tile(k_norm_g, (1, N_KV))

    def qkv_ep(acc, b, qg, kg, ones):
        proj = acc + b
        q, k, v = proj[:, :ATTN_DIM], proj[:, ATTN_DIM:ATTN_DIM + KV_DIM], proj[:, ATTN_DIM + KV_DIM:]
        return proj, _qk_normed(q, qg, ones, 1.0 / math.sqrt(HEAD_DIM)), _qk_normed(k, kg, ones, 1.0), v

    qkv, qn, kn, vv = _mm(
        "attn_qkv", h2, w_qkv_t, nt=True, tm=1024, tn=QKV_DIM, rows=256, ep_fn=qkv_ep,
        ep_in=((b_qkv_f, "row"), (qg_t, "whole"), (kg_t, "whole"), (_head_ones(), "whole")),
        outs=(("tile", F32), ("tile", BF, ATTN_DIM), ("tile", BF, KV_DIM), ("tile", BF, KV_DIM)))
    sinks1 = sinks[0]
    att = _attn_fwd(qn, kn, vv, bias, sinks1)
    x3, h3 = _mm("attn_out", att, w_o_f, nt=False, tm=1024, tn=1024, rows=256,
                 ep_in=((b_o_f, "row"), (x2, "tile"), (mlp_norm_g[1:2], "row")), ep_fn=_residual_norm_ep,
                 outs=RESIDUAL_NORM_OUTS)

    g_up1, g_down1 = gathered_group(3, "gather_wait_mlp1", (x3,))
    w_up_sm[1] = g_up1
    w_down_f[1] = g_down1.reshape(D_FF, Dm)
    up1 = _mlp_up(1, h3, w_up_sm[1])

    def loss_ep(acc, r, t):
        diff = acc + r - t
        dy = diff * (1.0 / Dm)
        return dy, dy, jnp.sum(diff * diff, axis=0, keepdims=True)

    dy, dy_bf, sq = _mm("mlp1_down_loss", up1, w_down_f[1], nt=False, tm=512, tn=1024, rows=256, a_fn=_relu2,
                        ep_in=((x3, "tile"), (tgt, "tile")), ep_fn=loss_ep,
                        outs=(("tile", F32), ("tile", BF), ("colsum", F32)))

    place = jnp.stack([shard, lax.axis_index("c")]).astype(jnp.int32)
    dx3, dx3_bf, dg_mlp1, db_o, dup1 = _mlp_bwd(1, dy, dy_bf, x3, mlp_norm_g[1:2], up1, w_up_sm[1], w_down_f[1])
    dw_down1 = _mm_tn("mlp1_dw_down", up1, dy_bf, tm=1024, tn=1024, tk=2048, a_fn=_relu2)
    dw_up1 = _mm_tn("mlp1_dw_up", h3, dup1, tm=1024, tn=1024, tk=2048, out_sm=N_SHARD)
    red_mlp1 = _Reduction("mlp1", [dw_up1, dw_down1.reshape(N_SHARD, D_FF // N_SHARD, Dm)], place)

    ident = lambda acc: (acc,)
    (datt,) = _mm("attn_dout", dx3_bf, w_o_f, nt=True, tm=1024, tn=1024, rows=256, ep_fn=ident, outs=(("tile", BF),),
                  deps=(red_mlp1.token,))
    dw_o = _mm_tn("attn_dw_o", att, dx3_bf, tm=1024, tn=1024, tk=2048)
    dqn, dkn, dvv, dbias, dsinks = _attn_bwd(qn, kn, vv, bias, sinks1, datt)
    drel = _bias_grad(dbias, bucket)
    dqkv, db_qkv, dqg_t, dkg_t = _qk_norm_bwd(qkv, dqn, dkn, dvv, qg_t, kg_t)
    dw_qkv_t = _mm_tn("attn_dw_qkv", dqkv, h2, tm=QKV_DIM, tn=1024, tk=2048)
    red_attn = _Reduction("attn", [dw_qkv_t.reshape(N_SHARD, QKV_DIM // N_SHARD, Dm),
                                   dw_o.reshape(N_SHARD, ATTN_DIM // N_SHARD, Dm)], place)
    dx2, dx2_bf, dg_attn, _ = _mm("attn_dx", dqkv, w_qkv_t, nt=False, tm=1024, tn=1024, rows=256,
                                  ep_in=((x2, "tile"), (attn_norm_f, "row"), (dx3, "tile")), ep_fn=_rms_bwd_ep,
                                  outs=RMS_BWD_OUTS, deps=(red_attn.token,))

    dx1, dx1_bf, dg_mlp0, db_out, dup0 = _mlp_bwd(0, dx2, dx2_bf, x1, mlp_norm_g[0:1], up0, w_up_sm[0], w_down_f[0])
    dw_down0 = _mm_tn("mlp0_dw_down", up0, dx2_bf, tm=1024, tn=1024, tk=2048, a_fn=_relu2)
    dw_up0 = _mm_tn("mlp0_dw_up", h1, dup0, tm=1024, tn=1024, tk=2048, out_sm=N_SHARD)
    dw_out = _mm_tn("conv_dw_out", s_act, dx1_bf, tm=1024, tn=1024, tk=2048)
    red_mlp0 = _Reduction("mlp0", [dw_up0, dw_down0.reshape(N_SHARD, D_FF // N_SHARD, Dm),
                                   dw_out.reshape(N_SHARD, Dm // N_SHARD, Dm)], place)
    (r_qkv, r_o) = red_attn.finish((dx1,))
    (r_up1, r_down1) = red_mlp1.finish((dx1,))

    dcv, dln_g, dln_b, ddw_b = _mm("conv_ds", dx1_bf, w_out_f, nt=True, tm=1024, tn=1024, rows=256,
                                   ep_in=((cv, "tile"), (conv_ln_g, "row"), (conv_ln_b, "row")),
                                   ep_fn=_ln_silu_bwd_ep,
                                   outs=(("tile", F32), ("colsum", F32), ("colsum", F32), ("colsum", F32)),
                                   deps=(red_mlp0.token,))
    du, db_in, ddw8 = _conv_bwd(u, dcv, dw_pad)
    (r_up0, r_down0, r_out) = red_mlp0.finish((du,))
    early = [r_out, r_qkv, r_o, r_up0, r_up1, r_down0, r_down1]
    join_sems, early, early_lands, join_token = _join_start("join_start", early)
    dw_in = _mm_tn("conv_dw_in", h0, du, tm=1024, tn=512, tk=4096, out_sm=N_SHARD)
    red_conv = _Reduction("conv", [dw_in], place)
    def first_layer_ep(*args):
        tot, _, dg, _ = _rms_bwd_ep(*args)
        return tot, dg

    gx, dg_conv = _mm("conv_dx", du, w_in_sm, nt=True, b_sm=True, tm=1024, tn=1024, rows=256,
                      ep_in=((x2d, "tile"), (conv_norm_g, "row"), (dx1, "tile")), ep_fn=first_layer_ep,
                      outs=(("tile", F32), ("colsum", F32)), deps=(red_conv.token, join_token))
    (r_in,) = red_conv.finish((gx,))

    dqg = dqg_t.reshape(N_HEADS, HEAD_DIM).sum(axis=0, keepdims=True)
    dkg = dkg_t.reshape(N_KV, HEAD_DIM).sum(axis=0, keepdims=True)
    small_full = [dg_conv, db_in, ddw8.sum(axis=1)[:CONV_W], ddw_b, dln_g, dln_b, db_out, dg_attn, db_qkv, dqg, dkg,
                  dsinks[None, :], db_o, drel.reshape(1, REL_BUCKETS * N_HEADS),
                  jnp.pad(dg_mlp0, ((0, 1), (0, 0))) + jnp.pad(dg_mlp1, ((1, 0), (0, 0))), sq]
    (sg_sems,), (sg_land,), sg_token = _gather_start(
        "small_grads_start", [own_slot(_pack(small_full), 8, me)], ((0,),), ALL_OTHERS, after=())

    early, early_sibling = _join_wait("join_wait", early, early_lands, join_sems, (gx, sg_token))
    r_out, r_qkv, r_o, r_up0, r_up1, r_down0, r_down1 = zip(early, early_sibling)
    r_in = (r_in,) + tuple(_join_halves("join_halves", [r_in], deps=(sg_token,)))

    big_out = {}
    qkv_t = [jnp.swapaxes(a, 1, 2) for a in (w_qkv, m_w_qkv, v_w_qkv)]
    for nm, w, m, v, gs in (("conv_w_in", conv_w_in, m_conv_w_in, v_conv_w_in, (r_in,)),
                            ("conv_w_out", conv_w_out, m_conv_w_out, v_conv_w_out, (r_out,)),
                            ("w_qkv", *qkv_t, (r_qkv,)),
                            ("w_o", w_o, m_w_o, v_w_o, (r_o,)),
                            ("w_up", w_up, m_w_up, v_w_up, (r_up0, r_up1)),
                            ("w_down", w_down, m_w_down, v_w_down, (r_down0, r_down1))):
        big_out[nm] = _adamw(f"adamw_{nm}", w, m, v, gs)

    (sg_land,) = _gather_wait("small_grads_wait", [sg_land], sg_sems, ALL_OTHERS,
                              [big_out[nm][0] for nm in big_out])
    big_out["w_qkv"] = tuple(jnp.swapaxes(a, 1, 2) for a in big_out["w_qkv"])
    small_sum = _sum8("small_grads_sum", sg_land)
    (r_norm, r_b_in, r_dw, r_dw_b, r_ln_g, r_ln_b, r_b_out, r_attn_norm, r_b_qkv, r_qg, r_kg, r_sinks, r_b_o, r_rel,
     r_mlp_norm, r_sq) = _unpack(small_sum, [a.shape for a in small_full])
    loss = 0.5 * jnp.sum(r_sq) * (1.0 / Dm)

    def cols(a, width):
        return lax.dynamic_slice_in_dim(a, shard * width, width, axis=a.ndim - 1)

    small_names = ["conv_norm_g", "conv_b_in", "conv_dw", "conv_dw_b", "conv_ln_g", "conv_ln_b", "conv_b_out",
                   "attn_norm_g", "b_qkv", "q_norm_g", "k_norm_g", "sinks", "b_o", "rel_bias", "mlp_norm_g"]
    small_g = [r_norm, r_b_in, cols(r_dw, Dm // N_SHARD)[None], r_dw_b, r_ln_g, r_ln_b, r_b_out,
               cols(r_attn_norm, Dm // N_SHARD), cols(r_b_qkv, QKV_DIM // N_SHARD), r_qg, r_kg, r_sinks,
               cols(r_b_o, Dm // N_SHARD), r_rel.reshape(N_HEADS, REL_BUCKETS), r_mlp_norm]
    small_w = [conv_norm_g, conv_b_in, conv_dw, conv_dw_b, conv_ln_g, conv_ln_b, conv_b_out, attn_norm_g, b_qkv,
               q_norm_g, k_norm_g, sinks, b_o, rel_bias.T, mlp_norm_g]
    small_m = [m_conv_norm_g, m_conv_b_in, m_conv_dw, m_conv_dw_b, m_conv_ln_g, m_conv_ln_b, m_conv_b_out,
               m_attn_norm_g, m_b_qkv, m_q_norm_g, m_k_norm_g, m_sinks, m_b_o, m_rel_bias.T, m_mlp_norm_g]
    small_v = [v_conv_norm_g, v_conv_b_in, v_conv_dw, v_conv_dw_b, v_conv_ln_g, v_conv_ln_b, v_conv_b_out,
               v_attn_norm_g, v_b_qkv, v_q_norm_g, v_k_norm_g, v_sinks, v_b_o, v_rel_bias.T, v_mlp_norm_g]
    flat2 = lambda a: a.reshape(-1, a.shape[-1])
    small_g = [flat2(g) for g in small_g]
    d_s, m_s, v_s = _adamw_small([flat2(w) for w in small_w], small_g, [flat2(m) for m in small_m],
                                 [flat2(v) for v in small_v])
    small_out = {}
    for nm, w, g, d, m2, v2 in zip(small_names, small_w, small_g, d_s, m_s, v_s):
        small_out[nm] = tuple(a.reshape(w.shape) for a in (g, d, m2, v2))
    small_out["rel_bias"] = tuple(a.T for a in small_out["rel_bias"])

    order = ["conv_norm_g", "conv_w_in", "conv_b_in", "conv_dw", "conv_dw_b", "conv_ln_g", "conv_ln_b", "conv_w_out",
             "conv_b_out", "attn_norm_g", "w_qkv", "b_qkv", "q_norm_g", "k_norm_g", "sinks", "w_o", "b_o", "rel_bias",
             "mlp_norm_g", "w_up", "w_down"]
    res = {**small_out, **big_out}
    outs = [loss, gx[None]]
    for part in range(4):
        outs += [res[nm][part] for nm in order]
    return tuple(outs)
```
